```python
import math
import jax, jax.numpy as jnp
from jax import lax
import numpy as np

D_MODEL = 1024
BATCH = 8
SEQ = 8192
DEPTH = 2

PLE_DIM = 256
MIX_WIDTH = D_MODEL
HEAD_DIM = 64
LRU_WIDTH = D_MODEL // 4
LRU_HEADS = LRU_WIDTH // HEAD_DIM
LRU_BLOCK = LRU_WIDTH // LRU_HEADS
LRU_C = 8.0
CONV_K = 4
ATT_WIDTH = D_MODEL // 4
ATT_HEADS = ATT_WIDTH // HEAD_DIM
ATT_BLOCK = 128
SSD_WIDTH = D_MODEL // 2
SSD_HEADS = SSD_WIDTH // HEAD_DIM
SSD_GROUPS = 2
SSD_HEADS_PER_GROUP = SSD_HEADS // SSD_GROUPS
SSD_STATE = 128
SSD_CHUNK = 128
SSD_CONV_DIM = SSD_WIDTH + 2 * SSD_GROUPS * SSD_STATE
FFN_DIM = 128 * ((8 * D_MODEL // 3 + 127) // 128)
ALPHA = (2.0 * DEPTH) ** 0.25
BETA = (8.0 * DEPTH) ** -0.25
LN_EPS = 1e-5
RMS_EPS = 1e-5
IN_SIZES = (LRU_WIDTH, LRU_WIDTH,
            ATT_WIDTH, ATT_WIDTH, ATT_WIDTH,
            ATT_HEADS,
            SSD_WIDTH, SSD_CONV_DIM, SSD_HEADS)
IN_WIDTH = sum(IN_SIZES)

kernel_name = "hymba_style_lru_fox_ssd_macaron_deepnorm"


def _split_points(sizes):
    pts, acc = [], 0
    for s in sizes[:-1]:
        acc += s
        pts.append(acc)
    return pts


def layer_norm(x, g, b):
    xf = x.astype(jnp.float32)
    mu = jnp.mean(xf, axis=-1, keepdims=True)
    var = jnp.mean(jnp.square(xf - mu), axis=-1, keepdims=True)
    y = (xf - mu) * lax.rsqrt(var + LN_EPS) * g.astype(jnp.float32) + b.astype(jnp.float32)
    return y.astype(x.dtype)


def swiglu(x, wg, wu, wd):
    return (jax.nn.silu(x @ wg) * (x @ wu)) @ wd


def causal_dwconv(u, w, b):
    c = u.shape[-1]
    y = lax.conv_general_dilated(u, w[:, None, :].astype(u.dtype), window_strides=(1,),
                                 padding=[(CONV_K - 1, 0)],
                                 dimension_numbers=("NWC", "WIO", "NWC"),
                                 feature_group_count=c)
    return y + b


def _lin_combine(e1, e2):
    a1, b1 = e1
    a2, b2 = e2
    return a1 * a2, a2 * b1 + b2


def rglru_block(u_raw, gate_raw, conv_w, conv_b, wa, ba, wx, bx, lam):
    bsz, s, _ = u_raw.shape
    u = causal_dwconv(u_raw, conv_w, conv_b)
    ub = u.reshape(bsz, s, LRU_HEADS, LRU_BLOCK)
    r = jax.nn.sigmoid(jnp.einsum("bshi,hij->bshj", ub, wa).reshape(bsz, s, LRU_WIDTH) + ba)
    ig = jax.nn.sigmoid(jnp.einsum("bshi,hij->bshj", ub, wx).reshape(bsz, s, LRU_WIDTH) + bx)
    log_a = -LRU_C * r.astype(jnp.float32) * jax.nn.softplus(-lam.astype(jnp.float32))
    a = jnp.exp(log_a)
    b = jnp.sqrt(-jnp.expm1(2.0 * log_a)) * (ig * u).astype(jnp.float32)
    _, h = lax.associative_scan(_lin_combine, (a, b), axis=1)
    return (h * jax.nn.gelu(gate_raw.astype(jnp.float32))).astype(u_raw.dtype)


def forgetting_attention(q, k, v, f_logit, b_f):
    bsz, s, _ = q.shape
    nblk = s // ATT_BLOCK
    log_f = jax.nn.log_sigmoid((f_logit + b_f).astype(jnp.float32))
    F = jnp.cumsum(log_f, axis=1).transpose(0, 2, 1)
    qh = q.astype(jnp.float32).reshape(bsz, s, ATT_HEADS, HEAD_DIM).transpose(0, 2, 1, 3) * (HEAD_DIM ** -0.5)
    kh = k.astype(jnp.float32).reshape(bsz, s, ATT_HEADS, HEAD_DIM).transpose(0, 2, 1, 3)
    vh = v.astype(jnp.float32).reshape(bsz, s, ATT_HEADS, HEAD_DIM).transpose(0, 2, 1, 3)
    q_blocks = qh.reshape(bsz, ATT_HEADS, nblk, ATT_BLOCK, HEAD_DIM).transpose(2, 0, 1, 3, 4)
    f_blocks = F.reshape(bsz, ATT_HEADS, nblk, ATT_BLOCK).transpose(2, 0, 1, 3)
    key_pos = jnp.arange(s)

    def one_block(args):
        qb, fb, bi = args
        logits = jnp.einsum("bhqd,bhkd->bhqk", qb, kh) + fb[..., :, None] - F[:, :, None, :]
        q_pos = bi * ATT_BLOCK + jnp.arange(ATT_BLOCK)
        logits = jnp.where(key_pos[None, :] <= q_pos[:, None], logits, -jnp.inf)
        w = jax.nn.softmax(logits, axis=-1)
        return jnp.einsum("bhqk,bhkd->bhqd", w, vh)

    out = lax.map(one_block, (q_blocks, f_blocks, jnp.arange(nblk)))
    out = out.transpose(1, 0, 3, 2, 4).reshape(bsz, s, ATT_WIDTH)
    return out.astype(q.dtype)


def segsum(x):
    t = x.shape[-1]
    xc = jnp.cumsum(x, axis=-1)
    seg = xc[..., :, None] - xc[..., None, :]
    mask = jnp.tril(jnp.ones((t, t), dtype=bool))
    return jnp.where(mask, seg, -jnp.inf)


def ssd_chunked(xs, a, bm, cm):
    bsz, s, h, p = xs.shape
    n = bm.shape[-1]
    c = s // SSD_CHUNK
    xs = xs.reshape(bsz, c, SSD_CHUNK, h, p)
    bm = bm.reshape(bsz, c, SSD_CHUNK, h, n)
    cm = cm.reshape(bsz, c, SSD_CHUNK, h, n)
    a = a.reshape(bsz, c, SSD_CHUNK, h).transpose(0, 3, 1, 2)
    a_cum = jnp.cumsum(a, axis=-1)
    L = jnp.exp(segsum(a))
    y_diag = jnp.einsum("bclhn,bcshn,bhcls,bcshp->bclhp", cm, bm, L, xs)
    decay_states = jnp.exp(a_cum[..., -1:] - a_cum)
    states = jnp.einsum("bclhn,bhcl,bclhp->bchpn", bm, decay_states, xs)
    states = jnp.concatenate([jnp.zeros_like(states[:, :1]), states], axis=1)
    decay_chunk = jnp.exp(segsum(jnp.pad(a_cum[..., -1], ((0, 0), (0, 0), (1, 0)))))
    states = jnp.einsum("bhzc,bchpn->bzhpn", decay_chunk, states)[:, :-1]
    y_off = jnp.einsum("bclhn,bchpn,bhcl->bclhp", cm, states, jnp.exp(a_cum))
    return (y_diag + y_off).reshape(bsz, s, h, p)


def ssd_mixer(z, xbc_raw, dt_raw, conv_w, conv_b, dt_bias, a_log, d_skip, norm_g):
    bsz, s, _ = z.shape
    xbc = jax.nn.silu(causal_dwconv(xbc_raw, conv_w, conv_b)).astype(jnp.float32)
    xs, bm, cm = jnp.split(xbc, [SSD_WIDTH, SSD_WIDTH + SSD_GROUPS * SSD_STATE], axis=-1)
    dt = jax.nn.softplus((dt_raw + dt_bias).astype(jnp.float32))
    A = -jnp.exp(a_log.astype(jnp.float32))
    xs_h = xs.reshape(bsz, s, SSD_HEADS, HEAD_DIM)
    bm_h = jnp.repeat(bm.reshape(bsz, s, SSD_GROUPS, SSD_STATE), SSD_HEADS_PER_GROUP, axis=2)
    cm_h = jnp.repeat(cm.reshape(bsz, s, SSD_GROUPS, SSD_STATE), SSD_HEADS_PER_GROUP, axis=2)
    y = ssd_chunked(xs_h * dt[..., None], A * dt, bm_h, cm_h)
    y = y + d_skip.astype(jnp.float32)[:, None] * xs_h
    y = y.reshape(bsz, s, SSD_WIDTH) * jax.nn.silu(z.astype(jnp.float32))
    yg = y.reshape(bsz, s, SSD_GROUPS, SSD_WIDTH // SSD_GROUPS)
    yg = yg * lax.rsqrt(jnp.mean(jnp.square(yg), axis=-1, keepdims=True) + RMS_EPS)
    y = yg.reshape(bsz, s, SSD_WIDTH) * norm_g.astype(jnp.float32)
    return y.astype(z.dtype)


def _fwd_setup_inputs(seed: int = 0) -> dict:
    key = jax.random.key(seed)
    ks = iter(jax.random.split(key, 33))
    f32 = jnp.float32

    def nrm(shape, scale):
        return scale * jax.random.normal(next(ks), shape, f32)

    def gain(shape):
        return 1.0 + nrm(shape, 0.02)

    d = DEPTH
    out = {}
    out["x"] = nrm((BATCH, SEQ, D_MODEL), 1.0)
    out["p"] = nrm((DEPTH, BATCH, SEQ, PLE_DIM), 1.0)
    out["ln1_g"] = gain((d, D_MODEL))
    out["ln1_b"] = nrm((d, D_MODEL), 0.02)
    out["ffn1_wg"] = nrm((d, D_MODEL, FFN_DIM), BETA * D_MODEL ** -0.5)
    out["ffn1_wu"] = nrm((d, D_MODEL, FFN_DIM), BETA * D_MODEL ** -0.5)
    out["ffn1_wd"] = nrm((d, FFN_DIM, D_MODEL), BETA * FFN_DIM ** -0.5)
    out["w_in"] = nrm((d, D_MODEL, IN_WIDTH), D_MODEL ** -0.5)
    out["lru_conv_w"] = nrm((d, CONV_K, LRU_WIDTH), CONV_K ** -0.5)
    out["lru_conv_b"] = nrm((d, LRU_WIDTH), 0.02)
    out["lru_wa"] = nrm((d, LRU_HEADS, LRU_BLOCK, LRU_BLOCK), LRU_BLOCK ** -0.5)
    out["lru_ba"] = nrm((d, LRU_WIDTH), 0.02)
    out["lru_wx"] = nrm((d, LRU_HEADS, LRU_BLOCK, LRU_BLOCK), LRU_BLOCK ** -0.5)
    out["lru_bx"] = nrm((d, LRU_WIDTH), 0.02)
    a_c = jax.random.uniform(next(ks), (d, LRU_WIDTH), f32, 0.9, 0.999)
    a0 = a_c ** (1.0 / LRU_C)
    out["lru_lambda"] = jnp.log(a0) - jnp.log1p(-a0)
    out["fox_bf"] = 3.0 + nrm((d, ATT_HEADS), 0.1)
    out["ssd_conv_w"] = nrm((d, CONV_K, SSD_CONV_DIM), CONV_K ** -0.5)
    out["ssd_conv_b"] = nrm((d, SSD_CONV_DIM), 0.02)
    dt0 = jnp.exp(jax.random.uniform(next(ks), (d, SSD_HEADS), f32, math.log(1e-3), math.log(1e-1)))
    out["ssd_dt_bias"] = dt0 + jnp.log(-jnp.expm1(-dt0))
    out["ssd_a_log"] = jnp.log(jax.random.uniform(next(ks), (d, SSD_HEADS), f32, 1.0, 16.0))
    out["ssd_d"] = gain((d, SSD_HEADS))
    out["ssd_norm_g"] = gain((d, SSD_WIDTH))
    out["w_out"] = nrm((d, MIX_WIDTH, D_MODEL), BETA * MIX_WIDTH ** -0.5)
    out["ln2_g"] = gain((d, D_MODEL))
    out["ln2_b"] = nrm((d, D_MODEL), 0.02)
    out["ffn2_wg"] = nrm((d, D_MODEL, FFN_DIM), BETA * D_MODEL ** -0.5)
    out["ffn2_wu"] = nrm((d, D_MODEL, FFN_DIM), BETA * D_MODEL ** -0.5)
    out["ffn2_wd"] = nrm((d, FFN_DIM, D_MODEL), BETA * FFN_DIM ** -0.5)
    out["ln3_g"] = gain((d, D_MODEL))
    out["ln3_b"] = nrm((d, D_MODEL), 0.02)
    out["pe_proj"] = nrm((d, PLE_DIM, D_MODEL), BETA * PLE_DIM ** -0.5)
    out["pe_gate_w"] = nrm((d, D_MODEL, D_MODEL), D_MODEL ** -0.5)
    out["pe_gate_b"] = nrm((d, D_MODEL), 0.02)
    return out


def _fwd_reference(x, p, ln1_g, ln1_b, ffn1_wg, ffn1_wu, ffn1_wd, w_in,
              lru_conv_w, lru_conv_b, lru_wa, lru_ba, lru_wx, lru_bx, lru_lambda,
              fox_bf, ssd_conv_w, ssd_conv_b, ssd_dt_bias, ssd_a_log, ssd_d, ssd_norm_g,
              w_out, ln2_g, ln2_b, ffn2_wg, ffn2_wu, ffn2_wd, ln3_g, ln3_b,
              pe_proj, pe_gate_w, pe_gate_b):
    pts = _split_points(IN_SIZES)
    for i in range(DEPTH):
        x = layer_norm(ALPHA * x + 0.5 * swiglu(x, ffn1_wg[i], ffn1_wu[i], ffn1_wd[i]), ln1_g[i], ln1_b[i])
        h = x @ w_in[i]
        u_lru, g_lru, q, k, v, f_logit, z, xbc, dt_raw = jnp.split(h, pts, axis=-1)
        y_a = rglru_block(u_lru, g_lru, lru_conv_w[i], lru_conv_b[i], lru_wa[i], lru_ba[i],
                          lru_wx[i], lru_bx[i], lru_lambda[i])
        y_b = forgetting_attention(q, k, v, f_logit, fox_bf[i])
        y_c = ssd_mixer(z, xbc, dt_raw, ssd_conv_w[i], ssd_conv_b[i], ssd_dt_bias[i],
                        ssd_a_log[i], ssd_d[i], ssd_norm_g[i])
        mix = jnp.concatenate([y_a, y_b, y_c], axis=-1) @ w_out[i]
        x = layer_norm(ALPHA * x + mix, ln2_g[i], ln2_b[i])
        x = layer_norm(ALPHA * x + 0.5 * swiglu(x, ffn2_wg[i], ffn2_wu[i], ffn2_wd[i]), ln3_g[i], ln3_b[i])
        x = x + jax.nn.sigmoid(x @ pe_gate_w[i] + pe_gate_b[i]) * (p[i] @ pe_proj[i])
    return x


import jax as _jax
import jax.numpy as _jnp

TWIN_FORMAT = 'train_step'
FWD_PARAMS = ['x', 'p', 'ln1_g', 'ln1_b', 'ffn1_wg', 'ffn1_wu', 'ffn1_wd', 'w_in', 'lru_conv_w', 'lru_conv_b', 'lru_wa', 'lru_ba', 'lru_wx', 'lru_bx', 'lru_lambda', 'fox_bf', 'ssd_conv_w', 'ssd_conv_b', 'ssd_dt_bias', 'ssd_a_log', 'ssd_d', 'ssd_norm_g', 'w_out', 'ln2_g', 'ln2_b', 'ffn2_wg', 'ffn2_wu', 'ffn2_wd', 'ln3_g', 'ln3_b', 'pe_proj', 'pe_gate_w', 'pe_gate_b']
TWIN_WEIGHTS = ['ln1_g', 'ln1_b', 'ffn1_wg', 'ffn1_wu', 'ffn1_wd', 'w_in', 'lru_conv_w', 'lru_conv_b', 'lru_wa', 'lru_ba', 'lru_wx', 'lru_bx', 'lru_lambda', 'fox_bf', 'ssd_conv_w', 'ssd_conv_b', 'ssd_dt_bias', 'ssd_a_log', 'ssd_d', 'ssd_norm_g', 'w_out', 'ln2_g', 'ln2_b', 'ffn2_wg', 'ffn2_wu', 'ffn2_wd', 'ln3_g', 'ln3_b', 'pe_proj', 'pe_gate_w', 'pe_gate_b']
TWIN_DIFF_INPUT = 'x'
TWIN_INPUTS = ['x', 'p', 'ln1_g', 'ln1_b', 'ffn1_wg', 'ffn1_wu', 'ffn1_wd', 'w_in', 'lru_conv_w', 'lru_conv_b', 'lru_wa', 'lru_ba', 'lru_wx', 'lru_bx', 'lru_lambda', 'fox_bf', 'ssd_conv_w', 'ssd_conv_b', 'ssd_dt_bias', 'ssd_a_log', 'ssd_d', 'ssd_norm_g', 'w_out', 'ln2_g', 'ln2_b', 'ffn2_wg', 'ffn2_wu', 'ffn2_wd', 'ln3_g', 'ln3_b', 'pe_proj', 'pe_gate_w', 'pe_gate_b', 'loss_target', 'm_ln1_g', 'm_ln1_b', 'm_ffn1_wg', 'm_ffn1_wu', 'm_ffn1_wd', 'm_w_in', 'm_lru_conv_w', 'm_lru_conv_b', 'm_lru_wa', 'm_lru_ba', 'm_lru_wx', 'm_lru_bx', 'm_lru_lambda', 'm_fox_bf', 'm_ssd_conv_w', 'm_ssd_conv_b', 'm_ssd_dt_bias', 'm_ssd_a_log', 'm_ssd_d', 'm_ssd_norm_g', 'm_w_out', 'm_ln2_g', 'm_ln2_b', 'm_ffn2_wg', 'm_ffn2_wu', 'm_ffn2_wd', 'm_ln3_g', 'm_ln3_b', 'm_pe_proj', 'm_pe_gate_w', 'm_pe_gate_b', 'v_ln1_g', 'v_ln1_b', 'v_ffn1_wg', 'v_ffn1_wu', 'v_ffn1_wd', 'v_w_in', 'v_lru_conv_w', 'v_lru_conv_b', 'v_lru_wa', 'v_lru_ba', 'v_lru_wx', 'v_lru_bx', 'v_lru_lambda', 'v_fox_bf', 'v_ssd_conv_w', 'v_ssd_conv_b', 'v_ssd_dt_bias', 'v_ssd_a_log', 'v_ssd_d', 'v_ssd_norm_g', 'v_w_out', 'v_ln2_g', 'v_ln2_b', 'v_ffn2_wg', 'v_ffn2_wu', 'v_ffn2_wd', 'v_ln3_g', 'v_ln3_b', 'v_pe_proj', 'v_pe_gate_w', 'v_pe_gate_b']
TWIN_OUTPUTS = ['loss', 'grad_x', 'grad_ln1_g', 'grad_ln1_b', 'grad_ffn1_wg', 'grad_ffn1_wu', 'grad_ffn1_wd', 'grad_w_in', 'grad_lru_conv_w', 'grad_lru_conv_b', 'grad_lru_wa', 'grad_lru_ba', 'grad_lru_wx', 'grad_lru_bx', 'grad_lru_lambda', 'grad_fox_bf', 'grad_ssd_conv_w', 'grad_ssd_conv_b', 'grad_ssd_dt_bias', 'grad_ssd_a_log', 'grad_ssd_d', 'grad_ssd_norm_g', 'grad_w_out', 'grad_ln2_g', 'grad_ln2_b', 'grad_ffn2_wg', 'grad_ffn2_wu', 'grad_ffn2_wd', 'grad_ln3_g', 'grad_ln3_b', 'grad_pe_proj', 'grad_pe_gate_w', 'grad_pe_gate_b', 'delta_ln1_g', 'delta_ln1_b', 'delta_ffn1_wg', 'delta_ffn1_wu', 'delta_ffn1_wd', 'delta_w_in', 'delta_lru_conv_w', 'delta_lru_conv_b', 'delta_lru_wa', 'delta_lru_ba', 'delta_lru_wx', 'delta_lru_bx', 'delta_lru_lambda', 'delta_fox_bf', 'delta_ssd_conv_w', 'delta_ssd_conv_b', 'delta_ssd_dt_bias', 'delta_ssd_a_log', 'delta_ssd_d', 'delta_ssd_norm_g', 'delta_w_out', 'delta_ln2_g', 'delta_ln2_b', 'delta_ffn2_wg', 'delta_ffn2_wu', 'delta_ffn2_wd', 'delta_ln3_g', 'delta_ln3_b', 'delta_pe_proj', 'delta_pe_gate_w', 'delta_pe_gate_b', 'new_m_ln1_g', 'new_m_ln1_b', 'new_m_ffn1_wg', 'new_m_ffn1_wu', 'new_m_ffn1_wd', 'new_m_w_in', 'new_m_lru_conv_w', 'new_m_lru_conv_b', 'new_m_lru_wa', 'new_m_lru_ba', 'new_m_lru_wx', 'new_m_lru_bx', 'new_m_lru_lambda', 'new_m_fox_bf', 'new_m_ssd_conv_w', 'new_m_ssd_conv_b', 'new_m_ssd_dt_bias', 'new_m_ssd_a_log', 'new_m_ssd_d', 'new_m_ssd_norm_g', 'new_m_w_out', 'new_m_ln2_g', 'new_m_ln2_b', 'new_m_ffn2_wg', 'new_m_ffn2_wu', 'new_m_ffn2_wd', 'new_m_ln3_g', 'new_m_ln3_b', 'new_m_pe_proj', 'new_m_pe_gate_w', 'new_m_pe_gate_b', 'new_v_ln1_g', 'new_v_ln1_b', 'new_v_ffn1_wg', 'new_v_ffn1_wu', 'new_v_ffn1_wd', 'new_v_w_in', 'new_v_lru_conv_w', 'new_v_lru_conv_b', 'new_v_lru_wa', 'new_v_lru_ba', 'new_v_lru_wx', 'new_v_lru_bx', 'new_v_lru_lambda', 'new_v_fox_bf', 'new_v_ssd_conv_w', 'new_v_ssd_conv_b', 'new_v_ssd_dt_bias', 'new_v_ssd_a_log', 'new_v_ssd_d', 'new_v_ssd_norm_g', 'new_v_w_out', 'new_v_ln2_g', 'new_v_ln2_b', 'new_v_ffn2_wg', 'new_v_ffn2_wu', 'new_v_ffn2_wd', 'new_v_ln3_g', 'new_v_ln3_b', 'new_v_pe_proj', 'new_v_pe_gate_w', 'new_v_pe_gate_b']
TWIN_LEAF_KINDS = {'loss': 'loss', 'grad_x': 'grad_x', 'grad_ln1_g': 'grad_w', 'grad_ln1_b': 'grad_w', 'grad_ffn1_wg': 'grad_w', 'grad_ffn1_wu': 'grad_w', 'grad_ffn1_wd': 'grad_w', 'grad_w_in': 'grad_w', 'grad_lru_conv_w': 'grad_w', 'grad_lru_conv_b': 'grad_w', 'grad_lru_wa': 'grad_w', 'grad_lru_ba': 'grad_w', 'grad_lru_wx': 'grad_w', 'grad_lru_bx': 'grad_w', 'grad_lru_lambda': 'grad_w', 'grad_fox_bf': 'grad_w', 'grad_ssd_conv_w': 'grad_w', 'grad_ssd_conv_b': 'grad_w', 'grad_ssd_dt_bias': 'grad_w', 'grad_ssd_a_log': 'grad_w', 'grad_ssd_d': 'grad_w', 'grad_ssd_norm_g': 'grad_w', 'grad_w_out': 'grad_w', 'grad_ln2_g': 'grad_w', 'grad_ln2_b': 'grad_w', 'grad_ffn2_wg': 'grad_w', 'grad_ffn2_wu': 'grad_w', 'grad_ffn2_wd': 'grad_w', 'grad_ln3_g': 'grad_w', 'grad_ln3_b': 'grad_w', 'grad_pe_proj': 'grad_w', 'grad_pe_gate_w': 'grad_w', 'grad_pe_gate_b': 'grad_w', 'delta_ln1_g': 'delta_w', 'delta_ln1_b': 'delta_w', 'delta_ffn1_wg': 'delta_w', 'delta_ffn1_wu': 'delta_w', 'delta_ffn1_wd': 'delta_w', 'delta_w_in': 'delta_w', 'delta_lru_conv_w': 'delta_w', 'delta_lru_conv_b': 'delta_w', 'delta_lru_wa': 'delta_w', 'delta_lru_ba': 'delta_w', 'delta_lru_wx': 'delta_w', 'delta_lru_bx': 'delta_w', 'delta_lru_lambda': 'delta_w', 'delta_fox_bf': 'delta_w', 'delta_ssd_conv_w': 'delta_w', 'delta_ssd_conv_b': 'delta_w', 'delta_ssd_dt_bias': 'delta_w', 'delta_ssd_a_log': 'delta_w', 'delta_ssd_d': 'delta_w', 'delta_ssd_norm_g': 'delta_w', 'delta_w_out': 'delta_w', 'delta_ln2_g': 'delta_w', 'delta_ln2_b': 'delta_w', 'delta_ffn2_wg': 'delta_w', 'delta_ffn2_wu': 'delta_w', 'delta_ffn2_wd': 'delta_w', 'delta_ln3_g': 'delta_w', 'delta_ln3_b': 'delta_w', 'delta_pe_proj': 'delta_w', 'delta_pe_gate_w': 'delta_w', 'delta_pe_gate_b': 'delta_w', 'new_m_ln1_g': 'new_m', 'new_m_ln1_b': 'new_m', 'new_m_ffn1_wg': 'new_m', 'new_m_ffn1_wu': 'new_m', 'new_m_ffn1_wd': 'new_m', 'new_m_w_in': 'new_m', 'new_m_lru_conv_w': 'new_m', 'new_m_lru_conv_b': 'new_m', 'new_m_lru_wa': 'new_m', 'new_m_lru_ba': 'new_m', 'new_m_lru_wx': 'new_m', 'new_m_lru_bx': 'new_m', 'new_m_lru_lambda': 'new_m', 'new_m_fox_bf': 'new_m', 'new_m_ssd_conv_w': 'new_m', 'new_m_ssd_conv_b': 'new_m', 'new_m_ssd_dt_bias': 'new_m', 'new_m_ssd_a_log': 'new_m', 'new_m_ssd_d': 'new_m', 'new_m_ssd_norm_g': 'new_m', 'new_m_w_out': 'new_m', 'new_m_ln2_g': 'new_m', 'new_m_ln2_b': 'new_m', 'new_m_ffn2_wg': 'new_m', 'new_m_ffn2_wu': 'new_m', 'new_m_ffn2_wd': 'new_m', 'new_m_ln3_g': 'new_m', 'new_m_ln3_b': 'new_m', 'new_m_pe_proj': 'new_m', 'new_m_pe_gate_w': 'new_m', 'new_m_pe_gate_b': 'new_m', 'new_v_ln1_g': 'new_v', 'new_v_ln1_b': 'new_v', 'new_v_ffn1_wg': 'new_v', 'new_v_ffn1_wu': 'new_v', 'new_v_ffn1_wd': 'new_v', 'new_v_w_in': 'new_v', 'new_v_lru_conv_w': 'new_v', 'new_v_lru_conv_b': 'new_v', 'new_v_lru_wa': 'new_v', 'new_v_lru_ba': 'new_v', 'new_v_lru_wx': 'new_v', 'new_v_lru_bx': 'new_v', 'new_v_lru_lambda': 'new_v', 'new_v_fox_bf': 'new_v', 'new_v_ssd_conv_w': 'new_v', 'new_v_ssd_conv_b': 'new_v', 'new_v_ssd_dt_bias': 'new_v', 'new_v_ssd_a_log': 'new_v', 'new_v_ssd_d': 'new_v', 'new_v_ssd_norm_g': 'new_v', 'new_v_w_out': 'new_v', 'new_v_ln2_g': 'new_v', 'new_v_ln2_b': 'new_v', 'new_v_ffn2_wg': 'new_v', 'new_v_ffn2_wu': 'new_v', 'new_v_ffn2_wd': 'new_v', 'new_v_ln3_g': 'new_v', 'new_v_ln3_b': 'new_v', 'new_v_pe_proj': 'new_v', 'new_v_pe_gate_w': 'new_v', 'new_v_pe_gate_b': 'new_v'}


def _forward(args):
    return _fwd_reference(*[args[k] for k in FWD_PARAMS])


def _output_shape():
    def fwd():
        inp = _fwd_setup_inputs(0)
        return _fwd_reference(*[inp[k] for k in FWD_PARAMS])
    out = _jax.eval_shape(fwd)
    return out.shape, out.dtype

N_MICROBATCH = 1
ADAM_LR = 0.001
ADAM_B1 = 0.9
ADAM_B2 = 0.999
ADAM_EPS = 1e-08
ADAM_WD = 0.01
ADAM_STEP = 10
PER_EXAMPLE_BATCH_AXIS = {'x': 0, 'p': 1, 'loss_target': 0}
SHARED_INPUTS = []
_WEIGHT_DTYPES = {'ln1_g': _jnp.float32, 'ln1_b': _jnp.float32, 'ffn1_wg': _jnp.float32, 'ffn1_wu': _jnp.float32, 'ffn1_wd': _jnp.float32, 'w_in': _jnp.float32, 'lru_conv_w': _jnp.float32, 'lru_conv_b': _jnp.float32, 'lru_wa': _jnp.float32, 'lru_ba': _jnp.float32, 'lru_wx': _jnp.float32, 'lru_bx': _jnp.float32, 'lru_lambda': _jnp.float32, 'fox_bf': _jnp.float32, 'ssd_conv_w': _jnp.float32, 'ssd_conv_b': _jnp.float32, 'ssd_dt_bias': _jnp.float32, 'ssd_a_log': _jnp.float32, 'ssd_d': _jnp.float32, 'ssd_norm_g': _jnp.float32, 'w_out': _jnp.float32, 'ln2_g': _jnp.float32, 'ln2_b': _jnp.float32, 'ffn2_wg': _jnp.float32, 'ffn2_wu': _jnp.float32, 'ffn2_wd': _jnp.float32, 'ln3_g': _jnp.float32, 'ln3_b': _jnp.float32, 'pe_proj': _jnp.float32, 'pe_gate_w': _jnp.float32, 'pe_gate_b': _jnp.float32}
MOMENT_SCALE = {'ln1_g': 1.904201e+00, 'ln1_b': 1.810102e+00, 'ffn1_wg': 8.244113e-03, 'ffn1_wu': 8.097429e-03, 'ffn1_wd': 1.341775e-02, 'w_in': 6.375727e-02, 'lru_conv_w': 6.576882e-02, 'lru_conv_b': 1.297032e+00, 'lru_wa': 3.669413e-02, 'lru_ba': 1.704247e-02, 'lru_wx': 6.746123e-02, 'lru_bx': 2.071812e-02, 'lru_lambda': 3.362853e-02, 'fox_bf': 1.017458e-01, 'ssd_conv_w': 7.805466e-02, 'ssd_conv_b': 1.568422e-01, 'ssd_dt_bias': 1.404118e-01, 'ssd_a_log': 5.434856e-01, 'ssd_d': 7.362233e-01, 'ssd_norm_g': 1.376769e-01, 'w_out': 1.782900e-01, 'ln2_g': 2.118161e+00, 'ln2_b': 1.768127e+00, 'ffn2_wg': 7.743390e-03, 'ffn2_wu': 7.627247e-03, 'ffn2_wd': 1.261282e-02, 'ln3_g': 4.571303e+01, 'ln3_b': 4.016000e+00, 'pe_proj': 2.984761e-01, 'pe_gate_w': 6.818845e-02, 'pe_gate_b': 1.139245e+00}


def _to_microbatches(a, axis):
    t = _jnp.moveaxis(a, axis, 0)
    t = t.reshape((N_MICROBATCH, t.shape[0] // N_MICROBATCH) + t.shape[1:])
    return _jnp.moveaxis(t, 1, axis + 1)


def setup_inputs(seed: int = 0) -> dict:
    inp = _fwd_setup_inputs(seed)
    key = _jax.random.fold_in(_jax.random.key(seed), 7919)
    shape, _ = _output_shape()
    out = dict(inp)
    out["loss_target"] = _jax.random.normal(_jax.random.fold_in(key, 0), shape, _jnp.float32)
    for i, name in enumerate(TWIN_WEIGHTS):
        w = inp[name].astype(_jnp.float32)
        if MOMENT_SCALE is None:
            s = _jnp.sqrt(_jnp.mean(_jnp.square(w)) + 1e-30)
        else:
            s = MOMENT_SCALE[name]
        km, kv = _jax.random.split(_jax.random.fold_in(key, i + 1))
        out[name] = w
        out["m_" + name] = s * _jax.random.normal(km, w.shape, _jnp.float32)
        out["v_" + name] = (s * s) * _jax.random.uniform(kv, w.shape, _jnp.float32, 0.5, 1.5)
    if N_MICROBATCH > 1:
        for name, axis in PER_EXAMPLE_BATCH_AXIS.items():
            out[name] = _to_microbatches(out[name], axis)
    return {'x': out['x'], 'p': out['p'], 'ln1_g': out['ln1_g'], 'ln1_b': out['ln1_b'], 'ffn1_wg': out['ffn1_wg'], 'ffn1_wu': out['ffn1_wu'], 'ffn1_wd': out['ffn1_wd'], 'w_in': out['w_in'], 'lru_conv_w': out['lru_conv_w'], 'lru_conv_b': out['lru_conv_b'], 'lru_wa': out['lru_wa'], 'lru_ba': out['lru_ba'], 'lru_wx': out['lru_wx'], 'lru_bx': out['lru_bx'], 'lru_lambda': out['lru_lambda'], 'fox_bf': out['fox_bf'], 'ssd_conv_w': out['ssd_conv_w'], 'ssd_conv_b': out['ssd_conv_b'], 'ssd_dt_bias': out['ssd_dt_bias'], 'ssd_a_log': out['ssd_a_log'], 'ssd_d': out['ssd_d'], 'ssd_norm_g': out['ssd_norm_g'], 'w_out': out['w_out'], 'ln2_g': out['ln2_g'], 'ln2_b': out['ln2_b'], 'ffn2_wg': out['ffn2_wg'], 'ffn2_wu': out['ffn2_wu'], 'ffn2_wd': out['ffn2_wd'], 'ln3_g': out['ln3_g'], 'ln3_b': out['ln3_b'], 'pe_proj': out['pe_proj'], 'pe_gate_w': out['pe_gate_w'], 'pe_gate_b': out['pe_gate_b'], 'loss_target': out['loss_target'], 'm_ln1_g': out['m_ln1_g'], 'm_ln1_b': out['m_ln1_b'], 'm_ffn1_wg': out['m_ffn1_wg'], 'm_ffn1_wu': out['m_ffn1_wu'], 'm_ffn1_wd': out['m_ffn1_wd'], 'm_w_in': out['m_w_in'], 'm_lru_conv_w': out['m_lru_conv_w'], 'm_lru_conv_b': out['m_lru_conv_b'], 'm_lru_wa': out['m_lru_wa'], 'm_lru_ba': out['m_lru_ba'], 'm_lru_wx': out['m_lru_wx'], 'm_lru_bx': out['m_lru_bx'], 'm_lru_lambda': out['m_lru_lambda'], 'm_fox_bf': out['m_fox_bf'], 'm_ssd_conv_w': out['m_ssd_conv_w'], 'm_ssd_conv_b': out['m_ssd_conv_b'], 'm_ssd_dt_bias': out['m_ssd_dt_bias'], 'm_ssd_a_log': out['m_ssd_a_log'], 'm_ssd_d': out['m_ssd_d'], 'm_ssd_norm_g': out['m_ssd_norm_g'], 'm_w_out': out['m_w_out'], 'm_ln2_g': out['m_ln2_g'], 'm_ln2_b': out['m_ln2_b'], 'm_ffn2_wg': out['m_ffn2_wg'], 'm_ffn2_wu': out['m_ffn2_wu'], 'm_ffn2_wd': out['m_ffn2_wd'], 'm_ln3_g': out['m_ln3_g'], 'm_ln3_b': out['m_ln3_b'], 'm_pe_proj': out['m_pe_proj'], 'm_pe_gate_w': out['m_pe_gate_w'], 'm_pe_gate_b': out['m_pe_gate_b'], 'v_ln1_g': out['v_ln1_g'], 'v_ln1_b': out['v_ln1_b'], 'v_ffn1_wg': out['v_ffn1_wg'], 'v_ffn1_wu': out['v_ffn1_wu'], 'v_ffn1_wd': out['v_ffn1_wd'], 'v_w_in': out['v_w_in'], 'v_lru_conv_w': out['v_lru_conv_w'], 'v_lru_conv_b': out['v_lru_conv_b'], 'v_lru_wa': out['v_lru_wa'], 'v_lru_ba': out['v_lru_ba'], 'v_lru_wx': out['v_lru_wx'], 'v_lru_bx': out['v_lru_bx'], 'v_lru_lambda': out['v_lru_lambda'], 'v_fox_bf': out['v_fox_bf'], 'v_ssd_conv_w': out['v_ssd_conv_w'], 'v_ssd_conv_b': out['v_ssd_conv_b'], 'v_ssd_dt_bias': out['v_ssd_dt_bias'], 'v_ssd_a_log': out['v_ssd_a_log'], 'v_ssd_d': out['v_ssd_d'], 'v_ssd_norm_g': out['v_ssd_norm_g'], 'v_w_out': out['v_w_out'], 'v_ln2_g': out['v_ln2_g'], 'v_ln2_b': out['v_ln2_b'], 'v_ffn2_wg': out['v_ffn2_wg'], 'v_ffn2_wu': out['v_ffn2_wu'], 'v_ffn2_wd': out['v_ffn2_wd'], 'v_ln3_g': out['v_ln3_g'], 'v_ln3_b': out['v_ln3_b'], 'v_pe_proj': out['v_pe_proj'], 'v_pe_gate_w': out['v_pe_gate_w'], 'v_pe_gate_b': out['v_pe_gate_b']}


def _loss(weights, diff, rest, loss_target):
    with _jax.named_scope("forward"):
        args = {**rest, TWIN_DIFF_INPUT: diff, **{k: w.astype(_WEIGHT_DTYPES[k]) for k, w in weights.items()}}
        y = _forward(args)
    with _jax.named_scope("loss_head"):
        err = _jnp.square(y.astype(_jnp.float32) - loss_target)
        return 0.5 * _jnp.sum(_jnp.mean(err, axis=-1)) if err.ndim else 0.5 * err


def _adamw(w, g, m, v):
    m = ADAM_B1 * m + (1.0 - ADAM_B1) * g
    v = ADAM_B2 * v + (1.0 - ADAM_B2) * _jnp.square(g)
    m_hat = m / (1.0 - ADAM_B1 ** ADAM_STEP)
    v_hat = v / (1.0 - ADAM_B2 ** ADAM_STEP)
    delta = -ADAM_LR * (m_hat / (_jnp.sqrt(v_hat) + ADAM_EPS) + ADAM_WD * w)
    return delta, m, v


def reference(x, p, ln1_g, ln1_b, ffn1_wg, ffn1_wu, ffn1_wd, w_in, lru_conv_w, lru_conv_b, lru_wa, lru_ba, lru_wx, lru_bx, lru_lambda, fox_bf, ssd_conv_w, ssd_conv_b, ssd_dt_bias, ssd_a_log, ssd_d, ssd_norm_g, w_out, ln2_g, ln2_b, ffn2_wg, ffn2_wu, ffn2_wd, ln3_g, ln3_b, pe_proj, pe_gate_w, pe_gate_b, loss_target, m_ln1_g, m_ln1_b, m_ffn1_wg, m_ffn1_wu, m_ffn1_wd, m_w_in, m_lru_conv_w, m_lru_conv_b, m_lru_wa, m_lru_ba, m_lru_wx, m_lru_bx, m_lru_lambda, m_fox_bf, m_ssd_conv_w, m_ssd_conv_b, m_ssd_dt_bias, m_ssd_a_log, m_ssd_d, m_ssd_norm_g, m_w_out, m_ln2_g, m_ln2_b, m_ffn2_wg, m_ffn2_wu, m_ffn2_wd, m_ln3_g, m_ln3_b, m_pe_proj, m_pe_gate_w, m_pe_gate_b, v_ln1_g, v_ln1_b, v_ffn1_wg, v_ffn1_wu, v_ffn1_wd, v_w_in, v_lru_conv_w, v_lru_conv_b, v_lru_wa, v_lru_ba, v_lru_wx, v_lru_bx, v_lru_lambda, v_fox_bf, v_ssd_conv_w, v_ssd_conv_b, v_ssd_dt_bias, v_ssd_a_log, v_ssd_d, v_ssd_norm_g, v_w_out, v_ln2_g, v_ln2_b, v_ffn2_wg, v_ffn2_wu, v_ffn2_wd, v_ln3_g, v_ln3_b, v_pe_proj, v_pe_gate_w, v_pe_gate_b):
    given = dict(x=x, p=p, ln1_g=ln1_g, ln1_b=ln1_b, ffn1_wg=ffn1_wg, ffn1_wu=ffn1_wu, ffn1_wd=ffn1_wd, w_in=w_in, lru_conv_w=lru_conv_w, lru_conv_b=lru_conv_b, lru_wa=lru_wa, lru_ba=lru_ba, lru_wx=lru_wx, lru_bx=lru_bx, lru_lambda=lru_lambda, fox_bf=fox_bf, ssd_conv_w=ssd_conv_w, ssd_conv_b=ssd_conv_b, ssd_dt_bias=ssd_dt_bias, ssd_a_log=ssd_a_log, ssd_d=ssd_d, ssd_norm_g=ssd_norm_g, w_out=w_out, ln2_g=ln2_g, ln2_b=ln2_b, ffn2_wg=ffn2_wg, ffn2_wu=ffn2_wu, ffn2_wd=ffn2_wd, ln3_g=ln3_g, ln3_b=ln3_b, pe_proj=pe_proj, pe_gate_w=pe_gate_w, pe_gate_b=pe_gate_b, loss_target=loss_target, m_ln1_g=m_ln1_g, m_ln1_b=m_ln1_b, m_ffn1_wg=m_ffn1_wg, m_ffn1_wu=m_ffn1_wu, m_ffn1_wd=m_ffn1_wd, m_w_in=m_w_in, m_lru_conv_w=m_lru_conv_w, m_lru_conv_b=m_lru_conv_b, m_lru_wa=m_lru_wa, m_lru_ba=m_lru_ba, m_lru_wx=m_lru_wx, m_lru_bx=m_lru_bx, m_lru_lambda=m_lru_lambda, m_fox_bf=m_fox_bf, m_ssd_conv_w=m_ssd_conv_w, m_ssd_conv_b=m_ssd_conv_b, m_ssd_dt_bias=m_ssd_dt_bias, m_ssd_a_log=m_ssd_a_log, m_ssd_d=m_ssd_d, m_ssd_norm_g=m_ssd_norm_g, m_w_out=m_w_out, m_ln2_g=m_ln2_g, m_ln2_b=m_ln2_b, m_ffn2_wg=m_ffn2_wg, m_ffn2_wu=m_ffn2_wu, m_ffn2_wd=m_ffn2_wd, m_ln3_g=m_ln3_g, m_ln3_b=m_ln3_b, m_pe_proj=m_pe_proj, m_pe_gate_w=m_pe_gate_w, m_pe_gate_b=m_pe_gate_b, v_ln1_g=v_ln1_g, v_ln1_b=v_ln1_b, v_ffn1_wg=v_ffn1_wg, v_ffn1_wu=v_ffn1_wu, v_ffn1_wd=v_ffn1_wd, v_w_in=v_w_in, v_lru_conv_w=v_lru_conv_w, v_lru_conv_b=v_lru_conv_b, v_lru_wa=v_lru_wa, v_lru_ba=v_lru_ba, v_lru_wx=v_lru_wx, v_lru_bx=v_lru_bx, v_lru_lambda=v_lru_lambda, v_fox_bf=v_fox_bf, v_ssd_conv_w=v_ssd_conv_w, v_ssd_conv_b=v_ssd_conv_b, v_ssd_dt_bias=v_ssd_dt_bias, v_ssd_a_log=v_ssd_a_log, v_ssd_d=v_ssd_d, v_ssd_norm_g=v_ssd_norm_g, v_w_out=v_w_out, v_ln2_g=v_ln2_g, v_ln2_b=v_ln2_b, v_ffn2_wg=v_ffn2_wg, v_ffn2_wu=v_ffn2_wu, v_ffn2_wd=v_ffn2_wd, v_ln3_g=v_ln3_g, v_ln3_b=v_ln3_b, v_pe_proj=v_pe_proj, v_pe_gate_w=v_pe_gate_w, v_pe_gate_b=v_pe_gate_b)
    weights = {n: given[n] for n in TWIN_WEIGHTS}
    shared = {n: given[n] for n in SHARED_INPUTS}
    per_example = {n: given[n] for n in ['x', 'p']}
    grad_fn = _jax.value_and_grad(_loss, argnums=(0, 1))

    def one_microbatch(ex, loss_target):
        ex = dict(ex)
        diff = ex.pop(TWIN_DIFF_INPUT)
        return grad_fn(weights, diff, {**shared, **ex}, loss_target)

    if N_MICROBATCH == 1:
        loss, (grad_w, grad_x) = one_microbatch(per_example, given["loss_target"])
    else:
        def body(carry, xs):
            loss_sum, grad_sum = carry
            l_k, (gw_k, gx_k) = one_microbatch(xs[0], xs[1])
            with _jax.named_scope("update"):
                return (loss_sum + l_k, _jax.tree.map(_jnp.add, grad_sum, gw_k)), gx_k

        init = (_jnp.zeros((), _jnp.float32), _jax.tree.map(_jnp.zeros_like, weights))
        (loss, grad_w), grad_x = _jax.lax.scan(body, init, (per_example, given["loss_target"]))
    with _jax.named_scope("update"):
        delta_w, new_m, new_v = {}, {}, {}
        for n in TWIN_WEIGHTS:
            delta_w[n], new_m[n], new_v[n] = _adamw(weights[n], grad_w[n], given["m_" + n], given["v_" + n])
    return (loss, grad_x, *[grad_w[n] for n in TWIN_WEIGHTS], *[delta_w[n] for n in TWIN_WEIGHTS],
            *[new_m[n] for n in TWIN_WEIGHTS], *[new_v[n] for n in TWIN_WEIGHTS])
```

```python
import functools
import math

import jax
import jax.numpy as jnp
from jax import lax
from jax.experimental import pallas as pl
from jax.experimental.pallas import tpu as pltpu

F32 = jnp.float32
BF16 = jnp.bfloat16

D_MODEL = 1024
DEPTH = 2
PLE_DIM = 256
HEAD_DIM = 64
LRU_WIDTH = 256
LRU_HEADS = 4
LRU_C = 8.0
CONV_K = 4
ATT_WIDTH = 256
ATT_HEADS = 4
SSD_WIDTH = 512
SSD_HEADS = 8
SSD_GROUPS = 2
SSD_STATE = 128
SSD_CHUNK = 128
SSD_CONV_DIM = 1024
FFN_DIM = 2816
ALPHA = (2.0 * DEPTH) ** 0.25
LN_EPS = 1e-5
RMS_EPS = 1e-5
IN_WIDTH = 2828
ADAM_LR = 0.001
ADAM_B1 = 0.9
ADAM_B2 = 0.999
ADAM_EPS = 1e-08
ADAM_WD = 0.01
ADAM_STEP = 10

H_WIDTH = 3072
COL_XBC, COL_Z, COL_U, COL_G, COL_Q, COL_K, COL_V, COL_SMALL = 0, 1024, 1536, 1792, 2048, 2304, 2560, 2816
LANE_F = 0
LANE_DT = 4
LANES = 128
SUBLANES = 8
NEG = -1e30

VMEM_LIMIT = 48 * 1024 * 1024

N_CHIPS = 4
MESH_AXES = ("x", "y", "c")


def _params(n):
    return pltpu.CompilerParams(dimension_semantics=("arbitrary",) * n, vmem_limit_bytes=VMEM_LIMIT)


def _pick(n, cands):
    for c in cands:
        if n % c == 0:
            return c
    return n


def _iota(shape, dim):
    return lax.broadcasted_iota(jnp.int32, shape, dim)


def _shift_down(x, s, prev8):
    if s == 0:
        return x
    r = pltpu.roll(x, s, 0)
    pr = pltpu.roll(prev8, s, 0)
    head = jnp.where(_iota(pr.shape, 0) < s, pr, r[:SUBLANES])
    return jnp.concatenate([head, r[SUBLANES:]], axis=0)


def _shift_up(x, s, next8):
    if s == 0:
        return x
    n = x.shape[0]
    r = pltpu.roll(x, n - s, 0)
    nr = pltpu.roll(next8, SUBLANES - s, 0)
    tail = jnp.where(_iota(nr.shape, 0) >= SUBLANES - s, nr, r[n - SUBLANES:])
    return jnp.concatenate([r[:n - SUBLANES], tail], axis=0)


def _scan_fwd(a, b):
    n = a.shape[0]
    row = _iota(a.shape, 0)
    d = 1
    while d < n:
        keep = row >= d
        a_s = jnp.where(keep, pltpu.roll(a, d, 0), 1.0)
        b_s = jnp.where(keep, pltpu.roll(b, d, 0), 0.0)
        b = a * b_s + b
        a = a * a_s
        d *= 2
    return a, b


def _scan_bwd(a, b):
    n = a.shape[0]
    row = _iota(a.shape, 0)
    d = 1
    while d < n:
        keep = row < n - d
        a_s = jnp.where(keep, pltpu.roll(a, n - d, 0), 1.0)
        b_s = jnp.where(keep, pltpu.roll(b, n - d, 0), 0.0)
        b = a * b_s + b
        a = a * a_s
        d *= 2
    return a, b


def _cumsum_rows(x, reverse=False):
    n = x.shape[0]
    row = _iota(x.shape, 0)
    d = 1
    while d < n:
        if reverse:
            x = x + jnp.where(row < n - d, pltpu.roll(x, n - d, 0), 0.0)
        else:
            x = x + jnp.where(row >= d, pltpu.roll(x, d, 0), 0.0)
        d *= 2
    return x


def _col(x, lane):
    return jnp.sum(jnp.where(_iota(x.shape, 1) == lane, x, 0.0), axis=1, keepdims=True)


def _row(x, r):
    return jnp.sum(jnp.where(_iota(x.shape, 0) == r, x, 0.0), axis=0, keepdims=True)


def _sigmoid(x):
    return jax.nn.sigmoid(x)


def _softplus(x):
    return jnp.maximum(x, 0.0) + jnp.log(1.0 + jnp.exp(-jnp.abs(x)))


def _gelu_and_grad(x):
    c0 = math.sqrt(2.0 / math.pi)
    inner = c0 * (x + 0.044715 * x * x * x)
    t = jnp.tanh(inner)
    g = 0.5 * x * (1.0 + t)
    dg = 0.5 * (1.0 + t) + 0.5 * x * (1.0 - t * t) * c0 * (1.0 + 3.0 * 0.044715 * x * x)
    return g, dg


def _dot(a, b, ca, cb):
    return lax.dot_general(a, b, (((ca,), (cb,)), ((), ())), preferred_element_type=F32)


def _conv_taps(xr, prev8, w, bias):
    y = bias + w[CONV_K - 1:CONV_K, :] * xr
    for j in range(CONV_K - 1):
        y = y + w[j:j + 1, :] * _shift_down(xr, CONV_K - 1 - j, prev8)
    return y


def _conv_taps_bwd(dy, next8, w, xr):
    dx = None
    dws = []
    for j in range(CONV_K):
        sh = _shift_up(dy, CONV_K - 1 - j, next8)
        term = w[j:j + 1, :] * sh
        dx = term if dx is None else dx + term
        dws.append(jnp.sum(sh * xr, axis=0, keepdims=True))
    return dx, jnp.concatenate(dws, axis=0)


def _head_expand(v, lane0, nheads, width):
    rows = v.shape[0]
    colhead = _iota((rows, width), 1) // HEAD_DIM
    out = jnp.zeros((rows, width), F32)
    for h in range(nheads):
        out = jnp.where(colhead == h, _col(v, lane0 + h), out)
    return out


def _head_reduce(x, lane0, nheads):
    rows = x.shape[0]
    colhead = _iota(x.shape, 1) // HEAD_DIM
    lane = _iota((rows, LANES), 1)
    out = jnp.zeros((rows, LANES), F32)
    for h in range(nheads):
        s = jnp.sum(jnp.where(colhead == h, x, 0.0), axis=1, keepdims=True)
        out = jnp.where(lane == lane0 + h, s, out)
    return out


def _mm(a, b, *, ta=False, tb=False, scale=1.0, out_dtype=F32, name):
    if ta:
        kk, m = a.shape
    else:
        m, kk = a.shape
    n = b.shape[0] if tb else b.shape[1]
    tm = _pick(m, (512, 256, 128))
    tn = _pick(n, (1024, 1408, 512, 256, 128))
    tk = _pick(kk, (1024, 1408, 512, 256, 128))
    nk = kk // tk
    dn_a = 0 if ta else 1
    dn_b = 1 if tb else 0

    def body(a_ref, b_ref, o_ref, acc):
        k = pl.program_id(2)

        @pl.when(k == 0)
        def _():
            acc[...] = jnp.zeros_like(acc)

        acc[...] += _dot(a_ref[...].astype(BF16), b_ref[...].astype(BF16), dn_a, dn_b)

        @pl.when(k == nk - 1)
        def _():
            o_ref[...] = (acc[...] * scale).astype(out_dtype)

    a_spec = pl.BlockSpec((tk, tm), lambda i, j, k: (k, i)) if ta else pl.BlockSpec((tm, tk), lambda i, j, k: (i, k))
    b_spec = pl.BlockSpec((tn, tk), lambda i, j, k: (j, k)) if tb else pl.BlockSpec((tk, tn), lambda i, j, k: (k, j))
    return pl.pallas_call(
        body, name=name, grid=(m // tm, n // tn, nk),
        in_specs=[a_spec, b_spec],
        out_specs=pl.BlockSpec((tm, tn), lambda i, j, k: (i, j)),
        out_shape=jax.ShapeDtypeStruct((m, n), out_dtype),
        scratch_shapes=[pltpu.VMEM((tm, tn), F32)],
        compiler_params=_params(3),
    )(a, b)


def _mm_swiglu(xb, wg, wu, *, name):
    t, d = xb.shape
    n = wg.shape[1]
    tm = _pick(t, (512, 256, 128))
    tn = _pick(n, (256, 128))

    def body(x_ref, wg_ref, wu_ref, g_ref, u_ref, a_ref):
        x = x_ref[...]
        g = _dot(x, wg_ref[...], 1, 0)
        u = _dot(x, wu_ref[...], 1, 0)
        g_ref[...] = g.astype(BF16)
        u_ref[...] = u.astype(BF16)
        a_ref[...] = (g * _sigmoid(g) * u).astype(BF16)

    o = jax.ShapeDtypeStruct((t, n), BF16)
    ospec = pl.BlockSpec((tm, tn), lambda i, j: (i, j))
    return pl.pallas_call(
        body, name=name, grid=(t // tm, n // tn),
        in_specs=[pl.BlockSpec((tm, d), lambda i, j: (i, 0)),
                  pl.BlockSpec((d, tn), lambda i, j: (0, j)),
                  pl.BlockSpec((d, tn), lambda i, j: (0, j))],
        out_specs=[ospec, ospec, ospec], out_shape=[o, o, o],
        compiler_params=_params(2),
    )(xb, wg, wu)


def _mm_swiglu_bwd(dr, wd, g, u, *, scale, name):
    t, d = dr.shape
    n = wd.shape[0]
    tm = _pick(t, (512, 256, 128))
    tn = _pick(n, (256, 128))

    def body(dr_ref, wd_ref, g_ref, u_ref, dg_ref, du_ref):
        da = _dot(dr_ref[...].astype(BF16), wd_ref[...], 1, 1) * scale
        gg = g_ref[...].astype(F32)
        uu = u_ref[...].astype(F32)
        sg = _sigmoid(gg)
        dg_ref[...] = (da * uu * (sg * (1.0 + gg * (1.0 - sg)))).astype(BF16)
        du_ref[...] = (da * gg * sg).astype(BF16)

    o = jax.ShapeDtypeStruct((t, n), BF16)
    ospec = pl.BlockSpec((tm, tn), lambda i, j: (i, j))
    return pl.pallas_call(
        body, name=name, grid=(t // tm, n // tn),
        in_specs=[pl.BlockSpec((tm, d), lambda i, j: (i, 0)),
                  pl.BlockSpec((tn, d), lambda i, j: (j, 0)),
                  ospec, ospec],
        out_specs=[ospec, ospec], out_shape=[o, o],
        compiler_params=_params(2),
    )(dr, wd, g, u)


def _mm_ln(a, w, resid, gain, bias, *, rscale, mscale, name):
    t, kk = a.shape
    d = w.shape[1]
    tm = _pick(t, (256, 128))
    tk = _pick(kk, (1024, 1408, 512, 256, 128))
    nk = kk // tk

    def body(a_ref, w_ref, r_ref, g_ref, b_ref, y_ref, yb_ref, xh_ref, rs_ref, acc):
        k = pl.program_id(1)

        @pl.when(k == 0)
        def _():
            acc[...] = jnp.zeros_like(acc)

        acc[...] += _dot(a_ref[...].astype(BF16), w_ref[...], 1, 0)

        @pl.when(k == nk - 1)
        def _():
            r = rscale * r_ref[...] + mscale * acc[...]
            mu = jnp.mean(r, axis=1, keepdims=True)
            xc = r - mu
            var = jnp.mean(xc * xc, axis=1, keepdims=True)
            rstd = lax.rsqrt(var + LN_EPS)
            xh = xc * rstd
            y = xh * g_ref[...] + b_ref[...]
            y_ref[...] = y
            yb_ref[...] = y.astype(BF16)
            xh_ref[...] = xh
            rs_ref[...] = rstd

    row = pl.BlockSpec((tm, d), lambda i, k: (i, 0))
    vec = pl.BlockSpec((1, d), lambda i, k: (0, 0))
    return pl.pallas_call(
        body, name=name, grid=(t // tm, nk),
        in_specs=[pl.BlockSpec((tm, tk), lambda i, k: (i, k)),
                  pl.BlockSpec((tk, d), lambda i, k: (k, 0)), row, vec, vec],
        out_specs=[row, row, row, pl.BlockSpec((tm, 1), lambda i, k: (i, 0))],
        out_shape=[jax.ShapeDtypeStruct((t, d), F32), jax.ShapeDtypeStruct((t, d), BF16),
                   jax.ShapeDtypeStruct((t, d), F32), jax.ShapeDtypeStruct((t, 1), F32)],
        scratch_shapes=[pltpu.VMEM((tm, d), F32)],
        compiler_params=_params(2),
    )(a, w, resid, gain.reshape(1, d), bias.reshape(1, d))


def _bwd_proj(pairs, resid, *, rscale, ln, name):
    t, kk = pairs[0][0].shape
    d = pairs[0][1].shape[0]
    tm = _pick(t, (256, 128))
    tk = _pick(kk, (1024, 1408, 512, 256, 128))
    nk = kk // tk
    nt = t // tm
    npair = len(pairs)
    has_ln = ln is not None

    def body(*refs):
        ab = refs[:2 * npair]
        r_ref = refs[2 * npair]
        pos = 2 * npair + 1
        if has_ln:
            xh_ref, rs_ref, g_ref = refs[pos:pos + 3]
            pos += 3
            o_ref, dg_ref, db_ref = refs[pos:pos + 3]
            pos += 3
        else:
            o_ref = refs[pos]
            pos += 1
        acc = refs[pos]
        i = pl.program_id(0)
        k = pl.program_id(1)

        @pl.when(k == 0)
        def _():
            acc[...] = jnp.zeros_like(acc)

        for q in range(npair):
            acc[...] += _dot(ab[2 * q][...].astype(BF16), ab[2 * q + 1][...], 1, 1)

        @pl.when(k == nk - 1)
        def _():
            dy = rscale * r_ref[...] + acc[...]
            if not has_ln:
                o_ref[...] = dy
                return
            xh = xh_ref[...]
            w = dy * g_ref[...]
            m1 = jnp.mean(w, axis=1, keepdims=True)
            m2 = jnp.mean(w * xh, axis=1, keepdims=True)
            o_ref[...] = rs_ref[...] * (w - m1 - xh * m2)

            @pl.when(i == 0)
            def _():
                dg_ref[...] = jnp.zeros_like(dg_ref)
                db_ref[...] = jnp.zeros_like(db_ref)

            dg_ref[...] += jnp.sum(dy * xh, axis=0, keepdims=True)
            db_ref[...] += jnp.sum(dy, axis=0, keepdims=True)

    row = pl.BlockSpec((tm, d), lambda i, k: (i, 0))
    vec = pl.BlockSpec((1, d), lambda i, k: (0, 0))
    in_specs, args = [], []
    for a, b in pairs:
        in_specs += [pl.BlockSpec((tm, tk), lambda i, k: (i, k)), pl.BlockSpec((d, tk), lambda i, k: (0, k))]
        args += [a, b]
    in_specs.append(row)
    args.append(resid)
    out_specs = [row]
    out_shape = [jax.ShapeDtypeStruct((t, d), F32)]
    if has_ln:
        xh, rs, gain = ln
        in_specs += [row, pl.BlockSpec((tm, 1), lambda i, k: (i, 0)), vec]
        args += [xh, rs, gain.reshape(1, d)]
        out_specs += [vec, vec]
        out_shape += [jax.ShapeDtypeStruct((1, d), F32)] * 2
    return pl.pallas_call(
        body, name=name, grid=(nt, nk), in_specs=in_specs, out_specs=out_specs, out_shape=out_shape,
        scratch_shapes=[pltpu.VMEM((tm, d), F32)],
        compiler_params=_params(2),
    )(*args)


def _mm_pe(x3, x3b, pb, wgate, bgate, wproj, *, name):
    t, d = x3.shape
    pd = pb.shape[1]
    tm = _pick(t, (512, 256, 128))
    tn = _pick(d, (512, 256, 128))

    def body(x_ref, xb_ref, p_ref, wg_ref, bg_ref, wp_ref, y_ref, yb_ref, sg_ref, e_ref):
        sg = _sigmoid(_dot(xb_ref[...], wg_ref[...], 1, 0) + bg_ref[...])
        e = _dot(p_ref[...], wp_ref[...], 1, 0)
        y = x_ref[...] + sg * e
        y_ref[...] = y
        yb_ref[...] = y.astype(BF16)
        sg_ref[...] = sg.astype(BF16)
        e_ref[...] = e.astype(BF16)

    ospec = pl.BlockSpec((tm, tn), lambda i, j: (i, j))
    ob = jax.ShapeDtypeStruct((t, d), BF16)
    return pl.pallas_call(
        body, name=name, grid=(t // tm, d // tn),
        in_specs=[ospec, pl.BlockSpec((tm, d), lambda i, j: (i, 0)), pl.BlockSpec((tm, pd), lambda i, j: (i, 0)),
                  pl.BlockSpec((d, tn), lambda i, j: (0, j)), pl.BlockSpec((1, tn), lambda i, j: (0, j)),
                  pl.BlockSpec((pd, tn), lambda i, j: (0, j))],
        out_specs=[ospec, ospec, ospec, ospec],
        out_shape=[jax.ShapeDtypeStruct((t, d), F32), ob, ob, ob],
        compiler_params=_params(2),
    )(x3, x3b, pb, wgate, bgate.reshape(1, d), wproj)


def _pe_bwd_elem(dx4, sg, e, *, name):
    t, d = dx4.shape
    tm = _pick(t, (512, 256, 128))

    def body(dx_ref, sg_ref, e_ref, dgp_ref, de_ref, db_ref):
        dx = dx_ref[...]
        s = sg_ref[...].astype(F32)
        dgp = dx * e_ref[...].astype(F32) * s * (1.0 - s)
        dgp_ref[...] = dgp.astype(BF16)
        de_ref[...] = (dx * s).astype(BF16)

        @pl.when(pl.program_id(0) == 0)
        def _():
            db_ref[...] = jnp.zeros_like(db_ref)

        db_ref[...] += jnp.sum(dgp, axis=0, keepdims=True)

    row = pl.BlockSpec((tm, d), lambda i: (i, 0))
    ob = jax.ShapeDtypeStruct((t, d), BF16)
    return pl.pallas_call(
        body, name=name, grid=(t // tm,), in_specs=[row, row, row],
        out_specs=[row, row, pl.BlockSpec((1, d), lambda i: (0, 0))],
        out_shape=[ob, ob, jax.ShapeDtypeStruct((1, d), F32)],
        compiler_params=_params(1),
    )(dx4, sg, e)


def _loss_kernel(y, target, *, name):
    t, d = y.shape
    tm = _pick(t, (512, 256, 128))

    def body(y_ref, t_ref, dy_ref, l_ref):
        diff = y_ref[...] - t_ref[...]
        dy_ref[...] = diff * (1.0 / d)

        @pl.when(pl.program_id(0) == 0)
        def _():
            l_ref[...] = jnp.zeros_like(l_ref)

        part = jnp.sum(jnp.mean(diff * diff, axis=1, keepdims=True), axis=0, keepdims=True)
        l_ref[...] += 0.5 * part

    row = pl.BlockSpec((tm, d), lambda i: (i, 0))
    return pl.pallas_call(
        body, name=name, grid=(t // tm,), in_specs=[row, row],
        out_specs=[row, pl.BlockSpec((1, 1), lambda i: (0, 0))],
        out_shape=[jax.ShapeDtypeStruct((t, d), F32), jax.ShapeDtypeStruct((1, 1), F32)],
        compiler_params=_params(1),
    )(y, target)


LRU_TM = 256


def _lru_gate_terms(r, lam):
    sp = _softplus(-lam)
    la = -LRU_C * r * sp
    a = jnp.exp(la)
    em = jnp.tanh(la) * (jnp.exp(2.0 * la) + 1.0)
    s = jnp.sqrt(-em)
    return la, a, s, sp


def _lru_fwd(hbuf, conv_w, conv_b, wa, ba, wx, bx, lam, *, name):
    t = hbuf.shape[0]
    w = LRU_WIDTH
    tm = _pick(t, (LRU_TM, 128))
    cu, cg = COL_U // w, COL_G // w
    hb = tm // SUBLANES

    def body(u_ref, up_ref, g_ref, cw_ref, cb_ref, wa_ref, ba_ref, wx_ref, bx_ref, lam_ref,
             y_ref, u_out, r_out, i_out, a_out, h_out, carry):
        i = pl.program_id(0)

        @pl.when(i == 0)
        def _():
            carry[...] = jnp.zeros_like(carry)

        prev = jnp.where(i == 0, 0.0, up_ref[...])
        u = _conv_taps(u_ref[...], prev, cw_ref[...], cb_ref[...])
        ub = u.astype(BF16)
        r = _sigmoid(_dot(ub, wa_ref[...], 1, 0) + ba_ref[...])
        ig = _sigmoid(_dot(ub, wx_ref[...], 1, 0) + bx_ref[...])
        _, a, s, _ = _lru_gate_terms(r, lam_ref[...])
        b = s * (ig * u)
        acum, hs = _scan_fwd(a, b)
        h = hs + acum * carry[0:1, :]
        carry[...] = jnp.broadcast_to(h[tm - 1:tm, :], carry.shape)
        gl, _ = _gelu_and_grad(g_ref[...])
        y_ref[...] = h * gl
        u_out[...] = u
        r_out[...] = r
        i_out[...] = ig
        a_out[...] = a
        h_out[...] = h

    row = pl.BlockSpec((tm, w), lambda i: (i, 0))
    vec = pl.BlockSpec((1, w), lambda i: (0, 0))
    mat = pl.BlockSpec((w, w), lambda i: (0, 0))
    o = jax.ShapeDtypeStruct((t, w), F32)
    return pl.pallas_call(
        body, name=name, grid=(t // tm,),
        in_specs=[pl.BlockSpec((tm, w), lambda i: (i, cu)),
                  pl.BlockSpec((SUBLANES, w), lambda i: (jnp.maximum(i * hb - 1, 0), cu)),
                  pl.BlockSpec((tm, w), lambda i: (i, cg)),
                  pl.BlockSpec((CONV_K, w), lambda i: (0, 0)), vec, mat, vec, mat, vec, vec],
        out_specs=[row] * 6, out_shape=[o] * 6,
        scratch_shapes=[pltpu.VMEM((SUBLANES, w), F32)],
        compiler_params=_params(1),
    )(hbuf, hbuf, hbuf, conv_w, conv_b, wa, ba, wx, bx, lam)


def _lru_bwd(dymix, hbuf, u, r, ig, a, h, conv_w, wa, wx, lam, *, name):
    t = hbuf.shape[0]
    w = LRU_WIDTH
    tm = _pick(t, (LRU_TM, 128))
    nb = t // tm
    cu, cg = COL_U // w, COL_G // w
    hb = tm // SUBLANES
    last8 = t // SUBLANES - 1

    def body(dy_ref, ur_ref, g_ref, u_ref, r_ref, i_ref, a_ref, an_ref, h_ref, hp_ref,
             cw_ref, wa_ref, wx_ref, lam_ref,
             dur_ref, dgr_ref, dcw_ref, dcb_ref, dwa_ref, dba_ref, dwx_ref, dbx_ref, dlam_ref,
             lcarry, dnext):
        i = pl.program_id(0)
        ib = nb - 1 - i

        @pl.when(i == 0)
        def _():
            lcarry[...] = jnp.zeros_like(lcarry)
            dnext[...] = jnp.zeros_like(dnext)
            for ref in (dcw_ref, dcb_ref, dwa_ref, dba_ref, dwx_ref, dbx_ref, dlam_ref):
                ref[...] = jnp.zeros_like(ref)

        dy = dy_ref[...]
        hh = h_ref[...]
        av = a_ref[...]
        uu = u_ref[...]
        rr = r_ref[...]
        ii = i_ref[...]
        lam_v = lam_ref[...]
        gl, dgl = _gelu_and_grad(g_ref[...])
        dgr_ref[...] = dy * hh * dgl
        dh_out = dy * gl
        a_next = _shift_up(av, 1, jnp.where(ib == nb - 1, 0.0, an_ref[...]))
        acum, ls = _scan_bwd(a_next, dh_out)
        lam_adj = ls + acum * lcarry[0:1, :]
        lcarry[...] = jnp.broadcast_to(lam_adj[0:1, :], lcarry.shape)
        h_prev = _shift_down(hh, 1, jnp.where(ib == 0, 0.0, hp_ref[...]))
        da = lam_adj * h_prev
        _, a2, s, sp = _lru_gate_terms(rr, lam_v)
        d_igu = lam_adj * s
        ds = lam_adj * ii * uu
        dla = da * a2 - ds * (a2 * a2) / s
        dr = dla * (-LRU_C * sp)
        dlam_ref[...] += jnp.sum(dla * (LRU_C * rr * _sigmoid(-lam_v)), axis=0, keepdims=True)
        dpre_r = dr * rr * (1.0 - rr)
        dpre_i = d_igu * uu * ii * (1.0 - ii)
        prb = dpre_r.astype(BF16)
        pib = dpre_i.astype(BF16)
        ub = uu.astype(BF16)
        du = d_igu * ii + _dot(prb, wa_ref[...], 1, 1) + _dot(pib, wx_ref[...], 1, 1)
        dwa_ref[...] += _dot(ub, prb, 0, 0)
        dwx_ref[...] += _dot(ub, pib, 0, 0)
        dba_ref[...] += jnp.sum(dpre_r, axis=0, keepdims=True)
        dbx_ref[...] += jnp.sum(dpre_i, axis=0, keepdims=True)
        dur, dws = _conv_taps_bwd(du, dnext[...], cw_ref[...], ur_ref[...])
        dur_ref[...] = dur
        dcw_ref[...] += dws
        dcb_ref[...] += jnp.sum(du, axis=0, keepdims=True)
        dnext[...] = du[:SUBLANES]

    def rowspec(col):
        return pl.BlockSpec((tm, w), lambda i: (nb - 1 - i, col))

    row = rowspec(0)
    nxt = pl.BlockSpec((SUBLANES, w), lambda i: (jnp.minimum((nb - i) * hb, last8), 0))
    prv = pl.BlockSpec((SUBLANES, w), lambda i: (jnp.maximum((nb - 1 - i) * hb - 1, 0), 0))
    vec = pl.BlockSpec((1, w), lambda i: (0, 0))
    mat = pl.BlockSpec((w, w), lambda i: (0, 0))
    cw = pl.BlockSpec((CONV_K, w), lambda i: (0, 0))
    o = jax.ShapeDtypeStruct((t, w), F32)
    v1 = jax.ShapeDtypeStruct((1, w), F32)
    m1 = jax.ShapeDtypeStruct((w, w), F32)
    return pl.pallas_call(
        body, name=name, grid=(nb,),
        in_specs=[rowspec(0), rowspec(cu), rowspec(cg), row, row, row, row, nxt, row, prv, cw, mat, mat, vec],
        out_specs=[row, row, cw, vec, mat, vec, mat, vec, vec],
        out_shape=[o, o, jax.ShapeDtypeStruct((CONV_K, w), F32), v1, m1, v1, m1, v1, v1],
        scratch_shapes=[pltpu.VMEM((SUBLANES, w), F32), pltpu.VMEM((SUBLANES, w), F32)],
        compiler_params=_params(1),
    )(dymix, hbuf, hbuf, u, r, ig, a, a, h, h, conv_w, wa, wx, lam)


FOX_T = 512
FOX_PREP_TM = 256


def _log_sigmoid(x):
    return jnp.minimum(x, 0.0) - jnp.log(1.0 + jnp.exp(-jnp.abs(x)))


def _fox_prep(hbuf, bf_vec, *, name):
    t = hbuf.shape[0]
    tm = _pick(t, (FOX_PREP_TM, 128))
    cs = COL_SMALL // LANES

    def body(s_ref, b_ref, f_ref, carry):
        i = pl.program_id(0)

        @pl.when(i == 0)
        def _():
            carry[...] = jnp.zeros_like(carry)

        lf = _log_sigmoid(s_ref[...] + b_ref[...])
        f = _cumsum_rows(lf) + carry[0:1, :]
        carry[...] = jnp.broadcast_to(f[tm - 1:tm, :], carry.shape)
        f_ref[...] = f

    return pl.pallas_call(
        body, name=name, grid=(t // tm,),
        in_specs=[pl.BlockSpec((tm, LANES), lambda i: (i, cs)), pl.BlockSpec((1, LANES), lambda i: (0, 0))],
        out_specs=pl.BlockSpec((tm, LANES), lambda i: (i, 0)),
        out_shape=jax.ShapeDtypeStruct((t, LANES), F32),
        scratch_shapes=[pltpu.VMEM((SUBLANES, LANES), F32)],
        compiler_params=_params(1),
    )(hbuf, bf_vec)


def _fox_post(dfc, hbuf, bf_vec, *, name):
    t = hbuf.shape[0]
    tm = _pick(t, (FOX_PREP_TM, 128))
    nb = t // tm
    cs = COL_SMALL // LANES

    def body(df_ref, s_ref, b_ref, o_ref, db_ref, carry):
        i = pl.program_id(0)

        @pl.when(i == 0)
        def _():
            carry[...] = jnp.zeros_like(carry)
            db_ref[...] = jnp.zeros_like(db_ref)

        dlf = _cumsum_rows(df_ref[...], reverse=True) + carry[0:1, :]
        carry[...] = jnp.broadcast_to(dlf[0:1, :], carry.shape)
        dl = dlf * _sigmoid(-(s_ref[...] + b_ref[...]))
        dl = jnp.where(_iota(dl.shape, 1) < ATT_HEADS, dl, 0.0)
        o_ref[...] = dl
        db_ref[...] += jnp.sum(dl, axis=0, keepdims=True)

    vec = pl.BlockSpec((1, LANES), lambda i: (0, 0))
    return pl.pallas_call(
        body, name=name, grid=(nb,),
        in_specs=[pl.BlockSpec((tm, LANES), lambda i: (nb - 1 - i, 0)),
                  pl.BlockSpec((tm, LANES), lambda i: (nb - 1 - i, cs)), vec],
        out_specs=[pl.BlockSpec((tm, LANES), lambda i: (nb - 1 - i, 0)), vec],
        out_shape=[jax.ShapeDtypeStruct((t, LANES), F32), jax.ShapeDtypeStruct((1, LANES), F32)],
        scratch_shapes=[pltpu.VMEM((SUBLANES, LANES), F32)],
        compiler_params=_params(1),
    )(dfc, hbuf, bf_vec)


def _fox_scores(qp, kpb, fq, fk, h, hm, causal):
    qm = jnp.where(hm, qp, 0.0).astype(BF16)
    s = _dot(qm, kpb, 1, 1) * (HEAD_DIM ** -0.5) + (_col(fq, h) - _row(fk, h))
    return jnp.where(causal, s, NEG), qm


def _fox_masks(i, j, tq):
    row = i * tq + _iota((tq, tq), 0)
    col = j * tq + _iota((tq, tq), 1)
    lane = _iota((1, LANES), 1)
    return col <= row, (lane < HEAD_DIM, lane >= HEAD_DIM)


def _fox_fwd(hbuf, fc, ft, *, name):
    t = hbuf.shape[0]
    w = ATT_WIDTH
    tq = _pick(t, (FOX_T, 256, 128))
    nq = t // tq
    cq, ck, cv = COL_Q // w, COL_K // w, COL_V // w

    def body(q_ref, k_ref, v_ref, fq_ref, fk_ref, o_ref, lse_ref, m_s, l_s, acc_s):
        i = pl.program_id(0)
        j = pl.program_id(1)

        @pl.when(j == 0)
        def _():
            m_s[...] = jnp.full_like(m_s, NEG)
            l_s[...] = jnp.zeros_like(l_s)
            acc_s[...] = jnp.zeros_like(acc_s)

        @pl.when(j <= i)
        def _():
            causal, hms = _fox_masks(i, j, tq)
            fq = fq_ref[...]
            fk = fk_ref[...]
            for pr in range(2):
                sl = slice(LANES * pr, LANES * (pr + 1))
                qp = q_ref[:, sl]
                kpb = k_ref[:, sl].astype(BF16)
                vpb = v_ref[:, sl].astype(BF16)
                for hh in range(2):
                    h = 2 * pr + hh
                    s, _ = _fox_scores(qp, kpb, fq, fk, h, hms[hh], causal)
                    m_prev = m_s[h]
                    m_new = jnp.maximum(m_prev, jnp.max(s, axis=1, keepdims=True))
                    alpha = jnp.exp(m_prev - m_new)
                    p = jnp.exp(s - m_new)
                    l_s[h] = alpha * l_s[h] + jnp.sum(p, axis=1, keepdims=True)
                    m_s[h] = m_new
                    pv = _dot(p.astype(BF16), vpb, 1, 0)
                    acc = acc_s[:, sl]
                    acc_s[:, sl] = jnp.where(hms[hh], alpha * acc + pv, acc)

        @pl.when(j == i)
        def _():
            _, hms = _fox_masks(i, j, tq)
            for pr in range(2):
                sl = slice(LANES * pr, LANES * (pr + 1))
                acc = acc_s[:, sl]
                o_ref[:, sl] = jnp.where(hms[0], acc / l_s[2 * pr], acc / l_s[2 * pr + 1])
                for hh in range(2):
                    h = 2 * pr + hh
                    lse_ref[h] = m_s[h] + jnp.log(l_s[h])

    return pl.pallas_call(
        body, name=name, grid=(nq, nq),
        in_specs=[pl.BlockSpec((tq, w), lambda i, j: (i, cq)),
                  pl.BlockSpec((tq, w), lambda i, j: (jnp.minimum(j, i), ck)),
                  pl.BlockSpec((tq, w), lambda i, j: (jnp.minimum(j, i), cv)),
                  pl.BlockSpec((tq, LANES), lambda i, j: (i, 0)),
                  pl.BlockSpec((SUBLANES, tq), lambda i, j: (0, jnp.minimum(j, i)))],
        out_specs=[pl.BlockSpec((tq, w), lambda i, j: (i, 0)),
                   pl.BlockSpec((ATT_HEADS, tq, 1), lambda i, j: (0, i, 0))],
        out_shape=[jax.ShapeDtypeStruct((t, w), F32), jax.ShapeDtypeStruct((ATT_HEADS, t, 1), F32)],
        scratch_shapes=[pltpu.VMEM((ATT_HEADS, tq, 1), F32), pltpu.VMEM((ATT_HEADS, tq, 1), F32),
                        pltpu.VMEM((tq, w), F32)],
        compiler_params=_params(2),
    )(hbuf, hbuf, hbuf, fc, ft)


def _fox_bwd_kv(hbuf, fc, ft, dymix, o, lse, *, name):
    t = hbuf.shape[0]
    w = ATT_WIDTH
    tq = _pick(t, (FOX_T, 256, 128))
    nq = t // tq
    cq, ck, cv = COL_Q // w, COL_K // w, COL_V // w
    cdo = ATT_WIDTH // w

    def body(q_ref, k_ref, v_ref, fq_ref, fk_ref, do_ref, o_ref, lse_ref, dk_ref, dv_ref, dfk_ref,
             dk_s, dv_s, dfk_s):
        j = pl.program_id(0)
        i = pl.program_id(1)

        @pl.when(i == 0)
        def _():
            dk_s[...] = jnp.zeros_like(dk_s)
            dv_s[...] = jnp.zeros_like(dv_s)
            dfk_s[...] = jnp.zeros_like(dfk_s)

        @pl.when(i >= j)
        def _():
            causal, hms = _fox_masks(i, j, tq)
            fq = fq_ref[...]
            fk = fk_ref[...]
            rows8 = _iota((SUBLANES, tq), 0)
            for pr in range(2):
                sl = slice(LANES * pr, LANES * (pr + 1))
                qp = q_ref[:, sl]
                kpb = k_ref[:, sl].astype(BF16)
                vpb = v_ref[:, sl].astype(BF16)
                dop = do_ref[:, sl]
                op = o_ref[:, sl]
                for hh in range(2):
                    h = 2 * pr + hh
                    s, qm = _fox_scores(qp, kpb, fq, fk, h, hms[hh], causal)
                    p = jnp.exp(s - lse_ref[h])
                    dom = jnp.where(hms[hh], dop, 0.0)
                    domb = dom.astype(BF16)
                    dv_s[:, sl] += _dot(p.astype(BF16), domb, 0, 0)
                    dp = _dot(domb, vpb, 1, 1)
                    delta = jnp.sum(dom * op, axis=1, keepdims=True)
                    ds = p * (dp - delta)
                    dk_s[:, sl] += _dot(ds.astype(BF16), qm, 0, 0) * (HEAD_DIM ** -0.5)
                    dfk_s[...] += jnp.where(rows8 == h, jnp.sum(ds, axis=0, keepdims=True), 0.0)

        @pl.when(i == nq - 1)
        def _():
            dk_ref[...] = dk_s[...]
            dv_ref[...] = dv_s[...]
            dfk_ref[...] = dfk_s[...]

    qi = lambda j, i: jnp.maximum(i, j)
    return pl.pallas_call(
        body, name=name, grid=(nq, nq),
        in_specs=[pl.BlockSpec((tq, w), lambda j, i: (qi(j, i), cq)),
                  pl.BlockSpec((tq, w), lambda j, i: (j, ck)),
                  pl.BlockSpec((tq, w), lambda j, i: (j, cv)),
                  pl.BlockSpec((tq, LANES), lambda j, i: (qi(j, i), 0)),
                  pl.BlockSpec((SUBLANES, tq), lambda j, i: (0, j)),
                  pl.BlockSpec((tq, w), lambda j, i: (qi(j, i), cdo)),
                  pl.BlockSpec((tq, w), lambda j, i: (qi(j, i), 0)),
                  pl.BlockSpec((ATT_HEADS, tq, 1), lambda j, i: (0, qi(j, i), 0))],
        out_specs=[pl.BlockSpec((tq, w), lambda j, i: (j, 0)), pl.BlockSpec((tq, w), lambda j, i: (j, 0)),
                   pl.BlockSpec((SUBLANES, tq), lambda j, i: (0, j))],
        out_shape=[jax.ShapeDtypeStruct((t, w), F32), jax.ShapeDtypeStruct((t, w), F32),
                   jax.ShapeDtypeStruct((SUBLANES, t), F32)],
        scratch_shapes=[pltpu.VMEM((tq, w), F32), pltpu.VMEM((tq, w), F32), pltpu.VMEM((SUBLANES, tq), F32)],
        compiler_params=_params(2),
    )(hbuf, hbuf, hbuf, fc, ft, dymix, o, lse)


def _fox_bwd_q(hbuf, fc, ft, dymix, o, lse, *, name):
    t = hbuf.shape[0]
    w = ATT_WIDTH
    tq = _pick(t, (FOX_T, 256, 128))
    nq = t // tq
    cq, ck, cv = COL_Q // w, COL_K // w, COL_V // w
    cdo = ATT_WIDTH // w

    def body(q_ref, k_ref, v_ref, fq_ref, fk_ref, do_ref, o_ref, lse_ref, dq_ref, dfq_ref, dq_s, dfq_s):
        i = pl.program_id(0)
        j = pl.program_id(1)

        @pl.when(j == 0)
        def _():
            dq_s[...] = jnp.zeros_like(dq_s)
            dfq_s[...] = jnp.zeros_like(dfq_s)

        @pl.when(j <= i)
        def _():
            causal, hms = _fox_masks(i, j, tq)
            fq = fq_ref[...]
            fk = fk_ref[...]
            for pr in range(2):
                sl = slice(LANES * pr, LANES * (pr + 1))
                qp = q_ref[:, sl]
                kpb = k_ref[:, sl].astype(BF16)
                vpb = v_ref[:, sl].astype(BF16)
                dop = do_ref[:, sl]
                op = o_ref[:, sl]
                for hh in range(2):
                    h = 2 * pr + hh
                    s, _ = _fox_scores(qp, kpb, fq, fk, h, hms[hh], causal)
                    p = jnp.exp(s - lse_ref[h])
                    dom = jnp.where(hms[hh], dop, 0.0)
                    dp = _dot(dom.astype(BF16), vpb, 1, 1)
                    delta = jnp.sum(dom * op, axis=1, keepdims=True)
                    ds = p * (dp - delta)
                    dq = _dot(ds.astype(BF16), kpb, 1, 0) * (HEAD_DIM ** -0.5)
                    dq_s[:, sl] += jnp.where(hms[hh], dq, 0.0)
                    dfq_s[h] += jnp.sum(ds, axis=1, keepdims=True)

        @pl.when(j == i)
        def _():
            dq_ref[...] = dq_s[...]
            dfq_ref[...] = dfq_s[...]

    kj = lambda i, j: jnp.minimum(j, i)
    return pl.pallas_call(
        body, name=name, grid=(nq, nq),
        in_specs=[pl.BlockSpec((tq, w), lambda i, j: (i, cq)),
                  pl.BlockSpec((tq, w), lambda i, j: (kj(i, j), ck)),
                  pl.BlockSpec((tq, w), lambda i, j: (kj(i, j), cv)),
                  pl.BlockSpec((tq, LANES), lambda i, j: (i, 0)),
                  pl.BlockSpec((SUBLANES, tq), lambda i, j: (0, kj(i, j))),
                  pl.BlockSpec((tq, w), lambda i, j: (i, cdo)),
                  pl.BlockSpec((tq, w), lambda i, j: (i, 0)),
                  pl.BlockSpec((ATT_HEADS, tq, 1), lambda i, j: (0, i, 0))],
        out_specs=[pl.BlockSpec((tq, w), lambda i, j: (i, 0)),
                   pl.BlockSpec((ATT_HEADS, tq, 1), lambda i, j: (0, i, 0))],
        out_shape=[jax.ShapeDtypeStruct((t, w), F32), jax.ShapeDtypeStruct((ATT_HEADS, t, 1), F32)],
        scratch_shapes=[pltpu.VMEM((tq, w), F32), pltpu.VMEM((ATT_HEADS, tq, 1), F32)],
        compiler_params=_params(2),
    )(hbuf, hbuf, hbuf, fc, ft, dymix, o, lse)


GROUP_W = SSD_WIDTH // SSD_GROUPS
HEADS_PER_GROUP = SSD_HEADS // SSD_GROUPS


def _ssd_chunk_common(xr, prev8, sm, cw, cb, dtb, avec):
    c = _conv_taps(xr, prev8, cw, cb)
    sig = _sigmoid(c)
    xa = c * sig
    dt = _softplus(sm + dtb)
    a = dt * avec
    acum = _cumsum_rows(a)
    return c, sig, xa, dt, acum


def _ssd_decays(acum, g):
    n = acum.shape[0]
    atot = acum[n - 1:n, :]
    lane0 = LANE_DT + HEADS_PER_GROUP * g
    e = _head_expand(jnp.exp(acum), lane0, HEADS_PER_GROUP, GROUP_W)
    dec = _head_expand(jnp.exp(atot - acum), lane0, HEADS_PER_GROUP, GROUP_W)
    etot = _head_expand(jnp.exp(atot), lane0, HEADS_PER_GROUP, GROUP_W)
    return e, dec, etot


def _ssd_ldec(acum, acum_t, lane, tril):
    return jnp.exp(jnp.where(tril, _col(acum, lane) - _row(acum_t, lane), NEG))


def _ssd_fwd(hbuf, conv_w, conv_b, dtb_vec, a_vec, d_exp, norm_g, *, name):
    t = hbuf.shape[0]
    L = SSD_CHUNK
    nc = t // L
    hb = L // SUBLANES
    cs = COL_SMALL // LANES
    cz = COL_Z // SSD_WIDTH

    def body(x_ref, xp_ref, z_ref, s_ref, cw_ref, cb_ref, dtb_ref, av_ref, dx_ref, ng_ref,
             yc_ref, y_ref, st_ref, state):
        i = pl.program_id(0)

        @pl.when(i == 0)
        def _():
            state[...] = jnp.zeros_like(state)

        prev = jnp.where(i == 0, 0.0, xp_ref[...])
        _, _, xa, dt, acum = _ssd_chunk_common(x_ref[...], prev, s_ref[...], cw_ref[...], cb_ref[...],
                                               dtb_ref[...], av_ref[...])
        acum_t = acum.T
        xs = xa[:, :SSD_WIDTH]
        xdt = xs * _head_expand(dt, LANE_DT, SSD_HEADS, SSD_WIDTH)
        tril = _iota((L, L), 0) >= _iota((L, L), 1)
        lane = _iota((1, LANES), 1)
        ys = []
        for g in range(SSD_GROUPS):
            bg = xa[:, SSD_WIDTH + SSD_STATE * g:SSD_WIDTH + SSD_STATE * (g + 1)].astype(BF16)
            cg = xa[:, SSD_WIDTH + SSD_STATE * (SSD_GROUPS + g):SSD_WIDTH + SSD_STATE * (SSD_GROUPS + g + 1)].astype(BF16)
            gm = _dot(cg, bg, 1, 1)
            e, dec, etot = _ssd_decays(acum, g)
            s_in = state[g]
            st_ref[0, g] = s_in
            xg = xdt[:, GROUP_W * g:GROUP_W * (g + 1)]
            y_off = e * _dot(cg, s_in.astype(BF16), 1, 0)
            state[g] = etot * s_in + _dot(bg, (dec * xg).astype(BF16), 0, 0)
            for pr in range(2):
                xp = xg[:, LANES * pr:LANES * (pr + 1)].astype(BF16)
                outs = []
                for hh in range(2):
                    h = HEADS_PER_GROUP * g + 2 * pr + hh
                    m = gm * _ssd_ldec(acum, acum_t, LANE_DT + h, tril)
                    outs.append(_dot(m.astype(BF16), xp, 1, 0))
                ys.append(jnp.where(lane < HEAD_DIM, outs[0], outs[1]) + y_off[:, LANES * pr:LANES * (pr + 1)])
        y = jnp.concatenate(ys, axis=1)
        y_ref[...] = y
        yd = y + dx_ref[...] * xs
        zz = z_ref[...]
        y2 = yd * zz * _sigmoid(zz)
        ng = ng_ref[...]
        outs = []
        for g in range(SSD_GROUPS):
            yg = y2[:, GROUP_W * g:GROUP_W * (g + 1)]
            rs = lax.rsqrt(jnp.mean(yg * yg, axis=1, keepdims=True) + RMS_EPS)
            outs.append(yg * rs * ng[:, GROUP_W * g:GROUP_W * (g + 1)])
        yc_ref[...] = jnp.concatenate(outs, axis=1)

    cdim = SSD_CONV_DIM
    vecc = pl.BlockSpec((1, cdim), lambda i: (0, 0))
    vecl = pl.BlockSpec((1, LANES), lambda i: (0, 0))
    vecw = pl.BlockSpec((1, SSD_WIDTH), lambda i: (0, 0))
    roww = pl.BlockSpec((L, SSD_WIDTH), lambda i: (i, 0))
    return pl.pallas_call(
        body, name=name, grid=(nc,),
        in_specs=[pl.BlockSpec((L, cdim), lambda i: (i, 0)),
                  pl.BlockSpec((SUBLANES, cdim), lambda i: (jnp.maximum(i * hb - 1, 0), 0)),
                  pl.BlockSpec((L, SSD_WIDTH), lambda i: (i, cz)),
                  pl.BlockSpec((L, LANES), lambda i: (i, cs)),
                  pl.BlockSpec((CONV_K, cdim), lambda i: (0, 0)), vecc, vecl, vecl, vecw, vecw],
        out_specs=[roww, roww, pl.BlockSpec((1, SSD_GROUPS, SSD_STATE, GROUP_W), lambda i: (i, 0, 0, 0))],
        out_shape=[jax.ShapeDtypeStruct((t, SSD_WIDTH), F32), jax.ShapeDtypeStruct((t, SSD_WIDTH), F32),
                   jax.ShapeDtypeStruct((nc, SSD_GROUPS, SSD_STATE, GROUP_W), F32)],
        scratch_shapes=[pltpu.VMEM((SSD_GROUPS, SSD_STATE, GROUP_W), F32)],
        compiler_params=_params(1),
    )(hbuf, hbuf, hbuf, hbuf, conv_w, conv_b, dtb_vec, a_vec, d_exp, norm_g)


def _ssd_bwd(dymix, hbuf, y_ssd, states, conv_w, conv_b, dtb_vec, a_vec, d_exp, norm_g, *, name):
    t = hbuf.shape[0]
    L = SSD_CHUNK
    nc = t // L
    hb = L // SUBLANES
    cs = COL_SMALL // LANES
    cz = COL_Z // SSD_WIDTH
    cdy = (LRU_WIDTH + ATT_WIDTH) // SSD_WIDTH
    cdim = SSD_CONV_DIM

    def body(dyc_ref, x_ref, xp_ref, z_ref, s_ref, y_ref, st_ref, cw_ref, cb_ref, dtb_ref, av_ref, dx_ref, ng_ref,
             dxr_ref, dz_ref, dsm_ref, dng_ref, dd_ref, da_ref, ddtb_ref, dcw_ref, dcb_ref,
             dstate, dnext):
        i = pl.program_id(0)
        ic = nc - 1 - i

        @pl.when(i == 0)
        def _():
            dstate[...] = jnp.zeros_like(dstate)
            dnext[...] = jnp.zeros_like(dnext)
            for ref in (dng_ref, dd_ref, da_ref, ddtb_ref, dcw_ref, dcb_ref):
                ref[...] = jnp.zeros_like(ref)

        xr = x_ref[...]
        sm = s_ref[...]
        prev = jnp.where(ic == 0, 0.0, xp_ref[...])
        avec = av_ref[...]
        c, sig, xa, dt, acum = _ssd_chunk_common(xr, prev, sm, cw_ref[...], cb_ref[...], dtb_ref[...], avec)
        acum_t = acum.T
        xs = xa[:, :SSD_WIDTH]
        dtx = _head_expand(dt, LANE_DT, SSD_HEADS, SSD_WIDTH)
        xdt = xs * dtx
        tril = _iota((L, L), 0) >= _iota((L, L), 1)
        lane = _iota((1, LANES), 1)
        hmasks = (lane < HEAD_DIM, lane >= HEAD_DIM)

        y = y_ref[...]
        dexp = dx_ref[...]
        yd = y + dexp * xs
        zz = z_ref[...]
        sz = _sigmoid(zz)
        siluz = zz * sz
        y2 = yd * siluz
        ng = ng_ref[...]
        dyc = dyc_ref[...]
        dy2s, dngs = [], []
        for g in range(SSD_GROUPS):
            sl = slice(GROUP_W * g, GROUP_W * (g + 1))
            yg = y2[:, sl]
            rs = lax.rsqrt(jnp.mean(yg * yg, axis=1, keepdims=True) + RMS_EPS)
            wv = dyc[:, sl] * ng[:, sl]
            dngs.append(jnp.sum(dyc[:, sl] * yg * rs, axis=0, keepdims=True))
            dy2s.append(rs * wv - yg * (rs * rs * rs) * jnp.mean(wv * yg, axis=1, keepdims=True))
        dy2 = jnp.concatenate(dy2s, axis=1)
        dng_ref[...] += jnp.concatenate(dngs, axis=1)
        dz_ref[...] = dy2 * yd * (sz * (1.0 + zz * (1.0 - sz)))
        dy = dy2 * siluz
        dd_ref[...] += jnp.sum(dy * xs, axis=0, keepdims=True)

        dxs, dbs, dcs = [], [], []
        datot = jnp.zeros((1, LANES), F32)
        lanes = _iota((L, LANES), 1)
        dacum = jnp.zeros((L, LANES), F32)
        for g in range(SSD_GROUPS):
            sl = slice(GROUP_W * g, GROUP_W * (g + 1))
            bg = xa[:, SSD_WIDTH + SSD_STATE * g:SSD_WIDTH + SSD_STATE * (g + 1)].astype(BF16)
            cg = xa[:, SSD_WIDTH + SSD_STATE * (SSD_GROUPS + g):SSD_WIDTH + SSD_STATE * (SSD_GROUPS + g + 1)].astype(BF16)
            gm = _dot(cg, bg, 1, 1)
            e, dec, etot = _ssd_decays(acum, g)
            s_in = st_ref[0, g]
            ds_out = dstate[g]
            dyg = dy[:, sl]
            xg = xdt[:, sl]
            edy = (e * dyg).astype(BF16)
            dstate[g] = etot * ds_out + _dot(cg, edy, 0, 0)
            dx_state = dec * _dot(bg, ds_out.astype(BF16), 1, 0)
            y_off = e * _dot(cg, s_in.astype(BF16), 1, 0)
            dacum = dacum + _head_reduce_group(dyg * y_off - xg * dx_state, g)
            dc_off = _dot(edy, s_in.astype(BF16), 1, 1)
            db_state = _dot((dec * xg).astype(BF16), ds_out.astype(BF16), 1, 1)
            dgsum = jnp.zeros((L, L), F32)
            dx_pairs = []
            for pr in range(2):
                psl = slice(LANES * pr, LANES * (pr + 1))
                xp = xg[:, psl]
                dyp = dyg[:, psl]
                dx_pair = jnp.zeros((L, LANES), F32)
                for hh in range(2):
                    h = HEADS_PER_GROUP * g + 2 * pr + hh
                    ldec = _ssd_ldec(acum, acum_t, LANE_DT + h, tril)
                    dym = jnp.where(hmasks[hh], dyp, 0.0).astype(BF16)
                    xm = jnp.where(hmasks[hh], xp, 0.0).astype(BF16)
                    dx_pair = dx_pair + _dot((gm * ldec).astype(BF16), dym, 0, 0)
                    dml = _dot(dym, xm, 1, 1) * ldec
                    dgsum = dgsum + dml
                    qm = dml * gm
                    seg = jnp.sum(qm, axis=1, keepdims=True) - jnp.sum(qm.T, axis=1, keepdims=True)
                    dacum = dacum + jnp.where(lanes == LANE_DT + h, seg, 0.0)
                dx_pairs.append(dx_pair)
            dgb = dgsum.astype(BF16)
            dcs.append(_dot(dgb, bg, 1, 0) + dc_off)
            dbs.append(_dot(dgb, cg, 0, 0) + db_state)
            dxg = jnp.concatenate(dx_pairs, axis=1) + dx_state
            dxs.append(dxg)
            v = jnp.sum(dx_state * xg, axis=0, keepdims=True) + etot * jnp.sum(ds_out * s_in, axis=0, keepdims=True)
            datot = datot + _head_reduce_row(v, LANE_DT + HEADS_PER_GROUP * g, HEADS_PER_GROUP)
        dx = jnp.concatenate(dxs, axis=1)
        dacum = dacum + jnp.where(_iota((L, LANES), 0) == L - 1, datot, 0.0)
        da = _cumsum_rows(dacum, reverse=True)
        ddt = da * avec + _head_reduce(dx * xs, LANE_DT, SSD_HEADS)
        da_ref[...] += jnp.sum(da * dt, axis=0, keepdims=True)
        ddt_raw = ddt * _sigmoid(sm + dtb_ref[...])
        ddt_raw = jnp.where((lanes >= LANE_DT) & (lanes < LANE_DT + SSD_HEADS), ddt_raw, 0.0)
        dsm_ref[...] = ddt_raw
        ddtb_ref[...] += jnp.sum(ddt_raw, axis=0, keepdims=True)
        dxs_total = dx * dtx + dexp * dy
        dxa = jnp.concatenate([dxs_total] + dbs + dcs, axis=1)
        dc = dxa * (sig * (1.0 + c * (1.0 - sig)))
        dxr, dws = _conv_taps_bwd(dc, dnext[...], cw_ref[...], xr)
        dxr_ref[...] = dxr
        dcw_ref[...] += dws
        dcb_ref[...] += jnp.sum(dc, axis=0, keepdims=True)
        dnext[...] = dc[:SUBLANES]

    rev = lambda i: nc - 1 - i
    vecc = pl.BlockSpec((1, cdim), lambda i: (0, 0))
    vecl = pl.BlockSpec((1, LANES), lambda i: (0, 0))
    vecw = pl.BlockSpec((1, SSD_WIDTH), lambda i: (0, 0))
    cwspec = pl.BlockSpec((CONV_K, cdim), lambda i: (0, 0))
    roww = pl.BlockSpec((L, SSD_WIDTH), lambda i: (rev(i), 0))
    return pl.pallas_call(
        body, name=name, grid=(nc,),
        in_specs=[pl.BlockSpec((L, SSD_WIDTH), lambda i: (rev(i), cdy)),
                  pl.BlockSpec((L, cdim), lambda i: (rev(i), 0)),
                  pl.BlockSpec((SUBLANES, cdim), lambda i: (jnp.maximum(rev(i) * hb - 1, 0), 0)),
                  pl.BlockSpec((L, SSD_WIDTH), lambda i: (rev(i), cz)),
                  pl.BlockSpec((L, LANES), lambda i: (rev(i), cs)),
                  roww,
                  pl.BlockSpec((1, SSD_GROUPS, SSD_STATE, GROUP_W), lambda i: (rev(i), 0, 0, 0)),
                  cwspec, vecc, vecl, vecl, vecw, vecw],
        out_specs=[pl.BlockSpec((L, cdim), lambda i: (rev(i), 0)), roww,
                   pl.BlockSpec((L, LANES), lambda i: (rev(i), 0)),
                   vecw, vecw, vecl, vecl, cwspec, vecc],
        out_shape=[jax.ShapeDtypeStruct((t, cdim), F32), jax.ShapeDtypeStruct((t, SSD_WIDTH), F32),
                   jax.ShapeDtypeStruct((t, LANES), F32),
                   jax.ShapeDtypeStruct((1, SSD_WIDTH), F32), jax.ShapeDtypeStruct((1, SSD_WIDTH), F32),
                   jax.ShapeDtypeStruct((1, LANES), F32), jax.ShapeDtypeStruct((1, LANES), F32),
                   jax.ShapeDtypeStruct((CONV_K, cdim), F32), jax.ShapeDtypeStruct((1, cdim), F32)],
        scratch_shapes=[pltpu.VMEM((SSD_GROUPS, SSD_STATE, GROUP_W), F32), pltpu.VMEM((SUBLANES, cdim), F32)],
        compiler_params=_params(1),
    )(dymix, hbuf, hbuf, hbuf, hbuf, y_ssd, states, conv_w, conv_b, dtb_vec, a_vec, d_exp, norm_g)


def _head_reduce_group(x, g):
    return _head_reduce(x, LANE_DT + HEADS_PER_GROUP * g, HEADS_PER_GROUP)


def _head_reduce_row(v, lane0, nheads):
    colhead = _iota(v.shape, 1) // HEAD_DIM
    lane = _iota((1, LANES), 1)
    out = jnp.zeros((1, LANES), F32)
    for h in range(nheads):
        s = jnp.sum(jnp.where(colhead == h, v, 0.0), axis=1, keepdims=True)
        out = jnp.where(lane == lane0 + h, s, out)
    return out


def _exchange(inp, axes, *, gather, src_by_core=False, name):
    n = 2 ** len(axes)
    if gather:
        item = inp.shape[1:] if src_by_core else inp.shape
    else:
        item = inp.shape[1:]
    out_shape = (n,) + tuple(item)

    def body(in_ref, out_ref, send_sems, recv_sems, local_sem):
        pos = {ax: lax.axis_index(ax) for ax in MESH_AXES}

        def slot_of(coord):
            s = 0
            for ax in axes:
                s = s * 2 + coord[ax]
            return s

        def src(slot):
            if gather:
                return in_ref.at[pos["c"]] if src_by_core else in_ref
            return in_ref.at[slot]

        me = slot_of(pos)
        local = pltpu.make_async_copy(src(me), out_ref.at[me], local_sem)
        local.start()
        copies = []
        for delta in range(1, n):
            coord = dict(pos)
            for b, ax in enumerate(reversed(axes)):
                if (delta >> b) & 1:
                    coord[ax] = 1 - pos[ax]
            cp = pltpu.make_async_remote_copy(
                src_ref=src(slot_of(coord)), dst_ref=out_ref.at[me],
                send_sem=send_sems.at[delta - 1], recv_sem=recv_sems.at[delta - 1],
                device_id=(coord["x"], coord["y"], coord["c"]), device_id_type=pl.DeviceIdType.MESH)
            cp.start()
            copies.append(cp)
        for cp in copies:
            cp.wait()
        local.wait()

    return pl.pallas_call(
        body, name=name,
        in_specs=[pl.BlockSpec(memory_space=pl.ANY)],
        out_specs=pl.BlockSpec(memory_space=pl.ANY),
        out_shape=jax.ShapeDtypeStruct(out_shape, inp.dtype),
        scratch_shapes=[pltpu.SemaphoreType.DMA((n - 1,)), pltpu.SemaphoreType.DMA((n - 1,)),
                        pltpu.SemaphoreType.DMA(())],
    )(inp)


def _sum_slots(buf, out_dtype, *, name):
    n, rows, cols = buf.shape
    tm = _pick(rows, (512, 256, 128, 8))
    if rows % tm:
        tm = rows

    def body(b_ref, o_ref):
        acc = b_ref[0].astype(F32)
        for s in range(1, n):
            acc = acc + b_ref[s].astype(F32)
        o_ref[...] = acc.astype(out_dtype)

    return pl.pallas_call(
        body, name=name, grid=(pl.cdiv(rows, tm),),
        in_specs=[pl.BlockSpec((n, tm, cols), lambda i: (0, i, 0))],
        out_specs=pl.BlockSpec((tm, cols), lambda i: (i, 0)),
        out_shape=jax.ShapeDtypeStruct((rows, cols), out_dtype),
        compiler_params=_params(1),
    )(buf)


def _adamw(w, g, m, v, *, name):
    shape = w.shape
    cols = shape[-1]
    rows = w.size // cols
    w2, g2, m2, v2 = (a.reshape(rows, cols) for a in (w, g, m, v))
    tm = _pick(rows, (256, 128, 64, 32, 16, 8))
    if rows % tm:
        tm = rows
    bc1 = 1.0 - ADAM_B1 ** ADAM_STEP
    bc2 = 1.0 - ADAM_B2 ** ADAM_STEP

    def body(w_ref, g_ref, m_ref, v_ref, d_ref, nm_ref, nv_ref):
        gg = g_ref[...]
        mm = ADAM_B1 * m_ref[...] + (1.0 - ADAM_B1) * gg
        vv = ADAM_B2 * v_ref[...] + (1.0 - ADAM_B2) * (gg * gg)
        m_hat = mm / bc1
        v_hat = vv / bc2
        d_ref[...] = -ADAM_LR * (m_hat / (jnp.sqrt(v_hat) + ADAM_EPS) + ADAM_WD * w_ref[...])
        nm_ref[...] = mm
        nv_ref[...] = vv

    spec = pl.BlockSpec((tm, cols), lambda i: (i, 0))
    o = jax.ShapeDtypeStruct((rows, cols), F32)
    outs = pl.pallas_call(
        body, name=name, grid=(rows // tm,), in_specs=[spec] * 4, out_specs=[spec] * 3, out_shape=[o] * 3,
        compiler_params=_params(1),
    )(w2, g2, m2, v2)
    return tuple(a.reshape(shape) for a in outs)


def _layer_fwd(li, x, xb, pb, W):
    nm = lambda s: f"l{li}_{s}"
    sv = {"x_in_b": xb}
    g1, u1, a1 = _mm_swiglu(xb, W["ffn1_wg"], W["ffn1_wu"], name=nm("ffn1_up"))
    x1, x1b, xh1, rs1 = _mm_ln(a1, W["ffn1_wd"], x, W["ln1_g"], W["ln1_b"], rscale=ALPHA, mscale=0.5, name=nm("ffn1_down_ln"))
    hbuf = _mm(x1b, W["w_in_p"], name=nm("in_proj"))
    ya, lu, lr, lig, la, lh = _lru_fwd(hbuf, W["lru_conv_w"], W["lru_conv_b"], W["lru_wa_bd"], W["lru_ba"],
                                       W["lru_wx_bd"], W["lru_bx"], W["lru_lambda"], name=nm("lru_fwd"))
    fc = _fox_prep(hbuf, W["fox_bf_vec"], name=nm("fox_prep"))
    ft = jnp.pad(fc[:, :ATT_HEADS].T, ((0, SUBLANES - ATT_HEADS), (0, 0)))
    yb, lse = _fox_fwd(hbuf, fc, ft, name=nm("fox_fwd"))
    yc, yssd, states = _ssd_fwd(hbuf, W["ssd_conv_w"], W["ssd_conv_b"], W["ssd_dtb_vec"], W["ssd_a_vec"],
                                W["ssd_d_exp"], W["ssd_norm_g"], name=nm("ssd_fwd"))
    ymix = jnp.concatenate([ya, yb, yc], axis=1).astype(BF16)
    x2, x2b, xh2, rs2 = _mm_ln(ymix, W["w_out"], x1, W["ln2_g"], W["ln2_b"], rscale=ALPHA, mscale=1.0, name=nm("out_proj_ln"))
    g2, u2, a2 = _mm_swiglu(x2b, W["ffn2_wg"], W["ffn2_wu"], name=nm("ffn2_up"))
    x3, x3b, xh3, rs3 = _mm_ln(a2, W["ffn2_wd"], x2, W["ln3_g"], W["ln3_b"], rscale=ALPHA, mscale=0.5, name=nm("ffn2_down_ln"))
    x4, x4b, sg, e = _mm_pe(x3, x3b, pb, W["pe_gate_w"], W["pe_gate_b"], W["pe_proj"], name=nm("ple"))
    sv.update(g1=g1, u1=u1, a1=a1, x1b=x1b, xh1=xh1, rs1=rs1, hbuf=hbuf, lu=lu, lr=lr, lig=lig, la=la, lh=lh,
              fc=fc, ft=ft, yb=yb, lse=lse, yssd=yssd, states=states, ymix=ymix, x2b=x2b, xh2=xh2, rs2=rs2,
              g2=g2, u2=u2, a2=a2, x3b=x3b, xh3=xh3, rs3=rs3, sg=sg, e=e, pb=pb)
    return x4, x4b, sv


def _layer_bwd(li, dx4, sv, W):
    nm = lambda s: f"l{li}_{s}"
    G = {}
    dgp, de, dbg = _pe_bwd_elem(dx4, sv["sg"], sv["e"], name=nm("ple_bwd"))
    G["pe_gate_b"] = dbg
    G["pe_gate_w"] = _mm(sv["x3b"], dgp, ta=True, name=nm("d_pe_gate_w"))
    G["pe_proj"] = _mm(sv["pb"], de, ta=True, name=nm("d_pe_proj"))
    dr3, G["ln3_g"], G["ln3_b"] = _bwd_proj([(dgp, W["pe_gate_w"])], dx4, rscale=1.0,
                                            ln=(sv["xh3"], sv["rs3"], W["ln3_g"]), name=nm("ln3_bwd"))
    G["ffn2_wd"] = _mm(sv["a2"], dr3, ta=True, scale=0.5, name=nm("d_ffn2_wd"))
    dg2, du2 = _mm_swiglu_bwd(dr3, W["ffn2_wd"], sv["g2"], sv["u2"], scale=0.5, name=nm("ffn2_act_bwd"))
    G["ffn2_wg"] = _mm(sv["x2b"], dg2, ta=True, name=nm("d_ffn2_wg"))
    G["ffn2_wu"] = _mm(sv["x2b"], du2, ta=True, name=nm("d_ffn2_wu"))
    dr2, G["ln2_g"], G["ln2_b"] = _bwd_proj([(dg2, W["ffn2_wg"]), (du2, W["ffn2_wu"])], dr3, rscale=ALPHA,
                                            ln=(sv["xh2"], sv["rs2"], W["ln2_g"]), name=nm("ln2_bwd"))
    G["w_out"] = _mm(sv["ymix"], dr2, ta=True, name=nm("d_w_out"))
    dymix = _mm(dr2, W["w_out"], tb=True, name=nm("d_ymix"))
    hbuf = sv["hbuf"]
    (dur, dgr, G["lru_conv_w"], G["lru_conv_b"], G["lru_wa_bd"], G["lru_ba"], G["lru_wx_bd"], G["lru_bx"],
     G["lru_lambda"]) = _lru_bwd(dymix, hbuf, sv["lu"], sv["lr"], sv["lig"], sv["la"], sv["lh"],
                                 W["lru_conv_w"], W["lru_wa_bd"], W["lru_wx_bd"], W["lru_lambda"], name=nm("lru_bwd"))
    dk, dv, dfk = _fox_bwd_kv(hbuf, sv["fc"], sv["ft"], dymix, sv["yb"], sv["lse"], name=nm("fox_bwd_kv"))
    dq, dfq = _fox_bwd_q(hbuf, sv["fc"], sv["ft"], dymix, sv["yb"], sv["lse"], name=nm("fox_bwd_q"))
    dfc = jnp.pad(dfq[:, :, 0].T - dfk[:ATT_HEADS].T, ((0, 0), (0, LANES - ATT_HEADS)))
    dsm_f, G["fox_bf_vec"] = _fox_post(dfc, hbuf, W["fox_bf_vec"], name=nm("fox_post"))
    (dxr, dz, dsm_dt, G["ssd_norm_g"], G["ssd_d_exp"], G["ssd_a_vec"], G["ssd_dtb_vec"], G["ssd_conv_w"],
     G["ssd_conv_b"]) = _ssd_bwd(dymix, hbuf, sv["yssd"], sv["states"], W["ssd_conv_w"], W["ssd_conv_b"],
                                 W["ssd_dtb_vec"], W["ssd_a_vec"], W["ssd_d_exp"], W["ssd_norm_g"], name=nm("ssd_bwd"))
    t = dx4.shape[0]
    dh = jnp.concatenate([dxr.astype(BF16), dz.astype(BF16), dur.astype(BF16), dgr.astype(BF16), dq.astype(BF16),
                          dk.astype(BF16), dv.astype(BF16), (dsm_f + dsm_dt).astype(BF16),
                          jnp.zeros((t, H_WIDTH - COL_SMALL - LANES), BF16)], axis=1)
    G["w_in_p"] = _mm(sv["x1b"], dh, ta=True, name=nm("d_w_in"))
    dr1, G["ln1_g"], G["ln1_b"] = _bwd_proj([(dh, W["w_in_p"])], dr2, rscale=ALPHA,
                                            ln=(sv["xh1"], sv["rs1"], W["ln1_g"]), name=nm("ln1_bwd"))
    G["ffn1_wd"] = _mm(sv["a1"], dr1, ta=True, scale=0.5, name=nm("d_ffn1_wd"))
    dg1, du1 = _mm_swiglu_bwd(dr1, W["ffn1_wd"], sv["g1"], sv["u1"], scale=0.5, name=nm("ffn1_act_bwd"))
    G["ffn1_wg"] = _mm(sv["x_in_b"], dg1, ta=True, name=nm("d_ffn1_wg"))
    G["ffn1_wu"] = _mm(sv["x_in_b"], du1, ta=True, name=nm("d_ffn1_wu"))
    (dx_in,) = _bwd_proj([(dg1, W["ffn1_wg"]), (du1, W["ffn1_wu"])], dr1, rscale=ALPHA, ln=None, name=nm("x_in_bwd"))
    return dx_in, G


def _block_diag(w):
    n, b, _ = w.shape
    eye = jnp.eye(n, dtype=w.dtype)
    return (eye[:, None, :, None] * w[:, :, None, :]).reshape(n * b, n * b)


def _block_diag_extract(m):
    n, b = LRU_HEADS, HEAD_DIM
    return jnp.stack([m[b * i:b * (i + 1), b * i:b * (i + 1)] for i in range(n)])


def _lane_vec(v, lane0):
    return jnp.zeros((1, LANES), F32).at[0, lane0:lane0 + v.shape[0]].set(v)


def _w_in_permute(w):
    d = w.shape[0]
    z = lambda n: jnp.zeros((d, n), w.dtype)
    return jnp.concatenate([w[:, 1796:2820], w[:, 1284:1796], w[:, 0:512], w[:, 512:1280],
                            w[:, 1280:1284], w[:, 2820:2828], z(LANES - 12), z(H_WIDTH - COL_SMALL - LANES)], axis=1)


def _w_in_unpermute(wp):
    return jnp.concatenate([wp[:, COL_U:COL_Q], wp[:, COL_Q:COL_SMALL], wp[:, COL_SMALL:COL_SMALL + 4],
                            wp[:, COL_Z:COL_U], wp[:, COL_XBC:COL_Z], wp[:, COL_SMALL + 4:COL_SMALL + 12]], axis=1)


def _layer_weights(li, full):
    g = lambda n: full[n][li]
    W = {n: g(n) for n in ("ffn1_wg", "ffn1_wu", "ffn1_wd", "w_out", "ffn2_wg", "ffn2_wu", "ffn2_wd",
                           "pe_proj", "pe_gate_w", "ln1_g", "ln1_b", "ln2_g", "ln2_b", "ln3_g", "ln3_b",
                           "pe_gate_b", "lru_conv_w", "ssd_conv_w")}
    W["w_in_p"] = _w_in_permute(g("w_in"))
    for n in ("lru_conv_b", "lru_ba", "lru_bx", "lru_lambda", "ssd_conv_b", "ssd_norm_g"):
        W[n] = g(n).reshape(1, -1)
    W["lru_wa_bd"] = _block_diag(g("lru_wa")).astype(BF16)
    W["lru_wx_bd"] = _block_diag(g("lru_wx")).astype(BF16)
    W["fox_bf_vec"] = _lane_vec(g("fox_bf"), LANE_F)
    W["ssd_dtb_vec"] = _lane_vec(g("ssd_dt_bias"), LANE_DT)
    W["ssd_a_vec"] = _lane_vec(-jnp.exp(g("ssd_a_log")), LANE_DT)
    W["ssd_d_exp"] = jnp.repeat(g("ssd_d"), HEAD_DIM).reshape(1, SSD_WIDTH)
    return W


def _layer_grads_to_reference(G, W):
    out = {n: G[n] for n in ("ffn1_wg", "ffn1_wu", "ffn1_wd", "w_out", "ffn2_wg", "ffn2_wu", "ffn2_wd",
                             "pe_proj", "pe_gate_w", "lru_conv_w", "ssd_conv_w")}
    for n in ("ln1_g", "ln1_b", "ln2_g", "ln2_b", "ln3_g", "ln3_b", "pe_gate_b", "lru_conv_b", "lru_ba", "lru_bx",
              "lru_lambda", "ssd_conv_b", "ssd_norm_g"):
        out[n] = G[n].reshape(-1)
    out["w_in"] = _w_in_unpermute(G["w_in_p"])
    out["lru_wa"] = _block_diag_extract(G["lru_wa_bd"])
    out["lru_wx"] = _block_diag_extract(G["lru_wx_bd"])
    out["fox_bf"] = G["fox_bf_vec"][0, LANE_F:LANE_F + ATT_HEADS]
    out["ssd_dt_bias"] = G["ssd_dtb_vec"][0, LANE_DT:LANE_DT + SSD_HEADS]
    out["ssd_a_log"] = G["ssd_a_vec"][0, LANE_DT:LANE_DT + SSD_HEADS] * W["ssd_a_vec"][0, LANE_DT:LANE_DT + SSD_HEADS]
    out["ssd_d"] = G["ssd_d_exp"].reshape(SSD_HEADS, HEAD_DIM).sum(axis=1)
    return out


def _local_step(x, p, target, full):
    Ws = [_layer_weights(li, full) for li in range(DEPTH)]
    saves = []
    xb = x.astype(BF16)
    for li in range(DEPTH):
        x, xb, sv = _layer_fwd(li, x, xb, p[li].astype(BF16), Ws[li])
        saves.append(sv)
    dx, loss = _loss_kernel(x, target, name="loss")
    grads = [None] * DEPTH
    for li in reversed(range(DEPTH)):
        dx, G = _layer_bwd(li, dx, saves[li], Ws[li])
        grads[li] = _layer_grads_to_reference(G, Ws[li])
    stacked = {n: jnp.stack([grads[li][n] for li in range(DEPTH)]) for n in grads[0]}
    return loss, dx, stacked


WEIGHTS = ['ln1_g', 'ln1_b', 'ffn1_wg', 'ffn1_wu', 'ffn1_wd', 'w_in', 'lru_conv_w', 'lru_conv_b', 'lru_wa', 'lru_ba',
           'lru_wx', 'lru_bx', 'lru_lambda', 'fox_bf', 'ssd_conv_w', 'ssd_conv_b', 'ssd_dt_bias', 'ssd_a_log', 'ssd_d',
           'ssd_norm_g', 'w_out', 'ln2_g', 'ln2_b', 'ffn2_wg', 'ffn2_wu', 'ffn2_wd', 'ln3_g', 'ln3_b', 'pe_proj',
           'pe_gate_w', 'pe_gate_b']
BIG = {'ffn1_wg': 2, 'ffn1_wu': 2, 'ffn1_wd': 1, 'w_in': 2, 'w_out': 1, 'ffn2_wg': 2, 'ffn2_wu': 2, 'ffn2_wd': 1,
       'pe_proj': 2, 'pe_gate_w': 1}
SMALL_SHARDED = {'lru_conv_w': 2, 'ssd_conv_w': 2}
PACK_COLS = 1024


def _unshard(seg, axis):
    moved = jnp.moveaxis(seg, 0, axis)
    shp = list(moved.shape)
    shp[axis:axis + 2] = [shp[axis] * shp[axis + 1]]
    return moved.reshape(shp)


def _shard4(full, axis):
    shp = list(full.shape)
    shp[axis:axis + 1] = [N_CHIPS, shp[axis] // N_CHIPS]
    return jnp.moveaxis(full.reshape(shp), axis, 0)


PACK_ROW_MULTIPLE = 1024


def _pack(arrs, dtype, cols, row_multiple=1):
    flat = jnp.concatenate([a.astype(dtype).reshape(-1) for a in arrs])
    pad = (-flat.shape[0]) % (cols * row_multiple)
    if pad:
        flat = jnp.concatenate([flat, jnp.zeros((pad,), dtype)])
    return flat.reshape(-1, cols)


def _unpack(flat, shapes):
    out, off = [], 0
    for s in shapes:
        n = math.prod(s)
        out.append(flat[off:off + n].reshape(s))
        off += n
    return out


def kernel(x, p, ln1_g, ln1_b, ffn1_wg, ffn1_wu, ffn1_wd, w_in, lru_conv_w, lru_conv_b, lru_wa, lru_ba, lru_wx, lru_bx, lru_lambda, fox_bf, ssd_conv_w, ssd_conv_b, ssd_dt_bias, ssd_a_log, ssd_d, ssd_norm_g, w_out, ln2_g, ln2_b, ffn2_wg, ffn2_wu, ffn2_wd, ln3_g, ln3_b, pe_proj, pe_gate_w, pe_gate_b, loss_target, m_ln1_g, m_ln1_b, m_ffn1_wg, m_ffn1_wu, m_ffn1_wd, m_w_in, m_lru_conv_w, m_lru_conv_b, m_lru_wa, m_lru_ba, m_lru_wx, m_lru_bx, m_lru_lambda, m_fox_bf, m_ssd_conv_w, m_ssd_conv_b, m_ssd_dt_bias, m_ssd_a_log, m_ssd_d, m_ssd_norm_g, m_w_out, m_ln2_g, m_ln2_b, m_ffn2_wg, m_ffn2_wu, m_ffn2_wd, m_ln3_g, m_ln3_b, m_pe_proj, m_pe_gate_w, m_pe_gate_b, v_ln1_g, v_ln1_b, v_ffn1_wg, v_ffn1_wu, v_ffn1_wd, v_w_in, v_lru_conv_w, v_lru_conv_b, v_lru_wa, v_lru_ba, v_lru_wx, v_lru_bx, v_lru_lambda, v_fox_bf, v_ssd_conv_w, v_ssd_conv_b, v_ssd_dt_bias, v_ssd_a_log, v_ssd_d, v_ssd_norm_g, v_w_out, v_ln2_g, v_ln2_b, v_ffn2_wg, v_ffn2_wu, v_ffn2_wd, v_ln3_g, v_ln3_b, v_pe_proj, v_pe_gate_w, v_pe_gate_b):
    args = locals()
    w_loc = {n: args[n] for n in WEIGHTS}
    m_loc = {n: args["m_" + n] for n in WEIGHTS}
    v_loc = {n: args["v_" + n] for n in WEIGHTS}
    chip = 2 * lax.axis_index("x") + lax.axis_index("y")
    big = list(BIG)
    small_sh = list(SMALL_SHARDED)
    small_rep = [n for n in WEIGHTS if n not in BIG and n not in SMALL_SHARDED]

    wpack = _pack([w_loc[n] for n in big], BF16, PACK_COLS, PACK_ROW_MULTIPLE)
    half = wpack.shape[0] // 2
    halves = _exchange(wpack.reshape(2, half, PACK_COLS), ("x", "y"), gather=True, src_by_core=True,
                       name="gather_w_chips")
    both = _exchange(halves, ("c",), gather=True, name="gather_w_cores")
    gathered = jnp.swapaxes(both, 0, 1).reshape(N_CHIPS, -1)
    full = {}
    for n, seg in zip(big, _unpack_rows(gathered, [w_loc[n].shape for n in big])):
        full[n] = _unshard(seg, BIG[n])
    spack = _pack([w_loc[n] for n in small_sh], F32, LANES)
    sg = _exchange(spack, ("x", "y"), gather=True, name="gather_conv_w").reshape(N_CHIPS, -1)
    for n, seg in zip(small_sh, _unpack_rows(sg, [w_loc[n].shape for n in small_sh])):
        full[n] = _unshard(seg, SMALL_SHARDED[n])
    for n in small_rep:
        full[n] = w_loc[n]

    loss, grad_x, g_full = _local_step(x[0], p[:, 0], loss_target[0], full)
    loss = lax.psum(loss[0, 0], MESH_AXES)

    gp = jnp.concatenate([_shard4(g_full[n], BIG[n]).reshape(N_CHIPS, -1).astype(BF16) for n in big], axis=1)
    gp = jnp.pad(gp, ((0, 0), (0, 2 * half * PACK_COLS - gp.shape[1])))
    gp = jnp.swapaxes(gp.reshape(N_CHIPS, 2, half, PACK_COLS), 0, 1)
    pair = _exchange(gp, ("c",), gather=False, name="reduce_cores")
    s2 = _sum_slots(pair.reshape(2, N_CHIPS * half, PACK_COLS), BF16, name="reduce_cores_sum")
    quad = _exchange(s2.reshape(N_CHIPS, half, PACK_COLS), ("x", "y"), gather=False, name="reduce_chips")
    red = _sum_slots(quad, F32, name="reduce_chips_sum")
    red = _exchange(red, ("c",), gather=True, name="reduce_share").reshape(-1)
    g_red = dict(zip(big, _unpack(red, [w_loc[n].shape for n in big])))
    small_all = small_rep + small_sh
    sgp = _pack([g_full[n] for n in small_all], F32, PACK_COLS)
    sall = _exchange(sgp, MESH_AXES, gather=True, name="reduce_small")
    sred = _sum_slots(sall, F32, name="reduce_small_sum").reshape(-1)
    for n, g in zip(small_all, _unpack(sred, [g_full[n].shape for n in small_all])):
        if n in SMALL_SHARDED:
            width = w_loc[n].shape[-1]
            g = lax.dynamic_slice_in_dim(g, chip * width, width, axis=SMALL_SHARDED[n])
        g_red[n] = g

    delta, new_m, new_v = {}, {}, {}
    for n in big:
        delta[n], new_m[n], new_v[n] = _adamw(w_loc[n], g_red[n], m_loc[n], v_loc[n], name="adamw_" + n)
    shapes = [w_loc[n].shape for n in small_all]
    packs = [_pack([d[n] for n in small_all], F32, LANES) for d in (w_loc, g_red, m_loc, v_loc)]
    outs = _adamw(*packs, name="adamw_small")
    for d, o in zip((delta, new_m, new_v), outs):
        for n, a in zip(small_all, _unpack(o.reshape(-1), shapes)):
            d[n] = a
    return (loss, grad_x[None], *[g_red[n] for n in WEIGHTS], *[delta[n] for n in WEIGHTS],
            *[new_m[n] for n in WEIGHTS], *[new_v[n] for n in WEIGHTS])


def _unpack_rows(gathered, shapes):
    out, off = [], 0
    for s in shapes:
        n = math.prod(s)
        out.append(gathered[:, off:off + n].reshape((N_CHIPS,) + tuple(s)))
        off += n
    return out
```

```python
import functools
import math

import jax
import jax.numpy as jnp
from jax import lax
from jax.experimental import pallas as pl
from jax.experimental.pallas import tpu as pltpu

F32 = jnp.float32
BF16 = jnp.bfloat16

D_MODEL = 1024
DEPTH = 2
PLE_DIM = 256
HEAD_DIM = 64
LRU_WIDTH = 256
LRU_HEADS = 4
LRU_C = 8.0
CONV_K = 4
ATT_WIDTH = 256
ATT_HEADS = 4
SSD_WIDTH = 512
SSD_HEADS = 8
SSD_GROUPS = 2
SSD_STATE = 128
SSD_CHUNK = 128
SSD_CONV_DIM = 1024
FFN_DIM = 2816
ALPHA = (2.0 * DEPTH) ** 0.25
LN_EPS = 1e-5
RMS_EPS = 1e-5
IN_WIDTH = 2828
ADAM_LR = 0.001
ADAM_B1 = 0.9
ADAM_B2 = 0.999
ADAM_EPS = 1e-08
ADAM_WD = 0.01
ADAM_STEP = 10

H_WIDTH = 3072
COL_XBC, COL_Z, COL_U, COL_G, COL_Q, COL_K, COL_V, COL_SMALL = 0, 1024, 1536, 1792, 2048, 2304, 2560, 2816
LANE_F = 0
LANE_DT = 4
LANES = 128
SUBLANES = 8
NEG = -1e30

VMEM_LIMIT = 48 * 1024 * 1024

N_CHIPS = 4
MESH_AXES = ("x", "y", "c")
SHARE = 768


def _params(n):
    return pltpu.CompilerParams(dimension_semantics=("arbitrary",) * n, vmem_limit_bytes=VMEM_LIMIT)


def _pick(n, cands):
    for c in cands:
        if n % c == 0:
            return c
    return n


def _iota(shape, dim):
    return lax.broadcasted_iota(jnp.int32, shape, dim)


def _shift_down(x, s, prev8):
    if s == 0:
        return x
    r = pltpu.roll(x, s, 0)
    pr = pltpu.roll(prev8, s, 0)
    head = jnp.where(_iota(pr.shape, 0) < s, pr, r[:SUBLANES])
    return jnp.concatenate([head, r[SUBLANES:]], axis=0)


def _shift_up(x, s, next8):
    if s == 0:
        return x
    n = x.shape[0]
    r = pltpu.roll(x, n - s, 0)
    nr = pltpu.roll(next8, SUBLANES - s, 0)
    tail = jnp.where(_iota(nr.shape, 0) >= SUBLANES - s, nr, r[n - SUBLANES:])
    return jnp.concatenate([r[:n - SUBLANES], tail], axis=0)


def _scan_fwd(a, b):
    n = a.shape[0]
    row = _iota(a.shape, 0)
    d = 1
    while d < n:
        keep = row >= d
        a_s = jnp.where(keep, pltpu.roll(a, d, 0), 1.0)
        b_s = jnp.where(keep, pltpu.roll(b, d, 0), 0.0)
        b = a * b_s + b
        a = a * a_s
        d *= 2
    return a, b


def _scan_bwd(a, b):
    n = a.shape[0]
    row = _iota(a.shape, 0)
    d = 1
    while d < n:
        keep = row < n - d
        a_s = jnp.where(keep, pltpu.roll(a, n - d, 0), 1.0)
        b_s = jnp.where(keep, pltpu.roll(b, n - d, 0), 0.0)
        b = a * b_s + b
        a = a * a_s
        d *= 2
    return a, b


def _cumsum_rows(x, reverse=False):
    n = x.shape[0]
    row = _iota(x.shape, 0)
    d = 1
    while d < n:
        if reverse:
            x = x + jnp.where(row < n - d, pltpu.roll(x, n - d, 0), 0.0)
        else:
            x = x + jnp.where(row >= d, pltpu.roll(x, d, 0), 0.0)
        d *= 2
    return x


def _col(x, lane):
    return jnp.sum(jnp.where(_iota(x.shape, 1) == lane, x, 0.0), axis=1, keepdims=True)


def _row(x, r):
    return jnp.sum(jnp.where(_iota(x.shape, 0) == r, x, 0.0), axis=0, keepdims=True)


def _sigmoid(x):
    return jax.nn.sigmoid(x)


def _softplus(x):
    return jnp.maximum(x, 0.0) + jnp.log(1.0 + jnp.exp(-jnp.abs(x)))


def _gelu_and_grad(x):
    c0 = math.sqrt(2.0 / math.pi)
    inner = c0 * (x + 0.044715 * x * x * x)
    t = jnp.tanh(inner)
    g = 0.5 * x * (1.0 + t)
    dg = 0.5 * (1.0 + t) + 0.5 * x * (1.0 - t * t) * c0 * (1.0 + 3.0 * 0.044715 * x * x)
    return g, dg


def _dot(a, b, ca, cb):
    return lax.dot_general(a, b, (((ca,), (cb,)), ((), ())), preferred_element_type=F32)


def _conv_taps(xr, prev8, w, bias):
    y = bias + w[CONV_K - 1:CONV_K, :] * xr
    for j in range(CONV_K - 1):
        y = y + w[j:j + 1, :] * _shift_down(xr, CONV_K - 1 - j, prev8)
    return y


def _conv_taps_bwd(dy, next8, w, xr):
    dx = None
    dws = []
    for j in range(CONV_K):
        sh = _shift_up(dy, CONV_K - 1 - j, next8)
        term = w[j:j + 1, :] * sh
        dx = term if dx is None else dx + term
        dws.append(jnp.sum(sh * xr, axis=0, keepdims=True))
    return dx, jnp.concatenate(dws, axis=0)


def _head_expand(v, lane0, nheads, width):
    rows = v.shape[0]
    colhead = _iota((rows, width), 1) // HEAD_DIM
    out = jnp.zeros((rows, width), F32)
    for h in range(nheads):
        out = jnp.where(colhead == h, _col(v, lane0 + h), out)
    return out


def _head_reduce(x, lane0, nheads):
    rows = x.shape[0]
    colhead = _iota(x.shape, 1) // HEAD_DIM
    lane = _iota((rows, LANES), 1)
    out = jnp.zeros((rows, LANES), F32)
    for h in range(nheads):
        s = jnp.sum(jnp.where(colhead == h, x, 0.0), axis=1, keepdims=True)
        out = jnp.where(lane == lane0 + h, s, out)
    return out


def _mm(a, b, *, ta=False, tb=False, scale=1.0, out_dtype=F32, chip_cols=False, name):
    if ta:
        kk, m = a.shape
    else:
        m, kk = a.shape
    n = b.shape[0] if tb else b.shape[1]
    tm = _pick(m, (512, 256, 128))
    tn = _pick(n // N_CHIPS, (768, 256, 128)) if chip_cols else _pick(n, (1024, 768, 512, 256, 128))
    tk = _pick(kk, (1024, 768, 512, 256, 128))
    nk = kk // tk
    dn_a = 0 if ta else 1
    dn_b = 1 if tb else 0
    if chip_cols:
        per = n // N_CHIPS // tn
        out_spec = pl.BlockSpec((None, tm, tn), lambda i, j, k: (j // per, i, j % per))
        out_shape = jax.ShapeDtypeStruct((N_CHIPS, m, n // N_CHIPS), out_dtype)
    else:
        out_spec = pl.BlockSpec((tm, tn), lambda i, j, k: (i, j))
        out_shape = jax.ShapeDtypeStruct((m, n), out_dtype)

    def body(a_ref, b_ref, o_ref, acc):
        k = pl.program_id(2)

        @pl.when(k == 0)
        def _():
            acc[...] = jnp.zeros_like(acc)

        acc[...] += _dot(a_ref[...].astype(BF16), b_ref[...].astype(BF16), dn_a, dn_b)

        @pl.when(k == nk - 1)
        def _():
            o_ref[...] = (acc[...] * scale).astype(out_dtype)

    a_spec = pl.BlockSpec((tk, tm), lambda i, j, k: (k, i)) if ta else pl.BlockSpec((tm, tk), lambda i, j, k: (i, k))
    b_spec = pl.BlockSpec((tn, tk), lambda i, j, k: (j, k)) if tb else pl.BlockSpec((tk, tn), lambda i, j, k: (k, j))
    return pl.pallas_call(
        body, name=name, grid=(m // tm, n // tn, nk),
        in_specs=[a_spec, b_spec],
        out_specs=out_spec, out_shape=out_shape,
        scratch_shapes=[pltpu.VMEM((tm, tn), F32)],
        compiler_params=_params(3),
    )(a, b)


def _mm_swiglu(xb, wg, wu, *, name):
    t, d = xb.shape
    share = wg.shape[2]
    n = N_CHIPS * share
    tm = _pick(t, (512, 256, 128))
    tn = _pick(share, (256, 128))
    per = share // tn

    def body(x_ref, wg_ref, wu_ref, g_ref, u_ref, a_ref):
        x = x_ref[...]
        g = _dot(x, wg_ref[...], 1, 0)
        u = _dot(x, wu_ref[...], 1, 0)
        g_ref[...] = g.astype(BF16)
        u_ref[...] = u.astype(BF16)
        a_ref[...] = (g * _sigmoid(g) * u).astype(BF16)

    o = jax.ShapeDtypeStruct((t, n), BF16)
    ospec = pl.BlockSpec((tm, tn), lambda i, j: (i, j))
    return pl.pallas_call(
        body, name=name, grid=(t // tm, n // tn),
        in_specs=[pl.BlockSpec((tm, d), lambda i, j: (i, 0)),
                  pl.BlockSpec((None, d, tn), lambda i, j: (j // per, 0, j % per)),
                  pl.BlockSpec((None, d, tn), lambda i, j: (j // per, 0, j % per))],
        out_specs=[ospec, ospec, ospec], out_shape=[o, o, o],
        compiler_params=_params(2),
    )(xb, wg, wu)


def _mm_swiglu_bwd(dr, wd, g, u, *, scale, name):
    t, d = dr.shape
    n = wd.shape[0]
    tm = _pick(t, (512, 256, 128))
    tn = _pick(n, (256, 128))

    def body(dr_ref, wd_ref, g_ref, u_ref, dg_ref, du_ref):
        da = _dot(dr_ref[...].astype(BF16), wd_ref[...], 1, 1) * scale
        gg = g_ref[...].astype(F32)
        uu = u_ref[...].astype(F32)
        sg = _sigmoid(gg)
        dg_ref[...] = (da * uu * (sg * (1.0 + gg * (1.0 - sg)))).astype(BF16)
        du_ref[...] = (da * gg * sg).astype(BF16)

    o = jax.ShapeDtypeStruct((t, n), BF16)
    ospec = pl.BlockSpec((tm, tn), lambda i, j: (i, j))
    return pl.pallas_call(
        body, name=name, grid=(t // tm, n // tn),
        in_specs=[pl.BlockSpec((tm, d), lambda i, j: (i, 0)),
                  pl.BlockSpec((tn, d), lambda i, j: (j, 0)),
                  ospec, ospec],
        out_specs=[ospec, ospec], out_shape=[o, o],
        compiler_params=_params(2),
    )(dr, wd, g, u)


def _mm_ln(a, w, resid, gain, bias, *, rscale, mscale, name):
    t, kk = a.shape
    d = w.shape[1]
    tm = _pick(t, (256, 128))
    tk = _pick(kk, (1024, 1408, 512, 256, 128))
    nk = kk // tk

    def body(a_ref, w_ref, r_ref, g_ref, b_ref, y_ref, yb_ref, xh_ref, rs_ref, acc):
        k = pl.program_id(1)

        @pl.when(k == 0)
        def _():
            acc[...] = jnp.zeros_like(acc)

        acc[...] += _dot(a_ref[...].astype(BF16), w_ref[...], 1, 0)

        @pl.when(k == nk - 1)
        def _():
            r = rscale * r_ref[...] + mscale * acc[...]
            mu = jnp.mean(r, axis=1, keepdims=True)
            xc = r - mu
            var = jnp.mean(xc * xc, axis=1, keepdims=True)
            rstd = lax.rsqrt(var + LN_EPS)
            xh = xc * rstd
            y = xh * g_ref[...] + b_ref[...]
            y_ref[...] = y
            yb_ref[...] = y.astype(BF16)
            xh_ref[...] = xh
            rs_ref[...] = rstd

    row = pl.BlockSpec((tm, d), lambda i, k: (i, 0))
    vec = pl.BlockSpec((1, d), lambda i, k: (0, 0))
    return pl.pallas_call(
        body, name=name, grid=(t // tm, nk),
        in_specs=[pl.BlockSpec((tm, tk), lambda i, k: (i, k)),
                  pl.BlockSpec((tk, d), lambda i, k: (k, 0)), row, vec, vec],
        out_specs=[row, row, row, pl.BlockSpec((tm, 1), lambda i, k: (i, 0))],
        out_shape=[jax.ShapeDtypeStruct((t, d), F32), jax.ShapeDtypeStruct((t, d), BF16),
                   jax.ShapeDtypeStruct((t, d), F32), jax.ShapeDtypeStruct((t, 1), F32)],
        scratch_shapes=[pltpu.VMEM((tm, d), F32)],
        compiler_params=_params(2),
    )(a, w, resid, gain.reshape(1, d), bias.reshape(1, d))


def _bwd_proj(pairs, resid, *, rscale, ln, name):
    t, kk = pairs[0][0].shape
    d = pairs[0][1].shape[-2]
    tm = _pick(t, (256, 128))
    tk = _pick(pairs[0][1].shape[-1], (1024, 768, 512, 256, 128))
    nk = kk // tk
    nt = t // tm
    npair = len(pairs)
    has_ln = ln is not None

    def body(*refs):
        ab = refs[:2 * npair]
        r_ref = refs[2 * npair]
        pos = 2 * npair + 1
        if has_ln:
            xh_ref, rs_ref, g_ref = refs[pos:pos + 3]
            pos += 3
            o_ref, dg_ref, db_ref = refs[pos:pos + 3]
            pos += 3
        else:
            o_ref = refs[pos]
            pos += 1
        acc = refs[pos]
        i = pl.program_id(0)
        k = pl.program_id(1)

        @pl.when(k == 0)
        def _():
            acc[...] = jnp.zeros_like(acc)

        for q in range(npair):
            acc[...] += _dot(ab[2 * q][...].astype(BF16), ab[2 * q + 1][...], 1, 1)

        @pl.when(k == nk - 1)
        def _():
            dy = rscale * r_ref[...] + acc[...]
            if not has_ln:
                o_ref[...] = dy
                return
            xh = xh_ref[...]
            w = dy * g_ref[...]
            m1 = jnp.mean(w, axis=1, keepdims=True)
            m2 = jnp.mean(w * xh, axis=1, keepdims=True)
            o_ref[...] = rs_ref[...] * (w - m1 - xh * m2)

            @pl.when(i == 0)
            def _():
                dg_ref[...] = jnp.zeros_like(dg_ref)
                db_ref[...] = jnp.zeros_like(db_ref)

            dg_ref[...] += jnp.sum(dy * xh, axis=0, keepdims=True)
            db_ref[...] += jnp.sum(dy, axis=0, keepdims=True)

    row = pl.BlockSpec((tm, d), lambda i, k: (i, 0))
    vec = pl.BlockSpec((1, d), lambda i, k: (0, 0))
    in_specs, args = [], []
    for a, b in pairs:
        if b.ndim == 3:
            per = b.shape[2] // tk
            b_spec = pl.BlockSpec((None, d, tk), lambda i, k, per=per: (k // per, 0, k % per))
        else:
            b_spec = pl.BlockSpec((d, tk), lambda i, k: (0, k))
        in_specs += [pl.BlockSpec((tm, tk), lambda i, k: (i, k)), b_spec]
        args += [a, b]
    in_specs.append(row)
    args.append(resid)
    out_specs = [row]
    out_shape = [jax.ShapeDtypeStruct((t, d), F32)]
    if has_ln:
        xh, rs, gain = ln
        in_specs += [row, pl.BlockSpec((tm, 1), lambda i, k: (i, 0)), vec]
        args += [xh, rs, gain.reshape(1, d)]
        out_specs += [vec, vec]
        out_shape += [jax.ShapeDtypeStruct((1, d), F32)] * 2
    return pl.pallas_call(
        body, name=name, grid=(nt, nk), in_specs=in_specs, out_specs=out_specs, out_shape=out_shape,
        scratch_shapes=[pltpu.VMEM((tm, d), F32)],
        compiler_params=_params(2),
    )(*args)


def _mm_pe(x3, x3b, pb, wgate, bgate, wproj, *, name):
    t, d = x3.shape
    pd = pb.shape[1]
    tm = _pick(t, (512, 256, 128))
    tn = _pick(d, (512, 256, 128))

    def body(x_ref, xb_ref, p_ref, wg_ref, bg_ref, wp_ref, y_ref, yb_ref, sg_ref, e_ref):
        sg = _sigmoid(_dot(xb_ref[...], wg_ref[...], 1, 0) + bg_ref[...])
        e = _dot(p_ref[...], wp_ref[...], 1, 0)
        y = x_ref[...] + sg * e
        y_ref[...] = y
        yb_ref[...] = y.astype(BF16)
        sg_ref[...] = sg.astype(BF16)
        e_ref[...] = e.astype(BF16)

    ospec = pl.BlockSpec((tm, tn), lambda i, j: (i, j))
    ob = jax.ShapeDtypeStruct((t, d), BF16)
    return pl.pallas_call(
        body, name=name, grid=(t // tm, d // tn),
        in_specs=[ospec, pl.BlockSpec((tm, d), lambda i, j: (i, 0)), pl.BlockSpec((tm, pd), lambda i, j: (i, 0)),
                  pl.BlockSpec((d, tn), lambda i, j: (0, j)), pl.BlockSpec((1, tn), lambda i, j: (0, j)),
                  pl.BlockSpec((pd, tn), lambda i, j: (0, j))],
        out_specs=[ospec, ospec, ospec, ospec],
        out_shape=[jax.ShapeDtypeStruct((t, d), F32), ob, ob, ob],
        compiler_params=_params(2),
    )(x3, x3b, pb, wgate, bgate.reshape(1, d), wproj)


def _pe_bwd_elem(dx4, sg, e, *, name):
    t, d = dx4.shape
    tm = _pick(t, (512, 256, 128))

    def body(dx_ref, sg_ref, e_ref, dgp_ref, de_ref, db_ref):
        dx = dx_ref[...]
        s = sg_ref[...].astype(F32)
        dgp = dx * e_ref[...].astype(F32) * s * (1.0 - s)
        dgp_ref[...] = dgp.astype(BF16)
        de_ref[...] = (dx * s).astype(BF16)

        @pl.when(pl.program_id(0) == 0)
        def _():
            db_ref[...] = jnp.zeros_like(db_ref)

        db_ref[...] += jnp.sum(dgp, axis=0, keepdims=True)

    row = pl.BlockSpec((tm, d), lambda i: (i, 0))
    ob = jax.ShapeDtypeStruct((t, d), BF16)
    return pl.pallas_call(
        body, name=name, grid=(t // tm,), in_specs=[row, row, row],
        out_specs=[row, row, pl.BlockSpec((1, d), lambda i: (0, 0))],
        out_shape=[ob, ob, jax.ShapeDtypeStruct((1, d), F32)],
        compiler_params=_params(1),
    )(dx4, sg, e)


def _loss_kernel(y, target, *, name):
    t, d = y.shape
    tm = _pick(t, (512, 256, 128))

    def body(y_ref, t_ref, dy_ref, l_ref):
        diff = y_ref[...] - t_ref[...]
        dy_ref[...] = diff * (1.0 / d)

        @pl.when(pl.program_id(0) == 0)
        def _():
            l_ref[...] = jnp.zeros_like(l_ref)

        part = jnp.sum(jnp.mean(diff * diff, axis=1, keepdims=True), axis=0, keepdims=True)
        l_ref[...] += 0.5 * part

    row = pl.BlockSpec((tm, d), lambda i: (i, 0))
    return pl.pallas_call(
        body, name=name, grid=(t // tm,), in_specs=[row, row],
        out_specs=[row, pl.BlockSpec((1, 1), lambda i: (0, 0))],
        out_shape=[jax.ShapeDtypeStruct((t, d), F32), jax.ShapeDtypeStruct((1, 1), F32)],
        compiler_params=_params(1),
    )(y, target)


LRU_TM = 256


def _lru_gate_terms(r, lam):
    sp = _softplus(-lam)
    la = -LRU_C * r * sp
    a = jnp.exp(la)
    em = jnp.tanh(la) * (jnp.exp(2.0 * la) + 1.0)
    s = jnp.sqrt(-em)
    return la, a, s, sp


def _lru_fwd(hbuf, conv_w, conv_b, wa, ba, wx, bx, lam, *, name):
    t = hbuf.shape[0]
    w = LRU_WIDTH
    tm = _pick(t, (LRU_TM, 128))
    cu, cg = COL_U // w, COL_G // w
    hb = tm // SUBLANES

    def body(u_ref, up_ref, g_ref, cw_ref, cb_ref, wa_ref, ba_ref, wx_ref, bx_ref, lam_ref,
             y_ref, u_out, r_out, i_out, a_out, h_out, carry):
        i = pl.program_id(0)

        @pl.when(i == 0)
        def _():
            carry[...] = jnp.zeros_like(carry)

        prev = jnp.where(i == 0, 0.0, up_ref[...])
        u = _conv_taps(u_ref[...], prev, cw_ref[...], cb_ref[...])
        ub = u.astype(BF16)
        r = _sigmoid(_dot(ub, wa_ref[...], 1, 0) + ba_ref[...])
        ig = _sigmoid(_dot(ub, wx_ref[...], 1, 0) + bx_ref[...])
        _, a, s, _ = _lru_gate_terms(r, lam_ref[...])
        b = s * (ig * u)
        acum, hs = _scan_fwd(a, b)
        h = hs + acum * carry[0:1, :]
        carry[...] = jnp.broadcast_to(h[tm - 1:tm, :], carry.shape)
        gl, _ = _gelu_and_grad(g_ref[...])
        y_ref[...] = h * gl
        u_out[...] = u
        r_out[...] = r
        i_out[...] = ig
        a_out[...] = a
        h_out[...] = h

    row = pl.BlockSpec((tm, w), lambda i: (i, 0))
    vec = pl.BlockSpec((1, w), lambda i: (0, 0))
    mat = pl.BlockSpec((w, w), lambda i: (0, 0))
    o = jax.ShapeDtypeStruct((t, w), F32)
    return pl.pallas_call(
        body, name=name, grid=(t // tm,),
        in_specs=[pl.BlockSpec((tm, w), lambda i: (i, cu)),
                  pl.BlockSpec((SUBLANES, w), lambda i: (jnp.maximum(i * hb - 1, 0), cu)),
                  pl.BlockSpec((tm, w), lambda i: (i, cg)),
                  pl.BlockSpec((CONV_K, w), lambda i: (0, 0)), vec, mat, vec, mat, vec, vec],
        out_specs=[row] * 6, out_shape=[o] * 6,
        scratch_shapes=[pltpu.VMEM((SUBLANES, w), F32)],
        compiler_params=_params(1),
    )(hbuf, hbuf, hbuf, conv_w, conv_b, wa, ba, wx, bx, lam)


def _lru_bwd(dymix, hbuf, u, r, ig, a, h, conv_w, wa, wx, lam, *, name):
    t = hbuf.shape[0]
    w = LRU_WIDTH
    tm = _pick(t, (LRU_TM, 128))
    nb = t // tm
    cu, cg = COL_U // w, COL_G // w
    hb = tm // SUBLANES
    last8 = t // SUBLANES - 1

    def body(dy_ref, ur_ref, g_ref, u_ref, r_ref, i_ref, a_ref, an_ref, h_ref, hp_ref,
             cw_ref, wa_ref, wx_ref, lam_ref,
             dur_ref, dgr_ref, dcw_ref, dcb_ref, dwa_ref, dba_ref, dwx_ref, dbx_ref, dlam_ref,
             lcarry, dnext):
        i = pl.program_id(0)
        ib = nb - 1 - i

        @pl.when(i == 0)
        def _():
            lcarry[...] = jnp.zeros_like(lcarry)
            dnext[...] = jnp.zeros_like(dnext)
            for ref in (dcw_ref, dcb_ref, dwa_ref, dba_ref, dwx_ref, dbx_ref, dlam_ref):
                ref[...] = jnp.zeros_like(ref)

        dy = dy_ref[...]
        hh = h_ref[...]
        av = a_ref[...]
        uu = u_ref[...]
        rr = r_ref[...]
        ii = i_ref[...]
        lam_v = lam_ref[...]
        gl, dgl = _gelu_and_grad(g_ref[...])
        dgr_ref[...] = dy * hh * dgl
        dh_out = dy * gl
        a_next = _shift_up(av, 1, jnp.where(ib == nb - 1, 0.0, an_ref[...]))
        acum, ls = _scan_bwd(a_next, dh_out)
        lam_adj = ls + acum * lcarry[0:1, :]
        lcarry[...] = jnp.broadcast_to(lam_adj[0:1, :], lcarry.shape)
        h_prev = _shift_down(hh, 1, jnp.where(ib == 0, 0.0, hp_ref[...]))
        da = lam_adj * h_prev
        _, a2, s, sp = _lru_gate_terms(rr, lam_v)
        d_igu = lam_adj * s
        ds = lam_adj * ii * uu
        dla = da * a2 - ds * (a2 * a2) / s
        dr = dla * (-LRU_C * sp)
        dlam_ref[...] += jnp.sum(dla * (LRU_C * rr * _sigmoid(-lam_v)), axis=0, keepdims=True)
        dpre_r = dr * rr * (1.0 - rr)
        dpre_i = d_igu * uu * ii * (1.0 - ii)
        prb = dpre_r.astype(BF16)
        pib = dpre_i.astype(BF16)
        ub = uu.astype(BF16)
        du = d_igu * ii + _dot(prb, wa_ref[...], 1, 1) + _dot(pib, wx_ref[...], 1, 1)
        dwa_ref[...] += _dot(ub, prb, 0, 0)
        dwx_ref[...] += _dot(ub, pib, 0, 0)
        dba_ref[...] += jnp.sum(dpre_r, axis=0, keepdims=True)
        dbx_ref[...] += jnp.sum(dpre_i, axis=0, keepdims=True)
        dur, dws = _conv_taps_bwd(du, dnext[...], cw_ref[...], ur_ref[...])
        dur_ref[...] = dur
        dcw_ref[...] += dws
        dcb_ref[...] += jnp.sum(du, axis=0, keepdims=True)
        dnext[...] = du[:SUBLANES]

    def rowspec(col):
        return pl.BlockSpec((tm, w), lambda i: (nb - 1 - i, col))

    row = rowspec(0)
    nxt = pl.BlockSpec((SUBLANES, w), lambda i: (jnp.minimum((nb - i) * hb, last8), 0))
    prv = pl.BlockSpec((SUBLANES, w), lambda i: (jnp.maximum((nb - 1 - i) * hb - 1, 0), 0))
    vec = pl.BlockSpec((1, w), lambda i: (0, 0))
    mat = pl.BlockSpec((w, w), lambda i: (0, 0))
    cw = pl.BlockSpec((CONV_K, w), lambda i: (0, 0))
    o = jax.ShapeDtypeStruct((t, w), F32)
    v1 = jax.ShapeDtypeStruct((1, w), F32)
    m1 = jax.ShapeDtypeStruct((w, w), F32)
    return pl.pallas_call(
        body, name=name, grid=(nb,),
        in_specs=[rowspec(0), rowspec(cu), rowspec(cg), row, row, row, row, nxt, row, prv, cw, mat, mat, vec],
        out_specs=[row, row, cw, vec, mat, vec, mat, vec, vec],
        out_shape=[o, o, jax.ShapeDtypeStruct((CONV_K, w), F32), v1, m1, v1, m1, v1, v1],
        scratch_shapes=[pltpu.VMEM((SUBLANES, w), F32), pltpu.VMEM((SUBLANES, w), F32)],
        compiler_params=_params(1),
    )(dymix, hbuf, hbuf, u, r, ig, a, a, h, h, conv_w, wa, wx, lam)


FOX_T = 512
FOX_PREP_TM = 256


def _log_sigmoid(x):
    return jnp.minimum(x, 0.0) - jnp.log(1.0 + jnp.exp(-jnp.abs(x)))


def _fox_prep(hbuf, bf_vec, *, name):
    t = hbuf.shape[0]
    tm = _pick(t, (FOX_PREP_TM, 128))
    cs = COL_SMALL // LANES

    def body(s_ref, b_ref, f_ref, carry):
        i = pl.program_id(0)

        @pl.when(i == 0)
        def _():
            carry[...] = jnp.zeros_like(carry)

        lf = _log_sigmoid(s_ref[...] + b_ref[...])
        f = _cumsum_rows(lf) + carry[0:1, :]
        carry[...] = jnp.broadcast_to(f[tm - 1:tm, :], carry.shape)
        f_ref[...] = f

    return pl.pallas_call(
        body, name=name, grid=(t // tm,),
        in_specs=[pl.BlockSpec((tm, LANES), lambda i: (i, cs)), pl.BlockSpec((1, LANES), lambda i: (0, 0))],
        out_specs=pl.BlockSpec((tm, LANES), lambda i: (i, 0)),
        out_shape=jax.ShapeDtypeStruct((t, LANES), F32),
        scratch_shapes=[pltpu.VMEM((SUBLANES, LANES), F32)],
        compiler_params=_params(1),
    )(hbuf, bf_vec)


def _fox_post(dfc, hbuf, bf_vec, *, name):
    t = hbuf.shape[0]
    tm = _pick(t, (FOX_PREP_TM, 128))
    nb = t // tm
    cs = COL_SMALL // LANES

    def body(df_ref, s_ref, b_ref, o_ref, db_ref, carry):
        i = pl.program_id(0)

        @pl.when(i == 0)
        def _():
            carry[...] = jnp.zeros_like(carry)
            db_ref[...] = jnp.zeros_like(db_ref)

        dlf = _cumsum_rows(df_ref[...], reverse=True) + carry[0:1, :]
        carry[...] = jnp.broadcast_to(dlf[0:1, :], carry.shape)
        dl = dlf * _sigmoid(-(s_ref[...] + b_ref[...]))
        dl = jnp.where(_iota(dl.shape, 1) < ATT_HEADS, dl, 0.0)
        o_ref[...] = dl
        db_ref[...] += jnp.sum(dl, axis=0, keepdims=True)

    vec = pl.BlockSpec((1, LANES), lambda i: (0, 0))
    return pl.pallas_call(
        body, name=name, grid=(nb,),
        in_specs=[pl.BlockSpec((tm, LANES), lambda i: (nb - 1 - i, 0)),
                  pl.BlockSpec((tm, LANES), lambda i: (nb - 1 - i, cs)), vec],
        out_specs=[pl.BlockSpec((tm, LANES), lambda i: (nb - 1 - i, 0)), vec],
        out_shape=[jax.ShapeDtypeStruct((t, LANES), F32), jax.ShapeDtypeStruct((1, LANES), F32)],
        scratch_shapes=[pltpu.VMEM((SUBLANES, LANES), F32)],
        compiler_params=_params(1),
    )(dfc, hbuf, bf_vec)


def _fox_scores(qp, kpb, fq, fk, h, hm, causal):
    qm = jnp.where(hm, qp, 0.0).astype(BF16)
    s = _dot(qm, kpb, 1, 1) * (HEAD_DIM ** -0.5) + (_col(fq, h) - _row(fk, h))
    return jnp.where(causal, s, NEG), qm


def _fox_masks(i, j, tq):
    row = i * tq + _iota((tq, tq), 0)
    col = j * tq + _iota((tq, tq), 1)
    lane = _iota((1, LANES), 1)
    return col <= row, (lane < HEAD_DIM, lane >= HEAD_DIM)


def _fox_fwd(hbuf, fc, ft, *, name):
    t = hbuf.shape[0]
    w = ATT_WIDTH
    tq = _pick(t, (FOX_T, 256, 128))
    nq = t // tq
    cq, ck, cv = COL_Q // w, COL_K // w, COL_V // w

    def body(q_ref, k_ref, v_ref, fq_ref, fk_ref, o_ref, lse_ref, m_s, l_s, acc_s):
        i = pl.program_id(0)
        j = pl.program_id(1)

        @pl.when(j == 0)
        def _():
            m_s[...] = jnp.full_like(m_s, NEG)
            l_s[...] = jnp.zeros_like(l_s)
            acc_s[...] = jnp.zeros_like(acc_s)

        @pl.when(j <= i)
        def _():
            causal, hms = _fox_masks(i, j, tq)
            fq = fq_ref[...]
            fk = fk_ref[...]
            for pr in range(2):
                sl = slice(LANES * pr, LANES * (pr + 1))
                qp = q_ref[:, sl]
                kpb = k_ref[:, sl].astype(BF16)
                vpb = v_ref[:, sl].astype(BF16)
                for hh in range(2):
                    h = 2 * pr + hh
                    s, _ = _fox_scores(qp, kpb, fq, fk, h, hms[hh], causal)
                    m_prev = m_s[h]
                    m_new = jnp.maximum(m_prev, jnp.max(s, axis=1, keepdims=True))
                    alpha = jnp.exp(m_prev - m_new)
                    p = jnp.exp(s - m_new)
                    l_s[h] = alpha * l_s[h] + jnp.sum(p, axis=1, keepdims=True)
                    m_s[h] = m_new
                    pv = _dot(p.astype(BF16), vpb, 1, 0)
                    acc = acc_s[:, sl]
                    acc_s[:, sl] = jnp.where(hms[hh], alpha * acc + pv, acc)

        @pl.when(j == i)
        def _():
            _, hms = _fox_masks(i, j, tq)
            for pr in range(2):
                sl = slice(LANES * pr, LANES * (pr + 1))
                acc = acc_s[:, sl]
                o_ref[:, sl] = jnp.where(hms[0], acc / l_s[2 * pr], acc / l_s[2 * pr + 1])
                for hh in range(2):
                    h = 2 * pr + hh
                    lse_ref[h] = m_s[h] + jnp.log(l_s[h])

    return pl.pallas_call(
        body, name=name, grid=(nq, nq),
        in_specs=[pl.BlockSpec((tq, w), lambda i, j: (i, cq)),
                  pl.BlockSpec((tq, w), lambda i, j: (jnp.minimum(j, i), ck)),
                  pl.BlockSpec((tq, w), lambda i, j: (jnp.minimum(j, i), cv)),
                  pl.BlockSpec((tq, LANES), lambda i, j: (i, 0)),
                  pl.BlockSpec((SUBLANES, tq), lambda i, j: (0, jnp.minimum(j, i)))],
        out_specs=[pl.BlockSpec((tq, w), lambda i, j: (i, 0)),
                   pl.BlockSpec((ATT_HEADS, tq, 1), lambda i, j: (0, i, 0))],
        out_shape=[jax.ShapeDtypeStruct((t, w), F32), jax.ShapeDtypeStruct((ATT_HEADS, t, 1), F32)],
        scratch_shapes=[pltpu.VMEM((ATT_HEADS, tq, 1), F32), pltpu.VMEM((ATT_HEADS, tq, 1), F32),
                        pltpu.VMEM((tq, w), F32)],
        compiler_params=_params(2),
    )(hbuf, hbuf, hbuf, fc, ft)


def _fox_bwd_kv(hbuf, fc, ft, dymix, o, lse, *, name):
    t = hbuf.shape[0]
    w = ATT_WIDTH
    tq = _pick(t, (FOX_T, 256, 128))
    nq = t // tq
    cq, ck, cv = COL_Q // w, COL_K // w, COL_V // w
    cdo = ATT_WIDTH // w

    def body(q_ref, k_ref, v_ref, fq_ref, fk_ref, do_ref, o_ref, lse_ref, dk_ref, dv_ref, dfk_ref,
             dk_s, dv_s, dfk_s):
        j = pl.program_id(0)
        i = pl.program_id(1)

        @pl.when(i == 0)
        def _():
            dk_s[...] = jnp.zeros_like(dk_s)
            dv_s[...] = jnp.zeros_like(dv_s)
            dfk_s[...] = jnp.zeros_like(dfk_s)

        @pl.when(i >= j)
        def _():
            causal, hms = _fox_masks(i, j, tq)
            fq = fq_ref[...]
            fk = fk_ref[...]
            rows8 = _iota((SUBLANES, tq), 0)
            for pr in range(2):
                sl = slice(LANES * pr, LANES * (pr + 1))
                qp = q_ref[:, sl]
                kpb = k_ref[:, sl].astype(BF16)
                vpb = v_ref[:, sl].astype(BF16)
                dop = do_ref[:, sl]
                op = o_ref[:, sl]
                for hh in range(2):
                    h = 2 * pr + hh
                    s, qm = _fox_scores(qp, kpb, fq, fk, h, hms[hh], causal)
                    p = jnp.exp(s - lse_ref[h])
                    dom = jnp.where(hms[hh], dop, 0.0)
                    domb = dom.astype(BF16)
                    dv_s[:, sl] += _dot(p.astype(BF16), domb, 0, 0)
                    dp = _dot(domb, vpb, 1, 1)
                    delta = jnp.sum(dom * op, axis=1, keepdims=True)
                    ds = p * (dp - delta)
                    dk_s[:, sl] += _dot(ds.astype(BF16), qm, 0, 0) * (HEAD_DIM ** -0.5)
                    dfk_s[...] += jnp.where(rows8 == h, jnp.sum(ds, axis=0, keepdims=True), 0.0)

        @pl.when(i == nq - 1)
        def _():
            dk_ref[...] = dk_s[...]
            dv_ref[...] = dv_s[...]
            dfk_ref[...] = dfk_s[...]

    qi = lambda j, i: jnp.maximum(i, j)
    return pl.pallas_call(
        body, name=name, grid=(nq, nq),
        in_specs=[pl.BlockSpec((tq, w), lambda j, i: (qi(j, i), cq)),
                  pl.BlockSpec((tq, w), lambda j, i: (j, ck)),
                  pl.BlockSpec((tq, w), lambda j, i: (j, cv)),
                  pl.BlockSpec((tq, LANES), lambda j, i: (qi(j, i), 0)),
                  pl.BlockSpec((SUBLANES, tq), lambda j, i: (0, j)),
                  pl.BlockSpec((tq, w), lambda j, i: (qi(j, i), cdo)),
                  pl.BlockSpec((tq, w), lambda j, i: (qi(j, i), 0)),
                  pl.BlockSpec((ATT_HEADS, tq, 1), lambda j, i: (0, qi(j, i), 0))],
        out_specs=[pl.BlockSpec((tq, w), lambda j, i: (j, 0)), pl.BlockSpec((tq, w), lambda j, i: (j, 0)),
                   pl.BlockSpec((SUBLANES, tq), lambda j, i: (0, j))],
        out_shape=[jax.ShapeDtypeStruct((t, w), F32), jax.ShapeDtypeStruct((t, w), F32),
                   jax.ShapeDtypeStruct((SUBLANES, t), F32)],
        scratch_shapes=[pltpu.VMEM((tq, w), F32), pltpu.VMEM((tq, w), F32), pltpu.VMEM((SUBLANES, tq), F32)],
        compiler_params=_params(2),
    )(hbuf, hbuf, hbuf, fc, ft, dymix, o, lse)


def _fox_bwd_q(hbuf, fc, ft, dymix, o, lse, *, name):
    t = hbuf.shape[0]
    w = ATT_WIDTH
    tq = _pick(t, (FOX_T, 256, 128))
    nq = t // tq
    cq, ck, cv = COL_Q // w, COL_K // w, COL_V // w
    cdo = ATT_WIDTH // w

    def body(q_ref, k_ref, v_ref, fq_ref, fk_ref, do_ref, o_ref, lse_ref, dq_ref, dfq_ref, dq_s, dfq_s):
        i = pl.program_id(0)
        j = pl.program_id(1)

        @pl.when(j == 0)
        def _():
            dq_s[...] = jnp.zeros_like(dq_s)
            dfq_s[...] = jnp.zeros_like(dfq_s)

        @pl.when(j <= i)
        def _():
            causal, hms = _fox_masks(i, j, tq)
            fq = fq_ref[...]
            fk = fk_ref[...]
            for pr in range(2):
                sl = slice(LANES * pr, LANES * (pr + 1))
                qp = q_ref[:, sl]
                kpb = k_ref[:, sl].astype(BF16)
                vpb = v_ref[:, sl].astype(BF16)
                dop = do_ref[:, sl]
                op = o_ref[:, sl]
                for hh in range(2):
                    h = 2 * pr + hh
                    s, _ = _fox_scores(qp, kpb, fq, fk, h, hms[hh], causal)
                    p = jnp.exp(s - lse_ref[h])
                    dom = jnp.where(hms[hh], dop, 0.0)
                    dp = _dot(dom.astype(BF16), vpb, 1, 1)
                    delta = jnp.sum(dom * op, axis=1, keepdims=True)
                    ds = p * (dp - delta)
                    dq = _dot(ds.astype(BF16), kpb, 1, 0) * (HEAD_DIM ** -0.5)
                    dq_s[:, sl] += jnp.where(hms[hh], dq, 0.0)
                    dfq_s[h] += jnp.sum(ds, axis=1, keepdims=True)

        @pl.when(j == i)
        def _():
            dq_ref[...] = dq_s[...]
            dfq_ref[...] = dfq_s[...]

    kj = lambda i, j: jnp.minimum(j, i)
    return pl.pallas_call(
        body, name=name, grid=(nq, nq),
        in_specs=[pl.BlockSpec((tq, w), lambda i, j: (i, cq)),
                  pl.BlockSpec((tq, w), lambda i, j: (kj(i, j), ck)),
                  pl.BlockSpec((tq, w), lambda i, j: (kj(i, j), cv)),
                  pl.BlockSpec((tq, LANES), lambda i, j: (i, 0)),
                  pl.BlockSpec((SUBLANES, tq), lambda i, j: (0, kj(i, j))),
                  pl.BlockSpec((tq, w), lambda i, j: (i, cdo)),
                  pl.BlockSpec((tq, w), lambda i, j: (i, 0)),
                  pl.BlockSpec((ATT_HEADS, tq, 1), lambda i, j: (0, i, 0))],
        out_specs=[pl.BlockSpec((tq, w), lambda i, j: (i, 0)),
                   pl.BlockSpec((ATT_HEADS, tq, 1), lambda i, j: (0, i, 0))],
        out_shape=[jax.ShapeDtypeStruct((t, w), F32), jax.ShapeDtypeStruct((ATT_HEADS, t, 1), F32)],
        scratch_shapes=[pltpu.VMEM((tq, w), F32), pltpu.VMEM((ATT_HEADS, tq, 1), F32)],
        compiler_params=_params(2),
    )(hbuf, hbuf, hbuf, fc, ft, dymix, o, lse)


GROUP_W = SSD_WIDTH // SSD_GROUPS
HEADS_PER_GROUP = SSD_HEADS // SSD_GROUPS


def _ssd_chunk_common(xr, prev8, sm, cw, cb, dtb, avec):
    c = _conv_taps(xr, prev8, cw, cb)
    sig = _sigmoid(c)
    xa = c * sig
    dt = _softplus(sm + dtb)
    a = dt * avec
    acum = _cumsum_rows(a)
    return c, sig, xa, dt, acum


def _ssd_decays(acum, g):
    n = acum.shape[0]
    atot = acum[n - 1:n, :]
    lane0 = LANE_DT + HEADS_PER_GROUP * g
    e = _head_expand(jnp.exp(acum), lane0, HEADS_PER_GROUP, GROUP_W)
    dec = _head_expand(jnp.exp(atot - acum), lane0, HEADS_PER_GROUP, GROUP_W)
    etot = _head_expand(jnp.exp(atot), lane0, HEADS_PER_GROUP, GROUP_W)
    return e, dec, etot


def _ssd_ldec(acum, acum_t, lane, tril):
    return jnp.exp(jnp.where(tril, _col(acum, lane) - _row(acum_t, lane), NEG))


def _ssd_fwd(hbuf, conv_w, conv_b, dtb_vec, a_vec, d_exp, norm_g, *, name):
    t = hbuf.shape[0]
    L = SSD_CHUNK
    nc = t // L
    hb = L // SUBLANES
    cs = COL_SMALL // LANES
    cz = COL_Z // SSD_WIDTH

    def body(x_ref, xp_ref, z_ref, s_ref, cw_ref, cb_ref, dtb_ref, av_ref, dx_ref, ng_ref,
             yc_ref, y_ref, st_ref, state):
        i = pl.program_id(0)

        @pl.when(i == 0)
        def _():
            state[...] = jnp.zeros_like(state)

        prev = jnp.where(i == 0, 0.0, xp_ref[...])
        _, _, xa, dt, acum = _ssd_chunk_common(x_ref[...], prev, s_ref[...], cw_ref[...], cb_ref[...],
                                               dtb_ref[...], av_ref[...])
        acum_t = acum.T
        xs = xa[:, :SSD_WIDTH]
        xdt = xs * _head_expand(dt, LANE_DT, SSD_HEADS, SSD_WIDTH)
        tril = _iota((L, L), 0) >= _iota((L, L), 1)
        lane = _iota((1, LANES), 1)
        ys = []
        for g in range(SSD_GROUPS):
            bg = xa[:, SSD_WIDTH + SSD_STATE * g:SSD_WIDTH + SSD_STATE * (g + 1)].astype(BF16)
            cg = xa[:, SSD_WIDTH + SSD_STATE * (SSD_GROUPS + g):SSD_WIDTH + SSD_STATE * (SSD_GROUPS + g + 1)].astype(BF16)
            gm = _dot(cg, bg, 1, 1)
            e, dec, etot = _ssd_decays(acum, g)
            s_in = state[g]
            st_ref[0, g] = s_in
            xg = xdt[:, GROUP_W * g:GROUP_W * (g + 1)]
            y_off = e * _dot(cg, s_in.astype(BF16), 1, 0)
            state[g] = etot * s_in + _dot(bg, (dec * xg).astype(BF16), 0, 0)
            for pr in range(2):
                xp = xg[:, LANES * pr:LANES * (pr + 1)].astype(BF16)
                outs = []
                for hh in range(2):
                    h = HEADS_PER_GROUP * g + 2 * pr + hh
                    m = gm * _ssd_ldec(acum, acum_t, LANE_DT + h, tril)
                    outs.append(_dot(m.astype(BF16), xp, 1, 0))
                ys.append(jnp.where(lane < HEAD_DIM, outs[0], outs[1]) + y_off[:, LANES * pr:LANES * (pr + 1)])
        y = jnp.concatenate(ys, axis=1)
        y_ref[...] = y
        yd = y + dx_ref[...] * xs
        zz = z_ref[...]
        y2 = yd * zz * _sigmoid(zz)
        ng = ng_ref[...]
        outs = []
        for g in range(SSD_GROUPS):
            yg = y2[:, GROUP_W * g:GROUP_W * (g + 1)]
            rs = lax.rsqrt(jnp.mean(yg * yg, axis=1, keepdims=True) + RMS_EPS)
            outs.append(yg * rs * ng[:, GROUP_W * g:GROUP_W * (g + 1)])
        yc_ref[...] = jnp.concatenate(outs, axis=1)

    cdim = SSD_CONV_DIM
    vecc = pl.BlockSpec((1, cdim), lambda i: (0, 0))
    vecl = pl.BlockSpec((1, LANES), lambda i: (0, 0))
    vecw = pl.BlockSpec((1, SSD_WIDTH), lambda i: (0, 0))
    roww = pl.BlockSpec((L, SSD_WIDTH), lambda i: (i, 0))
    return pl.pallas_call(
        body, name=name, grid=(nc,),
        in_specs=[pl.BlockSpec((L, cdim), lambda i: (i, 0)),
                  pl.BlockSpec((SUBLANES, cdim), lambda i: (jnp.maximum(i * hb - 1, 0), 0)),
                  pl.BlockSpec((L, SSD_WIDTH), lambda i: (i, cz)),
                  pl.BlockSpec((L, LANES), lambda i: (i, cs)),
                  pl.BlockSpec((CONV_K, cdim), lambda i: (0, 0)), vecc, vecl, vecl, vecw, vecw],
        out_specs=[roww, roww, pl.BlockSpec((1, SSD_GROUPS, SSD_STATE, GROUP_W), lambda i: (i, 0, 0, 0))],
        out_shape=[jax.ShapeDtypeStruct((t, SSD_WIDTH), F32), jax.ShapeDtypeStruct((t, SSD_WIDTH), F32),
                   jax.ShapeDtypeStruct((nc, SSD_GROUPS, SSD_STATE, GROUP_W), F32)],
        scratch_shapes=[pltpu.VMEM((SSD_GROUPS, SSD_STATE, GROUP_W), F32)],
        compiler_params=_params(1),
    )(hbuf, hbuf, hbuf, hbuf, conv_w, conv_b, dtb_vec, a_vec, d_exp, norm_g)


def _ssd_bwd(dymix, hbuf, y_ssd, states, conv_w, conv_b, dtb_vec, a_vec, d_exp, norm_g, *, name):
    t = hbuf.shape[0]
    L = SSD_CHUNK
    nc = t // L
    hb = L // SUBLANES
    cs = COL_SMALL // LANES
    cz = COL_Z // SSD_WIDTH
    cdy = (LRU_WIDTH + ATT_WIDTH) // SSD_WIDTH
    cdim = SSD_CONV_DIM

    def body(dyc_ref, x_ref, xp_ref, z_ref, s_ref, y_ref, st_ref, cw_ref, cb_ref, dtb_ref, av_ref, dx_ref, ng_ref,
             dxr_ref, dz_ref, dsm_ref, dng_ref, dd_ref, da_ref, ddtb_ref, dcw_ref, dcb_ref,
             dstate, dnext):
        i = pl.program_id(0)
        ic = nc - 1 - i

        @pl.when(i == 0)
        def _():
            dstate[...] = jnp.zeros_like(dstate)
            dnext[...] = jnp.zeros_like(dnext)
            for ref in (dng_ref, dd_ref, da_ref, ddtb_ref, dcw_ref, dcb_ref):
                ref[...] = jnp.zeros_like(ref)

        xr = x_ref[...]
        sm = s_ref[...]
        prev = jnp.where(ic == 0, 0.0, xp_ref[...])
        avec = av_ref[...]
        c, sig, xa, dt, acum = _ssd_chunk_common(xr, prev, sm, cw_ref[...], cb_ref[...], dtb_ref[...], avec)
        acum_t = acum.T
        xs = xa[:, :SSD_WIDTH]
        dtx = _head_expand(dt, LANE_DT, SSD_HEADS, SSD_WIDTH)
        xdt = xs * dtx
        tril = _iota((L, L), 0) >= _iota((L, L), 1)
        lane = _iota((1, LANES), 1)
        hmasks = (lane < HEAD_DIM, lane >= HEAD_DIM)

        y = y_ref[...]
        dexp = dx_ref[...]
        yd = y + dexp * xs
        zz = z_ref[...]
        sz = _sigmoid(zz)
        siluz = zz * sz
        y2 = yd * siluz
        ng = ng_ref[...]
        dyc = dyc_ref[...]
        dy2s, dngs = [], []
        for g in range(SSD_GROUPS):
            sl = slice(GROUP_W * g, GROUP_W * (g + 1))
            yg = y2[:, sl]
            rs = lax.rsqrt(jnp.mean(yg * yg, axis=1, keepdims=True) + RMS_EPS)
            wv = dyc[:, sl] * ng[:, sl]
            dngs.append(jnp.sum(dyc[:, sl] * yg * rs, axis=0, keepdims=True))
            dy2s.append(rs * wv - yg * (rs * rs * rs) * jnp.mean(wv * yg, axis=1, keepdims=True))
        dy2 = jnp.concatenate(dy2s, axis=1)
        dng_ref[...] += jnp.concatenate(dngs, axis=1)
        dz_ref[...] = dy2 * yd * (sz * (1.0 + zz * (1.0 - sz)))
        dy = dy2 * siluz
        dd_ref[...] += jnp.sum(dy * xs, axis=0, keepdims=True)

        dxs, dbs, dcs = [], [], []
        datot = jnp.zeros((1, LANES), F32)
        lanes = _iota((L, LANES), 1)
        dacum = jnp.zeros((L, LANES), F32)
        for g in range(SSD_GROUPS):
            sl = slice(GROUP_W * g, GROUP_W * (g + 1))
            bg = xa[:, SSD_WIDTH + SSD_STATE * g:SSD_WIDTH + SSD_STATE * (g + 1)].astype(BF16)
            cg = xa[:, SSD_WIDTH + SSD_STATE * (SSD_GROUPS + g):SSD_WIDTH + SSD_STATE * (SSD_GROUPS + g + 1)].astype(BF16)
            gm = _dot(cg, bg, 1, 1)
            e, dec, etot = _ssd_decays(acum, g)
            s_in = st_ref[0, g]
            ds_out = dstate[g]
            dyg = dy[:, sl]
            xg = xdt[:, sl]
            edy = (e * dyg).astype(BF16)
            dstate[g] = etot * ds_out + _dot(cg, edy, 0, 0)
            dx_state = dec * _dot(bg, ds_out.astype(BF16), 1, 0)
            y_off = e * _dot(cg, s_in.astype(BF16), 1, 0)
            dacum = dacum + _head_reduce_group(dyg * y_off - xg * dx_state, g)
            dc_off = _dot(edy, s_in.astype(BF16), 1, 1)
            db_state = _dot((dec * xg).astype(BF16), ds_out.astype(BF16), 1, 1)
            dgsum = jnp.zeros((L, L), F32)
            dx_pairs = []
            for pr in range(2):
                psl = slice(LANES * pr, LANES * (pr + 1))
                xp = xg[:, psl]
                dyp = dyg[:, psl]
                dx_pair = jnp.zeros((L, LANES), F32)
                for hh in range(2):
                    h = HEADS_PER_GROUP * g + 2 * pr + hh
                    ldec = _ssd_ldec(acum, acum_t, LANE_DT + h, tril)
                    dym = jnp.where(hmasks[hh], dyp, 0.0).astype(BF16)
                    xm = jnp.where(hmasks[hh], xp, 0.0).astype(BF16)
                    dx_pair = dx_pair + _dot((gm * ldec).astype(BF16), dym, 0, 0)
                    dml = _dot(dym, xm, 1, 1) * ldec
                    dgsum = dgsum + dml
                    qm = dml * gm
                    seg = jnp.sum(qm, axis=1, keepdims=True) - jnp.sum(qm.T, axis=1, keepdims=True)
                    dacum = dacum + jnp.where(lanes == LANE_DT + h, seg, 0.0)
                dx_pairs.append(dx_pair)
            dgb = dgsum.astype(BF16)
            dcs.append(_dot(dgb, bg, 1, 0) + dc_off)
            dbs.append(_dot(dgb, cg, 0, 0) + db_state)
            dxg = jnp.concatenate(dx_pairs, axis=1) + dx_state
            dxs.append(dxg)
            v = jnp.sum(dx_state * xg, axis=0, keepdims=True) + etot * jnp.sum(ds_out * s_in, axis=0, keepdims=True)
            datot = datot + _head_reduce_row(v, LANE_DT + HEADS_PER_GROUP * g, HEADS_PER_GROUP)
        dx = jnp.concatenate(dxs, axis=1)
        dacum = dacum + jnp.where(_iota((L, LANES), 0) == L - 1, datot, 0.0)
        da = _cumsum_rows(dacum, reverse=True)
        ddt = da * avec + _head_reduce(dx * xs, LANE_DT, SSD_HEADS)
        da_ref[...] += jnp.sum(da * dt, axis=0, keepdims=True)
        ddt_raw = ddt * _sigmoid(sm + dtb_ref[...])
        ddt_raw = jnp.where((lanes >= LANE_DT) & (lanes < LANE_DT + SSD_HEADS), ddt_raw, 0.0)
        dsm_ref[...] = ddt_raw
        ddtb_ref[...] += jnp.sum(ddt_raw, axis=0, keepdims=True)
        dxs_total = dx * dtx + dexp * dy
        dxa = jnp.concatenate([dxs_total] + dbs + dcs, axis=1)
        dc = dxa * (sig * (1.0 + c * (1.0 - sig)))
        dxr, dws = _conv_taps_bwd(dc, dnext[...], cw_ref[...], xr)
        dxr_ref[...] = dxr
        dcw_ref[...] += dws
        dcb_ref[...] += jnp.sum(dc, axis=0, keepdims=True)
        dnext[...] = dc[:SUBLANES]

    rev = lambda i: nc - 1 - i
    vecc = pl.BlockSpec((1, cdim), lambda i: (0, 0))
    vecl = pl.BlockSpec((1, LANES), lambda i: (0, 0))
    vecw = pl.BlockSpec((1, SSD_WIDTH), lambda i: (0, 0))
    cwspec = pl.BlockSpec((CONV_K, cdim), lambda i: (0, 0))
    roww = pl.BlockSpec((L, SSD_WIDTH), lambda i: (rev(i), 0))
    return pl.pallas_call(
        body, name=name, grid=(nc,),
        in_specs=[pl.BlockSpec((L, SSD_WIDTH), lambda i: (rev(i), cdy)),
                  pl.BlockSpec((L, cdim), lambda i: (rev(i), 0)),
                  pl.BlockSpec((SUBLANES, cdim), lambda i: (jnp.maximum(rev(i) * hb - 1, 0), 0)),
                  pl.BlockSpec((L, SSD_WIDTH), lambda i: (rev(i), cz)),
                  pl.BlockSpec((L, LANES), lambda i: (rev(i), cs)),
                  roww,
                  pl.BlockSpec((1, SSD_GROUPS, SSD_STATE, GROUP_W), lambda i: (rev(i), 0, 0, 0)),
                  cwspec, vecc, vecl, vecl, vecw, vecw],
        out_specs=[pl.BlockSpec((L, cdim), lambda i: (rev(i), 0)), roww,
                   pl.BlockSpec((L, LANES), lambda i: (rev(i), 0)),
                   vecw, vecw, vecl, vecl, cwspec, vecc],
        out_shape=[jax.ShapeDtypeStruct((t, cdim), F32), jax.ShapeDtypeStruct((t, SSD_WIDTH), F32),
                   jax.ShapeDtypeStruct((t, LANES), F32),
                   jax.ShapeDtypeStruct((1, SSD_WIDTH), F32), jax.ShapeDtypeStruct((1, SSD_WIDTH), F32),
                   jax.ShapeDtypeStruct((1, LANES), F32), jax.ShapeDtypeStruct((1, LANES), F32),
                   jax.ShapeDtypeStruct((CONV_K, cdim), F32), jax.ShapeDtypeStruct((1, cdim), F32)],
        scratch_shapes=[pltpu.VMEM((SSD_GROUPS, SSD_STATE, GROUP_W), F32), pltpu.VMEM((SUBLANES, cdim), F32)],
        compiler_params=_params(1),
    )(dymix, hbuf, hbuf, hbuf, hbuf, y_ssd, states, conv_w, conv_b, dtb_vec, a_vec, d_exp, norm_g)


def _head_reduce_group(x, g):
    return _head_reduce(x, LANE_DT + HEADS_PER_GROUP * g, HEADS_PER_GROUP)


def _head_reduce_row(v, lane0, nheads):
    colhead = _iota(v.shape, 1) // HEAD_DIM
    lane = _iota((1, LANES), 1)
    out = jnp.zeros((1, LANES), F32)
    for h in range(nheads):
        s = jnp.sum(jnp.where(colhead == h, v, 0.0), axis=1, keepdims=True)
        out = jnp.where(lane == lane0 + h, s, out)
    return out


def _exchange(inps, axes, *, mode, name):
    n = 2 ** len(axes)
    counts, out_shapes = [], []
    for a in inps:
        if mode == "gather":
            cnt, rest = a.shape[0], a.shape[1:]
        elif mode == "gather_half":
            cnt, rest = a.shape[0] // 2, a.shape[1:]
        elif mode == "a2a":
            cnt, rest = a.shape[1], a.shape[2:]
        else:
            cnt, rest = a.shape[0], a.shape[2:]
        counts.append(cnt)
        out_shapes.append(jax.ShapeDtypeStruct((n, cnt) + tuple(rest), a.dtype))
    units = sum(counts)
    na = len(inps)

    def body(*refs):
        in_refs, out_refs = refs[:na], refs[na:2 * na]
        send_sems, recv_sems, local_sems = refs[2 * na:]
        pos = {ax: lax.axis_index(ax) for ax in MESH_AXES}

        def slot_of(coord):
            s = 0
            for ax in axes:
                s = s * 2 + coord[ax]
            return s

        def src(a, it, slot):
            if mode == "gather":
                return in_refs[a].at[it]
            if mode == "gather_half":
                return in_refs[a].at[pos["c"] * counts[a] + it]
            if mode == "a2a":
                return in_refs[a].at[slot, it]
            return in_refs[a].at[it, slot]

        me = slot_of(pos)
        copies = []
        unit = 0
        for a in range(na):
            for it in range(counts[a]):
                cp = pltpu.make_async_copy(src(a, it, me), out_refs[a].at[me, it], local_sems.at[unit])
                cp.start()
                copies.append(cp)
                for delta in range(1, n):
                    coord = dict(pos)
                    for b, ax in enumerate(reversed(axes)):
                        if (delta >> b) & 1:
                            coord[ax] = 1 - pos[ax]
                    k = unit * (n - 1) + delta - 1
                    cp = pltpu.make_async_remote_copy(
                        src_ref=src(a, it, slot_of(coord)), dst_ref=out_refs[a].at[me, it],
                        send_sem=send_sems.at[k], recv_sem=recv_sems.at[k],
                        device_id=(coord["x"], coord["y"], coord["c"]), device_id_type=pl.DeviceIdType.MESH)
                    cp.start()
                    copies.append(cp)
                unit += 1
        for cp in copies:
            cp.wait()

    any_spec = pl.BlockSpec(memory_space=pl.ANY)
    return pl.pallas_call(
        body, name=name,
        in_specs=[any_spec] * na, out_specs=[any_spec] * na, out_shape=out_shapes,
        scratch_shapes=[pltpu.SemaphoreType.DMA((units * (n - 1),)), pltpu.SemaphoreType.DMA((units * (n - 1),)),
                        pltpu.SemaphoreType.DMA((units,))],
    )(*inps)


def _sum_slots(buf, out_dtype, *, name):
    n, rows, cols = buf.shape
    tm = _pick(rows, (512, 256, 128, 8))
    if rows % tm:
        tm = rows

    def body(b_ref, o_ref):
        acc = b_ref[0].astype(F32)
        for s in range(1, n):
            acc = acc + b_ref[s].astype(F32)
        o_ref[...] = acc.astype(out_dtype)

    return pl.pallas_call(
        body, name=name, grid=(pl.cdiv(rows, tm),),
        in_specs=[pl.BlockSpec((n, tm, cols), lambda i: (0, i, 0))],
        out_specs=pl.BlockSpec((tm, cols), lambda i: (i, 0)),
        out_shape=jax.ShapeDtypeStruct((rows, cols), out_dtype),
        compiler_params=_params(1),
    )(buf)


def _adamw(w, g, m, v, *, name):
    shape = w.shape
    cols = shape[-1]
    rows = w.size // cols
    w2, g2, m2, v2 = (a.reshape(rows, cols) for a in (w, g, m, v))
    tm = _pick(rows, (256, 128, 64, 32, 16, 8))
    if rows % tm:
        tm = rows
    bc1 = 1.0 - ADAM_B1 ** ADAM_STEP
    bc2 = 1.0 - ADAM_B2 ** ADAM_STEP

    def body(w_ref, g_ref, m_ref, v_ref, d_ref, nm_ref, nv_ref):
        gg = g_ref[...]
        mm = ADAM_B1 * m_ref[...] + (1.0 - ADAM_B1) * gg
        vv = ADAM_B2 * v_ref[...] + (1.0 - ADAM_B2) * (gg * gg)
        m_hat = mm / bc1
        v_hat = vv / bc2
        d_ref[...] = -ADAM_LR * (m_hat / (jnp.sqrt(v_hat) + ADAM_EPS) + ADAM_WD * w_ref[...])
        nm_ref[...] = mm
        nv_ref[...] = vv

    spec = pl.BlockSpec((tm, cols), lambda i: (i, 0))
    o = jax.ShapeDtypeStruct((rows, cols), F32)
    outs = pl.pallas_call(
        body, name=name, grid=(rows // tm,), in_specs=[spec] * 4, out_specs=[spec] * 3, out_shape=[o] * 3,
        compiler_params=_params(1),
    )(w2, g2, m2, v2)
    return tuple(a.reshape(shape) for a in outs)


def _layer_fwd(li, x, xb, pb, W):
    nm = lambda s: f"l{li}_{s}"
    sv = {"x_in_b": xb}
    g1, u1, a1 = _mm_swiglu(xb, W["ffn1_wg"], W["ffn1_wu"], name=nm("ffn1_up"))
    x1, x1b, xh1, rs1 = _mm_ln(a1, W["ffn1_wd"], x, W["ln1_g"], W["ln1_b"], rscale=ALPHA, mscale=0.5, name=nm("ffn1_down_ln"))
    hbuf = _mm(x1b, W["w_in_p"], name=nm("in_proj"))
    ya, lu, lr, lig, la, lh = _lru_fwd(hbuf, W["lru_conv_w"], W["lru_conv_b"], W["lru_wa_bd"], W["lru_ba"],
                                       W["lru_wx_bd"], W["lru_bx"], W["lru_lambda"], name=nm("lru_fwd"))
    fc = _fox_prep(hbuf, W["fox_bf_vec"], name=nm("fox_prep"))
    ft = jnp.pad(fc[:, :ATT_HEADS].T, ((0, SUBLANES - ATT_HEADS), (0, 0)))
    yb, lse = _fox_fwd(hbuf, fc, ft, name=nm("fox_fwd"))
    yc, yssd, states = _ssd_fwd(hbuf, W["ssd_conv_w"], W["ssd_conv_b"], W["ssd_dtb_vec"], W["ssd_a_vec"],
                                W["ssd_d_exp"], W["ssd_norm_g"], name=nm("ssd_fwd"))
    ymix = jnp.concatenate([ya, yb, yc], axis=1).astype(BF16)
    x2, x2b, xh2, rs2 = _mm_ln(ymix, W["w_out"], x1, W["ln2_g"], W["ln2_b"], rscale=ALPHA, mscale=1.0, name=nm("out_proj_ln"))
    g2, u2, a2 = _mm_swiglu(x2b, W["ffn2_wg"], W["ffn2_wu"], name=nm("ffn2_up"))
    x3, x3b, xh3, rs3 = _mm_ln(a2, W["ffn2_wd"], x2, W["ln3_g"], W["ln3_b"], rscale=ALPHA, mscale=0.5, name=nm("ffn2_down_ln"))
    x4, x4b, sg, e = _mm_pe(x3, x3b, pb, W["pe_gate_w"], W["pe_gate_b"], W["pe_proj"], name=nm("ple"))
    sv.update(g1=g1, u1=u1, a1=a1, x1b=x1b, xh1=xh1, rs1=rs1, hbuf=hbuf, lu=lu, lr=lr, lig=lig, la=la, lh=lh,
              fc=fc, ft=ft, yb=yb, lse=lse, yssd=yssd, states=states, ymix=ymix, x2b=x2b, xh2=xh2, rs2=rs2,
              g2=g2, u2=u2, a2=a2, x3b=x3b, xh3=xh3, rs3=rs3, sg=sg, e=e, pb=pb)
    return x4, x4b, sv


def _layer_bwd(li, dx4, sv, W):
    nm = lambda s: f"l{li}_{s}"
    G = {}
    dgp, de, dbg = _pe_bwd_elem(dx4, sv["sg"], sv["e"], name=nm("ple_bwd"))
    G["pe_gate_b"] = dbg
    G["pe_gate_w"] = _mm(sv["x3b"], dgp, ta=True, out_dtype=BF16, name=nm("d_pe_gate_w"))
    G["pe_proj"] = _mm(sv["pb"], de, ta=True, out_dtype=BF16, chip_cols=True, name=nm("d_pe_proj"))
    dr3, G["ln3_g"], G["ln3_b"] = _bwd_proj([(dgp, W["pe_gate_w"])], dx4, rscale=1.0,
                                            ln=(sv["xh3"], sv["rs3"], W["ln3_g"]), name=nm("ln3_bwd"))
    G["ffn2_wd"] = _mm(sv["a2"], dr3, ta=True, scale=0.5, out_dtype=BF16, name=nm("d_ffn2_wd"))
    dg2, du2 = _mm_swiglu_bwd(dr3, W["ffn2_wd"], sv["g2"], sv["u2"], scale=0.5, name=nm("ffn2_act_bwd"))
    G["ffn2_wg"] = _mm(sv["x2b"], dg2, ta=True, out_dtype=BF16, chip_cols=True, name=nm("d_ffn2_wg"))
    G["ffn2_wu"] = _mm(sv["x2b"], du2, ta=True, out_dtype=BF16, chip_cols=True, name=nm("d_ffn2_wu"))
    dr2, G["ln2_g"], G["ln2_b"] = _bwd_proj([(dg2, W["ffn2_wg"]), (du2, W["ffn2_wu"])], dr3, rscale=ALPHA,
                                            ln=(sv["xh2"], sv["rs2"], W["ln2_g"]), name=nm("ln2_bwd"))
    G["w_out"] = _mm(sv["ymix"], dr2, ta=True, out_dtype=BF16, name=nm("d_w_out"))
    dymix = _mm(dr2, W["w_out"], tb=True, name=nm("d_ymix"))
    hbuf = sv["hbuf"]
    (dur, dgr, G["lru_conv_w"], G["lru_conv_b"], G["lru_wa_bd"], G["lru_ba"], G["lru_wx_bd"], G["lru_bx"],
     G["lru_lambda"]) = _lru_bwd(dymix, hbuf, sv["lu"], sv["lr"], sv["lig"], sv["la"], sv["lh"],
                                 W["lru_conv_w"], W["lru_wa_bd"], W["lru_wx_bd"], W["lru_lambda"], name=nm("lru_bwd"))
    dk, dv, dfk = _fox_bwd_kv(hbuf, sv["fc"], sv["ft"], dymix, sv["yb"], sv["lse"], name=nm("fox_bwd_kv"))
    dq, dfq = _fox_bwd_q(hbuf, sv["fc"], sv["ft"], dymix, sv["yb"], sv["lse"], name=nm("fox_bwd_q"))
    dfc = jnp.pad(dfq[:, :, 0].T - dfk[:ATT_HEADS].T, ((0, 0), (0, LANES - ATT_HEADS)))
    dsm_f, G["fox_bf_vec"] = _fox_post(dfc, hbuf, W["fox_bf_vec"], name=nm("fox_post"))
    (dxr, dz, dsm_dt, G["ssd_norm_g"], G["ssd_d_exp"], G["ssd_a_vec"], G["ssd_dtb_vec"], G["ssd_conv_w"],
     G["ssd_conv_b"]) = _ssd_bwd(dymix, hbuf, sv["yssd"], sv["states"], W["ssd_conv_w"], W["ssd_conv_b"],
                                 W["ssd_dtb_vec"], W["ssd_a_vec"], W["ssd_d_exp"], W["ssd_norm_g"], name=nm("ssd_bwd"))
    t = dx4.shape[0]
    dh = jnp.concatenate([dxr.astype(BF16), dz.astype(BF16), dur.astype(BF16), dgr.astype(BF16), dq.astype(BF16),
                          dk.astype(BF16), dv.astype(BF16), (dsm_f + dsm_dt).astype(BF16),
                          jnp.zeros((t, H_WIDTH - COL_SMALL - LANES), BF16)], axis=1)
    G["w_in_p"] = _mm(sv["x1b"], dh, ta=True, name=nm("d_w_in"))
    dr1, G["ln1_g"], G["ln1_b"] = _bwd_proj([(dh, W["w_in_p"])], dr2, rscale=ALPHA,
                                            ln=(sv["xh1"], sv["rs1"], W["ln1_g"]), name=nm("ln1_bwd"))
    G["ffn1_wd"] = _mm(sv["a1"], dr1, ta=True, scale=0.5, out_dtype=BF16, name=nm("d_ffn1_wd"))
    dg1, du1 = _mm_swiglu_bwd(dr1, W["ffn1_wd"], sv["g1"], sv["u1"], scale=0.5, name=nm("ffn1_act_bwd"))
    G["ffn1_wg"] = _mm(sv["x_in_b"], dg1, ta=True, out_dtype=BF16, chip_cols=True, name=nm("d_ffn1_wg"))
    G["ffn1_wu"] = _mm(sv["x_in_b"], du1, ta=True, out_dtype=BF16, chip_cols=True, name=nm("d_ffn1_wu"))
    (dx_in,) = _bwd_proj([(dg1, W["ffn1_wg"]), (du1, W["ffn1_wu"])], dr1, rscale=ALPHA, ln=None, name=nm("x_in_bwd"))
    return dx_in, G


def _block_diag(w):
    n, b, _ = w.shape
    eye = jnp.eye(n, dtype=w.dtype)
    return (eye[:, None, :, None] * w[:, :, None, :]).reshape(n * b, n * b)


def _block_diag_extract(m):
    n, b = LRU_HEADS, HEAD_DIM
    return jnp.stack([m[b * i:b * (i + 1), b * i:b * (i + 1)] for i in range(n)])


def _lane_vec(v, lane0):
    return jnp.pad(v.astype(F32), (lane0, LANES - lane0 - v.shape[0])).reshape(1, LANES)


def _w_in_permute(w):
    d = w.shape[0]
    z = lambda n: jnp.zeros((d, n), w.dtype)
    return jnp.concatenate([w[:, 1796:2820], w[:, 1284:1796], w[:, 0:512], w[:, 512:1280],
                            w[:, 1280:1284], w[:, 2820:2828], z(LANES - 12), z(H_WIDTH - COL_SMALL - LANES)], axis=1)


def _w_in_unpermute(wp):
    return jnp.concatenate([wp[:, COL_U:COL_Q], wp[:, COL_Q:COL_SMALL], wp[:, COL_SMALL:COL_SMALL + 4],
                            wp[:, COL_Z:COL_U], wp[:, COL_XBC:COL_Z], wp[:, COL_SMALL + 4:COL_SMALL + 12]], axis=1)


def _layer_weights(li, chipw, small):
    g = lambda n: small[n][li]
    W = {n: g(n) for n in ("ln1_g", "ln1_b", "ln2_g", "ln2_b", "ln3_g", "ln3_b", "pe_gate_b", "lru_conv_w",
                           "ssd_conv_w")}
    for n in ("ffn1_wg", "ffn1_wu", "ffn2_wg", "ffn2_wu"):
        W[n] = chipw[n]
    for n in ("ffn1_wd", "ffn2_wd", "w_out", "pe_gate_w"):
        W[n] = chipw[n].reshape(-1, D_MODEL)
    W["pe_proj"] = jnp.moveaxis(chipw["pe_proj"], 0, 1).reshape(PLE_DIM, D_MODEL)
    w_in = jnp.moveaxis(chipw["w_in"][:, :, :IN_WIDTH // N_CHIPS], 0, 1).reshape(D_MODEL, IN_WIDTH)
    W["w_in_p"] = _w_in_permute(w_in)
    for n in ("lru_conv_b", "lru_ba", "lru_bx", "lru_lambda", "ssd_conv_b", "ssd_norm_g"):
        W[n] = g(n).reshape(1, -1)
    W["lru_wa_bd"] = _block_diag(g("lru_wa")).astype(BF16)
    W["lru_wx_bd"] = _block_diag(g("lru_wx")).astype(BF16)
    W["fox_bf_vec"] = _lane_vec(g("fox_bf"), LANE_F)
    W["ssd_dtb_vec"] = _lane_vec(g("ssd_dt_bias"), LANE_DT)
    W["ssd_a_vec"] = _lane_vec(-jnp.exp(g("ssd_a_log")), LANE_DT)
    W["ssd_d_exp"] = jnp.repeat(g("ssd_d"), HEAD_DIM).reshape(1, SSD_WIDTH)
    return W


def _layer_big_grads_by_chip(G):
    out = {n: G[n] for n in ("ffn1_wg", "ffn1_wu", "ffn2_wg", "ffn2_wu", "pe_proj")}
    for n in ("ffn1_wd", "ffn2_wd", "w_out", "pe_gate_w"):
        out[n] = G[n].reshape(N_CHIPS, -1, D_MODEL)
    share = IN_WIDTH // N_CHIPS
    d_w_in = jnp.moveaxis(_w_in_unpermute(G["w_in_p"]).reshape(D_MODEL, N_CHIPS, share), 1, 0)
    out["w_in"] = jnp.pad(d_w_in.astype(BF16), ((0, 0), (0, 0), (0, SHARE - share)))
    return out


def _layer_small_grads(G, W):
    out = {n: G[n] for n in ("lru_conv_w", "ssd_conv_w")}
    for n in ("ln1_g", "ln1_b", "ln2_g", "ln2_b", "ln3_g", "ln3_b", "pe_gate_b", "lru_conv_b", "lru_ba", "lru_bx",
              "lru_lambda", "ssd_conv_b", "ssd_norm_g"):
        out[n] = G[n].reshape(-1)
    out["lru_wa"] = _block_diag_extract(G["lru_wa_bd"])
    out["lru_wx"] = _block_diag_extract(G["lru_wx_bd"])
    out["fox_bf"] = G["fox_bf_vec"][0, LANE_F:LANE_F + ATT_HEADS]
    out["ssd_dt_bias"] = G["ssd_dtb_vec"][0, LANE_DT:LANE_DT + SSD_HEADS]
    out["ssd_a_log"] = G["ssd_a_vec"][0, LANE_DT:LANE_DT + SSD_HEADS] * W["ssd_a_vec"][0, LANE_DT:LANE_DT + SSD_HEADS]
    out["ssd_d"] = G["ssd_d_exp"].reshape(SSD_HEADS, HEAD_DIM).sum(axis=1)
    return out


def _local_step(x, p, target, Ws):
    saves = []
    xb = x.astype(BF16)
    for li in range(DEPTH):
        x, xb, sv = _layer_fwd(li, x, xb, p[li].astype(BF16), Ws[li])
        saves.append(sv)
    dx, loss = _loss_kernel(x, target, name="loss")
    big = [None] * DEPTH
    small = [None] * DEPTH
    for li in reversed(range(DEPTH)):
        dx, G = _layer_bwd(li, dx, saves[li], Ws[li])
        big[li] = _layer_big_grads_by_chip(G)
        small[li] = _layer_small_grads(G, Ws[li])
    stacked = {n: jnp.stack([small[li][n] for li in range(DEPTH)]) for n in small[0]}
    return loss, dx, big, stacked


WEIGHTS = ['ln1_g', 'ln1_b', 'ffn1_wg', 'ffn1_wu', 'ffn1_wd', 'w_in', 'lru_conv_w', 'lru_conv_b', 'lru_wa', 'lru_ba',
           'lru_wx', 'lru_bx', 'lru_lambda', 'fox_bf', 'ssd_conv_w', 'ssd_conv_b', 'ssd_dt_bias', 'ssd_a_log', 'ssd_d',
           'ssd_norm_g', 'w_out', 'ln2_g', 'ln2_b', 'ffn2_wg', 'ffn2_wu', 'ffn2_wd', 'ln3_g', 'ln3_b', 'pe_proj',
           'pe_gate_w', 'pe_gate_b']
CLASSES = (("ffn1_wg", "ffn1_wu", "ffn2_wg", "ffn2_wu", "w_in"),
           ("ffn1_wd", "ffn2_wd"),
           ("w_out", "pe_gate_w"),
           ("pe_proj",))
CLASS_PAD_AXIS = (1, 0, None, None)
BIG = {n: ci for ci, names in enumerate(CLASSES) for n in names}
SMALL_SHARDED = {'lru_conv_w': 2, 'ssd_conv_w': 2}
PACK_COLS = 1024


def _unshard(seg, axis):
    moved = jnp.moveaxis(seg, 0, axis)
    shp = list(moved.shape)
    shp[axis:axis + 2] = [shp[axis] * shp[axis + 1]]
    return moved.reshape(shp)


def _pad_axis(a, axis, size):
    if axis is None or a.shape[axis] == size:
        return a
    pads = [(0, 0)] * a.ndim
    pads[axis] = (0, size - a.shape[axis])
    return jnp.pad(a, pads)


def _pack(arrs, dtype, cols):
    flat = jnp.concatenate([a.astype(dtype).reshape(-1) for a in arrs])
    pad = (-flat.shape[0]) % cols
    if pad:
        flat = jnp.concatenate([flat, jnp.zeros((pad,), dtype)])
    return flat.reshape(-1, cols)


def _unpack(flat, shapes):
    out, off = [], 0
    for s in shapes:
        n = math.prod(s)
        out.append(flat[off:off + n].reshape(s))
        off += n
    return out


def kernel(x, p, ln1_g, ln1_b, ffn1_wg, ffn1_wu, ffn1_wd, w_in, lru_conv_w, lru_conv_b, lru_wa, lru_ba, lru_wx, lru_bx, lru_lambda, fox_bf, ssd_conv_w, ssd_conv_b, ssd_dt_bias, ssd_a_log, ssd_d, ssd_norm_g, w_out, ln2_g, ln2_b, ffn2_wg, ffn2_wu, ffn2_wd, ln3_g, ln3_b, pe_proj, pe_gate_w, pe_gate_b, loss_target, m_ln1_g, m_ln1_b, m_ffn1_wg, m_ffn1_wu, m_ffn1_wd, m_w_in, m_lru_conv_w, m_lru_conv_b, m_lru_wa, m_lru_ba, m_lru_wx, m_lru_bx, m_lru_lambda, m_fox_bf, m_ssd_conv_w, m_ssd_conv_b, m_ssd_dt_bias, m_ssd_a_log, m_ssd_d, m_ssd_norm_g, m_w_out, m_ln2_g, m_ln2_b, m_ffn2_wg, m_ffn2_wu, m_ffn2_wd, m_ln3_g, m_ln3_b, m_pe_proj, m_pe_gate_w, m_pe_gate_b, v_ln1_g, v_ln1_b, v_ffn1_wg, v_ffn1_wu, v_ffn1_wd, v_w_in, v_lru_conv_w, v_lru_conv_b, v_lru_wa, v_lru_ba, v_lru_wx, v_lru_bx, v_lru_lambda, v_fox_bf, v_ssd_conv_w, v_ssd_conv_b, v_ssd_dt_bias, v_ssd_a_log, v_ssd_d, v_ssd_norm_g, v_w_out, v_ln2_g, v_ln2_b, v_ffn2_wg, v_ffn2_wu, v_ffn2_wd, v_ln3_g, v_ln3_b, v_pe_proj, v_pe_gate_w, v_pe_gate_b):
    args = locals()
    w_loc = {n: args[n] for n in WEIGHTS}
    m_loc = {n: args["m_" + n] for n in WEIGHTS}
    v_loc = {n: args["v_" + n] for n in WEIGHTS}
    chip = 2 * lax.axis_index("x") + lax.axis_index("y")
    big = list(BIG)
    small_sh = list(SMALL_SHARDED)
    small_rep = [n for n in WEIGHTS if n not in BIG and n not in SMALL_SHARDED]

    srcs = [jnp.stack([_pad_axis(w_loc[n][li].astype(BF16), pad, SHARE) for li in range(DEPTH) for n in names])
            for names, pad in zip(CLASSES, CLASS_PAD_AXIS)]
    halves = _exchange(srcs, ("x", "y"), mode="gather_half", name="gather_w_chips")
    both = _exchange([h.reshape((-1,) + h.shape[2:]) for h in halves], ("c",), mode="gather",
                     name="gather_w_cores")
    chipw = [{} for _ in range(DEPTH)]
    for names, b in zip(CLASSES, both):
        hn = len(names)
        b = b.reshape((2, N_CHIPS, hn) + b.shape[2:])
        for li in range(DEPTH):
            for j, n in enumerate(names):
                it = li * hn + j
                chipw[li][n] = b[it // hn, :, it % hn]
    small = {n: w_loc[n] for n in small_rep}
    spack = _pack([w_loc[n] for n in small_sh], F32, LANES)
    (sg,) = _exchange([spack[None]], ("x", "y"), mode="gather", name="gather_conv_w")
    for n, seg in zip(small_sh, _unpack_rows(sg.reshape(N_CHIPS, -1), [w_loc[n].shape for n in small_sh])):
        small[n] = _unshard(seg, SMALL_SHARDED[n])
    Ws = [_layer_weights(li, chipw[li], small) for li in range(DEPTH)]

    loss, grad_x, g_big, g_small = _local_step(x[0], p[:, 0], loss_target[0], Ws)
    loss = lax.psum(loss[0, 0], MESH_AXES)

    gcls = [jnp.stack([g_big[li][n] for li in range(DEPTH) for n in names]) for names in CLASSES]
    pair = _exchange([g.reshape((2, -1) + g.shape[1:]) for g in gcls], ("c",), mode="a2a", name="reduce_cores")
    s2 = [_sum_slots(pr.reshape(2, -1, pr.shape[-1]), BF16, name=f"reduce_cores_sum{ci}").reshape(pr.shape[1:])
          for ci, pr in enumerate(pair)]
    quad = _exchange(s2, ("x", "y"), mode="a2a_inner", name="reduce_chips")
    red = [_sum_slots(q.reshape(N_CHIPS, -1, q.shape[-1]), F32, name=f"reduce_chips_sum{ci}").reshape(q.shape[1:])
           for ci, q in enumerate(quad)]
    shared = _exchange(red, ("c",), mode="gather", name="reduce_share")
    g_red = {}
    for names, sh in zip(CLASSES, shared):
        sh = sh.reshape((-1,) + sh.shape[2:])
        for j, n in enumerate(names):
            g = jnp.stack([sh[li * len(names) + j] for li in range(DEPTH)])
            g_red[n] = g[tuple(slice(0, s) for s in w_loc[n].shape)]
    small_all = small_rep + small_sh
    sgp = _pack([g_small[n] for n in small_all], F32, PACK_COLS)
    (sall,) = _exchange([sgp[None]], MESH_AXES, mode="gather", name="reduce_small")
    sred = _sum_slots(sall.reshape((2 ** len(MESH_AXES),) + sgp.shape), F32, name="reduce_small_sum").reshape(-1)
    for n, g in zip(small_all, _unpack(sred, [g_small[n].shape for n in small_all])):
        if n in SMALL_SHARDED:
            width = w_loc[n].shape[-1]
            g = lax.dynamic_slice_in_dim(g, chip * width, width, axis=SMALL_SHARDED[n])
        g_red[n] = g

    delta, new_m, new_v = {}, {}, {}
    for n in big:
        delta[n], new_m[n], new_v[n] = _adamw(w_loc[n], g_red[n], m_loc[n], v_loc[n], name="adamw_" + n)
    shapes = [w_loc[n].shape for n in small_all]
    packs = [_pack([d[n] for n in small_all], F32, LANES) for d in (w_loc, g_red, m_loc, v_loc)]
    outs = _adamw(*packs, name="adamw_small")
    for d, o in zip((delta, new_m, new_v), outs):
        for n, a in zip(small_all, _unpack(o.reshape(-1), shapes)):
            d[n] = a
    return (loss, grad_x[None], *[g_red[n] for n in WEIGHTS], *[delta[n] for n in WEIGHTS],
            *[new_m[n] for n in WEIGHTS], *[new_v[n] for n in WEIGHTS])


def _unpack_rows(gathered, shapes):
    out, off = [], 0
    for s in shapes:
        n = math.prod(s)
        out.append(gathered[:, off:off + n].reshape((N_CHIPS,) + tuple(s)))
        off += n
    return out
```

```python
import functools
import math

import jax
import jax.numpy as jnp
from jax import lax
from jax.experimental import pallas as pl
from jax.experimental.pallas import tpu as pltpu

F32 = jnp.float32
BF16 = jnp.bfloat16

D_MODEL = 1024
DEPTH = 2
PLE_DIM = 256
HEAD_DIM = 64
LRU_WIDTH = 256
LRU_HEADS = 4
LRU_C = 8.0
CONV_K = 4
ATT_WIDTH = 256
ATT_HEADS = 4
SSD_WIDTH = 512
SSD_HEADS = 8
SSD_GROUPS = 2
SSD_STATE = 128
SSD_CHUNK = 128
SSD_CONV_DIM = 1024
FFN_DIM = 2816
ALPHA = (2.0 * DEPTH) ** 0.25
LN_EPS = 1e-5
RMS_EPS = 1e-5
IN_WIDTH = 2828
ADAM_LR = 0.001
ADAM_B1 = 0.9
ADAM_B2 = 0.999
ADAM_EPS = 1e-08
ADAM_WD = 0.01
ADAM_STEP = 10

H_WIDTH = 3072
COL_XBC, COL_Z, COL_U, COL_G, COL_Q, COL_K, COL_V, COL_SMALL = 0, 1024, 1536, 1792, 2048, 2304, 2560, 2816
LANE_F = 0
LANE_DT = 4
LANES = 128
SUBLANES = 8
NEG = -1e30

VMEM_LIMIT = 48 * 1024 * 1024

N_CHIPS = 4
MESH_AXES = ("x", "y", "c")
SHARE = 768


def _params(n):
    return pltpu.CompilerParams(dimension_semantics=("arbitrary",) * n, vmem_limit_bytes=VMEM_LIMIT)


def _pick(n, cands):
    for c in cands:
        if n % c == 0:
            return c
    return n


def _iota(shape, dim):
    return lax.broadcasted_iota(jnp.int32, shape, dim)


def _shift_down(x, s, prev8):
    if s == 0:
        return x
    r = pltpu.roll(x, s, 0)
    pr = pltpu.roll(prev8, s, 0)
    head = jnp.where(_iota(pr.shape, 0) < s, pr, r[:SUBLANES])
    return jnp.concatenate([head, r[SUBLANES:]], axis=0)


def _shift_up(x, s, next8):
    if s == 0:
        return x
    n = x.shape[0]
    r = pltpu.roll(x, n - s, 0)
    nr = pltpu.roll(next8, SUBLANES - s, 0)
    tail = jnp.where(_iota(nr.shape, 0) >= SUBLANES - s, nr, r[n - SUBLANES:])
    return jnp.concatenate([r[:n - SUBLANES], tail], axis=0)


def _scan_fwd(a, b):
    n = a.shape[0]
    row = _iota(a.shape, 0)
    d = 1
    while d < n:
        keep = row >= d
        a_s = jnp.where(keep, pltpu.roll(a, d, 0), 1.0)
        b_s = jnp.where(keep, pltpu.roll(b, d, 0), 0.0)
        b = a * b_s + b
        a = a * a_s
        d *= 2
    return a, b


def _scan_bwd(a, b):
    n = a.shape[0]
    row = _iota(a.shape, 0)
    d = 1
    while d < n:
        keep = row < n - d
        a_s = jnp.where(keep, pltpu.roll(a, n - d, 0), 1.0)
        b_s = jnp.where(keep, pltpu.roll(b, n - d, 0), 0.0)
        b = a * b_s + b
        a = a * a_s
        d *= 2
    return a, b


def _cumsum_rows(x, reverse=False):
    n = x.shape[0]
    row = _iota(x.shape, 0)
    d = 1
    while d < n:
        if reverse:
            x = x + jnp.where(row < n - d, pltpu.roll(x, n - d, 0), 0.0)
        else:
            x = x + jnp.where(row >= d, pltpu.roll(x, d, 0), 0.0)
        d *= 2
    return x


def _col(x, lane):
    return jnp.sum(jnp.where(_iota(x.shape, 1) == lane, x, 0.0), axis=1, keepdims=True)


def _row(x, r):
    return jnp.sum(jnp.where(_iota(x.shape, 0) == r, x, 0.0), axis=0, keepdims=True)


def _sigmoid(x):
    return jax.nn.sigmoid(x)


def _softplus(x):
    return jnp.maximum(x, 0.0) + jnp.log(1.0 + jnp.exp(-jnp.abs(x)))


def _gelu_and_grad(x):
    c0 = math.sqrt(2.0 / math.pi)
    inner = c0 * (x + 0.044715 * x * x * x)
    t = jnp.tanh(inner)
    g = 0.5 * x * (1.0 + t)
    dg = 0.5 * (1.0 + t) + 0.5 * x * (1.0 - t * t) * c0 * (1.0 + 3.0 * 0.044715 * x * x)
    return g, dg


def _dot(a, b, ca, cb):
    return lax.dot_general(a, b, (((ca,), (cb,)), ((), ())), preferred_element_type=F32)


def _conv_taps(xr, prev8, w, bias):
    y = bias + w[CONV_K - 1:CONV_K, :] * xr
    for j in range(CONV_K - 1):
        y = y + w[j:j + 1, :] * _shift_down(xr, CONV_K - 1 - j, prev8)
    return y


def _conv_taps_bwd(dy, next8, w, xr):
    dx = None
    dws = []
    for j in range(CONV_K):
        sh = _shift_up(dy, CONV_K - 1 - j, next8)
        term = w[j:j + 1, :] * sh
        dx = term if dx is None else dx + term
        dws.append(jnp.sum(sh * xr, axis=0, keepdims=True))
    return dx, jnp.concatenate(dws, axis=0)


def _head_expand(v, lane0, nheads, width):
    rows = v.shape[0]
    colhead = _iota((rows, width), 1) // HEAD_DIM
    out = jnp.zeros((rows, width), F32)
    for h in range(nheads):
        out = jnp.where(colhead == h, _col(v, lane0 + h), out)
    return out


def _head_reduce(x, lane0, nheads):
    rows = x.shape[0]
    colhead = _iota(x.shape, 1) // HEAD_DIM
    lane = _iota((rows, LANES), 1)
    out = jnp.zeros((rows, LANES), F32)
    for h in range(nheads):
        s = jnp.sum(jnp.where(colhead == h, x, 0.0), axis=1, keepdims=True)
        out = jnp.where(lane == lane0 + h, s, out)
    return out


def _mm(a, b, *, ta=False, tb=False, scale=1.0, out_dtype=F32, chip_cols=False, name):
    if ta:
        kk, m = a.shape
    else:
        m, kk = a.shape
    n = b.shape[0] if tb else b.shape[1]
    tm = _pick(m, (1024, 512, 256, 128))
    tn = _pick(n // N_CHIPS, (768, 256, 128)) if chip_cols else _pick(n, (1024, 768, 512, 256, 128))
    tk = _pick(kk, (1024, 768, 512, 256, 128))
    nk = kk // tk
    dn_a = 0 if ta else 1
    dn_b = 1 if tb else 0
    if chip_cols:
        per = n // N_CHIPS // tn
        out_spec = pl.BlockSpec((None, tm, tn), lambda i, j, k: (j // per, i, j % per))
        out_shape = jax.ShapeDtypeStruct((N_CHIPS, m, n // N_CHIPS), out_dtype)
    else:
        out_spec = pl.BlockSpec((tm, tn), lambda i, j, k: (i, j))
        out_shape = jax.ShapeDtypeStruct((m, n), out_dtype)

    def body(a_ref, b_ref, o_ref, acc):
        k = pl.program_id(2)

        @pl.when(k == 0)
        def _():
            acc[...] = jnp.zeros_like(acc)

        acc[...] += _dot(a_ref[...].astype(BF16), b_ref[...].astype(BF16), dn_a, dn_b)

        @pl.when(k == nk - 1)
        def _():
            o_ref[...] = (acc[...] * scale).astype(out_dtype)

    a_spec = pl.BlockSpec((tk, tm), lambda i, j, k: (k, i)) if ta else pl.BlockSpec((tm, tk), lambda i, j, k: (i, k))
    b_spec = pl.BlockSpec((tn, tk), lambda i, j, k: (j, k)) if tb else pl.BlockSpec((tk, tn), lambda i, j, k: (k, j))
    return pl.pallas_call(
        body, name=name, grid=(m // tm, n // tn, nk),
        in_specs=[a_spec, b_spec],
        out_specs=out_spec, out_shape=out_shape,
        scratch_shapes=[pltpu.VMEM((tm, tn), F32)],
        compiler_params=_params(3),
    )(a, b)


def _mm_swiglu(xb, wg, wu, *, name):
    t, d = xb.shape
    share = wg.shape[2]
    n = N_CHIPS * share
    tm = _pick(t, (512, 256, 128))
    tn = _pick(share, (768, 256, 128))
    per = share // tn

    def body(x_ref, wg_ref, wu_ref, g_ref, u_ref, a_ref):
        x = x_ref[...]
        g = _dot(x, wg_ref[...], 1, 0)
        u = _dot(x, wu_ref[...], 1, 0)
        g_ref[...] = g.astype(BF16)
        u_ref[...] = u.astype(BF16)
        a_ref[...] = (g * _sigmoid(g) * u).astype(BF16)

    o = jax.ShapeDtypeStruct((t, n), BF16)
    ospec = pl.BlockSpec((tm, tn), lambda i, j: (i, j))
    return pl.pallas_call(
        body, name=name, grid=(t // tm, n // tn),
        in_specs=[pl.BlockSpec((tm, d), lambda i, j: (i, 0)),
                  pl.BlockSpec((None, d, tn), lambda i, j: (j // per, 0, j % per)),
                  pl.BlockSpec((None, d, tn), lambda i, j: (j // per, 0, j % per))],
        out_specs=[ospec, ospec, ospec], out_shape=[o, o, o],
        compiler_params=_params(2),
    )(xb, wg, wu)


def _mm_swiglu_bwd(dr, wd, g, u, *, scale, name):
    t, d = dr.shape
    n = wd.shape[0]
    tm = _pick(t, (512, 256, 128))
    tn = _pick(n, (768, 256, 128))

    def body(dr_ref, wd_ref, g_ref, u_ref, dg_ref, du_ref):
        da = _dot(dr_ref[...].astype(BF16), wd_ref[...], 1, 1) * scale
        gg = g_ref[...].astype(F32)
        uu = u_ref[...].astype(F32)
        sg = _sigmoid(gg)
        dg_ref[...] = (da * uu * (sg * (1.0 + gg * (1.0 - sg)))).astype(BF16)
        du_ref[...] = (da * gg * sg).astype(BF16)

    o = jax.ShapeDtypeStruct((t, n), BF16)
    ospec = pl.BlockSpec((tm, tn), lambda i, j: (i, j))
    return pl.pallas_call(
        body, name=name, grid=(t // tm, n // tn),
        in_specs=[pl.BlockSpec((tm, d), lambda i, j: (i, 0)),
                  pl.BlockSpec((tn, d), lambda i, j: (j, 0)),
                  ospec, ospec],
        out_specs=[ospec, ospec], out_shape=[o, o],
        compiler_params=_params(2),
    )(dr, wd, g, u)


def _mm_ln(a, w, resid, gain, bias, *, rscale, mscale, name):
    t, kk = a.shape
    d = w.shape[1]
    tm = _pick(t, (512, 256, 128))
    tk = _pick(kk, (1024, 1408, 512, 256, 128))
    nk = kk // tk

    def body(a_ref, w_ref, r_ref, g_ref, b_ref, y_ref, yb_ref, xh_ref, rs_ref, acc):
        k = pl.program_id(1)

        @pl.when(k == 0)
        def _():
            acc[...] = jnp.zeros_like(acc)

        acc[...] += _dot(a_ref[...].astype(BF16), w_ref[...], 1, 0)

        @pl.when(k == nk - 1)
        def _():
            r = rscale * r_ref[...] + mscale * acc[...]
            mu = jnp.mean(r, axis=1, keepdims=True)
            xc = r - mu
            var = jnp.mean(xc * xc, axis=1, keepdims=True)
            rstd = lax.rsqrt(var + LN_EPS)
            xh = xc * rstd
            y = xh * g_ref[...] + b_ref[...]
            y_ref[...] = y
            yb_ref[...] = y.astype(BF16)
            xh_ref[...] = xh
            rs_ref[...] = rstd

    row = pl.BlockSpec((tm, d), lambda i, k: (i, 0))
    vec = pl.BlockSpec((1, d), lambda i, k: (0, 0))
    return pl.pallas_call(
        body, name=name, grid=(t // tm, nk),
        in_specs=[pl.BlockSpec((tm, tk), lambda i, k: (i, k)),
                  pl.BlockSpec((tk, d), lambda i, k: (k, 0)), row, vec, vec],
        out_specs=[row, row, row, pl.BlockSpec((tm, 1), lambda i, k: (i, 0))],
        out_shape=[jax.ShapeDtypeStruct((t, d), F32), jax.ShapeDtypeStruct((t, d), BF16),
                   jax.ShapeDtypeStruct((t, d), F32), jax.ShapeDtypeStruct((t, 1), F32)],
        scratch_shapes=[pltpu.VMEM((tm, d), F32)],
        compiler_params=_params(2),
    )(a, w, resid, gain.reshape(1, d), bias.reshape(1, d))


def _bwd_proj(pairs, resid, *, rscale, ln, name):
    t, kk = pairs[0][0].shape
    d = pairs[0][1].shape[-2]
    tm = _pick(t, (512, 256, 128))
    tk = _pick(pairs[0][1].shape[-1], (1024, 768, 512, 256, 128))
    nk = kk // tk
    nt = t // tm
    npair = len(pairs)
    has_ln = ln is not None

    def body(*refs):
        ab = refs[:2 * npair]
        r_ref = refs[2 * npair]
        pos = 2 * npair + 1
        if has_ln:
            xh_ref, rs_ref, g_ref = refs[pos:pos + 3]
            pos += 3
            o_ref, dg_ref, db_ref = refs[pos:pos + 3]
            pos += 3
        else:
            o_ref = refs[pos]
            pos += 1
        acc = refs[pos]
        i = pl.program_id(0)
        k = pl.program_id(1)

        @pl.when(k == 0)
        def _():
            acc[...] = jnp.zeros_like(acc)

        for q in range(npair):
            acc[...] += _dot(ab[2 * q][...].astype(BF16), ab[2 * q + 1][...], 1, 1)

        @pl.when(k == nk - 1)
        def _():
            dy = rscale * r_ref[...] + acc[...]
            if not has_ln:
                o_ref[...] = dy
                return
            xh = xh_ref[...]
            w = dy * g_ref[...]
            m1 = jnp.mean(w, axis=1, keepdims=True)
            m2 = jnp.mean(w * xh, axis=1, keepdims=True)
            o_ref[...] = rs_ref[...] * (w - m1 - xh * m2)

            @pl.when(i == 0)
            def _():
                dg_ref[...] = jnp.zeros_like(dg_ref)
                db_ref[...] = jnp.zeros_like(db_ref)

            dg_ref[...] += jnp.sum(dy * xh, axis=0, keepdims=True)
            db_ref[...] += jnp.sum(dy, axis=0, keepdims=True)

    row = pl.BlockSpec((tm, d), lambda i, k: (i, 0))
    vec = pl.BlockSpec((1, d), lambda i, k: (0, 0))
    in_specs, args = [], []
    for a, b in pairs:
        if b.ndim == 3:
            per = b.shape[2] // tk
            b_spec = pl.BlockSpec((None, d, tk), lambda i, k, per=per: (k // per, 0, k % per))
        else:
            b_spec = pl.BlockSpec((d, tk), lambda i, k: (0, k))
        in_specs += [pl.BlockSpec((tm, tk), lambda i, k: (i, k)), b_spec]
        args += [a, b]
    in_specs.append(row)
    args.append(resid)
    out_specs = [row]
    out_shape = [jax.ShapeDtypeStruct((t, d), F32)]
    if has_ln:
        xh, rs, gain = ln
        in_specs += [row, pl.BlockSpec((tm, 1), lambda i, k: (i, 0)), vec]
        args += [xh, rs, gain.reshape(1, d)]
        out_specs += [vec, vec]
        out_shape += [jax.ShapeDtypeStruct((1, d), F32)] * 2
    return pl.pallas_call(
        body, name=name, grid=(nt, nk), in_specs=in_specs, out_specs=out_specs, out_shape=out_shape,
        scratch_shapes=[pltpu.VMEM((tm, d), F32)],
        compiler_params=_params(2),
    )(*args)


def _mm_pe(x3, x3b, pb, wgate, bgate, wproj, *, name):
    t, d = x3.shape
    pd = pb.shape[1]
    tm = _pick(t, (512, 256, 128))
    tn = _pick(d, (512, 256, 128))

    def body(x_ref, xb_ref, p_ref, wg_ref, bg_ref, wp_ref, y_ref, yb_ref, sg_ref, e_ref):
        sg = _sigmoid(_dot(xb_ref[...], wg_ref[...], 1, 0) + bg_ref[...])
        e = _dot(p_ref[...], wp_ref[...], 1, 0)
        y = x_ref[...] + sg * e
        y_ref[...] = y
        yb_ref[...] = y.astype(BF16)
        sg_ref[...] = sg.astype(BF16)
        e_ref[...] = e.astype(BF16)

    ospec = pl.BlockSpec((tm, tn), lambda i, j: (i, j))
    ob = jax.ShapeDtypeStruct((t, d), BF16)
    return pl.pallas_call(
        body, name=name, grid=(t // tm, d // tn),
        in_specs=[ospec, pl.BlockSpec((tm, d), lambda i, j: (i, 0)), pl.BlockSpec((tm, pd), lambda i, j: (i, 0)),
                  pl.BlockSpec((d, tn), lambda i, j: (0, j)), pl.BlockSpec((1, tn), lambda i, j: (0, j)),
                  pl.BlockSpec((pd, tn), lambda i, j: (0, j))],
        out_specs=[ospec, ospec, ospec, ospec],
        out_shape=[jax.ShapeDtypeStruct((t, d), F32), ob, ob, ob],
        compiler_params=_params(2),
    )(x3, x3b, pb, wgate, bgate.reshape(1, d), wproj)


def _pe_bwd_elem(dx4, sg, e, *, name):
    t, d = dx4.shape
    tm = _pick(t, (512, 256, 128))

    def body(dx_ref, sg_ref, e_ref, dgp_ref, de_ref, db_ref):
        dx = dx_ref[...]
        s = sg_ref[...].astype(F32)
        dgp = dx * e_ref[...].astype(F32) * s * (1.0 - s)
        dgp_ref[...] = dgp.astype(BF16)
        de_ref[...] = (dx * s).astype(BF16)

        @pl.when(pl.program_id(0) == 0)
        def _():
            db_ref[...] = jnp.zeros_like(db_ref)

        db_ref[...] += jnp.sum(dgp, axis=0, keepdims=True)

    row = pl.BlockSpec((tm, d), lambda i: (i, 0))
    ob = jax.ShapeDtypeStruct((t, d), BF16)
    return pl.pallas_call(
        body, name=name, grid=(t // tm,), in_specs=[row, row, row],
        out_specs=[row, row, pl.BlockSpec((1, d), lambda i: (0, 0))],
        out_shape=[ob, ob, jax.ShapeDtypeStruct((1, d), F32)],
        compiler_params=_params(1),
    )(dx4, sg, e)


def _loss_kernel(y, target, *, name):
    t, d = y.shape
    tm = _pick(t, (512, 256, 128))

    def body(y_ref, t_ref, dy_ref, l_ref):
        diff = y_ref[...] - t_ref[...]
        dy_ref[...] = diff * (1.0 / d)

        @pl.when(pl.program_id(0) == 0)
        def _():
            l_ref[...] = jnp.zeros_like(l_ref)

        part = jnp.sum(jnp.mean(diff * diff, axis=1, keepdims=True), axis=0, keepdims=True)
        l_ref[...] += 0.5 * part

    row = pl.BlockSpec((tm, d), lambda i: (i, 0))
    return pl.pallas_call(
        body, name=name, grid=(t // tm,), in_specs=[row, row],
        out_specs=[row, pl.BlockSpec((1, 1), lambda i: (0, 0))],
        out_shape=[jax.ShapeDtypeStruct((t, d), F32), jax.ShapeDtypeStruct((1, 1), F32)],
        compiler_params=_params(1),
    )(y, target)


LRU_TM = 256


def _lru_gate_terms(r, lam):
    sp = _softplus(-lam)
    la = -LRU_C * r * sp
    a = jnp.exp(la)
    em = jnp.tanh(la) * (jnp.exp(2.0 * la) + 1.0)
    s = jnp.sqrt(-em)
    return la, a, s, sp


def _lru_fwd(hbuf, conv_w, conv_b, wa, ba, wx, bx, lam, *, name):
    t = hbuf.shape[0]
    w = LRU_WIDTH
    tm = _pick(t, (LRU_TM, 128))
    cu, cg = COL_U // w, COL_G // w
    hb = tm // SUBLANES

    def body(u_ref, up_ref, g_ref, cw_ref, cb_ref, wa_ref, ba_ref, wx_ref, bx_ref, lam_ref,
             y_ref, u_out, r_out, i_out, a_out, h_out, carry):
        i = pl.program_id(0)

        @pl.when(i == 0)
        def _():
            carry[...] = jnp.zeros_like(carry)

        prev = jnp.where(i == 0, 0.0, up_ref[...])
        u = _conv_taps(u_ref[...], prev, cw_ref[...], cb_ref[...])
        ub = u.astype(BF16)
        r = _sigmoid(_dot(ub, wa_ref[...], 1, 0) + ba_ref[...])
        ig = _sigmoid(_dot(ub, wx_ref[...], 1, 0) + bx_ref[...])
        _, a, s, _ = _lru_gate_terms(r, lam_ref[...])
        b = s * (ig * u)
        acum, hs = _scan_fwd(a, b)
        h = hs + acum * carry[0:1, :]
        carry[...] = jnp.broadcast_to(h[tm - 1:tm, :], carry.shape)
        gl, _ = _gelu_and_grad(g_ref[...])
        y_ref[...] = h * gl
        u_out[...] = u
        r_out[...] = r
        i_out[...] = ig
        a_out[...] = a
        h_out[...] = h

    row = pl.BlockSpec((tm, w), lambda i: (i, 0))
    vec = pl.BlockSpec((1, w), lambda i: (0, 0))
    mat = pl.BlockSpec((w, w), lambda i: (0, 0))
    o = jax.ShapeDtypeStruct((t, w), F32)
    return pl.pallas_call(
        body, name=name, grid=(t // tm,),
        in_specs=[pl.BlockSpec((tm, w), lambda i: (i, cu)),
                  pl.BlockSpec((SUBLANES, w), lambda i: (jnp.maximum(i * hb - 1, 0), cu)),
                  pl.BlockSpec((tm, w), lambda i: (i, cg)),
                  pl.BlockSpec((CONV_K, w), lambda i: (0, 0)), vec, mat, vec, mat, vec, vec],
        out_specs=[row] * 6, out_shape=[o] * 6,
        scratch_shapes=[pltpu.VMEM((SUBLANES, w), F32)],
        compiler_params=_params(1),
    )(hbuf, hbuf, hbuf, conv_w, conv_b, wa, ba, wx, bx, lam)


def _lru_bwd(dymix, hbuf, u, r, ig, a, h, conv_w, wa, wx, lam, *, name):
    t = hbuf.shape[0]
    w = LRU_WIDTH
    tm = _pick(t, (LRU_TM, 128))
    nb = t // tm
    cu, cg = COL_U // w, COL_G // w
    hb = tm // SUBLANES
    last8 = t // SUBLANES - 1

    def body(dy_ref, ur_ref, g_ref, u_ref, r_ref, i_ref, a_ref, an_ref, h_ref, hp_ref,
             cw_ref, wa_ref, wx_ref, lam_ref,
             dur_ref, dgr_ref, dcw_ref, dcb_ref, dwa_ref, dba_ref, dwx_ref, dbx_ref, dlam_ref,
             lcarry, dnext):
        i = pl.program_id(0)
        ib = nb - 1 - i

        @pl.when(i == 0)
        def _():
            lcarry[...] = jnp.zeros_like(lcarry)
            dnext[...] = jnp.zeros_like(dnext)
            for ref in (dcw_ref, dcb_ref, dwa_ref, dba_ref, dwx_ref, dbx_ref, dlam_ref):
                ref[...] = jnp.zeros_like(ref)

        dy = dy_ref[...]
        hh = h_ref[...]
        av = a_ref[...]
        uu = u_ref[...]
        rr = r_ref[...]
        ii = i_ref[...]
        lam_v = lam_ref[...]
        gl, dgl = _gelu_and_grad(g_ref[...])
        dgr_ref[...] = dy * hh * dgl
        dh_out = dy * gl
        a_next = _shift_up(av, 1, jnp.where(ib == nb - 1, 0.0, an_ref[...]))
        acum, ls = _scan_bwd(a_next, dh_out)
        lam_adj = ls + acum * lcarry[0:1, :]
        lcarry[...] = jnp.broadcast_to(lam_adj[0:1, :], lcarry.shape)
        h_prev = _shift_down(hh, 1, jnp.where(ib == 0, 0.0, hp_ref[...]))
        da = lam_adj * h_prev
        _, a2, s, sp = _lru_gate_terms(rr, lam_v)
        d_igu = lam_adj * s
        ds = lam_adj * ii * uu
        dla = da * a2 - ds * (a2 * a2) / s
        dr = dla * (-LRU_C * sp)
        dlam_ref[...] += jnp.sum(dla * (LRU_C * rr * _sigmoid(-lam_v)), axis=0, keepdims=True)
        dpre_r = dr * rr * (1.0 - rr)
        dpre_i = d_igu * uu * ii * (1.0 - ii)
        prb = dpre_r.astype(BF16)
        pib = dpre_i.astype(BF16)
        ub = uu.astype(BF16)
        du = d_igu * ii + _dot(prb, wa_ref[...], 1, 1) + _dot(pib, wx_ref[...], 1, 1)
        dwa_ref[...] += _dot(ub, prb, 0, 0)
        dwx_ref[...] += _dot(ub, pib, 0, 0)
        dba_ref[...] += jnp.sum(dpre_r, axis=0, keepdims=True)
        dbx_ref[...] += jnp.sum(dpre_i, axis=0, keepdims=True)
        dur, dws = _conv_taps_bwd(du, dnext[...], cw_ref[...], ur_ref[...])
        dur_ref[...] = dur
        dcw_ref[...] += dws
        dcb_ref[...] += jnp.sum(du, axis=0, keepdims=True)
        dnext[...] = du[:SUBLANES]

    def rowspec(col):
        return pl.BlockSpec((tm, w), lambda i: (nb - 1 - i, col))

    row = rowspec(0)
    nxt = pl.BlockSpec((SUBLANES, w), lambda i: (jnp.minimum((nb - i) * hb, last8), 0))
    prv = pl.BlockSpec((SUBLANES, w), lambda i: (jnp.maximum((nb - 1 - i) * hb - 1, 0), 0))
    vec = pl.BlockSpec((1, w), lambda i: (0, 0))
    mat = pl.BlockSpec((w, w), lambda i: (0, 0))
    cw = pl.BlockSpec((CONV_K, w), lambda i: (0, 0))
    o = jax.ShapeDtypeStruct((t, w), F32)
    v1 = jax.ShapeDtypeStruct((1, w), F32)
    m1 = jax.ShapeDtypeStruct((w, w), F32)
    return pl.pallas_call(
        body, name=name, grid=(nb,),
        in_specs=[rowspec(0), rowspec(cu), rowspec(cg), row, row, row, row, nxt, row, prv, cw, mat, mat, vec],
        out_specs=[row, row, cw, vec, mat, vec, mat, vec, vec],
        out_shape=[o, o, jax.ShapeDtypeStruct((CONV_K, w), F32), v1, m1, v1, m1, v1, v1],
        scratch_shapes=[pltpu.VMEM((SUBLANES, w), F32), pltpu.VMEM((SUBLANES, w), F32)],
        compiler_params=_params(1),
    )(dymix, hbuf, hbuf, u, r, ig, a, a, h, h, conv_w, wa, wx, lam)


FOX_T = 512
FOX_PREP_TM = 256


def _log_sigmoid(x):
    return jnp.minimum(x, 0.0) - jnp.log(1.0 + jnp.exp(-jnp.abs(x)))


def _fox_prep(hbuf, bf_vec, *, name):
    t = hbuf.shape[0]
    tm = _pick(t, (FOX_PREP_TM, 128))
    cs = COL_SMALL // LANES

    def body(s_ref, b_ref, eq_ref, ek_ref, carry):
        i = pl.program_id(0)

        @pl.when(i == 0)
        def _():
            carry[...] = jnp.zeros_like(carry)

        lf = _log_sigmoid(s_ref[...] + b_ref[...])
        f = _cumsum_rows(lf) + carry[0:1, :]
        carry[...] = jnp.broadcast_to(f[tm - 1:tm, :], carry.shape)
        lane = _iota((tm, LANES), 1)
        for h in range(ATT_HEADS):
            base = HEAD_DIM * (1 - h % 2)
            fh = _col(f, h)
            hi = fh.astype(BF16).astype(F32)
            mid = (fh - hi).astype(BF16).astype(F32)
            lo = fh - hi - mid
            terms = jnp.where(lane == base, hi, jnp.where(lane == base + 1, mid, jnp.where(lane == base + 2, lo, 0.0)))
            terms_k = jnp.where(lane == base + 3, -hi,
                                jnp.where(lane == base + 4, -mid, jnp.where(lane == base + 5, -lo, 0.0)))
            ones_q = ((lane >= base + 3) & (lane < base + 6)).astype(F32)
            ones_k = ((lane >= base) & (lane < base + 3)).astype(F32)
            eq_ref[:, LANES * h:LANES * (h + 1)] = (terms + ones_q).astype(BF16)
            ek_ref[:, LANES * h:LANES * (h + 1)] = (terms_k + ones_k).astype(BF16)

    ospec = pl.BlockSpec((tm, ATT_HEADS * LANES), lambda i: (i, 0))
    o = jax.ShapeDtypeStruct((t, ATT_HEADS * LANES), BF16)
    return pl.pallas_call(
        body, name=name, grid=(t // tm,),
        in_specs=[pl.BlockSpec((tm, LANES), lambda i: (i, cs)), pl.BlockSpec((1, LANES), lambda i: (0, 0))],
        out_specs=[ospec, ospec], out_shape=[o, o],
        scratch_shapes=[pltpu.VMEM((SUBLANES, LANES), F32)],
        compiler_params=_params(1),
    )(hbuf, bf_vec)


def _fox_post(dfc, hbuf, bf_vec, *, name):
    t = hbuf.shape[0]
    tm = _pick(t, (FOX_PREP_TM, 128))
    nb = t // tm
    cs = COL_SMALL // LANES

    def body(df_ref, s_ref, b_ref, o_ref, db_ref, carry):
        i = pl.program_id(0)

        @pl.when(i == 0)
        def _():
            carry[...] = jnp.zeros_like(carry)
            db_ref[...] = jnp.zeros_like(db_ref)

        dlf = _cumsum_rows(df_ref[...], reverse=True) + carry[0:1, :]
        carry[...] = jnp.broadcast_to(dlf[0:1, :], carry.shape)
        dl = dlf * _sigmoid(-(s_ref[...] + b_ref[...]))
        dl = jnp.where(_iota(dl.shape, 1) < ATT_HEADS, dl, 0.0)
        o_ref[...] = dl
        db_ref[...] += jnp.sum(dl, axis=0, keepdims=True)

    vec = pl.BlockSpec((1, LANES), lambda i: (0, 0))
    return pl.pallas_call(
        body, name=name, grid=(nb,),
        in_specs=[pl.BlockSpec((tm, LANES), lambda i: (nb - 1 - i, 0)),
                  pl.BlockSpec((tm, LANES), lambda i: (nb - 1 - i, cs)), vec],
        out_specs=[pl.BlockSpec((tm, LANES), lambda i: (nb - 1 - i, 0)), vec],
        out_shape=[jax.ShapeDtypeStruct((t, LANES), F32), jax.ShapeDtypeStruct((1, LANES), F32)],
        scratch_shapes=[pltpu.VMEM((SUBLANES, LANES), F32)],
        compiler_params=_params(1),
    )(dfc, hbuf, bf_vec)


def _fox_scores(qp, kp, eq, ek, hm, causal):
    qm = jnp.where(hm, (qp * (HEAD_DIM ** -0.5)).astype(BF16), eq)
    km = jnp.where(hm, kp.astype(BF16), ek)
    s = _dot(qm, km, 1, 1)
    if causal is not None:
        s = jnp.where(causal, s, NEG)
    return s, qm, km


def _fox_masks(i, j, tq):
    row = i * tq + _iota((tq, tq), 0)
    col = j * tq + _iota((tq, tq), 1)
    lane = _iota((1, LANES), 1)
    return col <= row, (lane < HEAD_DIM, lane >= HEAD_DIM)


def _fox_fwd(hbuf, eq, ek, *, name):
    t = hbuf.shape[0]
    w = ATT_WIDTH
    tq = _pick(t, (FOX_T, 256, 128))
    nq = t // tq
    cq, ck, cv = COL_Q // w, COL_K // w, COL_V // w

    def body(q_ref, k_ref, v_ref, eq_ref, ek_ref, o_ref, lse_ref, m_s, l_s, acc_s):
        i = pl.program_id(0)
        j = pl.program_id(1)

        @pl.when(j == 0)
        def _():
            m_s[...] = jnp.full_like(m_s, NEG)
            l_s[...] = jnp.zeros_like(l_s)
            acc_s[...] = jnp.zeros_like(acc_s)

        def step(diagonal):
            causal, hms = _fox_masks(i, j, tq)
            for pr in range(2):
                sl = slice(LANES * pr, LANES * (pr + 1))
                qp = q_ref[:, sl]
                kp = k_ref[:, sl]
                vpb = v_ref[:, sl].astype(BF16)
                for hh in range(2):
                    h = 2 * pr + hh
                    hsl = slice(LANES * h, LANES * (h + 1))
                    s, _, _ = _fox_scores(qp, kp, eq_ref[:, hsl], ek_ref[:, hsl], hms[hh],
                                          causal if diagonal else None)
                    m_prev = m_s[h]
                    m_new = jnp.maximum(m_prev, jnp.max(s, axis=1, keepdims=True))
                    alpha = jnp.exp(m_prev - m_new)
                    p = jnp.exp(s - m_new)
                    l_s[h] = alpha * l_s[h] + jnp.sum(p, axis=1, keepdims=True)
                    m_s[h] = m_new
                    pv = _dot(p.astype(BF16), vpb, 1, 0)
                    acc = acc_s[:, sl]
                    acc_s[:, sl] = jnp.where(hms[hh], alpha * acc + pv, acc)

        @pl.when(j < i)
        def _():
            step(False)

        @pl.when(j == i)
        def _():
            step(True)
            _, hms = _fox_masks(i, j, tq)
            for pr in range(2):
                sl = slice(LANES * pr, LANES * (pr + 1))
                acc = acc_s[:, sl]
                o_ref[:, sl] = jnp.where(hms[0], acc / l_s[2 * pr], acc / l_s[2 * pr + 1])
                for hh in range(2):
                    h = 2 * pr + hh
                    lse_ref[h] = m_s[h] + jnp.log(l_s[h])

    return pl.pallas_call(
        body, name=name, grid=(nq, nq),
        in_specs=[pl.BlockSpec((tq, w), lambda i, j: (i, cq)),
                  pl.BlockSpec((tq, w), lambda i, j: (jnp.minimum(j, i), ck)),
                  pl.BlockSpec((tq, w), lambda i, j: (jnp.minimum(j, i), cv)),
                  pl.BlockSpec((tq, ATT_HEADS * LANES), lambda i, j: (i, 0)),
                  pl.BlockSpec((tq, ATT_HEADS * LANES), lambda i, j: (jnp.minimum(j, i), 0))],
        out_specs=[pl.BlockSpec((tq, w), lambda i, j: (i, 0)),
                   pl.BlockSpec((ATT_HEADS, tq, 1), lambda i, j: (0, i, 0))],
        out_shape=[jax.ShapeDtypeStruct((t, w), F32), jax.ShapeDtypeStruct((ATT_HEADS, t, 1), F32)],
        scratch_shapes=[pltpu.VMEM((ATT_HEADS, tq, 1), F32), pltpu.VMEM((ATT_HEADS, tq, 1), F32),
                        pltpu.VMEM((tq, w), F32)],
        compiler_params=_params(2),
    )(hbuf, hbuf, hbuf, eq, ek)


def _fox_bwd_kv(hbuf, eq, ek, dymix, o, lse, *, name):
    t = hbuf.shape[0]
    w = ATT_WIDTH
    tq = _pick(t, (FOX_T, 256, 128))
    nq = t // tq
    cq, ck, cv = COL_Q // w, COL_K // w, COL_V // w
    cdo = ATT_WIDTH // w

    def body(q_ref, k_ref, v_ref, eq_ref, ek_ref, do_ref, o_ref, lse_ref, dk_ref, dv_ref, dfk_ref,
             dk_s, dv_s, dfk_s):
        j = pl.program_id(0)
        i = pl.program_id(1)

        @pl.when(i == 0)
        def _():
            dk_s[...] = jnp.zeros_like(dk_s)
            dv_s[...] = jnp.zeros_like(dv_s)
            dfk_s[...] = jnp.zeros_like(dfk_s)

        def step(diagonal):
            causal, hms = _fox_masks(i, j, tq)
            rows8 = _iota((SUBLANES, tq), 0)
            for pr in range(2):
                sl = slice(LANES * pr, LANES * (pr + 1))
                qp = q_ref[:, sl]
                kp = k_ref[:, sl]
                vpb = v_ref[:, sl].astype(BF16)
                dop = do_ref[:, sl]
                op = o_ref[:, sl]
                for hh in range(2):
                    h = 2 * pr + hh
                    hsl = slice(LANES * h, LANES * (h + 1))
                    s, qm, _ = _fox_scores(qp, kp, eq_ref[:, hsl], ek_ref[:, hsl], hms[hh],
                                           causal if diagonal else None)
                    p = jnp.exp(s - lse_ref[h])
                    dom = jnp.where(hms[hh], dop, 0.0)
                    domb = dom.astype(BF16)
                    dv_s[:, sl] += _dot(p.astype(BF16), domb, 0, 0)
                    dp = _dot(domb, vpb, 1, 1)
                    delta = jnp.sum(dom * op, axis=1, keepdims=True)
                    ds = p * (dp - delta)
                    dk_s[:, sl] += jnp.where(hms[hh], _dot(ds.astype(BF16), qm, 0, 0), 0.0)
                    dfk_s[...] += jnp.where(rows8 == h, jnp.sum(ds, axis=0, keepdims=True), 0.0)

        @pl.when(i > j)
        def _():
            step(False)

        @pl.when(i == j)
        def _():
            step(True)

        @pl.when(i == nq - 1)
        def _():
            dk_ref[...] = dk_s[...]
            dv_ref[...] = dv_s[...]
            dfk_ref[...] = dfk_s[...]

    qi = lambda j, i: jnp.maximum(i, j)
    return pl.pallas_call(
        body, name=name, grid=(nq, nq),
        in_specs=[pl.BlockSpec((tq, w), lambda j, i: (qi(j, i), cq)),
                  pl.BlockSpec((tq, w), lambda j, i: (j, ck)),
                  pl.BlockSpec((tq, w), lambda j, i: (j, cv)),
                  pl.BlockSpec((tq, ATT_HEADS * LANES), lambda j, i: (qi(j, i), 0)),
                  pl.BlockSpec((tq, ATT_HEADS * LANES), lambda j, i: (j, 0)),
                  pl.BlockSpec((tq, w), lambda j, i: (qi(j, i), cdo)),
                  pl.BlockSpec((tq, w), lambda j, i: (qi(j, i), 0)),
                  pl.BlockSpec((ATT_HEADS, tq, 1), lambda j, i: (0, qi(j, i), 0))],
        out_specs=[pl.BlockSpec((tq, w), lambda j, i: (j, 0)), pl.BlockSpec((tq, w), lambda j, i: (j, 0)),
                   pl.BlockSpec((SUBLANES, tq), lambda j, i: (0, j))],
        out_shape=[jax.ShapeDtypeStruct((t, w), F32), jax.ShapeDtypeStruct((t, w), F32),
                   jax.ShapeDtypeStruct((SUBLANES, t), F32)],
        scratch_shapes=[pltpu.VMEM((tq, w), F32), pltpu.VMEM((tq, w), F32), pltpu.VMEM((SUBLANES, tq), F32)],
        compiler_params=_params(2),
    )(hbuf, hbuf, hbuf, eq, ek, dymix, o, lse)


def _fox_bwd_q(hbuf, eq, ek, dymix, o, lse, *, name):
    t = hbuf.shape[0]
    w = ATT_WIDTH
    tq = _pick(t, (FOX_T, 256, 128))
    nq = t // tq
    cq, ck, cv = COL_Q // w, COL_K // w, COL_V // w
    cdo = ATT_WIDTH // w

    def body(q_ref, k_ref, v_ref, eq_ref, ek_ref, do_ref, o_ref, lse_ref, dq_ref, dfq_ref, dq_s, dfq_s):
        i = pl.program_id(0)
        j = pl.program_id(1)

        @pl.when(j == 0)
        def _():
            dq_s[...] = jnp.zeros_like(dq_s)
            dfq_s[...] = jnp.zeros_like(dfq_s)

        def step(diagonal):
            causal, hms = _fox_masks(i, j, tq)
            for pr in range(2):
                sl = slice(LANES * pr, LANES * (pr + 1))
                qp = q_ref[:, sl]
                kp = k_ref[:, sl]
                vpb = v_ref[:, sl].astype(BF16)
                dop = do_ref[:, sl]
                op = o_ref[:, sl]
                for hh in range(2):
                    h = 2 * pr + hh
                    hsl = slice(LANES * h, LANES * (h + 1))
                    s, _, km = _fox_scores(qp, kp, eq_ref[:, hsl], ek_ref[:, hsl], hms[hh],
                                           causal if diagonal else None)
                    p = jnp.exp(s - lse_ref[h])
                    dom = jnp.where(hms[hh], dop, 0.0)
                    dp = _dot(dom.astype(BF16), vpb, 1, 1)
                    delta = jnp.sum(dom * op, axis=1, keepdims=True)
                    ds = p * (dp - delta)
                    dq = _dot(ds.astype(BF16), km, 1, 0) * (HEAD_DIM ** -0.5)
                    dq_s[:, sl] += jnp.where(hms[hh], dq, 0.0)
                    dfq_s[h] += jnp.sum(ds, axis=1, keepdims=True)

        @pl.when(j < i)
        def _():
            step(False)

        @pl.when(j == i)
        def _():
            step(True)
            dq_ref[...] = dq_s[...]
            dfq_ref[...] = dfq_s[...]

    kj = lambda i, j: jnp.minimum(j, i)
    return pl.pallas_call(
        body, name=name, grid=(nq, nq),
        in_specs=[pl.BlockSpec((tq, w), lambda i, j: (i, cq)),
                  pl.BlockSpec((tq, w), lambda i, j: (kj(i, j), ck)),
                  pl.BlockSpec((tq, w), lambda i, j: (kj(i, j), cv)),
                  pl.BlockSpec((tq, ATT_HEADS * LANES), lambda i, j: (i, 0)),
                  pl.BlockSpec((tq, ATT_HEADS * LANES), lambda i, j: (kj(i, j), 0)),
                  pl.BlockSpec((tq, w), lambda i, j: (i, cdo)),
                  pl.BlockSpec((tq, w), lambda i, j: (i, 0)),
                  pl.BlockSpec((ATT_HEADS, tq, 1), lambda i, j: (0, i, 0))],
        out_specs=[pl.BlockSpec((tq, w), lambda i, j: (i, 0)),
                   pl.BlockSpec((ATT_HEADS, tq, 1), lambda i, j: (0, i, 0))],
        out_shape=[jax.ShapeDtypeStruct((t, w), F32), jax.ShapeDtypeStruct((ATT_HEADS, t, 1), F32)],
        scratch_shapes=[pltpu.VMEM((tq, w), F32), pltpu.VMEM((ATT_HEADS, tq, 1), F32)],
        compiler_params=_params(2),
    )(hbuf, hbuf, hbuf, eq, ek, dymix, o, lse)


GROUP_W = SSD_WIDTH // SSD_GROUPS
HEADS_PER_GROUP = SSD_HEADS // SSD_GROUPS


def _ssd_chunk_common(xr, prev8, sm, cw, cb, dtb, avec):
    c = _conv_taps(xr, prev8, cw, cb)
    sig = _sigmoid(c)
    xa = c * sig
    dt = _softplus(sm + dtb)
    a = dt * avec
    acum = _cumsum_rows(a)
    return c, sig, xa, dt, acum


def _ssd_decays(acum, g):
    n = acum.shape[0]
    atot = acum[n - 1:n, :]
    lane0 = LANE_DT + HEADS_PER_GROUP * g
    e = _head_expand(jnp.exp(acum), lane0, HEADS_PER_GROUP, GROUP_W)
    dec = _head_expand(jnp.exp(atot - acum), lane0, HEADS_PER_GROUP, GROUP_W)
    etot = _head_expand(jnp.exp(atot), lane0, HEADS_PER_GROUP, GROUP_W)
    return e, dec, etot


def _ssd_ldec(acum, acum_t, lane, tril):
    return jnp.exp(jnp.where(tril, _col(acum, lane) - _row(acum_t, lane), NEG))


def _ssd_fwd(hbuf, conv_w, conv_b, dtb_vec, a_vec, d_exp, norm_g, *, name):
    t = hbuf.shape[0]
    L = SSD_CHUNK
    nc = t // L
    hb = L // SUBLANES
    cs = COL_SMALL // LANES
    cz = COL_Z // SSD_WIDTH

    def body(x_ref, xp_ref, z_ref, s_ref, cw_ref, cb_ref, dtb_ref, av_ref, dx_ref, ng_ref,
             yc_ref, y_ref, st_ref, state):
        i = pl.program_id(0)

        @pl.when(i == 0)
        def _():
            state[...] = jnp.zeros_like(state)

        prev = jnp.where(i == 0, 0.0, xp_ref[...])
        _, _, xa, dt, acum = _ssd_chunk_common(x_ref[...], prev, s_ref[...], cw_ref[...], cb_ref[...],
                                               dtb_ref[...], av_ref[...])
        acum_t = acum.T
        xs = xa[:, :SSD_WIDTH]
        xdt = xs * _head_expand(dt, LANE_DT, SSD_HEADS, SSD_WIDTH)
        tril = _iota((L, L), 0) >= _iota((L, L), 1)
        lane = _iota((1, LANES), 1)
        ys = []
        for g in range(SSD_GROUPS):
            bg = xa[:, SSD_WIDTH + SSD_STATE * g:SSD_WIDTH + SSD_STATE * (g + 1)].astype(BF16)
            cg = xa[:, SSD_WIDTH + SSD_STATE * (SSD_GROUPS + g):SSD_WIDTH + SSD_STATE * (SSD_GROUPS + g + 1)].astype(BF16)
            gm = _dot(cg, bg, 1, 1)
            e, dec, etot = _ssd_decays(acum, g)
            s_in = state[g]
            st_ref[0, g] = s_in
            xg = xdt[:, GROUP_W * g:GROUP_W * (g + 1)]
            y_off = e * _dot(cg, s_in.astype(BF16), 1, 0)
            state[g] = etot * s_in + _dot(bg, (dec * xg).astype(BF16), 0, 0)
            for pr in range(2):
                xp = xg[:, LANES * pr:LANES * (pr + 1)].astype(BF16)
                outs = []
                for hh in range(2):
                    h = HEADS_PER_GROUP * g + 2 * pr + hh
                    m = gm * _ssd_ldec(acum, acum_t, LANE_DT + h, tril)
                    outs.append(_dot(m.astype(BF16), xp, 1, 0))
                ys.append(jnp.where(lane < HEAD_DIM, outs[0], outs[1]) + y_off[:, LANES * pr:LANES * (pr + 1)])
        y = jnp.concatenate(ys, axis=1)
        y_ref[...] = y
        yd = y + dx_ref[...] * xs
        zz = z_ref[...]
        y2 = yd * zz * _sigmoid(zz)
        ng = ng_ref[...]
        outs = []
        for g in range(SSD_GROUPS):
            yg = y2[:, GROUP_W * g:GROUP_W * (g + 1)]
            rs = lax.rsqrt(jnp.mean(yg * yg, axis=1, keepdims=True) + RMS_EPS)
            outs.append(yg * rs * ng[:, GROUP_W * g:GROUP_W * (g + 1)])
        yc_ref[...] = jnp.concatenate(outs, axis=1)

    cdim = SSD_CONV_DIM
    vecc = pl.BlockSpec((1, cdim), lambda i: (0, 0))
    vecl = pl.BlockSpec((1, LANES), lambda i: (0, 0))
    vecw = pl.BlockSpec((1, SSD_WIDTH), lambda i: (0, 0))
    roww = pl.BlockSpec((L, SSD_WIDTH), lambda i: (i, 0))
    return pl.pallas_call(
        body, name=name, grid=(nc,),
        in_specs=[pl.BlockSpec((L, cdim), lambda i: (i, 0)),
                  pl.BlockSpec((SUBLANES, cdim), lambda i: (jnp.maximum(i * hb - 1, 0), 0)),
                  pl.BlockSpec((L, SSD_WIDTH), lambda i: (i, cz)),
                  pl.BlockSpec((L, LANES), lambda i: (i, cs)),
                  pl.BlockSpec((CONV_K, cdim), lambda i: (0, 0)), vecc, vecl, vecl, vecw, vecw],
        out_specs=[roww, roww, pl.BlockSpec((1, SSD_GROUPS, SSD_STATE, GROUP_W), lambda i: (i, 0, 0, 0))],
        out_shape=[jax.ShapeDtypeStruct((t, SSD_WIDTH), F32), jax.ShapeDtypeStruct((t, SSD_WIDTH), F32),
                   jax.ShapeDtypeStruct((nc, SSD_GROUPS, SSD_STATE, GROUP_W), F32)],
        scratch_shapes=[pltpu.VMEM((SSD_GROUPS, SSD_STATE, GROUP_W), F32)],
        compiler_params=_params(1),
    )(hbuf, hbuf, hbuf, hbuf, conv_w, conv_b, dtb_vec, a_vec, d_exp, norm_g)


def _ssd_bwd(dymix, hbuf, y_ssd, states, conv_w, conv_b, dtb_vec, a_vec, d_exp, norm_g, *, name):
    t = hbuf.shape[0]
    L = SSD_CHUNK
    nc = t // L
    hb = L // SUBLANES
    cs = COL_SMALL // LANES
    cz = COL_Z // SSD_WIDTH
    cdy = (LRU_WIDTH + ATT_WIDTH) // SSD_WIDTH
    cdim = SSD_CONV_DIM

    def body(dyc_ref, x_ref, xp_ref, z_ref, s_ref, y_ref, st_ref, cw_ref, cb_ref, dtb_ref, av_ref, dx_ref, ng_ref,
             dxr_ref, dz_ref, dsm_ref, dng_ref, dd_ref, da_ref, ddtb_ref, dcw_ref, dcb_ref,
             dstate, dnext):
        i = pl.program_id(0)
        ic = nc - 1 - i

        @pl.when(i == 0)
        def _():
            dstate[...] = jnp.zeros_like(dstate)
            dnext[...] = jnp.zeros_like(dnext)
            for ref in (dng_ref, dd_ref, da_ref, ddtb_ref, dcw_ref, dcb_ref):
                ref[...] = jnp.zeros_like(ref)

        xr = x_ref[...]
        sm = s_ref[...]
        prev = jnp.where(ic == 0, 0.0, xp_ref[...])
        avec = av_ref[...]
        c, sig, xa, dt, acum = _ssd_chunk_common(xr, prev, sm, cw_ref[...], cb_ref[...], dtb_ref[...], avec)
        acum_t = acum.T
        xs = xa[:, :SSD_WIDTH]
        dtx = _head_expand(dt, LANE_DT, SSD_HEADS, SSD_WIDTH)
        xdt = xs * dtx
        tril = _iota((L, L), 0) >= _iota((L, L), 1)
        lane = _iota((1, LANES), 1)
        hmasks = (lane < HEAD_DIM, lane >= HEAD_DIM)

        y = y_ref[...]
        dexp = dx_ref[...]
        yd = y + dexp * xs
        zz = z_ref[...]
        sz = _sigmoid(zz)
        siluz = zz * sz
        y2 = yd * siluz
        ng = ng_ref[...]
        dyc = dyc_ref[...]
        dy2s, dngs = [], []
        for g in range(SSD_GROUPS):
            sl = slice(GROUP_W * g, GROUP_W * (g + 1))
            yg = y2[:, sl]
            rs = lax.rsqrt(jnp.mean(yg * yg, axis=1, keepdims=True) + RMS_EPS)
            wv = dyc[:, sl] * ng[:, sl]
            dngs.append(jnp.sum(dyc[:, sl] * yg * rs, axis=0, keepdims=True))
            dy2s.append(rs * wv - yg * (rs * rs * rs) * jnp.mean(wv * yg, axis=1, keepdims=True))
        dy2 = jnp.concatenate(dy2s, axis=1)
        dng_ref[...] += jnp.concatenate(dngs, axis=1)
        dz_ref[...] = dy2 * yd * (sz * (1.0 + zz * (1.0 - sz)))
        dy = dy2 * siluz
        dd_ref[...] += jnp.sum(dy * xs, axis=0, keepdims=True)

        dxs, dbs, dcs = [], [], []
        datot = jnp.zeros((1, LANES), F32)
        lanes = _iota((L, LANES), 1)
        dacum = jnp.zeros((L, LANES), F32)
        for g in range(SSD_GROUPS):
            sl = slice(GROUP_W * g, GROUP_W * (g + 1))
            bg = xa[:, SSD_WIDTH + SSD_STATE * g:SSD_WIDTH + SSD_STATE * (g + 1)].astype(BF16)
            cg = xa[:, SSD_WIDTH + SSD_STATE * (SSD_GROUPS + g):SSD_WIDTH + SSD_STATE * (SSD_GROUPS + g + 1)].astype(BF16)
            gm = _dot(cg, bg, 1, 1)
            e, dec, etot = _ssd_decays(acum, g)
            s_in = st_ref[0, g]
            ds_out = dstate[g]
            dyg = dy[:, sl]
            xg = xdt[:, sl]
            edy = (e * dyg).astype(BF16)
            dstate[g] = etot * ds_out + _dot(cg, edy, 0, 0)
            dx_state = dec * _dot(bg, ds_out.astype(BF16), 1, 0)
            y_off = e * _dot(cg, s_in.astype(BF16), 1, 0)
            dacum = dacum + _head_reduce_group(dyg * y_off - xg * dx_state, g)
            dc_off = _dot(edy, s_in.astype(BF16), 1, 1)
            db_state = _dot((dec * xg).astype(BF16), ds_out.astype(BF16), 1, 1)
            dgsum = jnp.zeros((L, L), F32)
            dx_pairs = []
            for pr in range(2):
                psl = slice(LANES * pr, LANES * (pr + 1))
                xp = xg[:, psl]
                dyp = dyg[:, psl]
                dx_pair = jnp.zeros((L, LANES), F32)
                for hh in range(2):
                    h = HEADS_PER_GROUP * g + 2 * pr + hh
                    ldec = _ssd_ldec(acum, acum_t, LANE_DT + h, tril)
                    dym = jnp.where(hmasks[hh], dyp, 0.0).astype(BF16)
                    xm = jnp.where(hmasks[hh], xp, 0.0).astype(BF16)
                    dx_pair = dx_pair + _dot((gm * ldec).astype(BF16), dym, 0, 0)
                    dml = _dot(dym, xm, 1, 1) * ldec
                    dgsum = dgsum + dml
                    qm = dml * gm
                    seg = jnp.sum(qm, axis=1, keepdims=True) - jnp.sum(qm.T, axis=1, keepdims=True)
                    dacum = dacum + jnp.where(lanes == LANE_DT + h, seg, 0.0)
                dx_pairs.append(dx_pair)
            dgb = dgsum.astype(BF16)
            dcs.append(_dot(dgb, bg, 1, 0) + dc_off)
            dbs.append(_dot(dgb, cg, 0, 0) + db_state)
            dxg = jnp.concatenate(dx_pairs, axis=1) + dx_state
            dxs.append(dxg)
            v = jnp.sum(dx_state * xg, axis=0, keepdims=True) + etot * jnp.sum(ds_out * s_in, axis=0, keepdims=True)
            datot = datot + _head_reduce_row(v, LANE_DT + HEADS_PER_GROUP * g, HEADS_PER_GROUP)
        dx = jnp.concatenate(dxs, axis=1)
        dacum = dacum + jnp.where(_iota((L, LANES), 0) == L - 1, datot, 0.0)
        da = _cumsum_rows(dacum, reverse=True)
        ddt = da * avec + _head_reduce(dx * xs, LANE_DT, SSD_HEADS)
        da_ref[...] += jnp.sum(da * dt, axis=0, keepdims=True)
        ddt_raw = ddt * _sigmoid(sm + dtb_ref[...])
        ddt_raw = jnp.where((lanes >= LANE_DT) & (lanes < LANE_DT + SSD_HEADS), ddt_raw, 0.0)
        dsm_ref[...] = ddt_raw
        ddtb_ref[...] += jnp.sum(ddt_raw, axis=0, keepdims=True)
        dxs_total = dx * dtx + dexp * dy
        dxa = jnp.concatenate([dxs_total] + dbs + dcs, axis=1)
        dc = dxa * (sig * (1.0 + c * (1.0 - sig)))
        dxr, dws = _conv_taps_bwd(dc, dnext[...], cw_ref[...], xr)
        dxr_ref[...] = dxr
        dcw_ref[...] += dws
        dcb_ref[...] += jnp.sum(dc, axis=0, keepdims=True)
        dnext[...] = dc[:SUBLANES]

    rev = lambda i: nc - 1 - i
    vecc = pl.BlockSpec((1, cdim), lambda i: (0, 0))
    vecl = pl.BlockSpec((1, LANES), lambda i: (0, 0))
    vecw = pl.BlockSpec((1, SSD_WIDTH), lambda i: (0, 0))
    cwspec = pl.BlockSpec((CONV_K, cdim), lambda i: (0, 0))
    roww = pl.BlockSpec((L, SSD_WIDTH), lambda i: (rev(i), 0))
    return pl.pallas_call(
        body, name=name, grid=(nc,),
        in_specs=[pl.BlockSpec((L, SSD_WIDTH), lambda i: (rev(i), cdy)),
                  pl.BlockSpec((L, cdim), lambda i: (rev(i), 0)),
                  pl.BlockSpec((SUBLANES, cdim), lambda i: (jnp.maximum(rev(i) * hb - 1, 0), 0)),
                  pl.BlockSpec((L, SSD_WIDTH), lambda i: (rev(i), cz)),
                  pl.BlockSpec((L, LANES), lambda i: (rev(i), cs)),
                  roww,
                  pl.BlockSpec((1, SSD_GROUPS, SSD_STATE, GROUP_W), lambda i: (rev(i), 0, 0, 0)),
                  cwspec, vecc, vecl, vecl, vecw, vecw],
        out_specs=[pl.BlockSpec((L, cdim), lambda i: (rev(i), 0)), roww,
                   pl.BlockSpec((L, LANES), lambda i: (rev(i), 0)),
                   vecw, vecw, vecl, vecl, cwspec, vecc],
        out_shape=[jax.ShapeDtypeStruct((t, cdim), F32), jax.ShapeDtypeStruct((t, SSD_WIDTH), F32),
                   jax.ShapeDtypeStruct((t, LANES), F32),
                   jax.ShapeDtypeStruct((1, SSD_WIDTH), F32), jax.ShapeDtypeStruct((1, SSD_WIDTH), F32),
                   jax.ShapeDtypeStruct((1, LANES), F32), jax.ShapeDtypeStruct((1, LANES), F32),
                   jax.ShapeDtypeStruct((CONV_K, cdim), F32), jax.ShapeDtypeStruct((1, cdim), F32)],
        scratch_shapes=[pltpu.VMEM((SSD_GROUPS, SSD_STATE, GROUP_W), F32), pltpu.VMEM((SUBLANES, cdim), F32)],
        compiler_params=_params(1),
    )(dymix, hbuf, hbuf, hbuf, hbuf, y_ssd, states, conv_w, conv_b, dtb_vec, a_vec, d_exp, norm_g)


def _head_reduce_group(x, g):
    return _head_reduce(x, LANE_DT + HEADS_PER_GROUP * g, HEADS_PER_GROUP)


def _head_reduce_row(v, lane0, nheads):
    colhead = _iota(v.shape, 1) // HEAD_DIM
    lane = _iota((1, LANES), 1)
    out = jnp.zeros((1, LANES), F32)
    for h in range(nheads):
        s = jnp.sum(jnp.where(colhead == h, v, 0.0), axis=1, keepdims=True)
        out = jnp.where(lane == lane0 + h, s, out)
    return out


def _exchange(inps, axes, *, mode, local=True, name):
    n = 2 ** len(axes)
    counts, out_shapes = [], []
    for a in inps:
        if mode == "gather":
            cnt, rest = a.shape[0], a.shape[1:]
        elif mode == "gather_half":
            cnt, rest = a.shape[0] // 2, a.shape[1:]
        elif mode == "a2a":
            cnt, rest = a.shape[1], a.shape[2:]
        else:
            cnt, rest = a.shape[0], a.shape[2:]
        counts.append(cnt)
        out_shapes.append(jax.ShapeDtypeStruct((n, cnt) + tuple(rest), a.dtype))
    units = sum(counts)
    na = len(inps)

    def body(*refs):
        in_refs, out_refs = refs[:na], refs[na:2 * na]
        send_sems, recv_sems, local_sems = refs[2 * na:]
        pos = {ax: lax.axis_index(ax) for ax in MESH_AXES}

        def slot_of(coord):
            s = 0
            for ax in axes:
                s = s * 2 + coord[ax]
            return s

        def src(a, it, slot):
            if mode == "gather":
                return in_refs[a].at[it]
            if mode == "gather_half":
                return in_refs[a].at[pos["c"] * counts[a] + it]
            if mode == "a2a":
                return in_refs[a].at[slot, it]
            return in_refs[a].at[it, slot]

        me = slot_of(pos)
        copies = []
        unit = 0
        for a in range(na):
            for it in range(counts[a]):
                if local:
                    cp = pltpu.make_async_copy(src(a, it, me), out_refs[a].at[me, it], local_sems.at[unit])
                    cp.start()
                    copies.append(cp)
                for delta in range(1, n):
                    coord = dict(pos)
                    for b, ax in enumerate(reversed(axes)):
                        if (delta >> b) & 1:
                            coord[ax] = 1 - pos[ax]
                    k = unit * (n - 1) + delta - 1
                    cp = pltpu.make_async_remote_copy(
                        src_ref=src(a, it, slot_of(coord)), dst_ref=out_refs[a].at[me, it],
                        send_sem=send_sems.at[k], recv_sem=recv_sems.at[k],
                        device_id=(coord["x"], coord["y"], coord["c"]), device_id_type=pl.DeviceIdType.MESH)
                    cp.start()
                    copies.append(cp)
                unit += 1
        for cp in copies:
            cp.wait()

    any_spec = pl.BlockSpec(memory_space=pl.ANY)
    return pl.pallas_call(
        body, name=name,
        in_specs=[any_spec] * na, out_specs=[any_spec] * na, out_shape=out_shapes,
        scratch_shapes=[pltpu.SemaphoreType.DMA((units * (n - 1),)), pltpu.SemaphoreType.DMA((units * (n - 1),)),
                        pltpu.SemaphoreType.DMA((units,))],
    )(*inps)


def _sum_slots(buf, out_dtype, *, name):
    n, rows, cols = buf.shape
    tm = _pick(rows, (512, 256, 128, 8))
    if rows % tm:
        tm = rows

    def body(b_ref, o_ref):
        acc = b_ref[0].astype(F32)
        for s in range(1, n):
            acc = acc + b_ref[s].astype(F32)
        o_ref[...] = acc.astype(out_dtype)

    return pl.pallas_call(
        body, name=name, grid=(pl.cdiv(rows, tm),),
        in_specs=[pl.BlockSpec((n, tm, cols), lambda i: (0, i, 0))],
        out_specs=pl.BlockSpec((tm, cols), lambda i: (i, 0)),
        out_shape=jax.ShapeDtypeStruct((rows, cols), out_dtype),
        compiler_params=_params(1),
    )(buf)


def _adamw(w, g, m, v, *, name):
    shape = w.shape
    cols = shape[-1]
    rows = w.size // cols
    w2, g2, m2, v2 = (a.reshape(rows, cols) for a in (w, g, m, v))
    tm = _pick(rows, (256, 128, 64, 32, 16, 8))
    if rows % tm:
        tm = rows
    bc1 = 1.0 - ADAM_B1 ** ADAM_STEP
    bc2 = 1.0 - ADAM_B2 ** ADAM_STEP

    def body(w_ref, g_ref, m_ref, v_ref, d_ref, nm_ref, nv_ref):
        gg = g_ref[...]
        mm = ADAM_B1 * m_ref[...] + (1.0 - ADAM_B1) * gg
        vv = ADAM_B2 * v_ref[...] + (1.0 - ADAM_B2) * (gg * gg)
        m_hat = mm / bc1
        v_hat = vv / bc2
        d_ref[...] = -ADAM_LR * (m_hat / (jnp.sqrt(v_hat) + ADAM_EPS) + ADAM_WD * w_ref[...])
        nm_ref[...] = mm
        nv_ref[...] = vv

    spec = pl.BlockSpec((tm, cols), lambda i: (i, 0))
    o = jax.ShapeDtypeStruct((rows, cols), F32)
    outs = pl.pallas_call(
        body, name=name, grid=(rows // tm,), in_specs=[spec] * 4, out_specs=[spec] * 3, out_shape=[o] * 3,
        compiler_params=_params(1),
    )(w2, g2, m2, v2)
    return tuple(a.reshape(shape) for a in outs)


def _layer_fwd(li, x, xb, pb, W):
    nm = lambda s: f"l{li}_{s}"
    sv = {"x_in_b": xb}
    g1, u1, a1 = _mm_swiglu(xb, W["ffn1_wg"], W["ffn1_wu"], name=nm("ffn1_up"))
    x1, x1b, xh1, rs1 = _mm_ln(a1, W["ffn1_wd"], x, W["ln1_g"], W["ln1_b"], rscale=ALPHA, mscale=0.5, name=nm("ffn1_down_ln"))
    hbuf = _mm(x1b, W["w_in_p"], name=nm("in_proj"))
    ya, lu, lr, lig, la, lh = _lru_fwd(hbuf, W["lru_conv_w"], W["lru_conv_b"], W["lru_wa_bd"], W["lru_ba"],
                                       W["lru_wx_bd"], W["lru_bx"], W["lru_lambda"], name=nm("lru_fwd"))
    eq, ek = _fox_prep(hbuf, W["fox_bf_vec"], name=nm("fox_prep"))
    yb, lse = _fox_fwd(hbuf, eq, ek, name=nm("fox_fwd"))
    yc, yssd, states = _ssd_fwd(hbuf, W["ssd_conv_w"], W["ssd_conv_b"], W["ssd_dtb_vec"], W["ssd_a_vec"],
                                W["ssd_d_exp"], W["ssd_norm_g"], name=nm("ssd_fwd"))
    ymix = jnp.concatenate([ya, yb, yc], axis=1).astype(BF16)
    x2, x2b, xh2, rs2 = _mm_ln(ymix, W["w_out"], x1, W["ln2_g"], W["ln2_b"], rscale=ALPHA, mscale=1.0, name=nm("out_proj_ln"))
    g2, u2, a2 = _mm_swiglu(x2b, W["ffn2_wg"], W["ffn2_wu"], name=nm("ffn2_up"))
    x3, x3b, xh3, rs3 = _mm_ln(a2, W["ffn2_wd"], x2, W["ln3_g"], W["ln3_b"], rscale=ALPHA, mscale=0.5, name=nm("ffn2_down_ln"))
    x4, x4b, sg, e = _mm_pe(x3, x3b, pb, W["pe_gate_w"], W["pe_gate_b"], W["pe_proj"], name=nm("ple"))
    sv.update(g1=g1, u1=u1, a1=a1, x1b=x1b, xh1=xh1, rs1=rs1, hbuf=hbuf, lu=lu, lr=lr, lig=lig, la=la, lh=lh,
              eq=eq, ek=ek, yb=yb, lse=lse, yssd=yssd, states=states, ymix=ymix, x2b=x2b, xh2=xh2, rs2=rs2,
              g2=g2, u2=u2, a2=a2, x3b=x3b, xh3=xh3, rs3=rs3, sg=sg, e=e, pb=pb)
    return x4, x4b, sv


def _layer_bwd(li, dx4, sv, W):
    nm = lambda s: f"l{li}_{s}"
    G = {}
    dgp, de, dbg = _pe_bwd_elem(dx4, sv["sg"], sv["e"], name=nm("ple_bwd"))
    G["pe_gate_b"] = dbg
    G["pe_gate_w"] = _mm(sv["x3b"], dgp, ta=True, out_dtype=BF16, name=nm("d_pe_gate_w"))
    G["pe_proj"] = _mm(sv["pb"], de, ta=True, out_dtype=BF16, chip_cols=True, name=nm("d_pe_proj"))
    dr3, G["ln3_g"], G["ln3_b"] = _bwd_proj([(dgp, W["pe_gate_w"])], dx4, rscale=1.0,
                                            ln=(sv["xh3"], sv["rs3"], W["ln3_g"]), name=nm("ln3_bwd"))
    G["ffn2_wd"] = _mm(sv["a2"], dr3, ta=True, scale=0.5, out_dtype=BF16, name=nm("d_ffn2_wd"))
    dg2, du2 = _mm_swiglu_bwd(dr3, W["ffn2_wd"], sv["g2"], sv["u2"], scale=0.5, name=nm("ffn2_act_bwd"))
    G["ffn2_wg"] = _mm(sv["x2b"], dg2, ta=True, out_dtype=BF16, chip_cols=True, name=nm("d_ffn2_wg"))
    G["ffn2_wu"] = _mm(sv["x2b"], du2, ta=True, out_dtype=BF16, chip_cols=True, name=nm("d_ffn2_wu"))
    dr2, G["ln2_g"], G["ln2_b"] = _bwd_proj([(dg2, W["ffn2_wg"]), (du2, W["ffn2_wu"])], dr3, rscale=ALPHA,
                                            ln=(sv["xh2"], sv["rs2"], W["ln2_g"]), name=nm("ln2_bwd"))
    G["w_out"] = _mm(sv["ymix"], dr2, ta=True, out_dtype=BF16, name=nm("d_w_out"))
    dymix = _mm(dr2, W["w_out"], tb=True, name=nm("d_ymix"))
    hbuf = sv["hbuf"]
    (dur, dgr, G["lru_conv_w"], G["lru_conv_b"], G["lru_wa_bd"], G["lru_ba"], G["lru_wx_bd"], G["lru_bx"],
     G["lru_lambda"]) = _lru_bwd(dymix, hbuf, sv["lu"], sv["lr"], sv["lig"], sv["la"], sv["lh"],
                                 W["lru_conv_w"], W["lru_wa_bd"], W["lru_wx_bd"], W["lru_lambda"], name=nm("lru_bwd"))
    dk, dv, dfk = _fox_bwd_kv(hbuf, sv["eq"], sv["ek"], dymix, sv["yb"], sv["lse"], name=nm("fox_bwd_kv"))
    dq, dfq = _fox_bwd_q(hbuf, sv["eq"], sv["ek"], dymix, sv["yb"], sv["lse"], name=nm("fox_bwd_q"))
    dfc = jnp.pad(dfq[:, :, 0].T - dfk[:ATT_HEADS].T, ((0, 0), (0, LANES - ATT_HEADS)))
    dsm_f, G["fox_bf_vec"] = _fox_post(dfc, hbuf, W["fox_bf_vec"], name=nm("fox_post"))
    (dxr, dz, dsm_dt, G["ssd_norm_g"], G["ssd_d_exp"], G["ssd_a_vec"], G["ssd_dtb_vec"], G["ssd_conv_w"],
     G["ssd_conv_b"]) = _ssd_bwd(dymix, hbuf, sv["yssd"], sv["states"], W["ssd_conv_w"], W["ssd_conv_b"],
                                 W["ssd_dtb_vec"], W["ssd_a_vec"], W["ssd_d_exp"], W["ssd_norm_g"], name=nm("ssd_bwd"))
    t = dx4.shape[0]
    dh = jnp.concatenate([dxr.astype(BF16), dz.astype(BF16), dur.astype(BF16), dgr.astype(BF16), dq.astype(BF16),
                          dk.astype(BF16), dv.astype(BF16), (dsm_f + dsm_dt).astype(BF16),
                          jnp.zeros((t, H_WIDTH - COL_SMALL - LANES), BF16)], axis=1)
    G["w_in_p"] = _mm(sv["x1b"], dh, ta=True, name=nm("d_w_in"))
    dr1, G["ln1_g"], G["ln1_b"] = _bwd_proj([(dh, W["w_in_p"])], dr2, rscale=ALPHA,
                                            ln=(sv["xh1"], sv["rs1"], W["ln1_g"]), name=nm("ln1_bwd"))
    G["ffn1_wd"] = _mm(sv["a1"], dr1, ta=True, scale=0.5, out_dtype=BF16, name=nm("d_ffn1_wd"))
    dg1, du1 = _mm_swiglu_bwd(dr1, W["ffn1_wd"], sv["g1"], sv["u1"], scale=0.5, name=nm("ffn1_act_bwd"))
    G["ffn1_wg"] = _mm(sv["x_in_b"], dg1, ta=True, out_dtype=BF16, chip_cols=True, name=nm("d_ffn1_wg"))
    G["ffn1_wu"] = _mm(sv["x_in_b"], du1, ta=True, out_dtype=BF16, chip_cols=True, name=nm("d_ffn1_wu"))
    (dx_in,) = _bwd_proj([(dg1, W["ffn1_wg"]), (du1, W["ffn1_wu"])], dr1, rscale=ALPHA, ln=None, name=nm("x_in_bwd"))
    return dx_in, G


def _block_diag(w):
    n, b, _ = w.shape
    eye = jnp.eye(n, dtype=w.dtype)
    return (eye[:, None, :, None] * w[:, :, None, :]).reshape(n * b, n * b)


def _block_diag_extract(m):
    n, b = LRU_HEADS, HEAD_DIM
    return jnp.stack([m[b * i:b * (i + 1), b * i:b * (i + 1)] for i in range(n)])


def _lane_vec(v, lane0):
    return jnp.pad(v.astype(F32), (lane0, LANES - lane0 - v.shape[0])).reshape(1, LANES)


def _w_in_permute(w):
    d = w.shape[0]
    z = lambda n: jnp.zeros((d, n), w.dtype)
    return jnp.concatenate([w[:, 1796:2820], w[:, 1284:1796], w[:, 0:512], w[:, 512:1280],
                            w[:, 1280:1284], w[:, 2820:2828], z(LANES - 12), z(H_WIDTH - COL_SMALL - LANES)], axis=1)


def _w_in_unpermute(wp):
    return jnp.concatenate([wp[:, COL_U:COL_Q], wp[:, COL_Q:COL_SMALL], wp[:, COL_SMALL:COL_SMALL + 4],
                            wp[:, COL_Z:COL_U], wp[:, COL_XBC:COL_Z], wp[:, COL_SMALL + 4:COL_SMALL + 12]], axis=1)


def _layer_weights(li, chipw, small):
    g = lambda n: small[n][li]
    W = {n: g(n) for n in ("ln1_g", "ln1_b", "ln2_g", "ln2_b", "ln3_g", "ln3_b", "pe_gate_b", "lru_conv_w",
                           "ssd_conv_w")}
    for n in ("ffn1_wg", "ffn1_wu", "ffn2_wg", "ffn2_wu"):
        W[n] = chipw[n]
    for n in ("ffn1_wd", "ffn2_wd", "w_out", "pe_gate_w"):
        W[n] = chipw[n].reshape(-1, D_MODEL)
    W["pe_proj"] = jnp.moveaxis(chipw["pe_proj"], 0, 1).reshape(PLE_DIM, D_MODEL)
    w_in = jnp.moveaxis(chipw["w_in"][:, :, :IN_WIDTH // N_CHIPS], 0, 1).reshape(D_MODEL, IN_WIDTH)
    W["w_in_p"] = _w_in_permute(w_in)
    for n in ("lru_conv_b", "lru_ba", "lru_bx", "lru_lambda", "ssd_conv_b", "ssd_norm_g"):
        W[n] = g(n).reshape(1, -1)
    W["lru_wa_bd"] = _block_diag(g("lru_wa")).astype(BF16)
    W["lru_wx_bd"] = _block_diag(g("lru_wx")).astype(BF16)
    W["fox_bf_vec"] = _lane_vec(g("fox_bf"), LANE_F)
    W["ssd_dtb_vec"] = _lane_vec(g("ssd_dt_bias"), LANE_DT)
    W["ssd_a_vec"] = _lane_vec(-jnp.exp(g("ssd_a_log")), LANE_DT)
    W["ssd_d_exp"] = jnp.repeat(g("ssd_d"), HEAD_DIM).reshape(1, SSD_WIDTH)
    return W


def _layer_big_grads_by_chip(G):
    out = {n: G[n] for n in ("ffn1_wg", "ffn1_wu", "ffn2_wg", "ffn2_wu", "pe_proj")}
    for n in ("ffn1_wd", "ffn2_wd", "w_out", "pe_gate_w"):
        out[n] = G[n].reshape(N_CHIPS, -1, D_MODEL)
    share = IN_WIDTH // N_CHIPS
    d_w_in = jnp.moveaxis(_w_in_unpermute(G["w_in_p"]).reshape(D_MODEL, N_CHIPS, share), 1, 0)
    out["w_in"] = jnp.pad(d_w_in.astype(BF16), ((0, 0), (0, 0), (0, SHARE - share)))
    return out


def _layer_small_grads(G, W):
    out = {n: G[n] for n in ("lru_conv_w", "ssd_conv_w")}
    for n in ("ln1_g", "ln1_b", "ln2_g", "ln2_b", "ln3_g", "ln3_b", "pe_gate_b", "lru_conv_b", "lru_ba", "lru_bx",
              "lru_lambda", "ssd_conv_b", "ssd_norm_g"):
        out[n] = G[n].reshape(-1)
    out["lru_wa"] = _block_diag_extract(G["lru_wa_bd"])
    out["lru_wx"] = _block_diag_extract(G["lru_wx_bd"])
    out["fox_bf"] = G["fox_bf_vec"][0, LANE_F:LANE_F + ATT_HEADS]
    out["ssd_dt_bias"] = G["ssd_dtb_vec"][0, LANE_DT:LANE_DT + SSD_HEADS]
    out["ssd_a_log"] = G["ssd_a_vec"][0, LANE_DT:LANE_DT + SSD_HEADS] * W["ssd_a_vec"][0, LANE_DT:LANE_DT + SSD_HEADS]
    out["ssd_d"] = G["ssd_d_exp"].reshape(SSD_HEADS, HEAD_DIM).sum(axis=1)
    return out


def _local_step(x, p, target, Ws):
    saves = []
    xb = x.astype(BF16)
    for li in range(DEPTH):
        x, xb, sv = _layer_fwd(li, x, xb, p[li].astype(BF16), Ws[li])
        saves.append(sv)
    dx, loss = _loss_kernel(x, target, name="loss")
    big = [None] * DEPTH
    small = [None] * DEPTH
    for li in reversed(range(DEPTH)):
        dx, G = _layer_bwd(li, dx, saves[li], Ws[li])
        big[li] = _layer_big_grads_by_chip(G)
        small[li] = _layer_small_grads(G, Ws[li])
    stacked = {n: jnp.stack([small[li][n] for li in range(DEPTH)]) for n in small[0]}
    return loss, dx, big, stacked


WEIGHTS = ['ln1_g', 'ln1_b', 'ffn1_wg', 'ffn1_wu', 'ffn1_wd', 'w_in', 'lru_conv_w', 'lru_conv_b', 'lru_wa', 'lru_ba',
           'lru_wx', 'lru_bx', 'lru_lambda', 'fox_bf', 'ssd_conv_w', 'ssd_conv_b', 'ssd_dt_bias', 'ssd_a_log', 'ssd_d',
           'ssd_norm_g', 'w_out', 'ln2_g', 'ln2_b', 'ffn2_wg', 'ffn2_wu', 'ffn2_wd', 'ln3_g', 'ln3_b', 'pe_proj',
           'pe_gate_w', 'pe_gate_b']
CLASSES = (("ffn1_wg", "ffn1_wu", "ffn2_wg", "ffn2_wu", "w_in"),
           ("ffn1_wd", "ffn2_wd"),
           ("w_out", "pe_gate_w"),
           ("pe_proj",))
CLASS_PAD_AXIS = (1, 0, None, None)
BIG = {n: ci for ci, names in enumerate(CLASSES) for n in names}
SMALL_SHARDED = {'lru_conv_w': 2, 'ssd_conv_w': 2}
PACK_COLS = 1024


def _unshard(seg, axis):
    moved = jnp.moveaxis(seg, 0, axis)
    shp = list(moved.shape)
    shp[axis:axis + 2] = [shp[axis] * shp[axis + 1]]
    return moved.reshape(shp)


def _pad_axis(a, axis, size):
    if axis is None or a.shape[axis] == size:
        return a
    pads = [(0, 0)] * a.ndim
    pads[axis] = (0, size - a.shape[axis])
    return jnp.pad(a, pads)


def _pack(arrs, dtype, cols):
    flat = jnp.concatenate([a.astype(dtype).reshape(-1) for a in arrs])
    pad = (-flat.shape[0]) % cols
    if pad:
        flat = jnp.concatenate([flat, jnp.zeros((pad,), dtype)])
    return flat.reshape(-1, cols)


def _unpack(flat, shapes):
    out, off = [], 0
    for s in shapes:
        n = math.prod(s)
        out.append(flat[off:off + n].reshape(s))
        off += n
    return out


def kernel(x, p, ln1_g, ln1_b, ffn1_wg, ffn1_wu, ffn1_wd, w_in, lru_conv_w, lru_conv_b, lru_wa, lru_ba, lru_wx, lru_bx, lru_lambda, fox_bf, ssd_conv_w, ssd_conv_b, ssd_dt_bias, ssd_a_log, ssd_d, ssd_norm_g, w_out, ln2_g, ln2_b, ffn2_wg, ffn2_wu, ffn2_wd, ln3_g, ln3_b, pe_proj, pe_gate_w, pe_gate_b, loss_target, m_ln1_g, m_ln1_b, m_ffn1_wg, m_ffn1_wu, m_ffn1_wd, m_w_in, m_lru_conv_w, m_lru_conv_b, m_lru_wa, m_lru_ba, m_lru_wx, m_lru_bx, m_lru_lambda, m_fox_bf, m_ssd_conv_w, m_ssd_conv_b, m_ssd_dt_bias, m_ssd_a_log, m_ssd_d, m_ssd_norm_g, m_w_out, m_ln2_g, m_ln2_b, m_ffn2_wg, m_ffn2_wu, m_ffn2_wd, m_ln3_g, m_ln3_b, m_pe_proj, m_pe_gate_w, m_pe_gate_b, v_ln1_g, v_ln1_b, v_ffn1_wg, v_ffn1_wu, v_ffn1_wd, v_w_in, v_lru_conv_w, v_lru_conv_b, v_lru_wa, v_lru_ba, v_lru_wx, v_lru_bx, v_lru_lambda, v_fox_bf, v_ssd_conv_w, v_ssd_conv_b, v_ssd_dt_bias, v_ssd_a_log, v_ssd_d, v_ssd_norm_g, v_w_out, v_ln2_g, v_ln2_b, v_ffn2_wg, v_ffn2_wu, v_ffn2_wd, v_ln3_g, v_ln3_b, v_pe_proj, v_pe_gate_w, v_pe_gate_b):
    args = locals()
    w_loc = {n: args[n] for n in WEIGHTS}
    m_loc = {n: args["m_" + n] for n in WEIGHTS}
    v_loc = {n: args["v_" + n] for n in WEIGHTS}
    chip = 2 * lax.axis_index("x") + lax.axis_index("y")
    core = lax.axis_index("c")
    big = list(BIG)
    small_sh = list(SMALL_SHARDED)
    small_rep = [n for n in WEIGHTS if n not in BIG and n not in SMALL_SHARDED]

    srcs = [jnp.stack([_pad_axis(w_loc[n][li].astype(BF16), pad, SHARE) for li in range(DEPTH) for n in names])
            for names, pad in zip(CLASSES, CLASS_PAD_AXIS)]
    halves = _exchange(srcs, ("x", "y"), mode="gather_half", local=False, name="gather_w_chips")
    halves = [lax.dynamic_update_index_in_dim(
                  h, lax.dynamic_index_in_dim(s.reshape((2, -1) + s.shape[1:]), core, 0, keepdims=False), chip, 0)
              for h, s in zip(halves, srcs)]
    both = _exchange([h.reshape((-1,) + h.shape[2:]) for h in halves], ("c",), mode="gather", local=False,
                     name="gather_w_cores")
    chipw = [{} for _ in range(DEPTH)]
    for names, h, b in zip(CLASSES, halves, both):
        hn = len(names)
        b = b.reshape((2, N_CHIPS, hn) + b.shape[2:])
        for li in range(DEPTH):
            for j, n in enumerate(names):
                chipw[li][n] = jnp.where(core == li, h[:, j], b[li, :, j])
    small = {n: w_loc[n] for n in small_rep}
    spack = _pack([w_loc[n] for n in small_sh], F32, LANES)
    (sg,) = _exchange([spack[None]], ("x", "y"), mode="gather", name="gather_conv_w")
    for n, seg in zip(small_sh, _unpack_rows(sg.reshape(N_CHIPS, -1), [w_loc[n].shape for n in small_sh])):
        small[n] = _unshard(seg, SMALL_SHARDED[n])
    Ws = [_layer_weights(li, chipw[li], small) for li in range(DEPTH)]

    loss, grad_x, g_big, g_small = _local_step(x[0], p[:, 0], loss_target[0], Ws)
    loss = lax.psum(loss[0, 0], MESH_AXES)

    gcls = [jnp.stack([g_big[li][n] for li in range(DEPTH) for n in names]) for names in CLASSES]
    gcls = [g.reshape((2, -1) + g.shape[1:]) for g in gcls]
    pair = _exchange(gcls, ("c",), mode="a2a", local=False, name="reduce_cores")
    pair = [lax.dynamic_update_index_in_dim(pr, lax.dynamic_index_in_dim(g, core, 0, keepdims=False), core, 0)
            for pr, g in zip(pair, gcls)]
    s2 = [_sum_slots(pr.reshape(2, -1, pr.shape[-1]), BF16, name=f"reduce_cores_sum{ci}").reshape(pr.shape[1:])
          for ci, pr in enumerate(pair)]
    quad = _exchange(s2, ("x", "y"), mode="a2a_inner", local=False, name="reduce_chips")
    quad = [lax.dynamic_update_index_in_dim(q, lax.dynamic_index_in_dim(s, chip, 1, keepdims=False), chip, 0)
            for q, s in zip(quad, s2)]
    red = [_sum_slots(q.reshape(N_CHIPS, -1, q.shape[-1]), F32, name=f"reduce_chips_sum{ci}").reshape(q.shape[1:])
           for ci, q in enumerate(quad)]
    shared = _exchange(red, ("c",), mode="gather", local=False, name="reduce_share")
    shared = [lax.dynamic_update_index_in_dim(sh, r, core, 0) for sh, r in zip(shared, red)]
    g_red = {}
    for names, sh in zip(CLASSES, shared):
        sh = sh.reshape((-1,) + sh.shape[2:])
        for j, n in enumerate(names):
            g = jnp.stack([sh[li * len(names) + j] for li in range(DEPTH)])
            g_red[n] = g[tuple(slice(0, s) for s in w_loc[n].shape)]
    small_all = small_rep + small_sh
    sgp = _pack([g_small[n] for n in small_all], F32, PACK_COLS)
    (sall,) = _exchange([sgp[None]], MESH_AXES, mode="gather", name="reduce_small")
    sred = _sum_slots(sall.reshape((2 ** len(MESH_AXES),) + sgp.shape), F32, name="reduce_small_sum").reshape(-1)
    for n, g in zip(small_all, _unpack(sred, [g_small[n].shape for n in small_all])):
        if n in SMALL_SHARDED:
            width = w_loc[n].shape[-1]
            g = lax.dynamic_slice_in_dim(g, chip * width, width, axis=SMALL_SHARDED[n])
        g_red[n] = g

    delta, new_m, new_v = {}, {}, {}
    for n in big:
        delta[n], new_m[n], new_v[n] = _adamw(w_loc[n], g_red[n], m_loc[n], v_loc[n], name="adamw_" + n)
    shapes = [w_loc[n].shape for n in small_all]
    packs = [_pack([d[n] for n in small_all], F32, LANES) for d in (w_loc, g_red, m_loc, v_loc)]
    outs = _adamw(*packs, name="adamw_small")
    for d, o in zip((delta, new_m, new_v), outs):
        for n, a in zip(small_all, _unpack(o.reshape(-1), shapes)):
            d[n] = a
    return (loss, grad_x[None], *[g_red[n] for n in WEIGHTS], *[delta[n] for n in WEIGHTS],
            *[new_m[n] for n in WEIGHTS], *[new_v[n] for n in WEIGHTS])


def _unpack_rows(gathered, shapes):
    out, off = [], 0
    for s in shapes:
        n = math.prod(s)
        out.append(gathered[:, off:off + n].reshape((N_CHIPS,) + tuple(s)))
        off += n
    return out
```

```python
import functools
import math

import jax
import jax.numpy as jnp
from jax import lax
from jax.experimental import pallas as pl
from jax.experimental.pallas import tpu as pltpu

F32 = jnp.float32
BF16 = jnp.bfloat16

D_MODEL = 1024
DEPTH = 2
PLE_DIM = 256
HEAD_DIM = 64
LRU_WIDTH = 256
LRU_HEADS = 4
LRU_C = 8.0
CONV_K = 4
ATT_WIDTH = 256
ATT_HEADS = 4
SSD_WIDTH = 512
SSD_HEADS = 8
SSD_GROUPS = 2
SSD_STATE = 128
SSD_CHUNK = 128
SSD_CONV_DIM = 1024
FFN_DIM = 2816
ALPHA = (2.0 * DEPTH) ** 0.25
LN_EPS = 1e-5
RMS_EPS = 1e-5
IN_WIDTH = 2828
ADAM_LR = 0.001
ADAM_B1 = 0.9
ADAM_B2 = 0.999
ADAM_EPS = 1e-08
ADAM_WD = 0.01
ADAM_STEP = 10

H_WIDTH = 3072
COL_XBC, COL_Z, COL_U, COL_G, COL_Q, COL_K, COL_V, COL_SMALL = 0, 1024, 1536, 1792, 2048, 2304, 2560, 2816
LANE_F = 0
LANE_DT = 4
LANES = 128
SUBLANES = 8
NEG = -1e30

VMEM_LIMIT = 48 * 1024 * 1024

N_CHIPS = 4
MESH_AXES = ("x", "y", "c")
SHARE = 768


def _params(n):
    return pltpu.CompilerParams(dimension_semantics=("arbitrary",) * n, vmem_limit_bytes=VMEM_LIMIT)


def _pick(n, cands):
    for c in cands:
        if n % c == 0:
            return c
    return n


def _iota(shape, dim):
    return lax.broadcasted_iota(jnp.int32, shape, dim)


def _shift_down(x, s, prev8):
    if s == 0:
        return x
    r = pltpu.roll(x, s, 0)
    pr = pltpu.roll(prev8, s, 0)
    head = jnp.where(_iota(pr.shape, 0) < s, pr, r[:SUBLANES])
    return jnp.concatenate([head, r[SUBLANES:]], axis=0)


def _shift_up(x, s, next8):
    if s == 0:
        return x
    n = x.shape[0]
    r = pltpu.roll(x, n - s, 0)
    nr = pltpu.roll(next8, SUBLANES - s, 0)
    tail = jnp.where(_iota(nr.shape, 0) >= SUBLANES - s, nr, r[n - SUBLANES:])
    return jnp.concatenate([r[:n - SUBLANES], tail], axis=0)


def _scan_fwd(a, b):
    n = a.shape[0]
    row = _iota(a.shape, 0)
    d = 1
    while d < n:
        keep = row >= d
        a_s = jnp.where(keep, pltpu.roll(a, d, 0), 1.0)
        b_s = jnp.where(keep, pltpu.roll(b, d, 0), 0.0)
        b = a * b_s + b
        a = a * a_s
        d *= 2
    return a, b


def _scan_bwd(a, b):
    n = a.shape[0]
    row = _iota(a.shape, 0)
    d = 1
    while d < n:
        keep = row < n - d
        a_s = jnp.where(keep, pltpu.roll(a, n - d, 0), 1.0)
        b_s = jnp.where(keep, pltpu.roll(b, n - d, 0), 0.0)
        b = a * b_s + b
        a = a * a_s
        d *= 2
    return a, b


def _cumsum_rows(x, reverse=False):
    n = x.shape[0]
    row = _iota(x.shape, 0)
    d = 1
    while d < n:
        if reverse:
            x = x + jnp.where(row < n - d, pltpu.roll(x, n - d, 0), 0.0)
        else:
            x = x + jnp.where(row >= d, pltpu.roll(x, d, 0), 0.0)
        d *= 2
    return x


def _col(x, lane):
    return jnp.sum(jnp.where(_iota(x.shape, 1) == lane, x, 0.0), axis=1, keepdims=True)


def _row(x, r):
    return jnp.sum(jnp.where(_iota(x.shape, 0) == r, x, 0.0), axis=0, keepdims=True)


def _sigmoid(x):
    return jax.nn.sigmoid(x)


def _softplus(x):
    return jnp.maximum(x, 0.0) + jnp.log(1.0 + jnp.exp(-jnp.abs(x)))


def _gelu_and_grad(x):
    c0 = math.sqrt(2.0 / math.pi)
    inner = c0 * (x + 0.044715 * x * x * x)
    t = jnp.tanh(inner)
    g = 0.5 * x * (1.0 + t)
    dg = 0.5 * (1.0 + t) + 0.5 * x * (1.0 - t * t) * c0 * (1.0 + 3.0 * 0.044715 * x * x)
    return g, dg


def _dot(a, b, ca, cb):
    return lax.dot_general(a, b, (((ca,), (cb,)), ((), ())), preferred_element_type=F32)


def _conv_taps(xr, prev8, w, bias):
    y = bias + w[CONV_K - 1:CONV_K, :] * xr
    for j in range(CONV_K - 1):
        y = y + w[j:j + 1, :] * _shift_down(xr, CONV_K - 1 - j, prev8)
    return y


def _conv_taps_bwd(dy, next8, w, xr):
    dx = None
    dws = []
    for j in range(CONV_K):
        sh = _shift_up(dy, CONV_K - 1 - j, next8)
        term = w[j:j + 1, :] * sh
        dx = term if dx is None else dx + term
        dws.append(jnp.sum(sh * xr, axis=0, keepdims=True))
    return dx, jnp.concatenate(dws, axis=0)


def _head_expand(v, lane0, nheads, width):
    rows = v.shape[0]
    colhead = _iota((rows, width), 1) // HEAD_DIM
    out = jnp.zeros((rows, width), F32)
    for h in range(nheads):
        out = jnp.where(colhead == h, _col(v, lane0 + h), out)
    return out


def _head_reduce(x, lane0, nheads):
    rows = x.shape[0]
    colhead = _iota(x.shape, 1) // HEAD_DIM
    lane = _iota((rows, LANES), 1)
    out = jnp.zeros((rows, LANES), F32)
    for h in range(nheads):
        s = jnp.sum(jnp.where(colhead == h, x, 0.0), axis=1, keepdims=True)
        out = jnp.where(lane == lane0 + h, s, out)
    return out


def _mm(a, b, *, ta=False, tb=False, scale=1.0, out_dtype=F32, chip_cols=False, name):
    if ta:
        kk, m = a.shape
    else:
        m, kk = a.shape
    n = b.shape[0] if tb else b.shape[1]
    tm = _pick(m, (1024, 512, 256, 128))
    tn = _pick(n // N_CHIPS, (768, 256, 128)) if chip_cols else _pick(n, (1024, 768, 512, 256, 128))
    tk = _pick(kk, (1024, 768, 512, 256, 128))
    nk = kk // tk
    dn_a = 0 if ta else 1
    dn_b = 1 if tb else 0
    if chip_cols:
        per = n // N_CHIPS // tn
        out_spec = pl.BlockSpec((None, tm, tn), lambda i, j, k: (j // per, i, j % per))
        out_shape = jax.ShapeDtypeStruct((N_CHIPS, m, n // N_CHIPS), out_dtype)
    else:
        out_spec = pl.BlockSpec((tm, tn), lambda i, j, k: (i, j))
        out_shape = jax.ShapeDtypeStruct((m, n), out_dtype)

    def body(a_ref, b_ref, o_ref, acc):
        k = pl.program_id(2)

        @pl.when(k == 0)
        def _():
            acc[...] = jnp.zeros_like(acc)

        acc[...] += _dot(a_ref[...].astype(BF16), b_ref[...].astype(BF16), dn_a, dn_b)

        @pl.when(k == nk - 1)
        def _():
            o_ref[...] = (acc[...] * scale).astype(out_dtype)

    a_spec = pl.BlockSpec((tk, tm), lambda i, j, k: (k, i)) if ta else pl.BlockSpec((tm, tk), lambda i, j, k: (i, k))
    b_spec = pl.BlockSpec((tn, tk), lambda i, j, k: (j, k)) if tb else pl.BlockSpec((tk, tn), lambda i, j, k: (k, j))
    return pl.pallas_call(
        body, name=name, grid=(m // tm, n // tn, nk),
        in_specs=[a_spec, b_spec],
        out_specs=out_spec, out_shape=out_shape,
        scratch_shapes=[pltpu.VMEM((tm, tn), F32)],
        compiler_params=_params(3),
    )(a, b)


def _mm_swiglu(xb, wg, wu, *, name):
    t, d = xb.shape
    share = wg.shape[2]
    n = N_CHIPS * share
    tm = _pick(t, (512, 256, 128))
    tn = _pick(share, (768, 256, 128))
    per = share // tn

    def body(x_ref, wg_ref, wu_ref, g_ref, u_ref, a_ref):
        x = x_ref[...]
        g = _dot(x, wg_ref[...], 1, 0)
        u = _dot(x, wu_ref[...], 1, 0)
        g_ref[...] = g.astype(BF16)
        u_ref[...] = u.astype(BF16)
        a_ref[...] = (g * _sigmoid(g) * u).astype(BF16)

    o = jax.ShapeDtypeStruct((t, n), BF16)
    ospec = pl.BlockSpec((tm, tn), lambda i, j: (i, j))
    return pl.pallas_call(
        body, name=name, grid=(t // tm, n // tn),
        in_specs=[pl.BlockSpec((tm, d), lambda i, j: (i, 0)),
                  pl.BlockSpec((None, d, tn), lambda i, j: (j // per, 0, j % per)),
                  pl.BlockSpec((None, d, tn), lambda i, j: (j // per, 0, j % per))],
        out_specs=[ospec, ospec, ospec], out_shape=[o, o, o],
        compiler_params=_params(2),
    )(xb, wg, wu)


def _mm_swiglu_bwd(dr, wd, g, u, *, scale, name):
    t, d = dr.shape
    n = wd.shape[0]
    tm = _pick(t, (512, 256, 128))
    tn = _pick(n, (768, 256, 128))

    def body(dr_ref, wd_ref, g_ref, u_ref, dg_ref, du_ref):
        da = _dot(dr_ref[...].astype(BF16), wd_ref[...], 1, 1) * scale
        gg = g_ref[...].astype(F32)
        uu = u_ref[...].astype(F32)
        sg = _sigmoid(gg)
        dg_ref[...] = (da * uu * (sg * (1.0 + gg * (1.0 - sg)))).astype(BF16)
        du_ref[...] = (da * gg * sg).astype(BF16)

    o = jax.ShapeDtypeStruct((t, n), BF16)
    ospec = pl.BlockSpec((tm, tn), lambda i, j: (i, j))
    return pl.pallas_call(
        body, name=name, grid=(t // tm, n // tn),
        in_specs=[pl.BlockSpec((tm, d), lambda i, j: (i, 0)),
                  pl.BlockSpec((tn, d), lambda i, j: (j, 0)),
                  ospec, ospec],
        out_specs=[ospec, ospec], out_shape=[o, o],
        compiler_params=_params(2),
    )(dr, wd, g, u)


def _mm_ln(a, w, resid, gain, bias, *, rscale, mscale, name):
    t, kk = a.shape
    d = w.shape[1]
    tm = _pick(t, (512, 256, 128))
    tk = _pick(kk, (1024, 1408, 512, 256, 128))
    nk = kk // tk

    def body(a_ref, w_ref, r_ref, g_ref, b_ref, y_ref, yb_ref, xh_ref, rs_ref, acc):
        k = pl.program_id(1)

        @pl.when(k == 0)
        def _():
            acc[...] = jnp.zeros_like(acc)

        acc[...] += _dot(a_ref[...].astype(BF16), w_ref[...], 1, 0)

        @pl.when(k == nk - 1)
        def _():
            r = rscale * r_ref[...] + mscale * acc[...]
            mu = jnp.mean(r, axis=1, keepdims=True)
            xc = r - mu
            var = jnp.mean(xc * xc, axis=1, keepdims=True)
            rstd = lax.rsqrt(var + LN_EPS)
            xh = xc * rstd
            y = xh * g_ref[...] + b_ref[...]
            y_ref[...] = y
            yb_ref[...] = y.astype(BF16)
            xh_ref[...] = xh
            rs_ref[...] = rstd

    row = pl.BlockSpec((tm, d), lambda i, k: (i, 0))
    vec = pl.BlockSpec((1, d), lambda i, k: (0, 0))
    return pl.pallas_call(
        body, name=name, grid=(t // tm, nk),
        in_specs=[pl.BlockSpec((tm, tk), lambda i, k: (i, k)),
                  pl.BlockSpec((tk, d), lambda i, k: (k, 0)), row, vec, vec],
        out_specs=[row, row, row, pl.BlockSpec((tm, 1), lambda i, k: (i, 0))],
        out_shape=[jax.ShapeDtypeStruct((t, d), F32), jax.ShapeDtypeStruct((t, d), BF16),
                   jax.ShapeDtypeStruct((t, d), F32), jax.ShapeDtypeStruct((t, 1), F32)],
        scratch_shapes=[pltpu.VMEM((tm, d), F32)],
        compiler_params=_params(2),
    )(a, w, resid, gain.reshape(1, d), bias.reshape(1, d))


def _bwd_proj(pairs, resid, *, rscale, ln, name):
    t, kk = pairs[0][0].shape
    d = pairs[0][1].shape[-2]
    tm = _pick(t, (512, 256, 128))
    tk = _pick(pairs[0][1].shape[-1], (1024, 768, 512, 256, 128))
    nk = kk // tk
    nt = t // tm
    npair = len(pairs)
    has_ln = ln is not None

    def body(*refs):
        ab = refs[:2 * npair]
        r_ref = refs[2 * npair]
        pos = 2 * npair + 1
        if has_ln:
            xh_ref, rs_ref, g_ref = refs[pos:pos + 3]
            pos += 3
            o_ref, dg_ref, db_ref = refs[pos:pos + 3]
            pos += 3
        else:
            o_ref = refs[pos]
            pos += 1
        acc = refs[pos]
        i = pl.program_id(0)
        k = pl.program_id(1)

        @pl.when(k == 0)
        def _():
            acc[...] = jnp.zeros_like(acc)

        for q in range(npair):
            acc[...] += _dot(ab[2 * q][...].astype(BF16), ab[2 * q + 1][...], 1, 1)

        @pl.when(k == nk - 1)
        def _():
            dy = rscale * r_ref[...] + acc[...]
            if not has_ln:
                o_ref[...] = dy
                return
            xh = xh_ref[...]
            w = dy * g_ref[...]
            m1 = jnp.mean(w, axis=1, keepdims=True)
            m2 = jnp.mean(w * xh, axis=1, keepdims=True)
            o_ref[...] = rs_ref[...] * (w - m1 - xh * m2)

            @pl.when(i == 0)
            def _():
                dg_ref[...] = jnp.zeros_like(dg_ref)
                db_ref[...] = jnp.zeros_like(db_ref)

            dg_ref[...] += jnp.sum(dy * xh, axis=0, keepdims=True)
            db_ref[...] += jnp.sum(dy, axis=0, keepdims=True)

    row = pl.BlockSpec((tm, d), lambda i, k: (i, 0))
    vec = pl.BlockSpec((1, d), lambda i, k: (0, 0))
    in_specs, args = [], []
    for a, b in pairs:
        if b.ndim == 3:
            per = b.shape[2] // tk
            b_spec = pl.BlockSpec((None, d, tk), lambda i, k, per=per: (k // per, 0, k % per))
        else:
            b_spec = pl.BlockSpec((d, tk), lambda i, k: (0, k))
        in_specs += [pl.BlockSpec((tm, tk), lambda i, k: (i, k)), b_spec]
        args += [a, b]
    in_specs.append(row)
    args.append(resid)
    out_specs = [row]
    out_shape = [jax.ShapeDtypeStruct((t, d), F32)]
    if has_ln:
        xh, rs, gain = ln
        in_specs += [row, pl.BlockSpec((tm, 1), lambda i, k: (i, 0)), vec]
        args += [xh, rs, gain.reshape(1, d)]
        out_specs += [vec, vec]
        out_shape += [jax.ShapeDtypeStruct((1, d), F32)] * 2
    return pl.pallas_call(
        body, name=name, grid=(nt, nk), in_specs=in_specs, out_specs=out_specs, out_shape=out_shape,
        scratch_shapes=[pltpu.VMEM((tm, d), F32)],
        compiler_params=_params(2),
    )(*args)


def _mm_pe(x3, x3b, pb, wgate, bgate, wproj, *, name):
    t, d = x3.shape
    pd = pb.shape[1]
    tm = _pick(t, (512, 256, 128))
    tn = _pick(d, (512, 256, 128))

    def body(x_ref, xb_ref, p_ref, wg_ref, bg_ref, wp_ref, y_ref, yb_ref, sg_ref, e_ref):
        sg = _sigmoid(_dot(xb_ref[...], wg_ref[...], 1, 0) + bg_ref[...])
        e = _dot(p_ref[...], wp_ref[...], 1, 0)
        y = x_ref[...] + sg * e
        y_ref[...] = y
        yb_ref[...] = y.astype(BF16)
        sg_ref[...] = sg.astype(BF16)
        e_ref[...] = e.astype(BF16)

    ospec = pl.BlockSpec((tm, tn), lambda i, j: (i, j))
    ob = jax.ShapeDtypeStruct((t, d), BF16)
    return pl.pallas_call(
        body, name=name, grid=(t // tm, d // tn),
        in_specs=[ospec, pl.BlockSpec((tm, d), lambda i, j: (i, 0)), pl.BlockSpec((tm, pd), lambda i, j: (i, 0)),
                  pl.BlockSpec((d, tn), lambda i, j: (0, j)), pl.BlockSpec((1, tn), lambda i, j: (0, j)),
                  pl.BlockSpec((pd, tn), lambda i, j: (0, j))],
        out_specs=[ospec, ospec, ospec, ospec],
        out_shape=[jax.ShapeDtypeStruct((t, d), F32), ob, ob, ob],
        compiler_params=_params(2),
    )(x3, x3b, pb, wgate, bgate.reshape(1, d), wproj)


def _pe_bwd_elem(dx4, sg, e, *, name):
    t, d = dx4.shape
    tm = _pick(t, (512, 256, 128))

    def body(dx_ref, sg_ref, e_ref, dgp_ref, de_ref, db_ref):
        dx = dx_ref[...]
        s = sg_ref[...].astype(F32)
        dgp = dx * e_ref[...].astype(F32) * s * (1.0 - s)
        dgp_ref[...] = dgp.astype(BF16)
        de_ref[...] = (dx * s).astype(BF16)

        @pl.when(pl.program_id(0) == 0)
        def _():
            db_ref[...] = jnp.zeros_like(db_ref)

        db_ref[...] += jnp.sum(dgp, axis=0, keepdims=True)

    row = pl.BlockSpec((tm, d), lambda i: (i, 0))
    ob = jax.ShapeDtypeStruct((t, d), BF16)
    return pl.pallas_call(
        body, name=name, grid=(t // tm,), in_specs=[row, row, row],
        out_specs=[row, row, pl.BlockSpec((1, d), lambda i: (0, 0))],
        out_shape=[ob, ob, jax.ShapeDtypeStruct((1, d), F32)],
        compiler_params=_params(1),
    )(dx4, sg, e)


def _loss_kernel(y, target, *, name):
    t, d = y.shape
    tm = _pick(t, (512, 256, 128))

    def body(y_ref, t_ref, dy_ref, l_ref):
        diff = y_ref[...] - t_ref[...]
        dy_ref[...] = diff * (1.0 / d)

        @pl.when(pl.program_id(0) == 0)
        def _():
            l_ref[...] = jnp.zeros_like(l_ref)

        part = jnp.sum(jnp.mean(diff * diff, axis=1, keepdims=True), axis=0, keepdims=True)
        l_ref[...] += 0.5 * part

    row = pl.BlockSpec((tm, d), lambda i: (i, 0))
    return pl.pallas_call(
        body, name=name, grid=(t // tm,), in_specs=[row, row],
        out_specs=[row, pl.BlockSpec((1, 1), lambda i: (0, 0))],
        out_shape=[jax.ShapeDtypeStruct((t, d), F32), jax.ShapeDtypeStruct((1, 1), F32)],
        compiler_params=_params(1),
    )(y, target)


LRU_TM = 256


def _lru_gate_terms(r, lam):
    sp = _softplus(-lam)
    la = -LRU_C * r * sp
    a = jnp.exp(la)
    em = jnp.tanh(la) * (jnp.exp(2.0 * la) + 1.0)
    s = jnp.sqrt(-em)
    return la, a, s, sp


def _lru_fwd(hbuf, conv_w, conv_b, wa, ba, wx, bx, lam, *, name):
    t = hbuf.shape[0]
    w = LRU_WIDTH
    tm = _pick(t, (LRU_TM, 128))
    cu, cg = COL_U // w, COL_G // w
    hb = tm // SUBLANES

    def body(u_ref, up_ref, g_ref, cw_ref, cb_ref, wa_ref, ba_ref, wx_ref, bx_ref, lam_ref,
             y_ref, u_out, r_out, i_out, a_out, h_out, carry):
        i = pl.program_id(0)

        @pl.when(i == 0)
        def _():
            carry[...] = jnp.zeros_like(carry)

        prev = jnp.where(i == 0, 0.0, up_ref[...])
        u = _conv_taps(u_ref[...], prev, cw_ref[...], cb_ref[...])
        ub = u.astype(BF16)
        r = _sigmoid(_dot(ub, wa_ref[...], 1, 0) + ba_ref[...])
        ig = _sigmoid(_dot(ub, wx_ref[...], 1, 0) + bx_ref[...])
        _, a, s, _ = _lru_gate_terms(r, lam_ref[...])
        b = s * (ig * u)
        acum, hs = _scan_fwd(a, b)
        h = hs + acum * carry[0:1, :]
        carry[...] = jnp.broadcast_to(h[tm - 1:tm, :], carry.shape)
        gl, _ = _gelu_and_grad(g_ref[...])
        y_ref[...] = h * gl
        u_out[...] = u
        r_out[...] = r
        i_out[...] = ig
        a_out[...] = a
        h_out[...] = h

    row = pl.BlockSpec((tm, w), lambda i: (i, 0))
    vec = pl.BlockSpec((1, w), lambda i: (0, 0))
    mat = pl.BlockSpec((w, w), lambda i: (0, 0))
    o = jax.ShapeDtypeStruct((t, w), F32)
    return pl.pallas_call(
        body, name=name, grid=(t // tm,),
        in_specs=[pl.BlockSpec((tm, w), lambda i: (i, cu)),
                  pl.BlockSpec((SUBLANES, w), lambda i: (jnp.maximum(i * hb - 1, 0), cu)),
                  pl.BlockSpec((tm, w), lambda i: (i, cg)),
                  pl.BlockSpec((CONV_K, w), lambda i: (0, 0)), vec, mat, vec, mat, vec, vec],
        out_specs=[row] * 6, out_shape=[o] * 6,
        scratch_shapes=[pltpu.VMEM((SUBLANES, w), F32)],
        compiler_params=_params(1),
    )(hbuf, hbuf, hbuf, conv_w, conv_b, wa, ba, wx, bx, lam)


def _lru_bwd(dymix, hbuf, u, r, ig, a, h, conv_w, wa, wx, lam, *, name):
    t = hbuf.shape[0]
    w = LRU_WIDTH
    tm = _pick(t, (LRU_TM, 128))
    nb = t // tm
    cu, cg = COL_U // w, COL_G // w
    hb = tm // SUBLANES
    last8 = t // SUBLANES - 1

    def body(dy_ref, ur_ref, g_ref, u_ref, r_ref, i_ref, a_ref, an_ref, h_ref, hp_ref,
             cw_ref, wa_ref, wx_ref, lam_ref,
             dur_ref, dgr_ref, dcw_ref, dcb_ref, dwa_ref, dba_ref, dwx_ref, dbx_ref, dlam_ref,
             lcarry, dnext):
        i = pl.program_id(0)
        ib = nb - 1 - i

        @pl.when(i == 0)
        def _():
            lcarry[...] = jnp.zeros_like(lcarry)
            dnext[...] = jnp.zeros_like(dnext)
            for ref in (dcw_ref, dcb_ref, dwa_ref, dba_ref, dwx_ref, dbx_ref, dlam_ref):
                ref[...] = jnp.zeros_like(ref)

        dy = dy_ref[...]
        hh = h_ref[...]
        av = a_ref[...]
        uu = u_ref[...]
        rr = r_ref[...]
        ii = i_ref[...]
        lam_v = lam_ref[...]
        gl, dgl = _gelu_and_grad(g_ref[...])
        dgr_ref[...] = dy * hh * dgl
        dh_out = dy * gl
        a_next = _shift_up(av, 1, jnp.where(ib == nb - 1, 0.0, an_ref[...]))
        acum, ls = _scan_bwd(a_next, dh_out)
        lam_adj = ls + acum * lcarry[0:1, :]
        lcarry[...] = jnp.broadcast_to(lam_adj[0:1, :], lcarry.shape)
        h_prev = _shift_down(hh, 1, jnp.where(ib == 0, 0.0, hp_ref[...]))
        da = lam_adj * h_prev
        _, a2, s, sp = _lru_gate_terms(rr, lam_v)
        d_igu = lam_adj * s
        ds = lam_adj * ii * uu
        dla = da * a2 - ds * (a2 * a2) / s
        dr = dla * (-LRU_C * sp)
        dlam_ref[...] += jnp.sum(dla * (LRU_C * rr * _sigmoid(-lam_v)), axis=0, keepdims=True)
        dpre_r = dr * rr * (1.0 - rr)
        dpre_i = d_igu * uu * ii * (1.0 - ii)
        prb = dpre_r.astype(BF16)
        pib = dpre_i.astype(BF16)
        ub = uu.astype(BF16)
        du = d_igu * ii + _dot(prb, wa_ref[...], 1, 1) + _dot(pib, wx_ref[...], 1, 1)
        dwa_ref[...] += _dot(ub, prb, 0, 0)
        dwx_ref[...] += _dot(ub, pib, 0, 0)
        dba_ref[...] += jnp.sum(dpre_r, axis=0, keepdims=True)
        dbx_ref[...] += jnp.sum(dpre_i, axis=0, keepdims=True)
        dur, dws = _conv_taps_bwd(du, dnext[...], cw_ref[...], ur_ref[...])
        dur_ref[...] = dur
        dcw_ref[...] += dws
        dcb_ref[...] += jnp.sum(du, axis=0, keepdims=True)
        dnext[...] = du[:SUBLANES]

    def rowspec(col):
        return pl.BlockSpec((tm, w), lambda i: (nb - 1 - i, col))

    row = rowspec(0)
    nxt = pl.BlockSpec((SUBLANES, w), lambda i: (jnp.minimum((nb - i) * hb, last8), 0))
    prv = pl.BlockSpec((SUBLANES, w), lambda i: (jnp.maximum((nb - 1 - i) * hb - 1, 0), 0))
    vec = pl.BlockSpec((1, w), lambda i: (0, 0))
    mat = pl.BlockSpec((w, w), lambda i: (0, 0))
    cw = pl.BlockSpec((CONV_K, w), lambda i: (0, 0))
    o = jax.ShapeDtypeStruct((t, w), F32)
    v1 = jax.ShapeDtypeStruct((1, w), F32)
    m1 = jax.ShapeDtypeStruct((w, w), F32)
    return pl.pallas_call(
        body, name=name, grid=(nb,),
        in_specs=[rowspec(0), rowspec(cu), rowspec(cg), row, row, row, row, nxt, row, prv, cw, mat, mat, vec],
        out_specs=[row, row, cw, vec, mat, vec, mat, vec, vec],
        out_shape=[o, o, jax.ShapeDtypeStruct((CONV_K, w), F32), v1, m1, v1, m1, v1, v1],
        scratch_shapes=[pltpu.VMEM((SUBLANES, w), F32), pltpu.VMEM((SUBLANES, w), F32)],
        compiler_params=_params(1),
    )(dymix, hbuf, hbuf, u, r, ig, a, a, h, h, conv_w, wa, wx, lam)


FOX_T = 512
FOX_PREP_TM = 256


def _log_sigmoid(x):
    return jnp.minimum(x, 0.0) - jnp.log(1.0 + jnp.exp(-jnp.abs(x)))


def _fox_prep(hbuf, bf_vec, *, name):
    t = hbuf.shape[0]
    tm = _pick(t, (FOX_PREP_TM, 128))
    cs = COL_SMALL // LANES

    def body(s_ref, b_ref, eq_ref, ek_ref, carry):
        i = pl.program_id(0)

        @pl.when(i == 0)
        def _():
            carry[...] = jnp.zeros_like(carry)

        lf = _log_sigmoid(s_ref[...] + b_ref[...])
        f = _cumsum_rows(lf) + carry[0:1, :]
        carry[...] = jnp.broadcast_to(f[tm - 1:tm, :], carry.shape)
        lane = _iota((tm, LANES), 1)
        for h in range(ATT_HEADS):
            base = HEAD_DIM * (1 - h % 2)
            fh = _col(f, h)
            hi = fh.astype(BF16).astype(F32)
            mid = (fh - hi).astype(BF16).astype(F32)
            lo = fh - hi - mid
            terms = jnp.where(lane == base, hi, jnp.where(lane == base + 1, mid, jnp.where(lane == base + 2, lo, 0.0)))
            terms_k = jnp.where(lane == base + 3, -hi,
                                jnp.where(lane == base + 4, -mid, jnp.where(lane == base + 5, -lo, 0.0)))
            ones_q = ((lane >= base + 3) & (lane < base + 6)).astype(F32)
            ones_k = ((lane >= base) & (lane < base + 3)).astype(F32)
            eq_ref[:, LANES * h:LANES * (h + 1)] = (terms + ones_q).astype(BF16)
            ek_ref[:, LANES * h:LANES * (h + 1)] = (terms_k + ones_k).astype(BF16)

    ospec = pl.BlockSpec((tm, ATT_HEADS * LANES), lambda i: (i, 0))
    o = jax.ShapeDtypeStruct((t, ATT_HEADS * LANES), BF16)
    return pl.pallas_call(
        body, name=name, grid=(t // tm,),
        in_specs=[pl.BlockSpec((tm, LANES), lambda i: (i, cs)), pl.BlockSpec((1, LANES), lambda i: (0, 0))],
        out_specs=[ospec, ospec], out_shape=[o, o],
        scratch_shapes=[pltpu.VMEM((SUBLANES, LANES), F32)],
        compiler_params=_params(1),
    )(hbuf, bf_vec)


def _fox_post(dfc, hbuf, bf_vec, *, name):
    t = hbuf.shape[0]
    tm = _pick(t, (FOX_PREP_TM, 128))
    nb = t // tm
    cs = COL_SMALL // LANES

    def body(df_ref, s_ref, b_ref, o_ref, db_ref, carry):
        i = pl.program_id(0)

        @pl.when(i == 0)
        def _():
            carry[...] = jnp.zeros_like(carry)
            db_ref[...] = jnp.zeros_like(db_ref)

        dlf = _cumsum_rows(df_ref[...], reverse=True) + carry[0:1, :]
        carry[...] = jnp.broadcast_to(dlf[0:1, :], carry.shape)
        dl = dlf * _sigmoid(-(s_ref[...] + b_ref[...]))
        dl = jnp.where(_iota(dl.shape, 1) < ATT_HEADS, dl, 0.0)
        o_ref[...] = dl
        db_ref[...] += jnp.sum(dl, axis=0, keepdims=True)

    vec = pl.BlockSpec((1, LANES), lambda i: (0, 0))
    return pl.pallas_call(
        body, name=name, grid=(nb,),
        in_specs=[pl.BlockSpec((tm, LANES), lambda i: (nb - 1 - i, 0)),
                  pl.BlockSpec((tm, LANES), lambda i: (nb - 1 - i, cs)), vec],
        out_specs=[pl.BlockSpec((tm, LANES), lambda i: (nb - 1 - i, 0)), vec],
        out_shape=[jax.ShapeDtypeStruct((t, LANES), F32), jax.ShapeDtypeStruct((1, LANES), F32)],
        scratch_shapes=[pltpu.VMEM((SUBLANES, LANES), F32)],
        compiler_params=_params(1),
    )(dfc, hbuf, bf_vec)


def _fox_scores(qp, kp, eq, ek, hm, causal):
    qm = jnp.where(hm, (qp * (HEAD_DIM ** -0.5)).astype(BF16), eq)
    km = jnp.where(hm, kp.astype(BF16), ek)
    s = _dot(qm, km, 1, 1)
    if causal is not None:
        s = jnp.where(causal, s, NEG)
    return s, qm, km


def _fox_masks(i, j, tq):
    row = i * tq + _iota((tq, tq), 0)
    col = j * tq + _iota((tq, tq), 1)
    lane = _iota((1, LANES), 1)
    return col <= row, (lane < HEAD_DIM, lane >= HEAD_DIM)


def _fox_fwd(hbuf, eq, ek, *, name):
    t = hbuf.shape[0]
    w = ATT_WIDTH
    tq = _pick(t, (FOX_T, 256, 128))
    nq = t // tq
    cq, ck, cv = COL_Q // w, COL_K // w, COL_V // w

    def body(q_ref, k_ref, v_ref, eq_ref, ek_ref, o_ref, lse_ref, m_s, l_s, acc_s):
        i = pl.program_id(0)
        j = pl.program_id(1)

        @pl.when(j == 0)
        def _():
            m_s[...] = jnp.full_like(m_s, NEG)
            l_s[...] = jnp.zeros_like(l_s)
            acc_s[...] = jnp.zeros_like(acc_s)

        def step(diagonal):
            _, hms = _fox_masks(i, j, tq)
            keys_first = (j * tq + _iota((tq, tq), 0)) <= (i * tq + _iota((tq, tq), 1))
            half = _iota((LANES, 1), 0)
            hrows = (half < HEAD_DIM, half >= HEAD_DIM)
            m_all = m_s[...]
            l_all = l_s[...]
            acc_old = [acc_s[LANES * pr:LANES * (pr + 1), :] for pr in range(2)]
            m_out, l_out, acc_out = [], [], []
            for pr in range(2):
                sl = slice(LANES * pr, LANES * (pr + 1))
                qp = q_ref[:, sl]
                kp = k_ref[:, sl]
                vt = v_ref[:, sl].T.astype(BF16)
                acc = acc_old[pr]
                for hh in range(2):
                    h = 2 * pr + hh
                    hsl = slice(LANES * h, LANES * (h + 1))
                    qm = jnp.where(hms[hh], (qp * (HEAD_DIM ** -0.5)).astype(BF16), eq_ref[:, hsl])
                    km = jnp.where(hms[hh], kp.astype(BF16), ek_ref[:, hsl])
                    st = _dot(km, qm, 1, 1)
                    if diagonal:
                        st = jnp.where(keys_first, st, NEG)
                    m_old = m_all[h:h + 1, :]
                    m_new = jnp.maximum(m_old, jnp.max(st, axis=0, keepdims=True))
                    alpha = jnp.exp(m_old - m_new)
                    pt = jnp.exp(st - m_new)
                    l_out.append(alpha * l_all[h:h + 1, :] + jnp.sum(pt, axis=0, keepdims=True))
                    m_out.append(m_new)
                    pv = _dot(vt, pt.astype(BF16), 1, 0)
                    acc = jnp.where(hrows[hh], alpha * acc_old[pr] + pv, acc)
                acc_out.append(acc)
            for h in range(ATT_HEADS):
                m_s[h:h + 1, :] = m_out[h]
                l_s[h:h + 1, :] = l_out[h]
            for pr in range(2):
                acc_s[LANES * pr:LANES * (pr + 1), :] = acc_out[pr]

        @pl.when(j < i)
        def _():
            step(False)

        @pl.when(j == i)
        def _():
            step(True)
            half = _iota((LANES, 1), 0)
            l_all = l_s[...]
            for pr in range(2):
                acc = acc_s[LANES * pr:LANES * (pr + 1), :]
                o_t = jnp.where(half < HEAD_DIM, acc / l_all[2 * pr:2 * pr + 1, :], acc / l_all[2 * pr + 1:2 * pr + 2, :])
                o_ref[:, LANES * pr:LANES * (pr + 1)] = o_t.T
            lse = m_s[...] + jnp.log(l_s[...])
            lse_ref[...] = jnp.where(_iota(lse.shape, 0) < ATT_HEADS, lse, 0.0)

    return pl.pallas_call(
        body, name=name, grid=(nq, nq),
        in_specs=[pl.BlockSpec((tq, w), lambda i, j: (i, cq)),
                  pl.BlockSpec((tq, w), lambda i, j: (jnp.minimum(j, i), ck)),
                  pl.BlockSpec((tq, w), lambda i, j: (jnp.minimum(j, i), cv)),
                  pl.BlockSpec((tq, ATT_HEADS * LANES), lambda i, j: (i, 0)),
                  pl.BlockSpec((tq, ATT_HEADS * LANES), lambda i, j: (jnp.minimum(j, i), 0))],
        out_specs=[pl.BlockSpec((tq, w), lambda i, j: (i, 0)),
                   pl.BlockSpec((SUBLANES, tq), lambda i, j: (0, i))],
        out_shape=[jax.ShapeDtypeStruct((t, w), F32), jax.ShapeDtypeStruct((SUBLANES, t), F32)],
        scratch_shapes=[pltpu.VMEM((SUBLANES, tq), F32), pltpu.VMEM((SUBLANES, tq), F32),
                        pltpu.VMEM((w, tq), F32)],
        compiler_params=_params(2),
    )(hbuf, hbuf, hbuf, eq, ek)


def _fox_delta(dymix, o, *, name):
    t, w = o.shape
    tm = _pick(t, (512, 256, 128))
    cdo = ATT_WIDTH // w

    def body(do_ref, o_ref, d_ref):
        d_ref[...] = _head_reduce(do_ref[...] * o_ref[...], 0, ATT_HEADS)

    return pl.pallas_call(
        body, name=name, grid=(t // tm,),
        in_specs=[pl.BlockSpec((tm, w), lambda i: (i, cdo)), pl.BlockSpec((tm, w), lambda i: (i, 0))],
        out_specs=pl.BlockSpec((tm, LANES), lambda i: (i, 0)),
        out_shape=jax.ShapeDtypeStruct((t, LANES), F32),
        compiler_params=_params(1),
    )(dymix, o)


def _fox_bwd_kv(hbuf, eq, ek, dymix, lse_rows, delta_rows, *, name):
    t = hbuf.shape[0]
    w = ATT_WIDTH
    tq = _pick(t, (FOX_T, 256, 128))
    nq = t // tq
    cq, ck, cv = COL_Q // w, COL_K // w, COL_V // w
    cdo = ATT_WIDTH // w

    def body(q_ref, k_ref, v_ref, eq_ref, ek_ref, do_ref, lse_ref, dl_ref, dk_ref, dv_ref, dfk_ref,
             dk_s, dv_s, dfk_s):
        j = pl.program_id(0)
        i = pl.program_id(1)

        @pl.when(i == 0)
        def _():
            dk_s[...] = jnp.zeros_like(dk_s)
            dv_s[...] = jnp.zeros_like(dv_s)
            dfk_s[...] = jnp.zeros_like(dfk_s)

        def step(diagonal):
            _, hms = _fox_masks(i, j, tq)
            keys_first = (j * tq + _iota((tq, tq), 0)) <= (i * tq + _iota((tq, tq), 1))
            lse_all = lse_ref[...]
            dl_all = dl_ref[...]
            dvs, dks, dfks = [], [], []
            for pr in range(2):
                sl = slice(LANES * pr, LANES * (pr + 1))
                qp = q_ref[:, sl]
                kp = k_ref[:, sl]
                vpb = v_ref[:, sl].astype(BF16)
                dop = do_ref[:, sl]
                dv_p = jnp.zeros((tq, LANES), F32)
                dk_p = jnp.zeros((tq, LANES), F32)
                for hh in range(2):
                    h = 2 * pr + hh
                    hsl = slice(LANES * h, LANES * (h + 1))
                    qm = jnp.where(hms[hh], (qp * (HEAD_DIM ** -0.5)).astype(BF16), eq_ref[:, hsl])
                    km = jnp.where(hms[hh], kp.astype(BF16), ek_ref[:, hsl])
                    st = _dot(km, qm, 1, 1)
                    if diagonal:
                        st = jnp.where(keys_first, st, NEG)
                    pt = jnp.exp(st - lse_all[h:h + 1, :])
                    domb = jnp.where(hms[hh], dop, 0.0).astype(BF16)
                    dv_p = dv_p + _dot(pt.astype(BF16), domb, 1, 0)
                    dpt = _dot(vpb, domb, 1, 1)
                    dst = pt * (dpt - dl_all[h:h + 1, :])
                    dk_p = dk_p + jnp.where(hms[hh], _dot(dst.astype(BF16), qm, 1, 0), 0.0)
                    part = dst[:, 0:LANES]
                    for c in range(1, tq // LANES):
                        part = part + dst[:, LANES * c:LANES * (c + 1)]
                    dfks.append(part)
                dvs.append(dv_p)
                dks.append(dk_p)
            dv_s[...] += jnp.concatenate(dvs, axis=1)
            dk_s[...] += jnp.concatenate(dks, axis=1)
            for h in range(ATT_HEADS):
                dfk_s[h] += dfks[h]

        @pl.when(i > j)
        def _():
            step(False)

        @pl.when(i == j)
        def _():
            step(True)

        @pl.when(i == nq - 1)
        def _():
            dk_ref[...] = dk_s[...]
            dv_ref[...] = dv_s[...]
            lane = _iota((tq, LANES), 1)
            out = jnp.zeros((tq, LANES), F32)
            for h in range(ATT_HEADS):
                out = jnp.where(lane == h, jnp.sum(dfk_s[h], axis=1, keepdims=True), out)
            dfk_ref[...] = out

    qi = lambda j, i: jnp.maximum(i, j)
    rows = pl.BlockSpec((SUBLANES, tq), lambda j, i: (0, qi(j, i)))
    return pl.pallas_call(
        body, name=name, grid=(nq, nq),
        in_specs=[pl.BlockSpec((tq, w), lambda j, i: (qi(j, i), cq)),
                  pl.BlockSpec((tq, w), lambda j, i: (j, ck)),
                  pl.BlockSpec((tq, w), lambda j, i: (j, cv)),
                  pl.BlockSpec((tq, ATT_HEADS * LANES), lambda j, i: (qi(j, i), 0)),
                  pl.BlockSpec((tq, ATT_HEADS * LANES), lambda j, i: (j, 0)),
                  pl.BlockSpec((tq, w), lambda j, i: (qi(j, i), cdo)),
                  rows, rows],
        out_specs=[pl.BlockSpec((tq, w), lambda j, i: (j, 0)), pl.BlockSpec((tq, w), lambda j, i: (j, 0)),
                   pl.BlockSpec((tq, LANES), lambda j, i: (j, 0))],
        out_shape=[jax.ShapeDtypeStruct((t, w), F32), jax.ShapeDtypeStruct((t, w), F32),
                   jax.ShapeDtypeStruct((t, LANES), F32)],
        scratch_shapes=[pltpu.VMEM((tq, w), F32), pltpu.VMEM((tq, w), F32),
                        pltpu.VMEM((ATT_HEADS, tq, LANES), F32)],
        compiler_params=_params(2),
    )(hbuf, hbuf, hbuf, eq, ek, dymix, lse_rows, delta_rows)


def _fox_bwd_q(hbuf, eq, ek, dymix, delta, lse, *, name):
    t = hbuf.shape[0]
    w = ATT_WIDTH
    tq = _pick(t, (FOX_T, 256, 128))
    nq = t // tq
    cq, ck, cv = COL_Q // w, COL_K // w, COL_V // w
    cdo = ATT_WIDTH // w

    def body(q_ref, k_ref, v_ref, eq_ref, ek_ref, do_ref, dl_ref, lse_ref, dq_ref, dfq_ref, dq_s, dfq_s):
        i = pl.program_id(0)
        j = pl.program_id(1)

        @pl.when(j == 0)
        def _():
            dq_s[...] = jnp.zeros_like(dq_s)
            dfq_s[...] = jnp.zeros_like(dfq_s)

        def step(diagonal):
            causal, hms = _fox_masks(i, j, tq)
            dqs, dfqs = [], []
            for pr in range(2):
                sl = slice(LANES * pr, LANES * (pr + 1))
                qp = q_ref[:, sl]
                kp = k_ref[:, sl]
                vpb = v_ref[:, sl].astype(BF16)
                dop = do_ref[:, sl]
                dq_p = jnp.zeros((tq, LANES), F32)
                for hh in range(2):
                    h = 2 * pr + hh
                    hsl = slice(LANES * h, LANES * (h + 1))
                    s, _, km = _fox_scores(qp, kp, eq_ref[:, hsl], ek_ref[:, hsl], hms[hh],
                                           causal if diagonal else None)
                    p = jnp.exp(s - lse_ref[h])
                    dom = jnp.where(hms[hh], dop, 0.0)
                    dp = _dot(dom.astype(BF16), vpb, 1, 1)
                    ds = p * (dp - dl_ref[h])
                    dq = _dot(ds.astype(BF16), km, 1, 0) * (HEAD_DIM ** -0.5)
                    dq_p = dq_p + jnp.where(hms[hh], dq, 0.0)
                    dfqs.append(jnp.sum(ds, axis=1, keepdims=True))
                dqs.append(dq_p)
            dq_s[...] += jnp.concatenate(dqs, axis=1)
            for h in range(ATT_HEADS):
                dfq_s[h] += dfqs[h]

        @pl.when(j < i)
        def _():
            step(False)

        @pl.when(j == i)
        def _():
            step(True)
            dq_ref[...] = dq_s[...]
            dfq_ref[...] = dfq_s[...]

    kj = lambda i, j: jnp.minimum(j, i)
    return pl.pallas_call(
        body, name=name, grid=(nq, nq),
        in_specs=[pl.BlockSpec((tq, w), lambda i, j: (i, cq)),
                  pl.BlockSpec((tq, w), lambda i, j: (kj(i, j), ck)),
                  pl.BlockSpec((tq, w), lambda i, j: (kj(i, j), cv)),
                  pl.BlockSpec((tq, ATT_HEADS * LANES), lambda i, j: (i, 0)),
                  pl.BlockSpec((tq, ATT_HEADS * LANES), lambda i, j: (kj(i, j), 0)),
                  pl.BlockSpec((tq, w), lambda i, j: (i, cdo)),
                  pl.BlockSpec((ATT_HEADS, tq, 1), lambda i, j: (0, i, 0)),
                  pl.BlockSpec((ATT_HEADS, tq, 1), lambda i, j: (0, i, 0))],
        out_specs=[pl.BlockSpec((tq, w), lambda i, j: (i, 0)),
                   pl.BlockSpec((ATT_HEADS, tq, 1), lambda i, j: (0, i, 0))],
        out_shape=[jax.ShapeDtypeStruct((t, w), F32), jax.ShapeDtypeStruct((ATT_HEADS, t, 1), F32)],
        scratch_shapes=[pltpu.VMEM((tq, w), F32), pltpu.VMEM((ATT_HEADS, tq, 1), F32)],
        compiler_params=_params(2),
    )(hbuf, hbuf, hbuf, eq, ek, dymix, delta, lse)


GROUP_W = SSD_WIDTH // SSD_GROUPS
HEADS_PER_GROUP = SSD_HEADS // SSD_GROUPS


def _ssd_chunk_common(xr, prev8, sm, cw, cb, dtb, avec):
    c = _conv_taps(xr, prev8, cw, cb)
    sig = _sigmoid(c)
    xa = c * sig
    dt = _softplus(sm + dtb)
    a = dt * avec
    acum = _cumsum_rows(a)
    return c, sig, xa, dt, acum


def _ssd_decays(acum, g):
    n = acum.shape[0]
    atot = acum[n - 1:n, :]
    lane0 = LANE_DT + HEADS_PER_GROUP * g
    e = _head_expand(jnp.exp(acum), lane0, HEADS_PER_GROUP, GROUP_W)
    dec = _head_expand(jnp.exp(atot - acum), lane0, HEADS_PER_GROUP, GROUP_W)
    etot = _head_expand(jnp.exp(atot), lane0, HEADS_PER_GROUP, GROUP_W)
    return e, dec, etot


def _ssd_ldec(acum, acum_t, lane, tril):
    return jnp.exp(jnp.where(tril, _col(acum, lane) - _row(acum_t, lane), NEG))


def _ssd_fwd(hbuf, conv_w, conv_b, dtb_vec, a_vec, d_exp, norm_g, *, name):
    t = hbuf.shape[0]
    L = SSD_CHUNK
    nc = t // L
    hb = L // SUBLANES
    cs = COL_SMALL // LANES
    cz = COL_Z // SSD_WIDTH

    def body(x_ref, xp_ref, z_ref, s_ref, cw_ref, cb_ref, dtb_ref, av_ref, dx_ref, ng_ref,
             yc_ref, y_ref, st_ref, state):
        i = pl.program_id(0)

        @pl.when(i == 0)
        def _():
            state[...] = jnp.zeros_like(state)

        prev = jnp.where(i == 0, 0.0, xp_ref[...])
        _, _, xa, dt, acum = _ssd_chunk_common(x_ref[...], prev, s_ref[...], cw_ref[...], cb_ref[...],
                                               dtb_ref[...], av_ref[...])
        acum_t = acum.T
        xs = xa[:, :SSD_WIDTH]
        xdt = xs * _head_expand(dt, LANE_DT, SSD_HEADS, SSD_WIDTH)
        tril = _iota((L, L), 0) >= _iota((L, L), 1)
        lane = _iota((1, LANES), 1)
        ys = []
        for g in range(SSD_GROUPS):
            bg = xa[:, SSD_WIDTH + SSD_STATE * g:SSD_WIDTH + SSD_STATE * (g + 1)].astype(BF16)
            cg = xa[:, SSD_WIDTH + SSD_STATE * (SSD_GROUPS + g):SSD_WIDTH + SSD_STATE * (SSD_GROUPS + g + 1)].astype(BF16)
            gm = _dot(cg, bg, 1, 1)
            e, dec, etot = _ssd_decays(acum, g)
            s_in = state[g]
            st_ref[0, g] = s_in
            xg = xdt[:, GROUP_W * g:GROUP_W * (g + 1)]
            y_off = e * _dot(cg, s_in.astype(BF16), 1, 0)
            state[g] = etot * s_in + _dot(bg, (dec * xg).astype(BF16), 0, 0)
            for pr in range(2):
                xp = xg[:, LANES * pr:LANES * (pr + 1)].astype(BF16)
                outs = []
                for hh in range(2):
                    h = HEADS_PER_GROUP * g + 2 * pr + hh
                    m = gm * _ssd_ldec(acum, acum_t, LANE_DT + h, tril)
                    outs.append(_dot(m.astype(BF16), xp, 1, 0))
                ys.append(jnp.where(lane < HEAD_DIM, outs[0], outs[1]) + y_off[:, LANES * pr:LANES * (pr + 1)])
        y = jnp.concatenate(ys, axis=1)
        y_ref[...] = y
        yd = y + dx_ref[...] * xs
        zz = z_ref[...]
        y2 = yd * zz * _sigmoid(zz)
        ng = ng_ref[...]
        outs = []
        for g in range(SSD_GROUPS):
            yg = y2[:, GROUP_W * g:GROUP_W * (g + 1)]
            rs = lax.rsqrt(jnp.mean(yg * yg, axis=1, keepdims=True) + RMS_EPS)
            outs.append(yg * rs * ng[:, GROUP_W * g:GROUP_W * (g + 1)])
        yc_ref[...] = jnp.concatenate(outs, axis=1)

    cdim = SSD_CONV_DIM
    vecc = pl.BlockSpec((1, cdim), lambda i: (0, 0))
    vecl = pl.BlockSpec((1, LANES), lambda i: (0, 0))
    vecw = pl.BlockSpec((1, SSD_WIDTH), lambda i: (0, 0))
    roww = pl.BlockSpec((L, SSD_WIDTH), lambda i: (i, 0))
    return pl.pallas_call(
        body, name=name, grid=(nc,),
        in_specs=[pl.BlockSpec((L, cdim), lambda i: (i, 0)),
                  pl.BlockSpec((SUBLANES, cdim), lambda i: (jnp.maximum(i * hb - 1, 0), 0)),
                  pl.BlockSpec((L, SSD_WIDTH), lambda i: (i, cz)),
                  pl.BlockSpec((L, LANES), lambda i: (i, cs)),
                  pl.BlockSpec((CONV_K, cdim), lambda i: (0, 0)), vecc, vecl, vecl, vecw, vecw],
        out_specs=[roww, roww, pl.BlockSpec((1, SSD_GROUPS, SSD_STATE, GROUP_W), lambda i: (i, 0, 0, 0))],
        out_shape=[jax.ShapeDtypeStruct((t, SSD_WIDTH), F32), jax.ShapeDtypeStruct((t, SSD_WIDTH), F32),
                   jax.ShapeDtypeStruct((nc, SSD_GROUPS, SSD_STATE, GROUP_W), F32)],
        scratch_shapes=[pltpu.VMEM((SSD_GROUPS, SSD_STATE, GROUP_W), F32)],
        compiler_params=_params(1),
    )(hbuf, hbuf, hbuf, hbuf, conv_w, conv_b, dtb_vec, a_vec, d_exp, norm_g)


def _ssd_bwd(dymix, hbuf, y_ssd, states, conv_w, conv_b, dtb_vec, a_vec, d_exp, norm_g, *, name):
    t = hbuf.shape[0]
    L = SSD_CHUNK
    nc = t // L
    hb = L // SUBLANES
    cs = COL_SMALL // LANES
    cz = COL_Z // SSD_WIDTH
    cdy = (LRU_WIDTH + ATT_WIDTH) // SSD_WIDTH
    cdim = SSD_CONV_DIM

    def body(dyc_ref, x_ref, xp_ref, z_ref, s_ref, y_ref, st_ref, cw_ref, cb_ref, dtb_ref, av_ref, dx_ref, ng_ref,
             dxr_ref, dz_ref, dsm_ref, dng_ref, dd_ref, da_ref, ddtb_ref, dcw_ref, dcb_ref,
             dstate, dnext):
        i = pl.program_id(0)
        ic = nc - 1 - i

        @pl.when(i == 0)
        def _():
            dstate[...] = jnp.zeros_like(dstate)
            dnext[...] = jnp.zeros_like(dnext)
            for ref in (dng_ref, dd_ref, da_ref, ddtb_ref, dcw_ref, dcb_ref):
                ref[...] = jnp.zeros_like(ref)

        xr = x_ref[...]
        sm = s_ref[...]
        prev = jnp.where(ic == 0, 0.0, xp_ref[...])
        avec = av_ref[...]
        c, sig, xa, dt, acum = _ssd_chunk_common(xr, prev, sm, cw_ref[...], cb_ref[...], dtb_ref[...], avec)
        acum_t = acum.T
        xs = xa[:, :SSD_WIDTH]
        dtx = _head_expand(dt, LANE_DT, SSD_HEADS, SSD_WIDTH)
        xdt = xs * dtx
        tril = _iota((L, L), 0) >= _iota((L, L), 1)
        lane = _iota((1, LANES), 1)
        hmasks = (lane < HEAD_DIM, lane >= HEAD_DIM)

        y = y_ref[...]
        dexp = dx_ref[...]
        yd = y + dexp * xs
        zz = z_ref[...]
        sz = _sigmoid(zz)
        siluz = zz * sz
        y2 = yd * siluz
        ng = ng_ref[...]
        dyc = dyc_ref[...]
        dy2s, dngs = [], []
        for g in range(SSD_GROUPS):
            sl = slice(GROUP_W * g, GROUP_W * (g + 1))
            yg = y2[:, sl]
            rs = lax.rsqrt(jnp.mean(yg * yg, axis=1, keepdims=True) + RMS_EPS)
            wv = dyc[:, sl] * ng[:, sl]
            dngs.append(jnp.sum(dyc[:, sl] * yg * rs, axis=0, keepdims=True))
            dy2s.append(rs * wv - yg * (rs * rs * rs) * jnp.mean(wv * yg, axis=1, keepdims=True))
        dy2 = jnp.concatenate(dy2s, axis=1)
        dng_ref[...] += jnp.concatenate(dngs, axis=1)
        dz_ref[...] = dy2 * yd * (sz * (1.0 + zz * (1.0 - sz)))
        dy = dy2 * siluz
        dd_ref[...] += jnp.sum(dy * xs, axis=0, keepdims=True)

        dxs, dbs, dcs = [], [], []
        datot = jnp.zeros((1, LANES), F32)
        lanes = _iota((L, LANES), 1)
        dacum = jnp.zeros((L, LANES), F32)
        for g in range(SSD_GROUPS):
            sl = slice(GROUP_W * g, GROUP_W * (g + 1))
            bg = xa[:, SSD_WIDTH + SSD_STATE * g:SSD_WIDTH + SSD_STATE * (g + 1)].astype(BF16)
            cg = xa[:, SSD_WIDTH + SSD_STATE * (SSD_GROUPS + g):SSD_WIDTH + SSD_STATE * (SSD_GROUPS + g + 1)].astype(BF16)
            gm = _dot(cg, bg, 1, 1)
            e, dec, etot = _ssd_decays(acum, g)
            s_in = st_ref[0, g]
            ds_out = dstate[g]
            dyg = dy[:, sl]
            xg = xdt[:, sl]
            edy = (e * dyg).astype(BF16)
            dstate[g] = etot * ds_out + _dot(cg, edy, 0, 0)
            dx_state = dec * _dot(bg, ds_out.astype(BF16), 1, 0)
            y_off = e * _dot(cg, s_in.astype(BF16), 1, 0)
            dacum = dacum + _head_reduce_group(dyg * y_off - xg * dx_state, g)
            dc_off = _dot(edy, s_in.astype(BF16), 1, 1)
            db_state = _dot((dec * xg).astype(BF16), ds_out.astype(BF16), 1, 1)
            dgsum = jnp.zeros((L, L), F32)
            dx_pairs = []
            for pr in range(2):
                psl = slice(LANES * pr, LANES * (pr + 1))
                xp = xg[:, psl]
                dyp = dyg[:, psl]
                dx_pair = jnp.zeros((L, LANES), F32)
                for hh in range(2):
                    h = HEADS_PER_GROUP * g + 2 * pr + hh
                    ldec = _ssd_ldec(acum, acum_t, LANE_DT + h, tril)
                    dym = jnp.where(hmasks[hh], dyp, 0.0).astype(BF16)
                    xm = jnp.where(hmasks[hh], xp, 0.0).astype(BF16)
                    dx_pair = dx_pair + _dot((gm * ldec).astype(BF16), dym, 0, 0)
                    dml = _dot(dym, xm, 1, 1) * ldec
                    dgsum = dgsum + dml
                    qm = dml * gm
                    seg = jnp.sum(qm, axis=1, keepdims=True) - jnp.sum(qm.T, axis=1, keepdims=True)
                    dacum = dacum + jnp.where(lanes == LANE_DT + h, seg, 0.0)
                dx_pairs.append(dx_pair)
            dgb = dgsum.astype(BF16)
            dcs.append(_dot(dgb, bg, 1, 0) + dc_off)
            dbs.append(_dot(dgb, cg, 0, 0) + db_state)
            dxg = jnp.concatenate(dx_pairs, axis=1) + dx_state
            dxs.append(dxg)
            v = jnp.sum(dx_state * xg, axis=0, keepdims=True) + etot * jnp.sum(ds_out * s_in, axis=0, keepdims=True)
            datot = datot + _head_reduce_row(v, LANE_DT + HEADS_PER_GROUP * g, HEADS_PER_GROUP)
        dx = jnp.concatenate(dxs, axis=1)
        dacum = dacum + jnp.where(_iota((L, LANES), 0) == L - 1, datot, 0.0)
        da = _cumsum_rows(dacum, reverse=True)
        ddt = da * avec + _head_reduce(dx * xs, LANE_DT, SSD_HEADS)
        da_ref[...] += jnp.sum(da * dt, axis=0, keepdims=True)
        ddt_raw = ddt * _sigmoid(sm + dtb_ref[...])
        ddt_raw = jnp.where((lanes >= LANE_DT) & (lanes < LANE_DT + SSD_HEADS), ddt_raw, 0.0)
        dsm_ref[...] = ddt_raw
        ddtb_ref[...] += jnp.sum(ddt_raw, axis=0, keepdims=True)
        dxs_total = dx * dtx + dexp * dy
        dxa = jnp.concatenate([dxs_total] + dbs + dcs, axis=1)
        dc = dxa * (sig * (1.0 + c * (1.0 - sig)))
        dxr, dws = _conv_taps_bwd(dc, dnext[...], cw_ref[...], xr)
        dxr_ref[...] = dxr
        dcw_ref[...] += dws
        dcb_ref[...] += jnp.sum(dc, axis=0, keepdims=True)
        dnext[...] = dc[:SUBLANES]

    rev = lambda i: nc - 1 - i
    vecc = pl.BlockSpec((1, cdim), lambda i: (0, 0))
    vecl = pl.BlockSpec((1, LANES), lambda i: (0, 0))
    vecw = pl.BlockSpec((1, SSD_WIDTH), lambda i: (0, 0))
    cwspec = pl.BlockSpec((CONV_K, cdim), lambda i: (0, 0))
    roww = pl.BlockSpec((L, SSD_WIDTH), lambda i: (rev(i), 0))
    return pl.pallas_call(
        body, name=name, grid=(nc,),
        in_specs=[pl.BlockSpec((L, SSD_WIDTH), lambda i: (rev(i), cdy)),
                  pl.BlockSpec((L, cdim), lambda i: (rev(i), 0)),
                  pl.BlockSpec((SUBLANES, cdim), lambda i: (jnp.maximum(rev(i) * hb - 1, 0), 0)),
                  pl.BlockSpec((L, SSD_WIDTH), lambda i: (rev(i), cz)),
                  pl.BlockSpec((L, LANES), lambda i: (rev(i), cs)),
                  roww,
                  pl.BlockSpec((1, SSD_GROUPS, SSD_STATE, GROUP_W), lambda i: (rev(i), 0, 0, 0)),
                  cwspec, vecc, vecl, vecl, vecw, vecw],
        out_specs=[pl.BlockSpec((L, cdim), lambda i: (rev(i), 0)), roww,
                   pl.BlockSpec((L, LANES), lambda i: (rev(i), 0)),
                   vecw, vecw, vecl, vecl, cwspec, vecc],
        out_shape=[jax.ShapeDtypeStruct((t, cdim), F32), jax.ShapeDtypeStruct((t, SSD_WIDTH), F32),
                   jax.ShapeDtypeStruct((t, LANES), F32),
                   jax.ShapeDtypeStruct((1, SSD_WIDTH), F32), jax.ShapeDtypeStruct((1, SSD_WIDTH), F32),
                   jax.ShapeDtypeStruct((1, LANES), F32), jax.ShapeDtypeStruct((1, LANES), F32),
                   jax.ShapeDtypeStruct((CONV_K, cdim), F32), jax.ShapeDtypeStruct((1, cdim), F32)],
        scratch_shapes=[pltpu.VMEM((SSD_GROUPS, SSD_STATE, GROUP_W), F32), pltpu.VMEM((SUBLANES, cdim), F32)],
        compiler_params=_params(1),
    )(dymix, hbuf, hbuf, hbuf, hbuf, y_ssd, states, conv_w, conv_b, dtb_vec, a_vec, d_exp, norm_g)


def _head_reduce_group(x, g):
    return _head_reduce(x, LANE_DT + HEADS_PER_GROUP * g, HEADS_PER_GROUP)


def _head_reduce_row(v, lane0, nheads):
    colhead = _iota(v.shape, 1) // HEAD_DIM
    lane = _iota((1, LANES), 1)
    out = jnp.zeros((1, LANES), F32)
    for h in range(nheads):
        s = jnp.sum(jnp.where(colhead == h, v, 0.0), axis=1, keepdims=True)
        out = jnp.where(lane == lane0 + h, s, out)
    return out


def _exchange(inps, axes, *, mode, local=True, name):
    n = 2 ** len(axes)
    counts, out_shapes = [], []
    for a in inps:
        if mode == "gather":
            cnt, rest = a.shape[0], a.shape[1:]
        elif mode == "gather_half":
            cnt, rest = a.shape[0] // 2, a.shape[1:]
        elif mode == "a2a":
            cnt, rest = a.shape[1], a.shape[2:]
        else:
            cnt, rest = a.shape[0], a.shape[2:]
        counts.append(cnt)
        out_shapes.append(jax.ShapeDtypeStruct((n, cnt) + tuple(rest), a.dtype))
    units = sum(counts)
    na = len(inps)

    def body(*refs):
        in_refs, out_refs = refs[:na], refs[na:2 * na]
        send_sems, recv_sems, local_sems = refs[2 * na:]
        pos = {ax: lax.axis_index(ax) for ax in MESH_AXES}

        def slot_of(coord):
            s = 0
            for ax in axes:
                s = s * 2 + coord[ax]
            return s

        def src(a, it, slot):
            if mode == "gather":
                return in_refs[a].at[it]
            if mode == "gather_half":
                return in_refs[a].at[pos["c"] * counts[a] + it]
            if mode == "a2a":
                return in_refs[a].at[slot, it]
            return in_refs[a].at[it, slot]

        me = slot_of(pos)
        copies = []
        unit = 0
        for a in range(na):
            for it in range(counts[a]):
                if local:
                    cp = pltpu.make_async_copy(src(a, it, me), out_refs[a].at[me, it], local_sems.at[unit])
                    cp.start()
                    copies.append(cp)
                for delta in range(1, n):
                    coord = dict(pos)
                    for b, ax in enumerate(reversed(axes)):
                        if (delta >> b) & 1:
                            coord[ax] = 1 - pos[ax]
                    k = unit * (n - 1) + delta - 1
                    cp = pltpu.make_async_remote_copy(
                        src_ref=src(a, it, slot_of(coord)), dst_ref=out_refs[a].at[me, it],
                        send_sem=send_sems.at[k], recv_sem=recv_sems.at[k],
                        device_id=(coord["x"], coord["y"], coord["c"]), device_id_type=pl.DeviceIdType.MESH)
                    cp.start()
                    copies.append(cp)
                unit += 1
        for cp in copies:
            cp.wait()

    any_spec = pl.BlockSpec(memory_space=pl.ANY)
    return pl.pallas_call(
        body, name=name,
        in_specs=[any_spec] * na, out_specs=[any_spec] * na, out_shape=out_shapes,
        scratch_shapes=[pltpu.SemaphoreType.DMA((units * (n - 1),)), pltpu.SemaphoreType.DMA((units * (n - 1),)),
                        pltpu.SemaphoreType.DMA((units,))],
    )(*inps)


def _sum_slots(buf, out_dtype, *, name):
    n, rows, cols = buf.shape
    tm = _pick(rows, (512, 256, 128, 8))
    if rows % tm:
        tm = rows

    def body(b_ref, o_ref):
        acc = b_ref[0].astype(F32)
        for s in range(1, n):
            acc = acc + b_ref[s].astype(F32)
        o_ref[...] = acc.astype(out_dtype)

    return pl.pallas_call(
        body, name=name, grid=(pl.cdiv(rows, tm),),
        in_specs=[pl.BlockSpec((n, tm, cols), lambda i: (0, i, 0))],
        out_specs=pl.BlockSpec((tm, cols), lambda i: (i, 0)),
        out_shape=jax.ShapeDtypeStruct((rows, cols), out_dtype),
        compiler_params=_params(1),
    )(buf)


def _adamw(w, g, m, v, *, name):
    shape = w.shape
    cols = shape[-1]
    rows = w.size // cols
    w2, g2, m2, v2 = (a.reshape(rows, cols) for a in (w, g, m, v))
    tm = _pick(rows, (256, 128, 64, 32, 16, 8))
    if rows % tm:
        tm = rows
    bc1 = 1.0 - ADAM_B1 ** ADAM_STEP
    bc2 = 1.0 - ADAM_B2 ** ADAM_STEP

    def body(w_ref, g_ref, m_ref, v_ref, d_ref, nm_ref, nv_ref):
        gg = g_ref[...]
        mm = ADAM_B1 * m_ref[...] + (1.0 - ADAM_B1) * gg
        vv = ADAM_B2 * v_ref[...] + (1.0 - ADAM_B2) * (gg * gg)
        m_hat = mm / bc1
        v_hat = vv / bc2
        d_ref[...] = -ADAM_LR * (m_hat / (jnp.sqrt(v_hat) + ADAM_EPS) + ADAM_WD * w_ref[...])
        nm_ref[...] = mm
        nv_ref[...] = vv

    spec = pl.BlockSpec((tm, cols), lambda i: (i, 0))
    o = jax.ShapeDtypeStruct((rows, cols), F32)
    outs = pl.pallas_call(
        body, name=name, grid=(rows // tm,), in_specs=[spec] * 4, out_specs=[spec] * 3, out_shape=[o] * 3,
        compiler_params=_params(1),
    )(w2, g2, m2, v2)
    return tuple(a.reshape(shape) for a in outs)


def _layer_fwd(li, x, xb, pb, W):
    nm = lambda s: f"l{li}_{s}"
    sv = {"x_in_b": xb}
    g1, u1, a1 = _mm_swiglu(xb, W["ffn1_wg"], W["ffn1_wu"], name=nm("ffn1_up"))
    x1, x1b, xh1, rs1 = _mm_ln(a1, W["ffn1_wd"], x, W["ln1_g"], W["ln1_b"], rscale=ALPHA, mscale=0.5, name=nm("ffn1_down_ln"))
    hbuf = _mm(x1b, W["w_in_p"], name=nm("in_proj"))
    ya, lu, lr, lig, la, lh = _lru_fwd(hbuf, W["lru_conv_w"], W["lru_conv_b"], W["lru_wa_bd"], W["lru_ba"],
                                       W["lru_wx_bd"], W["lru_bx"], W["lru_lambda"], name=nm("lru_fwd"))
    eq, ek = _fox_prep(hbuf, W["fox_bf_vec"], name=nm("fox_prep"))
    yb, lse_rows = _fox_fwd(hbuf, eq, ek, name=nm("fox_fwd"))
    lse = lse_rows[:ATT_HEADS, :, None]
    yc, yssd, states = _ssd_fwd(hbuf, W["ssd_conv_w"], W["ssd_conv_b"], W["ssd_dtb_vec"], W["ssd_a_vec"],
                                W["ssd_d_exp"], W["ssd_norm_g"], name=nm("ssd_fwd"))
    ymix = jnp.concatenate([ya, yb, yc], axis=1).astype(BF16)
    x2, x2b, xh2, rs2 = _mm_ln(ymix, W["w_out"], x1, W["ln2_g"], W["ln2_b"], rscale=ALPHA, mscale=1.0, name=nm("out_proj_ln"))
    g2, u2, a2 = _mm_swiglu(x2b, W["ffn2_wg"], W["ffn2_wu"], name=nm("ffn2_up"))
    x3, x3b, xh3, rs3 = _mm_ln(a2, W["ffn2_wd"], x2, W["ln3_g"], W["ln3_b"], rscale=ALPHA, mscale=0.5, name=nm("ffn2_down_ln"))
    x4, x4b, sg, e = _mm_pe(x3, x3b, pb, W["pe_gate_w"], W["pe_gate_b"], W["pe_proj"], name=nm("ple"))
    sv.update(g1=g1, u1=u1, a1=a1, x1b=x1b, xh1=xh1, rs1=rs1, hbuf=hbuf, lu=lu, lr=lr, lig=lig, la=la, lh=lh,
              eq=eq, ek=ek, lse_rows=lse_rows, yb=yb, lse=lse, yssd=yssd, states=states, ymix=ymix, x2b=x2b, xh2=xh2, rs2=rs2,
              g2=g2, u2=u2, a2=a2, x3b=x3b, xh3=xh3, rs3=rs3, sg=sg, e=e, pb=pb)
    return x4, x4b, sv


def _layer_bwd(li, dx4, sv, W):
    nm = lambda s: f"l{li}_{s}"
    G = {}
    dgp, de, dbg = _pe_bwd_elem(dx4, sv["sg"], sv["e"], name=nm("ple_bwd"))
    G["pe_gate_b"] = dbg
    G["pe_gate_w"] = _mm(sv["x3b"], dgp, ta=True, out_dtype=BF16, name=nm("d_pe_gate_w"))
    G["pe_proj"] = _mm(sv["pb"], de, ta=True, out_dtype=BF16, chip_cols=True, name=nm("d_pe_proj"))
    dr3, G["ln3_g"], G["ln3_b"] = _bwd_proj([(dgp, W["pe_gate_w"])], dx4, rscale=1.0,
                                            ln=(sv["xh3"], sv["rs3"], W["ln3_g"]), name=nm("ln3_bwd"))
    G["ffn2_wd"] = _mm(sv["a2"], dr3, ta=True, scale=0.5, out_dtype=BF16, name=nm("d_ffn2_wd"))
    dg2, du2 = _mm_swiglu_bwd(dr3, W["ffn2_wd"], sv["g2"], sv["u2"], scale=0.5, name=nm("ffn2_act_bwd"))
    G["ffn2_wg"] = _mm(sv["x2b"], dg2, ta=True, out_dtype=BF16, chip_cols=True, name=nm("d_ffn2_wg"))
    G["ffn2_wu"] = _mm(sv["x2b"], du2, ta=True, out_dtype=BF16, chip_cols=True, name=nm("d_ffn2_wu"))
    dr2, G["ln2_g"], G["ln2_b"] = _bwd_proj([(dg2, W["ffn2_wg"]), (du2, W["ffn2_wu"])], dr3, rscale=ALPHA,
                                            ln=(sv["xh2"], sv["rs2"], W["ln2_g"]), name=nm("ln2_bwd"))
    G["w_out"] = _mm(sv["ymix"], dr2, ta=True, out_dtype=BF16, name=nm("d_w_out"))
    dymix = _mm(dr2, W["w_out"], tb=True, name=nm("d_ymix"))
    hbuf = sv["hbuf"]
    (dur, dgr, G["lru_conv_w"], G["lru_conv_b"], G["lru_wa_bd"], G["lru_ba"], G["lru_wx_bd"], G["lru_bx"],
     G["lru_lambda"]) = _lru_bwd(dymix, hbuf, sv["lu"], sv["lr"], sv["lig"], sv["la"], sv["lh"],
                                 W["lru_conv_w"], W["lru_wa_bd"], W["lru_wx_bd"], W["lru_lambda"], name=nm("lru_bwd"))
    delta = _fox_delta(dymix, sv["yb"], name=nm("fox_delta"))
    delta_rows = jnp.pad(delta[:, :ATT_HEADS].T, ((0, SUBLANES - ATT_HEADS), (0, 0)))
    dk, dv, dfk = _fox_bwd_kv(hbuf, sv["eq"], sv["ek"], dymix, sv["lse_rows"], delta_rows, name=nm("fox_bwd_kv"))
    dq, dfq = _fox_bwd_q(hbuf, sv["eq"], sv["ek"], dymix, delta[:, :ATT_HEADS].T[:, :, None], sv["lse"],
                         name=nm("fox_bwd_q"))
    dfc = jnp.pad(dfq[:, :, 0].T, ((0, 0), (0, LANES - ATT_HEADS))) - dfk
    dsm_f, G["fox_bf_vec"] = _fox_post(dfc, hbuf, W["fox_bf_vec"], name=nm("fox_post"))
    (dxr, dz, dsm_dt, G["ssd_norm_g"], G["ssd_d_exp"], G["ssd_a_vec"], G["ssd_dtb_vec"], G["ssd_conv_w"],
     G["ssd_conv_b"]) = _ssd_bwd(dymix, hbuf, sv["yssd"], sv["states"], W["ssd_conv_w"], W["ssd_conv_b"],
                                 W["ssd_dtb_vec"], W["ssd_a_vec"], W["ssd_d_exp"], W["ssd_norm_g"], name=nm("ssd_bwd"))
    t = dx4.shape[0]
    dh = jnp.concatenate([dxr.astype(BF16), dz.astype(BF16), dur.astype(BF16), dgr.astype(BF16), dq.astype(BF16),
                          dk.astype(BF16), dv.astype(BF16), (dsm_f + dsm_dt).astype(BF16),
                          jnp.zeros((t, H_WIDTH - COL_SMALL - LANES), BF16)], axis=1)
    G["w_in_p"] = _mm(sv["x1b"], dh, ta=True, name=nm("d_w_in"))
    dr1, G["ln1_g"], G["ln1_b"] = _bwd_proj([(dh, W["w_in_p"])], dr2, rscale=ALPHA,
                                            ln=(sv["xh1"], sv["rs1"], W["ln1_g"]), name=nm("ln1_bwd"))
    G["ffn1_wd"] = _mm(sv["a1"], dr1, ta=True, scale=0.5, out_dtype=BF16, name=nm("d_ffn1_wd"))
    dg1, du1 = _mm_swiglu_bwd(dr1, W["ffn1_wd"], sv["g1"], sv["u1"], scale=0.5, name=nm("ffn1_act_bwd"))
    G["ffn1_wg"] = _mm(sv["x_in_b"], dg1, ta=True, out_dtype=BF16, chip_cols=True, name=nm("d_ffn1_wg"))
    G["ffn1_wu"] = _mm(sv["x_in_b"], du1, ta=True, out_dtype=BF16, chip_cols=True, name=nm("d_ffn1_wu"))
    (dx_in,) = _bwd_proj([(dg1, W["ffn1_wg"]), (du1, W["ffn1_wu"])], dr1, rscale=ALPHA, ln=None, name=nm("x_in_bwd"))
    return dx_in, G


def _block_diag(w):
    n, b, _ = w.shape
    eye = jnp.eye(n, dtype=w.dtype)
    return (eye[:, None, :, None] * w[:, :, None, :]).reshape(n * b, n * b)


def _block_diag_extract(m):
    n, b = LRU_HEADS, HEAD_DIM
    return jnp.stack([m[b * i:b * (i + 1), b * i:b * (i + 1)] for i in range(n)])


def _lane_vec(v, lane0):
    return jnp.pad(v.astype(F32), (lane0, LANES - lane0 - v.shape[0])).reshape(1, LANES)


def _w_in_permute(w):
    d = w.shape[0]
    z = lambda n: jnp.zeros((d, n), w.dtype)
    return jnp.concatenate([w[:, 1796:2820], w[:, 1284:1796], w[:, 0:512], w[:, 512:1280],
                            w[:, 1280:1284], w[:, 2820:2828], z(LANES - 12), z(H_WIDTH - COL_SMALL - LANES)], axis=1)


def _w_in_unpermute(wp):
    return jnp.concatenate([wp[:, COL_U:COL_Q], wp[:, COL_Q:COL_SMALL], wp[:, COL_SMALL:COL_SMALL + 4],
                            wp[:, COL_Z:COL_U], wp[:, COL_XBC:COL_Z], wp[:, COL_SMALL + 4:COL_SMALL + 12]], axis=1)


def _layer_weights(li, chipw, small):
    g = lambda n: small[n][li]
    W = {n: g(n) for n in ("ln1_g", "ln1_b", "ln2_g", "ln2_b", "ln3_g", "ln3_b", "pe_gate_b", "lru_conv_w",
                           "ssd_conv_w")}
    for n in ("ffn1_wg", "ffn1_wu", "ffn2_wg", "ffn2_wu"):
        W[n] = chipw[n]
    for n in ("ffn1_wd", "ffn2_wd", "w_out", "pe_gate_w"):
        W[n] = chipw[n].reshape(-1, D_MODEL)
    W["pe_proj"] = jnp.moveaxis(chipw["pe_proj"], 0, 1).reshape(PLE_DIM, D_MODEL)
    w_in = jnp.moveaxis(chipw["w_in"][:, :, :IN_WIDTH // N_CHIPS], 0, 1).reshape(D_MODEL, IN_WIDTH)
    W["w_in_p"] = _w_in_permute(w_in)
    for n in ("lru_conv_b", "lru_ba", "lru_bx", "lru_lambda", "ssd_conv_b", "ssd_norm_g"):
        W[n] = g(n).reshape(1, -1)
    W["lru_wa_bd"] = _block_diag(g("lru_wa")).astype(BF16)
    W["lru_wx_bd"] = _block_diag(g("lru_wx")).astype(BF16)
    W["fox_bf_vec"] = _lane_vec(g("fox_bf"), LANE_F)
    W["ssd_dtb_vec"] = _lane_vec(g("ssd_dt_bias"), LANE_DT)
    W["ssd_a_vec"] = _lane_vec(-jnp.exp(g("ssd_a_log")), LANE_DT)
    W["ssd_d_exp"] = jnp.repeat(g("ssd_d"), HEAD_DIM).reshape(1, SSD_WIDTH)
    return W


def _layer_big_grads_by_chip(G):
    out = {n: G[n] for n in ("ffn1_wg", "ffn1_wu", "ffn2_wg", "ffn2_wu", "pe_proj")}
    for n in ("ffn1_wd", "ffn2_wd", "w_out", "pe_gate_w"):
        out[n] = G[n].reshape(N_CHIPS, -1, D_MODEL)
    share = IN_WIDTH // N_CHIPS
    d_w_in = jnp.moveaxis(_w_in_unpermute(G["w_in_p"]).reshape(D_MODEL, N_CHIPS, share), 1, 0)
    out["w_in"] = jnp.pad(d_w_in.astype(BF16), ((0, 0), (0, 0), (0, SHARE - share)))
    return out


def _layer_small_grads(G, W):
    out = {n: G[n] for n in ("lru_conv_w", "ssd_conv_w")}
    for n in ("ln1_g", "ln1_b", "ln2_g", "ln2_b", "ln3_g", "ln3_b", "pe_gate_b", "lru_conv_b", "lru_ba", "lru_bx",
              "lru_lambda", "ssd_conv_b", "ssd_norm_g"):
        out[n] = G[n].reshape(-1)
    out["lru_wa"] = _block_diag_extract(G["lru_wa_bd"])
    out["lru_wx"] = _block_diag_extract(G["lru_wx_bd"])
    out["fox_bf"] = G["fox_bf_vec"][0, LANE_F:LANE_F + ATT_HEADS]
    out["ssd_dt_bias"] = G["ssd_dtb_vec"][0, LANE_DT:LANE_DT + SSD_HEADS]
    out["ssd_a_log"] = G["ssd_a_vec"][0, LANE_DT:LANE_DT + SSD_HEADS] * W["ssd_a_vec"][0, LANE_DT:LANE_DT + SSD_HEADS]
    out["ssd_d"] = G["ssd_d_exp"].reshape(SSD_HEADS, HEAD_DIM).sum(axis=1)
    return out


def _local_step(x, p, target, Ws):
    saves = []
    xb = x.astype(BF16)
    for li in range(DEPTH):
        x, xb, sv = _layer_fwd(li, x, xb, p[li].astype(BF16), Ws[li])
        saves.append(sv)
    dx, loss = _loss_kernel(x, target, name="loss")
    big = [None] * DEPTH
    small = [None] * DEPTH
    for li in reversed(range(DEPTH)):
        dx, G = _layer_bwd(li, dx, saves[li], Ws[li])
        big[li] = _layer_big_grads_by_chip(G)
        small[li] = _layer_small_grads(G, Ws[li])
    stacked = {n: jnp.stack([small[li][n] for li in range(DEPTH)]) for n in small[0]}
    return loss, dx, big, stacked


WEIGHTS = ['ln1_g', 'ln1_b', 'ffn1_wg', 'ffn1_wu', 'ffn1_wd', 'w_in', 'lru_conv_w', 'lru_conv_b', 'lru_wa', 'lru_ba',
           'lru_wx', 'lru_bx', 'lru_lambda', 'fox_bf', 'ssd_conv_w', 'ssd_conv_b', 'ssd_dt_bias', 'ssd_a_log', 'ssd_d',
           'ssd_norm_g', 'w_out', 'ln2_g', 'ln2_b', 'ffn2_wg', 'ffn2_wu', 'ffn2_wd', 'ln3_g', 'ln3_b', 'pe_proj',
           'pe_gate_w', 'pe_gate_b']
CLASSES = (("ffn1_wg", "ffn1_wu", "ffn2_wg", "ffn2_wu", "w_in"),
           ("ffn1_wd", "ffn2_wd"),
           ("w_out", "pe_gate_w"),
           ("pe_proj",))
CLASS_PAD_AXIS = (1, 0, None, None)
BIG = {n: ci for ci, names in enumerate(CLASSES) for n in names}
SMALL_SHARDED = {'lru_conv_w': 2, 'ssd_conv_w': 2}
PACK_COLS = 1024


def _unshard(seg, axis):
    moved = jnp.moveaxis(seg, 0, axis)
    shp = list(moved.shape)
    shp[axis:axis + 2] = [shp[axis] * shp[axis + 1]]
    return moved.reshape(shp)


def _pad_axis(a, axis, size):
    if axis is None or a.shape[axis] == size:
        return a
    pads = [(0, 0)] * a.ndim
    pads[axis] = (0, size - a.shape[axis])
    return jnp.pad(a, pads)


def _pack(arrs, dtype, cols):
    flat = jnp.concatenate([a.astype(dtype).reshape(-1) for a in arrs])
    pad = (-flat.shape[0]) % cols
    if pad:
        flat = jnp.concatenate([flat, jnp.zeros((pad,), dtype)])
    return flat.reshape(-1, cols)


def _unpack(flat, shapes):
    out, off = [], 0
    for s in shapes:
        n = math.prod(s)
        out.append(flat[off:off + n].reshape(s))
        off += n
    return out


def kernel(x, p, ln1_g, ln1_b, ffn1_wg, ffn1_wu, ffn1_wd, w_in, lru_conv_w, lru_conv_b, lru_wa, lru_ba, lru_wx, lru_bx, lru_lambda, fox_bf, ssd_conv_w, ssd_conv_b, ssd_dt_bias, ssd_a_log, ssd_d, ssd_norm_g, w_out, ln2_g, ln2_b, ffn2_wg, ffn2_wu, ffn2_wd, ln3_g, ln3_b, pe_proj, pe_gate_w, pe_gate_b, loss_target, m_ln1_g, m_ln1_b, m_ffn1_wg, m_ffn1_wu, m_ffn1_wd, m_w_in, m_lru_conv_w, m_lru_conv_b, m_lru_wa, m_lru_ba, m_lru_wx, m_lru_bx, m_lru_lambda, m_fox_bf, m_ssd_conv_w, m_ssd_conv_b, m_ssd_dt_bias, m_ssd_a_log, m_ssd_d, m_ssd_norm_g, m_w_out, m_ln2_g, m_ln2_b, m_ffn2_wg, m_ffn2_wu, m_ffn2_wd, m_ln3_g, m_ln3_b, m_pe_proj, m_pe_gate_w, m_pe_gate_b, v_ln1_g, v_ln1_b, v_ffn1_wg, v_ffn1_wu, v_ffn1_wd, v_w_in, v_lru_conv_w, v_lru_conv_b, v_lru_wa, v_lru_ba, v_lru_wx, v_lru_bx, v_lru_lambda, v_fox_bf, v_ssd_conv_w, v_ssd_conv_b, v_ssd_dt_bias, v_ssd_a_log, v_ssd_d, v_ssd_norm_g, v_w_out, v_ln2_g, v_ln2_b, v_ffn2_wg, v_ffn2_wu, v_ffn2_wd, v_ln3_g, v_ln3_b, v_pe_proj, v_pe_gate_w, v_pe_gate_b):
    args = locals()
    w_loc = {n: args[n] for n in WEIGHTS}
    m_loc = {n: args["m_" + n] for n in WEIGHTS}
    v_loc = {n: args["v_" + n] for n in WEIGHTS}
    chip = 2 * lax.axis_index("x") + lax.axis_index("y")
    core = lax.axis_index("c")
    big = list(BIG)
    small_sh = list(SMALL_SHARDED)
    small_rep = [n for n in WEIGHTS if n not in BIG and n not in SMALL_SHARDED]

    srcs = [jnp.stack([_pad_axis(w_loc[n][li].astype(BF16), pad, SHARE) for li in range(DEPTH) for n in names])
            for names, pad in zip(CLASSES, CLASS_PAD_AXIS)]
    halves = _exchange(srcs, ("x", "y"), mode="gather_half", local=False, name="gather_w_chips")
    halves = [lax.dynamic_update_index_in_dim(
                  h, lax.dynamic_index_in_dim(s.reshape((2, -1) + s.shape[1:]), core, 0, keepdims=False), chip, 0)
              for h, s in zip(halves, srcs)]
    both = _exchange([h.reshape((-1,) + h.shape[2:]) for h in halves], ("c",), mode="gather", local=False,
                     name="gather_w_cores")
    both = [lax.dynamic_update_index_in_dim(b, h.reshape(b.shape[1:]), core, 0) for b, h in zip(both, halves)]
    chipw = [{} for _ in range(DEPTH)]
    for names, b in zip(CLASSES, both):
        hn = len(names)
        b = b.reshape((2, N_CHIPS, hn) + b.shape[2:])
        for li in range(DEPTH):
            for j, n in enumerate(names):
                chipw[li][n] = b[li, :, j]
    small = {n: w_loc[n] for n in small_rep}
    spack = _pack([w_loc[n] for n in small_sh], F32, LANES)
    (sg,) = _exchange([spack[None]], ("x", "y"), mode="gather", name="gather_conv_w")
    for n, seg in zip(small_sh, _unpack_rows(sg.reshape(N_CHIPS, -1), [w_loc[n].shape for n in small_sh])):
        small[n] = _unshard(seg, SMALL_SHARDED[n])
    Ws = [_layer_weights(li, chipw[li], small) for li in range(DEPTH)]

    loss, grad_x, g_big, g_small = _local_step(x[0], p[:, 0], loss_target[0], Ws)
    loss = lax.psum(loss[0, 0], MESH_AXES)

    gcls = [jnp.stack([g_big[li][n] for li in range(DEPTH) for n in names]) for names in CLASSES]
    gcls = [g.reshape((2, -1) + g.shape[1:]) for g in gcls]
    pair = _exchange(gcls, ("c",), mode="a2a", local=False, name="reduce_cores")
    pair = [lax.dynamic_update_index_in_dim(pr, lax.dynamic_index_in_dim(g, core, 0, keepdims=False), core, 0)
            for pr, g in zip(pair, gcls)]
    s2 = [_sum_slots(pr.reshape(2, -1, pr.shape[-1]), BF16, name=f"reduce_cores_sum{ci}").reshape(pr.shape[1:])
          for ci, pr in enumerate(pair)]
    quad = _exchange(s2, ("x", "y"), mode="a2a_inner", local=False, name="reduce_chips")
    quad = [lax.dynamic_update_index_in_dim(q, lax.dynamic_index_in_dim(s, chip, 1, keepdims=False), chip, 0)
            for q, s in zip(quad, s2)]
    red = [_sum_slots(q.reshape(N_CHIPS, -1, q.shape[-1]), F32, name=f"reduce_chips_sum{ci}").reshape(q.shape[1:])
           for ci, q in enumerate(quad)]
    shared = _exchange(red, ("c",), mode="gather", local=False, name="reduce_share")
    shared = [lax.dynamic_update_index_in_dim(sh, r, core, 0) for sh, r in zip(shared, red)]
    g_red = {}
    for names, sh in zip(CLASSES, shared):
        sh = sh.reshape((-1,) + sh.shape[2:])
        for j, n in enumerate(names):
            g = jnp.stack([sh[li * len(names) + j] for li in range(DEPTH)])
            g_red[n] = g[tuple(slice(0, s) for s in w_loc[n].shape)]
    small_all = small_rep + small_sh
    sgp = _pack([g_small[n] for n in small_all], F32, PACK_COLS)
    (sall,) = _exchange([sgp[None]], MESH_AXES, mode="gather", name="reduce_small")
    sred = _sum_slots(sall.reshape((2 ** len(MESH_AXES),) + sgp.shape), F32, name="reduce_small_sum").reshape(-1)
    for n, g in zip(small_all, _unpack(sred, [g_small[n].shape for n in small_all])):
        if n in SMALL_SHARDED:
            width = w_loc[n].shape[-1]
            g = lax.dynamic_slice_in_dim(g, chip * width, width, axis=SMALL_SHARDED[n])
        g_red[n] = g

    delta, new_m, new_v = {}, {}, {}
    for n in big:
        delta[n], new_m[n], new_v[n] = _adamw(w_loc[n], g_red[n], m_loc[n], v_loc[n], name="adamw_" + n)
    shapes = [w_loc[n].shape for n in small_all]
    packs = [_pack([d[n] for n in small_all], F32, LANES) for d in (w_loc, g_red, m_loc, v_loc)]
    outs = _adamw(*packs, name="adamw_small")
    for d, o in zip((delta, new_m, new_v), outs):
        for n, a in zip(small_all, _unpack(o.reshape(-1), shapes)):
            d[n] = a
    return (loss, grad_x[None], *[g_red[n] for n in WEIGHTS], *[delta[n] for n in WEIGHTS],
            *[new_m[n] for n in WEIGHTS], *[new_v[n] for n in WEIGHTS])


def _unpack_rows(gathered, shapes):
    out, off = [], 0
    for s in shapes:
        n = math.prod(s)
        out.append(gathered[:, off:off + n].reshape((N_CHIPS,) + tuple(s)))
        off += n
    return out
```

```python
import functools
import math

import jax
import jax.numpy as jnp
from jax import lax
from jax.experimental import pallas as pl
from jax.experimental.pallas import tpu as pltpu

F32 = jnp.float32
BF16 = jnp.bfloat16

D_MODEL = 1024
DEPTH = 2
PLE_DIM = 256
HEAD_DIM = 64
LRU_WIDTH = 256
LRU_HEADS = 4
LRU_C = 8.0
CONV_K = 4
ATT_WIDTH = 256
ATT_HEADS = 4
SSD_WIDTH = 512
SSD_HEADS = 8
SSD_GROUPS = 2
SSD_STATE = 128
SSD_CHUNK = 128
SSD_CONV_DIM = 1024
FFN_DIM = 2816
ALPHA = (2.0 * DEPTH) ** 0.25
LN_EPS = 1e-5
RMS_EPS = 1e-5
IN_WIDTH = 2828
ADAM_LR = 0.001
ADAM_B1 = 0.9
ADAM_B2 = 0.999
ADAM_EPS = 1e-08
ADAM_WD = 0.01
ADAM_STEP = 10

H_WIDTH = 3072
COL_XBC, COL_Z, COL_U, COL_G, COL_Q, COL_K, COL_V, COL_SMALL = 0, 1024, 1536, 1792, 2048, 2304, 2560, 2816
LANE_F = 0
LANE_DT = 4
LANES = 128
SUBLANES = 8
NEG = -1e30

VMEM_LIMIT = 48 * 1024 * 1024

N_CHIPS = 4
MESH_AXES = ("x", "y", "c")
SHARE = 768


def _params(n):
    return pltpu.CompilerParams(dimension_semantics=("arbitrary",) * n, vmem_limit_bytes=VMEM_LIMIT)


def _pick(n, cands):
    for c in cands:
        if n % c == 0:
            return c
    return n


def _iota(shape, dim):
    return lax.broadcasted_iota(jnp.int32, shape, dim)


def _shift_down(x, s, prev8):
    if s == 0:
        return x
    r = pltpu.roll(x, s, 0)
    pr = pltpu.roll(prev8, s, 0)
    head = jnp.where(_iota(pr.shape, 0) < s, pr, r[:SUBLANES])
    return jnp.concatenate([head, r[SUBLANES:]], axis=0)


def _shift_up(x, s, next8):
    if s == 0:
        return x
    n = x.shape[0]
    r = pltpu.roll(x, n - s, 0)
    nr = pltpu.roll(next8, SUBLANES - s, 0)
    tail = jnp.where(_iota(nr.shape, 0) >= SUBLANES - s, nr, r[n - SUBLANES:])
    return jnp.concatenate([r[:n - SUBLANES], tail], axis=0)


def _scan_fwd(a, b):
    n = a.shape[0]
    row = _iota(a.shape, 0)
    d = 1
    while d < n:
        keep = row >= d
        a_s = jnp.where(keep, pltpu.roll(a, d, 0), 1.0)
        b_s = jnp.where(keep, pltpu.roll(b, d, 0), 0.0)
        b = a * b_s + b
        a = a * a_s
        d *= 2
    return a, b


def _scan_bwd(a, b):
    n = a.shape[0]
    row = _iota(a.shape, 0)
    d = 1
    while d < n:
        keep = row < n - d
        a_s = jnp.where(keep, pltpu.roll(a, n - d, 0), 1.0)
        b_s = jnp.where(keep, pltpu.roll(b, n - d, 0), 0.0)
        b = a * b_s + b
        a = a * a_s
        d *= 2
    return a, b


def _cumsum_rows(x, reverse=False):
    n = x.shape[0]
    row = _iota(x.shape, 0)
    d = 1
    while d < n:
        if reverse:
            x = x + jnp.where(row < n - d, pltpu.roll(x, n - d, 0), 0.0)
        else:
            x = x + jnp.where(row >= d, pltpu.roll(x, d, 0), 0.0)
        d *= 2
    return x


def _col(x, lane):
    return jnp.sum(jnp.where(_iota(x.shape, 1) == lane, x, 0.0), axis=1, keepdims=True)


def _row(x, r):
    return jnp.sum(jnp.where(_iota(x.shape, 0) == r, x, 0.0), axis=0, keepdims=True)


def _sigmoid(x):
    return jax.nn.sigmoid(x)


def _softplus(x):
    return jnp.maximum(x, 0.0) + jnp.log(1.0 + jnp.exp(-jnp.abs(x)))


def _gelu_and_grad(x):
    c0 = math.sqrt(2.0 / math.pi)
    inner = c0 * (x + 0.044715 * x * x * x)
    t = jnp.tanh(inner)
    g = 0.5 * x * (1.0 + t)
    dg = 0.5 * (1.0 + t) + 0.5 * x * (1.0 - t * t) * c0 * (1.0 + 3.0 * 0.044715 * x * x)
    return g, dg


def _dot(a, b, ca, cb):
    return lax.dot_general(a, b, (((ca,), (cb,)), ((), ())), preferred_element_type=F32)


def _conv_taps(xr, prev8, w, bias):
    y = bias + w[CONV_K - 1:CONV_K, :] * xr
    for j in range(CONV_K - 1):
        y = y + w[j:j + 1, :] * _shift_down(xr, CONV_K - 1 - j, prev8)
    return y


def _conv_taps_bwd(dy, next8, w, xr):
    dx = None
    dws = []
    for j in range(CONV_K):
        sh = _shift_up(dy, CONV_K - 1 - j, next8)
        term = w[j:j + 1, :] * sh
        dx = term if dx is None else dx + term
        dws.append(jnp.sum(sh * xr, axis=0, keepdims=True))
    return dx, jnp.concatenate(dws, axis=0)


def _head_expand(v, lane0, nheads, width):
    rows = v.shape[0]
    colhead = _iota((rows, width), 1) // HEAD_DIM
    out = jnp.zeros((rows, width), F32)
    for h in range(nheads):
        out = jnp.where(colhead == h, _col(v, lane0 + h), out)
    return out


def _head_reduce(x, lane0, nheads):
    rows = x.shape[0]
    colhead = _iota(x.shape, 1) // HEAD_DIM
    lane = _iota((rows, LANES), 1)
    out = jnp.zeros((rows, LANES), F32)
    for h in range(nheads):
        s = jnp.sum(jnp.where(colhead == h, x, 0.0), axis=1, keepdims=True)
        out = jnp.where(lane == lane0 + h, s, out)
    return out


def _mm(a, b, *, ta=False, tb=False, scale=1.0, out_dtype=F32, chip_cols=False, name):
    if ta:
        kk, m = a.shape
    else:
        m, kk = a.shape
    n = b.shape[0] if tb else b.shape[1]
    tm = _pick(m, (1024, 512, 256, 128))
    tn = _pick(n // N_CHIPS, (768, 256, 128)) if chip_cols else _pick(n, (1024, 768, 512, 256, 128))
    tk = _pick(kk, (1024, 768, 512, 256, 128))
    nk = kk // tk
    dn_a = 0 if ta else 1
    dn_b = 1 if tb else 0
    if chip_cols:
        per = n // N_CHIPS // tn
        out_spec = pl.BlockSpec((None, tm, tn), lambda i, j, k: (j // per, i, j % per))
        out_shape = jax.ShapeDtypeStruct((N_CHIPS, m, n // N_CHIPS), out_dtype)
    else:
        out_spec = pl.BlockSpec((tm, tn), lambda i, j, k: (i, j))
        out_shape = jax.ShapeDtypeStruct((m, n), out_dtype)

    def body(a_ref, b_ref, o_ref, acc):
        k = pl.program_id(2)

        @pl.when(k == 0)
        def _():
            acc[...] = jnp.zeros_like(acc)

        acc[...] += _dot(a_ref[...].astype(BF16), b_ref[...].astype(BF16), dn_a, dn_b)

        @pl.when(k == nk - 1)
        def _():
            o_ref[...] = (acc[...] * scale).astype(out_dtype)

    a_spec = pl.BlockSpec((tk, tm), lambda i, j, k: (k, i)) if ta else pl.BlockSpec((tm, tk), lambda i, j, k: (i, k))
    b_spec = pl.BlockSpec((tn, tk), lambda i, j, k: (j, k)) if tb else pl.BlockSpec((tk, tn), lambda i, j, k: (k, j))
    return pl.pallas_call(
        body, name=name, grid=(m // tm, n // tn, nk),
        in_specs=[a_spec, b_spec],
        out_specs=out_spec, out_shape=out_shape,
        scratch_shapes=[pltpu.VMEM((tm, tn), F32)],
        compiler_params=_params(3),
    )(a, b)


def _mm_swiglu(xb, wg, wu, *, name):
    t, d = xb.shape
    share = wg.shape[2]
    n = N_CHIPS * share
    tm = _pick(t, (512, 256, 128))
    tn = _pick(share, (768, 256, 128))
    per = share // tn

    def body(x_ref, wg_ref, wu_ref, g_ref, u_ref, a_ref):
        x = x_ref[...]
        g = _dot(x, wg_ref[...], 1, 0)
        u = _dot(x, wu_ref[...], 1, 0)
        g_ref[...] = g.astype(BF16)
        u_ref[...] = u.astype(BF16)
        a_ref[...] = (g * _sigmoid(g) * u).astype(BF16)

    o = jax.ShapeDtypeStruct((t, n), BF16)
    ospec = pl.BlockSpec((tm, tn), lambda i, j: (i, j))
    return pl.pallas_call(
        body, name=name, grid=(t // tm, n // tn),
        in_specs=[pl.BlockSpec((tm, d), lambda i, j: (i, 0)),
                  pl.BlockSpec((None, d, tn), lambda i, j: (j // per, 0, j % per)),
                  pl.BlockSpec((None, d, tn), lambda i, j: (j // per, 0, j % per))],
        out_specs=[ospec, ospec, ospec], out_shape=[o, o, o],
        compiler_params=_params(2),
    )(xb, wg, wu)


def _mm_swiglu_bwd(dr, wd, g, u, *, scale, name):
    t, d = dr.shape
    n = wd.shape[0]
    tm = _pick(t, (512, 256, 128))
    tn = _pick(n, (768, 256, 128))

    def body(dr_ref, wd_ref, g_ref, u_ref, dg_ref, du_ref):
        da = _dot(dr_ref[...].astype(BF16), wd_ref[...], 1, 1) * scale
        gg = g_ref[...].astype(F32)
        uu = u_ref[...].astype(F32)
        sg = _sigmoid(gg)
        dg_ref[...] = (da * uu * (sg * (1.0 + gg * (1.0 - sg)))).astype(BF16)
        du_ref[...] = (da * gg * sg).astype(BF16)

    o = jax.ShapeDtypeStruct((t, n), BF16)
    ospec = pl.BlockSpec((tm, tn), lambda i, j: (i, j))
    return pl.pallas_call(
        body, name=name, grid=(t // tm, n // tn),
        in_specs=[pl.BlockSpec((tm, d), lambda i, j: (i, 0)),
                  pl.BlockSpec((tn, d), lambda i, j: (j, 0)),
                  ospec, ospec],
        out_specs=[ospec, ospec], out_shape=[o, o],
        compiler_params=_params(2),
    )(dr, wd, g, u)


def _mm_ln(a, w, resid, gain, bias, *, rscale, mscale, name):
    t, kk = a.shape
    d = w.shape[1]
    tm = _pick(t, (512, 256, 128))
    tk = _pick(kk, (1024, 1408, 512, 256, 128))
    nk = kk // tk

    def body(a_ref, w_ref, r_ref, g_ref, b_ref, y_ref, yb_ref, xh_ref, rs_ref, acc):
        k = pl.program_id(1)

        @pl.when(k == 0)
        def _():
            acc[...] = jnp.zeros_like(acc)

        acc[...] += _dot(a_ref[...].astype(BF16), w_ref[...], 1, 0)

        @pl.when(k == nk - 1)
        def _():
            r = rscale * r_ref[...] + mscale * acc[...]
            mu = jnp.mean(r, axis=1, keepdims=True)
            xc = r - mu
            var = jnp.mean(xc * xc, axis=1, keepdims=True)
            rstd = lax.rsqrt(var + LN_EPS)
            xh = xc * rstd
            y = xh * g_ref[...] + b_ref[...]
            y_ref[...] = y
            yb_ref[...] = y.astype(BF16)
            xh_ref[...] = xh
            rs_ref[...] = rstd

    row = pl.BlockSpec((tm, d), lambda i, k: (i, 0))
    vec = pl.BlockSpec((1, d), lambda i, k: (0, 0))
    return pl.pallas_call(
        body, name=name, grid=(t // tm, nk),
        in_specs=[pl.BlockSpec((tm, tk), lambda i, k: (i, k)),
                  pl.BlockSpec((tk, d), lambda i, k: (k, 0)), row, vec, vec],
        out_specs=[row, row, row, pl.BlockSpec((tm, 1), lambda i, k: (i, 0))],
        out_shape=[jax.ShapeDtypeStruct((t, d), F32), jax.ShapeDtypeStruct((t, d), BF16),
                   jax.ShapeDtypeStruct((t, d), F32), jax.ShapeDtypeStruct((t, 1), F32)],
        scratch_shapes=[pltpu.VMEM((tm, d), F32)],
        compiler_params=_params(2),
    )(a, w, resid, gain.reshape(1, d), bias.reshape(1, d))


def _bwd_proj(pairs, resid, *, rscale, ln, name):
    t, kk = pairs[0][0].shape
    d = pairs[0][1].shape[-2]
    tm = _pick(t, (512, 256, 128))
    tk = _pick(pairs[0][1].shape[-1], (1024, 768, 512, 256, 128))
    nk = kk // tk
    nt = t // tm
    npair = len(pairs)
    has_ln = ln is not None

    def body(*refs):
        ab = refs[:2 * npair]
        r_ref = refs[2 * npair]
        pos = 2 * npair + 1
        if has_ln:
            xh_ref, rs_ref, g_ref = refs[pos:pos + 3]
            pos += 3
            o_ref, dg_ref, db_ref = refs[pos:pos + 3]
            pos += 3
        else:
            o_ref = refs[pos]
            pos += 1
        acc = refs[pos]
        i = pl.program_id(0)
        k = pl.program_id(1)

        @pl.when(k == 0)
        def _():
            acc[...] = jnp.zeros_like(acc)

        for q in range(npair):
            acc[...] += _dot(ab[2 * q][...].astype(BF16), ab[2 * q + 1][...], 1, 1)

        @pl.when(k == nk - 1)
        def _():
            dy = rscale * r_ref[...] + acc[...]
            if not has_ln:
                o_ref[...] = dy
                return
            xh = xh_ref[...]
            w = dy * g_ref[...]
            m1 = jnp.mean(w, axis=1, keepdims=True)
            m2 = jnp.mean(w * xh, axis=1, keepdims=True)
            o_ref[...] = rs_ref[...] * (w - m1 - xh * m2)

            @pl.when(i == 0)
            def _():
                dg_ref[...] = jnp.zeros_like(dg_ref)
                db_ref[...] = jnp.zeros_like(db_ref)

            dg_ref[...] += jnp.sum(dy * xh, axis=0, keepdims=True)
            db_ref[...] += jnp.sum(dy, axis=0, keepdims=True)

    row = pl.BlockSpec((tm, d), lambda i, k: (i, 0))
    vec = pl.BlockSpec((1, d), lambda i, k: (0, 0))
    in_specs, args = [], []
    for a, b in pairs:
        if b.ndim == 3:
            per = b.shape[2] // tk
            b_spec = pl.BlockSpec((None, d, tk), lambda i, k, per=per: (k // per, 0, k % per))
        else:
            b_spec = pl.BlockSpec((d, tk), lambda i, k: (0, k))
        in_specs += [pl.BlockSpec((tm, tk), lambda i, k: (i, k)), b_spec]
        args += [a, b]
    in_specs.append(row)
    args.append(resid)
    out_specs = [row]
    out_shape = [jax.ShapeDtypeStruct((t, d), F32)]
    if has_ln:
        xh, rs, gain = ln
        in_specs += [row, pl.BlockSpec((tm, 1), lambda i, k: (i, 0)), vec]
        args += [xh, rs, gain.reshape(1, d)]
        out_specs += [vec, vec]
        out_shape += [jax.ShapeDtypeStruct((1, d), F32)] * 2
    return pl.pallas_call(
        body, name=name, grid=(nt, nk), in_specs=in_specs, out_specs=out_specs, out_shape=out_shape,
        scratch_shapes=[pltpu.VMEM((tm, d), F32)],
        compiler_params=_params(2),
    )(*args)


def _mm_pe(x3, x3b, pb, wgate, bgate, wproj, *, name):
    t, d = x3.shape
    pd = pb.shape[1]
    tm = _pick(t, (512, 256, 128))
    tn = _pick(d, (512, 256, 128))

    def body(x_ref, xb_ref, p_ref, wg_ref, bg_ref, wp_ref, y_ref, yb_ref, sg_ref, e_ref):
        sg = _sigmoid(_dot(xb_ref[...], wg_ref[...], 1, 0) + bg_ref[...])
        e = _dot(p_ref[...], wp_ref[...], 1, 0)
        y = x_ref[...] + sg * e
        y_ref[...] = y
        yb_ref[...] = y.astype(BF16)
        sg_ref[...] = sg.astype(BF16)
        e_ref[...] = e.astype(BF16)

    ospec = pl.BlockSpec((tm, tn), lambda i, j: (i, j))
    ob = jax.ShapeDtypeStruct((t, d), BF16)
    return pl.pallas_call(
        body, name=name, grid=(t // tm, d // tn),
        in_specs=[ospec, pl.BlockSpec((tm, d), lambda i, j: (i, 0)), pl.BlockSpec((tm, pd), lambda i, j: (i, 0)),
                  pl.BlockSpec((d, tn), lambda i, j: (0, j)), pl.BlockSpec((1, tn), lambda i, j: (0, j)),
                  pl.BlockSpec((pd, tn), lambda i, j: (0, j))],
        out_specs=[ospec, ospec, ospec, ospec],
        out_shape=[jax.ShapeDtypeStruct((t, d), F32), ob, ob, ob],
        compiler_params=_params(2),
    )(x3, x3b, pb, wgate, bgate.reshape(1, d), wproj)


def _pe_bwd_elem(dx4, sg, e, *, name):
    t, d = dx4.shape
    tm = _pick(t, (512, 256, 128))

    def body(dx_ref, sg_ref, e_ref, dgp_ref, de_ref, db_ref):
        dx = dx_ref[...]
        s = sg_ref[...].astype(F32)
        dgp = dx * e_ref[...].astype(F32) * s * (1.0 - s)
        dgp_ref[...] = dgp.astype(BF16)
        de_ref[...] = (dx * s).astype(BF16)

        @pl.when(pl.program_id(0) == 0)
        def _():
            db_ref[...] = jnp.zeros_like(db_ref)

        db_ref[...] += jnp.sum(dgp, axis=0, keepdims=True)

    row = pl.BlockSpec((tm, d), lambda i: (i, 0))
    ob = jax.ShapeDtypeStruct((t, d), BF16)
    return pl.pallas_call(
        body, name=name, grid=(t // tm,), in_specs=[row, row, row],
        out_specs=[row, row, pl.BlockSpec((1, d), lambda i: (0, 0))],
        out_shape=[ob, ob, jax.ShapeDtypeStruct((1, d), F32)],
        compiler_params=_params(1),
    )(dx4, sg, e)


def _loss_kernel(y, target, *, name):
    t, d = y.shape
    tm = _pick(t, (512, 256, 128))

    def body(y_ref, t_ref, dy_ref, l_ref):
        diff = y_ref[...] - t_ref[...]
        dy_ref[...] = diff * (1.0 / d)

        @pl.when(pl.program_id(0) == 0)
        def _():
            l_ref[...] = jnp.zeros_like(l_ref)

        part = jnp.sum(jnp.mean(diff * diff, axis=1, keepdims=True), axis=0, keepdims=True)
        l_ref[...] += 0.5 * part

    row = pl.BlockSpec((tm, d), lambda i: (i, 0))
    return pl.pallas_call(
        body, name=name, grid=(t // tm,), in_specs=[row, row],
        out_specs=[row, pl.BlockSpec((1, 1), lambda i: (0, 0))],
        out_shape=[jax.ShapeDtypeStruct((t, d), F32), jax.ShapeDtypeStruct((1, 1), F32)],
        compiler_params=_params(1),
    )(y, target)


LRU_TM = 256


def _lru_gate_terms(r, lam):
    sp = _softplus(-lam)
    la = -LRU_C * r * sp
    a = jnp.exp(la)
    em = jnp.tanh(la) * (jnp.exp(2.0 * la) + 1.0)
    s = jnp.sqrt(-em)
    return la, a, s, sp


def _lru_fwd(hbuf, conv_w, conv_b, wa, ba, wx, bx, lam, *, name):
    t = hbuf.shape[0]
    w = LRU_WIDTH
    tm = _pick(t, (LRU_TM, 128))
    cu, cg = COL_U // w, COL_G // w
    hb = tm // SUBLANES

    def body(u_ref, up_ref, g_ref, cw_ref, cb_ref, wa_ref, ba_ref, wx_ref, bx_ref, lam_ref,
             y_ref, u_out, r_out, i_out, a_out, h_out, carry):
        i = pl.program_id(0)

        @pl.when(i == 0)
        def _():
            carry[...] = jnp.zeros_like(carry)

        prev = jnp.where(i == 0, 0.0, up_ref[...])
        u = _conv_taps(u_ref[...], prev, cw_ref[...], cb_ref[...])
        ub = u.astype(BF16)
        r = _sigmoid(_dot(ub, wa_ref[...], 1, 0) + ba_ref[...])
        ig = _sigmoid(_dot(ub, wx_ref[...], 1, 0) + bx_ref[...])
        _, a, s, _ = _lru_gate_terms(r, lam_ref[...])
        b = s * (ig * u)
        acum, hs = _scan_fwd(a, b)
        h = hs + acum * carry[0:1, :]
        carry[...] = jnp.broadcast_to(h[tm - 1:tm, :], carry.shape)
        gl, _ = _gelu_and_grad(g_ref[...])
        y_ref[...] = h * gl
        u_out[...] = u
        r_out[...] = r
        i_out[...] = ig
        a_out[...] = a
        h_out[...] = h

    row = pl.BlockSpec((tm, w), lambda i: (i, 0))
    vec = pl.BlockSpec((1, w), lambda i: (0, 0))
    mat = pl.BlockSpec((w, w), lambda i: (0, 0))
    o = jax.ShapeDtypeStruct((t, w), F32)
    return pl.pallas_call(
        body, name=name, grid=(t // tm,),
        in_specs=[pl.BlockSpec((tm, w), lambda i: (i, cu)),
                  pl.BlockSpec((SUBLANES, w), lambda i: (jnp.maximum(i * hb - 1, 0), cu)),
                  pl.BlockSpec((tm, w), lambda i: (i, cg)),
                  pl.BlockSpec((CONV_K, w), lambda i: (0, 0)), vec, mat, vec, mat, vec, vec],
        out_specs=[row] * 6, out_shape=[o] * 6,
        scratch_shapes=[pltpu.VMEM((SUBLANES, w), F32)],
        compiler_params=_params(1),
    )(hbuf, hbuf, hbuf, conv_w, conv_b, wa, ba, wx, bx, lam)


def _lru_bwd(dymix, hbuf, u, r, ig, a, h, conv_w, wa, wx, lam, *, name):
    t = hbuf.shape[0]
    w = LRU_WIDTH
    tm = _pick(t, (LRU_TM, 128))
    nb = t // tm
    cu, cg = COL_U // w, COL_G // w
    hb = tm // SUBLANES
    last8 = t // SUBLANES - 1

    def body(dy_ref, ur_ref, g_ref, u_ref, r_ref, i_ref, a_ref, an_ref, h_ref, hp_ref,
             cw_ref, wa_ref, wx_ref, lam_ref,
             dur_ref, dgr_ref, dcw_ref, dcb_ref, dwa_ref, dba_ref, dwx_ref, dbx_ref, dlam_ref,
             lcarry, dnext):
        i = pl.program_id(0)
        ib = nb - 1 - i

        @pl.when(i == 0)
        def _():
            lcarry[...] = jnp.zeros_like(lcarry)
            dnext[...] = jnp.zeros_like(dnext)
            for ref in (dcw_ref, dcb_ref, dwa_ref, dba_ref, dwx_ref, dbx_ref, dlam_ref):
                ref[...] = jnp.zeros_like(ref)

        dy = dy_ref[...]
        hh = h_ref[...]
        av = a_ref[...]
        uu = u_ref[...]
        rr = r_ref[...]
        ii = i_ref[...]
        lam_v = lam_ref[...]
        gl, dgl = _gelu_and_grad(g_ref[...])
        dgr_ref[...] = (dy * hh * dgl).astype(BF16)
        dh_out = dy * gl
        a_next = _shift_up(av, 1, jnp.where(ib == nb - 1, 0.0, an_ref[...]))
        acum, ls = _scan_bwd(a_next, dh_out)
        lam_adj = ls + acum * lcarry[0:1, :]
        lcarry[...] = jnp.broadcast_to(lam_adj[0:1, :], lcarry.shape)
        h_prev = _shift_down(hh, 1, jnp.where(ib == 0, 0.0, hp_ref[...]))
        da = lam_adj * h_prev
        _, a2, s, sp = _lru_gate_terms(rr, lam_v)
        d_igu = lam_adj * s
        ds = lam_adj * ii * uu
        dla = da * a2 - ds * (a2 * a2) / s
        dr = dla * (-LRU_C * sp)
        dlam_ref[...] += jnp.sum(dla * (LRU_C * rr * _sigmoid(-lam_v)), axis=0, keepdims=True)
        dpre_r = dr * rr * (1.0 - rr)
        dpre_i = d_igu * uu * ii * (1.0 - ii)
        prb = dpre_r.astype(BF16)
        pib = dpre_i.astype(BF16)
        ub = uu.astype(BF16)
        du = d_igu * ii + _dot(prb, wa_ref[...], 1, 1) + _dot(pib, wx_ref[...], 1, 1)
        dwa_ref[...] += _dot(ub, prb, 0, 0)
        dwx_ref[...] += _dot(ub, pib, 0, 0)
        dba_ref[...] += jnp.sum(dpre_r, axis=0, keepdims=True)
        dbx_ref[...] += jnp.sum(dpre_i, axis=0, keepdims=True)
        dur, dws = _conv_taps_bwd(du, dnext[...], cw_ref[...], ur_ref[...])
        dur_ref[...] = dur.astype(BF16)
        dcw_ref[...] += dws
        dcb_ref[...] += jnp.sum(du, axis=0, keepdims=True)
        dnext[...] = du[:SUBLANES]

    def rowspec(col):
        return pl.BlockSpec((tm, w), lambda i: (nb - 1 - i, col))

    row = rowspec(0)
    nxt = pl.BlockSpec((SUBLANES, w), lambda i: (jnp.minimum((nb - i) * hb, last8), 0))
    prv = pl.BlockSpec((SUBLANES, w), lambda i: (jnp.maximum((nb - 1 - i) * hb - 1, 0), 0))
    vec = pl.BlockSpec((1, w), lambda i: (0, 0))
    mat = pl.BlockSpec((w, w), lambda i: (0, 0))
    cw = pl.BlockSpec((CONV_K, w), lambda i: (0, 0))
    o = jax.ShapeDtypeStruct((t, w), BF16)
    v1 = jax.ShapeDtypeStruct((1, w), F32)
    m1 = jax.ShapeDtypeStruct((w, w), F32)
    return pl.pallas_call(
        body, name=name, grid=(nb,),
        in_specs=[rowspec(0), rowspec(cu), rowspec(cg), row, row, row, row, nxt, row, prv, cw, mat, mat, vec],
        out_specs=[row, row, cw, vec, mat, vec, mat, vec, vec],
        out_shape=[o, o, jax.ShapeDtypeStruct((CONV_K, w), F32), v1, m1, v1, m1, v1, v1],
        scratch_shapes=[pltpu.VMEM((SUBLANES, w), F32), pltpu.VMEM((SUBLANES, w), F32)],
        compiler_params=_params(1),
    )(dymix, hbuf, hbuf, u, r, ig, a, a, h, h, conv_w, wa, wx, lam)


FOX_T = 512
FOX_PREP_TM = 256


def _log_sigmoid(x):
    return jnp.minimum(x, 0.0) - jnp.log(1.0 + jnp.exp(-jnp.abs(x)))


def _fox_prep(hbuf, bf_vec, *, name):
    t = hbuf.shape[0]
    tm = _pick(t, (FOX_PREP_TM, 128))
    cs = COL_SMALL // LANES

    def body(s_ref, b_ref, eq_ref, ek_ref, carry):
        i = pl.program_id(0)

        @pl.when(i == 0)
        def _():
            carry[...] = jnp.zeros_like(carry)

        lf = _log_sigmoid(s_ref[...] + b_ref[...])
        f = _cumsum_rows(lf) + carry[0:1, :]
        carry[...] = jnp.broadcast_to(f[tm - 1:tm, :], carry.shape)
        lane = _iota((tm, LANES), 1)
        for h in range(ATT_HEADS):
            base = HEAD_DIM * (1 - h % 2)
            fh = _col(f, h)
            hi = fh.astype(BF16).astype(F32)
            mid = (fh - hi).astype(BF16).astype(F32)
            lo = fh - hi - mid
            terms = jnp.where(lane == base, hi, jnp.where(lane == base + 1, mid, jnp.where(lane == base + 2, lo, 0.0)))
            terms_k = jnp.where(lane == base + 3, -hi,
                                jnp.where(lane == base + 4, -mid, jnp.where(lane == base + 5, -lo, 0.0)))
            ones_q = ((lane >= base + 3) & (lane < base + 6)).astype(F32)
            ones_k = ((lane >= base) & (lane < base + 3)).astype(F32)
            eq_ref[:, LANES * h:LANES * (h + 1)] = (terms + ones_q).astype(BF16)
            ek_ref[:, LANES * h:LANES * (h + 1)] = (terms_k + ones_k).astype(BF16)

    ospec = pl.BlockSpec((tm, ATT_HEADS * LANES), lambda i: (i, 0))
    o = jax.ShapeDtypeStruct((t, ATT_HEADS * LANES), BF16)
    return pl.pallas_call(
        body, name=name, grid=(t // tm,),
        in_specs=[pl.BlockSpec((tm, LANES), lambda i: (i, cs)), pl.BlockSpec((1, LANES), lambda i: (0, 0))],
        out_specs=[ospec, ospec], out_shape=[o, o],
        scratch_shapes=[pltpu.VMEM((SUBLANES, LANES), F32)],
        compiler_params=_params(1),
    )(hbuf, bf_vec)


def _fox_post(dfc, hbuf, bf_vec, *, name):
    t = hbuf.shape[0]
    tm = _pick(t, (FOX_PREP_TM, 128))
    nb = t // tm
    cs = COL_SMALL // LANES

    def body(df_ref, s_ref, b_ref, o_ref, db_ref, carry):
        i = pl.program_id(0)

        @pl.when(i == 0)
        def _():
            carry[...] = jnp.zeros_like(carry)
            db_ref[...] = jnp.zeros_like(db_ref)

        dlf = _cumsum_rows(df_ref[...], reverse=True) + carry[0:1, :]
        carry[...] = jnp.broadcast_to(dlf[0:1, :], carry.shape)
        dl = dlf * _sigmoid(-(s_ref[...] + b_ref[...]))
        dl = jnp.where(_iota(dl.shape, 1) < ATT_HEADS, dl, 0.0)
        o_ref[...] = dl
        db_ref[...] += jnp.sum(dl, axis=0, keepdims=True)

    vec = pl.BlockSpec((1, LANES), lambda i: (0, 0))
    return pl.pallas_call(
        body, name=name, grid=(nb,),
        in_specs=[pl.BlockSpec((tm, LANES), lambda i: (nb - 1 - i, 0)),
                  pl.BlockSpec((tm, LANES), lambda i: (nb - 1 - i, cs)), vec],
        out_specs=[pl.BlockSpec((tm, LANES), lambda i: (nb - 1 - i, 0)), vec],
        out_shape=[jax.ShapeDtypeStruct((t, LANES), F32), jax.ShapeDtypeStruct((1, LANES), F32)],
        scratch_shapes=[pltpu.VMEM((SUBLANES, LANES), F32)],
        compiler_params=_params(1),
    )(dfc, hbuf, bf_vec)


def _fox_masks(i, j, tq):
    row = i * tq + _iota((tq, tq), 0)
    col = j * tq + _iota((tq, tq), 1)
    lane = _iota((1, LANES), 1)
    return col <= row, (lane < HEAD_DIM, lane >= HEAD_DIM)


def _fox_fwd(hbuf, eq, ek, *, name):
    t = hbuf.shape[0]
    w = ATT_WIDTH
    tq = _pick(t, (FOX_T, 256, 128))
    nq = t // tq
    cq, ck, cv = COL_Q // w, COL_K // w, COL_V // w

    def body(q_ref, k_ref, v_ref, eq_ref, ek_ref, o_ref, lse_ref, m_s, l_s, acc_s):
        i = pl.program_id(0)
        j = pl.program_id(1)

        @pl.when(j == 0)
        def _():
            m_s[...] = jnp.full_like(m_s, NEG)
            l_s[...] = jnp.zeros_like(l_s)
            acc_s[...] = jnp.zeros_like(acc_s)

        def step(diagonal):
            _, hms = _fox_masks(i, j, tq)
            keys_first = (j * tq + _iota((tq, tq), 0)) <= (i * tq + _iota((tq, tq), 1))
            half = _iota((LANES, 1), 0)
            hrows = (half < HEAD_DIM, half >= HEAD_DIM)
            m_all = m_s[...]
            l_all = l_s[...]
            acc_old = [acc_s[LANES * pr:LANES * (pr + 1), :] for pr in range(2)]
            m_out, l_out, acc_out = [], [], []
            for pr in range(2):
                sl = slice(LANES * pr, LANES * (pr + 1))
                qp = q_ref[:, sl]
                kp = k_ref[:, sl]
                vt = v_ref[:, sl].T.astype(BF16)
                acc = acc_old[pr]
                for hh in range(2):
                    h = 2 * pr + hh
                    hsl = slice(LANES * h, LANES * (h + 1))
                    qm = jnp.where(hms[hh], (qp * (HEAD_DIM ** -0.5)).astype(BF16), eq_ref[:, hsl])
                    km = jnp.where(hms[hh], kp.astype(BF16), ek_ref[:, hsl])
                    st = _dot(km, qm, 1, 1)
                    if diagonal:
                        st = jnp.where(keys_first, st, NEG)
                    m_old = m_all[h:h + 1, :]
                    m_new = jnp.maximum(m_old, jnp.max(st, axis=0, keepdims=True))
                    alpha = jnp.exp(m_old - m_new)
                    pt = jnp.exp(st - m_new)
                    l_out.append(alpha * l_all[h:h + 1, :] + jnp.sum(pt, axis=0, keepdims=True))
                    m_out.append(m_new)
                    pv = _dot(vt, pt.astype(BF16), 1, 0)
                    acc = jnp.where(hrows[hh], alpha * acc_old[pr] + pv, acc)
                acc_out.append(acc)
            for h in range(ATT_HEADS):
                m_s[h:h + 1, :] = m_out[h]
                l_s[h:h + 1, :] = l_out[h]
            for pr in range(2):
                acc_s[LANES * pr:LANES * (pr + 1), :] = acc_out[pr]

        @pl.when(j < i)
        def _():
            step(False)

        @pl.when(j == i)
        def _():
            step(True)
            half = _iota((LANES, 1), 0)
            l_all = l_s[...]
            for pr in range(2):
                acc = acc_s[LANES * pr:LANES * (pr + 1), :]
                o_t = jnp.where(half < HEAD_DIM, acc / l_all[2 * pr:2 * pr + 1, :], acc / l_all[2 * pr + 1:2 * pr + 2, :])
                o_ref[:, LANES * pr:LANES * (pr + 1)] = o_t.T
            lse = m_s[...] + jnp.log(l_s[...])
            lse_ref[...] = jnp.where(_iota(lse.shape, 0) < ATT_HEADS, lse, 0.0)

    return pl.pallas_call(
        body, name=name, grid=(nq, nq),
        in_specs=[pl.BlockSpec((tq, w), lambda i, j: (i, cq)),
                  pl.BlockSpec((tq, w), lambda i, j: (jnp.minimum(j, i), ck)),
                  pl.BlockSpec((tq, w), lambda i, j: (jnp.minimum(j, i), cv)),
                  pl.BlockSpec((tq, ATT_HEADS * LANES), lambda i, j: (i, 0)),
                  pl.BlockSpec((tq, ATT_HEADS * LANES), lambda i, j: (jnp.minimum(j, i), 0))],
        out_specs=[pl.BlockSpec((tq, w), lambda i, j: (i, 0)),
                   pl.BlockSpec((SUBLANES, tq), lambda i, j: (0, i))],
        out_shape=[jax.ShapeDtypeStruct((t, w), F32), jax.ShapeDtypeStruct((SUBLANES, t), F32)],
        scratch_shapes=[pltpu.VMEM((SUBLANES, tq), F32), pltpu.VMEM((SUBLANES, tq), F32),
                        pltpu.VMEM((w, tq), F32)],
        compiler_params=_params(2),
    )(hbuf, hbuf, hbuf, eq, ek)


def _fox_delta(dymix, o, *, name):
    t, w = o.shape
    tm = _pick(t, (512, 256, 128))
    cdo = ATT_WIDTH // w

    def body(do_ref, o_ref, d_ref):
        d_ref[...] = _head_reduce(do_ref[...] * o_ref[...], 0, ATT_HEADS)

    return pl.pallas_call(
        body, name=name, grid=(t // tm,),
        in_specs=[pl.BlockSpec((tm, w), lambda i: (i, cdo)), pl.BlockSpec((tm, w), lambda i: (i, 0))],
        out_specs=pl.BlockSpec((tm, LANES), lambda i: (i, 0)),
        out_shape=jax.ShapeDtypeStruct((t, LANES), F32),
        compiler_params=_params(1),
    )(dymix, o)


def _fox_bwd(hbuf, eq, ek, dymix, lse_rows, delta_rows, *, name):
    t = hbuf.shape[0]
    w = ATT_WIDTH
    tq = _pick(t, (FOX_T, 256, 128))
    nq = t // tq
    cq, ck, cv = COL_Q // w, COL_K // w, COL_V // w
    cdo = ATT_WIDTH // w

    def body(q_ref, k_ref, v_ref, eq_ref, ek_ref, do_ref, lse_ref, dl_ref, dk_ref, dv_ref, dfk_ref, dqt_ref, dfq_ref,
             dk_s, dv_s, dfk_s):
        j = pl.program_id(0)
        i = pl.program_id(1)

        @pl.when((i == 0) & (j == 0))
        def _():
            dqt_ref[...] = jnp.zeros_like(dqt_ref)
            dfq_ref[...] = jnp.zeros_like(dfq_ref)

        @pl.when(i == 0)
        def _():
            dk_s[...] = jnp.zeros_like(dk_s)
            dv_s[...] = jnp.zeros_like(dv_s)
            dfk_s[...] = jnp.zeros_like(dfk_s)

        def step(diagonal):
            _, hms = _fox_masks(i, j, tq)
            keys_first = (j * tq + _iota((tq, tq), 0)) <= (i * tq + _iota((tq, tq), 1))
            half = _iota((LANES, 1), 0)
            hrows = (half < HEAD_DIM, half >= HEAD_DIM)
            lse_all = lse_ref[...]
            dl_all = dl_ref[...]
            dvs, dks, dfks, dqts, dfqs = [], [], [], [], []
            for pr in range(2):
                sl = slice(LANES * pr, LANES * (pr + 1))
                qp = q_ref[:, sl]
                kp = k_ref[:, sl]
                kt = kp.T.astype(BF16)
                vpb = v_ref[:, sl].astype(BF16)
                dop = do_ref[:, sl]
                dv_p = jnp.zeros((tq, LANES), F32)
                dk_p = jnp.zeros((tq, LANES), F32)
                dqt_p = jnp.zeros((LANES, tq), F32)
                for hh in range(2):
                    h = 2 * pr + hh
                    hsl = slice(LANES * h, LANES * (h + 1))
                    qm = jnp.where(hms[hh], (qp * (HEAD_DIM ** -0.5)).astype(BF16), eq_ref[:, hsl])
                    km = jnp.where(hms[hh], kp.astype(BF16), ek_ref[:, hsl])
                    st = _dot(km, qm, 1, 1)
                    if diagonal:
                        st = jnp.where(keys_first, st, NEG)
                    pt = jnp.exp(st - lse_all[h:h + 1, :])
                    domb = jnp.where(hms[hh], dop, 0.0).astype(BF16)
                    dv_p = dv_p + _dot(pt.astype(BF16), domb, 1, 0)
                    dpt = _dot(vpb, domb, 1, 1)
                    dst = pt * (dpt - dl_all[h:h + 1, :])
                    dstb = dst.astype(BF16)
                    dk_p = dk_p + jnp.where(hms[hh], _dot(dstb, qm, 1, 0), 0.0)
                    dqt_p = dqt_p + _dot(jnp.where(hrows[hh], kt, 0.0), dstb, 1, 0)
                    part = dst[:, 0:LANES]
                    for c in range(1, tq // LANES):
                        part = part + dst[:, LANES * c:LANES * (c + 1)]
                    dfks.append(part)
                    dfqs.append(jnp.sum(dst, axis=0, keepdims=True))
                dvs.append(dv_p)
                dks.append(dk_p)
                dqts.append(dqt_p)
            dv_s[...] += jnp.concatenate(dvs, axis=1)
            dk_s[...] += jnp.concatenate(dks, axis=1)
            for h in range(ATT_HEADS):
                dfk_s[h] += dfks[h]
            cols = pl.ds(pl.multiple_of(i * tq, tq), tq)
            dqt_ref[:, cols] += jnp.concatenate(dqts, axis=0) * (HEAD_DIM ** -0.5)
            dfq_ref[:, cols] += jnp.concatenate(dfqs + [jnp.zeros((SUBLANES - ATT_HEADS, tq), F32)], axis=0)

        @pl.when(i > j)
        def _():
            step(False)

        @pl.when(i == j)
        def _():
            step(True)

        @pl.when(i == nq - 1)
        def _():
            dk_ref[...] = dk_s[...].astype(BF16)
            dv_ref[...] = dv_s[...].astype(BF16)
            lane = _iota((tq, LANES), 1)
            out = jnp.zeros((tq, LANES), F32)
            for h in range(ATT_HEADS):
                out = jnp.where(lane == h, jnp.sum(dfk_s[h], axis=1, keepdims=True), out)
            dfk_ref[...] = out

    qi = lambda j, i: jnp.maximum(i, j)
    rows = pl.BlockSpec((SUBLANES, tq), lambda j, i: (0, qi(j, i)))
    return pl.pallas_call(
        body, name=name, grid=(nq, nq),
        in_specs=[pl.BlockSpec((tq, w), lambda j, i: (qi(j, i), cq)),
                  pl.BlockSpec((tq, w), lambda j, i: (j, ck)),
                  pl.BlockSpec((tq, w), lambda j, i: (j, cv)),
                  pl.BlockSpec((tq, ATT_HEADS * LANES), lambda j, i: (qi(j, i), 0)),
                  pl.BlockSpec((tq, ATT_HEADS * LANES), lambda j, i: (j, 0)),
                  pl.BlockSpec((tq, w), lambda j, i: (qi(j, i), cdo)),
                  rows, rows],
        out_specs=[pl.BlockSpec((tq, w), lambda j, i: (j, 0)), pl.BlockSpec((tq, w), lambda j, i: (j, 0)),
                   pl.BlockSpec((tq, LANES), lambda j, i: (j, 0)),
                   pl.BlockSpec((w, t), lambda j, i: (0, 0)), pl.BlockSpec((SUBLANES, t), lambda j, i: (0, 0))],
        out_shape=[jax.ShapeDtypeStruct((t, w), BF16), jax.ShapeDtypeStruct((t, w), BF16),
                   jax.ShapeDtypeStruct((t, LANES), F32),
                   jax.ShapeDtypeStruct((w, t), F32), jax.ShapeDtypeStruct((SUBLANES, t), F32)],
        scratch_shapes=[pltpu.VMEM((tq, w), F32), pltpu.VMEM((tq, w), F32),
                        pltpu.VMEM((ATT_HEADS, tq, LANES), F32)],
        compiler_params=_params(2),
    )(hbuf, hbuf, hbuf, eq, ek, dymix, lse_rows, delta_rows)


GROUP_W = SSD_WIDTH // SSD_GROUPS
HEADS_PER_GROUP = SSD_HEADS // SSD_GROUPS


def _ssd_chunk_common(xr, prev8, sm, cw, cb, dtb, avec):
    c = _conv_taps(xr, prev8, cw, cb)
    sig = _sigmoid(c)
    xa = c * sig
    dt = _softplus(sm + dtb)
    a = dt * avec
    acum = _cumsum_rows(a)
    return c, sig, xa, dt, acum


def _ssd_decays(acum, g):
    n = acum.shape[0]
    atot = acum[n - 1:n, :]
    lane0 = LANE_DT + HEADS_PER_GROUP * g
    e = _head_expand(jnp.exp(acum), lane0, HEADS_PER_GROUP, GROUP_W)
    dec = _head_expand(jnp.exp(atot - acum), lane0, HEADS_PER_GROUP, GROUP_W)
    etot = _head_expand(jnp.exp(atot), lane0, HEADS_PER_GROUP, GROUP_W)
    return e, dec, etot


def _ssd_ldec(acum, acum_t, lane, tril):
    return jnp.exp(jnp.where(tril, _col(acum, lane) - _row(acum_t, lane), NEG))


def _ssd_fwd(hbuf, conv_w, conv_b, dtb_vec, a_vec, d_exp, norm_g, *, name):
    t = hbuf.shape[0]
    L = SSD_CHUNK
    nc = t // L
    hb = L // SUBLANES
    cs = COL_SMALL // LANES
    cz = COL_Z // SSD_WIDTH

    def body(x_ref, xp_ref, z_ref, s_ref, cw_ref, cb_ref, dtb_ref, av_ref, dx_ref, ng_ref,
             yc_ref, y_ref, st_ref, state):
        i = pl.program_id(0)

        @pl.when(i == 0)
        def _():
            state[...] = jnp.zeros_like(state)

        prev = jnp.where(i == 0, 0.0, xp_ref[...])
        _, _, xa, dt, acum = _ssd_chunk_common(x_ref[...], prev, s_ref[...], cw_ref[...], cb_ref[...],
                                               dtb_ref[...], av_ref[...])
        acum_t = acum.T
        xs = xa[:, :SSD_WIDTH]
        xdt = xs * _head_expand(dt, LANE_DT, SSD_HEADS, SSD_WIDTH)
        tril = _iota((L, L), 0) >= _iota((L, L), 1)
        lane = _iota((1, LANES), 1)
        ys = []
        for g in range(SSD_GROUPS):
            bg = xa[:, SSD_WIDTH + SSD_STATE * g:SSD_WIDTH + SSD_STATE * (g + 1)].astype(BF16)
            cg = xa[:, SSD_WIDTH + SSD_STATE * (SSD_GROUPS + g):SSD_WIDTH + SSD_STATE * (SSD_GROUPS + g + 1)].astype(BF16)
            gm = _dot(cg, bg, 1, 1)
            e, dec, etot = _ssd_decays(acum, g)
            s_in = state[g]
            st_ref[0, g] = s_in
            xg = xdt[:, GROUP_W * g:GROUP_W * (g + 1)]
            y_off = e * _dot(cg, s_in.astype(BF16), 1, 0)
            state[g] = etot * s_in + _dot(bg, (dec * xg).astype(BF16), 0, 0)
            for pr in range(2):
                xp = xg[:, LANES * pr:LANES * (pr + 1)].astype(BF16)
                outs = []
                for hh in range(2):
                    h = HEADS_PER_GROUP * g + 2 * pr + hh
                    m = gm * _ssd_ldec(acum, acum_t, LANE_DT + h, tril)
                    outs.append(_dot(m.astype(BF16), xp, 1, 0))
                ys.append(jnp.where(lane < HEAD_DIM, outs[0], outs[1]) + y_off[:, LANES * pr:LANES * (pr + 1)])
        y = jnp.concatenate(ys, axis=1)
        y_ref[...] = y
        yd = y + dx_ref[...] * xs
        zz = z_ref[...]
        y2 = yd * zz * _sigmoid(zz)
        ng = ng_ref[...]
        outs = []
        for g in range(SSD_GROUPS):
            yg = y2[:, GROUP_W * g:GROUP_W * (g + 1)]
            rs = lax.rsqrt(jnp.mean(yg * yg, axis=1, keepdims=True) + RMS_EPS)
            outs.append(yg * rs * ng[:, GROUP_W * g:GROUP_W * (g + 1)])
        yc_ref[...] = jnp.concatenate(outs, axis=1)

    cdim = SSD_CONV_DIM
    vecc = pl.BlockSpec((1, cdim), lambda i: (0, 0))
    vecl = pl.BlockSpec((1, LANES), lambda i: (0, 0))
    vecw = pl.BlockSpec((1, SSD_WIDTH), lambda i: (0, 0))
    roww = pl.BlockSpec((L, SSD_WIDTH), lambda i: (i, 0))
    return pl.pallas_call(
        body, name=name, grid=(nc,),
        in_specs=[pl.BlockSpec((L, cdim), lambda i: (i, 0)),
                  pl.BlockSpec((SUBLANES, cdim), lambda i: (jnp.maximum(i * hb - 1, 0), 0)),
                  pl.BlockSpec((L, SSD_WIDTH), lambda i: (i, cz)),
                  pl.BlockSpec((L, LANES), lambda i: (i, cs)),
                  pl.BlockSpec((CONV_K, cdim), lambda i: (0, 0)), vecc, vecl, vecl, vecw, vecw],
        out_specs=[roww, roww, pl.BlockSpec((1, SSD_GROUPS, SSD_STATE, GROUP_W), lambda i: (i, 0, 0, 0))],
        out_shape=[jax.ShapeDtypeStruct((t, SSD_WIDTH), F32), jax.ShapeDtypeStruct((t, SSD_WIDTH), F32),
                   jax.ShapeDtypeStruct((nc, SSD_GROUPS, SSD_STATE, GROUP_W), F32)],
        scratch_shapes=[pltpu.VMEM((SSD_GROUPS, SSD_STATE, GROUP_W), F32)],
        compiler_params=_params(1),
    )(hbuf, hbuf, hbuf, hbuf, conv_w, conv_b, dtb_vec, a_vec, d_exp, norm_g)


def _ssd_bwd(dymix, hbuf, y_ssd, states, conv_w, conv_b, dtb_vec, a_vec, d_exp, norm_g, *, name):
    t = hbuf.shape[0]
    L = SSD_CHUNK
    nc = t // L
    hb = L // SUBLANES
    cs = COL_SMALL // LANES
    cz = COL_Z // SSD_WIDTH
    cdy = (LRU_WIDTH + ATT_WIDTH) // SSD_WIDTH
    cdim = SSD_CONV_DIM

    def body(dyc_ref, x_ref, xp_ref, z_ref, s_ref, y_ref, st_ref, cw_ref, cb_ref, dtb_ref, av_ref, dx_ref, ng_ref,
             dxr_ref, dz_ref, dsm_ref, dng_ref, dd_ref, da_ref, ddtb_ref, dcw_ref, dcb_ref,
             dstate, dnext):
        i = pl.program_id(0)
        ic = nc - 1 - i

        @pl.when(i == 0)
        def _():
            dstate[...] = jnp.zeros_like(dstate)
            dnext[...] = jnp.zeros_like(dnext)
            for ref in (dng_ref, dd_ref, da_ref, ddtb_ref, dcw_ref, dcb_ref):
                ref[...] = jnp.zeros_like(ref)

        xr = x_ref[...]
        sm = s_ref[...]
        prev = jnp.where(ic == 0, 0.0, xp_ref[...])
        avec = av_ref[...]
        c, sig, xa, dt, acum = _ssd_chunk_common(xr, prev, sm, cw_ref[...], cb_ref[...], dtb_ref[...], avec)
        acum_t = acum.T
        xs = xa[:, :SSD_WIDTH]
        dtx = _head_expand(dt, LANE_DT, SSD_HEADS, SSD_WIDTH)
        xdt = xs * dtx
        tril = _iota((L, L), 0) >= _iota((L, L), 1)
        lane = _iota((1, LANES), 1)
        hmasks = (lane < HEAD_DIM, lane >= HEAD_DIM)

        y = y_ref[...]
        dexp = dx_ref[...]
        yd = y + dexp * xs
        zz = z_ref[...]
        sz = _sigmoid(zz)
        siluz = zz * sz
        y2 = yd * siluz
        ng = ng_ref[...]
        dyc = dyc_ref[...]
        dy2s, dngs = [], []
        for g in range(SSD_GROUPS):
            sl = slice(GROUP_W * g, GROUP_W * (g + 1))
            yg = y2[:, sl]
            rs = lax.rsqrt(jnp.mean(yg * yg, axis=1, keepdims=True) + RMS_EPS)
            wv = dyc[:, sl] * ng[:, sl]
            dngs.append(jnp.sum(dyc[:, sl] * yg * rs, axis=0, keepdims=True))
            dy2s.append(rs * wv - yg * (rs * rs * rs) * jnp.mean(wv * yg, axis=1, keepdims=True))
        dy2 = jnp.concatenate(dy2s, axis=1)
        dng_ref[...] += jnp.concatenate(dngs, axis=1)
        dz_ref[...] = (dy2 * yd * (sz * (1.0 + zz * (1.0 - sz)))).astype(BF16)
        dy = dy2 * siluz
        dd_ref[...] += jnp.sum(dy * xs, axis=0, keepdims=True)

        dxs, dbs, dcs = [], [], []
        datot = jnp.zeros((1, LANES), F32)
        lanes = _iota((L, LANES), 1)
        dacum = jnp.zeros((L, LANES), F32)
        for g in range(SSD_GROUPS):
            sl = slice(GROUP_W * g, GROUP_W * (g + 1))
            bg = xa[:, SSD_WIDTH + SSD_STATE * g:SSD_WIDTH + SSD_STATE * (g + 1)].astype(BF16)
            cg = xa[:, SSD_WIDTH + SSD_STATE * (SSD_GROUPS + g):SSD_WIDTH + SSD_STATE * (SSD_GROUPS + g + 1)].astype(BF16)
            gm = _dot(cg, bg, 1, 1)
            e, dec, etot = _ssd_decays(acum, g)
            s_in = st_ref[0, g]
            ds_out = dstate[g]
            dyg = dy[:, sl]
            xg = xdt[:, sl]
            edy = (e * dyg).astype(BF16)
            dstate[g] = etot * ds_out + _dot(cg, edy, 0, 0)
            dx_state = dec * _dot(bg, ds_out.astype(BF16), 1, 0)
            y_off = e * _dot(cg, s_in.astype(BF16), 1, 0)
            dacum = dacum + _head_reduce_group(dyg * y_off - xg * dx_state, g)
            dc_off = _dot(edy, s_in.astype(BF16), 1, 1)
            db_state = _dot((dec * xg).astype(BF16), ds_out.astype(BF16), 1, 1)
            dgsum = jnp.zeros((L, L), F32)
            dx_pairs = []
            for pr in range(2):
                psl = slice(LANES * pr, LANES * (pr + 1))
                xp = xg[:, psl]
                dyp = dyg[:, psl]
                dx_pair = jnp.zeros((L, LANES), F32)
                for hh in range(2):
                    h = HEADS_PER_GROUP * g + 2 * pr + hh
                    ldec = _ssd_ldec(acum, acum_t, LANE_DT + h, tril)
                    dym = jnp.where(hmasks[hh], dyp, 0.0).astype(BF16)
                    xm = jnp.where(hmasks[hh], xp, 0.0).astype(BF16)
                    dx_pair = dx_pair + _dot((gm * ldec).astype(BF16), dym, 0, 0)
                    dml = _dot(dym, xm, 1, 1) * ldec
                    dgsum = dgsum + dml
                    qm = dml * gm
                    seg = jnp.sum(qm, axis=1, keepdims=True) - jnp.sum(qm.T, axis=1, keepdims=True)
                    dacum = dacum + jnp.where(lanes == LANE_DT + h, seg, 0.0)
                dx_pairs.append(dx_pair)
            dgb = dgsum.astype(BF16)
            dcs.append(_dot(dgb, bg, 1, 0) + dc_off)
            dbs.append(_dot(dgb, cg, 0, 0) + db_state)
            dxg = jnp.concatenate(dx_pairs, axis=1) + dx_state
            dxs.append(dxg)
            v = jnp.sum(dx_state * xg, axis=0, keepdims=True) + etot * jnp.sum(ds_out * s_in, axis=0, keepdims=True)
            datot = datot + _head_reduce_row(v, LANE_DT + HEADS_PER_GROUP * g, HEADS_PER_GROUP)
        dx = jnp.concatenate(dxs, axis=1)
        dacum = dacum + jnp.where(_iota((L, LANES), 0) == L - 1, datot, 0.0)
        da = _cumsum_rows(dacum, reverse=True)
        ddt = da * avec + _head_reduce(dx * xs, LANE_DT, SSD_HEADS)
        da_ref[...] += jnp.sum(da * dt, axis=0, keepdims=True)
        ddt_raw = ddt * _sigmoid(sm + dtb_ref[...])
        ddt_raw = jnp.where((lanes >= LANE_DT) & (lanes < LANE_DT + SSD_HEADS), ddt_raw, 0.0)
        dsm_ref[...] = ddt_raw
        ddtb_ref[...] += jnp.sum(ddt_raw, axis=0, keepdims=True)
        dxs_total = dx * dtx + dexp * dy
        dxa = jnp.concatenate([dxs_total] + dbs + dcs, axis=1)
        dc = dxa * (sig * (1.0 + c * (1.0 - sig)))
        dxr, dws = _conv_taps_bwd(dc, dnext[...], cw_ref[...], xr)
        dxr_ref[...] = dxr.astype(BF16)
        dcw_ref[...] += dws
        dcb_ref[...] += jnp.sum(dc, axis=0, keepdims=True)
        dnext[...] = dc[:SUBLANES]

    rev = lambda i: nc - 1 - i
    vecc = pl.BlockSpec((1, cdim), lambda i: (0, 0))
    vecl = pl.BlockSpec((1, LANES), lambda i: (0, 0))
    vecw = pl.BlockSpec((1, SSD_WIDTH), lambda i: (0, 0))
    cwspec = pl.BlockSpec((CONV_K, cdim), lambda i: (0, 0))
    roww = pl.BlockSpec((L, SSD_WIDTH), lambda i: (rev(i), 0))
    return pl.pallas_call(
        body, name=name, grid=(nc,),
        in_specs=[pl.BlockSpec((L, SSD_WIDTH), lambda i: (rev(i), cdy)),
                  pl.BlockSpec((L, cdim), lambda i: (rev(i), 0)),
                  pl.BlockSpec((SUBLANES, cdim), lambda i: (jnp.maximum(rev(i) * hb - 1, 0), 0)),
                  pl.BlockSpec((L, SSD_WIDTH), lambda i: (rev(i), cz)),
                  pl.BlockSpec((L, LANES), lambda i: (rev(i), cs)),
                  roww,
                  pl.BlockSpec((1, SSD_GROUPS, SSD_STATE, GROUP_W), lambda i: (rev(i), 0, 0, 0)),
                  cwspec, vecc, vecl, vecl, vecw, vecw],
        out_specs=[pl.BlockSpec((L, cdim), lambda i: (rev(i), 0)), roww,
                   pl.BlockSpec((L, LANES), lambda i: (rev(i), 0)),
                   vecw, vecw, vecl, vecl, cwspec, vecc],
        out_shape=[jax.ShapeDtypeStruct((t, cdim), BF16), jax.ShapeDtypeStruct((t, SSD_WIDTH), BF16),
                   jax.ShapeDtypeStruct((t, LANES), F32),
                   jax.ShapeDtypeStruct((1, SSD_WIDTH), F32), jax.ShapeDtypeStruct((1, SSD_WIDTH), F32),
                   jax.ShapeDtypeStruct((1, LANES), F32), jax.ShapeDtypeStruct((1, LANES), F32),
                   jax.ShapeDtypeStruct((CONV_K, cdim), F32), jax.ShapeDtypeStruct((1, cdim), F32)],
        scratch_shapes=[pltpu.VMEM((SSD_GROUPS, SSD_STATE, GROUP_W), F32), pltpu.VMEM((SUBLANES, cdim), F32)],
        compiler_params=_params(1),
    )(dymix, hbuf, hbuf, hbuf, hbuf, y_ssd, states, conv_w, conv_b, dtb_vec, a_vec, d_exp, norm_g)


def _head_reduce_group(x, g):
    return _head_reduce(x, LANE_DT + HEADS_PER_GROUP * g, HEADS_PER_GROUP)


def _head_reduce_row(v, lane0, nheads):
    colhead = _iota(v.shape, 1) // HEAD_DIM
    lane = _iota((1, LANES), 1)
    out = jnp.zeros((1, LANES), F32)
    for h in range(nheads):
        s = jnp.sum(jnp.where(colhead == h, v, 0.0), axis=1, keepdims=True)
        out = jnp.where(lane == lane0 + h, s, out)
    return out


def _exchange(inps, axes, *, mode, local=True, name):
    n = 2 ** len(axes)
    counts, out_shapes = [], []
    for a in inps:
        if mode == "gather":
            cnt, rest = a.shape[0], a.shape[1:]
        elif mode == "gather_half":
            cnt, rest = a.shape[0] // 2, a.shape[1:]
        elif mode == "a2a":
            cnt, rest = a.shape[1], a.shape[2:]
        else:
            cnt, rest = a.shape[0], a.shape[2:]
        counts.append(cnt)
        out_shapes.append(jax.ShapeDtypeStruct((n, cnt) + tuple(rest), a.dtype))
    units = sum(counts)
    na = len(inps)

    def body(*refs):
        in_refs, out_refs = refs[:na], refs[na:2 * na]
        send_sems, recv_sems, local_sems = refs[2 * na:]
        pos = {ax: lax.axis_index(ax) for ax in MESH_AXES}

        def slot_of(coord):
            s = 0
            for ax in axes:
                s = s * 2 + coord[ax]
            return s

        def src(a, it, slot):
            if mode == "gather":
                return in_refs[a].at[it]
            if mode == "gather_half":
                return in_refs[a].at[pos["c"] * counts[a] + it]
            if mode == "a2a":
                return in_refs[a].at[slot, it]
            return in_refs[a].at[it, slot]

        me = slot_of(pos)
        copies = []
        unit = 0
        for a in range(na):
            for it in range(counts[a]):
                if local:
                    cp = pltpu.make_async_copy(src(a, it, me), out_refs[a].at[me, it], local_sems.at[unit])
                    cp.start()
                    copies.append(cp)
                for delta in range(1, n):
                    coord = dict(pos)
                    for b, ax in enumerate(reversed(axes)):
                        if (delta >> b) & 1:
                            coord[ax] = 1 - pos[ax]
                    k = unit * (n - 1) + delta - 1
                    cp = pltpu.make_async_remote_copy(
                        src_ref=src(a, it, slot_of(coord)), dst_ref=out_refs[a].at[me, it],
                        send_sem=send_sems.at[k], recv_sem=recv_sems.at[k],
                        device_id=(coord["x"], coord["y"], coord["c"]), device_id_type=pl.DeviceIdType.MESH)
                    cp.start()
                    copies.append(cp)
                unit += 1
        for cp in copies:
            cp.wait()

    any_spec = pl.BlockSpec(memory_space=pl.ANY)
    return pl.pallas_call(
        body, name=name,
        in_specs=[any_spec] * na, out_specs=[any_spec] * na, out_shape=out_shapes,
        scratch_shapes=[pltpu.SemaphoreType.DMA((units * (n - 1),)), pltpu.SemaphoreType.DMA((units * (n - 1),)),
                        pltpu.SemaphoreType.DMA((units,))],
    )(*inps)


def _sum_slots(buf, out_dtype, *, name):
    n, rows, cols = buf.shape
    tm = _pick(rows, (512, 256, 128, 8))
    if rows % tm:
        tm = rows

    def body(b_ref, o_ref):
        acc = b_ref[0].astype(F32)
        for s in range(1, n):
            acc = acc + b_ref[s].astype(F32)
        o_ref[...] = acc.astype(out_dtype)

    return pl.pallas_call(
        body, name=name, grid=(pl.cdiv(rows, tm),),
        in_specs=[pl.BlockSpec((n, tm, cols), lambda i: (0, i, 0))],
        out_specs=pl.BlockSpec((tm, cols), lambda i: (i, 0)),
        out_shape=jax.ShapeDtypeStruct((rows, cols), out_dtype),
        compiler_params=_params(1),
    )(buf)


def _adamw(w, g, m, v, *, name):
    shape = w.shape
    cols = shape[-1]
    rows = w.size // cols
    w2, g2, m2, v2 = (a.reshape(rows, cols) for a in (w, g, m, v))
    tm = _pick(rows, (256, 128, 64, 32, 16, 8))
    if rows % tm:
        tm = rows
    bc1 = 1.0 - ADAM_B1 ** ADAM_STEP
    bc2 = 1.0 - ADAM_B2 ** ADAM_STEP

    def body(w_ref, g_ref, m_ref, v_ref, d_ref, nm_ref, nv_ref):
        gg = g_ref[...]
        mm = ADAM_B1 * m_ref[...] + (1.0 - ADAM_B1) * gg
        vv = ADAM_B2 * v_ref[...] + (1.0 - ADAM_B2) * (gg * gg)
        m_hat = mm / bc1
        v_hat = vv / bc2
        d_ref[...] = -ADAM_LR * (m_hat / (jnp.sqrt(v_hat) + ADAM_EPS) + ADAM_WD * w_ref[...])
        nm_ref[...] = mm
        nv_ref[...] = vv

    spec = pl.BlockSpec((tm, cols), lambda i: (i, 0))
    o = jax.ShapeDtypeStruct((rows, cols), F32)
    outs = pl.pallas_call(
        body, name=name, grid=(rows // tm,), in_specs=[spec] * 4, out_specs=[spec] * 3, out_shape=[o] * 3,
        compiler_params=_params(1),
    )(w2, g2, m2, v2)
    return tuple(a.reshape(shape) for a in outs)


def _layer_fwd(li, x, xb, pb, W):
    nm = lambda s: f"l{li}_{s}"
    sv = {"x_in_b": xb}
    g1, u1, a1 = _mm_swiglu(xb, W["ffn1_wg"], W["ffn1_wu"], name=nm("ffn1_up"))
    x1, x1b, xh1, rs1 = _mm_ln(a1, W["ffn1_wd"], x, W["ln1_g"], W["ln1_b"], rscale=ALPHA, mscale=0.5, name=nm("ffn1_down_ln"))
    hbuf = _mm(x1b, W["w_in_p"], name=nm("in_proj"))
    ya, lu, lr, lig, la, lh = _lru_fwd(hbuf, W["lru_conv_w"], W["lru_conv_b"], W["lru_wa_bd"], W["lru_ba"],
                                       W["lru_wx_bd"], W["lru_bx"], W["lru_lambda"], name=nm("lru_fwd"))
    eq, ek = _fox_prep(hbuf, W["fox_bf_vec"], name=nm("fox_prep"))
    yb, lse_rows = _fox_fwd(hbuf, eq, ek, name=nm("fox_fwd"))
    yc, yssd, states = _ssd_fwd(hbuf, W["ssd_conv_w"], W["ssd_conv_b"], W["ssd_dtb_vec"], W["ssd_a_vec"],
                                W["ssd_d_exp"], W["ssd_norm_g"], name=nm("ssd_fwd"))
    ymix = jnp.concatenate([ya, yb, yc], axis=1).astype(BF16)
    x2, x2b, xh2, rs2 = _mm_ln(ymix, W["w_out"], x1, W["ln2_g"], W["ln2_b"], rscale=ALPHA, mscale=1.0, name=nm("out_proj_ln"))
    g2, u2, a2 = _mm_swiglu(x2b, W["ffn2_wg"], W["ffn2_wu"], name=nm("ffn2_up"))
    x3, x3b, xh3, rs3 = _mm_ln(a2, W["ffn2_wd"], x2, W["ln3_g"], W["ln3_b"], rscale=ALPHA, mscale=0.5, name=nm("ffn2_down_ln"))
    x4, x4b, sg, e = _mm_pe(x3, x3b, pb, W["pe_gate_w"], W["pe_gate_b"], W["pe_proj"], name=nm("ple"))
    sv.update(g1=g1, u1=u1, a1=a1, x1b=x1b, xh1=xh1, rs1=rs1, hbuf=hbuf, lu=lu, lr=lr, lig=lig, la=la, lh=lh,
              eq=eq, ek=ek, lse_rows=lse_rows, yb=yb, yssd=yssd, states=states, ymix=ymix, x2b=x2b, xh2=xh2, rs2=rs2,
              g2=g2, u2=u2, a2=a2, x3b=x3b, xh3=xh3, rs3=rs3, sg=sg, e=e, pb=pb)
    return x4, x4b, sv


def _layer_bwd(li, dx4, sv, W):
    nm = lambda s: f"l{li}_{s}"
    G = {}
    dgp, de, dbg = _pe_bwd_elem(dx4, sv["sg"], sv["e"], name=nm("ple_bwd"))
    G["pe_gate_b"] = dbg
    G["pe_gate_w"] = _mm(sv["x3b"], dgp, ta=True, out_dtype=BF16, name=nm("d_pe_gate_w"))
    G["pe_proj"] = _mm(sv["pb"], de, ta=True, out_dtype=BF16, chip_cols=True, name=nm("d_pe_proj"))
    dr3, G["ln3_g"], G["ln3_b"] = _bwd_proj([(dgp, W["pe_gate_w"])], dx4, rscale=1.0,
                                            ln=(sv["xh3"], sv["rs3"], W["ln3_g"]), name=nm("ln3_bwd"))
    G["ffn2_wd"] = _mm(sv["a2"], dr3, ta=True, scale=0.5, out_dtype=BF16, name=nm("d_ffn2_wd"))
    dg2, du2 = _mm_swiglu_bwd(dr3, W["ffn2_wd"], sv["g2"], sv["u2"], scale=0.5, name=nm("ffn2_act_bwd"))
    G["ffn2_wg"] = _mm(sv["x2b"], dg2, ta=True, out_dtype=BF16, chip_cols=True, name=nm("d_ffn2_wg"))
    G["ffn2_wu"] = _mm(sv["x2b"], du2, ta=True, out_dtype=BF16, chip_cols=True, name=nm("d_ffn2_wu"))
    dr2, G["ln2_g"], G["ln2_b"] = _bwd_proj([(dg2, W["ffn2_wg"]), (du2, W["ffn2_wu"])], dr3, rscale=ALPHA,
                                            ln=(sv["xh2"], sv["rs2"], W["ln2_g"]), name=nm("ln2_bwd"))
    G["w_out"] = _mm(sv["ymix"], dr2, ta=True, out_dtype=BF16, name=nm("d_w_out"))
    dymix = _mm(dr2, W["w_out"], tb=True, name=nm("d_ymix"))
    hbuf = sv["hbuf"]
    (dur, dgr, G["lru_conv_w"], G["lru_conv_b"], G["lru_wa_bd"], G["lru_ba"], G["lru_wx_bd"], G["lru_bx"],
     G["lru_lambda"]) = _lru_bwd(dymix, hbuf, sv["lu"], sv["lr"], sv["lig"], sv["la"], sv["lh"],
                                 W["lru_conv_w"], W["lru_wa_bd"], W["lru_wx_bd"], W["lru_lambda"], name=nm("lru_bwd"))
    delta = _fox_delta(dymix, sv["yb"], name=nm("fox_delta"))
    delta_rows = jnp.pad(delta[:, :ATT_HEADS].T, ((0, SUBLANES - ATT_HEADS), (0, 0)))
    dk, dv, dfk, dqt, dfq = _fox_bwd(hbuf, sv["eq"], sv["ek"], dymix, sv["lse_rows"], delta_rows, name=nm("fox_bwd"))
    dq = dqt.T
    dfc = jnp.pad(dfq[:ATT_HEADS].T, ((0, 0), (0, LANES - ATT_HEADS))) - dfk
    dsm_f, G["fox_bf_vec"] = _fox_post(dfc, hbuf, W["fox_bf_vec"], name=nm("fox_post"))
    (dxr, dz, dsm_dt, G["ssd_norm_g"], G["ssd_d_exp"], G["ssd_a_vec"], G["ssd_dtb_vec"], G["ssd_conv_w"],
     G["ssd_conv_b"]) = _ssd_bwd(dymix, hbuf, sv["yssd"], sv["states"], W["ssd_conv_w"], W["ssd_conv_b"],
                                 W["ssd_dtb_vec"], W["ssd_a_vec"], W["ssd_d_exp"], W["ssd_norm_g"], name=nm("ssd_bwd"))
    t = dx4.shape[0]
    dh = jnp.concatenate([dxr.astype(BF16), dz.astype(BF16), dur.astype(BF16), dgr.astype(BF16), dq.astype(BF16),
                          dk.astype(BF16), dv.astype(BF16), (dsm_f + dsm_dt).astype(BF16),
                          jnp.zeros((t, H_WIDTH - COL_SMALL - LANES), BF16)], axis=1)
    G["w_in_p"] = _mm(sv["x1b"], dh, ta=True, name=nm("d_w_in"))
    dr1, G["ln1_g"], G["ln1_b"] = _bwd_proj([(dh, W["w_in_p"])], dr2, rscale=ALPHA,
                                            ln=(sv["xh1"], sv["rs1"], W["ln1_g"]), name=nm("ln1_bwd"))
    G["ffn1_wd"] = _mm(sv["a1"], dr1, ta=True, scale=0.5, out_dtype=BF16, name=nm("d_ffn1_wd"))
    dg1, du1 = _mm_swiglu_bwd(dr1, W["ffn1_wd"], sv["g1"], sv["u1"], scale=0.5, name=nm("ffn1_act_bwd"))
    G["ffn1_wg"] = _mm(sv["x_in_b"], dg1, ta=True, out_dtype=BF16, chip_cols=True, name=nm("d_ffn1_wg"))
    G["ffn1_wu"] = _mm(sv["x_in_b"], du1, ta=True, out_dtype=BF16, chip_cols=True, name=nm("d_ffn1_wu"))
    (dx_in,) = _bwd_proj([(dg1, W["ffn1_wg"]), (du1, W["ffn1_wu"])], dr1, rscale=ALPHA, ln=None, name=nm("x_in_bwd"))
    return dx_in, G


def _block_diag(w):
    n, b, _ = w.shape
    eye = jnp.eye(n, dtype=w.dtype)
    return (eye[:, None, :, None] * w[:, :, None, :]).reshape(n * b, n * b)


def _block_diag_extract(m):
    n, b = LRU_HEADS, HEAD_DIM
    return jnp.stack([m[b * i:b * (i + 1), b * i:b * (i + 1)] for i in range(n)])


def _lane_vec(v, lane0):
    return jnp.pad(v.astype(F32), (lane0, LANES - lane0 - v.shape[0])).reshape(1, LANES)


def _w_in_permute(w):
    d = w.shape[0]
    z = lambda n: jnp.zeros((d, n), w.dtype)
    return jnp.concatenate([w[:, 1796:2820], w[:, 1284:1796], w[:, 0:512], w[:, 512:1280],
                            w[:, 1280:1284], w[:, 2820:2828], z(LANES - 12), z(H_WIDTH - COL_SMALL - LANES)], axis=1)


def _w_in_unpermute(wp):
    return jnp.concatenate([wp[:, COL_U:COL_Q], wp[:, COL_Q:COL_SMALL], wp[:, COL_SMALL:COL_SMALL + 4],
                            wp[:, COL_Z:COL_U], wp[:, COL_XBC:COL_Z], wp[:, COL_SMALL + 4:COL_SMALL + 12]], axis=1)


def _layer_weights(li, chipw, small):
    g = lambda n: small[n][li]
    W = {n: g(n) for n in ("ln1_g", "ln1_b", "ln2_g", "ln2_b", "ln3_g", "ln3_b", "pe_gate_b", "lru_conv_w",
                           "ssd_conv_w")}
    for n in ("ffn1_wg", "ffn1_wu", "ffn2_wg", "ffn2_wu"):
        W[n] = chipw[n]
    for n in ("ffn1_wd", "ffn2_wd", "w_out", "pe_gate_w"):
        W[n] = chipw[n].reshape(-1, D_MODEL)
    W["pe_proj"] = jnp.moveaxis(chipw["pe_proj"], 0, 1).reshape(PLE_DIM, D_MODEL)
    w_in = jnp.moveaxis(chipw["w_in"][:, :, :IN_WIDTH // N_CHIPS], 0, 1).reshape(D_MODEL, IN_WIDTH)
    W["w_in_p"] = _w_in_permute(w_in)
    for n in ("lru_conv_b", "lru_ba", "lru_bx", "lru_lambda", "ssd_conv_b", "ssd_norm_g"):
        W[n] = g(n).reshape(1, -1)
    W["lru_wa_bd"] = _block_diag(g("lru_wa")).astype(BF16)
    W["lru_wx_bd"] = _block_diag(g("lru_wx")).astype(BF16)
    W["fox_bf_vec"] = _lane_vec(g("fox_bf"), LANE_F)
    W["ssd_dtb_vec"] = _lane_vec(g("ssd_dt_bias"), LANE_DT)
    W["ssd_a_vec"] = _lane_vec(-jnp.exp(g("ssd_a_log")), LANE_DT)
    W["ssd_d_exp"] = jnp.repeat(g("ssd_d"), HEAD_DIM).reshape(1, SSD_WIDTH)
    return W


def _layer_big_grads_by_chip(G):
    out = {n: G[n] for n in ("ffn1_wg", "ffn1_wu", "ffn2_wg", "ffn2_wu", "pe_proj")}
    for n in ("ffn1_wd", "ffn2_wd", "w_out", "pe_gate_w"):
        out[n] = G[n].reshape(N_CHIPS, -1, D_MODEL)
    share = IN_WIDTH // N_CHIPS
    d_w_in = jnp.moveaxis(_w_in_unpermute(G["w_in_p"]).reshape(D_MODEL, N_CHIPS, share), 1, 0)
    out["w_in"] = jnp.pad(d_w_in.astype(BF16), ((0, 0), (0, 0), (0, SHARE - share)))
    return out


def _layer_small_grads(G, W):
    out = {n: G[n] for n in ("lru_conv_w", "ssd_conv_w")}
    for n in ("ln1_g", "ln1_b", "ln2_g", "ln2_b", "ln3_g", "ln3_b", "pe_gate_b", "lru_conv_b", "lru_ba", "lru_bx",
              "lru_lambda", "ssd_conv_b", "ssd_norm_g"):
        out[n] = G[n].reshape(-1)
    out["lru_wa"] = _block_diag_extract(G["lru_wa_bd"])
    out["lru_wx"] = _block_diag_extract(G["lru_wx_bd"])
    out["fox_bf"] = G["fox_bf_vec"][0, LANE_F:LANE_F + ATT_HEADS]
    out["ssd_dt_bias"] = G["ssd_dtb_vec"][0, LANE_DT:LANE_DT + SSD_HEADS]
    out["ssd_a_log"] = G["ssd_a_vec"][0, LANE_DT:LANE_DT + SSD_HEADS] * W["ssd_a_vec"][0, LANE_DT:LANE_DT + SSD_HEADS]
    out["ssd_d"] = G["ssd_d_exp"].reshape(SSD_HEADS, HEAD_DIM).sum(axis=1)
    return out


def _local_step(x, p, target, Ws):
    saves = []
    xb = x.astype(BF16)
    for li in range(DEPTH):
        x, xb, sv = _layer_fwd(li, x, xb, p[li].astype(BF16), Ws[li])
        saves.append(sv)
    dx, loss = _loss_kernel(x, target, name="loss")
    big = [None] * DEPTH
    small = [None] * DEPTH
    for li in reversed(range(DEPTH)):
        dx, G = _layer_bwd(li, dx, saves[li], Ws[li])
        big[li] = _layer_big_grads_by_chip(G)
        small[li] = _layer_small_grads(G, Ws[li])
    stacked = {n: jnp.stack([small[li][n] for li in range(DEPTH)]) for n in small[0]}
    return loss, dx, big, stacked


WEIGHTS = ['ln1_g', 'ln1_b', 'ffn1_wg', 'ffn1_wu', 'ffn1_wd', 'w_in', 'lru_conv_w', 'lru_conv_b', 'lru_wa', 'lru_ba',
           'lru_wx', 'lru_bx', 'lru_lambda', 'fox_bf', 'ssd_conv_w', 'ssd_conv_b', 'ssd_dt_bias', 'ssd_a_log', 'ssd_d',
           'ssd_norm_g', 'w_out', 'ln2_g', 'ln2_b', 'ffn2_wg', 'ffn2_wu', 'ffn2_wd', 'ln3_g', 'ln3_b', 'pe_proj',
           'pe_gate_w', 'pe_gate_b']
CLASSES = (("ffn1_wg", "ffn1_wu", "ffn2_wg", "ffn2_wu", "w_in"),
           ("ffn1_wd", "ffn2_wd"),
           ("w_out", "pe_gate_w"),
           ("pe_proj",))
CLASS_PAD_AXIS = (1, 0, None, None)
BIG = {n: ci for ci, names in enumerate(CLASSES) for n in names}
SMALL_SHARDED = {'lru_conv_w': 2, 'ssd_conv_w': 2}
PACK_COLS = 1024


def _unshard(seg, axis):
    moved = jnp.moveaxis(seg, 0, axis)
    shp = list(moved.shape)
    shp[axis:axis + 2] = [shp[axis] * shp[axis + 1]]
    return moved.reshape(shp)


def _pad_axis(a, axis, size):
    if axis is None or a.shape[axis] == size:
        return a
    pads = [(0, 0)] * a.ndim
    pads[axis] = (0, size - a.shape[axis])
    return jnp.pad(a, pads)


def _pack(arrs, dtype, cols):
    flat = jnp.concatenate([a.astype(dtype).reshape(-1) for a in arrs])
    pad = (-flat.shape[0]) % cols
    if pad:
        flat = jnp.concatenate([flat, jnp.zeros((pad,), dtype)])
    return flat.reshape(-1, cols)


def _unpack(flat, shapes):
    out, off = [], 0
    for s in shapes:
        n = math.prod(s)
        out.append(flat[off:off + n].reshape(s))
        off += n
    return out


def kernel(x, p, ln1_g, ln1_b, ffn1_wg, ffn1_wu, ffn1_wd, w_in, lru_conv_w, lru_conv_b, lru_wa, lru_ba, lru_wx, lru_bx, lru_lambda, fox_bf, ssd_conv_w, ssd_conv_b, ssd_dt_bias, ssd_a_log, ssd_d, ssd_norm_g, w_out, ln2_g, ln2_b, ffn2_wg, ffn2_wu, ffn2_wd, ln3_g, ln3_b, pe_proj, pe_gate_w, pe_gate_b, loss_target, m_ln1_g, m_ln1_b, m_ffn1_wg, m_ffn1_wu, m_ffn1_wd, m_w_in, m_lru_conv_w, m_lru_conv_b, m_lru_wa, m_lru_ba, m_lru_wx, m_lru_bx, m_lru_lambda, m_fox_bf, m_ssd_conv_w, m_ssd_conv_b, m_ssd_dt_bias, m_ssd_a_log, m_ssd_d, m_ssd_norm_g, m_w_out, m_ln2_g, m_ln2_b, m_ffn2_wg, m_ffn2_wu, m_ffn2_wd, m_ln3_g, m_ln3_b, m_pe_proj, m_pe_gate_w, m_pe_gate_b, v_ln1_g, v_ln1_b, v_ffn1_wg, v_ffn1_wu, v_ffn1_wd, v_w_in, v_lru_conv_w, v_lru_conv_b, v_lru_wa, v_lru_ba, v_lru_wx, v_lru_bx, v_lru_lambda, v_fox_bf, v_ssd_conv_w, v_ssd_conv_b, v_ssd_dt_bias, v_ssd_a_log, v_ssd_d, v_ssd_norm_g, v_w_out, v_ln2_g, v_ln2_b, v_ffn2_wg, v_ffn2_wu, v_ffn2_wd, v_ln3_g, v_ln3_b, v_pe_proj, v_pe_gate_w, v_pe_gate_b):
    args = locals()
    w_loc = {n: args[n] for n in WEIGHTS}
    m_loc = {n: args["m_" + n] for n in WEIGHTS}
    v_loc = {n: args["v_" + n] for n in WEIGHTS}
    chip = 2 * lax.axis_index("x") + lax.axis_index("y")
    core = lax.axis_index("c")
    big = list(BIG)
    small_sh = list(SMALL_SHARDED)
    small_rep = [n for n in WEIGHTS if n not in BIG and n not in SMALL_SHARDED]

    srcs = [jnp.stack([_pad_axis(w_loc[n][li].astype(BF16), pad, SHARE) for li in range(DEPTH) for n in names])
            for names, pad in zip(CLASSES, CLASS_PAD_AXIS)]
    halves = _exchange(srcs, ("x", "y"), mode="gather_half", local=False, name="gather_w_chips")
    halves = [lax.dynamic_update_index_in_dim(
                  h, lax.dynamic_index_in_dim(s.reshape((2, -1) + s.shape[1:]), core, 0, keepdims=False), chip, 0)
              for h, s in zip(halves, srcs)]
    both = _exchange([h.reshape((-1,) + h.shape[2:]) for h in halves], ("c",), mode="gather", local=False,
                     name="gather_w_cores")
    both = [lax.dynamic_update_index_in_dim(b, h.reshape(b.shape[1:]), core, 0) for b, h in zip(both, halves)]
    chipw = [{} for _ in range(DEPTH)]
    for names, b in zip(CLASSES, both):
        hn = len(names)
        b = b.reshape((2, N_CHIPS, hn) + b.shape[2:])
        for li in range(DEPTH):
            for j, n in enumerate(names):
                chipw[li][n] = b[li, :, j]
    small = {n: w_loc[n] for n in small_rep}
    spack = _pack([w_loc[n] for n in small_sh], F32, LANES)
    (sg,) = _exchange([spack[None]], ("x", "y"), mode="gather", name="gather_conv_w")
    for n, seg in zip(small_sh, _unpack_rows(sg.reshape(N_CHIPS, -1), [w_loc[n].shape for n in small_sh])):
        small[n] = _unshard(seg, SMALL_SHARDED[n])
    Ws = [_layer_weights(li, chipw[li], small) for li in range(DEPTH)]

    loss, grad_x, g_big, g_small = _local_step(x[0], p[:, 0], loss_target[0], Ws)
    loss = lax.psum(loss[0, 0], MESH_AXES)

    gcls = [jnp.stack([g_big[li][n] for li in range(DEPTH) for n in names]) for names in CLASSES]
    gcls = [g.reshape((2, -1) + g.shape[1:]) for g in gcls]
    pair = _exchange(gcls, ("c",), mode="a2a", local=False, name="reduce_cores")
    pair = [lax.dynamic_update_index_in_dim(pr, lax.dynamic_index_in_dim(g, core, 0, keepdims=False), core, 0)
            for pr, g in zip(pair, gcls)]
    s2 = [_sum_slots(pr.reshape(2, -1, pr.shape[-1]), BF16, name=f"reduce_cores_sum{ci}").reshape(pr.shape[1:])
          for ci, pr in enumerate(pair)]
    quad = _exchange(s2, ("x", "y"), mode="a2a_inner", local=False, name="reduce_chips")
    quad = [lax.dynamic_update_index_in_dim(q, lax.dynamic_index_in_dim(s, chip, 1, keepdims=False), chip, 0)
            for q, s in zip(quad, s2)]
    red = [_sum_slots(q.reshape(N_CHIPS, -1, q.shape[-1]), F32, name=f"reduce_chips_sum{ci}").reshape(q.shape[1:])
           for ci, q in enumerate(quad)]
    shared = _exchange(red, ("c",), mode="gather", local=False, name="reduce_share")
    shared = [lax.dynamic_update_index_in_dim(sh, r, core, 0) for sh, r in zip(shared, red)]
    g_red = {}
    for names, sh in zip(CLASSES, shared):
        sh = sh.reshape((-1,) + sh.shape[2:])
        for j, n in enumerate(names):
            g = jnp.stack([sh[li * len(names) + j] for li in range(DEPTH)])
            g_red[n] = g[tuple(slice(0, s) for s in w_loc[n].shape)]
    small_all = small_rep + small_sh
    sgp = _pack([g_small[n] for n in small_all], F32, PACK_COLS)
    (sall,) = _exchange([sgp[None]], MESH_AXES, mode="gather", name="reduce_small")
    sred = _sum_slots(sall.reshape((2 ** len(MESH_AXES),) + sgp.shape), F32, name="reduce_small_sum").reshape(-1)
    for n, g in zip(small_all, _unpack(sred, [g_small[n].shape for n in small_all])):
        if n in SMALL_SHARDED:
            width = w_loc[n].shape[-1]
            g = lax.dynamic_slice_in_dim(g, chip * width, width, axis=SMALL_SHARDED[n])
        g_red[n] = g

    delta, new_m, new_v = {}, {}, {}
    for n in big:
        delta[n], new_m[n], new_v[n] = _adamw(w_loc[n], g_red[n], m_loc[n], v_loc[n], name="adamw_" + n)
    shapes = [w_loc[n].shape for n in small_all]
    packs = [_pack([d[n] for n in small_all], F32, LANES) for d in (w_loc, g_red, m_loc, v_loc)]
    outs = _adamw(*packs, name="adamw_small")
    for d, o in zip((delta, new_m, new_v), outs):
        for n, a in zip(small_all, _unpack(o.reshape(-1), shapes)):
            d[n] = a
    return (loss, grad_x[None], *[g_red[n] for n in WEIGHTS], *[delta[n] for n in WEIGHTS],
            *[new_m[n] for n in WEIGHTS], *[new_v[n] for n in WEIGHTS])


def _unpack_rows(gathered, shapes):
    out, off = [], 0
    for s in shapes:
        n = math.prod(s)
        out.append(gathered[:, off:off + n].reshape((N_CHIPS,) + tuple(s)))
        off += n
    return out
```

```python
import functools
import math

import jax
import jax.numpy as jnp
from jax import lax
from jax.experimental import pallas as pl
from jax.experimental.pallas import tpu as pltpu

F32 = jnp.float32
BF16 = jnp.bfloat16

D_MODEL = 1024
DEPTH = 2
PLE_DIM = 256
HEAD_DIM = 64
LRU_WIDTH = 256
LRU_HEADS = 4
LRU_C = 8.0
CONV_K = 4
ATT_WIDTH = 256
ATT_HEADS = 4
SSD_WIDTH = 512
SSD_HEADS = 8
SSD_GROUPS = 2
SSD_STATE = 128
SSD_CHUNK = 128
SSD_CONV_DIM = 1024
FFN_DIM = 2816
ALPHA = (2.0 * DEPTH) ** 0.25
LN_EPS = 1e-5
RMS_EPS = 1e-5
IN_WIDTH = 2828
ADAM_LR = 0.001
ADAM_B1 = 0.9
ADAM_B2 = 0.999
ADAM_EPS = 1e-08
ADAM_WD = 0.01
ADAM_STEP = 10

H_WIDTH = 3072
COL_XBC, COL_Z, COL_U, COL_G, COL_Q, COL_K, COL_V, COL_SMALL = 0, 1024, 1536, 1792, 2048, 2304, 2560, 2816
LANE_F = 0
LANE_DT = 4
LANES = 128
SUBLANES = 8
NEG = -1e30

VMEM_LIMIT = 48 * 1024 * 1024

N_CHIPS = 4
MESH_AXES = ("x", "y", "c")
SHARE = 768


def _params(n):
    return pltpu.CompilerParams(dimension_semantics=("arbitrary",) * n, vmem_limit_bytes=VMEM_LIMIT)


def _pick(n, cands):
    for c in cands:
        if n % c == 0:
            return c
    return n


def _iota(shape, dim):
    return lax.broadcasted_iota(jnp.int32, shape, dim)


def _shift_down(x, s, prev8):
    if s == 0:
        return x
    r = pltpu.roll(x, s, 0)
    pr = pltpu.roll(prev8, s, 0)
    head = jnp.where(_iota(pr.shape, 0) < s, pr, r[:SUBLANES])
    return jnp.concatenate([head, r[SUBLANES:]], axis=0)


def _shift_up(x, s, next8):
    if s == 0:
        return x
    n = x.shape[0]
    r = pltpu.roll(x, n - s, 0)
    nr = pltpu.roll(next8, SUBLANES - s, 0)
    tail = jnp.where(_iota(nr.shape, 0) >= SUBLANES - s, nr, r[n - SUBLANES:])
    return jnp.concatenate([r[:n - SUBLANES], tail], axis=0)


def _scan_fwd(a, b):
    n = a.shape[0]
    row = _iota(a.shape, 0)
    d = 1
    while d < n:
        keep = row >= d
        a_s = jnp.where(keep, pltpu.roll(a, d, 0), 1.0)
        b_s = jnp.where(keep, pltpu.roll(b, d, 0), 0.0)
        b = a * b_s + b
        a = a * a_s
        d *= 2
    return a, b


def _scan_bwd(a, b):
    n = a.shape[0]
    row = _iota(a.shape, 0)
    d = 1
    while d < n:
        keep = row < n - d
        a_s = jnp.where(keep, pltpu.roll(a, n - d, 0), 1.0)
        b_s = jnp.where(keep, pltpu.roll(b, n - d, 0), 0.0)
        b = a * b_s + b
        a = a * a_s
        d *= 2
    return a, b


def _cumsum_rows(x, reverse=False):
    n = x.shape[0]
    row = _iota(x.shape, 0)
    d = 1
    while d < n:
        if reverse:
            x = x + jnp.where(row < n - d, pltpu.roll(x, n - d, 0), 0.0)
        else:
            x = x + jnp.where(row >= d, pltpu.roll(x, d, 0), 0.0)
        d *= 2
    return x


def _col(x, lane):
    return jnp.sum(jnp.where(_iota(x.shape, 1) == lane, x, 0.0), axis=1, keepdims=True)


def _row(x, r):
    return jnp.sum(jnp.where(_iota(x.shape, 0) == r, x, 0.0), axis=0, keepdims=True)


def _sigmoid(x):
    return jax.nn.sigmoid(x)


def _softplus(x):
    return jnp.maximum(x, 0.0) + jnp.log(1.0 + jnp.exp(-jnp.abs(x)))


def _gelu_and_grad(x):
    c0 = math.sqrt(2.0 / math.pi)
    inner = c0 * (x + 0.044715 * x * x * x)
    t = jnp.tanh(inner)
    g = 0.5 * x * (1.0 + t)
    dg = 0.5 * (1.0 + t) + 0.5 * x * (1.0 - t * t) * c0 * (1.0 + 3.0 * 0.044715 * x * x)
    return g, dg


def _dot(a, b, ca, cb):
    return lax.dot_general(a, b, (((ca,), (cb,)), ((), ())), preferred_element_type=F32)


def _conv_taps(xr, prev8, w, bias):
    y = bias + w[CONV_K - 1:CONV_K, :] * xr
    for j in range(CONV_K - 1):
        y = y + w[j:j + 1, :] * _shift_down(xr, CONV_K - 1 - j, prev8)
    return y


def _conv_taps_bwd(dy, next8, w, xr):
    dx = None
    dws = []
    for j in range(CONV_K):
        sh = _shift_up(dy, CONV_K - 1 - j, next8)
        term = w[j:j + 1, :] * sh
        dx = term if dx is None else dx + term
        dws.append(jnp.sum(sh * xr, axis=0, keepdims=True))
    return dx, jnp.concatenate(dws, axis=0)


def _head_expand(v, lane0, nheads, width):
    rows = v.shape[0]
    colhead = _iota((rows, width), 1) // HEAD_DIM
    out = jnp.zeros((rows, width), F32)
    for h in range(nheads):
        out = jnp.where(colhead == h, _col(v, lane0 + h), out)
    return out


def _head_reduce(x, lane0, nheads):
    rows = x.shape[0]
    colhead = _iota(x.shape, 1) // HEAD_DIM
    lane = _iota((rows, LANES), 1)
    out = jnp.zeros((rows, LANES), F32)
    for h in range(nheads):
        s = jnp.sum(jnp.where(colhead == h, x, 0.0), axis=1, keepdims=True)
        out = jnp.where(lane == lane0 + h, s, out)
    return out


def _mm(a, b, *, ta=False, tb=False, scale=1.0, out_dtype=F32, chip_cols=False, name):
    if ta:
        kk, m = a.shape
    else:
        m, kk = a.shape
    n = b.shape[0] if tb else b.shape[1]
    tm = _pick(m, (1024, 512, 256, 128))
    tn = _pick(n // N_CHIPS, (768, 256, 128)) if chip_cols else _pick(n, (1024, 768, 512, 256, 128))
    tk = _pick(kk, (1024, 768, 512, 256, 128))
    nk = kk // tk
    dn_a = 0 if ta else 1
    dn_b = 1 if tb else 0
    if chip_cols:
        per = n // N_CHIPS // tn
        out_spec = pl.BlockSpec((None, tm, tn), lambda i, j, k: (j // per, i, j % per))
        out_shape = jax.ShapeDtypeStruct((N_CHIPS, m, n // N_CHIPS), out_dtype)
    else:
        out_spec = pl.BlockSpec((tm, tn), lambda i, j, k: (i, j))
        out_shape = jax.ShapeDtypeStruct((m, n), out_dtype)

    def body(a_ref, b_ref, o_ref, acc):
        k = pl.program_id(2)

        @pl.when(k == 0)
        def _():
            acc[...] = jnp.zeros_like(acc)

        acc[...] += _dot(a_ref[...].astype(BF16), b_ref[...].astype(BF16), dn_a, dn_b)

        @pl.when(k == nk - 1)
        def _():
            o_ref[...] = (acc[...] * scale).astype(out_dtype)

    a_spec = pl.BlockSpec((tk, tm), lambda i, j, k: (k, i)) if ta else pl.BlockSpec((tm, tk), lambda i, j, k: (i, k))
    b_spec = pl.BlockSpec((tn, tk), lambda i, j, k: (j, k)) if tb else pl.BlockSpec((tk, tn), lambda i, j, k: (k, j))
    return pl.pallas_call(
        body, name=name, grid=(m // tm, n // tn, nk),
        in_specs=[a_spec, b_spec],
        out_specs=out_spec, out_shape=out_shape,
        scratch_shapes=[pltpu.VMEM((tm, tn), F32)],
        compiler_params=_params(3),
    )(a, b)


def _mm_swiglu(xb, wg, wu, *, name):
    t, d = xb.shape
    share = wg.shape[2]
    n = N_CHIPS * share
    tm = _pick(t, (512, 256, 128))
    tn = _pick(share, (768, 256, 128))
    per = share // tn

    def body(x_ref, wg_ref, wu_ref, g_ref, u_ref, a_ref):
        x = x_ref[...]
        g = _dot(x, wg_ref[...], 1, 0)
        u = _dot(x, wu_ref[...], 1, 0)
        g_ref[...] = g.astype(BF16)
        u_ref[...] = u.astype(BF16)
        a_ref[...] = (g * _sigmoid(g) * u).astype(BF16)

    o = jax.ShapeDtypeStruct((t, n), BF16)
    ospec = pl.BlockSpec((tm, tn), lambda j, i: (i, j))
    return pl.pallas_call(
        body, name=name, grid=(n // tn, t // tm),
        in_specs=[pl.BlockSpec((tm, d), lambda j, i: (i, 0)),
                  pl.BlockSpec((None, d, tn), lambda j, i: (j // per, 0, j % per)),
                  pl.BlockSpec((None, d, tn), lambda j, i: (j // per, 0, j % per))],
        out_specs=[ospec, ospec, ospec], out_shape=[o, o, o],
        compiler_params=_params(2),
    )(xb, wg, wu)


def _mm_swiglu_bwd(dr, wd, g, u, *, scale, name):
    t, d = dr.shape
    n = wd.shape[0]
    tm = _pick(t, (512, 256, 128))
    tn = _pick(n, (768, 256, 128))

    def body(dr_ref, wd_ref, g_ref, u_ref, dg_ref, du_ref):
        da = _dot(dr_ref[...].astype(BF16), wd_ref[...], 1, 1) * scale
        gg = g_ref[...].astype(F32)
        uu = u_ref[...].astype(F32)
        sg = _sigmoid(gg)
        dg_ref[...] = (da * uu * (sg * (1.0 + gg * (1.0 - sg)))).astype(BF16)
        du_ref[...] = (da * gg * sg).astype(BF16)

    o = jax.ShapeDtypeStruct((t, n), BF16)
    ospec = pl.BlockSpec((tm, tn), lambda j, i: (i, j))
    return pl.pallas_call(
        body, name=name, grid=(n // tn, t // tm),
        in_specs=[pl.BlockSpec((tm, d), lambda j, i: (i, 0)),
                  pl.BlockSpec((tn, d), lambda j, i: (j, 0)),
                  ospec, ospec],
        out_specs=[ospec, ospec], out_shape=[o, o],
        compiler_params=_params(2),
    )(dr, wd, g, u)


def _mm_ln(a, w, resid, gain, bias, *, rscale, mscale, name):
    t, kk = a.shape
    d = w.shape[1]
    tm = _pick(t, (512, 256, 128))
    tk = kk
    nk = kk // tk

    def body(a_ref, w_ref, r_ref, g_ref, b_ref, y_ref, yb_ref, xh_ref, rs_ref, acc):
        k = pl.program_id(1)

        @pl.when(k == 0)
        def _():
            acc[...] = jnp.zeros_like(acc)

        acc[...] += _dot(a_ref[...].astype(BF16), w_ref[...], 1, 0)

        @pl.when(k == nk - 1)
        def _():
            r = rscale * r_ref[...] + mscale * acc[...]
            mu = jnp.mean(r, axis=1, keepdims=True)
            xc = r - mu
            var = jnp.mean(xc * xc, axis=1, keepdims=True)
            rstd = lax.rsqrt(var + LN_EPS)
            xh = xc * rstd
            y = xh * g_ref[...] + b_ref[...]
            y_ref[...] = y
            yb_ref[...] = y.astype(BF16)
            xh_ref[...] = xh
            rs_ref[...] = rstd

    row = pl.BlockSpec((tm, d), lambda i, k: (i, 0))
    vec = pl.BlockSpec((1, d), lambda i, k: (0, 0))
    return pl.pallas_call(
        body, name=name, grid=(t // tm, nk),
        in_specs=[pl.BlockSpec((tm, tk), lambda i, k: (i, k)),
                  pl.BlockSpec((tk, d), lambda i, k: (k, 0)), row, vec, vec],
        out_specs=[row, row, row, pl.BlockSpec((tm, 1), lambda i, k: (i, 0))],
        out_shape=[jax.ShapeDtypeStruct((t, d), F32), jax.ShapeDtypeStruct((t, d), BF16),
                   jax.ShapeDtypeStruct((t, d), F32), jax.ShapeDtypeStruct((t, 1), F32)],
        scratch_shapes=[pltpu.VMEM((tm, d), F32)],
        compiler_params=_params(2),
    )(a, w, resid, gain.reshape(1, d), bias.reshape(1, d))


def _bwd_proj(pairs, resid, *, rscale, ln, name):
    t, kk = pairs[0][0].shape
    d = pairs[0][1].shape[-2]
    tm = _pick(t, (512, 256, 128))
    tk = _pick(pairs[0][1].shape[-1], (1024, 768, 512, 256, 128))
    nk = kk // tk
    nt = t // tm
    npair = len(pairs)
    has_ln = ln is not None

    def body(*refs):
        ab = refs[:2 * npair]
        r_ref = refs[2 * npair]
        pos = 2 * npair + 1
        if has_ln:
            xh_ref, rs_ref, g_ref = refs[pos:pos + 3]
            pos += 3
            o_ref, ob_ref, dg_ref, db_ref = refs[pos:pos + 4]
            pos += 4
        else:
            o_ref = refs[pos]
            pos += 1
        acc = refs[pos]
        i = pl.program_id(0)
        k = pl.program_id(1)

        @pl.when(k == 0)
        def _():
            acc[...] = jnp.zeros_like(acc)

        for q in range(npair):
            acc[...] += _dot(ab[2 * q][...].astype(BF16), ab[2 * q + 1][...], 1, 1)

        @pl.when(k == nk - 1)
        def _():
            dy = rscale * r_ref[...] + acc[...]
            if not has_ln:
                o_ref[...] = dy
                return
            xh = xh_ref[...]
            w = dy * g_ref[...]
            m1 = jnp.mean(w, axis=1, keepdims=True)
            m2 = jnp.mean(w * xh, axis=1, keepdims=True)
            dr = rs_ref[...] * (w - m1 - xh * m2)
            o_ref[...] = dr
            ob_ref[...] = dr.astype(BF16)

            @pl.when(i == 0)
            def _():
                dg_ref[...] = jnp.zeros_like(dg_ref)
                db_ref[...] = jnp.zeros_like(db_ref)

            dg_ref[...] += jnp.sum(dy * xh, axis=0, keepdims=True)
            db_ref[...] += jnp.sum(dy, axis=0, keepdims=True)

    row = pl.BlockSpec((tm, d), lambda i, k: (i, 0))
    vec = pl.BlockSpec((1, d), lambda i, k: (0, 0))
    in_specs, args = [], []
    for a, b in pairs:
        if b.ndim == 3:
            per = b.shape[2] // tk
            b_spec = pl.BlockSpec((None, d, tk), lambda i, k, per=per: (k // per, 0, k % per))
        else:
            b_spec = pl.BlockSpec((d, tk), lambda i, k: (0, k))
        in_specs += [pl.BlockSpec((tm, tk), lambda i, k: (i, k)), b_spec]
        args += [a, b]
    in_specs.append(row)
    args.append(resid)
    out_specs = [row]
    out_shape = [jax.ShapeDtypeStruct((t, d), F32)]
    if has_ln:
        xh, rs, gain = ln
        in_specs += [row, pl.BlockSpec((tm, 1), lambda i, k: (i, 0)), vec]
        args += [xh, rs, gain.reshape(1, d)]
        out_specs += [row, vec, vec]
        out_shape += [jax.ShapeDtypeStruct((t, d), BF16)] + [jax.ShapeDtypeStruct((1, d), F32)] * 2
    return pl.pallas_call(
        body, name=name, grid=(nt, nk), in_specs=in_specs, out_specs=out_specs, out_shape=out_shape,
        scratch_shapes=[pltpu.VMEM((tm, d), F32)],
        compiler_params=_params(2),
    )(*args)


def _mm_pe(x3, x3b, pb, wgate, bgate, wproj, *, name):
    t, d = x3.shape
    pd = pb.shape[1]
    tm = _pick(t, (512, 256, 128))
    tn = _pick(d, (512, 256, 128))

    def body(x_ref, xb_ref, p_ref, wg_ref, bg_ref, wp_ref, y_ref, yb_ref, sg_ref, e_ref):
        sg = _sigmoid(_dot(xb_ref[...], wg_ref[...], 1, 0) + bg_ref[...])
        e = _dot(p_ref[...], wp_ref[...], 1, 0)
        y = x_ref[...] + sg * e
        y_ref[...] = y
        yb_ref[...] = y.astype(BF16)
        sg_ref[...] = sg.astype(BF16)
        e_ref[...] = e.astype(BF16)

    ospec = pl.BlockSpec((tm, tn), lambda i, j: (i, j))
    ob = jax.ShapeDtypeStruct((t, d), BF16)
    return pl.pallas_call(
        body, name=name, grid=(t // tm, d // tn),
        in_specs=[ospec, pl.BlockSpec((tm, d), lambda i, j: (i, 0)), pl.BlockSpec((tm, pd), lambda i, j: (i, 0)),
                  pl.BlockSpec((d, tn), lambda i, j: (0, j)), pl.BlockSpec((1, tn), lambda i, j: (0, j)),
                  pl.BlockSpec((pd, tn), lambda i, j: (0, j))],
        out_specs=[ospec, ospec, ospec, ospec],
        out_shape=[jax.ShapeDtypeStruct((t, d), F32), ob, ob, ob],
        compiler_params=_params(2),
    )(x3, x3b, pb, wgate, bgate.reshape(1, d), wproj)


def _pe_bwd_elem(dx4, sg, e, *, name):
    t, d = dx4.shape
    tm = _pick(t, (512, 256, 128))

    def body(dx_ref, sg_ref, e_ref, dgp_ref, de_ref, db_ref):
        dx = dx_ref[...]
        s = sg_ref[...].astype(F32)
        dgp = dx * e_ref[...].astype(F32) * s * (1.0 - s)
        dgp_ref[...] = dgp.astype(BF16)
        de_ref[...] = (dx * s).astype(BF16)

        @pl.when(pl.program_id(0) == 0)
        def _():
            db_ref[...] = jnp.zeros_like(db_ref)

        db_ref[...] += jnp.sum(dgp, axis=0, keepdims=True)

    row = pl.BlockSpec((tm, d), lambda i: (i, 0))
    ob = jax.ShapeDtypeStruct((t, d), BF16)
    return pl.pallas_call(
        body, name=name, grid=(t // tm,), in_specs=[row, row, row],
        out_specs=[row, row, pl.BlockSpec((1, d), lambda i: (0, 0))],
        out_shape=[ob, ob, jax.ShapeDtypeStruct((1, d), F32)],
        compiler_params=_params(1),
    )(dx4, sg, e)


def _loss_kernel(y, target, *, name):
    t, d = y.shape
    tm = _pick(t, (512, 256, 128))

    def body(y_ref, t_ref, dy_ref, l_ref):
        diff = y_ref[...] - t_ref[...]
        dy_ref[...] = diff * (1.0 / d)

        @pl.when(pl.program_id(0) == 0)
        def _():
            l_ref[...] = jnp.zeros_like(l_ref)

        part = jnp.sum(jnp.mean(diff * diff, axis=1, keepdims=True), axis=0, keepdims=True)
        l_ref[...] += 0.5 * part

    row = pl.BlockSpec((tm, d), lambda i: (i, 0))
    return pl.pallas_call(
        body, name=name, grid=(t // tm,), in_specs=[row, row],
        out_specs=[row, pl.BlockSpec((1, 1), lambda i: (0, 0))],
        out_shape=[jax.ShapeDtypeStruct((t, d), F32), jax.ShapeDtypeStruct((1, 1), F32)],
        compiler_params=_params(1),
    )(y, target)


LRU_TM = 256


def _lru_gate_terms(r, lam):
    sp = _softplus(-lam)
    la = -LRU_C * r * sp
    a = jnp.exp(la)
    em = jnp.tanh(la) * (jnp.exp(2.0 * la) + 1.0)
    s = jnp.sqrt(-em)
    return la, a, s, sp


def _lru_fwd(hbuf, conv_w, conv_b, wa, ba, wx, bx, lam, *, name):
    t = hbuf.shape[0]
    w = LRU_WIDTH
    tm = _pick(t, (LRU_TM, 128))
    cu, cg = COL_U // w, COL_G // w
    hb = tm // SUBLANES

    def body(u_ref, up_ref, g_ref, cw_ref, cb_ref, wa_ref, ba_ref, wx_ref, bx_ref, lam_ref,
             y_ref, u_out, r_out, i_out, a_out, h_out, carry):
        i = pl.program_id(0)

        @pl.when(i == 0)
        def _():
            carry[...] = jnp.zeros_like(carry)

        prev = jnp.where(i == 0, 0.0, up_ref[...])
        u = _conv_taps(u_ref[...], prev, cw_ref[...], cb_ref[...])
        ub = u.astype(BF16)
        r = _sigmoid(_dot(ub, wa_ref[...], 1, 0) + ba_ref[...])
        ig = _sigmoid(_dot(ub, wx_ref[...], 1, 0) + bx_ref[...])
        _, a, s, _ = _lru_gate_terms(r, lam_ref[...])
        b = s * (ig * u)
        acum, hs = _scan_fwd(a, b)
        h = hs + acum * carry[0:1, :]
        carry[...] = jnp.broadcast_to(h[tm - 1:tm, :], carry.shape)
        gl, _ = _gelu_and_grad(g_ref[...])
        y_ref[...] = h * gl
        u_out[...] = u
        r_out[...] = r
        i_out[...] = ig
        a_out[...] = a
        h_out[...] = h

    row = pl.BlockSpec((tm, w), lambda i: (i, 0))
    vec = pl.BlockSpec((1, w), lambda i: (0, 0))
    mat = pl.BlockSpec((w, w), lambda i: (0, 0))
    o = jax.ShapeDtypeStruct((t, w), F32)
    return pl.pallas_call(
        body, name=name, grid=(t // tm,),
        in_specs=[pl.BlockSpec((tm, w), lambda i: (i, cu)),
                  pl.BlockSpec((SUBLANES, w), lambda i: (jnp.maximum(i * hb - 1, 0), cu)),
                  pl.BlockSpec((tm, w), lambda i: (i, cg)),
                  pl.BlockSpec((CONV_K, w), lambda i: (0, 0)), vec, mat, vec, mat, vec, vec],
        out_specs=[row] * 6, out_shape=[o] * 6,
        scratch_shapes=[pltpu.VMEM((SUBLANES, w), F32)],
        compiler_params=_params(1),
    )(hbuf, hbuf, hbuf, conv_w, conv_b, wa, ba, wx, bx, lam)


def _lru_bwd(dymix, hbuf, u, r, ig, a, h, conv_w, wa, wx, lam, *, name):
    t = hbuf.shape[0]
    w = LRU_WIDTH
    tm = _pick(t, (LRU_TM, 128))
    nb = t // tm
    cu, cg = COL_U // w, COL_G // w
    hb = tm // SUBLANES
    last8 = t // SUBLANES - 1

    def body(dy_ref, ur_ref, g_ref, u_ref, r_ref, i_ref, a_ref, an_ref, h_ref, hp_ref,
             cw_ref, wa_ref, wx_ref, lam_ref,
             dur_ref, dgr_ref, dcw_ref, dcb_ref, dwa_ref, dba_ref, dwx_ref, dbx_ref, dlam_ref,
             lcarry, dnext):
        i = pl.program_id(0)
        ib = nb - 1 - i

        @pl.when(i == 0)
        def _():
            lcarry[...] = jnp.zeros_like(lcarry)
            dnext[...] = jnp.zeros_like(dnext)
            for ref in (dcw_ref, dcb_ref, dwa_ref, dba_ref, dwx_ref, dbx_ref, dlam_ref):
                ref[...] = jnp.zeros_like(ref)

        dy = dy_ref[...]
        hh = h_ref[...]
        av = a_ref[...]
        uu = u_ref[...]
        rr = r_ref[...]
        ii = i_ref[...]
        lam_v = lam_ref[...]
        gl, dgl = _gelu_and_grad(g_ref[...])
        dgr_ref[...] = (dy * hh * dgl).astype(BF16)
        dh_out = dy * gl
        a_next = _shift_up(av, 1, jnp.where(ib == nb - 1, 0.0, an_ref[...]))
        acum, ls = _scan_bwd(a_next, dh_out)
        lam_adj = ls + acum * lcarry[0:1, :]
        lcarry[...] = jnp.broadcast_to(lam_adj[0:1, :], lcarry.shape)
        h_prev = _shift_down(hh, 1, jnp.where(ib == 0, 0.0, hp_ref[...]))
        da = lam_adj * h_prev
        _, a2, s, sp = _lru_gate_terms(rr, lam_v)
        d_igu = lam_adj * s
        ds = lam_adj * ii * uu
        dla = da * a2 - ds * (a2 * a2) / s
        dr = dla * (-LRU_C * sp)
        dlam_ref[...] += jnp.sum(dla * (LRU_C * rr * _sigmoid(-lam_v)), axis=0, keepdims=True)
        dpre_r = dr * rr * (1.0 - rr)
        dpre_i = d_igu * uu * ii * (1.0 - ii)
        prb = dpre_r.astype(BF16)
        pib = dpre_i.astype(BF16)
        ub = uu.astype(BF16)
        du = d_igu * ii + _dot(prb, wa_ref[...], 1, 1) + _dot(pib, wx_ref[...], 1, 1)
        dwa_ref[...] += _dot(ub, prb, 0, 0)
        dwx_ref[...] += _dot(ub, pib, 0, 0)
        dba_ref[...] += jnp.sum(dpre_r, axis=0, keepdims=True)
        dbx_ref[...] += jnp.sum(dpre_i, axis=0, keepdims=True)
        dur, dws = _conv_taps_bwd(du, dnext[...], cw_ref[...], ur_ref[...])
        dur_ref[...] = dur.astype(BF16)
        dcw_ref[...] += dws
        dcb_ref[...] += jnp.sum(du, axis=0, keepdims=True)
        dnext[...] = du[:SUBLANES]

    def rowspec(col):
        return pl.BlockSpec((tm, w), lambda i: (nb - 1 - i, col))

    row = rowspec(0)
    nxt = pl.BlockSpec((SUBLANES, w), lambda i: (jnp.minimum((nb - i) * hb, last8), 0))
    prv = pl.BlockSpec((SUBLANES, w), lambda i: (jnp.maximum((nb - 1 - i) * hb - 1, 0), 0))
    vec = pl.BlockSpec((1, w), lambda i: (0, 0))
    mat = pl.BlockSpec((w, w), lambda i: (0, 0))
    cw = pl.BlockSpec((CONV_K, w), lambda i: (0, 0))
    o = jax.ShapeDtypeStruct((t, w), BF16)
    v1 = jax.ShapeDtypeStruct((1, w), F32)
    m1 = jax.ShapeDtypeStruct((w, w), F32)
    return pl.pallas_call(
        body, name=name, grid=(nb,),
        in_specs=[rowspec(0), rowspec(cu), rowspec(cg), row, row, row, row, nxt, row, prv, cw, mat, mat, vec],
        out_specs=[row, row, cw, vec, mat, vec, mat, vec, vec],
        out_shape=[o, o, jax.ShapeDtypeStruct((CONV_K, w), F32), v1, m1, v1, m1, v1, v1],
        scratch_shapes=[pltpu.VMEM((SUBLANES, w), F32), pltpu.VMEM((SUBLANES, w), F32)],
        compiler_params=_params(1),
    )(dymix, hbuf, hbuf, u, r, ig, a, a, h, h, conv_w, wa, wx, lam)


FOX_T = 512
FOX_PREP_TM = 256


def _log_sigmoid(x):
    return jnp.minimum(x, 0.0) - jnp.log(1.0 + jnp.exp(-jnp.abs(x)))


def _fox_prep(hbuf, bf_vec, *, name):
    t = hbuf.shape[0]
    tm = _pick(t, (FOX_PREP_TM, 128))
    cs = COL_SMALL // LANES

    def body(s_ref, b_ref, eq_ref, ek_ref, carry):
        i = pl.program_id(0)

        @pl.when(i == 0)
        def _():
            carry[...] = jnp.zeros_like(carry)

        lf = _log_sigmoid(s_ref[...] + b_ref[...])
        f = _cumsum_rows(lf) + carry[0:1, :]
        carry[...] = jnp.broadcast_to(f[tm - 1:tm, :], carry.shape)
        lane = _iota((tm, LANES), 1)
        for h in range(ATT_HEADS):
            base = HEAD_DIM * (1 - h % 2)
            fh = _col(f, h)
            hi = fh.astype(BF16).astype(F32)
            mid = (fh - hi).astype(BF16).astype(F32)
            lo = fh - hi - mid
            terms = jnp.where(lane == base, hi, jnp.where(lane == base + 1, mid, jnp.where(lane == base + 2, lo, 0.0)))
            terms_k = jnp.where(lane == base + 3, -hi,
                                jnp.where(lane == base + 4, -mid, jnp.where(lane == base + 5, -lo, 0.0)))
            ones_q = ((lane >= base + 3) & (lane < base + 6)).astype(F32)
            ones_k = ((lane >= base) & (lane < base + 3)).astype(F32)
            eq_ref[:, LANES * h:LANES * (h + 1)] = (terms + ones_q).astype(BF16)
            ek_ref[:, LANES * h:LANES * (h + 1)] = (terms_k + ones_k).astype(BF16)

    ospec = pl.BlockSpec((tm, ATT_HEADS * LANES), lambda i: (i, 0))
    o = jax.ShapeDtypeStruct((t, ATT_HEADS * LANES), BF16)
    return pl.pallas_call(
        body, name=name, grid=(t // tm,),
        in_specs=[pl.BlockSpec((tm, LANES), lambda i: (i, cs)), pl.BlockSpec((1, LANES), lambda i: (0, 0))],
        out_specs=[ospec, ospec], out_shape=[o, o],
        scratch_shapes=[pltpu.VMEM((SUBLANES, LANES), F32)],
        compiler_params=_params(1),
    )(hbuf, bf_vec)


def _fox_post(dfc, hbuf, bf_vec, *, name):
    t = hbuf.shape[0]
    tm = _pick(t, (FOX_PREP_TM, 128))
    nb = t // tm
    cs = COL_SMALL // LANES

    def body(df_ref, s_ref, b_ref, o_ref, db_ref, carry):
        i = pl.program_id(0)

        @pl.when(i == 0)
        def _():
            carry[...] = jnp.zeros_like(carry)
            db_ref[...] = jnp.zeros_like(db_ref)

        dlf = _cumsum_rows(df_ref[...], reverse=True) + carry[0:1, :]
        carry[...] = jnp.broadcast_to(dlf[0:1, :], carry.shape)
        dl = dlf * _sigmoid(-(s_ref[...] + b_ref[...]))
        dl = jnp.where(_iota(dl.shape, 1) < ATT_HEADS, dl, 0.0)
        o_ref[...] = dl
        db_ref[...] += jnp.sum(dl, axis=0, keepdims=True)

    vec = pl.BlockSpec((1, LANES), lambda i: (0, 0))
    return pl.pallas_call(
        body, name=name, grid=(nb,),
        in_specs=[pl.BlockSpec((tm, LANES), lambda i: (nb - 1 - i, 0)),
                  pl.BlockSpec((tm, LANES), lambda i: (nb - 1 - i, cs)), vec],
        out_specs=[pl.BlockSpec((tm, LANES), lambda i: (nb - 1 - i, 0)), vec],
        out_shape=[jax.ShapeDtypeStruct((t, LANES), F32), jax.ShapeDtypeStruct((1, LANES), F32)],
        scratch_shapes=[pltpu.VMEM((SUBLANES, LANES), F32)],
        compiler_params=_params(1),
    )(dfc, hbuf, bf_vec)


def _fox_masks(i, j, tq):
    row = i * tq + _iota((tq, tq), 0)
    col = j * tq + _iota((tq, tq), 1)
    lane = _iota((1, LANES), 1)
    return col <= row, (lane < HEAD_DIM, lane >= HEAD_DIM)


def _fox_fwd(hbuf, eq, ek, *, name):
    t = hbuf.shape[0]
    w = ATT_WIDTH
    tq = _pick(t, (FOX_T, 256, 128))
    nq = t // tq
    cq, ck, cv = COL_Q // w, COL_K // w, COL_V // w

    def body(q_ref, k_ref, v_ref, eq_ref, ek_ref, o_ref, lse_ref, m_s, l_s, acc_s):
        i = pl.program_id(0)
        j = pl.program_id(1)

        @pl.when(j == 0)
        def _():
            m_s[...] = jnp.full_like(m_s, NEG)
            l_s[...] = jnp.zeros_like(l_s)
            acc_s[...] = jnp.zeros_like(acc_s)

        def step(diagonal):
            _, hms = _fox_masks(i, j, tq)
            keys_first = (j * tq + _iota((tq, tq), 0)) <= (i * tq + _iota((tq, tq), 1))
            half = _iota((LANES, 1), 0)
            hrows = (half < HEAD_DIM, half >= HEAD_DIM)
            m_all = m_s[...]
            l_all = l_s[...]
            acc_old = [acc_s[LANES * pr:LANES * (pr + 1), :] for pr in range(2)]
            m_out, l_out, acc_out = [], [], []
            for pr in range(2):
                sl = slice(LANES * pr, LANES * (pr + 1))
                qp = q_ref[:, sl]
                kp = k_ref[:, sl]
                vt = v_ref[:, sl].T.astype(BF16)
                acc = acc_old[pr]
                for hh in range(2):
                    h = 2 * pr + hh
                    hsl = slice(LANES * h, LANES * (h + 1))
                    qm = jnp.where(hms[hh], (qp * (HEAD_DIM ** -0.5)).astype(BF16), eq_ref[:, hsl])
                    km = jnp.where(hms[hh], kp.astype(BF16), ek_ref[:, hsl])
                    st = _dot(km, qm, 1, 1)
                    if diagonal:
                        st = jnp.where(keys_first, st, NEG)
                    m_old = m_all[h:h + 1, :]
                    m_new = jnp.maximum(m_old, jnp.max(st, axis=0, keepdims=True))
                    alpha = jnp.exp(m_old - m_new)
                    pt = jnp.exp(st - m_new)
                    l_out.append(alpha * l_all[h:h + 1, :] + jnp.sum(pt, axis=0, keepdims=True))
                    m_out.append(m_new)
                    pv = _dot(vt, pt.astype(BF16), 1, 0)
                    acc = jnp.where(hrows[hh], alpha * acc_old[pr] + pv, acc)
                acc_out.append(acc)
            for h in range(ATT_HEADS):
                m_s[h:h + 1, :] = m_out[h]
                l_s[h:h + 1, :] = l_out[h]
            for pr in range(2):
                acc_s[LANES * pr:LANES * (pr + 1), :] = acc_out[pr]

        @pl.when(j < i)
        def _():
            step(False)

        @pl.when(j == i)
        def _():
            step(True)
            half = _iota((LANES, 1), 0)
            l_all = l_s[...]
            for pr in range(2):
                acc = acc_s[LANES * pr:LANES * (pr + 1), :]
                o_t = jnp.where(half < HEAD_DIM, acc / l_all[2 * pr:2 * pr + 1, :], acc / l_all[2 * pr + 1:2 * pr + 2, :])
                o_ref[:, LANES * pr:LANES * (pr + 1)] = o_t.T
            lse = m_s[...] + jnp.log(l_s[...])
            lse_ref[...] = jnp.where(_iota(lse.shape, 0) < ATT_HEADS, lse, 0.0)

    return pl.pallas_call(
        body, name=name, grid=(nq, nq),
        in_specs=[pl.BlockSpec((tq, w), lambda i, j: (i, cq)),
                  pl.BlockSpec((tq, w), lambda i, j: (jnp.minimum(j, i), ck)),
                  pl.BlockSpec((tq, w), lambda i, j: (jnp.minimum(j, i), cv)),
                  pl.BlockSpec((tq, ATT_HEADS * LANES), lambda i, j: (i, 0)),
                  pl.BlockSpec((tq, ATT_HEADS * LANES), lambda i, j: (jnp.minimum(j, i), 0))],
        out_specs=[pl.BlockSpec((tq, w), lambda i, j: (i, 0)),
                   pl.BlockSpec((SUBLANES, tq), lambda i, j: (0, i))],
        out_shape=[jax.ShapeDtypeStruct((t, w), F32), jax.ShapeDtypeStruct((SUBLANES, t), F32)],
        scratch_shapes=[pltpu.VMEM((SUBLANES, tq), F32), pltpu.VMEM((SUBLANES, tq), F32),
                        pltpu.VMEM((w, tq), F32)],
        compiler_params=_params(2),
    )(hbuf, hbuf, hbuf, eq, ek)


def _fox_delta(dymix, o, *, name):
    t, w = o.shape
    tm = _pick(t, (512, 256, 128))
    cdo = ATT_WIDTH // w

    def body(do_ref, o_ref, d_ref):
        d_ref[...] = _head_reduce(do_ref[...] * o_ref[...], 0, ATT_HEADS)

    return pl.pallas_call(
        body, name=name, grid=(t // tm,),
        in_specs=[pl.BlockSpec((tm, w), lambda i: (i, cdo)), pl.BlockSpec((tm, w), lambda i: (i, 0))],
        out_specs=pl.BlockSpec((tm, LANES), lambda i: (i, 0)),
        out_shape=jax.ShapeDtypeStruct((t, LANES), F32),
        compiler_params=_params(1),
    )(dymix, o)


def _fox_bwd(hbuf, eq, ek, dymix, lse_rows, delta_rows, *, name):
    t = hbuf.shape[0]
    w = ATT_WIDTH
    tq = _pick(t, (FOX_T, 256, 128))
    nq = t // tq
    cq, ck, cv = COL_Q // w, COL_K // w, COL_V // w
    cdo = ATT_WIDTH // w

    def body(q_ref, k_ref, v_ref, eq_ref, ek_ref, do_ref, lse_ref, dl_ref, dk_ref, dv_ref, dfk_ref, dqt_ref, dfq_ref,
             dk_s, dv_s, dfk_s):
        j = pl.program_id(0)
        i = pl.program_id(1)

        @pl.when((i == 0) & (j == 0))
        def _():
            dqt_ref[...] = jnp.zeros_like(dqt_ref)
            dfq_ref[...] = jnp.zeros_like(dfq_ref)

        @pl.when(i == 0)
        def _():
            dk_s[...] = jnp.zeros_like(dk_s)
            dv_s[...] = jnp.zeros_like(dv_s)
            dfk_s[...] = jnp.zeros_like(dfk_s)

        def step(diagonal):
            _, hms = _fox_masks(i, j, tq)
            keys_first = (j * tq + _iota((tq, tq), 0)) <= (i * tq + _iota((tq, tq), 1))
            half = _iota((LANES, 1), 0)
            hrows = (half < HEAD_DIM, half >= HEAD_DIM)
            lse_all = lse_ref[...]
            dl_all = dl_ref[...]
            dvs, dks, dfks, dqts, dfqs = [], [], [], [], []
            for pr in range(2):
                sl = slice(LANES * pr, LANES * (pr + 1))
                qp = q_ref[:, sl]
                kp = k_ref[:, sl]
                kt = kp.T.astype(BF16)
                vpb = v_ref[:, sl].astype(BF16)
                dop = do_ref[:, sl]
                dv_p = jnp.zeros((tq, LANES), F32)
                dk_p = jnp.zeros((tq, LANES), F32)
                dqt_p = jnp.zeros((LANES, tq), F32)
                for hh in range(2):
                    h = 2 * pr + hh
                    hsl = slice(LANES * h, LANES * (h + 1))
                    qm = jnp.where(hms[hh], (qp * (HEAD_DIM ** -0.5)).astype(BF16), eq_ref[:, hsl])
                    km = jnp.where(hms[hh], kp.astype(BF16), ek_ref[:, hsl])
                    st = _dot(km, qm, 1, 1)
                    if diagonal:
                        st = jnp.where(keys_first, st, NEG)
                    pt = jnp.exp(st - lse_all[h:h + 1, :])
                    domb = jnp.where(hms[hh], dop, 0.0).astype(BF16)
                    dv_p = dv_p + _dot(pt.astype(BF16), domb, 1, 0)
                    dpt = _dot(vpb, domb, 1, 1)
                    dst = pt * (dpt - dl_all[h:h + 1, :])
                    dstb = dst.astype(BF16)
                    dk_p = dk_p + jnp.where(hms[hh], _dot(dstb, qm, 1, 0), 0.0)
                    dqt_p = dqt_p + _dot(jnp.where(hrows[hh], kt, 0.0), dstb, 1, 0)
                    part = dst[:, 0:LANES]
                    for c in range(1, tq // LANES):
                        part = part + dst[:, LANES * c:LANES * (c + 1)]
                    dfks.append(part)
                    dfqs.append(jnp.sum(dst, axis=0, keepdims=True))
                dvs.append(dv_p)
                dks.append(dk_p)
                dqts.append(dqt_p)
            dv_s[...] += jnp.concatenate(dvs, axis=1)
            dk_s[...] += jnp.concatenate(dks, axis=1)
            for h in range(ATT_HEADS):
                dfk_s[h] += dfks[h]
            cols = pl.ds(pl.multiple_of(i * tq, tq), tq)
            dqt_ref[:, cols] += jnp.concatenate(dqts, axis=0) * (HEAD_DIM ** -0.5)
            dfq_ref[:, cols] += jnp.concatenate(dfqs + [jnp.zeros((SUBLANES - ATT_HEADS, tq), F32)], axis=0)

        @pl.when(i > j)
        def _():
            step(False)

        @pl.when(i == j)
        def _():
            step(True)

        @pl.when(i == nq - 1)
        def _():
            dk_ref[...] = dk_s[...].astype(BF16)
            dv_ref[...] = dv_s[...].astype(BF16)
            lane = _iota((tq, LANES), 1)
            out = jnp.zeros((tq, LANES), F32)
            for h in range(ATT_HEADS):
                out = jnp.where(lane == h, jnp.sum(dfk_s[h], axis=1, keepdims=True), out)
            dfk_ref[...] = out

    qi = lambda j, i: jnp.maximum(i, j)
    rows = pl.BlockSpec((SUBLANES, tq), lambda j, i: (0, qi(j, i)))
    return pl.pallas_call(
        body, name=name, grid=(nq, nq),
        in_specs=[pl.BlockSpec((tq, w), lambda j, i: (qi(j, i), cq)),
                  pl.BlockSpec((tq, w), lambda j, i: (j, ck)),
                  pl.BlockSpec((tq, w), lambda j, i: (j, cv)),
                  pl.BlockSpec((tq, ATT_HEADS * LANES), lambda j, i: (qi(j, i), 0)),
                  pl.BlockSpec((tq, ATT_HEADS * LANES), lambda j, i: (j, 0)),
                  pl.BlockSpec((tq, w), lambda j, i: (qi(j, i), cdo)),
                  rows, rows],
        out_specs=[pl.BlockSpec((tq, w), lambda j, i: (j, 0)), pl.BlockSpec((tq, w), lambda j, i: (j, 0)),
                   pl.BlockSpec((tq, LANES), lambda j, i: (j, 0)),
                   pl.BlockSpec((w, t), lambda j, i: (0, 0)), pl.BlockSpec((SUBLANES, t), lambda j, i: (0, 0))],
        out_shape=[jax.ShapeDtypeStruct((t, w), BF16), jax.ShapeDtypeStruct((t, w), BF16),
                   jax.ShapeDtypeStruct((t, LANES), F32),
                   jax.ShapeDtypeStruct((w, t), F32), jax.ShapeDtypeStruct((SUBLANES, t), F32)],
        scratch_shapes=[pltpu.VMEM((tq, w), F32), pltpu.VMEM((tq, w), F32),
                        pltpu.VMEM((ATT_HEADS, tq, LANES), F32)],
        compiler_params=_params(2),
    )(hbuf, hbuf, hbuf, eq, ek, dymix, lse_rows, delta_rows)


GROUP_W = SSD_WIDTH // SSD_GROUPS
HEADS_PER_GROUP = SSD_HEADS // SSD_GROUPS


def _ssd_chunk_common(xr, prev8, sm, cw, cb, dtb, avec):
    c = _conv_taps(xr, prev8, cw, cb)
    sig = _sigmoid(c)
    xa = c * sig
    dt = _softplus(sm + dtb)
    a = dt * avec
    acum = _cumsum_rows(a)
    return c, sig, xa, dt, acum


def _ssd_decays(acum, g):
    n = acum.shape[0]
    atot = acum[n - 1:n, :]
    lane0 = LANE_DT + HEADS_PER_GROUP * g
    e = _head_expand(jnp.exp(acum), lane0, HEADS_PER_GROUP, GROUP_W)
    dec = _head_expand(jnp.exp(atot - acum), lane0, HEADS_PER_GROUP, GROUP_W)
    etot = _head_expand(jnp.exp(atot), lane0, HEADS_PER_GROUP, GROUP_W)
    return e, dec, etot


def _ssd_ldec(acum, acum_t, lane, tril):
    return jnp.exp(jnp.where(tril, _col(acum, lane) - _row(acum_t, lane), NEG))


def _ssd_fwd(hbuf, conv_w, conv_b, dtb_vec, a_vec, d_exp, norm_g, *, name):
    t = hbuf.shape[0]
    L = SSD_CHUNK
    nc = t // L
    hb = L // SUBLANES
    cs = COL_SMALL // LANES
    cz = COL_Z // SSD_WIDTH

    def body(x_ref, xp_ref, z_ref, s_ref, cw_ref, cb_ref, dtb_ref, av_ref, dx_ref, ng_ref,
             yc_ref, y_ref, st_ref, state):
        i = pl.program_id(0)

        @pl.when(i == 0)
        def _():
            state[...] = jnp.zeros_like(state)

        prev = jnp.where(i == 0, 0.0, xp_ref[...])
        _, _, xa, dt, acum = _ssd_chunk_common(x_ref[...], prev, s_ref[...], cw_ref[...], cb_ref[...],
                                               dtb_ref[...], av_ref[...])
        acum_t = acum.T
        xs = xa[:, :SSD_WIDTH]
        xdt = xs * _head_expand(dt, LANE_DT, SSD_HEADS, SSD_WIDTH)
        tril = _iota((L, L), 0) >= _iota((L, L), 1)
        lane = _iota((1, LANES), 1)
        ys = []
        for g in range(SSD_GROUPS):
            bg = xa[:, SSD_WIDTH + SSD_STATE * g:SSD_WIDTH + SSD_STATE * (g + 1)].astype(BF16)
            cg = xa[:, SSD_WIDTH + SSD_STATE * (SSD_GROUPS + g):SSD_WIDTH + SSD_STATE * (SSD_GROUPS + g + 1)].astype(BF16)
            gm = _dot(cg, bg, 1, 1)
            e, dec, etot = _ssd_decays(acum, g)
            s_in = state[g]
            st_ref[0, g] = s_in
            xg = xdt[:, GROUP_W * g:GROUP_W * (g + 1)]
            y_off = e * _dot(cg, s_in.astype(BF16), 1, 0)
            state[g] = etot * s_in + _dot(bg, (dec * xg).astype(BF16), 0, 0)
            for pr in range(2):
                xp = xg[:, LANES * pr:LANES * (pr + 1)].astype(BF16)
                outs = []
                for hh in range(2):
                    h = HEADS_PER_GROUP * g + 2 * pr + hh
                    m = gm * _ssd_ldec(acum, acum_t, LANE_DT + h, tril)
                    outs.append(_dot(m.astype(BF16), xp, 1, 0))
                ys.append(jnp.where(lane < HEAD_DIM, outs[0], outs[1]) + y_off[:, LANES * pr:LANES * (pr + 1)])
        y = jnp.concatenate(ys, axis=1)
        y_ref[...] = y
        yd = y + dx_ref[...] * xs
        zz = z_ref[...]
        y2 = yd * zz * _sigmoid(zz)
        ng = ng_ref[...]
        outs = []
        for g in range(SSD_GROUPS):
            yg = y2[:, GROUP_W * g:GROUP_W * (g + 1)]
            rs = lax.rsqrt(jnp.mean(yg * yg, axis=1, keepdims=True) + RMS_EPS)
            outs.append(yg * rs * ng[:, GROUP_W * g:GROUP_W * (g + 1)])
        yc_ref[...] = jnp.concatenate(outs, axis=1)

    cdim = SSD_CONV_DIM
    vecc = pl.BlockSpec((1, cdim), lambda i: (0, 0))
    vecl = pl.BlockSpec((1, LANES), lambda i: (0, 0))
    vecw = pl.BlockSpec((1, SSD_WIDTH), lambda i: (0, 0))
    roww = pl.BlockSpec((L, SSD_WIDTH), lambda i: (i, 0))
    return pl.pallas_call(
        body, name=name, grid=(nc,),
        in_specs=[pl.BlockSpec((L, cdim), lambda i: (i, 0)),
                  pl.BlockSpec((SUBLANES, cdim), lambda i: (jnp.maximum(i * hb - 1, 0), 0)),
                  pl.BlockSpec((L, SSD_WIDTH), lambda i: (i, cz)),
                  pl.BlockSpec((L, LANES), lambda i: (i, cs)),
                  pl.BlockSpec((CONV_K, cdim), lambda i: (0, 0)), vecc, vecl, vecl, vecw, vecw],
        out_specs=[roww, roww, pl.BlockSpec((1, SSD_GROUPS, SSD_STATE, GROUP_W), lambda i: (i, 0, 0, 0))],
        out_shape=[jax.ShapeDtypeStruct((t, SSD_WIDTH), F32), jax.ShapeDtypeStruct((t, SSD_WIDTH), F32),
                   jax.ShapeDtypeStruct((nc, SSD_GROUPS, SSD_STATE, GROUP_W), F32)],
        scratch_shapes=[pltpu.VMEM((SSD_GROUPS, SSD_STATE, GROUP_W), F32)],
        compiler_params=_params(1),
    )(hbuf, hbuf, hbuf, hbuf, conv_w, conv_b, dtb_vec, a_vec, d_exp, norm_g)


def _ssd_bwd(dymix, hbuf, y_ssd, states, conv_w, conv_b, dtb_vec, a_vec, d_exp, norm_g, *, name):
    t = hbuf.shape[0]
    L = SSD_CHUNK
    nc = t // L
    hb = L // SUBLANES
    cs = COL_SMALL // LANES
    cz = COL_Z // SSD_WIDTH
    cdy = (LRU_WIDTH + ATT_WIDTH) // SSD_WIDTH
    cdim = SSD_CONV_DIM

    def body(dyc_ref, x_ref, xp_ref, z_ref, s_ref, y_ref, st_ref, cw_ref, cb_ref, dtb_ref, av_ref, dx_ref, ng_ref,
             dxr_ref, dz_ref, dsm_ref, dng_ref, dd_ref, da_ref, ddtb_ref, dcw_ref, dcb_ref,
             dstate, dnext):
        i = pl.program_id(0)
        ic = nc - 1 - i

        @pl.when(i == 0)
        def _():
            dstate[...] = jnp.zeros_like(dstate)
            dnext[...] = jnp.zeros_like(dnext)
            for ref in (dng_ref, dd_ref, da_ref, ddtb_ref, dcw_ref, dcb_ref):
                ref[...] = jnp.zeros_like(ref)

        xr = x_ref[...]
        sm = s_ref[...]
        prev = jnp.where(ic == 0, 0.0, xp_ref[...])
        avec = av_ref[...]
        c, sig, xa, dt, acum = _ssd_chunk_common(xr, prev, sm, cw_ref[...], cb_ref[...], dtb_ref[...], avec)
        acum_t = acum.T
        xs = xa[:, :SSD_WIDTH]
        dtx = _head_expand(dt, LANE_DT, SSD_HEADS, SSD_WIDTH)
        xdt = xs * dtx
        tril = _iota((L, L), 0) >= _iota((L, L), 1)
        lane = _iota((1, LANES), 1)
        hmasks = (lane < HEAD_DIM, lane >= HEAD_DIM)

        y = y_ref[...]
        dexp = dx_ref[...]
        yd = y + dexp * xs
        zz = z_ref[...]
        sz = _sigmoid(zz)
        siluz = zz * sz
        y2 = yd * siluz
        ng = ng_ref[...]
        dyc = dyc_ref[...]
        dy2s, dngs = [], []
        for g in range(SSD_GROUPS):
            sl = slice(GROUP_W * g, GROUP_W * (g + 1))
            yg = y2[:, sl]
            rs = lax.rsqrt(jnp.mean(yg * yg, axis=1, keepdims=True) + RMS_EPS)
            wv = dyc[:, sl] * ng[:, sl]
            dngs.append(jnp.sum(dyc[:, sl] * yg * rs, axis=0, keepdims=True))
            dy2s.append(rs * wv - yg * (rs * rs * rs) * jnp.mean(wv * yg, axis=1, keepdims=True))
        dy2 = jnp.concatenate(dy2s, axis=1)
        dng_ref[...] += jnp.concatenate(dngs, axis=1)
        dz_ref[...] = (dy2 * yd * (sz * (1.0 + zz * (1.0 - sz)))).astype(BF16)
        dy = dy2 * siluz
        dd_ref[...] += jnp.sum(dy * xs, axis=0, keepdims=True)

        dxs, dbs, dcs = [], [], []
        datot = jnp.zeros((1, LANES), F32)
        lanes = _iota((L, LANES), 1)
        dacum = jnp.zeros((L, LANES), F32)
        for g in range(SSD_GROUPS):
            sl = slice(GROUP_W * g, GROUP_W * (g + 1))
            bg = xa[:, SSD_WIDTH + SSD_STATE * g:SSD_WIDTH + SSD_STATE * (g + 1)].astype(BF16)
            cg = xa[:, SSD_WIDTH + SSD_STATE * (SSD_GROUPS + g):SSD_WIDTH + SSD_STATE * (SSD_GROUPS + g + 1)].astype(BF16)
            gm = _dot(cg, bg, 1, 1)
            e, dec, etot = _ssd_decays(acum, g)
            s_in = st_ref[0, g]
            ds_out = dstate[g]
            dyg = dy[:, sl]
            xg = xdt[:, sl]
            edy = (e * dyg).astype(BF16)
            dstate[g] = etot * ds_out + _dot(cg, edy, 0, 0)
            dx_state = dec * _dot(bg, ds_out.astype(BF16), 1, 0)
            y_off = e * _dot(cg, s_in.astype(BF16), 1, 0)
            dacum = dacum + _head_reduce_group(dyg * y_off - xg * dx_state, g)
            dc_off = _dot(edy, s_in.astype(BF16), 1, 1)
            db_state = _dot((dec * xg).astype(BF16), ds_out.astype(BF16), 1, 1)
            dgsum = jnp.zeros((L, L), F32)
            dx_pairs = []
            for pr in range(2):
                psl = slice(LANES * pr, LANES * (pr + 1))
                xp = xg[:, psl]
                dyp = dyg[:, psl]
                dx_pair = jnp.zeros((L, LANES), F32)
                for hh in range(2):
                    h = HEADS_PER_GROUP * g + 2 * pr + hh
                    ldec = _ssd_ldec(acum, acum_t, LANE_DT + h, tril)
                    dym = jnp.where(hmasks[hh], dyp, 0.0).astype(BF16)
                    xm = jnp.where(hmasks[hh], xp, 0.0).astype(BF16)
                    dx_pair = dx_pair + _dot((gm * ldec).astype(BF16), dym, 0, 0)
                    dml = _dot(dym, xm, 1, 1) * ldec
                    dgsum = dgsum + dml
                    qm = dml * gm
                    seg = jnp.sum(qm, axis=1, keepdims=True) - jnp.sum(qm.T, axis=1, keepdims=True)
                    dacum = dacum + jnp.where(lanes == LANE_DT + h, seg, 0.0)
                dx_pairs.append(dx_pair)
            dgb = dgsum.astype(BF16)
            dcs.append(_dot(dgb, bg, 1, 0) + dc_off)
            dbs.append(_dot(dgb, cg, 0, 0) + db_state)
            dxg = jnp.concatenate(dx_pairs, axis=1) + dx_state
            dxs.append(dxg)
            v = jnp.sum(dx_state * xg, axis=0, keepdims=True) + etot * jnp.sum(ds_out * s_in, axis=0, keepdims=True)
            datot = datot + _head_reduce_row(v, LANE_DT + HEADS_PER_GROUP * g, HEADS_PER_GROUP)
        dx = jnp.concatenate(dxs, axis=1)
        dacum = dacum + jnp.where(_iota((L, LANES), 0) == L - 1, datot, 0.0)
        da = _cumsum_rows(dacum, reverse=True)
        ddt = da * avec + _head_reduce(dx * xs, LANE_DT, SSD_HEADS)
        da_ref[...] += jnp.sum(da * dt, axis=0, keepdims=True)
        ddt_raw = ddt * _sigmoid(sm + dtb_ref[...])
        ddt_raw = jnp.where((lanes >= LANE_DT) & (lanes < LANE_DT + SSD_HEADS), ddt_raw, 0.0)
        dsm_ref[...] = ddt_raw
        ddtb_ref[...] += jnp.sum(ddt_raw, axis=0, keepdims=True)
        dxs_total = dx * dtx + dexp * dy
        dxa = jnp.concatenate([dxs_total] + dbs + dcs, axis=1)
        dc = dxa * (sig * (1.0 + c * (1.0 - sig)))
        dxr, dws = _conv_taps_bwd(dc, dnext[...], cw_ref[...], xr)
        dxr_ref[...] = dxr.astype(BF16)
        dcw_ref[...] += dws
        dcb_ref[...] += jnp.sum(dc, axis=0, keepdims=True)
        dnext[...] = dc[:SUBLANES]

    rev = lambda i: nc - 1 - i
    vecc = pl.BlockSpec((1, cdim), lambda i: (0, 0))
    vecl = pl.BlockSpec((1, LANES), lambda i: (0, 0))
    vecw = pl.BlockSpec((1, SSD_WIDTH), lambda i: (0, 0))
    cwspec = pl.BlockSpec((CONV_K, cdim), lambda i: (0, 0))
    roww = pl.BlockSpec((L, SSD_WIDTH), lambda i: (rev(i), 0))
    return pl.pallas_call(
        body, name=name, grid=(nc,),
        in_specs=[pl.BlockSpec((L, SSD_WIDTH), lambda i: (rev(i), cdy)),
                  pl.BlockSpec((L, cdim), lambda i: (rev(i), 0)),
                  pl.BlockSpec((SUBLANES, cdim), lambda i: (jnp.maximum(rev(i) * hb - 1, 0), 0)),
                  pl.BlockSpec((L, SSD_WIDTH), lambda i: (rev(i), cz)),
                  pl.BlockSpec((L, LANES), lambda i: (rev(i), cs)),
                  roww,
                  pl.BlockSpec((1, SSD_GROUPS, SSD_STATE, GROUP_W), lambda i: (rev(i), 0, 0, 0)),
                  cwspec, vecc, vecl, vecl, vecw, vecw],
        out_specs=[pl.BlockSpec((L, cdim), lambda i: (rev(i), 0)), roww,
                   pl.BlockSpec((L, LANES), lambda i: (rev(i), 0)),
                   vecw, vecw, vecl, vecl, cwspec, vecc],
        out_shape=[jax.ShapeDtypeStruct((t, cdim), BF16), jax.ShapeDtypeStruct((t, SSD_WIDTH), BF16),
                   jax.ShapeDtypeStruct((t, LANES), F32),
                   jax.ShapeDtypeStruct((1, SSD_WIDTH), F32), jax.ShapeDtypeStruct((1, SSD_WIDTH), F32),
                   jax.ShapeDtypeStruct((1, LANES), F32), jax.ShapeDtypeStruct((1, LANES), F32),
                   jax.ShapeDtypeStruct((CONV_K, cdim), F32), jax.ShapeDtypeStruct((1, cdim), F32)],
        scratch_shapes=[pltpu.VMEM((SSD_GROUPS, SSD_STATE, GROUP_W), F32), pltpu.VMEM((SUBLANES, cdim), F32)],
        compiler_params=_params(1),
    )(dymix, hbuf, hbuf, hbuf, hbuf, y_ssd, states, conv_w, conv_b, dtb_vec, a_vec, d_exp, norm_g)


def _head_reduce_group(x, g):
    return _head_reduce(x, LANE_DT + HEADS_PER_GROUP * g, HEADS_PER_GROUP)


def _head_reduce_row(v, lane0, nheads):
    colhead = _iota(v.shape, 1) // HEAD_DIM
    lane = _iota((1, LANES), 1)
    out = jnp.zeros((1, LANES), F32)
    for h in range(nheads):
        s = jnp.sum(jnp.where(colhead == h, v, 0.0), axis=1, keepdims=True)
        out = jnp.where(lane == lane0 + h, s, out)
    return out


def _exchange(inps, axes, *, mode, local=True, name):
    n = 2 ** len(axes)
    counts, out_shapes = [], []
    for a in inps:
        if mode == "gather":
            cnt, rest = a.shape[0], a.shape[1:]
        elif mode == "gather_half":
            cnt, rest = a.shape[0] // 2, a.shape[1:]
        elif mode == "a2a":
            cnt, rest = a.shape[1], a.shape[2:]
        else:
            cnt, rest = a.shape[0], a.shape[2:]
        counts.append(cnt)
        out_shapes.append(jax.ShapeDtypeStruct((n, cnt) + tuple(rest), a.dtype))
    units = sum(counts)
    na = len(inps)

    def body(*refs):
        in_refs, out_refs = refs[:na], refs[na:2 * na]
        send_sems, recv_sems, local_sems = refs[2 * na:]
        pos = {ax: lax.axis_index(ax) for ax in MESH_AXES}

        def slot_of(coord):
            s = 0
            for ax in axes:
                s = s * 2 + coord[ax]
            return s

        def src(a, it, slot):
            if mode == "gather":
                return in_refs[a].at[it]
            if mode == "gather_half":
                return in_refs[a].at[pos["c"] * counts[a] + it]
            if mode == "a2a":
                return in_refs[a].at[slot, it]
            return in_refs[a].at[it, slot]

        me = slot_of(pos)
        copies = []
        unit = 0
        for a in range(na):
            for it in range(counts[a]):
                if local:
                    cp = pltpu.make_async_copy(src(a, it, me), out_refs[a].at[me, it], local_sems.at[unit])
                    cp.start()
                    copies.append(cp)
                for delta in range(1, n):
                    coord = dict(pos)
                    for b, ax in enumerate(reversed(axes)):
                        if (delta >> b) & 1:
                            coord[ax] = 1 - pos[ax]
                    k = unit * (n - 1) + delta - 1
                    cp = pltpu.make_async_remote_copy(
                        src_ref=src(a, it, slot_of(coord)), dst_ref=out_refs[a].at[me, it],
                        send_sem=send_sems.at[k], recv_sem=recv_sems.at[k],
                        device_id=(coord["x"], coord["y"], coord["c"]), device_id_type=pl.DeviceIdType.MESH)
                    cp.start()
                    copies.append(cp)
                unit += 1
        for cp in copies:
            cp.wait()

    any_spec = pl.BlockSpec(memory_space=pl.ANY)
    return pl.pallas_call(
        body, name=name,
        in_specs=[any_spec] * na, out_specs=[any_spec] * na, out_shape=out_shapes,
        scratch_shapes=[pltpu.SemaphoreType.DMA((units * (n - 1),)), pltpu.SemaphoreType.DMA((units * (n - 1),)),
                        pltpu.SemaphoreType.DMA((units,))],
    )(*inps)


def _sum_slots(buf, out_dtype, *, name):
    n, rows, cols = buf.shape
    tm = _pick(rows, (512, 256, 128, 8))
    if rows % tm:
        tm = rows

    def body(b_ref, o_ref):
        acc = b_ref[0].astype(F32)
        for s in range(1, n):
            acc = acc + b_ref[s].astype(F32)
        o_ref[...] = acc.astype(out_dtype)

    return pl.pallas_call(
        body, name=name, grid=(pl.cdiv(rows, tm),),
        in_specs=[pl.BlockSpec((n, tm, cols), lambda i: (0, i, 0))],
        out_specs=pl.BlockSpec((tm, cols), lambda i: (i, 0)),
        out_shape=jax.ShapeDtypeStruct((rows, cols), out_dtype),
        compiler_params=_params(1),
    )(buf)


def _adamw(w, g, m, v, *, name):
    shape = w.shape
    cols = shape[-1]
    rows = w.size // cols
    w2, g2, m2, v2 = (a.reshape(rows, cols) for a in (w, g, m, v))
    tm = _pick(rows, (256, 128, 64, 32, 16, 8))
    if rows % tm:
        tm = rows
    bc1 = 1.0 - ADAM_B1 ** ADAM_STEP
    bc2 = 1.0 - ADAM_B2 ** ADAM_STEP

    def body(w_ref, g_ref, m_ref, v_ref, d_ref, nm_ref, nv_ref):
        gg = g_ref[...]
        mm = ADAM_B1 * m_ref[...] + (1.0 - ADAM_B1) * gg
        vv = ADAM_B2 * v_ref[...] + (1.0 - ADAM_B2) * (gg * gg)
        m_hat = mm / bc1
        v_hat = vv / bc2
        d_ref[...] = -ADAM_LR * (m_hat / (jnp.sqrt(v_hat) + ADAM_EPS) + ADAM_WD * w_ref[...])
        nm_ref[...] = mm
        nv_ref[...] = vv

    spec = pl.BlockSpec((tm, cols), lambda i: (i, 0))
    o = jax.ShapeDtypeStruct((rows, cols), F32)
    outs = pl.pallas_call(
        body, name=name, grid=(rows // tm,), in_specs=[spec] * 4, out_specs=[spec] * 3, out_shape=[o] * 3,
        compiler_params=_params(1),
    )(w2, g2, m2, v2)
    return tuple(a.reshape(shape) for a in outs)


def _layer_fwd(li, x, xb, pb, W):
    nm = lambda s: f"l{li}_{s}"
    sv = {"x_in_b": xb}
    g1, u1, a1 = _mm_swiglu(xb, W["ffn1_wg"], W["ffn1_wu"], name=nm("ffn1_up"))
    x1, x1b, xh1, rs1 = _mm_ln(a1, W["ffn1_wd"], x, W["ln1_g"], W["ln1_b"], rscale=ALPHA, mscale=0.5, name=nm("ffn1_down_ln"))
    hbuf = _mm(x1b, W["w_in_p"], name=nm("in_proj"))
    ya, lu, lr, lig, la, lh = _lru_fwd(hbuf, W["lru_conv_w"], W["lru_conv_b"], W["lru_wa_bd"], W["lru_ba"],
                                       W["lru_wx_bd"], W["lru_bx"], W["lru_lambda"], name=nm("lru_fwd"))
    eq, ek = _fox_prep(hbuf, W["fox_bf_vec"], name=nm("fox_prep"))
    yb, lse_rows = _fox_fwd(hbuf, eq, ek, name=nm("fox_fwd"))
    yc, yssd, states = _ssd_fwd(hbuf, W["ssd_conv_w"], W["ssd_conv_b"], W["ssd_dtb_vec"], W["ssd_a_vec"],
                                W["ssd_d_exp"], W["ssd_norm_g"], name=nm("ssd_fwd"))
    ymix = jnp.concatenate([ya, yb, yc], axis=1).astype(BF16)
    x2, x2b, xh2, rs2 = _mm_ln(ymix, W["w_out"], x1, W["ln2_g"], W["ln2_b"], rscale=ALPHA, mscale=1.0, name=nm("out_proj_ln"))
    g2, u2, a2 = _mm_swiglu(x2b, W["ffn2_wg"], W["ffn2_wu"], name=nm("ffn2_up"))
    x3, x3b, xh3, rs3 = _mm_ln(a2, W["ffn2_wd"], x2, W["ln3_g"], W["ln3_b"], rscale=ALPHA, mscale=0.5, name=nm("ffn2_down_ln"))
    x4, x4b, sg, e = _mm_pe(x3, x3b, pb, W["pe_gate_w"], W["pe_gate_b"], W["pe_proj"], name=nm("ple"))
    sv.update(g1=g1, u1=u1, a1=a1, x1b=x1b, xh1=xh1, rs1=rs1, hbuf=hbuf, lu=lu, lr=lr, lig=lig, la=la, lh=lh,
              eq=eq, ek=ek, lse_rows=lse_rows, yb=yb, yssd=yssd, states=states, ymix=ymix, x2b=x2b, xh2=xh2, rs2=rs2,
              g2=g2, u2=u2, a2=a2, x3b=x3b, xh3=xh3, rs3=rs3, sg=sg, e=e, pb=pb)
    return x4, x4b, sv


def _layer_bwd(li, dx4, sv, W):
    nm = lambda s: f"l{li}_{s}"
    G = {}
    dgp, de, dbg = _pe_bwd_elem(dx4, sv["sg"], sv["e"], name=nm("ple_bwd"))
    G["pe_gate_b"] = dbg
    G["pe_gate_w"] = _mm(sv["x3b"], dgp, ta=True, out_dtype=BF16, name=nm("d_pe_gate_w"))
    G["pe_proj"] = _mm(sv["pb"], de, ta=True, out_dtype=BF16, chip_cols=True, name=nm("d_pe_proj"))
    dr3, dr3b, G["ln3_g"], G["ln3_b"] = _bwd_proj([(dgp, W["pe_gate_w"])], dx4, rscale=1.0,
                                                  ln=(sv["xh3"], sv["rs3"], W["ln3_g"]), name=nm("ln3_bwd"))
    G["ffn2_wd"] = _mm(sv["a2"], dr3b, ta=True, scale=0.5, out_dtype=BF16, name=nm("d_ffn2_wd"))
    dg2, du2 = _mm_swiglu_bwd(dr3b, W["ffn2_wd"], sv["g2"], sv["u2"], scale=0.5, name=nm("ffn2_act_bwd"))
    G["ffn2_wg"] = _mm(sv["x2b"], dg2, ta=True, out_dtype=BF16, chip_cols=True, name=nm("d_ffn2_wg"))
    G["ffn2_wu"] = _mm(sv["x2b"], du2, ta=True, out_dtype=BF16, chip_cols=True, name=nm("d_ffn2_wu"))
    dr2, dr2b, G["ln2_g"], G["ln2_b"] = _bwd_proj([(dg2, W["ffn2_wg"]), (du2, W["ffn2_wu"])], dr3, rscale=ALPHA,
                                                  ln=(sv["xh2"], sv["rs2"], W["ln2_g"]), name=nm("ln2_bwd"))
    G["w_out"] = _mm(sv["ymix"], dr2b, ta=True, out_dtype=BF16, name=nm("d_w_out"))
    dymix = _mm(dr2b, W["w_out"], tb=True, name=nm("d_ymix"))
    hbuf = sv["hbuf"]
    (dur, dgr, G["lru_conv_w"], G["lru_conv_b"], G["lru_wa_bd"], G["lru_ba"], G["lru_wx_bd"], G["lru_bx"],
     G["lru_lambda"]) = _lru_bwd(dymix, hbuf, sv["lu"], sv["lr"], sv["lig"], sv["la"], sv["lh"],
                                 W["lru_conv_w"], W["lru_wa_bd"], W["lru_wx_bd"], W["lru_lambda"], name=nm("lru_bwd"))
    delta = _fox_delta(dymix, sv["yb"], name=nm("fox_delta"))
    delta_rows = jnp.pad(delta[:, :ATT_HEADS].T, ((0, SUBLANES - ATT_HEADS), (0, 0)))
    dk, dv, dfk, dqt, dfq = _fox_bwd(hbuf, sv["eq"], sv["ek"], dymix, sv["lse_rows"], delta_rows, name=nm("fox_bwd"))
    dq = dqt.T
    dfc = jnp.pad(dfq[:ATT_HEADS].T, ((0, 0), (0, LANES - ATT_HEADS))) - dfk
    dsm_f, G["fox_bf_vec"] = _fox_post(dfc, hbuf, W["fox_bf_vec"], name=nm("fox_post"))
    (dxr, dz, dsm_dt, G["ssd_norm_g"], G["ssd_d_exp"], G["ssd_a_vec"], G["ssd_dtb_vec"], G["ssd_conv_w"],
     G["ssd_conv_b"]) = _ssd_bwd(dymix, hbuf, sv["yssd"], sv["states"], W["ssd_conv_w"], W["ssd_conv_b"],
                                 W["ssd_dtb_vec"], W["ssd_a_vec"], W["ssd_d_exp"], W["ssd_norm_g"], name=nm("ssd_bwd"))
    t = dx4.shape[0]
    dh = jnp.concatenate([dxr.astype(BF16), dz.astype(BF16), dur.astype(BF16), dgr.astype(BF16), dq.astype(BF16),
                          dk.astype(BF16), dv.astype(BF16), (dsm_f + dsm_dt).astype(BF16),
                          jnp.zeros((t, H_WIDTH - COL_SMALL - LANES), BF16)], axis=1)
    G["w_in_p"] = _mm(sv["x1b"], dh, ta=True, name=nm("d_w_in"))
    dr1, dr1b, G["ln1_g"], G["ln1_b"] = _bwd_proj([(dh, W["w_in_p"])], dr2, rscale=ALPHA,
                                                  ln=(sv["xh1"], sv["rs1"], W["ln1_g"]), name=nm("ln1_bwd"))
    G["ffn1_wd"] = _mm(sv["a1"], dr1b, ta=True, scale=0.5, out_dtype=BF16, name=nm("d_ffn1_wd"))
    dg1, du1 = _mm_swiglu_bwd(dr1b, W["ffn1_wd"], sv["g1"], sv["u1"], scale=0.5, name=nm("ffn1_act_bwd"))
    G["ffn1_wg"] = _mm(sv["x_in_b"], dg1, ta=True, out_dtype=BF16, chip_cols=True, name=nm("d_ffn1_wg"))
    G["ffn1_wu"] = _mm(sv["x_in_b"], du1, ta=True, out_dtype=BF16, chip_cols=True, name=nm("d_ffn1_wu"))
    (dx_in,) = _bwd_proj([(dg1, W["ffn1_wg"]), (du1, W["ffn1_wu"])], dr1, rscale=ALPHA, ln=None, name=nm("x_in_bwd"))
    return dx_in, G


def _block_diag(w):
    n, b, _ = w.shape
    eye = jnp.eye(n, dtype=w.dtype)
    return (eye[:, None, :, None] * w[:, :, None, :]).reshape(n * b, n * b)


def _block_diag_extract(m):
    n, b = LRU_HEADS, HEAD_DIM
    return jnp.stack([m[b * i:b * (i + 1), b * i:b * (i + 1)] for i in range(n)])


def _lane_vec(v, lane0):
    return jnp.pad(v.astype(F32), (lane0, LANES - lane0 - v.shape[0])).reshape(1, LANES)


def _w_in_permute(w):
    d = w.shape[0]
    z = lambda n: jnp.zeros((d, n), w.dtype)
    return jnp.concatenate([w[:, 1796:2820], w[:, 1284:1796], w[:, 0:512], w[:, 512:1280],
                            w[:, 1280:1284], w[:, 2820:2828], z(LANES - 12), z(H_WIDTH - COL_SMALL - LANES)], axis=1)


def _w_in_unpermute(wp):
    return jnp.concatenate([wp[:, COL_U:COL_Q], wp[:, COL_Q:COL_SMALL], wp[:, COL_SMALL:COL_SMALL + 4],
                            wp[:, COL_Z:COL_U], wp[:, COL_XBC:COL_Z], wp[:, COL_SMALL + 4:COL_SMALL + 12]], axis=1)


def _layer_weights(li, chipw, small):
    g = lambda n: small[n][li]
    W = {n: g(n) for n in ("ln1_g", "ln1_b", "ln2_g", "ln2_b", "ln3_g", "ln3_b", "pe_gate_b", "lru_conv_w",
                           "ssd_conv_w")}
    for n in ("ffn1_wg", "ffn1_wu", "ffn2_wg", "ffn2_wu"):
        W[n] = chipw[n]
    for n in ("ffn1_wd", "ffn2_wd", "w_out", "pe_gate_w"):
        W[n] = chipw[n].reshape(-1, D_MODEL)
    W["pe_proj"] = jnp.moveaxis(chipw["pe_proj"], 0, 1).reshape(PLE_DIM, D_MODEL)
    w_in = jnp.moveaxis(chipw["w_in"][:, :, :IN_WIDTH // N_CHIPS], 0, 1).reshape(D_MODEL, IN_WIDTH)
    W["w_in_p"] = _w_in_permute(w_in)
    for n in ("lru_conv_b", "lru_ba", "lru_bx", "lru_lambda", "ssd_conv_b", "ssd_norm_g"):
        W[n] = g(n).reshape(1, -1)
    W["lru_wa_bd"] = _block_diag(g("lru_wa")).astype(BF16)
    W["lru_wx_bd"] = _block_diag(g("lru_wx")).astype(BF16)
    W["fox_bf_vec"] = _lane_vec(g("fox_bf"), LANE_F)
    W["ssd_dtb_vec"] = _lane_vec(g("ssd_dt_bias"), LANE_DT)
    W["ssd_a_vec"] = _lane_vec(-jnp.exp(g("ssd_a_log")), LANE_DT)
    W["ssd_d_exp"] = jnp.repeat(g("ssd_d"), HEAD_DIM).reshape(1, SSD_WIDTH)
    return W


def _layer_big_grads_by_chip(G):
    out = {n: G[n] for n in ("ffn1_wg", "ffn1_wu", "ffn2_wg", "ffn2_wu", "pe_proj")}
    for n in ("ffn1_wd", "ffn2_wd", "w_out", "pe_gate_w"):
        out[n] = G[n].reshape(N_CHIPS, -1, D_MODEL)
    share = IN_WIDTH // N_CHIPS
    d_w_in = jnp.moveaxis(_w_in_unpermute(G["w_in_p"]).reshape(D_MODEL, N_CHIPS, share), 1, 0)
    out["w_in"] = jnp.pad(d_w_in.astype(BF16), ((0, 0), (0, 0), (0, SHARE - share)))
    return out


def _layer_small_grads(G, W):
    out = {n: G[n] for n in ("lru_conv_w", "ssd_conv_w")}
    for n in ("ln1_g", "ln1_b", "ln2_g", "ln2_b", "ln3_g", "ln3_b", "pe_gate_b", "lru_conv_b", "lru_ba", "lru_bx",
              "lru_lambda", "ssd_conv_b", "ssd_norm_g"):
        out[n] = G[n].reshape(-1)
    out["lru_wa"] = _block_diag_extract(G["lru_wa_bd"])
    out["lru_wx"] = _block_diag_extract(G["lru_wx_bd"])
    out["fox_bf"] = G["fox_bf_vec"][0, LANE_F:LANE_F + ATT_HEADS]
    out["ssd_dt_bias"] = G["ssd_dtb_vec"][0, LANE_DT:LANE_DT + SSD_HEADS]
    out["ssd_a_log"] = G["ssd_a_vec"][0, LANE_DT:LANE_DT + SSD_HEADS] * W["ssd_a_vec"][0, LANE_DT:LANE_DT + SSD_HEADS]
    out["ssd_d"] = G["ssd_d_exp"].reshape(SSD_HEADS, HEAD_DIM).sum(axis=1)
    return out


def _local_step(x, p, target, Ws):
    saves = []
    xb = x.astype(BF16)
    for li in range(DEPTH):
        x, xb, sv = _layer_fwd(li, x, xb, p[li].astype(BF16), Ws[li])
        saves.append(sv)
    dx, loss = _loss_kernel(x, target, name="loss")
    big = [None] * DEPTH
    small = [None] * DEPTH
    for li in reversed(range(DEPTH)):
        dx, G = _layer_bwd(li, dx, saves[li], Ws[li])
        big[li] = _layer_big_grads_by_chip(G)
        small[li] = _layer_small_grads(G, Ws[li])
    stacked = {n: jnp.stack([small[li][n] for li in range(DEPTH)]) for n in small[0]}
    return loss, dx, big, stacked


WEIGHTS = ['ln1_g', 'ln1_b', 'ffn1_wg', 'ffn1_wu', 'ffn1_wd', 'w_in', 'lru_conv_w', 'lru_conv_b', 'lru_wa', 'lru_ba',
           'lru_wx', 'lru_bx', 'lru_lambda', 'fox_bf', 'ssd_conv_w', 'ssd_conv_b', 'ssd_dt_bias', 'ssd_a_log', 'ssd_d',
           'ssd_norm_g', 'w_out', 'ln2_g', 'ln2_b', 'ffn2_wg', 'ffn2_wu', 'ffn2_wd', 'ln3_g', 'ln3_b', 'pe_proj',
           'pe_gate_w', 'pe_gate_b']
CLASSES = (("ffn1_wg", "ffn1_wu", "ffn2_wg", "ffn2_wu", "w_in"),
           ("ffn1_wd", "ffn2_wd"),
           ("w_out", "pe_gate_w"),
           ("pe_proj",))
CLASS_PAD_AXIS = (1, 0, None, None)
BIG = {n: ci for ci, names in enumerate(CLASSES) for n in names}
SMALL_SHARDED = {'lru_conv_w': 2, 'ssd_conv_w': 2}
PACK_COLS = 1024


def _unshard(seg, axis):
    moved = jnp.moveaxis(seg, 0, axis)
    shp = list(moved.shape)
    shp[axis:axis + 2] = [shp[axis] * shp[axis + 1]]
    return moved.reshape(shp)


def _pad_axis(a, axis, size):
    if axis is None or a.shape[axis] == size:
        return a
    pads = [(0, 0)] * a.ndim
    pads[axis] = (0, size - a.shape[axis])
    return jnp.pad(a, pads)


def _pack(arrs, dtype, cols):
    flat = jnp.concatenate([a.astype(dtype).reshape(-1) for a in arrs])
    pad = (-flat.shape[0]) % cols
    if pad:
        flat = jnp.concatenate([flat, jnp.zeros((pad,), dtype)])
    return flat.reshape(-1, cols)


def _unpack(flat, shapes):
    out, off = [], 0
    for s in shapes:
        n = math.prod(s)
        out.append(flat[off:off + n].reshape(s))
        off += n
    return out


def kernel(x, p, ln1_g, ln1_b, ffn1_wg, ffn1_wu, ffn1_wd, w_in, lru_conv_w, lru_conv_b, lru_wa, lru_ba, lru_wx, lru_bx, lru_lambda, fox_bf, ssd_conv_w, ssd_conv_b, ssd_dt_bias, ssd_a_log, ssd_d, ssd_norm_g, w_out, ln2_g, ln2_b, ffn2_wg, ffn2_wu, ffn2_wd, ln3_g, ln3_b, pe_proj, pe_gate_w, pe_gate_b, loss_target, m_ln1_g, m_ln1_b, m_ffn1_wg, m_ffn1_wu, m_ffn1_wd, m_w_in, m_lru_conv_w, m_lru_conv_b, m_lru_wa, m_lru_ba, m_lru_wx, m_lru_bx, m_lru_lambda, m_fox_bf, m_ssd_conv_w, m_ssd_conv_b, m_ssd_dt_bias, m_ssd_a_log, m_ssd_d, m_ssd_norm_g, m_w_out, m_ln2_g, m_ln2_b, m_ffn2_wg, m_ffn2_wu, m_ffn2_wd, m_ln3_g, m_ln3_b, m_pe_proj, m_pe_gate_w, m_pe_gate_b, v_ln1_g, v_ln1_b, v_ffn1_wg, v_ffn1_wu, v_ffn1_wd, v_w_in, v_lru_conv_w, v_lru_conv_b, v_lru_wa, v_lru_ba, v_lru_wx, v_lru_bx, v_lru_lambda, v_fox_bf, v_ssd_conv_w, v_ssd_conv_b, v_ssd_dt_bias, v_ssd_a_log, v_ssd_d, v_ssd_norm_g, v_w_out, v_ln2_g, v_ln2_b, v_ffn2_wg, v_ffn2_wu, v_ffn2_wd, v_ln3_g, v_ln3_b, v_pe_proj, v_pe_gate_w, v_pe_gate_b):
    args = locals()
    w_loc = {n: args[n] for n in WEIGHTS}
    m_loc = {n: args["m_" + n] for n in WEIGHTS}
    v_loc = {n: args["v_" + n] for n in WEIGHTS}
    chip = 2 * lax.axis_index("x") + lax.axis_index("y")
    core = lax.axis_index("c")
    big = list(BIG)
    small_sh = list(SMALL_SHARDED)
    small_rep = [n for n in WEIGHTS if n not in BIG and n not in SMALL_SHARDED]

    srcs = [jnp.stack([_pad_axis(w_loc[n][li].astype(BF16), pad, SHARE) for li in range(DEPTH) for n in names])
            for names, pad in zip(CLASSES, CLASS_PAD_AXIS)]
    halves = _exchange(srcs, ("x", "y"), mode="gather_half", local=False, name="gather_w_chips")
    halves = [lax.dynamic_update_index_in_dim(
                  h, lax.dynamic_index_in_dim(s.reshape((2, -1) + s.shape[1:]), core, 0, keepdims=False), chip, 0)
              for h, s in zip(halves, srcs)]
    both = _exchange([h.reshape((-1,) + h.shape[2:]) for h in halves], ("c",), mode="gather", local=False,
                     name="gather_w_cores")
    both = [lax.dynamic_update_index_in_dim(b, h.reshape(b.shape[1:]), core, 0) for b, h in zip(both, halves)]
    chipw = [{} for _ in range(DEPTH)]
    for names, b in zip(CLASSES, both):
        hn = len(names)
        b = b.reshape((2, N_CHIPS, hn) + b.shape[2:])
        for li in range(DEPTH):
            for j, n in enumerate(names):
                chipw[li][n] = b[li, :, j]
    small = {n: w_loc[n] for n in small_rep}
    spack = _pack([w_loc[n] for n in small_sh], F32, LANES)
    (sg,) = _exchange([spack[None]], ("x", "y"), mode="gather", name="gather_conv_w")
    for n, seg in zip(small_sh, _unpack_rows(sg.reshape(N_CHIPS, -1), [w_loc[n].shape for n in small_sh])):
        small[n] = _unshard(seg, SMALL_SHARDED[n])
    Ws = [_layer_weights(li, chipw[li], small) for li in range(DEPTH)]

    loss, grad_x, g_big, g_small = _local_step(x[0], p[:, 0], loss_target[0], Ws)
    loss = lax.psum(loss[0, 0], MESH_AXES)

    gcls = [jnp.stack([g_big[li][n] for li in range(DEPTH) for n in names]) for names in CLASSES]
    gcls = [g.reshape((2, -1) + g.shape[1:]) for g in gcls]
    pair = _exchange(gcls, ("c",), mode="a2a", local=False, name="reduce_cores")
    pair = [lax.dynamic_update_index_in_dim(pr, lax.dynamic_index_in_dim(g, core, 0, keepdims=False), core, 0)
            for pr, g in zip(pair, gcls)]
    s2 = [_sum_slots(pr.reshape(2, -1, pr.shape[-1]), BF16, name=f"reduce_cores_sum{ci}").reshape(pr.shape[1:])
          for ci, pr in enumerate(pair)]
    quad = _exchange(s2, ("x", "y"), mode="a2a_inner", local=False, name="reduce_chips")
    quad = [lax.dynamic_update_index_in_dim(q, lax.dynamic_index_in_dim(s, chip, 1, keepdims=False), chip, 0)
            for q, s in zip(quad, s2)]
    red = [_sum_slots(q.reshape(N_CHIPS, -1, q.shape[-1]), F32, name=f"reduce_chips_sum{ci}").reshape(q.shape[1:])
           for ci, q in enumerate(quad)]
    shared = _exchange(red, ("c",), mode="gather", local=False, name="reduce_share")
    shared = [lax.dynamic_update_index_in_dim(sh, r, core, 0) for sh, r in zip(shared, red)]
    g_red = {}
    for names, sh in zip(CLASSES, shared):
        sh = sh.reshape((-1,) + sh.shape[2:])
        for j, n in enumerate(names):
            g = jnp.stack([sh[li * len(names) + j] for li in range(DEPTH)])
            g_red[n] = g[tuple(slice(0, s) for s in w_loc[n].shape)]
    small_all = small_rep + small_sh
    sgp = _pack([g_small[n] for n in small_all], F32, PACK_COLS)
    (sall,) = _exchange([sgp[None]], MESH_AXES, mode="gather", name="reduce_small")
    sred = _sum_slots(sall.reshape((2 ** len(MESH_AXES),) + sgp.shape), F32, name="reduce_small_sum").reshape(-1)
    for n, g in zip(small_all, _unpack(sred, [g_small[n].shape for n in small_all])):
        if n in SMALL_SHARDED:
            width = w_loc[n].shape[-1]
            g = lax.dynamic_slice_in_dim(g, chip * width, width, axis=SMALL_SHARDED[n])
        g_red[n] = g

    delta, new_m, new_v = {}, {}, {}
    for n in big:
        delta[n], new_m[n], new_v[n] = _adamw(w_loc[n], g_red[n], m_loc[n], v_loc[n], name="adamw_" + n)
    shapes = [w_loc[n].shape for n in small_all]
    packs = [_pack([d[n] for n in small_all], F32, LANES) for d in (w_loc, g_red, m_loc, v_loc)]
    outs = _adamw(*packs, name="adamw_small")
    for d, o in zip((delta, new_m, new_v), outs):
        for n, a in zip(small_all, _unpack(o.reshape(-1), shapes)):
            d[n] = a
    return (loss, grad_x[None], *[g_red[n] for n in WEIGHTS], *[delta[n] for n in WEIGHTS],
            *[new_m[n] for n in WEIGHTS], *[new_v[n] for n in WEIGHTS])


def _unpack_rows(gathered, shapes):
    out, off = [], 0
    for s in shapes:
        n = math.prod(s)
        out.append(gathered[:, off:off + n].reshape((N_CHIPS,) + tuple(s)))
        off += n
    return out
```

```python
import functools
import math

import jax
import jax.numpy as jnp
from jax import lax
from jax.experimental import pallas as pl
from jax.experimental.pallas import tpu as pltpu

F32 = jnp.float32
BF16 = jnp.bfloat16

D_MODEL = 1024
DEPTH = 2
PLE_DIM = 256
HEAD_DIM = 64
LRU_WIDTH = 256
LRU_HEADS = 4
LRU_C = 8.0
CONV_K = 4
ATT_WIDTH = 256
ATT_HEADS = 4
SSD_WIDTH = 512
SSD_HEADS = 8
SSD_GROUPS = 2
SSD_STATE = 128
SSD_CHUNK = 128
SSD_CONV_DIM = 1024
FFN_DIM = 2816
ALPHA = (2.0 * DEPTH) ** 0.25
LN_EPS = 1e-5
RMS_EPS = 1e-5
IN_WIDTH = 2828
ADAM_LR = 0.001
ADAM_B1 = 0.9
ADAM_B2 = 0.999
ADAM_EPS = 1e-08
ADAM_WD = 0.01
ADAM_STEP = 10

H_WIDTH = 3072
COL_XBC, COL_Z, COL_U, COL_G, COL_Q, COL_K, COL_V, COL_SMALL = 0, 1024, 1536, 1792, 2048, 2304, 2560, 2816
LANE_F = 0
LANE_DT = 4
LANES = 128
SUBLANES = 8
NEG = -1e30

VMEM_LIMIT = 48 * 1024 * 1024

N_CHIPS = 4
MESH_AXES = ("x", "y", "c")
SHARE = 768


def _params(n):
    return pltpu.CompilerParams(dimension_semantics=("arbitrary",) * n, vmem_limit_bytes=VMEM_LIMIT)


def _pick(n, cands):
    for c in cands:
        if n % c == 0:
            return c
    return n


def _iota(shape, dim):
    return lax.broadcasted_iota(jnp.int32, shape, dim)


def _shift_down(x, s, prev8):
    if s == 0:
        return x
    r = pltpu.roll(x, s, 0)
    pr = pltpu.roll(prev8, s, 0)
    head = jnp.where(_iota(pr.shape, 0) < s, pr, r[:SUBLANES])
    return jnp.concatenate([head, r[SUBLANES:]], axis=0)


def _shift_up(x, s, next8):
    if s == 0:
        return x
    n = x.shape[0]
    r = pltpu.roll(x, n - s, 0)
    nr = pltpu.roll(next8, SUBLANES - s, 0)
    tail = jnp.where(_iota(nr.shape, 0) >= SUBLANES - s, nr, r[n - SUBLANES:])
    return jnp.concatenate([r[:n - SUBLANES], tail], axis=0)


def _scan_fwd(a, b):
    n = a.shape[0]
    row = _iota(a.shape, 0)
    d = 1
    while d < n:
        keep = row >= d
        a_s = jnp.where(keep, pltpu.roll(a, d, 0), 1.0)
        b_s = jnp.where(keep, pltpu.roll(b, d, 0), 0.0)
        b = a * b_s + b
        a = a * a_s
        d *= 2
    return a, b


def _scan_bwd(a, b):
    n = a.shape[0]
    row = _iota(a.shape, 0)
    d = 1
    while d < n:
        keep = row < n - d
        a_s = jnp.where(keep, pltpu.roll(a, n - d, 0), 1.0)
        b_s = jnp.where(keep, pltpu.roll(b, n - d, 0), 0.0)
        b = a * b_s + b
        a = a * a_s
        d *= 2
    return a, b


def _cumsum_rows(x, reverse=False):
    n = x.shape[0]
    row = _iota(x.shape, 0)
    d = 1
    while d < n:
        if reverse:
            x = x + jnp.where(row < n - d, pltpu.roll(x, n - d, 0), 0.0)
        else:
            x = x + jnp.where(row >= d, pltpu.roll(x, d, 0), 0.0)
        d *= 2
    return x


def _col(x, lane):
    return jnp.sum(jnp.where(_iota(x.shape, 1) == lane, x, 0.0), axis=1, keepdims=True)


def _row(x, r):
    return jnp.sum(jnp.where(_iota(x.shape, 0) == r, x, 0.0), axis=0, keepdims=True)


def _sigmoid(x):
    return jax.nn.sigmoid(x)


def _softplus(x):
    return jnp.maximum(x, 0.0) + jnp.log(1.0 + jnp.exp(-jnp.abs(x)))


def _gelu_and_grad(x):
    c0 = math.sqrt(2.0 / math.pi)
    inner = c0 * (x + 0.044715 * x * x * x)
    t = jnp.tanh(inner)
    g = 0.5 * x * (1.0 + t)
    dg = 0.5 * (1.0 + t) + 0.5 * x * (1.0 - t * t) * c0 * (1.0 + 3.0 * 0.044715 * x * x)
    return g, dg


def _dot(a, b, ca, cb):
    return lax.dot_general(a, b, (((ca,), (cb,)), ((), ())), preferred_element_type=F32)


def _conv_taps(xr, prev8, w, bias):
    y = bias + w[CONV_K - 1:CONV_K, :] * xr
    for j in range(CONV_K - 1):
        y = y + w[j:j + 1, :] * _shift_down(xr, CONV_K - 1 - j, prev8)
    return y


def _conv_taps_bwd(dy, next8, w, xr):
    dx = None
    dws = []
    for j in range(CONV_K):
        sh = _shift_up(dy, CONV_K - 1 - j, next8)
        term = w[j:j + 1, :] * sh
        dx = term if dx is None else dx + term
        dws.append(jnp.sum(sh * xr, axis=0, keepdims=True))
    return dx, jnp.concatenate(dws, axis=0)


def _head_expand(v, lane0, nheads, width):
    rows = v.shape[0]
    colhead = _iota((rows, width), 1) // HEAD_DIM
    out = jnp.zeros((rows, width), F32)
    for h in range(nheads):
        out = jnp.where(colhead == h, _col(v, lane0 + h), out)
    return out


def _head_reduce(x, lane0, nheads):
    rows = x.shape[0]
    colhead = _iota(x.shape, 1) // HEAD_DIM
    lane = _iota((rows, LANES), 1)
    out = jnp.zeros((rows, LANES), F32)
    for h in range(nheads):
        s = jnp.sum(jnp.where(colhead == h, x, 0.0), axis=1, keepdims=True)
        out = jnp.where(lane == lane0 + h, s, out)
    return out


def _mm(a, b, *, ta=False, tb=False, scale=1.0, out_dtype=F32, chip_cols=False, name):
    if ta:
        kk, m = a.shape
    else:
        m, kk = a.shape
    n = b.shape[0] if tb else b.shape[1]
    tm = _pick(m, (1024, 512, 256, 128))
    tn = _pick(n // N_CHIPS, (768, 256, 128)) if chip_cols else _pick(n, (1024, 768, 512, 256, 128))
    tk = _pick(kk, (1024, 768, 512, 256, 128))
    nk = kk // tk
    dn_a = 0 if ta else 1
    dn_b = 1 if tb else 0
    if chip_cols:
        per = n // N_CHIPS // tn
        out_spec = pl.BlockSpec((None, tm, tn), lambda i, j, k: (j // per, i, j % per))
        out_shape = jax.ShapeDtypeStruct((N_CHIPS, m, n // N_CHIPS), out_dtype)
    else:
        out_spec = pl.BlockSpec((tm, tn), lambda i, j, k: (i, j))
        out_shape = jax.ShapeDtypeStruct((m, n), out_dtype)

    def body(a_ref, b_ref, o_ref, acc):
        k = pl.program_id(2)

        @pl.when(k == 0)
        def _():
            acc[...] = jnp.zeros_like(acc)

        acc[...] += _dot(a_ref[...].astype(BF16), b_ref[...].astype(BF16), dn_a, dn_b)

        @pl.when(k == nk - 1)
        def _():
            o_ref[...] = (acc[...] * scale).astype(out_dtype)

    a_spec = pl.BlockSpec((tk, tm), lambda i, j, k: (k, i)) if ta else pl.BlockSpec((tm, tk), lambda i, j, k: (i, k))
    b_spec = pl.BlockSpec((tn, tk), lambda i, j, k: (j, k)) if tb else pl.BlockSpec((tk, tn), lambda i, j, k: (k, j))
    return pl.pallas_call(
        body, name=name, grid=(m // tm, n // tn, nk),
        in_specs=[a_spec, b_spec],
        out_specs=out_spec, out_shape=out_shape,
        scratch_shapes=[pltpu.VMEM((tm, tn), F32)],
        compiler_params=_params(3),
    )(a, b)


def _mm_swiglu(xb, wg, wu, *, name):
    t, d = xb.shape
    share = wg.shape[2]
    n = N_CHIPS * share
    tm = _pick(t, (512, 256, 128))
    tn = _pick(share, (768, 256, 128))
    per = share // tn

    def body(x_ref, wg_ref, wu_ref, g_ref, u_ref, a_ref):
        x = x_ref[...]
        g = _dot(x, wg_ref[...], 1, 0)
        u = _dot(x, wu_ref[...], 1, 0)
        g_ref[...] = g.astype(BF16)
        u_ref[...] = u.astype(BF16)
        a_ref[...] = (g * _sigmoid(g) * u).astype(BF16)

    o = jax.ShapeDtypeStruct((t, n), BF16)
    ospec = pl.BlockSpec((tm, tn), lambda j, i: (i, j))
    return pl.pallas_call(
        body, name=name, grid=(n // tn, t // tm),
        in_specs=[pl.BlockSpec((tm, d), lambda j, i: (i, 0)),
                  pl.BlockSpec((None, d, tn), lambda j, i: (j // per, 0, j % per)),
                  pl.BlockSpec((None, d, tn), lambda j, i: (j // per, 0, j % per))],
        out_specs=[ospec, ospec, ospec], out_shape=[o, o, o],
        compiler_params=_params(2),
    )(xb, wg, wu)


def _mm_swiglu_bwd(dr, wd, g, u, *, scale, name):
    t, d = dr.shape
    n = wd.shape[0]
    tm = _pick(t, (512, 256, 128))
    tn = _pick(n, (768, 256, 128))

    def body(dr_ref, wd_ref, g_ref, u_ref, dg_ref, du_ref):
        da = _dot(dr_ref[...].astype(BF16), wd_ref[...], 1, 1) * scale
        gg = g_ref[...].astype(F32)
        uu = u_ref[...].astype(F32)
        sg = _sigmoid(gg)
        dg_ref[...] = (da * uu * (sg * (1.0 + gg * (1.0 - sg)))).astype(BF16)
        du_ref[...] = (da * gg * sg).astype(BF16)

    o = jax.ShapeDtypeStruct((t, n), BF16)
    ospec = pl.BlockSpec((tm, tn), lambda j, i: (i, j))
    return pl.pallas_call(
        body, name=name, grid=(n // tn, t // tm),
        in_specs=[pl.BlockSpec((tm, d), lambda j, i: (i, 0)),
                  pl.BlockSpec((tn, d), lambda j, i: (j, 0)),
                  ospec, ospec],
        out_specs=[ospec, ospec], out_shape=[o, o],
        compiler_params=_params(2),
    )(dr, wd, g, u)


def _mm_ln(a, w, resid, gain, bias, *, rscale, mscale, name):
    t, kk = a.shape
    d = w.shape[1]
    tm = _pick(t, (512, 256, 128))
    tk = kk
    nk = kk // tk

    def body(a_ref, w_ref, r_ref, g_ref, b_ref, y_ref, yb_ref, xh_ref, rs_ref, acc):
        k = pl.program_id(1)

        @pl.when(k == 0)
        def _():
            acc[...] = jnp.zeros_like(acc)

        acc[...] += _dot(a_ref[...].astype(BF16), w_ref[...], 1, 0)

        @pl.when(k == nk - 1)
        def _():
            r = rscale * r_ref[...] + mscale * acc[...]
            mu = jnp.mean(r, axis=1, keepdims=True)
            xc = r - mu
            var = jnp.mean(xc * xc, axis=1, keepdims=True)
            rstd = lax.rsqrt(var + LN_EPS)
            xh = xc * rstd
            y = xh * g_ref[...] + b_ref[...]
            y_ref[...] = y
            yb_ref[...] = y.astype(BF16)
            xh_ref[...] = xh
            rs_ref[...] = rstd

    row = pl.BlockSpec((tm, d), lambda i, k: (i, 0))
    vec = pl.BlockSpec((1, d), lambda i, k: (0, 0))
    return pl.pallas_call(
        body, name=name, grid=(t // tm, nk),
        in_specs=[pl.BlockSpec((tm, tk), lambda i, k: (i, k)),
                  pl.BlockSpec((tk, d), lambda i, k: (k, 0)), row, vec, vec],
        out_specs=[row, row, row, pl.BlockSpec((tm, 1), lambda i, k: (i, 0))],
        out_shape=[jax.ShapeDtypeStruct((t, d), F32), jax.ShapeDtypeStruct((t, d), BF16),
                   jax.ShapeDtypeStruct((t, d), F32), jax.ShapeDtypeStruct((t, 1), F32)],
        scratch_shapes=[pltpu.VMEM((tm, d), F32)],
        compiler_params=_params(2),
    )(a, w, resid, gain.reshape(1, d), bias.reshape(1, d))


def _bwd_proj(pairs, resid, *, rscale, ln, name):
    t, kk = pairs[0][0].shape
    d = pairs[0][1].shape[-2]
    tm = _pick(t, (512, 256, 128))
    tk = _pick(pairs[0][1].shape[-1], (1024, 768, 512, 256, 128))
    nk = kk // tk
    nt = t // tm
    npair = len(pairs)
    has_ln = ln is not None

    def body(*refs):
        ab = refs[:2 * npair]
        r_ref = refs[2 * npair]
        pos = 2 * npair + 1
        if has_ln:
            xh_ref, rs_ref, g_ref = refs[pos:pos + 3]
            pos += 3
            o_ref, ob_ref, dg_ref, db_ref = refs[pos:pos + 4]
            pos += 4
        else:
            o_ref = refs[pos]
            pos += 1
        acc = refs[pos]
        i = pl.program_id(0)
        k = pl.program_id(1)

        @pl.when(k == 0)
        def _():
            acc[...] = jnp.zeros_like(acc)

        for q in range(npair):
            acc[...] += _dot(ab[2 * q][...].astype(BF16), ab[2 * q + 1][...], 1, 1)

        @pl.when(k == nk - 1)
        def _():
            dy = rscale * r_ref[...] + acc[...]
            if not has_ln:
                o_ref[...] = dy
                return
            xh = xh_ref[...]
            w = dy * g_ref[...]
            m1 = jnp.mean(w, axis=1, keepdims=True)
            m2 = jnp.mean(w * xh, axis=1, keepdims=True)
            dr = rs_ref[...] * (w - m1 - xh * m2)
            o_ref[...] = dr
            ob_ref[...] = dr.astype(BF16)

            @pl.when(i == 0)
            def _():
                dg_ref[...] = jnp.zeros_like(dg_ref)
                db_ref[...] = jnp.zeros_like(db_ref)

            dg_ref[...] += jnp.sum(dy * xh, axis=0, keepdims=True)
            db_ref[...] += jnp.sum(dy, axis=0, keepdims=True)

    row = pl.BlockSpec((tm, d), lambda i, k: (i, 0))
    vec = pl.BlockSpec((1, d), lambda i, k: (0, 0))
    in_specs, args = [], []
    for a, b in pairs:
        if b.ndim == 3:
            per = b.shape[2] // tk
            b_spec = pl.BlockSpec((None, d, tk), lambda i, k, per=per: (k // per, 0, k % per))
        else:
            b_spec = pl.BlockSpec((d, tk), lambda i, k: (0, k))
        in_specs += [pl.BlockSpec((tm, tk), lambda i, k: (i, k)), b_spec]
        args += [a, b]
    in_specs.append(row)
    args.append(resid)
    out_specs = [row]
    out_shape = [jax.ShapeDtypeStruct((t, d), F32)]
    if has_ln:
        xh, rs, gain = ln
        in_specs += [row, pl.BlockSpec((tm, 1), lambda i, k: (i, 0)), vec]
        args += [xh, rs, gain.reshape(1, d)]
        out_specs += [row, vec, vec]
        out_shape += [jax.ShapeDtypeStruct((t, d), BF16)] + [jax.ShapeDtypeStruct((1, d), F32)] * 2
    return pl.pallas_call(
        body, name=name, grid=(nt, nk), in_specs=in_specs, out_specs=out_specs, out_shape=out_shape,
        scratch_shapes=[pltpu.VMEM((tm, d), F32)],
        compiler_params=_params(2),
    )(*args)


def _mm_pe(x3, x3b, pb, wgate, bgate, wproj, *, name):
    t, d = x3.shape
    pd = pb.shape[1]
    tm = _pick(t, (512, 256, 128))
    tn = _pick(d, (512, 256, 128))

    def body(x_ref, xb_ref, p_ref, wg_ref, bg_ref, wp_ref, y_ref, yb_ref, sg_ref, e_ref):
        sg = _sigmoid(_dot(xb_ref[...], wg_ref[...], 1, 0) + bg_ref[...])
        e = _dot(p_ref[...], wp_ref[...], 1, 0)
        y = x_ref[...] + sg * e
        y_ref[...] = y
        yb_ref[...] = y.astype(BF16)
        sg_ref[...] = sg.astype(BF16)
        e_ref[...] = e.astype(BF16)

    ospec = pl.BlockSpec((tm, tn), lambda i, j: (i, j))
    ob = jax.ShapeDtypeStruct((t, d), BF16)
    return pl.pallas_call(
        body, name=name, grid=(t // tm, d // tn),
        in_specs=[ospec, pl.BlockSpec((tm, d), lambda i, j: (i, 0)), pl.BlockSpec((tm, pd), lambda i, j: (i, 0)),
                  pl.BlockSpec((d, tn), lambda i, j: (0, j)), pl.BlockSpec((1, tn), lambda i, j: (0, j)),
                  pl.BlockSpec((pd, tn), lambda i, j: (0, j))],
        out_specs=[ospec, ospec, ospec, ospec],
        out_shape=[jax.ShapeDtypeStruct((t, d), F32), ob, ob, ob],
        compiler_params=_params(2),
    )(x3, x3b, pb, wgate, bgate.reshape(1, d), wproj)


def _pe_bwd_elem(dx4, sg, e, *, name):
    t, d = dx4.shape
    tm = _pick(t, (512, 256, 128))

    def body(dx_ref, sg_ref, e_ref, dgp_ref, de_ref, db_ref):
        dx = dx_ref[...]
        s = sg_ref[...].astype(F32)
        dgp = dx * e_ref[...].astype(F32) * s * (1.0 - s)
        dgp_ref[...] = dgp.astype(BF16)
        de_ref[...] = (dx * s).astype(BF16)

        @pl.when(pl.program_id(0) == 0)
        def _():
            db_ref[...] = jnp.zeros_like(db_ref)

        db_ref[...] += jnp.sum(dgp, axis=0, keepdims=True)

    row = pl.BlockSpec((tm, d), lambda i: (i, 0))
    ob = jax.ShapeDtypeStruct((t, d), BF16)
    return pl.pallas_call(
        body, name=name, grid=(t // tm,), in_specs=[row, row, row],
        out_specs=[row, row, pl.BlockSpec((1, d), lambda i: (0, 0))],
        out_shape=[ob, ob, jax.ShapeDtypeStruct((1, d), F32)],
        compiler_params=_params(1),
    )(dx4, sg, e)


def _loss_kernel(y, target, *, name):
    t, d = y.shape
    tm = _pick(t, (512, 256, 128))

    def body(y_ref, t_ref, dy_ref, l_ref):
        diff = y_ref[...] - t_ref[...]
        dy_ref[...] = diff * (1.0 / d)

        @pl.when(pl.program_id(0) == 0)
        def _():
            l_ref[...] = jnp.zeros_like(l_ref)

        part = jnp.sum(jnp.mean(diff * diff, axis=1, keepdims=True), axis=0, keepdims=True)
        l_ref[...] += 0.5 * part

    row = pl.BlockSpec((tm, d), lambda i: (i, 0))
    return pl.pallas_call(
        body, name=name, grid=(t // tm,), in_specs=[row, row],
        out_specs=[row, pl.BlockSpec((1, 1), lambda i: (0, 0))],
        out_shape=[jax.ShapeDtypeStruct((t, d), F32), jax.ShapeDtypeStruct((1, 1), F32)],
        compiler_params=_params(1),
    )(y, target)


LRU_TM = 256


def _lru_gate_terms(r, lam):
    sp = _softplus(-lam)
    la = -LRU_C * r * sp
    a = jnp.exp(la)
    em = jnp.tanh(la) * (jnp.exp(2.0 * la) + 1.0)
    s = jnp.sqrt(-em)
    return la, a, s, sp


def _lru_fwd(hbuf, conv_w, conv_b, wa, ba, wx, bx, lam, *, name):
    t = hbuf.shape[0]
    w = LRU_WIDTH
    tm = _pick(t, (LRU_TM, 128))
    cu, cg = COL_U // w, COL_G // w
    hb = tm // SUBLANES

    def body(u_ref, up_ref, g_ref, cw_ref, cb_ref, wa_ref, ba_ref, wx_ref, bx_ref, lam_ref,
             y_ref, u_out, r_out, i_out, a_out, h_out, carry):
        i = pl.program_id(0)

        @pl.when(i == 0)
        def _():
            carry[...] = jnp.zeros_like(carry)

        prev = jnp.where(i == 0, 0.0, up_ref[...])
        u = _conv_taps(u_ref[...], prev, cw_ref[...], cb_ref[...])
        ub = u.astype(BF16)
        r = _sigmoid(_dot(ub, wa_ref[...], 1, 0) + ba_ref[...])
        ig = _sigmoid(_dot(ub, wx_ref[...], 1, 0) + bx_ref[...])
        _, a, s, _ = _lru_gate_terms(r, lam_ref[...])
        b = s * (ig * u)
        acum, hs = _scan_fwd(a, b)
        h = hs + acum * carry[0:1, :]
        carry[...] = jnp.broadcast_to(h[tm - 1:tm, :], carry.shape)
        gl, _ = _gelu_and_grad(g_ref[...])
        y_ref[...] = h * gl
        u_out[...] = u
        r_out[...] = r
        i_out[...] = ig
        a_out[...] = a
        h_out[...] = h

    row = pl.BlockSpec((tm, w), lambda i: (i, 0))
    vec = pl.BlockSpec((1, w), lambda i: (0, 0))
    mat = pl.BlockSpec((w, w), lambda i: (0, 0))
    o = jax.ShapeDtypeStruct((t, w), F32)
    return pl.pallas_call(
        body, name=name, grid=(t // tm,),
        in_specs=[pl.BlockSpec((tm, w), lambda i: (i, cu)),
                  pl.BlockSpec((SUBLANES, w), lambda i: (jnp.maximum(i * hb - 1, 0), cu)),
                  pl.BlockSpec((tm, w), lambda i: (i, cg)),
                  pl.BlockSpec((CONV_K, w), lambda i: (0, 0)), vec, mat, vec, mat, vec, vec],
        out_specs=[row] * 6, out_shape=[o] * 6,
        scratch_shapes=[pltpu.VMEM((SUBLANES, w), F32)],
        compiler_params=_params(1),
    )(hbuf, hbuf, hbuf, conv_w, conv_b, wa, ba, wx, bx, lam)


def _lru_bwd(dymix, hbuf, u, r, ig, a, h, conv_w, wa, wx, lam, *, name):
    t = hbuf.shape[0]
    w = LRU_WIDTH
    tm = _pick(t, (LRU_TM, 128))
    nb = t // tm
    cu, cg = COL_U // w, COL_G // w
    hb = tm // SUBLANES
    last8 = t // SUBLANES - 1

    def body(dy_ref, ur_ref, g_ref, u_ref, r_ref, i_ref, a_ref, an_ref, h_ref, hp_ref,
             cw_ref, wa_ref, wx_ref, lam_ref,
             dur_ref, dgr_ref, dcw_ref, dcb_ref, dwa_ref, dba_ref, dwx_ref, dbx_ref, dlam_ref,
             lcarry, dnext):
        i = pl.program_id(0)
        ib = nb - 1 - i

        @pl.when(i == 0)
        def _():
            lcarry[...] = jnp.zeros_like(lcarry)
            dnext[...] = jnp.zeros_like(dnext)
            for ref in (dcw_ref, dcb_ref, dwa_ref, dba_ref, dwx_ref, dbx_ref, dlam_ref):
                ref[...] = jnp.zeros_like(ref)

        dy = dy_ref[...]
        hh = h_ref[...]
        av = a_ref[...]
        uu = u_ref[...]
        rr = r_ref[...]
        ii = i_ref[...]
        lam_v = lam_ref[...]
        gl, dgl = _gelu_and_grad(g_ref[...])
        dgr_ref[...] = (dy * hh * dgl).astype(BF16)
        dh_out = dy * gl
        a_next = _shift_up(av, 1, jnp.where(ib == nb - 1, 0.0, an_ref[...]))
        acum, ls = _scan_bwd(a_next, dh_out)
        lam_adj = ls + acum * lcarry[0:1, :]
        lcarry[...] = jnp.broadcast_to(lam_adj[0:1, :], lcarry.shape)
        h_prev = _shift_down(hh, 1, jnp.where(ib == 0, 0.0, hp_ref[...]))
        da = lam_adj * h_prev
        _, a2, s, sp = _lru_gate_terms(rr, lam_v)
        d_igu = lam_adj * s
        ds = lam_adj * ii * uu
        dla = da * a2 - ds * (a2 * a2) / s
        dr = dla * (-LRU_C * sp)
        dlam_ref[...] += jnp.sum(dla * (LRU_C * rr * _sigmoid(-lam_v)), axis=0, keepdims=True)
        dpre_r = dr * rr * (1.0 - rr)
        dpre_i = d_igu * uu * ii * (1.0 - ii)
        prb = dpre_r.astype(BF16)
        pib = dpre_i.astype(BF16)
        ub = uu.astype(BF16)
        du = d_igu * ii + _dot(prb, wa_ref[...], 1, 1) + _dot(pib, wx_ref[...], 1, 1)
        dwa_ref[...] += _dot(ub, prb, 0, 0)
        dwx_ref[...] += _dot(ub, pib, 0, 0)
        dba_ref[...] += jnp.sum(dpre_r, axis=0, keepdims=True)
        dbx_ref[...] += jnp.sum(dpre_i, axis=0, keepdims=True)
        dur, dws = _conv_taps_bwd(du, dnext[...], cw_ref[...], ur_ref[...])
        dur_ref[...] = dur.astype(BF16)
        dcw_ref[...] += dws
        dcb_ref[...] += jnp.sum(du, axis=0, keepdims=True)
        dnext[...] = du[:SUBLANES]

    def rowspec(col):
        return pl.BlockSpec((tm, w), lambda i: (nb - 1 - i, col))

    row = rowspec(0)
    nxt = pl.BlockSpec((SUBLANES, w), lambda i: (jnp.minimum((nb - i) * hb, last8), 0))
    prv = pl.BlockSpec((SUBLANES, w), lambda i: (jnp.maximum((nb - 1 - i) * hb - 1, 0), 0))
    vec = pl.BlockSpec((1, w), lambda i: (0, 0))
    mat = pl.BlockSpec((w, w), lambda i: (0, 0))
    cw = pl.BlockSpec((CONV_K, w), lambda i: (0, 0))
    o = jax.ShapeDtypeStruct((t, w), BF16)
    v1 = jax.ShapeDtypeStruct((1, w), F32)
    m1 = jax.ShapeDtypeStruct((w, w), F32)
    return pl.pallas_call(
        body, name=name, grid=(nb,),
        in_specs=[rowspec(0), rowspec(cu), rowspec(cg), row, row, row, row, nxt, row, prv, cw, mat, mat, vec],
        out_specs=[row, row, cw, vec, mat, vec, mat, vec, vec],
        out_shape=[o, o, jax.ShapeDtypeStruct((CONV_K, w), F32), v1, m1, v1, m1, v1, v1],
        scratch_shapes=[pltpu.VMEM((SUBLANES, w), F32), pltpu.VMEM((SUBLANES, w), F32)],
        compiler_params=_params(1),
    )(dymix, hbuf, hbuf, u, r, ig, a, a, h, h, conv_w, wa, wx, lam)


FOX_T = 512
FOX_PREP_TM = 256


def _log_sigmoid(x):
    return jnp.minimum(x, 0.0) - jnp.log(1.0 + jnp.exp(-jnp.abs(x)))


def _fox_prep(hbuf, bf_vec, *, name):
    t = hbuf.shape[0]
    tm = _pick(t, (FOX_PREP_TM, 128))
    cs = COL_SMALL // LANES

    def body(s_ref, b_ref, eq_ref, ek_ref, carry):
        i = pl.program_id(0)

        @pl.when(i == 0)
        def _():
            carry[...] = jnp.zeros_like(carry)

        lf = _log_sigmoid(s_ref[...] + b_ref[...])
        f = _cumsum_rows(lf) + carry[0:1, :]
        carry[...] = jnp.broadcast_to(f[tm - 1:tm, :], carry.shape)
        lane = _iota((tm, LANES), 1)
        for h in range(ATT_HEADS):
            base = HEAD_DIM * (1 - h % 2)
            fh = _col(f, h)
            hi = fh.astype(BF16).astype(F32)
            mid = (fh - hi).astype(BF16).astype(F32)
            lo = fh - hi - mid
            terms = jnp.where(lane == base, hi, jnp.where(lane == base + 1, mid, jnp.where(lane == base + 2, lo, 0.0)))
            terms_k = jnp.where(lane == base + 3, -hi,
                                jnp.where(lane == base + 4, -mid, jnp.where(lane == base + 5, -lo, 0.0)))
            ones_q = ((lane >= base + 3) & (lane < base + 6)).astype(F32)
            ones_k = ((lane >= base) & (lane < base + 3)).astype(F32)
            eq_ref[:, LANES * h:LANES * (h + 1)] = (terms + ones_q).astype(BF16)
            ek_ref[:, LANES * h:LANES * (h + 1)] = (terms_k + ones_k).astype(BF16)

    ospec = pl.BlockSpec((tm, ATT_HEADS * LANES), lambda i: (i, 0))
    o = jax.ShapeDtypeStruct((t, ATT_HEADS * LANES), BF16)
    return pl.pallas_call(
        body, name=name, grid=(t // tm,),
        in_specs=[pl.BlockSpec((tm, LANES), lambda i: (i, cs)), pl.BlockSpec((1, LANES), lambda i: (0, 0))],
        out_specs=[ospec, ospec], out_shape=[o, o],
        scratch_shapes=[pltpu.VMEM((SUBLANES, LANES), F32)],
        compiler_params=_params(1),
    )(hbuf, bf_vec)


def _fox_post(dfc, hbuf, bf_vec, *, name):
    t = hbuf.shape[0]
    tm = _pick(t, (FOX_PREP_TM, 128))
    nb = t // tm
    cs = COL_SMALL // LANES

    def body(df_ref, s_ref, b_ref, o_ref, db_ref, carry):
        i = pl.program_id(0)

        @pl.when(i == 0)
        def _():
            carry[...] = jnp.zeros_like(carry)
            db_ref[...] = jnp.zeros_like(db_ref)

        dlf = _cumsum_rows(df_ref[...], reverse=True) + carry[0:1, :]
        carry[...] = jnp.broadcast_to(dlf[0:1, :], carry.shape)
        dl = dlf * _sigmoid(-(s_ref[...] + b_ref[...]))
        dl = jnp.where(_iota(dl.shape, 1) < ATT_HEADS, dl, 0.0)
        o_ref[...] = dl
        db_ref[...] += jnp.sum(dl, axis=0, keepdims=True)

    vec = pl.BlockSpec((1, LANES), lambda i: (0, 0))
    return pl.pallas_call(
        body, name=name, grid=(nb,),
        in_specs=[pl.BlockSpec((tm, LANES), lambda i: (nb - 1 - i, 0)),
                  pl.BlockSpec((tm, LANES), lambda i: (nb - 1 - i, cs)), vec],
        out_specs=[pl.BlockSpec((tm, LANES), lambda i: (nb - 1 - i, 0)), vec],
        out_shape=[jax.ShapeDtypeStruct((t, LANES), F32), jax.ShapeDtypeStruct((1, LANES), F32)],
        scratch_shapes=[pltpu.VMEM((SUBLANES, LANES), F32)],
        compiler_params=_params(1),
    )(dfc, hbuf, bf_vec)


def _fox_masks(i, j, tq):
    row = i * tq + _iota((tq, tq), 0)
    col = j * tq + _iota((tq, tq), 1)
    lane = _iota((1, LANES), 1)
    return col <= row, (lane < HEAD_DIM, lane >= HEAD_DIM)


def _hosting(body, n_in, n_out, n_scratch, comm, nsteps):
    if comm is None:
        return body
    na, no = len(comm.arrays), len(comm.out_shapes)

    def hosted(*refs):
        o0 = n_in + na
        s0 = o0 + n_out + no
        cargs = (refs[n_in:o0], refs[o0 + n_out:s0]) + tuple(refs[s0 + n_scratch:])
        a, b = pl.program_id(0), pl.program_id(1)

        @pl.when((a == 0) & (b == 0))
        def _():
            comm.start(*cargs)

        body(*refs[:n_in], *refs[o0:o0 + n_out], *refs[s0:s0 + n_scratch])

        @pl.when((a == nsteps - 1) & (b == nsteps - 1))
        def _():
            comm.finish(*cargs)

    return hosted


def _hosted_call(body, comm, nsteps, *, name, in_specs, out_specs, out_shape, scratch_shapes, args):
    n_out = len(out_shape)
    if comm is not None:
        cin, cout, sems = comm.specs()
        body = _hosting(body, len(in_specs), n_out, len(scratch_shapes), comm, nsteps)
        in_specs, out_specs = in_specs + cin, out_specs + cout
        out_shape, scratch_shapes, args = out_shape + comm.out_shapes, scratch_shapes + sems, args + list(comm.arrays)
    outs = pl.pallas_call(body, name=name, grid=(nsteps, nsteps), in_specs=in_specs, out_specs=out_specs,
                          out_shape=out_shape, scratch_shapes=scratch_shapes, compiler_params=_params(2))(*args)
    return outs[:n_out], outs[n_out:]


def _fox_fwd(hbuf, eq, ek, *, comm=None, name):
    t = hbuf.shape[0]
    w = ATT_WIDTH
    tq = _pick(t, (FOX_T, 256, 128))
    nq = t // tq
    cq, ck, cv = COL_Q // w, COL_K // w, COL_V // w

    def body(q_ref, k_ref, v_ref, eq_ref, ek_ref, o_ref, lse_ref, m_s, l_s, acc_s):
        i = pl.program_id(0)
        j = pl.program_id(1)

        @pl.when(j == 0)
        def _():
            m_s[...] = jnp.full_like(m_s, NEG)
            l_s[...] = jnp.zeros_like(l_s)
            acc_s[...] = jnp.zeros_like(acc_s)

        def step(diagonal):
            _, hms = _fox_masks(i, j, tq)
            keys_first = (j * tq + _iota((tq, tq), 0)) <= (i * tq + _iota((tq, tq), 1))
            half = _iota((LANES, 1), 0)
            hrows = (half < HEAD_DIM, half >= HEAD_DIM)
            m_all = m_s[...]
            l_all = l_s[...]
            acc_old = [acc_s[LANES * pr:LANES * (pr + 1), :] for pr in range(2)]
            m_out, l_out, acc_out = [], [], []
            for pr in range(2):
                sl = slice(LANES * pr, LANES * (pr + 1))
                qp = q_ref[:, sl]
                kp = k_ref[:, sl]
                vt = v_ref[:, sl].T.astype(BF16)
                acc = acc_old[pr]
                for hh in range(2):
                    h = 2 * pr + hh
                    hsl = slice(LANES * h, LANES * (h + 1))
                    qm = jnp.where(hms[hh], (qp * (HEAD_DIM ** -0.5)).astype(BF16), eq_ref[:, hsl])
                    km = jnp.where(hms[hh], kp.astype(BF16), ek_ref[:, hsl])
                    st = _dot(km, qm, 1, 1)
                    if diagonal:
                        st = jnp.where(keys_first, st, NEG)
                    m_old = m_all[h:h + 1, :]
                    m_new = jnp.maximum(m_old, jnp.max(st, axis=0, keepdims=True))
                    alpha = jnp.exp(m_old - m_new)
                    pt = jnp.exp(st - m_new)
                    l_out.append(alpha * l_all[h:h + 1, :] + jnp.sum(pt, axis=0, keepdims=True))
                    m_out.append(m_new)
                    pv = _dot(vt, pt.astype(BF16), 1, 0)
                    acc = jnp.where(hrows[hh], alpha * acc_old[pr] + pv, acc)
                acc_out.append(acc)
            for h in range(ATT_HEADS):
                m_s[h:h + 1, :] = m_out[h]
                l_s[h:h + 1, :] = l_out[h]
            for pr in range(2):
                acc_s[LANES * pr:LANES * (pr + 1), :] = acc_out[pr]

        @pl.when(j < i)
        def _():
            step(False)

        @pl.when(j == i)
        def _():
            step(True)
            half = _iota((LANES, 1), 0)
            l_all = l_s[...]
            for pr in range(2):
                acc = acc_s[LANES * pr:LANES * (pr + 1), :]
                o_t = jnp.where(half < HEAD_DIM, acc / l_all[2 * pr:2 * pr + 1, :], acc / l_all[2 * pr + 1:2 * pr + 2, :])
                o_ref[:, LANES * pr:LANES * (pr + 1)] = o_t.T
            lse = m_s[...] + jnp.log(l_s[...])
            lse_ref[...] = jnp.where(_iota(lse.shape, 0) < ATT_HEADS, lse, 0.0)

    return _hosted_call(
        body, comm, nq, name=name,
        in_specs=[pl.BlockSpec((tq, w), lambda i, j: (i, cq)),
                  pl.BlockSpec((tq, w), lambda i, j: (jnp.minimum(j, i), ck)),
                  pl.BlockSpec((tq, w), lambda i, j: (jnp.minimum(j, i), cv)),
                  pl.BlockSpec((tq, ATT_HEADS * LANES), lambda i, j: (i, 0)),
                  pl.BlockSpec((tq, ATT_HEADS * LANES), lambda i, j: (jnp.minimum(j, i), 0))],
        out_specs=[pl.BlockSpec((tq, w), lambda i, j: (i, 0)),
                   pl.BlockSpec((SUBLANES, tq), lambda i, j: (0, i))],
        out_shape=[jax.ShapeDtypeStruct((t, w), F32), jax.ShapeDtypeStruct((SUBLANES, t), F32)],
        scratch_shapes=[pltpu.VMEM((SUBLANES, tq), F32), pltpu.VMEM((SUBLANES, tq), F32),
                        pltpu.VMEM((w, tq), F32)],
        args=[hbuf, hbuf, hbuf, eq, ek])


def _fox_delta(dymix, o, *, name):
    t, w = o.shape
    tm = _pick(t, (512, 256, 128))
    cdo = ATT_WIDTH // w

    def body(do_ref, o_ref, d_ref):
        d_ref[...] = _head_reduce(do_ref[...] * o_ref[...], 0, ATT_HEADS)

    return pl.pallas_call(
        body, name=name, grid=(t // tm,),
        in_specs=[pl.BlockSpec((tm, w), lambda i: (i, cdo)), pl.BlockSpec((tm, w), lambda i: (i, 0))],
        out_specs=pl.BlockSpec((tm, LANES), lambda i: (i, 0)),
        out_shape=jax.ShapeDtypeStruct((t, LANES), F32),
        compiler_params=_params(1),
    )(dymix, o)


def _fox_bwd(hbuf, eq, ek, dymix, lse_rows, delta_rows, *, comm=None, name):
    t = hbuf.shape[0]
    w = ATT_WIDTH
    tq = _pick(t, (FOX_T, 256, 128))
    nq = t // tq
    cq, ck, cv = COL_Q // w, COL_K // w, COL_V // w
    cdo = ATT_WIDTH // w

    def body(q_ref, k_ref, v_ref, eq_ref, ek_ref, do_ref, lse_ref, dl_ref, dk_ref, dv_ref, dfk_ref, dqt_ref, dfq_ref,
             dk_s, dv_s, dfk_s):
        j = pl.program_id(0)
        i = pl.program_id(1)

        @pl.when((i == 0) & (j == 0))
        def _():
            dqt_ref[...] = jnp.zeros_like(dqt_ref)
            dfq_ref[...] = jnp.zeros_like(dfq_ref)

        @pl.when(i == 0)
        def _():
            dk_s[...] = jnp.zeros_like(dk_s)
            dv_s[...] = jnp.zeros_like(dv_s)
            dfk_s[...] = jnp.zeros_like(dfk_s)

        def step(diagonal):
            _, hms = _fox_masks(i, j, tq)
            keys_first = (j * tq + _iota((tq, tq), 0)) <= (i * tq + _iota((tq, tq), 1))
            half = _iota((LANES, 1), 0)
            hrows = (half < HEAD_DIM, half >= HEAD_DIM)
            lse_all = lse_ref[...]
            dl_all = dl_ref[...]
            dvs, dks, dfks, dqts, dfqs = [], [], [], [], []
            for pr in range(2):
                sl = slice(LANES * pr, LANES * (pr + 1))
                qp = q_ref[:, sl]
                kp = k_ref[:, sl]
                kt = kp.T.astype(BF16)
                vpb = v_ref[:, sl].astype(BF16)
                dop = do_ref[:, sl]
                dv_p = jnp.zeros((tq, LANES), F32)
                dk_p = jnp.zeros((tq, LANES), F32)
                dqt_p = jnp.zeros((LANES, tq), F32)
                for hh in range(2):
                    h = 2 * pr + hh
                    hsl = slice(LANES * h, LANES * (h + 1))
                    qm = jnp.where(hms[hh], (qp * (HEAD_DIM ** -0.5)).astype(BF16), eq_ref[:, hsl])
                    km = jnp.where(hms[hh], kp.astype(BF16), ek_ref[:, hsl])
                    st = _dot(km, qm, 1, 1)
                    if diagonal:
                        st = jnp.where(keys_first, st, NEG)
                    pt = jnp.exp(st - lse_all[h:h + 1, :])
                    domb = jnp.where(hms[hh], dop, 0.0).astype(BF16)
                    dv_p = dv_p + _dot(pt.astype(BF16), domb, 1, 0)
                    dpt = _dot(vpb, domb, 1, 1)
                    dst = pt * (dpt - dl_all[h:h + 1, :])
                    dstb = dst.astype(BF16)
                    dk_p = dk_p + jnp.where(hms[hh], _dot(dstb, qm, 1, 0), 0.0)
                    dqt_p = dqt_p + _dot(jnp.where(hrows[hh], kt, 0.0), dstb, 1, 0)
                    part = dst[:, 0:LANES]
                    for c in range(1, tq // LANES):
                        part = part + dst[:, LANES * c:LANES * (c + 1)]
                    dfks.append(part)
                    dfqs.append(jnp.sum(dst, axis=0, keepdims=True))
                dvs.append(dv_p)
                dks.append(dk_p)
                dqts.append(dqt_p)
            dv_s[...] += jnp.concatenate(dvs, axis=1)
            dk_s[...] += jnp.concatenate(dks, axis=1)
            for h in range(ATT_HEADS):
                dfk_s[h] += dfks[h]
            cols = pl.ds(pl.multiple_of(i * tq, tq), tq)
            dqt_ref[:, cols] += jnp.concatenate(dqts, axis=0) * (HEAD_DIM ** -0.5)
            dfq_ref[:, cols] += jnp.concatenate(dfqs + [jnp.zeros((SUBLANES - ATT_HEADS, tq), F32)], axis=0)

        @pl.when(i > j)
        def _():
            step(False)

        @pl.when(i == j)
        def _():
            step(True)

        @pl.when(i == nq - 1)
        def _():
            dk_ref[...] = dk_s[...].astype(BF16)
            dv_ref[...] = dv_s[...].astype(BF16)
            lane = _iota((tq, LANES), 1)
            out = jnp.zeros((tq, LANES), F32)
            for h in range(ATT_HEADS):
                out = jnp.where(lane == h, jnp.sum(dfk_s[h], axis=1, keepdims=True), out)
            dfk_ref[...] = out

    qi = lambda j, i: jnp.maximum(i, j)
    rows = pl.BlockSpec((SUBLANES, tq), lambda j, i: (0, qi(j, i)))
    return _hosted_call(
        body, comm, nq, name=name,
        in_specs=[pl.BlockSpec((tq, w), lambda j, i: (qi(j, i), cq)),
                  pl.BlockSpec((tq, w), lambda j, i: (j, ck)),
                  pl.BlockSpec((tq, w), lambda j, i: (j, cv)),
                  pl.BlockSpec((tq, ATT_HEADS * LANES), lambda j, i: (qi(j, i), 0)),
                  pl.BlockSpec((tq, ATT_HEADS * LANES), lambda j, i: (j, 0)),
                  pl.BlockSpec((tq, w), lambda j, i: (qi(j, i), cdo)),
                  rows, rows],
        out_specs=[pl.BlockSpec((tq, w), lambda j, i: (j, 0)), pl.BlockSpec((tq, w), lambda j, i: (j, 0)),
                   pl.BlockSpec((tq, LANES), lambda j, i: (j, 0)),
                   pl.BlockSpec((w, t), lambda j, i: (0, 0)), pl.BlockSpec((SUBLANES, t), lambda j, i: (0, 0))],
        out_shape=[jax.ShapeDtypeStruct((t, w), BF16), jax.ShapeDtypeStruct((t, w), BF16),
                   jax.ShapeDtypeStruct((t, LANES), F32),
                   jax.ShapeDtypeStruct((w, t), F32), jax.ShapeDtypeStruct((SUBLANES, t), F32)],
        scratch_shapes=[pltpu.VMEM((tq, w), F32), pltpu.VMEM((tq, w), F32),
                        pltpu.VMEM((ATT_HEADS, tq, LANES), F32)],
        args=[hbuf, hbuf, hbuf, eq, ek, dymix, lse_rows, delta_rows])


GROUP_W = SSD_WIDTH // SSD_GROUPS
HEADS_PER_GROUP = SSD_HEADS // SSD_GROUPS


def _ssd_chunk_common(xr, prev8, sm, cw, cb, dtb, avec):
    c = _conv_taps(xr, prev8, cw, cb)
    sig = _sigmoid(c)
    xa = c * sig
    dt = _softplus(sm + dtb)
    a = dt * avec
    acum = _cumsum_rows(a)
    return c, sig, xa, dt, acum


def _ssd_decays(acum, g):
    n = acum.shape[0]
    atot = acum[n - 1:n, :]
    lane0 = LANE_DT + HEADS_PER_GROUP * g
    e = _head_expand(jnp.exp(acum), lane0, HEADS_PER_GROUP, GROUP_W)
    dec = _head_expand(jnp.exp(atot - acum), lane0, HEADS_PER_GROUP, GROUP_W)
    etot = _head_expand(jnp.exp(atot), lane0, HEADS_PER_GROUP, GROUP_W)
    return e, dec, etot


def _ssd_ldec(acum, acum_t, lane, tril):
    return jnp.exp(jnp.where(tril, _col(acum, lane) - _row(acum_t, lane), NEG))


def _ssd_fwd(hbuf, conv_w, conv_b, dtb_vec, a_vec, d_exp, norm_g, *, name):
    t = hbuf.shape[0]
    L = SSD_CHUNK
    nc = t // L
    hb = L // SUBLANES
    cs = COL_SMALL // LANES
    cz = COL_Z // SSD_WIDTH

    def body(x_ref, xp_ref, z_ref, s_ref, cw_ref, cb_ref, dtb_ref, av_ref, dx_ref, ng_ref,
             yc_ref, y_ref, st_ref, state):
        i = pl.program_id(0)

        @pl.when(i == 0)
        def _():
            state[...] = jnp.zeros_like(state)

        prev = jnp.where(i == 0, 0.0, xp_ref[...])
        _, _, xa, dt, acum = _ssd_chunk_common(x_ref[...], prev, s_ref[...], cw_ref[...], cb_ref[...],
                                               dtb_ref[...], av_ref[...])
        acum_t = acum.T
        xs = xa[:, :SSD_WIDTH]
        xdt = xs * _head_expand(dt, LANE_DT, SSD_HEADS, SSD_WIDTH)
        tril = _iota((L, L), 0) >= _iota((L, L), 1)
        lane = _iota((1, LANES), 1)
        ys = []
        for g in range(SSD_GROUPS):
            bg = xa[:, SSD_WIDTH + SSD_STATE * g:SSD_WIDTH + SSD_STATE * (g + 1)].astype(BF16)
            cg = xa[:, SSD_WIDTH + SSD_STATE * (SSD_GROUPS + g):SSD_WIDTH + SSD_STATE * (SSD_GROUPS + g + 1)].astype(BF16)
            gm = _dot(cg, bg, 1, 1)
            e, dec, etot = _ssd_decays(acum, g)
            s_in = state[g]
            st_ref[0, g] = s_in
            xg = xdt[:, GROUP_W * g:GROUP_W * (g + 1)]
            y_off = e * _dot(cg, s_in.astype(BF16), 1, 0)
            state[g] = etot * s_in + _dot(bg, (dec * xg).astype(BF16), 0, 0)
            for pr in range(2):
                xp = xg[:, LANES * pr:LANES * (pr + 1)].astype(BF16)
                outs = []
                for hh in range(2):
                    h = HEADS_PER_GROUP * g + 2 * pr + hh
                    m = gm * _ssd_ldec(acum, acum_t, LANE_DT + h, tril)
                    outs.append(_dot(m.astype(BF16), xp, 1, 0))
                ys.append(jnp.where(lane < HEAD_DIM, outs[0], outs[1]) + y_off[:, LANES * pr:LANES * (pr + 1)])
        y = jnp.concatenate(ys, axis=1)
        y_ref[...] = y
        yd = y + dx_ref[...] * xs
        zz = z_ref[...]
        y2 = yd * zz * _sigmoid(zz)
        ng = ng_ref[...]
        outs = []
        for g in range(SSD_GROUPS):
            yg = y2[:, GROUP_W * g:GROUP_W * (g + 1)]
            rs = lax.rsqrt(jnp.mean(yg * yg, axis=1, keepdims=True) + RMS_EPS)
            outs.append(yg * rs * ng[:, GROUP_W * g:GROUP_W * (g + 1)])
        yc_ref[...] = jnp.concatenate(outs, axis=1)

    cdim = SSD_CONV_DIM
    vecc = pl.BlockSpec((1, cdim), lambda i: (0, 0))
    vecl = pl.BlockSpec((1, LANES), lambda i: (0, 0))
    vecw = pl.BlockSpec((1, SSD_WIDTH), lambda i: (0, 0))
    roww = pl.BlockSpec((L, SSD_WIDTH), lambda i: (i, 0))
    return pl.pallas_call(
        body, name=name, grid=(nc,),
        in_specs=[pl.BlockSpec((L, cdim), lambda i: (i, 0)),
                  pl.BlockSpec((SUBLANES, cdim), lambda i: (jnp.maximum(i * hb - 1, 0), 0)),
                  pl.BlockSpec((L, SSD_WIDTH), lambda i: (i, cz)),
                  pl.BlockSpec((L, LANES), lambda i: (i, cs)),
                  pl.BlockSpec((CONV_K, cdim), lambda i: (0, 0)), vecc, vecl, vecl, vecw, vecw],
        out_specs=[roww, roww, pl.BlockSpec((1, SSD_GROUPS, SSD_STATE, GROUP_W), lambda i: (i, 0, 0, 0))],
        out_shape=[jax.ShapeDtypeStruct((t, SSD_WIDTH), F32), jax.ShapeDtypeStruct((t, SSD_WIDTH), F32),
                   jax.ShapeDtypeStruct((nc, SSD_GROUPS, SSD_STATE, GROUP_W), F32)],
        scratch_shapes=[pltpu.VMEM((SSD_GROUPS, SSD_STATE, GROUP_W), F32)],
        compiler_params=_params(1),
    )(hbuf, hbuf, hbuf, hbuf, conv_w, conv_b, dtb_vec, a_vec, d_exp, norm_g)


def _ssd_bwd(dymix, hbuf, y_ssd, states, conv_w, conv_b, dtb_vec, a_vec, d_exp, norm_g, *, name):
    t = hbuf.shape[0]
    L = SSD_CHUNK
    nc = t // L
    hb = L // SUBLANES
    cs = COL_SMALL // LANES
    cz = COL_Z // SSD_WIDTH
    cdy = (LRU_WIDTH + ATT_WIDTH) // SSD_WIDTH
    cdim = SSD_CONV_DIM

    def body(dyc_ref, x_ref, xp_ref, z_ref, s_ref, y_ref, st_ref, cw_ref, cb_ref, dtb_ref, av_ref, dx_ref, ng_ref,
             dxr_ref, dz_ref, dsm_ref, dng_ref, dd_ref, da_ref, ddtb_ref, dcw_ref, dcb_ref,
             dstate, dnext):
        i = pl.program_id(0)
        ic = nc - 1 - i

        @pl.when(i == 0)
        def _():
            dstate[...] = jnp.zeros_like(dstate)
            dnext[...] = jnp.zeros_like(dnext)
            for ref in (dng_ref, dd_ref, da_ref, ddtb_ref, dcw_ref, dcb_ref):
                ref[...] = jnp.zeros_like(ref)

        xr = x_ref[...]
        sm = s_ref[...]
        prev = jnp.where(ic == 0, 0.0, xp_ref[...])
        avec = av_ref[...]
        c, sig, xa, dt, acum = _ssd_chunk_common(xr, prev, sm, cw_ref[...], cb_ref[...], dtb_ref[...], avec)
        acum_t = acum.T
        xs = xa[:, :SSD_WIDTH]
        dtx = _head_expand(dt, LANE_DT, SSD_HEADS, SSD_WIDTH)
        xdt = xs * dtx
        tril = _iota((L, L), 0) >= _iota((L, L), 1)
        lane = _iota((1, LANES), 1)
        hmasks = (lane < HEAD_DIM, lane >= HEAD_DIM)

        y = y_ref[...]
        dexp = dx_ref[...]
        yd = y + dexp * xs
        zz = z_ref[...]
        sz = _sigmoid(zz)
        siluz = zz * sz
        y2 = yd * siluz
        ng = ng_ref[...]
        dyc = dyc_ref[...]
        dy2s, dngs = [], []
        for g in range(SSD_GROUPS):
            sl = slice(GROUP_W * g, GROUP_W * (g + 1))
            yg = y2[:, sl]
            rs = lax.rsqrt(jnp.mean(yg * yg, axis=1, keepdims=True) + RMS_EPS)
            wv = dyc[:, sl] * ng[:, sl]
            dngs.append(jnp.sum(dyc[:, sl] * yg * rs, axis=0, keepdims=True))
            dy2s.append(rs * wv - yg * (rs * rs * rs) * jnp.mean(wv * yg, axis=1, keepdims=True))
        dy2 = jnp.concatenate(dy2s, axis=1)
        dng_ref[...] += jnp.concatenate(dngs, axis=1)
        dz_ref[...] = (dy2 * yd * (sz * (1.0 + zz * (1.0 - sz)))).astype(BF16)
        dy = dy2 * siluz
        dd_ref[...] += jnp.sum(dy * xs, axis=0, keepdims=True)

        dxs, dbs, dcs = [], [], []
        datot = jnp.zeros((1, LANES), F32)
        lanes = _iota((L, LANES), 1)
        dacum = jnp.zeros((L, LANES), F32)
        for g in range(SSD_GROUPS):
            sl = slice(GROUP_W * g, GROUP_W * (g + 1))
            bg = xa[:, SSD_WIDTH + SSD_STATE * g:SSD_WIDTH + SSD_STATE * (g + 1)].astype(BF16)
            cg = xa[:, SSD_WIDTH + SSD_STATE * (SSD_GROUPS + g):SSD_WIDTH + SSD_STATE * (SSD_GROUPS + g + 1)].astype(BF16)
            gm = _dot(cg, bg, 1, 1)
            e, dec, etot = _ssd_decays(acum, g)
            s_in = st_ref[0, g]
            ds_out = dstate[g]
            dyg = dy[:, sl]
            xg = xdt[:, sl]
            edy = (e * dyg).astype(BF16)
            dstate[g] = etot * ds_out + _dot(cg, edy, 0, 0)
            dx_state = dec * _dot(bg, ds_out.astype(BF16), 1, 0)
            y_off = e * _dot(cg, s_in.astype(BF16), 1, 0)
            dacum = dacum + _head_reduce_group(dyg * y_off - xg * dx_state, g)
            dc_off = _dot(edy, s_in.astype(BF16), 1, 1)
            db_state = _dot((dec * xg).astype(BF16), ds_out.astype(BF16), 1, 1)
            dgsum = jnp.zeros((L, L), F32)
            dx_pairs = []
            for pr in range(2):
                psl = slice(LANES * pr, LANES * (pr + 1))
                xp = xg[:, psl]
                dyp = dyg[:, psl]
                dx_pair = jnp.zeros((L, LANES), F32)
                for hh in range(2):
                    h = HEADS_PER_GROUP * g + 2 * pr + hh
                    ldec = _ssd_ldec(acum, acum_t, LANE_DT + h, tril)
                    dym = jnp.where(hmasks[hh], dyp, 0.0).astype(BF16)
                    xm = jnp.where(hmasks[hh], xp, 0.0).astype(BF16)
                    dx_pair = dx_pair + _dot((gm * ldec).astype(BF16), dym, 0, 0)
                    dml = _dot(dym, xm, 1, 1) * ldec
                    dgsum = dgsum + dml
                    qm = dml * gm
                    seg = jnp.sum(qm, axis=1, keepdims=True) - jnp.sum(qm.T, axis=1, keepdims=True)
                    dacum = dacum + jnp.where(lanes == LANE_DT + h, seg, 0.0)
                dx_pairs.append(dx_pair)
            dgb = dgsum.astype(BF16)
            dcs.append(_dot(dgb, bg, 1, 0) + dc_off)
            dbs.append(_dot(dgb, cg, 0, 0) + db_state)
            dxg = jnp.concatenate(dx_pairs, axis=1) + dx_state
            dxs.append(dxg)
            v = jnp.sum(dx_state * xg, axis=0, keepdims=True) + etot * jnp.sum(ds_out * s_in, axis=0, keepdims=True)
            datot = datot + _head_reduce_row(v, LANE_DT + HEADS_PER_GROUP * g, HEADS_PER_GROUP)
        dx = jnp.concatenate(dxs, axis=1)
        dacum = dacum + jnp.where(_iota((L, LANES), 0) == L - 1, datot, 0.0)
        da = _cumsum_rows(dacum, reverse=True)
        ddt = da * avec + _head_reduce(dx * xs, LANE_DT, SSD_HEADS)
        da_ref[...] += jnp.sum(da * dt, axis=0, keepdims=True)
        ddt_raw = ddt * _sigmoid(sm + dtb_ref[...])
        ddt_raw = jnp.where((lanes >= LANE_DT) & (lanes < LANE_DT + SSD_HEADS), ddt_raw, 0.0)
        dsm_ref[...] = ddt_raw
        ddtb_ref[...] += jnp.sum(ddt_raw, axis=0, keepdims=True)
        dxs_total = dx * dtx + dexp * dy
        dxa = jnp.concatenate([dxs_total] + dbs + dcs, axis=1)
        dc = dxa * (sig * (1.0 + c * (1.0 - sig)))
        dxr, dws = _conv_taps_bwd(dc, dnext[...], cw_ref[...], xr)
        dxr_ref[...] = dxr.astype(BF16)
        dcw_ref[...] += dws
        dcb_ref[...] += jnp.sum(dc, axis=0, keepdims=True)
        dnext[...] = dc[:SUBLANES]

    rev = lambda i: nc - 1 - i
    vecc = pl.BlockSpec((1, cdim), lambda i: (0, 0))
    vecl = pl.BlockSpec((1, LANES), lambda i: (0, 0))
    vecw = pl.BlockSpec((1, SSD_WIDTH), lambda i: (0, 0))
    cwspec = pl.BlockSpec((CONV_K, cdim), lambda i: (0, 0))
    roww = pl.BlockSpec((L, SSD_WIDTH), lambda i: (rev(i), 0))
    return pl.pallas_call(
        body, name=name, grid=(nc,),
        in_specs=[pl.BlockSpec((L, SSD_WIDTH), lambda i: (rev(i), cdy)),
                  pl.BlockSpec((L, cdim), lambda i: (rev(i), 0)),
                  pl.BlockSpec((SUBLANES, cdim), lambda i: (jnp.maximum(rev(i) * hb - 1, 0), 0)),
                  pl.BlockSpec((L, SSD_WIDTH), lambda i: (rev(i), cz)),
                  pl.BlockSpec((L, LANES), lambda i: (rev(i), cs)),
                  roww,
                  pl.BlockSpec((1, SSD_GROUPS, SSD_STATE, GROUP_W), lambda i: (rev(i), 0, 0, 0)),
                  cwspec, vecc, vecl, vecl, vecw, vecw],
        out_specs=[pl.BlockSpec((L, cdim), lambda i: (rev(i), 0)), roww,
                   pl.BlockSpec((L, LANES), lambda i: (rev(i), 0)),
                   vecw, vecw, vecl, vecl, cwspec, vecc],
        out_shape=[jax.ShapeDtypeStruct((t, cdim), BF16), jax.ShapeDtypeStruct((t, SSD_WIDTH), BF16),
                   jax.ShapeDtypeStruct((t, LANES), F32),
                   jax.ShapeDtypeStruct((1, SSD_WIDTH), F32), jax.ShapeDtypeStruct((1, SSD_WIDTH), F32),
                   jax.ShapeDtypeStruct((1, LANES), F32), jax.ShapeDtypeStruct((1, LANES), F32),
                   jax.ShapeDtypeStruct((CONV_K, cdim), F32), jax.ShapeDtypeStruct((1, cdim), F32)],
        scratch_shapes=[pltpu.VMEM((SSD_GROUPS, SSD_STATE, GROUP_W), F32), pltpu.VMEM((SUBLANES, cdim), F32)],
        compiler_params=_params(1),
    )(dymix, hbuf, hbuf, hbuf, hbuf, y_ssd, states, conv_w, conv_b, dtb_vec, a_vec, d_exp, norm_g)


def _head_reduce_group(x, g):
    return _head_reduce(x, LANE_DT + HEADS_PER_GROUP * g, HEADS_PER_GROUP)


def _head_reduce_row(v, lane0, nheads):
    colhead = _iota(v.shape, 1) // HEAD_DIM
    lane = _iota((1, LANES), 1)
    out = jnp.zeros((1, LANES), F32)
    for h in range(nheads):
        s = jnp.sum(jnp.where(colhead == h, v, 0.0), axis=1, keepdims=True)
        out = jnp.where(lane == lane0 + h, s, out)
    return out


def _exchange(inps, axes, *, mode, local=True, name):
    n = 2 ** len(axes)
    counts, out_shapes = [], []
    for a in inps:
        if mode == "gather":
            cnt, rest = a.shape[0], a.shape[1:]
        elif mode == "gather_half":
            cnt, rest = a.shape[0] // 2, a.shape[1:]
        elif mode == "a2a":
            cnt, rest = a.shape[1], a.shape[2:]
        else:
            cnt, rest = a.shape[0], a.shape[2:]
        counts.append(cnt)
        out_shapes.append(jax.ShapeDtypeStruct((n, cnt) + tuple(rest), a.dtype))
    units = sum(counts)
    na = len(inps)

    def body(*refs):
        in_refs, out_refs = refs[:na], refs[na:2 * na]
        send_sems, recv_sems, local_sems = refs[2 * na:]
        pos = {ax: lax.axis_index(ax) for ax in MESH_AXES}

        def slot_of(coord):
            s = 0
            for ax in axes:
                s = s * 2 + coord[ax]
            return s

        def src(a, it, slot):
            if mode == "gather":
                return in_refs[a].at[it]
            if mode == "gather_half":
                return in_refs[a].at[pos["c"] * counts[a] + it]
            if mode == "a2a":
                return in_refs[a].at[slot, it]
            return in_refs[a].at[it, slot]

        me = slot_of(pos)
        copies = []
        unit = 0
        for a in range(na):
            for it in range(counts[a]):
                if local:
                    cp = pltpu.make_async_copy(src(a, it, me), out_refs[a].at[me, it], local_sems.at[unit])
                    cp.start()
                    copies.append(cp)
                for delta in range(1, n):
                    coord = dict(pos)
                    for b, ax in enumerate(reversed(axes)):
                        if (delta >> b) & 1:
                            coord[ax] = 1 - pos[ax]
                    k = unit * (n - 1) + delta - 1
                    cp = pltpu.make_async_remote_copy(
                        src_ref=src(a, it, slot_of(coord)), dst_ref=out_refs[a].at[me, it],
                        send_sem=send_sems.at[k], recv_sem=recv_sems.at[k],
                        device_id=(coord["x"], coord["y"], coord["c"]), device_id_type=pl.DeviceIdType.MESH)
                    cp.start()
                    copies.append(cp)
                unit += 1
        for cp in copies:
            cp.wait()

    any_spec = pl.BlockSpec(memory_space=pl.ANY)
    return pl.pallas_call(
        body, name=name,
        in_specs=[any_spec] * na, out_specs=[any_spec] * na, out_shape=out_shapes,
        scratch_shapes=[pltpu.SemaphoreType.DMA((units * (n - 1),)), pltpu.SemaphoreType.DMA((units * (n - 1),)),
                        pltpu.SemaphoreType.DMA((units,))],
    )(*inps)


class _Comm:
    def __init__(self, arrays, out_shapes, n_sems, start, finish):
        self.arrays, self.out_shapes, self.n_sems, self.start, self.finish = arrays, out_shapes, n_sems, start, finish

    def specs(self):
        any_spec = pl.BlockSpec(memory_space=pl.ANY)
        sems = [pltpu.SemaphoreType.DMA((self.n_sems,)), pltpu.SemaphoreType.DMA((self.n_sems,))]
        return [any_spec] * len(self.arrays), [any_spec] * len(self.out_shapes), sems


def _run_comm(comm, *, name):
    na, no = len(comm.arrays), len(comm.out_shapes)

    def body(*refs):
        args = (refs[:na], refs[na:na + no]) + tuple(refs[na + no:])
        comm.start(*args)
        comm.finish(*args)

    in_specs, out_specs, sems = comm.specs()
    return pl.pallas_call(body, name=name, in_specs=in_specs, out_specs=out_specs, out_shape=comm.out_shapes,
                          scratch_shapes=sems)(*comm.arrays)


def _chip_peer(x, y, d):
    px = 1 - x if d & 2 else x
    py = 1 - y if d & 1 else y
    return px, py, 2 * px + py


def _gather_layer_comm(srcs, li):
    counts = [s.shape[0] for s in srcs]
    units = [(a, it) for a in range(len(srcs)) for it in range(counts[a])]
    n_ici = 3 * len(units)
    out_shapes = [jax.ShapeDtypeStruct((N_CHIPS,) + s.shape, s.dtype) for s in srcs]

    def ici(ins, outs, ssem, rsem, u, d):
        x, y, c = (lax.axis_index(ax) for ax in MESH_AXES)
        a, it = units[u]
        px, py, _ = _chip_peer(x, y, d)
        k = 3 * u + d - 1
        return pltpu.make_async_remote_copy(
            src_ref=ins[a].at[it], dst_ref=outs[a].at[2 * x + y, it], send_sem=ssem.at[k], recv_sem=rsem.at[k],
            device_id=(px, py, c), device_id_type=pl.DeviceIdType.MESH)

    def arrived(ins, outs, ssem, rsem, u, d):
        x, y, c = (lax.axis_index(ax) for ax in MESH_AXES)
        a, it = units[u]
        _, _, pk = _chip_peer(x, y, d)
        k = 3 * u + d - 1
        return pltpu.make_async_remote_copy(
            src_ref=ins[a].at[it], dst_ref=outs[a].at[pk, it], send_sem=ssem.at[k], recv_sem=rsem.at[k],
            device_id=(x, y, c), device_id_type=pl.DeviceIdType.MESH)

    def forward(ins, outs, ssem, rsem, u, slot):
        x, y, c = (lax.axis_index(ax) for ax in MESH_AXES)
        a, it = units[u]
        pk = 2 * x + y if slot == 0 else _chip_peer(x, y, slot)[2]
        src = ins[a].at[it] if slot == 0 else outs[a].at[pk, it]
        k = n_ici + 4 * u + slot
        return pltpu.make_async_remote_copy(
            src_ref=src, dst_ref=outs[a].at[pk, it], send_sem=ssem.at[k], recv_sem=rsem.at[k],
            device_id=(x, y, 1 - c), device_id_type=pl.DeviceIdType.MESH)

    def start(ins, outs, ssem, rsem):
        @pl.when(lax.axis_index("c") == li)
        def _():
            for u in range(len(units)):
                for d in range(1, N_CHIPS):
                    ici(ins, outs, ssem, rsem, u, d).start()

    def finish(ins, outs, ssem, rsem):
        c = lax.axis_index("c")

        @pl.when(c == li)
        def _():
            for u in range(len(units)):
                forward(ins, outs, ssem, rsem, u, 0).start()
                for d in range(1, N_CHIPS):
                    arrived(ins, outs, ssem, rsem, u, d).wait_recv()
                    forward(ins, outs, ssem, rsem, u, d).start()
            for u in range(len(units)):
                for d in range(1, N_CHIPS):
                    ici(ins, outs, ssem, rsem, u, d).wait_send()
                for slot in range(N_CHIPS):
                    forward(ins, outs, ssem, rsem, u, slot).wait_send()

        @pl.when(c != li)
        def _():
            for u in range(len(units)):
                for slot in range(N_CHIPS):
                    forward(ins, outs, ssem, rsem, u, slot).wait_recv()

    return _Comm(srcs, out_shapes, n_ici + 4 * len(units), start, finish)


def _reduce_chips_comm(sums, li):
    counts = [s.shape[0] for s in sums]
    units = [(a, it) for a in range(len(sums)) for it in range(counts[a])]
    out_shapes = [jax.ShapeDtypeStruct((N_CHIPS, s.shape[0]) + s.shape[2:], s.dtype) for s in sums]

    def copy(ins, outs, ssem, rsem, u, d):
        x, y, c = (lax.axis_index(ax) for ax in MESH_AXES)
        a, it = units[u]
        px, py, pk = _chip_peer(x, y, d)
        k = 3 * u + d - 1
        return pltpu.make_async_remote_copy(
            src_ref=ins[a].at[it, pk], dst_ref=outs[a].at[2 * x + y, it], send_sem=ssem.at[k], recv_sem=rsem.at[k],
            device_id=(px, py, c), device_id_type=pl.DeviceIdType.MESH)

    def start(ins, outs, ssem, rsem):
        @pl.when(lax.axis_index("c") == li)
        def _():
            for u in range(len(units)):
                for d in range(1, N_CHIPS):
                    copy(ins, outs, ssem, rsem, u, d).start()

    def finish(ins, outs, ssem, rsem):
        @pl.when(lax.axis_index("c") == li)
        def _():
            for u in range(len(units)):
                for d in range(1, N_CHIPS):
                    copy(ins, outs, ssem, rsem, u, d).wait()

    return _Comm(sums, out_shapes, 3 * len(units), start, finish)


def _sum_slots(buf, out_dtype, *, name):
    n, rows, cols = buf.shape
    tm = _pick(rows, (512, 256, 128, 8))
    if rows % tm:
        tm = rows

    def body(b_ref, o_ref):
        acc = b_ref[0].astype(F32)
        for s in range(1, n):
            acc = acc + b_ref[s].astype(F32)
        o_ref[...] = acc.astype(out_dtype)

    return pl.pallas_call(
        body, name=name, grid=(pl.cdiv(rows, tm),),
        in_specs=[pl.BlockSpec((n, tm, cols), lambda i: (0, i, 0))],
        out_specs=pl.BlockSpec((tm, cols), lambda i: (i, 0)),
        out_shape=jax.ShapeDtypeStruct((rows, cols), out_dtype),
        compiler_params=_params(1),
    )(buf)


def _adamw(w, g, m, v, *, name):
    shape = w.shape
    cols = shape[-1]
    rows = w.size // cols
    w2, g2, m2, v2 = (a.reshape(rows, cols) for a in (w, g, m, v))
    tm = _pick(rows, (256, 128, 64, 32, 16, 8))
    if rows % tm:
        tm = rows
    bc1 = 1.0 - ADAM_B1 ** ADAM_STEP
    bc2 = 1.0 - ADAM_B2 ** ADAM_STEP

    def body(w_ref, g_ref, m_ref, v_ref, d_ref, nm_ref, nv_ref):
        gg = g_ref[...]
        mm = ADAM_B1 * m_ref[...] + (1.0 - ADAM_B1) * gg
        vv = ADAM_B2 * v_ref[...] + (1.0 - ADAM_B2) * (gg * gg)
        m_hat = mm / bc1
        v_hat = vv / bc2
        d_ref[...] = -ADAM_LR * (m_hat / (jnp.sqrt(v_hat) + ADAM_EPS) + ADAM_WD * w_ref[...])
        nm_ref[...] = mm
        nv_ref[...] = vv

    spec = pl.BlockSpec((tm, cols), lambda i: (i, 0))
    o = jax.ShapeDtypeStruct((rows, cols), F32)
    outs = pl.pallas_call(
        body, name=name, grid=(rows // tm,), in_specs=[spec] * 4, out_specs=[spec] * 3, out_shape=[o] * 3,
        compiler_params=_params(1),
    )(w2, g2, m2, v2)
    return tuple(a.reshape(shape) for a in outs)


def _layer_fwd(li, x, xb, pb, W, comm=None):
    nm = lambda s: f"l{li}_{s}"
    sv = {"x_in_b": xb}
    g1, u1, a1 = _mm_swiglu(xb, W["ffn1_wg"], W["ffn1_wu"], name=nm("ffn1_up"))
    x1, x1b, xh1, rs1 = _mm_ln(a1, W["ffn1_wd"], x, W["ln1_g"], W["ln1_b"], rscale=ALPHA, mscale=0.5, name=nm("ffn1_down_ln"))
    hbuf = _mm(x1b, W["w_in_p"], name=nm("in_proj"))
    ya, lu, lr, lig, la, lh = _lru_fwd(hbuf, W["lru_conv_w"], W["lru_conv_b"], W["lru_wa_bd"], W["lru_ba"],
                                       W["lru_wx_bd"], W["lru_bx"], W["lru_lambda"], name=nm("lru_fwd"))
    eq, ek = _fox_prep(hbuf, W["fox_bf_vec"], name=nm("fox_prep"))
    (yb, lse_rows), comm_out = _fox_fwd(hbuf, eq, ek, comm=comm, name=nm("fox_fwd"))
    yc, yssd, states = _ssd_fwd(hbuf, W["ssd_conv_w"], W["ssd_conv_b"], W["ssd_dtb_vec"], W["ssd_a_vec"],
                                W["ssd_d_exp"], W["ssd_norm_g"], name=nm("ssd_fwd"))
    ymix = jnp.concatenate([ya, yb, yc], axis=1).astype(BF16)
    x2, x2b, xh2, rs2 = _mm_ln(ymix, W["w_out"], x1, W["ln2_g"], W["ln2_b"], rscale=ALPHA, mscale=1.0, name=nm("out_proj_ln"))
    g2, u2, a2 = _mm_swiglu(x2b, W["ffn2_wg"], W["ffn2_wu"], name=nm("ffn2_up"))
    x3, x3b, xh3, rs3 = _mm_ln(a2, W["ffn2_wd"], x2, W["ln3_g"], W["ln3_b"], rscale=ALPHA, mscale=0.5, name=nm("ffn2_down_ln"))
    x4, x4b, sg, e = _mm_pe(x3, x3b, pb, W["pe_gate_w"], W["pe_gate_b"], W["pe_proj"], name=nm("ple"))
    sv.update(g1=g1, u1=u1, a1=a1, x1b=x1b, xh1=xh1, rs1=rs1, hbuf=hbuf, lu=lu, lr=lr, lig=lig, la=la, lh=lh,
              eq=eq, ek=ek, lse_rows=lse_rows, yb=yb, yssd=yssd, states=states, ymix=ymix, x2b=x2b, xh2=xh2, rs2=rs2,
              g2=g2, u2=u2, a2=a2, x3b=x3b, xh3=xh3, rs3=rs3, sg=sg, e=e, pb=pb)
    return x4, x4b, sv, comm_out


def _layer_bwd(li, dx4, sv, W, comm=None):
    nm = lambda s: f"l{li}_{s}"
    G = {}
    dgp, de, dbg = _pe_bwd_elem(dx4, sv["sg"], sv["e"], name=nm("ple_bwd"))
    G["pe_gate_b"] = dbg
    G["pe_gate_w"] = _mm(sv["x3b"], dgp, ta=True, out_dtype=BF16, name=nm("d_pe_gate_w"))
    G["pe_proj"] = _mm(sv["pb"], de, ta=True, out_dtype=BF16, chip_cols=True, name=nm("d_pe_proj"))
    dr3, dr3b, G["ln3_g"], G["ln3_b"] = _bwd_proj([(dgp, W["pe_gate_w"])], dx4, rscale=1.0,
                                                  ln=(sv["xh3"], sv["rs3"], W["ln3_g"]), name=nm("ln3_bwd"))
    G["ffn2_wd"] = _mm(sv["a2"], dr3b, ta=True, scale=0.5, out_dtype=BF16, name=nm("d_ffn2_wd"))
    dg2, du2 = _mm_swiglu_bwd(dr3b, W["ffn2_wd"], sv["g2"], sv["u2"], scale=0.5, name=nm("ffn2_act_bwd"))
    G["ffn2_wg"] = _mm(sv["x2b"], dg2, ta=True, out_dtype=BF16, chip_cols=True, name=nm("d_ffn2_wg"))
    G["ffn2_wu"] = _mm(sv["x2b"], du2, ta=True, out_dtype=BF16, chip_cols=True, name=nm("d_ffn2_wu"))
    dr2, dr2b, G["ln2_g"], G["ln2_b"] = _bwd_proj([(dg2, W["ffn2_wg"]), (du2, W["ffn2_wu"])], dr3, rscale=ALPHA,
                                                  ln=(sv["xh2"], sv["rs2"], W["ln2_g"]), name=nm("ln2_bwd"))
    G["w_out"] = _mm(sv["ymix"], dr2b, ta=True, out_dtype=BF16, name=nm("d_w_out"))
    dymix = _mm(dr2b, W["w_out"], tb=True, name=nm("d_ymix"))
    hbuf = sv["hbuf"]
    (dur, dgr, G["lru_conv_w"], G["lru_conv_b"], G["lru_wa_bd"], G["lru_ba"], G["lru_wx_bd"], G["lru_bx"],
     G["lru_lambda"]) = _lru_bwd(dymix, hbuf, sv["lu"], sv["lr"], sv["lig"], sv["la"], sv["lh"],
                                 W["lru_conv_w"], W["lru_wa_bd"], W["lru_wx_bd"], W["lru_lambda"], name=nm("lru_bwd"))
    delta = _fox_delta(dymix, sv["yb"], name=nm("fox_delta"))
    delta_rows = jnp.pad(delta[:, :ATT_HEADS].T, ((0, SUBLANES - ATT_HEADS), (0, 0)))
    (dk, dv, dfk, dqt, dfq), comm_out = _fox_bwd(hbuf, sv["eq"], sv["ek"], dymix, sv["lse_rows"], delta_rows,
                                                 comm=comm, name=nm("fox_bwd"))
    dq = dqt.T
    dfc = jnp.pad(dfq[:ATT_HEADS].T, ((0, 0), (0, LANES - ATT_HEADS))) - dfk
    dsm_f, G["fox_bf_vec"] = _fox_post(dfc, hbuf, W["fox_bf_vec"], name=nm("fox_post"))
    (dxr, dz, dsm_dt, G["ssd_norm_g"], G["ssd_d_exp"], G["ssd_a_vec"], G["ssd_dtb_vec"], G["ssd_conv_w"],
     G["ssd_conv_b"]) = _ssd_bwd(dymix, hbuf, sv["yssd"], sv["states"], W["ssd_conv_w"], W["ssd_conv_b"],
                                 W["ssd_dtb_vec"], W["ssd_a_vec"], W["ssd_d_exp"], W["ssd_norm_g"], name=nm("ssd_bwd"))
    t = dx4.shape[0]
    dh = jnp.concatenate([dxr.astype(BF16), dz.astype(BF16), dur.astype(BF16), dgr.astype(BF16), dq.astype(BF16),
                          dk.astype(BF16), dv.astype(BF16), (dsm_f + dsm_dt).astype(BF16),
                          jnp.zeros((t, H_WIDTH - COL_SMALL - LANES), BF16)], axis=1)
    G["w_in_p"] = _mm(sv["x1b"], dh, ta=True, name=nm("d_w_in"))
    dr1, dr1b, G["ln1_g"], G["ln1_b"] = _bwd_proj([(dh, W["w_in_p"])], dr2, rscale=ALPHA,
                                                  ln=(sv["xh1"], sv["rs1"], W["ln1_g"]), name=nm("ln1_bwd"))
    G["ffn1_wd"] = _mm(sv["a1"], dr1b, ta=True, scale=0.5, out_dtype=BF16, name=nm("d_ffn1_wd"))
    dg1, du1 = _mm_swiglu_bwd(dr1b, W["ffn1_wd"], sv["g1"], sv["u1"], scale=0.5, name=nm("ffn1_act_bwd"))
    G["ffn1_wg"] = _mm(sv["x_in_b"], dg1, ta=True, out_dtype=BF16, chip_cols=True, name=nm("d_ffn1_wg"))
    G["ffn1_wu"] = _mm(sv["x_in_b"], du1, ta=True, out_dtype=BF16, chip_cols=True, name=nm("d_ffn1_wu"))
    (dx_in,) = _bwd_proj([(dg1, W["ffn1_wg"]), (du1, W["ffn1_wu"])], dr1, rscale=ALPHA, ln=None, name=nm("x_in_bwd"))
    return dx_in, G, comm_out


def _block_diag(w):
    n, b, _ = w.shape
    eye = jnp.eye(n, dtype=w.dtype)
    return (eye[:, None, :, None] * w[:, :, None, :]).reshape(n * b, n * b)


def _block_diag_extract(m):
    n, b = LRU_HEADS, HEAD_DIM
    return jnp.stack([m[b * i:b * (i + 1), b * i:b * (i + 1)] for i in range(n)])


def _lane_vec(v, lane0):
    return jnp.pad(v.astype(F32), (lane0, LANES - lane0 - v.shape[0])).reshape(1, LANES)


def _w_in_permute(w):
    d = w.shape[0]
    z = lambda n: jnp.zeros((d, n), w.dtype)
    return jnp.concatenate([w[:, 1796:2820], w[:, 1284:1796], w[:, 0:512], w[:, 512:1280],
                            w[:, 1280:1284], w[:, 2820:2828], z(LANES - 12), z(H_WIDTH - COL_SMALL - LANES)], axis=1)


def _w_in_unpermute(wp):
    return jnp.concatenate([wp[:, COL_U:COL_Q], wp[:, COL_Q:COL_SMALL], wp[:, COL_SMALL:COL_SMALL + 4],
                            wp[:, COL_Z:COL_U], wp[:, COL_XBC:COL_Z], wp[:, COL_SMALL + 4:COL_SMALL + 12]], axis=1)


def _layer_weights(li, chipw, small):
    g = lambda n: small[n][li]
    W = {n: g(n) for n in ("ln1_g", "ln1_b", "ln2_g", "ln2_b", "ln3_g", "ln3_b", "pe_gate_b", "lru_conv_w",
                           "ssd_conv_w")}
    for n in ("ffn1_wg", "ffn1_wu", "ffn2_wg", "ffn2_wu"):
        W[n] = chipw[n]
    for n in ("ffn1_wd", "ffn2_wd", "w_out", "pe_gate_w"):
        W[n] = chipw[n].reshape(-1, D_MODEL)
    W["pe_proj"] = jnp.moveaxis(chipw["pe_proj"], 0, 1).reshape(PLE_DIM, D_MODEL)
    w_in = jnp.moveaxis(chipw["w_in"][:, :, :IN_WIDTH // N_CHIPS], 0, 1).reshape(D_MODEL, IN_WIDTH)
    W["w_in_p"] = _w_in_permute(w_in)
    for n in ("lru_conv_b", "lru_ba", "lru_bx", "lru_lambda", "ssd_conv_b", "ssd_norm_g"):
        W[n] = g(n).reshape(1, -1)
    W["lru_wa_bd"] = _block_diag(g("lru_wa")).astype(BF16)
    W["lru_wx_bd"] = _block_diag(g("lru_wx")).astype(BF16)
    W["fox_bf_vec"] = _lane_vec(g("fox_bf"), LANE_F)
    W["ssd_dtb_vec"] = _lane_vec(g("ssd_dt_bias"), LANE_DT)
    W["ssd_a_vec"] = _lane_vec(-jnp.exp(g("ssd_a_log")), LANE_DT)
    W["ssd_d_exp"] = jnp.repeat(g("ssd_d"), HEAD_DIM).reshape(1, SSD_WIDTH)
    return W


def _layer_big_grads_by_chip(G):
    out = {n: G[n] for n in ("ffn1_wg", "ffn1_wu", "ffn2_wg", "ffn2_wu", "pe_proj")}
    for n in ("ffn1_wd", "ffn2_wd", "w_out", "pe_gate_w"):
        out[n] = G[n].reshape(N_CHIPS, -1, D_MODEL)
    share = IN_WIDTH // N_CHIPS
    d_w_in = jnp.moveaxis(_w_in_unpermute(G["w_in_p"]).reshape(D_MODEL, N_CHIPS, share), 1, 0)
    out["w_in"] = jnp.pad(d_w_in.astype(BF16), ((0, 0), (0, 0), (0, SHARE - share)))
    return out


def _layer_small_grads(G, W):
    out = {n: G[n] for n in ("lru_conv_w", "ssd_conv_w")}
    for n in ("ln1_g", "ln1_b", "ln2_g", "ln2_b", "ln3_g", "ln3_b", "pe_gate_b", "lru_conv_b", "lru_ba", "lru_bx",
              "lru_lambda", "ssd_conv_b", "ssd_norm_g"):
        out[n] = G[n].reshape(-1)
    out["lru_wa"] = _block_diag_extract(G["lru_wa_bd"])
    out["lru_wx"] = _block_diag_extract(G["lru_wx_bd"])
    out["fox_bf"] = G["fox_bf_vec"][0, LANE_F:LANE_F + ATT_HEADS]
    out["ssd_dt_bias"] = G["ssd_dtb_vec"][0, LANE_DT:LANE_DT + SSD_HEADS]
    out["ssd_a_log"] = G["ssd_a_vec"][0, LANE_DT:LANE_DT + SSD_HEADS] * W["ssd_a_vec"][0, LANE_DT:LANE_DT + SSD_HEADS]
    out["ssd_d"] = G["ssd_d_exp"].reshape(SSD_HEADS, HEAD_DIM).sum(axis=1)
    return out


WEIGHTS = ['ln1_g', 'ln1_b', 'ffn1_wg', 'ffn1_wu', 'ffn1_wd', 'w_in', 'lru_conv_w', 'lru_conv_b', 'lru_wa', 'lru_ba',
           'lru_wx', 'lru_bx', 'lru_lambda', 'fox_bf', 'ssd_conv_w', 'ssd_conv_b', 'ssd_dt_bias', 'ssd_a_log', 'ssd_d',
           'ssd_norm_g', 'w_out', 'ln2_g', 'ln2_b', 'ffn2_wg', 'ffn2_wu', 'ffn2_wd', 'ln3_g', 'ln3_b', 'pe_proj',
           'pe_gate_w', 'pe_gate_b']
CLASSES = (("ffn1_wg", "ffn1_wu", "ffn2_wg", "ffn2_wu", "w_in"),
           ("ffn1_wd", "ffn2_wd"),
           ("w_out", "pe_gate_w"),
           ("pe_proj",))
CLASS_PAD_AXIS = (1, 0, None, None)
BIG = {n: ci for ci, names in enumerate(CLASSES) for n in names}
SMALL_SHARDED = {'lru_conv_w': 2, 'ssd_conv_w': 2}
PACK_COLS = 1024


def _unshard(seg, axis):
    moved = jnp.moveaxis(seg, 0, axis)
    shp = list(moved.shape)
    shp[axis:axis + 2] = [shp[axis] * shp[axis + 1]]
    return moved.reshape(shp)


def _pad_axis(a, axis, size):
    if axis is None or a.shape[axis] == size:
        return a
    pads = [(0, 0)] * a.ndim
    pads[axis] = (0, size - a.shape[axis])
    return jnp.pad(a, pads)


def _pack(arrs, dtype, cols):
    flat = jnp.concatenate([a.astype(dtype).reshape(-1) for a in arrs])
    pad = (-flat.shape[0]) % cols
    if pad:
        flat = jnp.concatenate([flat, jnp.zeros((pad,), dtype)])
    return flat.reshape(-1, cols)


def _unpack(flat, shapes):
    out, off = [], 0
    for s in shapes:
        n = math.prod(s)
        out.append(flat[off:off + n].reshape(s))
        off += n
    return out


def kernel(x, p, ln1_g, ln1_b, ffn1_wg, ffn1_wu, ffn1_wd, w_in, lru_conv_w, lru_conv_b, lru_wa, lru_ba, lru_wx, lru_bx, lru_lambda, fox_bf, ssd_conv_w, ssd_conv_b, ssd_dt_bias, ssd_a_log, ssd_d, ssd_norm_g, w_out, ln2_g, ln2_b, ffn2_wg, ffn2_wu, ffn2_wd, ln3_g, ln3_b, pe_proj, pe_gate_w, pe_gate_b, loss_target, m_ln1_g, m_ln1_b, m_ffn1_wg, m_ffn1_wu, m_ffn1_wd, m_w_in, m_lru_conv_w, m_lru_conv_b, m_lru_wa, m_lru_ba, m_lru_wx, m_lru_bx, m_lru_lambda, m_fox_bf, m_ssd_conv_w, m_ssd_conv_b, m_ssd_dt_bias, m_ssd_a_log, m_ssd_d, m_ssd_norm_g, m_w_out, m_ln2_g, m_ln2_b, m_ffn2_wg, m_ffn2_wu, m_ffn2_wd, m_ln3_g, m_ln3_b, m_pe_proj, m_pe_gate_w, m_pe_gate_b, v_ln1_g, v_ln1_b, v_ffn1_wg, v_ffn1_wu, v_ffn1_wd, v_w_in, v_lru_conv_w, v_lru_conv_b, v_lru_wa, v_lru_ba, v_lru_wx, v_lru_bx, v_lru_lambda, v_fox_bf, v_ssd_conv_w, v_ssd_conv_b, v_ssd_dt_bias, v_ssd_a_log, v_ssd_d, v_ssd_norm_g, v_w_out, v_ln2_g, v_ln2_b, v_ffn2_wg, v_ffn2_wu, v_ffn2_wd, v_ln3_g, v_ln3_b, v_pe_proj, v_pe_gate_w, v_pe_gate_b):
    args = locals()
    w_loc = {n: args[n] for n in WEIGHTS}
    m_loc = {n: args["m_" + n] for n in WEIGHTS}
    v_loc = {n: args["v_" + n] for n in WEIGHTS}
    chip = 2 * lax.axis_index("x") + lax.axis_index("y")
    core = lax.axis_index("c")
    big = list(BIG)
    small_sh = list(SMALL_SHARDED)
    small_rep = [n for n in WEIGHTS if n not in BIG and n not in SMALL_SHARDED]

    def layer_srcs(li):
        return [jnp.stack([_pad_axis(w_loc[n][li].astype(BF16), pad, SHARE) for n in names])
                for names, pad in zip(CLASSES, CLASS_PAD_AXIS)]

    def chip_weights(gathered, srcs):
        out = {}
        for names, g, s in zip(CLASSES, gathered, srcs):
            g = lax.dynamic_update_index_in_dim(g, s, chip, 0)
            for j, n in enumerate(names):
                out[n] = g[:, j]
        return out

    def pair_sums(g_big, li):
        gcls = [jnp.stack([g_big[n] for n in names]) for names in CLASSES]
        flat = [g.reshape((-1,) + g.shape[2:]) for g in gcls]
        pair = _exchange(flat, ("c",), mode="gather", local=False, name=f"reduce_cores_l{li}")
        pair = [lax.dynamic_update_index_in_dim(pr, f, core, 0) for pr, f in zip(pair, flat)]
        return [_sum_slots(pr.reshape(2, -1, pr.shape[-1]), BF16, name=f"reduce_cores_sum_l{li}_{ci}").reshape(g.shape)
                for ci, (pr, g) in enumerate(zip(pair, gcls))]

    def finish_reduce(quad, sums, li):
        quad = [lax.dynamic_update_index_in_dim(q, lax.dynamic_index_in_dim(s, chip, 1, keepdims=False), chip, 0)
                for q, s in zip(quad, sums)]
        red = [_sum_slots(q.reshape(N_CHIPS, -1, q.shape[-1]), F32,
                          name=f"reduce_chips_sum_l{li}_{ci}").reshape(q.shape[1:]) for ci, q in enumerate(quad)]
        shared = _exchange(red, ("c",), mode="gather", local=False, name=f"reduce_share_l{li}")
        return [lax.dynamic_update_index_in_dim(sh, r, core, 0)[li] for sh, r in zip(shared, red)]

    srcs = [layer_srcs(li) for li in range(DEPTH)]
    gathered0 = _run_comm(_gather_layer_comm(srcs[0], 0), name="gather_w_l0")
    small = {n: w_loc[n] for n in small_rep}
    spack = _pack([w_loc[n] for n in small_sh], F32, LANES)
    (sg,) = _exchange([spack[None]], ("x", "y"), mode="gather", name="gather_conv_w")
    for n, seg in zip(small_sh, _unpack_rows(sg.reshape(N_CHIPS, -1), [w_loc[n].shape for n in small_sh])):
        small[n] = _unshard(seg, SMALL_SHARDED[n])

    Ws = [_layer_weights(0, chip_weights(gathered0, srcs[0]), small), None]
    xs = x[0]
    xs, xb, sv0, gathered1 = _layer_fwd(0, xs, xs.astype(BF16), p[0, 0].astype(BF16), Ws[0],
                                        comm=_gather_layer_comm(srcs[1], 1))
    Ws[1] = _layer_weights(1, chip_weights(gathered1, srcs[1]), small)
    xs, _, sv1, _ = _layer_fwd(1, xs, xb, p[1, 0].astype(BF16), Ws[1])
    dx, loss = _loss_kernel(xs, loss_target[0], name="loss")
    loss = lax.psum(loss[0, 0], MESH_AXES)
    dx, G1, _ = _layer_bwd(1, dx, sv1, Ws[1])
    sums1 = pair_sums(_layer_big_grads_by_chip(G1), 1)
    grad_x, G0, quad1 = _layer_bwd(0, dx, sv0, Ws[0], comm=_reduce_chips_comm(sums1, 1))

    red = [None, finish_reduce(quad1, sums1, 1)]
    sums0 = pair_sums(_layer_big_grads_by_chip(G0), 0)
    red[0] = finish_reduce(_run_comm(_reduce_chips_comm(sums0, 0), name="reduce_chips_l0"), sums0, 0)
    g_red = {}
    for ci, names in enumerate(CLASSES):
        for j, n in enumerate(names):
            g = jnp.stack([red[li][ci][j] for li in range(DEPTH)])
            g_red[n] = g[tuple(slice(0, s) for s in w_loc[n].shape)]
    small_l = [_layer_small_grads(G0, Ws[0]), _layer_small_grads(G1, Ws[1])]
    g_small = {n: jnp.stack([small_l[li][n] for li in range(DEPTH)]) for n in small_l[0]}
    small_all = small_rep + small_sh
    sgp = _pack([g_small[n] for n in small_all], F32, PACK_COLS)
    (sall,) = _exchange([sgp[None]], MESH_AXES, mode="gather", name="reduce_small")
    sred = _sum_slots(sall.reshape((2 ** len(MESH_AXES),) + sgp.shape), F32, name="reduce_small_sum").reshape(-1)
    for n, g in zip(small_all, _unpack(sred, [g_small[n].shape for n in small_all])):
        if n in SMALL_SHARDED:
            width = w_loc[n].shape[-1]
            g = lax.dynamic_slice_in_dim(g, chip * width, width, axis=SMALL_SHARDED[n])
        g_red[n] = g

    delta, new_m, new_v = {}, {}, {}
    for n in big:
        delta[n], new_m[n], new_v[n] = _adamw(w_loc[n], g_red[n], m_loc[n], v_loc[n], name="adamw_" + n)
    shapes = [w_loc[n].shape for n in small_all]
    packs = [_pack([d[n] for n in small_all], F32, LANES) for d in (w_loc, g_red, m_loc, v_loc)]
    outs = _adamw(*packs, name="adamw_small")
    for d, o in zip((delta, new_m, new_v), outs):
        for n, a in zip(small_all, _unpack(o.reshape(-1), shapes)):
            d[n] = a
    return (loss, grad_x[None], *[g_red[n] for n in WEIGHTS], *[delta[n] for n in WEIGHTS],
            *[new_m[n] for n in WEIGHTS], *[new_v[n] for n in WEIGHTS])


def _unpack_rows(gathered, shapes):
    out, off = [], 0
    for s in shapes:
        n = math.prod(s)
        out.append(gathered[:, off:off + n].reshape((N_CHIPS,) + tuple(s)))
        off += n
    return out
```

```python
import functools
import math

import jax
import jax.numpy as jnp
from jax import lax
from jax.experimental import pallas as pl
from jax.experimental.pallas import tpu as pltpu

F32 = jnp.float32
BF16 = jnp.bfloat16

D_MODEL = 1024
DEPTH = 2
PLE_DIM = 256
HEAD_DIM = 64
LRU_WIDTH = 256
LRU_HEADS = 4
LRU_C = 8.0
CONV_K = 4
ATT_WIDTH = 256
ATT_HEADS = 4
SSD_WIDTH = 512
SSD_HEADS = 8
SSD_GROUPS = 2
SSD_STATE = 128
SSD_CHUNK = 128
SSD_CONV_DIM = 1024
FFN_DIM = 2816
ALPHA = (2.0 * DEPTH) ** 0.25
LN_EPS = 1e-5
RMS_EPS = 1e-5
IN_WIDTH = 2828
ADAM_LR = 0.001
ADAM_B1 = 0.9
ADAM_B2 = 0.999
ADAM_EPS = 1e-08
ADAM_WD = 0.01
ADAM_STEP = 10

H_WIDTH = 3072
COL_XBC, COL_Z, COL_U, COL_G, COL_Q, COL_K, COL_V, COL_SMALL = 0, 1024, 1536, 1792, 2048, 2304, 2560, 2816
LANE_F = 0
LANE_DT = 4
LANES = 128
SUBLANES = 8
NEG = -1e30

VMEM_LIMIT = 48 * 1024 * 1024

N_CHIPS = 4
MESH_AXES = ("x", "y", "c")
SHARE = 768


def _params(n):
    return pltpu.CompilerParams(dimension_semantics=("arbitrary",) * n, vmem_limit_bytes=VMEM_LIMIT)


def _pick(n, cands):
    for c in cands:
        if n % c == 0:
            return c
    return n


def _iota(shape, dim):
    return lax.broadcasted_iota(jnp.int32, shape, dim)


def _shift_down(x, s, prev8):
    if s == 0:
        return x
    r = pltpu.roll(x, s, 0)
    pr = pltpu.roll(prev8, s, 0)
    head = jnp.where(_iota(pr.shape, 0) < s, pr, r[:SUBLANES])
    return jnp.concatenate([head, r[SUBLANES:]], axis=0)


def _shift_up(x, s, next8):
    if s == 0:
        return x
    n = x.shape[0]
    r = pltpu.roll(x, n - s, 0)
    nr = pltpu.roll(next8, SUBLANES - s, 0)
    tail = jnp.where(_iota(nr.shape, 0) >= SUBLANES - s, nr, r[n - SUBLANES:])
    return jnp.concatenate([r[:n - SUBLANES], tail], axis=0)


def _scan_fwd(a, b):
    n = a.shape[0]
    row = _iota(a.shape, 0)
    d = 1
    while d < n:
        keep = row >= d
        a_s = jnp.where(keep, pltpu.roll(a, d, 0), 1.0)
        b_s = jnp.where(keep, pltpu.roll(b, d, 0), 0.0)
        b = a * b_s + b
        a = a * a_s
        d *= 2
    return a, b


def _scan_bwd(a, b):
    n = a.shape[0]
    row = _iota(a.shape, 0)
    d = 1
    while d < n:
        keep = row < n - d
        a_s = jnp.where(keep, pltpu.roll(a, n - d, 0), 1.0)
        b_s = jnp.where(keep, pltpu.roll(b, n - d, 0), 0.0)
        b = a * b_s + b
        a = a * a_s
        d *= 2
    return a, b


def _cumsum_rows(x, reverse=False):
    n = x.shape[0]
    row = _iota(x.shape, 0)
    d = 1
    while d < n:
        if reverse:
            x = x + jnp.where(row < n - d, pltpu.roll(x, n - d, 0), 0.0)
        else:
            x = x + jnp.where(row >= d, pltpu.roll(x, d, 0), 0.0)
        d *= 2
    return x


def _col(x, lane):
    return jnp.sum(jnp.where(_iota(x.shape, 1) == lane, x, 0.0), axis=1, keepdims=True)


def _row(x, r):
    return jnp.sum(jnp.where(_iota(x.shape, 0) == r, x, 0.0), axis=0, keepdims=True)


def _sigmoid(x):
    return jax.nn.sigmoid(x)


def _softplus(x):
    return jnp.maximum(x, 0.0) + jnp.log(1.0 + jnp.exp(-jnp.abs(x)))


def _gelu_and_grad(x):
    c0 = math.sqrt(2.0 / math.pi)
    inner = c0 * (x + 0.044715 * x * x * x)
    t = jnp.tanh(inner)
    g = 0.5 * x * (1.0 + t)
    dg = 0.5 * (1.0 + t) + 0.5 * x * (1.0 - t * t) * c0 * (1.0 + 3.0 * 0.044715 * x * x)
    return g, dg


def _dot(a, b, ca, cb):
    return lax.dot_general(a, b, (((ca,), (cb,)), ((), ())), preferred_element_type=F32)


def _conv_taps(xr, prev8, w, bias):
    y = bias + w[CONV_K - 1:CONV_K, :] * xr
    for j in range(CONV_K - 1):
        y = y + w[j:j + 1, :] * _shift_down(xr, CONV_K - 1 - j, prev8)
    return y


def _conv_taps_bwd(dy, next8, w, xr):
    dx = None
    dws = []
    for j in range(CONV_K):
        sh = _shift_up(dy, CONV_K - 1 - j, next8)
        term = w[j:j + 1, :] * sh
        dx = term if dx is None else dx + term
        dws.append(jnp.sum(sh * xr, axis=0, keepdims=True))
    return dx, jnp.concatenate(dws, axis=0)


def _head_expand(v, lane0, nheads, width):
    rows = v.shape[0]
    colhead = _iota((rows, width), 1) // HEAD_DIM
    out = jnp.zeros((rows, width), F32)
    for h in range(nheads):
        out = jnp.where(colhead == h, _col(v, lane0 + h), out)
    return out


def _head_reduce(x, lane0, nheads):
    rows = x.shape[0]
    colhead = _iota(x.shape, 1) // HEAD_DIM
    lane = _iota((rows, LANES), 1)
    out = jnp.zeros((rows, LANES), F32)
    for h in range(nheads):
        s = jnp.sum(jnp.where(colhead == h, x, 0.0), axis=1, keepdims=True)
        out = jnp.where(lane == lane0 + h, s, out)
    return out


def _mm(a, b, *, ta=False, tb=False, scale=1.0, out_dtype=F32, chip_cols=False, name):
    if ta:
        kk, m = a.shape
    else:
        m, kk = a.shape
    n = b.shape[0] if tb else b.shape[1]
    tm = _pick(m, (1024, 512, 256, 128))
    tn = _pick(n // N_CHIPS, (768, 256, 128)) if chip_cols else _pick(n, (1024, 768, 512, 256, 128))
    tk = _pick(kk, (1024, 768, 512, 256, 128))
    nk = kk // tk
    dn_a = 0 if ta else 1
    dn_b = 1 if tb else 0
    if chip_cols:
        per = n // N_CHIPS // tn
        out_spec = pl.BlockSpec((None, tm, tn), lambda i, j, k: (j // per, i, j % per))
        out_shape = jax.ShapeDtypeStruct((N_CHIPS, m, n // N_CHIPS), out_dtype)
    else:
        out_spec = pl.BlockSpec((tm, tn), lambda i, j, k: (i, j))
        out_shape = jax.ShapeDtypeStruct((m, n), out_dtype)

    def body(a_ref, b_ref, o_ref, acc):
        k = pl.program_id(2)

        @pl.when(k == 0)
        def _():
            acc[...] = jnp.zeros_like(acc)

        acc[...] += _dot(a_ref[...].astype(BF16), b_ref[...].astype(BF16), dn_a, dn_b)

        @pl.when(k == nk - 1)
        def _():
            o_ref[...] = (acc[...] * scale).astype(out_dtype)

    a_spec = pl.BlockSpec((tk, tm), lambda i, j, k: (k, i)) if ta else pl.BlockSpec((tm, tk), lambda i, j, k: (i, k))
    b_spec = pl.BlockSpec((tn, tk), lambda i, j, k: (j, k)) if tb else pl.BlockSpec((tk, tn), lambda i, j, k: (k, j))
    return pl.pallas_call(
        body, name=name, grid=(m // tm, n // tn, nk),
        in_specs=[a_spec, b_spec],
        out_specs=out_spec, out_shape=out_shape,
        scratch_shapes=[pltpu.VMEM((tm, tn), F32)],
        compiler_params=_params(3),
    )(a, b)


def _mm_swiglu(xb, wg, wu, *, name):
    t, d = xb.shape
    share = wg.shape[2]
    n = N_CHIPS * share
    tm = _pick(t, (512, 256, 128))
    tn = _pick(share, (768, 256, 128))
    per = share // tn

    def body(x_ref, wg_ref, wu_ref, g_ref, u_ref, a_ref):
        x = x_ref[...]
        g = _dot(x, wg_ref[...], 1, 0)
        u = _dot(x, wu_ref[...], 1, 0)
        g_ref[...] = g.astype(BF16)
        u_ref[...] = u.astype(BF16)
        a_ref[...] = (g * _sigmoid(g) * u).astype(BF16)

    o = jax.ShapeDtypeStruct((t, n), BF16)
    ospec = pl.BlockSpec((tm, tn), lambda j, i: (i, j))
    return pl.pallas_call(
        body, name=name, grid=(n // tn, t // tm),
        in_specs=[pl.BlockSpec((tm, d), lambda j, i: (i, 0)),
                  pl.BlockSpec((None, d, tn), lambda j, i: (j // per, 0, j % per)),
                  pl.BlockSpec((None, d, tn), lambda j, i: (j // per, 0, j % per))],
        out_specs=[ospec, ospec, ospec], out_shape=[o, o, o],
        compiler_params=_params(2),
    )(xb, wg, wu)


def _mm_swiglu_bwd(dr, wd, g, u, *, scale, name):
    t, d = dr.shape
    n = wd.shape[0]
    tm = _pick(t, (512, 256, 128))
    tn = _pick(n, (768, 256, 128))

    def body(dr_ref, wd_ref, g_ref, u_ref, dg_ref, du_ref):
        da = _dot(dr_ref[...].astype(BF16), wd_ref[...], 1, 1) * scale
        gg = g_ref[...].astype(F32)
        uu = u_ref[...].astype(F32)
        sg = _sigmoid(gg)
        dg_ref[...] = (da * uu * (sg * (1.0 + gg * (1.0 - sg)))).astype(BF16)
        du_ref[...] = (da * gg * sg).astype(BF16)

    o = jax.ShapeDtypeStruct((t, n), BF16)
    ospec = pl.BlockSpec((tm, tn), lambda j, i: (i, j))
    return pl.pallas_call(
        body, name=name, grid=(n // tn, t // tm),
        in_specs=[pl.BlockSpec((tm, d), lambda j, i: (i, 0)),
                  pl.BlockSpec((tn, d), lambda j, i: (j, 0)),
                  ospec, ospec],
        out_specs=[ospec, ospec], out_shape=[o, o],
        compiler_params=_params(2),
    )(dr, wd, g, u)


def _mm_ln(a, w, resid, gain, bias, *, rscale, mscale, name):
    t, kk = a.shape
    d = w.shape[1]
    tm = _pick(t, (512, 256, 128))
    tk = kk
    nk = kk // tk

    def body(a_ref, w_ref, r_ref, g_ref, b_ref, y_ref, yb_ref, xh_ref, rs_ref, acc):
        k = pl.program_id(1)

        @pl.when(k == 0)
        def _():
            acc[...] = jnp.zeros_like(acc)

        acc[...] += _dot(a_ref[...].astype(BF16), w_ref[...], 1, 0)

        @pl.when(k == nk - 1)
        def _():
            r = rscale * r_ref[...] + mscale * acc[...]
            mu = jnp.mean(r, axis=1, keepdims=True)
            xc = r - mu
            var = jnp.mean(xc * xc, axis=1, keepdims=True)
            rstd = lax.rsqrt(var + LN_EPS)
            xh = xc * rstd
            y = xh * g_ref[...] + b_ref[...]
            y_ref[...] = y
            yb_ref[...] = y.astype(BF16)
            xh_ref[...] = xh
            rs_ref[...] = rstd

    row = pl.BlockSpec((tm, d), lambda i, k: (i, 0))
    vec = pl.BlockSpec((1, d), lambda i, k: (0, 0))
    return pl.pallas_call(
        body, name=name, grid=(t // tm, nk),
        in_specs=[pl.BlockSpec((tm, tk), lambda i, k: (i, k)),
                  pl.BlockSpec((tk, d), lambda i, k: (k, 0)), row, vec, vec],
        out_specs=[row, row, row, pl.BlockSpec((tm, 1), lambda i, k: (i, 0))],
        out_shape=[jax.ShapeDtypeStruct((t, d), F32), jax.ShapeDtypeStruct((t, d), BF16),
                   jax.ShapeDtypeStruct((t, d), F32), jax.ShapeDtypeStruct((t, 1), F32)],
        scratch_shapes=[pltpu.VMEM((tm, d), F32)],
        compiler_params=_params(2),
    )(a, w, resid, gain.reshape(1, d), bias.reshape(1, d))


def _bwd_proj(pairs, resid, *, rscale, ln, name):
    t, kk = pairs[0][0].shape
    d = pairs[0][1].shape[-2]
    tm = _pick(t, (512, 256, 128))
    tk = _pick(pairs[0][1].shape[-1], (1024, 768, 512, 256, 128))
    nk = kk // tk
    nt = t // tm
    npair = len(pairs)
    has_ln = ln is not None

    def body(*refs):
        ab = refs[:2 * npair]
        r_ref = refs[2 * npair]
        pos = 2 * npair + 1
        if has_ln:
            xh_ref, rs_ref, g_ref = refs[pos:pos + 3]
            pos += 3
            o_ref, ob_ref, dg_ref, db_ref = refs[pos:pos + 4]
            pos += 4
        else:
            o_ref = refs[pos]
            pos += 1
        acc = refs[pos]
        i = pl.program_id(0)
        k = pl.program_id(1)

        @pl.when(k == 0)
        def _():
            acc[...] = jnp.zeros_like(acc)

        for q in range(npair):
            acc[...] += _dot(ab[2 * q][...].astype(BF16), ab[2 * q + 1][...], 1, 1)

        @pl.when(k == nk - 1)
        def _():
            dy = rscale * r_ref[...] + acc[...]
            if not has_ln:
                o_ref[...] = dy
                return
            xh = xh_ref[...]
            w = dy * g_ref[...]
            m1 = jnp.mean(w, axis=1, keepdims=True)
            m2 = jnp.mean(w * xh, axis=1, keepdims=True)
            dr = rs_ref[...] * (w - m1 - xh * m2)
            o_ref[...] = dr
            ob_ref[...] = dr.astype(BF16)

            @pl.when(i == 0)
            def _():
                dg_ref[...] = jnp.zeros_like(dg_ref)
                db_ref[...] = jnp.zeros_like(db_ref)

            dg_ref[...] += jnp.sum(dy * xh, axis=0, keepdims=True)
            db_ref[...] += jnp.sum(dy, axis=0, keepdims=True)

    row = pl.BlockSpec((tm, d), lambda i, k: (i, 0))
    vec = pl.BlockSpec((1, d), lambda i, k: (0, 0))
    in_specs, args = [], []
    for a, b in pairs:
        if b.ndim == 3:
            per = b.shape[2] // tk
            b_spec = pl.BlockSpec((None, d, tk), lambda i, k, per=per: (k // per, 0, k % per))
        else:
            b_spec = pl.BlockSpec((d, tk), lambda i, k: (0, k))
        in_specs += [pl.BlockSpec((tm, tk), lambda i, k: (i, k)), b_spec]
        args += [a, b]
    in_specs.append(row)
    args.append(resid)
    out_specs = [row]
    out_shape = [jax.ShapeDtypeStruct((t, d), F32)]
    if has_ln:
        xh, rs, gain = ln
        in_specs += [row, pl.BlockSpec((tm, 1), lambda i, k: (i, 0)), vec]
        args += [xh, rs, gain.reshape(1, d)]
        out_specs += [row, vec, vec]
        out_shape += [jax.ShapeDtypeStruct((t, d), BF16)] + [jax.ShapeDtypeStruct((1, d), F32)] * 2
    return pl.pallas_call(
        body, name=name, grid=(nt, nk), in_specs=in_specs, out_specs=out_specs, out_shape=out_shape,
        scratch_shapes=[pltpu.VMEM((tm, d), F32)],
        compiler_params=_params(2),
    )(*args)


def _mm_pe(x3, x3b, pb, wgate, bgate, wproj, *, name):
    t, d = x3.shape
    pd = pb.shape[1]
    tm = _pick(t, (512, 256, 128))
    tn = _pick(d, (512, 256, 128))

    def body(x_ref, xb_ref, p_ref, wg_ref, bg_ref, wp_ref, y_ref, yb_ref, sg_ref, e_ref):
        sg = _sigmoid(_dot(xb_ref[...], wg_ref[...], 1, 0) + bg_ref[...])
        e = _dot(p_ref[...], wp_ref[...], 1, 0)
        y = x_ref[...] + sg * e
        y_ref[...] = y
        yb_ref[...] = y.astype(BF16)
        sg_ref[...] = sg.astype(BF16)
        e_ref[...] = e.astype(BF16)

    ospec = pl.BlockSpec((tm, tn), lambda i, j: (i, j))
    ob = jax.ShapeDtypeStruct((t, d), BF16)
    return pl.pallas_call(
        body, name=name, grid=(t // tm, d // tn),
        in_specs=[ospec, pl.BlockSpec((tm, d), lambda i, j: (i, 0)), pl.BlockSpec((tm, pd), lambda i, j: (i, 0)),
                  pl.BlockSpec((d, tn), lambda i, j: (0, j)), pl.BlockSpec((1, tn), lambda i, j: (0, j)),
                  pl.BlockSpec((pd, tn), lambda i, j: (0, j))],
        out_specs=[ospec, ospec, ospec, ospec],
        out_shape=[jax.ShapeDtypeStruct((t, d), F32), ob, ob, ob],
        compiler_params=_params(2),
    )(x3, x3b, pb, wgate, bgate.reshape(1, d), wproj)


def _pe_bwd_elem(dx4, sg, e, *, name):
    t, d = dx4.shape
    tm = _pick(t, (512, 256, 128))

    def body(dx_ref, sg_ref, e_ref, dgp_ref, de_ref, db_ref):
        dx = dx_ref[...]
        s = sg_ref[...].astype(F32)
        dgp = dx * e_ref[...].astype(F32) * s * (1.0 - s)
        dgp_ref[...] = dgp.astype(BF16)
        de_ref[...] = (dx * s).astype(BF16)

        @pl.when(pl.program_id(0) == 0)
        def _():
            db_ref[...] = jnp.zeros_like(db_ref)

        db_ref[...] += jnp.sum(dgp, axis=0, keepdims=True)

    row = pl.BlockSpec((tm, d), lambda i: (i, 0))
    ob = jax.ShapeDtypeStruct((t, d), BF16)
    return pl.pallas_call(
        body, name=name, grid=(t // tm,), in_specs=[row, row, row],
        out_specs=[row, row, pl.BlockSpec((1, d), lambda i: (0, 0))],
        out_shape=[ob, ob, jax.ShapeDtypeStruct((1, d), F32)],
        compiler_params=_params(1),
    )(dx4, sg, e)


def _loss_kernel(y, target, *, name):
    t, d = y.shape
    tm = _pick(t, (512, 256, 128))

    def body(y_ref, t_ref, dy_ref, l_ref):
        diff = y_ref[...] - t_ref[...]
        dy_ref[...] = diff * (1.0 / d)

        @pl.when(pl.program_id(0) == 0)
        def _():
            l_ref[...] = jnp.zeros_like(l_ref)

        part = jnp.sum(jnp.mean(diff * diff, axis=1, keepdims=True), axis=0, keepdims=True)
        l_ref[...] += 0.5 * part

    row = pl.BlockSpec((tm, d), lambda i: (i, 0))
    return pl.pallas_call(
        body, name=name, grid=(t // tm,), in_specs=[row, row],
        out_specs=[row, pl.BlockSpec((1, 1), lambda i: (0, 0))],
        out_shape=[jax.ShapeDtypeStruct((t, d), F32), jax.ShapeDtypeStruct((1, 1), F32)],
        compiler_params=_params(1),
    )(y, target)


LRU_TM = 256


def _lru_gate_terms(r, lam):
    sp = _softplus(-lam)
    la = -LRU_C * r * sp
    a = jnp.exp(la)
    em = jnp.tanh(la) * (jnp.exp(2.0 * la) + 1.0)
    s = jnp.sqrt(-em)
    return la, a, s, sp


def _lru_fwd(hbuf, conv_w, conv_b, wa, ba, wx, bx, lam, *, name):
    t = hbuf.shape[0]
    w = LRU_WIDTH
    tm = _pick(t, (LRU_TM, 128))
    cu, cg = COL_U // w, COL_G // w
    hb = tm // SUBLANES

    def body(u_ref, up_ref, g_ref, cw_ref, cb_ref, wa_ref, ba_ref, wx_ref, bx_ref, lam_ref,
             y_ref, u_out, r_out, i_out, a_out, h_out, carry):
        i = pl.program_id(0)

        @pl.when(i == 0)
        def _():
            carry[...] = jnp.zeros_like(carry)

        prev = jnp.where(i == 0, 0.0, up_ref[...])
        u = _conv_taps(u_ref[...], prev, cw_ref[...], cb_ref[...])
        ub = u.astype(BF16)
        r = _sigmoid(_dot(ub, wa_ref[...], 1, 0) + ba_ref[...])
        ig = _sigmoid(_dot(ub, wx_ref[...], 1, 0) + bx_ref[...])
        _, a, s, _ = _lru_gate_terms(r, lam_ref[...])
        b = s * (ig * u)
        acum, hs = _scan_fwd(a, b)
        h = hs + acum * carry[0:1, :]
        carry[...] = jnp.broadcast_to(h[tm - 1:tm, :], carry.shape)
        gl, _ = _gelu_and_grad(g_ref[...])
        y_ref[...] = h * gl
        u_out[...] = u
        r_out[...] = r
        i_out[...] = ig
        a_out[...] = a
        h_out[...] = h

    row = pl.BlockSpec((tm, w), lambda i: (i, 0))
    vec = pl.BlockSpec((1, w), lambda i: (0, 0))
    mat = pl.BlockSpec((w, w), lambda i: (0, 0))
    o = jax.ShapeDtypeStruct((t, w), F32)
    return pl.pallas_call(
        body, name=name, grid=(t // tm,),
        in_specs=[pl.BlockSpec((tm, w), lambda i: (i, cu)),
                  pl.BlockSpec((SUBLANES, w), lambda i: (jnp.maximum(i * hb - 1, 0), cu)),
                  pl.BlockSpec((tm, w), lambda i: (i, cg)),
                  pl.BlockSpec((CONV_K, w), lambda i: (0, 0)), vec, mat, vec, mat, vec, vec],
        out_specs=[row] * 6, out_shape=[o] * 6,
        scratch_shapes=[pltpu.VMEM((SUBLANES, w), F32)],
        compiler_params=_params(1),
    )(hbuf, hbuf, hbuf, conv_w, conv_b, wa, ba, wx, bx, lam)


def _lru_bwd(dymix, hbuf, u, r, ig, a, h, conv_w, wa, wx, lam, *, name):
    t = hbuf.shape[0]
    w = LRU_WIDTH
    tm = _pick(t, (LRU_TM, 128))
    nb = t // tm
    cu, cg = COL_U // w, COL_G // w
    hb = tm // SUBLANES
    last8 = t // SUBLANES - 1

    def body(dy_ref, ur_ref, g_ref, u_ref, r_ref, i_ref, a_ref, an_ref, h_ref, hp_ref,
             cw_ref, wa_ref, wx_ref, lam_ref,
             dur_ref, dgr_ref, dcw_ref, dcb_ref, dwa_ref, dba_ref, dwx_ref, dbx_ref, dlam_ref,
             lcarry, dnext):
        i = pl.program_id(0)
        ib = nb - 1 - i

        @pl.when(i == 0)
        def _():
            lcarry[...] = jnp.zeros_like(lcarry)
            dnext[...] = jnp.zeros_like(dnext)
            for ref in (dcw_ref, dcb_ref, dwa_ref, dba_ref, dwx_ref, dbx_ref, dlam_ref):
                ref[...] = jnp.zeros_like(ref)

        dy = dy_ref[...]
        hh = h_ref[...]
        av = a_ref[...]
        uu = u_ref[...]
        rr = r_ref[...]
        ii = i_ref[...]
        lam_v = lam_ref[...]
        gl, dgl = _gelu_and_grad(g_ref[...])
        dgr_ref[...] = (dy * hh * dgl).astype(BF16)
        dh_out = dy * gl
        a_next = _shift_up(av, 1, jnp.where(ib == nb - 1, 0.0, an_ref[...]))
        acum, ls = _scan_bwd(a_next, dh_out)
        lam_adj = ls + acum * lcarry[0:1, :]
        lcarry[...] = jnp.broadcast_to(lam_adj[0:1, :], lcarry.shape)
        h_prev = _shift_down(hh, 1, jnp.where(ib == 0, 0.0, hp_ref[...]))
        da = lam_adj * h_prev
        _, a2, s, sp = _lru_gate_terms(rr, lam_v)
        d_igu = lam_adj * s
        ds = lam_adj * ii * uu
        dla = da * a2 - ds * (a2 * a2) / s
        dr = dla * (-LRU_C * sp)
        dlam_ref[...] += jnp.sum(dla * (LRU_C * rr * _sigmoid(-lam_v)), axis=0, keepdims=True)
        dpre_r = dr * rr * (1.0 - rr)
        dpre_i = d_igu * uu * ii * (1.0 - ii)
        prb = dpre_r.astype(BF16)
        pib = dpre_i.astype(BF16)
        ub = uu.astype(BF16)
        du = d_igu * ii + _dot(prb, wa_ref[...], 1, 1) + _dot(pib, wx_ref[...], 1, 1)
        dwa_ref[...] += _dot(ub, prb, 0, 0)
        dwx_ref[...] += _dot(ub, pib, 0, 0)
        dba_ref[...] += jnp.sum(dpre_r, axis=0, keepdims=True)
        dbx_ref[...] += jnp.sum(dpre_i, axis=0, keepdims=True)
        dur, dws = _conv_taps_bwd(du, dnext[...], cw_ref[...], ur_ref[...])
        dur_ref[...] = dur.astype(BF16)
        dcw_ref[...] += dws
        dcb_ref[...] += jnp.sum(du, axis=0, keepdims=True)
        dnext[...] = du[:SUBLANES]

    def rowspec(col):
        return pl.BlockSpec((tm, w), lambda i: (nb - 1 - i, col))

    row = rowspec(0)
    nxt = pl.BlockSpec((SUBLANES, w), lambda i: (jnp.minimum((nb - i) * hb, last8), 0))
    prv = pl.BlockSpec((SUBLANES, w), lambda i: (jnp.maximum((nb - 1 - i) * hb - 1, 0), 0))
    vec = pl.BlockSpec((1, w), lambda i: (0, 0))
    mat = pl.BlockSpec((w, w), lambda i: (0, 0))
    cw = pl.BlockSpec((CONV_K, w), lambda i: (0, 0))
    o = jax.ShapeDtypeStruct((t, w), BF16)
    v1 = jax.ShapeDtypeStruct((1, w), F32)
    m1 = jax.ShapeDtypeStruct((w, w), F32)
    return pl.pallas_call(
        body, name=name, grid=(nb,),
        in_specs=[rowspec(0), rowspec(cu), rowspec(cg), row, row, row, row, nxt, row, prv, cw, mat, mat, vec],
        out_specs=[row, row, cw, vec, mat, vec, mat, vec, vec],
        out_shape=[o, o, jax.ShapeDtypeStruct((CONV_K, w), F32), v1, m1, v1, m1, v1, v1],
        scratch_shapes=[pltpu.VMEM((SUBLANES, w), F32), pltpu.VMEM((SUBLANES, w), F32)],
        compiler_params=_params(1),
    )(dymix, hbuf, hbuf, u, r, ig, a, a, h, h, conv_w, wa, wx, lam)


FOX_T = 512
FOX_PREP_TM = 256


def _log_sigmoid(x):
    return jnp.minimum(x, 0.0) - jnp.log(1.0 + jnp.exp(-jnp.abs(x)))


def _fox_prep(hbuf, bf_vec, *, name):
    t = hbuf.shape[0]
    tm = _pick(t, (FOX_PREP_TM, 128))
    cs = COL_SMALL // LANES

    def body(s_ref, b_ref, eq_ref, ek_ref, carry):
        i = pl.program_id(0)

        @pl.when(i == 0)
        def _():
            carry[...] = jnp.zeros_like(carry)

        lf = _log_sigmoid(s_ref[...] + b_ref[...])
        f = _cumsum_rows(lf) + carry[0:1, :]
        carry[...] = jnp.broadcast_to(f[tm - 1:tm, :], carry.shape)
        lane = _iota((tm, LANES), 1)
        for h in range(ATT_HEADS):
            base = HEAD_DIM * (1 - h % 2)
            fh = _col(f, h)
            hi = fh.astype(BF16).astype(F32)
            mid = (fh - hi).astype(BF16).astype(F32)
            lo = fh - hi - mid
            terms = jnp.where(lane == base, hi, jnp.where(lane == base + 1, mid, jnp.where(lane == base + 2, lo, 0.0)))
            terms_k = jnp.where(lane == base + 3, -hi,
                                jnp.where(lane == base + 4, -mid, jnp.where(lane == base + 5, -lo, 0.0)))
            ones_q = ((lane >= base + 3) & (lane < base + 6)).astype(F32)
            ones_k = ((lane >= base) & (lane < base + 3)).astype(F32)
            eq_ref[:, LANES * h:LANES * (h + 1)] = (terms + ones_q).astype(BF16)
            ek_ref[:, LANES * h:LANES * (h + 1)] = (terms_k + ones_k).astype(BF16)

    ospec = pl.BlockSpec((tm, ATT_HEADS * LANES), lambda i: (i, 0))
    o = jax.ShapeDtypeStruct((t, ATT_HEADS * LANES), BF16)
    return pl.pallas_call(
        body, name=name, grid=(t // tm,),
        in_specs=[pl.BlockSpec((tm, LANES), lambda i: (i, cs)), pl.BlockSpec((1, LANES), lambda i: (0, 0))],
        out_specs=[ospec, ospec], out_shape=[o, o],
        scratch_shapes=[pltpu.VMEM((SUBLANES, LANES), F32)],
        compiler_params=_params(1),
    )(hbuf, bf_vec)


def _fox_post(dfc, hbuf, bf_vec, *, name):
    t = hbuf.shape[0]
    tm = _pick(t, (FOX_PREP_TM, 128))
    nb = t // tm
    cs = COL_SMALL // LANES

    def body(df_ref, s_ref, b_ref, o_ref, db_ref, carry):
        i = pl.program_id(0)

        @pl.when(i == 0)
        def _():
            carry[...] = jnp.zeros_like(carry)
            db_ref[...] = jnp.zeros_like(db_ref)

        dlf = _cumsum_rows(df_ref[...], reverse=True) + carry[0:1, :]
        carry[...] = jnp.broadcast_to(dlf[0:1, :], carry.shape)
        dl = dlf * _sigmoid(-(s_ref[...] + b_ref[...]))
        dl = jnp.where(_iota(dl.shape, 1) < ATT_HEADS, dl, 0.0)
        o_ref[...] = dl
        db_ref[...] += jnp.sum(dl, axis=0, keepdims=True)

    vec = pl.BlockSpec((1, LANES), lambda i: (0, 0))
    return pl.pallas_call(
        body, name=name, grid=(nb,),
        in_specs=[pl.BlockSpec((tm, LANES), lambda i: (nb - 1 - i, 0)),
                  pl.BlockSpec((tm, LANES), lambda i: (nb - 1 - i, cs)), vec],
        out_specs=[pl.BlockSpec((tm, LANES), lambda i: (nb - 1 - i, 0)), vec],
        out_shape=[jax.ShapeDtypeStruct((t, LANES), F32), jax.ShapeDtypeStruct((1, LANES), F32)],
        scratch_shapes=[pltpu.VMEM((SUBLANES, LANES), F32)],
        compiler_params=_params(1),
    )(dfc, hbuf, bf_vec)


def _fox_masks(i, j, tq):
    row = i * tq + _iota((tq, tq), 0)
    col = j * tq + _iota((tq, tq), 1)
    lane = _iota((1, LANES), 1)
    return col <= row, (lane < HEAD_DIM, lane >= HEAD_DIM)


def _hosting(body, n_in, n_out, n_scratch, comm, nsteps):
    if comm is None:
        return body
    na, no = len(comm.arrays), len(comm.out_shapes)

    def hosted(*refs):
        o0 = n_in + na
        s0 = o0 + n_out + no
        cargs = (refs[n_in:o0], refs[o0 + n_out:s0]) + tuple(refs[s0 + n_scratch:])
        a, b = pl.program_id(0), pl.program_id(1)

        @pl.when((a == 0) & (b == 0))
        def _():
            comm.start(*cargs)

        body(*refs[:n_in], *refs[o0:o0 + n_out], *refs[s0:s0 + n_scratch])

        @pl.when((a == nsteps - 1) & (b == nsteps - 1))
        def _():
            comm.finish(*cargs)

    return hosted


def _hosted_call(body, comm, nsteps, *, name, in_specs, out_specs, out_shape, scratch_shapes, args):
    n_out = len(out_shape)
    if comm is not None:
        cin, cout, sems = comm.specs()
        body = _hosting(body, len(in_specs), n_out, len(scratch_shapes), comm, nsteps)
        in_specs, out_specs = in_specs + cin, out_specs + cout
        out_shape, scratch_shapes, args = out_shape + comm.out_shapes, scratch_shapes + sems, args + list(comm.arrays)
    outs = pl.pallas_call(body, name=name, grid=(nsteps, nsteps), in_specs=in_specs, out_specs=out_specs,
                          out_shape=out_shape, scratch_shapes=scratch_shapes, compiler_params=_params(2))(*args)
    return outs[:n_out], outs[n_out:]


def _fox_fwd(hbuf, eq, ek, *, comm=None, name):
    t = hbuf.shape[0]
    w = ATT_WIDTH
    tq = _pick(t, (FOX_T, 256, 128))
    nq = t // tq
    cq, ck, cv = COL_Q // w, COL_K // w, COL_V // w

    def body(q_ref, k_ref, v_ref, eq_ref, ek_ref, o_ref, lse_ref, m_s, l_s, acc_s):
        i = pl.program_id(0)
        j = pl.program_id(1)

        @pl.when(j == 0)
        def _():
            m_s[...] = jnp.full_like(m_s, NEG)
            l_s[...] = jnp.zeros_like(l_s)
            acc_s[...] = jnp.zeros_like(acc_s)

        def step(diagonal):
            _, hms = _fox_masks(i, j, tq)
            keys_first = (j * tq + _iota((tq, tq), 0)) <= (i * tq + _iota((tq, tq), 1))
            half = _iota((LANES, 1), 0)
            hrows = (half < HEAD_DIM, half >= HEAD_DIM)
            m_all = m_s[...]
            l_all = l_s[...]
            acc_old = [acc_s[LANES * pr:LANES * (pr + 1), :] for pr in range(2)]
            m_out, l_out, acc_out = [], [], []
            for pr in range(2):
                sl = slice(LANES * pr, LANES * (pr + 1))
                qp = q_ref[:, sl]
                kp = k_ref[:, sl]
                vt = v_ref[:, sl].T.astype(BF16)
                acc = acc_old[pr]
                for hh in range(2):
                    h = 2 * pr + hh
                    hsl = slice(LANES * h, LANES * (h + 1))
                    qm = jnp.where(hms[hh], (qp * (HEAD_DIM ** -0.5)).astype(BF16), eq_ref[:, hsl])
                    km = jnp.where(hms[hh], kp.astype(BF16), ek_ref[:, hsl])
                    st = _dot(km, qm, 1, 1)
                    if diagonal:
                        st = jnp.where(keys_first, st, NEG)
                    m_old = m_all[h:h + 1, :]
                    m_new = jnp.maximum(m_old, jnp.max(st, axis=0, keepdims=True))
                    alpha = jnp.exp(m_old - m_new)
                    pt = jnp.exp(st - m_new)
                    l_out.append(alpha * l_all[h:h + 1, :] + jnp.sum(pt, axis=0, keepdims=True))
                    m_out.append(m_new)
                    pv = _dot(vt, pt.astype(BF16), 1, 0)
                    acc = jnp.where(hrows[hh], alpha * acc_old[pr] + pv, acc)
                acc_out.append(acc)
            for h in range(ATT_HEADS):
                m_s[h:h + 1, :] = m_out[h]
                l_s[h:h + 1, :] = l_out[h]
            for pr in range(2):
                acc_s[LANES * pr:LANES * (pr + 1), :] = acc_out[pr]

        @pl.when(j < i)
        def _():
            step(False)

        @pl.when(j == i)
        def _():
            step(True)
            half = _iota((LANES, 1), 0)
            l_all = l_s[...]
            for pr in range(2):
                acc = acc_s[LANES * pr:LANES * (pr + 1), :]
                o_t = jnp.where(half < HEAD_DIM, acc / l_all[2 * pr:2 * pr + 1, :], acc / l_all[2 * pr + 1:2 * pr + 2, :])
                o_ref[:, LANES * pr:LANES * (pr + 1)] = o_t.T
            lse = m_s[...] + jnp.log(l_s[...])
            lse_ref[...] = jnp.where(_iota(lse.shape, 0) < ATT_HEADS, lse, 0.0)

    return _hosted_call(
        body, comm, nq, name=name,
        in_specs=[pl.BlockSpec((tq, w), lambda i, j: (i, cq)),
                  pl.BlockSpec((tq, w), lambda i, j: (jnp.minimum(j, i), ck)),
                  pl.BlockSpec((tq, w), lambda i, j: (jnp.minimum(j, i), cv)),
                  pl.BlockSpec((tq, ATT_HEADS * LANES), lambda i, j: (i, 0)),
                  pl.BlockSpec((tq, ATT_HEADS * LANES), lambda i, j: (jnp.minimum(j, i), 0))],
        out_specs=[pl.BlockSpec((tq, w), lambda i, j: (i, 0)),
                   pl.BlockSpec((SUBLANES, tq), lambda i, j: (0, i))],
        out_shape=[jax.ShapeDtypeStruct((t, w), F32), jax.ShapeDtypeStruct((SUBLANES, t), F32)],
        scratch_shapes=[pltpu.VMEM((SUBLANES, tq), F32), pltpu.VMEM((SUBLANES, tq), F32),
                        pltpu.VMEM((w, tq), F32)],
        args=[hbuf, hbuf, hbuf, eq, ek])


def _fox_delta(dymix, o, *, name):
    t, w = o.shape
    tm = _pick(t, (512, 256, 128))
    cdo = ATT_WIDTH // w

    def body(do_ref, o_ref, d_ref):
        d_ref[...] = _head_reduce(do_ref[...] * o_ref[...], 0, ATT_HEADS)

    return pl.pallas_call(
        body, name=name, grid=(t // tm,),
        in_specs=[pl.BlockSpec((tm, w), lambda i: (i, cdo)), pl.BlockSpec((tm, w), lambda i: (i, 0))],
        out_specs=pl.BlockSpec((tm, LANES), lambda i: (i, 0)),
        out_shape=jax.ShapeDtypeStruct((t, LANES), F32),
        compiler_params=_params(1),
    )(dymix, o)


def _fox_bwd(hbuf, eq, ek, dymix, lse_rows, delta_rows, *, comm=None, name):
    t = hbuf.shape[0]
    w = ATT_WIDTH
    tq = _pick(t, (FOX_T, 256, 128))
    nq = t // tq
    cq, ck, cv = COL_Q // w, COL_K // w, COL_V // w
    cdo = ATT_WIDTH // w

    def body(q_ref, k_ref, v_ref, eq_ref, ek_ref, do_ref, lse_ref, dl_ref, dk_ref, dv_ref, dfk_ref, dqt_ref, dfq_ref,
             dk_s, dv_s, dfk_s):
        j = pl.program_id(0)
        i = pl.program_id(1)

        @pl.when((i == 0) & (j == 0))
        def _():
            dqt_ref[...] = jnp.zeros_like(dqt_ref)
            dfq_ref[...] = jnp.zeros_like(dfq_ref)

        @pl.when(i == 0)
        def _():
            dk_s[...] = jnp.zeros_like(dk_s)
            dv_s[...] = jnp.zeros_like(dv_s)
            dfk_s[...] = jnp.zeros_like(dfk_s)

        def step(diagonal):
            _, hms = _fox_masks(i, j, tq)
            keys_first = (j * tq + _iota((tq, tq), 0)) <= (i * tq + _iota((tq, tq), 1))
            half = _iota((LANES, 1), 0)
            hrows = (half < HEAD_DIM, half >= HEAD_DIM)
            lse_all = lse_ref[...]
            dl_all = dl_ref[...]
            dvs, dks, dfks, dqts, dfqs = [], [], [], [], []
            for pr in range(2):
                sl = slice(LANES * pr, LANES * (pr + 1))
                qp = q_ref[:, sl]
                kp = k_ref[:, sl]
                kt = kp.T.astype(BF16)
                vpb = v_ref[:, sl].astype(BF16)
                dop = do_ref[:, sl]
                dv_p = jnp.zeros((tq, LANES), F32)
                dk_p = jnp.zeros((tq, LANES), F32)
                dqt_p = jnp.zeros((LANES, tq), F32)
                for hh in range(2):
                    h = 2 * pr + hh
                    hsl = slice(LANES * h, LANES * (h + 1))
                    qm = jnp.where(hms[hh], (qp * (HEAD_DIM ** -0.5)).astype(BF16), eq_ref[:, hsl])
                    km = jnp.where(hms[hh], kp.astype(BF16), ek_ref[:, hsl])
                    st = _dot(km, qm, 1, 1)
                    if diagonal:
                        st = jnp.where(keys_first, st, NEG)
                    pt = jnp.exp(st - lse_all[h:h + 1, :])
                    domb = jnp.where(hms[hh], dop, 0.0).astype(BF16)
                    dv_p = dv_p + _dot(pt.astype(BF16), domb, 1, 0)
                    dpt = _dot(vpb, domb, 1, 1)
                    dst = pt * (dpt - dl_all[h:h + 1, :])
                    dstb = dst.astype(BF16)
                    dk_p = dk_p + jnp.where(hms[hh], _dot(dstb, qm, 1, 0), 0.0)
                    dqt_p = dqt_p + _dot(jnp.where(hrows[hh], kt, 0.0), dstb, 1, 0)
                    part = dst[:, 0:LANES]
                    for c in range(1, tq // LANES):
                        part = part + dst[:, LANES * c:LANES * (c + 1)]
                    dfks.append(part)
                    dfqs.append(jnp.sum(dst, axis=0, keepdims=True))
                dvs.append(dv_p)
                dks.append(dk_p)
                dqts.append(dqt_p)
            dv_s[...] += jnp.concatenate(dvs, axis=1)
            dk_s[...] += jnp.concatenate(dks, axis=1)
            for h in range(ATT_HEADS):
                dfk_s[h] += dfks[h]
            cols = pl.ds(pl.multiple_of(i * tq, tq), tq)
            dqt_ref[:, cols] += jnp.concatenate(dqts, axis=0) * (HEAD_DIM ** -0.5)
            dfq_ref[:, cols] += jnp.concatenate(dfqs + [jnp.zeros((SUBLANES - ATT_HEADS, tq), F32)], axis=0)

        @pl.when(i > j)
        def _():
            step(False)

        @pl.when(i == j)
        def _():
            step(True)

        @pl.when(i == nq - 1)
        def _():
            dk_ref[...] = dk_s[...].astype(BF16)
            dv_ref[...] = dv_s[...].astype(BF16)
            lane = _iota((tq, LANES), 1)
            out = jnp.zeros((tq, LANES), F32)
            for h in range(ATT_HEADS):
                out = jnp.where(lane == h, jnp.sum(dfk_s[h], axis=1, keepdims=True), out)
            dfk_ref[...] = out

    qi = lambda j, i: jnp.maximum(i, j)
    rows = pl.BlockSpec((SUBLANES, tq), lambda j, i: (0, qi(j, i)))
    return _hosted_call(
        body, comm, nq, name=name,
        in_specs=[pl.BlockSpec((tq, w), lambda j, i: (qi(j, i), cq)),
                  pl.BlockSpec((tq, w), lambda j, i: (j, ck)),
                  pl.BlockSpec((tq, w), lambda j, i: (j, cv)),
                  pl.BlockSpec((tq, ATT_HEADS * LANES), lambda j, i: (qi(j, i), 0)),
                  pl.BlockSpec((tq, ATT_HEADS * LANES), lambda j, i: (j, 0)),
                  pl.BlockSpec((tq, w), lambda j, i: (qi(j, i), cdo)),
                  rows, rows],
        out_specs=[pl.BlockSpec((tq, w), lambda j, i: (j, 0)), pl.BlockSpec((tq, w), lambda j, i: (j, 0)),
                   pl.BlockSpec((tq, LANES), lambda j, i: (j, 0)),
                   pl.BlockSpec((w, t), lambda j, i: (0, 0)), pl.BlockSpec((SUBLANES, t), lambda j, i: (0, 0))],
        out_shape=[jax.ShapeDtypeStruct((t, w), BF16), jax.ShapeDtypeStruct((t, w), BF16),
                   jax.ShapeDtypeStruct((t, LANES), F32),
                   jax.ShapeDtypeStruct((w, t), F32), jax.ShapeDtypeStruct((SUBLANES, t), F32)],
        scratch_shapes=[pltpu.VMEM((tq, w), F32), pltpu.VMEM((tq, w), F32),
                        pltpu.VMEM((ATT_HEADS, tq, LANES), F32)],
        args=[hbuf, hbuf, hbuf, eq, ek, dymix, lse_rows, delta_rows])


GROUP_W = SSD_WIDTH // SSD_GROUPS
HEADS_PER_GROUP = SSD_HEADS // SSD_GROUPS


def _ssd_chunk_common(xr, prev8, sm, cw, cb, dtb, avec):
    c = _conv_taps(xr, prev8, cw, cb)
    sig = _sigmoid(c)
    xa = c * sig
    dt = _softplus(sm + dtb)
    a = dt * avec
    acum = _cumsum_rows(a)
    return c, sig, xa, dt, acum


def _ssd_decays(acum, g):
    n = acum.shape[0]
    atot = acum[n - 1:n, :]
    lane0 = LANE_DT + HEADS_PER_GROUP * g
    e = _head_expand(jnp.exp(acum), lane0, HEADS_PER_GROUP, GROUP_W)
    dec = _head_expand(jnp.exp(atot - acum), lane0, HEADS_PER_GROUP, GROUP_W)
    etot = _head_expand(jnp.exp(atot), lane0, HEADS_PER_GROUP, GROUP_W)
    return e, dec, etot


def _ssd_ldec(acum, acum_t, lane, tril):
    return jnp.exp(jnp.where(tril, _col(acum, lane) - _row(acum_t, lane), NEG))


def _ssd_fwd(hbuf, conv_w, conv_b, dtb_vec, a_vec, d_exp, norm_g, *, name):
    t = hbuf.shape[0]
    L = SSD_CHUNK
    nc = t // L
    hb = L // SUBLANES
    cs = COL_SMALL // LANES
    cz = COL_Z // SSD_WIDTH

    def body(x_ref, xp_ref, z_ref, s_ref, cw_ref, cb_ref, dtb_ref, av_ref, dx_ref, ng_ref,
             yc_ref, y_ref, st_ref, state):
        i = pl.program_id(0)

        @pl.when(i == 0)
        def _():
            state[...] = jnp.zeros_like(state)

        prev = jnp.where(i == 0, 0.0, xp_ref[...])
        _, _, xa, dt, acum = _ssd_chunk_common(x_ref[...], prev, s_ref[...], cw_ref[...], cb_ref[...],
                                               dtb_ref[...], av_ref[...])
        acum_t = acum.T
        xs = xa[:, :SSD_WIDTH]
        xdt = xs * _head_expand(dt, LANE_DT, SSD_HEADS, SSD_WIDTH)
        tril = _iota((L, L), 0) >= _iota((L, L), 1)
        lane = _iota((1, LANES), 1)
        ys = []
        for g in range(SSD_GROUPS):
            bg = xa[:, SSD_WIDTH + SSD_STATE * g:SSD_WIDTH + SSD_STATE * (g + 1)].astype(BF16)
            cg = xa[:, SSD_WIDTH + SSD_STATE * (SSD_GROUPS + g):SSD_WIDTH + SSD_STATE * (SSD_GROUPS + g + 1)].astype(BF16)
            gm = _dot(cg, bg, 1, 1)
            e, dec, etot = _ssd_decays(acum, g)
            s_in = state[g]
            st_ref[0, g] = s_in
            xg = xdt[:, GROUP_W * g:GROUP_W * (g + 1)]
            y_off = e * _dot(cg, s_in.astype(BF16), 1, 0)
            state[g] = etot * s_in + _dot(bg, (dec * xg).astype(BF16), 0, 0)
            for pr in range(2):
                xp = xg[:, LANES * pr:LANES * (pr + 1)].astype(BF16)
                outs = []
                for hh in range(2):
                    h = HEADS_PER_GROUP * g + 2 * pr + hh
                    m = gm * _ssd_ldec(acum, acum_t, LANE_DT + h, tril)
                    outs.append(_dot(m.astype(BF16), xp, 1, 0))
                ys.append(jnp.where(lane < HEAD_DIM, outs[0], outs[1]) + y_off[:, LANES * pr:LANES * (pr + 1)])
        y = jnp.concatenate(ys, axis=1)
        y_ref[...] = y
        yd = y + dx_ref[...] * xs
        zz = z_ref[...]
        y2 = yd * zz * _sigmoid(zz)
        ng = ng_ref[...]
        outs = []
        for g in range(SSD_GROUPS):
            yg = y2[:, GROUP_W * g:GROUP_W * (g + 1)]
            rs = lax.rsqrt(jnp.mean(yg * yg, axis=1, keepdims=True) + RMS_EPS)
            outs.append(yg * rs * ng[:, GROUP_W * g:GROUP_W * (g + 1)])
        yc_ref[...] = jnp.concatenate(outs, axis=1)

    cdim = SSD_CONV_DIM
    vecc = pl.BlockSpec((1, cdim), lambda i: (0, 0))
    vecl = pl.BlockSpec((1, LANES), lambda i: (0, 0))
    vecw = pl.BlockSpec((1, SSD_WIDTH), lambda i: (0, 0))
    roww = pl.BlockSpec((L, SSD_WIDTH), lambda i: (i, 0))
    return pl.pallas_call(
        body, name=name, grid=(nc,),
        in_specs=[pl.BlockSpec((L, cdim), lambda i: (i, 0)),
                  pl.BlockSpec((SUBLANES, cdim), lambda i: (jnp.maximum(i * hb - 1, 0), 0)),
                  pl.BlockSpec((L, SSD_WIDTH), lambda i: (i, cz)),
                  pl.BlockSpec((L, LANES), lambda i: (i, cs)),
                  pl.BlockSpec((CONV_K, cdim), lambda i: (0, 0)), vecc, vecl, vecl, vecw, vecw],
        out_specs=[roww, roww, pl.BlockSpec((1, SSD_GROUPS, SSD_STATE, GROUP_W), lambda i: (i, 0, 0, 0))],
        out_shape=[jax.ShapeDtypeStruct((t, SSD_WIDTH), F32), jax.ShapeDtypeStruct((t, SSD_WIDTH), F32),
                   jax.ShapeDtypeStruct((nc, SSD_GROUPS, SSD_STATE, GROUP_W), F32)],
        scratch_shapes=[pltpu.VMEM((SSD_GROUPS, SSD_STATE, GROUP_W), F32)],
        compiler_params=_params(1),
    )(hbuf, hbuf, hbuf, hbuf, conv_w, conv_b, dtb_vec, a_vec, d_exp, norm_g)


def _ssd_bwd(dymix, hbuf, y_ssd, states, conv_w, conv_b, dtb_vec, a_vec, d_exp, norm_g, *, name):
    t = hbuf.shape[0]
    L = SSD_CHUNK
    nc = t // L
    hb = L // SUBLANES
    cs = COL_SMALL // LANES
    cz = COL_Z // SSD_WIDTH
    cdy = (LRU_WIDTH + ATT_WIDTH) // SSD_WIDTH
    cdim = SSD_CONV_DIM

    def body(dyc_ref, x_ref, xp_ref, z_ref, s_ref, y_ref, st_ref, cw_ref, cb_ref, dtb_ref, av_ref, dx_ref, ng_ref,
             dxr_ref, dz_ref, dsm_ref, dng_ref, dd_ref, da_ref, ddtb_ref, dcw_ref, dcb_ref,
             dstate, dnext):
        i = pl.program_id(0)
        ic = nc - 1 - i

        @pl.when(i == 0)
        def _():
            dstate[...] = jnp.zeros_like(dstate)
            dnext[...] = jnp.zeros_like(dnext)
            for ref in (dng_ref, dd_ref, da_ref, ddtb_ref, dcw_ref, dcb_ref):
                ref[...] = jnp.zeros_like(ref)

        xr = x_ref[...]
        sm = s_ref[...]
        prev = jnp.where(ic == 0, 0.0, xp_ref[...])
        avec = av_ref[...]
        c, sig, xa, dt, acum = _ssd_chunk_common(xr, prev, sm, cw_ref[...], cb_ref[...], dtb_ref[...], avec)
        acum_t = acum.T
        xs = xa[:, :SSD_WIDTH]
        dtx = _head_expand(dt, LANE_DT, SSD_HEADS, SSD_WIDTH)
        xdt = xs * dtx
        tril = _iota((L, L), 0) >= _iota((L, L), 1)
        lane = _iota((1, LANES), 1)
        hmasks = (lane < HEAD_DIM, lane >= HEAD_DIM)

        y = y_ref[...]
        dexp = dx_ref[...]
        yd = y + dexp * xs
        zz = z_ref[...]
        sz = _sigmoid(zz)
        siluz = zz * sz
        y2 = yd * siluz
        ng = ng_ref[...]
        dyc = dyc_ref[...]
        dy2s, dngs = [], []
        for g in range(SSD_GROUPS):
            sl = slice(GROUP_W * g, GROUP_W * (g + 1))
            yg = y2[:, sl]
            rs = lax.rsqrt(jnp.mean(yg * yg, axis=1, keepdims=True) + RMS_EPS)
            wv = dyc[:, sl] * ng[:, sl]
            dngs.append(jnp.sum(dyc[:, sl] * yg * rs, axis=0, keepdims=True))
            dy2s.append(rs * wv - yg * (rs * rs * rs) * jnp.mean(wv * yg, axis=1, keepdims=True))
        dy2 = jnp.concatenate(dy2s, axis=1)
        dng_ref[...] += jnp.concatenate(dngs, axis=1)
        dz_ref[...] = (dy2 * yd * (sz * (1.0 + zz * (1.0 - sz)))).astype(BF16)
        dy = dy2 * siluz
        dd_ref[...] += jnp.sum(dy * xs, axis=0, keepdims=True)

        dxs, dbs, dcs = [], [], []
        datot = jnp.zeros((1, LANES), F32)
        lanes = _iota((L, LANES), 1)
        dacum = jnp.zeros((L, LANES), F32)
        for g in range(SSD_GROUPS):
            sl = slice(GROUP_W * g, GROUP_W * (g + 1))
            bg = xa[:, SSD_WIDTH + SSD_STATE * g:SSD_WIDTH + SSD_STATE * (g + 1)].astype(BF16)
            cg = xa[:, SSD_WIDTH + SSD_STATE * (SSD_GROUPS + g):SSD_WIDTH + SSD_STATE * (SSD_GROUPS + g + 1)].astype(BF16)
            gm = _dot(cg, bg, 1, 1)
            e, dec, etot = _ssd_decays(acum, g)
            s_in = st_ref[0, g]
            ds_out = dstate[g]
            dyg = dy[:, sl]
            xg = xdt[:, sl]
            edy = (e * dyg).astype(BF16)
            dstate[g] = etot * ds_out + _dot(cg, edy, 0, 0)
            dx_state = dec * _dot(bg, ds_out.astype(BF16), 1, 0)
            y_off = e * _dot(cg, s_in.astype(BF16), 1, 0)
            dacum = dacum + _head_reduce_group(dyg * y_off - xg * dx_state, g)
            dc_off = _dot(edy, s_in.astype(BF16), 1, 1)
            db_state = _dot((dec * xg).astype(BF16), ds_out.astype(BF16), 1, 1)
            dgsum = jnp.zeros((L, L), F32)
            dx_pairs = []
            for pr in range(2):
                psl = slice(LANES * pr, LANES * (pr + 1))
                xp = xg[:, psl]
                dyp = dyg[:, psl]
                dx_pair = jnp.zeros((L, LANES), F32)
                for hh in range(2):
                    h = HEADS_PER_GROUP * g + 2 * pr + hh
                    ldec = _ssd_ldec(acum, acum_t, LANE_DT + h, tril)
                    dym = jnp.where(hmasks[hh], dyp, 0.0).astype(BF16)
                    xm = jnp.where(hmasks[hh], xp, 0.0).astype(BF16)
                    dx_pair = dx_pair + _dot((gm * ldec).astype(BF16), dym, 0, 0)
                    dml = _dot(dym, xm, 1, 1) * ldec
                    dgsum = dgsum + dml
                    qm = dml * gm
                    seg = jnp.sum(qm, axis=1, keepdims=True) - jnp.sum(qm.T, axis=1, keepdims=True)
                    dacum = dacum + jnp.where(lanes == LANE_DT + h, seg, 0.0)
                dx_pairs.append(dx_pair)
            dgb = dgsum.astype(BF16)
            dcs.append(_dot(dgb, bg, 1, 0) + dc_off)
            dbs.append(_dot(dgb, cg, 0, 0) + db_state)
            dxg = jnp.concatenate(dx_pairs, axis=1) + dx_state
            dxs.append(dxg)
            v = jnp.sum(dx_state * xg, axis=0, keepdims=True) + etot * jnp.sum(ds_out * s_in, axis=0, keepdims=True)
            datot = datot + _head_reduce_row(v, LANE_DT + HEADS_PER_GROUP * g, HEADS_PER_GROUP)
        dx = jnp.concatenate(dxs, axis=1)
        dacum = dacum + jnp.where(_iota((L, LANES), 0) == L - 1, datot, 0.0)
        da = _cumsum_rows(dacum, reverse=True)
        ddt = da * avec + _head_reduce(dx * xs, LANE_DT, SSD_HEADS)
        da_ref[...] += jnp.sum(da * dt, axis=0, keepdims=True)
        ddt_raw = ddt * _sigmoid(sm + dtb_ref[...])
        ddt_raw = jnp.where((lanes >= LANE_DT) & (lanes < LANE_DT + SSD_HEADS), ddt_raw, 0.0)
        dsm_ref[...] = ddt_raw
        ddtb_ref[...] += jnp.sum(ddt_raw, axis=0, keepdims=True)
        dxs_total = dx * dtx + dexp * dy
        dxa = jnp.concatenate([dxs_total] + dbs + dcs, axis=1)
        dc = dxa * (sig * (1.0 + c * (1.0 - sig)))
        dxr, dws = _conv_taps_bwd(dc, dnext[...], cw_ref[...], xr)
        dxr_ref[...] = dxr.astype(BF16)
        dcw_ref[...] += dws
        dcb_ref[...] += jnp.sum(dc, axis=0, keepdims=True)
        dnext[...] = dc[:SUBLANES]

    rev = lambda i: nc - 1 - i
    vecc = pl.BlockSpec((1, cdim), lambda i: (0, 0))
    vecl = pl.BlockSpec((1, LANES), lambda i: (0, 0))
    vecw = pl.BlockSpec((1, SSD_WIDTH), lambda i: (0, 0))
    cwspec = pl.BlockSpec((CONV_K, cdim), lambda i: (0, 0))
    roww = pl.BlockSpec((L, SSD_WIDTH), lambda i: (rev(i), 0))
    return pl.pallas_call(
        body, name=name, grid=(nc,),
        in_specs=[pl.BlockSpec((L, SSD_WIDTH), lambda i: (rev(i), cdy)),
                  pl.BlockSpec((L, cdim), lambda i: (rev(i), 0)),
                  pl.BlockSpec((SUBLANES, cdim), lambda i: (jnp.maximum(rev(i) * hb - 1, 0), 0)),
                  pl.BlockSpec((L, SSD_WIDTH), lambda i: (rev(i), cz)),
                  pl.BlockSpec((L, LANES), lambda i: (rev(i), cs)),
                  roww,
                  pl.BlockSpec((1, SSD_GROUPS, SSD_STATE, GROUP_W), lambda i: (rev(i), 0, 0, 0)),
                  cwspec, vecc, vecl, vecl, vecw, vecw],
        out_specs=[pl.BlockSpec((L, cdim), lambda i: (rev(i), 0)), roww,
                   pl.BlockSpec((L, LANES), lambda i: (rev(i), 0)),
                   vecw, vecw, vecl, vecl, cwspec, vecc],
        out_shape=[jax.ShapeDtypeStruct((t, cdim), BF16), jax.ShapeDtypeStruct((t, SSD_WIDTH), BF16),
                   jax.ShapeDtypeStruct((t, LANES), F32),
                   jax.ShapeDtypeStruct((1, SSD_WIDTH), F32), jax.ShapeDtypeStruct((1, SSD_WIDTH), F32),
                   jax.ShapeDtypeStruct((1, LANES), F32), jax.ShapeDtypeStruct((1, LANES), F32),
                   jax.ShapeDtypeStruct((CONV_K, cdim), F32), jax.ShapeDtypeStruct((1, cdim), F32)],
        scratch_shapes=[pltpu.VMEM((SSD_GROUPS, SSD_STATE, GROUP_W), F32), pltpu.VMEM((SUBLANES, cdim), F32)],
        compiler_params=_params(1),
    )(dymix, hbuf, hbuf, hbuf, hbuf, y_ssd, states, conv_w, conv_b, dtb_vec, a_vec, d_exp, norm_g)


def _head_reduce_group(x, g):
    return _head_reduce(x, LANE_DT + HEADS_PER_GROUP * g, HEADS_PER_GROUP)


def _head_reduce_row(v, lane0, nheads):
    colhead = _iota(v.shape, 1) // HEAD_DIM
    lane = _iota((1, LANES), 1)
    out = jnp.zeros((1, LANES), F32)
    for h in range(nheads):
        s = jnp.sum(jnp.where(colhead == h, v, 0.0), axis=1, keepdims=True)
        out = jnp.where(lane == lane0 + h, s, out)
    return out


def _exchange(inps, axes, *, swap=False, name):
    n = 2 ** len(axes)
    assert not swap or n == 2
    counts = [a.shape[0] for a in inps]
    out_shapes = [jax.ShapeDtypeStruct(a.shape if swap else (n,) + a.shape, a.dtype) for a in inps]
    units = sum(counts)
    na = len(inps)

    def body(*refs):
        in_refs, out_refs = refs[:na], refs[na:2 * na]
        send_sems, recv_sems, local_sems = refs[2 * na:]
        pos = {ax: lax.axis_index(ax) for ax in MESH_AXES}

        def slot_of(coord):
            s = 0
            for ax in axes:
                s = s * 2 + coord[ax]
            return s

        me = slot_of(pos)
        copies = []
        unit = 0
        for a in range(na):
            for it in range(counts[a]):
                dst = out_refs[a].at[it] if swap else out_refs[a].at[me, it]
                if not swap:
                    cp = pltpu.make_async_copy(in_refs[a].at[it], dst, local_sems.at[unit])
                    cp.start()
                    copies.append(cp)
                for delta in range(1, n):
                    coord = dict(pos)
                    for b, ax in enumerate(reversed(axes)):
                        if (delta >> b) & 1:
                            coord[ax] = 1 - pos[ax]
                    k = unit * (n - 1) + delta - 1
                    cp = pltpu.make_async_remote_copy(
                        src_ref=in_refs[a].at[it], dst_ref=dst,
                        send_sem=send_sems.at[k], recv_sem=recv_sems.at[k],
                        device_id=(coord["x"], coord["y"], coord["c"]), device_id_type=pl.DeviceIdType.MESH)
                    cp.start()
                    copies.append(cp)
                unit += 1
        for cp in copies:
            cp.wait()

    any_spec = pl.BlockSpec(memory_space=pl.ANY)
    return pl.pallas_call(
        body, name=name,
        in_specs=[any_spec] * na, out_specs=[any_spec] * na, out_shape=out_shapes,
        scratch_shapes=[pltpu.SemaphoreType.DMA((units * (n - 1),)), pltpu.SemaphoreType.DMA((units * (n - 1),)),
                        pltpu.SemaphoreType.DMA((units,))],
    )(*inps)


class _Comm:
    def __init__(self, arrays, out_shapes, n_sems, start, finish):
        self.arrays, self.out_shapes, self.n_sems, self.start, self.finish = arrays, out_shapes, n_sems, start, finish

    def specs(self):
        any_spec = pl.BlockSpec(memory_space=pl.ANY)
        sems = [pltpu.SemaphoreType.DMA((self.n_sems,)), pltpu.SemaphoreType.DMA((self.n_sems,))]
        return [any_spec] * len(self.arrays), [any_spec] * len(self.out_shapes), sems


def _run_comm(comm, *, name):
    na, no = len(comm.arrays), len(comm.out_shapes)

    def body(*refs):
        args = (refs[:na], refs[na:na + no]) + tuple(refs[na + no:])
        comm.start(*args)
        comm.finish(*args)

    in_specs, out_specs, sems = comm.specs()
    return pl.pallas_call(body, name=name, in_specs=in_specs, out_specs=out_specs, out_shape=comm.out_shapes,
                          scratch_shapes=sems)(*comm.arrays)


def _chip_peer(x, y, d):
    px = 1 - x if d & 2 else x
    py = 1 - y if d & 1 else y
    return px, py, 2 * px + py


def _gather_layer_comm(srcs, li):
    counts = [s.shape[0] for s in srcs]
    units = [(a, it) for a in range(len(srcs)) for it in range(counts[a])]
    n_ici = 3 * len(units)
    out_shapes = [jax.ShapeDtypeStruct((N_CHIPS,) + s.shape, s.dtype) for s in srcs]

    def ici(ins, outs, ssem, rsem, u, d):
        x, y, c = (lax.axis_index(ax) for ax in MESH_AXES)
        a, it = units[u]
        px, py, _ = _chip_peer(x, y, d)
        k = 3 * u + d - 1
        return pltpu.make_async_remote_copy(
            src_ref=ins[a].at[it], dst_ref=outs[a].at[2 * x + y, it], send_sem=ssem.at[k], recv_sem=rsem.at[k],
            device_id=(px, py, c), device_id_type=pl.DeviceIdType.MESH)

    def arrived(ins, outs, ssem, rsem, u, d):
        x, y, c = (lax.axis_index(ax) for ax in MESH_AXES)
        a, it = units[u]
        _, _, pk = _chip_peer(x, y, d)
        k = 3 * u + d - 1
        return pltpu.make_async_remote_copy(
            src_ref=ins[a].at[it], dst_ref=outs[a].at[pk, it], send_sem=ssem.at[k], recv_sem=rsem.at[k],
            device_id=(x, y, c), device_id_type=pl.DeviceIdType.MESH)

    def forward(ins, outs, ssem, rsem, u, slot):
        x, y, c = (lax.axis_index(ax) for ax in MESH_AXES)
        a, it = units[u]
        pk = 2 * x + y if slot == 0 else _chip_peer(x, y, slot)[2]
        src = ins[a].at[it] if slot == 0 else outs[a].at[pk, it]
        k = n_ici + 4 * u + slot
        return pltpu.make_async_remote_copy(
            src_ref=src, dst_ref=outs[a].at[pk, it], send_sem=ssem.at[k], recv_sem=rsem.at[k],
            device_id=(x, y, 1 - c), device_id_type=pl.DeviceIdType.MESH)

    def start(ins, outs, ssem, rsem):
        @pl.when(lax.axis_index("c") == li)
        def _():
            for u in range(len(units)):
                for d in range(1, N_CHIPS):
                    ici(ins, outs, ssem, rsem, u, d).start()

    def finish(ins, outs, ssem, rsem):
        c = lax.axis_index("c")

        @pl.when(c == li)
        def _():
            for u in range(len(units)):
                forward(ins, outs, ssem, rsem, u, 0).start()
                for d in range(1, N_CHIPS):
                    arrived(ins, outs, ssem, rsem, u, d).wait_recv()
                    forward(ins, outs, ssem, rsem, u, d).start()
            for u in range(len(units)):
                for d in range(1, N_CHIPS):
                    ici(ins, outs, ssem, rsem, u, d).wait_send()
                for slot in range(N_CHIPS):
                    forward(ins, outs, ssem, rsem, u, slot).wait_send()

        @pl.when(c != li)
        def _():
            for u in range(len(units)):
                for slot in range(N_CHIPS):
                    forward(ins, outs, ssem, rsem, u, slot).wait_recv()

    return _Comm(srcs, out_shapes, n_ici + 4 * len(units), start, finish)


def _reduce_chips_comm(sums, li):
    counts = [s.shape[0] for s in sums]
    units = [(a, it) for a in range(len(sums)) for it in range(counts[a])]
    out_shapes = [jax.ShapeDtypeStruct((N_CHIPS, s.shape[0]) + s.shape[2:], s.dtype) for s in sums]

    def copy(ins, outs, ssem, rsem, u, d):
        x, y, c = (lax.axis_index(ax) for ax in MESH_AXES)
        a, it = units[u]
        px, py, pk = _chip_peer(x, y, d)
        k = 3 * u + d - 1
        return pltpu.make_async_remote_copy(
            src_ref=ins[a].at[it, pk], dst_ref=outs[a].at[2 * x + y, it], send_sem=ssem.at[k], recv_sem=rsem.at[k],
            device_id=(px, py, c), device_id_type=pl.DeviceIdType.MESH)

    def start(ins, outs, ssem, rsem):
        @pl.when(lax.axis_index("c") == li)
        def _():
            for u in range(len(units)):
                for d in range(1, N_CHIPS):
                    copy(ins, outs, ssem, rsem, u, d).start()

    def finish(ins, outs, ssem, rsem):
        @pl.when(lax.axis_index("c") == li)
        def _():
            for u in range(len(units)):
                for d in range(1, N_CHIPS):
                    copy(ins, outs, ssem, rsem, u, d).wait()

    return _Comm(sums, out_shapes, 3 * len(units), start, finish)


def _sum_slots(buf, out_dtype, *, name):
    n, rows, cols = buf.shape
    tm = _pick(rows, (512, 256, 128, 8))
    if rows % tm:
        tm = rows

    def body(b_ref, o_ref):
        acc = b_ref[0].astype(F32)
        for s in range(1, n):
            acc = acc + b_ref[s].astype(F32)
        o_ref[...] = acc.astype(out_dtype)

    return pl.pallas_call(
        body, name=name, grid=(pl.cdiv(rows, tm),),
        in_specs=[pl.BlockSpec((n, tm, cols), lambda i: (0, i, 0))],
        out_specs=pl.BlockSpec((tm, cols), lambda i: (i, 0)),
        out_shape=jax.ShapeDtypeStruct((rows, cols), out_dtype),
        compiler_params=_params(1),
    )(buf)


def _sum_pair(a, b, out_dtype, *, name):
    shape = a.shape
    cols = shape[-1]
    a2, b2 = a.reshape(-1, cols), b.reshape(-1, cols)
    rows = a2.shape[0]
    tm = _pick(rows, (512, 256, 128, 8))

    def body(a_ref, b_ref, o_ref):
        o_ref[...] = (a_ref[...].astype(F32) + b_ref[...].astype(F32)).astype(out_dtype)

    spec = pl.BlockSpec((tm, cols), lambda i: (i, 0))
    return pl.pallas_call(
        body, name=name, grid=(rows // tm,), in_specs=[spec, spec], out_specs=spec,
        out_shape=jax.ShapeDtypeStruct((rows, cols), out_dtype), compiler_params=_params(1),
    )(a2, b2).reshape(shape)


def _adamw(w, g, m, v, *, name):
    shape = w.shape
    cols = shape[-1]
    rows = w.size // cols
    w2, g2, m2, v2 = (a.reshape(rows, cols) for a in (w, g, m, v))
    tm = _pick(rows, (256, 128, 64, 32, 16, 8))
    if rows % tm:
        tm = rows
    bc1 = 1.0 - ADAM_B1 ** ADAM_STEP
    bc2 = 1.0 - ADAM_B2 ** ADAM_STEP

    def body(w_ref, g_ref, m_ref, v_ref, d_ref, nm_ref, nv_ref):
        gg = g_ref[...]
        mm = ADAM_B1 * m_ref[...] + (1.0 - ADAM_B1) * gg
        vv = ADAM_B2 * v_ref[...] + (1.0 - ADAM_B2) * (gg * gg)
        m_hat = mm / bc1
        v_hat = vv / bc2
        d_ref[...] = -ADAM_LR * (m_hat / (jnp.sqrt(v_hat) + ADAM_EPS) + ADAM_WD * w_ref[...])
        nm_ref[...] = mm
        nv_ref[...] = vv

    spec = pl.BlockSpec((tm, cols), lambda i: (i, 0))
    o = jax.ShapeDtypeStruct((rows, cols), F32)
    outs = pl.pallas_call(
        body, name=name, grid=(rows // tm,), in_specs=[spec] * 4, out_specs=[spec] * 3, out_shape=[o] * 3,
        compiler_params=_params(1),
    )(w2, g2, m2, v2)
    return tuple(a.reshape(shape) for a in outs)


def _layer_fwd(li, x, xb, pb, W, comm=None):
    nm = lambda s: f"l{li}_{s}"
    sv = {"x_in_b": xb}
    g1, u1, a1 = _mm_swiglu(xb, W["ffn1_wg"], W["ffn1_wu"], name=nm("ffn1_up"))
    x1, x1b, xh1, rs1 = _mm_ln(a1, W["ffn1_wd"], x, W["ln1_g"], W["ln1_b"], rscale=ALPHA, mscale=0.5, name=nm("ffn1_down_ln"))
    hbuf = _mm(x1b, W["w_in_p"], name=nm("in_proj"))
    ya, lu, lr, lig, la, lh = _lru_fwd(hbuf, W["lru_conv_w"], W["lru_conv_b"], W["lru_wa_bd"], W["lru_ba"],
                                       W["lru_wx_bd"], W["lru_bx"], W["lru_lambda"], name=nm("lru_fwd"))
    eq, ek = _fox_prep(hbuf, W["fox_bf_vec"], name=nm("fox_prep"))
    (yb, lse_rows), comm_out = _fox_fwd(hbuf, eq, ek, comm=comm, name=nm("fox_fwd"))
    yc, yssd, states = _ssd_fwd(hbuf, W["ssd_conv_w"], W["ssd_conv_b"], W["ssd_dtb_vec"], W["ssd_a_vec"],
                                W["ssd_d_exp"], W["ssd_norm_g"], name=nm("ssd_fwd"))
    ymix = jnp.concatenate([ya, yb, yc], axis=1).astype(BF16)
    x2, x2b, xh2, rs2 = _mm_ln(ymix, W["w_out"], x1, W["ln2_g"], W["ln2_b"], rscale=ALPHA, mscale=1.0, name=nm("out_proj_ln"))
    g2, u2, a2 = _mm_swiglu(x2b, W["ffn2_wg"], W["ffn2_wu"], name=nm("ffn2_up"))
    x3, x3b, xh3, rs3 = _mm_ln(a2, W["ffn2_wd"], x2, W["ln3_g"], W["ln3_b"], rscale=ALPHA, mscale=0.5, name=nm("ffn2_down_ln"))
    x4, x4b, sg, e = _mm_pe(x3, x3b, pb, W["pe_gate_w"], W["pe_gate_b"], W["pe_proj"], name=nm("ple"))
    sv.update(g1=g1, u1=u1, a1=a1, x1b=x1b, xh1=xh1, rs1=rs1, hbuf=hbuf, lu=lu, lr=lr, lig=lig, la=la, lh=lh,
              eq=eq, ek=ek, lse_rows=lse_rows, yb=yb, yssd=yssd, states=states, ymix=ymix, x2b=x2b, xh2=xh2, rs2=rs2,
              g2=g2, u2=u2, a2=a2, x3b=x3b, xh3=xh3, rs3=rs3, sg=sg, e=e, pb=pb)
    return x4, x4b, sv, comm_out


def _layer_bwd(li, dx4, sv, W, comm=None):
    nm = lambda s: f"l{li}_{s}"
    G = {}
    dgp, de, dbg = _pe_bwd_elem(dx4, sv["sg"], sv["e"], name=nm("ple_bwd"))
    G["pe_gate_b"] = dbg
    G["pe_gate_w"] = _mm(sv["x3b"], dgp, ta=True, out_dtype=BF16, name=nm("d_pe_gate_w"))
    G["pe_proj"] = _mm(sv["pb"], de, ta=True, out_dtype=BF16, chip_cols=True, name=nm("d_pe_proj"))
    dr3, dr3b, G["ln3_g"], G["ln3_b"] = _bwd_proj([(dgp, W["pe_gate_w"])], dx4, rscale=1.0,
                                                  ln=(sv["xh3"], sv["rs3"], W["ln3_g"]), name=nm("ln3_bwd"))
    G["ffn2_wd"] = _mm(sv["a2"], dr3b, ta=True, scale=0.5, out_dtype=BF16, name=nm("d_ffn2_wd"))
    dg2, du2 = _mm_swiglu_bwd(dr3b, W["ffn2_wd"], sv["g2"], sv["u2"], scale=0.5, name=nm("ffn2_act_bwd"))
    G["ffn2_wg"] = _mm(sv["x2b"], dg2, ta=True, out_dtype=BF16, chip_cols=True, name=nm("d_ffn2_wg"))
    G["ffn2_wu"] = _mm(sv["x2b"], du2, ta=True, out_dtype=BF16, chip_cols=True, name=nm("d_ffn2_wu"))
    dr2, dr2b, G["ln2_g"], G["ln2_b"] = _bwd_proj([(dg2, W["ffn2_wg"]), (du2, W["ffn2_wu"])], dr3, rscale=ALPHA,
                                                  ln=(sv["xh2"], sv["rs2"], W["ln2_g"]), name=nm("ln2_bwd"))
    G["w_out"] = _mm(sv["ymix"], dr2b, ta=True, out_dtype=BF16, name=nm("d_w_out"))
    dymix = _mm(dr2b, W["w_out"], tb=True, name=nm("d_ymix"))
    hbuf = sv["hbuf"]
    (dur, dgr, G["lru_conv_w"], G["lru_conv_b"], G["lru_wa_bd"], G["lru_ba"], G["lru_wx_bd"], G["lru_bx"],
     G["lru_lambda"]) = _lru_bwd(dymix, hbuf, sv["lu"], sv["lr"], sv["lig"], sv["la"], sv["lh"],
                                 W["lru_conv_w"], W["lru_wa_bd"], W["lru_wx_bd"], W["lru_lambda"], name=nm("lru_bwd"))
    delta = _fox_delta(dymix, sv["yb"], name=nm("fox_delta"))
    delta_rows = jnp.pad(delta[:, :ATT_HEADS].T, ((0, SUBLANES - ATT_HEADS), (0, 0)))
    (dk, dv, dfk, dqt, dfq), comm_out = _fox_bwd(hbuf, sv["eq"], sv["ek"], dymix, sv["lse_rows"], delta_rows,
                                                 comm=comm, name=nm("fox_bwd"))
    dq = dqt.T
    dfc = jnp.pad(dfq[:ATT_HEADS].T, ((0, 0), (0, LANES - ATT_HEADS))) - dfk
    dsm_f, G["fox_bf_vec"] = _fox_post(dfc, hbuf, W["fox_bf_vec"], name=nm("fox_post"))
    (dxr, dz, dsm_dt, G["ssd_norm_g"], G["ssd_d_exp"], G["ssd_a_vec"], G["ssd_dtb_vec"], G["ssd_conv_w"],
     G["ssd_conv_b"]) = _ssd_bwd(dymix, hbuf, sv["yssd"], sv["states"], W["ssd_conv_w"], W["ssd_conv_b"],
                                 W["ssd_dtb_vec"], W["ssd_a_vec"], W["ssd_d_exp"], W["ssd_norm_g"], name=nm("ssd_bwd"))
    t = dx4.shape[0]
    dh = jnp.concatenate([dxr.astype(BF16), dz.astype(BF16), dur.astype(BF16), dgr.astype(BF16), dq.astype(BF16),
                          dk.astype(BF16), dv.astype(BF16), (dsm_f + dsm_dt).astype(BF16),
                          jnp.zeros((t, H_WIDTH - COL_SMALL - LANES), BF16)], axis=1)
    G["w_in_p"] = _mm(sv["x1b"], dh, ta=True, name=nm("d_w_in"))
    dr1, dr1b, G["ln1_g"], G["ln1_b"] = _bwd_proj([(dh, W["w_in_p"])], dr2, rscale=ALPHA,
                                                  ln=(sv["xh1"], sv["rs1"], W["ln1_g"]), name=nm("ln1_bwd"))
    G["ffn1_wd"] = _mm(sv["a1"], dr1b, ta=True, scale=0.5, out_dtype=BF16, name=nm("d_ffn1_wd"))
    dg1, du1 = _mm_swiglu_bwd(dr1b, W["ffn1_wd"], sv["g1"], sv["u1"], scale=0.5, name=nm("ffn1_act_bwd"))
    G["ffn1_wg"] = _mm(sv["x_in_b"], dg1, ta=True, out_dtype=BF16, chip_cols=True, name=nm("d_ffn1_wg"))
    G["ffn1_wu"] = _mm(sv["x_in_b"], du1, ta=True, out_dtype=BF16, chip_cols=True, name=nm("d_ffn1_wu"))
    (dx_in,) = _bwd_proj([(dg1, W["ffn1_wg"]), (du1, W["ffn1_wu"])], dr1, rscale=ALPHA, ln=None, name=nm("x_in_bwd"))
    return dx_in, G, comm_out


def _block_diag(w):
    n, b, _ = w.shape
    eye = jnp.eye(n, dtype=w.dtype)
    return (eye[:, None, :, None] * w[:, :, None, :]).reshape(n * b, n * b)


def _block_diag_extract(m):
    n, b = LRU_HEADS, HEAD_DIM
    return jnp.stack([m[b * i:b * (i + 1), b * i:b * (i + 1)] for i in range(n)])


def _lane_vec(v, lane0):
    return jnp.pad(v.astype(F32), (lane0, LANES - lane0 - v.shape[0])).reshape(1, LANES)


def _w_in_permute(w):
    d = w.shape[0]
    z = lambda n: jnp.zeros((d, n), w.dtype)
    return jnp.concatenate([w[:, 1796:2820], w[:, 1284:1796], w[:, 0:512], w[:, 512:1280],
                            w[:, 1280:1284], w[:, 2820:2828], z(LANES - 12), z(H_WIDTH - COL_SMALL - LANES)], axis=1)


def _w_in_unpermute(wp):
    return jnp.concatenate([wp[:, COL_U:COL_Q], wp[:, COL_Q:COL_SMALL], wp[:, COL_SMALL:COL_SMALL + 4],
                            wp[:, COL_Z:COL_U], wp[:, COL_XBC:COL_Z], wp[:, COL_SMALL + 4:COL_SMALL + 12]], axis=1)


def _layer_weights(li, chipw, small):
    g = lambda n: small[n][li]
    W = {n: g(n) for n in ("ln1_g", "ln1_b", "ln2_g", "ln2_b", "ln3_g", "ln3_b", "pe_gate_b", "lru_conv_w",
                           "ssd_conv_w")}
    for n in ("ffn1_wg", "ffn1_wu", "ffn2_wg", "ffn2_wu"):
        W[n] = chipw[n]
    for n in ("ffn1_wd", "ffn2_wd", "w_out", "pe_gate_w"):
        W[n] = chipw[n].reshape(-1, D_MODEL)
    W["pe_proj"] = jnp.moveaxis(chipw["pe_proj"], 0, 1).reshape(PLE_DIM, D_MODEL)
    w_in = jnp.moveaxis(chipw["w_in"][:, :, :IN_WIDTH // N_CHIPS], 0, 1).reshape(D_MODEL, IN_WIDTH)
    W["w_in_p"] = _w_in_permute(w_in)
    for n in ("lru_conv_b", "lru_ba", "lru_bx", "lru_lambda", "ssd_conv_b", "ssd_norm_g"):
        W[n] = g(n).reshape(1, -1)
    W["lru_wa_bd"] = _block_diag(g("lru_wa")).astype(BF16)
    W["lru_wx_bd"] = _block_diag(g("lru_wx")).astype(BF16)
    W["fox_bf_vec"] = _lane_vec(g("fox_bf"), LANE_F)
    W["ssd_dtb_vec"] = _lane_vec(g("ssd_dt_bias"), LANE_DT)
    W["ssd_a_vec"] = _lane_vec(-jnp.exp(g("ssd_a_log")), LANE_DT)
    W["ssd_d_exp"] = jnp.repeat(g("ssd_d"), HEAD_DIM).reshape(1, SSD_WIDTH)
    return W


def _layer_big_grads_by_chip(G):
    out = {n: G[n] for n in ("ffn1_wg", "ffn1_wu", "ffn2_wg", "ffn2_wu", "pe_proj")}
    for n in ("ffn1_wd", "ffn2_wd", "w_out", "pe_gate_w"):
        out[n] = G[n].reshape(N_CHIPS, -1, D_MODEL)
    share = IN_WIDTH // N_CHIPS
    d_w_in = jnp.moveaxis(_w_in_unpermute(G["w_in_p"]).reshape(D_MODEL, N_CHIPS, share), 1, 0)
    out["w_in"] = jnp.pad(d_w_in.astype(BF16), ((0, 0), (0, 0), (0, SHARE - share)))
    return out


def _layer_small_grads(G, W):
    out = {n: G[n] for n in ("lru_conv_w", "ssd_conv_w")}
    for n in ("ln1_g", "ln1_b", "ln2_g", "ln2_b", "ln3_g", "ln3_b", "pe_gate_b", "lru_conv_b", "lru_ba", "lru_bx",
              "lru_lambda", "ssd_conv_b", "ssd_norm_g"):
        out[n] = G[n].reshape(-1)
    out["lru_wa"] = _block_diag_extract(G["lru_wa_bd"])
    out["lru_wx"] = _block_diag_extract(G["lru_wx_bd"])
    out["fox_bf"] = G["fox_bf_vec"][0, LANE_F:LANE_F + ATT_HEADS]
    out["ssd_dt_bias"] = G["ssd_dtb_vec"][0, LANE_DT:LANE_DT + SSD_HEADS]
    out["ssd_a_log"] = G["ssd_a_vec"][0, LANE_DT:LANE_DT + SSD_HEADS] * W["ssd_a_vec"][0, LANE_DT:LANE_DT + SSD_HEADS]
    out["ssd_d"] = G["ssd_d_exp"].reshape(SSD_HEADS, HEAD_DIM).sum(axis=1)
    return out


WEIGHTS = ['ln1_g', 'ln1_b', 'ffn1_wg', 'ffn1_wu', 'ffn1_wd', 'w_in', 'lru_conv_w', 'lru_conv_b', 'lru_wa', 'lru_ba',
           'lru_wx', 'lru_bx', 'lru_lambda', 'fox_bf', 'ssd_conv_w', 'ssd_conv_b', 'ssd_dt_bias', 'ssd_a_log', 'ssd_d',
           'ssd_norm_g', 'w_out', 'ln2_g', 'ln2_b', 'ffn2_wg', 'ffn2_wu', 'ffn2_wd', 'ln3_g', 'ln3_b', 'pe_proj',
           'pe_gate_w', 'pe_gate_b']
CLASSES = (("ffn1_wg", "ffn1_wu", "ffn2_wg", "ffn2_wu", "w_in"),
           ("ffn1_wd", "ffn2_wd"),
           ("w_out", "pe_gate_w"),
           ("pe_proj",))
CLASS_PAD_AXIS = (1, 0, None, None)
BIG = {n: ci for ci, names in enumerate(CLASSES) for n in names}
SMALL_SHARDED = {'lru_conv_w': 2, 'ssd_conv_w': 2}
PACK_COLS = 1024


def _unshard(seg, axis):
    moved = jnp.moveaxis(seg, 0, axis)
    shp = list(moved.shape)
    shp[axis:axis + 2] = [shp[axis] * shp[axis + 1]]
    return moved.reshape(shp)


def _pad_axis(a, axis, size):
    if axis is None or a.shape[axis] == size:
        return a
    pads = [(0, 0)] * a.ndim
    pads[axis] = (0, size - a.shape[axis])
    return jnp.pad(a, pads)


def _pack(arrs, dtype, cols):
    flat = jnp.concatenate([a.astype(dtype).reshape(-1) for a in arrs])
    pad = (-flat.shape[0]) % cols
    if pad:
        flat = jnp.concatenate([flat, jnp.zeros((pad,), dtype)])
    return flat.reshape(-1, cols)


def _unpack(flat, shapes):
    out, off = [], 0
    for s in shapes:
        n = math.prod(s)
        out.append(flat[off:off + n].reshape(s))
        off += n
    return out


def kernel(x, p, ln1_g, ln1_b, ffn1_wg, ffn1_wu, ffn1_wd, w_in, lru_conv_w, lru_conv_b, lru_wa, lru_ba, lru_wx, lru_bx, lru_lambda, fox_bf, ssd_conv_w, ssd_conv_b, ssd_dt_bias, ssd_a_log, ssd_d, ssd_norm_g, w_out, ln2_g, ln2_b, ffn2_wg, ffn2_wu, ffn2_wd, ln3_g, ln3_b, pe_proj, pe_gate_w, pe_gate_b, loss_target, m_ln1_g, m_ln1_b, m_ffn1_wg, m_ffn1_wu, m_ffn1_wd, m_w_in, m_lru_conv_w, m_lru_conv_b, m_lru_wa, m_lru_ba, m_lru_wx, m_lru_bx, m_lru_lambda, m_fox_bf, m_ssd_conv_w, m_ssd_conv_b, m_ssd_dt_bias, m_ssd_a_log, m_ssd_d, m_ssd_norm_g, m_w_out, m_ln2_g, m_ln2_b, m_ffn2_wg, m_ffn2_wu, m_ffn2_wd, m_ln3_g, m_ln3_b, m_pe_proj, m_pe_gate_w, m_pe_gate_b, v_ln1_g, v_ln1_b, v_ffn1_wg, v_ffn1_wu, v_ffn1_wd, v_w_in, v_lru_conv_w, v_lru_conv_b, v_lru_wa, v_lru_ba, v_lru_wx, v_lru_bx, v_lru_lambda, v_fox_bf, v_ssd_conv_w, v_ssd_conv_b, v_ssd_dt_bias, v_ssd_a_log, v_ssd_d, v_ssd_norm_g, v_w_out, v_ln2_g, v_ln2_b, v_ffn2_wg, v_ffn2_wu, v_ffn2_wd, v_ln3_g, v_ln3_b, v_pe_proj, v_pe_gate_w, v_pe_gate_b):
    args = locals()
    w_loc = {n: args[n] for n in WEIGHTS}
    m_loc = {n: args["m_" + n] for n in WEIGHTS}
    v_loc = {n: args["v_" + n] for n in WEIGHTS}
    chip = 2 * lax.axis_index("x") + lax.axis_index("y")
    core = lax.axis_index("c")
    big = list(BIG)
    small_sh = list(SMALL_SHARDED)
    small_rep = [n for n in WEIGHTS if n not in BIG and n not in SMALL_SHARDED]

    def layer_srcs(li):
        return [jnp.stack([_pad_axis(w_loc[n][li].astype(BF16), pad, SHARE) for n in names])
                for names, pad in zip(CLASSES, CLASS_PAD_AXIS)]

    def chip_weights(gathered, srcs):
        out = {}
        for names, g, s in zip(CLASSES, gathered, srcs):
            g = lax.dynamic_update_index_in_dim(g, s, chip, 0)
            for j, n in enumerate(names):
                out[n] = g[:, j]
        return out

    def pair_sums(g_big, li):
        gcls = [jnp.stack([g_big[n] for n in names]) for names in CLASSES]
        flat = [g.reshape((-1,) + g.shape[2:]) for g in gcls]
        theirs = _exchange(flat, ("c",), swap=True, name=f"reduce_cores_l{li}")
        return [_sum_pair(f, r, BF16, name=f"reduce_cores_sum_l{li}_{ci}").reshape(g.shape)
                for ci, (f, r, g) in enumerate(zip(flat, theirs, gcls))]

    def finish_reduce(quad, sums, li):
        quad = [lax.dynamic_update_index_in_dim(q, lax.dynamic_index_in_dim(s, chip, 1, keepdims=False), chip, 0)
                for q, s in zip(quad, sums)]
        red = [_sum_slots(q.reshape(N_CHIPS, -1, q.shape[-1]), F32,
                          name=f"reduce_chips_sum_l{li}_{ci}").reshape(q.shape[1:]) for ci, q in enumerate(quad)]
        theirs = _exchange(red, ("c",), swap=True, name=f"reduce_share_l{li}")
        return [jnp.where(core == li, r, rv) for r, rv in zip(red, theirs)]

    srcs = [layer_srcs(li) for li in range(DEPTH)]
    gathered0 = _run_comm(_gather_layer_comm(srcs[0], 0), name="gather_w_l0")
    small = {n: w_loc[n] for n in small_rep}
    spack = _pack([w_loc[n] for n in small_sh], F32, LANES)
    (sg,) = _exchange([spack[None]], ("x", "y"), name="gather_conv_w")
    for n, seg in zip(small_sh, _unpack_rows(sg.reshape(N_CHIPS, -1), [w_loc[n].shape for n in small_sh])):
        small[n] = _unshard(seg, SMALL_SHARDED[n])

    Ws = [_layer_weights(0, chip_weights(gathered0, srcs[0]), small), None]
    xs = x[0]
    xs, xb, sv0, gathered1 = _layer_fwd(0, xs, xs.astype(BF16), p[0, 0].astype(BF16), Ws[0],
                                        comm=_gather_layer_comm(srcs[1], 1))
    Ws[1] = _layer_weights(1, chip_weights(gathered1, srcs[1]), small)
    xs, _, sv1, _ = _layer_fwd(1, xs, xb, p[1, 0].astype(BF16), Ws[1])
    dx, loss = _loss_kernel(xs, loss_target[0], name="loss")
    loss = lax.psum(loss[0, 0], MESH_AXES)
    dx, G1, _ = _layer_bwd(1, dx, sv1, Ws[1])
    sums1 = pair_sums(_layer_big_grads_by_chip(G1), 1)
    grad_x, G0, quad1 = _layer_bwd(0, dx, sv0, Ws[0], comm=_reduce_chips_comm(sums1, 1))

    red = [None, finish_reduce(quad1, sums1, 1)]
    sums0 = pair_sums(_layer_big_grads_by_chip(G0), 0)
    red[0] = finish_reduce(_run_comm(_reduce_chips_comm(sums0, 0), name="reduce_chips_l0"), sums0, 0)
    g_red = {}
    for ci, names in enumerate(CLASSES):
        for j, n in enumerate(names):
            g = jnp.stack([red[li][ci][j] for li in range(DEPTH)])
            g_red[n] = g[tuple(slice(0, s) for s in w_loc[n].shape)]
    small_l = [_layer_small_grads(G0, Ws[0]), _layer_small_grads(G1, Ws[1])]
    g_small = {n: jnp.stack([small_l[li][n] for li in range(DEPTH)]) for n in small_l[0]}
    small_all = small_rep + small_sh
    sgp = _pack([g_small[n] for n in small_all], F32, PACK_COLS)
    (sall,) = _exchange([sgp[None]], MESH_AXES, name="reduce_small")
    sred = _sum_slots(sall.reshape((2 ** len(MESH_AXES),) + sgp.shape), F32, name="reduce_small_sum").reshape(-1)
    for n, g in zip(small_all, _unpack(sred, [g_small[n].shape for n in small_all])):
        if n in SMALL_SHARDED:
            width = w_loc[n].shape[-1]
            g = lax.dynamic_slice_in_dim(g, chip * width, width, axis=SMALL_SHARDED[n])
        g_red[n] = g

    delta, new_m, new_v = {}, {}, {}
    for n in big:
        delta[n], new_m[n], new_v[n] = _adamw(w_loc[n], g_red[n], m_loc[n], v_loc[n], name="adamw_" + n)
    shapes = [w_loc[n].shape for n in small_all]
    packs = [_pack([d[n] for n in small_all], F32, LANES) for d in (w_loc, g_red, m_loc, v_loc)]
    outs = _adamw(*packs, name="adamw_small")
    for d, o in zip((delta, new_m, new_v), outs):
        for n, a in zip(small_all, _unpack(o.reshape(-1), shapes)):
            d[n] = a
    return (loss, grad_x[None], *[g_red[n] for n in WEIGHTS], *[delta[n] for n in WEIGHTS],
            *[new_m[n] for n in WEIGHTS], *[new_v[n] for n in WEIGHTS])


def _unpack_rows(gathered, shapes):
    out, off = [], 0
    for s in shapes:
        n = math.prod(s)
        out.append(gathered[:, off:off + n].reshape((N_CHIPS,) + tuple(s)))
        off += n
    return out
```

```python
import functools
import math

import jax
import jax.numpy as jnp
from jax import lax
from jax.experimental import pallas as pl
from jax.experimental.pallas import tpu as pltpu

F32 = jnp.float32
BF16 = jnp.bfloat16

D_MODEL = 1024
DEPTH = 2
PLE_DIM = 256
HEAD_DIM = 64
LRU_WIDTH = 256
LRU_HEADS = 4
LRU_C = 8.0
CONV_K = 4
ATT_WIDTH = 256
ATT_HEADS = 4
SSD_WIDTH = 512
SSD_HEADS = 8
SSD_GROUPS = 2
SSD_STATE = 128
SSD_CHUNK = 128
SSD_CONV_DIM = 1024
FFN_DIM = 2816
ALPHA = (2.0 * DEPTH) ** 0.25
LN_EPS = 1e-5
RMS_EPS = 1e-5
IN_WIDTH = 2828
ADAM_LR = 0.001
ADAM_B1 = 0.9
ADAM_B2 = 0.999
ADAM_EPS = 1e-08
ADAM_WD = 0.01
ADAM_STEP = 10

H_WIDTH = 3072
COL_XBC, COL_Z, COL_U, COL_G, COL_Q, COL_K, COL_V, COL_SMALL = 0, 1024, 1536, 1792, 2048, 2304, 2560, 2816
LANE_F = 0
LANE_DT = 4
LANES = 128
SUBLANES = 8
NEG = -1e30

VMEM_LIMIT = 48 * 1024 * 1024

N_CHIPS = 4
MESH_AXES = ("x", "y", "c")
SHARE = 768


def _params(n):
    return pltpu.CompilerParams(dimension_semantics=("arbitrary",) * n, vmem_limit_bytes=VMEM_LIMIT)


def _pick(n, cands):
    for c in cands:
        if n % c == 0:
            return c
    return n


def _iota(shape, dim):
    return lax.broadcasted_iota(jnp.int32, shape, dim)


def _shift_down(x, s, prev8):
    if s == 0:
        return x
    r = pltpu.roll(x, s, 0)
    pr = pltpu.roll(prev8, s, 0)
    head = jnp.where(_iota(pr.shape, 0) < s, pr, r[:SUBLANES])
    return jnp.concatenate([head, r[SUBLANES:]], axis=0)


def _shift_up(x, s, next8):
    if s == 0:
        return x
    n = x.shape[0]
    r = pltpu.roll(x, n - s, 0)
    nr = pltpu.roll(next8, SUBLANES - s, 0)
    tail = jnp.where(_iota(nr.shape, 0) >= SUBLANES - s, nr, r[n - SUBLANES:])
    return jnp.concatenate([r[:n - SUBLANES], tail], axis=0)


def _scan_fwd(a, b):
    n = a.shape[0]
    row = _iota(a.shape, 0)
    d = 1
    while d < n:
        keep = row >= d
        a_s = jnp.where(keep, pltpu.roll(a, d, 0), 1.0)
        b_s = jnp.where(keep, pltpu.roll(b, d, 0), 0.0)
        b = a * b_s + b
        a = a * a_s
        d *= 2
    return a, b


def _scan_bwd(a, b):
    n = a.shape[0]
    row = _iota(a.shape, 0)
    d = 1
    while d < n:
        keep = row < n - d
        a_s = jnp.where(keep, pltpu.roll(a, n - d, 0), 1.0)
        b_s = jnp.where(keep, pltpu.roll(b, n - d, 0), 0.0)
        b = a * b_s + b
        a = a * a_s
        d *= 2
    return a, b


def _cumsum_rows(x, reverse=False):
    n = x.shape[0]
    row = _iota(x.shape, 0)
    d = 1
    while d < n:
        if reverse:
            x = x + jnp.where(row < n - d, pltpu.roll(x, n - d, 0), 0.0)
        else:
            x = x + jnp.where(row >= d, pltpu.roll(x, d, 0), 0.0)
        d *= 2
    return x


def _col(x, lane):
    return jnp.sum(jnp.where(_iota(x.shape, 1) == lane, x, 0.0), axis=1, keepdims=True)


def _row(x, r):
    return jnp.sum(jnp.where(_iota(x.shape, 0) == r, x, 0.0), axis=0, keepdims=True)


def _sigmoid(x):
    return jax.nn.sigmoid(x)


def _softplus(x):
    return jnp.maximum(x, 0.0) + jnp.log(1.0 + jnp.exp(-jnp.abs(x)))


def _gelu_and_grad(x):
    c0 = math.sqrt(2.0 / math.pi)
    inner = c0 * (x + 0.044715 * x * x * x)
    t = jnp.tanh(inner)
    g = 0.5 * x * (1.0 + t)
    dg = 0.5 * (1.0 + t) + 0.5 * x * (1.0 - t * t) * c0 * (1.0 + 3.0 * 0.044715 * x * x)
    return g, dg


def _dot(a, b, ca, cb):
    return lax.dot_general(a, b, (((ca,), (cb,)), ((), ())), preferred_element_type=F32)


def _conv_taps(xr, prev8, w, bias):
    y = bias + w[CONV_K - 1:CONV_K, :] * xr
    for j in range(CONV_K - 1):
        y = y + w[j:j + 1, :] * _shift_down(xr, CONV_K - 1 - j, prev8)
    return y


def _conv_taps_bwd(dy, next8, w, xr):
    dx = None
    dws = []
    for j in range(CONV_K):
        sh = _shift_up(dy, CONV_K - 1 - j, next8)
        term = w[j:j + 1, :] * sh
        dx = term if dx is None else dx + term
        dws.append(jnp.sum(sh * xr, axis=0, keepdims=True))
    return dx, jnp.concatenate(dws, axis=0)


def _head_expand(v, lane0, nheads, width):
    rows = v.shape[0]
    colhead = _iota((rows, width), 1) // HEAD_DIM
    out = jnp.zeros((rows, width), F32)
    for h in range(nheads):
        out = jnp.where(colhead == h, _col(v, lane0 + h), out)
    return out


def _head_reduce(x, lane0, nheads):
    rows = x.shape[0]
    colhead = _iota(x.shape, 1) // HEAD_DIM
    lane = _iota((rows, LANES), 1)
    out = jnp.zeros((rows, LANES), F32)
    for h in range(nheads):
        s = jnp.sum(jnp.where(colhead == h, x, 0.0), axis=1, keepdims=True)
        out = jnp.where(lane == lane0 + h, s, out)
    return out


def _mm(a, b, *, ta=False, tb=False, scale=1.0, out_dtype=F32, chip_cols=False, name):
    if ta:
        kk, m = a.shape
    else:
        m, kk = a.shape
    n = b.shape[0] if tb else b.shape[1]
    tm = _pick(m, (1024, 512, 256, 128))
    tn = _pick(n // N_CHIPS, (768, 256, 128)) if chip_cols else _pick(n, (1024, 768, 512, 256, 128))
    tk = _pick(kk, (1024, 768, 512, 256, 128))
    nk = kk // tk
    dn_a = 0 if ta else 1
    dn_b = 1 if tb else 0
    if chip_cols:
        per = n // N_CHIPS // tn
        out_spec = pl.BlockSpec((None, tm, tn), lambda i, j, k: (j // per, i, j % per))
        out_shape = jax.ShapeDtypeStruct((N_CHIPS, m, n // N_CHIPS), out_dtype)
    else:
        out_spec = pl.BlockSpec((tm, tn), lambda i, j, k: (i, j))
        out_shape = jax.ShapeDtypeStruct((m, n), out_dtype)

    def body(a_ref, b_ref, o_ref, acc):
        k = pl.program_id(2)

        @pl.when(k == 0)
        def _():
            acc[...] = jnp.zeros_like(acc)

        acc[...] += _dot(a_ref[...].astype(BF16), b_ref[...].astype(BF16), dn_a, dn_b)

        @pl.when(k == nk - 1)
        def _():
            o_ref[...] = (acc[...] * scale).astype(out_dtype)

    a_spec = pl.BlockSpec((tk, tm), lambda i, j, k: (k, i)) if ta else pl.BlockSpec((tm, tk), lambda i, j, k: (i, k))
    b_spec = pl.BlockSpec((tn, tk), lambda i, j, k: (j, k)) if tb else pl.BlockSpec((tk, tn), lambda i, j, k: (k, j))
    return pl.pallas_call(
        body, name=name, grid=(m // tm, n // tn, nk),
        in_specs=[a_spec, b_spec],
        out_specs=out_spec, out_shape=out_shape,
        scratch_shapes=[pltpu.VMEM((tm, tn), F32)],
        compiler_params=_params(3),
    )(a, b)


def _mm_swiglu(xb, wg, wu, *, comm=None, name):
    t, d = xb.shape
    share = wg.shape[2]
    n = N_CHIPS * share
    tm = _pick(t, (512, 256, 128))
    tn = _pick(share, (768, 256, 128))
    per = share // tn

    def body(x_ref, wg_ref, wu_ref, g_ref, u_ref, a_ref):
        x = x_ref[...]
        g = _dot(x, wg_ref[...], 1, 0)
        u = _dot(x, wu_ref[...], 1, 0)
        g_ref[...] = g.astype(BF16)
        u_ref[...] = u.astype(BF16)
        a_ref[...] = (g * _sigmoid(g) * u).astype(BF16)

    o = jax.ShapeDtypeStruct((t, n), BF16)
    ospec = pl.BlockSpec((tm, tn), lambda j, i: (i, j))
    return _hosted_call(
        body, comm, (n // tn, t // tm), name=name,
        in_specs=[pl.BlockSpec((tm, d), lambda j, i: (i, 0)),
                  pl.BlockSpec((None, d, tn), lambda j, i: (j // per, 0, j % per)),
                  pl.BlockSpec((None, d, tn), lambda j, i: (j // per, 0, j % per))],
        out_specs=[ospec, ospec, ospec], out_shape=[o, o, o], scratch_shapes=[], args=[xb, wg, wu])


def _mm_swiglu_bwd(dr, wd, g, u, *, scale, name):
    t, d = dr.shape
    n = wd.shape[0]
    tm = _pick(t, (512, 256, 128))
    tn = _pick(n, (768, 256, 128))

    def body(dr_ref, wd_ref, g_ref, u_ref, dg_ref, du_ref):
        da = _dot(dr_ref[...].astype(BF16), wd_ref[...], 1, 1) * scale
        gg = g_ref[...].astype(F32)
        uu = u_ref[...].astype(F32)
        sg = _sigmoid(gg)
        dg_ref[...] = (da * uu * (sg * (1.0 + gg * (1.0 - sg)))).astype(BF16)
        du_ref[...] = (da * gg * sg).astype(BF16)

    o = jax.ShapeDtypeStruct((t, n), BF16)
    ospec = pl.BlockSpec((tm, tn), lambda j, i: (i, j))
    return pl.pallas_call(
        body, name=name, grid=(n // tn, t // tm),
        in_specs=[pl.BlockSpec((tm, d), lambda j, i: (i, 0)),
                  pl.BlockSpec((tn, d), lambda j, i: (j, 0)),
                  ospec, ospec],
        out_specs=[ospec, ospec], out_shape=[o, o],
        compiler_params=_params(2),
    )(dr, wd, g, u)


def _mm_ln(a, w, resid, gain, bias, *, rscale, mscale, name):
    t, kk = a.shape
    d = w.shape[1]
    tm = _pick(t, (512, 256, 128))
    tk = kk
    nk = kk // tk

    def body(a_ref, w_ref, r_ref, g_ref, b_ref, y_ref, yb_ref, xh_ref, rs_ref, acc):
        k = pl.program_id(1)

        @pl.when(k == 0)
        def _():
            acc[...] = jnp.zeros_like(acc)

        acc[...] += _dot(a_ref[...].astype(BF16), w_ref[...], 1, 0)

        @pl.when(k == nk - 1)
        def _():
            r = rscale * r_ref[...] + mscale * acc[...]
            mu = jnp.mean(r, axis=1, keepdims=True)
            xc = r - mu
            var = jnp.mean(xc * xc, axis=1, keepdims=True)
            rstd = lax.rsqrt(var + LN_EPS)
            xh = xc * rstd
            y = xh * g_ref[...] + b_ref[...]
            y_ref[...] = y
            yb_ref[...] = y.astype(BF16)
            xh_ref[...] = xh
            rs_ref[...] = rstd

    row = pl.BlockSpec((tm, d), lambda i, k: (i, 0))
    vec = pl.BlockSpec((1, d), lambda i, k: (0, 0))
    return pl.pallas_call(
        body, name=name, grid=(t // tm, nk),
        in_specs=[pl.BlockSpec((tm, tk), lambda i, k: (i, k)),
                  pl.BlockSpec((tk, d), lambda i, k: (k, 0)), row, vec, vec],
        out_specs=[row, row, row, pl.BlockSpec((tm, 1), lambda i, k: (i, 0))],
        out_shape=[jax.ShapeDtypeStruct((t, d), F32), jax.ShapeDtypeStruct((t, d), BF16),
                   jax.ShapeDtypeStruct((t, d), F32), jax.ShapeDtypeStruct((t, 1), F32)],
        scratch_shapes=[pltpu.VMEM((tm, d), F32)],
        compiler_params=_params(2),
    )(a, w, resid, gain.reshape(1, d), bias.reshape(1, d))


def _bwd_proj(pairs, resid, *, rscale, ln, name):
    t, kk = pairs[0][0].shape
    d = pairs[0][1].shape[-2]
    tm = _pick(t, (512, 256, 128))
    tk = _pick(pairs[0][1].shape[-1], (1024, 768, 512, 256, 128))
    nk = kk // tk
    nt = t // tm
    npair = len(pairs)
    has_ln = ln is not None

    def body(*refs):
        ab = refs[:2 * npair]
        r_ref = refs[2 * npair]
        pos = 2 * npair + 1
        if has_ln:
            xh_ref, rs_ref, g_ref = refs[pos:pos + 3]
            pos += 3
            o_ref, ob_ref, dg_ref, db_ref = refs[pos:pos + 4]
            pos += 4
        else:
            o_ref = refs[pos]
            pos += 1
        acc = refs[pos]
        i = pl.program_id(0)
        k = pl.program_id(1)

        @pl.when(k == 0)
        def _():
            acc[...] = jnp.zeros_like(acc)

        for q in range(npair):
            acc[...] += _dot(ab[2 * q][...].astype(BF16), ab[2 * q + 1][...], 1, 1)

        @pl.when(k == nk - 1)
        def _():
            dy = rscale * r_ref[...] + acc[...]
            if not has_ln:
                o_ref[...] = dy
                return
            xh = xh_ref[...]
            w = dy * g_ref[...]
            m1 = jnp.mean(w, axis=1, keepdims=True)
            m2 = jnp.mean(w * xh, axis=1, keepdims=True)
            dr = rs_ref[...] * (w - m1 - xh * m2)
            o_ref[...] = dr
            ob_ref[...] = dr.astype(BF16)

            @pl.when(i == 0)
            def _():
                dg_ref[...] = jnp.zeros_like(dg_ref)
                db_ref[...] = jnp.zeros_like(db_ref)

            dg_ref[...] += jnp.sum(dy * xh, axis=0, keepdims=True)
            db_ref[...] += jnp.sum(dy, axis=0, keepdims=True)

    row = pl.BlockSpec((tm, d), lambda i, k: (i, 0))
    vec = pl.BlockSpec((1, d), lambda i, k: (0, 0))
    in_specs, args = [], []
    for a, b in pairs:
        if b.ndim == 3:
            per = b.shape[2] // tk
            b_spec = pl.BlockSpec((None, d, tk), lambda i, k, per=per: (k // per, 0, k % per))
        else:
            b_spec = pl.BlockSpec((d, tk), lambda i, k: (0, k))
        in_specs += [pl.BlockSpec((tm, tk), lambda i, k: (i, k)), b_spec]
        args += [a, b]
    in_specs.append(row)
    args.append(resid)
    out_specs = [row]
    out_shape = [jax.ShapeDtypeStruct((t, d), F32)]
    if has_ln:
        xh, rs, gain = ln
        in_specs += [row, pl.BlockSpec((tm, 1), lambda i, k: (i, 0)), vec]
        args += [xh, rs, gain.reshape(1, d)]
        out_specs += [row, vec, vec]
        out_shape += [jax.ShapeDtypeStruct((t, d), BF16)] + [jax.ShapeDtypeStruct((1, d), F32)] * 2
    return pl.pallas_call(
        body, name=name, grid=(nt, nk), in_specs=in_specs, out_specs=out_specs, out_shape=out_shape,
        scratch_shapes=[pltpu.VMEM((tm, d), F32)],
        compiler_params=_params(2),
    )(*args)


def _mm_pe(x3, x3b, pb, wgate, bgate, wproj, *, name):
    t, d = x3.shape
    pd = pb.shape[1]
    tm = _pick(t, (512, 256, 128))
    tn = _pick(d, (512, 256, 128))

    def body(x_ref, xb_ref, p_ref, wg_ref, bg_ref, wp_ref, y_ref, yb_ref, sg_ref, e_ref):
        sg = _sigmoid(_dot(xb_ref[...], wg_ref[...], 1, 0) + bg_ref[...])
        e = _dot(p_ref[...], wp_ref[...], 1, 0)
        y = x_ref[...] + sg * e
        y_ref[...] = y
        yb_ref[...] = y.astype(BF16)
        sg_ref[...] = sg.astype(BF16)
        e_ref[...] = e.astype(BF16)

    ospec = pl.BlockSpec((tm, tn), lambda i, j: (i, j))
    ob = jax.ShapeDtypeStruct((t, d), BF16)
    return pl.pallas_call(
        body, name=name, grid=(t // tm, d // tn),
        in_specs=[ospec, pl.BlockSpec((tm, d), lambda i, j: (i, 0)), pl.BlockSpec((tm, pd), lambda i, j: (i, 0)),
                  pl.BlockSpec((d, tn), lambda i, j: (0, j)), pl.BlockSpec((1, tn), lambda i, j: (0, j)),
                  pl.BlockSpec((pd, tn), lambda i, j: (0, j))],
        out_specs=[ospec, ospec, ospec, ospec],
        out_shape=[jax.ShapeDtypeStruct((t, d), F32), ob, ob, ob],
        compiler_params=_params(2),
    )(x3, x3b, pb, wgate, bgate.reshape(1, d), wproj)


def _pe_bwd_elem(dx4, sg, e, *, name):
    t, d = dx4.shape
    tm = _pick(t, (512, 256, 128))

    def body(dx_ref, sg_ref, e_ref, dgp_ref, de_ref, db_ref):
        dx = dx_ref[...]
        s = sg_ref[...].astype(F32)
        dgp = dx * e_ref[...].astype(F32) * s * (1.0 - s)
        dgp_ref[...] = dgp.astype(BF16)
        de_ref[...] = (dx * s).astype(BF16)

        @pl.when(pl.program_id(0) == 0)
        def _():
            db_ref[...] = jnp.zeros_like(db_ref)

        db_ref[...] += jnp.sum(dgp, axis=0, keepdims=True)

    row = pl.BlockSpec((tm, d), lambda i: (i, 0))
    ob = jax.ShapeDtypeStruct((t, d), BF16)
    return pl.pallas_call(
        body, name=name, grid=(t // tm,), in_specs=[row, row, row],
        out_specs=[row, row, pl.BlockSpec((1, d), lambda i: (0, 0))],
        out_shape=[ob, ob, jax.ShapeDtypeStruct((1, d), F32)],
        compiler_params=_params(1),
    )(dx4, sg, e)


def _loss_kernel(y, target, *, name):
    t, d = y.shape
    tm = _pick(t, (512, 256, 128))

    def body(y_ref, t_ref, dy_ref, l_ref):
        diff = y_ref[...] - t_ref[...]
        dy_ref[...] = diff * (1.0 / d)

        @pl.when(pl.program_id(0) == 0)
        def _():
            l_ref[...] = jnp.zeros_like(l_ref)

        part = jnp.sum(jnp.mean(diff * diff, axis=1, keepdims=True), axis=0, keepdims=True)
        l_ref[...] += 0.5 * part

    row = pl.BlockSpec((tm, d), lambda i: (i, 0))
    return pl.pallas_call(
        body, name=name, grid=(t // tm,), in_specs=[row, row],
        out_specs=[row, pl.BlockSpec((1, 1), lambda i: (0, 0))],
        out_shape=[jax.ShapeDtypeStruct((t, d), F32), jax.ShapeDtypeStruct((1, 1), F32)],
        compiler_params=_params(1),
    )(y, target)


LRU_TM = 256


def _lru_gate_terms(r, lam):
    sp = _softplus(-lam)
    la = -LRU_C * r * sp
    a = jnp.exp(la)
    em = jnp.tanh(la) * (jnp.exp(2.0 * la) + 1.0)
    s = jnp.sqrt(-em)
    return la, a, s, sp


def _lru_fwd(hbuf, conv_w, conv_b, wa, ba, wx, bx, lam, *, name):
    t = hbuf.shape[0]
    w = LRU_WIDTH
    tm = _pick(t, (LRU_TM, 128))
    cu, cg = COL_U // w, COL_G // w
    hb = tm // SUBLANES

    def body(u_ref, up_ref, g_ref, cw_ref, cb_ref, wa_ref, ba_ref, wx_ref, bx_ref, lam_ref,
             y_ref, u_out, r_out, i_out, a_out, h_out, carry):
        i = pl.program_id(0)

        @pl.when(i == 0)
        def _():
            carry[...] = jnp.zeros_like(carry)

        prev = jnp.where(i == 0, 0.0, up_ref[...])
        u = _conv_taps(u_ref[...], prev, cw_ref[...], cb_ref[...])
        ub = u.astype(BF16)
        r = _sigmoid(_dot(ub, wa_ref[...], 1, 0) + ba_ref[...])
        ig = _sigmoid(_dot(ub, wx_ref[...], 1, 0) + bx_ref[...])
        _, a, s, _ = _lru_gate_terms(r, lam_ref[...])
        b = s * (ig * u)
        acum, hs = _scan_fwd(a, b)
        h = hs + acum * carry[0:1, :]
        carry[...] = jnp.broadcast_to(h[tm - 1:tm, :], carry.shape)
        gl, _ = _gelu_and_grad(g_ref[...])
        y_ref[...] = h * gl
        u_out[...] = u
        r_out[...] = r
        i_out[...] = ig
        a_out[...] = a
        h_out[...] = h

    row = pl.BlockSpec((tm, w), lambda i: (i, 0))
    vec = pl.BlockSpec((1, w), lambda i: (0, 0))
    mat = pl.BlockSpec((w, w), lambda i: (0, 0))
    o = jax.ShapeDtypeStruct((t, w), F32)
    return pl.pallas_call(
        body, name=name, grid=(t // tm,),
        in_specs=[pl.BlockSpec((tm, w), lambda i: (i, cu)),
                  pl.BlockSpec((SUBLANES, w), lambda i: (jnp.maximum(i * hb - 1, 0), cu)),
                  pl.BlockSpec((tm, w), lambda i: (i, cg)),
                  pl.BlockSpec((CONV_K, w), lambda i: (0, 0)), vec, mat, vec, mat, vec, vec],
        out_specs=[row] * 6, out_shape=[o] * 6,
        scratch_shapes=[pltpu.VMEM((SUBLANES, w), F32)],
        compiler_params=_params(1),
    )(hbuf, hbuf, hbuf, conv_w, conv_b, wa, ba, wx, bx, lam)


def _lru_bwd(dymix, hbuf, u, r, ig, a, h, conv_w, wa, wx, lam, *, name):
    t = hbuf.shape[0]
    w = LRU_WIDTH
    tm = _pick(t, (LRU_TM, 128))
    nb = t // tm
    cu, cg = COL_U // w, COL_G // w
    hb = tm // SUBLANES
    last8 = t // SUBLANES - 1

    def body(dy_ref, ur_ref, g_ref, u_ref, r_ref, i_ref, a_ref, an_ref, h_ref, hp_ref,
             cw_ref, wa_ref, wx_ref, lam_ref,
             dur_ref, dgr_ref, dcw_ref, dcb_ref, dwa_ref, dba_ref, dwx_ref, dbx_ref, dlam_ref,
             lcarry, dnext):
        i = pl.program_id(0)
        ib = nb - 1 - i

        @pl.when(i == 0)
        def _():
            lcarry[...] = jnp.zeros_like(lcarry)
            dnext[...] = jnp.zeros_like(dnext)
            for ref in (dcw_ref, dcb_ref, dwa_ref, dba_ref, dwx_ref, dbx_ref, dlam_ref):
                ref[...] = jnp.zeros_like(ref)

        dy = dy_ref[...]
        hh = h_ref[...]
        av = a_ref[...]
        uu = u_ref[...]
        rr = r_ref[...]
        ii = i_ref[...]
        lam_v = lam_ref[...]
        gl, dgl = _gelu_and_grad(g_ref[...])
        dgr_ref[...] = (dy * hh * dgl).astype(BF16)
        dh_out = dy * gl
        a_next = _shift_up(av, 1, jnp.where(ib == nb - 1, 0.0, an_ref[...]))
        acum, ls = _scan_bwd(a_next, dh_out)
        lam_adj = ls + acum * lcarry[0:1, :]
        lcarry[...] = jnp.broadcast_to(lam_adj[0:1, :], lcarry.shape)
        h_prev = _shift_down(hh, 1, jnp.where(ib == 0, 0.0, hp_ref[...]))
        da = lam_adj * h_prev
        _, a2, s, sp = _lru_gate_terms(rr, lam_v)
        d_igu = lam_adj * s
        ds = lam_adj * ii * uu
        dla = da * a2 - ds * (a2 * a2) / s
        dr = dla * (-LRU_C * sp)
        dlam_ref[...] += jnp.sum(dla * (LRU_C * rr * _sigmoid(-lam_v)), axis=0, keepdims=True)
        dpre_r = dr * rr * (1.0 - rr)
        dpre_i = d_igu * uu * ii * (1.0 - ii)
        prb = dpre_r.astype(BF16)
        pib = dpre_i.astype(BF16)
        ub = uu.astype(BF16)
        du = d_igu * ii + _dot(prb, wa_ref[...], 1, 1) + _dot(pib, wx_ref[...], 1, 1)
        dwa_ref[...] += _dot(ub, prb, 0, 0)
        dwx_ref[...] += _dot(ub, pib, 0, 0)
        dba_ref[...] += jnp.sum(dpre_r, axis=0, keepdims=True)
        dbx_ref[...] += jnp.sum(dpre_i, axis=0, keepdims=True)
        dur, dws = _conv_taps_bwd(du, dnext[...], cw_ref[...], ur_ref[...])
        dur_ref[...] = dur.astype(BF16)
        dcw_ref[...] += dws
        dcb_ref[...] += jnp.sum(du, axis=0, keepdims=True)
        dnext[...] = du[:SUBLANES]

    def rowspec(col):
        return pl.BlockSpec((tm, w), lambda i: (nb - 1 - i, col))

    row = rowspec(0)
    nxt = pl.BlockSpec((SUBLANES, w), lambda i: (jnp.minimum((nb - i) * hb, last8), 0))
    prv = pl.BlockSpec((SUBLANES, w), lambda i: (jnp.maximum((nb - 1 - i) * hb - 1, 0), 0))
    vec = pl.BlockSpec((1, w), lambda i: (0, 0))
    mat = pl.BlockSpec((w, w), lambda i: (0, 0))
    cw = pl.BlockSpec((CONV_K, w), lambda i: (0, 0))
    o = jax.ShapeDtypeStruct((t, w), BF16)
    v1 = jax.ShapeDtypeStruct((1, w), F32)
    m1 = jax.ShapeDtypeStruct((w, w), F32)
    return pl.pallas_call(
        body, name=name, grid=(nb,),
        in_specs=[rowspec(0), rowspec(cu), rowspec(cg), row, row, row, row, nxt, row, prv, cw, mat, mat, vec],
        out_specs=[row, row, cw, vec, mat, vec, mat, vec, vec],
        out_shape=[o, o, jax.ShapeDtypeStruct((CONV_K, w), F32), v1, m1, v1, m1, v1, v1],
        scratch_shapes=[pltpu.VMEM((SUBLANES, w), F32), pltpu.VMEM((SUBLANES, w), F32)],
        compiler_params=_params(1),
    )(dymix, hbuf, hbuf, u, r, ig, a, a, h, h, conv_w, wa, wx, lam)


FOX_T = 512
FOX_PREP_TM = 256


def _log_sigmoid(x):
    return jnp.minimum(x, 0.0) - jnp.log(1.0 + jnp.exp(-jnp.abs(x)))


def _fox_prep(hbuf, bf_vec, *, name):
    t = hbuf.shape[0]
    tm = _pick(t, (FOX_PREP_TM, 128))
    cs = COL_SMALL // LANES

    def body(s_ref, b_ref, eq_ref, ek_ref, carry):
        i = pl.program_id(0)

        @pl.when(i == 0)
        def _():
            carry[...] = jnp.zeros_like(carry)

        lf = _log_sigmoid(s_ref[...] + b_ref[...])
        f = _cumsum_rows(lf) + carry[0:1, :]
        carry[...] = jnp.broadcast_to(f[tm - 1:tm, :], carry.shape)
        lane = _iota((tm, LANES), 1)
        for h in range(ATT_HEADS):
            base = HEAD_DIM * (1 - h % 2)
            fh = _col(f, h)
            hi = fh.astype(BF16).astype(F32)
            mid = (fh - hi).astype(BF16).astype(F32)
            lo = fh - hi - mid
            terms = jnp.where(lane == base, hi, jnp.where(lane == base + 1, mid, jnp.where(lane == base + 2, lo, 0.0)))
            terms_k = jnp.where(lane == base + 3, -hi,
                                jnp.where(lane == base + 4, -mid, jnp.where(lane == base + 5, -lo, 0.0)))
            ones_q = ((lane >= base + 3) & (lane < base + 6)).astype(F32)
            ones_k = ((lane >= base) & (lane < base + 3)).astype(F32)
            eq_ref[:, LANES * h:LANES * (h + 1)] = (terms + ones_q).astype(BF16)
            ek_ref[:, LANES * h:LANES * (h + 1)] = (terms_k + ones_k).astype(BF16)

    ospec = pl.BlockSpec((tm, ATT_HEADS * LANES), lambda i: (i, 0))
    o = jax.ShapeDtypeStruct((t, ATT_HEADS * LANES), BF16)
    return pl.pallas_call(
        body, name=name, grid=(t // tm,),
        in_specs=[pl.BlockSpec((tm, LANES), lambda i: (i, cs)), pl.BlockSpec((1, LANES), lambda i: (0, 0))],
        out_specs=[ospec, ospec], out_shape=[o, o],
        scratch_shapes=[pltpu.VMEM((SUBLANES, LANES), F32)],
        compiler_params=_params(1),
    )(hbuf, bf_vec)


def _fox_post(dfc, hbuf, bf_vec, *, name):
    t = hbuf.shape[0]
    tm = _pick(t, (FOX_PREP_TM, 128))
    nb = t // tm
    cs = COL_SMALL // LANES

    def body(df_ref, s_ref, b_ref, o_ref, db_ref, carry):
        i = pl.program_id(0)

        @pl.when(i == 0)
        def _():
            carry[...] = jnp.zeros_like(carry)
            db_ref[...] = jnp.zeros_like(db_ref)

        dlf = _cumsum_rows(df_ref[...], reverse=True) + carry[0:1, :]
        carry[...] = jnp.broadcast_to(dlf[0:1, :], carry.shape)
        dl = dlf * _sigmoid(-(s_ref[...] + b_ref[...]))
        dl = jnp.where(_iota(dl.shape, 1) < ATT_HEADS, dl, 0.0)
        o_ref[...] = dl
        db_ref[...] += jnp.sum(dl, axis=0, keepdims=True)

    vec = pl.BlockSpec((1, LANES), lambda i: (0, 0))
    return pl.pallas_call(
        body, name=name, grid=(nb,),
        in_specs=[pl.BlockSpec((tm, LANES), lambda i: (nb - 1 - i, 0)),
                  pl.BlockSpec((tm, LANES), lambda i: (nb - 1 - i, cs)), vec],
        out_specs=[pl.BlockSpec((tm, LANES), lambda i: (nb - 1 - i, 0)), vec],
        out_shape=[jax.ShapeDtypeStruct((t, LANES), F32), jax.ShapeDtypeStruct((1, LANES), F32)],
        scratch_shapes=[pltpu.VMEM((SUBLANES, LANES), F32)],
        compiler_params=_params(1),
    )(dfc, hbuf, bf_vec)


def _fox_masks(i, j, tq):
    row = i * tq + _iota((tq, tq), 0)
    col = j * tq + _iota((tq, tq), 1)
    lane = _iota((1, LANES), 1)
    return col <= row, (lane < HEAD_DIM, lane >= HEAD_DIM)


def _hosting(body, n_in, n_out, n_scratch, comm, grid):
    na, no = len(comm.arrays), len(comm.out_shapes)

    def hosted(*refs):
        o0 = n_in + na
        s0 = o0 + n_out + no
        cargs = (refs[n_in:o0], refs[o0 + n_out:s0]) + tuple(refs[s0 + n_scratch:])
        a, b = pl.program_id(0), pl.program_id(1)

        @pl.when((a == 0) & (b == 0))
        def _():
            comm.start(*cargs)

        body(*refs[:n_in], *refs[o0:o0 + n_out], *refs[s0:s0 + n_scratch])

        @pl.when((a == grid[0] - 1) & (b == grid[1] - 1))
        def _():
            comm.finish(*cargs)

    return hosted


def _hosted_call(body, comm, grid, *, name, in_specs, out_specs, out_shape, scratch_shapes, args):
    n_out = len(out_shape)
    if comm is not None:
        cin, cout, sems = comm.specs()
        body = _hosting(body, len(in_specs), n_out, len(scratch_shapes), comm, grid)
        in_specs, out_specs = in_specs + cin, out_specs + cout
        out_shape, scratch_shapes, args = out_shape + comm.out_shapes, scratch_shapes + sems, args + list(comm.arrays)
    outs = pl.pallas_call(body, name=name, grid=grid, in_specs=in_specs, out_specs=out_specs,
                          out_shape=out_shape, scratch_shapes=scratch_shapes, compiler_params=_params(2))(*args)
    return outs[:n_out], outs[n_out:]


def _merge_comms(comms):
    comms = [c for c in comms if c is not None]
    if len(comms) <= 1:
        return comms[0] if comms else None

    def both(which):
        def run(ins, outs, ssem, rsem):
            ia = io = 0
            for c in comms:
                na, no = len(c.arrays), len(c.out_shapes)
                getattr(c, which)(ins[ia:ia + na], outs[io:io + no], ssem, rsem)
                ia, io = ia + na, io + no
        return run

    spans = sorted((c.base, c.base + c.n_own) for c in comms)
    assert all(a[1] <= b[0] for a, b in zip(spans, spans[1:])), "semaphore ranges overlap"
    return _Comm(sum((list(c.arrays) for c in comms), []), sum((list(c.out_shapes) for c in comms), []),
                 spans[-1][1], both("start"), both("finish"))


def _fox_fwd(hbuf, eq, ek, *, comm=None, name):
    t = hbuf.shape[0]
    w = ATT_WIDTH
    tq = _pick(t, (FOX_T, 256, 128))
    nq = t // tq
    cq, ck, cv = COL_Q // w, COL_K // w, COL_V // w

    def body(q_ref, k_ref, v_ref, eq_ref, ek_ref, o_ref, lse_ref, m_s, l_s, acc_s):
        i = pl.program_id(0)
        j = pl.program_id(1)

        @pl.when(j == 0)
        def _():
            m_s[...] = jnp.full_like(m_s, NEG)
            l_s[...] = jnp.zeros_like(l_s)
            acc_s[...] = jnp.zeros_like(acc_s)

        def step(diagonal):
            _, hms = _fox_masks(i, j, tq)
            keys_first = (j * tq + _iota((tq, tq), 0)) <= (i * tq + _iota((tq, tq), 1))
            half = _iota((LANES, 1), 0)
            hrows = (half < HEAD_DIM, half >= HEAD_DIM)
            m_all = m_s[...]
            l_all = l_s[...]
            acc_old = [acc_s[LANES * pr:LANES * (pr + 1), :] for pr in range(2)]
            m_out, l_out, acc_out = [], [], []
            for pr in range(2):
                sl = slice(LANES * pr, LANES * (pr + 1))
                qp = q_ref[:, sl]
                kp = k_ref[:, sl]
                vt = v_ref[:, sl].T.astype(BF16)
                acc = acc_old[pr]
                for hh in range(2):
                    h = 2 * pr + hh
                    hsl = slice(LANES * h, LANES * (h + 1))
                    qm = jnp.where(hms[hh], (qp * (HEAD_DIM ** -0.5)).astype(BF16), eq_ref[:, hsl])
                    km = jnp.where(hms[hh], kp.astype(BF16), ek_ref[:, hsl])
                    st = _dot(km, qm, 1, 1)
                    if diagonal:
                        st = jnp.where(keys_first, st, NEG)
                    m_old = m_all[h:h + 1, :]
                    m_new = jnp.maximum(m_old, jnp.max(st, axis=0, keepdims=True))
                    alpha = jnp.exp(m_old - m_new)
                    pt = jnp.exp(st - m_new)
                    l_out.append(alpha * l_all[h:h + 1, :] + jnp.sum(pt, axis=0, keepdims=True))
                    m_out.append(m_new)
                    pv = _dot(vt, pt.astype(BF16), 1, 0)
                    acc = jnp.where(hrows[hh], alpha * acc_old[pr] + pv, acc)
                acc_out.append(acc)
            for h in range(ATT_HEADS):
                m_s[h:h + 1, :] = m_out[h]
                l_s[h:h + 1, :] = l_out[h]
            for pr in range(2):
                acc_s[LANES * pr:LANES * (pr + 1), :] = acc_out[pr]

        @pl.when(j < i)
        def _():
            step(False)

        @pl.when(j == i)
        def _():
            step(True)
            half = _iota((LANES, 1), 0)
            l_all = l_s[...]
            for pr in range(2):
                acc = acc_s[LANES * pr:LANES * (pr + 1), :]
                o_t = jnp.where(half < HEAD_DIM, acc / l_all[2 * pr:2 * pr + 1, :], acc / l_all[2 * pr + 1:2 * pr + 2, :])
                o_ref[:, LANES * pr:LANES * (pr + 1)] = o_t.T
            lse = m_s[...] + jnp.log(l_s[...])
            lse_ref[...] = jnp.where(_iota(lse.shape, 0) < ATT_HEADS, lse, 0.0)

    return _hosted_call(
        body, comm, (nq, nq), name=name,
        in_specs=[pl.BlockSpec((tq, w), lambda i, j: (i, cq)),
                  pl.BlockSpec((tq, w), lambda i, j: (jnp.minimum(j, i), ck)),
                  pl.BlockSpec((tq, w), lambda i, j: (jnp.minimum(j, i), cv)),
                  pl.BlockSpec((tq, ATT_HEADS * LANES), lambda i, j: (i, 0)),
                  pl.BlockSpec((tq, ATT_HEADS * LANES), lambda i, j: (jnp.minimum(j, i), 0))],
        out_specs=[pl.BlockSpec((tq, w), lambda i, j: (i, 0)),
                   pl.BlockSpec((SUBLANES, tq), lambda i, j: (0, i))],
        out_shape=[jax.ShapeDtypeStruct((t, w), F32), jax.ShapeDtypeStruct((SUBLANES, t), F32)],
        scratch_shapes=[pltpu.VMEM((SUBLANES, tq), F32), pltpu.VMEM((SUBLANES, tq), F32),
                        pltpu.VMEM((w, tq), F32)],
        args=[hbuf, hbuf, hbuf, eq, ek])


def _fox_delta(dymix, o, *, name):
    t, w = o.shape
    tm = _pick(t, (512, 256, 128))
    cdo = ATT_WIDTH // w

    def body(do_ref, o_ref, d_ref):
        d_ref[...] = _head_reduce(do_ref[...] * o_ref[...], 0, ATT_HEADS)

    return pl.pallas_call(
        body, name=name, grid=(t // tm,),
        in_specs=[pl.BlockSpec((tm, w), lambda i: (i, cdo)), pl.BlockSpec((tm, w), lambda i: (i, 0))],
        out_specs=pl.BlockSpec((tm, LANES), lambda i: (i, 0)),
        out_shape=jax.ShapeDtypeStruct((t, LANES), F32),
        compiler_params=_params(1),
    )(dymix, o)


def _fox_bwd(hbuf, eq, ek, dymix, lse_rows, delta_rows, *, comm=None, name):
    t = hbuf.shape[0]
    w = ATT_WIDTH
    tq = _pick(t, (FOX_T, 256, 128))
    nq = t // tq
    cq, ck, cv = COL_Q // w, COL_K // w, COL_V // w
    cdo = ATT_WIDTH // w

    def body(q_ref, k_ref, v_ref, eq_ref, ek_ref, do_ref, lse_ref, dl_ref, dk_ref, dv_ref, dfk_ref, dqt_ref, dfq_ref,
             dk_s, dv_s, dfk_s):
        j = pl.program_id(0)
        i = pl.program_id(1)

        @pl.when((i == 0) & (j == 0))
        def _():
            dqt_ref[...] = jnp.zeros_like(dqt_ref)
            dfq_ref[...] = jnp.zeros_like(dfq_ref)

        @pl.when(i == 0)
        def _():
            dk_s[...] = jnp.zeros_like(dk_s)
            dv_s[...] = jnp.zeros_like(dv_s)
            dfk_s[...] = jnp.zeros_like(dfk_s)

        def step(diagonal):
            _, hms = _fox_masks(i, j, tq)
            keys_first = (j * tq + _iota((tq, tq), 0)) <= (i * tq + _iota((tq, tq), 1))
            half = _iota((LANES, 1), 0)
            hrows = (half < HEAD_DIM, half >= HEAD_DIM)
            lse_all = lse_ref[...]
            dl_all = dl_ref[...]
            dvs, dks, dfks, dqts, dfqs = [], [], [], [], []
            for pr in range(2):
                sl = slice(LANES * pr, LANES * (pr + 1))
                qp = q_ref[:, sl]
                kp = k_ref[:, sl]
                kt = kp.T.astype(BF16)
                vpb = v_ref[:, sl].astype(BF16)
                dop = do_ref[:, sl]
                dv_p = jnp.zeros((tq, LANES), F32)
                dk_p = jnp.zeros((tq, LANES), F32)
                dqt_p = jnp.zeros((LANES, tq), F32)
                for hh in range(2):
                    h = 2 * pr + hh
                    hsl = slice(LANES * h, LANES * (h + 1))
                    qm = jnp.where(hms[hh], (qp * (HEAD_DIM ** -0.5)).astype(BF16), eq_ref[:, hsl])
                    km = jnp.where(hms[hh], kp.astype(BF16), ek_ref[:, hsl])
                    st = _dot(km, qm, 1, 1)
                    if diagonal:
                        st = jnp.where(keys_first, st, NEG)
                    pt = jnp.exp(st - lse_all[h:h + 1, :])
                    domb = jnp.where(hms[hh], dop, 0.0).astype(BF16)
                    dv_p = dv_p + _dot(pt.astype(BF16), domb, 1, 0)
                    dpt = _dot(vpb, domb, 1, 1)
                    dst = pt * (dpt - dl_all[h:h + 1, :])
                    dstb = dst.astype(BF16)
                    dk_p = dk_p + jnp.where(hms[hh], _dot(dstb, qm, 1, 0), 0.0)
                    dqt_p = dqt_p + _dot(jnp.where(hrows[hh], kt, 0.0), dstb, 1, 0)
                    part = dst[:, 0:LANES]
                    for c in range(1, tq // LANES):
                        part = part + dst[:, LANES * c:LANES * (c + 1)]
                    dfks.append(part)
                    dfqs.append(jnp.sum(dst, axis=0, keepdims=True))
                dvs.append(dv_p)
                dks.append(dk_p)
                dqts.append(dqt_p)
            dv_s[...] += jnp.concatenate(dvs, axis=1)
            dk_s[...] += jnp.concatenate(dks, axis=1)
            for h in range(ATT_HEADS):
                dfk_s[h] += dfks[h]
            cols = pl.ds(pl.multiple_of(i * tq, tq), tq)
            dqt_ref[:, cols] += jnp.concatenate(dqts, axis=0) * (HEAD_DIM ** -0.5)
            dfq_ref[:, cols] += jnp.concatenate(dfqs + [jnp.zeros((SUBLANES - ATT_HEADS, tq), F32)], axis=0)

        @pl.when(i > j)
        def _():
            step(False)

        @pl.when(i == j)
        def _():
            step(True)

        @pl.when(i == nq - 1)
        def _():
            dk_ref[...] = dk_s[...].astype(BF16)
            dv_ref[...] = dv_s[...].astype(BF16)
            lane = _iota((tq, LANES), 1)
            out = jnp.zeros((tq, LANES), F32)
            for h in range(ATT_HEADS):
                out = jnp.where(lane == h, jnp.sum(dfk_s[h], axis=1, keepdims=True), out)
            dfk_ref[...] = out

    qi = lambda j, i: jnp.maximum(i, j)
    rows = pl.BlockSpec((SUBLANES, tq), lambda j, i: (0, qi(j, i)))
    return _hosted_call(
        body, comm, (nq, nq), name=name,
        in_specs=[pl.BlockSpec((tq, w), lambda j, i: (qi(j, i), cq)),
                  pl.BlockSpec((tq, w), lambda j, i: (j, ck)),
                  pl.BlockSpec((tq, w), lambda j, i: (j, cv)),
                  pl.BlockSpec((tq, ATT_HEADS * LANES), lambda j, i: (qi(j, i), 0)),
                  pl.BlockSpec((tq, ATT_HEADS * LANES), lambda j, i: (j, 0)),
                  pl.BlockSpec((tq, w), lambda j, i: (qi(j, i), cdo)),
                  rows, rows],
        out_specs=[pl.BlockSpec((tq, w), lambda j, i: (j, 0)), pl.BlockSpec((tq, w), lambda j, i: (j, 0)),
                   pl.BlockSpec((tq, LANES), lambda j, i: (j, 0)),
                   pl.BlockSpec((w, t), lambda j, i: (0, 0)), pl.BlockSpec((SUBLANES, t), lambda j, i: (0, 0))],
        out_shape=[jax.ShapeDtypeStruct((t, w), BF16), jax.ShapeDtypeStruct((t, w), BF16),
                   jax.ShapeDtypeStruct((t, LANES), F32),
                   jax.ShapeDtypeStruct((w, t), F32), jax.ShapeDtypeStruct((SUBLANES, t), F32)],
        scratch_shapes=[pltpu.VMEM((tq, w), F32), pltpu.VMEM((tq, w), F32),
                        pltpu.VMEM((ATT_HEADS, tq, LANES), F32)],
        args=[hbuf, hbuf, hbuf, eq, ek, dymix, lse_rows, delta_rows])


GROUP_W = SSD_WIDTH // SSD_GROUPS
HEADS_PER_GROUP = SSD_HEADS // SSD_GROUPS


def _ssd_chunk_common(xr, prev8, sm, cw, cb, dtb, avec):
    c = _conv_taps(xr, prev8, cw, cb)
    sig = _sigmoid(c)
    xa = c * sig
    dt = _softplus(sm + dtb)
    a = dt * avec
    acum = _cumsum_rows(a)
    return c, sig, xa, dt, acum


def _ssd_decays(acum, g):
    n = acum.shape[0]
    atot = acum[n - 1:n, :]
    lane0 = LANE_DT + HEADS_PER_GROUP * g
    e = _head_expand(jnp.exp(acum), lane0, HEADS_PER_GROUP, GROUP_W)
    dec = _head_expand(jnp.exp(atot - acum), lane0, HEADS_PER_GROUP, GROUP_W)
    etot = _head_expand(jnp.exp(atot), lane0, HEADS_PER_GROUP, GROUP_W)
    return e, dec, etot


def _ssd_ldec(acum, acum_t, lane, tril):
    return jnp.exp(jnp.where(tril, _col(acum, lane) - _row(acum_t, lane), NEG))


def _ssd_fwd(hbuf, conv_w, conv_b, dtb_vec, a_vec, d_exp, norm_g, *, name):
    t = hbuf.shape[0]
    L = SSD_CHUNK
    nc = t // L
    hb = L // SUBLANES
    cs = COL_SMALL // LANES
    cz = COL_Z // SSD_WIDTH

    def body(x_ref, xp_ref, z_ref, s_ref, cw_ref, cb_ref, dtb_ref, av_ref, dx_ref, ng_ref,
             yc_ref, y_ref, st_ref, state):
        i = pl.program_id(0)

        @pl.when(i == 0)
        def _():
            state[...] = jnp.zeros_like(state)

        prev = jnp.where(i == 0, 0.0, xp_ref[...])
        _, _, xa, dt, acum = _ssd_chunk_common(x_ref[...], prev, s_ref[...], cw_ref[...], cb_ref[...],
                                               dtb_ref[...], av_ref[...])
        acum_t = acum.T
        xs = xa[:, :SSD_WIDTH]
        xdt = xs * _head_expand(dt, LANE_DT, SSD_HEADS, SSD_WIDTH)
        tril = _iota((L, L), 0) >= _iota((L, L), 1)
        lane = _iota((1, LANES), 1)
        ys = []
        for g in range(SSD_GROUPS):
            bg = xa[:, SSD_WIDTH + SSD_STATE * g:SSD_WIDTH + SSD_STATE * (g + 1)].astype(BF16)
            cg = xa[:, SSD_WIDTH + SSD_STATE * (SSD_GROUPS + g):SSD_WIDTH + SSD_STATE * (SSD_GROUPS + g + 1)].astype(BF16)
            gm = _dot(cg, bg, 1, 1)
            e, dec, etot = _ssd_decays(acum, g)
            s_in = state[g]
            st_ref[0, g] = s_in
            xg = xdt[:, GROUP_W * g:GROUP_W * (g + 1)]
            y_off = e * _dot(cg, s_in.astype(BF16), 1, 0)
            state[g] = etot * s_in + _dot(bg, (dec * xg).astype(BF16), 0, 0)
            for pr in range(2):
                xp = xg[:, LANES * pr:LANES * (pr + 1)].astype(BF16)
                outs = []
                for hh in range(2):
                    h = HEADS_PER_GROUP * g + 2 * pr + hh
                    m = gm * _ssd_ldec(acum, acum_t, LANE_DT + h, tril)
                    outs.append(_dot(m.astype(BF16), xp, 1, 0))
                ys.append(jnp.where(lane < HEAD_DIM, outs[0], outs[1]) + y_off[:, LANES * pr:LANES * (pr + 1)])
        y = jnp.concatenate(ys, axis=1)
        y_ref[...] = y
        yd = y + dx_ref[...] * xs
        zz = z_ref[...]
        y2 = yd * zz * _sigmoid(zz)
        ng = ng_ref[...]
        outs = []
        for g in range(SSD_GROUPS):
            yg = y2[:, GROUP_W * g:GROUP_W * (g + 1)]
            rs = lax.rsqrt(jnp.mean(yg * yg, axis=1, keepdims=True) + RMS_EPS)
            outs.append(yg * rs * ng[:, GROUP_W * g:GROUP_W * (g + 1)])
        yc_ref[...] = jnp.concatenate(outs, axis=1)

    cdim = SSD_CONV_DIM
    vecc = pl.BlockSpec((1, cdim), lambda i: (0, 0))
    vecl = pl.BlockSpec((1, LANES), lambda i: (0, 0))
    vecw = pl.BlockSpec((1, SSD_WIDTH), lambda i: (0, 0))
    roww = pl.BlockSpec((L, SSD_WIDTH), lambda i: (i, 0))
    return pl.pallas_call(
        body, name=name, grid=(nc,),
        in_specs=[pl.BlockSpec((L, cdim), lambda i: (i, 0)),
                  pl.BlockSpec((SUBLANES, cdim), lambda i: (jnp.maximum(i * hb - 1, 0), 0)),
                  pl.BlockSpec((L, SSD_WIDTH), lambda i: (i, cz)),
                  pl.BlockSpec((L, LANES), lambda i: (i, cs)),
                  pl.BlockSpec((CONV_K, cdim), lambda i: (0, 0)), vecc, vecl, vecl, vecw, vecw],
        out_specs=[roww, roww, pl.BlockSpec((1, SSD_GROUPS, SSD_STATE, GROUP_W), lambda i: (i, 0, 0, 0))],
        out_shape=[jax.ShapeDtypeStruct((t, SSD_WIDTH), F32), jax.ShapeDtypeStruct((t, SSD_WIDTH), F32),
                   jax.ShapeDtypeStruct((nc, SSD_GROUPS, SSD_STATE, GROUP_W), F32)],
        scratch_shapes=[pltpu.VMEM((SSD_GROUPS, SSD_STATE, GROUP_W), F32)],
        compiler_params=_params(1),
    )(hbuf, hbuf, hbuf, hbuf, conv_w, conv_b, dtb_vec, a_vec, d_exp, norm_g)


def _ssd_bwd(dymix, hbuf, y_ssd, states, conv_w, conv_b, dtb_vec, a_vec, d_exp, norm_g, *, name):
    t = hbuf.shape[0]
    L = SSD_CHUNK
    nc = t // L
    hb = L // SUBLANES
    cs = COL_SMALL // LANES
    cz = COL_Z // SSD_WIDTH
    cdy = (LRU_WIDTH + ATT_WIDTH) // SSD_WIDTH
    cdim = SSD_CONV_DIM

    def body(dyc_ref, x_ref, xp_ref, z_ref, s_ref, y_ref, st_ref, cw_ref, cb_ref, dtb_ref, av_ref, dx_ref, ng_ref,
             dxr_ref, dz_ref, dsm_ref, dng_ref, dd_ref, da_ref, ddtb_ref, dcw_ref, dcb_ref,
             dstate, dnext):
        i = pl.program_id(0)
        ic = nc - 1 - i

        @pl.when(i == 0)
        def _():
            dstate[...] = jnp.zeros_like(dstate)
            dnext[...] = jnp.zeros_like(dnext)
            for ref in (dng_ref, dd_ref, da_ref, ddtb_ref, dcw_ref, dcb_ref):
                ref[...] = jnp.zeros_like(ref)

        xr = x_ref[...]
        sm = s_ref[...]
        prev = jnp.where(ic == 0, 0.0, xp_ref[...])
        avec = av_ref[...]
        c, sig, xa, dt, acum = _ssd_chunk_common(xr, prev, sm, cw_ref[...], cb_ref[...], dtb_ref[...], avec)
        acum_t = acum.T
        xs = xa[:, :SSD_WIDTH]
        dtx = _head_expand(dt, LANE_DT, SSD_HEADS, SSD_WIDTH)
        xdt = xs * dtx
        tril = _iota((L, L), 0) >= _iota((L, L), 1)
        lane = _iota((1, LANES), 1)
        hmasks = (lane < HEAD_DIM, lane >= HEAD_DIM)

        y = y_ref[...]
        dexp = dx_ref[...]
        yd = y + dexp * xs
        zz = z_ref[...]
        sz = _sigmoid(zz)
        siluz = zz * sz
        y2 = yd * siluz
        ng = ng_ref[...]
        dyc = dyc_ref[...]
        dy2s, dngs = [], []
        for g in range(SSD_GROUPS):
            sl = slice(GROUP_W * g, GROUP_W * (g + 1))
            yg = y2[:, sl]
            rs = lax.rsqrt(jnp.mean(yg * yg, axis=1, keepdims=True) + RMS_EPS)
            wv = dyc[:, sl] * ng[:, sl]
            dngs.append(jnp.sum(dyc[:, sl] * yg * rs, axis=0, keepdims=True))
            dy2s.append(rs * wv - yg * (rs * rs * rs) * jnp.mean(wv * yg, axis=1, keepdims=True))
        dy2 = jnp.concatenate(dy2s, axis=1)
        dng_ref[...] += jnp.concatenate(dngs, axis=1)
        dz_ref[...] = (dy2 * yd * (sz * (1.0 + zz * (1.0 - sz)))).astype(BF16)
        dy = dy2 * siluz
        dd_ref[...] += jnp.sum(dy * xs, axis=0, keepdims=True)

        dxs, dbs, dcs = [], [], []
        datot = jnp.zeros((1, LANES), F32)
        lanes = _iota((L, LANES), 1)
        dacum = jnp.zeros((L, LANES), F32)
        for g in range(SSD_GROUPS):
            sl = slice(GROUP_W * g, GROUP_W * (g + 1))
            bg = xa[:, SSD_WIDTH + SSD_STATE * g:SSD_WIDTH + SSD_STATE * (g + 1)].astype(BF16)
            cg = xa[:, SSD_WIDTH + SSD_STATE * (SSD_GROUPS + g):SSD_WIDTH + SSD_STATE * (SSD_GROUPS + g + 1)].astype(BF16)
            gm = _dot(cg, bg, 1, 1)
            e, dec, etot = _ssd_decays(acum, g)
            s_in = st_ref[0, g]
            ds_out = dstate[g]
            dyg = dy[:, sl]
            xg = xdt[:, sl]
            edy = (e * dyg).astype(BF16)
            dstate[g] = etot * ds_out + _dot(cg, edy, 0, 0)
            dx_state = dec * _dot(bg, ds_out.astype(BF16), 1, 0)
            y_off = e * _dot(cg, s_in.astype(BF16), 1, 0)
            dacum = dacum + _head_reduce_group(dyg * y_off - xg * dx_state, g)
            dc_off = _dot(edy, s_in.astype(BF16), 1, 1)
            db_state = _dot((dec * xg).astype(BF16), ds_out.astype(BF16), 1, 1)
            dgsum = jnp.zeros((L, L), F32)
            dx_pairs = []
            for pr in range(2):
                psl = slice(LANES * pr, LANES * (pr + 1))
                xp = xg[:, psl]
                dyp = dyg[:, psl]
                dx_pair = jnp.zeros((L, LANES), F32)
                for hh in range(2):
                    h = HEADS_PER_GROUP * g + 2 * pr + hh
                    ldec = _ssd_ldec(acum, acum_t, LANE_DT + h, tril)
                    dym = jnp.where(hmasks[hh], dyp, 0.0).astype(BF16)
                    xm = jnp.where(hmasks[hh], xp, 0.0).astype(BF16)
                    dx_pair = dx_pair + _dot((gm * ldec).astype(BF16), dym, 0, 0)
                    dml = _dot(dym, xm, 1, 1) * ldec
                    dgsum = dgsum + dml
                    qm = dml * gm
                    seg = jnp.sum(qm, axis=1, keepdims=True) - jnp.sum(qm.T, axis=1, keepdims=True)
                    dacum = dacum + jnp.where(lanes == LANE_DT + h, seg, 0.0)
                dx_pairs.append(dx_pair)
            dgb = dgsum.astype(BF16)
            dcs.append(_dot(dgb, bg, 1, 0) + dc_off)
            dbs.append(_dot(dgb, cg, 0, 0) + db_state)
            dxg = jnp.concatenate(dx_pairs, axis=1) + dx_state
            dxs.append(dxg)
            v = jnp.sum(dx_state * xg, axis=0, keepdims=True) + etot * jnp.sum(ds_out * s_in, axis=0, keepdims=True)
            datot = datot + _head_reduce_row(v, LANE_DT + HEADS_PER_GROUP * g, HEADS_PER_GROUP)
        dx = jnp.concatenate(dxs, axis=1)
        dacum = dacum + jnp.where(_iota((L, LANES), 0) == L - 1, datot, 0.0)
        da = _cumsum_rows(dacum, reverse=True)
        ddt = da * avec + _head_reduce(dx * xs, LANE_DT, SSD_HEADS)
        da_ref[...] += jnp.sum(da * dt, axis=0, keepdims=True)
        ddt_raw = ddt * _sigmoid(sm + dtb_ref[...])
        ddt_raw = jnp.where((lanes >= LANE_DT) & (lanes < LANE_DT + SSD_HEADS), ddt_raw, 0.0)
        dsm_ref[...] = ddt_raw
        ddtb_ref[...] += jnp.sum(ddt_raw, axis=0, keepdims=True)
        dxs_total = dx * dtx + dexp * dy
        dxa = jnp.concatenate([dxs_total] + dbs + dcs, axis=1)
        dc = dxa * (sig * (1.0 + c * (1.0 - sig)))
        dxr, dws = _conv_taps_bwd(dc, dnext[...], cw_ref[...], xr)
        dxr_ref[...] = dxr.astype(BF16)
        dcw_ref[...] += dws
        dcb_ref[...] += jnp.sum(dc, axis=0, keepdims=True)
        dnext[...] = dc[:SUBLANES]

    rev = lambda i: nc - 1 - i
    vecc = pl.BlockSpec((1, cdim), lambda i: (0, 0))
    vecl = pl.BlockSpec((1, LANES), lambda i: (0, 0))
    vecw = pl.BlockSpec((1, SSD_WIDTH), lambda i: (0, 0))
    cwspec = pl.BlockSpec((CONV_K, cdim), lambda i: (0, 0))
    roww = pl.BlockSpec((L, SSD_WIDTH), lambda i: (rev(i), 0))
    return pl.pallas_call(
        body, name=name, grid=(nc,),
        in_specs=[pl.BlockSpec((L, SSD_WIDTH), lambda i: (rev(i), cdy)),
                  pl.BlockSpec((L, cdim), lambda i: (rev(i), 0)),
                  pl.BlockSpec((SUBLANES, cdim), lambda i: (jnp.maximum(rev(i) * hb - 1, 0), 0)),
                  pl.BlockSpec((L, SSD_WIDTH), lambda i: (rev(i), cz)),
                  pl.BlockSpec((L, LANES), lambda i: (rev(i), cs)),
                  roww,
                  pl.BlockSpec((1, SSD_GROUPS, SSD_STATE, GROUP_W), lambda i: (rev(i), 0, 0, 0)),
                  cwspec, vecc, vecl, vecl, vecw, vecw],
        out_specs=[pl.BlockSpec((L, cdim), lambda i: (rev(i), 0)), roww,
                   pl.BlockSpec((L, LANES), lambda i: (rev(i), 0)),
                   vecw, vecw, vecl, vecl, cwspec, vecc],
        out_shape=[jax.ShapeDtypeStruct((t, cdim), BF16), jax.ShapeDtypeStruct((t, SSD_WIDTH), BF16),
                   jax.ShapeDtypeStruct((t, LANES), F32),
                   jax.ShapeDtypeStruct((1, SSD_WIDTH), F32), jax.ShapeDtypeStruct((1, SSD_WIDTH), F32),
                   jax.ShapeDtypeStruct((1, LANES), F32), jax.ShapeDtypeStruct((1, LANES), F32),
                   jax.ShapeDtypeStruct((CONV_K, cdim), F32), jax.ShapeDtypeStruct((1, cdim), F32)],
        scratch_shapes=[pltpu.VMEM((SSD_GROUPS, SSD_STATE, GROUP_W), F32), pltpu.VMEM((SUBLANES, cdim), F32)],
        compiler_params=_params(1),
    )(dymix, hbuf, hbuf, hbuf, hbuf, y_ssd, states, conv_w, conv_b, dtb_vec, a_vec, d_exp, norm_g)


def _head_reduce_group(x, g):
    return _head_reduce(x, LANE_DT + HEADS_PER_GROUP * g, HEADS_PER_GROUP)


def _head_reduce_row(v, lane0, nheads):
    colhead = _iota(v.shape, 1) // HEAD_DIM
    lane = _iota((1, LANES), 1)
    out = jnp.zeros((1, LANES), F32)
    for h in range(nheads):
        s = jnp.sum(jnp.where(colhead == h, v, 0.0), axis=1, keepdims=True)
        out = jnp.where(lane == lane0 + h, s, out)
    return out


def _exchange(inps, axes, *, swap=False, name):
    n = 2 ** len(axes)
    assert not swap or n == 2
    counts = [a.shape[0] for a in inps]
    out_shapes = [jax.ShapeDtypeStruct(a.shape if swap else (n,) + a.shape, a.dtype) for a in inps]
    units = sum(counts)
    na = len(inps)

    def body(*refs):
        in_refs, out_refs = refs[:na], refs[na:2 * na]
        send_sems, recv_sems, local_sems = refs[2 * na:]
        pos = {ax: lax.axis_index(ax) for ax in MESH_AXES}

        def slot_of(coord):
            s = 0
            for ax in axes:
                s = s * 2 + coord[ax]
            return s

        me = slot_of(pos)
        copies = []
        unit = 0
        for a in range(na):
            for it in range(counts[a]):
                dst = out_refs[a].at[it] if swap else out_refs[a].at[me, it]
                if not swap:
                    cp = pltpu.make_async_copy(in_refs[a].at[it], dst, local_sems.at[unit])
                    cp.start()
                    copies.append(cp)
                for delta in range(1, n):
                    coord = dict(pos)
                    for b, ax in enumerate(reversed(axes)):
                        if (delta >> b) & 1:
                            coord[ax] = 1 - pos[ax]
                    k = unit * (n - 1) + delta - 1
                    cp = pltpu.make_async_remote_copy(
                        src_ref=in_refs[a].at[it], dst_ref=dst,
                        send_sem=send_sems.at[k], recv_sem=recv_sems.at[k],
                        device_id=(coord["x"], coord["y"], coord["c"]), device_id_type=pl.DeviceIdType.MESH)
                    cp.start()
                    copies.append(cp)
                unit += 1
        for cp in copies:
            cp.wait()

    any_spec = pl.BlockSpec(memory_space=pl.ANY)
    return pl.pallas_call(
        body, name=name,
        in_specs=[any_spec] * na, out_specs=[any_spec] * na, out_shape=out_shapes,
        scratch_shapes=[pltpu.SemaphoreType.DMA((units * (n - 1),)), pltpu.SemaphoreType.DMA((units * (n - 1),)),
                        pltpu.SemaphoreType.DMA((units,))],
    )(*inps)


class _Comm:
    def __init__(self, arrays, out_shapes, n_own, start, finish, base=0):
        self.arrays, self.out_shapes, self.start, self.finish = arrays, out_shapes, start, finish
        self.base, self.n_own, self.n_sems = base, n_own, base + n_own

    def specs(self):
        any_spec = pl.BlockSpec(memory_space=pl.ANY)
        sems = [pltpu.SemaphoreType.DMA((self.n_sems,)), pltpu.SemaphoreType.DMA((self.n_sems,))]
        return [any_spec] * len(self.arrays), [any_spec] * len(self.out_shapes), sems


def _run_comm(comm, *, name):
    na, no = len(comm.arrays), len(comm.out_shapes)

    def body(*refs):
        args = (refs[:na], refs[na:na + no]) + tuple(refs[na + no:])
        comm.start(*args)
        comm.finish(*args)

    in_specs, out_specs, sems = comm.specs()
    return pl.pallas_call(body, name=name, in_specs=in_specs, out_specs=out_specs, out_shape=comm.out_shapes,
                          scratch_shapes=sems)(*comm.arrays)


def _chip_peer(x, y, d):
    px = 1 - x if d & 2 else x
    py = 1 - y if d & 1 else y
    return px, py, 2 * px + py


def _gather_layer_comm(srcs, li, base=0):
    counts = [s.shape[0] for s in srcs]
    units = [(a, it) for a in range(len(srcs)) for it in range(counts[a])]
    n_ici = 3 * len(units)
    out_shapes = [jax.ShapeDtypeStruct((N_CHIPS,) + s.shape, s.dtype) for s in srcs]

    def ici(ins, outs, ssem, rsem, u, d):
        x, y, c = (lax.axis_index(ax) for ax in MESH_AXES)
        a, it = units[u]
        px, py, _ = _chip_peer(x, y, d)
        k = base + 3 * u + d - 1
        return pltpu.make_async_remote_copy(
            src_ref=ins[a].at[it], dst_ref=outs[a].at[2 * x + y, it], send_sem=ssem.at[k], recv_sem=rsem.at[k],
            device_id=(px, py, c), device_id_type=pl.DeviceIdType.MESH)

    def arrived(ins, outs, ssem, rsem, u, d):
        x, y, c = (lax.axis_index(ax) for ax in MESH_AXES)
        a, it = units[u]
        _, _, pk = _chip_peer(x, y, d)
        k = base + 3 * u + d - 1
        return pltpu.make_async_remote_copy(
            src_ref=ins[a].at[it], dst_ref=outs[a].at[pk, it], send_sem=ssem.at[k], recv_sem=rsem.at[k],
            device_id=(x, y, c), device_id_type=pl.DeviceIdType.MESH)

    def forward(ins, outs, ssem, rsem, u, slot):
        x, y, c = (lax.axis_index(ax) for ax in MESH_AXES)
        a, it = units[u]
        pk = 2 * x + y if slot == 0 else _chip_peer(x, y, slot)[2]
        src = ins[a].at[it] if slot == 0 else outs[a].at[pk, it]
        k = base + n_ici + 4 * u + slot
        return pltpu.make_async_remote_copy(
            src_ref=src, dst_ref=outs[a].at[pk, it], send_sem=ssem.at[k], recv_sem=rsem.at[k],
            device_id=(x, y, 1 - c), device_id_type=pl.DeviceIdType.MESH)

    def start(ins, outs, ssem, rsem):
        @pl.when(lax.axis_index("c") == li)
        def _():
            for u in range(len(units)):
                for d in range(1, N_CHIPS):
                    ici(ins, outs, ssem, rsem, u, d).start()

    def finish(ins, outs, ssem, rsem):
        c = lax.axis_index("c")

        @pl.when(c == li)
        def _():
            for u in range(len(units)):
                forward(ins, outs, ssem, rsem, u, 0).start()
                for d in range(1, N_CHIPS):
                    arrived(ins, outs, ssem, rsem, u, d).wait_recv()
                    forward(ins, outs, ssem, rsem, u, d).start()
            for u in range(len(units)):
                for d in range(1, N_CHIPS):
                    ici(ins, outs, ssem, rsem, u, d).wait_send()
                for slot in range(N_CHIPS):
                    forward(ins, outs, ssem, rsem, u, slot).wait_send()

        @pl.when(c != li)
        def _():
            for u in range(len(units)):
                for slot in range(N_CHIPS):
                    forward(ins, outs, ssem, rsem, u, slot).wait_recv()

    return _Comm(srcs, out_shapes, n_ici + 4 * len(units), start, finish, base)


def _reduce_chips_comm(sums, li, base=0):
    counts = [s.shape[0] for s in sums]
    units = [(a, it) for a in range(len(sums)) for it in range(counts[a])]
    out_shapes = [jax.ShapeDtypeStruct((N_CHIPS, s.shape[0]) + s.shape[2:], s.dtype) for s in sums]

    def copy(ins, outs, ssem, rsem, u, d):
        x, y, c = (lax.axis_index(ax) for ax in MESH_AXES)
        a, it = units[u]
        px, py, pk = _chip_peer(x, y, d)
        k = base + 3 * u + d - 1
        return pltpu.make_async_remote_copy(
            src_ref=ins[a].at[it, pk], dst_ref=outs[a].at[2 * x + y, it], send_sem=ssem.at[k], recv_sem=rsem.at[k],
            device_id=(px, py, c), device_id_type=pl.DeviceIdType.MESH)

    def start(ins, outs, ssem, rsem):
        @pl.when(lax.axis_index("c") == li)
        def _():
            for u in range(len(units)):
                for d in range(1, N_CHIPS):
                    copy(ins, outs, ssem, rsem, u, d).start()

    def finish(ins, outs, ssem, rsem):
        @pl.when(lax.axis_index("c") == li)
        def _():
            for u in range(len(units)):
                for d in range(1, N_CHIPS):
                    copy(ins, outs, ssem, rsem, u, d).wait()

    return _Comm(sums, out_shapes, 3 * len(units), start, finish, base)


def _sum_slots(buf, out_dtype, *, name):
    n, rows, cols = buf.shape
    tm = _pick(rows, (512, 256, 128, 8))
    if rows % tm:
        tm = rows

    def body(b_ref, o_ref):
        acc = b_ref[0].astype(F32)
        for s in range(1, n):
            acc = acc + b_ref[s].astype(F32)
        o_ref[...] = acc.astype(out_dtype)

    return pl.pallas_call(
        body, name=name, grid=(pl.cdiv(rows, tm),),
        in_specs=[pl.BlockSpec((n, tm, cols), lambda i: (0, i, 0))],
        out_specs=pl.BlockSpec((tm, cols), lambda i: (i, 0)),
        out_shape=jax.ShapeDtypeStruct((rows, cols), out_dtype),
        compiler_params=_params(1),
    )(buf)


def _sum_pair(a, b, out_dtype, *, name):
    shape = a.shape
    cols = shape[-1]
    a2, b2 = a.reshape(-1, cols), b.reshape(-1, cols)
    rows = a2.shape[0]
    tm = _pick(rows, (512, 256, 128, 8))

    def body(a_ref, b_ref, o_ref):
        o_ref[...] = (a_ref[...].astype(F32) + b_ref[...].astype(F32)).astype(out_dtype)

    spec = pl.BlockSpec((tm, cols), lambda i: (i, 0))
    return pl.pallas_call(
        body, name=name, grid=(rows // tm,), in_specs=[spec, spec], out_specs=spec,
        out_shape=jax.ShapeDtypeStruct((rows, cols), out_dtype), compiler_params=_params(1),
    )(a2, b2).reshape(shape)


def _adamw(w, g, m, v, *, name):
    shape = w.shape
    cols = shape[-1]
    rows = w.size // cols
    w2, g2, m2, v2 = (a.reshape(rows, cols) for a in (w, g, m, v))
    tm = _pick(rows, (256, 128, 64, 32, 16, 8))
    if rows % tm:
        tm = rows
    bc1 = 1.0 - ADAM_B1 ** ADAM_STEP
    bc2 = 1.0 - ADAM_B2 ** ADAM_STEP

    def body(w_ref, g_ref, m_ref, v_ref, d_ref, nm_ref, nv_ref):
        gg = g_ref[...]
        mm = ADAM_B1 * m_ref[...] + (1.0 - ADAM_B1) * gg
        vv = ADAM_B2 * v_ref[...] + (1.0 - ADAM_B2) * (gg * gg)
        m_hat = mm / bc1
        v_hat = vv / bc2
        d_ref[...] = -ADAM_LR * (m_hat / (jnp.sqrt(v_hat) + ADAM_EPS) + ADAM_WD * w_ref[...])
        nm_ref[...] = mm
        nv_ref[...] = vv

    spec = pl.BlockSpec((tm, cols), lambda i: (i, 0))
    o = jax.ShapeDtypeStruct((rows, cols), F32)
    outs = pl.pallas_call(
        body, name=name, grid=(rows // tm,), in_specs=[spec] * 4, out_specs=[spec] * 3, out_shape=[o] * 3,
        compiler_params=_params(1),
    )(w2, g2, m2, v2)
    return tuple(a.reshape(shape) for a in outs)


def _layer_fwd(li, x, xb, pb, W, comm=None, late=None):
    nm = lambda s: f"l{li}_{s}"
    sv = {"x_in_b": xb}
    (g1, u1, a1), late_out = _mm_swiglu(xb, W["ffn1_wg"], W["ffn1_wu"], comm=late[0] if late else None,
                                        name=nm("ffn1_up"))
    if late:
        W = {**W, **late[1](late_out)}
    x1, x1b, xh1, rs1 = _mm_ln(a1, W["ffn1_wd"], x, W["ln1_g"], W["ln1_b"], rscale=ALPHA, mscale=0.5, name=nm("ffn1_down_ln"))
    hbuf = _mm(x1b, W["w_in_p"], name=nm("in_proj"))
    ya, lu, lr, lig, la, lh = _lru_fwd(hbuf, W["lru_conv_w"], W["lru_conv_b"], W["lru_wa_bd"], W["lru_ba"],
                                       W["lru_wx_bd"], W["lru_bx"], W["lru_lambda"], name=nm("lru_fwd"))
    eq, ek = _fox_prep(hbuf, W["fox_bf_vec"], name=nm("fox_prep"))
    (yb, lse_rows), comm_out = _fox_fwd(hbuf, eq, ek, comm=comm, name=nm("fox_fwd"))
    yc, yssd, states = _ssd_fwd(hbuf, W["ssd_conv_w"], W["ssd_conv_b"], W["ssd_dtb_vec"], W["ssd_a_vec"],
                                W["ssd_d_exp"], W["ssd_norm_g"], name=nm("ssd_fwd"))
    ymix = jnp.concatenate([ya, yb, yc], axis=1).astype(BF16)
    x2, x2b, xh2, rs2 = _mm_ln(ymix, W["w_out"], x1, W["ln2_g"], W["ln2_b"], rscale=ALPHA, mscale=1.0, name=nm("out_proj_ln"))
    (g2, u2, a2), _ = _mm_swiglu(x2b, W["ffn2_wg"], W["ffn2_wu"], name=nm("ffn2_up"))
    x3, x3b, xh3, rs3 = _mm_ln(a2, W["ffn2_wd"], x2, W["ln3_g"], W["ln3_b"], rscale=ALPHA, mscale=0.5, name=nm("ffn2_down_ln"))
    x4, x4b, sg, e = _mm_pe(x3, x3b, pb, W["pe_gate_w"], W["pe_gate_b"], W["pe_proj"], name=nm("ple"))
    sv.update(g1=g1, u1=u1, a1=a1, x1b=x1b, xh1=xh1, rs1=rs1, hbuf=hbuf, lu=lu, lr=lr, lig=lig, la=la, lh=lh,
              eq=eq, ek=ek, lse_rows=lse_rows, yb=yb, yssd=yssd, states=states, ymix=ymix, x2b=x2b, xh2=xh2, rs2=rs2,
              g2=g2, u2=u2, a2=a2, x3b=x3b, xh3=xh3, rs3=rs3, sg=sg, e=e, pb=pb)
    return x4, x4b, sv, W, comm_out


def _layer_bwd(li, dx4, sv, W, comm=None, late=None):
    nm = lambda s: f"l{li}_{s}"
    G = {}
    dgp, de, dbg = _pe_bwd_elem(dx4, sv["sg"], sv["e"], name=nm("ple_bwd"))
    G["pe_gate_b"] = dbg
    G["pe_gate_w"] = _mm(sv["x3b"], dgp, ta=True, out_dtype=BF16, name=nm("d_pe_gate_w"))
    G["pe_proj"] = _mm(sv["pb"], de, ta=True, out_dtype=BF16, chip_cols=True, name=nm("d_pe_proj"))
    dr3, dr3b, G["ln3_g"], G["ln3_b"] = _bwd_proj([(dgp, W["pe_gate_w"])], dx4, rscale=1.0,
                                                  ln=(sv["xh3"], sv["rs3"], W["ln3_g"]), name=nm("ln3_bwd"))
    G["ffn2_wd"] = _mm(sv["a2"], dr3b, ta=True, scale=0.5, out_dtype=BF16, name=nm("d_ffn2_wd"))
    dg2, du2 = _mm_swiglu_bwd(dr3b, W["ffn2_wd"], sv["g2"], sv["u2"], scale=0.5, name=nm("ffn2_act_bwd"))
    G["ffn2_wg"] = _mm(sv["x2b"], dg2, ta=True, out_dtype=BF16, chip_cols=True, name=nm("d_ffn2_wg"))
    G["ffn2_wu"] = _mm(sv["x2b"], du2, ta=True, out_dtype=BF16, chip_cols=True, name=nm("d_ffn2_wu"))
    dr2, dr2b, G["ln2_g"], G["ln2_b"] = _bwd_proj([(dg2, W["ffn2_wg"]), (du2, W["ffn2_wu"])], dr3, rscale=ALPHA,
                                                  ln=(sv["xh2"], sv["rs2"], W["ln2_g"]), name=nm("ln2_bwd"))
    G["w_out"] = _mm(sv["ymix"], dr2b, ta=True, out_dtype=BF16, name=nm("d_w_out"))
    dymix = _mm(dr2b, W["w_out"], tb=True, name=nm("d_ymix"))
    hbuf = sv["hbuf"]
    (dur, dgr, G["lru_conv_w"], G["lru_conv_b"], G["lru_wa_bd"], G["lru_ba"], G["lru_wx_bd"], G["lru_bx"],
     G["lru_lambda"]) = _lru_bwd(dymix, hbuf, sv["lu"], sv["lr"], sv["lig"], sv["la"], sv["lh"],
                                 W["lru_conv_w"], W["lru_wa_bd"], W["lru_wx_bd"], W["lru_lambda"], name=nm("lru_bwd"))
    delta = _fox_delta(dymix, sv["yb"], name=nm("fox_delta"))
    delta_rows = jnp.pad(delta[:, :ATT_HEADS].T, ((0, SUBLANES - ATT_HEADS), (0, 0)))
    comm = _merge_comms([comm, late(G) if late else None])
    (dk, dv, dfk, dqt, dfq), comm_out = _fox_bwd(hbuf, sv["eq"], sv["ek"], dymix, sv["lse_rows"], delta_rows,
                                                 comm=comm, name=nm("fox_bwd"))
    dq = dqt.T
    dfc = jnp.pad(dfq[:ATT_HEADS].T, ((0, 0), (0, LANES - ATT_HEADS))) - dfk
    dsm_f, G["fox_bf_vec"] = _fox_post(dfc, hbuf, W["fox_bf_vec"], name=nm("fox_post"))
    (dxr, dz, dsm_dt, G["ssd_norm_g"], G["ssd_d_exp"], G["ssd_a_vec"], G["ssd_dtb_vec"], G["ssd_conv_w"],
     G["ssd_conv_b"]) = _ssd_bwd(dymix, hbuf, sv["yssd"], sv["states"], W["ssd_conv_w"], W["ssd_conv_b"],
                                 W["ssd_dtb_vec"], W["ssd_a_vec"], W["ssd_d_exp"], W["ssd_norm_g"], name=nm("ssd_bwd"))
    t = dx4.shape[0]
    dh = jnp.concatenate([dxr.astype(BF16), dz.astype(BF16), dur.astype(BF16), dgr.astype(BF16), dq.astype(BF16),
                          dk.astype(BF16), dv.astype(BF16), (dsm_f + dsm_dt).astype(BF16),
                          jnp.zeros((t, H_WIDTH - COL_SMALL - LANES), BF16)], axis=1)
    G["w_in_p"] = _mm(sv["x1b"], dh, ta=True, name=nm("d_w_in"))
    dr1, dr1b, G["ln1_g"], G["ln1_b"] = _bwd_proj([(dh, W["w_in_p"])], dr2, rscale=ALPHA,
                                                  ln=(sv["xh1"], sv["rs1"], W["ln1_g"]), name=nm("ln1_bwd"))
    G["ffn1_wd"] = _mm(sv["a1"], dr1b, ta=True, scale=0.5, out_dtype=BF16, name=nm("d_ffn1_wd"))
    dg1, du1 = _mm_swiglu_bwd(dr1b, W["ffn1_wd"], sv["g1"], sv["u1"], scale=0.5, name=nm("ffn1_act_bwd"))
    G["ffn1_wg"] = _mm(sv["x_in_b"], dg1, ta=True, out_dtype=BF16, chip_cols=True, name=nm("d_ffn1_wg"))
    G["ffn1_wu"] = _mm(sv["x_in_b"], du1, ta=True, out_dtype=BF16, chip_cols=True, name=nm("d_ffn1_wu"))
    (dx_in,) = _bwd_proj([(dg1, W["ffn1_wg"]), (du1, W["ffn1_wu"])], dr1, rscale=ALPHA, ln=None, name=nm("x_in_bwd"))
    return dx_in, G, comm_out


def _block_diag(w):
    n, b, _ = w.shape
    eye = jnp.eye(n, dtype=w.dtype)
    return (eye[:, None, :, None] * w[:, :, None, :]).reshape(n * b, n * b)


def _block_diag_extract(m):
    n, b = LRU_HEADS, HEAD_DIM
    return jnp.stack([m[b * i:b * (i + 1), b * i:b * (i + 1)] for i in range(n)])


def _lane_vec(v, lane0):
    return jnp.pad(v.astype(F32), (lane0, LANES - lane0 - v.shape[0])).reshape(1, LANES)


def _w_in_permute(w):
    d = w.shape[0]
    z = lambda n: jnp.zeros((d, n), w.dtype)
    return jnp.concatenate([w[:, 1796:2820], w[:, 1284:1796], w[:, 0:512], w[:, 512:1280],
                            w[:, 1280:1284], w[:, 2820:2828], z(LANES - 12), z(H_WIDTH - COL_SMALL - LANES)], axis=1)


def _w_in_unpermute(wp):
    return jnp.concatenate([wp[:, COL_U:COL_Q], wp[:, COL_Q:COL_SMALL], wp[:, COL_SMALL:COL_SMALL + 4],
                            wp[:, COL_Z:COL_U], wp[:, COL_XBC:COL_Z], wp[:, COL_SMALL + 4:COL_SMALL + 12]], axis=1)


def _big_weights(chipw):
    W = {}
    for n, w in chipw.items():
        if n in ("ffn1_wg", "ffn1_wu", "ffn2_wg", "ffn2_wu"):
            W[n] = w
        elif n in ("ffn1_wd", "ffn2_wd", "w_out", "pe_gate_w"):
            W[n] = w.reshape(-1, D_MODEL)
        elif n == "pe_proj":
            W[n] = jnp.moveaxis(w, 0, 1).reshape(PLE_DIM, D_MODEL)
        else:
            w_in = jnp.moveaxis(w[:, :, :IN_WIDTH // N_CHIPS], 0, 1).reshape(D_MODEL, IN_WIDTH)
            W["w_in_p"] = _w_in_permute(w_in)
    return W


def _small_weights(li, small):
    g = lambda n: small[n][li]
    W = {n: g(n) for n in ("ln1_g", "ln1_b", "ln2_g", "ln2_b", "ln3_g", "ln3_b", "pe_gate_b", "lru_conv_w",
                           "ssd_conv_w")}
    for n in ("lru_conv_b", "lru_ba", "lru_bx", "lru_lambda", "ssd_conv_b", "ssd_norm_g"):
        W[n] = g(n).reshape(1, -1)
    W["lru_wa_bd"] = _block_diag(g("lru_wa")).astype(BF16)
    W["lru_wx_bd"] = _block_diag(g("lru_wx")).astype(BF16)
    W["fox_bf_vec"] = _lane_vec(g("fox_bf"), LANE_F)
    W["ssd_dtb_vec"] = _lane_vec(g("ssd_dt_bias"), LANE_DT)
    W["ssd_a_vec"] = _lane_vec(-jnp.exp(g("ssd_a_log")), LANE_DT)
    W["ssd_d_exp"] = jnp.repeat(g("ssd_d"), HEAD_DIM).reshape(1, SSD_WIDTH)
    return W


def _big_grad_by_chip(G, n):
    if n in ("ffn1_wg", "ffn1_wu", "ffn2_wg", "ffn2_wu", "pe_proj"):
        return G[n]
    if n in ("ffn1_wd", "ffn2_wd", "w_out", "pe_gate_w"):
        return G[n].reshape(N_CHIPS, -1, D_MODEL)
    share = IN_WIDTH // N_CHIPS
    d_w_in = jnp.moveaxis(_w_in_unpermute(G["w_in_p"]).reshape(D_MODEL, N_CHIPS, share), 1, 0)
    return jnp.pad(d_w_in.astype(BF16), ((0, 0), (0, 0), (0, SHARE - share)))


def _layer_small_grads(G, W):
    out = {n: G[n] for n in ("lru_conv_w", "ssd_conv_w")}
    for n in ("ln1_g", "ln1_b", "ln2_g", "ln2_b", "ln3_g", "ln3_b", "pe_gate_b", "lru_conv_b", "lru_ba", "lru_bx",
              "lru_lambda", "ssd_conv_b", "ssd_norm_g"):
        out[n] = G[n].reshape(-1)
    out["lru_wa"] = _block_diag_extract(G["lru_wa_bd"])
    out["lru_wx"] = _block_diag_extract(G["lru_wx_bd"])
    out["fox_bf"] = G["fox_bf_vec"][0, LANE_F:LANE_F + ATT_HEADS]
    out["ssd_dt_bias"] = G["ssd_dtb_vec"][0, LANE_DT:LANE_DT + SSD_HEADS]
    out["ssd_a_log"] = G["ssd_a_vec"][0, LANE_DT:LANE_DT + SSD_HEADS] * W["ssd_a_vec"][0, LANE_DT:LANE_DT + SSD_HEADS]
    out["ssd_d"] = G["ssd_d_exp"].reshape(SSD_HEADS, HEAD_DIM).sum(axis=1)
    return out


WEIGHTS = ['ln1_g', 'ln1_b', 'ffn1_wg', 'ffn1_wu', 'ffn1_wd', 'w_in', 'lru_conv_w', 'lru_conv_b', 'lru_wa', 'lru_ba',
           'lru_wx', 'lru_bx', 'lru_lambda', 'fox_bf', 'ssd_conv_w', 'ssd_conv_b', 'ssd_dt_bias', 'ssd_a_log', 'ssd_d',
           'ssd_norm_g', 'w_out', 'ln2_g', 'ln2_b', 'ffn2_wg', 'ffn2_wu', 'ffn2_wd', 'ln3_g', 'ln3_b', 'pe_proj',
           'pe_gate_w', 'pe_gate_b']
EARLY = ((("ffn1_wg", "ffn1_wu", "w_in"), 1),
         (("ffn1_wd",), 0))
LATE = ((("ffn2_wg", "ffn2_wu"), 1),
        (("ffn2_wd",), 0),
        (("w_out", "pe_gate_w"), None),
        (("pe_proj",), None))
BIG = {n: pad for names, pad in EARLY + LATE for n in names}
SMALL_SHARDED = {'lru_conv_w': 2, 'ssd_conv_w': 2}
PACK_COLS = 1024


def _unshard(seg, axis):
    moved = jnp.moveaxis(seg, 0, axis)
    shp = list(moved.shape)
    shp[axis:axis + 2] = [shp[axis] * shp[axis + 1]]
    return moved.reshape(shp)


def _pad_axis(a, axis, size):
    if axis is None or a.shape[axis] == size:
        return a
    pads = [(0, 0)] * a.ndim
    pads[axis] = (0, size - a.shape[axis])
    return jnp.pad(a, pads)


def _pack(arrs, dtype, cols):
    flat = jnp.concatenate([a.astype(dtype).reshape(-1) for a in arrs])
    pad = (-flat.shape[0]) % cols
    if pad:
        flat = jnp.concatenate([flat, jnp.zeros((pad,), dtype)])
    return flat.reshape(-1, cols)


def _unpack(flat, shapes):
    out, off = [], 0
    for s in shapes:
        n = math.prod(s)
        out.append(flat[off:off + n].reshape(s))
        off += n
    return out


def kernel(x, p, ln1_g, ln1_b, ffn1_wg, ffn1_wu, ffn1_wd, w_in, lru_conv_w, lru_conv_b, lru_wa, lru_ba, lru_wx, lru_bx, lru_lambda, fox_bf, ssd_conv_w, ssd_conv_b, ssd_dt_bias, ssd_a_log, ssd_d, ssd_norm_g, w_out, ln2_g, ln2_b, ffn2_wg, ffn2_wu, ffn2_wd, ln3_g, ln3_b, pe_proj, pe_gate_w, pe_gate_b, loss_target, m_ln1_g, m_ln1_b, m_ffn1_wg, m_ffn1_wu, m_ffn1_wd, m_w_in, m_lru_conv_w, m_lru_conv_b, m_lru_wa, m_lru_ba, m_lru_wx, m_lru_bx, m_lru_lambda, m_fox_bf, m_ssd_conv_w, m_ssd_conv_b, m_ssd_dt_bias, m_ssd_a_log, m_ssd_d, m_ssd_norm_g, m_w_out, m_ln2_g, m_ln2_b, m_ffn2_wg, m_ffn2_wu, m_ffn2_wd, m_ln3_g, m_ln3_b, m_pe_proj, m_pe_gate_w, m_pe_gate_b, v_ln1_g, v_ln1_b, v_ffn1_wg, v_ffn1_wu, v_ffn1_wd, v_w_in, v_lru_conv_w, v_lru_conv_b, v_lru_wa, v_lru_ba, v_lru_wx, v_lru_bx, v_lru_lambda, v_fox_bf, v_ssd_conv_w, v_ssd_conv_b, v_ssd_dt_bias, v_ssd_a_log, v_ssd_d, v_ssd_norm_g, v_w_out, v_ln2_g, v_ln2_b, v_ffn2_wg, v_ffn2_wu, v_ffn2_wd, v_ln3_g, v_ln3_b, v_pe_proj, v_pe_gate_w, v_pe_gate_b):
    args = locals()
    w_loc = {n: args[n] for n in WEIGHTS}
    m_loc = {n: args["m_" + n] for n in WEIGHTS}
    v_loc = {n: args["v_" + n] for n in WEIGHTS}
    chip = 2 * lax.axis_index("x") + lax.axis_index("y")
    core = lax.axis_index("c")
    big = list(BIG)
    small_sh = list(SMALL_SHARDED)
    small_rep = [n for n in WEIGHTS if n not in BIG and n not in SMALL_SHARDED]

    def srcs_of(li, groups):
        return [jnp.stack([_pad_axis(w_loc[n][li].astype(BF16), pad, SHARE) for n in names]) for names, pad in groups]

    def chip_weights(gathered, srcs, groups):
        out = {}
        for (names, _), g, s in zip(groups, gathered, srcs):
            g = lax.dynamic_update_index_in_dim(g, s, chip, 0)
            for j, n in enumerate(names):
                out[n] = g[:, j]
        return _big_weights(out)

    def pair_sums(G, groups, tag):
        gs = [jnp.stack([_big_grad_by_chip(G, n) for n in names]) for names, _ in groups]
        flat = [g.reshape((-1,) + g.shape[2:]) for g in gs]
        theirs = _exchange(flat, ("c",), swap=True, name=f"reduce_cores_{tag}")
        return [_sum_pair(f, r, BF16, name=f"reduce_cores_sum_{tag}_{gi}").reshape(g.shape)
                for gi, (f, r, g) in enumerate(zip(flat, theirs, gs))]

    def finish_reduce(quad, sums, li, groups, tag):
        quad = [lax.dynamic_update_index_in_dim(q, lax.dynamic_index_in_dim(s, chip, 1, keepdims=False), chip, 0)
                for q, s in zip(quad, sums)]
        red = [_sum_slots(q.reshape(N_CHIPS, -1, q.shape[-1]), F32,
                          name=f"reduce_chips_sum_{tag}_{gi}").reshape(q.shape[1:]) for gi, q in enumerate(quad)]
        theirs = _exchange(red, ("c",), swap=True, name=f"reduce_share_{tag}")
        out = {}
        for (names, _), r, rv in zip(groups, red, theirs):
            both = jnp.where(core == li, r, rv)
            for j, n in enumerate(names):
                out[n] = both[j]
        return out

    everything = EARLY + LATE
    src_e0, src_l0, src_1 = srcs_of(0, EARLY), srcs_of(0, LATE), srcs_of(1, everything)
    gathered = _run_comm(_gather_layer_comm(src_e0, 0), name="gather_w_l0")
    small = {n: w_loc[n] for n in small_rep}
    spack = _pack([w_loc[n] for n in small_sh], F32, LANES)
    (sg,) = _exchange([spack[None]], ("x", "y"), name="gather_conv_w")
    for n, seg in zip(small_sh, _unpack_rows(sg.reshape(N_CHIPS, -1), [w_loc[n].shape for n in small_sh])):
        small[n] = _unshard(seg, SMALL_SHARDED[n])

    W0 = {**_small_weights(0, small), **chip_weights(gathered, src_e0, EARLY)}
    xs = x[0]
    xs, xb, sv0, W0, gathered = _layer_fwd(
        0, xs, xs.astype(BF16), p[0, 0].astype(BF16), W0, comm=_gather_layer_comm(src_1, 1),
        late=(_gather_layer_comm(src_l0, 0), lambda got: chip_weights(got, src_l0, LATE)))
    W1 = {**_small_weights(1, small), **chip_weights(gathered, src_1, everything)}
    xs, _, sv1, _, _ = _layer_fwd(1, xs, xb, p[1, 0].astype(BF16), W1)
    dx, loss = _loss_kernel(xs, loss_target[0], name="loss")
    loss = lax.psum(loss[0, 0], MESH_AXES)
    dx, G1, _ = _layer_bwd(1, dx, sv1, W1)
    sums1 = pair_sums(G1, everything, "l1")
    comm1 = _reduce_chips_comm(sums1, 1)
    late_sums = []

    def late0(G):
        late_sums.extend(pair_sums(G, LATE, "l0_late"))
        return _reduce_chips_comm(late_sums, 0, base=comm1.n_sems)

    grad_x, G0, quads = _layer_bwd(0, dx, sv0, W0, comm=comm1, late=late0)

    n1 = len(comm1.out_shapes)
    red = [{**finish_reduce(quads[n1:], late_sums, 0, LATE, "l0_late")},
           finish_reduce(quads[:n1], sums1, 1, everything, "l1")]
    sums0 = pair_sums(G0, EARLY, "l0")
    red[0].update(finish_reduce(_run_comm(_reduce_chips_comm(sums0, 0), name="reduce_chips_l0"), sums0, 0, EARLY, "l0"))
    g_red = {}
    for n in big:
        g = jnp.stack([red[li][n] for li in range(DEPTH)])
        g_red[n] = g[tuple(slice(0, s) for s in w_loc[n].shape)]
    small_l = [_layer_small_grads(G0, W0), _layer_small_grads(G1, W1)]
    g_small = {n: jnp.stack([small_l[li][n] for li in range(DEPTH)]) for n in small_l[0]}
    small_all = small_rep + small_sh
    sgp = _pack([g_small[n] for n in small_all], F32, PACK_COLS)
    (sall,) = _exchange([sgp[None]], MESH_AXES, name="reduce_small")
    sred = _sum_slots(sall.reshape((2 ** len(MESH_AXES),) + sgp.shape), F32, name="reduce_small_sum").reshape(-1)
    for n, g in zip(small_all, _unpack(sred, [g_small[n].shape for n in small_all])):
        if n in SMALL_SHARDED:
            width = w_loc[n].shape[-1]
            g = lax.dynamic_slice_in_dim(g, chip * width, width, axis=SMALL_SHARDED[n])
        g_red[n] = g

    delta, new_m, new_v = {}, {}, {}
    for n in big:
        delta[n], new_m[n], new_v[n] = _adamw(w_loc[n], g_red[n], m_loc[n], v_loc[n], name="adamw_" + n)
    shapes = [w_loc[n].shape for n in small_all]
    packs = [_pack([d[n] for n in small_all], F32, LANES) for d in (w_loc, g_red, m_loc, v_loc)]
    outs = _adamw(*packs, name="adamw_small")
    for d, o in zip((delta, new_m, new_v), outs):
        for n, a in zip(small_all, _unpack(o.reshape(-1), shapes)):
            d[n] = a
    return (loss, grad_x[None], *[g_red[n] for n in WEIGHTS], *[delta[n] for n in WEIGHTS],
            *[new_m[n] for n in WEIGHTS], *[new_v[n] for n in WEIGHTS])


def _unpack_rows(gathered, shapes):
    out, off = [], 0
    for s in shapes:
        n = math.prod(s)
        out.append(gathered[:, off:off + n].reshape((N_CHIPS,) + tuple(s)))
        off += n
    return out
```

```python
import functools
import math

import jax
import jax.numpy as jnp
from jax import lax
from jax.experimental import pallas as pl
from jax.experimental.pallas import tpu as pltpu

F32 = jnp.float32
BF16 = jnp.bfloat16

D_MODEL = 1024
DEPTH = 2
PLE_DIM = 256
HEAD_DIM = 64
LRU_WIDTH = 256
LRU_HEADS = 4
LRU_C = 8.0
CONV_K = 4
ATT_WIDTH = 256
ATT_HEADS = 4
SSD_WIDTH = 512
SSD_HEADS = 8
SSD_GROUPS = 2
SSD_STATE = 128
SSD_CHUNK = 128
SSD_CONV_DIM = 1024
FFN_DIM = 2816
ALPHA = (2.0 * DEPTH) ** 0.25
LN_EPS = 1e-5
RMS_EPS = 1e-5
IN_WIDTH = 2828
ADAM_LR = 0.001
ADAM_B1 = 0.9
ADAM_B2 = 0.999
ADAM_EPS = 1e-08
ADAM_WD = 0.01
ADAM_STEP = 10

H_WIDTH = 3072
COL_XBC, COL_Z, COL_U, COL_G, COL_Q, COL_K, COL_V, COL_SMALL = 0, 1024, 1536, 1792, 2048, 2304, 2560, 2816
LANE_F = 0
LANE_DT = 4
LANES = 128
SUBLANES = 8
NEG = -1e30

VMEM_LIMIT = 48 * 1024 * 1024

N_CHIPS = 4
MESH_AXES = ("x", "y", "c")
SHARE = 768


def _params(n):
    return pltpu.CompilerParams(dimension_semantics=("arbitrary",) * n, vmem_limit_bytes=VMEM_LIMIT)


def _pick(n, cands):
    for c in cands:
        if n % c == 0:
            return c
    return n


def _iota(shape, dim):
    return lax.broadcasted_iota(jnp.int32, shape, dim)


def _shift_down(x, s, prev8):
    if s == 0:
        return x
    r = pltpu.roll(x, s, 0)
    pr = pltpu.roll(prev8, s, 0)
    head = jnp.where(_iota(pr.shape, 0) < s, pr, r[:SUBLANES])
    return jnp.concatenate([head, r[SUBLANES:]], axis=0)


def _shift_up(x, s, next8):
    if s == 0:
        return x
    n = x.shape[0]
    r = pltpu.roll(x, n - s, 0)
    nr = pltpu.roll(next8, SUBLANES - s, 0)
    tail = jnp.where(_iota(nr.shape, 0) >= SUBLANES - s, nr, r[n - SUBLANES:])
    return jnp.concatenate([r[:n - SUBLANES], tail], axis=0)


def _scan_fwd(a, b):
    n = a.shape[0]
    row = _iota(a.shape, 0)
    d = 1
    while d < n:
        keep = row >= d
        a_s = jnp.where(keep, pltpu.roll(a, d, 0), 1.0)
        b_s = jnp.where(keep, pltpu.roll(b, d, 0), 0.0)
        b = a * b_s + b
        a = a * a_s
        d *= 2
    return a, b


def _scan_bwd(a, b):
    n = a.shape[0]
    row = _iota(a.shape, 0)
    d = 1
    while d < n:
        keep = row < n - d
        a_s = jnp.where(keep, pltpu.roll(a, n - d, 0), 1.0)
        b_s = jnp.where(keep, pltpu.roll(b, n - d, 0), 0.0)
        b = a * b_s + b
        a = a * a_s
        d *= 2
    return a, b


def _cumsum_rows(x, reverse=False):
    n = x.shape[0]
    row = _iota(x.shape, 0)
    d = 1
    while d < n:
        if reverse:
            x = x + jnp.where(row < n - d, pltpu.roll(x, n - d, 0), 0.0)
        else:
            x = x + jnp.where(row >= d, pltpu.roll(x, d, 0), 0.0)
        d *= 2
    return x


def _col(x, lane):
    return jnp.sum(jnp.where(_iota(x.shape, 1) == lane, x, 0.0), axis=1, keepdims=True)


def _row(x, r):
    return jnp.sum(jnp.where(_iota(x.shape, 0) == r, x, 0.0), axis=0, keepdims=True)


def _sigmoid(x):
    return jax.nn.sigmoid(x)


def _softplus(x):
    return jnp.maximum(x, 0.0) + jnp.log(1.0 + jnp.exp(-jnp.abs(x)))


def _gelu_and_grad(x):
    c0 = math.sqrt(2.0 / math.pi)
    inner = c0 * (x + 0.044715 * x * x * x)
    t = jnp.tanh(inner)
    g = 0.5 * x * (1.0 + t)
    dg = 0.5 * (1.0 + t) + 0.5 * x * (1.0 - t * t) * c0 * (1.0 + 3.0 * 0.044715 * x * x)
    return g, dg


def _dot(a, b, ca, cb):
    return lax.dot_general(a, b, (((ca,), (cb,)), ((), ())), preferred_element_type=F32)


def _conv_taps(xr, prev8, w, bias):
    y = bias + w[CONV_K - 1:CONV_K, :] * xr
    for j in range(CONV_K - 1):
        y = y + w[j:j + 1, :] * _shift_down(xr, CONV_K - 1 - j, prev8)
    return y


def _conv_taps_bwd(dy, next8, w, xr):
    dx = None
    dws = []
    for j in range(CONV_K):
        sh = _shift_up(dy, CONV_K - 1 - j, next8)
        term = w[j:j + 1, :] * sh
        dx = term if dx is None else dx + term
        dws.append(jnp.sum(sh * xr, axis=0, keepdims=True))
    return dx, jnp.concatenate(dws, axis=0)


def _head_expand(v, lane0, nheads, width):
    rows = v.shape[0]
    colhead = _iota((rows, width), 1) // HEAD_DIM
    out = jnp.zeros((rows, width), F32)
    for h in range(nheads):
        out = jnp.where(colhead == h, _col(v, lane0 + h), out)
    return out


def _head_reduce(x, lane0, nheads):
    rows = x.shape[0]
    colhead = _iota(x.shape, 1) // HEAD_DIM
    lane = _iota((rows, LANES), 1)
    out = jnp.zeros((rows, LANES), F32)
    for h in range(nheads):
        s = jnp.sum(jnp.where(colhead == h, x, 0.0), axis=1, keepdims=True)
        out = jnp.where(lane == lane0 + h, s, out)
    return out


def _mm(a, b, *, ta=False, tb=False, scale=1.0, out_dtype=F32, chip_cols=False, name):
    if ta:
        kk, m = a.shape
    else:
        m, kk = a.shape
    n = b.shape[0] if tb else b.shape[1]
    tm = _pick(m, (1024, 512, 256, 128))
    tn = _pick(n // N_CHIPS, (768, 256, 128)) if chip_cols else _pick(n, (1024, 768, 512, 256, 128))
    tk = _pick(kk, (1024, 768, 512, 256, 128))
    nk = kk // tk
    dn_a = 0 if ta else 1
    dn_b = 1 if tb else 0
    if chip_cols:
        per = n // N_CHIPS // tn
        out_spec = pl.BlockSpec((None, tm, tn), lambda i, j, k: (j // per, i, j % per))
        out_shape = jax.ShapeDtypeStruct((N_CHIPS, m, n // N_CHIPS), out_dtype)
    else:
        out_spec = pl.BlockSpec((tm, tn), lambda i, j, k: (i, j))
        out_shape = jax.ShapeDtypeStruct((m, n), out_dtype)

    def body(a_ref, b_ref, o_ref, acc):
        k = pl.program_id(2)

        @pl.when(k == 0)
        def _():
            acc[...] = jnp.zeros_like(acc)

        acc[...] += _dot(a_ref[...].astype(BF16), b_ref[...].astype(BF16), dn_a, dn_b)

        @pl.when(k == nk - 1)
        def _():
            o_ref[...] = (acc[...] * scale).astype(out_dtype)

    a_spec = pl.BlockSpec((tk, tm), lambda i, j, k: (k, i)) if ta else pl.BlockSpec((tm, tk), lambda i, j, k: (i, k))
    b_spec = pl.BlockSpec((tn, tk), lambda i, j, k: (j, k)) if tb else pl.BlockSpec((tk, tn), lambda i, j, k: (k, j))
    return pl.pallas_call(
        body, name=name, grid=(m // tm, n // tn, nk),
        in_specs=[a_spec, b_spec],
        out_specs=out_spec, out_shape=out_shape,
        scratch_shapes=[pltpu.VMEM((tm, tn), F32)],
        compiler_params=_params(3),
    )(a, b)


def _mm_swiglu(xb, wg, wu, *, comm=None, name):
    t, d = xb.shape
    share = wg.shape[2]
    n = N_CHIPS * share
    tm = _pick(t, (512, 256, 128))
    tn = _pick(share, (768, 256, 128))
    per = share // tn

    def body(x_ref, wg_ref, wu_ref, g_ref, u_ref, a_ref):
        x = x_ref[...]
        g = _dot(x, wg_ref[...], 1, 0)
        u = _dot(x, wu_ref[...], 1, 0)
        g_ref[...] = g.astype(BF16)
        u_ref[...] = u.astype(BF16)
        a_ref[...] = (g * _sigmoid(g) * u).astype(BF16)

    o = jax.ShapeDtypeStruct((t, n), BF16)
    ospec = pl.BlockSpec((tm, tn), lambda j, i: (i, j))
    return _hosted_call(
        body, comm, (n // tn, t // tm), name=name,
        in_specs=[pl.BlockSpec((tm, d), lambda j, i: (i, 0)),
                  pl.BlockSpec((None, d, tn), lambda j, i: (j // per, 0, j % per)),
                  pl.BlockSpec((None, d, tn), lambda j, i: (j // per, 0, j % per))],
        out_specs=[ospec, ospec, ospec], out_shape=[o, o, o], scratch_shapes=[], args=[xb, wg, wu])


def _mm_swiglu_bwd(dr, wd, g, u, *, scale, name):
    t, d = dr.shape
    n = wd.shape[0]
    tm = _pick(t, (512, 256, 128))
    tn = _pick(n, (768, 256, 128))

    def body(dr_ref, wd_ref, g_ref, u_ref, dg_ref, du_ref):
        da = _dot(dr_ref[...].astype(BF16), wd_ref[...], 1, 1) * scale
        gg = g_ref[...].astype(F32)
        uu = u_ref[...].astype(F32)
        sg = _sigmoid(gg)
        dg_ref[...] = (da * uu * (sg * (1.0 + gg * (1.0 - sg)))).astype(BF16)
        du_ref[...] = (da * gg * sg).astype(BF16)

    o = jax.ShapeDtypeStruct((t, n), BF16)
    ospec = pl.BlockSpec((tm, tn), lambda j, i: (i, j))
    return pl.pallas_call(
        body, name=name, grid=(n // tn, t // tm),
        in_specs=[pl.BlockSpec((tm, d), lambda j, i: (i, 0)),
                  pl.BlockSpec((tn, d), lambda j, i: (j, 0)),
                  ospec, ospec],
        out_specs=[ospec, ospec], out_shape=[o, o],
        compiler_params=_params(2),
    )(dr, wd, g, u)


def _mm_ln(a, w, resid, gain, bias, *, rscale, mscale, name):
    t, kk = a.shape
    d = w.shape[1]
    tm = _pick(t, (512, 256, 128))
    tk = kk
    nk = kk // tk

    def body(a_ref, w_ref, r_ref, g_ref, b_ref, y_ref, yb_ref, xh_ref, rs_ref, acc):
        k = pl.program_id(1)

        @pl.when(k == 0)
        def _():
            acc[...] = jnp.zeros_like(acc)

        acc[...] += _dot(a_ref[...].astype(BF16), w_ref[...], 1, 0)

        @pl.when(k == nk - 1)
        def _():
            r = rscale * r_ref[...] + mscale * acc[...]
            mu = jnp.mean(r, axis=1, keepdims=True)
            xc = r - mu
            var = jnp.mean(xc * xc, axis=1, keepdims=True)
            rstd = lax.rsqrt(var + LN_EPS)
            xh = xc * rstd
            y = xh * g_ref[...] + b_ref[...]
            y_ref[...] = y
            yb_ref[...] = y.astype(BF16)
            xh_ref[...] = xh
            rs_ref[...] = rstd

    row = pl.BlockSpec((tm, d), lambda i, k: (i, 0))
    vec = pl.BlockSpec((1, d), lambda i, k: (0, 0))
    return pl.pallas_call(
        body, name=name, grid=(t // tm, nk),
        in_specs=[pl.BlockSpec((tm, tk), lambda i, k: (i, k)),
                  pl.BlockSpec((tk, d), lambda i, k: (k, 0)), row, vec, vec],
        out_specs=[row, row, row, pl.BlockSpec((tm, 1), lambda i, k: (i, 0))],
        out_shape=[jax.ShapeDtypeStruct((t, d), F32), jax.ShapeDtypeStruct((t, d), BF16),
                   jax.ShapeDtypeStruct((t, d), F32), jax.ShapeDtypeStruct((t, 1), F32)],
        scratch_shapes=[pltpu.VMEM((tm, d), F32)],
        compiler_params=_params(2),
    )(a, w, resid, gain.reshape(1, d), bias.reshape(1, d))


def _bwd_proj(pairs, resid, *, rscale, ln, name):
    t, kk = pairs[0][0].shape
    d = pairs[0][1].shape[-2]
    tm = _pick(t, (512, 256, 128))
    tk = _pick(pairs[0][1].shape[-1], (1024, 768, 512, 256, 128))
    nk = kk // tk
    nt = t // tm
    npair = len(pairs)
    has_ln = ln is not None

    def body(*refs):
        ab = refs[:2 * npair]
        r_ref = refs[2 * npair]
        pos = 2 * npair + 1
        if has_ln:
            xh_ref, rs_ref, g_ref = refs[pos:pos + 3]
            pos += 3
            o_ref, ob_ref, dg_ref, db_ref = refs[pos:pos + 4]
            pos += 4
        else:
            o_ref = refs[pos]
            pos += 1
        acc = refs[pos]
        i = pl.program_id(0)
        k = pl.program_id(1)

        @pl.when(k == 0)
        def _():
            acc[...] = jnp.zeros_like(acc)

        for q in range(npair):
            acc[...] += _dot(ab[2 * q][...].astype(BF16), ab[2 * q + 1][...], 1, 1)

        @pl.when(k == nk - 1)
        def _():
            dy = rscale * r_ref[...] + acc[...]
            if not has_ln:
                o_ref[...] = dy
                return
            xh = xh_ref[...]
            w = dy * g_ref[...]
            m1 = jnp.mean(w, axis=1, keepdims=True)
            m2 = jnp.mean(w * xh, axis=1, keepdims=True)
            dr = rs_ref[...] * (w - m1 - xh * m2)
            o_ref[...] = dr
            ob_ref[...] = dr.astype(BF16)

            @pl.when(i == 0)
            def _():
                dg_ref[...] = jnp.zeros_like(dg_ref)
                db_ref[...] = jnp.zeros_like(db_ref)

            dg_ref[...] += jnp.sum(dy * xh, axis=0, keepdims=True)
            db_ref[...] += jnp.sum(dy, axis=0, keepdims=True)

    row = pl.BlockSpec((tm, d), lambda i, k: (i, 0))
    vec = pl.BlockSpec((1, d), lambda i, k: (0, 0))
    in_specs, args = [], []
    for a, b in pairs:
        if b.ndim == 3:
            per = b.shape[2] // tk
            b_spec = pl.BlockSpec((None, d, tk), lambda i, k, per=per: (k // per, 0, k % per))
        else:
            b_spec = pl.BlockSpec((d, tk), lambda i, k: (0, k))
        in_specs += [pl.BlockSpec((tm, tk), lambda i, k: (i, k)), b_spec]
        args += [a, b]
    in_specs.append(row)
    args.append(resid)
    out_specs = [row]
    out_shape = [jax.ShapeDtypeStruct((t, d), F32)]
    if has_ln:
        xh, rs, gain = ln
        in_specs += [row, pl.BlockSpec((tm, 1), lambda i, k: (i, 0)), vec]
        args += [xh, rs, gain.reshape(1, d)]
        out_specs += [row, vec, vec]
        out_shape += [jax.ShapeDtypeStruct((t, d), BF16)] + [jax.ShapeDtypeStruct((1, d), F32)] * 2
    return pl.pallas_call(
        body, name=name, grid=(nt, nk), in_specs=in_specs, out_specs=out_specs, out_shape=out_shape,
        scratch_shapes=[pltpu.VMEM((tm, d), F32)],
        compiler_params=_params(2),
    )(*args)


def _mm_pe(x3, x3b, pb, wgate, bgate, wproj, *, name):
    t, d = x3.shape
    pd = pb.shape[1]
    tm = _pick(t, (512, 256, 128))
    tn = _pick(d, (512, 256, 128))

    def body(x_ref, xb_ref, p_ref, wg_ref, bg_ref, wp_ref, y_ref, yb_ref, sg_ref, e_ref):
        sg = _sigmoid(_dot(xb_ref[...], wg_ref[...], 1, 0) + bg_ref[...])
        e = _dot(p_ref[...], wp_ref[...], 1, 0)
        y = x_ref[...] + sg * e
        y_ref[...] = y
        yb_ref[...] = y.astype(BF16)
        sg_ref[...] = sg.astype(BF16)
        e_ref[...] = e.astype(BF16)

    ospec = pl.BlockSpec((tm, tn), lambda i, j: (i, j))
    ob = jax.ShapeDtypeStruct((t, d), BF16)
    return pl.pallas_call(
        body, name=name, grid=(t // tm, d // tn),
        in_specs=[ospec, pl.BlockSpec((tm, d), lambda i, j: (i, 0)), pl.BlockSpec((tm, pd), lambda i, j: (i, 0)),
                  pl.BlockSpec((d, tn), lambda i, j: (0, j)), pl.BlockSpec((1, tn), lambda i, j: (0, j)),
                  pl.BlockSpec((pd, tn), lambda i, j: (0, j))],
        out_specs=[ospec, ospec, ospec, ospec],
        out_shape=[jax.ShapeDtypeStruct((t, d), F32), ob, ob, ob],
        compiler_params=_params(2),
    )(x3, x3b, pb, wgate, bgate.reshape(1, d), wproj)


def _pe_bwd_elem(dx4, sg, e, *, name):
    t, d = dx4.shape
    tm = _pick(t, (512, 256, 128))

    def body(dx_ref, sg_ref, e_ref, dgp_ref, de_ref, db_ref):
        dx = dx_ref[...]
        s = sg_ref[...].astype(F32)
        dgp = dx * e_ref[...].astype(F32) * s * (1.0 - s)
        dgp_ref[...] = dgp.astype(BF16)
        de_ref[...] = (dx * s).astype(BF16)

        @pl.when(pl.program_id(0) == 0)
        def _():
            db_ref[...] = jnp.zeros_like(db_ref)

        db_ref[...] += jnp.sum(dgp, axis=0, keepdims=True)

    row = pl.BlockSpec((tm, d), lambda i: (i, 0))
    ob = jax.ShapeDtypeStruct((t, d), BF16)
    return pl.pallas_call(
        body, name=name, grid=(t // tm,), in_specs=[row, row, row],
        out_specs=[row, row, pl.BlockSpec((1, d), lambda i: (0, 0))],
        out_shape=[ob, ob, jax.ShapeDtypeStruct((1, d), F32)],
        compiler_params=_params(1),
    )(dx4, sg, e)


def _loss_kernel(y, target, *, name):
    t, d = y.shape
    tm = _pick(t, (512, 256, 128))

    def body(y_ref, t_ref, dy_ref, l_ref):
        diff = y_ref[...] - t_ref[...]
        dy_ref[...] = diff * (1.0 / d)

        @pl.when(pl.program_id(0) == 0)
        def _():
            l_ref[...] = jnp.zeros_like(l_ref)

        part = jnp.sum(jnp.mean(diff * diff, axis=1, keepdims=True), axis=0, keepdims=True)
        l_ref[...] += 0.5 * part

    row = pl.BlockSpec((tm, d), lambda i: (i, 0))
    return pl.pallas_call(
        body, name=name, grid=(t // tm,), in_specs=[row, row],
        out_specs=[row, pl.BlockSpec((1, 1), lambda i: (0, 0))],
        out_shape=[jax.ShapeDtypeStruct((t, d), F32), jax.ShapeDtypeStruct((1, 1), F32)],
        compiler_params=_params(1),
    )(y, target)


LRU_TM = 256


def _lru_gate_terms(r, lam):
    sp = _softplus(-lam)
    la = -LRU_C * r * sp
    a = jnp.exp(la)
    em = jnp.tanh(la) * (jnp.exp(2.0 * la) + 1.0)
    s = jnp.sqrt(-em)
    return la, a, s, sp


def _lru_fwd(hbuf, conv_w, conv_b, wa, ba, wx, bx, lam, *, name):
    t = hbuf.shape[0]
    w = LRU_WIDTH
    tm = _pick(t, (LRU_TM, 128))
    cu, cg = COL_U // w, COL_G // w
    hb = tm // SUBLANES

    def body(u_ref, up_ref, g_ref, cw_ref, cb_ref, wa_ref, ba_ref, wx_ref, bx_ref, lam_ref,
             y_ref, u_out, r_out, i_out, a_out, h_out, carry):
        i = pl.program_id(0)

        @pl.when(i == 0)
        def _():
            carry[...] = jnp.zeros_like(carry)

        prev = jnp.where(i == 0, 0.0, up_ref[...])
        u = _conv_taps(u_ref[...], prev, cw_ref[...], cb_ref[...])
        ub = u.astype(BF16)
        r = _sigmoid(_dot(ub, wa_ref[...], 1, 0) + ba_ref[...])
        ig = _sigmoid(_dot(ub, wx_ref[...], 1, 0) + bx_ref[...])
        _, a, s, _ = _lru_gate_terms(r, lam_ref[...])
        b = s * (ig * u)
        acum, hs = _scan_fwd(a, b)
        h = hs + acum * carry[0:1, :]
        carry[...] = jnp.broadcast_to(h[tm - 1:tm, :], carry.shape)
        gl, _ = _gelu_and_grad(g_ref[...])
        y_ref[...] = h * gl
        u_out[...] = u
        r_out[...] = r
        i_out[...] = ig
        a_out[...] = a
        h_out[...] = h

    row = pl.BlockSpec((tm, w), lambda i: (i, 0))
    vec = pl.BlockSpec((1, w), lambda i: (0, 0))
    mat = pl.BlockSpec((w, w), lambda i: (0, 0))
    o = jax.ShapeDtypeStruct((t, w), F32)
    return pl.pallas_call(
        body, name=name, grid=(t // tm,),
        in_specs=[pl.BlockSpec((tm, w), lambda i: (i, cu)),
                  pl.BlockSpec((SUBLANES, w), lambda i: (jnp.maximum(i * hb - 1, 0), cu)),
                  pl.BlockSpec((tm, w), lambda i: (i, cg)),
                  pl.BlockSpec((CONV_K, w), lambda i: (0, 0)), vec, mat, vec, mat, vec, vec],
        out_specs=[row] * 6, out_shape=[o] * 6,
        scratch_shapes=[pltpu.VMEM((SUBLANES, w), F32)],
        compiler_params=_params(1),
    )(hbuf, hbuf, hbuf, conv_w, conv_b, wa, ba, wx, bx, lam)


def _lru_bwd(dymix, hbuf, u, r, ig, a, h, conv_w, wa, wx, lam, *, name):
    t = hbuf.shape[0]
    w = LRU_WIDTH
    tm = _pick(t, (LRU_TM, 128))
    nb = t // tm
    cu, cg = COL_U // w, COL_G // w
    hb = tm // SUBLANES
    last8 = t // SUBLANES - 1

    def body(dy_ref, ur_ref, g_ref, u_ref, r_ref, i_ref, a_ref, an_ref, h_ref, hp_ref,
             cw_ref, wa_ref, wx_ref, lam_ref,
             dur_ref, dgr_ref, dcw_ref, dcb_ref, dwa_ref, dba_ref, dwx_ref, dbx_ref, dlam_ref,
             lcarry, dnext):
        i = pl.program_id(0)
        ib = nb - 1 - i

        @pl.when(i == 0)
        def _():
            lcarry[...] = jnp.zeros_like(lcarry)
            dnext[...] = jnp.zeros_like(dnext)
            for ref in (dcw_ref, dcb_ref, dwa_ref, dba_ref, dwx_ref, dbx_ref, dlam_ref):
                ref[...] = jnp.zeros_like(ref)

        dy = dy_ref[...]
        hh = h_ref[...]
        av = a_ref[...]
        uu = u_ref[...]
        rr = r_ref[...]
        ii = i_ref[...]
        lam_v = lam_ref[...]
        gl, dgl = _gelu_and_grad(g_ref[...])
        dgr_ref[...] = (dy * hh * dgl).astype(BF16)
        dh_out = dy * gl
        a_next = _shift_up(av, 1, jnp.where(ib == nb - 1, 0.0, an_ref[...]))
        acum, ls = _scan_bwd(a_next, dh_out)
        lam_adj = ls + acum * lcarry[0:1, :]
        lcarry[...] = jnp.broadcast_to(lam_adj[0:1, :], lcarry.shape)
        h_prev = _shift_down(hh, 1, jnp.where(ib == 0, 0.0, hp_ref[...]))
        da = lam_adj * h_prev
        _, a2, s, sp = _lru_gate_terms(rr, lam_v)
        d_igu = lam_adj * s
        ds = lam_adj * ii * uu
        dla = da * a2 - ds * (a2 * a2) / s
        dr = dla * (-LRU_C * sp)
        dlam_ref[...] += jnp.sum(dla * (LRU_C * rr * _sigmoid(-lam_v)), axis=0, keepdims=True)
        dpre_r = dr * rr * (1.0 - rr)
        dpre_i = d_igu * uu * ii * (1.0 - ii)
        prb = dpre_r.astype(BF16)
        pib = dpre_i.astype(BF16)
        ub = uu.astype(BF16)
        du = d_igu * ii + _dot(prb, wa_ref[...], 1, 1) + _dot(pib, wx_ref[...], 1, 1)
        dwa_ref[...] += _dot(ub, prb, 0, 0)
        dwx_ref[...] += _dot(ub, pib, 0, 0)
        dba_ref[...] += jnp.sum(dpre_r, axis=0, keepdims=True)
        dbx_ref[...] += jnp.sum(dpre_i, axis=0, keepdims=True)
        dur, dws = _conv_taps_bwd(du, dnext[...], cw_ref[...], ur_ref[...])
        dur_ref[...] = dur.astype(BF16)
        dcw_ref[...] += dws
        dcb_ref[...] += jnp.sum(du, axis=0, keepdims=True)
        dnext[...] = du[:SUBLANES]

    def rowspec(col):
        return pl.BlockSpec((tm, w), lambda i: (nb - 1 - i, col))

    row = rowspec(0)
    nxt = pl.BlockSpec((SUBLANES, w), lambda i: (jnp.minimum((nb - i) * hb, last8), 0))
    prv = pl.BlockSpec((SUBLANES, w), lambda i: (jnp.maximum((nb - 1 - i) * hb - 1, 0), 0))
    vec = pl.BlockSpec((1, w), lambda i: (0, 0))
    mat = pl.BlockSpec((w, w), lambda i: (0, 0))
    cw = pl.BlockSpec((CONV_K, w), lambda i: (0, 0))
    o = jax.ShapeDtypeStruct((t, w), BF16)
    v1 = jax.ShapeDtypeStruct((1, w), F32)
    m1 = jax.ShapeDtypeStruct((w, w), F32)
    return pl.pallas_call(
        body, name=name, grid=(nb,),
        in_specs=[rowspec(0), rowspec(cu), rowspec(cg), row, row, row, row, nxt, row, prv, cw, mat, mat, vec],
        out_specs=[row, row, cw, vec, mat, vec, mat, vec, vec],
        out_shape=[o, o, jax.ShapeDtypeStruct((CONV_K, w), F32), v1, m1, v1, m1, v1, v1],
        scratch_shapes=[pltpu.VMEM((SUBLANES, w), F32), pltpu.VMEM((SUBLANES, w), F32)],
        compiler_params=_params(1),
    )(dymix, hbuf, hbuf, u, r, ig, a, a, h, h, conv_w, wa, wx, lam)


FOX_T = 512
FOX_PREP_TM = 256


def _log_sigmoid(x):
    return jnp.minimum(x, 0.0) - jnp.log(1.0 + jnp.exp(-jnp.abs(x)))


def _fox_prep(hbuf, bf_vec, *, name):
    t = hbuf.shape[0]
    tm = _pick(t, (FOX_PREP_TM, 128))
    cs = COL_SMALL // LANES

    def body(s_ref, b_ref, eq_ref, ek_ref, carry):
        i = pl.program_id(0)

        @pl.when(i == 0)
        def _():
            carry[...] = jnp.zeros_like(carry)

        lf = _log_sigmoid(s_ref[...] + b_ref[...])
        f = _cumsum_rows(lf) + carry[0:1, :]
        carry[...] = jnp.broadcast_to(f[tm - 1:tm, :], carry.shape)
        lane = _iota((tm, LANES), 1)
        for h in range(ATT_HEADS):
            base = HEAD_DIM * (1 - h % 2)
            fh = _col(f, h)
            hi = fh.astype(BF16).astype(F32)
            mid = (fh - hi).astype(BF16).astype(F32)
            lo = fh - hi - mid
            terms = jnp.where(lane == base, hi, jnp.where(lane == base + 1, mid, jnp.where(lane == base + 2, lo, 0.0)))
            terms_k = jnp.where(lane == base + 3, -hi,
                                jnp.where(lane == base + 4, -mid, jnp.where(lane == base + 5, -lo, 0.0)))
            ones_q = ((lane >= base + 3) & (lane < base + 6)).astype(F32)
            ones_k = ((lane >= base) & (lane < base + 3)).astype(F32)
            eq_ref[:, LANES * h:LANES * (h + 1)] = (terms + ones_q).astype(BF16)
            ek_ref[:, LANES * h:LANES * (h + 1)] = (terms_k + ones_k).astype(BF16)

    ospec = pl.BlockSpec((tm, ATT_HEADS * LANES), lambda i: (i, 0))
    o = jax.ShapeDtypeStruct((t, ATT_HEADS * LANES), BF16)
    return pl.pallas_call(
        body, name=name, grid=(t // tm,),
        in_specs=[pl.BlockSpec((tm, LANES), lambda i: (i, cs)), pl.BlockSpec((1, LANES), lambda i: (0, 0))],
        out_specs=[ospec, ospec], out_shape=[o, o],
        scratch_shapes=[pltpu.VMEM((SUBLANES, LANES), F32)],
        compiler_params=_params(1),
    )(hbuf, bf_vec)


def _fox_post(dfc, hbuf, bf_vec, *, name):
    t = hbuf.shape[0]
    tm = _pick(t, (FOX_PREP_TM, 128))
    nb = t // tm
    cs = COL_SMALL // LANES

    def body(df_ref, s_ref, b_ref, o_ref, db_ref, carry):
        i = pl.program_id(0)

        @pl.when(i == 0)
        def _():
            carry[...] = jnp.zeros_like(carry)
            db_ref[...] = jnp.zeros_like(db_ref)

        dlf = _cumsum_rows(df_ref[...], reverse=True) + carry[0:1, :]
        carry[...] = jnp.broadcast_to(dlf[0:1, :], carry.shape)
        dl = dlf * _sigmoid(-(s_ref[...] + b_ref[...]))
        dl = jnp.where(_iota(dl.shape, 1) < ATT_HEADS, dl, 0.0)
        o_ref[...] = dl
        db_ref[...] += jnp.sum(dl, axis=0, keepdims=True)

    vec = pl.BlockSpec((1, LANES), lambda i: (0, 0))
    return pl.pallas_call(
        body, name=name, grid=(nb,),
        in_specs=[pl.BlockSpec((tm, LANES), lambda i: (nb - 1 - i, 0)),
                  pl.BlockSpec((tm, LANES), lambda i: (nb - 1 - i, cs)), vec],
        out_specs=[pl.BlockSpec((tm, LANES), lambda i: (nb - 1 - i, 0)), vec],
        out_shape=[jax.ShapeDtypeStruct((t, LANES), F32), jax.ShapeDtypeStruct((1, LANES), F32)],
        scratch_shapes=[pltpu.VMEM((SUBLANES, LANES), F32)],
        compiler_params=_params(1),
    )(dfc, hbuf, bf_vec)


def _fox_masks(i, j, tq):
    row = i * tq + _iota((tq, tq), 0)
    col = j * tq + _iota((tq, tq), 1)
    lane = _iota((1, LANES), 1)
    return col <= row, (lane < HEAD_DIM, lane >= HEAD_DIM)


def _hosting(body, n_in, n_out, n_scratch, comm, grid):
    na, no = len(comm.arrays), len(comm.out_shapes)

    def hosted(*refs):
        o0 = n_in + na
        s0 = o0 + n_out + no
        cargs = (refs[n_in:o0], refs[o0 + n_out:s0]) + tuple(refs[s0 + n_scratch:])
        a, b = pl.program_id(0), pl.program_id(1)

        @pl.when((a == 0) & (b == 0))
        def _():
            comm.start(*cargs)

        body(*refs[:n_in], *refs[o0:o0 + n_out], *refs[s0:s0 + n_scratch])

        @pl.when((a == grid[0] - 1) & (b == grid[1] - 1))
        def _():
            comm.finish(*cargs)

    return hosted


def _hosted_call(body, comm, grid, *, name, in_specs, out_specs, out_shape, scratch_shapes, args):
    n_out = len(out_shape)
    if comm is not None:
        cin, cout, sems = comm.specs()
        body = _hosting(body, len(in_specs), n_out, len(scratch_shapes), comm, grid)
        in_specs, out_specs = in_specs + cin, out_specs + cout
        out_shape, scratch_shapes, args = out_shape + comm.out_shapes, scratch_shapes + sems, args + list(comm.arrays)
    outs = pl.pallas_call(body, name=name, grid=grid, in_specs=in_specs, out_specs=out_specs,
                          out_shape=out_shape, scratch_shapes=scratch_shapes, compiler_params=_params(2))(*args)
    return outs[:n_out], outs[n_out:]


def _merge_comms(comms):
    comms = [c for c in comms if c is not None]
    if len(comms) <= 1:
        return comms[0] if comms else None

    def both(which):
        def run(ins, outs, ssem, rsem):
            ia = io = 0
            for c in comms:
                na, no = len(c.arrays), len(c.out_shapes)
                getattr(c, which)(ins[ia:ia + na], outs[io:io + no], ssem, rsem)
                ia, io = ia + na, io + no
        return run

    spans = sorted((c.base, c.base + c.n_own) for c in comms)
    assert all(a[1] <= b[0] for a, b in zip(spans, spans[1:])), "semaphore ranges overlap"
    return _Comm(sum((list(c.arrays) for c in comms), []), sum((list(c.out_shapes) for c in comms), []),
                 spans[-1][1], both("start"), both("finish"))


def _fox_fwd(hbuf, eq, ek, *, comm=None, name):
    t = hbuf.shape[0]
    w = ATT_WIDTH
    tq = _pick(t, (FOX_T, 256, 128))
    nq = t // tq
    cq, ck, cv = COL_Q // w, COL_K // w, COL_V // w

    def body(q_ref, k_ref, v_ref, eq_ref, ek_ref, o_ref, lse_ref, m_s, l_s, acc_s):
        i = pl.program_id(0)
        j = pl.program_id(1)

        @pl.when(j == 0)
        def _():
            m_s[...] = jnp.full_like(m_s, NEG)
            l_s[...] = jnp.zeros_like(l_s)
            acc_s[...] = jnp.zeros_like(acc_s)

        def step(diagonal):
            _, hms = _fox_masks(i, j, tq)
            keys_first = (j * tq + _iota((tq, tq), 0)) <= (i * tq + _iota((tq, tq), 1))
            half = _iota((LANES, 1), 0)
            hrows = (half < HEAD_DIM, half >= HEAD_DIM)
            m_all = m_s[...]
            l_all = l_s[...]
            acc_old = [acc_s[LANES * pr:LANES * (pr + 1), :] for pr in range(2)]
            m_out, l_out, acc_out = [], [], []
            for pr in range(2):
                sl = slice(LANES * pr, LANES * (pr + 1))
                qp = q_ref[:, sl]
                kp = k_ref[:, sl]
                vt = v_ref[:, sl].T.astype(BF16)
                acc = acc_old[pr]
                for hh in range(2):
                    h = 2 * pr + hh
                    hsl = slice(LANES * h, LANES * (h + 1))
                    qm = jnp.where(hms[hh], (qp * (HEAD_DIM ** -0.5)).astype(BF16), eq_ref[:, hsl])
                    km = jnp.where(hms[hh], kp.astype(BF16), ek_ref[:, hsl])
                    st = _dot(km, qm, 1, 1)
                    if diagonal:
                        st = jnp.where(keys_first, st, NEG)
                    m_old = m_all[h:h + 1, :]
                    m_new = jnp.maximum(m_old, jnp.max(st, axis=0, keepdims=True))
                    alpha = jnp.exp(m_old - m_new)
                    pt = jnp.exp(st - m_new)
                    l_out.append(alpha * l_all[h:h + 1, :] + jnp.sum(pt, axis=0, keepdims=True))
                    m_out.append(m_new)
                    pv = _dot(vt, pt.astype(BF16), 1, 0)
                    acc = jnp.where(hrows[hh], alpha * acc_old[pr] + pv, acc)
                acc_out.append(acc)
            for h in range(ATT_HEADS):
                m_s[h:h + 1, :] = m_out[h]
                l_s[h:h + 1, :] = l_out[h]
            for pr in range(2):
                acc_s[LANES * pr:LANES * (pr + 1), :] = acc_out[pr]

        @pl.when(j < i)
        def _():
            step(False)

        @pl.when(j == i)
        def _():
            step(True)
            half = _iota((LANES, 1), 0)
            l_all = l_s[...]
            for pr in range(2):
                acc = acc_s[LANES * pr:LANES * (pr + 1), :]
                o_t = jnp.where(half < HEAD_DIM, acc / l_all[2 * pr:2 * pr + 1, :], acc / l_all[2 * pr + 1:2 * pr + 2, :])
                o_ref[:, LANES * pr:LANES * (pr + 1)] = o_t.T
            lse = m_s[...] + jnp.log(l_s[...])
            lse_ref[...] = jnp.where(_iota(lse.shape, 0) < ATT_HEADS, lse, 0.0)

    return _hosted_call(
        body, comm, (nq, nq), name=name,
        in_specs=[pl.BlockSpec((tq, w), lambda i, j: (i, cq)),
                  pl.BlockSpec((tq, w), lambda i, j: (jnp.minimum(j, i), ck)),
                  pl.BlockSpec((tq, w), lambda i, j: (jnp.minimum(j, i), cv)),
                  pl.BlockSpec((tq, ATT_HEADS * LANES), lambda i, j: (i, 0)),
                  pl.BlockSpec((tq, ATT_HEADS * LANES), lambda i, j: (jnp.minimum(j, i), 0))],
        out_specs=[pl.BlockSpec((tq, w), lambda i, j: (i, 0)),
                   pl.BlockSpec((SUBLANES, tq), lambda i, j: (0, i))],
        out_shape=[jax.ShapeDtypeStruct((t, w), F32), jax.ShapeDtypeStruct((SUBLANES, t), F32)],
        scratch_shapes=[pltpu.VMEM((SUBLANES, tq), F32), pltpu.VMEM((SUBLANES, tq), F32),
                        pltpu.VMEM((w, tq), F32)],
        args=[hbuf, hbuf, hbuf, eq, ek])


def _fox_delta(dymix, o, *, name):
    t, w = o.shape
    tm = _pick(t, (512, 256, 128))
    cdo = ATT_WIDTH // w

    def body(do_ref, o_ref, d_ref):
        d_ref[...] = _head_reduce(do_ref[...] * o_ref[...], 0, ATT_HEADS)

    return pl.pallas_call(
        body, name=name, grid=(t // tm,),
        in_specs=[pl.BlockSpec((tm, w), lambda i: (i, cdo)), pl.BlockSpec((tm, w), lambda i: (i, 0))],
        out_specs=pl.BlockSpec((tm, LANES), lambda i: (i, 0)),
        out_shape=jax.ShapeDtypeStruct((t, LANES), F32),
        compiler_params=_params(1),
    )(dymix, o)


def _fox_bwd(hbuf, eq, ek, dymix, lse_rows, delta_rows, *, comm=None, name):
    t = hbuf.shape[0]
    w = ATT_WIDTH
    tq = _pick(t, (FOX_T, 256, 128))
    nq = t // tq
    cq, ck, cv = COL_Q // w, COL_K // w, COL_V // w
    cdo = ATT_WIDTH // w

    def body(q_ref, k_ref, v_ref, eq_ref, ek_ref, do_ref, lse_ref, dl_ref, dk_ref, dv_ref, dfk_ref, dqt_ref, dfq_ref,
             dk_s, dv_s, dfk_s):
        j = pl.program_id(0)
        i = pl.program_id(1)

        @pl.when((i == 0) & (j == 0))
        def _():
            dqt_ref[...] = jnp.zeros_like(dqt_ref)
            dfq_ref[...] = jnp.zeros_like(dfq_ref)

        @pl.when(i == 0)
        def _():
            dk_s[...] = jnp.zeros_like(dk_s)
            dv_s[...] = jnp.zeros_like(dv_s)
            dfk_s[...] = jnp.zeros_like(dfk_s)

        def step(diagonal):
            _, hms = _fox_masks(i, j, tq)
            keys_first = (j * tq + _iota((tq, tq), 0)) <= (i * tq + _iota((tq, tq), 1))
            half = _iota((LANES, 1), 0)
            hrows = (half < HEAD_DIM, half >= HEAD_DIM)
            lse_all = lse_ref[...]
            dl_all = dl_ref[...]
            dvs, dks, dfks, dqts, dfqs = [], [], [], [], []
            for pr in range(2):
                sl = slice(LANES * pr, LANES * (pr + 1))
                qp = q_ref[:, sl]
                kp = k_ref[:, sl]
                kt = kp.T.astype(BF16)
                vpb = v_ref[:, sl].astype(BF16)
                dop = do_ref[:, sl]
                dv_p = jnp.zeros((tq, LANES), F32)
                dk_p = jnp.zeros((tq, LANES), F32)
                dqt_p = jnp.zeros((LANES, tq), F32)
                for hh in range(2):
                    h = 2 * pr + hh
                    hsl = slice(LANES * h, LANES * (h + 1))
                    qm = jnp.where(hms[hh], (qp * (HEAD_DIM ** -0.5)).astype(BF16), eq_ref[:, hsl])
                    km = jnp.where(hms[hh], kp.astype(BF16), ek_ref[:, hsl])
                    st = _dot(km, qm, 1, 1)
                    if diagonal:
                        st = jnp.where(keys_first, st, NEG)
                    pt = jnp.exp(st - lse_all[h:h + 1, :])
                    domb = jnp.where(hms[hh], dop, 0.0).astype(BF16)
                    dv_p = dv_p + _dot(pt.astype(BF16), domb, 1, 0)
                    dpt = _dot(vpb, domb, 1, 1)
                    dst = pt * (dpt - dl_all[h:h + 1, :])
                    dstb = dst.astype(BF16)
                    dk_p = dk_p + jnp.where(hms[hh], _dot(dstb, qm, 1, 0), 0.0)
                    dqt_p = dqt_p + _dot(jnp.where(hrows[hh], kt, 0.0), dstb, 1, 0)
                    part = dst[:, 0:LANES]
                    for c in range(1, tq // LANES):
                        part = part + dst[:, LANES * c:LANES * (c + 1)]
                    dfks.append(part)
                    dfqs.append(jnp.sum(dst, axis=0, keepdims=True))
                dvs.append(dv_p)
                dks.append(dk_p)
                dqts.append(dqt_p)
            dv_s[...] += jnp.concatenate(dvs, axis=1)
            dk_s[...] += jnp.concatenate(dks, axis=1)
            for h in range(ATT_HEADS):
                dfk_s[h] += dfks[h]
            cols = pl.ds(pl.multiple_of(i * tq, tq), tq)
            dqt_ref[:, cols] += jnp.concatenate(dqts, axis=0) * (HEAD_DIM ** -0.5)
            dfq_ref[:, cols] += jnp.concatenate(dfqs + [jnp.zeros((SUBLANES - ATT_HEADS, tq), F32)], axis=0)

        @pl.when(i > j)
        def _():
            step(False)

        @pl.when(i == j)
        def _():
            step(True)

        @pl.when(i == nq - 1)
        def _():
            dk_ref[...] = dk_s[...].astype(BF16)
            dv_ref[...] = dv_s[...].astype(BF16)
            lane = _iota((tq, LANES), 1)
            out = jnp.zeros((tq, LANES), F32)
            for h in range(ATT_HEADS):
                out = jnp.where(lane == h, jnp.sum(dfk_s[h], axis=1, keepdims=True), out)
            dfk_ref[...] = out

    qi = lambda j, i: jnp.maximum(i, j)
    rows = pl.BlockSpec((SUBLANES, tq), lambda j, i: (0, qi(j, i)))
    return _hosted_call(
        body, comm, (nq, nq), name=name,
        in_specs=[pl.BlockSpec((tq, w), lambda j, i: (qi(j, i), cq)),
                  pl.BlockSpec((tq, w), lambda j, i: (j, ck)),
                  pl.BlockSpec((tq, w), lambda j, i: (j, cv)),
                  pl.BlockSpec((tq, ATT_HEADS * LANES), lambda j, i: (qi(j, i), 0)),
                  pl.BlockSpec((tq, ATT_HEADS * LANES), lambda j, i: (j, 0)),
                  pl.BlockSpec((tq, w), lambda j, i: (qi(j, i), cdo)),
                  rows, rows],
        out_specs=[pl.BlockSpec((tq, w), lambda j, i: (j, 0)), pl.BlockSpec((tq, w), lambda j, i: (j, 0)),
                   pl.BlockSpec((tq, LANES), lambda j, i: (j, 0)),
                   pl.BlockSpec((w, t), lambda j, i: (0, 0)), pl.BlockSpec((SUBLANES, t), lambda j, i: (0, 0))],
        out_shape=[jax.ShapeDtypeStruct((t, w), BF16), jax.ShapeDtypeStruct((t, w), BF16),
                   jax.ShapeDtypeStruct((t, LANES), F32),
                   jax.ShapeDtypeStruct((w, t), F32), jax.ShapeDtypeStruct((SUBLANES, t), F32)],
        scratch_shapes=[pltpu.VMEM((tq, w), F32), pltpu.VMEM((tq, w), F32),
                        pltpu.VMEM((ATT_HEADS, tq, LANES), F32)],
        args=[hbuf, hbuf, hbuf, eq, ek, dymix, lse_rows, delta_rows])


GROUP_W = SSD_WIDTH // SSD_GROUPS
HEADS_PER_GROUP = SSD_HEADS // SSD_GROUPS


def _ssd_chunk_common(xr, prev8, sm, cw, cb, dtb, avec):
    c = _conv_taps(xr, prev8, cw, cb)
    sig = _sigmoid(c)
    xa = c * sig
    dt = _softplus(sm + dtb)
    a = dt * avec
    acum = _cumsum_rows(a)
    return c, sig, xa, dt, acum


def _ssd_head_cols(acum, acum_t):
    cols = [_col(acum, LANE_DT + h) for h in range(SSD_HEADS)]
    rows = [_row(acum_t, LANE_DT + h) for h in range(SSD_HEADS)]
    return cols, rows


def _expand_heads(vals, width):
    rows = vals[0].shape[0]
    colhead = _iota((rows, width), 1) // HEAD_DIM
    out = jnp.broadcast_to(vals[0], (rows, width))
    for h in range(1, len(vals)):
        out = jnp.where(colhead == h, vals[h], out)
    return out


def _ssd_decays(cols, g):
    mine = cols[HEADS_PER_GROUP * g:HEADS_PER_GROUP * (g + 1)]
    n = mine[0].shape[0]
    atots = [c[n - 1:n, :] for c in mine]
    e = _expand_heads([jnp.exp(c) for c in mine], GROUP_W)
    dec = _expand_heads([jnp.exp(t - c) for c, t in zip(mine, atots)], GROUP_W)
    etot = _expand_heads([jnp.exp(t) for t in atots], GROUP_W)
    return e, dec, etot


def _ssd_ldec(cols, rows, h, tril):
    return jnp.exp(jnp.where(tril, cols[h] - rows[h], NEG))


def _ssd_fwd(hbuf, conv_w, conv_b, dtb_vec, a_vec, d_exp, norm_g, *, name):
    t = hbuf.shape[0]
    L = SSD_CHUNK
    nc = t // L
    hb = L // SUBLANES
    cs = COL_SMALL // LANES
    cz = COL_Z // SSD_WIDTH

    def body(x_ref, xp_ref, z_ref, s_ref, cw_ref, cb_ref, dtb_ref, av_ref, dx_ref, ng_ref,
             yc_ref, y_ref, st_ref, state):
        i = pl.program_id(0)

        @pl.when(i == 0)
        def _():
            state[...] = jnp.zeros_like(state)

        prev = jnp.where(i == 0, 0.0, xp_ref[...])
        _, _, xa, dt, acum = _ssd_chunk_common(x_ref[...], prev, s_ref[...], cw_ref[...], cb_ref[...],
                                               dtb_ref[...], av_ref[...])
        cols, rows = _ssd_head_cols(acum, acum.T)
        xs = xa[:, :SSD_WIDTH]
        xdt = xs * _head_expand(dt, LANE_DT, SSD_HEADS, SSD_WIDTH)
        tril = _iota((L, L), 0) >= _iota((L, L), 1)
        lane = _iota((1, LANES), 1)
        ys = []
        for g in range(SSD_GROUPS):
            bg = xa[:, SSD_WIDTH + SSD_STATE * g:SSD_WIDTH + SSD_STATE * (g + 1)].astype(BF16)
            cg = xa[:, SSD_WIDTH + SSD_STATE * (SSD_GROUPS + g):SSD_WIDTH + SSD_STATE * (SSD_GROUPS + g + 1)].astype(BF16)
            gm = _dot(cg, bg, 1, 1)
            e, dec, etot = _ssd_decays(cols, g)
            s_in = state[g]
            st_ref[0, g] = s_in
            xg = xdt[:, GROUP_W * g:GROUP_W * (g + 1)]
            y_off = e * _dot(cg, s_in.astype(BF16), 1, 0)
            state[g] = etot * s_in + _dot(bg, (dec * xg).astype(BF16), 0, 0)
            for pr in range(2):
                xp = xg[:, LANES * pr:LANES * (pr + 1)].astype(BF16)
                outs = []
                for hh in range(2):
                    h = HEADS_PER_GROUP * g + 2 * pr + hh
                    m = gm * _ssd_ldec(cols, rows, h, tril)
                    outs.append(_dot(m.astype(BF16), xp, 1, 0))
                ys.append(jnp.where(lane < HEAD_DIM, outs[0], outs[1]) + y_off[:, LANES * pr:LANES * (pr + 1)])
        y = jnp.concatenate(ys, axis=1)
        y_ref[...] = y
        yd = y + dx_ref[...] * xs
        zz = z_ref[...]
        y2 = yd * zz * _sigmoid(zz)
        ng = ng_ref[...]
        outs = []
        for g in range(SSD_GROUPS):
            yg = y2[:, GROUP_W * g:GROUP_W * (g + 1)]
            rs = lax.rsqrt(jnp.mean(yg * yg, axis=1, keepdims=True) + RMS_EPS)
            outs.append(yg * rs * ng[:, GROUP_W * g:GROUP_W * (g + 1)])
        yc_ref[...] = jnp.concatenate(outs, axis=1)

    cdim = SSD_CONV_DIM
    vecc = pl.BlockSpec((1, cdim), lambda i: (0, 0))
    vecl = pl.BlockSpec((1, LANES), lambda i: (0, 0))
    vecw = pl.BlockSpec((1, SSD_WIDTH), lambda i: (0, 0))
    roww = pl.BlockSpec((L, SSD_WIDTH), lambda i: (i, 0))
    return pl.pallas_call(
        body, name=name, grid=(nc,),
        in_specs=[pl.BlockSpec((L, cdim), lambda i: (i, 0)),
                  pl.BlockSpec((SUBLANES, cdim), lambda i: (jnp.maximum(i * hb - 1, 0), 0)),
                  pl.BlockSpec((L, SSD_WIDTH), lambda i: (i, cz)),
                  pl.BlockSpec((L, LANES), lambda i: (i, cs)),
                  pl.BlockSpec((CONV_K, cdim), lambda i: (0, 0)), vecc, vecl, vecl, vecw, vecw],
        out_specs=[roww, roww, pl.BlockSpec((1, SSD_GROUPS, SSD_STATE, GROUP_W), lambda i: (i, 0, 0, 0))],
        out_shape=[jax.ShapeDtypeStruct((t, SSD_WIDTH), F32), jax.ShapeDtypeStruct((t, SSD_WIDTH), F32),
                   jax.ShapeDtypeStruct((nc, SSD_GROUPS, SSD_STATE, GROUP_W), F32)],
        scratch_shapes=[pltpu.VMEM((SSD_GROUPS, SSD_STATE, GROUP_W), F32)],
        compiler_params=_params(1),
    )(hbuf, hbuf, hbuf, hbuf, conv_w, conv_b, dtb_vec, a_vec, d_exp, norm_g)


def _ssd_bwd(dymix, hbuf, y_ssd, states, conv_w, conv_b, dtb_vec, a_vec, d_exp, norm_g, *, name):
    t = hbuf.shape[0]
    L = SSD_CHUNK
    nc = t // L
    hb = L // SUBLANES
    cs = COL_SMALL // LANES
    cz = COL_Z // SSD_WIDTH
    cdy = (LRU_WIDTH + ATT_WIDTH) // SSD_WIDTH
    cdim = SSD_CONV_DIM

    def body(dyc_ref, x_ref, xp_ref, z_ref, s_ref, y_ref, st_ref, cw_ref, cb_ref, dtb_ref, av_ref, dx_ref, ng_ref,
             dxr_ref, dz_ref, dsm_ref, dng_ref, dd_ref, da_ref, ddtb_ref, dcw_ref, dcb_ref,
             dstate, dnext):
        i = pl.program_id(0)
        ic = nc - 1 - i

        @pl.when(i == 0)
        def _():
            dstate[...] = jnp.zeros_like(dstate)
            dnext[...] = jnp.zeros_like(dnext)
            for ref in (dng_ref, dd_ref, da_ref, ddtb_ref, dcw_ref, dcb_ref):
                ref[...] = jnp.zeros_like(ref)

        xr = x_ref[...]
        sm = s_ref[...]
        prev = jnp.where(ic == 0, 0.0, xp_ref[...])
        avec = av_ref[...]
        c, sig, xa, dt, acum = _ssd_chunk_common(xr, prev, sm, cw_ref[...], cb_ref[...], dtb_ref[...], avec)
        cols, rows = _ssd_head_cols(acum, acum.T)
        xs = xa[:, :SSD_WIDTH]
        dtx = _head_expand(dt, LANE_DT, SSD_HEADS, SSD_WIDTH)
        xdt = xs * dtx
        tril = _iota((L, L), 0) >= _iota((L, L), 1)
        lane = _iota((1, LANES), 1)
        hmasks = (lane < HEAD_DIM, lane >= HEAD_DIM)

        y = y_ref[...]
        dexp = dx_ref[...]
        yd = y + dexp * xs
        zz = z_ref[...]
        sz = _sigmoid(zz)
        siluz = zz * sz
        y2 = yd * siluz
        ng = ng_ref[...]
        dyc = dyc_ref[...]
        dy2s, dngs = [], []
        for g in range(SSD_GROUPS):
            sl = slice(GROUP_W * g, GROUP_W * (g + 1))
            yg = y2[:, sl]
            rs = lax.rsqrt(jnp.mean(yg * yg, axis=1, keepdims=True) + RMS_EPS)
            wv = dyc[:, sl] * ng[:, sl]
            dngs.append(jnp.sum(dyc[:, sl] * yg * rs, axis=0, keepdims=True))
            dy2s.append(rs * wv - yg * (rs * rs * rs) * jnp.mean(wv * yg, axis=1, keepdims=True))
        dy2 = jnp.concatenate(dy2s, axis=1)
        dng_ref[...] += jnp.concatenate(dngs, axis=1)
        dz_ref[...] = (dy2 * yd * (sz * (1.0 + zz * (1.0 - sz)))).astype(BF16)
        dy = dy2 * siluz
        dd_ref[...] += jnp.sum(dy * xs, axis=0, keepdims=True)

        dxs, dbs, dcs = [], [], []
        datot = jnp.zeros((1, LANES), F32)
        lanes = _iota((L, LANES), 1)
        dacum = jnp.zeros((L, LANES), F32)
        for g in range(SSD_GROUPS):
            sl = slice(GROUP_W * g, GROUP_W * (g + 1))
            bg = xa[:, SSD_WIDTH + SSD_STATE * g:SSD_WIDTH + SSD_STATE * (g + 1)].astype(BF16)
            cg = xa[:, SSD_WIDTH + SSD_STATE * (SSD_GROUPS + g):SSD_WIDTH + SSD_STATE * (SSD_GROUPS + g + 1)].astype(BF16)
            gm = _dot(cg, bg, 1, 1)
            e, dec, etot = _ssd_decays(cols, g)
            s_in = st_ref[0, g]
            ds_out = dstate[g]
            dyg = dy[:, sl]
            xg = xdt[:, sl]
            edy = (e * dyg).astype(BF16)
            dstate[g] = etot * ds_out + _dot(cg, edy, 0, 0)
            dx_state = dec * _dot(bg, ds_out.astype(BF16), 1, 0)
            y_off = e * _dot(cg, s_in.astype(BF16), 1, 0)
            dacum = dacum + _head_reduce_group(dyg * y_off - xg * dx_state, g)
            dc_off = _dot(edy, s_in.astype(BF16), 1, 1)
            db_state = _dot((dec * xg).astype(BF16), ds_out.astype(BF16), 1, 1)
            dgsum = jnp.zeros((L, L), F32)
            dx_pairs = []
            for pr in range(2):
                psl = slice(LANES * pr, LANES * (pr + 1))
                xp = xg[:, psl]
                dyp = dyg[:, psl]
                dx_pair = jnp.zeros((L, LANES), F32)
                for hh in range(2):
                    h = HEADS_PER_GROUP * g + 2 * pr + hh
                    ldec = _ssd_ldec(cols, rows, h, tril)
                    dym = jnp.where(hmasks[hh], dyp, 0.0).astype(BF16)
                    xm = jnp.where(hmasks[hh], xp, 0.0).astype(BF16)
                    dx_pair = dx_pair + _dot((gm * ldec).astype(BF16), dym, 0, 0)
                    dml = _dot(dym, xm, 1, 1) * ldec
                    dgsum = dgsum + dml
                    qm = dml * gm
                    seg = jnp.sum(qm, axis=1, keepdims=True) - jnp.sum(qm.T, axis=1, keepdims=True)
                    dacum = dacum + jnp.where(lanes == LANE_DT + h, seg, 0.0)
                dx_pairs.append(dx_pair)
            dgb = dgsum.astype(BF16)
            dcs.append(_dot(dgb, bg, 1, 0) + dc_off)
            dbs.append(_dot(dgb, cg, 0, 0) + db_state)
            dxg = jnp.concatenate(dx_pairs, axis=1) + dx_state
            dxs.append(dxg)
            v = jnp.sum(dx_state * xg, axis=0, keepdims=True) + etot * jnp.sum(ds_out * s_in, axis=0, keepdims=True)
            datot = datot + _head_reduce_row(v, LANE_DT + HEADS_PER_GROUP * g, HEADS_PER_GROUP)
        dx = jnp.concatenate(dxs, axis=1)
        dacum = dacum + jnp.where(_iota((L, LANES), 0) == L - 1, datot, 0.0)
        da = _cumsum_rows(dacum, reverse=True)
        ddt = da * avec + _head_reduce(dx * xs, LANE_DT, SSD_HEADS)
        da_ref[...] += jnp.sum(da * dt, axis=0, keepdims=True)
        ddt_raw = ddt * _sigmoid(sm + dtb_ref[...])
        ddt_raw = jnp.where((lanes >= LANE_DT) & (lanes < LANE_DT + SSD_HEADS), ddt_raw, 0.0)
        dsm_ref[...] = ddt_raw
        ddtb_ref[...] += jnp.sum(ddt_raw, axis=0, keepdims=True)
        dxs_total = dx * dtx + dexp * dy
        dxa = jnp.concatenate([dxs_total] + dbs + dcs, axis=1)
        dc = dxa * (sig * (1.0 + c * (1.0 - sig)))
        dxr, dws = _conv_taps_bwd(dc, dnext[...], cw_ref[...], xr)
        dxr_ref[...] = dxr.astype(BF16)
        dcw_ref[...] += dws
        dcb_ref[...] += jnp.sum(dc, axis=0, keepdims=True)
        dnext[...] = dc[:SUBLANES]

    rev = lambda i: nc - 1 - i
    vecc = pl.BlockSpec((1, cdim), lambda i: (0, 0))
    vecl = pl.BlockSpec((1, LANES), lambda i: (0, 0))
    vecw = pl.BlockSpec((1, SSD_WIDTH), lambda i: (0, 0))
    cwspec = pl.BlockSpec((CONV_K, cdim), lambda i: (0, 0))
    roww = pl.BlockSpec((L, SSD_WIDTH), lambda i: (rev(i), 0))
    return pl.pallas_call(
        body, name=name, grid=(nc,),
        in_specs=[pl.BlockSpec((L, SSD_WIDTH), lambda i: (rev(i), cdy)),
                  pl.BlockSpec((L, cdim), lambda i: (rev(i), 0)),
                  pl.BlockSpec((SUBLANES, cdim), lambda i: (jnp.maximum(rev(i) * hb - 1, 0), 0)),
                  pl.BlockSpec((L, SSD_WIDTH), lambda i: (rev(i), cz)),
                  pl.BlockSpec((L, LANES), lambda i: (rev(i), cs)),
                  roww,
                  pl.BlockSpec((1, SSD_GROUPS, SSD_STATE, GROUP_W), lambda i: (rev(i), 0, 0, 0)),
                  cwspec, vecc, vecl, vecl, vecw, vecw],
        out_specs=[pl.BlockSpec((L, cdim), lambda i: (rev(i), 0)), roww,
                   pl.BlockSpec((L, LANES), lambda i: (rev(i), 0)),
                   vecw, vecw, vecl, vecl, cwspec, vecc],
        out_shape=[jax.ShapeDtypeStruct((t, cdim), BF16), jax.ShapeDtypeStruct((t, SSD_WIDTH), BF16),
                   jax.ShapeDtypeStruct((t, LANES), F32),
                   jax.ShapeDtypeStruct((1, SSD_WIDTH), F32), jax.ShapeDtypeStruct((1, SSD_WIDTH), F32),
                   jax.ShapeDtypeStruct((1, LANES), F32), jax.ShapeDtypeStruct((1, LANES), F32),
                   jax.ShapeDtypeStruct((CONV_K, cdim), F32), jax.ShapeDtypeStruct((1, cdim), F32)],
        scratch_shapes=[pltpu.VMEM((SSD_GROUPS, SSD_STATE, GROUP_W), F32), pltpu.VMEM((SUBLANES, cdim), F32)],
        compiler_params=_params(1),
    )(dymix, hbuf, hbuf, hbuf, hbuf, y_ssd, states, conv_w, conv_b, dtb_vec, a_vec, d_exp, norm_g)


def _head_reduce_group(x, g):
    return _head_reduce(x, LANE_DT + HEADS_PER_GROUP * g, HEADS_PER_GROUP)


def _head_reduce_row(v, lane0, nheads):
    colhead = _iota(v.shape, 1) // HEAD_DIM
    lane = _iota((1, LANES), 1)
    out = jnp.zeros((1, LANES), F32)
    for h in range(nheads):
        s = jnp.sum(jnp.where(colhead == h, v, 0.0), axis=1, keepdims=True)
        out = jnp.where(lane == lane0 + h, s, out)
    return out


def _exchange(inps, axes, *, swap=False, name):
    n = 2 ** len(axes)
    assert not swap or n == 2
    counts = [a.shape[0] for a in inps]
    out_shapes = [jax.ShapeDtypeStruct(a.shape if swap else (n,) + a.shape, a.dtype) for a in inps]
    units = sum(counts)
    na = len(inps)

    def body(*refs):
        in_refs, out_refs = refs[:na], refs[na:2 * na]
        send_sems, recv_sems, local_sems = refs[2 * na:]
        pos = {ax: lax.axis_index(ax) for ax in MESH_AXES}

        def slot_of(coord):
            s = 0
            for ax in axes:
                s = s * 2 + coord[ax]
            return s

        me = slot_of(pos)
        copies = []
        unit = 0
        for a in range(na):
            for it in range(counts[a]):
                dst = out_refs[a].at[it] if swap else out_refs[a].at[me, it]
                if not swap:
                    cp = pltpu.make_async_copy(in_refs[a].at[it], dst, local_sems.at[unit])
                    cp.start()
                    copies.append(cp)
                for delta in range(1, n):
                    coord = dict(pos)
                    for b, ax in enumerate(reversed(axes)):
                        if (delta >> b) & 1:
                            coord[ax] = 1 - pos[ax]
                    k = unit * (n - 1) + delta - 1
                    cp = pltpu.make_async_remote_copy(
                        src_ref=in_refs[a].at[it], dst_ref=dst,
                        send_sem=send_sems.at[k], recv_sem=recv_sems.at[k],
                        device_id=(coord["x"], coord["y"], coord["c"]), device_id_type=pl.DeviceIdType.MESH)
                    cp.start()
                    copies.append(cp)
                unit += 1
        for cp in copies:
            cp.wait()

    any_spec = pl.BlockSpec(memory_space=pl.ANY)
    return pl.pallas_call(
        body, name=name,
        in_specs=[any_spec] * na, out_specs=[any_spec] * na, out_shape=out_shapes,
        scratch_shapes=[pltpu.SemaphoreType.DMA((units * (n - 1),)), pltpu.SemaphoreType.DMA((units * (n - 1),)),
                        pltpu.SemaphoreType.DMA((units,))],
    )(*inps)


class _Comm:
    def __init__(self, arrays, out_shapes, n_own, start, finish, base=0):
        self.arrays, self.out_shapes, self.start, self.finish = arrays, out_shapes, start, finish
        self.base, self.n_own, self.n_sems = base, n_own, base + n_own

    def specs(self):
        any_spec = pl.BlockSpec(memory_space=pl.ANY)
        sems = [pltpu.SemaphoreType.DMA((self.n_sems,)), pltpu.SemaphoreType.DMA((self.n_sems,))]
        return [any_spec] * len(self.arrays), [any_spec] * len(self.out_shapes), sems


def _run_comm(comm, *, name):
    na, no = len(comm.arrays), len(comm.out_shapes)

    def body(*refs):
        args = (refs[:na], refs[na:na + no]) + tuple(refs[na + no:])
        comm.start(*args)
        comm.finish(*args)

    in_specs, out_specs, sems = comm.specs()
    return pl.pallas_call(body, name=name, in_specs=in_specs, out_specs=out_specs, out_shape=comm.out_shapes,
                          scratch_shapes=sems)(*comm.arrays)


def _chip_peer(x, y, d):
    px = 1 - x if d & 2 else x
    py = 1 - y if d & 1 else y
    return px, py, 2 * px + py


def _gather_layer_comm(srcs, li, base=0):
    counts = [s.shape[0] for s in srcs]
    units = [(a, it) for a in range(len(srcs)) for it in range(counts[a])]
    n_ici = 3 * len(units)
    out_shapes = [jax.ShapeDtypeStruct((N_CHIPS,) + s.shape, s.dtype) for s in srcs]

    def ici(ins, outs, ssem, rsem, u, d):
        x, y, c = (lax.axis_index(ax) for ax in MESH_AXES)
        a, it = units[u]
        px, py, _ = _chip_peer(x, y, d)
        k = base + 3 * u + d - 1
        return pltpu.make_async_remote_copy(
            src_ref=ins[a].at[it], dst_ref=outs[a].at[2 * x + y, it], send_sem=ssem.at[k], recv_sem=rsem.at[k],
            device_id=(px, py, c), device_id_type=pl.DeviceIdType.MESH)

    def arrived(ins, outs, ssem, rsem, u, d):
        x, y, c = (lax.axis_index(ax) for ax in MESH_AXES)
        a, it = units[u]
        _, _, pk = _chip_peer(x, y, d)
        k = base + 3 * u + d - 1
        return pltpu.make_async_remote_copy(
            src_ref=ins[a].at[it], dst_ref=outs[a].at[pk, it], send_sem=ssem.at[k], recv_sem=rsem.at[k],
            device_id=(x, y, c), device_id_type=pl.DeviceIdType.MESH)

    def forward(ins, outs, ssem, rsem, u, slot):
        x, y, c = (lax.axis_index(ax) for ax in MESH_AXES)
        a, it = units[u]
        pk = 2 * x + y if slot == 0 else _chip_peer(x, y, slot)[2]
        src = ins[a].at[it] if slot == 0 else outs[a].at[pk, it]
        k = base + n_ici + 4 * u + slot
        return pltpu.make_async_remote_copy(
            src_ref=src, dst_ref=outs[a].at[pk, it], send_sem=ssem.at[k], recv_sem=rsem.at[k],
            device_id=(x, y, 1 - c), device_id_type=pl.DeviceIdType.MESH)

    def start(ins, outs, ssem, rsem):
        for u in range(len(units)):
            forward(ins, outs, ssem, rsem, u, 0).start()

        @pl.when(lax.axis_index("c") == li)
        def _():
            for u in range(len(units)):
                for d in range(1, N_CHIPS):
                    ici(ins, outs, ssem, rsem, u, d).start()

    def finish(ins, outs, ssem, rsem):
        c = lax.axis_index("c")

        @pl.when(c == li)
        def _():
            for u in range(len(units)):
                for d in range(1, N_CHIPS):
                    arrived(ins, outs, ssem, rsem, u, d).wait_recv()
                    forward(ins, outs, ssem, rsem, u, d).start()
            for u in range(len(units)):
                for d in range(1, N_CHIPS):
                    ici(ins, outs, ssem, rsem, u, d).wait_send()
                    forward(ins, outs, ssem, rsem, u, d).wait_send()

        @pl.when(c != li)
        def _():
            for u in range(len(units)):
                for d in range(1, N_CHIPS):
                    forward(ins, outs, ssem, rsem, u, d).wait_recv()

        for u in range(len(units)):
            forward(ins, outs, ssem, rsem, u, 0).wait()

    return _Comm(srcs, out_shapes, n_ici + 4 * len(units), start, finish, base)


def _reduce_chips_comm(sums, li, base=0):
    counts = [s.shape[0] for s in sums]
    units = [(a, it) for a in range(len(sums)) for it in range(counts[a])]
    out_shapes = [jax.ShapeDtypeStruct((N_CHIPS, s.shape[0]) + s.shape[2:], s.dtype) for s in sums]

    def copy(ins, outs, ssem, rsem, u, d):
        x, y, c = (lax.axis_index(ax) for ax in MESH_AXES)
        a, it = units[u]
        px, py, pk = _chip_peer(x, y, d)
        k = base + 3 * u + d - 1
        return pltpu.make_async_remote_copy(
            src_ref=ins[a].at[it, pk], dst_ref=outs[a].at[2 * x + y, it], send_sem=ssem.at[k], recv_sem=rsem.at[k],
            device_id=(px, py, c), device_id_type=pl.DeviceIdType.MESH)

    def start(ins, outs, ssem, rsem):
        @pl.when(lax.axis_index("c") == li)
        def _():
            for u in range(len(units)):
                for d in range(1, N_CHIPS):
                    copy(ins, outs, ssem, rsem, u, d).start()

    def finish(ins, outs, ssem, rsem):
        @pl.when(lax.axis_index("c") == li)
        def _():
            for u in range(len(units)):
                for d in range(1, N_CHIPS):
                    copy(ins, outs, ssem, rsem, u, d).wait()

    return _Comm(sums, out_shapes, 3 * len(units), start, finish, base)


def _sum_slots(buf, out_dtype, *, name):
    n, rows, cols = buf.shape
    tm = _pick(rows, (512, 256, 128, 8))
    if rows % tm:
        tm = rows

    def body(b_ref, o_ref):
        acc = b_ref[0].astype(F32)
        for s in range(1, n):
            acc = acc + b_ref[s].astype(F32)
        o_ref[...] = acc.astype(out_dtype)

    return pl.pallas_call(
        body, name=name, grid=(pl.cdiv(rows, tm),),
        in_specs=[pl.BlockSpec((n, tm, cols), lambda i: (0, i, 0))],
        out_specs=pl.BlockSpec((tm, cols), lambda i: (i, 0)),
        out_shape=jax.ShapeDtypeStruct((rows, cols), out_dtype),
        compiler_params=_params(1),
    )(buf)


def _sum_pair(a, b, out_dtype, *, name):
    shape = a.shape
    cols = shape[-1]
    a2, b2 = a.reshape(-1, cols), b.reshape(-1, cols)
    rows = a2.shape[0]
    tm = _pick(rows, (512, 256, 128, 8))

    def body(a_ref, b_ref, o_ref):
        o_ref[...] = (a_ref[...].astype(F32) + b_ref[...].astype(F32)).astype(out_dtype)

    spec = pl.BlockSpec((tm, cols), lambda i: (i, 0))
    return pl.pallas_call(
        body, name=name, grid=(rows // tm,), in_specs=[spec, spec], out_specs=spec,
        out_shape=jax.ShapeDtypeStruct((rows, cols), out_dtype), compiler_params=_params(1),
    )(a2, b2).reshape(shape)


def _adamw(w, g, m, v, *, name):
    shape = w.shape
    cols = shape[-1]
    rows = w.size // cols
    w2, g2, m2, v2 = (a.reshape(rows, cols) for a in (w, g, m, v))
    tm = _pick(rows, (256, 128, 64, 32, 16, 8))
    if rows % tm:
        tm = rows
    bc1 = 1.0 - ADAM_B1 ** ADAM_STEP
    bc2 = 1.0 - ADAM_B2 ** ADAM_STEP

    def body(w_ref, g_ref, m_ref, v_ref, d_ref, nm_ref, nv_ref):
        gg = g_ref[...]
        mm = ADAM_B1 * m_ref[...] + (1.0 - ADAM_B1) * gg
        vv = ADAM_B2 * v_ref[...] + (1.0 - ADAM_B2) * (gg * gg)
        m_hat = mm / bc1
        v_hat = vv / bc2
        d_ref[...] = -ADAM_LR * (m_hat / (jnp.sqrt(v_hat) + ADAM_EPS) + ADAM_WD * w_ref[...])
        nm_ref[...] = mm
        nv_ref[...] = vv

    spec = pl.BlockSpec((tm, cols), lambda i: (i, 0))
    o = jax.ShapeDtypeStruct((rows, cols), F32)
    outs = pl.pallas_call(
        body, name=name, grid=(rows // tm,), in_specs=[spec] * 4, out_specs=[spec] * 3, out_shape=[o] * 3,
        compiler_params=_params(1),
    )(w2, g2, m2, v2)
    return tuple(a.reshape(shape) for a in outs)


def _layer_fwd(li, x, xb, pb, W, up=None, att=None):
    nm = lambda s: f"l{li}_{s}"
    sv = {"x_in_b": xb}
    (g1, u1, a1), got = _mm_swiglu(xb, W["ffn1_wg"], W["ffn1_wu"], comm=up[0] if up else None, name=nm("ffn1_up"))
    if up:
        W = {**W, **up[1](got)}
    x1, x1b, xh1, rs1 = _mm_ln(a1, W["ffn1_wd"], x, W["ln1_g"], W["ln1_b"], rscale=ALPHA, mscale=0.5, name=nm("ffn1_down_ln"))
    hbuf = _mm(x1b, W["w_in_p"], name=nm("in_proj"))
    ya, lu, lr, lig, la, lh = _lru_fwd(hbuf, W["lru_conv_w"], W["lru_conv_b"], W["lru_wa_bd"], W["lru_ba"],
                                       W["lru_wx_bd"], W["lru_bx"], W["lru_lambda"], name=nm("lru_fwd"))
    eq, ek = _fox_prep(hbuf, W["fox_bf_vec"], name=nm("fox_prep"))
    (yb, lse_rows), got = _fox_fwd(hbuf, eq, ek, comm=att[0] if att else None, name=nm("fox_fwd"))
    if att:
        W = {**W, **att[1](got)}
    yc, yssd, states = _ssd_fwd(hbuf, W["ssd_conv_w"], W["ssd_conv_b"], W["ssd_dtb_vec"], W["ssd_a_vec"],
                                W["ssd_d_exp"], W["ssd_norm_g"], name=nm("ssd_fwd"))
    ymix = jnp.concatenate([ya, yb, yc], axis=1).astype(BF16)
    x2, x2b, xh2, rs2 = _mm_ln(ymix, W["w_out"], x1, W["ln2_g"], W["ln2_b"], rscale=ALPHA, mscale=1.0, name=nm("out_proj_ln"))
    (g2, u2, a2), _ = _mm_swiglu(x2b, W["ffn2_wg"], W["ffn2_wu"], name=nm("ffn2_up"))
    x3, x3b, xh3, rs3 = _mm_ln(a2, W["ffn2_wd"], x2, W["ln3_g"], W["ln3_b"], rscale=ALPHA, mscale=0.5, name=nm("ffn2_down_ln"))
    x4, x4b, sg, e = _mm_pe(x3, x3b, pb, W["pe_gate_w"], W["pe_gate_b"], W["pe_proj"], name=nm("ple"))
    sv.update(g1=g1, u1=u1, a1=a1, x1b=x1b, xh1=xh1, rs1=rs1, hbuf=hbuf, lu=lu, lr=lr, lig=lig, la=la, lh=lh,
              eq=eq, ek=ek, lse_rows=lse_rows, yb=yb, yssd=yssd, states=states, ymix=ymix, x2b=x2b, xh2=xh2, rs2=rs2,
              g2=g2, u2=u2, a2=a2, x3b=x3b, xh3=xh3, rs3=rs3, sg=sg, e=e, pb=pb)
    return x4, x4b, sv, W


def _layer_bwd(li, dx4, sv, W, comm=None, late=None):
    nm = lambda s: f"l{li}_{s}"
    G = {}
    dgp, de, dbg = _pe_bwd_elem(dx4, sv["sg"], sv["e"], name=nm("ple_bwd"))
    G["pe_gate_b"] = dbg
    G["pe_gate_w"] = _mm(sv["x3b"], dgp, ta=True, out_dtype=BF16, name=nm("d_pe_gate_w"))
    G["pe_proj"] = _mm(sv["pb"], de, ta=True, out_dtype=BF16, chip_cols=True, name=nm("d_pe_proj"))
    dr3, dr3b, G["ln3_g"], G["ln3_b"] = _bwd_proj([(dgp, W["pe_gate_w"])], dx4, rscale=1.0,
                                                  ln=(sv["xh3"], sv["rs3"], W["ln3_g"]), name=nm("ln3_bwd"))
    G["ffn2_wd"] = _mm(sv["a2"], dr3b, ta=True, scale=0.5, out_dtype=BF16, name=nm("d_ffn2_wd"))
    dg2, du2 = _mm_swiglu_bwd(dr3b, W["ffn2_wd"], sv["g2"], sv["u2"], scale=0.5, name=nm("ffn2_act_bwd"))
    G["ffn2_wg"] = _mm(sv["x2b"], dg2, ta=True, out_dtype=BF16, chip_cols=True, name=nm("d_ffn2_wg"))
    G["ffn2_wu"] = _mm(sv["x2b"], du2, ta=True, out_dtype=BF16, chip_cols=True, name=nm("d_ffn2_wu"))
    dr2, dr2b, G["ln2_g"], G["ln2_b"] = _bwd_proj([(dg2, W["ffn2_wg"]), (du2, W["ffn2_wu"])], dr3, rscale=ALPHA,
                                                  ln=(sv["xh2"], sv["rs2"], W["ln2_g"]), name=nm("ln2_bwd"))
    G["w_out"] = _mm(sv["ymix"], dr2b, ta=True, out_dtype=BF16, name=nm("d_w_out"))
    dymix = _mm(dr2b, W["w_out"], tb=True, name=nm("d_ymix"))
    hbuf = sv["hbuf"]
    (dur, dgr, G["lru_conv_w"], G["lru_conv_b"], G["lru_wa_bd"], G["lru_ba"], G["lru_wx_bd"], G["lru_bx"],
     G["lru_lambda"]) = _lru_bwd(dymix, hbuf, sv["lu"], sv["lr"], sv["lig"], sv["la"], sv["lh"],
                                 W["lru_conv_w"], W["lru_wa_bd"], W["lru_wx_bd"], W["lru_lambda"], name=nm("lru_bwd"))
    delta = _fox_delta(dymix, sv["yb"], name=nm("fox_delta"))
    delta_rows = jnp.pad(delta[:, :ATT_HEADS].T, ((0, SUBLANES - ATT_HEADS), (0, 0)))
    comm = _merge_comms([comm, late(G) if late else None])
    (dk, dv, dfk, dqt, dfq), comm_out = _fox_bwd(hbuf, sv["eq"], sv["ek"], dymix, sv["lse_rows"], delta_rows,
                                                 comm=comm, name=nm("fox_bwd"))
    dq = dqt.T
    dfc = jnp.pad(dfq[:ATT_HEADS].T, ((0, 0), (0, LANES - ATT_HEADS))) - dfk
    dsm_f, G["fox_bf_vec"] = _fox_post(dfc, hbuf, W["fox_bf_vec"], name=nm("fox_post"))
    (dxr, dz, dsm_dt, G["ssd_norm_g"], G["ssd_d_exp"], G["ssd_a_vec"], G["ssd_dtb_vec"], G["ssd_conv_w"],
     G["ssd_conv_b"]) = _ssd_bwd(dymix, hbuf, sv["yssd"], sv["states"], W["ssd_conv_w"], W["ssd_conv_b"],
                                 W["ssd_dtb_vec"], W["ssd_a_vec"], W["ssd_d_exp"], W["ssd_norm_g"], name=nm("ssd_bwd"))
    t = dx4.shape[0]
    dh = jnp.concatenate([dxr.astype(BF16), dz.astype(BF16), dur.astype(BF16), dgr.astype(BF16), dq.astype(BF16),
                          dk.astype(BF16), dv.astype(BF16), (dsm_f + dsm_dt).astype(BF16),
                          jnp.zeros((t, H_WIDTH - COL_SMALL - LANES), BF16)], axis=1)
    G["w_in_p"] = _mm(sv["x1b"], dh, ta=True, name=nm("d_w_in"))
    dr1, dr1b, G["ln1_g"], G["ln1_b"] = _bwd_proj([(dh, W["w_in_p"])], dr2, rscale=ALPHA,
                                                  ln=(sv["xh1"], sv["rs1"], W["ln1_g"]), name=nm("ln1_bwd"))
    G["ffn1_wd"] = _mm(sv["a1"], dr1b, ta=True, scale=0.5, out_dtype=BF16, name=nm("d_ffn1_wd"))
    dg1, du1 = _mm_swiglu_bwd(dr1b, W["ffn1_wd"], sv["g1"], sv["u1"], scale=0.5, name=nm("ffn1_act_bwd"))
    G["ffn1_wg"] = _mm(sv["x_in_b"], dg1, ta=True, out_dtype=BF16, chip_cols=True, name=nm("d_ffn1_wg"))
    G["ffn1_wu"] = _mm(sv["x_in_b"], du1, ta=True, out_dtype=BF16, chip_cols=True, name=nm("d_ffn1_wu"))
    (dx_in,) = _bwd_proj([(dg1, W["ffn1_wg"]), (du1, W["ffn1_wu"])], dr1, rscale=ALPHA, ln=None, name=nm("x_in_bwd"))
    return dx_in, G, comm_out


def _block_diag(w):
    n, b, _ = w.shape
    eye = jnp.eye(n, dtype=w.dtype)
    return (eye[:, None, :, None] * w[:, :, None, :]).reshape(n * b, n * b)


def _block_diag_extract(m):
    n, b = LRU_HEADS, HEAD_DIM
    return jnp.stack([m[b * i:b * (i + 1), b * i:b * (i + 1)] for i in range(n)])


def _lane_vec(v, lane0):
    return jnp.pad(v.astype(F32), (lane0, LANES - lane0 - v.shape[0])).reshape(1, LANES)


def _w_in_permute(w):
    d = w.shape[0]
    z = lambda n: jnp.zeros((d, n), w.dtype)
    return jnp.concatenate([w[:, 1796:2820], w[:, 1284:1796], w[:, 0:512], w[:, 512:1280],
                            w[:, 1280:1284], w[:, 2820:2828], z(LANES - 12), z(H_WIDTH - COL_SMALL - LANES)], axis=1)


def _w_in_unpermute(wp):
    return jnp.concatenate([wp[:, COL_U:COL_Q], wp[:, COL_Q:COL_SMALL], wp[:, COL_SMALL:COL_SMALL + 4],
                            wp[:, COL_Z:COL_U], wp[:, COL_XBC:COL_Z], wp[:, COL_SMALL + 4:COL_SMALL + 12]], axis=1)


def _big_weights(chipw):
    W = {}
    for n, w in chipw.items():
        if n in ("ffn1_wg", "ffn1_wu", "ffn2_wg", "ffn2_wu"):
            W[n] = w
        elif n in ("ffn1_wd", "ffn2_wd", "w_out", "pe_gate_w"):
            W[n] = w.reshape(-1, D_MODEL)
        elif n == "pe_proj":
            W[n] = jnp.moveaxis(w, 0, 1).reshape(PLE_DIM, D_MODEL)
        else:
            w_in = jnp.moveaxis(w[:, :, :IN_WIDTH // N_CHIPS], 0, 1).reshape(D_MODEL, IN_WIDTH)
            W["w_in_p"] = _w_in_permute(w_in)
    return W


def _small_weights(li, small):
    g = lambda n: small[n][li]
    W = {n: g(n) for n in ("ln1_g", "ln1_b", "ln2_g", "ln2_b", "ln3_g", "ln3_b", "pe_gate_b", "lru_conv_w",
                           "ssd_conv_w")}
    for n in ("lru_conv_b", "lru_ba", "lru_bx", "lru_lambda", "ssd_conv_b", "ssd_norm_g"):
        W[n] = g(n).reshape(1, -1)
    W["lru_wa_bd"] = _block_diag(g("lru_wa")).astype(BF16)
    W["lru_wx_bd"] = _block_diag(g("lru_wx")).astype(BF16)
    W["fox_bf_vec"] = _lane_vec(g("fox_bf"), LANE_F)
    W["ssd_dtb_vec"] = _lane_vec(g("ssd_dt_bias"), LANE_DT)
    W["ssd_a_vec"] = _lane_vec(-jnp.exp(g("ssd_a_log")), LANE_DT)
    W["ssd_d_exp"] = jnp.repeat(g("ssd_d"), HEAD_DIM).reshape(1, SSD_WIDTH)
    return W


def _big_grad_by_chip(G, n):
    if n in ("ffn1_wg", "ffn1_wu", "ffn2_wg", "ffn2_wu", "pe_proj"):
        return G[n]
    if n in ("ffn1_wd", "ffn2_wd", "w_out", "pe_gate_w"):
        return G[n].reshape(N_CHIPS, -1, D_MODEL)
    share = IN_WIDTH // N_CHIPS
    d_w_in = jnp.moveaxis(_w_in_unpermute(G["w_in_p"]).reshape(D_MODEL, N_CHIPS, share), 1, 0)
    return jnp.pad(d_w_in.astype(BF16), ((0, 0), (0, 0), (0, SHARE - share)))


def _layer_small_grads(G, W):
    out = {n: G[n] for n in ("lru_conv_w", "ssd_conv_w")}
    for n in ("ln1_g", "ln1_b", "ln2_g", "ln2_b", "ln3_g", "ln3_b", "pe_gate_b", "lru_conv_b", "lru_ba", "lru_bx",
              "lru_lambda", "ssd_conv_b", "ssd_norm_g"):
        out[n] = G[n].reshape(-1)
    out["lru_wa"] = _block_diag_extract(G["lru_wa_bd"])
    out["lru_wx"] = _block_diag_extract(G["lru_wx_bd"])
    out["fox_bf"] = G["fox_bf_vec"][0, LANE_F:LANE_F + ATT_HEADS]
    out["ssd_dt_bias"] = G["ssd_dtb_vec"][0, LANE_DT:LANE_DT + SSD_HEADS]
    out["ssd_a_log"] = G["ssd_a_vec"][0, LANE_DT:LANE_DT + SSD_HEADS] * W["ssd_a_vec"][0, LANE_DT:LANE_DT + SSD_HEADS]
    out["ssd_d"] = G["ssd_d_exp"].reshape(SSD_HEADS, HEAD_DIM).sum(axis=1)
    return out


WEIGHTS = ['ln1_g', 'ln1_b', 'ffn1_wg', 'ffn1_wu', 'ffn1_wd', 'w_in', 'lru_conv_w', 'lru_conv_b', 'lru_wa', 'lru_ba',
           'lru_wx', 'lru_bx', 'lru_lambda', 'fox_bf', 'ssd_conv_w', 'ssd_conv_b', 'ssd_dt_bias', 'ssd_a_log', 'ssd_d',
           'ssd_norm_g', 'w_out', 'ln2_g', 'ln2_b', 'ffn2_wg', 'ffn2_wu', 'ffn2_wd', 'ln3_g', 'ln3_b', 'pe_proj',
           'pe_gate_w', 'pe_gate_b']
FIRST = ((("ffn1_wg", "ffn1_wu"), 1),)
NEXT = ((("w_in",), 1),
        (("ffn1_wd",), 0))
EARLY = FIRST + NEXT
LATE = ((("ffn2_wg", "ffn2_wu"), 1),
        (("ffn2_wd",), 0),
        (("w_out", "pe_gate_w"), None),
        (("pe_proj",), None))
BIG = {n: pad for names, pad in EARLY + LATE for n in names}
SMALL_SHARDED = {'lru_conv_w': 2, 'ssd_conv_w': 2}
PACK_COLS = 1024


def _unshard(seg, axis):
    moved = jnp.moveaxis(seg, 0, axis)
    shp = list(moved.shape)
    shp[axis:axis + 2] = [shp[axis] * shp[axis + 1]]
    return moved.reshape(shp)


def _pad_axis(a, axis, size):
    if axis is None or a.shape[axis] == size:
        return a
    pads = [(0, 0)] * a.ndim
    pads[axis] = (0, size - a.shape[axis])
    return jnp.pad(a, pads)


def _pack(arrs, dtype, cols):
    flat = jnp.concatenate([a.astype(dtype).reshape(-1) for a in arrs])
    pad = (-flat.shape[0]) % cols
    if pad:
        flat = jnp.concatenate([flat, jnp.zeros((pad,), dtype)])
    return flat.reshape(-1, cols)


def _unpack(flat, shapes):
    out, off = [], 0
    for s in shapes:
        n = math.prod(s)
        out.append(flat[off:off + n].reshape(s))
        off += n
    return out


def kernel(x, p, ln1_g, ln1_b, ffn1_wg, ffn1_wu, ffn1_wd, w_in, lru_conv_w, lru_conv_b, lru_wa, lru_ba, lru_wx, lru_bx, lru_lambda, fox_bf, ssd_conv_w, ssd_conv_b, ssd_dt_bias, ssd_a_log, ssd_d, ssd_norm_g, w_out, ln2_g, ln2_b, ffn2_wg, ffn2_wu, ffn2_wd, ln3_g, ln3_b, pe_proj, pe_gate_w, pe_gate_b, loss_target, m_ln1_g, m_ln1_b, m_ffn1_wg, m_ffn1_wu, m_ffn1_wd, m_w_in, m_lru_conv_w, m_lru_conv_b, m_lru_wa, m_lru_ba, m_lru_wx, m_lru_bx, m_lru_lambda, m_fox_bf, m_ssd_conv_w, m_ssd_conv_b, m_ssd_dt_bias, m_ssd_a_log, m_ssd_d, m_ssd_norm_g, m_w_out, m_ln2_g, m_ln2_b, m_ffn2_wg, m_ffn2_wu, m_ffn2_wd, m_ln3_g, m_ln3_b, m_pe_proj, m_pe_gate_w, m_pe_gate_b, v_ln1_g, v_ln1_b, v_ffn1_wg, v_ffn1_wu, v_ffn1_wd, v_w_in, v_lru_conv_w, v_lru_conv_b, v_lru_wa, v_lru_ba, v_lru_wx, v_lru_bx, v_lru_lambda, v_fox_bf, v_ssd_conv_w, v_ssd_conv_b, v_ssd_dt_bias, v_ssd_a_log, v_ssd_d, v_ssd_norm_g, v_w_out, v_ln2_g, v_ln2_b, v_ffn2_wg, v_ffn2_wu, v_ffn2_wd, v_ln3_g, v_ln3_b, v_pe_proj, v_pe_gate_w, v_pe_gate_b):
    args = locals()
    w_loc = {n: args[n] for n in WEIGHTS}
    m_loc = {n: args["m_" + n] for n in WEIGHTS}
    v_loc = {n: args["v_" + n] for n in WEIGHTS}
    chip = 2 * lax.axis_index("x") + lax.axis_index("y")
    core = lax.axis_index("c")
    big = list(BIG)
    small_sh = list(SMALL_SHARDED)
    small_rep = [n for n in WEIGHTS if n not in BIG and n not in SMALL_SHARDED]

    def srcs_of(li, groups):
        return [jnp.stack([_pad_axis(w_loc[n][li].astype(BF16), pad, SHARE) for n in names]) for names, pad in groups]

    def gather_comm(li, groups, base=0):
        return _gather_layer_comm(srcs_of(li, groups), li, base)

    def chip_weights(gathered, groups):
        return _big_weights({n: g[:, j] for (names, _), g in zip(groups, gathered) for j, n in enumerate(names)})

    def pair_sums(G, groups, tag):
        gs = [jnp.stack([_big_grad_by_chip(G, n) for n in names]) for names, _ in groups]
        flat = [g.reshape((-1,) + g.shape[2:]) for g in gs]
        theirs = _exchange(flat, ("c",), swap=True, name=f"reduce_cores_{tag}")
        return [_sum_pair(f, r, BF16, name=f"reduce_cores_sum_{tag}_{gi}").reshape(g.shape)
                for gi, (f, r, g) in enumerate(zip(flat, theirs, gs))]

    def finish_reduce(quad, sums, li, groups, tag):
        quad = [lax.dynamic_update_index_in_dim(q, lax.dynamic_index_in_dim(s, chip, 1, keepdims=False), chip, 0)
                for q, s in zip(quad, sums)]
        red = [_sum_slots(q.reshape(N_CHIPS, -1, q.shape[-1]), F32,
                          name=f"reduce_chips_sum_{tag}_{gi}").reshape(q.shape[1:]) for gi, q in enumerate(quad)]
        theirs = _exchange(red, ("c",), swap=True, name=f"reduce_share_{tag}")
        out = {}
        for (names, _), r, rv in zip(groups, red, theirs):
            both = jnp.where(core == li, r, rv)
            for j, n in enumerate(names):
                out[n] = both[j]
        return out

    everything = EARLY + LATE
    first0 = _run_comm(gather_comm(0, FIRST), name="gather_w_l0")
    small = {n: w_loc[n] for n in small_rep}
    spack = _pack([w_loc[n] for n in small_sh], F32, LANES)
    (sg,) = _exchange([spack[None]], ("x", "y"), name="gather_conv_w")
    for n, seg in zip(small_sh, _unpack_rows(sg.reshape(N_CHIPS, -1), [w_loc[n].shape for n in small_sh])):
        small[n] = _unshard(seg, SMALL_SHARDED[n])

    W0 = {**_small_weights(0, small), **chip_weights(first0, FIRST)}
    late0_comm = gather_comm(0, LATE)
    early1 = []

    def in_attention0(got):
        early1.extend(got[len(LATE):])
        return chip_weights(got[:len(LATE)], LATE)

    xs = x[0]
    xs, xb, sv0, W0 = _layer_fwd(
        0, xs, xs.astype(BF16), p[0, 0].astype(BF16), W0,
        up=(gather_comm(0, NEXT), lambda got: chip_weights(got, NEXT)),
        att=(_merge_comms([late0_comm, gather_comm(1, EARLY, base=late0_comm.n_sems)]), in_attention0))
    W1 = {**_small_weights(1, small), **chip_weights(early1, EARLY)}
    xs, _, sv1, W1 = _layer_fwd(1, xs, xb, p[1, 0].astype(BF16), W1,
                                att=(gather_comm(1, LATE), lambda got: chip_weights(got, LATE)))
    dx, loss = _loss_kernel(xs, loss_target[0], name="loss")
    loss = lax.psum(loss[0, 0], MESH_AXES)
    dx, G1, _ = _layer_bwd(1, dx, sv1, W1)
    sums1 = pair_sums(G1, everything, "l1")
    comm1 = _reduce_chips_comm(sums1, 1)
    late_sums = []

    def late0(G):
        late_sums.extend(pair_sums(G, LATE, "l0_late"))
        return _reduce_chips_comm(late_sums, 0, base=comm1.n_sems)

    grad_x, G0, quads = _layer_bwd(0, dx, sv0, W0, comm=comm1, late=late0)

    n1 = len(comm1.out_shapes)
    red = [{**finish_reduce(quads[n1:], late_sums, 0, LATE, "l0_late")},
           finish_reduce(quads[:n1], sums1, 1, everything, "l1")]
    sums0 = pair_sums(G0, EARLY, "l0")
    red[0].update(finish_reduce(_run_comm(_reduce_chips_comm(sums0, 0), name="reduce_chips_l0"), sums0, 0, EARLY, "l0"))
    g_red = {}
    for n in big:
        g = jnp.stack([red[li][n] for li in range(DEPTH)])
        g_red[n] = g[tuple(slice(0, s) for s in w_loc[n].shape)]
    small_l = [_layer_small_grads(G0, W0), _layer_small_grads(G1, W1)]
    g_small = {n: jnp.stack([small_l[li][n] for li in range(DEPTH)]) for n in small_l[0]}
    small_all = small_rep + small_sh
    sgp = _pack([g_small[n] for n in small_all], F32, PACK_COLS)
    (sall,) = _exchange([sgp[None]], MESH_AXES, name="reduce_small")
    sred = _sum_slots(sall.reshape((2 ** len(MESH_AXES),) + sgp.shape), F32, name="reduce_small_sum").reshape(-1)
    for n, g in zip(small_all, _unpack(sred, [g_small[n].shape for n in small_all])):
        if n in SMALL_SHARDED:
            width = w_loc[n].shape[-1]
            g = lax.dynamic_slice_in_dim(g, chip * width, width, axis=SMALL_SHARDED[n])
        g_red[n] = g

    delta, new_m, new_v = {}, {}, {}
    for n in big:
        delta[n], new_m[n], new_v[n] = _adamw(w_loc[n], g_red[n], m_loc[n], v_loc[n], name="adamw_" + n)
    shapes = [w_loc[n].shape for n in small_all]
    packs = [_pack([d[n] for n in small_all], F32, LANES) for d in (w_loc, g_red, m_loc, v_loc)]
    outs = _adamw(*packs, name="adamw_small")
    for d, o in zip((delta, new_m, new_v), outs):
        for n, a in zip(small_all, _unpack(o.reshape(-1), shapes)):
            d[n] = a
    return (loss, grad_x[None], *[g_red[n] for n in WEIGHTS], *[delta[n] for n in WEIGHTS],
            *[new_m[n] for n in WEIGHTS], *[new_v[n] for n in WEIGHTS])


def _unpack_rows(gathered, shapes):
    out, off = [], 0
    for s in shapes:
        n = math.prod(s)
        out.append(gathered[:, off:off + n].reshape((N_CHIPS,) + tuple(s)))
        off += n
    return out
```

```python
import functools
import math

import jax
import jax.numpy as jnp
from jax import lax
from jax.experimental import pallas as pl
from jax.experimental.pallas import tpu as pltpu

F32 = jnp.float32
BF16 = jnp.bfloat16

D_MODEL = 1024
DEPTH = 2
PLE_DIM = 256
HEAD_DIM = 64
LRU_WIDTH = 256
LRU_HEADS = 4
LRU_C = 8.0
CONV_K = 4
ATT_WIDTH = 256
ATT_HEADS = 4
SSD_WIDTH = 512
SSD_HEADS = 8
SSD_GROUPS = 2
SSD_STATE = 128
SSD_CHUNK = 128
SSD_CONV_DIM = 1024
FFN_DIM = 2816
ALPHA = (2.0 * DEPTH) ** 0.25
LN_EPS = 1e-5
RMS_EPS = 1e-5
IN_WIDTH = 2828
ADAM_LR = 0.001
ADAM_B1 = 0.9
ADAM_B2 = 0.999
ADAM_EPS = 1e-08
ADAM_WD = 0.01
ADAM_STEP = 10

H_WIDTH = 3072
COL_XBC, COL_Z, COL_U, COL_G, COL_Q, COL_K, COL_V, COL_SMALL = 0, 1024, 1536, 1792, 2048, 2304, 2560, 2816
LANE_F = 0
LANE_DT = 4
LANES = 128
SUBLANES = 8
NEG = -1e30

VMEM_LIMIT = 48 * 1024 * 1024

N_CHIPS = 4
MESH_AXES = ("x", "y", "c")
SHARE = 768


def _params(n):
    return pltpu.CompilerParams(dimension_semantics=("arbitrary",) * n, vmem_limit_bytes=VMEM_LIMIT)


def _pick(n, cands):
    for c in cands:
        if n % c == 0:
            return c
    return n


def _iota(shape, dim):
    return lax.broadcasted_iota(jnp.int32, shape, dim)


def _shift_down(x, s, prev8):
    if s == 0:
        return x
    r = pltpu.roll(x, s, 0)
    pr = pltpu.roll(prev8, s, 0)
    head = jnp.where(_iota(pr.shape, 0) < s, pr, r[:SUBLANES])
    return jnp.concatenate([head, r[SUBLANES:]], axis=0)


def _shift_up(x, s, next8):
    if s == 0:
        return x
    n = x.shape[0]
    r = pltpu.roll(x, n - s, 0)
    nr = pltpu.roll(next8, SUBLANES - s, 0)
    tail = jnp.where(_iota(nr.shape, 0) >= SUBLANES - s, nr, r[n - SUBLANES:])
    return jnp.concatenate([r[:n - SUBLANES], tail], axis=0)


def _scan_fwd(a, b):
    n = a.shape[0]
    row = _iota(a.shape, 0)
    d = 1
    while d < n:
        keep = row >= d
        a_s = jnp.where(keep, pltpu.roll(a, d, 0), 1.0)
        b_s = jnp.where(keep, pltpu.roll(b, d, 0), 0.0)
        b = a * b_s + b
        a = a * a_s
        d *= 2
    return a, b


def _scan_bwd(a, b):
    n = a.shape[0]
    row = _iota(a.shape, 0)
    d = 1
    while d < n:
        keep = row < n - d
        a_s = jnp.where(keep, pltpu.roll(a, n - d, 0), 1.0)
        b_s = jnp.where(keep, pltpu.roll(b, n - d, 0), 0.0)
        b = a * b_s + b
        a = a * a_s
        d *= 2
    return a, b


def _cumsum_rows(x, reverse=False):
    n = x.shape[0]
    row = _iota(x.shape, 0)
    d = 1
    while d < n:
        if reverse:
            x = x + jnp.where(row < n - d, pltpu.roll(x, n - d, 0), 0.0)
        else:
            x = x + jnp.where(row >= d, pltpu.roll(x, d, 0), 0.0)
        d *= 2
    return x


def _col(x, lane):
    return jnp.sum(jnp.where(_iota(x.shape, 1) == lane, x, 0.0), axis=1, keepdims=True)


def _row(x, r):
    return jnp.sum(jnp.where(_iota(x.shape, 0) == r, x, 0.0), axis=0, keepdims=True)


def _sigmoid(x):
    return jax.nn.sigmoid(x)


def _softplus(x):
    return jnp.maximum(x, 0.0) + jnp.log(1.0 + jnp.exp(-jnp.abs(x)))


def _gelu_and_grad(x):
    c0 = math.sqrt(2.0 / math.pi)
    inner = c0 * (x + 0.044715 * x * x * x)
    t = jnp.tanh(inner)
    g = 0.5 * x * (1.0 + t)
    dg = 0.5 * (1.0 + t) + 0.5 * x * (1.0 - t * t) * c0 * (1.0 + 3.0 * 0.044715 * x * x)
    return g, dg


def _dot(a, b, ca, cb):
    return lax.dot_general(a, b, (((ca,), (cb,)), ((), ())), preferred_element_type=F32)


def _conv_taps(xr, prev8, w, bias):
    y = bias + w[CONV_K - 1:CONV_K, :] * xr
    for j in range(CONV_K - 1):
        y = y + w[j:j + 1, :] * _shift_down(xr, CONV_K - 1 - j, prev8)
    return y


def _conv_taps_bwd(dy, next8, w, xr):
    dx = None
    dws = []
    for j in range(CONV_K):
        sh = _shift_up(dy, CONV_K - 1 - j, next8)
        term = w[j:j + 1, :] * sh
        dx = term if dx is None else dx + term
        dws.append(jnp.sum(sh * xr, axis=0, keepdims=True))
    return dx, jnp.concatenate(dws, axis=0)


def _head_expand(v, lane0, nheads, width):
    rows = v.shape[0]
    colhead = _iota((rows, width), 1) // HEAD_DIM
    out = jnp.zeros((rows, width), F32)
    for h in range(nheads):
        out = jnp.where(colhead == h, _col(v, lane0 + h), out)
    return out


def _head_reduce(x, lane0, nheads):
    rows = x.shape[0]
    colhead = _iota(x.shape, 1) // HEAD_DIM
    lane = _iota((rows, LANES), 1)
    out = jnp.zeros((rows, LANES), F32)
    for h in range(nheads):
        s = jnp.sum(jnp.where(colhead == h, x, 0.0), axis=1, keepdims=True)
        out = jnp.where(lane == lane0 + h, s, out)
    return out


def _mm(a, b, *, ta=False, tb=False, scale=1.0, out_dtype=F32, chip_cols=False, name):
    if ta:
        kk, m = a.shape
    else:
        m, kk = a.shape
    n = b.shape[0] if tb else b.shape[1]
    tm = _pick(m, (1024, 512, 256, 128))
    tk = _pick(kk, (1024, 768, 512, 256, 128))
    nk = kk // tk
    dn_a = 0 if ta else 1
    dn_b = 1 if tb else 0
    share = n // N_CHIPS
    if chip_cols:
        tn = n
        out_spec = pl.BlockSpec((N_CHIPS, tm, share), lambda i, j, k: (0, i, 0))
        out_shape = jax.ShapeDtypeStruct((N_CHIPS, m, share), out_dtype)
    else:
        tn = _pick(n, (1024, 768, 512, 256, 128))
        out_spec = pl.BlockSpec((tm, tn), lambda i, j, k: (i, j))
        out_shape = jax.ShapeDtypeStruct((m, n), out_dtype)

    def body(a_ref, b_ref, o_ref, acc):
        k = pl.program_id(2)

        @pl.when(k == 0)
        def _():
            acc[...] = jnp.zeros_like(acc)

        acc[...] += _dot(a_ref[...].astype(BF16), b_ref[...].astype(BF16), dn_a, dn_b)

        @pl.when(k == nk - 1)
        def _():
            if chip_cols:
                for c in range(N_CHIPS):
                    o_ref[c] = (acc[:, share * c:share * (c + 1)] * scale).astype(out_dtype)
            else:
                o_ref[...] = (acc[...] * scale).astype(out_dtype)

    a_spec = pl.BlockSpec((tk, tm), lambda i, j, k: (k, i)) if ta else pl.BlockSpec((tm, tk), lambda i, j, k: (i, k))
    b_spec = pl.BlockSpec((tn, tk), lambda i, j, k: (j, k)) if tb else pl.BlockSpec((tk, tn), lambda i, j, k: (k, j))
    return pl.pallas_call(
        body, name=name, grid=(m // tm, n // tn, nk),
        in_specs=[a_spec, b_spec],
        out_specs=out_spec, out_shape=out_shape,
        scratch_shapes=[pltpu.VMEM((tm, tn), F32)],
        compiler_params=_params(3),
    )(a, b)


def _mm_swiglu(xb, wg, wu, *, comm=None, name):
    t, d = xb.shape
    share = wg.shape[2]
    n = N_CHIPS * share
    tm = _pick(t, (512, 256, 128))
    tn = _pick(share, (768, 256, 128))
    per = share // tn

    def body(x_ref, wg_ref, wu_ref, g_ref, u_ref, a_ref):
        x = x_ref[...]
        g = _dot(x, wg_ref[...], 1, 0)
        u = _dot(x, wu_ref[...], 1, 0)
        g_ref[...] = g.astype(BF16)
        u_ref[...] = u.astype(BF16)
        a_ref[...] = (g * _sigmoid(g) * u).astype(BF16)

    o = jax.ShapeDtypeStruct((t, n), BF16)
    ospec = pl.BlockSpec((tm, tn), lambda j, i: (i, j))
    return _hosted_call(
        body, comm, (n // tn, t // tm), name=name,
        in_specs=[pl.BlockSpec((tm, d), lambda j, i: (i, 0)),
                  pl.BlockSpec((None, d, tn), lambda j, i: (j // per, 0, j % per)),
                  pl.BlockSpec((None, d, tn), lambda j, i: (j // per, 0, j % per))],
        out_specs=[ospec, ospec, ospec], out_shape=[o, o, o], scratch_shapes=[], args=[xb, wg, wu])


def _mm_swiglu_bwd(dr, wd, g, u, *, scale, name):
    t, d = dr.shape
    n = wd.shape[0]
    tm = _pick(t, (512, 256, 128))
    tn = _pick(n, (768, 256, 128))

    def body(dr_ref, wd_ref, g_ref, u_ref, dg_ref, du_ref):
        da = _dot(dr_ref[...].astype(BF16), wd_ref[...], 1, 1) * scale
        gg = g_ref[...].astype(F32)
        uu = u_ref[...].astype(F32)
        sg = _sigmoid(gg)
        dg_ref[...] = (da * uu * (sg * (1.0 + gg * (1.0 - sg)))).astype(BF16)
        du_ref[...] = (da * gg * sg).astype(BF16)

    o = jax.ShapeDtypeStruct((t, n), BF16)
    ospec = pl.BlockSpec((tm, tn), lambda j, i: (i, j))
    return pl.pallas_call(
        body, name=name, grid=(n // tn, t // tm),
        in_specs=[pl.BlockSpec((tm, d), lambda j, i: (i, 0)),
                  pl.BlockSpec((tn, d), lambda j, i: (j, 0)),
                  ospec, ospec],
        out_specs=[ospec, ospec], out_shape=[o, o],
        compiler_params=_params(2),
    )(dr, wd, g, u)


def _mm_ln(a, w, resid, gain, bias, *, rscale, mscale, name):
    t, kk = a.shape
    d = w.shape[1]
    tm = _pick(t, (512, 256, 128))
    tk = kk
    nk = kk // tk

    def body(a_ref, w_ref, r_ref, g_ref, b_ref, y_ref, yb_ref, xh_ref, rs_ref, acc):
        k = pl.program_id(1)

        @pl.when(k == 0)
        def _():
            acc[...] = jnp.zeros_like(acc)

        acc[...] += _dot(a_ref[...].astype(BF16), w_ref[...], 1, 0)

        @pl.when(k == nk - 1)
        def _():
            r = rscale * r_ref[...] + mscale * acc[...]
            mu = jnp.mean(r, axis=1, keepdims=True)
            xc = r - mu
            var = jnp.mean(xc * xc, axis=1, keepdims=True)
            rstd = lax.rsqrt(var + LN_EPS)
            xh = xc * rstd
            y = xh * g_ref[...] + b_ref[...]
            y_ref[...] = y
            yb_ref[...] = y.astype(BF16)
            xh_ref[...] = xh
            rs_ref[...] = rstd

    row = pl.BlockSpec((tm, d), lambda i, k: (i, 0))
    vec = pl.BlockSpec((1, d), lambda i, k: (0, 0))
    return pl.pallas_call(
        body, name=name, grid=(t // tm, nk),
        in_specs=[pl.BlockSpec((tm, tk), lambda i, k: (i, k)),
                  pl.BlockSpec((tk, d), lambda i, k: (k, 0)), row, vec, vec],
        out_specs=[row, row, row, pl.BlockSpec((tm, 1), lambda i, k: (i, 0))],
        out_shape=[jax.ShapeDtypeStruct((t, d), F32), jax.ShapeDtypeStruct((t, d), BF16),
                   jax.ShapeDtypeStruct((t, d), F32), jax.ShapeDtypeStruct((t, 1), F32)],
        scratch_shapes=[pltpu.VMEM((tm, d), F32)],
        compiler_params=_params(2),
    )(a, w, resid, gain.reshape(1, d), bias.reshape(1, d))


def _bwd_proj(pairs, resid, *, rscale, ln, name):
    t, kk = pairs[0][0].shape
    d = pairs[0][1].shape[-2]
    has_ln = ln is not None
    tm = _pick(t, (512, 256, 128) if has_ln else (1024, 512, 256, 128))
    tk = _pick(pairs[0][1].shape[-1], (1024, 768, 512, 256, 128))
    nk = kk // tk
    nt = t // tm
    npair = len(pairs)

    def body(*refs):
        ab = refs[:2 * npair]
        r_ref = refs[2 * npair]
        pos = 2 * npair + 1
        if has_ln:
            xh_ref, rs_ref, g_ref = refs[pos:pos + 3]
            pos += 3
            o_ref, ob_ref, dg_ref, db_ref = refs[pos:pos + 4]
            pos += 4
        else:
            o_ref = refs[pos]
            pos += 1
        acc = refs[pos]
        i = pl.program_id(0)
        k = pl.program_id(1)

        @pl.when(k == 0)
        def _():
            acc[...] = jnp.zeros_like(acc)

        for q in range(npair):
            acc[...] += _dot(ab[2 * q][...].astype(BF16), ab[2 * q + 1][...], 1, 1)

        @pl.when(k == nk - 1)
        def _():
            dy = rscale * r_ref[...] + acc[...]
            if not has_ln:
                o_ref[...] = dy
                return
            xh = xh_ref[...]
            w = dy * g_ref[...]
            m1 = jnp.mean(w, axis=1, keepdims=True)
            m2 = jnp.mean(w * xh, axis=1, keepdims=True)
            dr = rs_ref[...] * (w - m1 - xh * m2)
            o_ref[...] = dr
            ob_ref[...] = dr.astype(BF16)

            @pl.when(i == 0)
            def _():
                dg_ref[...] = jnp.zeros_like(dg_ref)
                db_ref[...] = jnp.zeros_like(db_ref)

            dg_ref[...] += jnp.sum(dy * xh, axis=0, keepdims=True)
            db_ref[...] += jnp.sum(dy, axis=0, keepdims=True)

    row = pl.BlockSpec((tm, d), lambda i, k: (i, 0))
    vec = pl.BlockSpec((1, d), lambda i, k: (0, 0))
    in_specs, args = [], []
    for a, b in pairs:
        if b.ndim == 3:
            per = b.shape[2] // tk
            b_spec = pl.BlockSpec((None, d, tk), lambda i, k, per=per: (k // per, 0, k % per))
        else:
            b_spec = pl.BlockSpec((d, tk), lambda i, k: (0, k))
        in_specs += [pl.BlockSpec((tm, tk), lambda i, k: (i, k)), b_spec]
        args += [a, b]
    in_specs.append(row)
    args.append(resid)
    out_specs = [row]
    out_shape = [jax.ShapeDtypeStruct((t, d), F32)]
    if has_ln:
        xh, rs, gain = ln
        in_specs += [row, pl.BlockSpec((tm, 1), lambda i, k: (i, 0)), vec]
        args += [xh, rs, gain.reshape(1, d)]
        out_specs += [row, vec, vec]
        out_shape += [jax.ShapeDtypeStruct((t, d), BF16)] + [jax.ShapeDtypeStruct((1, d), F32)] * 2
    return pl.pallas_call(
        body, name=name, grid=(nt, nk), in_specs=in_specs, out_specs=out_specs, out_shape=out_shape,
        scratch_shapes=[pltpu.VMEM((tm, d), F32)],
        compiler_params=_params(2),
    )(*args)


def _mm_pe(x3, x3b, pb, wgate, bgate, wproj, *, name):
    t, d = x3.shape
    pd = pb.shape[1]
    tm = _pick(t, (512, 256, 128))
    tn = _pick(d, (512, 256, 128))

    def body(x_ref, xb_ref, p_ref, wg_ref, bg_ref, wp_ref, y_ref, yb_ref, sg_ref, e_ref):
        sg = _sigmoid(_dot(xb_ref[...], wg_ref[...], 1, 0) + bg_ref[...])
        e = _dot(p_ref[...], wp_ref[...], 1, 0)
        y = x_ref[...] + sg * e
        y_ref[...] = y
        yb_ref[...] = y.astype(BF16)
        sg_ref[...] = sg.astype(BF16)
        e_ref[...] = e.astype(BF16)

    ospec = pl.BlockSpec((tm, tn), lambda i, j: (i, j))
    ob = jax.ShapeDtypeStruct((t, d), BF16)
    return pl.pallas_call(
        body, name=name, grid=(t // tm, d // tn),
        in_specs=[ospec, pl.BlockSpec((tm, d), lambda i, j: (i, 0)), pl.BlockSpec((tm, pd), lambda i, j: (i, 0)),
                  pl.BlockSpec((d, tn), lambda i, j: (0, j)), pl.BlockSpec((1, tn), lambda i, j: (0, j)),
                  pl.BlockSpec((pd, tn), lambda i, j: (0, j))],
        out_specs=[ospec, ospec, ospec, ospec],
        out_shape=[jax.ShapeDtypeStruct((t, d), F32), ob, ob, ob],
        compiler_params=_params(2),
    )(x3, x3b, pb, wgate, bgate.reshape(1, d), wproj)


def _pe_bwd_elem(dx4, sg, e, *, name):
    t, d = dx4.shape
    tm = _pick(t, (512, 256, 128))

    def body(dx_ref, sg_ref, e_ref, dgp_ref, de_ref, db_ref):
        dx = dx_ref[...]
        s = sg_ref[...].astype(F32)
        dgp = dx * e_ref[...].astype(F32) * s * (1.0 - s)
        dgp_ref[...] = dgp.astype(BF16)
        de_ref[...] = (dx * s).astype(BF16)

        @pl.when(pl.program_id(0) == 0)
        def _():
            db_ref[...] = jnp.zeros_like(db_ref)

        db_ref[...] += jnp.sum(dgp, axis=0, keepdims=True)

    row = pl.BlockSpec((tm, d), lambda i: (i, 0))
    ob = jax.ShapeDtypeStruct((t, d), BF16)
    return pl.pallas_call(
        body, name=name, grid=(t // tm,), in_specs=[row, row, row],
        out_specs=[row, row, pl.BlockSpec((1, d), lambda i: (0, 0))],
        out_shape=[ob, ob, jax.ShapeDtypeStruct((1, d), F32)],
        compiler_params=_params(1),
    )(dx4, sg, e)


def _loss_kernel(y, target, *, name):
    t, d = y.shape
    tm = _pick(t, (512, 256, 128))

    def body(y_ref, t_ref, dy_ref, l_ref):
        diff = y_ref[...] - t_ref[...]
        dy_ref[...] = diff * (1.0 / d)

        @pl.when(pl.program_id(0) == 0)
        def _():
            l_ref[...] = jnp.zeros_like(l_ref)

        part = jnp.sum(jnp.mean(diff * diff, axis=1, keepdims=True), axis=0, keepdims=True)
        l_ref[...] += 0.5 * part

    row = pl.BlockSpec((tm, d), lambda i: (i, 0))
    return pl.pallas_call(
        body, name=name, grid=(t // tm,), in_specs=[row, row],
        out_specs=[row, pl.BlockSpec((1, 1), lambda i: (0, 0))],
        out_shape=[jax.ShapeDtypeStruct((t, d), F32), jax.ShapeDtypeStruct((1, 1), F32)],
        compiler_params=_params(1),
    )(y, target)


LRU_TM = 256


def _lru_gate_terms(r, lam):
    sp = _softplus(-lam)
    la = -LRU_C * r * sp
    a = jnp.exp(la)
    em = jnp.tanh(la) * (jnp.exp(2.0 * la) + 1.0)
    s = jnp.sqrt(-em)
    return la, a, s, sp


def _lru_fwd(hbuf, conv_w, conv_b, wa, ba, wx, bx, lam, *, name):
    t = hbuf.shape[0]
    w = LRU_WIDTH
    tm = _pick(t, (LRU_TM, 128))
    cu, cg = COL_U // w, COL_G // w
    hb = tm // SUBLANES

    def body(u_ref, up_ref, g_ref, cw_ref, cb_ref, wa_ref, ba_ref, wx_ref, bx_ref, lam_ref,
             y_ref, u_out, r_out, i_out, a_out, h_out, carry):
        i = pl.program_id(0)

        @pl.when(i == 0)
        def _():
            carry[...] = jnp.zeros_like(carry)

        prev = jnp.where(i == 0, 0.0, up_ref[...])
        u = _conv_taps(u_ref[...], prev, cw_ref[...], cb_ref[...])
        ub = u.astype(BF16)
        r = _sigmoid(_dot(ub, wa_ref[...], 1, 0) + ba_ref[...])
        ig = _sigmoid(_dot(ub, wx_ref[...], 1, 0) + bx_ref[...])
        _, a, s, _ = _lru_gate_terms(r, lam_ref[...])
        b = s * (ig * u)
        acum, hs = _scan_fwd(a, b)
        h = hs + acum * carry[0:1, :]
        carry[...] = jnp.broadcast_to(h[tm - 1:tm, :], carry.shape)
        gl, _ = _gelu_and_grad(g_ref[...])
        y_ref[...] = h * gl
        u_out[...] = u
        r_out[...] = r
        i_out[...] = ig
        a_out[...] = a
        h_out[...] = h

    row = pl.BlockSpec((tm, w), lambda i: (i, 0))
    vec = pl.BlockSpec((1, w), lambda i: (0, 0))
    mat = pl.BlockSpec((w, w), lambda i: (0, 0))
    o = jax.ShapeDtypeStruct((t, w), F32)
    return pl.pallas_call(
        body, name=name, grid=(t // tm,),
        in_specs=[pl.BlockSpec((tm, w), lambda i: (i, cu)),
                  pl.BlockSpec((SUBLANES, w), lambda i: (jnp.maximum(i * hb - 1, 0), cu)),
                  pl.BlockSpec((tm, w), lambda i: (i, cg)),
                  pl.BlockSpec((CONV_K, w), lambda i: (0, 0)), vec, mat, vec, mat, vec, vec],
        out_specs=[row] * 6, out_shape=[o] * 6,
        scratch_shapes=[pltpu.VMEM((SUBLANES, w), F32)],
        compiler_params=_params(1),
    )(hbuf, hbuf, hbuf, conv_w, conv_b, wa, ba, wx, bx, lam)


def _lru_bwd(dymix, hbuf, u, r, ig, a, h, conv_w, wa, wx, lam, *, name):
    t = hbuf.shape[0]
    w = LRU_WIDTH
    tm = _pick(t, (LRU_TM, 128))
    nb = t // tm
    cu, cg = COL_U // w, COL_G // w
    hb = tm // SUBLANES
    last8 = t // SUBLANES - 1

    def body(dy_ref, ur_ref, g_ref, u_ref, r_ref, i_ref, a_ref, an_ref, h_ref, hp_ref,
             cw_ref, wa_ref, wx_ref, lam_ref,
             dur_ref, dgr_ref, dcw_ref, dcb_ref, dwa_ref, dba_ref, dwx_ref, dbx_ref, dlam_ref,
             lcarry, dnext):
        i = pl.program_id(0)
        ib = nb - 1 - i

        @pl.when(i == 0)
        def _():
            lcarry[...] = jnp.zeros_like(lcarry)
            dnext[...] = jnp.zeros_like(dnext)
            for ref in (dcw_ref, dcb_ref, dwa_ref, dba_ref, dwx_ref, dbx_ref, dlam_ref):
                ref[...] = jnp.zeros_like(ref)

        dy = dy_ref[...]
        hh = h_ref[...]
        av = a_ref[...]
        uu = u_ref[...]
        rr = r_ref[...]
        ii = i_ref[...]
        lam_v = lam_ref[...]
        gl, dgl = _gelu_and_grad(g_ref[...])
        dgr_ref[...] = (dy * hh * dgl).astype(BF16)
        dh_out = dy * gl
        a_next = _shift_up(av, 1, jnp.where(ib == nb - 1, 0.0, an_ref[...]))
        acum, ls = _scan_bwd(a_next, dh_out)
        lam_adj = ls + acum * lcarry[0:1, :]
        lcarry[...] = jnp.broadcast_to(lam_adj[0:1, :], lcarry.shape)
        h_prev = _shift_down(hh, 1, jnp.where(ib == 0, 0.0, hp_ref[...]))
        da = lam_adj * h_prev
        _, a2, s, sp = _lru_gate_terms(rr, lam_v)
        d_igu = lam_adj * s
        ds = lam_adj * ii * uu
        dla = da * a2 - ds * (a2 * a2) / s
        dr = dla * (-LRU_C * sp)
        dlam_ref[...] += jnp.sum(dla * (LRU_C * rr * _sigmoid(-lam_v)), axis=0, keepdims=True)
        dpre_r = dr * rr * (1.0 - rr)
        dpre_i = d_igu * uu * ii * (1.0 - ii)
        prb = dpre_r.astype(BF16)
        pib = dpre_i.astype(BF16)
        ub = uu.astype(BF16)
        du = d_igu * ii + _dot(prb, wa_ref[...], 1, 1) + _dot(pib, wx_ref[...], 1, 1)
        dwa_ref[...] += _dot(ub, prb, 0, 0)
        dwx_ref[...] += _dot(ub, pib, 0, 0)
        dba_ref[...] += jnp.sum(dpre_r, axis=0, keepdims=True)
        dbx_ref[...] += jnp.sum(dpre_i, axis=0, keepdims=True)
        dur, dws = _conv_taps_bwd(du, dnext[...], cw_ref[...], ur_ref[...])
        dur_ref[...] = dur.astype(BF16)
        dcw_ref[...] += dws
        dcb_ref[...] += jnp.sum(du, axis=0, keepdims=True)
        dnext[...] = du[:SUBLANES]

    def rowspec(col):
        return pl.BlockSpec((tm, w), lambda i: (nb - 1 - i, col))

    row = rowspec(0)
    nxt = pl.BlockSpec((SUBLANES, w), lambda i: (jnp.minimum((nb - i) * hb, last8), 0))
    prv = pl.BlockSpec((SUBLANES, w), lambda i: (jnp.maximum((nb - 1 - i) * hb - 1, 0), 0))
    vec = pl.BlockSpec((1, w), lambda i: (0, 0))
    mat = pl.BlockSpec((w, w), lambda i: (0, 0))
    cw = pl.BlockSpec((CONV_K, w), lambda i: (0, 0))
    o = jax.ShapeDtypeStruct((t, w), BF16)
    v1 = jax.ShapeDtypeStruct((1, w), F32)
    m1 = jax.ShapeDtypeStruct((w, w), F32)
    return pl.pallas_call(
        body, name=name, grid=(nb,),
        in_specs=[rowspec(0), rowspec(cu), rowspec(cg), row, row, row, row, nxt, row, prv, cw, mat, mat, vec],
        out_specs=[row, row, cw, vec, mat, vec, mat, vec, vec],
        out_shape=[o, o, jax.ShapeDtypeStruct((CONV_K, w), F32), v1, m1, v1, m1, v1, v1],
        scratch_shapes=[pltpu.VMEM((SUBLANES, w), F32), pltpu.VMEM((SUBLANES, w), F32)],
        compiler_params=_params(1),
    )(dymix, hbuf, hbuf, u, r, ig, a, a, h, h, conv_w, wa, wx, lam)


FOX_T = 512
FOX_PREP_TM = 256


def _log_sigmoid(x):
    return jnp.minimum(x, 0.0) - jnp.log(1.0 + jnp.exp(-jnp.abs(x)))


def _fox_prep(hbuf, bf_vec, *, name):
    t = hbuf.shape[0]
    tm = _pick(t, (FOX_PREP_TM, 128))
    cs = COL_SMALL // LANES

    def body(s_ref, b_ref, eq_ref, ek_ref, carry):
        i = pl.program_id(0)

        @pl.when(i == 0)
        def _():
            carry[...] = jnp.zeros_like(carry)

        lf = _log_sigmoid(s_ref[...] + b_ref[...])
        f = _cumsum_rows(lf) + carry[0:1, :]
        carry[...] = jnp.broadcast_to(f[tm - 1:tm, :], carry.shape)
        lane = _iota((tm, LANES), 1)
        for h in range(ATT_HEADS):
            base = HEAD_DIM * (1 - h % 2)
            fh = _col(f, h)
            hi = fh.astype(BF16).astype(F32)
            mid = (fh - hi).astype(BF16).astype(F32)
            lo = fh - hi - mid
            terms = jnp.where(lane == base, hi, jnp.where(lane == base + 1, mid, jnp.where(lane == base + 2, lo, 0.0)))
            terms_k = jnp.where(lane == base + 3, -hi,
                                jnp.where(lane == base + 4, -mid, jnp.where(lane == base + 5, -lo, 0.0)))
            ones_q = ((lane >= base + 3) & (lane < base + 6)).astype(F32)
            ones_k = ((lane >= base) & (lane < base + 3)).astype(F32)
            eq_ref[:, LANES * h:LANES * (h + 1)] = (terms + ones_q).astype(BF16)
            ek_ref[:, LANES * h:LANES * (h + 1)] = (terms_k + ones_k).astype(BF16)

    ospec = pl.BlockSpec((tm, ATT_HEADS * LANES), lambda i: (i, 0))
    o = jax.ShapeDtypeStruct((t, ATT_HEADS * LANES), BF16)
    return pl.pallas_call(
        body, name=name, grid=(t // tm,),
        in_specs=[pl.BlockSpec((tm, LANES), lambda i: (i, cs)), pl.BlockSpec((1, LANES), lambda i: (0, 0))],
        out_specs=[ospec, ospec], out_shape=[o, o],
        scratch_shapes=[pltpu.VMEM((SUBLANES, LANES), F32)],
        compiler_params=_params(1),
    )(hbuf, bf_vec)


def _fox_post(dfc, hbuf, bf_vec, *, name):
    t = hbuf.shape[0]
    tm = _pick(t, (FOX_PREP_TM, 128))
    nb = t // tm
    cs = COL_SMALL // LANES

    def body(df_ref, s_ref, b_ref, o_ref, db_ref, carry):
        i = pl.program_id(0)

        @pl.when(i == 0)
        def _():
            carry[...] = jnp.zeros_like(carry)
            db_ref[...] = jnp.zeros_like(db_ref)

        dlf = _cumsum_rows(df_ref[...], reverse=True) + carry[0:1, :]
        carry[...] = jnp.broadcast_to(dlf[0:1, :], carry.shape)
        dl = dlf * _sigmoid(-(s_ref[...] + b_ref[...]))
        dl = jnp.where(_iota(dl.shape, 1) < ATT_HEADS, dl, 0.0)
        o_ref[...] = dl
        db_ref[...] += jnp.sum(dl, axis=0, keepdims=True)

    vec = pl.BlockSpec((1, LANES), lambda i: (0, 0))
    return pl.pallas_call(
        body, name=name, grid=(nb,),
        in_specs=[pl.BlockSpec((tm, LANES), lambda i: (nb - 1 - i, 0)),
                  pl.BlockSpec((tm, LANES), lambda i: (nb - 1 - i, cs)), vec],
        out_specs=[pl.BlockSpec((tm, LANES), lambda i: (nb - 1 - i, 0)), vec],
        out_shape=[jax.ShapeDtypeStruct((t, LANES), F32), jax.ShapeDtypeStruct((1, LANES), F32)],
        scratch_shapes=[pltpu.VMEM((SUBLANES, LANES), F32)],
        compiler_params=_params(1),
    )(dfc, hbuf, bf_vec)


def _fox_masks(i, j, tq):
    row = i * tq + _iota((tq, tq), 0)
    col = j * tq + _iota((tq, tq), 1)
    lane = _iota((1, LANES), 1)
    return col <= row, (lane < HEAD_DIM, lane >= HEAD_DIM)


def _hosting(body, n_in, n_out, n_scratch, comm, grid):
    na, no = len(comm.arrays), len(comm.out_shapes)

    def hosted(*refs):
        o0 = n_in + na
        s0 = o0 + n_out + no
        cargs = (refs[n_in:o0], refs[o0 + n_out:s0]) + tuple(refs[s0 + n_scratch:])
        a, b = pl.program_id(0), pl.program_id(1)

        @pl.when((a == 0) & (b == 0))
        def _():
            comm.start(*cargs)

        @pl.when((a == grid[0] - 1) & (b == 0))
        def _():
            comm.middle(*cargs)

        body(*refs[:n_in], *refs[o0:o0 + n_out], *refs[s0:s0 + n_scratch])

        @pl.when((a == grid[0] - 1) & (b == grid[1] - 1))
        def _():
            comm.finish(*cargs)

    return hosted


def _hosted_call(body, comm, grid, *, name, in_specs, out_specs, out_shape, scratch_shapes, args):
    n_out = len(out_shape)
    if comm is not None:
        cin, cout, sems = comm.specs()
        body = _hosting(body, len(in_specs), n_out, len(scratch_shapes), comm, grid)
        in_specs, out_specs = in_specs + cin, out_specs + cout
        out_shape, scratch_shapes, args = out_shape + comm.out_shapes, scratch_shapes + sems, args + list(comm.arrays)
    outs = pl.pallas_call(body, name=name, grid=grid, in_specs=in_specs, out_specs=out_specs,
                          out_shape=out_shape, scratch_shapes=scratch_shapes, compiler_params=_params(2))(*args)
    return outs[:n_out], outs[n_out:]


def _merge_comms(comms):
    comms = [c for c in comms if c is not None]
    if len(comms) <= 1:
        return comms[0] if comms else None

    def both(which):
        def run(ins, outs, ssem, rsem):
            ia = io = 0
            for c in comms:
                na, no = len(c.arrays), len(c.out_shapes)
                getattr(c, which)(ins[ia:ia + na], outs[io:io + no], ssem, rsem)
                ia, io = ia + na, io + no
        return run

    spans = sorted((c.base, c.base + c.n_own) for c in comms)
    assert all(a[1] <= b[0] for a, b in zip(spans, spans[1:])), "semaphore ranges overlap"
    return _Comm(sum((list(c.arrays) for c in comms), []), sum((list(c.out_shapes) for c in comms), []),
                 spans[-1][1], both("start"), both("finish"), middle=both("middle"))


def _fox_fwd(hbuf, eq, ek, *, comm=None, name):
    t = hbuf.shape[0]
    w = ATT_WIDTH
    tq = _pick(t, (FOX_T, 256, 128))
    nq = t // tq
    cq, ck, cv = COL_Q // w, COL_K // w, COL_V // w

    def body(q_ref, k_ref, v_ref, eq_ref, ek_ref, o_ref, lse_ref, m_s, l_s, acc_s):
        i = pl.program_id(0)
        j = pl.program_id(1)

        @pl.when(j == 0)
        def _():
            m_s[...] = jnp.full_like(m_s, NEG)
            l_s[...] = jnp.zeros_like(l_s)
            acc_s[...] = jnp.zeros_like(acc_s)

        def step(diagonal):
            _, hms = _fox_masks(i, j, tq)
            keys_first = (j * tq + _iota((tq, tq), 0)) <= (i * tq + _iota((tq, tq), 1))
            half = _iota((LANES, 1), 0)
            hrows = (half < HEAD_DIM, half >= HEAD_DIM)
            m_all = m_s[...]
            l_all = l_s[...]
            acc_old = [acc_s[LANES * pr:LANES * (pr + 1), :] for pr in range(2)]
            m_out, l_out, acc_out = [], [], []
            for pr in range(2):
                sl = slice(LANES * pr, LANES * (pr + 1))
                qp = q_ref[:, sl]
                kp = k_ref[:, sl]
                vt = v_ref[:, sl].T.astype(BF16)
                acc = acc_old[pr]
                for hh in range(2):
                    h = 2 * pr + hh
                    hsl = slice(LANES * h, LANES * (h + 1))
                    qm = jnp.where(hms[hh], (qp * (HEAD_DIM ** -0.5)).astype(BF16), eq_ref[:, hsl])
                    km = jnp.where(hms[hh], kp.astype(BF16), ek_ref[:, hsl])
                    st = _dot(km, qm, 1, 1)
                    if diagonal:
                        st = jnp.where(keys_first, st, NEG)
                    m_old = m_all[h:h + 1, :]
                    m_new = jnp.maximum(m_old, jnp.max(st, axis=0, keepdims=True))
                    alpha = jnp.exp(m_old - m_new)
                    pt = jnp.exp(st - m_new)
                    l_out.append(alpha * l_all[h:h + 1, :] + jnp.sum(pt, axis=0, keepdims=True))
                    m_out.append(m_new)
                    pv = _dot(vt, pt.astype(BF16), 1, 0)
                    acc = jnp.where(hrows[hh], alpha * acc_old[pr] + pv, acc)
                acc_out.append(acc)
            for h in range(ATT_HEADS):
                m_s[h:h + 1, :] = m_out[h]
                l_s[h:h + 1, :] = l_out[h]
            for pr in range(2):
                acc_s[LANES * pr:LANES * (pr + 1), :] = acc_out[pr]

        @pl.when(j < i)
        def _():
            step(False)

        @pl.when(j == i)
        def _():
            step(True)
            half = _iota((LANES, 1), 0)
            l_all = l_s[...]
            for pr in range(2):
                acc = acc_s[LANES * pr:LANES * (pr + 1), :]
                o_t = jnp.where(half < HEAD_DIM, acc / l_all[2 * pr:2 * pr + 1, :], acc / l_all[2 * pr + 1:2 * pr + 2, :])
                o_ref[:, LANES * pr:LANES * (pr + 1)] = o_t.T
            lse = m_s[...] + jnp.log(l_s[...])
            lse_ref[...] = jnp.where(_iota(lse.shape, 0) < ATT_HEADS, lse, 0.0)

    return _hosted_call(
        body, comm, (nq, nq), name=name,
        in_specs=[pl.BlockSpec((tq, w), lambda i, j: (i, cq)),
                  pl.BlockSpec((tq, w), lambda i, j: (jnp.minimum(j, i), ck)),
                  pl.BlockSpec((tq, w), lambda i, j: (jnp.minimum(j, i), cv)),
                  pl.BlockSpec((tq, ATT_HEADS * LANES), lambda i, j: (i, 0)),
                  pl.BlockSpec((tq, ATT_HEADS * LANES), lambda i, j: (jnp.minimum(j, i), 0))],
        out_specs=[pl.BlockSpec((tq, w), lambda i, j: (i, 0)),
                   pl.BlockSpec((SUBLANES, tq), lambda i, j: (0, i))],
        out_shape=[jax.ShapeDtypeStruct((t, w), F32), jax.ShapeDtypeStruct((SUBLANES, t), F32)],
        scratch_shapes=[pltpu.VMEM((SUBLANES, tq), F32), pltpu.VMEM((SUBLANES, tq), F32),
                        pltpu.VMEM((w, tq), F32)],
        args=[hbuf, hbuf, hbuf, eq, ek])


def _fox_delta(dymix, o, *, name):
    t, w = o.shape
    tm = _pick(t, (512, 256, 128))
    cdo = ATT_WIDTH // w

    def body(do_ref, o_ref, d_ref):
        d_ref[...] = _head_reduce(do_ref[...] * o_ref[...], 0, ATT_HEADS)

    return pl.pallas_call(
        body, name=name, grid=(t // tm,),
        in_specs=[pl.BlockSpec((tm, w), lambda i: (i, cdo)), pl.BlockSpec((tm, w), lambda i: (i, 0))],
        out_specs=pl.BlockSpec((tm, LANES), lambda i: (i, 0)),
        out_shape=jax.ShapeDtypeStruct((t, LANES), F32),
        compiler_params=_params(1),
    )(dymix, o)


def _fox_bwd(hbuf, eq, ek, dymix, lse_rows, delta_rows, *, comm=None, name):
    t = hbuf.shape[0]
    w = ATT_WIDTH
    tq = _pick(t, (FOX_T, 256, 128))
    nq = t // tq
    cq, ck, cv = COL_Q // w, COL_K // w, COL_V // w
    cdo = ATT_WIDTH // w

    def body(q_ref, k_ref, v_ref, eq_ref, ek_ref, do_ref, lse_ref, dl_ref, dk_ref, dv_ref, dfk_ref, dqt_ref, dfq_ref,
             dk_s, dv_s, dfk_s):
        j = pl.program_id(0)
        i = pl.program_id(1)

        @pl.when((i == 0) & (j == 0))
        def _():
            dqt_ref[...] = jnp.zeros_like(dqt_ref)
            dfq_ref[...] = jnp.zeros_like(dfq_ref)

        @pl.when(i == 0)
        def _():
            dk_s[...] = jnp.zeros_like(dk_s)
            dv_s[...] = jnp.zeros_like(dv_s)
            dfk_s[...] = jnp.zeros_like(dfk_s)

        def step(diagonal):
            _, hms = _fox_masks(i, j, tq)
            keys_first = (j * tq + _iota((tq, tq), 0)) <= (i * tq + _iota((tq, tq), 1))
            half = _iota((LANES, 1), 0)
            hrows = (half < HEAD_DIM, half >= HEAD_DIM)
            lse_all = lse_ref[...]
            dl_all = dl_ref[...]
            dvs, dks, dfks, dqts, dfqs = [], [], [], [], []
            for pr in range(2):
                sl = slice(LANES * pr, LANES * (pr + 1))
                qp = q_ref[:, sl]
                kp = k_ref[:, sl]
                kt = kp.T.astype(BF16)
                vpb = v_ref[:, sl].astype(BF16)
                dop = do_ref[:, sl]
                dv_p = jnp.zeros((tq, LANES), F32)
                dk_p = jnp.zeros((tq, LANES), F32)
                dqt_p = jnp.zeros((LANES, tq), F32)
                for hh in range(2):
                    h = 2 * pr + hh
                    hsl = slice(LANES * h, LANES * (h + 1))
                    qm = jnp.where(hms[hh], (qp * (HEAD_DIM ** -0.5)).astype(BF16), eq_ref[:, hsl])
                    km = jnp.where(hms[hh], kp.astype(BF16), ek_ref[:, hsl])
                    st = _dot(km, qm, 1, 1)
                    if diagonal:
                        st = jnp.where(keys_first, st, NEG)
                    pt = jnp.exp(st - lse_all[h:h + 1, :])
                    domb = jnp.where(hms[hh], dop, 0.0).astype(BF16)
                    dv_p = dv_p + _dot(pt.astype(BF16), domb, 1, 0)
                    dpt = _dot(vpb, domb, 1, 1)
                    dst = pt * (dpt - dl_all[h:h + 1, :])
                    dstb = dst.astype(BF16)
                    dk_p = dk_p + jnp.where(hms[hh], _dot(dstb, qm, 1, 0), 0.0)
                    dqt_p = dqt_p + _dot(jnp.where(hrows[hh], kt, 0.0), dstb, 1, 0)
                    part = dst[:, 0:LANES]
                    for c in range(1, tq // LANES):
                        part = part + dst[:, LANES * c:LANES * (c + 1)]
                    dfks.append(part)
                    dfqs.append(jnp.sum(dst, axis=0, keepdims=True))
                dvs.append(dv_p)
                dks.append(dk_p)
                dqts.append(dqt_p)
            dv_s[...] += jnp.concatenate(dvs, axis=1)
            dk_s[...] += jnp.concatenate(dks, axis=1)
            for h in range(ATT_HEADS):
                dfk_s[h] += dfks[h]
            cols = pl.ds(pl.multiple_of(i * tq, tq), tq)
            dqt_ref[:, cols] += jnp.concatenate(dqts, axis=0) * (HEAD_DIM ** -0.5)
            dfq_ref[:, cols] += jnp.concatenate(dfqs + [jnp.zeros((SUBLANES - ATT_HEADS, tq), F32)], axis=0)

        @pl.when(i > j)
        def _():
            step(False)

        @pl.when(i == j)
        def _():
            step(True)

        @pl.when(i == nq - 1)
        def _():
            dk_ref[...] = dk_s[...].astype(BF16)
            dv_ref[...] = dv_s[...].astype(BF16)
            lane = _iota((tq, LANES), 1)
            out = jnp.zeros((tq, LANES), F32)
            for h in range(ATT_HEADS):
                out = jnp.where(lane == h, jnp.sum(dfk_s[h], axis=1, keepdims=True), out)
            dfk_ref[...] = out

    qi = lambda j, i: jnp.maximum(i, j)
    rows = pl.BlockSpec((SUBLANES, tq), lambda j, i: (0, qi(j, i)))
    return _hosted_call(
        body, comm, (nq, nq), name=name,
        in_specs=[pl.BlockSpec((tq, w), lambda j, i: (qi(j, i), cq)),
                  pl.BlockSpec((tq, w), lambda j, i: (j, ck)),
                  pl.BlockSpec((tq, w), lambda j, i: (j, cv)),
                  pl.BlockSpec((tq, ATT_HEADS * LANES), lambda j, i: (qi(j, i), 0)),
                  pl.BlockSpec((tq, ATT_HEADS * LANES), lambda j, i: (j, 0)),
                  pl.BlockSpec((tq, w), lambda j, i: (qi(j, i), cdo)),
                  rows, rows],
        out_specs=[pl.BlockSpec((tq, w), lambda j, i: (j, 0)), pl.BlockSpec((tq, w), lambda j, i: (j, 0)),
                   pl.BlockSpec((tq, LANES), lambda j, i: (j, 0)),
                   pl.BlockSpec((w, t), lambda j, i: (0, 0)), pl.BlockSpec((SUBLANES, t), lambda j, i: (0, 0))],
        out_shape=[jax.ShapeDtypeStruct((t, w), BF16), jax.ShapeDtypeStruct((t, w), BF16),
                   jax.ShapeDtypeStruct((t, LANES), F32),
                   jax.ShapeDtypeStruct((w, t), F32), jax.ShapeDtypeStruct((SUBLANES, t), F32)],
        scratch_shapes=[pltpu.VMEM((tq, w), F32), pltpu.VMEM((tq, w), F32),
                        pltpu.VMEM((ATT_HEADS, tq, LANES), F32)],
        args=[hbuf, hbuf, hbuf, eq, ek, dymix, lse_rows, delta_rows])


GROUP_W = SSD_WIDTH // SSD_GROUPS
HEADS_PER_GROUP = SSD_HEADS // SSD_GROUPS


def _ssd_chunk_common(xr, prev8, sm, cw, cb, dtb, avec):
    c = _conv_taps(xr, prev8, cw, cb)
    sig = _sigmoid(c)
    xa = c * sig
    dt = _softplus(sm + dtb)
    a = dt * avec
    acum = _cumsum_rows(a)
    return c, sig, xa, dt, acum


def _ssd_head_cols(acum, acum_t):
    cols = [_col(acum, LANE_DT + h) for h in range(SSD_HEADS)]
    rows = [_row(acum_t, LANE_DT + h) for h in range(SSD_HEADS)]
    return cols, rows


def _expand_heads(vals, width):
    rows = vals[0].shape[0]
    colhead = _iota((rows, width), 1) // HEAD_DIM
    out = jnp.broadcast_to(vals[0], (rows, width))
    for h in range(1, len(vals)):
        out = jnp.where(colhead == h, vals[h], out)
    return out


def _ssd_decays(cols, g):
    mine = cols[HEADS_PER_GROUP * g:HEADS_PER_GROUP * (g + 1)]
    n = mine[0].shape[0]
    atots = [c[n - 1:n, :] for c in mine]
    e = _expand_heads([jnp.exp(c) for c in mine], GROUP_W)
    dec = _expand_heads([jnp.exp(t - c) for c, t in zip(mine, atots)], GROUP_W)
    etot = _expand_heads([jnp.exp(t) for t in atots], GROUP_W)
    return e, dec, etot


def _ssd_ldec(cols, rows, h, tril):
    return jnp.exp(jnp.where(tril, cols[h] - rows[h], NEG))


def _ssd_fwd(hbuf, conv_w, conv_b, dtb_vec, a_vec, d_exp, norm_g, *, name):
    t = hbuf.shape[0]
    L = SSD_CHUNK
    nc = t // L
    hb = L // SUBLANES
    cs = COL_SMALL // LANES
    cz = COL_Z // SSD_WIDTH

    def body(x_ref, xp_ref, z_ref, s_ref, cw_ref, cb_ref, dtb_ref, av_ref, dx_ref, ng_ref,
             yc_ref, y_ref, st_ref, state):
        i = pl.program_id(0)

        @pl.when(i == 0)
        def _():
            state[...] = jnp.zeros_like(state)

        prev = jnp.where(i == 0, 0.0, xp_ref[...])
        _, _, xa, dt, acum = _ssd_chunk_common(x_ref[...], prev, s_ref[...], cw_ref[...], cb_ref[...],
                                               dtb_ref[...], av_ref[...])
        cols, rows = _ssd_head_cols(acum, acum.T)
        xs = xa[:, :SSD_WIDTH]
        xdt = xs * _head_expand(dt, LANE_DT, SSD_HEADS, SSD_WIDTH)
        tril = _iota((L, L), 0) >= _iota((L, L), 1)
        lane = _iota((1, LANES), 1)
        ys = []
        for g in range(SSD_GROUPS):
            bg = xa[:, SSD_WIDTH + SSD_STATE * g:SSD_WIDTH + SSD_STATE * (g + 1)].astype(BF16)
            cg = xa[:, SSD_WIDTH + SSD_STATE * (SSD_GROUPS + g):SSD_WIDTH + SSD_STATE * (SSD_GROUPS + g + 1)].astype(BF16)
            gm = _dot(cg, bg, 1, 1)
            e, dec, etot = _ssd_decays(cols, g)
            s_in = state[g]
            st_ref[0, g] = s_in
            xg = xdt[:, GROUP_W * g:GROUP_W * (g + 1)]
            y_off = e * _dot(cg, s_in.astype(BF16), 1, 0)
            state[g] = etot * s_in + _dot(bg, (dec * xg).astype(BF16), 0, 0)
            for pr in range(2):
                xp = xg[:, LANES * pr:LANES * (pr + 1)].astype(BF16)
                outs = []
                for hh in range(2):
                    h = HEADS_PER_GROUP * g + 2 * pr + hh
                    m = gm * _ssd_ldec(cols, rows, h, tril)
                    outs.append(_dot(m.astype(BF16), xp, 1, 0))
                ys.append(jnp.where(lane < HEAD_DIM, outs[0], outs[1]) + y_off[:, LANES * pr:LANES * (pr + 1)])
        y = jnp.concatenate(ys, axis=1)
        y_ref[...] = y
        yd = y + dx_ref[...] * xs
        zz = z_ref[...]
        y2 = yd * zz * _sigmoid(zz)
        ng = ng_ref[...]
        outs = []
        for g in range(SSD_GROUPS):
            yg = y2[:, GROUP_W * g:GROUP_W * (g + 1)]
            rs = lax.rsqrt(jnp.mean(yg * yg, axis=1, keepdims=True) + RMS_EPS)
            outs.append(yg * rs * ng[:, GROUP_W * g:GROUP_W * (g + 1)])
        yc_ref[...] = jnp.concatenate(outs, axis=1)

    cdim = SSD_CONV_DIM
    vecc = pl.BlockSpec((1, cdim), lambda i: (0, 0))
    vecl = pl.BlockSpec((1, LANES), lambda i: (0, 0))
    vecw = pl.BlockSpec((1, SSD_WIDTH), lambda i: (0, 0))
    roww = pl.BlockSpec((L, SSD_WIDTH), lambda i: (i, 0))
    return pl.pallas_call(
        body, name=name, grid=(nc,),
        in_specs=[pl.BlockSpec((L, cdim), lambda i: (i, 0)),
                  pl.BlockSpec((SUBLANES, cdim), lambda i: (jnp.maximum(i * hb - 1, 0), 0)),
                  pl.BlockSpec((L, SSD_WIDTH), lambda i: (i, cz)),
                  pl.BlockSpec((L, LANES), lambda i: (i, cs)),
                  pl.BlockSpec((CONV_K, cdim), lambda i: (0, 0)), vecc, vecl, vecl, vecw, vecw],
        out_specs=[roww, roww, pl.BlockSpec((1, SSD_GROUPS, SSD_STATE, GROUP_W), lambda i: (i, 0, 0, 0))],
        out_shape=[jax.ShapeDtypeStruct((t, SSD_WIDTH), F32), jax.ShapeDtypeStruct((t, SSD_WIDTH), F32),
                   jax.ShapeDtypeStruct((nc, SSD_GROUPS, SSD_STATE, GROUP_W), F32)],
        scratch_shapes=[pltpu.VMEM((SSD_GROUPS, SSD_STATE, GROUP_W), F32)],
        compiler_params=_params(1),
    )(hbuf, hbuf, hbuf, hbuf, conv_w, conv_b, dtb_vec, a_vec, d_exp, norm_g)


def _ssd_bwd(dymix, hbuf, y_ssd, states, conv_w, conv_b, dtb_vec, a_vec, d_exp, norm_g, *, name):
    t = hbuf.shape[0]
    L = SSD_CHUNK
    nc = t // L
    hb = L // SUBLANES
    cs = COL_SMALL // LANES
    cz = COL_Z // SSD_WIDTH
    cdy = (LRU_WIDTH + ATT_WIDTH) // SSD_WIDTH
    cdim = SSD_CONV_DIM

    def body(dyc_ref, x_ref, xp_ref, z_ref, s_ref, y_ref, st_ref, cw_ref, cb_ref, dtb_ref, av_ref, dx_ref, ng_ref,
             dxr_ref, dz_ref, dsm_ref, dng_ref, dd_ref, da_ref, ddtb_ref, dcw_ref, dcb_ref,
             dstate, dnext):
        i = pl.program_id(0)
        ic = nc - 1 - i

        @pl.when(i == 0)
        def _():
            dstate[...] = jnp.zeros_like(dstate)
            dnext[...] = jnp.zeros_like(dnext)
            for ref in (dng_ref, dd_ref, da_ref, ddtb_ref, dcw_ref, dcb_ref):
                ref[...] = jnp.zeros_like(ref)

        xr = x_ref[...]
        sm = s_ref[...]
        prev = jnp.where(ic == 0, 0.0, xp_ref[...])
        avec = av_ref[...]
        c, sig, xa, dt, acum = _ssd_chunk_common(xr, prev, sm, cw_ref[...], cb_ref[...], dtb_ref[...], avec)
        cols, rows = _ssd_head_cols(acum, acum.T)
        xs = xa[:, :SSD_WIDTH]
        dtx = _head_expand(dt, LANE_DT, SSD_HEADS, SSD_WIDTH)
        xdt = xs * dtx
        tril = _iota((L, L), 0) >= _iota((L, L), 1)
        lane = _iota((1, LANES), 1)
        hmasks = (lane < HEAD_DIM, lane >= HEAD_DIM)

        y = y_ref[...]
        dexp = dx_ref[...]
        yd = y + dexp * xs
        zz = z_ref[...]
        sz = _sigmoid(zz)
        siluz = zz * sz
        y2 = yd * siluz
        ng = ng_ref[...]
        dyc = dyc_ref[...]
        dy2s, dngs = [], []
        for g in range(SSD_GROUPS):
            sl = slice(GROUP_W * g, GROUP_W * (g + 1))
            yg = y2[:, sl]
            rs = lax.rsqrt(jnp.mean(yg * yg, axis=1, keepdims=True) + RMS_EPS)
            wv = dyc[:, sl] * ng[:, sl]
            dngs.append(jnp.sum(dyc[:, sl] * yg * rs, axis=0, keepdims=True))
            dy2s.append(rs * wv - yg * (rs * rs * rs) * jnp.mean(wv * yg, axis=1, keepdims=True))
        dy2 = jnp.concatenate(dy2s, axis=1)
        dng_ref[...] += jnp.concatenate(dngs, axis=1)
        dz_ref[...] = (dy2 * yd * (sz * (1.0 + zz * (1.0 - sz)))).astype(BF16)
        dy = dy2 * siluz
        dd_ref[...] += jnp.sum(dy * xs, axis=0, keepdims=True)

        dxs, dbs, dcs = [], [], []
        datot = jnp.zeros((1, LANES), F32)
        lanes = _iota((L, LANES), 1)
        dacum = jnp.zeros((L, LANES), F32)
        for g in range(SSD_GROUPS):
            sl = slice(GROUP_W * g, GROUP_W * (g + 1))
            bg = xa[:, SSD_WIDTH + SSD_STATE * g:SSD_WIDTH + SSD_STATE * (g + 1)].astype(BF16)
            cg = xa[:, SSD_WIDTH + SSD_STATE * (SSD_GROUPS + g):SSD_WIDTH + SSD_STATE * (SSD_GROUPS + g + 1)].astype(BF16)
            gm = _dot(cg, bg, 1, 1)
            e, dec, etot = _ssd_decays(cols, g)
            s_in = st_ref[0, g]
            ds_out = dstate[g]
            dyg = dy[:, sl]
            xg = xdt[:, sl]
            edy = (e * dyg).astype(BF16)
            dstate[g] = etot * ds_out + _dot(cg, edy, 0, 0)
            dx_state = dec * _dot(bg, ds_out.astype(BF16), 1, 0)
            y_off = e * _dot(cg, s_in.astype(BF16), 1, 0)
            dacum = dacum + _head_reduce_group(dyg * y_off - xg * dx_state, g)
            dc_off = _dot(edy, s_in.astype(BF16), 1, 1)
            db_state = _dot((dec * xg).astype(BF16), ds_out.astype(BF16), 1, 1)
            dgsum = jnp.zeros((L, L), F32)
            dx_pairs = []
            for pr in range(2):
                psl = slice(LANES * pr, LANES * (pr + 1))
                xp = xg[:, psl]
                dyp = dyg[:, psl]
                dx_pair = jnp.zeros((L, LANES), F32)
                for hh in range(2):
                    h = HEADS_PER_GROUP * g + 2 * pr + hh
                    ldec = _ssd_ldec(cols, rows, h, tril)
                    dym = jnp.where(hmasks[hh], dyp, 0.0).astype(BF16)
                    xm = jnp.where(hmasks[hh], xp, 0.0).astype(BF16)
                    dx_pair = dx_pair + _dot((gm * ldec).astype(BF16), dym, 0, 0)
                    dml = _dot(dym, xm, 1, 1) * ldec
                    dgsum = dgsum + dml
                    qm = dml * gm
                    seg = jnp.sum(qm, axis=1, keepdims=True) - jnp.sum(qm.T, axis=1, keepdims=True)
                    dacum = dacum + jnp.where(lanes == LANE_DT + h, seg, 0.0)
                dx_pairs.append(dx_pair)
            dgb = dgsum.astype(BF16)
            dcs.append(_dot(dgb, bg, 1, 0) + dc_off)
            dbs.append(_dot(dgb, cg, 0, 0) + db_state)
            dxg = jnp.concatenate(dx_pairs, axis=1) + dx_state
            dxs.append(dxg)
            v = jnp.sum(dx_state * xg, axis=0, keepdims=True) + etot * jnp.sum(ds_out * s_in, axis=0, keepdims=True)
            datot = datot + _head_reduce_row(v, LANE_DT + HEADS_PER_GROUP * g, HEADS_PER_GROUP)
        dx = jnp.concatenate(dxs, axis=1)
        dacum = dacum + jnp.where(_iota((L, LANES), 0) == L - 1, datot, 0.0)
        da = _cumsum_rows(dacum, reverse=True)
        ddt = da * avec + _head_reduce(dx * xs, LANE_DT, SSD_HEADS)
        da_ref[...] += jnp.sum(da * dt, axis=0, keepdims=True)
        ddt_raw = ddt * _sigmoid(sm + dtb_ref[...])
        ddt_raw = jnp.where((lanes >= LANE_DT) & (lanes < LANE_DT + SSD_HEADS), ddt_raw, 0.0)
        dsm_ref[...] = ddt_raw
        ddtb_ref[...] += jnp.sum(ddt_raw, axis=0, keepdims=True)
        dxs_total = dx * dtx + dexp * dy
        dxa = jnp.concatenate([dxs_total] + dbs + dcs, axis=1)
        dc = dxa * (sig * (1.0 + c * (1.0 - sig)))
        dxr, dws = _conv_taps_bwd(dc, dnext[...], cw_ref[...], xr)
        dxr_ref[...] = dxr.astype(BF16)
        dcw_ref[...] += dws
        dcb_ref[...] += jnp.sum(dc, axis=0, keepdims=True)
        dnext[...] = dc[:SUBLANES]

    rev = lambda i: nc - 1 - i
    vecc = pl.BlockSpec((1, cdim), lambda i: (0, 0))
    vecl = pl.BlockSpec((1, LANES), lambda i: (0, 0))
    vecw = pl.BlockSpec((1, SSD_WIDTH), lambda i: (0, 0))
    cwspec = pl.BlockSpec((CONV_K, cdim), lambda i: (0, 0))
    roww = pl.BlockSpec((L, SSD_WIDTH), lambda i: (rev(i), 0))
    return pl.pallas_call(
        body, name=name, grid=(nc,),
        in_specs=[pl.BlockSpec((L, SSD_WIDTH), lambda i: (rev(i), cdy)),
                  pl.BlockSpec((L, cdim), lambda i: (rev(i), 0)),
                  pl.BlockSpec((SUBLANES, cdim), lambda i: (jnp.maximum(rev(i) * hb - 1, 0), 0)),
                  pl.BlockSpec((L, SSD_WIDTH), lambda i: (rev(i), cz)),
                  pl.BlockSpec((L, LANES), lambda i: (rev(i), cs)),
                  roww,
                  pl.BlockSpec((1, SSD_GROUPS, SSD_STATE, GROUP_W), lambda i: (rev(i), 0, 0, 0)),
                  cwspec, vecc, vecl, vecl, vecw, vecw],
        out_specs=[pl.BlockSpec((L, cdim), lambda i: (rev(i), 0)), roww,
                   pl.BlockSpec((L, LANES), lambda i: (rev(i), 0)),
                   vecw, vecw, vecl, vecl, cwspec, vecc],
        out_shape=[jax.ShapeDtypeStruct((t, cdim), BF16), jax.ShapeDtypeStruct((t, SSD_WIDTH), BF16),
                   jax.ShapeDtypeStruct((t, LANES), F32),
                   jax.ShapeDtypeStruct((1, SSD_WIDTH), F32), jax.ShapeDtypeStruct((1, SSD_WIDTH), F32),
                   jax.ShapeDtypeStruct((1, LANES), F32), jax.ShapeDtypeStruct((1, LANES), F32),
                   jax.ShapeDtypeStruct((CONV_K, cdim), F32), jax.ShapeDtypeStruct((1, cdim), F32)],
        scratch_shapes=[pltpu.VMEM((SSD_GROUPS, SSD_STATE, GROUP_W), F32), pltpu.VMEM((SUBLANES, cdim), F32)],
        compiler_params=_params(1),
    )(dymix, hbuf, hbuf, hbuf, hbuf, y_ssd, states, conv_w, conv_b, dtb_vec, a_vec, d_exp, norm_g)


def _head_reduce_group(x, g):
    return _head_reduce(x, LANE_DT + HEADS_PER_GROUP * g, HEADS_PER_GROUP)


def _head_reduce_row(v, lane0, nheads):
    colhead = _iota(v.shape, 1) // HEAD_DIM
    lane = _iota((1, LANES), 1)
    out = jnp.zeros((1, LANES), F32)
    for h in range(nheads):
        s = jnp.sum(jnp.where(colhead == h, v, 0.0), axis=1, keepdims=True)
        out = jnp.where(lane == lane0 + h, s, out)
    return out


def _exchange(inps, axes, *, swap=False, name):
    n = 2 ** len(axes)
    assert not swap or n == 2
    counts = [a.shape[0] for a in inps]
    out_shapes = [jax.ShapeDtypeStruct(a.shape if swap else (n,) + a.shape, a.dtype) for a in inps]
    units = sum(counts)
    na = len(inps)

    def body(*refs):
        in_refs, out_refs = refs[:na], refs[na:2 * na]
        send_sems, recv_sems, local_sems = refs[2 * na:]
        pos = {ax: lax.axis_index(ax) for ax in MESH_AXES}

        def slot_of(coord):
            s = 0
            for ax in axes:
                s = s * 2 + coord[ax]
            return s

        me = slot_of(pos)
        copies = []
        unit = 0
        for a in range(na):
            for it in range(counts[a]):
                dst = out_refs[a].at[it] if swap else out_refs[a].at[me, it]
                if not swap:
                    cp = pltpu.make_async_copy(in_refs[a].at[it], dst, local_sems.at[unit])
                    cp.start()
                    copies.append(cp)
                for delta in range(1, n):
                    coord = dict(pos)
                    for b, ax in enumerate(reversed(axes)):
                        if (delta >> b) & 1:
                            coord[ax] = 1 - pos[ax]
                    k = unit * (n - 1) + delta - 1
                    cp = pltpu.make_async_remote_copy(
                        src_ref=in_refs[a].at[it], dst_ref=dst,
                        send_sem=send_sems.at[k], recv_sem=recv_sems.at[k],
                        device_id=(coord["x"], coord["y"], coord["c"]), device_id_type=pl.DeviceIdType.MESH)
                    cp.start()
                    copies.append(cp)
                unit += 1
        for cp in copies:
            cp.wait()

    any_spec = pl.BlockSpec(memory_space=pl.ANY)
    return pl.pallas_call(
        body, name=name,
        in_specs=[any_spec] * na, out_specs=[any_spec] * na, out_shape=out_shapes,
        scratch_shapes=[pltpu.SemaphoreType.DMA((units * (n - 1),)), pltpu.SemaphoreType.DMA((units * (n - 1),)),
                        pltpu.SemaphoreType.DMA((units,))],
    )(*inps)


class _Comm:
    def __init__(self, arrays, out_shapes, n_own, start, finish, base=0, middle=None):
        self.arrays, self.out_shapes, self.start, self.finish = arrays, out_shapes, start, finish
        self.middle = middle or (lambda *refs: None)
        self.base, self.n_own, self.n_sems = base, n_own, base + n_own

    def specs(self):
        any_spec = pl.BlockSpec(memory_space=pl.ANY)
        sems = [pltpu.SemaphoreType.DMA((self.n_sems,)), pltpu.SemaphoreType.DMA((self.n_sems,))]
        return [any_spec] * len(self.arrays), [any_spec] * len(self.out_shapes), sems


def _run_comm(comm, *, name):
    na, no = len(comm.arrays), len(comm.out_shapes)

    def body(*refs):
        args = (refs[:na], refs[na:na + no]) + tuple(refs[na + no:])
        comm.start(*args)
        comm.middle(*args)
        comm.finish(*args)

    in_specs, out_specs, sems = comm.specs()
    return pl.pallas_call(body, name=name, in_specs=in_specs, out_specs=out_specs, out_shape=comm.out_shapes,
                          scratch_shapes=sems)(*comm.arrays)


def _chip_peer(x, y, d):
    px = 1 - x if d & 2 else x
    py = 1 - y if d & 1 else y
    return px, py, 2 * px + py


def _gather_layer_comm(srcs, li, base=0):
    counts = [s.shape[0] for s in srcs]
    units = [(a, it) for a in range(len(srcs)) for it in range(counts[a])]
    n_ici = 3 * len(units)
    out_shapes = [jax.ShapeDtypeStruct((N_CHIPS,) + s.shape, s.dtype) for s in srcs]

    def ici(ins, outs, ssem, rsem, u, d):
        x, y, c = (lax.axis_index(ax) for ax in MESH_AXES)
        a, it = units[u]
        px, py, _ = _chip_peer(x, y, d)
        k = base + 3 * u + d - 1
        return pltpu.make_async_remote_copy(
            src_ref=ins[a].at[it], dst_ref=outs[a].at[2 * x + y, it], send_sem=ssem.at[k], recv_sem=rsem.at[k],
            device_id=(px, py, c), device_id_type=pl.DeviceIdType.MESH)

    def arrived(ins, outs, ssem, rsem, u, d):
        x, y, c = (lax.axis_index(ax) for ax in MESH_AXES)
        a, it = units[u]
        _, _, pk = _chip_peer(x, y, d)
        k = base + 3 * u + d - 1
        return pltpu.make_async_remote_copy(
            src_ref=ins[a].at[it], dst_ref=outs[a].at[pk, it], send_sem=ssem.at[k], recv_sem=rsem.at[k],
            device_id=(x, y, c), device_id_type=pl.DeviceIdType.MESH)

    def forward(ins, outs, ssem, rsem, u, slot):
        x, y, c = (lax.axis_index(ax) for ax in MESH_AXES)
        a, it = units[u]
        pk = 2 * x + y if slot == 0 else _chip_peer(x, y, slot)[2]
        src = ins[a].at[it] if slot == 0 else outs[a].at[pk, it]
        k = base + n_ici + 4 * u + slot
        return pltpu.make_async_remote_copy(
            src_ref=src, dst_ref=outs[a].at[pk, it], send_sem=ssem.at[k], recv_sem=rsem.at[k],
            device_id=(x, y, 1 - c), device_id_type=pl.DeviceIdType.MESH)

    def start(ins, outs, ssem, rsem):
        for u in range(len(units)):
            forward(ins, outs, ssem, rsem, u, 0).start()

        @pl.when(lax.axis_index("c") == li)
        def _():
            for u in range(len(units)):
                for d in range(1, N_CHIPS):
                    ici(ins, outs, ssem, rsem, u, d).start()

    def middle(ins, outs, ssem, rsem):
        @pl.when(lax.axis_index("c") == li)
        def _():
            for u in range(len(units)):
                for d in range(1, N_CHIPS):
                    arrived(ins, outs, ssem, rsem, u, d).wait_recv()
                    forward(ins, outs, ssem, rsem, u, d).start()

    def finish(ins, outs, ssem, rsem):
        c = lax.axis_index("c")

        @pl.when(c == li)
        def _():
            for u in range(len(units)):
                for d in range(1, N_CHIPS):
                    ici(ins, outs, ssem, rsem, u, d).wait_send()
                    forward(ins, outs, ssem, rsem, u, d).wait_send()

        @pl.when(c != li)
        def _():
            for u in range(len(units)):
                for d in range(1, N_CHIPS):
                    forward(ins, outs, ssem, rsem, u, d).wait_recv()

        for u in range(len(units)):
            forward(ins, outs, ssem, rsem, u, 0).wait()

    return _Comm(srcs, out_shapes, n_ici + 4 * len(units), start, finish, base, middle)


def _reduce_chips_comm(sums, li, base=0):
    counts = [s.shape[0] for s in sums]
    units = [(a, it) for a in range(len(sums)) for it in range(counts[a])]
    out_shapes = [jax.ShapeDtypeStruct((N_CHIPS, s.shape[0]) + s.shape[2:], s.dtype) for s in sums]

    def copy(ins, outs, ssem, rsem, u, d):
        x, y, c = (lax.axis_index(ax) for ax in MESH_AXES)
        a, it = units[u]
        px, py, pk = _chip_peer(x, y, d)
        k = base + 3 * u + d - 1
        return pltpu.make_async_remote_copy(
            src_ref=ins[a].at[it, pk], dst_ref=outs[a].at[2 * x + y, it], send_sem=ssem.at[k], recv_sem=rsem.at[k],
            device_id=(px, py, c), device_id_type=pl.DeviceIdType.MESH)

    def start(ins, outs, ssem, rsem):
        @pl.when(lax.axis_index("c") == li)
        def _():
            for u in range(len(units)):
                for d in range(1, N_CHIPS):
                    copy(ins, outs, ssem, rsem, u, d).start()

    def finish(ins, outs, ssem, rsem):
        @pl.when(lax.axis_index("c") == li)
        def _():
            for u in range(len(units)):
                for d in range(1, N_CHIPS):
                    copy(ins, outs, ssem, rsem, u, d).wait()

    return _Comm(sums, out_shapes, 3 * len(units), start, finish, base)


def _sum_slots(buf, out_dtype, *, name):
    n, rows, cols = buf.shape
    tm = _pick(rows, (512, 256, 128, 8))
    if rows % tm:
        tm = rows

    def body(b_ref, o_ref):
        acc = b_ref[0].astype(F32)
        for s in range(1, n):
            acc = acc + b_ref[s].astype(F32)
        o_ref[...] = acc.astype(out_dtype)

    return pl.pallas_call(
        body, name=name, grid=(pl.cdiv(rows, tm),),
        in_specs=[pl.BlockSpec((n, tm, cols), lambda i: (0, i, 0))],
        out_specs=pl.BlockSpec((tm, cols), lambda i: (i, 0)),
        out_shape=jax.ShapeDtypeStruct((rows, cols), out_dtype),
        compiler_params=_params(1),
    )(buf)


def _sum_pair(a, b, out_dtype, *, name):
    shape = a.shape
    cols = shape[-1]
    a2, b2 = a.reshape(-1, cols), b.reshape(-1, cols)
    rows = a2.shape[0]
    tm = _pick(rows, (512, 256, 128, 8))

    def body(a_ref, b_ref, o_ref):
        o_ref[...] = (a_ref[...].astype(F32) + b_ref[...].astype(F32)).astype(out_dtype)

    spec = pl.BlockSpec((tm, cols), lambda i: (i, 0))
    return pl.pallas_call(
        body, name=name, grid=(rows // tm,), in_specs=[spec, spec], out_specs=spec,
        out_shape=jax.ShapeDtypeStruct((rows, cols), out_dtype), compiler_params=_params(1),
    )(a2, b2).reshape(shape)


def _adamw(w, g, m, v, *, name):
    shape = w.shape
    cols = shape[-1]
    rows = w.size // cols
    w2, g2, m2, v2 = (a.reshape(rows, cols) for a in (w, g, m, v))
    tm = _pick(rows, (256, 128, 64, 32, 16, 8))
    if rows % tm:
        tm = rows
    bc1 = 1.0 - ADAM_B1 ** ADAM_STEP
    bc2 = 1.0 - ADAM_B2 ** ADAM_STEP

    def body(w_ref, g_ref, m_ref, v_ref, d_ref, nm_ref, nv_ref):
        gg = g_ref[...]
        mm = ADAM_B1 * m_ref[...] + (1.0 - ADAM_B1) * gg
        vv = ADAM_B2 * v_ref[...] + (1.0 - ADAM_B2) * (gg * gg)
        m_hat = mm / bc1
        v_hat = vv / bc2
        d_ref[...] = -ADAM_LR * (m_hat / (jnp.sqrt(v_hat) + ADAM_EPS) + ADAM_WD * w_ref[...])
        nm_ref[...] = mm
        nv_ref[...] = vv

    spec = pl.BlockSpec((tm, cols), lambda i: (i, 0))
    o = jax.ShapeDtypeStruct((rows, cols), F32)
    outs = pl.pallas_call(
        body, name=name, grid=(rows // tm,), in_specs=[spec] * 4, out_specs=[spec] * 3, out_shape=[o] * 3,
        compiler_params=_params(1),
    )(w2, g2, m2, v2)
    return tuple(a.reshape(shape) for a in outs)


def _layer_fwd(li, x, xb, pb, W, up=None, att=None):
    nm = lambda s: f"l{li}_{s}"
    sv = {"x_in_b": xb}
    (g1, u1, a1), got = _mm_swiglu(xb, W["ffn1_wg"], W["ffn1_wu"], comm=up[0] if up else None, name=nm("ffn1_up"))
    if up:
        W = {**W, **up[1](got)}
    x1, x1b, xh1, rs1 = _mm_ln(a1, W["ffn1_wd"], x, W["ln1_g"], W["ln1_b"], rscale=ALPHA, mscale=0.5, name=nm("ffn1_down_ln"))
    hbuf = _mm(x1b, W["w_in_p"], name=nm("in_proj"))
    ya, lu, lr, lig, la, lh = _lru_fwd(hbuf, W["lru_conv_w"], W["lru_conv_b"], W["lru_wa_bd"], W["lru_ba"],
                                       W["lru_wx_bd"], W["lru_bx"], W["lru_lambda"], name=nm("lru_fwd"))
    eq, ek = _fox_prep(hbuf, W["fox_bf_vec"], name=nm("fox_prep"))
    (yb, lse_rows), got = _fox_fwd(hbuf, eq, ek, comm=att[0] if att else None, name=nm("fox_fwd"))
    if att:
        W = {**W, **att[1](got)}
    yc, yssd, states = _ssd_fwd(hbuf, W["ssd_conv_w"], W["ssd_conv_b"], W["ssd_dtb_vec"], W["ssd_a_vec"],
                                W["ssd_d_exp"], W["ssd_norm_g"], name=nm("ssd_fwd"))
    ymix = jnp.concatenate([ya, yb, yc], axis=1).astype(BF16)
    x2, x2b, xh2, rs2 = _mm_ln(ymix, W["w_out"], x1, W["ln2_g"], W["ln2_b"], rscale=ALPHA, mscale=1.0, name=nm("out_proj_ln"))
    (g2, u2, a2), _ = _mm_swiglu(x2b, W["ffn2_wg"], W["ffn2_wu"], name=nm("ffn2_up"))
    x3, x3b, xh3, rs3 = _mm_ln(a2, W["ffn2_wd"], x2, W["ln3_g"], W["ln3_b"], rscale=ALPHA, mscale=0.5, name=nm("ffn2_down_ln"))
    x4, x4b, sg, e = _mm_pe(x3, x3b, pb, W["pe_gate_w"], W["pe_gate_b"], W["pe_proj"], name=nm("ple"))
    sv.update(g1=g1, u1=u1, a1=a1, x1b=x1b, xh1=xh1, rs1=rs1, hbuf=hbuf, lu=lu, lr=lr, lig=lig, la=la, lh=lh,
              eq=eq, ek=ek, lse_rows=lse_rows, yb=yb, yssd=yssd, states=states, ymix=ymix, x2b=x2b, xh2=xh2, rs2=rs2,
              g2=g2, u2=u2, a2=a2, x3b=x3b, xh3=xh3, rs3=rs3, sg=sg, e=e, pb=pb)
    return x4, x4b, sv, W


def _layer_bwd(li, dx4, sv, W, comm=None, late=None):
    nm = lambda s: f"l{li}_{s}"
    G = {}
    dgp, de, dbg = _pe_bwd_elem(dx4, sv["sg"], sv["e"], name=nm("ple_bwd"))
    G["pe_gate_b"] = dbg
    G["pe_gate_w"] = _mm(sv["x3b"], dgp, ta=True, out_dtype=BF16, name=nm("d_pe_gate_w"))
    G["pe_proj"] = _mm(sv["pb"], de, ta=True, out_dtype=BF16, chip_cols=True, name=nm("d_pe_proj"))
    dr3, dr3b, G["ln3_g"], G["ln3_b"] = _bwd_proj([(dgp, W["pe_gate_w"])], dx4, rscale=1.0,
                                                  ln=(sv["xh3"], sv["rs3"], W["ln3_g"]), name=nm("ln3_bwd"))
    G["ffn2_wd"] = _mm(sv["a2"], dr3b, ta=True, scale=0.5, out_dtype=BF16, name=nm("d_ffn2_wd"))
    dg2, du2 = _mm_swiglu_bwd(dr3b, W["ffn2_wd"], sv["g2"], sv["u2"], scale=0.5, name=nm("ffn2_act_bwd"))
    G["ffn2_wg"] = _mm(sv["x2b"], dg2, ta=True, out_dtype=BF16, chip_cols=True, name=nm("d_ffn2_wg"))
    G["ffn2_wu"] = _mm(sv["x2b"], du2, ta=True, out_dtype=BF16, chip_cols=True, name=nm("d_ffn2_wu"))
    dr2, dr2b, G["ln2_g"], G["ln2_b"] = _bwd_proj([(dg2, W["ffn2_wg"]), (du2, W["ffn2_wu"])], dr3, rscale=ALPHA,
                                                  ln=(sv["xh2"], sv["rs2"], W["ln2_g"]), name=nm("ln2_bwd"))
    G["w_out"] = _mm(sv["ymix"], dr2b, ta=True, out_dtype=BF16, name=nm("d_w_out"))
    dymix = _mm(dr2b, W["w_out"], tb=True, name=nm("d_ymix"))
    hbuf = sv["hbuf"]
    (dur, dgr, G["lru_conv_w"], G["lru_conv_b"], G["lru_wa_bd"], G["lru_ba"], G["lru_wx_bd"], G["lru_bx"],
     G["lru_lambda"]) = _lru_bwd(dymix, hbuf, sv["lu"], sv["lr"], sv["lig"], sv["la"], sv["lh"],
                                 W["lru_conv_w"], W["lru_wa_bd"], W["lru_wx_bd"], W["lru_lambda"], name=nm("lru_bwd"))
    delta = _fox_delta(dymix, sv["yb"], name=nm("fox_delta"))
    delta_rows = jnp.pad(delta[:, :ATT_HEADS].T, ((0, SUBLANES - ATT_HEADS), (0, 0)))
    comm = _merge_comms([comm, late(G) if late else None])
    (dk, dv, dfk, dqt, dfq), comm_out = _fox_bwd(hbuf, sv["eq"], sv["ek"], dymix, sv["lse_rows"], delta_rows,
                                                 comm=comm, name=nm("fox_bwd"))
    dq = dqt.T
    dfc = jnp.pad(dfq[:ATT_HEADS].T, ((0, 0), (0, LANES - ATT_HEADS))) - dfk
    dsm_f, G["fox_bf_vec"] = _fox_post(dfc, hbuf, W["fox_bf_vec"], name=nm("fox_post"))
    (dxr, dz, dsm_dt, G["ssd_norm_g"], G["ssd_d_exp"], G["ssd_a_vec"], G["ssd_dtb_vec"], G["ssd_conv_w"],
     G["ssd_conv_b"]) = _ssd_bwd(dymix, hbuf, sv["yssd"], sv["states"], W["ssd_conv_w"], W["ssd_conv_b"],
                                 W["ssd_dtb_vec"], W["ssd_a_vec"], W["ssd_d_exp"], W["ssd_norm_g"], name=nm("ssd_bwd"))
    t = dx4.shape[0]
    dh = jnp.concatenate([dxr.astype(BF16), dz.astype(BF16), dur.astype(BF16), dgr.astype(BF16), dq.astype(BF16),
                          dk.astype(BF16), dv.astype(BF16), (dsm_f + dsm_dt).astype(BF16),
                          jnp.zeros((t, H_WIDTH - COL_SMALL - LANES), BF16)], axis=1)
    G["w_in_p"] = _mm(sv["x1b"], dh, ta=True, name=nm("d_w_in"))
    dr1, dr1b, G["ln1_g"], G["ln1_b"] = _bwd_proj([(dh, W["w_in_p"])], dr2, rscale=ALPHA,
                                                  ln=(sv["xh1"], sv["rs1"], W["ln1_g"]), name=nm("ln1_bwd"))
    G["ffn1_wd"] = _mm(sv["a1"], dr1b, ta=True, scale=0.5, out_dtype=BF16, name=nm("d_ffn1_wd"))
    dg1, du1 = _mm_swiglu_bwd(dr1b, W["ffn1_wd"], sv["g1"], sv["u1"], scale=0.5, name=nm("ffn1_act_bwd"))
    G["ffn1_wg"] = _mm(sv["x_in_b"], dg1, ta=True, out_dtype=BF16, chip_cols=True, name=nm("d_ffn1_wg"))
    G["ffn1_wu"] = _mm(sv["x_in_b"], du1, ta=True, out_dtype=BF16, chip_cols=True, name=nm("d_ffn1_wu"))
    (dx_in,) = _bwd_proj([(dg1, W["ffn1_wg"]), (du1, W["ffn1_wu"])], dr1, rscale=ALPHA, ln=None, name=nm("x_in_bwd"))
    return dx_in, G, comm_out


def _block_diag(w):
    n, b, _ = w.shape
    eye = jnp.eye(n, dtype=w.dtype)
    return (eye[:, None, :, None] * w[:, :, None, :]).reshape(n * b, n * b)


def _block_diag_extract(m):
    n, b = LRU_HEADS, HEAD_DIM
    return jnp.stack([m[b * i:b * (i + 1), b * i:b * (i + 1)] for i in range(n)])


def _lane_vec(v, lane0):
    return jnp.pad(v.astype(F32), (lane0, LANES - lane0 - v.shape[0])).reshape(1, LANES)


def _w_in_permute(w):
    d = w.shape[0]
    z = lambda n: jnp.zeros((d, n), w.dtype)
    return jnp.concatenate([w[:, 1796:2820], w[:, 1284:1796], w[:, 0:512], w[:, 512:1280],
                            w[:, 1280:1284], w[:, 2820:2828], z(LANES - 12), z(H_WIDTH - COL_SMALL - LANES)], axis=1)


def _w_in_unpermute(wp):
    return jnp.concatenate([wp[:, COL_U:COL_Q], wp[:, COL_Q:COL_SMALL], wp[:, COL_SMALL:COL_SMALL + 4],
                            wp[:, COL_Z:COL_U], wp[:, COL_XBC:COL_Z], wp[:, COL_SMALL + 4:COL_SMALL + 12]], axis=1)


def _big_weights(chipw):
    W = {}
    for n, w in chipw.items():
        if n in ("ffn1_wg", "ffn1_wu", "ffn2_wg", "ffn2_wu"):
            W[n] = w
        elif n in ("ffn1_wd", "ffn2_wd", "w_out", "pe_gate_w"):
            W[n] = w.reshape(-1, D_MODEL)
        elif n == "pe_proj":
            W[n] = jnp.moveaxis(w, 0, 1).reshape(PLE_DIM, D_MODEL)
        else:
            w_in = jnp.moveaxis(w[:, :, :IN_WIDTH // N_CHIPS], 0, 1).reshape(D_MODEL, IN_WIDTH)
            W["w_in_p"] = _w_in_permute(w_in)
    return W


def _small_weights(li, small):
    g = lambda n: small[n][li]
    W = {n: g(n) for n in ("ln1_g", "ln1_b", "ln2_g", "ln2_b", "ln3_g", "ln3_b", "pe_gate_b", "lru_conv_w",
                           "ssd_conv_w")}
    for n in ("lru_conv_b", "lru_ba", "lru_bx", "lru_lambda", "ssd_conv_b", "ssd_norm_g"):
        W[n] = g(n).reshape(1, -1)
    W["lru_wa_bd"] = _block_diag(g("lru_wa")).astype(BF16)
    W["lru_wx_bd"] = _block_diag(g("lru_wx")).astype(BF16)
    W["fox_bf_vec"] = _lane_vec(g("fox_bf"), LANE_F)
    W["ssd_dtb_vec"] = _lane_vec(g("ssd_dt_bias"), LANE_DT)
    W["ssd_a_vec"] = _lane_vec(-jnp.exp(g("ssd_a_log")), LANE_DT)
    W["ssd_d_exp"] = jnp.repeat(g("ssd_d"), HEAD_DIM).reshape(1, SSD_WIDTH)
    return W


def _big_grad_by_chip(G, n):
    if n in ("ffn1_wg", "ffn1_wu", "ffn2_wg", "ffn2_wu", "pe_proj"):
        return G[n]
    if n in ("ffn1_wd", "ffn2_wd", "w_out", "pe_gate_w"):
        return G[n].reshape(N_CHIPS, -1, D_MODEL)
    share = IN_WIDTH // N_CHIPS
    d_w_in = jnp.moveaxis(_w_in_unpermute(G["w_in_p"]).reshape(D_MODEL, N_CHIPS, share), 1, 0)
    return jnp.pad(d_w_in.astype(BF16), ((0, 0), (0, 0), (0, SHARE - share)))


def _layer_small_grads(G, W):
    out = {n: G[n] for n in ("lru_conv_w", "ssd_conv_w")}
    for n in ("ln1_g", "ln1_b", "ln2_g", "ln2_b", "ln3_g", "ln3_b", "pe_gate_b", "lru_conv_b", "lru_ba", "lru_bx",
              "lru_lambda", "ssd_conv_b", "ssd_norm_g"):
        out[n] = G[n].reshape(-1)
    out["lru_wa"] = _block_diag_extract(G["lru_wa_bd"])
    out["lru_wx"] = _block_diag_extract(G["lru_wx_bd"])
    out["fox_bf"] = G["fox_bf_vec"][0, LANE_F:LANE_F + ATT_HEADS]
    out["ssd_dt_bias"] = G["ssd_dtb_vec"][0, LANE_DT:LANE_DT + SSD_HEADS]
    out["ssd_a_log"] = G["ssd_a_vec"][0, LANE_DT:LANE_DT + SSD_HEADS] * W["ssd_a_vec"][0, LANE_DT:LANE_DT + SSD_HEADS]
    out["ssd_d"] = G["ssd_d_exp"].reshape(SSD_HEADS, HEAD_DIM).sum(axis=1)
    return out


WEIGHTS = ['ln1_g', 'ln1_b', 'ffn1_wg', 'ffn1_wu', 'ffn1_wd', 'w_in', 'lru_conv_w', 'lru_conv_b', 'lru_wa', 'lru_ba',
           'lru_wx', 'lru_bx', 'lru_lambda', 'fox_bf', 'ssd_conv_w', 'ssd_conv_b', 'ssd_dt_bias', 'ssd_a_log', 'ssd_d',
           'ssd_norm_g', 'w_out', 'ln2_g', 'ln2_b', 'ffn2_wg', 'ffn2_wu', 'ffn2_wd', 'ln3_g', 'ln3_b', 'pe_proj',
           'pe_gate_w', 'pe_gate_b']
FIRST = ((("ffn1_wg", "ffn1_wu"), 1),)
NEXT = ((("w_in",), 1),
        (("ffn1_wd",), 0))
EARLY = FIRST + NEXT
LATE = ((("ffn2_wg", "ffn2_wu"), 1),
        (("ffn2_wd",), 0),
        (("w_out", "pe_gate_w"), None),
        (("pe_proj",), None))
BIG = {n: pad for names, pad in EARLY + LATE for n in names}
SMALL_SHARDED = {'lru_conv_w': 2, 'ssd_conv_w': 2}
PACK_COLS = 1024


def _unshard(seg, axis):
    moved = jnp.moveaxis(seg, 0, axis)
    shp = list(moved.shape)
    shp[axis:axis + 2] = [shp[axis] * shp[axis + 1]]
    return moved.reshape(shp)


def _pad_axis(a, axis, size):
    if axis is None or a.shape[axis] == size:
        return a
    pads = [(0, 0)] * a.ndim
    pads[axis] = (0, size - a.shape[axis])
    return jnp.pad(a, pads)


def _pack(arrs, dtype, cols):
    flat = jnp.concatenate([a.astype(dtype).reshape(-1) for a in arrs])
    pad = (-flat.shape[0]) % cols
    if pad:
        flat = jnp.concatenate([flat, jnp.zeros((pad,), dtype)])
    return flat.reshape(-1, cols)


def _unpack(flat, shapes):
    out, off = [], 0
    for s in shapes:
        n = math.prod(s)
        out.append(flat[off:off + n].reshape(s))
        off += n
    return out


def kernel(x, p, ln1_g, ln1_b, ffn1_wg, ffn1_wu, ffn1_wd, w_in, lru_conv_w, lru_conv_b, lru_wa, lru_ba, lru_wx, lru_bx, lru_lambda, fox_bf, ssd_conv_w, ssd_conv_b, ssd_dt_bias, ssd_a_log, ssd_d, ssd_norm_g, w_out, ln2_g, ln2_b, ffn2_wg, ffn2_wu, ffn2_wd, ln3_g, ln3_b, pe_proj, pe_gate_w, pe_gate_b, loss_target, m_ln1_g, m_ln1_b, m_ffn1_wg, m_ffn1_wu, m_ffn1_wd, m_w_in, m_lru_conv_w, m_lru_conv_b, m_lru_wa, m_lru_ba, m_lru_wx, m_lru_bx, m_lru_lambda, m_fox_bf, m_ssd_conv_w, m_ssd_conv_b, m_ssd_dt_bias, m_ssd_a_log, m_ssd_d, m_ssd_norm_g, m_w_out, m_ln2_g, m_ln2_b, m_ffn2_wg, m_ffn2_wu, m_ffn2_wd, m_ln3_g, m_ln3_b, m_pe_proj, m_pe_gate_w, m_pe_gate_b, v_ln1_g, v_ln1_b, v_ffn1_wg, v_ffn1_wu, v_ffn1_wd, v_w_in, v_lru_conv_w, v_lru_conv_b, v_lru_wa, v_lru_ba, v_lru_wx, v_lru_bx, v_lru_lambda, v_fox_bf, v_ssd_conv_w, v_ssd_conv_b, v_ssd_dt_bias, v_ssd_a_log, v_ssd_d, v_ssd_norm_g, v_w_out, v_ln2_g, v_ln2_b, v_ffn2_wg, v_ffn2_wu, v_ffn2_wd, v_ln3_g, v_ln3_b, v_pe_proj, v_pe_gate_w, v_pe_gate_b):
    args = locals()
    w_loc = {n: args[n] for n in WEIGHTS}
    m_loc = {n: args["m_" + n] for n in WEIGHTS}
    v_loc = {n: args["v_" + n] for n in WEIGHTS}
    chip = 2 * lax.axis_index("x") + lax.axis_index("y")
    core = lax.axis_index("c")
    big = list(BIG)
    small_sh = list(SMALL_SHARDED)
    small_rep = [n for n in WEIGHTS if n not in BIG and n not in SMALL_SHARDED]

    def srcs_of(li, groups):
        return [jnp.stack([_pad_axis(w_loc[n][li].astype(BF16), pad, SHARE) for n in names]) for names, pad in groups]

    def gather_comm(li, groups, base=0):
        return _gather_layer_comm(srcs_of(li, groups), li, base)

    def chip_weights(gathered, groups):
        return _big_weights({n: g[:, j] for (names, _), g in zip(groups, gathered) for j, n in enumerate(names)})

    def pair_sums(G, groups, tag):
        gs = [jnp.stack([_big_grad_by_chip(G, n) for n in names]) for names, _ in groups]
        flat = [g.reshape((-1,) + g.shape[2:]) for g in gs]
        theirs = _exchange(flat, ("c",), swap=True, name=f"reduce_cores_{tag}")
        return [_sum_pair(f, r, BF16, name=f"reduce_cores_sum_{tag}_{gi}").reshape(g.shape)
                for gi, (f, r, g) in enumerate(zip(flat, theirs, gs))]

    def finish_reduce(quad, sums, li, groups, tag):
        quad = [lax.dynamic_update_index_in_dim(q, lax.dynamic_index_in_dim(s, chip, 1, keepdims=False), chip, 0)
                for q, s in zip(quad, sums)]
        red = [_sum_slots(q.reshape(N_CHIPS, -1, q.shape[-1]), F32,
                          name=f"reduce_chips_sum_{tag}_{gi}").reshape(q.shape[1:]) for gi, q in enumerate(quad)]
        theirs = _exchange(red, ("c",), swap=True, name=f"reduce_share_{tag}")
        out = {}
        for (names, _), r, rv in zip(groups, red, theirs):
            both = jnp.where(core == li, r, rv)
            for j, n in enumerate(names):
                out[n] = both[j]
        return out

    everything = EARLY + LATE
    first0 = _run_comm(gather_comm(0, FIRST), name="gather_w_l0")
    small = {n: w_loc[n] for n in small_rep}
    spack = _pack([w_loc[n] for n in small_sh], F32, LANES)
    (sg,) = _exchange([spack[None]], ("x", "y"), name="gather_conv_w")
    for n, seg in zip(small_sh, _unpack_rows(sg.reshape(N_CHIPS, -1), [w_loc[n].shape for n in small_sh])):
        small[n] = _unshard(seg, SMALL_SHARDED[n])

    W0 = {**_small_weights(0, small), **chip_weights(first0, FIRST)}
    late0_comm = gather_comm(0, LATE)
    early1 = []

    def in_attention0(got):
        early1.extend(got[len(LATE):])
        return chip_weights(got[:len(LATE)], LATE)

    xs = x[0]
    xs, xb, sv0, W0 = _layer_fwd(
        0, xs, xs.astype(BF16), p[0, 0].astype(BF16), W0,
        up=(gather_comm(0, NEXT), lambda got: chip_weights(got, NEXT)),
        att=(_merge_comms([late0_comm, gather_comm(1, EARLY, base=late0_comm.n_sems)]), in_attention0))
    W1 = {**_small_weights(1, small), **chip_weights(early1, EARLY)}
    xs, _, sv1, W1 = _layer_fwd(1, xs, xb, p[1, 0].astype(BF16), W1,
                                att=(gather_comm(1, LATE), lambda got: chip_weights(got, LATE)))
    dx, loss = _loss_kernel(xs, loss_target[0], name="loss")
    loss = lax.psum(loss[0, 0], MESH_AXES)
    dx, G1, _ = _layer_bwd(1, dx, sv1, W1)
    sums1 = pair_sums(G1, everything, "l1")
    comm1 = _reduce_chips_comm(sums1, 1)
    late_sums = []

    def late0(G):
        late_sums.extend(pair_sums(G, LATE, "l0_late"))
        return _reduce_chips_comm(late_sums, 0, base=comm1.n_sems)

    grad_x, G0, quads = _layer_bwd(0, dx, sv0, W0, comm=comm1, late=late0)

    n1 = len(comm1.out_shapes)
    red = [{**finish_reduce(quads[n1:], late_sums, 0, LATE, "l0_late")},
           finish_reduce(quads[:n1], sums1, 1, everything, "l1")]
    sums0 = pair_sums(G0, EARLY, "l0")
    red[0].update(finish_reduce(_run_comm(_reduce_chips_comm(sums0, 0), name="reduce_chips_l0"), sums0, 0, EARLY, "l0"))
    g_red = {}
    for n in big:
        g = jnp.stack([red[li][n] for li in range(DEPTH)])
        g_red[n] = g[tuple(slice(0, s) for s in w_loc[n].shape)]
    small_l = [_layer_small_grads(G0, W0), _layer_small_grads(G1, W1)]
    g_small = {n: jnp.stack([small_l[li][n] for li in range(DEPTH)]) for n in small_l[0]}
    small_all = small_rep + small_sh
    sgp = _pack([g_small[n] for n in small_all], F32, PACK_COLS)
    (sall,) = _exchange([sgp[None]], MESH_AXES, name="reduce_small")
    sred = _sum_slots(sall.reshape((2 ** len(MESH_AXES),) + sgp.shape), F32, name="reduce_small_sum").reshape(-1)
    for n, g in zip(small_all, _unpack(sred, [g_small[n].shape for n in small_all])):
        if n in SMALL_SHARDED:
            width = w_loc[n].shape[-1]
            g = lax.dynamic_slice_in_dim(g, chip * width, width, axis=SMALL_SHARDED[n])
        g_red[n] = g

    delta, new_m, new_v = {}, {}, {}
    for n in big:
        delta[n], new_m[n], new_v[n] = _adamw(w_loc[n], g_red[n], m_loc[n], v_loc[n], name="adamw_" + n)
    shapes = [w_loc[n].shape for n in small_all]
    packs = [_pack([d[n] for n in small_all], F32, LANES) for d in (w_loc, g_red, m_loc, v_loc)]
    outs = _adamw(*packs, name="adamw_small")
    for d, o in zip((delta, new_m, new_v), outs):
        for n, a in zip(small_all, _unpack(o.reshape(-1), shapes)):
            d[n] = a
    return (loss, grad_x[None], *[g_red[n] for n in WEIGHTS], *[delta[n] for n in WEIGHTS],
            *[new_m[n] for n in WEIGHTS], *[new_v[n] for n in WEIGHTS])


def _unpack_rows(gathered, shapes):
    out, off = [], 0
    for s in shapes:
        n = math.prod(s)
        out.append(gathered[:, off:off + n].reshape((N_CHIPS,) + tuple(s)))
        off += n
    return out
```

```python
import functools
import math

import jax
import jax.numpy as jnp
from jax import lax
from jax.experimental import pallas as pl
from jax.experimental.pallas import tpu as pltpu

F32 = jnp.float32
BF16 = jnp.bfloat16

D_MODEL = 1024
DEPTH = 2
PLE_DIM = 256
HEAD_DIM = 64
LRU_WIDTH = 256
LRU_HEADS = 4
LRU_C = 8.0
CONV_K = 4
ATT_WIDTH = 256
ATT_HEADS = 4
SSD_WIDTH = 512
SSD_HEADS = 8
SSD_GROUPS = 2
SSD_STATE = 128
SSD_CHUNK = 128
SSD_CONV_DIM = 1024
FFN_DIM = 2816
ALPHA = (2.0 * DEPTH) ** 0.25
LN_EPS = 1e-5
RMS_EPS = 1e-5
IN_WIDTH = 2828
ADAM_LR = 0.001
ADAM_B1 = 0.9
ADAM_B2 = 0.999
ADAM_EPS = 1e-08
ADAM_WD = 0.01
ADAM_STEP = 10

H_WIDTH = 3072
COL_XBC, COL_Z, COL_U, COL_G, COL_Q, COL_K, COL_V, COL_SMALL = 0, 1024, 1536, 1792, 2048, 2304, 2560, 2816
LANE_F = 0
LANE_DT = 4
LANES = 128
SUBLANES = 8
NEG = -1e30

VMEM_LIMIT = 48 * 1024 * 1024

N_CHIPS = 4
MESH_AXES = ("x", "y", "c")
SHARE = 768


def _params(n):
    return pltpu.CompilerParams(dimension_semantics=("arbitrary",) * n, vmem_limit_bytes=VMEM_LIMIT)


def _pick(n, cands):
    for c in cands:
        if n % c == 0:
            return c
    return n


def _iota(shape, dim):
    return lax.broadcasted_iota(jnp.int32, shape, dim)


def _shift_down(x, s, prev8):
    if s == 0:
        return x
    r = pltpu.roll(x, s, 0)
    pr = pltpu.roll(prev8, s, 0)
    head = jnp.where(_iota(pr.shape, 0) < s, pr, r[:SUBLANES])
    return jnp.concatenate([head, r[SUBLANES:]], axis=0)


def _shift_up(x, s, next8):
    if s == 0:
        return x
    n = x.shape[0]
    r = pltpu.roll(x, n - s, 0)
    nr = pltpu.roll(next8, SUBLANES - s, 0)
    tail = jnp.where(_iota(nr.shape, 0) >= SUBLANES - s, nr, r[n - SUBLANES:])
    return jnp.concatenate([r[:n - SUBLANES], tail], axis=0)


def _scan_fwd(a, b):
    n = a.shape[0]
    row = _iota(a.shape, 0)
    d = 1
    while d < n:
        keep = row >= d
        a_s = jnp.where(keep, pltpu.roll(a, d, 0), 1.0)
        b_s = jnp.where(keep, pltpu.roll(b, d, 0), 0.0)
        b = a * b_s + b
        a = a * a_s
        d *= 2
    return a, b


def _scan_bwd(a, b):
    n = a.shape[0]
    row = _iota(a.shape, 0)
    d = 1
    while d < n:
        keep = row < n - d
        a_s = jnp.where(keep, pltpu.roll(a, n - d, 0), 1.0)
        b_s = jnp.where(keep, pltpu.roll(b, n - d, 0), 0.0)
        b = a * b_s + b
        a = a * a_s
        d *= 2
    return a, b


def _cumsum_rows(x, reverse=False):
    n = x.shape[0]
    row = _iota(x.shape, 0)
    d = 1
    while d < n:
        if reverse:
            x = x + jnp.where(row < n - d, pltpu.roll(x, n - d, 0), 0.0)
        else:
            x = x + jnp.where(row >= d, pltpu.roll(x, d, 0), 0.0)
        d *= 2
    return x


def _col(x, lane):
    return jnp.sum(jnp.where(_iota(x.shape, 1) == lane, x, 0.0), axis=1, keepdims=True)


def _row(x, r):
    return jnp.sum(jnp.where(_iota(x.shape, 0) == r, x, 0.0), axis=0, keepdims=True)


def _sigmoid(x):
    return jax.nn.sigmoid(x)


def _softplus(x):
    return jnp.maximum(x, 0.0) + jnp.log(1.0 + jnp.exp(-jnp.abs(x)))


def _gelu_and_grad(x):
    c0 = math.sqrt(2.0 / math.pi)
    inner = c0 * (x + 0.044715 * x * x * x)
    t = jnp.tanh(inner)
    g = 0.5 * x * (1.0 + t)
    dg = 0.5 * (1.0 + t) + 0.5 * x * (1.0 - t * t) * c0 * (1.0 + 3.0 * 0.044715 * x * x)
    return g, dg


def _dot(a, b, ca, cb):
    return lax.dot_general(a, b, (((ca,), (cb,)), ((), ())), preferred_element_type=F32)


def _conv_taps(xr, prev8, w, bias):
    y = bias + w[CONV_K - 1:CONV_K, :] * xr
    for j in range(CONV_K - 1):
        y = y + w[j:j + 1, :] * _shift_down(xr, CONV_K - 1 - j, prev8)
    return y


def _conv_taps_bwd(dy, next8, w, xr):
    dx = None
    dws = []
    for j in range(CONV_K):
        sh = _shift_up(dy, CONV_K - 1 - j, next8)
        term = w[j:j + 1, :] * sh
        dx = term if dx is None else dx + term
        dws.append(jnp.sum(sh * xr, axis=0, keepdims=True))
    return dx, jnp.concatenate(dws, axis=0)


def _head_expand(v, lane0, nheads, width):
    rows = v.shape[0]
    colhead = _iota((rows, width), 1) // HEAD_DIM
    out = jnp.zeros((rows, width), F32)
    for h in range(nheads):
        out = jnp.where(colhead == h, _col(v, lane0 + h), out)
    return out


def _head_reduce(x, lane0, nheads):
    rows = x.shape[0]
    colhead = _iota(x.shape, 1) // HEAD_DIM
    lane = _iota((rows, LANES), 1)
    out = jnp.zeros((rows, LANES), F32)
    for h in range(nheads):
        s = jnp.sum(jnp.where(colhead == h, x, 0.0), axis=1, keepdims=True)
        out = jnp.where(lane == lane0 + h, s, out)
    return out


def _mm(a, b, *, ta=False, tb=False, scale=1.0, out_dtype=F32, chip_cols=False, name):
    if ta:
        kk, m = a.shape
    else:
        m, kk = a.shape
    n = b.shape[0] if tb else b.shape[1]
    tm = _pick(m, (1024, 512, 256, 128))
    tk = _pick(kk, (1024, 768, 512, 256, 128))
    nk = kk // tk
    dn_a = 0 if ta else 1
    dn_b = 1 if tb else 0
    share = n // N_CHIPS
    if chip_cols:
        tn = n
        out_spec = pl.BlockSpec((N_CHIPS, tm, share), lambda i, j, k: (0, i, 0))
        out_shape = jax.ShapeDtypeStruct((N_CHIPS, m, share), out_dtype)
    else:
        tn = _pick(n, (1024, 768, 512, 256, 128))
        out_spec = pl.BlockSpec((tm, tn), lambda i, j, k: (i, j))
        out_shape = jax.ShapeDtypeStruct((m, n), out_dtype)

    def body(a_ref, b_ref, o_ref, acc):
        k = pl.program_id(2)

        @pl.when(k == 0)
        def _():
            acc[...] = jnp.zeros_like(acc)

        acc[...] += _dot(a_ref[...].astype(BF16), b_ref[...].astype(BF16), dn_a, dn_b)

        @pl.when(k == nk - 1)
        def _():
            if chip_cols:
                for c in range(N_CHIPS):
                    o_ref[c] = (acc[:, share * c:share * (c + 1)] * scale).astype(out_dtype)
            else:
                o_ref[...] = (acc[...] * scale).astype(out_dtype)

    a_spec = pl.BlockSpec((tk, tm), lambda i, j, k: (k, i)) if ta else pl.BlockSpec((tm, tk), lambda i, j, k: (i, k))
    b_spec = pl.BlockSpec((tn, tk), lambda i, j, k: (j, k)) if tb else pl.BlockSpec((tk, tn), lambda i, j, k: (k, j))
    return pl.pallas_call(
        body, name=name, grid=(m // tm, n // tn, nk),
        in_specs=[a_spec, b_spec],
        out_specs=out_spec, out_shape=out_shape,
        scratch_shapes=[pltpu.VMEM((tm, tn), F32)],
        compiler_params=_params(3),
    )(a, b)


def _mm_swiglu(xb, wg, wu, *, comm=None, name):
    t, d = xb.shape
    share = wg.shape[2]
    n = N_CHIPS * share
    tm = _pick(t, (512, 256, 128))
    tn = _pick(share, (768, 256, 128))
    per = share // tn

    def body(x_ref, wg_ref, wu_ref, g_ref, u_ref, a_ref):
        x = x_ref[...]
        g = _dot(x, wg_ref[...], 1, 0)
        u = _dot(x, wu_ref[...], 1, 0)
        g_ref[...] = g.astype(BF16)
        u_ref[...] = u.astype(BF16)
        a_ref[...] = (g * _sigmoid(g) * u).astype(BF16)

    o = jax.ShapeDtypeStruct((t, n), BF16)
    ospec = pl.BlockSpec((tm, tn), lambda j, i: (i, j))
    return _hosted_call(
        body, comm, (n // tn, t // tm), name=name,
        in_specs=[pl.BlockSpec((tm, d), lambda j, i: (i, 0)),
                  pl.BlockSpec((None, d, tn), lambda j, i: (j // per, 0, j % per)),
                  pl.BlockSpec((None, d, tn), lambda j, i: (j // per, 0, j % per))],
        out_specs=[ospec, ospec, ospec], out_shape=[o, o, o], scratch_shapes=[], args=[xb, wg, wu])


def _mm_swiglu_bwd(dr, wd, g, u, *, scale, name):
    t, d = dr.shape
    n = wd.shape[0]
    tm = _pick(t, (512, 256, 128))
    tn = _pick(n, (768, 256, 128))

    def body(dr_ref, wd_ref, g_ref, u_ref, dg_ref, du_ref):
        da = _dot(dr_ref[...].astype(BF16), wd_ref[...], 1, 1) * scale
        gg = g_ref[...].astype(F32)
        uu = u_ref[...].astype(F32)
        sg = _sigmoid(gg)
        dg_ref[...] = (da * uu * (sg * (1.0 + gg * (1.0 - sg)))).astype(BF16)
        du_ref[...] = (da * gg * sg).astype(BF16)

    o = jax.ShapeDtypeStruct((t, n), BF16)
    ospec = pl.BlockSpec((tm, tn), lambda j, i: (i, j))
    return pl.pallas_call(
        body, name=name, grid=(n // tn, t // tm),
        in_specs=[pl.BlockSpec((tm, d), lambda j, i: (i, 0)),
                  pl.BlockSpec((tn, d), lambda j, i: (j, 0)),
                  ospec, ospec],
        out_specs=[ospec, ospec], out_shape=[o, o],
        compiler_params=_params(2),
    )(dr, wd, g, u)


def _mm_ln(a, w, resid, gain, bias, *, rscale, mscale, name):
    t, kk = a.shape
    d = w.shape[1]
    tm = _pick(t, (512, 256, 128))
    tk = kk
    nk = kk // tk

    def body(a_ref, w_ref, r_ref, g_ref, b_ref, y_ref, yb_ref, xh_ref, rs_ref, acc):
        k = pl.program_id(1)

        @pl.when(k == 0)
        def _():
            acc[...] = jnp.zeros_like(acc)

        acc[...] += _dot(a_ref[...].astype(BF16), w_ref[...], 1, 0)

        @pl.when(k == nk - 1)
        def _():
            r = rscale * r_ref[...] + mscale * acc[...]
            mu = jnp.mean(r, axis=1, keepdims=True)
            xc = r - mu
            var = jnp.mean(xc * xc, axis=1, keepdims=True)
            rstd = lax.rsqrt(var + LN_EPS)
            xh = xc * rstd
            y = xh * g_ref[...] + b_ref[...]
            y_ref[...] = y
            yb_ref[...] = y.astype(BF16)
            xh_ref[...] = xh
            rs_ref[...] = rstd

    row = pl.BlockSpec((tm, d), lambda i, k: (i, 0))
    vec = pl.BlockSpec((1, d), lambda i, k: (0, 0))
    return pl.pallas_call(
        body, name=name, grid=(t // tm, nk),
        in_specs=[pl.BlockSpec((tm, tk), lambda i, k: (i, k)),
                  pl.BlockSpec((tk, d), lambda i, k: (k, 0)), row, vec, vec],
        out_specs=[row, row, row, pl.BlockSpec((tm, 1), lambda i, k: (i, 0))],
        out_shape=[jax.ShapeDtypeStruct((t, d), F32), jax.ShapeDtypeStruct((t, d), BF16),
                   jax.ShapeDtypeStruct((t, d), F32), jax.ShapeDtypeStruct((t, 1), F32)],
        scratch_shapes=[pltpu.VMEM((tm, d), F32)],
        compiler_params=_params(2),
    )(a, w, resid, gain.reshape(1, d), bias.reshape(1, d))


def _bwd_proj(pairs, resid, *, rscale, ln, comm=None, name):
    t, kk = pairs[0][0].shape
    d = pairs[0][1].shape[-2]
    has_ln = ln is not None
    tm = _pick(t, (512, 256, 128) if has_ln else (1024, 512, 256, 128))
    tk = _pick(pairs[0][1].shape[-1], (1024, 768, 512, 256, 128))
    nk = kk // tk
    nt = t // tm
    npair = len(pairs)

    def body(*refs):
        ab = refs[:2 * npair]
        r_ref = refs[2 * npair]
        pos = 2 * npair + 1
        if has_ln:
            xh_ref, rs_ref, g_ref = refs[pos:pos + 3]
            pos += 3
            o_ref, ob_ref, dg_ref, db_ref = refs[pos:pos + 4]
            pos += 4
        else:
            o_ref = refs[pos]
            pos += 1
        acc = refs[pos]
        i = pl.program_id(0)
        k = pl.program_id(1)

        @pl.when(k == 0)
        def _():
            acc[...] = jnp.zeros_like(acc)

        for q in range(npair):
            acc[...] += _dot(ab[2 * q][...].astype(BF16), ab[2 * q + 1][...], 1, 1)

        @pl.when(k == nk - 1)
        def _():
            dy = rscale * r_ref[...] + acc[...]
            if not has_ln:
                o_ref[...] = dy
                return
            xh = xh_ref[...]
            w = dy * g_ref[...]
            m1 = jnp.mean(w, axis=1, keepdims=True)
            m2 = jnp.mean(w * xh, axis=1, keepdims=True)
            dr = rs_ref[...] * (w - m1 - xh * m2)
            o_ref[...] = dr
            ob_ref[...] = dr.astype(BF16)

            @pl.when(i == 0)
            def _():
                dg_ref[...] = jnp.zeros_like(dg_ref)
                db_ref[...] = jnp.zeros_like(db_ref)

            dg_ref[...] += jnp.sum(dy * xh, axis=0, keepdims=True)
            db_ref[...] += jnp.sum(dy, axis=0, keepdims=True)

    row = pl.BlockSpec((tm, d), lambda i, k: (i, 0))
    vec = pl.BlockSpec((1, d), lambda i, k: (0, 0))
    in_specs, args = [], []
    for a, b in pairs:
        if b.ndim == 3:
            per = b.shape[2] // tk
            b_spec = pl.BlockSpec((None, d, tk), lambda i, k, per=per: (k // per, 0, k % per))
        else:
            b_spec = pl.BlockSpec((d, tk), lambda i, k: (0, k))
        in_specs += [pl.BlockSpec((tm, tk), lambda i, k: (i, k)), b_spec]
        args += [a, b]
    in_specs.append(row)
    args.append(resid)
    out_specs = [row]
    out_shape = [jax.ShapeDtypeStruct((t, d), F32)]
    if has_ln:
        xh, rs, gain = ln
        in_specs += [row, pl.BlockSpec((tm, 1), lambda i, k: (i, 0)), vec]
        args += [xh, rs, gain.reshape(1, d)]
        out_specs += [row, vec, vec]
        out_shape += [jax.ShapeDtypeStruct((t, d), BF16)] + [jax.ShapeDtypeStruct((1, d), F32)] * 2
    outs, got = _hosted_call(body, comm, (nt, nk), name=name, in_specs=in_specs, out_specs=out_specs,
                             out_shape=out_shape, scratch_shapes=[pltpu.VMEM((tm, d), F32)], args=args)
    return tuple(outs) if comm is None else tuple(outs) + (got,)


def _mm_pe(x3, x3b, pb, wgate, bgate, wproj, *, name):
    t, d = x3.shape
    pd = pb.shape[1]
    tm = _pick(t, (512, 256, 128))
    tn = _pick(d, (512, 256, 128))

    def body(x_ref, xb_ref, p_ref, wg_ref, bg_ref, wp_ref, y_ref, yb_ref, sg_ref, e_ref):
        sg = _sigmoid(_dot(xb_ref[...], wg_ref[...], 1, 0) + bg_ref[...])
        e = _dot(p_ref[...], wp_ref[...], 1, 0)
        y = x_ref[...] + sg * e
        y_ref[...] = y
        yb_ref[...] = y.astype(BF16)
        sg_ref[...] = sg.astype(BF16)
        e_ref[...] = e.astype(BF16)

    ospec = pl.BlockSpec((tm, tn), lambda i, j: (i, j))
    ob = jax.ShapeDtypeStruct((t, d), BF16)
    return pl.pallas_call(
        body, name=name, grid=(t // tm, d // tn),
        in_specs=[ospec, pl.BlockSpec((tm, d), lambda i, j: (i, 0)), pl.BlockSpec((tm, pd), lambda i, j: (i, 0)),
                  pl.BlockSpec((d, tn), lambda i, j: (0, j)), pl.BlockSpec((1, tn), lambda i, j: (0, j)),
                  pl.BlockSpec((pd, tn), lambda i, j: (0, j))],
        out_specs=[ospec, ospec, ospec, ospec],
        out_shape=[jax.ShapeDtypeStruct((t, d), F32), ob, ob, ob],
        compiler_params=_params(2),
    )(x3, x3b, pb, wgate, bgate.reshape(1, d), wproj)


def _pe_bwd_elem(dx4, sg, e, *, name):
    t, d = dx4.shape
    tm = _pick(t, (512, 256, 128))

    def body(dx_ref, sg_ref, e_ref, dgp_ref, de_ref, db_ref):
        dx = dx_ref[...]
        s = sg_ref[...].astype(F32)
        dgp = dx * e_ref[...].astype(F32) * s * (1.0 - s)
        dgp_ref[...] = dgp.astype(BF16)
        de_ref[...] = (dx * s).astype(BF16)

        @pl.when(pl.program_id(0) == 0)
        def _():
            db_ref[...] = jnp.zeros_like(db_ref)

        db_ref[...] += jnp.sum(dgp, axis=0, keepdims=True)

    row = pl.BlockSpec((tm, d), lambda i: (i, 0))
    ob = jax.ShapeDtypeStruct((t, d), BF16)
    return pl.pallas_call(
        body, name=name, grid=(t // tm,), in_specs=[row, row, row],
        out_specs=[row, row, pl.BlockSpec((1, d), lambda i: (0, 0))],
        out_shape=[ob, ob, jax.ShapeDtypeStruct((1, d), F32)],
        compiler_params=_params(1),
    )(dx4, sg, e)


def _loss_kernel(y, target, *, name):
    t, d = y.shape
    tm = _pick(t, (512, 256, 128))

    def body(y_ref, t_ref, dy_ref, l_ref):
        diff = y_ref[...] - t_ref[...]
        dy_ref[...] = diff * (1.0 / d)

        @pl.when(pl.program_id(0) == 0)
        def _():
            l_ref[...] = jnp.zeros_like(l_ref)

        part = jnp.sum(jnp.mean(diff * diff, axis=1, keepdims=True), axis=0, keepdims=True)
        l_ref[...] += 0.5 * part

    row = pl.BlockSpec((tm, d), lambda i: (i, 0))
    return pl.pallas_call(
        body, name=name, grid=(t // tm,), in_specs=[row, row],
        out_specs=[row, pl.BlockSpec((1, 1), lambda i: (0, 0))],
        out_shape=[jax.ShapeDtypeStruct((t, d), F32), jax.ShapeDtypeStruct((1, 1), F32)],
        compiler_params=_params(1),
    )(y, target)


LRU_TM = 256


def _lru_gate_terms(r, lam):
    sp = _softplus(-lam)
    la = -LRU_C * r * sp
    a = jnp.exp(la)
    em = jnp.tanh(la) * (jnp.exp(2.0 * la) + 1.0)
    s = jnp.sqrt(-em)
    return la, a, s, sp


def _lru_fwd(hbuf, conv_w, conv_b, wa, ba, wx, bx, lam, *, name):
    t = hbuf.shape[0]
    w = LRU_WIDTH
    tm = _pick(t, (LRU_TM, 128))
    cu, cg = COL_U // w, COL_G // w
    hb = tm // SUBLANES

    def body(u_ref, up_ref, g_ref, cw_ref, cb_ref, wa_ref, ba_ref, wx_ref, bx_ref, lam_ref,
             y_ref, u_out, r_out, i_out, a_out, h_out, carry):
        i = pl.program_id(0)

        @pl.when(i == 0)
        def _():
            carry[...] = jnp.zeros_like(carry)

        prev = jnp.where(i == 0, 0.0, up_ref[...])
        u = _conv_taps(u_ref[...], prev, cw_ref[...], cb_ref[...])
        ub = u.astype(BF16)
        r = _sigmoid(_dot(ub, wa_ref[...], 1, 0) + ba_ref[...])
        ig = _sigmoid(_dot(ub, wx_ref[...], 1, 0) + bx_ref[...])
        _, a, s, _ = _lru_gate_terms(r, lam_ref[...])
        b = s * (ig * u)
        acum, hs = _scan_fwd(a, b)
        h = hs + acum * carry[0:1, :]
        carry[...] = jnp.broadcast_to(h[tm - 1:tm, :], carry.shape)
        gl, _ = _gelu_and_grad(g_ref[...])
        y_ref[...] = h * gl
        u_out[...] = u
        r_out[...] = r
        i_out[...] = ig
        a_out[...] = a
        h_out[...] = h

    row = pl.BlockSpec((tm, w), lambda i: (i, 0))
    vec = pl.BlockSpec((1, w), lambda i: (0, 0))
    mat = pl.BlockSpec((w, w), lambda i: (0, 0))
    o = jax.ShapeDtypeStruct((t, w), F32)
    return pl.pallas_call(
        body, name=name, grid=(t // tm,),
        in_specs=[pl.BlockSpec((tm, w), lambda i: (i, cu)),
                  pl.BlockSpec((SUBLANES, w), lambda i: (jnp.maximum(i * hb - 1, 0), cu)),
                  pl.BlockSpec((tm, w), lambda i: (i, cg)),
                  pl.BlockSpec((CONV_K, w), lambda i: (0, 0)), vec, mat, vec, mat, vec, vec],
        out_specs=[row] * 6, out_shape=[o] * 6,
        scratch_shapes=[pltpu.VMEM((SUBLANES, w), F32)],
        compiler_params=_params(1),
    )(hbuf, hbuf, hbuf, conv_w, conv_b, wa, ba, wx, bx, lam)


def _lru_bwd(dymix, hbuf, u, r, ig, a, h, conv_w, wa, wx, lam, *, name):
    t = hbuf.shape[0]
    w = LRU_WIDTH
    tm = _pick(t, (LRU_TM, 128))
    nb = t // tm
    cu, cg = COL_U // w, COL_G // w
    hb = tm // SUBLANES
    last8 = t // SUBLANES - 1

    def body(dy_ref, ur_ref, g_ref, u_ref, r_ref, i_ref, a_ref, an_ref, h_ref, hp_ref,
             cw_ref, wa_ref, wx_ref, lam_ref,
             dur_ref, dgr_ref, dcw_ref, dcb_ref, dwa_ref, dba_ref, dwx_ref, dbx_ref, dlam_ref,
             lcarry, dnext):
        i = pl.program_id(0)
        ib = nb - 1 - i

        @pl.when(i == 0)
        def _():
            lcarry[...] = jnp.zeros_like(lcarry)
            dnext[...] = jnp.zeros_like(dnext)
            for ref in (dcw_ref, dcb_ref, dwa_ref, dba_ref, dwx_ref, dbx_ref, dlam_ref):
                ref[...] = jnp.zeros_like(ref)

        dy = dy_ref[...]
        hh = h_ref[...]
        av = a_ref[...]
        uu = u_ref[...]
        rr = r_ref[...]
        ii = i_ref[...]
        lam_v = lam_ref[...]
        gl, dgl = _gelu_and_grad(g_ref[...])
        dgr_ref[...] = (dy * hh * dgl).astype(BF16)
        dh_out = dy * gl
        a_next = _shift_up(av, 1, jnp.where(ib == nb - 1, 0.0, an_ref[...]))
        acum, ls = _scan_bwd(a_next, dh_out)
        lam_adj = ls + acum * lcarry[0:1, :]
        lcarry[...] = jnp.broadcast_to(lam_adj[0:1, :], lcarry.shape)
        h_prev = _shift_down(hh, 1, jnp.where(ib == 0, 0.0, hp_ref[...]))
        da = lam_adj * h_prev
        _, a2, s, sp = _lru_gate_terms(rr, lam_v)
        d_igu = lam_adj * s
        ds = lam_adj * ii * uu
        dla = da * a2 - ds * (a2 * a2) / s
        dr = dla * (-LRU_C * sp)
        dlam_ref[...] += jnp.sum(dla * (LRU_C * rr * _sigmoid(-lam_v)), axis=0, keepdims=True)
        dpre_r = dr * rr * (1.0 - rr)
        dpre_i = d_igu * uu * ii * (1.0 - ii)
        prb = dpre_r.astype(BF16)
        pib = dpre_i.astype(BF16)
        ub = uu.astype(BF16)
        du = d_igu * ii + _dot(prb, wa_ref[...], 1, 1) + _dot(pib, wx_ref[...], 1, 1)
        dwa_ref[...] += _dot(ub, prb, 0, 0)
        dwx_ref[...] += _dot(ub, pib, 0, 0)
        dba_ref[...] += jnp.sum(dpre_r, axis=0, keepdims=True)
        dbx_ref[...] += jnp.sum(dpre_i, axis=0, keepdims=True)
        dur, dws = _conv_taps_bwd(du, dnext[...], cw_ref[...], ur_ref[...])
        dur_ref[...] = dur.astype(BF16)
        dcw_ref[...] += dws
        dcb_ref[...] += jnp.sum(du, axis=0, keepdims=True)
        dnext[...] = du[:SUBLANES]

    def rowspec(col):
        return pl.BlockSpec((tm, w), lambda i: (nb - 1 - i, col))

    row = rowspec(0)
    nxt = pl.BlockSpec((SUBLANES, w), lambda i: (jnp.minimum((nb - i) * hb, last8), 0))
    prv = pl.BlockSpec((SUBLANES, w), lambda i: (jnp.maximum((nb - 1 - i) * hb - 1, 0), 0))
    vec = pl.BlockSpec((1, w), lambda i: (0, 0))
    mat = pl.BlockSpec((w, w), lambda i: (0, 0))
    cw = pl.BlockSpec((CONV_K, w), lambda i: (0, 0))
    o = jax.ShapeDtypeStruct((t, w), BF16)
    v1 = jax.ShapeDtypeStruct((1, w), F32)
    m1 = jax.ShapeDtypeStruct((w, w), F32)
    return pl.pallas_call(
        body, name=name, grid=(nb,),
        in_specs=[rowspec(0), rowspec(cu), rowspec(cg), row, row, row, row, nxt, row, prv, cw, mat, mat, vec],
        out_specs=[row, row, cw, vec, mat, vec, mat, vec, vec],
        out_shape=[o, o, jax.ShapeDtypeStruct((CONV_K, w), F32), v1, m1, v1, m1, v1, v1],
        scratch_shapes=[pltpu.VMEM((SUBLANES, w), F32), pltpu.VMEM((SUBLANES, w), F32)],
        compiler_params=_params(1),
    )(dymix, hbuf, hbuf, u, r, ig, a, a, h, h, conv_w, wa, wx, lam)


FOX_T = 512
FOX_PREP_TM = 256


def _log_sigmoid(x):
    return jnp.minimum(x, 0.0) - jnp.log(1.0 + jnp.exp(-jnp.abs(x)))


def _fox_prep(hbuf, bf_vec, *, name):
    t = hbuf.shape[0]
    tm = _pick(t, (FOX_PREP_TM, 128))
    cs = COL_SMALL // LANES

    def body(s_ref, b_ref, eq_ref, ek_ref, carry):
        i = pl.program_id(0)

        @pl.when(i == 0)
        def _():
            carry[...] = jnp.zeros_like(carry)

        lf = _log_sigmoid(s_ref[...] + b_ref[...])
        f = _cumsum_rows(lf) + carry[0:1, :]
        carry[...] = jnp.broadcast_to(f[tm - 1:tm, :], carry.shape)
        lane = _iota((tm, LANES), 1)
        for h in range(ATT_HEADS):
            base = HEAD_DIM * (1 - h % 2)
            fh = _col(f, h)
            hi = fh.astype(BF16).astype(F32)
            mid = (fh - hi).astype(BF16).astype(F32)
            lo = fh - hi - mid
            terms = jnp.where(lane == base, hi, jnp.where(lane == base + 1, mid, jnp.where(lane == base + 2, lo, 0.0)))
            terms_k = jnp.where(lane == base + 3, -hi,
                                jnp.where(lane == base + 4, -mid, jnp.where(lane == base + 5, -lo, 0.0)))
            ones_q = ((lane >= base + 3) & (lane < base + 6)).astype(F32)
            ones_k = ((lane >= base) & (lane < base + 3)).astype(F32)
            eq_ref[:, LANES * h:LANES * (h + 1)] = (terms + ones_q).astype(BF16)
            ek_ref[:, LANES * h:LANES * (h + 1)] = (terms_k + ones_k).astype(BF16)

    ospec = pl.BlockSpec((tm, ATT_HEADS * LANES), lambda i: (i, 0))
    o = jax.ShapeDtypeStruct((t, ATT_HEADS * LANES), BF16)
    return pl.pallas_call(
        body, name=name, grid=(t // tm,),
        in_specs=[pl.BlockSpec((tm, LANES), lambda i: (i, cs)), pl.BlockSpec((1, LANES), lambda i: (0, 0))],
        out_specs=[ospec, ospec], out_shape=[o, o],
        scratch_shapes=[pltpu.VMEM((SUBLANES, LANES), F32)],
        compiler_params=_params(1),
    )(hbuf, bf_vec)


def _fox_post(dfc, hbuf, bf_vec, *, name):
    t = hbuf.shape[0]
    tm = _pick(t, (FOX_PREP_TM, 128))
    nb = t // tm
    cs = COL_SMALL // LANES

    def body(df_ref, s_ref, b_ref, o_ref, db_ref, carry):
        i = pl.program_id(0)

        @pl.when(i == 0)
        def _():
            carry[...] = jnp.zeros_like(carry)
            db_ref[...] = jnp.zeros_like(db_ref)

        dlf = _cumsum_rows(df_ref[...], reverse=True) + carry[0:1, :]
        carry[...] = jnp.broadcast_to(dlf[0:1, :], carry.shape)
        dl = dlf * _sigmoid(-(s_ref[...] + b_ref[...]))
        dl = jnp.where(_iota(dl.shape, 1) < ATT_HEADS, dl, 0.0)
        o_ref[...] = dl
        db_ref[...] += jnp.sum(dl, axis=0, keepdims=True)

    vec = pl.BlockSpec((1, LANES), lambda i: (0, 0))
    return pl.pallas_call(
        body, name=name, grid=(nb,),
        in_specs=[pl.BlockSpec((tm, LANES), lambda i: (nb - 1 - i, 0)),
                  pl.BlockSpec((tm, LANES), lambda i: (nb - 1 - i, cs)), vec],
        out_specs=[pl.BlockSpec((tm, LANES), lambda i: (nb - 1 - i, 0)), vec],
        out_shape=[jax.ShapeDtypeStruct((t, LANES), F32), jax.ShapeDtypeStruct((1, LANES), F32)],
        scratch_shapes=[pltpu.VMEM((SUBLANES, LANES), F32)],
        compiler_params=_params(1),
    )(dfc, hbuf, bf_vec)


def _fox_masks(i, j, tq):
    row = i * tq + _iota((tq, tq), 0)
    col = j * tq + _iota((tq, tq), 1)
    lane = _iota((1, LANES), 1)
    return col <= row, (lane < HEAD_DIM, lane >= HEAD_DIM)


def _hosting(body, n_in, n_out, n_scratch, comm, grid):
    na, no = len(comm.arrays), len(comm.out_shapes)

    def hosted(*refs):
        o0 = n_in + na
        s0 = o0 + n_out + no
        cargs = (refs[n_in:o0], refs[o0 + n_out:s0]) + tuple(refs[s0 + n_scratch:])
        a, b = pl.program_id(0), pl.program_id(1)

        @pl.when((a == 0) & (b == 0))
        def _():
            comm.start(*cargs)

        @pl.when((a == grid[0] - 1) & (b == 0))
        def _():
            comm.middle(*cargs)

        body(*refs[:n_in], *refs[o0:o0 + n_out], *refs[s0:s0 + n_scratch])

        @pl.when((a == grid[0] - 1) & (b == grid[1] - 1))
        def _():
            comm.finish(*cargs)

    return hosted


def _hosted_call(body, comm, grid, *, name, in_specs, out_specs, out_shape, scratch_shapes, args):
    n_out = len(out_shape)
    if comm is not None:
        cin, cout, sems = comm.specs()
        body = _hosting(body, len(in_specs), n_out, len(scratch_shapes), comm, grid)
        in_specs, out_specs = in_specs + cin, out_specs + cout
        out_shape, scratch_shapes, args = out_shape + comm.out_shapes, scratch_shapes + sems, args + list(comm.arrays)
    outs = pl.pallas_call(body, name=name, grid=grid, in_specs=in_specs, out_specs=out_specs,
                          out_shape=out_shape, scratch_shapes=scratch_shapes, compiler_params=_params(2))(*args)
    return outs[:n_out], outs[n_out:]


def _merge_comms(comms):
    comms = [c for c in comms if c is not None]
    if len(comms) <= 1:
        return comms[0] if comms else None

    def both(which):
        def run(ins, outs, ssem, rsem):
            ia = io = 0
            for c in comms:
                na, no = len(c.arrays), len(c.out_shapes)
                getattr(c, which)(ins[ia:ia + na], outs[io:io + no], ssem, rsem)
                ia, io = ia + na, io + no
        return run

    spans = sorted((c.base, c.base + c.n_own) for c in comms)
    assert all(a[1] <= b[0] for a, b in zip(spans, spans[1:])), "semaphore ranges overlap"
    return _Comm(sum((list(c.arrays) for c in comms), []), sum((list(c.out_shapes) for c in comms), []),
                 spans[-1][1], both("start"), both("finish"), middle=both("middle"))


def _fox_fwd(hbuf, eq, ek, *, comm=None, name):
    t = hbuf.shape[0]
    w = ATT_WIDTH
    tq = _pick(t, (FOX_T, 256, 128))
    nq = t // tq
    cq, ck, cv = COL_Q // w, COL_K // w, COL_V // w

    def body(q_ref, k_ref, v_ref, eq_ref, ek_ref, o_ref, lse_ref, m_s, l_s, acc_s):
        i = pl.program_id(0)
        j = pl.program_id(1)

        @pl.when(j == 0)
        def _():
            m_s[...] = jnp.full_like(m_s, NEG)
            l_s[...] = jnp.zeros_like(l_s)
            acc_s[...] = jnp.zeros_like(acc_s)

        def step(diagonal):
            _, hms = _fox_masks(i, j, tq)
            keys_first = (j * tq + _iota((tq, tq), 0)) <= (i * tq + _iota((tq, tq), 1))
            half = _iota((LANES, 1), 0)
            hrows = (half < HEAD_DIM, half >= HEAD_DIM)
            m_all = m_s[...]
            l_all = l_s[...]
            acc_old = [acc_s[LANES * pr:LANES * (pr + 1), :] for pr in range(2)]
            m_out, l_out, acc_out = [], [], []
            for pr in range(2):
                sl = slice(LANES * pr, LANES * (pr + 1))
                qp = q_ref[:, sl]
                kp = k_ref[:, sl]
                vt = v_ref[:, sl].T.astype(BF16)
                acc = acc_old[pr]
                for hh in range(2):
                    h = 2 * pr + hh
                    hsl = slice(LANES * h, LANES * (h + 1))
                    qm = jnp.where(hms[hh], (qp * (HEAD_DIM ** -0.5)).astype(BF16), eq_ref[:, hsl])
                    km = jnp.where(hms[hh], kp.astype(BF16), ek_ref[:, hsl])
                    st = _dot(km, qm, 1, 1)
                    if diagonal:
                        st = jnp.where(keys_first, st, NEG)
                    m_old = m_all[h:h + 1, :]
                    m_new = jnp.maximum(m_old, jnp.max(st, axis=0, keepdims=True))
                    alpha = jnp.exp(m_old - m_new)
                    pt = jnp.exp(st - m_new)
                    l_out.append(alpha * l_all[h:h + 1, :] + jnp.sum(pt, axis=0, keepdims=True))
                    m_out.append(m_new)
                    pv = _dot(vt, pt.astype(BF16), 1, 0)
                    acc = jnp.where(hrows[hh], alpha * acc_old[pr] + pv, acc)
                acc_out.append(acc)
            for h in range(ATT_HEADS):
                m_s[h:h + 1, :] = m_out[h]
                l_s[h:h + 1, :] = l_out[h]
            for pr in range(2):
                acc_s[LANES * pr:LANES * (pr + 1), :] = acc_out[pr]

        @pl.when(j < i)
        def _():
            step(False)

        @pl.when(j == i)
        def _():
            step(True)
            half = _iota((LANES, 1), 0)
            l_all = l_s[...]
            for pr in range(2):
                acc = acc_s[LANES * pr:LANES * (pr + 1), :]
                o_t = jnp.where(half < HEAD_DIM, acc / l_all[2 * pr:2 * pr + 1, :], acc / l_all[2 * pr + 1:2 * pr + 2, :])
                o_ref[:, LANES * pr:LANES * (pr + 1)] = o_t.T
            lse = m_s[...] + jnp.log(l_s[...])
            lse_ref[...] = jnp.where(_iota(lse.shape, 0) < ATT_HEADS, lse, 0.0)

    return _hosted_call(
        body, comm, (nq, nq), name=name,
        in_specs=[pl.BlockSpec((tq, w), lambda i, j: (i, cq)),
                  pl.BlockSpec((tq, w), lambda i, j: (jnp.minimum(j, i), ck)),
                  pl.BlockSpec((tq, w), lambda i, j: (jnp.minimum(j, i), cv)),
                  pl.BlockSpec((tq, ATT_HEADS * LANES), lambda i, j: (i, 0)),
                  pl.BlockSpec((tq, ATT_HEADS * LANES), lambda i, j: (jnp.minimum(j, i), 0))],
        out_specs=[pl.BlockSpec((tq, w), lambda i, j: (i, 0)),
                   pl.BlockSpec((SUBLANES, tq), lambda i, j: (0, i))],
        out_shape=[jax.ShapeDtypeStruct((t, w), F32), jax.ShapeDtypeStruct((SUBLANES, t), F32)],
        scratch_shapes=[pltpu.VMEM((SUBLANES, tq), F32), pltpu.VMEM((SUBLANES, tq), F32),
                        pltpu.VMEM((w, tq), F32)],
        args=[hbuf, hbuf, hbuf, eq, ek])


def _fox_delta(dymix, o, *, name):
    t, w = o.shape
    tm = _pick(t, (512, 256, 128))
    cdo = ATT_WIDTH // w

    def body(do_ref, o_ref, d_ref):
        d_ref[...] = _head_reduce(do_ref[...] * o_ref[...], 0, ATT_HEADS)

    return pl.pallas_call(
        body, name=name, grid=(t // tm,),
        in_specs=[pl.BlockSpec((tm, w), lambda i: (i, cdo)), pl.BlockSpec((tm, w), lambda i: (i, 0))],
        out_specs=pl.BlockSpec((tm, LANES), lambda i: (i, 0)),
        out_shape=jax.ShapeDtypeStruct((t, LANES), F32),
        compiler_params=_params(1),
    )(dymix, o)


def _fox_bwd(hbuf, eq, ek, dymix, lse_rows, delta_rows, *, comm=None, name):
    t = hbuf.shape[0]
    w = ATT_WIDTH
    tq = _pick(t, (FOX_T, 256, 128))
    nq = t // tq
    cq, ck, cv = COL_Q // w, COL_K // w, COL_V // w
    cdo = ATT_WIDTH // w

    def body(q_ref, k_ref, v_ref, eq_ref, ek_ref, do_ref, lse_ref, dl_ref, dk_ref, dv_ref, dfk_ref, dqt_ref, dfq_ref,
             dk_s, dv_s, dfk_s):
        j = pl.program_id(0)
        i = pl.program_id(1)

        @pl.when((i == 0) & (j == 0))
        def _():
            dqt_ref[...] = jnp.zeros_like(dqt_ref)
            dfq_ref[...] = jnp.zeros_like(dfq_ref)

        @pl.when(i == 0)
        def _():
            dk_s[...] = jnp.zeros_like(dk_s)
            dv_s[...] = jnp.zeros_like(dv_s)
            dfk_s[...] = jnp.zeros_like(dfk_s)

        def step(diagonal):
            _, hms = _fox_masks(i, j, tq)
            keys_first = (j * tq + _iota((tq, tq), 0)) <= (i * tq + _iota((tq, tq), 1))
            half = _iota((LANES, 1), 0)
            hrows = (half < HEAD_DIM, half >= HEAD_DIM)
            lse_all = lse_ref[...]
            dl_all = dl_ref[...]
            dvs, dks, dfks, dqts, dfqs = [], [], [], [], []
            for pr in range(2):
                sl = slice(LANES * pr, LANES * (pr + 1))
                qp = q_ref[:, sl]
                kp = k_ref[:, sl]
                kt = kp.T.astype(BF16)
                vpb = v_ref[:, sl].astype(BF16)
                dop = do_ref[:, sl]
                dv_p = jnp.zeros((tq, LANES), F32)
                dk_p = jnp.zeros((tq, LANES), F32)
                dqt_p = jnp.zeros((LANES, tq), F32)
                for hh in range(2):
                    h = 2 * pr + hh
                    hsl = slice(LANES * h, LANES * (h + 1))
                    qm = jnp.where(hms[hh], (qp * (HEAD_DIM ** -0.5)).astype(BF16), eq_ref[:, hsl])
                    km = jnp.where(hms[hh], kp.astype(BF16), ek_ref[:, hsl])
                    st = _dot(km, qm, 1, 1)
                    if diagonal:
                        st = jnp.where(keys_first, st, NEG)
                    pt = jnp.exp(st - lse_all[h:h + 1, :])
                    domb = jnp.where(hms[hh], dop, 0.0).astype(BF16)
                    dv_p = dv_p + _dot(pt.astype(BF16), domb, 1, 0)
                    dpt = _dot(vpb, domb, 1, 1)
                    dst = pt * (dpt - dl_all[h:h + 1, :])
                    dstb = dst.astype(BF16)
                    dk_p = dk_p + jnp.where(hms[hh], _dot(dstb, qm, 1, 0), 0.0)
                    dqt_p = dqt_p + _dot(jnp.where(hrows[hh], kt, 0.0), dstb, 1, 0)
                    part = dst[:, 0:LANES]
                    for c in range(1, tq // LANES):
                        part = part + dst[:, LANES * c:LANES * (c + 1)]
                    dfks.append(part)
                    dfqs.append(jnp.sum(dst, axis=0, keepdims=True))
                dvs.append(dv_p)
                dks.append(dk_p)
                dqts.append(dqt_p)
            dv_s[...] += jnp.concatenate(dvs, axis=1)
            dk_s[...] += jnp.concatenate(dks, axis=1)
            for h in range(ATT_HEADS):
                dfk_s[h] += dfks[h]
            cols = pl.ds(pl.multiple_of(i * tq, tq), tq)
            dqt_ref[:, cols] += jnp.concatenate(dqts, axis=0) * (HEAD_DIM ** -0.5)
            dfq_ref[:, cols] += jnp.concatenate(dfqs + [jnp.zeros((SUBLANES - ATT_HEADS, tq), F32)], axis=0)

        @pl.when(i > j)
        def _():
            step(False)

        @pl.when(i == j)
        def _():
            step(True)

        @pl.when(i == nq - 1)
        def _():
            dk_ref[...] = dk_s[...].astype(BF16)
            dv_ref[...] = dv_s[...].astype(BF16)
            lane = _iota((tq, LANES), 1)
            out = jnp.zeros((tq, LANES), F32)
            for h in range(ATT_HEADS):
                out = jnp.where(lane == h, jnp.sum(dfk_s[h], axis=1, keepdims=True), out)
            dfk_ref[...] = out

    qi = lambda j, i: jnp.maximum(i, j)
    rows = pl.BlockSpec((SUBLANES, tq), lambda j, i: (0, qi(j, i)))
    return _hosted_call(
        body, comm, (nq, nq), name=name,
        in_specs=[pl.BlockSpec((tq, w), lambda j, i: (qi(j, i), cq)),
                  pl.BlockSpec((tq, w), lambda j, i: (j, ck)),
                  pl.BlockSpec((tq, w), lambda j, i: (j, cv)),
                  pl.BlockSpec((tq, ATT_HEADS * LANES), lambda j, i: (qi(j, i), 0)),
                  pl.BlockSpec((tq, ATT_HEADS * LANES), lambda j, i: (j, 0)),
                  pl.BlockSpec((tq, w), lambda j, i: (qi(j, i), cdo)),
                  rows, rows],
        out_specs=[pl.BlockSpec((tq, w), lambda j, i: (j, 0)), pl.BlockSpec((tq, w), lambda j, i: (j, 0)),
                   pl.BlockSpec((tq, LANES), lambda j, i: (j, 0)),
                   pl.BlockSpec((w, t), lambda j, i: (0, 0)), pl.BlockSpec((SUBLANES, t), lambda j, i: (0, 0))],
        out_shape=[jax.ShapeDtypeStruct((t, w), BF16), jax.ShapeDtypeStruct((t, w), BF16),
                   jax.ShapeDtypeStruct((t, LANES), F32),
                   jax.ShapeDtypeStruct((w, t), F32), jax.ShapeDtypeStruct((SUBLANES, t), F32)],
        scratch_shapes=[pltpu.VMEM((tq, w), F32), pltpu.VMEM((tq, w), F32),
                        pltpu.VMEM((ATT_HEADS, tq, LANES), F32)],
        args=[hbuf, hbuf, hbuf, eq, ek, dymix, lse_rows, delta_rows])


GROUP_W = SSD_WIDTH // SSD_GROUPS
HEADS_PER_GROUP = SSD_HEADS // SSD_GROUPS


def _ssd_chunk_common(xr, prev8, sm, cw, cb, dtb, avec):
    c = _conv_taps(xr, prev8, cw, cb)
    sig = _sigmoid(c)
    xa = c * sig
    dt = _softplus(sm + dtb)
    a = dt * avec
    acum = _cumsum_rows(a)
    return c, sig, xa, dt, acum


def _ssd_head_cols(acum, acum_t):
    cols = [_col(acum, LANE_DT + h) for h in range(SSD_HEADS)]
    rows = [_row(acum_t, LANE_DT + h) for h in range(SSD_HEADS)]
    return cols, rows


def _expand_heads(vals, width):
    rows = vals[0].shape[0]
    colhead = _iota((rows, width), 1) // HEAD_DIM
    out = jnp.broadcast_to(vals[0], (rows, width))
    for h in range(1, len(vals)):
        out = jnp.where(colhead == h, vals[h], out)
    return out


def _ssd_decays(cols, g):
    mine = cols[HEADS_PER_GROUP * g:HEADS_PER_GROUP * (g + 1)]
    n = mine[0].shape[0]
    atots = [c[n - 1:n, :] for c in mine]
    e = _expand_heads([jnp.exp(c) for c in mine], GROUP_W)
    dec = _expand_heads([jnp.exp(t - c) for c, t in zip(mine, atots)], GROUP_W)
    etot = _expand_heads([jnp.exp(t) for t in atots], GROUP_W)
    return e, dec, etot


def _ssd_ldec(cols, rows, h, tril):
    return jnp.exp(jnp.where(tril, cols[h] - rows[h], NEG))


def _ssd_fwd(hbuf, conv_w, conv_b, dtb_vec, a_vec, d_exp, norm_g, *, name):
    t = hbuf.shape[0]
    L = SSD_CHUNK
    nc = t // L
    hb = L // SUBLANES
    cs = COL_SMALL // LANES
    cz = COL_Z // SSD_WIDTH

    def body(x_ref, xp_ref, z_ref, s_ref, cw_ref, cb_ref, dtb_ref, av_ref, dx_ref, ng_ref,
             yc_ref, y_ref, st_ref, state):
        i = pl.program_id(0)

        @pl.when(i == 0)
        def _():
            state[...] = jnp.zeros_like(state)

        prev = jnp.where(i == 0, 0.0, xp_ref[...])
        _, _, xa, dt, acum = _ssd_chunk_common(x_ref[...], prev, s_ref[...], cw_ref[...], cb_ref[...],
                                               dtb_ref[...], av_ref[...])
        cols, rows = _ssd_head_cols(acum, acum.T)
        xs = xa[:, :SSD_WIDTH]
        xdt = xs * _head_expand(dt, LANE_DT, SSD_HEADS, SSD_WIDTH)
        tril = _iota((L, L), 0) >= _iota((L, L), 1)
        lane = _iota((1, LANES), 1)
        ys = []
        for g in range(SSD_GROUPS):
            bg = xa[:, SSD_WIDTH + SSD_STATE * g:SSD_WIDTH + SSD_STATE * (g + 1)].astype(BF16)
            cg = xa[:, SSD_WIDTH + SSD_STATE * (SSD_GROUPS + g):SSD_WIDTH + SSD_STATE * (SSD_GROUPS + g + 1)].astype(BF16)
            gm = _dot(cg, bg, 1, 1)
            e, dec, etot = _ssd_decays(cols, g)
            s_in = state[g]
            st_ref[0, g] = s_in
            xg = xdt[:, GROUP_W * g:GROUP_W * (g + 1)]
            y_off = e * _dot(cg, s_in.astype(BF16), 1, 0)
            state[g] = etot * s_in + _dot(bg, (dec * xg).astype(BF16), 0, 0)
            for pr in range(2):
                xp = xg[:, LANES * pr:LANES * (pr + 1)].astype(BF16)
                outs = []
                for hh in range(2):
                    h = HEADS_PER_GROUP * g + 2 * pr + hh
                    m = gm * _ssd_ldec(cols, rows, h, tril)
                    outs.append(_dot(m.astype(BF16), xp, 1, 0))
                ys.append(jnp.where(lane < HEAD_DIM, outs[0], outs[1]) + y_off[:, LANES * pr:LANES * (pr + 1)])
        y = jnp.concatenate(ys, axis=1)
        y_ref[...] = y
        yd = y + dx_ref[...] * xs
        zz = z_ref[...]
        y2 = yd * zz * _sigmoid(zz)
        ng = ng_ref[...]
        outs = []
        for g in range(SSD_GROUPS):
            yg = y2[:, GROUP_W * g:GROUP_W * (g + 1)]
            rs = lax.rsqrt(jnp.mean(yg * yg, axis=1, keepdims=True) + RMS_EPS)
            outs.append(yg * rs * ng[:, GROUP_W * g:GROUP_W * (g + 1)])
        yc_ref[...] = jnp.concatenate(outs, axis=1)

    cdim = SSD_CONV_DIM
    vecc = pl.BlockSpec((1, cdim), lambda i: (0, 0))
    vecl = pl.BlockSpec((1, LANES), lambda i: (0, 0))
    vecw = pl.BlockSpec((1, SSD_WIDTH), lambda i: (0, 0))
    roww = pl.BlockSpec((L, SSD_WIDTH), lambda i: (i, 0))
    return pl.pallas_call(
        body, name=name, grid=(nc,),
        in_specs=[pl.BlockSpec((L, cdim), lambda i: (i, 0)),
                  pl.BlockSpec((SUBLANES, cdim), lambda i: (jnp.maximum(i * hb - 1, 0), 0)),
                  pl.BlockSpec((L, SSD_WIDTH), lambda i: (i, cz)),
                  pl.BlockSpec((L, LANES), lambda i: (i, cs)),
                  pl.BlockSpec((CONV_K, cdim), lambda i: (0, 0)), vecc, vecl, vecl, vecw, vecw],
        out_specs=[roww, roww, pl.BlockSpec((1, SSD_GROUPS, SSD_STATE, GROUP_W), lambda i: (i, 0, 0, 0))],
        out_shape=[jax.ShapeDtypeStruct((t, SSD_WIDTH), F32), jax.ShapeDtypeStruct((t, SSD_WIDTH), F32),
                   jax.ShapeDtypeStruct((nc, SSD_GROUPS, SSD_STATE, GROUP_W), F32)],
        scratch_shapes=[pltpu.VMEM((SSD_GROUPS, SSD_STATE, GROUP_W), F32)],
        compiler_params=_params(1),
    )(hbuf, hbuf, hbuf, hbuf, conv_w, conv_b, dtb_vec, a_vec, d_exp, norm_g)


def _ssd_bwd(dymix, hbuf, y_ssd, states, conv_w, conv_b, dtb_vec, a_vec, d_exp, norm_g, *, name):
    t = hbuf.shape[0]
    L = SSD_CHUNK
    nc = t // L
    hb = L // SUBLANES
    cs = COL_SMALL // LANES
    cz = COL_Z // SSD_WIDTH
    cdy = (LRU_WIDTH + ATT_WIDTH) // SSD_WIDTH
    cdim = SSD_CONV_DIM

    def body(dyc_ref, x_ref, xp_ref, z_ref, s_ref, y_ref, st_ref, cw_ref, cb_ref, dtb_ref, av_ref, dx_ref, ng_ref,
             dxr_ref, dz_ref, dsm_ref, dng_ref, dd_ref, da_ref, ddtb_ref, dcw_ref, dcb_ref,
             dstate, dnext):
        i = pl.program_id(0)
        ic = nc - 1 - i

        @pl.when(i == 0)
        def _():
            dstate[...] = jnp.zeros_like(dstate)
            dnext[...] = jnp.zeros_like(dnext)
            for ref in (dng_ref, dd_ref, da_ref, ddtb_ref, dcw_ref, dcb_ref):
                ref[...] = jnp.zeros_like(ref)

        xr = x_ref[...]
        sm = s_ref[...]
        prev = jnp.where(ic == 0, 0.0, xp_ref[...])
        avec = av_ref[...]
        c, sig, xa, dt, acum = _ssd_chunk_common(xr, prev, sm, cw_ref[...], cb_ref[...], dtb_ref[...], avec)
        cols, rows = _ssd_head_cols(acum, acum.T)
        xs = xa[:, :SSD_WIDTH]
        dtx = _head_expand(dt, LANE_DT, SSD_HEADS, SSD_WIDTH)
        xdt = xs * dtx
        tril = _iota((L, L), 0) >= _iota((L, L), 1)
        lane = _iota((1, LANES), 1)
        hmasks = (lane < HEAD_DIM, lane >= HEAD_DIM)

        y = y_ref[...]
        dexp = dx_ref[...]
        yd = y + dexp * xs
        zz = z_ref[...]
        sz = _sigmoid(zz)
        siluz = zz * sz
        y2 = yd * siluz
        ng = ng_ref[...]
        dyc = dyc_ref[...]
        dy2s, dngs = [], []
        for g in range(SSD_GROUPS):
            sl = slice(GROUP_W * g, GROUP_W * (g + 1))
            yg = y2[:, sl]
            rs = lax.rsqrt(jnp.mean(yg * yg, axis=1, keepdims=True) + RMS_EPS)
            wv = dyc[:, sl] * ng[:, sl]
            dngs.append(jnp.sum(dyc[:, sl] * yg * rs, axis=0, keepdims=True))
            dy2s.append(rs * wv - yg * (rs * rs * rs) * jnp.mean(wv * yg, axis=1, keepdims=True))
        dy2 = jnp.concatenate(dy2s, axis=1)
        dng_ref[...] += jnp.concatenate(dngs, axis=1)
        dz_ref[...] = (dy2 * yd * (sz * (1.0 + zz * (1.0 - sz)))).astype(BF16)
        dy = dy2 * siluz
        dd_ref[...] += jnp.sum(dy * xs, axis=0, keepdims=True)

        dxs, dbs, dcs = [], [], []
        datot = jnp.zeros((1, LANES), F32)
        lanes = _iota((L, LANES), 1)
        dacum = jnp.zeros((L, LANES), F32)
        for g in range(SSD_GROUPS):
            sl = slice(GROUP_W * g, GROUP_W * (g + 1))
            bg = xa[:, SSD_WIDTH + SSD_STATE * g:SSD_WIDTH + SSD_STATE * (g + 1)].astype(BF16)
            cg = xa[:, SSD_WIDTH + SSD_STATE * (SSD_GROUPS + g):SSD_WIDTH + SSD_STATE * (SSD_GROUPS + g + 1)].astype(BF16)
            gm = _dot(cg, bg, 1, 1)
            e, dec, etot = _ssd_decays(cols, g)
            s_in = st_ref[0, g]
            ds_out = dstate[g]
            dyg = dy[:, sl]
            xg = xdt[:, sl]
            edy = (e * dyg).astype(BF16)
            dstate[g] = etot * ds_out + _dot(cg, edy, 0, 0)
            dx_state = dec * _dot(bg, ds_out.astype(BF16), 1, 0)
            y_off = e * _dot(cg, s_in.astype(BF16), 1, 0)
            dacum = dacum + _head_reduce_group(dyg * y_off - xg * dx_state, g)
            dc_off = _dot(edy, s_in.astype(BF16), 1, 1)
            db_state = _dot((dec * xg).astype(BF16), ds_out.astype(BF16), 1, 1)
            dgsum = jnp.zeros((L, L), F32)
            dx_pairs = []
            for pr in range(2):
                psl = slice(LANES * pr, LANES * (pr + 1))
                xp = xg[:, psl]
                dyp = dyg[:, psl]
                dx_pair = jnp.zeros((L, LANES), F32)
                for hh in range(2):
                    h = HEADS_PER_GROUP * g + 2 * pr + hh
                    ldec = _ssd_ldec(cols, rows, h, tril)
                    dym = jnp.where(hmasks[hh], dyp, 0.0).astype(BF16)
                    xm = jnp.where(hmasks[hh], xp, 0.0).astype(BF16)
                    dx_pair = dx_pair + _dot((gm * ldec).astype(BF16), dym, 0, 0)
                    dml = _dot(dym, xm, 1, 1) * ldec
                    dgsum = dgsum + dml
                    qm = dml * gm
                    seg = jnp.sum(qm, axis=1, keepdims=True) - jnp.sum(qm.T, axis=1, keepdims=True)
                    dacum = dacum + jnp.where(lanes == LANE_DT + h, seg, 0.0)
                dx_pairs.append(dx_pair)
            dgb = dgsum.astype(BF16)
            dcs.append(_dot(dgb, bg, 1, 0) + dc_off)
            dbs.append(_dot(dgb, cg, 0, 0) + db_state)
            dxg = jnp.concatenate(dx_pairs, axis=1) + dx_state
            dxs.append(dxg)
            v = jnp.sum(dx_state * xg, axis=0, keepdims=True) + etot * jnp.sum(ds_out * s_in, axis=0, keepdims=True)
            datot = datot + _head_reduce_row(v, LANE_DT + HEADS_PER_GROUP * g, HEADS_PER_GROUP)
        dx = jnp.concatenate(dxs, axis=1)
        dacum = dacum + jnp.where(_iota((L, LANES), 0) == L - 1, datot, 0.0)
        da = _cumsum_rows(dacum, reverse=True)
        ddt = da * avec + _head_reduce(dx * xs, LANE_DT, SSD_HEADS)
        da_ref[...] += jnp.sum(da * dt, axis=0, keepdims=True)
        ddt_raw = ddt * _sigmoid(sm + dtb_ref[...])
        ddt_raw = jnp.where((lanes >= LANE_DT) & (lanes < LANE_DT + SSD_HEADS), ddt_raw, 0.0)
        dsm_ref[...] = ddt_raw
        ddtb_ref[...] += jnp.sum(ddt_raw, axis=0, keepdims=True)
        dxs_total = dx * dtx + dexp * dy
        dxa = jnp.concatenate([dxs_total] + dbs + dcs, axis=1)
        dc = dxa * (sig * (1.0 + c * (1.0 - sig)))
        dxr, dws = _conv_taps_bwd(dc, dnext[...], cw_ref[...], xr)
        dxr_ref[...] = dxr.astype(BF16)
        dcw_ref[...] += dws
        dcb_ref[...] += jnp.sum(dc, axis=0, keepdims=True)
        dnext[...] = dc[:SUBLANES]

    rev = lambda i: nc - 1 - i
    vecc = pl.BlockSpec((1, cdim), lambda i: (0, 0))
    vecl = pl.BlockSpec((1, LANES), lambda i: (0, 0))
    vecw = pl.BlockSpec((1, SSD_WIDTH), lambda i: (0, 0))
    cwspec = pl.BlockSpec((CONV_K, cdim), lambda i: (0, 0))
    roww = pl.BlockSpec((L, SSD_WIDTH), lambda i: (rev(i), 0))
    return pl.pallas_call(
        body, name=name, grid=(nc,),
        in_specs=[pl.BlockSpec((L, SSD_WIDTH), lambda i: (rev(i), cdy)),
                  pl.BlockSpec((L, cdim), lambda i: (rev(i), 0)),
                  pl.BlockSpec((SUBLANES, cdim), lambda i: (jnp.maximum(rev(i) * hb - 1, 0), 0)),
                  pl.BlockSpec((L, SSD_WIDTH), lambda i: (rev(i), cz)),
                  pl.BlockSpec((L, LANES), lambda i: (rev(i), cs)),
                  roww,
                  pl.BlockSpec((1, SSD_GROUPS, SSD_STATE, GROUP_W), lambda i: (rev(i), 0, 0, 0)),
                  cwspec, vecc, vecl, vecl, vecw, vecw],
        out_specs=[pl.BlockSpec((L, cdim), lambda i: (rev(i), 0)), roww,
                   pl.BlockSpec((L, LANES), lambda i: (rev(i), 0)),
                   vecw, vecw, vecl, vecl, cwspec, vecc],
        out_shape=[jax.ShapeDtypeStruct((t, cdim), BF16), jax.ShapeDtypeStruct((t, SSD_WIDTH), BF16),
                   jax.ShapeDtypeStruct((t, LANES), F32),
                   jax.ShapeDtypeStruct((1, SSD_WIDTH), F32), jax.ShapeDtypeStruct((1, SSD_WIDTH), F32),
                   jax.ShapeDtypeStruct((1, LANES), F32), jax.ShapeDtypeStruct((1, LANES), F32),
                   jax.ShapeDtypeStruct((CONV_K, cdim), F32), jax.ShapeDtypeStruct((1, cdim), F32)],
        scratch_shapes=[pltpu.VMEM((SSD_GROUPS, SSD_STATE, GROUP_W), F32), pltpu.VMEM((SUBLANES, cdim), F32)],
        compiler_params=_params(1),
    )(dymix, hbuf, hbuf, hbuf, hbuf, y_ssd, states, conv_w, conv_b, dtb_vec, a_vec, d_exp, norm_g)


def _head_reduce_group(x, g):
    return _head_reduce(x, LANE_DT + HEADS_PER_GROUP * g, HEADS_PER_GROUP)


def _head_reduce_row(v, lane0, nheads):
    colhead = _iota(v.shape, 1) // HEAD_DIM
    lane = _iota((1, LANES), 1)
    out = jnp.zeros((1, LANES), F32)
    for h in range(nheads):
        s = jnp.sum(jnp.where(colhead == h, v, 0.0), axis=1, keepdims=True)
        out = jnp.where(lane == lane0 + h, s, out)
    return out


def _exchange(inps, axes, *, swap=False, name):
    n = 2 ** len(axes)
    assert not swap or n == 2
    counts = [a.shape[0] for a in inps]
    out_shapes = [jax.ShapeDtypeStruct(a.shape if swap else (n,) + a.shape, a.dtype) for a in inps]
    units = sum(counts)
    na = len(inps)

    def body(*refs):
        in_refs, out_refs = refs[:na], refs[na:2 * na]
        send_sems, recv_sems, local_sems = refs[2 * na:]
        pos = {ax: lax.axis_index(ax) for ax in MESH_AXES}

        def slot_of(coord):
            s = 0
            for ax in axes:
                s = s * 2 + coord[ax]
            return s

        me = slot_of(pos)
        copies = []
        unit = 0
        for a in range(na):
            for it in range(counts[a]):
                dst = out_refs[a].at[it] if swap else out_refs[a].at[me, it]
                if not swap:
                    cp = pltpu.make_async_copy(in_refs[a].at[it], dst, local_sems.at[unit])
                    cp.start()
                    copies.append(cp)
                for delta in range(1, n):
                    coord = dict(pos)
                    for b, ax in enumerate(reversed(axes)):
                        if (delta >> b) & 1:
                            coord[ax] = 1 - pos[ax]
                    k = unit * (n - 1) + delta - 1
                    cp = pltpu.make_async_remote_copy(
                        src_ref=in_refs[a].at[it], dst_ref=dst,
                        send_sem=send_sems.at[k], recv_sem=recv_sems.at[k],
                        device_id=(coord["x"], coord["y"], coord["c"]), device_id_type=pl.DeviceIdType.MESH)
                    cp.start()
                    copies.append(cp)
                unit += 1
        for cp in copies:
            cp.wait()

    any_spec = pl.BlockSpec(memory_space=pl.ANY)
    return pl.pallas_call(
        body, name=name,
        in_specs=[any_spec] * na, out_specs=[any_spec] * na, out_shape=out_shapes,
        scratch_shapes=[pltpu.SemaphoreType.DMA((units * (n - 1),)), pltpu.SemaphoreType.DMA((units * (n - 1),)),
                        pltpu.SemaphoreType.DMA((units,))],
    )(*inps)


class _Comm:
    def __init__(self, arrays, out_shapes, n_own, start, finish, base=0, middle=None):
        self.arrays, self.out_shapes, self.start, self.finish = arrays, out_shapes, start, finish
        self.middle = middle or (lambda *refs: None)
        self.base, self.n_own, self.n_sems = base, n_own, base + n_own

    def specs(self):
        any_spec = pl.BlockSpec(memory_space=pl.ANY)
        sems = [pltpu.SemaphoreType.DMA((self.n_sems,)), pltpu.SemaphoreType.DMA((self.n_sems,))]
        return [any_spec] * len(self.arrays), [any_spec] * len(self.out_shapes), sems


def _run_comm(comm, *, name):
    na, no = len(comm.arrays), len(comm.out_shapes)

    def body(*refs):
        args = (refs[:na], refs[na:na + no]) + tuple(refs[na + no:])
        comm.start(*args)
        comm.middle(*args)
        comm.finish(*args)

    in_specs, out_specs, sems = comm.specs()
    return pl.pallas_call(body, name=name, in_specs=in_specs, out_specs=out_specs, out_shape=comm.out_shapes,
                          scratch_shapes=sems)(*comm.arrays)


def _chip_peer(x, y, d):
    px = 1 - x if d & 2 else x
    py = 1 - y if d & 1 else y
    return px, py, 2 * px + py


def _gather_layer_comm(srcs, li, base=0):
    counts = [s.shape[0] for s in srcs]
    units = [(a, it) for a in range(len(srcs)) for it in range(counts[a])]
    n_ici = 3 * len(units)
    out_shapes = [jax.ShapeDtypeStruct((N_CHIPS,) + s.shape, s.dtype) for s in srcs]

    def ici(ins, outs, ssem, rsem, u, d):
        x, y, c = (lax.axis_index(ax) for ax in MESH_AXES)
        a, it = units[u]
        px, py, _ = _chip_peer(x, y, d)
        k = base + 3 * u + d - 1
        return pltpu.make_async_remote_copy(
            src_ref=ins[a].at[it], dst_ref=outs[a].at[2 * x + y, it], send_sem=ssem.at[k], recv_sem=rsem.at[k],
            device_id=(px, py, c), device_id_type=pl.DeviceIdType.MESH)

    def arrived(ins, outs, ssem, rsem, u, d):
        x, y, c = (lax.axis_index(ax) for ax in MESH_AXES)
        a, it = units[u]
        _, _, pk = _chip_peer(x, y, d)
        k = base + 3 * u + d - 1
        return pltpu.make_async_remote_copy(
            src_ref=ins[a].at[it], dst_ref=outs[a].at[pk, it], send_sem=ssem.at[k], recv_sem=rsem.at[k],
            device_id=(x, y, c), device_id_type=pl.DeviceIdType.MESH)

    def forward(ins, outs, ssem, rsem, u, slot):
        x, y, c = (lax.axis_index(ax) for ax in MESH_AXES)
        a, it = units[u]
        pk = 2 * x + y if slot == 0 else _chip_peer(x, y, slot)[2]
        src = ins[a].at[it] if slot == 0 else outs[a].at[pk, it]
        k = base + n_ici + 4 * u + slot
        return pltpu.make_async_remote_copy(
            src_ref=src, dst_ref=outs[a].at[pk, it], send_sem=ssem.at[k], recv_sem=rsem.at[k],
            device_id=(x, y, 1 - c), device_id_type=pl.DeviceIdType.MESH)

    def start(ins, outs, ssem, rsem):
        for u in range(len(units)):
            forward(ins, outs, ssem, rsem, u, 0).start()

        @pl.when(lax.axis_index("c") == li)
        def _():
            for u in range(len(units)):
                for d in range(1, N_CHIPS):
                    ici(ins, outs, ssem, rsem, u, d).start()

    def middle(ins, outs, ssem, rsem):
        @pl.when(lax.axis_index("c") == li)
        def _():
            for u in range(len(units)):
                for d in range(1, N_CHIPS):
                    arrived(ins, outs, ssem, rsem, u, d).wait_recv()
                    forward(ins, outs, ssem, rsem, u, d).start()

    def finish(ins, outs, ssem, rsem):
        c = lax.axis_index("c")

        @pl.when(c == li)
        def _():
            for u in range(len(units)):
                for d in range(1, N_CHIPS):
                    ici(ins, outs, ssem, rsem, u, d).wait_send()
                    forward(ins, outs, ssem, rsem, u, d).wait_send()

        @pl.when(c != li)
        def _():
            for u in range(len(units)):
                for d in range(1, N_CHIPS):
                    forward(ins, outs, ssem, rsem, u, d).wait_recv()

        for u in range(len(units)):
            forward(ins, outs, ssem, rsem, u, 0).wait()

    return _Comm(srcs, out_shapes, n_ici + 4 * len(units), start, finish, base, middle)


def _reduce_chips_comm(sums, li, base=0):
    counts = [s.shape[0] for s in sums]
    units = [(a, it) for a in range(len(sums)) for it in range(counts[a])]
    out_shapes = [jax.ShapeDtypeStruct((N_CHIPS, s.shape[0]) + s.shape[2:], s.dtype) for s in sums]

    def copy(ins, outs, ssem, rsem, u, d):
        x, y, c = (lax.axis_index(ax) for ax in MESH_AXES)
        a, it = units[u]
        px, py, pk = _chip_peer(x, y, d)
        k = base + 3 * u + d - 1
        return pltpu.make_async_remote_copy(
            src_ref=ins[a].at[it, pk], dst_ref=outs[a].at[2 * x + y, it], send_sem=ssem.at[k], recv_sem=rsem.at[k],
            device_id=(px, py, c), device_id_type=pl.DeviceIdType.MESH)

    def start(ins, outs, ssem, rsem):
        @pl.when(lax.axis_index("c") == li)
        def _():
            for u in range(len(units)):
                for d in range(1, N_CHIPS):
                    copy(ins, outs, ssem, rsem, u, d).start()

    def finish(ins, outs, ssem, rsem):
        @pl.when(lax.axis_index("c") == li)
        def _():
            for u in range(len(units)):
                for d in range(1, N_CHIPS):
                    copy(ins, outs, ssem, rsem, u, d).wait()

    return _Comm(sums, out_shapes, 3 * len(units), start, finish, base)


def _sum_slots(buf, out_dtype, *, name):
    n, rows, cols = buf.shape
    tm = _pick(rows, (512, 256, 128, 8))
    if rows % tm:
        tm = rows

    def body(b_ref, o_ref):
        acc = b_ref[0].astype(F32)
        for s in range(1, n):
            acc = acc + b_ref[s].astype(F32)
        o_ref[...] = acc.astype(out_dtype)

    return pl.pallas_call(
        body, name=name, grid=(pl.cdiv(rows, tm),),
        in_specs=[pl.BlockSpec((n, tm, cols), lambda i: (0, i, 0))],
        out_specs=pl.BlockSpec((tm, cols), lambda i: (i, 0)),
        out_shape=jax.ShapeDtypeStruct((rows, cols), out_dtype),
        compiler_params=_params(1),
    )(buf)


def _sum_pair(a, b, out_dtype, *, name):
    shape = a.shape
    cols = shape[-1]
    a2, b2 = a.reshape(-1, cols), b.reshape(-1, cols)
    rows = a2.shape[0]
    tm = _pick(rows, (512, 256, 128, 8))

    def body(a_ref, b_ref, o_ref):
        o_ref[...] = (a_ref[...].astype(F32) + b_ref[...].astype(F32)).astype(out_dtype)

    spec = pl.BlockSpec((tm, cols), lambda i: (i, 0))
    return pl.pallas_call(
        body, name=name, grid=(rows // tm,), in_specs=[spec, spec], out_specs=spec,
        out_shape=jax.ShapeDtypeStruct((rows, cols), out_dtype), compiler_params=_params(1),
    )(a2, b2).reshape(shape)


def _adamw(w, g, m, v, *, name):
    shape = w.shape
    cols = shape[-1]
    rows = w.size // cols
    w2, g2, m2, v2 = (a.reshape(rows, cols) for a in (w, g, m, v))
    tm = _pick(rows, (256, 128, 64, 32, 16, 8))
    if rows % tm:
        tm = rows
    bc1 = 1.0 - ADAM_B1 ** ADAM_STEP
    bc2 = 1.0 - ADAM_B2 ** ADAM_STEP

    def body(w_ref, g_ref, m_ref, v_ref, d_ref, nm_ref, nv_ref):
        gg = g_ref[...]
        mm = ADAM_B1 * m_ref[...] + (1.0 - ADAM_B1) * gg
        vv = ADAM_B2 * v_ref[...] + (1.0 - ADAM_B2) * (gg * gg)
        m_hat = mm / bc1
        v_hat = vv / bc2
        d_ref[...] = -ADAM_LR * (m_hat / (jnp.sqrt(v_hat) + ADAM_EPS) + ADAM_WD * w_ref[...])
        nm_ref[...] = mm
        nv_ref[...] = vv

    spec = pl.BlockSpec((tm, cols), lambda i: (i, 0))
    o = jax.ShapeDtypeStruct((rows, cols), F32)
    outs = pl.pallas_call(
        body, name=name, grid=(rows // tm,), in_specs=[spec] * 4, out_specs=[spec] * 3, out_shape=[o] * 3,
        compiler_params=_params(1),
    )(w2, g2, m2, v2)
    return tuple(a.reshape(shape) for a in outs)


def _layer_fwd(li, x, xb, pb, W, up=None, att=None):
    nm = lambda s: f"l{li}_{s}"
    sv = {"x_in_b": xb}
    (g1, u1, a1), got = _mm_swiglu(xb, W["ffn1_wg"], W["ffn1_wu"], comm=up[0] if up else None, name=nm("ffn1_up"))
    if up:
        W = {**W, **up[1](got)}
    x1, x1b, xh1, rs1 = _mm_ln(a1, W["ffn1_wd"], x, W["ln1_g"], W["ln1_b"], rscale=ALPHA, mscale=0.5, name=nm("ffn1_down_ln"))
    hbuf = _mm(x1b, W["w_in_p"], name=nm("in_proj"))
    ya, lu, lr, lig, la, lh = _lru_fwd(hbuf, W["lru_conv_w"], W["lru_conv_b"], W["lru_wa_bd"], W["lru_ba"],
                                       W["lru_wx_bd"], W["lru_bx"], W["lru_lambda"], name=nm("lru_fwd"))
    eq, ek = _fox_prep(hbuf, W["fox_bf_vec"], name=nm("fox_prep"))
    (yb, lse_rows), got = _fox_fwd(hbuf, eq, ek, comm=att[0] if att else None, name=nm("fox_fwd"))
    if att:
        W = {**W, **att[1](got)}
    yc, yssd, states = _ssd_fwd(hbuf, W["ssd_conv_w"], W["ssd_conv_b"], W["ssd_dtb_vec"], W["ssd_a_vec"],
                                W["ssd_d_exp"], W["ssd_norm_g"], name=nm("ssd_fwd"))
    ymix = jnp.concatenate([ya, yb, yc], axis=1).astype(BF16)
    x2, x2b, xh2, rs2 = _mm_ln(ymix, W["w_out"], x1, W["ln2_g"], W["ln2_b"], rscale=ALPHA, mscale=1.0, name=nm("out_proj_ln"))
    (g2, u2, a2), _ = _mm_swiglu(x2b, W["ffn2_wg"], W["ffn2_wu"], name=nm("ffn2_up"))
    x3, x3b, xh3, rs3 = _mm_ln(a2, W["ffn2_wd"], x2, W["ln3_g"], W["ln3_b"], rscale=ALPHA, mscale=0.5, name=nm("ffn2_down_ln"))
    x4, x4b, sg, e = _mm_pe(x3, x3b, pb, W["pe_gate_w"], W["pe_gate_b"], W["pe_proj"], name=nm("ple"))
    sv.update(g1=g1, u1=u1, a1=a1, x1b=x1b, xh1=xh1, rs1=rs1, hbuf=hbuf, lu=lu, lr=lr, lig=lig, la=la, lh=lh,
              eq=eq, ek=ek, lse_rows=lse_rows, yb=yb, yssd=yssd, states=states, ymix=ymix, x2b=x2b, xh2=xh2, rs2=rs2,
              g2=g2, u2=u2, a2=a2, x3b=x3b, xh3=xh3, rs3=rs3, sg=sg, e=e, pb=pb)
    return x4, x4b, sv, W


def _layer_bwd(li, dx4, sv, W, comm=None, late=None, last=None):
    nm = lambda s: f"l{li}_{s}"
    G = {}
    dgp, de, dbg = _pe_bwd_elem(dx4, sv["sg"], sv["e"], name=nm("ple_bwd"))
    G["pe_gate_b"] = dbg
    G["pe_gate_w"] = _mm(sv["x3b"], dgp, ta=True, out_dtype=BF16, name=nm("d_pe_gate_w"))
    G["pe_proj"] = _mm(sv["pb"], de, ta=True, out_dtype=BF16, chip_cols=True, name=nm("d_pe_proj"))
    dr3, dr3b, G["ln3_g"], G["ln3_b"] = _bwd_proj([(dgp, W["pe_gate_w"])], dx4, rscale=1.0,
                                                  ln=(sv["xh3"], sv["rs3"], W["ln3_g"]), name=nm("ln3_bwd"))
    G["ffn2_wd"] = _mm(sv["a2"], dr3b, ta=True, scale=0.5, out_dtype=BF16, name=nm("d_ffn2_wd"))
    dg2, du2 = _mm_swiglu_bwd(dr3b, W["ffn2_wd"], sv["g2"], sv["u2"], scale=0.5, name=nm("ffn2_act_bwd"))
    G["ffn2_wg"] = _mm(sv["x2b"], dg2, ta=True, out_dtype=BF16, chip_cols=True, name=nm("d_ffn2_wg"))
    G["ffn2_wu"] = _mm(sv["x2b"], du2, ta=True, out_dtype=BF16, chip_cols=True, name=nm("d_ffn2_wu"))
    dr2, dr2b, G["ln2_g"], G["ln2_b"] = _bwd_proj([(dg2, W["ffn2_wg"]), (du2, W["ffn2_wu"])], dr3, rscale=ALPHA,
                                                  ln=(sv["xh2"], sv["rs2"], W["ln2_g"]), name=nm("ln2_bwd"))
    G["w_out"] = _mm(sv["ymix"], dr2b, ta=True, out_dtype=BF16, name=nm("d_w_out"))
    dymix = _mm(dr2b, W["w_out"], tb=True, name=nm("d_ymix"))
    hbuf = sv["hbuf"]
    (dur, dgr, G["lru_conv_w"], G["lru_conv_b"], G["lru_wa_bd"], G["lru_ba"], G["lru_wx_bd"], G["lru_bx"],
     G["lru_lambda"]) = _lru_bwd(dymix, hbuf, sv["lu"], sv["lr"], sv["lig"], sv["la"], sv["lh"],
                                 W["lru_conv_w"], W["lru_wa_bd"], W["lru_wx_bd"], W["lru_lambda"], name=nm("lru_bwd"))
    delta = _fox_delta(dymix, sv["yb"], name=nm("fox_delta"))
    delta_rows = jnp.pad(delta[:, :ATT_HEADS].T, ((0, SUBLANES - ATT_HEADS), (0, 0)))
    comm = _merge_comms([comm, late(G) if late else None])
    (dk, dv, dfk, dqt, dfq), comm_out = _fox_bwd(hbuf, sv["eq"], sv["ek"], dymix, sv["lse_rows"], delta_rows,
                                                 comm=comm, name=nm("fox_bwd"))
    dq = dqt.T
    dfc = jnp.pad(dfq[:ATT_HEADS].T, ((0, 0), (0, LANES - ATT_HEADS))) - dfk
    dsm_f, G["fox_bf_vec"] = _fox_post(dfc, hbuf, W["fox_bf_vec"], name=nm("fox_post"))
    (dxr, dz, dsm_dt, G["ssd_norm_g"], G["ssd_d_exp"], G["ssd_a_vec"], G["ssd_dtb_vec"], G["ssd_conv_w"],
     G["ssd_conv_b"]) = _ssd_bwd(dymix, hbuf, sv["yssd"], sv["states"], W["ssd_conv_w"], W["ssd_conv_b"],
                                 W["ssd_dtb_vec"], W["ssd_a_vec"], W["ssd_d_exp"], W["ssd_norm_g"], name=nm("ssd_bwd"))
    t = dx4.shape[0]
    dh = jnp.concatenate([dxr.astype(BF16), dz.astype(BF16), dur.astype(BF16), dgr.astype(BF16), dq.astype(BF16),
                          dk.astype(BF16), dv.astype(BF16), (dsm_f + dsm_dt).astype(BF16),
                          jnp.zeros((t, H_WIDTH - COL_SMALL - LANES), BF16)], axis=1)
    G["w_in_p"] = _mm(sv["x1b"], dh, ta=True, name=nm("d_w_in"))
    dr1, dr1b, G["ln1_g"], G["ln1_b"] = _bwd_proj([(dh, W["w_in_p"])], dr2, rscale=ALPHA,
                                                  ln=(sv["xh1"], sv["rs1"], W["ln1_g"]), name=nm("ln1_bwd"))
    G["ffn1_wd"] = _mm(sv["a1"], dr1b, ta=True, scale=0.5, out_dtype=BF16, name=nm("d_ffn1_wd"))
    dg1, du1 = _mm_swiglu_bwd(dr1b, W["ffn1_wd"], sv["g1"], sv["u1"], scale=0.5, name=nm("ffn1_act_bwd"))
    G["ffn1_wg"] = _mm(sv["x_in_b"], dg1, ta=True, out_dtype=BF16, chip_cols=True, name=nm("d_ffn1_wg"))
    G["ffn1_wu"] = _mm(sv["x_in_b"], du1, ta=True, out_dtype=BF16, chip_cols=True, name=nm("d_ffn1_wu"))
    dx_in, *last_out = _bwd_proj([(dg1, W["ffn1_wg"]), (du1, W["ffn1_wu"])], dr1, rscale=ALPHA, ln=None,
                                 comm=last(G) if last else None, name=nm("x_in_bwd"))
    return dx_in, G, comm_out, (last_out[0] if last_out else None)


def _block_diag(w):
    n, b, _ = w.shape
    eye = jnp.eye(n, dtype=w.dtype)
    return (eye[:, None, :, None] * w[:, :, None, :]).reshape(n * b, n * b)


def _block_diag_extract(m):
    n, b = LRU_HEADS, HEAD_DIM
    return jnp.stack([m[b * i:b * (i + 1), b * i:b * (i + 1)] for i in range(n)])


def _lane_vec(v, lane0):
    return jnp.pad(v.astype(F32), (lane0, LANES - lane0 - v.shape[0])).reshape(1, LANES)


def _w_in_permute(w):
    d = w.shape[0]
    z = lambda n: jnp.zeros((d, n), w.dtype)
    return jnp.concatenate([w[:, 1796:2820], w[:, 1284:1796], w[:, 0:512], w[:, 512:1280],
                            w[:, 1280:1284], w[:, 2820:2828], z(LANES - 12), z(H_WIDTH - COL_SMALL - LANES)], axis=1)


def _w_in_unpermute(wp):
    return jnp.concatenate([wp[:, COL_U:COL_Q], wp[:, COL_Q:COL_SMALL], wp[:, COL_SMALL:COL_SMALL + 4],
                            wp[:, COL_Z:COL_U], wp[:, COL_XBC:COL_Z], wp[:, COL_SMALL + 4:COL_SMALL + 12]], axis=1)


def _big_weights(chipw):
    W = {}
    for n, w in chipw.items():
        if n in ("ffn1_wg", "ffn1_wu", "ffn2_wg", "ffn2_wu"):
            W[n] = w
        elif n in ("ffn1_wd", "ffn2_wd", "w_out", "pe_gate_w"):
            W[n] = w.reshape(-1, D_MODEL)
        elif n == "pe_proj":
            W[n] = jnp.moveaxis(w, 0, 1).reshape(PLE_DIM, D_MODEL)
        else:
            w_in = jnp.moveaxis(w[:, :, :IN_WIDTH // N_CHIPS], 0, 1).reshape(D_MODEL, IN_WIDTH)
            W["w_in_p"] = _w_in_permute(w_in)
    return W


def _small_weights(li, small):
    g = lambda n: small[n][li]
    W = {n: g(n) for n in ("ln1_g", "ln1_b", "ln2_g", "ln2_b", "ln3_g", "ln3_b", "pe_gate_b", "lru_conv_w",
                           "ssd_conv_w")}
    for n in ("lru_conv_b", "lru_ba", "lru_bx", "lru_lambda", "ssd_conv_b", "ssd_norm_g"):
        W[n] = g(n).reshape(1, -1)
    W["lru_wa_bd"] = _block_diag(g("lru_wa")).astype(BF16)
    W["lru_wx_bd"] = _block_diag(g("lru_wx")).astype(BF16)
    W["fox_bf_vec"] = _lane_vec(g("fox_bf"), LANE_F)
    W["ssd_dtb_vec"] = _lane_vec(g("ssd_dt_bias"), LANE_DT)
    W["ssd_a_vec"] = _lane_vec(-jnp.exp(g("ssd_a_log")), LANE_DT)
    W["ssd_d_exp"] = jnp.repeat(g("ssd_d"), HEAD_DIM).reshape(1, SSD_WIDTH)
    return W


def _big_grad_by_chip(G, n):
    if n in ("ffn1_wg", "ffn1_wu", "ffn2_wg", "ffn2_wu", "pe_proj"):
        return G[n]
    if n in ("ffn1_wd", "ffn2_wd", "w_out", "pe_gate_w"):
        return G[n].reshape(N_CHIPS, -1, D_MODEL)
    share = IN_WIDTH // N_CHIPS
    d_w_in = jnp.moveaxis(_w_in_unpermute(G["w_in_p"]).reshape(D_MODEL, N_CHIPS, share), 1, 0)
    return jnp.pad(d_w_in.astype(BF16), ((0, 0), (0, 0), (0, SHARE - share)))


def _layer_small_grads(G, W):
    out = {n: G[n] for n in ("lru_conv_w", "ssd_conv_w")}
    for n in ("ln1_g", "ln1_b", "ln2_g", "ln2_b", "ln3_g", "ln3_b", "pe_gate_b", "lru_conv_b", "lru_ba", "lru_bx",
              "lru_lambda", "ssd_conv_b", "ssd_norm_g"):
        out[n] = G[n].reshape(-1)
    out["lru_wa"] = _block_diag_extract(G["lru_wa_bd"])
    out["lru_wx"] = _block_diag_extract(G["lru_wx_bd"])
    out["fox_bf"] = G["fox_bf_vec"][0, LANE_F:LANE_F + ATT_HEADS]
    out["ssd_dt_bias"] = G["ssd_dtb_vec"][0, LANE_DT:LANE_DT + SSD_HEADS]
    out["ssd_a_log"] = G["ssd_a_vec"][0, LANE_DT:LANE_DT + SSD_HEADS] * W["ssd_a_vec"][0, LANE_DT:LANE_DT + SSD_HEADS]
    out["ssd_d"] = G["ssd_d_exp"].reshape(SSD_HEADS, HEAD_DIM).sum(axis=1)
    return out


WEIGHTS = ['ln1_g', 'ln1_b', 'ffn1_wg', 'ffn1_wu', 'ffn1_wd', 'w_in', 'lru_conv_w', 'lru_conv_b', 'lru_wa', 'lru_ba',
           'lru_wx', 'lru_bx', 'lru_lambda', 'fox_bf', 'ssd_conv_w', 'ssd_conv_b', 'ssd_dt_bias', 'ssd_a_log', 'ssd_d',
           'ssd_norm_g', 'w_out', 'ln2_g', 'ln2_b', 'ffn2_wg', 'ffn2_wu', 'ffn2_wd', 'ln3_g', 'ln3_b', 'pe_proj',
           'pe_gate_w', 'pe_gate_b']
FIRST = ((("ffn1_wg", "ffn1_wu"), 1),)
NEXT = ((("w_in",), 1),
        (("ffn1_wd",), 0))
EARLY = FIRST + NEXT
LATE = ((("ffn2_wg", "ffn2_wu"), 1),
        (("ffn2_wd",), 0),
        (("w_out", "pe_gate_w"), None),
        (("pe_proj",), None))
BIG = {n: pad for names, pad in EARLY + LATE for n in names}
SMALL_SHARDED = {'lru_conv_w': 2, 'ssd_conv_w': 2}
PACK_COLS = 1024


def _unshard(seg, axis):
    moved = jnp.moveaxis(seg, 0, axis)
    shp = list(moved.shape)
    shp[axis:axis + 2] = [shp[axis] * shp[axis + 1]]
    return moved.reshape(shp)


def _pad_axis(a, axis, size):
    if axis is None or a.shape[axis] == size:
        return a
    pads = [(0, 0)] * a.ndim
    pads[axis] = (0, size - a.shape[axis])
    return jnp.pad(a, pads)


def _pack(arrs, dtype, cols):
    flat = jnp.concatenate([a.astype(dtype).reshape(-1) for a in arrs])
    pad = (-flat.shape[0]) % cols
    if pad:
        flat = jnp.concatenate([flat, jnp.zeros((pad,), dtype)])
    return flat.reshape(-1, cols)


def _unpack(flat, shapes):
    out, off = [], 0
    for s in shapes:
        n = math.prod(s)
        out.append(flat[off:off + n].reshape(s))
        off += n
    return out


def kernel(x, p, ln1_g, ln1_b, ffn1_wg, ffn1_wu, ffn1_wd, w_in, lru_conv_w, lru_conv_b, lru_wa, lru_ba, lru_wx, lru_bx, lru_lambda, fox_bf, ssd_conv_w, ssd_conv_b, ssd_dt_bias, ssd_a_log, ssd_d, ssd_norm_g, w_out, ln2_g, ln2_b, ffn2_wg, ffn2_wu, ffn2_wd, ln3_g, ln3_b, pe_proj, pe_gate_w, pe_gate_b, loss_target, m_ln1_g, m_ln1_b, m_ffn1_wg, m_ffn1_wu, m_ffn1_wd, m_w_in, m_lru_conv_w, m_lru_conv_b, m_lru_wa, m_lru_ba, m_lru_wx, m_lru_bx, m_lru_lambda, m_fox_bf, m_ssd_conv_w, m_ssd_conv_b, m_ssd_dt_bias, m_ssd_a_log, m_ssd_d, m_ssd_norm_g, m_w_out, m_ln2_g, m_ln2_b, m_ffn2_wg, m_ffn2_wu, m_ffn2_wd, m_ln3_g, m_ln3_b, m_pe_proj, m_pe_gate_w, m_pe_gate_b, v_ln1_g, v_ln1_b, v_ffn1_wg, v_ffn1_wu, v_ffn1_wd, v_w_in, v_lru_conv_w, v_lru_conv_b, v_lru_wa, v_lru_ba, v_lru_wx, v_lru_bx, v_lru_lambda, v_fox_bf, v_ssd_conv_w, v_ssd_conv_b, v_ssd_dt_bias, v_ssd_a_log, v_ssd_d, v_ssd_norm_g, v_w_out, v_ln2_g, v_ln2_b, v_ffn2_wg, v_ffn2_wu, v_ffn2_wd, v_ln3_g, v_ln3_b, v_pe_proj, v_pe_gate_w, v_pe_gate_b):
    args = locals()
    w_loc = {n: args[n] for n in WEIGHTS}
    m_loc = {n: args["m_" + n] for n in WEIGHTS}
    v_loc = {n: args["v_" + n] for n in WEIGHTS}
    chip = 2 * lax.axis_index("x") + lax.axis_index("y")
    core = lax.axis_index("c")
    big = list(BIG)
    small_sh = list(SMALL_SHARDED)
    small_rep = [n for n in WEIGHTS if n not in BIG and n not in SMALL_SHARDED]

    def srcs_of(li, groups):
        return [jnp.stack([_pad_axis(w_loc[n][li].astype(BF16), pad, SHARE) for n in names]) for names, pad in groups]

    def gather_comm(li, groups, base=0):
        return _gather_layer_comm(srcs_of(li, groups), li, base)

    def chip_weights(gathered, groups):
        return _big_weights({n: g[:, j] for (names, _), g in zip(groups, gathered) for j, n in enumerate(names)})

    def pair_sums(G, groups, tag):
        gs = [jnp.stack([_big_grad_by_chip(G, n) for n in names]) for names, _ in groups]
        flat = [g.reshape((-1,) + g.shape[2:]) for g in gs]
        theirs = _exchange(flat, ("c",), swap=True, name=f"reduce_cores_{tag}")
        return [_sum_pair(f, r, BF16, name=f"reduce_cores_sum_{tag}_{gi}").reshape(g.shape)
                for gi, (f, r, g) in enumerate(zip(flat, theirs, gs))]

    def finish_reduce(quad, sums, li, groups, tag):
        quad = [lax.dynamic_update_index_in_dim(q, lax.dynamic_index_in_dim(s, chip, 1, keepdims=False), chip, 0)
                for q, s in zip(quad, sums)]
        red = [_sum_slots(q.reshape(N_CHIPS, -1, q.shape[-1]), F32,
                          name=f"reduce_chips_sum_{tag}_{gi}").reshape(q.shape[1:]) for gi, q in enumerate(quad)]
        theirs = _exchange(red, ("c",), swap=True, name=f"reduce_share_{tag}")
        out = {}
        for (names, _), r, rv in zip(groups, red, theirs):
            both = jnp.where(core == li, r, rv)
            for j, n in enumerate(names):
                out[n] = both[j]
        return out

    everything = EARLY + LATE
    first0 = _run_comm(gather_comm(0, FIRST), name="gather_w_l0")
    small = {n: w_loc[n] for n in small_rep}
    spack = _pack([w_loc[n] for n in small_sh], F32, LANES)
    (sg,) = _exchange([spack[None]], ("x", "y"), name="gather_conv_w")
    for n, seg in zip(small_sh, _unpack_rows(sg.reshape(N_CHIPS, -1), [w_loc[n].shape for n in small_sh])):
        small[n] = _unshard(seg, SMALL_SHARDED[n])

    W0 = {**_small_weights(0, small), **chip_weights(first0, FIRST)}
    late0_comm = gather_comm(0, LATE)
    early1 = []

    def in_attention0(got):
        early1.extend(got[len(LATE):])
        return chip_weights(got[:len(LATE)], LATE)

    xs = x[0]
    xs, xb, sv0, W0 = _layer_fwd(
        0, xs, xs.astype(BF16), p[0, 0].astype(BF16), W0,
        up=(gather_comm(0, NEXT), lambda got: chip_weights(got, NEXT)),
        att=(_merge_comms([late0_comm, gather_comm(1, EARLY, base=late0_comm.n_sems)]), in_attention0))
    W1 = {**_small_weights(1, small), **chip_weights(early1, EARLY)}
    xs, _, sv1, W1 = _layer_fwd(1, xs, xb, p[1, 0].astype(BF16), W1,
                                att=(gather_comm(1, LATE), lambda got: chip_weights(got, LATE)))
    dx, loss = _loss_kernel(xs, loss_target[0], name="loss")
    loss = lax.psum(loss[0, 0], MESH_AXES)
    dx, G1, _, _ = _layer_bwd(1, dx, sv1, W1)
    sums1 = pair_sums(G1, everything, "l1")
    comm1 = _reduce_chips_comm(sums1, 1)
    late_sums, early_sums = [], []

    def late0(G):
        late_sums.extend(pair_sums(G, LATE, "l0_late"))
        return _reduce_chips_comm(late_sums, 0, base=comm1.n_sems)

    def last0(G):
        early_sums.extend(pair_sums(G, EARLY, "l0"))
        return _reduce_chips_comm(early_sums, 0)

    grad_x, G0, quads, quads0 = _layer_bwd(0, dx, sv0, W0, comm=comm1, late=late0, last=last0)

    n1 = len(comm1.out_shapes)
    red = [{**finish_reduce(quads[n1:], late_sums, 0, LATE, "l0_late"),
            **finish_reduce(quads0, early_sums, 0, EARLY, "l0")},
           finish_reduce(quads[:n1], sums1, 1, everything, "l1")]
    g_red = {}
    for n in big:
        g = jnp.stack([red[li][n] for li in range(DEPTH)])
        g_red[n] = g[tuple(slice(0, s) for s in w_loc[n].shape)]
    small_l = [_layer_small_grads(G0, W0), _layer_small_grads(G1, W1)]
    g_small = {n: jnp.stack([small_l[li][n] for li in range(DEPTH)]) for n in small_l[0]}
    small_all = small_rep + small_sh
    sgp = _pack([g_small[n] for n in small_all], F32, PACK_COLS)
    (sall,) = _exchange([sgp[None]], MESH_AXES, name="reduce_small")
    sred = _sum_slots(sall.reshape((2 ** len(MESH_AXES),) + sgp.shape), F32, name="reduce_small_sum").reshape(-1)
    for n, g in zip(small_all, _unpack(sred, [g_small[n].shape for n in small_all])):
        if n in SMALL_SHARDED:
            width = w_loc[n].shape[-1]
            g = lax.dynamic_slice_in_dim(g, chip * width, width, axis=SMALL_SHARDED[n])
        g_red[n] = g

    delta, new_m, new_v = {}, {}, {}
    for n in big:
        delta[n], new_m[n], new_v[n] = _adamw(w_loc[n], g_red[n], m_loc[n], v_loc[n], name="adamw_" + n)
    shapes = [w_loc[n].shape for n in small_all]
    packs = [_pack([d[n] for n in small_all], F32, LANES) for d in (w_loc, g_red, m_loc, v_loc)]
    outs = _adamw(*packs, name="adamw_small")
    for d, o in zip((delta, new_m, new_v), outs):
        for n, a in zip(small_all, _unpack(o.reshape(-1), shapes)):
            d[n] = a
    return (loss, grad_x[None], *[g_red[n] for n in WEIGHTS], *[delta[n] for n in WEIGHTS],
            *[new_m[n] for n in WEIGHTS], *[new_v[n] for n in WEIGHTS])


def _unpack_rows(gathered, shapes):
    out, off = [], 0
    for s in shapes:
        n = math.prod(s)
        out.append(gathered[:, off:off + n].reshape((N_CHIPS,) + tuple(s)))
        off += n
    return out
```

```python
import math

import jax
import jax.numpy as jnp
from jax import lax
from jax.experimental import pallas as pl
from jax.experimental.pallas import tpu as pltpu

F32 = jnp.float32
BF16 = jnp.bfloat16

D_MODEL = 1024
DEPTH = 2
PLE_DIM = 256
HEAD_DIM = 64
LRU_WIDTH = 256
LRU_HEADS = 4
LRU_C = 8.0
CONV_K = 4
ATT_WIDTH = 256
ATT_HEADS = 4
SSD_WIDTH = 512
SSD_HEADS = 8
SSD_GROUPS = 2
SSD_STATE = 128
SSD_CHUNK = 128
SSD_CONV_DIM = 1024
ALPHA = (2.0 * DEPTH) ** 0.25
LN_EPS = 1e-5
RMS_EPS = 1e-5
IN_WIDTH = 2828
ADAM_LR = 0.001
ADAM_B1 = 0.9
ADAM_B2 = 0.999
ADAM_EPS = 1e-08
ADAM_WD = 0.01
ADAM_STEP = 10

H_WIDTH = 3072
COL_XBC, COL_Z, COL_U, COL_G, COL_Q, COL_K, COL_V, COL_SMALL = 0, 1024, 1536, 1792, 2048, 2304, 2560, 2816
LANE_F = 0
LANE_DT = 4
LANES = 128
SUBLANES = 8
NEG = -1e30

VMEM_LIMIT = 48 * 1024 * 1024

N_CHIPS = 4
MESH_AXES = ("x", "y", "c")
SHARE = 768


def _params(n):
    return pltpu.CompilerParams(dimension_semantics=("arbitrary",) * n, vmem_limit_bytes=VMEM_LIMIT)


def _pick(n, cands):
    for c in cands:
        if n % c == 0:
            return c
    return n


def _iota(shape, dim):
    return lax.broadcasted_iota(jnp.int32, shape, dim)


def _shift_down(x, s, prev8):
    if s == 0:
        return x
    r = pltpu.roll(x, s, 0)
    pr = pltpu.roll(prev8, s, 0)
    head = jnp.where(_iota(pr.shape, 0) < s, pr, r[:SUBLANES])
    return jnp.concatenate([head, r[SUBLANES:]], axis=0)


def _shift_up(x, s, next8):
    if s == 0:
        return x
    n = x.shape[0]
    r = pltpu.roll(x, n - s, 0)
    nr = pltpu.roll(next8, SUBLANES - s, 0)
    tail = jnp.where(_iota(nr.shape, 0) >= SUBLANES - s, nr, r[n - SUBLANES:])
    return jnp.concatenate([r[:n - SUBLANES], tail], axis=0)


def _scan_fwd(a, b):
    n = a.shape[0]
    row = _iota(a.shape, 0)
    d = 1
    while d < n:
        keep = row >= d
        a_s = jnp.where(keep, pltpu.roll(a, d, 0), 1.0)
        b_s = jnp.where(keep, pltpu.roll(b, d, 0), 0.0)
        b = a * b_s + b
        a = a * a_s
        d *= 2
    return a, b


def _scan_bwd(a, b):
    n = a.shape[0]
    row = _iota(a.shape, 0)
    d = 1
    while d < n:
        keep = row < n - d
        a_s = jnp.where(keep, pltpu.roll(a, n - d, 0), 1.0)
        b_s = jnp.where(keep, pltpu.roll(b, n - d, 0), 0.0)
        b = a * b_s + b
        a = a * a_s
        d *= 2
    return a, b


def _cumsum_rows(x, reverse=False):
    n = x.shape[0]
    row = _iota(x.shape, 0)
    d = 1
    while d < n:
        if reverse:
            x = x + jnp.where(row < n - d, pltpu.roll(x, n - d, 0), 0.0)
        else:
            x = x + jnp.where(row >= d, pltpu.roll(x, d, 0), 0.0)
        d *= 2
    return x


def _col(x, lane):
    return jnp.sum(jnp.where(_iota(x.shape, 1) == lane, x, 0.0), axis=1, keepdims=True)


def _row(x, r):
    return jnp.sum(jnp.where(_iota(x.shape, 0) == r, x, 0.0), axis=0, keepdims=True)


def _sigmoid(x):
    return jax.nn.sigmoid(x)


def _softplus(x):
    return jnp.maximum(x, 0.0) + jnp.log(1.0 + jnp.exp(-jnp.abs(x)))


def _gelu_and_grad(x):
    c0 = math.sqrt(2.0 / math.pi)
    inner = c0 * (x + 0.044715 * x * x * x)
    t = jnp.tanh(inner)
    g = 0.5 * x * (1.0 + t)
    dg = 0.5 * (1.0 + t) + 0.5 * x * (1.0 - t * t) * c0 * (1.0 + 3.0 * 0.044715 * x * x)
    return g, dg


def _dot(a, b, ca, cb):
    return lax.dot_general(a, b, (((ca,), (cb,)), ((), ())), preferred_element_type=F32)


def _conv_taps(xr, prev8, w, bias):
    y = bias + w[CONV_K - 1:CONV_K, :] * xr
    for j in range(CONV_K - 1):
        y = y + w[j:j + 1, :] * _shift_down(xr, CONV_K - 1 - j, prev8)
    return y


def _conv_taps_bwd(dy, next8, w, xr):
    dx = None
    dws = []
    for j in range(CONV_K):
        sh = _shift_up(dy, CONV_K - 1 - j, next8)
        term = w[j:j + 1, :] * sh
        dx = term if dx is None else dx + term
        dws.append(jnp.sum(sh * xr, axis=0, keepdims=True))
    return dx, jnp.concatenate(dws, axis=0)


def _head_expand(v, lane0, nheads, width):
    rows = v.shape[0]
    colhead = _iota((rows, width), 1) // HEAD_DIM
    out = jnp.zeros((rows, width), F32)
    for h in range(nheads):
        out = jnp.where(colhead == h, _col(v, lane0 + h), out)
    return out


def _head_reduce(x, lane0, nheads):
    rows = x.shape[0]
    colhead = _iota(x.shape, 1) // HEAD_DIM
    lane = _iota((rows, LANES), 1)
    out = jnp.zeros((rows, LANES), F32)
    for h in range(nheads):
        s = jnp.sum(jnp.where(colhead == h, x, 0.0), axis=1, keepdims=True)
        out = jnp.where(lane == lane0 + h, s, out)
    return out


def _mm(a, b, *, ta=False, tb=False, scale=1.0, out_dtype=F32, chip_cols=False, name):
    if ta:
        kk, m = a.shape
    else:
        m, kk = a.shape
    n = b.shape[0] if tb else b.shape[1]
    tm = _pick(m, (1024, 512, 256, 128))
    tk = _pick(kk, (1024, 768, 512, 256, 128))
    nk = kk // tk
    dn_a = 0 if ta else 1
    dn_b = 1 if tb else 0
    share = n // N_CHIPS
    if chip_cols:
        tn = n
        out_spec = pl.BlockSpec((N_CHIPS, tm, share), lambda i, j, k: (0, i, 0))
        out_shape = jax.ShapeDtypeStruct((N_CHIPS, m, share), out_dtype)
    else:
        tn = _pick(n, (1024, 768, 512, 256, 128))
        out_spec = pl.BlockSpec((tm, tn), lambda i, j, k: (i, j))
        out_shape = jax.ShapeDtypeStruct((m, n), out_dtype)

    def body(a_ref, b_ref, o_ref, acc):
        k = pl.program_id(2)

        @pl.when(k == 0)
        def _():
            acc[...] = jnp.zeros_like(acc)

        acc[...] += _dot(a_ref[...].astype(BF16), b_ref[...].astype(BF16), dn_a, dn_b)

        @pl.when(k == nk - 1)
        def _():
            if chip_cols:
                for c in range(N_CHIPS):
                    o_ref[c] = (acc[:, share * c:share * (c + 1)] * scale).astype(out_dtype)
            else:
                o_ref[...] = (acc[...] * scale).astype(out_dtype)

    a_spec = pl.BlockSpec((tk, tm), lambda i, j, k: (k, i)) if ta else pl.BlockSpec((tm, tk), lambda i, j, k: (i, k))
    b_spec = pl.BlockSpec((tn, tk), lambda i, j, k: (j, k)) if tb else pl.BlockSpec((tk, tn), lambda i, j, k: (k, j))
    return pl.pallas_call(
        body, name=name, grid=(m // tm, n // tn, nk),
        in_specs=[a_spec, b_spec],
        out_specs=out_spec, out_shape=out_shape,
        scratch_shapes=[pltpu.VMEM((tm, tn), F32)],
        compiler_params=_params(3),
    )(a, b)


def _mm_swiglu(xb, wg, wu, *, comm=None, name):
    t, d = xb.shape
    share = wg.shape[2]
    n = N_CHIPS * share
    tm = _pick(t, (512, 256, 128))
    tn = _pick(share, (768, 256, 128))
    per = share // tn

    def body(x_ref, wg_ref, wu_ref, g_ref, u_ref, a_ref):
        x = x_ref[...]
        g = _dot(x, wg_ref[...], 1, 0)
        u = _dot(x, wu_ref[...], 1, 0)
        g_ref[...] = g.astype(BF16)
        u_ref[...] = u.astype(BF16)
        a_ref[...] = (g * _sigmoid(g) * u).astype(BF16)

    o = jax.ShapeDtypeStruct((t, n), BF16)
    ospec = pl.BlockSpec((tm, tn), lambda j, i: (i, j))
    return _hosted_call(
        body, comm, (n // tn, t // tm), name=name,
        in_specs=[pl.BlockSpec((tm, d), lambda j, i: (i, 0)),
                  pl.BlockSpec((None, d, tn), lambda j, i: (j // per, 0, j % per)),
                  pl.BlockSpec((None, d, tn), lambda j, i: (j // per, 0, j % per))],
        out_specs=[ospec, ospec, ospec], out_shape=[o, o, o], scratch_shapes=[], args=[xb, wg, wu])


def _mm_swiglu_bwd(dr, wd, g, u, *, scale, name):
    t, d = dr.shape
    n = wd.shape[0]
    tm = _pick(t, (512, 256, 128))
    tn = _pick(n, (768, 256, 128))

    def body(dr_ref, wd_ref, g_ref, u_ref, dg_ref, du_ref):
        da = _dot(dr_ref[...].astype(BF16), wd_ref[...], 1, 1) * scale
        gg = g_ref[...].astype(F32)
        uu = u_ref[...].astype(F32)
        sg = _sigmoid(gg)
        dg_ref[...] = (da * uu * (sg * (1.0 + gg * (1.0 - sg)))).astype(BF16)
        du_ref[...] = (da * gg * sg).astype(BF16)

    o = jax.ShapeDtypeStruct((t, n), BF16)
    ospec = pl.BlockSpec((tm, tn), lambda j, i: (i, j))
    return pl.pallas_call(
        body, name=name, grid=(n // tn, t // tm),
        in_specs=[pl.BlockSpec((tm, d), lambda j, i: (i, 0)),
                  pl.BlockSpec((tn, d), lambda j, i: (j, 0)),
                  ospec, ospec],
        out_specs=[ospec, ospec], out_shape=[o, o],
        compiler_params=_params(2),
    )(dr, wd, g, u)


def _mm_ln(a, w, resid, gain, bias, *, rscale, mscale, name):
    t, kk = a.shape
    d = w.shape[1]
    tm = _pick(t, (512, 256, 128))
    tk = kk
    nk = kk // tk

    def body(a_ref, w_ref, r_ref, g_ref, b_ref, y_ref, yb_ref, xh_ref, rs_ref, acc):
        k = pl.program_id(1)

        @pl.when(k == 0)
        def _():
            acc[...] = jnp.zeros_like(acc)

        acc[...] += _dot(a_ref[...].astype(BF16), w_ref[...], 1, 0)

        @pl.when(k == nk - 1)
        def _():
            r = rscale * r_ref[...] + mscale * acc[...]
            mu = jnp.mean(r, axis=1, keepdims=True)
            xc = r - mu
            var = jnp.mean(xc * xc, axis=1, keepdims=True)
            rstd = lax.rsqrt(var + LN_EPS)
            xh = xc * rstd
            y = xh * g_ref[...] + b_ref[...]
            y_ref[...] = y
            yb_ref[...] = y.astype(BF16)
            xh_ref[...] = xh
            rs_ref[...] = rstd

    row = pl.BlockSpec((tm, d), lambda i, k: (i, 0))
    vec = pl.BlockSpec((1, d), lambda i, k: (0, 0))
    return pl.pallas_call(
        body, name=name, grid=(t // tm, nk),
        in_specs=[pl.BlockSpec((tm, tk), lambda i, k: (i, k)),
                  pl.BlockSpec((tk, d), lambda i, k: (k, 0)), row, vec, vec],
        out_specs=[row, row, row, pl.BlockSpec((tm, 1), lambda i, k: (i, 0))],
        out_shape=[jax.ShapeDtypeStruct((t, d), F32), jax.ShapeDtypeStruct((t, d), BF16),
                   jax.ShapeDtypeStruct((t, d), F32), jax.ShapeDtypeStruct((t, 1), F32)],
        scratch_shapes=[pltpu.VMEM((tm, d), F32)],
        compiler_params=_params(2),
    )(a, w, resid, gain.reshape(1, d), bias.reshape(1, d))


def _bwd_proj(pairs, resid, *, rscale, ln, comm=None, name):
    t, kk = pairs[0][0].shape
    d = pairs[0][1].shape[-2]
    has_ln = ln is not None
    tm = _pick(t, (512, 256, 128) if has_ln else (1024, 512, 256, 128))
    tk = _pick(pairs[0][1].shape[-1], (1024, 768, 512, 256, 128))
    nk = kk // tk
    nt = t // tm
    npair = len(pairs)

    def body(*refs):
        ab = refs[:2 * npair]
        r_ref = refs[2 * npair]
        pos = 2 * npair + 1
        if has_ln:
            xh_ref, rs_ref, g_ref = refs[pos:pos + 3]
            pos += 3
            o_ref, ob_ref, dg_ref, db_ref = refs[pos:pos + 4]
            pos += 4
        else:
            o_ref = refs[pos]
            pos += 1
        acc = refs[pos]
        i = pl.program_id(0)
        k = pl.program_id(1)

        @pl.when(k == 0)
        def _():
            acc[...] = jnp.zeros_like(acc)

        for q in range(npair):
            acc[...] += _dot(ab[2 * q][...].astype(BF16), ab[2 * q + 1][...], 1, 1)

        @pl.when(k == nk - 1)
        def _():
            dy = rscale * r_ref[...] + acc[...]
            if not has_ln:
                o_ref[...] = dy
                return
            xh = xh_ref[...]
            w = dy * g_ref[...]
            m1 = jnp.mean(w, axis=1, keepdims=True)
            m2 = jnp.mean(w * xh, axis=1, keepdims=True)
            dr = rs_ref[...] * (w - m1 - xh * m2)
            o_ref[...] = dr
            ob_ref[...] = dr.astype(BF16)

            @pl.when(i == 0)
            def _():
                dg_ref[...] = jnp.zeros_like(dg_ref)
                db_ref[...] = jnp.zeros_like(db_ref)

            dg_ref[...] += jnp.sum(dy * xh, axis=0, keepdims=True)
            db_ref[...] += jnp.sum(dy, axis=0, keepdims=True)

    row = pl.BlockSpec((tm, d), lambda i, k: (i, 0))
    vec = pl.BlockSpec((1, d), lambda i, k: (0, 0))
    in_specs, args = [], []
    for a, b in pairs:
        if b.ndim == 3:
            per = b.shape[2] // tk
            b_spec = pl.BlockSpec((None, d, tk), lambda i, k, per=per: (k // per, 0, k % per))
        else:
            b_spec = pl.BlockSpec((d, tk), lambda i, k: (0, k))
        in_specs += [pl.BlockSpec((tm, tk), lambda i, k: (i, k)), b_spec]
        args += [a, b]
    in_specs.append(row)
    args.append(resid)
    out_specs = [row]
    out_shape = [jax.ShapeDtypeStruct((t, d), F32)]
    if has_ln:
        xh, rs, gain = ln
        in_specs += [row, pl.BlockSpec((tm, 1), lambda i, k: (i, 0)), vec]
        args += [xh, rs, gain.reshape(1, d)]
        out_specs += [row, vec, vec]
        out_shape += [jax.ShapeDtypeStruct((t, d), BF16)] + [jax.ShapeDtypeStruct((1, d), F32)] * 2
    outs, got = _hosted_call(body, comm, (nt, nk), name=name, in_specs=in_specs, out_specs=out_specs,
                             out_shape=out_shape, scratch_shapes=[pltpu.VMEM((tm, d), F32)], args=args)
    return tuple(outs) if comm is None else tuple(outs) + (got,)


def _mm_pe(x3, x3b, pb, wgate, bgate, wproj, *, name):
    t, d = x3.shape
    pd = pb.shape[1]
    tm = _pick(t, (512, 256, 128))
    tn = _pick(d, (512, 256, 128))

    def body(x_ref, xb_ref, p_ref, wg_ref, bg_ref, wp_ref, y_ref, yb_ref, sg_ref, e_ref):
        sg = _sigmoid(_dot(xb_ref[...], wg_ref[...], 1, 0) + bg_ref[...])
        e = _dot(p_ref[...], wp_ref[...], 1, 0)
        y = x_ref[...] + sg * e
        y_ref[...] = y
        yb_ref[...] = y.astype(BF16)
        sg_ref[...] = sg.astype(BF16)
        e_ref[...] = e.astype(BF16)

    ospec = pl.BlockSpec((tm, tn), lambda i, j: (i, j))
    ob = jax.ShapeDtypeStruct((t, d), BF16)
    return pl.pallas_call(
        body, name=name, grid=(t // tm, d // tn),
        in_specs=[ospec, pl.BlockSpec((tm, d), lambda i, j: (i, 0)), pl.BlockSpec((tm, pd), lambda i, j: (i, 0)),
                  pl.BlockSpec((d, tn), lambda i, j: (0, j)), pl.BlockSpec((1, tn), lambda i, j: (0, j)),
                  pl.BlockSpec((pd, tn), lambda i, j: (0, j))],
        out_specs=[ospec, ospec, ospec, ospec],
        out_shape=[jax.ShapeDtypeStruct((t, d), F32), ob, ob, ob],
        compiler_params=_params(2),
    )(x3, x3b, pb, wgate, bgate.reshape(1, d), wproj)


def _pe_bwd_elem(dx4, sg, e, *, name):
    t, d = dx4.shape
    tm = _pick(t, (512, 256, 128))

    def body(dx_ref, sg_ref, e_ref, dgp_ref, de_ref, db_ref):
        dx = dx_ref[...]
        s = sg_ref[...].astype(F32)
        dgp = dx * e_ref[...].astype(F32) * s * (1.0 - s)
        dgp_ref[...] = dgp.astype(BF16)
        de_ref[...] = (dx * s).astype(BF16)

        @pl.when(pl.program_id(0) == 0)
        def _():
            db_ref[...] = jnp.zeros_like(db_ref)

        db_ref[...] += jnp.sum(dgp, axis=0, keepdims=True)

    row = pl.BlockSpec((tm, d), lambda i: (i, 0))
    ob = jax.ShapeDtypeStruct((t, d), BF16)
    return pl.pallas_call(
        body, name=name, grid=(t // tm,), in_specs=[row, row, row],
        out_specs=[row, row, pl.BlockSpec((1, d), lambda i: (0, 0))],
        out_shape=[ob, ob, jax.ShapeDtypeStruct((1, d), F32)],
        compiler_params=_params(1),
    )(dx4, sg, e)


def _loss_kernel(y, target, *, name):
    t, d = y.shape
    tm = _pick(t, (512, 256, 128))

    def body(y_ref, t_ref, dy_ref, l_ref):
        diff = y_ref[...] - t_ref[...]
        dy_ref[...] = diff * (1.0 / d)

        @pl.when(pl.program_id(0) == 0)
        def _():
            l_ref[...] = jnp.zeros_like(l_ref)

        part = jnp.sum(jnp.mean(diff * diff, axis=1, keepdims=True), axis=0, keepdims=True)
        l_ref[...] += 0.5 * part

    row = pl.BlockSpec((tm, d), lambda i: (i, 0))
    return pl.pallas_call(
        body, name=name, grid=(t // tm,), in_specs=[row, row],
        out_specs=[row, pl.BlockSpec((1, 1), lambda i: (0, 0))],
        out_shape=[jax.ShapeDtypeStruct((t, d), F32), jax.ShapeDtypeStruct((1, 1), F32)],
        compiler_params=_params(1),
    )(y, target)


LRU_TM = 256


def _lru_gate_terms(r, lam):
    sp = _softplus(-lam)
    la = -LRU_C * r * sp
    a = jnp.exp(la)
    em = jnp.tanh(la) * (jnp.exp(2.0 * la) + 1.0)
    s = jnp.sqrt(-em)
    return la, a, s, sp


def _lru_fwd(hbuf, conv_w, conv_b, wa, ba, wx, bx, lam, *, name):
    t = hbuf.shape[0]
    w = LRU_WIDTH
    tm = _pick(t, (LRU_TM, 128))
    cu, cg = COL_U // w, COL_G // w
    hb = tm // SUBLANES

    def body(u_ref, up_ref, g_ref, cw_ref, cb_ref, wa_ref, ba_ref, wx_ref, bx_ref, lam_ref,
             y_ref, u_out, r_out, i_out, a_out, h_out, carry):
        i = pl.program_id(0)

        @pl.when(i == 0)
        def _():
            carry[...] = jnp.zeros_like(carry)

        prev = jnp.where(i == 0, 0.0, up_ref[...])
        u = _conv_taps(u_ref[...], prev, cw_ref[...], cb_ref[...])
        ub = u.astype(BF16)
        r = _sigmoid(_dot(ub, wa_ref[...], 1, 0) + ba_ref[...])
        ig = _sigmoid(_dot(ub, wx_ref[...], 1, 0) + bx_ref[...])
        _, a, s, _ = _lru_gate_terms(r, lam_ref[...])
        b = s * (ig * u)
        acum, hs = _scan_fwd(a, b)
        h = hs + acum * carry[0:1, :]
        carry[...] = jnp.broadcast_to(h[tm - 1:tm, :], carry.shape)
        gl, _ = _gelu_and_grad(g_ref[...])
        y_ref[...] = h * gl
        u_out[...] = u
        r_out[...] = r
        i_out[...] = ig
        a_out[...] = a
        h_out[...] = h

    row = pl.BlockSpec((tm, w), lambda i: (i, 0))
    vec = pl.BlockSpec((1, w), lambda i: (0, 0))
    mat = pl.BlockSpec((w, w), lambda i: (0, 0))
    o = jax.ShapeDtypeStruct((t, w), F32)
    return pl.pallas_call(
        body, name=name, grid=(t // tm,),
        in_specs=[pl.BlockSpec((tm, w), lambda i: (i, cu)),
                  pl.BlockSpec((SUBLANES, w), lambda i: (jnp.maximum(i * hb - 1, 0), cu)),
                  pl.BlockSpec((tm, w), lambda i: (i, cg)),
                  pl.BlockSpec((CONV_K, w), lambda i: (0, 0)), vec, mat, vec, mat, vec, vec],
        out_specs=[row] * 6, out_shape=[o] * 6,
        scratch_shapes=[pltpu.VMEM((SUBLANES, w), F32)],
        compiler_params=_params(1),
    )(hbuf, hbuf, hbuf, conv_w, conv_b, wa, ba, wx, bx, lam)


def _lru_bwd(dymix, hbuf, u, r, ig, a, h, conv_w, wa, wx, lam, *, name):
    t = hbuf.shape[0]
    w = LRU_WIDTH
    tm = _pick(t, (LRU_TM, 128))
    nb = t // tm
    cu, cg = COL_U // w, COL_G // w
    hb = tm // SUBLANES
    last8 = t // SUBLANES - 1

    def body(dy_ref, ur_ref, g_ref, u_ref, r_ref, i_ref, a_ref, an_ref, h_ref, hp_ref,
             cw_ref, wa_ref, wx_ref, lam_ref,
             dur_ref, dgr_ref, dcw_ref, dcb_ref, dwa_ref, dba_ref, dwx_ref, dbx_ref, dlam_ref,
             lcarry, dnext):
        i = pl.program_id(0)
        ib = nb - 1 - i

        @pl.when(i == 0)
        def _():
            lcarry[...] = jnp.zeros_like(lcarry)
            dnext[...] = jnp.zeros_like(dnext)
            for ref in (dcw_ref, dcb_ref, dwa_ref, dba_ref, dwx_ref, dbx_ref, dlam_ref):
                ref[...] = jnp.zeros_like(ref)

        dy = dy_ref[...]
        hh = h_ref[...]
        av = a_ref[...]
        uu = u_ref[...]
        rr = r_ref[...]
        ii = i_ref[...]
        lam_v = lam_ref[...]
        gl, dgl = _gelu_and_grad(g_ref[...])
        dgr_ref[...] = (dy * hh * dgl).astype(BF16)
        dh_out = dy * gl
        a_next = _shift_up(av, 1, jnp.where(ib == nb - 1, 0.0, an_ref[...]))
        acum, ls = _scan_bwd(a_next, dh_out)
        lam_adj = ls + acum * lcarry[0:1, :]
        lcarry[...] = jnp.broadcast_to(lam_adj[0:1, :], lcarry.shape)
        h_prev = _shift_down(hh, 1, jnp.where(ib == 0, 0.0, hp_ref[...]))
        da = lam_adj * h_prev
        _, a2, s, sp = _lru_gate_terms(rr, lam_v)
        d_igu = lam_adj * s
        ds = lam_adj * ii * uu
        dla = da * a2 - ds * (a2 * a2) / s
        dr = dla * (-LRU_C * sp)
        dlam_ref[...] += jnp.sum(dla * (LRU_C * rr * _sigmoid(-lam_v)), axis=0, keepdims=True)
        dpre_r = dr * rr * (1.0 - rr)
        dpre_i = d_igu * uu * ii * (1.0 - ii)
        prb = dpre_r.astype(BF16)
        pib = dpre_i.astype(BF16)
        ub = uu.astype(BF16)
        du = d_igu * ii + _dot(prb, wa_ref[...], 1, 1) + _dot(pib, wx_ref[...], 1, 1)
        dwa_ref[...] += _dot(ub, prb, 0, 0)
        dwx_ref[...] += _dot(ub, pib, 0, 0)
        dba_ref[...] += jnp.sum(dpre_r, axis=0, keepdims=True)
        dbx_ref[...] += jnp.sum(dpre_i, axis=0, keepdims=True)
        dur, dws = _conv_taps_bwd(du, dnext[...], cw_ref[...], ur_ref[...])
        dur_ref[...] = dur.astype(BF16)
        dcw_ref[...] += dws
        dcb_ref[...] += jnp.sum(du, axis=0, keepdims=True)
        dnext[...] = du[:SUBLANES]

    def rowspec(col):
        return pl.BlockSpec((tm, w), lambda i: (nb - 1 - i, col))

    row = rowspec(0)
    nxt = pl.BlockSpec((SUBLANES, w), lambda i: (jnp.minimum((nb - i) * hb, last8), 0))
    prv = pl.BlockSpec((SUBLANES, w), lambda i: (jnp.maximum((nb - 1 - i) * hb - 1, 0), 0))
    vec = pl.BlockSpec((1, w), lambda i: (0, 0))
    mat = pl.BlockSpec((w, w), lambda i: (0, 0))
    cw = pl.BlockSpec((CONV_K, w), lambda i: (0, 0))
    o = jax.ShapeDtypeStruct((t, w), BF16)
    v1 = jax.ShapeDtypeStruct((1, w), F32)
    m1 = jax.ShapeDtypeStruct((w, w), F32)
    return pl.pallas_call(
        body, name=name, grid=(nb,),
        in_specs=[rowspec(0), rowspec(cu), rowspec(cg), row, row, row, row, nxt, row, prv, cw, mat, mat, vec],
        out_specs=[row, row, cw, vec, mat, vec, mat, vec, vec],
        out_shape=[o, o, jax.ShapeDtypeStruct((CONV_K, w), F32), v1, m1, v1, m1, v1, v1],
        scratch_shapes=[pltpu.VMEM((SUBLANES, w), F32), pltpu.VMEM((SUBLANES, w), F32)],
        compiler_params=_params(1),
    )(dymix, hbuf, hbuf, u, r, ig, a, a, h, h, conv_w, wa, wx, lam)


FOX_T = 512
FOX_PREP_TM = 256


def _log_sigmoid(x):
    return jnp.minimum(x, 0.0) - jnp.log(1.0 + jnp.exp(-jnp.abs(x)))


def _fox_prep(hbuf, bf_vec, *, name):
    t = hbuf.shape[0]
    tm = _pick(t, (FOX_PREP_TM, 128))
    cs = COL_SMALL // LANES

    def body(s_ref, b_ref, eq_ref, ek_ref, carry):
        i = pl.program_id(0)

        @pl.when(i == 0)
        def _():
            carry[...] = jnp.zeros_like(carry)

        lf = _log_sigmoid(s_ref[...] + b_ref[...])
        f = _cumsum_rows(lf) + carry[0:1, :]
        carry[...] = jnp.broadcast_to(f[tm - 1:tm, :], carry.shape)
        lane = _iota((tm, LANES), 1)
        for h in range(ATT_HEADS):
            base = HEAD_DIM * (1 - h % 2)
            fh = _col(f, h)
            hi = fh.astype(BF16).astype(F32)
            mid = (fh - hi).astype(BF16).astype(F32)
            lo = fh - hi - mid
            terms = jnp.where(lane == base, hi, jnp.where(lane == base + 1, mid, jnp.where(lane == base + 2, lo, 0.0)))
            terms_k = jnp.where(lane == base + 3, -hi,
                                jnp.where(lane == base + 4, -mid, jnp.where(lane == base + 5, -lo, 0.0)))
            ones_q = ((lane >= base + 3) & (lane < base + 6)).astype(F32)
            ones_k = ((lane >= base) & (lane < base + 3)).astype(F32)
            eq_ref[:, LANES * h:LANES * (h + 1)] = (terms + ones_q).astype(BF16)
            ek_ref[:, LANES * h:LANES * (h + 1)] = (terms_k + ones_k).astype(BF16)

    ospec = pl.BlockSpec((tm, ATT_HEADS * LANES), lambda i: (i, 0))
    o = jax.ShapeDtypeStruct((t, ATT_HEADS * LANES), BF16)
    return pl.pallas_call(
        body, name=name, grid=(t // tm,),
        in_specs=[pl.BlockSpec((tm, LANES), lambda i: (i, cs)), pl.BlockSpec((1, LANES), lambda i: (0, 0))],
        out_specs=[ospec, ospec], out_shape=[o, o],
        scratch_shapes=[pltpu.VMEM((SUBLANES, LANES), F32)],
        compiler_params=_params(1),
    )(hbuf, bf_vec)


def _fox_post(dfc, hbuf, bf_vec, *, name):
    t = hbuf.shape[0]
    tm = _pick(t, (FOX_PREP_TM, 128))
    nb = t // tm
    cs = COL_SMALL // LANES

    def body(df_ref, s_ref, b_ref, o_ref, db_ref, carry):
        i = pl.program_id(0)

        @pl.when(i == 0)
        def _():
            carry[...] = jnp.zeros_like(carry)
            db_ref[...] = jnp.zeros_like(db_ref)

        dlf = _cumsum_rows(df_ref[...], reverse=True) + carry[0:1, :]
        carry[...] = jnp.broadcast_to(dlf[0:1, :], carry.shape)
        dl = dlf * _sigmoid(-(s_ref[...] + b_ref[...]))
        dl = jnp.where(_iota(dl.shape, 1) < ATT_HEADS, dl, 0.0)
        o_ref[...] = dl
        db_ref[...] += jnp.sum(dl, axis=0, keepdims=True)

    vec = pl.BlockSpec((1, LANES), lambda i: (0, 0))
    return pl.pallas_call(
        body, name=name, grid=(nb,),
        in_specs=[pl.BlockSpec((tm, LANES), lambda i: (nb - 1 - i, 0)),
                  pl.BlockSpec((tm, LANES), lambda i: (nb - 1 - i, cs)), vec],
        out_specs=[pl.BlockSpec((tm, LANES), lambda i: (nb - 1 - i, 0)), vec],
        out_shape=[jax.ShapeDtypeStruct((t, LANES), F32), jax.ShapeDtypeStruct((1, LANES), F32)],
        scratch_shapes=[pltpu.VMEM((SUBLANES, LANES), F32)],
        compiler_params=_params(1),
    )(dfc, hbuf, bf_vec)


def _fox_masks(i, j, tq):
    row = i * tq + _iota((tq, tq), 0)
    col = j * tq + _iota((tq, tq), 1)
    lane = _iota((1, LANES), 1)
    return col <= row, (lane < HEAD_DIM, lane >= HEAD_DIM)


def _hosting(body, n_in, n_out, n_scratch, comm, grid):
    na, no = len(comm.arrays), len(comm.out_shapes)

    def hosted(*refs):
        o0 = n_in + na
        s0 = o0 + n_out + no
        cargs = (refs[n_in:o0], refs[o0 + n_out:s0]) + tuple(refs[s0 + n_scratch:])
        a, b = pl.program_id(0), pl.program_id(1)

        @pl.when((a == 0) & (b == 0))
        def _():
            comm.start(*cargs)

        @pl.when((a == grid[0] - 1) & (b == 0))
        def _():
            comm.middle(*cargs)

        body(*refs[:n_in], *refs[o0:o0 + n_out], *refs[s0:s0 + n_scratch])

        @pl.when((a == grid[0] - 1) & (b == grid[1] - 1))
        def _():
            comm.finish(*cargs)

    return hosted


def _hosted_call(body, comm, grid, *, name, in_specs, out_specs, out_shape, scratch_shapes, args):
    n_out = len(out_shape)
    if comm is not None:
        cin, cout, sems = comm.specs()
        body = _hosting(body, len(in_specs), n_out, len(scratch_shapes), comm, grid)
        in_specs, out_specs = in_specs + cin, out_specs + cout
        out_shape, scratch_shapes, args = out_shape + comm.out_shapes, scratch_shapes + sems, args + list(comm.arrays)
    outs = pl.pallas_call(body, name=name, grid=grid, in_specs=in_specs, out_specs=out_specs,
                          out_shape=out_shape, scratch_shapes=scratch_shapes, compiler_params=_params(2))(*args)
    return outs[:n_out], outs[n_out:]


def _merge_comms(comms):
    comms = [c for c in comms if c is not None]
    if len(comms) <= 1:
        return comms[0] if comms else None

    def both(which):
        def run(ins, outs, ssem, rsem):
            ia = io = 0
            for c in comms:
                na, no = len(c.arrays), len(c.out_shapes)
                getattr(c, which)(ins[ia:ia + na], outs[io:io + no], ssem, rsem)
                ia, io = ia + na, io + no
        return run

    spans = sorted((c.base, c.base + c.n_own) for c in comms)
    assert all(a[1] <= b[0] for a, b in zip(spans, spans[1:])), "semaphore ranges overlap"
    return _Comm(sum((list(c.arrays) for c in comms), []), sum((list(c.out_shapes) for c in comms), []),
                 spans[-1][1], both("start"), both("finish"), middle=both("middle"))


def _fox_fwd(hbuf, eq, ek, *, comm=None, name):
    t = hbuf.shape[0]
    w = ATT_WIDTH
    tq = _pick(t, (FOX_T, 256, 128))
    nq = t // tq
    cq, ck, cv = COL_Q // w, COL_K // w, COL_V // w

    def body(q_ref, k_ref, v_ref, eq_ref, ek_ref, o_ref, lse_ref, m_s, l_s, acc_s):
        i = pl.program_id(0)
        j = pl.program_id(1)

        @pl.when(j == 0)
        def _():
            m_s[...] = jnp.full_like(m_s, NEG)
            l_s[...] = jnp.zeros_like(l_s)
            acc_s[...] = jnp.zeros_like(acc_s)

        def step(diagonal):
            _, hms = _fox_masks(i, j, tq)
            keys_first = (j * tq + _iota((tq, tq), 0)) <= (i * tq + _iota((tq, tq), 1))
            half = _iota((LANES, 1), 0)
            hrows = (half < HEAD_DIM, half >= HEAD_DIM)
            m_all = m_s[...]
            l_all = l_s[...]
            acc_old = [acc_s[LANES * pr:LANES * (pr + 1), :] for pr in range(2)]
            m_out, l_out, acc_out = [], [], []
            for pr in range(2):
                sl = slice(LANES * pr, LANES * (pr + 1))
                qp = q_ref[:, sl]
                kp = k_ref[:, sl]
                vt = v_ref[:, sl].T.astype(BF16)
                acc = acc_old[pr]
                for hh in range(2):
                    h = 2 * pr + hh
                    hsl = slice(LANES * h, LANES * (h + 1))
                    qm = jnp.where(hms[hh], (qp * (HEAD_DIM ** -0.5)).astype(BF16), eq_ref[:, hsl])
                    km = jnp.where(hms[hh], kp.astype(BF16), ek_ref[:, hsl])
                    st = _dot(km, qm, 1, 1)
                    if diagonal:
                        st = jnp.where(keys_first, st, NEG)
                    m_old = m_all[h:h + 1, :]
                    m_new = jnp.maximum(m_old, jnp.max(st, axis=0, keepdims=True))
                    alpha = jnp.exp(m_old - m_new)
                    pt = jnp.exp(st - m_new)
                    l_out.append(alpha * l_all[h:h + 1, :] + jnp.sum(pt, axis=0, keepdims=True))
                    m_out.append(m_new)
                    pv = _dot(vt, pt.astype(BF16), 1, 0)
                    acc = jnp.where(hrows[hh], alpha * acc_old[pr] + pv, acc)
                acc_out.append(acc)
            for h in range(ATT_HEADS):
                m_s[h:h + 1, :] = m_out[h]
                l_s[h:h + 1, :] = l_out[h]
            for pr in range(2):
                acc_s[LANES * pr:LANES * (pr + 1), :] = acc_out[pr]

        @pl.when(j < i)
        def _():
            step(False)

        @pl.when(j == i)
        def _():
            step(True)
            half = _iota((LANES, 1), 0)
            l_all = l_s[...]
            for pr in range(2):
                acc = acc_s[LANES * pr:LANES * (pr + 1), :]
                o_t = jnp.where(half < HEAD_DIM, acc / l_all[2 * pr:2 * pr + 1, :], acc / l_all[2 * pr + 1:2 * pr + 2, :])
                o_ref[:, LANES * pr:LANES * (pr + 1)] = o_t.T
            lse = m_s[...] + jnp.log(l_s[...])
            lse_ref[...] = jnp.where(_iota(lse.shape, 0) < ATT_HEADS, lse, 0.0)

    return _hosted_call(
        body, comm, (nq, nq), name=name,
        in_specs=[pl.BlockSpec((tq, w), lambda i, j: (i, cq)),
                  pl.BlockSpec((tq, w), lambda i, j: (jnp.minimum(j, i), ck)),
                  pl.BlockSpec((tq, w), lambda i, j: (jnp.minimum(j, i), cv)),
                  pl.BlockSpec((tq, ATT_HEADS * LANES), lambda i, j: (i, 0)),
                  pl.BlockSpec((tq, ATT_HEADS * LANES), lambda i, j: (jnp.minimum(j, i), 0))],
        out_specs=[pl.BlockSpec((tq, w), lambda i, j: (i, 0)),
                   pl.BlockSpec((SUBLANES, tq), lambda i, j: (0, i))],
        out_shape=[jax.ShapeDtypeStruct((t, w), F32), jax.ShapeDtypeStruct((SUBLANES, t), F32)],
        scratch_shapes=[pltpu.VMEM((SUBLANES, tq), F32), pltpu.VMEM((SUBLANES, tq), F32),
                        pltpu.VMEM((w, tq), F32)],
        args=[hbuf, hbuf, hbuf, eq, ek])


def _fox_delta(dymix, o, *, name):
    t, w = o.shape
    tm = _pick(t, (512, 256, 128))
    cdo = ATT_WIDTH // w

    def body(do_ref, o_ref, d_ref):
        d_ref[...] = _head_reduce(do_ref[...] * o_ref[...], 0, ATT_HEADS)

    return pl.pallas_call(
        body, name=name, grid=(t // tm,),
        in_specs=[pl.BlockSpec((tm, w), lambda i: (i, cdo)), pl.BlockSpec((tm, w), lambda i: (i, 0))],
        out_specs=pl.BlockSpec((tm, LANES), lambda i: (i, 0)),
        out_shape=jax.ShapeDtypeStruct((t, LANES), F32),
        compiler_params=_params(1),
    )(dymix, o)


def _fox_bwd(hbuf, eq, ek, dymix, lse_rows, delta_rows, *, comm=None, name):
    t = hbuf.shape[0]
    w = ATT_WIDTH
    tq = _pick(t, (FOX_T, 256, 128))
    nq = t // tq
    cq, ck, cv = COL_Q // w, COL_K // w, COL_V // w
    cdo = ATT_WIDTH // w

    def body(q_ref, k_ref, v_ref, eq_ref, ek_ref, do_ref, lse_ref, dl_ref, dk_ref, dv_ref, dfk_ref, dqt_ref, dfq_ref,
             dk_s, dv_s, dfk_s):
        j = pl.program_id(0)
        i = pl.program_id(1)

        @pl.when((i == 0) & (j == 0))
        def _():
            dqt_ref[...] = jnp.zeros_like(dqt_ref)
            dfq_ref[...] = jnp.zeros_like(dfq_ref)

        @pl.when(i == 0)
        def _():
            dk_s[...] = jnp.zeros_like(dk_s)
            dv_s[...] = jnp.zeros_like(dv_s)
            dfk_s[...] = jnp.zeros_like(dfk_s)

        def step(diagonal):
            _, hms = _fox_masks(i, j, tq)
            keys_first = (j * tq + _iota((tq, tq), 0)) <= (i * tq + _iota((tq, tq), 1))
            half = _iota((LANES, 1), 0)
            hrows = (half < HEAD_DIM, half >= HEAD_DIM)
            lse_all = lse_ref[...]
            dl_all = dl_ref[...]
            dvs, dks, dfks, dqts, dfqs = [], [], [], [], []
            for pr in range(2):
                sl = slice(LANES * pr, LANES * (pr + 1))
                qp = q_ref[:, sl]
                kp = k_ref[:, sl]
                kt = kp.T.astype(BF16)
                vpb = v_ref[:, sl].astype(BF16)
                dop = do_ref[:, sl]
                dv_p = jnp.zeros((tq, LANES), F32)
                dk_p = jnp.zeros((tq, LANES), F32)
                dqt_p = jnp.zeros((LANES, tq), F32)
                for hh in range(2):
                    h = 2 * pr + hh
                    hsl = slice(LANES * h, LANES * (h + 1))
                    qm = jnp.where(hms[hh], (qp * (HEAD_DIM ** -0.5)).astype(BF16), eq_ref[:, hsl])
                    km = jnp.where(hms[hh], kp.astype(BF16), ek_ref[:, hsl])
                    st = _dot(km, qm, 1, 1)
                    if diagonal:
                        st = jnp.where(keys_first, st, NEG)
                    pt = jnp.exp(st - lse_all[h:h + 1, :])
                    domb = jnp.where(hms[hh], dop, 0.0).astype(BF16)
                    dv_p = dv_p + _dot(pt.astype(BF16), domb, 1, 0)
                    dpt = _dot(vpb, domb, 1, 1)
                    dst = pt * (dpt - dl_all[h:h + 1, :])
                    dstb = dst.astype(BF16)
                    dk_p = dk_p + jnp.where(hms[hh], _dot(dstb, qm, 1, 0), 0.0)
                    dqt_p = dqt_p + _dot(jnp.where(hrows[hh], kt, 0.0), dstb, 1, 0)
                    part = dst[:, 0:LANES]
                    for c in range(1, tq // LANES):
                        part = part + dst[:, LANES * c:LANES * (c + 1)]
                    dfks.append(part)
                    dfqs.append(jnp.sum(dst, axis=0, keepdims=True))
                dvs.append(dv_p)
                dks.append(dk_p)
                dqts.append(dqt_p)
            dv_s[...] += jnp.concatenate(dvs, axis=1)
            dk_s[...] += jnp.concatenate(dks, axis=1)
            for h in range(ATT_HEADS):
                dfk_s[h] += dfks[h]
            cols = pl.ds(pl.multiple_of(i * tq, tq), tq)
            dqt_ref[:, cols] += jnp.concatenate(dqts, axis=0) * (HEAD_DIM ** -0.5)
            dfq_ref[:, cols] += jnp.concatenate(dfqs + [jnp.zeros((SUBLANES - ATT_HEADS, tq), F32)], axis=0)

        @pl.when(i > j)
        def _():
            step(False)

        @pl.when(i == j)
        def _():
            step(True)

        @pl.when(i == nq - 1)
        def _():
            dk_ref[...] = dk_s[...].astype(BF16)
            dv_ref[...] = dv_s[...].astype(BF16)
            lane = _iota((tq, LANES), 1)
            out = jnp.zeros((tq, LANES), F32)
            for h in range(ATT_HEADS):
                out = jnp.where(lane == h, jnp.sum(dfk_s[h], axis=1, keepdims=True), out)
            dfk_ref[...] = out

    qi = lambda j, i: jnp.maximum(i, j)
    rows = pl.BlockSpec((SUBLANES, tq), lambda j, i: (0, qi(j, i)))
    return _hosted_call(
        body, comm, (nq, nq), name=name,
        in_specs=[pl.BlockSpec((tq, w), lambda j, i: (qi(j, i), cq)),
                  pl.BlockSpec((tq, w), lambda j, i: (j, ck)),
                  pl.BlockSpec((tq, w), lambda j, i: (j, cv)),
                  pl.BlockSpec((tq, ATT_HEADS * LANES), lambda j, i: (qi(j, i), 0)),
                  pl.BlockSpec((tq, ATT_HEADS * LANES), lambda j, i: (j, 0)),
                  pl.BlockSpec((tq, w), lambda j, i: (qi(j, i), cdo)),
                  rows, rows],
        out_specs=[pl.BlockSpec((tq, w), lambda j, i: (j, 0)), pl.BlockSpec((tq, w), lambda j, i: (j, 0)),
                   pl.BlockSpec((tq, LANES), lambda j, i: (j, 0)),
                   pl.BlockSpec((w, t), lambda j, i: (0, 0)), pl.BlockSpec((SUBLANES, t), lambda j, i: (0, 0))],
        out_shape=[jax.ShapeDtypeStruct((t, w), BF16), jax.ShapeDtypeStruct((t, w), BF16),
                   jax.ShapeDtypeStruct((t, LANES), F32),
                   jax.ShapeDtypeStruct((w, t), F32), jax.ShapeDtypeStruct((SUBLANES, t), F32)],
        scratch_shapes=[pltpu.VMEM((tq, w), F32), pltpu.VMEM((tq, w), F32),
                        pltpu.VMEM((ATT_HEADS, tq, LANES), F32)],
        args=[hbuf, hbuf, hbuf, eq, ek, dymix, lse_rows, delta_rows])


GROUP_W = SSD_WIDTH // SSD_GROUPS
HEADS_PER_GROUP = SSD_HEADS // SSD_GROUPS


def _ssd_chunk_common(xr, prev8, sm, cw, cb, dtb, avec):
    c = _conv_taps(xr, prev8, cw, cb)
    sig = _sigmoid(c)
    xa = c * sig
    dt = _softplus(sm + dtb)
    a = dt * avec
    acum = _cumsum_rows(a)
    return c, sig, xa, dt, acum


def _ssd_head_cols(acum, acum_t):
    cols = [_col(acum, LANE_DT + h) for h in range(SSD_HEADS)]
    rows = [_row(acum_t, LANE_DT + h) for h in range(SSD_HEADS)]
    return cols, rows


def _expand_heads(vals, width):
    rows = vals[0].shape[0]
    colhead = _iota((rows, width), 1) // HEAD_DIM
    out = jnp.broadcast_to(vals[0], (rows, width))
    for h in range(1, len(vals)):
        out = jnp.where(colhead == h, vals[h], out)
    return out


def _ssd_decays(cols, g):
    mine = cols[HEADS_PER_GROUP * g:HEADS_PER_GROUP * (g + 1)]
    n = mine[0].shape[0]
    atots = [c[n - 1:n, :] for c in mine]
    e = _expand_heads([jnp.exp(c) for c in mine], GROUP_W)
    dec = _expand_heads([jnp.exp(t - c) for c, t in zip(mine, atots)], GROUP_W)
    etot = _expand_heads([jnp.exp(t) for t in atots], GROUP_W)
    return e, dec, etot


def _ssd_ldec(cols, rows, h, tril):
    return jnp.exp(jnp.where(tril, cols[h] - rows[h], NEG))


def _ssd_fwd(hbuf, conv_w, conv_b, dtb_vec, a_vec, d_exp, norm_g, *, name):
    t = hbuf.shape[0]
    L = SSD_CHUNK
    nc = t // L
    hb = L // SUBLANES
    cs = COL_SMALL // LANES
    cz = COL_Z // SSD_WIDTH

    def body(x_ref, xp_ref, z_ref, s_ref, cw_ref, cb_ref, dtb_ref, av_ref, dx_ref, ng_ref,
             yc_ref, y_ref, st_ref, state):
        i = pl.program_id(0)

        @pl.when(i == 0)
        def _():
            state[...] = jnp.zeros_like(state)

        prev = jnp.where(i == 0, 0.0, xp_ref[...])
        _, _, xa, dt, acum = _ssd_chunk_common(x_ref[...], prev, s_ref[...], cw_ref[...], cb_ref[...],
                                               dtb_ref[...], av_ref[...])
        cols, rows = _ssd_head_cols(acum, acum.T)
        xs = xa[:, :SSD_WIDTH]
        xdt = xs * _head_expand(dt, LANE_DT, SSD_HEADS, SSD_WIDTH)
        tril = _iota((L, L), 0) >= _iota((L, L), 1)
        lane = _iota((1, LANES), 1)
        ys = []
        for g in range(SSD_GROUPS):
            bg = xa[:, SSD_WIDTH + SSD_STATE * g:SSD_WIDTH + SSD_STATE * (g + 1)].astype(BF16)
            cg = xa[:, SSD_WIDTH + SSD_STATE * (SSD_GROUPS + g):SSD_WIDTH + SSD_STATE * (SSD_GROUPS + g + 1)].astype(BF16)
            gm = _dot(cg, bg, 1, 1)
            e, dec, etot = _ssd_decays(cols, g)
            s_in = state[g]
            st_ref[0, g] = s_in
            xg = xdt[:, GROUP_W * g:GROUP_W * (g + 1)]
            y_off = e * _dot(cg, s_in.astype(BF16), 1, 0)
            state[g] = etot * s_in + _dot(bg, (dec * xg).astype(BF16), 0, 0)
            for pr in range(2):
                xp = xg[:, LANES * pr:LANES * (pr + 1)].astype(BF16)
                outs = []
                for hh in range(2):
                    h = HEADS_PER_GROUP * g + 2 * pr + hh
                    m = gm * _ssd_ldec(cols, rows, h, tril)
                    outs.append(_dot(m.astype(BF16), xp, 1, 0))
                ys.append(jnp.where(lane < HEAD_DIM, outs[0], outs[1]) + y_off[:, LANES * pr:LANES * (pr + 1)])
        y = jnp.concatenate(ys, axis=1)
        y_ref[...] = y
        yd = y + dx_ref[...] * xs
        zz = z_ref[...]
        y2 = yd * zz * _sigmoid(zz)
        ng = ng_ref[...]
        outs = []
        for g in range(SSD_GROUPS):
            yg = y2[:, GROUP_W * g:GROUP_W * (g + 1)]
            rs = lax.rsqrt(jnp.mean(yg * yg, axis=1, keepdims=True) + RMS_EPS)
            outs.append(yg * rs * ng[:, GROUP_W * g:GROUP_W * (g + 1)])
        yc_ref[...] = jnp.concatenate(outs, axis=1)

    cdim = SSD_CONV_DIM
    vecc = pl.BlockSpec((1, cdim), lambda i: (0, 0))
    vecl = pl.BlockSpec((1, LANES), lambda i: (0, 0))
    vecw = pl.BlockSpec((1, SSD_WIDTH), lambda i: (0, 0))
    roww = pl.BlockSpec((L, SSD_WIDTH), lambda i: (i, 0))
    return pl.pallas_call(
        body, name=name, grid=(nc,),
        in_specs=[pl.BlockSpec((L, cdim), lambda i: (i, 0)),
                  pl.BlockSpec((SUBLANES, cdim), lambda i: (jnp.maximum(i * hb - 1, 0), 0)),
                  pl.BlockSpec((L, SSD_WIDTH), lambda i: (i, cz)),
                  pl.BlockSpec((L, LANES), lambda i: (i, cs)),
                  pl.BlockSpec((CONV_K, cdim), lambda i: (0, 0)), vecc, vecl, vecl, vecw, vecw],
        out_specs=[roww, roww, pl.BlockSpec((1, SSD_GROUPS, SSD_STATE, GROUP_W), lambda i: (i, 0, 0, 0))],
        out_shape=[jax.ShapeDtypeStruct((t, SSD_WIDTH), F32), jax.ShapeDtypeStruct((t, SSD_WIDTH), F32),
                   jax.ShapeDtypeStruct((nc, SSD_GROUPS, SSD_STATE, GROUP_W), F32)],
        scratch_shapes=[pltpu.VMEM((SSD_GROUPS, SSD_STATE, GROUP_W), F32)],
        compiler_params=_params(1),
    )(hbuf, hbuf, hbuf, hbuf, conv_w, conv_b, dtb_vec, a_vec, d_exp, norm_g)


def _ssd_bwd(dymix, hbuf, y_ssd, states, conv_w, conv_b, dtb_vec, a_vec, d_exp, norm_g, *, name):
    t = hbuf.shape[0]
    L = SSD_CHUNK
    nc = t // L
    hb = L // SUBLANES
    cs = COL_SMALL // LANES
    cz = COL_Z // SSD_WIDTH
    cdy = (LRU_WIDTH + ATT_WIDTH) // SSD_WIDTH
    cdim = SSD_CONV_DIM

    def body(dyc_ref, x_ref, xp_ref, z_ref, s_ref, y_ref, st_ref, cw_ref, cb_ref, dtb_ref, av_ref, dx_ref, ng_ref,
             dxr_ref, dz_ref, dsm_ref, dng_ref, dd_ref, da_ref, ddtb_ref, dcw_ref, dcb_ref,
             dstate, dnext):
        i = pl.program_id(0)
        ic = nc - 1 - i

        @pl.when(i == 0)
        def _():
            dstate[...] = jnp.zeros_like(dstate)
            dnext[...] = jnp.zeros_like(dnext)
            for ref in (dng_ref, dd_ref, da_ref, ddtb_ref, dcw_ref, dcb_ref):
                ref[...] = jnp.zeros_like(ref)

        xr = x_ref[...]
        sm = s_ref[...]
        prev = jnp.where(ic == 0, 0.0, xp_ref[...])
        avec = av_ref[...]
        c, sig, xa, dt, acum = _ssd_chunk_common(xr, prev, sm, cw_ref[...], cb_ref[...], dtb_ref[...], avec)
        cols, rows = _ssd_head_cols(acum, acum.T)
        xs = xa[:, :SSD_WIDTH]
        dtx = _head_expand(dt, LANE_DT, SSD_HEADS, SSD_WIDTH)
        xdt = xs * dtx
        tril = _iota((L, L), 0) >= _iota((L, L), 1)
        lane = _iota((1, LANES), 1)
        hmasks = (lane < HEAD_DIM, lane >= HEAD_DIM)

        y = y_ref[...]
        dexp = dx_ref[...]
        yd = y + dexp * xs
        zz = z_ref[...]
        sz = _sigmoid(zz)
        siluz = zz * sz
        y2 = yd * siluz
        ng = ng_ref[...]
        dyc = dyc_ref[...]
        dy2s, dngs = [], []
        for g in range(SSD_GROUPS):
            sl = slice(GROUP_W * g, GROUP_W * (g + 1))
            yg = y2[:, sl]
            rs = lax.rsqrt(jnp.mean(yg * yg, axis=1, keepdims=True) + RMS_EPS)
            wv = dyc[:, sl] * ng[:, sl]
            dngs.append(jnp.sum(dyc[:, sl] * yg * rs, axis=0, keepdims=True))
            dy2s.append(rs * wv - yg * (rs * rs * rs) * jnp.mean(wv * yg, axis=1, keepdims=True))
        dy2 = jnp.concatenate(dy2s, axis=1)
        dng_ref[...] += jnp.concatenate(dngs, axis=1)
        dz_ref[...] = (dy2 * yd * (sz * (1.0 + zz * (1.0 - sz)))).astype(BF16)
        dy = dy2 * siluz
        dd_ref[...] += jnp.sum(dy * xs, axis=0, keepdims=True)

        dxs, dbs, dcs = [], [], []
        datot = jnp.zeros((1, LANES), F32)
        lanes = _iota((L, LANES), 1)
        dacum = jnp.zeros((L, LANES), F32)
        for g in range(SSD_GROUPS):
            sl = slice(GROUP_W * g, GROUP_W * (g + 1))
            bg = xa[:, SSD_WIDTH + SSD_STATE * g:SSD_WIDTH + SSD_STATE * (g + 1)].astype(BF16)
            cg = xa[:, SSD_WIDTH + SSD_STATE * (SSD_GROUPS + g):SSD_WIDTH + SSD_STATE * (SSD_GROUPS + g + 1)].astype(BF16)
            gm = _dot(cg, bg, 1, 1)
            e, dec, etot = _ssd_decays(cols, g)
            s_in = st_ref[0, g]
            ds_out = dstate[g]
            dyg = dy[:, sl]
            xg = xdt[:, sl]
            edy = (e * dyg).astype(BF16)
            dstate[g] = etot * ds_out + _dot(cg, edy, 0, 0)
            dx_state = dec * _dot(bg, ds_out.astype(BF16), 1, 0)
            y_off = e * _dot(cg, s_in.astype(BF16), 1, 0)
            dacum = dacum + _head_reduce_group(dyg * y_off - xg * dx_state, g)
            dc_off = _dot(edy, s_in.astype(BF16), 1, 1)
            db_state = _dot((dec * xg).astype(BF16), ds_out.astype(BF16), 1, 1)
            dgsum = jnp.zeros((L, L), F32)
            dx_pairs = []
            for pr in range(2):
                psl = slice(LANES * pr, LANES * (pr + 1))
                xp = xg[:, psl]
                dyp = dyg[:, psl]
                dx_pair = jnp.zeros((L, LANES), F32)
                for hh in range(2):
                    h = HEADS_PER_GROUP * g + 2 * pr + hh
                    ldec = _ssd_ldec(cols, rows, h, tril)
                    dym = jnp.where(hmasks[hh], dyp, 0.0).astype(BF16)
                    xm = jnp.where(hmasks[hh], xp, 0.0).astype(BF16)
                    dx_pair = dx_pair + _dot((gm * ldec).astype(BF16), dym, 0, 0)
                    dml = _dot(dym, xm, 1, 1) * ldec
                    dgsum = dgsum + dml
                    qm = dml * gm
                    seg = jnp.sum(qm, axis=1, keepdims=True) - jnp.sum(qm.T, axis=1, keepdims=True)
                    dacum = dacum + jnp.where(lanes == LANE_DT + h, seg, 0.0)
                dx_pairs.append(dx_pair)
            dgb = dgsum.astype(BF16)
            dcs.append(_dot(dgb, bg, 1, 0) + dc_off)
            dbs.append(_dot(dgb, cg, 0, 0) + db_state)
            dxg = jnp.concatenate(dx_pairs, axis=1) + dx_state
            dxs.append(dxg)
            v = jnp.sum(dx_state * xg, axis=0, keepdims=True) + etot * jnp.sum(ds_out * s_in, axis=0, keepdims=True)
            datot = datot + _head_reduce_row(v, LANE_DT + HEADS_PER_GROUP * g, HEADS_PER_GROUP)
        dx = jnp.concatenate(dxs, axis=1)
        dacum = dacum + jnp.where(_iota((L, LANES), 0) == L - 1, datot, 0.0)
        da = _cumsum_rows(dacum, reverse=True)
        ddt = da * avec + _head_reduce(dx * xs, LANE_DT, SSD_HEADS)
        da_ref[...] += jnp.sum(da * dt, axis=0, keepdims=True)
        ddt_raw = ddt * _sigmoid(sm + dtb_ref[...])
        ddt_raw = jnp.where((lanes >= LANE_DT) & (lanes < LANE_DT + SSD_HEADS), ddt_raw, 0.0)
        dsm_ref[...] = ddt_raw
        ddtb_ref[...] += jnp.sum(ddt_raw, axis=0, keepdims=True)
        dxs_total = dx * dtx + dexp * dy
        dxa = jnp.concatenate([dxs_total] + dbs + dcs, axis=1)
        dc = dxa * (sig * (1.0 + c * (1.0 - sig)))
        dxr, dws = _conv_taps_bwd(dc, dnext[...], cw_ref[...], xr)
        dxr_ref[...] = dxr.astype(BF16)
        dcw_ref[...] += dws
        dcb_ref[...] += jnp.sum(dc, axis=0, keepdims=True)
        dnext[...] = dc[:SUBLANES]

    rev = lambda i: nc - 1 - i
    vecc = pl.BlockSpec((1, cdim), lambda i: (0, 0))
    vecl = pl.BlockSpec((1, LANES), lambda i: (0, 0))
    vecw = pl.BlockSpec((1, SSD_WIDTH), lambda i: (0, 0))
    cwspec = pl.BlockSpec((CONV_K, cdim), lambda i: (0, 0))
    roww = pl.BlockSpec((L, SSD_WIDTH), lambda i: (rev(i), 0))
    return pl.pallas_call(
        body, name=name, grid=(nc,),
        in_specs=[pl.BlockSpec((L, SSD_WIDTH), lambda i: (rev(i), cdy)),
                  pl.BlockSpec((L, cdim), lambda i: (rev(i), 0)),
                  pl.BlockSpec((SUBLANES, cdim), lambda i: (jnp.maximum(rev(i) * hb - 1, 0), 0)),
                  pl.BlockSpec((L, SSD_WIDTH), lambda i: (rev(i), cz)),
                  pl.BlockSpec((L, LANES), lambda i: (rev(i), cs)),
                  roww,
                  pl.BlockSpec((1, SSD_GROUPS, SSD_STATE, GROUP_W), lambda i: (rev(i), 0, 0, 0)),
                  cwspec, vecc, vecl, vecl, vecw, vecw],
        out_specs=[pl.BlockSpec((L, cdim), lambda i: (rev(i), 0)), roww,
                   pl.BlockSpec((L, LANES), lambda i: (rev(i), 0)),
                   vecw, vecw, vecl, vecl, cwspec, vecc],
        out_shape=[jax.ShapeDtypeStruct((t, cdim), BF16), jax.ShapeDtypeStruct((t, SSD_WIDTH), BF16),
                   jax.ShapeDtypeStruct((t, LANES), F32),
                   jax.ShapeDtypeStruct((1, SSD_WIDTH), F32), jax.ShapeDtypeStruct((1, SSD_WIDTH), F32),
                   jax.ShapeDtypeStruct((1, LANES), F32), jax.ShapeDtypeStruct((1, LANES), F32),
                   jax.ShapeDtypeStruct((CONV_K, cdim), F32), jax.ShapeDtypeStruct((1, cdim), F32)],
        scratch_shapes=[pltpu.VMEM((SSD_GROUPS, SSD_STATE, GROUP_W), F32), pltpu.VMEM((SUBLANES, cdim), F32)],
        compiler_params=_params(1),
    )(dymix, hbuf, hbuf, hbuf, hbuf, y_ssd, states, conv_w, conv_b, dtb_vec, a_vec, d_exp, norm_g)


def _head_reduce_group(x, g):
    return _head_reduce(x, LANE_DT + HEADS_PER_GROUP * g, HEADS_PER_GROUP)


def _head_reduce_row(v, lane0, nheads):
    colhead = _iota(v.shape, 1) // HEAD_DIM
    lane = _iota((1, LANES), 1)
    out = jnp.zeros((1, LANES), F32)
    for h in range(nheads):
        s = jnp.sum(jnp.where(colhead == h, v, 0.0), axis=1, keepdims=True)
        out = jnp.where(lane == lane0 + h, s, out)
    return out


def _exchange(inps, axes, *, swap=False, name):
    n = 2 ** len(axes)
    assert not swap or n == 2
    counts = [a.shape[0] for a in inps]
    out_shapes = [jax.ShapeDtypeStruct(a.shape if swap else (n,) + a.shape, a.dtype) for a in inps]
    units = sum(counts)
    na = len(inps)

    def body(*refs):
        in_refs, out_refs = refs[:na], refs[na:2 * na]
        send_sems, recv_sems, local_sems = refs[2 * na:]
        pos = {ax: lax.axis_index(ax) for ax in MESH_AXES}

        def slot_of(coord):
            s = 0
            for ax in axes:
                s = s * 2 + coord[ax]
            return s

        me = slot_of(pos)
        copies = []
        unit = 0
        for a in range(na):
            for it in range(counts[a]):
                dst = out_refs[a].at[it] if swap else out_refs[a].at[me, it]
                if not swap:
                    cp = pltpu.make_async_copy(in_refs[a].at[it], dst, local_sems.at[unit])
                    cp.start()
                    copies.append(cp)
                for delta in range(1, n):
                    coord = dict(pos)
                    for b, ax in enumerate(reversed(axes)):
                        if (delta >> b) & 1:
                            coord[ax] = 1 - pos[ax]
                    k = unit * (n - 1) + delta - 1
                    cp = pltpu.make_async_remote_copy(
                        src_ref=in_refs[a].at[it], dst_ref=dst,
                        send_sem=send_sems.at[k], recv_sem=recv_sems.at[k],
                        device_id=(coord["x"], coord["y"], coord["c"]), device_id_type=pl.DeviceIdType.MESH)
                    cp.start()
                    copies.append(cp)
                unit += 1
        for cp in copies:
            cp.wait()

    any_spec = pl.BlockSpec(memory_space=pl.ANY)
    return pl.pallas_call(
        body, name=name,
        in_specs=[any_spec] * na, out_specs=[any_spec] * na, out_shape=out_shapes,
        scratch_shapes=[pltpu.SemaphoreType.DMA((units * (n - 1),)), pltpu.SemaphoreType.DMA((units * (n - 1),)),
                        pltpu.SemaphoreType.DMA((units,))],
    )(*inps)


class _Comm:
    def __init__(self, arrays, out_shapes, n_own, start, finish, base=0, middle=None):
        self.arrays, self.out_shapes, self.start, self.finish = arrays, out_shapes, start, finish
        self.middle = middle or (lambda *refs: None)
        self.base, self.n_own, self.n_sems = base, n_own, base + n_own

    def specs(self):
        any_spec = pl.BlockSpec(memory_space=pl.ANY)
        sems = [pltpu.SemaphoreType.DMA((self.n_sems,)), pltpu.SemaphoreType.DMA((self.n_sems,))]
        return [any_spec] * len(self.arrays), [any_spec] * len(self.out_shapes), sems


def _run_comm(comm, *, name):
    na, no = len(comm.arrays), len(comm.out_shapes)

    def body(*refs):
        args = (refs[:na], refs[na:na + no]) + tuple(refs[na + no:])
        comm.start(*args)
        comm.middle(*args)
        comm.finish(*args)

    in_specs, out_specs, sems = comm.specs()
    return pl.pallas_call(body, name=name, in_specs=in_specs, out_specs=out_specs, out_shape=comm.out_shapes,
                          scratch_shapes=sems)(*comm.arrays)


def _chip_peer(x, y, d):
    px = 1 - x if d & 2 else x
    py = 1 - y if d & 1 else y
    return px, py, 2 * px + py


def _gather_layer_comm(srcs, li, base=0):
    counts = [s.shape[0] for s in srcs]
    units = [(a, it) for a in range(len(srcs)) for it in range(counts[a])]
    n_ici = 3 * len(units)
    out_shapes = [jax.ShapeDtypeStruct((N_CHIPS,) + s.shape, s.dtype) for s in srcs]

    def ici(ins, outs, ssem, rsem, u, d):
        x, y, c = (lax.axis_index(ax) for ax in MESH_AXES)
        a, it = units[u]
        px, py, _ = _chip_peer(x, y, d)
        k = base + 3 * u + d - 1
        return pltpu.make_async_remote_copy(
            src_ref=ins[a].at[it], dst_ref=outs[a].at[2 * x + y, it], send_sem=ssem.at[k], recv_sem=rsem.at[k],
            device_id=(px, py, c), device_id_type=pl.DeviceIdType.MESH)

    def arrived(ins, outs, ssem, rsem, u, d):
        x, y, c = (lax.axis_index(ax) for ax in MESH_AXES)
        a, it = units[u]
        _, _, pk = _chip_peer(x, y, d)
        k = base + 3 * u + d - 1
        return pltpu.make_async_remote_copy(
            src_ref=ins[a].at[it], dst_ref=outs[a].at[pk, it], send_sem=ssem.at[k], recv_sem=rsem.at[k],
            device_id=(x, y, c), device_id_type=pl.DeviceIdType.MESH)

    def forward(ins, outs, ssem, rsem, u, slot):
        x, y, c = (lax.axis_index(ax) for ax in MESH_AXES)
        a, it = units[u]
        pk = 2 * x + y if slot == 0 else _chip_peer(x, y, slot)[2]
        src = ins[a].at[it] if slot == 0 else outs[a].at[pk, it]
        k = base + n_ici + 4 * u + slot
        return pltpu.make_async_remote_copy(
            src_ref=src, dst_ref=outs[a].at[pk, it], send_sem=ssem.at[k], recv_sem=rsem.at[k],
            device_id=(x, y, 1 - c), device_id_type=pl.DeviceIdType.MESH)

    def start(ins, outs, ssem, rsem):
        for u in range(len(units)):
            forward(ins, outs, ssem, rsem, u, 0).start()

        @pl.when(lax.axis_index("c") == li)
        def _():
            for u in range(len(units)):
                for d in range(1, N_CHIPS):
                    ici(ins, outs, ssem, rsem, u, d).start()

    def middle(ins, outs, ssem, rsem):
        @pl.when(lax.axis_index("c") == li)
        def _():
            for u in range(len(units)):
                for d in range(1, N_CHIPS):
                    arrived(ins, outs, ssem, rsem, u, d).wait_recv()
                    forward(ins, outs, ssem, rsem, u, d).start()

    def finish(ins, outs, ssem, rsem):
        c = lax.axis_index("c")

        @pl.when(c == li)
        def _():
            for u in range(len(units)):
                for d in range(1, N_CHIPS):
                    ici(ins, outs, ssem, rsem, u, d).wait_send()
                    forward(ins, outs, ssem, rsem, u, d).wait_send()

        @pl.when(c != li)
        def _():
            for u in range(len(units)):
                for d in range(1, N_CHIPS):
                    forward(ins, outs, ssem, rsem, u, d).wait_recv()

        for u in range(len(units)):
            forward(ins, outs, ssem, rsem, u, 0).wait()

    return _Comm(srcs, out_shapes, n_ici + 4 * len(units), start, finish, base, middle)


def _reduce_chips_comm(sums, li, base=0):
    counts = [s.shape[0] for s in sums]
    units = [(a, it) for a in range(len(sums)) for it in range(counts[a])]
    out_shapes = [jax.ShapeDtypeStruct((N_CHIPS, s.shape[0]) + s.shape[2:], s.dtype) for s in sums]

    def copy(ins, outs, ssem, rsem, u, d):
        x, y, c = (lax.axis_index(ax) for ax in MESH_AXES)
        a, it = units[u]
        px, py, pk = _chip_peer(x, y, d)
        k = base + 3 * u + d - 1
        return pltpu.make_async_remote_copy(
            src_ref=ins[a].at[it, pk], dst_ref=outs[a].at[2 * x + y, it], send_sem=ssem.at[k], recv_sem=rsem.at[k],
            device_id=(px, py, c), device_id_type=pl.DeviceIdType.MESH)

    def start(ins, outs, ssem, rsem):
        @pl.when(lax.axis_index("c") == li)
        def _():
            for u in range(len(units)):
                for d in range(1, N_CHIPS):
                    copy(ins, outs, ssem, rsem, u, d).start()

    def finish(ins, outs, ssem, rsem):
        @pl.when(lax.axis_index("c") == li)
        def _():
            for u in range(len(units)):
                for d in range(1, N_CHIPS):
                    copy(ins, outs, ssem, rsem, u, d).wait()

    return _Comm(sums, out_shapes, 3 * len(units), start, finish, base)


def _sum_slots(buf, out_dtype, *, name):
    n, rows, cols = buf.shape
    tm = _pick(rows, (512, 256, 128, 8))
    if rows % tm:
        tm = rows

    def body(b_ref, o_ref):
        acc = b_ref[0].astype(F32)
        for s in range(1, n):
            acc = acc + b_ref[s].astype(F32)
        o_ref[...] = acc.astype(out_dtype)

    return pl.pallas_call(
        body, name=name, grid=(pl.cdiv(rows, tm),),
        in_specs=[pl.BlockSpec((n, tm, cols), lambda i: (0, i, 0))],
        out_specs=pl.BlockSpec((tm, cols), lambda i: (i, 0)),
        out_shape=jax.ShapeDtypeStruct((rows, cols), out_dtype),
        compiler_params=_params(1),
    )(buf)


def _sum_pair(a, b, out_dtype, *, name):
    shape = a.shape
    cols = shape[-1]
    a2, b2 = a.reshape(-1, cols), b.reshape(-1, cols)
    rows = a2.shape[0]
    tm = _pick(rows, (512, 256, 128, 8))

    def body(a_ref, b_ref, o_ref):
        o_ref[...] = (a_ref[...].astype(F32) + b_ref[...].astype(F32)).astype(out_dtype)

    spec = pl.BlockSpec((tm, cols), lambda i: (i, 0))
    return pl.pallas_call(
        body, name=name, grid=(rows // tm,), in_specs=[spec, spec], out_specs=spec,
        out_shape=jax.ShapeDtypeStruct((rows, cols), out_dtype), compiler_params=_params(1),
    )(a2, b2).reshape(shape)


def _adamw(w, g, m, v, *, name):
    shape = w.shape
    cols = shape[-1]
    rows = w.size // cols
    w2, g2, m2, v2 = (a.reshape(rows, cols) for a in (w, g, m, v))
    tm = _pick(rows, (256, 128, 64, 32, 16, 8))
    if rows % tm:
        tm = rows
    bc1 = 1.0 - ADAM_B1 ** ADAM_STEP
    bc2 = 1.0 - ADAM_B2 ** ADAM_STEP

    def body(w_ref, g_ref, m_ref, v_ref, d_ref, nm_ref, nv_ref):
        gg = g_ref[...]
        mm = ADAM_B1 * m_ref[...] + (1.0 - ADAM_B1) * gg
        vv = ADAM_B2 * v_ref[...] + (1.0 - ADAM_B2) * (gg * gg)
        m_hat = mm / bc1
        v_hat = vv / bc2
        d_ref[...] = -ADAM_LR * (m_hat / (jnp.sqrt(v_hat) + ADAM_EPS) + ADAM_WD * w_ref[...])
        nm_ref[...] = mm
        nv_ref[...] = vv

    spec = pl.BlockSpec((tm, cols), lambda i: (i, 0))
    o = jax.ShapeDtypeStruct((rows, cols), F32)
    outs = pl.pallas_call(
        body, name=name, grid=(rows // tm,), in_specs=[spec] * 4, out_specs=[spec] * 3, out_shape=[o] * 3,
        compiler_params=_params(1),
    )(w2, g2, m2, v2)
    return tuple(a.reshape(shape) for a in outs)


def _layer_fwd(li, x, xb, pb, W, up=None, att=None):
    nm = lambda s: f"l{li}_{s}"
    sv = {"x_in_b": xb}
    (g1, u1, a1), got = _mm_swiglu(xb, W["ffn1_wg"], W["ffn1_wu"], comm=up[0] if up else None, name=nm("ffn1_up"))
    if up:
        W = {**W, **up[1](got)}
    x1, x1b, xh1, rs1 = _mm_ln(a1, W["ffn1_wd"], x, W["ln1_g"], W["ln1_b"], rscale=ALPHA, mscale=0.5, name=nm("ffn1_down_ln"))
    hbuf = _mm(x1b, W["w_in_p"], name=nm("in_proj"))
    ya, lu, lr, lig, la, lh = _lru_fwd(hbuf, W["lru_conv_w"], W["lru_conv_b"], W["lru_wa_bd"], W["lru_ba"],
                                       W["lru_wx_bd"], W["lru_bx"], W["lru_lambda"], name=nm("lru_fwd"))
    eq, ek = _fox_prep(hbuf, W["fox_bf_vec"], name=nm("fox_prep"))
    (yb, lse_rows), got = _fox_fwd(hbuf, eq, ek, comm=att[0] if att else None, name=nm("fox_fwd"))
    if att:
        W = {**W, **att[1](got)}
    yc, yssd, states = _ssd_fwd(hbuf, W["ssd_conv_w"], W["ssd_conv_b"], W["ssd_dtb_vec"], W["ssd_a_vec"],
                                W["ssd_d_exp"], W["ssd_norm_g"], name=nm("ssd_fwd"))
    ymix = jnp.concatenate([ya, yb, yc], axis=1).astype(BF16)
    x2, x2b, xh2, rs2 = _mm_ln(ymix, W["w_out"], x1, W["ln2_g"], W["ln2_b"], rscale=ALPHA, mscale=1.0, name=nm("out_proj_ln"))
    (g2, u2, a2), _ = _mm_swiglu(x2b, W["ffn2_wg"], W["ffn2_wu"], name=nm("ffn2_up"))
    x3, x3b, xh3, rs3 = _mm_ln(a2, W["ffn2_wd"], x2, W["ln3_g"], W["ln3_b"], rscale=ALPHA, mscale=0.5, name=nm("ffn2_down_ln"))
    x4, x4b, sg, e = _mm_pe(x3, x3b, pb, W["pe_gate_w"], W["pe_gate_b"], W["pe_proj"], name=nm("ple"))
    sv.update(g1=g1, u1=u1, a1=a1, x1b=x1b, xh1=xh1, rs1=rs1, hbuf=hbuf, lu=lu, lr=lr, lig=lig, la=la, lh=lh,
              eq=eq, ek=ek, lse_rows=lse_rows, yb=yb, yssd=yssd, states=states, ymix=ymix, x2b=x2b, xh2=xh2, rs2=rs2,
              g2=g2, u2=u2, a2=a2, x3b=x3b, xh3=xh3, rs3=rs3, sg=sg, e=e, pb=pb)
    return x4, x4b, sv, W


def _layer_bwd(li, dx4, sv, W, comm=None, late=None, last=None):
    nm = lambda s: f"l{li}_{s}"
    G = {}
    dgp, de, dbg = _pe_bwd_elem(dx4, sv["sg"], sv["e"], name=nm("ple_bwd"))
    G["pe_gate_b"] = dbg
    G["pe_gate_w"] = _mm(sv["x3b"], dgp, ta=True, out_dtype=BF16, name=nm("d_pe_gate_w"))
    G["pe_proj"] = _mm(sv["pb"], de, ta=True, out_dtype=BF16, chip_cols=True, name=nm("d_pe_proj"))
    dr3, dr3b, G["ln3_g"], G["ln3_b"] = _bwd_proj([(dgp, W["pe_gate_w"])], dx4, rscale=1.0,
                                                  ln=(sv["xh3"], sv["rs3"], W["ln3_g"]), name=nm("ln3_bwd"))
    G["ffn2_wd"] = _mm(sv["a2"], dr3b, ta=True, scale=0.5, out_dtype=BF16, name=nm("d_ffn2_wd"))
    dg2, du2 = _mm_swiglu_bwd(dr3b, W["ffn2_wd"], sv["g2"], sv["u2"], scale=0.5, name=nm("ffn2_act_bwd"))
    G["ffn2_wg"] = _mm(sv["x2b"], dg2, ta=True, out_dtype=BF16, chip_cols=True, name=nm("d_ffn2_wg"))
    G["ffn2_wu"] = _mm(sv["x2b"], du2, ta=True, out_dtype=BF16, chip_cols=True, name=nm("d_ffn2_wu"))
    dr2, dr2b, G["ln2_g"], G["ln2_b"] = _bwd_proj([(dg2, W["ffn2_wg"]), (du2, W["ffn2_wu"])], dr3, rscale=ALPHA,
                                                  ln=(sv["xh2"], sv["rs2"], W["ln2_g"]), name=nm("ln2_bwd"))
    G["w_out"] = _mm(sv["ymix"], dr2b, ta=True, out_dtype=BF16, name=nm("d_w_out"))
    dymix = _mm(dr2b, W["w_out"], tb=True, name=nm("d_ymix"))
    hbuf = sv["hbuf"]
    (dur, dgr, G["lru_conv_w"], G["lru_conv_b"], G["lru_wa_bd"], G["lru_ba"], G["lru_wx_bd"], G["lru_bx"],
     G["lru_lambda"]) = _lru_bwd(dymix, hbuf, sv["lu"], sv["lr"], sv["lig"], sv["la"], sv["lh"],
                                 W["lru_conv_w"], W["lru_wa_bd"], W["lru_wx_bd"], W["lru_lambda"], name=nm("lru_bwd"))
    delta = _fox_delta(dymix, sv["yb"], name=nm("fox_delta"))
    delta_rows = jnp.pad(delta[:, :ATT_HEADS].T, ((0, SUBLANES - ATT_HEADS), (0, 0)))
    comm = _merge_comms([comm, late(G) if late else None])
    (dk, dv, dfk, dqt, dfq), comm_out = _fox_bwd(hbuf, sv["eq"], sv["ek"], dymix, sv["lse_rows"], delta_rows,
                                                 comm=comm, name=nm("fox_bwd"))
    dq = dqt.T
    dfc = jnp.pad(dfq[:ATT_HEADS].T, ((0, 0), (0, LANES - ATT_HEADS))) - dfk
    dsm_f, G["fox_bf_vec"] = _fox_post(dfc, hbuf, W["fox_bf_vec"], name=nm("fox_post"))
    (dxr, dz, dsm_dt, G["ssd_norm_g"], G["ssd_d_exp"], G["ssd_a_vec"], G["ssd_dtb_vec"], G["ssd_conv_w"],
     G["ssd_conv_b"]) = _ssd_bwd(dymix, hbuf, sv["yssd"], sv["states"], W["ssd_conv_w"], W["ssd_conv_b"],
                                 W["ssd_dtb_vec"], W["ssd_a_vec"], W["ssd_d_exp"], W["ssd_norm_g"], name=nm("ssd_bwd"))
    t = dx4.shape[0]
    dh = jnp.concatenate([dxr.astype(BF16), dz.astype(BF16), dur.astype(BF16), dgr.astype(BF16), dq.astype(BF16),
                          dk.astype(BF16), dv.astype(BF16), (dsm_f + dsm_dt).astype(BF16),
                          jnp.zeros((t, H_WIDTH - COL_SMALL - LANES), BF16)], axis=1)
    G["w_in_p"] = _mm(sv["x1b"], dh, ta=True, name=nm("d_w_in"))
    dr1, dr1b, G["ln1_g"], G["ln1_b"] = _bwd_proj([(dh, W["w_in_p"])], dr2, rscale=ALPHA,
                                                  ln=(sv["xh1"], sv["rs1"], W["ln1_g"]), name=nm("ln1_bwd"))
    G["ffn1_wd"] = _mm(sv["a1"], dr1b, ta=True, scale=0.5, out_dtype=BF16, name=nm("d_ffn1_wd"))
    dg1, du1 = _mm_swiglu_bwd(dr1b, W["ffn1_wd"], sv["g1"], sv["u1"], scale=0.5, name=nm("ffn1_act_bwd"))
    G["ffn1_wg"] = _mm(sv["x_in_b"], dg1, ta=True, out_dtype=BF16, chip_cols=True, name=nm("d_ffn1_wg"))
    G["ffn1_wu"] = _mm(sv["x_in_b"], du1, ta=True, out_dtype=BF16, chip_cols=True, name=nm("d_ffn1_wu"))
    dx_in, *last_out = _bwd_proj([(dg1, W["ffn1_wg"]), (du1, W["ffn1_wu"])], dr1, rscale=ALPHA, ln=None,
                                 comm=last(G) if last else None, name=nm("x_in_bwd"))
    return dx_in, G, comm_out, (last_out[0] if last_out else None)


def _block_diag(w):
    n, b, _ = w.shape
    eye = jnp.eye(n, dtype=w.dtype)
    return (eye[:, None, :, None] * w[:, :, None, :]).reshape(n * b, n * b)


def _block_diag_extract(m):
    n, b = LRU_HEADS, HEAD_DIM
    return jnp.stack([m[b * i:b * (i + 1), b * i:b * (i + 1)] for i in range(n)])


def _lane_vec(v, lane0):
    return jnp.pad(v.astype(F32), (lane0, LANES - lane0 - v.shape[0])).reshape(1, LANES)


def _w_in_permute(w):
    d = w.shape[0]
    z = lambda n: jnp.zeros((d, n), w.dtype)
    return jnp.concatenate([w[:, 1796:2820], w[:, 1284:1796], w[:, 0:512], w[:, 512:1280],
                            w[:, 1280:1284], w[:, 2820:2828], z(LANES - 12), z(H_WIDTH - COL_SMALL - LANES)], axis=1)


def _w_in_unpermute(wp):
    return jnp.concatenate([wp[:, COL_U:COL_Q], wp[:, COL_Q:COL_SMALL], wp[:, COL_SMALL:COL_SMALL + 4],
                            wp[:, COL_Z:COL_U], wp[:, COL_XBC:COL_Z], wp[:, COL_SMALL + 4:COL_SMALL + 12]], axis=1)


def _big_weights(chipw):
    W = {}
    for n, w in chipw.items():
        if n in ("ffn1_wg", "ffn1_wu", "ffn2_wg", "ffn2_wu"):
            W[n] = w
        elif n in ("ffn1_wd", "ffn2_wd", "w_out", "pe_gate_w"):
            W[n] = w.reshape(-1, D_MODEL)
        elif n == "pe_proj":
            W[n] = jnp.moveaxis(w, 0, 1).reshape(PLE_DIM, D_MODEL)
        else:
            w_in = jnp.moveaxis(w[:, :, :IN_WIDTH // N_CHIPS], 0, 1).reshape(D_MODEL, IN_WIDTH)
            W["w_in_p"] = _w_in_permute(w_in)
    return W


def _small_weights(li, small):
    g = lambda n: small[n][li]
    W = {n: g(n) for n in ("ln1_g", "ln1_b", "ln2_g", "ln2_b", "ln3_g", "ln3_b", "pe_gate_b", "lru_conv_w",
                           "ssd_conv_w")}
    for n in ("lru_conv_b", "lru_ba", "lru_bx", "lru_lambda", "ssd_conv_b", "ssd_norm_g"):
        W[n] = g(n).reshape(1, -1)
    W["lru_wa_bd"] = _block_diag(g("lru_wa")).astype(BF16)
    W["lru_wx_bd"] = _block_diag(g("lru_wx")).astype(BF16)
    W["fox_bf_vec"] = _lane_vec(g("fox_bf"), LANE_F)
    W["ssd_dtb_vec"] = _lane_vec(g("ssd_dt_bias"), LANE_DT)
    W["ssd_a_vec"] = _lane_vec(-jnp.exp(g("ssd_a_log")), LANE_DT)
    W["ssd_d_exp"] = jnp.repeat(g("ssd_d"), HEAD_DIM).reshape(1, SSD_WIDTH)
    return W


def _big_grad_by_chip(G, n):
    if n in ("ffn1_wg", "ffn1_wu", "ffn2_wg", "ffn2_wu", "pe_proj"):
        return G[n]
    if n in ("ffn1_wd", "ffn2_wd", "w_out", "pe_gate_w"):
        return G[n].reshape(N_CHIPS, -1, D_MODEL)
    share = IN_WIDTH // N_CHIPS
    d_w_in = jnp.moveaxis(_w_in_unpermute(G["w_in_p"]).reshape(D_MODEL, N_CHIPS, share), 1, 0)
    return jnp.pad(d_w_in.astype(BF16), ((0, 0), (0, 0), (0, SHARE - share)))


def _layer_small_grads(G, W):
    out = {n: G[n] for n in ("lru_conv_w", "ssd_conv_w")}
    for n in ("ln1_g", "ln1_b", "ln2_g", "ln2_b", "ln3_g", "ln3_b", "pe_gate_b", "lru_conv_b", "lru_ba", "lru_bx",
              "lru_lambda", "ssd_conv_b", "ssd_norm_g"):
        out[n] = G[n].reshape(-1)
    out["lru_wa"] = _block_diag_extract(G["lru_wa_bd"])
    out["lru_wx"] = _block_diag_extract(G["lru_wx_bd"])
    out["fox_bf"] = G["fox_bf_vec"][0, LANE_F:LANE_F + ATT_HEADS]
    out["ssd_dt_bias"] = G["ssd_dtb_vec"][0, LANE_DT:LANE_DT + SSD_HEADS]
    out["ssd_a_log"] = G["ssd_a_vec"][0, LANE_DT:LANE_DT + SSD_HEADS] * W["ssd_a_vec"][0, LANE_DT:LANE_DT + SSD_HEADS]
    out["ssd_d"] = G["ssd_d_exp"].reshape(SSD_HEADS, HEAD_DIM).sum(axis=1)
    return out


WEIGHTS = ['ln1_g', 'ln1_b', 'ffn1_wg', 'ffn1_wu', 'ffn1_wd', 'w_in', 'lru_conv_w', 'lru_conv_b', 'lru_wa', 'lru_ba',
           'lru_wx', 'lru_bx', 'lru_lambda', 'fox_bf', 'ssd_conv_w', 'ssd_conv_b', 'ssd_dt_bias', 'ssd_a_log', 'ssd_d',
           'ssd_norm_g', 'w_out', 'ln2_g', 'ln2_b', 'ffn2_wg', 'ffn2_wu', 'ffn2_wd', 'ln3_g', 'ln3_b', 'pe_proj',
           'pe_gate_w', 'pe_gate_b']
FIRST = ((("ffn1_wg",), 1), (("ffn1_wu",), 1))
NEXT = ((("w_in",), 1),
        (("ffn1_wd",), 0))
EARLY = FIRST + NEXT
LATE = ((("ffn2_wg",), 1), (("ffn2_wu",), 1),
        (("ffn2_wd",), 0),
        (("w_out",), None), (("pe_gate_w",), None),
        (("pe_proj",), None))
BIG = {n: pad for names, pad in EARLY + LATE for n in names}
SMALL_SHARDED = {'lru_conv_w': 2, 'ssd_conv_w': 2}
PACK_COLS = 1024


def _unshard(seg, axis):
    moved = jnp.moveaxis(seg, 0, axis)
    shp = list(moved.shape)
    shp[axis:axis + 2] = [shp[axis] * shp[axis + 1]]
    return moved.reshape(shp)


def _pad_axis(a, axis, size):
    if axis is None or a.shape[axis] == size:
        return a
    pads = [(0, 0)] * a.ndim
    pads[axis] = (0, size - a.shape[axis])
    return jnp.pad(a, pads)


def _pack(arrs, dtype, cols):
    flat = jnp.concatenate([a.astype(dtype).reshape(-1) for a in arrs])
    pad = (-flat.shape[0]) % cols
    if pad:
        flat = jnp.concatenate([flat, jnp.zeros((pad,), dtype)])
    return flat.reshape(-1, cols)


def _unpack(flat, shapes):
    out, off = [], 0
    for s in shapes:
        n = math.prod(s)
        out.append(flat[off:off + n].reshape(s))
        off += n
    return out


def kernel(x, p, ln1_g, ln1_b, ffn1_wg, ffn1_wu, ffn1_wd, w_in, lru_conv_w, lru_conv_b, lru_wa, lru_ba, lru_wx, lru_bx, lru_lambda, fox_bf, ssd_conv_w, ssd_conv_b, ssd_dt_bias, ssd_a_log, ssd_d, ssd_norm_g, w_out, ln2_g, ln2_b, ffn2_wg, ffn2_wu, ffn2_wd, ln3_g, ln3_b, pe_proj, pe_gate_w, pe_gate_b, loss_target, m_ln1_g, m_ln1_b, m_ffn1_wg, m_ffn1_wu, m_ffn1_wd, m_w_in, m_lru_conv_w, m_lru_conv_b, m_lru_wa, m_lru_ba, m_lru_wx, m_lru_bx, m_lru_lambda, m_fox_bf, m_ssd_conv_w, m_ssd_conv_b, m_ssd_dt_bias, m_ssd_a_log, m_ssd_d, m_ssd_norm_g, m_w_out, m_ln2_g, m_ln2_b, m_ffn2_wg, m_ffn2_wu, m_ffn2_wd, m_ln3_g, m_ln3_b, m_pe_proj, m_pe_gate_w, m_pe_gate_b, v_ln1_g, v_ln1_b, v_ffn1_wg, v_ffn1_wu, v_ffn1_wd, v_w_in, v_lru_conv_w, v_lru_conv_b, v_lru_wa, v_lru_ba, v_lru_wx, v_lru_bx, v_lru_lambda, v_fox_bf, v_ssd_conv_w, v_ssd_conv_b, v_ssd_dt_bias, v_ssd_a_log, v_ssd_d, v_ssd_norm_g, v_w_out, v_ln2_g, v_ln2_b, v_ffn2_wg, v_ffn2_wu, v_ffn2_wd, v_ln3_g, v_ln3_b, v_pe_proj, v_pe_gate_w, v_pe_gate_b):
    args = locals()
    w_loc = {n: args[n] for n in WEIGHTS}
    m_loc = {n: args["m_" + n] for n in WEIGHTS}
    v_loc = {n: args["v_" + n] for n in WEIGHTS}
    chip = 2 * lax.axis_index("x") + lax.axis_index("y")
    core = lax.axis_index("c")
    big = list(BIG)
    small_sh = list(SMALL_SHARDED)
    small_rep = [n for n in WEIGHTS if n not in BIG and n not in SMALL_SHARDED]

    def srcs_of(li, groups):
        return [jnp.stack([_pad_axis(w_loc[n][li].astype(BF16), pad, SHARE) for n in names]) for names, pad in groups]

    def gather_comm(li, groups, base=0):
        return _gather_layer_comm(srcs_of(li, groups), li, base)

    def chip_weights(gathered, groups):
        return _big_weights({n: g[:, j] for (names, _), g in zip(groups, gathered) for j, n in enumerate(names)})

    def pair_sums(G, groups, tag):
        gs = [jnp.stack([_big_grad_by_chip(G, n) for n in names]) for names, _ in groups]
        flat = [g.reshape((-1,) + g.shape[2:]) for g in gs]
        theirs = _exchange(flat, ("c",), swap=True, name=f"reduce_cores_{tag}")
        return [_sum_pair(f, r, BF16, name=f"reduce_cores_sum_{tag}_{gi}").reshape(g.shape)
                for gi, (f, r, g) in enumerate(zip(flat, theirs, gs))]

    def finish_reduce(quad, sums, li, groups, tag):
        quad = [lax.dynamic_update_index_in_dim(q, lax.dynamic_index_in_dim(s, chip, 1, keepdims=False), chip, 0)
                for q, s in zip(quad, sums)]
        red = [_sum_slots(q.reshape(N_CHIPS, -1, q.shape[-1]), F32,
                          name=f"reduce_chips_sum_{tag}_{gi}").reshape(q.shape[1:]) for gi, q in enumerate(quad)]
        theirs = _exchange(red, ("c",), swap=True, name=f"reduce_share_{tag}")
        out = {}
        for (names, _), r, rv in zip(groups, red, theirs):
            both = jnp.where(core == li, r, rv)
            for j, n in enumerate(names):
                out[n] = both[j]
        return out

    everything = EARLY + LATE
    first0 = _run_comm(gather_comm(0, FIRST), name="gather_w_l0")
    small = {n: w_loc[n] for n in small_rep}
    spack = _pack([w_loc[n] for n in small_sh], F32, LANES)
    (sg,) = _exchange([spack[None]], ("x", "y"), name="gather_conv_w")
    for n, seg in zip(small_sh, _unpack_rows(sg.reshape(N_CHIPS, -1), [w_loc[n].shape for n in small_sh])):
        small[n] = _unshard(seg, SMALL_SHARDED[n])

    W0 = {**_small_weights(0, small), **chip_weights(first0, FIRST)}
    late0_comm = gather_comm(0, LATE)
    early1 = []

    def in_attention0(got):
        early1.extend(got[len(LATE):])
        return chip_weights(got[:len(LATE)], LATE)

    xs = x[0]
    xs, xb, sv0, W0 = _layer_fwd(
        0, xs, xs.astype(BF16), p[0, 0].astype(BF16), W0,
        up=(gather_comm(0, NEXT), lambda got: chip_weights(got, NEXT)),
        att=(_merge_comms([late0_comm, gather_comm(1, EARLY, base=late0_comm.n_sems)]), in_attention0))
    W1 = {**_small_weights(1, small), **chip_weights(early1, EARLY)}
    xs, _, sv1, W1 = _layer_fwd(1, xs, xb, p[1, 0].astype(BF16), W1,
                                att=(gather_comm(1, LATE), lambda got: chip_weights(got, LATE)))
    dx, loss = _loss_kernel(xs, loss_target[0], name="loss")
    loss = lax.psum(loss[0, 0], MESH_AXES)
    dx, G1, _, _ = _layer_bwd(1, dx, sv1, W1)
    sums1 = pair_sums(G1, everything, "l1")
    comm1 = _reduce_chips_comm(sums1, 1)
    late_sums, early_sums = [], []

    def late0(G):
        late_sums.extend(pair_sums(G, LATE, "l0_late"))
        return _reduce_chips_comm(late_sums, 0, base=comm1.n_sems)

    def last0(G):
        early_sums.extend(pair_sums(G, EARLY, "l0"))
        return _reduce_chips_comm(early_sums, 0)

    grad_x, G0, quads, quads0 = _layer_bwd(0, dx, sv0, W0, comm=comm1, late=late0, last=last0)

    n1 = len(comm1.out_shapes)
    red = [{**finish_reduce(quads[n1:], late_sums, 0, LATE, "l0_late"),
            **finish_reduce(quads0, early_sums, 0, EARLY, "l0")},
           finish_reduce(quads[:n1], sums1, 1, everything, "l1")]
    g_red = {}
    for n in big:
        g = jnp.stack([red[li][n] for li in range(DEPTH)])
        g_red[n] = g[tuple(slice(0, s) for s in w_loc[n].shape)]
    small_l = [_layer_small_grads(G0, W0), _layer_small_grads(G1, W1)]
    g_small = {n: jnp.stack([small_l[li][n] for li in range(DEPTH)]) for n in small_l[0]}
    small_all = small_rep + small_sh
    sgp = _pack([g_small[n] for n in small_all], F32, PACK_COLS)
    (sall,) = _exchange([sgp[None]], MESH_AXES, name="reduce_small")
    sred = _sum_slots(sall.reshape((2 ** len(MESH_AXES),) + sgp.shape), F32, name="reduce_small_sum").reshape(-1)
    for n, g in zip(small_all, _unpack(sred, [g_small[n].shape for n in small_all])):
        if n in SMALL_SHARDED:
            width = w_loc[n].shape[-1]
            g = lax.dynamic_slice_in_dim(g, chip * width, width, axis=SMALL_SHARDED[n])
        g_red[n] = g

    delta, new_m, new_v = {}, {}, {}
    for n in big:
        delta[n], new_m[n], new_v[n] = _adamw(w_loc[n], g_red[n], m_loc[n], v_loc[n], name="adamw_" + n)
    shapes = [w_loc[n].shape for n in small_all]
    packs = [_pack([d[n] for n in small_all], F32, LANES) for d in (w_loc, g_red, m_loc, v_loc)]
    outs = _adamw(*packs, name="adamw_small")
    for d, o in zip((delta, new_m, new_v), outs):
        for n, a in zip(small_all, _unpack(o.reshape(-1), shapes)):
            d[n] = a
    return (loss, grad_x[None], *[g_red[n] for n in WEIGHTS], *[delta[n] for n in WEIGHTS],
            *[new_m[n] for n in WEIGHTS], *[new_v[n] for n in WEIGHTS])


def _unpack_rows(gathered, shapes):
    out, off = [], 0
    for s in shapes:
        n = math.prod(s)
        out.append(gathered[:, off:off + n].reshape((N_CHIPS,) + tuple(s)))
        off += n
    return out
```

```python
import math

import jax
import jax.numpy as jnp
from jax import lax
from jax.experimental import pallas as pl
from jax.experimental.pallas import tpu as pltpu

F32 = jnp.float32
BF16 = jnp.bfloat16

D_MODEL = 1024
DEPTH = 2
PLE_DIM = 256
HEAD_DIM = 64
LRU_WIDTH = 256
LRU_HEADS = 4
LRU_C = 8.0
CONV_K = 4
ATT_WIDTH = 256
ATT_HEADS = 4
SSD_WIDTH = 512
SSD_HEADS = 8
SSD_GROUPS = 2
SSD_STATE = 128
SSD_CHUNK = 128
SSD_CONV_DIM = 1024
ALPHA = (2.0 * DEPTH) ** 0.25
LN_EPS = 1e-5
RMS_EPS = 1e-5
IN_WIDTH = 2828
ADAM_LR = 0.001
ADAM_B1 = 0.9
ADAM_B2 = 0.999
ADAM_EPS = 1e-08
ADAM_WD = 0.01
ADAM_STEP = 10

H_WIDTH = 3072
COL_XBC, COL_Z, COL_U, COL_G, COL_Q, COL_K, COL_V, COL_SMALL = 0, 1024, 1536, 1792, 2048, 2304, 2560, 2816
LANE_F = 0
LANE_DT = 4
LANES = 128
SUBLANES = 8
NEG = -1e30

VMEM_LIMIT = 48 * 1024 * 1024

N_CHIPS = 4
MESH_AXES = ("x", "y", "c")
SHARE = 768


def _params(n):
    return pltpu.CompilerParams(dimension_semantics=("arbitrary",) * n, vmem_limit_bytes=VMEM_LIMIT)


def _pick(n, cands):
    for c in cands:
        if n % c == 0:
            return c
    return n


def _iota(shape, dim):
    return lax.broadcasted_iota(jnp.int32, shape, dim)


def _shift_down(x, s, prev8):
    if s == 0:
        return x
    r = pltpu.roll(x, s, 0)
    pr = pltpu.roll(prev8, s, 0)
    head = jnp.where(_iota(pr.shape, 0) < s, pr, r[:SUBLANES])
    return jnp.concatenate([head, r[SUBLANES:]], axis=0)


def _shift_up(x, s, next8):
    if s == 0:
        return x
    n = x.shape[0]
    r = pltpu.roll(x, n - s, 0)
    nr = pltpu.roll(next8, SUBLANES - s, 0)
    tail = jnp.where(_iota(nr.shape, 0) >= SUBLANES - s, nr, r[n - SUBLANES:])
    return jnp.concatenate([r[:n - SUBLANES], tail], axis=0)


def _scan_fwd(a, b):
    n = a.shape[0]
    row = _iota(a.shape, 0)
    d = 1
    while d < n:
        keep = row >= d
        a_s = jnp.where(keep, pltpu.roll(a, d, 0), 1.0)
        b_s = jnp.where(keep, pltpu.roll(b, d, 0), 0.0)
        b = a * b_s + b
        a = a * a_s
        d *= 2
    return a, b


def _scan_bwd(a, b):
    n = a.shape[0]
    row = _iota(a.shape, 0)
    d = 1
    while d < n:
        keep = row < n - d
        a_s = jnp.where(keep, pltpu.roll(a, n - d, 0), 1.0)
        b_s = jnp.where(keep, pltpu.roll(b, n - d, 0), 0.0)
        b = a * b_s + b
        a = a * a_s
        d *= 2
    return a, b


def _cumsum_rows(x, reverse=False):
    n = x.shape[0]
    row = _iota(x.shape, 0)
    d = 1
    while d < n:
        if reverse:
            x = x + jnp.where(row < n - d, pltpu.roll(x, n - d, 0), 0.0)
        else:
            x = x + jnp.where(row >= d, pltpu.roll(x, d, 0), 0.0)
        d *= 2
    return x


def _col(x, lane):
    return jnp.sum(jnp.where(_iota(x.shape, 1) == lane, x, 0.0), axis=1, keepdims=True)


def _row(x, r):
    return jnp.sum(jnp.where(_iota(x.shape, 0) == r, x, 0.0), axis=0, keepdims=True)


def _sigmoid(x):
    return jax.nn.sigmoid(x)


def _softplus(x):
    return jnp.maximum(x, 0.0) + jnp.log(1.0 + jnp.exp(-jnp.abs(x)))


def _gelu_and_grad(x):
    c0 = math.sqrt(2.0 / math.pi)
    inner = c0 * (x + 0.044715 * x * x * x)
    t = jnp.tanh(inner)
    g = 0.5 * x * (1.0 + t)
    dg = 0.5 * (1.0 + t) + 0.5 * x * (1.0 - t * t) * c0 * (1.0 + 3.0 * 0.044715 * x * x)
    return g, dg


def _dot(a, b, ca, cb):
    return lax.dot_general(a, b, (((ca,), (cb,)), ((), ())), preferred_element_type=F32)


def _conv_taps(xr, prev8, w, bias):
    y = bias + w[CONV_K - 1:CONV_K, :] * xr
    for j in range(CONV_K - 1):
        y = y + w[j:j + 1, :] * _shift_down(xr, CONV_K - 1 - j, prev8)
    return y


def _conv_taps_bwd(dy, next8, w, xr):
    dx = None
    dws = []
    for j in range(CONV_K):
        sh = _shift_up(dy, CONV_K - 1 - j, next8)
        term = w[j:j + 1, :] * sh
        dx = term if dx is None else dx + term
        dws.append(jnp.sum(sh * xr, axis=0, keepdims=True))
    return dx, jnp.concatenate(dws, axis=0)


def _head_expand(v, lane0, nheads, width):
    rows = v.shape[0]
    colhead = _iota((rows, width), 1) // HEAD_DIM
    out = jnp.zeros((rows, width), F32)
    for h in range(nheads):
        out = jnp.where(colhead == h, _col(v, lane0 + h), out)
    return out


def _head_reduce(x, lane0, nheads):
    rows = x.shape[0]
    colhead = _iota(x.shape, 1) // HEAD_DIM
    lane = _iota((rows, LANES), 1)
    out = jnp.zeros((rows, LANES), F32)
    for h in range(nheads):
        s = jnp.sum(jnp.where(colhead == h, x, 0.0), axis=1, keepdims=True)
        out = jnp.where(lane == lane0 + h, s, out)
    return out


def _mm(a, b, *, ta=False, tb=False, scale=1.0, out_dtype=F32, chip_cols=False, name):
    if ta:
        kk, m = a.shape
    else:
        m, kk = a.shape
    n = b.shape[0] if tb else b.shape[1]
    tm = _pick(m, (1024, 512, 256, 128))
    tk = _pick(kk, (1024, 768, 512, 256, 128))
    nk = kk // tk
    dn_a = 0 if ta else 1
    dn_b = 1 if tb else 0
    share = n // N_CHIPS
    if chip_cols:
        tn = n
        out_spec = pl.BlockSpec((N_CHIPS, tm, share), lambda i, j, k: (0, i, 0))
        out_shape = jax.ShapeDtypeStruct((N_CHIPS, m, share), out_dtype)
    else:
        tn = _pick(n, (1024, 768, 512, 256, 128))
        out_spec = pl.BlockSpec((tm, tn), lambda i, j, k: (i, j))
        out_shape = jax.ShapeDtypeStruct((m, n), out_dtype)

    def body(a_ref, b_ref, o_ref, acc):
        k = pl.program_id(2)

        @pl.when(k == 0)
        def _():
            acc[...] = jnp.zeros_like(acc)

        acc[...] += _dot(a_ref[...].astype(BF16), b_ref[...].astype(BF16), dn_a, dn_b)

        @pl.when(k == nk - 1)
        def _():
            if chip_cols:
                for c in range(N_CHIPS):
                    o_ref[c] = (acc[:, share * c:share * (c + 1)] * scale).astype(out_dtype)
            else:
                o_ref[...] = (acc[...] * scale).astype(out_dtype)

    a_spec = pl.BlockSpec((tk, tm), lambda i, j, k: (k, i)) if ta else pl.BlockSpec((tm, tk), lambda i, j, k: (i, k))
    b_spec = pl.BlockSpec((tn, tk), lambda i, j, k: (j, k)) if tb else pl.BlockSpec((tk, tn), lambda i, j, k: (k, j))
    return pl.pallas_call(
        body, name=name, grid=(m // tm, n // tn, nk),
        in_specs=[a_spec, b_spec],
        out_specs=out_spec, out_shape=out_shape,
        scratch_shapes=[pltpu.VMEM((tm, tn), F32)],
        compiler_params=_params(3),
    )(a, b)


def _mm_swiglu(xb, wg, wu, *, comm=None, name):
    t, d = xb.shape
    share = wg.shape[2]
    n = N_CHIPS * share
    tm = _pick(t, (512, 256, 128))
    tn = _pick(share, (768, 256, 128))
    per = share // tn

    def body(x_ref, wg_ref, wu_ref, g_ref, u_ref, a_ref):
        x = x_ref[...]
        g = _dot(x, wg_ref[...], 1, 0)
        u = _dot(x, wu_ref[...], 1, 0)
        g_ref[...] = g.astype(BF16)
        u_ref[...] = u.astype(BF16)
        a_ref[...] = (g * _sigmoid(g) * u).astype(BF16)

    o = jax.ShapeDtypeStruct((t, n), BF16)
    ospec = pl.BlockSpec((tm, tn), lambda j, i: (i, j))
    return _hosted_call(
        body, comm, (n // tn, t // tm), name=name,
        in_specs=[pl.BlockSpec((tm, d), lambda j, i: (i, 0)),
                  pl.BlockSpec((None, d, tn), lambda j, i: (j // per, 0, j % per)),
                  pl.BlockSpec((None, d, tn), lambda j, i: (j // per, 0, j % per))],
        out_specs=[ospec, ospec, ospec], out_shape=[o, o, o], scratch_shapes=[], args=[xb, wg, wu])


def _mm_swiglu_bwd(dr, wd, g, u, *, scale, name):
    t, d = dr.shape
    n = wd.shape[0]
    tm = _pick(t, (512, 256, 128))
    tn = _pick(n, (768, 256, 128))

    def body(dr_ref, wd_ref, g_ref, u_ref, dg_ref, du_ref):
        da = _dot(dr_ref[...].astype(BF16), wd_ref[...], 1, 1) * scale
        gg = g_ref[...].astype(F32)
        uu = u_ref[...].astype(F32)
        sg = _sigmoid(gg)
        dg_ref[...] = (da * uu * (sg * (1.0 + gg * (1.0 - sg)))).astype(BF16)
        du_ref[...] = (da * gg * sg).astype(BF16)

    o = jax.ShapeDtypeStruct((t, n), BF16)
    ospec = pl.BlockSpec((tm, tn), lambda j, i: (i, j))
    return pl.pallas_call(
        body, name=name, grid=(n // tn, t // tm),
        in_specs=[pl.BlockSpec((tm, d), lambda j, i: (i, 0)),
                  pl.BlockSpec((tn, d), lambda j, i: (j, 0)),
                  ospec, ospec],
        out_specs=[ospec, ospec], out_shape=[o, o],
        compiler_params=_params(2),
    )(dr, wd, g, u)


def _mm_ln(a, w, resid, gain, bias, *, rscale, mscale, name):
    t, kk = a.shape
    d = w.shape[1]
    tm = _pick(t, (512, 256, 128))
    tk = kk
    nk = kk // tk

    def body(a_ref, w_ref, r_ref, g_ref, b_ref, y_ref, yb_ref, xh_ref, rs_ref, acc):
        k = pl.program_id(1)

        @pl.when(k == 0)
        def _():
            acc[...] = jnp.zeros_like(acc)

        acc[...] += _dot(a_ref[...].astype(BF16), w_ref[...], 1, 0)

        @pl.when(k == nk - 1)
        def _():
            r = rscale * r_ref[...] + mscale * acc[...]
            mu = jnp.mean(r, axis=1, keepdims=True)
            xc = r - mu
            var = jnp.mean(xc * xc, axis=1, keepdims=True)
            rstd = lax.rsqrt(var + LN_EPS)
            xh = xc * rstd
            y = xh * g_ref[...] + b_ref[...]
            y_ref[...] = y
            yb_ref[...] = y.astype(BF16)
            xh_ref[...] = xh
            rs_ref[...] = rstd

    row = pl.BlockSpec((tm, d), lambda i, k: (i, 0))
    vec = pl.BlockSpec((1, d), lambda i, k: (0, 0))
    return pl.pallas_call(
        body, name=name, grid=(t // tm, nk),
        in_specs=[pl.BlockSpec((tm, tk), lambda i, k: (i, k)),
                  pl.BlockSpec((tk, d), lambda i, k: (k, 0)), row, vec, vec],
        out_specs=[row, row, row, pl.BlockSpec((tm, 1), lambda i, k: (i, 0))],
        out_shape=[jax.ShapeDtypeStruct((t, d), F32), jax.ShapeDtypeStruct((t, d), BF16),
                   jax.ShapeDtypeStruct((t, d), F32), jax.ShapeDtypeStruct((t, 1), F32)],
        scratch_shapes=[pltpu.VMEM((tm, d), F32)],
        compiler_params=_params(2),
    )(a, w, resid, gain.reshape(1, d), bias.reshape(1, d))


def _bwd_proj(pairs, resid, *, rscale, ln, comm=None, name):
    t, kk = pairs[0][0].shape
    d = pairs[0][1].shape[-2]
    has_ln = ln is not None
    tm = _pick(t, (512, 256, 128) if has_ln else (1024, 512, 256, 128))
    tk = _pick(pairs[0][1].shape[-1], (1024, 768, 512, 256, 128))
    nk = kk // tk
    nt = t // tm
    npair = len(pairs)

    def body(*refs):
        ab = refs[:2 * npair]
        r_ref = refs[2 * npair]
        pos = 2 * npair + 1
        if has_ln:
            xh_ref, rs_ref, g_ref = refs[pos:pos + 3]
            pos += 3
            o_ref, ob_ref, dg_ref, db_ref = refs[pos:pos + 4]
            pos += 4
        else:
            o_ref = refs[pos]
            pos += 1
        acc = refs[pos]
        i = pl.program_id(0)
        k = pl.program_id(1)

        @pl.when(k == 0)
        def _():
            acc[...] = jnp.zeros_like(acc)

        for q in range(npair):
            acc[...] += _dot(ab[2 * q][...].astype(BF16), ab[2 * q + 1][...], 1, 1)

        @pl.when(k == nk - 1)
        def _():
            dy = rscale * r_ref[...] + acc[...]
            if not has_ln:
                o_ref[...] = dy
                return
            xh = xh_ref[...]
            w = dy * g_ref[...]
            m1 = jnp.mean(w, axis=1, keepdims=True)
            m2 = jnp.mean(w * xh, axis=1, keepdims=True)
            dr = rs_ref[...] * (w - m1 - xh * m2)
            o_ref[...] = dr
            ob_ref[...] = dr.astype(BF16)

            @pl.when(i == 0)
            def _():
                dg_ref[...] = jnp.zeros_like(dg_ref)
                db_ref[...] = jnp.zeros_like(db_ref)

            dg_ref[...] += jnp.sum(dy * xh, axis=0, keepdims=True)
            db_ref[...] += jnp.sum(dy, axis=0, keepdims=True)

    row = pl.BlockSpec((tm, d), lambda i, k: (i, 0))
    vec = pl.BlockSpec((1, d), lambda i, k: (0, 0))
    in_specs, args = [], []
    for a, b in pairs:
        if b.ndim == 3:
            per = b.shape[2] // tk
            b_spec = pl.BlockSpec((None, d, tk), lambda i, k, per=per: (k // per, 0, k % per))
        else:
            b_spec = pl.BlockSpec((d, tk), lambda i, k: (0, k))
        in_specs += [pl.BlockSpec((tm, tk), lambda i, k: (i, k)), b_spec]
        args += [a, b]
    in_specs.append(row)
    args.append(resid)
    out_specs = [row]
    out_shape = [jax.ShapeDtypeStruct((t, d), F32)]
    if has_ln:
        xh, rs, gain = ln
        in_specs += [row, pl.BlockSpec((tm, 1), lambda i, k: (i, 0)), vec]
        args += [xh, rs, gain.reshape(1, d)]
        out_specs += [row, vec, vec]
        out_shape += [jax.ShapeDtypeStruct((t, d), BF16)] + [jax.ShapeDtypeStruct((1, d), F32)] * 2
    outs, got = _hosted_call(body, comm, (nt, nk), name=name, in_specs=in_specs, out_specs=out_specs,
                             out_shape=out_shape, scratch_shapes=[pltpu.VMEM((tm, d), F32)], args=args)
    return tuple(outs) if comm is None else tuple(outs) + (got,)


def _mm_pe(x3, x3b, pb, wgate, bgate, wproj, *, name):
    t, d = x3.shape
    pd = pb.shape[1]
    tm = _pick(t, (512, 256, 128))
    tn = _pick(d, (512, 256, 128))

    def body(x_ref, xb_ref, p_ref, wg_ref, bg_ref, wp_ref, y_ref, yb_ref, sg_ref, e_ref):
        sg = _sigmoid(_dot(xb_ref[...], wg_ref[...], 1, 0) + bg_ref[...])
        e = _dot(p_ref[...], wp_ref[...], 1, 0)
        y = x_ref[...] + sg * e
        y_ref[...] = y
        yb_ref[...] = y.astype(BF16)
        sg_ref[...] = sg.astype(BF16)
        e_ref[...] = e.astype(BF16)

    ospec = pl.BlockSpec((tm, tn), lambda i, j: (i, j))
    ob = jax.ShapeDtypeStruct((t, d), BF16)
    return pl.pallas_call(
        body, name=name, grid=(t // tm, d // tn),
        in_specs=[ospec, pl.BlockSpec((tm, d), lambda i, j: (i, 0)), pl.BlockSpec((tm, pd), lambda i, j: (i, 0)),
                  pl.BlockSpec((d, tn), lambda i, j: (0, j)), pl.BlockSpec((1, tn), lambda i, j: (0, j)),
                  pl.BlockSpec((pd, tn), lambda i, j: (0, j))],
        out_specs=[ospec, ospec, ospec, ospec],
        out_shape=[jax.ShapeDtypeStruct((t, d), F32), ob, ob, ob],
        compiler_params=_params(2),
    )(x3, x3b, pb, wgate, bgate.reshape(1, d), wproj)


def _pe_bwd_elem(dx4, sg, e, *, name):
    t, d = dx4.shape
    tm = _pick(t, (512, 256, 128))

    def body(dx_ref, sg_ref, e_ref, dgp_ref, de_ref, db_ref):
        dx = dx_ref[...]
        s = sg_ref[...].astype(F32)
        dgp = dx * e_ref[...].astype(F32) * s * (1.0 - s)
        dgp_ref[...] = dgp.astype(BF16)
        de_ref[...] = (dx * s).astype(BF16)

        @pl.when(pl.program_id(0) == 0)
        def _():
            db_ref[...] = jnp.zeros_like(db_ref)

        db_ref[...] += jnp.sum(dgp, axis=0, keepdims=True)

    row = pl.BlockSpec((tm, d), lambda i: (i, 0))
    ob = jax.ShapeDtypeStruct((t, d), BF16)
    return pl.pallas_call(
        body, name=name, grid=(t // tm,), in_specs=[row, row, row],
        out_specs=[row, row, pl.BlockSpec((1, d), lambda i: (0, 0))],
        out_shape=[ob, ob, jax.ShapeDtypeStruct((1, d), F32)],
        compiler_params=_params(1),
    )(dx4, sg, e)


def _loss_kernel(y, target, *, name):
    t, d = y.shape
    tm = _pick(t, (512, 256, 128))

    def body(y_ref, t_ref, dy_ref, l_ref):
        diff = y_ref[...] - t_ref[...]
        dy_ref[...] = diff * (1.0 / d)

        @pl.when(pl.program_id(0) == 0)
        def _():
            l_ref[...] = jnp.zeros_like(l_ref)

        part = jnp.sum(jnp.mean(diff * diff, axis=1, keepdims=True), axis=0, keepdims=True)
        l_ref[...] += 0.5 * part

    row = pl.BlockSpec((tm, d), lambda i: (i, 0))
    return pl.pallas_call(
        body, name=name, grid=(t // tm,), in_specs=[row, row],
        out_specs=[row, pl.BlockSpec((1, 1), lambda i: (0, 0))],
        out_shape=[jax.ShapeDtypeStruct((t, d), F32), jax.ShapeDtypeStruct((1, 1), F32)],
        compiler_params=_params(1),
    )(y, target)


LRU_TM = 256


def _lru_gate_terms(r, lam):
    sp = _softplus(-lam)
    la = -LRU_C * r * sp
    a = jnp.exp(la)
    em = jnp.tanh(la) * (jnp.exp(2.0 * la) + 1.0)
    s = jnp.sqrt(-em)
    return la, a, s, sp


def _lru_fwd(hbuf, conv_w, conv_b, wa, ba, wx, bx, lam, *, name):
    t = hbuf.shape[0]
    w = LRU_WIDTH
    tm = _pick(t, (LRU_TM, 128))
    cu, cg = COL_U // w, COL_G // w
    hb = tm // SUBLANES

    def body(u_ref, up_ref, g_ref, cw_ref, cb_ref, wa_ref, ba_ref, wx_ref, bx_ref, lam_ref,
             y_ref, u_out, r_out, i_out, a_out, h_out, carry):
        i = pl.program_id(0)

        @pl.when(i == 0)
        def _():
            carry[...] = jnp.zeros_like(carry)

        prev = jnp.where(i == 0, 0.0, up_ref[...])
        u = _conv_taps(u_ref[...], prev, cw_ref[...], cb_ref[...])
        ub = u.astype(BF16)
        r = _sigmoid(_dot(ub, wa_ref[...], 1, 0) + ba_ref[...])
        ig = _sigmoid(_dot(ub, wx_ref[...], 1, 0) + bx_ref[...])
        _, a, s, _ = _lru_gate_terms(r, lam_ref[...])
        b = s * (ig * u)
        acum, hs = _scan_fwd(a, b)
        h = hs + acum * carry[0:1, :]
        carry[...] = jnp.broadcast_to(h[tm - 1:tm, :], carry.shape)
        gl, _ = _gelu_and_grad(g_ref[...])
        y_ref[...] = h * gl
        u_out[...] = u
        r_out[...] = r
        i_out[...] = ig
        a_out[...] = a
        h_out[...] = h

    row = pl.BlockSpec((tm, w), lambda i: (i, 0))
    vec = pl.BlockSpec((1, w), lambda i: (0, 0))
    mat = pl.BlockSpec((w, w), lambda i: (0, 0))
    o = jax.ShapeDtypeStruct((t, w), F32)
    return pl.pallas_call(
        body, name=name, grid=(t // tm,),
        in_specs=[pl.BlockSpec((tm, w), lambda i: (i, cu)),
                  pl.BlockSpec((SUBLANES, w), lambda i: (jnp.maximum(i * hb - 1, 0), cu)),
                  pl.BlockSpec((tm, w), lambda i: (i, cg)),
                  pl.BlockSpec((CONV_K, w), lambda i: (0, 0)), vec, mat, vec, mat, vec, vec],
        out_specs=[row] * 6, out_shape=[o] * 6,
        scratch_shapes=[pltpu.VMEM((SUBLANES, w), F32)],
        compiler_params=_params(1),
    )(hbuf, hbuf, hbuf, conv_w, conv_b, wa, ba, wx, bx, lam)


def _lru_bwd(dymix, hbuf, u, r, ig, a, h, conv_w, wa, wx, lam, *, name):
    t = hbuf.shape[0]
    w = LRU_WIDTH
    tm = _pick(t, (LRU_TM, 128))
    nb = t // tm
    cu, cg = COL_U // w, COL_G // w
    hb = tm // SUBLANES
    last8 = t // SUBLANES - 1

    def body(dy_ref, ur_ref, g_ref, u_ref, r_ref, i_ref, a_ref, an_ref, h_ref, hp_ref,
             cw_ref, wa_ref, wx_ref, lam_ref,
             dur_ref, dgr_ref, dcw_ref, dcb_ref, dwa_ref, dba_ref, dwx_ref, dbx_ref, dlam_ref,
             lcarry, dnext):
        i = pl.program_id(0)
        ib = nb - 1 - i

        @pl.when(i == 0)
        def _():
            lcarry[...] = jnp.zeros_like(lcarry)
            dnext[...] = jnp.zeros_like(dnext)
            for ref in (dcw_ref, dcb_ref, dwa_ref, dba_ref, dwx_ref, dbx_ref, dlam_ref):
                ref[...] = jnp.zeros_like(ref)

        dy = dy_ref[...]
        hh = h_ref[...]
        av = a_ref[...]
        uu = u_ref[...]
        rr = r_ref[...]
        ii = i_ref[...]
        lam_v = lam_ref[...]
        gl, dgl = _gelu_and_grad(g_ref[...])
        dgr_ref[...] = (dy * hh * dgl).astype(BF16)
        dh_out = dy * gl
        a_next = _shift_up(av, 1, jnp.where(ib == nb - 1, 0.0, an_ref[...]))
        acum, ls = _scan_bwd(a_next, dh_out)
        lam_adj = ls + acum * lcarry[0:1, :]
        lcarry[...] = jnp.broadcast_to(lam_adj[0:1, :], lcarry.shape)
        h_prev = _shift_down(hh, 1, jnp.where(ib == 0, 0.0, hp_ref[...]))
        da = lam_adj * h_prev
        _, a2, s, sp = _lru_gate_terms(rr, lam_v)
        d_igu = lam_adj * s
        ds = lam_adj * ii * uu
        dla = da * a2 - ds * (a2 * a2) / s
        dr = dla * (-LRU_C * sp)
        dlam_ref[...] += jnp.sum(dla * (LRU_C * rr * _sigmoid(-lam_v)), axis=0, keepdims=True)
        dpre_r = dr * rr * (1.0 - rr)
        dpre_i = d_igu * uu * ii * (1.0 - ii)
        prb = dpre_r.astype(BF16)
        pib = dpre_i.astype(BF16)
        ub = uu.astype(BF16)
        du = d_igu * ii + _dot(prb, wa_ref[...], 1, 1) + _dot(pib, wx_ref[...], 1, 1)
        dwa_ref[...] += _dot(ub, prb, 0, 0)
        dwx_ref[...] += _dot(ub, pib, 0, 0)
        dba_ref[...] += jnp.sum(dpre_r, axis=0, keepdims=True)
        dbx_ref[...] += jnp.sum(dpre_i, axis=0, keepdims=True)
        dur, dws = _conv_taps_bwd(du, dnext[...], cw_ref[...], ur_ref[...])
        dur_ref[...] = dur.astype(BF16)
        dcw_ref[...] += dws
        dcb_ref[...] += jnp.sum(du, axis=0, keepdims=True)
        dnext[...] = du[:SUBLANES]

    def rowspec(col):
        return pl.BlockSpec((tm, w), lambda i: (nb - 1 - i, col))

    row = rowspec(0)
    nxt = pl.BlockSpec((SUBLANES, w), lambda i: (jnp.minimum((nb - i) * hb, last8), 0))
    prv = pl.BlockSpec((SUBLANES, w), lambda i: (jnp.maximum((nb - 1 - i) * hb - 1, 0), 0))
    vec = pl.BlockSpec((1, w), lambda i: (0, 0))
    mat = pl.BlockSpec((w, w), lambda i: (0, 0))
    cw = pl.BlockSpec((CONV_K, w), lambda i: (0, 0))
    o = jax.ShapeDtypeStruct((t, w), BF16)
    v1 = jax.ShapeDtypeStruct((1, w), F32)
    m1 = jax.ShapeDtypeStruct((w, w), F32)
    return pl.pallas_call(
        body, name=name, grid=(nb,),
        in_specs=[rowspec(0), rowspec(cu), rowspec(cg), row, row, row, row, nxt, row, prv, cw, mat, mat, vec],
        out_specs=[row, row, cw, vec, mat, vec, mat, vec, vec],
        out_shape=[o, o, jax.ShapeDtypeStruct((CONV_K, w), F32), v1, m1, v1, m1, v1, v1],
        scratch_shapes=[pltpu.VMEM((SUBLANES, w), F32), pltpu.VMEM((SUBLANES, w), F32)],
        compiler_params=_params(1),
    )(dymix, hbuf, hbuf, u, r, ig, a, a, h, h, conv_w, wa, wx, lam)


FOX_T = 512
FOX_PREP_TM = 256


def _log_sigmoid(x):
    return jnp.minimum(x, 0.0) - jnp.log(1.0 + jnp.exp(-jnp.abs(x)))


def _fox_prep(hbuf, bf_vec, *, name):
    t = hbuf.shape[0]
    tm = _pick(t, (FOX_PREP_TM, 128))
    cs = COL_SMALL // LANES

    def body(s_ref, b_ref, eq_ref, ek_ref, carry):
        i = pl.program_id(0)

        @pl.when(i == 0)
        def _():
            carry[...] = jnp.zeros_like(carry)

        lf = _log_sigmoid(s_ref[...] + b_ref[...])
        f = _cumsum_rows(lf) + carry[0:1, :]
        carry[...] = jnp.broadcast_to(f[tm - 1:tm, :], carry.shape)
        lane = _iota((tm, LANES), 1)
        for h in range(ATT_HEADS):
            base = HEAD_DIM * (1 - h % 2)
            fh = _col(f, h)
            hi = fh.astype(BF16).astype(F32)
            mid = (fh - hi).astype(BF16).astype(F32)
            lo = fh - hi - mid
            terms = jnp.where(lane == base, hi, jnp.where(lane == base + 1, mid, jnp.where(lane == base + 2, lo, 0.0)))
            terms_k = jnp.where(lane == base + 3, -hi,
                                jnp.where(lane == base + 4, -mid, jnp.where(lane == base + 5, -lo, 0.0)))
            ones_q = ((lane >= base + 3) & (lane < base + 6)).astype(F32)
            ones_k = ((lane >= base) & (lane < base + 3)).astype(F32)
            eq_ref[:, LANES * h:LANES * (h + 1)] = (terms + ones_q).astype(BF16)
            ek_ref[:, LANES * h:LANES * (h + 1)] = (terms_k + ones_k).astype(BF16)

    ospec = pl.BlockSpec((tm, ATT_HEADS * LANES), lambda i: (i, 0))
    o = jax.ShapeDtypeStruct((t, ATT_HEADS * LANES), BF16)
    return pl.pallas_call(
        body, name=name, grid=(t // tm,),
        in_specs=[pl.BlockSpec((tm, LANES), lambda i: (i, cs)), pl.BlockSpec((1, LANES), lambda i: (0, 0))],
        out_specs=[ospec, ospec], out_shape=[o, o],
        scratch_shapes=[pltpu.VMEM((SUBLANES, LANES), F32)],
        compiler_params=_params(1),
    )(hbuf, bf_vec)


def _fox_post(dfc, hbuf, bf_vec, *, name):
    t = hbuf.shape[0]
    tm = _pick(t, (FOX_PREP_TM, 128))
    nb = t // tm
    cs = COL_SMALL // LANES

    def body(df_ref, s_ref, b_ref, o_ref, db_ref, carry):
        i = pl.program_id(0)

        @pl.when(i == 0)
        def _():
            carry[...] = jnp.zeros_like(carry)
            db_ref[...] = jnp.zeros_like(db_ref)

        dlf = _cumsum_rows(df_ref[...], reverse=True) + carry[0:1, :]
        carry[...] = jnp.broadcast_to(dlf[0:1, :], carry.shape)
        dl = dlf * _sigmoid(-(s_ref[...] + b_ref[...]))
        dl = jnp.where(_iota(dl.shape, 1) < ATT_HEADS, dl, 0.0)
        o_ref[...] = dl
        db_ref[...] += jnp.sum(dl, axis=0, keepdims=True)

    vec = pl.BlockSpec((1, LANES), lambda i: (0, 0))
    return pl.pallas_call(
        body, name=name, grid=(nb,),
        in_specs=[pl.BlockSpec((tm, LANES), lambda i: (nb - 1 - i, 0)),
                  pl.BlockSpec((tm, LANES), lambda i: (nb - 1 - i, cs)), vec],
        out_specs=[pl.BlockSpec((tm, LANES), lambda i: (nb - 1 - i, 0)), vec],
        out_shape=[jax.ShapeDtypeStruct((t, LANES), F32), jax.ShapeDtypeStruct((1, LANES), F32)],
        scratch_shapes=[pltpu.VMEM((SUBLANES, LANES), F32)],
        compiler_params=_params(1),
    )(dfc, hbuf, bf_vec)


def _fox_masks(i, j, tq):
    row = i * tq + _iota((tq, tq), 0)
    col = j * tq + _iota((tq, tq), 1)
    lane = _iota((1, LANES), 1)
    return col <= row, (lane < HEAD_DIM, lane >= HEAD_DIM)


def _hosting(body, n_in, n_out, n_scratch, comm, grid):
    na, no = len(comm.arrays), len(comm.out_shapes)

    def hosted(*refs):
        o0 = n_in + na
        s0 = o0 + n_out + no
        cargs = (refs[n_in:o0], refs[o0 + n_out:s0]) + tuple(refs[s0 + n_scratch:])
        a, b = pl.program_id(0), pl.program_id(1)

        @pl.when((a == 0) & (b == 0))
        def _():
            comm.start(*cargs)

        @pl.when((a == grid[0] - 1) & (b == 0))
        def _():
            comm.middle(*cargs)

        body(*refs[:n_in], *refs[o0:o0 + n_out], *refs[s0:s0 + n_scratch])

        @pl.when((a == grid[0] - 1) & (b == grid[1] - 1))
        def _():
            comm.finish(*cargs)

    return hosted


def _hosted_call(body, comm, grid, *, name, in_specs, out_specs, out_shape, scratch_shapes, args):
    n_out = len(out_shape)
    if comm is not None:
        cin, cout, sems = comm.specs()
        body = _hosting(body, len(in_specs), n_out, len(scratch_shapes), comm, grid)
        in_specs, out_specs = in_specs + cin, out_specs + cout
        out_shape, scratch_shapes, args = out_shape + comm.out_shapes, scratch_shapes + sems, args + list(comm.arrays)
    outs = pl.pallas_call(body, name=name, grid=grid, in_specs=in_specs, out_specs=out_specs,
                          out_shape=out_shape, scratch_shapes=scratch_shapes, compiler_params=_params(2))(*args)
    return outs[:n_out], outs[n_out:]


def _merge_comms(comms):
    comms = [c for c in comms if c is not None]
    if len(comms) <= 1:
        return comms[0] if comms else None

    def both(which):
        def run(ins, outs, ssem, rsem):
            ia = io = 0
            for c in comms:
                na, no = len(c.arrays), len(c.out_shapes)
                getattr(c, which)(ins[ia:ia + na], outs[io:io + no], ssem, rsem)
                ia, io = ia + na, io + no
        return run

    spans = sorted((c.base, c.base + c.n_own) for c in comms)
    assert all(a[1] <= b[0] for a, b in zip(spans, spans[1:])), "semaphore ranges overlap"
    return _Comm(sum((list(c.arrays) for c in comms), []), sum((list(c.out_shapes) for c in comms), []),
                 spans[-1][1], both("start"), both("finish"), middle=both("middle"))


def _fox_fwd(hbuf, eq, ek, *, comm=None, name):
    t = hbuf.shape[0]
    w = ATT_WIDTH
    tq = _pick(t, (FOX_T, 256, 128))
    nq = t // tq
    cq, ck, cv = COL_Q // w, COL_K // w, COL_V // w

    def body(q_ref, k_ref, v_ref, eq_ref, ek_ref, o_ref, lse_ref, m_s, l_s, acc_s):
        i = pl.program_id(0)
        j = pl.program_id(1)

        @pl.when(j == 0)
        def _():
            m_s[...] = jnp.full_like(m_s, NEG)
            l_s[...] = jnp.zeros_like(l_s)
            acc_s[...] = jnp.zeros_like(acc_s)

        def step(diagonal):
            _, hms = _fox_masks(i, j, tq)
            keys_first = (j * tq + _iota((tq, tq), 0)) <= (i * tq + _iota((tq, tq), 1))
            half = _iota((LANES, 1), 0)
            hrows = (half < HEAD_DIM, half >= HEAD_DIM)
            m_all = m_s[...]
            l_all = l_s[...]
            acc_old = [acc_s[LANES * pr:LANES * (pr + 1), :] for pr in range(2)]
            m_out, l_out, acc_out = [], [], []
            for pr in range(2):
                sl = slice(LANES * pr, LANES * (pr + 1))
                qp = q_ref[:, sl]
                kp = k_ref[:, sl]
                vt = v_ref[:, sl].T.astype(BF16)
                acc = acc_old[pr]
                for hh in range(2):
                    h = 2 * pr + hh
                    hsl = slice(LANES * h, LANES * (h + 1))
                    qm = jnp.where(hms[hh], (qp * (HEAD_DIM ** -0.5)).astype(BF16), eq_ref[:, hsl])
                    km = jnp.where(hms[hh], kp.astype(BF16), ek_ref[:, hsl])
                    st = _dot(km, qm, 1, 1)
                    if diagonal:
                        st = jnp.where(keys_first, st, NEG)
                    m_old = m_all[h:h + 1, :]
                    m_new = jnp.maximum(m_old, jnp.max(st, axis=0, keepdims=True))
                    alpha = jnp.exp(m_old - m_new)
                    pt = jnp.exp(st - m_new)
                    l_out.append(alpha * l_all[h:h + 1, :] + jnp.sum(pt, axis=0, keepdims=True))
                    m_out.append(m_new)
                    pv = _dot(vt, pt.astype(BF16), 1, 0)
                    acc = jnp.where(hrows[hh], alpha * acc_old[pr] + pv, acc)
                acc_out.append(acc)
            for h in range(ATT_HEADS):
                m_s[h:h + 1, :] = m_out[h]
                l_s[h:h + 1, :] = l_out[h]
            for pr in range(2):
                acc_s[LANES * pr:LANES * (pr + 1), :] = acc_out[pr]

        @pl.when(j < i)
        def _():
            step(False)

        @pl.when(j == i)
        def _():
            step(True)
            half = _iota((LANES, 1), 0)
            l_all = l_s[...]
            for pr in range(2):
                acc = acc_s[LANES * pr:LANES * (pr + 1), :]
                o_t = jnp.where(half < HEAD_DIM, acc / l_all[2 * pr:2 * pr + 1, :], acc / l_all[2 * pr + 1:2 * pr + 2, :])
                o_ref[:, LANES * pr:LANES * (pr + 1)] = o_t.T
            lse = m_s[...] + jnp.log(l_s[...])
            lse_ref[...] = jnp.where(_iota(lse.shape, 0) < ATT_HEADS, lse, 0.0)

    return _hosted_call(
        body, comm, (nq, nq), name=name,
        in_specs=[pl.BlockSpec((tq, w), lambda i, j: (i, cq)),
                  pl.BlockSpec((tq, w), lambda i, j: (jnp.minimum(j, i), ck)),
                  pl.BlockSpec((tq, w), lambda i, j: (jnp.minimum(j, i), cv)),
                  pl.BlockSpec((tq, ATT_HEADS * LANES), lambda i, j: (i, 0)),
                  pl.BlockSpec((tq, ATT_HEADS * LANES), lambda i, j: (jnp.minimum(j, i), 0))],
        out_specs=[pl.BlockSpec((tq, w), lambda i, j: (i, 0)),
                   pl.BlockSpec((SUBLANES, tq), lambda i, j: (0, i))],
        out_shape=[jax.ShapeDtypeStruct((t, w), F32), jax.ShapeDtypeStruct((SUBLANES, t), F32)],
        scratch_shapes=[pltpu.VMEM((SUBLANES, tq), F32), pltpu.VMEM((SUBLANES, tq), F32),
                        pltpu.VMEM((w, tq), F32)],
        args=[hbuf, hbuf, hbuf, eq, ek])


def _fox_delta(dymix, o, *, name):
    t, w = o.shape
    tm = _pick(t, (512, 256, 128))
    cdo = ATT_WIDTH // w

    def body(do_ref, o_ref, d_ref):
        d_ref[...] = _head_reduce(do_ref[...] * o_ref[...], 0, ATT_HEADS)

    return pl.pallas_call(
        body, name=name, grid=(t // tm,),
        in_specs=[pl.BlockSpec((tm, w), lambda i: (i, cdo)), pl.BlockSpec((tm, w), lambda i: (i, 0))],
        out_specs=pl.BlockSpec((tm, LANES), lambda i: (i, 0)),
        out_shape=jax.ShapeDtypeStruct((t, LANES), F32),
        compiler_params=_params(1),
    )(dymix, o)


def _fox_bwd(hbuf, eq, ek, dymix, lse_rows, delta_rows, *, comm=None, name):
    t = hbuf.shape[0]
    w = ATT_WIDTH
    tq = _pick(t, (FOX_T, 256, 128))
    nq = t // tq
    cq, ck, cv = COL_Q // w, COL_K // w, COL_V // w
    cdo = ATT_WIDTH // w

    def body(q_ref, k_ref, v_ref, eq_ref, ek_ref, do_ref, lse_ref, dl_ref, dk_ref, dv_ref, dfk_ref, dqt_ref, dfq_ref,
             dk_s, dv_s, dfk_s):
        j = pl.program_id(0)
        i = pl.program_id(1)

        @pl.when((i == 0) & (j == 0))
        def _():
            dqt_ref[...] = jnp.zeros_like(dqt_ref)
            dfq_ref[...] = jnp.zeros_like(dfq_ref)

        @pl.when(i == 0)
        def _():
            dk_s[...] = jnp.zeros_like(dk_s)
            dv_s[...] = jnp.zeros_like(dv_s)
            dfk_s[...] = jnp.zeros_like(dfk_s)

        def step(diagonal):
            _, hms = _fox_masks(i, j, tq)
            keys_first = (j * tq + _iota((tq, tq), 0)) <= (i * tq + _iota((tq, tq), 1))
            half = _iota((LANES, 1), 0)
            hrows = (half < HEAD_DIM, half >= HEAD_DIM)
            lse_all = lse_ref[...]
            dl_all = dl_ref[...]
            dvs, dks, dfks, dqts, dfqs = [], [], [], [], []
            for pr in range(2):
                sl = slice(LANES * pr, LANES * (pr + 1))
                qp = q_ref[:, sl]
                kp = k_ref[:, sl]
                kt = kp.T.astype(BF16)
                vpb = v_ref[:, sl].astype(BF16)
                dop = do_ref[:, sl]
                dv_p = jnp.zeros((tq, LANES), F32)
                dk_p = jnp.zeros((tq, LANES), F32)
                dqt_p = jnp.zeros((LANES, tq), F32)
                for hh in range(2):
                    h = 2 * pr + hh
                    hsl = slice(LANES * h, LANES * (h + 1))
                    qm = jnp.where(hms[hh], (qp * (HEAD_DIM ** -0.5)).astype(BF16), eq_ref[:, hsl])
                    km = jnp.where(hms[hh], kp.astype(BF16), ek_ref[:, hsl])
                    st = _dot(km, qm, 1, 1)
                    if diagonal:
                        st = jnp.where(keys_first, st, NEG)
                    pt = jnp.exp(st - lse_all[h:h + 1, :])
                    domb = jnp.where(hms[hh], dop, 0.0).astype(BF16)
                    dv_p = dv_p + _dot(pt.astype(BF16), domb, 1, 0)
                    dpt = _dot(vpb, domb, 1, 1)
                    dst = pt * (dpt - dl_all[h:h + 1, :])
                    dstb = dst.astype(BF16)
                    dk_p = dk_p + jnp.where(hms[hh], _dot(dstb, qm, 1, 0), 0.0)
                    dqt_p = dqt_p + _dot(jnp.where(hrows[hh], kt, 0.0), dstb, 1, 0)
                    part = dst[:, 0:LANES]
                    for c in range(1, tq // LANES):
                        part = part + dst[:, LANES * c:LANES * (c + 1)]
                    dfks.append(part)
                    dfqs.append(jnp.sum(dst, axis=0, keepdims=True))
                dvs.append(dv_p)
                dks.append(dk_p)
                dqts.append(dqt_p)
            dv_s[...] += jnp.concatenate(dvs, axis=1)
            dk_s[...] += jnp.concatenate(dks, axis=1)
            for h in range(ATT_HEADS):
                dfk_s[h] += dfks[h]
            cols = pl.ds(pl.multiple_of(i * tq, tq), tq)
            dqt_ref[:, cols] += jnp.concatenate(dqts, axis=0) * (HEAD_DIM ** -0.5)
            dfq_ref[:, cols] += jnp.concatenate(dfqs + [jnp.zeros((SUBLANES - ATT_HEADS, tq), F32)], axis=0)

        @pl.when(i > j)
        def _():
            step(False)

        @pl.when(i == j)
        def _():
            step(True)

        @pl.when(i == nq - 1)
        def _():
            dk_ref[...] = dk_s[...].astype(BF16)
            dv_ref[...] = dv_s[...].astype(BF16)
            lane = _iota((tq, LANES), 1)
            out = jnp.zeros((tq, LANES), F32)
            for h in range(ATT_HEADS):
                out = jnp.where(lane == h, jnp.sum(dfk_s[h], axis=1, keepdims=True), out)
            dfk_ref[...] = out

    qi = lambda j, i: jnp.maximum(i, j)
    rows = pl.BlockSpec((SUBLANES, tq), lambda j, i: (0, qi(j, i)))
    return _hosted_call(
        body, comm, (nq, nq), name=name,
        in_specs=[pl.BlockSpec((tq, w), lambda j, i: (qi(j, i), cq)),
                  pl.BlockSpec((tq, w), lambda j, i: (j, ck)),
                  pl.BlockSpec((tq, w), lambda j, i: (j, cv)),
                  pl.BlockSpec((tq, ATT_HEADS * LANES), lambda j, i: (qi(j, i), 0)),
                  pl.BlockSpec((tq, ATT_HEADS * LANES), lambda j, i: (j, 0)),
                  pl.BlockSpec((tq, w), lambda j, i: (qi(j, i), cdo)),
                  rows, rows],
        out_specs=[pl.BlockSpec((tq, w), lambda j, i: (j, 0)), pl.BlockSpec((tq, w), lambda j, i: (j, 0)),
                   pl.BlockSpec((tq, LANES), lambda j, i: (j, 0)),
                   pl.BlockSpec((w, t), lambda j, i: (0, 0)), pl.BlockSpec((SUBLANES, t), lambda j, i: (0, 0))],
        out_shape=[jax.ShapeDtypeStruct((t, w), BF16), jax.ShapeDtypeStruct((t, w), BF16),
                   jax.ShapeDtypeStruct((t, LANES), F32),
                   jax.ShapeDtypeStruct((w, t), F32), jax.ShapeDtypeStruct((SUBLANES, t), F32)],
        scratch_shapes=[pltpu.VMEM((tq, w), F32), pltpu.VMEM((tq, w), F32),
                        pltpu.VMEM((ATT_HEADS, tq, LANES), F32)],
        args=[hbuf, hbuf, hbuf, eq, ek, dymix, lse_rows, delta_rows])


GROUP_W = SSD_WIDTH // SSD_GROUPS
HEADS_PER_GROUP = SSD_HEADS // SSD_GROUPS


def _ssd_chunk_common(xr, prev8, sm, cw, cb, dtb, avec):
    c = _conv_taps(xr, prev8, cw, cb)
    sig = _sigmoid(c)
    xa = c * sig
    dt = _softplus(sm + dtb)
    a = dt * avec
    acum = _cumsum_rows(a)
    return c, sig, xa, dt, acum


def _ssd_head_cols(acum, acum_t):
    cols = [_col(acum, LANE_DT + h) for h in range(SSD_HEADS)]
    rows = [_row(acum_t, LANE_DT + h) for h in range(SSD_HEADS)]
    return cols, rows


def _expand_heads(vals, width):
    rows = vals[0].shape[0]
    colhead = _iota((rows, width), 1) // HEAD_DIM
    out = jnp.broadcast_to(vals[0], (rows, width))
    for h in range(1, len(vals)):
        out = jnp.where(colhead == h, vals[h], out)
    return out


def _ssd_decays(cols, g):
    mine = cols[HEADS_PER_GROUP * g:HEADS_PER_GROUP * (g + 1)]
    n = mine[0].shape[0]
    atots = [c[n - 1:n, :] for c in mine]
    e = _expand_heads([jnp.exp(c) for c in mine], GROUP_W)
    dec = _expand_heads([jnp.exp(t - c) for c, t in zip(mine, atots)], GROUP_W)
    etot = _expand_heads([jnp.exp(t) for t in atots], GROUP_W)
    return e, dec, etot


def _ssd_ldec(cols, rows, h, tril):
    return jnp.exp(jnp.where(tril, cols[h] - rows[h], NEG))


def _ssd_fwd(hbuf, conv_w, conv_b, dtb_vec, a_vec, d_exp, norm_g, *, name):
    t = hbuf.shape[0]
    L = SSD_CHUNK
    nc = t // L
    hb = L // SUBLANES
    cs = COL_SMALL // LANES
    cz = COL_Z // SSD_WIDTH

    def body(x_ref, xp_ref, z_ref, s_ref, cw_ref, cb_ref, dtb_ref, av_ref, dx_ref, ng_ref,
             yc_ref, y_ref, st_ref, state):
        i = pl.program_id(0)

        @pl.when(i == 0)
        def _():
            state[...] = jnp.zeros_like(state)

        prev = jnp.where(i == 0, 0.0, xp_ref[...])
        _, _, xa, dt, acum = _ssd_chunk_common(x_ref[...], prev, s_ref[...], cw_ref[...], cb_ref[...],
                                               dtb_ref[...], av_ref[...])
        cols, rows = _ssd_head_cols(acum, acum.T)
        xs = xa[:, :SSD_WIDTH]
        xdt = xs * _head_expand(dt, LANE_DT, SSD_HEADS, SSD_WIDTH)
        tril = _iota((L, L), 0) >= _iota((L, L), 1)
        lane = _iota((1, LANES), 1)
        ys = []
        for g in range(SSD_GROUPS):
            bg = xa[:, SSD_WIDTH + SSD_STATE * g:SSD_WIDTH + SSD_STATE * (g + 1)].astype(BF16)
            cg = xa[:, SSD_WIDTH + SSD_STATE * (SSD_GROUPS + g):SSD_WIDTH + SSD_STATE * (SSD_GROUPS + g + 1)].astype(BF16)
            gm = _dot(cg, bg, 1, 1)
            e, dec, etot = _ssd_decays(cols, g)
            s_in = state[g]
            st_ref[0, g] = s_in
            xg = xdt[:, GROUP_W * g:GROUP_W * (g + 1)]
            y_off = e * _dot(cg, s_in.astype(BF16), 1, 0)
            state[g] = etot * s_in + _dot(bg, (dec * xg).astype(BF16), 0, 0)
            for pr in range(2):
                xp = xg[:, LANES * pr:LANES * (pr + 1)].astype(BF16)
                outs = []
                for hh in range(2):
                    h = HEADS_PER_GROUP * g + 2 * pr + hh
                    m = gm * _ssd_ldec(cols, rows, h, tril)
                    outs.append(_dot(m.astype(BF16), xp, 1, 0))
                ys.append(jnp.where(lane < HEAD_DIM, outs[0], outs[1]) + y_off[:, LANES * pr:LANES * (pr + 1)])
        y = jnp.concatenate(ys, axis=1)
        y_ref[...] = y
        yd = y + dx_ref[...] * xs
        zz = z_ref[...]
        y2 = yd * zz * _sigmoid(zz)
        ng = ng_ref[...]
        outs = []
        for g in range(SSD_GROUPS):
            yg = y2[:, GROUP_W * g:GROUP_W * (g + 1)]
            rs = lax.rsqrt(jnp.mean(yg * yg, axis=1, keepdims=True) + RMS_EPS)
            outs.append(yg * rs * ng[:, GROUP_W * g:GROUP_W * (g + 1)])
        yc_ref[...] = jnp.concatenate(outs, axis=1)

    cdim = SSD_CONV_DIM
    vecc = pl.BlockSpec((1, cdim), lambda i: (0, 0))
    vecl = pl.BlockSpec((1, LANES), lambda i: (0, 0))
    vecw = pl.BlockSpec((1, SSD_WIDTH), lambda i: (0, 0))
    roww = pl.BlockSpec((L, SSD_WIDTH), lambda i: (i, 0))
    return pl.pallas_call(
        body, name=name, grid=(nc,),
        in_specs=[pl.BlockSpec((L, cdim), lambda i: (i, 0)),
                  pl.BlockSpec((SUBLANES, cdim), lambda i: (jnp.maximum(i * hb - 1, 0), 0)),
                  pl.BlockSpec((L, SSD_WIDTH), lambda i: (i, cz)),
                  pl.BlockSpec((L, LANES), lambda i: (i, cs)),
                  pl.BlockSpec((CONV_K, cdim), lambda i: (0, 0)), vecc, vecl, vecl, vecw, vecw],
        out_specs=[roww, roww, pl.BlockSpec((1, SSD_GROUPS, SSD_STATE, GROUP_W), lambda i: (i, 0, 0, 0))],
        out_shape=[jax.ShapeDtypeStruct((t, SSD_WIDTH), F32), jax.ShapeDtypeStruct((t, SSD_WIDTH), F32),
                   jax.ShapeDtypeStruct((nc, SSD_GROUPS, SSD_STATE, GROUP_W), F32)],
        scratch_shapes=[pltpu.VMEM((SSD_GROUPS, SSD_STATE, GROUP_W), F32)],
        compiler_params=_params(1),
    )(hbuf, hbuf, hbuf, hbuf, conv_w, conv_b, dtb_vec, a_vec, d_exp, norm_g)


def _ssd_bwd(dymix, hbuf, y_ssd, states, conv_w, conv_b, dtb_vec, a_vec, d_exp, norm_g, *, name):
    t = hbuf.shape[0]
    L = SSD_CHUNK
    nc = t // L
    hb = L // SUBLANES
    cs = COL_SMALL // LANES
    cz = COL_Z // SSD_WIDTH
    cdy = (LRU_WIDTH + ATT_WIDTH) // SSD_WIDTH
    cdim = SSD_CONV_DIM

    def body(dyc_ref, x_ref, xp_ref, z_ref, s_ref, y_ref, st_ref, cw_ref, cb_ref, dtb_ref, av_ref, dx_ref, ng_ref,
             dxr_ref, dz_ref, dsm_ref, dng_ref, dd_ref, da_ref, ddtb_ref, dcw_ref, dcb_ref,
             dstate, dnext):
        i = pl.program_id(0)
        ic = nc - 1 - i

        @pl.when(i == 0)
        def _():
            dstate[...] = jnp.zeros_like(dstate)
            dnext[...] = jnp.zeros_like(dnext)
            for ref in (dng_ref, dd_ref, da_ref, ddtb_ref, dcw_ref, dcb_ref):
                ref[...] = jnp.zeros_like(ref)

        xr = x_ref[...]
        sm = s_ref[...]
        prev = jnp.where(ic == 0, 0.0, xp_ref[...])
        avec = av_ref[...]
        c, sig, xa, dt, acum = _ssd_chunk_common(xr, prev, sm, cw_ref[...], cb_ref[...], dtb_ref[...], avec)
        cols, rows = _ssd_head_cols(acum, acum.T)
        xs = xa[:, :SSD_WIDTH]
        dtx = _head_expand(dt, LANE_DT, SSD_HEADS, SSD_WIDTH)
        xdt = xs * dtx
        tril = _iota((L, L), 0) >= _iota((L, L), 1)
        lane = _iota((1, LANES), 1)
        hmasks = (lane < HEAD_DIM, lane >= HEAD_DIM)

        y = y_ref[...]
        dexp = dx_ref[...]
        yd = y + dexp * xs
        zz = z_ref[...]
        sz = _sigmoid(zz)
        siluz = zz * sz
        y2 = yd * siluz
        ng = ng_ref[...]
        dyc = dyc_ref[...]
        dy2s, dngs = [], []
        for g in range(SSD_GROUPS):
            sl = slice(GROUP_W * g, GROUP_W * (g + 1))
            yg = y2[:, sl]
            rs = lax.rsqrt(jnp.mean(yg * yg, axis=1, keepdims=True) + RMS_EPS)
            wv = dyc[:, sl] * ng[:, sl]
            dngs.append(jnp.sum(dyc[:, sl] * yg * rs, axis=0, keepdims=True))
            dy2s.append(rs * wv - yg * (rs * rs * rs) * jnp.mean(wv * yg, axis=1, keepdims=True))
        dy2 = jnp.concatenate(dy2s, axis=1)
        dng_ref[...] += jnp.concatenate(dngs, axis=1)
        dz_ref[...] = (dy2 * yd * (sz * (1.0 + zz * (1.0 - sz)))).astype(BF16)
        dy = dy2 * siluz
        dd_ref[...] += jnp.sum(dy * xs, axis=0, keepdims=True)

        dxs, dbs, dcs = [], [], []
        datot = jnp.zeros((1, LANES), F32)
        lanes = _iota((L, LANES), 1)
        dacum = jnp.zeros((L, LANES), F32)
        for g in range(SSD_GROUPS):
            sl = slice(GROUP_W * g, GROUP_W * (g + 1))
            bg = xa[:, SSD_WIDTH + SSD_STATE * g:SSD_WIDTH + SSD_STATE * (g + 1)].astype(BF16)
            cg = xa[:, SSD_WIDTH + SSD_STATE * (SSD_GROUPS + g):SSD_WIDTH + SSD_STATE * (SSD_GROUPS + g + 1)].astype(BF16)
            gm = _dot(cg, bg, 1, 1)
            e, dec, etot = _ssd_decays(cols, g)
            s_in = st_ref[0, g]
            ds_out = dstate[g]
            dyg = dy[:, sl]
            xg = xdt[:, sl]
            edy = (e * dyg).astype(BF16)
            dstate[g] = etot * ds_out + _dot(cg, edy, 0, 0)
            dx_state = dec * _dot(bg, ds_out.astype(BF16), 1, 0)
            y_off = e * _dot(cg, s_in.astype(BF16), 1, 0)
            dacum = dacum + _head_reduce_group(dyg * y_off - xg * dx_state, g)
            dc_off = _dot(edy, s_in.astype(BF16), 1, 1)
            db_state = _dot((dec * xg).astype(BF16), ds_out.astype(BF16), 1, 1)
            dgsum = jnp.zeros((L, L), F32)
            dx_pairs = []
            for pr in range(2):
                psl = slice(LANES * pr, LANES * (pr + 1))
                xp = xg[:, psl]
                dyp = dyg[:, psl]
                dx_pair = jnp.zeros((L, LANES), F32)
                for hh in range(2):
                    h = HEADS_PER_GROUP * g + 2 * pr + hh
                    ldec = _ssd_ldec(cols, rows, h, tril)
                    dym = jnp.where(hmasks[hh], dyp, 0.0).astype(BF16)
                    xm = jnp.where(hmasks[hh], xp, 0.0).astype(BF16)
                    dx_pair = dx_pair + _dot((gm * ldec).astype(BF16), dym, 0, 0)
                    dml = _dot(dym, xm, 1, 1) * ldec
                    dgsum = dgsum + dml
                    qm = dml * gm
                    seg = jnp.sum(qm, axis=1, keepdims=True) - jnp.sum(qm.T, axis=1, keepdims=True)
                    dacum = dacum + jnp.where(lanes == LANE_DT + h, seg, 0.0)
                dx_pairs.append(dx_pair)
            dgb = dgsum.astype(BF16)
            dcs.append(_dot(dgb, bg, 1, 0) + dc_off)
            dbs.append(_dot(dgb, cg, 0, 0) + db_state)
            dxg = jnp.concatenate(dx_pairs, axis=1) + dx_state
            dxs.append(dxg)
            v = jnp.sum(dx_state * xg, axis=0, keepdims=True) + etot * jnp.sum(ds_out * s_in, axis=0, keepdims=True)
            datot = datot + _head_reduce_row(v, LANE_DT + HEADS_PER_GROUP * g, HEADS_PER_GROUP)
        dx = jnp.concatenate(dxs, axis=1)
        dacum = dacum + jnp.where(_iota((L, LANES), 0) == L - 1, datot, 0.0)
        da = _cumsum_rows(dacum, reverse=True)
        ddt = da * avec + _head_reduce(dx * xs, LANE_DT, SSD_HEADS)
        da_ref[...] += jnp.sum(da * dt, axis=0, keepdims=True)
        ddt_raw = ddt * _sigmoid(sm + dtb_ref[...])
        ddt_raw = jnp.where((lanes >= LANE_DT) & (lanes < LANE_DT + SSD_HEADS), ddt_raw, 0.0)
        dsm_ref[...] = ddt_raw
        ddtb_ref[...] += jnp.sum(ddt_raw, axis=0, keepdims=True)
        dxs_total = dx * dtx + dexp * dy
        dxa = jnp.concatenate([dxs_total] + dbs + dcs, axis=1)
        dc = dxa * (sig * (1.0 + c * (1.0 - sig)))
        dxr, dws = _conv_taps_bwd(dc, dnext[...], cw_ref[...], xr)
        dxr_ref[...] = dxr.astype(BF16)
        dcw_ref[...] += dws
        dcb_ref[...] += jnp.sum(dc, axis=0, keepdims=True)
        dnext[...] = dc[:SUBLANES]

    rev = lambda i: nc - 1 - i
    vecc = pl.BlockSpec((1, cdim), lambda i: (0, 0))
    vecl = pl.BlockSpec((1, LANES), lambda i: (0, 0))
    vecw = pl.BlockSpec((1, SSD_WIDTH), lambda i: (0, 0))
    cwspec = pl.BlockSpec((CONV_K, cdim), lambda i: (0, 0))
    roww = pl.BlockSpec((L, SSD_WIDTH), lambda i: (rev(i), 0))
    return pl.pallas_call(
        body, name=name, grid=(nc,),
        in_specs=[pl.BlockSpec((L, SSD_WIDTH), lambda i: (rev(i), cdy)),
                  pl.BlockSpec((L, cdim), lambda i: (rev(i), 0)),
                  pl.BlockSpec((SUBLANES, cdim), lambda i: (jnp.maximum(rev(i) * hb - 1, 0), 0)),
                  pl.BlockSpec((L, SSD_WIDTH), lambda i: (rev(i), cz)),
                  pl.BlockSpec((L, LANES), lambda i: (rev(i), cs)),
                  roww,
                  pl.BlockSpec((1, SSD_GROUPS, SSD_STATE, GROUP_W), lambda i: (rev(i), 0, 0, 0)),
                  cwspec, vecc, vecl, vecl, vecw, vecw],
        out_specs=[pl.BlockSpec((L, cdim), lambda i: (rev(i), 0)), roww,
                   pl.BlockSpec((L, LANES), lambda i: (rev(i), 0)),
                   vecw, vecw, vecl, vecl, cwspec, vecc],
        out_shape=[jax.ShapeDtypeStruct((t, cdim), BF16), jax.ShapeDtypeStruct((t, SSD_WIDTH), BF16),
                   jax.ShapeDtypeStruct((t, LANES), F32),
                   jax.ShapeDtypeStruct((1, SSD_WIDTH), F32), jax.ShapeDtypeStruct((1, SSD_WIDTH), F32),
                   jax.ShapeDtypeStruct((1, LANES), F32), jax.ShapeDtypeStruct((1, LANES), F32),
                   jax.ShapeDtypeStruct((CONV_K, cdim), F32), jax.ShapeDtypeStruct((1, cdim), F32)],
        scratch_shapes=[pltpu.VMEM((SSD_GROUPS, SSD_STATE, GROUP_W), F32), pltpu.VMEM((SUBLANES, cdim), F32)],
        compiler_params=_params(1),
    )(dymix, hbuf, hbuf, hbuf, hbuf, y_ssd, states, conv_w, conv_b, dtb_vec, a_vec, d_exp, norm_g)


def _head_reduce_group(x, g):
    return _head_reduce(x, LANE_DT + HEADS_PER_GROUP * g, HEADS_PER_GROUP)


def _head_reduce_row(v, lane0, nheads):
    colhead = _iota(v.shape, 1) // HEAD_DIM
    lane = _iota((1, LANES), 1)
    out = jnp.zeros((1, LANES), F32)
    for h in range(nheads):
        s = jnp.sum(jnp.where(colhead == h, v, 0.0), axis=1, keepdims=True)
        out = jnp.where(lane == lane0 + h, s, out)
    return out


def _exchange(inps, axes, *, swap=False, name):
    n = 2 ** len(axes)
    assert not swap or n == 2
    counts = [a.shape[0] for a in inps]
    out_shapes = [jax.ShapeDtypeStruct(a.shape if swap else (n,) + a.shape, a.dtype) for a in inps]
    units = sum(counts)
    na = len(inps)

    def body(*refs):
        in_refs, out_refs = refs[:na], refs[na:2 * na]
        send_sems, recv_sems, local_sems = refs[2 * na:]
        pos = {ax: lax.axis_index(ax) for ax in MESH_AXES}

        def slot_of(coord):
            s = 0
            for ax in axes:
                s = s * 2 + coord[ax]
            return s

        me = slot_of(pos)
        copies = []
        unit = 0
        for a in range(na):
            for it in range(counts[a]):
                dst = out_refs[a].at[it] if swap else out_refs[a].at[me, it]
                if not swap:
                    cp = pltpu.make_async_copy(in_refs[a].at[it], dst, local_sems.at[unit])
                    cp.start()
                    copies.append(cp)
                for delta in range(1, n):
                    coord = dict(pos)
                    for b, ax in enumerate(reversed(axes)):
                        if (delta >> b) & 1:
                            coord[ax] = 1 - pos[ax]
                    k = unit * (n - 1) + delta - 1
                    cp = pltpu.make_async_remote_copy(
                        src_ref=in_refs[a].at[it], dst_ref=dst,
                        send_sem=send_sems.at[k], recv_sem=recv_sems.at[k],
                        device_id=(coord["x"], coord["y"], coord["c"]), device_id_type=pl.DeviceIdType.MESH)
                    cp.start()
                    copies.append(cp)
                unit += 1
        for cp in copies:
            cp.wait()

    any_spec = pl.BlockSpec(memory_space=pl.ANY)
    return pl.pallas_call(
        body, name=name,
        in_specs=[any_spec] * na, out_specs=[any_spec] * na, out_shape=out_shapes,
        scratch_shapes=[pltpu.SemaphoreType.DMA((units * (n - 1),)), pltpu.SemaphoreType.DMA((units * (n - 1),)),
                        pltpu.SemaphoreType.DMA((units,))],
    )(*inps)


class _Comm:
    def __init__(self, arrays, out_shapes, n_own, start, finish, base=0, middle=None):
        self.arrays, self.out_shapes, self.start, self.finish = arrays, out_shapes, start, finish
        self.middle = middle or (lambda *refs: None)
        self.base, self.n_own, self.n_sems = base, n_own, base + n_own

    def specs(self):
        any_spec = pl.BlockSpec(memory_space=pl.ANY)
        sems = [pltpu.SemaphoreType.DMA((self.n_sems,)), pltpu.SemaphoreType.DMA((self.n_sems,))]
        return [any_spec] * len(self.arrays), [any_spec] * len(self.out_shapes), sems


def _run_comm(comm, *, name):
    na, no = len(comm.arrays), len(comm.out_shapes)

    def body(*refs):
        args = (refs[:na], refs[na:na + no]) + tuple(refs[na + no:])
        comm.start(*args)
        comm.middle(*args)
        comm.finish(*args)

    in_specs, out_specs, sems = comm.specs()
    return pl.pallas_call(body, name=name, in_specs=in_specs, out_specs=out_specs, out_shape=comm.out_shapes,
                          scratch_shapes=sems)(*comm.arrays)


def _chip_peer(x, y, d):
    px = 1 - x if d & 2 else x
    py = 1 - y if d & 1 else y
    return px, py, 2 * px + py


def _gather_layer_comm(srcs, li, base=0):
    counts = [s.shape[0] for s in srcs]
    units = [(a, it) for a in range(len(srcs)) for it in range(counts[a])]
    n_ici = 3 * len(units)
    out_shapes = [jax.ShapeDtypeStruct((N_CHIPS,) + s.shape, s.dtype) for s in srcs]

    def ici(ins, outs, ssem, rsem, u, d):
        x, y, c = (lax.axis_index(ax) for ax in MESH_AXES)
        a, it = units[u]
        px, py, _ = _chip_peer(x, y, d)
        k = base + 3 * u + d - 1
        return pltpu.make_async_remote_copy(
            src_ref=ins[a].at[it], dst_ref=outs[a].at[2 * x + y, it], send_sem=ssem.at[k], recv_sem=rsem.at[k],
            device_id=(px, py, c), device_id_type=pl.DeviceIdType.MESH)

    def arrived(ins, outs, ssem, rsem, u, d):
        x, y, c = (lax.axis_index(ax) for ax in MESH_AXES)
        a, it = units[u]
        _, _, pk = _chip_peer(x, y, d)
        k = base + 3 * u + d - 1
        return pltpu.make_async_remote_copy(
            src_ref=ins[a].at[it], dst_ref=outs[a].at[pk, it], send_sem=ssem.at[k], recv_sem=rsem.at[k],
            device_id=(x, y, c), device_id_type=pl.DeviceIdType.MESH)

    def forward(ins, outs, ssem, rsem, u, slot):
        x, y, c = (lax.axis_index(ax) for ax in MESH_AXES)
        a, it = units[u]
        pk = 2 * x + y if slot == 0 else _chip_peer(x, y, slot)[2]
        src = ins[a].at[it] if slot == 0 else outs[a].at[pk, it]
        k = base + n_ici + 4 * u + slot
        return pltpu.make_async_remote_copy(
            src_ref=src, dst_ref=outs[a].at[pk, it], send_sem=ssem.at[k], recv_sem=rsem.at[k],
            device_id=(x, y, 1 - c), device_id_type=pl.DeviceIdType.MESH)

    def start(ins, outs, ssem, rsem):
        for u in range(len(units)):
            forward(ins, outs, ssem, rsem, u, 0).start()

        @pl.when(lax.axis_index("c") == li)
        def _():
            for u in range(len(units)):
                for d in range(1, N_CHIPS):
                    ici(ins, outs, ssem, rsem, u, d).start()

    def middle(ins, outs, ssem, rsem):
        @pl.when(lax.axis_index("c") == li)
        def _():
            for u in range(len(units)):
                for d in range(1, N_CHIPS):
                    arrived(ins, outs, ssem, rsem, u, d).wait_recv()
                    forward(ins, outs, ssem, rsem, u, d).start()

    def finish(ins, outs, ssem, rsem):
        c = lax.axis_index("c")

        @pl.when(c == li)
        def _():
            for u in range(len(units)):
                for d in range(1, N_CHIPS):
                    ici(ins, outs, ssem, rsem, u, d).wait_send()
                    forward(ins, outs, ssem, rsem, u, d).wait_send()

        @pl.when(c != li)
        def _():
            for u in range(len(units)):
                for d in range(1, N_CHIPS):
                    forward(ins, outs, ssem, rsem, u, d).wait_recv()

        for u in range(len(units)):
            forward(ins, outs, ssem, rsem, u, 0).wait()

    return _Comm(srcs, out_shapes, n_ici + 4 * len(units), start, finish, base, middle)


def _reduce_chips_comm(sums, li, base=0):
    counts = [s.shape[0] for s in sums]
    units = [(a, it) for a in range(len(sums)) for it in range(counts[a])]
    out_shapes = [jax.ShapeDtypeStruct((N_CHIPS, s.shape[0]) + s.shape[2:], s.dtype) for s in sums]

    def copy(ins, outs, ssem, rsem, u, d):
        x, y, c = (lax.axis_index(ax) for ax in MESH_AXES)
        a, it = units[u]
        px, py, pk = _chip_peer(x, y, d)
        k = base + 3 * u + d - 1
        return pltpu.make_async_remote_copy(
            src_ref=ins[a].at[it, pk], dst_ref=outs[a].at[2 * x + y, it], send_sem=ssem.at[k], recv_sem=rsem.at[k],
            device_id=(px, py, c), device_id_type=pl.DeviceIdType.MESH)

    def start(ins, outs, ssem, rsem):
        @pl.when(lax.axis_index("c") == li)
        def _():
            for u in range(len(units)):
                for d in range(1, N_CHIPS):
                    copy(ins, outs, ssem, rsem, u, d).start()

    def finish(ins, outs, ssem, rsem):
        @pl.when(lax.axis_index("c") == li)
        def _():
            for u in range(len(units)):
                for d in range(1, N_CHIPS):
                    copy(ins, outs, ssem, rsem, u, d).wait()

    return _Comm(sums, out_shapes, 3 * len(units), start, finish, base)


def _sum_slots(buf, out_dtype, *, name):
    n, rows, cols = buf.shape
    tm = _pick(rows, (512, 256, 128, 8))
    if rows % tm:
        tm = rows

    def body(b_ref, o_ref):
        acc = b_ref[0].astype(F32)
        for s in range(1, n):
            acc = acc + b_ref[s].astype(F32)
        o_ref[...] = acc.astype(out_dtype)

    return pl.pallas_call(
        body, name=name, grid=(pl.cdiv(rows, tm),),
        in_specs=[pl.BlockSpec((n, tm, cols), lambda i: (0, i, 0))],
        out_specs=pl.BlockSpec((tm, cols), lambda i: (i, 0)),
        out_shape=jax.ShapeDtypeStruct((rows, cols), out_dtype),
        compiler_params=_params(1),
    )(buf)


def _sum_pair(a, b, out_dtype, *, name):
    shape = a.shape
    cols = shape[-1]
    a2, b2 = a.reshape(-1, cols), b.reshape(-1, cols)
    rows = a2.shape[0]
    tm = _pick(rows, (512, 256, 128, 8))

    def body(a_ref, b_ref, o_ref):
        o_ref[...] = (a_ref[...].astype(F32) + b_ref[...].astype(F32)).astype(out_dtype)

    spec = pl.BlockSpec((tm, cols), lambda i: (i, 0))
    return pl.pallas_call(
        body, name=name, grid=(rows // tm,), in_specs=[spec, spec], out_specs=spec,
        out_shape=jax.ShapeDtypeStruct((rows, cols), out_dtype), compiler_params=_params(1),
    )(a2, b2).reshape(shape)


def _adamw(w, g, m, v, *, name):
    shape = w.shape
    cols = shape[-1]
    rows = w.size // cols
    w2, g2, m2, v2 = (a.reshape(rows, cols) for a in (w, g, m, v))
    tm = _pick(rows, (256, 128, 64, 32, 16, 8))
    if rows % tm:
        tm = rows
    bc1 = 1.0 - ADAM_B1 ** ADAM_STEP
    bc2 = 1.0 - ADAM_B2 ** ADAM_STEP

    def body(w_ref, g_ref, m_ref, v_ref, d_ref, nm_ref, nv_ref):
        gg = g_ref[...]
        mm = ADAM_B1 * m_ref[...] + (1.0 - ADAM_B1) * gg
        vv = ADAM_B2 * v_ref[...] + (1.0 - ADAM_B2) * (gg * gg)
        m_hat = mm / bc1
        v_hat = vv / bc2
        d_ref[...] = -ADAM_LR * (m_hat / (jnp.sqrt(v_hat) + ADAM_EPS) + ADAM_WD * w_ref[...])
        nm_ref[...] = mm
        nv_ref[...] = vv

    spec = pl.BlockSpec((tm, cols), lambda i: (i, 0))
    o = jax.ShapeDtypeStruct((rows, cols), F32)
    outs = pl.pallas_call(
        body, name=name, grid=(rows // tm,), in_specs=[spec] * 4, out_specs=[spec] * 3, out_shape=[o] * 3,
        compiler_params=_params(1),
    )(w2, g2, m2, v2)
    return tuple(a.reshape(shape) for a in outs)


def _layer_fwd(li, x, xb, pb, W, up=None, att=None):
    nm = lambda s: f"l{li}_{s}"
    sv = {"x_in_b": xb}
    (g1, u1, a1), got = _mm_swiglu(xb, W["ffn1_wg"], W["ffn1_wu"], comm=up[0] if up else None, name=nm("ffn1_up"))
    if up:
        W = {**W, **up[1](got)}
    x1, x1b, xh1, rs1 = _mm_ln(a1, W["ffn1_wd"], x, W["ln1_g"], W["ln1_b"], rscale=ALPHA, mscale=0.5, name=nm("ffn1_down_ln"))
    hbuf = _mm(x1b, W["w_in_p"], name=nm("in_proj"))
    ya, lu, lr, lig, la, lh = _lru_fwd(hbuf, W["lru_conv_w"], W["lru_conv_b"], W["lru_wa_bd"], W["lru_ba"],
                                       W["lru_wx_bd"], W["lru_bx"], W["lru_lambda"], name=nm("lru_fwd"))
    eq, ek = _fox_prep(hbuf, W["fox_bf_vec"], name=nm("fox_prep"))
    (yb, lse_rows), got = _fox_fwd(hbuf, eq, ek, comm=att[0] if att else None, name=nm("fox_fwd"))
    if att:
        W = {**W, **att[1](got)}
    yc, yssd, states = _ssd_fwd(hbuf, W["ssd_conv_w"], W["ssd_conv_b"], W["ssd_dtb_vec"], W["ssd_a_vec"],
                                W["ssd_d_exp"], W["ssd_norm_g"], name=nm("ssd_fwd"))
    ymix = jnp.concatenate([ya, yb, yc], axis=1).astype(BF16)
    x2, x2b, xh2, rs2 = _mm_ln(ymix, W["w_out"], x1, W["ln2_g"], W["ln2_b"], rscale=ALPHA, mscale=1.0, name=nm("out_proj_ln"))
    (g2, u2, a2), _ = _mm_swiglu(x2b, W["ffn2_wg"], W["ffn2_wu"], name=nm("ffn2_up"))
    x3, x3b, xh3, rs3 = _mm_ln(a2, W["ffn2_wd"], x2, W["ln3_g"], W["ln3_b"], rscale=ALPHA, mscale=0.5, name=nm("ffn2_down_ln"))
    x4, x4b, sg, e = _mm_pe(x3, x3b, pb, W["pe_gate_w"], W["pe_gate_b"], W["pe_proj"], name=nm("ple"))
    sv.update(g1=g1, u1=u1, a1=a1, x1b=x1b, xh1=xh1, rs1=rs1, hbuf=hbuf, lu=lu, lr=lr, lig=lig, la=la, lh=lh,
              eq=eq, ek=ek, lse_rows=lse_rows, yb=yb, yssd=yssd, states=states, ymix=ymix, x2b=x2b, xh2=xh2, rs2=rs2,
              g2=g2, u2=u2, a2=a2, x3b=x3b, xh3=xh3, rs3=rs3, sg=sg, e=e, pb=pb)
    return x4, x4b, sv, W


def _layer_bwd(li, dx4, sv, W, comm=None, late=None, last=None):
    nm = lambda s: f"l{li}_{s}"
    G = {}
    dgp, de, dbg = _pe_bwd_elem(dx4, sv["sg"], sv["e"], name=nm("ple_bwd"))
    G["pe_gate_b"] = dbg
    G["pe_gate_w"] = _mm(sv["x3b"], dgp, ta=True, out_dtype=BF16, name=nm("d_pe_gate_w"))
    G["pe_proj"] = _mm(sv["pb"], de, ta=True, out_dtype=BF16, chip_cols=True, name=nm("d_pe_proj"))
    dr3, dr3b, G["ln3_g"], G["ln3_b"] = _bwd_proj([(dgp, W["pe_gate_w"])], dx4, rscale=1.0,
                                                  ln=(sv["xh3"], sv["rs3"], W["ln3_g"]), name=nm("ln3_bwd"))
    G["ffn2_wd"] = _mm(sv["a2"], dr3b, ta=True, scale=0.5, out_dtype=BF16, name=nm("d_ffn2_wd"))
    dg2, du2 = _mm_swiglu_bwd(dr3b, W["ffn2_wd"], sv["g2"], sv["u2"], scale=0.5, name=nm("ffn2_act_bwd"))
    G["ffn2_wg"] = _mm(sv["x2b"], dg2, ta=True, out_dtype=BF16, chip_cols=True, name=nm("d_ffn2_wg"))
    G["ffn2_wu"] = _mm(sv["x2b"], du2, ta=True, out_dtype=BF16, chip_cols=True, name=nm("d_ffn2_wu"))
    dr2, dr2b, G["ln2_g"], G["ln2_b"] = _bwd_proj([(dg2, W["ffn2_wg"]), (du2, W["ffn2_wu"])], dr3, rscale=ALPHA,
                                                  ln=(sv["xh2"], sv["rs2"], W["ln2_g"]), name=nm("ln2_bwd"))
    G["w_out"] = _mm(sv["ymix"], dr2b, ta=True, out_dtype=BF16, name=nm("d_w_out"))
    dymix = _mm(dr2b, W["w_out"], tb=True, name=nm("d_ymix"))
    hbuf = sv["hbuf"]
    (dur, dgr, G["lru_conv_w"], G["lru_conv_b"], G["lru_wa_bd"], G["lru_ba"], G["lru_wx_bd"], G["lru_bx"],
     G["lru_lambda"]) = _lru_bwd(dymix, hbuf, sv["lu"], sv["lr"], sv["lig"], sv["la"], sv["lh"],
                                 W["lru_conv_w"], W["lru_wa_bd"], W["lru_wx_bd"], W["lru_lambda"], name=nm("lru_bwd"))
    delta = _fox_delta(dymix, sv["yb"], name=nm("fox_delta"))
    delta_rows = jnp.pad(delta[:, :ATT_HEADS].T, ((0, SUBLANES - ATT_HEADS), (0, 0)))
    comm = _merge_comms([comm, late(G) if late else None])
    (dk, dv, dfk, dqt, dfq), comm_out = _fox_bwd(hbuf, sv["eq"], sv["ek"], dymix, sv["lse_rows"], delta_rows,
                                                 comm=comm, name=nm("fox_bwd"))
    dq = dqt.T
    dfc = jnp.pad(dfq[:ATT_HEADS].T, ((0, 0), (0, LANES - ATT_HEADS))) - dfk
    dsm_f, G["fox_bf_vec"] = _fox_post(dfc, hbuf, W["fox_bf_vec"], name=nm("fox_post"))
    (dxr, dz, dsm_dt, G["ssd_norm_g"], G["ssd_d_exp"], G["ssd_a_vec"], G["ssd_dtb_vec"], G["ssd_conv_w"],
     G["ssd_conv_b"]) = _ssd_bwd(dymix, hbuf, sv["yssd"], sv["states"], W["ssd_conv_w"], W["ssd_conv_b"],
                                 W["ssd_dtb_vec"], W["ssd_a_vec"], W["ssd_d_exp"], W["ssd_norm_g"], name=nm("ssd_bwd"))
    t = dx4.shape[0]
    dh = jnp.concatenate([dxr.astype(BF16), dz.astype(BF16), dur.astype(BF16), dgr.astype(BF16), dq.astype(BF16),
                          dk.astype(BF16), dv.astype(BF16), (dsm_f + dsm_dt).astype(BF16),
                          jnp.zeros((t, H_WIDTH - COL_SMALL - LANES), BF16)], axis=1)
    G["w_in_p"] = _mm(sv["x1b"], dh, ta=True, name=nm("d_w_in"))
    dr1, dr1b, G["ln1_g"], G["ln1_b"] = _bwd_proj([(dh, W["w_in_p"])], dr2, rscale=ALPHA,
                                                  ln=(sv["xh1"], sv["rs1"], W["ln1_g"]), name=nm("ln1_bwd"))
    G["ffn1_wd"] = _mm(sv["a1"], dr1b, ta=True, scale=0.5, out_dtype=BF16, name=nm("d_ffn1_wd"))
    dg1, du1 = _mm_swiglu_bwd(dr1b, W["ffn1_wd"], sv["g1"], sv["u1"], scale=0.5, name=nm("ffn1_act_bwd"))
    G["ffn1_wg"] = _mm(sv["x_in_b"], dg1, ta=True, out_dtype=BF16, chip_cols=True, name=nm("d_ffn1_wg"))
    G["ffn1_wu"] = _mm(sv["x_in_b"], du1, ta=True, out_dtype=BF16, chip_cols=True, name=nm("d_ffn1_wu"))
    dx_in, *last_out = _bwd_proj([(dg1, W["ffn1_wg"]), (du1, W["ffn1_wu"])], dr1, rscale=ALPHA, ln=None,
                                 comm=last(G) if last else None, name=nm("x_in_bwd"))
    return dx_in, G, comm_out, (last_out[0] if last_out else None)


def _block_diag(w):
    n, b, _ = w.shape
    eye = jnp.eye(n, dtype=w.dtype)
    return (eye[:, None, :, None] * w[:, :, None, :]).reshape(n * b, n * b)


def _block_diag_extract(m):
    n, b = LRU_HEADS, HEAD_DIM
    return jnp.stack([m[b * i:b * (i + 1), b * i:b * (i + 1)] for i in range(n)])


def _lane_vec(v, lane0):
    return jnp.pad(v.astype(F32), (lane0, LANES - lane0 - v.shape[0])).reshape(1, LANES)


def _w_in_permute(w):
    d = w.shape[0]
    z = lambda n: jnp.zeros((d, n), w.dtype)
    return jnp.concatenate([w[:, 1796:2820], w[:, 1284:1796], w[:, 0:512], w[:, 512:1280],
                            w[:, 1280:1284], w[:, 2820:2828], z(LANES - 12), z(H_WIDTH - COL_SMALL - LANES)], axis=1)


def _w_in_unpermute(wp):
    return jnp.concatenate([wp[:, COL_U:COL_Q], wp[:, COL_Q:COL_SMALL], wp[:, COL_SMALL:COL_SMALL + 4],
                            wp[:, COL_Z:COL_U], wp[:, COL_XBC:COL_Z], wp[:, COL_SMALL + 4:COL_SMALL + 12]], axis=1)


def _big_weights(chipw):
    W = {}
    for n, w in chipw.items():
        if n in ("ffn1_wg", "ffn1_wu", "ffn2_wg", "ffn2_wu"):
            W[n] = w
        elif n in ("ffn1_wd", "ffn2_wd", "w_out", "pe_gate_w"):
            W[n] = w.reshape(-1, D_MODEL)
        elif n == "pe_proj":
            W[n] = jnp.moveaxis(w, 0, 1).reshape(PLE_DIM, D_MODEL)
        else:
            w_in = jnp.moveaxis(w[:, :, :IN_WIDTH // N_CHIPS], 0, 1).reshape(D_MODEL, IN_WIDTH)
            W["w_in_p"] = _w_in_permute(w_in)
    return W


def _small_weights(li, small):
    g = lambda n: small[n][li]
    W = {n: g(n) for n in ("ln1_g", "ln1_b", "ln2_g", "ln2_b", "ln3_g", "ln3_b", "pe_gate_b", "lru_conv_w",
                           "ssd_conv_w")}
    for n in ("lru_conv_b", "lru_ba", "lru_bx", "lru_lambda", "ssd_conv_b", "ssd_norm_g"):
        W[n] = g(n).reshape(1, -1)
    W["lru_wa_bd"] = _block_diag(g("lru_wa")).astype(BF16)
    W["lru_wx_bd"] = _block_diag(g("lru_wx")).astype(BF16)
    W["fox_bf_vec"] = _lane_vec(g("fox_bf"), LANE_F)
    W["ssd_dtb_vec"] = _lane_vec(g("ssd_dt_bias"), LANE_DT)
    W["ssd_a_vec"] = _lane_vec(-jnp.exp(g("ssd_a_log")), LANE_DT)
    W["ssd_d_exp"] = jnp.repeat(g("ssd_d"), HEAD_DIM).reshape(1, SSD_WIDTH)
    return W


def _big_grad_by_chip(G, n):
    if n in ("ffn1_wg", "ffn1_wu", "ffn2_wg", "ffn2_wu", "pe_proj"):
        return G[n]
    if n in ("ffn1_wd", "ffn2_wd", "w_out", "pe_gate_w"):
        return G[n].reshape(N_CHIPS, -1, D_MODEL)
    share = IN_WIDTH // N_CHIPS
    d_w_in = jnp.moveaxis(_w_in_unpermute(G["w_in_p"]).reshape(D_MODEL, N_CHIPS, share), 1, 0)
    return jnp.pad(d_w_in.astype(BF16), ((0, 0), (0, 0), (0, SHARE - share)))


def _layer_small_grads(G, W):
    out = {n: G[n] for n in ("lru_conv_w", "ssd_conv_w")}
    for n in ("ln1_g", "ln1_b", "ln2_g", "ln2_b", "ln3_g", "ln3_b", "pe_gate_b", "lru_conv_b", "lru_ba", "lru_bx",
              "lru_lambda", "ssd_conv_b", "ssd_norm_g"):
        out[n] = G[n].reshape(-1)
    out["lru_wa"] = _block_diag_extract(G["lru_wa_bd"])
    out["lru_wx"] = _block_diag_extract(G["lru_wx_bd"])
    out["fox_bf"] = G["fox_bf_vec"][0, LANE_F:LANE_F + ATT_HEADS]
    out["ssd_dt_bias"] = G["ssd_dtb_vec"][0, LANE_DT:LANE_DT + SSD_HEADS]
    out["ssd_a_log"] = G["ssd_a_vec"][0, LANE_DT:LANE_DT + SSD_HEADS] * W["ssd_a_vec"][0, LANE_DT:LANE_DT + SSD_HEADS]
    out["ssd_d"] = G["ssd_d_exp"].reshape(SSD_HEADS, HEAD_DIM).sum(axis=1)
    return out


WEIGHTS = ['ln1_g', 'ln1_b', 'ffn1_wg', 'ffn1_wu', 'ffn1_wd', 'w_in', 'lru_conv_w', 'lru_conv_b', 'lru_wa', 'lru_ba',
           'lru_wx', 'lru_bx', 'lru_lambda', 'fox_bf', 'ssd_conv_w', 'ssd_conv_b', 'ssd_dt_bias', 'ssd_a_log', 'ssd_d',
           'ssd_norm_g', 'w_out', 'ln2_g', 'ln2_b', 'ffn2_wg', 'ffn2_wu', 'ffn2_wd', 'ln3_g', 'ln3_b', 'pe_proj',
           'pe_gate_w', 'pe_gate_b']
FIRST = ((("ffn1_wg",), 1), (("ffn1_wu",), 1))
NEXT = ((("w_in",), 1),
        (("ffn1_wd",), 0))
EARLY = FIRST + NEXT
LATE = ((("ffn2_wg",), 1), (("ffn2_wu",), 1),
        (("ffn2_wd",), 0),
        (("w_out",), None), (("pe_gate_w",), None),
        (("pe_proj",), None))
BIG = {n: pad for names, pad in EARLY + LATE for n in names}
SMALL_SHARDED = {'lru_conv_w': 2, 'ssd_conv_w': 2}


def _unshard(seg, axis):
    moved = jnp.moveaxis(seg, 0, axis)
    shp = list(moved.shape)
    shp[axis:axis + 2] = [shp[axis] * shp[axis + 1]]
    return moved.reshape(shp)


def _pad_axis(a, axis, size):
    if axis is None or a.shape[axis] == size:
        return a
    pads = [(0, 0)] * a.ndim
    pads[axis] = (0, size - a.shape[axis])
    return jnp.pad(a, pads)


PACK_TILE = SUBLANES * LANES


def _pack(arrs):
    rows = []
    for a in arrs:
        flat = a.astype(F32).reshape(-1)
        rows.append(jnp.pad(flat, (0, (-flat.shape[0]) % PACK_TILE)).reshape(-1, LANES))
    return jnp.concatenate(rows, axis=0)


def _unpack(packed, shapes):
    out, off = [], 0
    for s in shapes:
        n = math.prod(s)
        r = -(-n // PACK_TILE) * SUBLANES
        out.append(packed[off:off + r].reshape(-1)[:n].reshape(s))
        off += r
    return out


def kernel(x, p, ln1_g, ln1_b, ffn1_wg, ffn1_wu, ffn1_wd, w_in, lru_conv_w, lru_conv_b, lru_wa, lru_ba, lru_wx, lru_bx, lru_lambda, fox_bf, ssd_conv_w, ssd_conv_b, ssd_dt_bias, ssd_a_log, ssd_d, ssd_norm_g, w_out, ln2_g, ln2_b, ffn2_wg, ffn2_wu, ffn2_wd, ln3_g, ln3_b, pe_proj, pe_gate_w, pe_gate_b, loss_target, m_ln1_g, m_ln1_b, m_ffn1_wg, m_ffn1_wu, m_ffn1_wd, m_w_in, m_lru_conv_w, m_lru_conv_b, m_lru_wa, m_lru_ba, m_lru_wx, m_lru_bx, m_lru_lambda, m_fox_bf, m_ssd_conv_w, m_ssd_conv_b, m_ssd_dt_bias, m_ssd_a_log, m_ssd_d, m_ssd_norm_g, m_w_out, m_ln2_g, m_ln2_b, m_ffn2_wg, m_ffn2_wu, m_ffn2_wd, m_ln3_g, m_ln3_b, m_pe_proj, m_pe_gate_w, m_pe_gate_b, v_ln1_g, v_ln1_b, v_ffn1_wg, v_ffn1_wu, v_ffn1_wd, v_w_in, v_lru_conv_w, v_lru_conv_b, v_lru_wa, v_lru_ba, v_lru_wx, v_lru_bx, v_lru_lambda, v_fox_bf, v_ssd_conv_w, v_ssd_conv_b, v_ssd_dt_bias, v_ssd_a_log, v_ssd_d, v_ssd_norm_g, v_w_out, v_ln2_g, v_ln2_b, v_ffn2_wg, v_ffn2_wu, v_ffn2_wd, v_ln3_g, v_ln3_b, v_pe_proj, v_pe_gate_w, v_pe_gate_b):
    args = locals()
    w_loc = {n: args[n] for n in WEIGHTS}
    m_loc = {n: args["m_" + n] for n in WEIGHTS}
    v_loc = {n: args["v_" + n] for n in WEIGHTS}
    chip = 2 * lax.axis_index("x") + lax.axis_index("y")
    core = lax.axis_index("c")
    big = list(BIG)
    small_sh = list(SMALL_SHARDED)
    small_rep = [n for n in WEIGHTS if n not in BIG and n not in SMALL_SHARDED]

    def srcs_of(li, groups):
        return [jnp.stack([_pad_axis(w_loc[n][li].astype(BF16), pad, SHARE) for n in names]) for names, pad in groups]

    def gather_comm(li, groups, base=0):
        return _gather_layer_comm(srcs_of(li, groups), li, base)

    def chip_weights(gathered, groups):
        return _big_weights({n: g[:, j] for (names, _), g in zip(groups, gathered) for j, n in enumerate(names)})

    def pair_sums(G, groups, tag):
        gs = [jnp.stack([_big_grad_by_chip(G, n) for n in names]) for names, _ in groups]
        flat = [g.reshape((-1,) + g.shape[2:]) for g in gs]
        theirs = _exchange(flat, ("c",), swap=True, name=f"reduce_cores_{tag}")
        return [_sum_pair(f, r, BF16, name=f"reduce_cores_sum_{tag}_{gi}").reshape(g.shape)
                for gi, (f, r, g) in enumerate(zip(flat, theirs, gs))]

    def finish_reduce(quad, sums, li, groups, tag):
        quad = [lax.dynamic_update_index_in_dim(q, lax.dynamic_index_in_dim(s, chip, 1, keepdims=False), chip, 0)
                for q, s in zip(quad, sums)]
        red = [_sum_slots(q.reshape(N_CHIPS, -1, q.shape[-1]), F32,
                          name=f"reduce_chips_sum_{tag}_{gi}").reshape(q.shape[1:]) for gi, q in enumerate(quad)]
        theirs = _exchange(red, ("c",), swap=True, name=f"reduce_share_{tag}")
        out = {}
        for (names, _), r, rv in zip(groups, red, theirs):
            both = jnp.where(core == li, r, rv)
            for j, n in enumerate(names):
                out[n] = both[j]
        return out

    everything = EARLY + LATE
    first0 = _run_comm(gather_comm(0, FIRST), name="gather_w_l0")
    small = {n: w_loc[n] for n in small_rep}
    (sg,) = _exchange([_pack([w_loc[n] for n in small_sh])[None]], ("x", "y"), name="gather_conv_w")
    shards = [_unpack(sg[k, 0], [w_loc[n].shape for n in small_sh]) for k in range(N_CHIPS)]
    for j, n in enumerate(small_sh):
        small[n] = _unshard(jnp.stack([shards[k][j] for k in range(N_CHIPS)]), SMALL_SHARDED[n])

    W0 = {**_small_weights(0, small), **chip_weights(first0, FIRST)}
    late0_comm = gather_comm(0, LATE)
    early1 = []

    def in_attention0(got):
        early1.extend(got[len(LATE):])
        return chip_weights(got[:len(LATE)], LATE)

    xs = x[0]
    xs, xb, sv0, W0 = _layer_fwd(
        0, xs, xs.astype(BF16), p[0, 0].astype(BF16), W0,
        up=(gather_comm(0, NEXT), lambda got: chip_weights(got, NEXT)),
        att=(_merge_comms([late0_comm, gather_comm(1, EARLY, base=late0_comm.n_sems)]), in_attention0))
    W1 = {**_small_weights(1, small), **chip_weights(early1, EARLY)}
    xs, _, sv1, W1 = _layer_fwd(1, xs, xb, p[1, 0].astype(BF16), W1,
                                att=(gather_comm(1, LATE), lambda got: chip_weights(got, LATE)))
    dx, loss = _loss_kernel(xs, loss_target[0], name="loss")
    loss = lax.psum(loss[0, 0], MESH_AXES)
    dx, G1, _, _ = _layer_bwd(1, dx, sv1, W1)
    sums1 = pair_sums(G1, everything, "l1")
    comm1 = _reduce_chips_comm(sums1, 1)
    late_sums, early_sums = [], []

    def late0(G):
        late_sums.extend(pair_sums(G, LATE, "l0_late"))
        return _reduce_chips_comm(late_sums, 0, base=comm1.n_sems)

    def last0(G):
        early_sums.extend(pair_sums(G, EARLY, "l0"))
        return _reduce_chips_comm(early_sums, 0)

    grad_x, G0, quads, quads0 = _layer_bwd(0, dx, sv0, W0, comm=comm1, late=late0, last=last0)

    n1 = len(comm1.out_shapes)
    red = [{**finish_reduce(quads[n1:], late_sums, 0, LATE, "l0_late"),
            **finish_reduce(quads0, early_sums, 0, EARLY, "l0")},
           finish_reduce(quads[:n1], sums1, 1, everything, "l1")]
    g_red = {}
    for n in big:
        g = jnp.stack([red[li][n] for li in range(DEPTH)])
        g_red[n] = g[tuple(slice(0, s) for s in w_loc[n].shape)]
    small_l = [_layer_small_grads(G0, W0), _layer_small_grads(G1, W1)]
    g_small = {n: jnp.stack([small_l[li][n] for li in range(DEPTH)]) for n in small_l[0]}
    small_all = small_rep + small_sh
    sgp = _pack([g_small[n] for n in small_all])
    (sall,) = _exchange([sgp[None]], MESH_AXES, name="reduce_small")
    sred = _sum_slots(sall.reshape((2 ** len(MESH_AXES),) + sgp.shape), F32, name="reduce_small_sum")
    for n, g in zip(small_all, _unpack(sred, [g_small[n].shape for n in small_all])):
        if n in SMALL_SHARDED:
            width = w_loc[n].shape[-1]
            g = lax.dynamic_slice_in_dim(g, chip * width, width, axis=SMALL_SHARDED[n])
        g_red[n] = g

    delta, new_m, new_v = {}, {}, {}
    for n in big:
        delta[n], new_m[n], new_v[n] = _adamw(w_loc[n], g_red[n], m_loc[n], v_loc[n], name="adamw_" + n)
    shapes = [w_loc[n].shape for n in small_all]
    packs = [_pack([d[n] for n in small_all]) for d in (w_loc, g_red, m_loc, v_loc)]
    outs = _adamw(*packs, name="adamw_small")
    for d, o in zip((delta, new_m, new_v), outs):
        for n, a in zip(small_all, _unpack(o, shapes)):
            d[n] = a
    return (loss, grad_x[None], *[g_red[n] for n in WEIGHTS], *[delta[n] for n in WEIGHTS],
            *[new_m[n] for n in WEIGHTS], *[new_v[n] for n in WEIGHTS])
```

```python
import math

import jax
import jax.numpy as jnp
from jax import lax
from jax.experimental import pallas as pl
from jax.experimental.pallas import tpu as pltpu

F32 = jnp.float32
BF16 = jnp.bfloat16

D_MODEL = 1024
DEPTH = 2
PLE_DIM = 256
HEAD_DIM = 64
LRU_WIDTH = 256
LRU_HEADS = 4
LRU_C = 8.0
CONV_K = 4
ATT_WIDTH = 256
ATT_HEADS = 4
SSD_WIDTH = 512
SSD_HEADS = 8
SSD_GROUPS = 2
SSD_STATE = 128
SSD_CHUNK = 128
SSD_CONV_DIM = 1024
ALPHA = (2.0 * DEPTH) ** 0.25
LN_EPS = 1e-5
RMS_EPS = 1e-5
IN_WIDTH = 2828
ADAM_LR = 0.001
ADAM_B1 = 0.9
ADAM_B2 = 0.999
ADAM_EPS = 1e-08
ADAM_WD = 0.01
ADAM_STEP = 10

H_WIDTH = 3072
COL_XBC, COL_Z, COL_U, COL_G, COL_Q, COL_K, COL_V, COL_SMALL = 0, 1024, 1536, 1792, 2048, 2304, 2560, 2816
LANE_F = 0
LANE_DT = 4
LANES = 128
SUBLANES = 8
NEG = -1e30

VMEM_LIMIT = 48 * 1024 * 1024

N_CHIPS = 4
MESH_AXES = ("x", "y", "c")
SHARE = 768


def _params(n):
    return pltpu.CompilerParams(dimension_semantics=("arbitrary",) * n, vmem_limit_bytes=VMEM_LIMIT)


def _pick(n, cands):
    for c in cands:
        if n % c == 0:
            return c
    return n


def _iota(shape, dim):
    return lax.broadcasted_iota(jnp.int32, shape, dim)


def _shift_down(x, s, prev8):
    if s == 0:
        return x
    r = pltpu.roll(x, s, 0)
    pr = pltpu.roll(prev8, s, 0)
    head = jnp.where(_iota(pr.shape, 0) < s, pr, r[:SUBLANES])
    return jnp.concatenate([head, r[SUBLANES:]], axis=0)


def _shift_up(x, s, next8):
    if s == 0:
        return x
    n = x.shape[0]
    r = pltpu.roll(x, n - s, 0)
    nr = pltpu.roll(next8, SUBLANES - s, 0)
    tail = jnp.where(_iota(nr.shape, 0) >= SUBLANES - s, nr, r[n - SUBLANES:])
    return jnp.concatenate([r[:n - SUBLANES], tail], axis=0)


def _scan_fwd(a, b):
    n = a.shape[0]
    row = _iota(a.shape, 0)
    d = 1
    while d < n:
        keep = row >= d
        a_s = jnp.where(keep, pltpu.roll(a, d, 0), 1.0)
        b_s = jnp.where(keep, pltpu.roll(b, d, 0), 0.0)
        b = a * b_s + b
        a = a * a_s
        d *= 2
    return a, b


def _scan_bwd(a, b):
    n = a.shape[0]
    row = _iota(a.shape, 0)
    d = 1
    while d < n:
        keep = row < n - d
        a_s = jnp.where(keep, pltpu.roll(a, n - d, 0), 1.0)
        b_s = jnp.where(keep, pltpu.roll(b, n - d, 0), 0.0)
        b = a * b_s + b
        a = a * a_s
        d *= 2
    return a, b


def _cumsum_rows(x, reverse=False):
    n = x.shape[0]
    row = _iota(x.shape, 0)
    d = 1
    while d < n:
        if reverse:
            x = x + jnp.where(row < n - d, pltpu.roll(x, n - d, 0), 0.0)
        else:
            x = x + jnp.where(row >= d, pltpu.roll(x, d, 0), 0.0)
        d *= 2
    return x


def _col(x, lane):
    return jnp.sum(jnp.where(_iota(x.shape, 1) == lane, x, 0.0), axis=1, keepdims=True)


def _row(x, r):
    return jnp.sum(jnp.where(_iota(x.shape, 0) == r, x, 0.0), axis=0, keepdims=True)


def _sigmoid(x):
    return jax.nn.sigmoid(x)


def _softplus(x):
    return jnp.maximum(x, 0.0) + jnp.log(1.0 + jnp.exp(-jnp.abs(x)))


def _gelu_and_grad(x):
    c0 = math.sqrt(2.0 / math.pi)
    inner = c0 * (x + 0.044715 * x * x * x)
    t = jnp.tanh(inner)
    g = 0.5 * x * (1.0 + t)
    dg = 0.5 * (1.0 + t) + 0.5 * x * (1.0 - t * t) * c0 * (1.0 + 3.0 * 0.044715 * x * x)
    return g, dg


def _dot(a, b, ca, cb):
    return lax.dot_general(a, b, (((ca,), (cb,)), ((), ())), preferred_element_type=F32)


def _conv_taps(xr, prev8, w, bias):
    y = bias + w[CONV_K - 1:CONV_K, :] * xr
    for j in range(CONV_K - 1):
        y = y + w[j:j + 1, :] * _shift_down(xr, CONV_K - 1 - j, prev8)
    return y


def _conv_taps_bwd(dy, next8, w, xr):
    dx = None
    dws = []
    for j in range(CONV_K):
        sh = _shift_up(dy, CONV_K - 1 - j, next8)
        term = w[j:j + 1, :] * sh
        dx = term if dx is None else dx + term
        dws.append(jnp.sum(sh * xr, axis=0, keepdims=True))
    return dx, jnp.concatenate(dws, axis=0)


def _head_expand(v, lane0, nheads, width):
    rows = v.shape[0]
    colhead = _iota((rows, width), 1) // HEAD_DIM
    out = jnp.zeros((rows, width), F32)
    for h in range(nheads):
        out = jnp.where(colhead == h, _col(v, lane0 + h), out)
    return out


def _head_reduce(x, lane0, nheads):
    rows = x.shape[0]
    colhead = _iota(x.shape, 1) // HEAD_DIM
    lane = _iota((rows, LANES), 1)
    out = jnp.zeros((rows, LANES), F32)
    for h in range(nheads):
        s = jnp.sum(jnp.where(colhead == h, x, 0.0), axis=1, keepdims=True)
        out = jnp.where(lane == lane0 + h, s, out)
    return out


def _mm(a, b, *, ta=False, tb=False, scale=1.0, out_dtype=F32, chip_cols=False, name):
    if ta:
        kk, m = a.shape
    else:
        m, kk = a.shape
    n = b.shape[0] if tb else b.shape[1]
    tm = _pick(m, (1024, 512, 256, 128))
    tk = _pick(kk, (1024, 768, 512, 256, 128))
    nk = kk // tk
    dn_a = 0 if ta else 1
    dn_b = 1 if tb else 0
    share = n // N_CHIPS
    if chip_cols:
        tn = n
        out_spec = pl.BlockSpec((N_CHIPS, tm, share), lambda i, j, k: (0, i, 0))
        out_shape = jax.ShapeDtypeStruct((N_CHIPS, m, share), out_dtype)
    else:
        tn = _pick(n, (1024, 768, 512, 256, 128))
        out_spec = pl.BlockSpec((tm, tn), lambda i, j, k: (i, j))
        out_shape = jax.ShapeDtypeStruct((m, n), out_dtype)

    def body(a_ref, b_ref, o_ref, acc):
        k = pl.program_id(2)

        @pl.when(k == 0)
        def _():
            acc[...] = jnp.zeros_like(acc)

        acc[...] += _dot(a_ref[...].astype(BF16), b_ref[...].astype(BF16), dn_a, dn_b)

        @pl.when(k == nk - 1)
        def _():
            if chip_cols:
                for c in range(N_CHIPS):
                    o_ref[c] = (acc[:, share * c:share * (c + 1)] * scale).astype(out_dtype)
            else:
                o_ref[...] = (acc[...] * scale).astype(out_dtype)

    a_spec = pl.BlockSpec((tk, tm), lambda i, j, k: (k, i)) if ta else pl.BlockSpec((tm, tk), lambda i, j, k: (i, k))
    b_spec = pl.BlockSpec((tn, tk), lambda i, j, k: (j, k)) if tb else pl.BlockSpec((tk, tn), lambda i, j, k: (k, j))
    return pl.pallas_call(
        body, name=name, grid=(m // tm, n // tn, nk),
        in_specs=[a_spec, b_spec],
        out_specs=out_spec, out_shape=out_shape,
        scratch_shapes=[pltpu.VMEM((tm, tn), F32)],
        compiler_params=_params(3),
    )(a, b)


def _mm_swiglu(xb, wg, wu, *, comm=None, name):
    t, d = xb.shape
    share = wg.shape[2]
    n = N_CHIPS * share
    tm = _pick(t, (512, 256, 128))
    tn = _pick(share, (768, 256, 128))
    per = share // tn

    def body(x_ref, wg_ref, wu_ref, g_ref, u_ref, a_ref):
        x = x_ref[...]
        g = _dot(x, wg_ref[...], 1, 0)
        u = _dot(x, wu_ref[...], 1, 0)
        g_ref[...] = g.astype(BF16)
        u_ref[...] = u.astype(BF16)
        a_ref[...] = (g * _sigmoid(g) * u).astype(BF16)

    o = jax.ShapeDtypeStruct((t, n), BF16)
    ospec = pl.BlockSpec((tm, tn), lambda j, i: (i, j))
    return _hosted_call(
        body, comm, (n // tn, t // tm), name=name,
        in_specs=[pl.BlockSpec((tm, d), lambda j, i: (i, 0)),
                  pl.BlockSpec((None, d, tn), lambda j, i: (j // per, 0, j % per)),
                  pl.BlockSpec((None, d, tn), lambda j, i: (j // per, 0, j % per))],
        out_specs=[ospec, ospec, ospec], out_shape=[o, o, o], scratch_shapes=[], args=[xb, wg, wu])


def _mm_swiglu_bwd(dr, wd, g, u, *, scale, name):
    t, d = dr.shape
    n = wd.shape[0]
    tm = _pick(t, (512, 256, 128))
    tn = _pick(n, (768, 256, 128))

    def body(dr_ref, wd_ref, g_ref, u_ref, dg_ref, du_ref):
        da = _dot(dr_ref[...].astype(BF16), wd_ref[...], 1, 1) * scale
        gg = g_ref[...].astype(F32)
        uu = u_ref[...].astype(F32)
        sg = _sigmoid(gg)
        dg_ref[...] = (da * uu * (sg * (1.0 + gg * (1.0 - sg)))).astype(BF16)
        du_ref[...] = (da * gg * sg).astype(BF16)

    o = jax.ShapeDtypeStruct((t, n), BF16)
    ospec = pl.BlockSpec((tm, tn), lambda j, i: (i, j))
    return pl.pallas_call(
        body, name=name, grid=(n // tn, t // tm),
        in_specs=[pl.BlockSpec((tm, d), lambda j, i: (i, 0)),
                  pl.BlockSpec((tn, d), lambda j, i: (j, 0)),
                  ospec, ospec],
        out_specs=[ospec, ospec], out_shape=[o, o],
        compiler_params=_params(2),
    )(dr, wd, g, u)


def _mm_ln(a, w, resid, gain, bias, *, rscale, mscale, name):
    t, kk = a.shape
    d = w.shape[1]
    tm = _pick(t, (512, 256, 128))
    tk = kk
    nk = kk // tk

    def body(a_ref, w_ref, r_ref, g_ref, b_ref, y_ref, yb_ref, xh_ref, rs_ref, acc):
        k = pl.program_id(1)

        @pl.when(k == 0)
        def _():
            acc[...] = jnp.zeros_like(acc)

        acc[...] += _dot(a_ref[...].astype(BF16), w_ref[...], 1, 0)

        @pl.when(k == nk - 1)
        def _():
            r = rscale * r_ref[...] + mscale * acc[...]
            mu = jnp.mean(r, axis=1, keepdims=True)
            xc = r - mu
            var = jnp.mean(xc * xc, axis=1, keepdims=True)
            rstd = lax.rsqrt(var + LN_EPS)
            xh = xc * rstd
            y = xh * g_ref[...] + b_ref[...]
            y_ref[...] = y
            yb_ref[...] = y.astype(BF16)
            xh_ref[...] = xh
            rs_ref[...] = rstd

    row = pl.BlockSpec((tm, d), lambda i, k: (i, 0))
    vec = pl.BlockSpec((1, d), lambda i, k: (0, 0))
    return pl.pallas_call(
        body, name=name, grid=(t // tm, nk),
        in_specs=[pl.BlockSpec((tm, tk), lambda i, k: (i, k)),
                  pl.BlockSpec((tk, d), lambda i, k: (k, 0)), row, vec, vec],
        out_specs=[row, row, row, pl.BlockSpec((tm, 1), lambda i, k: (i, 0))],
        out_shape=[jax.ShapeDtypeStruct((t, d), F32), jax.ShapeDtypeStruct((t, d), BF16),
                   jax.ShapeDtypeStruct((t, d), F32), jax.ShapeDtypeStruct((t, 1), F32)],
        scratch_shapes=[pltpu.VMEM((tm, d), F32)],
        compiler_params=_params(2),
    )(a, w, resid, gain.reshape(1, d), bias.reshape(1, d))


def _bwd_proj(pairs, resid, *, rscale, ln, comm=None, name):
    t, kk = pairs[0][0].shape
    d = pairs[0][1].shape[-2]
    has_ln = ln is not None
    tm = _pick(t, (512, 256, 128) if has_ln else (1024, 512, 256, 128))
    tk = _pick(pairs[0][1].shape[-1], (1024, 768, 512, 256, 128))
    nk = kk // tk
    nt = t // tm
    npair = len(pairs)

    def body(*refs):
        ab = refs[:2 * npair]
        r_ref = refs[2 * npair]
        pos = 2 * npair + 1
        if has_ln:
            xh_ref, rs_ref, g_ref = refs[pos:pos + 3]
            pos += 3
            o_ref, ob_ref, dg_ref, db_ref = refs[pos:pos + 4]
            pos += 4
        else:
            o_ref = refs[pos]
            pos += 1
        acc = refs[pos]
        i = pl.program_id(0)
        k = pl.program_id(1)

        @pl.when(k == 0)
        def _():
            acc[...] = jnp.zeros_like(acc)

        for q in range(npair):
            acc[...] += _dot(ab[2 * q][...].astype(BF16), ab[2 * q + 1][...], 1, 1)

        @pl.when(k == nk - 1)
        def _():
            dy = rscale * r_ref[...] + acc[...]
            if not has_ln:
                o_ref[...] = dy
                return
            xh = xh_ref[...]
            w = dy * g_ref[...]
            m1 = jnp.mean(w, axis=1, keepdims=True)
            m2 = jnp.mean(w * xh, axis=1, keepdims=True)
            dr = rs_ref[...] * (w - m1 - xh * m2)
            o_ref[...] = dr
            ob_ref[...] = dr.astype(BF16)

            @pl.when(i == 0)
            def _():
                dg_ref[...] = jnp.zeros_like(dg_ref)
                db_ref[...] = jnp.zeros_like(db_ref)

            dg_ref[...] += jnp.sum(dy * xh, axis=0, keepdims=True)
            db_ref[...] += jnp.sum(dy, axis=0, keepdims=True)

    row = pl.BlockSpec((tm, d), lambda i, k: (i, 0))
    vec = pl.BlockSpec((1, d), lambda i, k: (0, 0))
    in_specs, args = [], []
    for a, b in pairs:
        if b.ndim == 3:
            per = b.shape[2] // tk
            b_spec = pl.BlockSpec((None, d, tk), lambda i, k, per=per: (k // per, 0, k % per))
        else:
            b_spec = pl.BlockSpec((d, tk), lambda i, k: (0, k))
        in_specs += [pl.BlockSpec((tm, tk), lambda i, k: (i, k)), b_spec]
        args += [a, b]
    in_specs.append(row)
    args.append(resid)
    out_specs = [row]
    out_shape = [jax.ShapeDtypeStruct((t, d), F32)]
    if has_ln:
        xh, rs, gain = ln
        in_specs += [row, pl.BlockSpec((tm, 1), lambda i, k: (i, 0)), vec]
        args += [xh, rs, gain.reshape(1, d)]
        out_specs += [row, vec, vec]
        out_shape += [jax.ShapeDtypeStruct((t, d), BF16)] + [jax.ShapeDtypeStruct((1, d), F32)] * 2
    outs, got = _hosted_call(body, comm, (nt, nk), name=name, in_specs=in_specs, out_specs=out_specs,
                             out_shape=out_shape, scratch_shapes=[pltpu.VMEM((tm, d), F32)], args=args)
    return tuple(outs) if comm is None else tuple(outs) + (got,)


def _mm_pe(x3, x3b, pb, wgate, bgate, wproj, *, name):
    t, d = x3.shape
    pd = pb.shape[1]
    tm = _pick(t, (512, 256, 128))
    tn = _pick(d, (512, 256, 128))

    def body(x_ref, xb_ref, p_ref, wg_ref, bg_ref, wp_ref, y_ref, yb_ref, sg_ref, e_ref):
        sg = _sigmoid(_dot(xb_ref[...], wg_ref[...], 1, 0) + bg_ref[...])
        e = _dot(p_ref[...], wp_ref[...], 1, 0)
        y = x_ref[...] + sg * e
        y_ref[...] = y
        yb_ref[...] = y.astype(BF16)
        sg_ref[...] = sg.astype(BF16)
        e_ref[...] = e.astype(BF16)

    ospec = pl.BlockSpec((tm, tn), lambda i, j: (i, j))
    ob = jax.ShapeDtypeStruct((t, d), BF16)
    return pl.pallas_call(
        body, name=name, grid=(t // tm, d // tn),
        in_specs=[ospec, pl.BlockSpec((tm, d), lambda i, j: (i, 0)), pl.BlockSpec((tm, pd), lambda i, j: (i, 0)),
                  pl.BlockSpec((d, tn), lambda i, j: (0, j)), pl.BlockSpec((1, tn), lambda i, j: (0, j)),
                  pl.BlockSpec((pd, tn), lambda i, j: (0, j))],
        out_specs=[ospec, ospec, ospec, ospec],
        out_shape=[jax.ShapeDtypeStruct((t, d), F32), ob, ob, ob],
        compiler_params=_params(2),
    )(x3, x3b, pb, wgate, bgate.reshape(1, d), wproj)


def _pe_bwd_elem(dx4, sg, e, *, name):
    t, d = dx4.shape
    tm = _pick(t, (512, 256, 128))

    def body(dx_ref, sg_ref, e_ref, dgp_ref, de_ref, db_ref):
        dx = dx_ref[...]
        s = sg_ref[...].astype(F32)
        dgp = dx * e_ref[...].astype(F32) * s * (1.0 - s)
        dgp_ref[...] = dgp.astype(BF16)
        de_ref[...] = (dx * s).astype(BF16)

        @pl.when(pl.program_id(0) == 0)
        def _():
            db_ref[...] = jnp.zeros_like(db_ref)

        db_ref[...] += jnp.sum(dgp, axis=0, keepdims=True)

    row = pl.BlockSpec((tm, d), lambda i: (i, 0))
    ob = jax.ShapeDtypeStruct((t, d), BF16)
    return pl.pallas_call(
        body, name=name, grid=(t // tm,), in_specs=[row, row, row],
        out_specs=[row, row, pl.BlockSpec((1, d), lambda i: (0, 0))],
        out_shape=[ob, ob, jax.ShapeDtypeStruct((1, d), F32)],
        compiler_params=_params(1),
    )(dx4, sg, e)


def _assemble(pieces, width, *, name):
    t = pieces[0].shape[0]
    tm = _pick(t, (512, 256, 128))
    widths = [p.shape[1] for p in pieces]

    def body(*refs):
        o_ref = refs[-1]
        off = 0
        for p_ref, w in zip(refs[:-1], widths):
            o_ref[:, off:off + w] = p_ref[...].astype(BF16)
            off += w
        if off < width:
            o_ref[:, off:] = jnp.zeros((tm, width - off), BF16)

    return pl.pallas_call(
        body, name=name, grid=(t // tm,),
        in_specs=[pl.BlockSpec((tm, w), lambda i: (i, 0)) for w in widths],
        out_specs=pl.BlockSpec((tm, width), lambda i: (i, 0)),
        out_shape=jax.ShapeDtypeStruct((t, width), BF16),
        compiler_params=_params(1),
    )(*pieces)


def _loss_kernel(y, target, *, name):
    t, d = y.shape
    tm = _pick(t, (512, 256, 128))

    def body(y_ref, t_ref, dy_ref, l_ref):
        diff = y_ref[...] - t_ref[...]
        dy_ref[...] = diff * (1.0 / d)

        @pl.when(pl.program_id(0) == 0)
        def _():
            l_ref[...] = jnp.zeros_like(l_ref)

        part = jnp.sum(jnp.mean(diff * diff, axis=1, keepdims=True), axis=0, keepdims=True)
        l_ref[...] += 0.5 * part

    row = pl.BlockSpec((tm, d), lambda i: (i, 0))
    return pl.pallas_call(
        body, name=name, grid=(t // tm,), in_specs=[row, row],
        out_specs=[row, pl.BlockSpec((1, 1), lambda i: (0, 0))],
        out_shape=[jax.ShapeDtypeStruct((t, d), F32), jax.ShapeDtypeStruct((1, 1), F32)],
        compiler_params=_params(1),
    )(y, target)


LRU_TM = 256


def _lru_gate_terms(r, lam):
    sp = _softplus(-lam)
    la = -LRU_C * r * sp
    a = jnp.exp(la)
    em = jnp.tanh(la) * (jnp.exp(2.0 * la) + 1.0)
    s = jnp.sqrt(-em)
    return la, a, s, sp


def _lru_fwd(hbuf, conv_w, conv_b, wa, ba, wx, bx, lam, *, name):
    t = hbuf.shape[0]
    w = LRU_WIDTH
    tm = _pick(t, (LRU_TM, 128))
    cu, cg = COL_U // w, COL_G // w
    hb = tm // SUBLANES

    def body(u_ref, up_ref, g_ref, cw_ref, cb_ref, wa_ref, ba_ref, wx_ref, bx_ref, lam_ref,
             y_ref, u_out, r_out, i_out, a_out, h_out, carry):
        i = pl.program_id(0)

        @pl.when(i == 0)
        def _():
            carry[...] = jnp.zeros_like(carry)

        prev = jnp.where(i == 0, 0.0, up_ref[...])
        u = _conv_taps(u_ref[...], prev, cw_ref[...], cb_ref[...])
        ub = u.astype(BF16)
        r = _sigmoid(_dot(ub, wa_ref[...], 1, 0) + ba_ref[...])
        ig = _sigmoid(_dot(ub, wx_ref[...], 1, 0) + bx_ref[...])
        _, a, s, _ = _lru_gate_terms(r, lam_ref[...])
        b = s * (ig * u)
        acum, hs = _scan_fwd(a, b)
        h = hs + acum * carry[0:1, :]
        carry[...] = jnp.broadcast_to(h[tm - 1:tm, :], carry.shape)
        gl, _ = _gelu_and_grad(g_ref[...])
        y_ref[...] = h * gl
        u_out[...] = u
        r_out[...] = r
        i_out[...] = ig
        a_out[...] = a
        h_out[...] = h

    row = pl.BlockSpec((tm, w), lambda i: (i, 0))
    vec = pl.BlockSpec((1, w), lambda i: (0, 0))
    mat = pl.BlockSpec((w, w), lambda i: (0, 0))
    o = jax.ShapeDtypeStruct((t, w), F32)
    return pl.pallas_call(
        body, name=name, grid=(t // tm,),
        in_specs=[pl.BlockSpec((tm, w), lambda i: (i, cu)),
                  pl.BlockSpec((SUBLANES, w), lambda i: (jnp.maximum(i * hb - 1, 0), cu)),
                  pl.BlockSpec((tm, w), lambda i: (i, cg)),
                  pl.BlockSpec((CONV_K, w), lambda i: (0, 0)), vec, mat, vec, mat, vec, vec],
        out_specs=[row] * 6, out_shape=[o] * 6,
        scratch_shapes=[pltpu.VMEM((SUBLANES, w), F32)],
        compiler_params=_params(1),
    )(hbuf, hbuf, hbuf, conv_w, conv_b, wa, ba, wx, bx, lam)


def _lru_bwd(dymix, hbuf, u, r, ig, a, h, conv_w, wa, wx, lam, *, name):
    t = hbuf.shape[0]
    w = LRU_WIDTH
    tm = _pick(t, (LRU_TM, 128))
    nb = t // tm
    cu, cg = COL_U // w, COL_G // w
    hb = tm // SUBLANES
    last8 = t // SUBLANES - 1

    def body(dy_ref, ur_ref, g_ref, u_ref, r_ref, i_ref, a_ref, an_ref, h_ref, hp_ref,
             cw_ref, wa_ref, wx_ref, lam_ref,
             dur_ref, dgr_ref, dcw_ref, dcb_ref, dwa_ref, dba_ref, dwx_ref, dbx_ref, dlam_ref,
             lcarry, dnext):
        i = pl.program_id(0)
        ib = nb - 1 - i

        @pl.when(i == 0)
        def _():
            lcarry[...] = jnp.zeros_like(lcarry)
            dnext[...] = jnp.zeros_like(dnext)
            for ref in (dcw_ref, dcb_ref, dwa_ref, dba_ref, dwx_ref, dbx_ref, dlam_ref):
                ref[...] = jnp.zeros_like(ref)

        dy = dy_ref[...]
        hh = h_ref[...]
        av = a_ref[...]
        uu = u_ref[...]
        rr = r_ref[...]
        ii = i_ref[...]
        lam_v = lam_ref[...]
        gl, dgl = _gelu_and_grad(g_ref[...])
        dgr_ref[...] = (dy * hh * dgl).astype(BF16)
        dh_out = dy * gl
        a_next = _shift_up(av, 1, jnp.where(ib == nb - 1, 0.0, an_ref[...]))
        acum, ls = _scan_bwd(a_next, dh_out)
        lam_adj = ls + acum * lcarry[0:1, :]
        lcarry[...] = jnp.broadcast_to(lam_adj[0:1, :], lcarry.shape)
        h_prev = _shift_down(hh, 1, jnp.where(ib == 0, 0.0, hp_ref[...]))
        da = lam_adj * h_prev
        _, a2, s, sp = _lru_gate_terms(rr, lam_v)
        d_igu = lam_adj * s
        ds = lam_adj * ii * uu
        dla = da * a2 - ds * (a2 * a2) / s
        dr = dla * (-LRU_C * sp)
        dlam_ref[...] += jnp.sum(dla * (LRU_C * rr * _sigmoid(-lam_v)), axis=0, keepdims=True)
        dpre_r = dr * rr * (1.0 - rr)
        dpre_i = d_igu * uu * ii * (1.0 - ii)
        prb = dpre_r.astype(BF16)
        pib = dpre_i.astype(BF16)
        ub = uu.astype(BF16)
        du = d_igu * ii + _dot(prb, wa_ref[...], 1, 1) + _dot(pib, wx_ref[...], 1, 1)
        dwa_ref[...] += _dot(ub, prb, 0, 0)
        dwx_ref[...] += _dot(ub, pib, 0, 0)
        dba_ref[...] += jnp.sum(dpre_r, axis=0, keepdims=True)
        dbx_ref[...] += jnp.sum(dpre_i, axis=0, keepdims=True)
        dur, dws = _conv_taps_bwd(du, dnext[...], cw_ref[...], ur_ref[...])
        dur_ref[...] = dur.astype(BF16)
        dcw_ref[...] += dws
        dcb_ref[...] += jnp.sum(du, axis=0, keepdims=True)
        dnext[...] = du[:SUBLANES]

    def rowspec(col):
        return pl.BlockSpec((tm, w), lambda i: (nb - 1 - i, col))

    row = rowspec(0)
    nxt = pl.BlockSpec((SUBLANES, w), lambda i: (jnp.minimum((nb - i) * hb, last8), 0))
    prv = pl.BlockSpec((SUBLANES, w), lambda i: (jnp.maximum((nb - 1 - i) * hb - 1, 0), 0))
    vec = pl.BlockSpec((1, w), lambda i: (0, 0))
    mat = pl.BlockSpec((w, w), lambda i: (0, 0))
    cw = pl.BlockSpec((CONV_K, w), lambda i: (0, 0))
    o = jax.ShapeDtypeStruct((t, w), BF16)
    v1 = jax.ShapeDtypeStruct((1, w), F32)
    m1 = jax.ShapeDtypeStruct((w, w), F32)
    return pl.pallas_call(
        body, name=name, grid=(nb,),
        in_specs=[rowspec(0), rowspec(cu), rowspec(cg), row, row, row, row, nxt, row, prv, cw, mat, mat, vec],
        out_specs=[row, row, cw, vec, mat, vec, mat, vec, vec],
        out_shape=[o, o, jax.ShapeDtypeStruct((CONV_K, w), F32), v1, m1, v1, m1, v1, v1],
        scratch_shapes=[pltpu.VMEM((SUBLANES, w), F32), pltpu.VMEM((SUBLANES, w), F32)],
        compiler_params=_params(1),
    )(dymix, hbuf, hbuf, u, r, ig, a, a, h, h, conv_w, wa, wx, lam)


FOX_T = 512
FOX_PREP_TM = 256


def _log_sigmoid(x):
    return jnp.minimum(x, 0.0) - jnp.log(1.0 + jnp.exp(-jnp.abs(x)))


def _fox_prep(hbuf, bf_vec, *, name):
    t = hbuf.shape[0]
    tm = _pick(t, (FOX_PREP_TM, 128))
    cs = COL_SMALL // LANES

    def body(s_ref, b_ref, eq_ref, ek_ref, carry):
        i = pl.program_id(0)

        @pl.when(i == 0)
        def _():
            carry[...] = jnp.zeros_like(carry)

        lf = _log_sigmoid(s_ref[...] + b_ref[...])
        f = _cumsum_rows(lf) + carry[0:1, :]
        carry[...] = jnp.broadcast_to(f[tm - 1:tm, :], carry.shape)
        lane = _iota((tm, LANES), 1)
        for h in range(ATT_HEADS):
            base = HEAD_DIM * (1 - h % 2)
            fh = _col(f, h)
            hi = fh.astype(BF16).astype(F32)
            mid = (fh - hi).astype(BF16).astype(F32)
            lo = fh - hi - mid
            terms = jnp.where(lane == base, hi, jnp.where(lane == base + 1, mid, jnp.where(lane == base + 2, lo, 0.0)))
            terms_k = jnp.where(lane == base + 3, -hi,
                                jnp.where(lane == base + 4, -mid, jnp.where(lane == base + 5, -lo, 0.0)))
            ones_q = ((lane >= base + 3) & (lane < base + 6)).astype(F32)
            ones_k = ((lane >= base) & (lane < base + 3)).astype(F32)
            eq_ref[:, LANES * h:LANES * (h + 1)] = (terms + ones_q).astype(BF16)
            ek_ref[:, LANES * h:LANES * (h + 1)] = (terms_k + ones_k).astype(BF16)

    ospec = pl.BlockSpec((tm, ATT_HEADS * LANES), lambda i: (i, 0))
    o = jax.ShapeDtypeStruct((t, ATT_HEADS * LANES), BF16)
    return pl.pallas_call(
        body, name=name, grid=(t // tm,),
        in_specs=[pl.BlockSpec((tm, LANES), lambda i: (i, cs)), pl.BlockSpec((1, LANES), lambda i: (0, 0))],
        out_specs=[ospec, ospec], out_shape=[o, o],
        scratch_shapes=[pltpu.VMEM((SUBLANES, LANES), F32)],
        compiler_params=_params(1),
    )(hbuf, bf_vec)


def _fox_post(dfc, hbuf, bf_vec, *, name):
    t = hbuf.shape[0]
    tm = _pick(t, (FOX_PREP_TM, 128))
    nb = t // tm
    cs = COL_SMALL // LANES

    def body(df_ref, s_ref, b_ref, o_ref, db_ref, carry):
        i = pl.program_id(0)

        @pl.when(i == 0)
        def _():
            carry[...] = jnp.zeros_like(carry)
            db_ref[...] = jnp.zeros_like(db_ref)

        dlf = _cumsum_rows(df_ref[...], reverse=True) + carry[0:1, :]
        carry[...] = jnp.broadcast_to(dlf[0:1, :], carry.shape)
        dl = dlf * _sigmoid(-(s_ref[...] + b_ref[...]))
        dl = jnp.where(_iota(dl.shape, 1) < ATT_HEADS, dl, 0.0)
        o_ref[...] = dl
        db_ref[...] += jnp.sum(dl, axis=0, keepdims=True)

    vec = pl.BlockSpec((1, LANES), lambda i: (0, 0))
    return pl.pallas_call(
        body, name=name, grid=(nb,),
        in_specs=[pl.BlockSpec((tm, LANES), lambda i: (nb - 1 - i, 0)),
                  pl.BlockSpec((tm, LANES), lambda i: (nb - 1 - i, cs)), vec],
        out_specs=[pl.BlockSpec((tm, LANES), lambda i: (nb - 1 - i, 0)), vec],
        out_shape=[jax.ShapeDtypeStruct((t, LANES), F32), jax.ShapeDtypeStruct((1, LANES), F32)],
        scratch_shapes=[pltpu.VMEM((SUBLANES, LANES), F32)],
        compiler_params=_params(1),
    )(dfc, hbuf, bf_vec)


def _fox_masks(i, j, tq):
    row = i * tq + _iota((tq, tq), 0)
    col = j * tq + _iota((tq, tq), 1)
    lane = _iota((1, LANES), 1)
    return col <= row, (lane < HEAD_DIM, lane >= HEAD_DIM)


def _hosting(body, n_in, n_out, n_scratch, comm, grid):
    na, no = len(comm.arrays), len(comm.out_shapes)

    def hosted(*refs):
        o0 = n_in + na
        s0 = o0 + n_out + no
        cargs = (refs[n_in:o0], refs[o0 + n_out:s0]) + tuple(refs[s0 + n_scratch:])
        a, b = pl.program_id(0), pl.program_id(1)

        @pl.when((a == 0) & (b == 0))
        def _():
            comm.start(*cargs)

        @pl.when((a == grid[0] - 1) & (b == 0))
        def _():
            comm.middle(*cargs)

        body(*refs[:n_in], *refs[o0:o0 + n_out], *refs[s0:s0 + n_scratch])

        @pl.when((a == grid[0] - 1) & (b == grid[1] - 1))
        def _():
            comm.finish(*cargs)

    return hosted


def _hosted_call(body, comm, grid, *, name, in_specs, out_specs, out_shape, scratch_shapes, args):
    n_out = len(out_shape)
    if comm is not None:
        cin, cout, sems = comm.specs()
        body = _hosting(body, len(in_specs), n_out, len(scratch_shapes), comm, grid)
        in_specs, out_specs = in_specs + cin, out_specs + cout
        out_shape, scratch_shapes, args = out_shape + comm.out_shapes, scratch_shapes + sems, args + list(comm.arrays)
    outs = pl.pallas_call(body, name=name, grid=grid, in_specs=in_specs, out_specs=out_specs,
                          out_shape=out_shape, scratch_shapes=scratch_shapes, compiler_params=_params(2))(*args)
    return outs[:n_out], outs[n_out:]


def _merge_comms(comms):
    comms = [c for c in comms if c is not None]
    if len(comms) <= 1:
        return comms[0] if comms else None

    def both(which):
        def run(ins, outs, ssem, rsem):
            ia = io = 0
            for c in comms:
                na, no = len(c.arrays), len(c.out_shapes)
                getattr(c, which)(ins[ia:ia + na], outs[io:io + no], ssem, rsem)
                ia, io = ia + na, io + no
        return run

    spans = sorted((c.base, c.base + c.n_own) for c in comms)
    assert all(a[1] <= b[0] for a, b in zip(spans, spans[1:])), "semaphore ranges overlap"
    return _Comm(sum((list(c.arrays) for c in comms), []), sum((list(c.out_shapes) for c in comms), []),
                 spans[-1][1], both("start"), both("finish"), middle=both("middle"))


def _fox_fwd(hbuf, eq, ek, *, comm=None, name):
    t = hbuf.shape[0]
    w = ATT_WIDTH
    tq = _pick(t, (FOX_T, 256, 128))
    nq = t // tq
    cq, ck, cv = COL_Q // w, COL_K // w, COL_V // w

    def body(q_ref, k_ref, v_ref, eq_ref, ek_ref, o_ref, lse_ref, m_s, l_s, acc_s):
        i = pl.program_id(0)
        j = pl.program_id(1)

        @pl.when(j == 0)
        def _():
            m_s[...] = jnp.full_like(m_s, NEG)
            l_s[...] = jnp.zeros_like(l_s)
            acc_s[...] = jnp.zeros_like(acc_s)

        def step(diagonal):
            _, hms = _fox_masks(i, j, tq)
            keys_first = (j * tq + _iota((tq, tq), 0)) <= (i * tq + _iota((tq, tq), 1))
            half = _iota((LANES, 1), 0)
            hrows = (half < HEAD_DIM, half >= HEAD_DIM)
            m_all = m_s[...]
            l_all = l_s[...]
            acc_old = [acc_s[LANES * pr:LANES * (pr + 1), :] for pr in range(2)]
            m_out, l_out, acc_out = [], [], []
            for pr in range(2):
                sl = slice(LANES * pr, LANES * (pr + 1))
                qp = q_ref[:, sl]
                kp = k_ref[:, sl]
                vt = v_ref[:, sl].T.astype(BF16)
                acc = acc_old[pr]
                for hh in range(2):
                    h = 2 * pr + hh
                    hsl = slice(LANES * h, LANES * (h + 1))
                    qm = jnp.where(hms[hh], (qp * (HEAD_DIM ** -0.5)).astype(BF16), eq_ref[:, hsl])
                    km = jnp.where(hms[hh], kp.astype(BF16), ek_ref[:, hsl])
                    st = _dot(km, qm, 1, 1)
                    if diagonal:
                        st = jnp.where(keys_first, st, NEG)
                    m_old = m_all[h:h + 1, :]
                    m_new = jnp.maximum(m_old, jnp.max(st, axis=0, keepdims=True))
                    alpha = jnp.exp(m_old - m_new)
                    pt = jnp.exp(st - m_new)
                    l_out.append(alpha * l_all[h:h + 1, :] + jnp.sum(pt, axis=0, keepdims=True))
                    m_out.append(m_new)
                    pv = _dot(vt, pt.astype(BF16), 1, 0)
                    acc = jnp.where(hrows[hh], alpha * acc_old[pr] + pv, acc)
                acc_out.append(acc)
            for h in range(ATT_HEADS):
                m_s[h:h + 1, :] = m_out[h]
                l_s[h:h + 1, :] = l_out[h]
            for pr in range(2):
                acc_s[LANES * pr:LANES * (pr + 1), :] = acc_out[pr]

        @pl.when(j < i)
        def _():
            step(False)

        @pl.when(j == i)
        def _():
            step(True)
            half = _iota((LANES, 1), 0)
            l_all = l_s[...]
            for pr in range(2):
                acc = acc_s[LANES * pr:LANES * (pr + 1), :]
                o_t = jnp.where(half < HEAD_DIM, acc / l_all[2 * pr:2 * pr + 1, :], acc / l_all[2 * pr + 1:2 * pr + 2, :])
                o_ref[:, LANES * pr:LANES * (pr + 1)] = o_t.T
            lse = m_s[...] + jnp.log(l_s[...])
            lse_ref[...] = jnp.where(_iota(lse.shape, 0) < ATT_HEADS, lse, 0.0)

    return _hosted_call(
        body, comm, (nq, nq), name=name,
        in_specs=[pl.BlockSpec((tq, w), lambda i, j: (i, cq)),
                  pl.BlockSpec((tq, w), lambda i, j: (jnp.minimum(j, i), ck)),
                  pl.BlockSpec((tq, w), lambda i, j: (jnp.minimum(j, i), cv)),
                  pl.BlockSpec((tq, ATT_HEADS * LANES), lambda i, j: (i, 0)),
                  pl.BlockSpec((tq, ATT_HEADS * LANES), lambda i, j: (jnp.minimum(j, i), 0))],
        out_specs=[pl.BlockSpec((tq, w), lambda i, j: (i, 0)),
                   pl.BlockSpec((SUBLANES, tq), lambda i, j: (0, i))],
        out_shape=[jax.ShapeDtypeStruct((t, w), F32), jax.ShapeDtypeStruct((SUBLANES, t), F32)],
        scratch_shapes=[pltpu.VMEM((SUBLANES, tq), F32), pltpu.VMEM((SUBLANES, tq), F32),
                        pltpu.VMEM((w, tq), F32)],
        args=[hbuf, hbuf, hbuf, eq, ek])


def _fox_delta(dymix, o, *, name):
    t, w = o.shape
    tm = _pick(t, (512, 256, 128))
    cdo = ATT_WIDTH // w

    def body(do_ref, o_ref, d_ref):
        d_ref[...] = _head_reduce(do_ref[...] * o_ref[...], 0, ATT_HEADS)

    return pl.pallas_call(
        body, name=name, grid=(t // tm,),
        in_specs=[pl.BlockSpec((tm, w), lambda i: (i, cdo)), pl.BlockSpec((tm, w), lambda i: (i, 0))],
        out_specs=pl.BlockSpec((tm, LANES), lambda i: (i, 0)),
        out_shape=jax.ShapeDtypeStruct((t, LANES), F32),
        compiler_params=_params(1),
    )(dymix, o)


def _fox_bwd(hbuf, eq, ek, dymix, lse_rows, delta_rows, *, comm=None, name):
    t = hbuf.shape[0]
    w = ATT_WIDTH
    tq = _pick(t, (FOX_T, 256, 128))
    nq = t // tq
    cq, ck, cv = COL_Q // w, COL_K // w, COL_V // w
    cdo = ATT_WIDTH // w

    def body(q_ref, k_ref, v_ref, eq_ref, ek_ref, do_ref, lse_ref, dl_ref, dk_ref, dv_ref, dfk_ref, dqt_ref, dfq_ref,
             dk_s, dv_s, dfk_s):
        j = pl.program_id(0)
        i = pl.program_id(1)

        @pl.when((i == 0) & (j == 0))
        def _():
            dqt_ref[...] = jnp.zeros_like(dqt_ref)
            dfq_ref[...] = jnp.zeros_like(dfq_ref)

        @pl.when(i == 0)
        def _():
            dk_s[...] = jnp.zeros_like(dk_s)
            dv_s[...] = jnp.zeros_like(dv_s)
            dfk_s[...] = jnp.zeros_like(dfk_s)

        def step(diagonal):
            _, hms = _fox_masks(i, j, tq)
            keys_first = (j * tq + _iota((tq, tq), 0)) <= (i * tq + _iota((tq, tq), 1))
            half = _iota((LANES, 1), 0)
            hrows = (half < HEAD_DIM, half >= HEAD_DIM)
            lse_all = lse_ref[...]
            dl_all = dl_ref[...]
            dvs, dks, dfks, dqts, dfqs = [], [], [], [], []
            for pr in range(2):
                sl = slice(LANES * pr, LANES * (pr + 1))
                qp = q_ref[:, sl]
                kp = k_ref[:, sl]
                kt = kp.T.astype(BF16)
                vpb = v_ref[:, sl].astype(BF16)
                dop = do_ref[:, sl]
                dv_p = jnp.zeros((tq, LANES), F32)
                dk_p = jnp.zeros((tq, LANES), F32)
                dqt_p = jnp.zeros((LANES, tq), F32)
                for hh in range(2):
                    h = 2 * pr + hh
                    hsl = slice(LANES * h, LANES * (h + 1))
                    qm = jnp.where(hms[hh], (qp * (HEAD_DIM ** -0.5)).astype(BF16), eq_ref[:, hsl])
                    km = jnp.where(hms[hh], kp.astype(BF16), ek_ref[:, hsl])
                    st = _dot(km, qm, 1, 1)
                    if diagonal:
                        st = jnp.where(keys_first, st, NEG)
                    pt = jnp.exp(st - lse_all[h:h + 1, :])
                    domb = jnp.where(hms[hh], dop, 0.0).astype(BF16)
                    dv_p = dv_p + _dot(pt.astype(BF16), domb, 1, 0)
                    dpt = _dot(vpb, domb, 1, 1)
                    dst = pt * (dpt - dl_all[h:h + 1, :])
                    dstb = dst.astype(BF16)
                    dk_p = dk_p + jnp.where(hms[hh], _dot(dstb, qm, 1, 0), 0.0)
                    dqt_p = dqt_p + _dot(jnp.where(hrows[hh], kt, 0.0), dstb, 1, 0)
                    part = dst[:, 0:LANES]
                    for c in range(1, tq // LANES):
                        part = part + dst[:, LANES * c:LANES * (c + 1)]
                    dfks.append(part)
                    dfqs.append(jnp.sum(dst, axis=0, keepdims=True))
                dvs.append(dv_p)
                dks.append(dk_p)
                dqts.append(dqt_p)
            dv_s[...] += jnp.concatenate(dvs, axis=1)
            dk_s[...] += jnp.concatenate(dks, axis=1)
            for h in range(ATT_HEADS):
                dfk_s[h] += dfks[h]
            cols = pl.ds(pl.multiple_of(i * tq, tq), tq)
            dqt_ref[:, cols] += jnp.concatenate(dqts, axis=0) * (HEAD_DIM ** -0.5)
            dfq_ref[:, cols] += jnp.concatenate(dfqs + [jnp.zeros((SUBLANES - ATT_HEADS, tq), F32)], axis=0)

        @pl.when(i > j)
        def _():
            step(False)

        @pl.when(i == j)
        def _():
            step(True)

        @pl.when(i == nq - 1)
        def _():
            dk_ref[...] = dk_s[...].astype(BF16)
            dv_ref[...] = dv_s[...].astype(BF16)
            lane = _iota((tq, LANES), 1)
            out = jnp.zeros((tq, LANES), F32)
            for h in range(ATT_HEADS):
                out = jnp.where(lane == h, jnp.sum(dfk_s[h], axis=1, keepdims=True), out)
            dfk_ref[...] = out

    qi = lambda j, i: jnp.maximum(i, j)
    rows = pl.BlockSpec((SUBLANES, tq), lambda j, i: (0, qi(j, i)))
    return _hosted_call(
        body, comm, (nq, nq), name=name,
        in_specs=[pl.BlockSpec((tq, w), lambda j, i: (qi(j, i), cq)),
                  pl.BlockSpec((tq, w), lambda j, i: (j, ck)),
                  pl.BlockSpec((tq, w), lambda j, i: (j, cv)),
                  pl.BlockSpec((tq, ATT_HEADS * LANES), lambda j, i: (qi(j, i), 0)),
                  pl.BlockSpec((tq, ATT_HEADS * LANES), lambda j, i: (j, 0)),
                  pl.BlockSpec((tq, w), lambda j, i: (qi(j, i), cdo)),
                  rows, rows],
        out_specs=[pl.BlockSpec((tq, w), lambda j, i: (j, 0)), pl.BlockSpec((tq, w), lambda j, i: (j, 0)),
                   pl.BlockSpec((tq, LANES), lambda j, i: (j, 0)),
                   pl.BlockSpec((w, t), lambda j, i: (0, 0)), pl.BlockSpec((SUBLANES, t), lambda j, i: (0, 0))],
        out_shape=[jax.ShapeDtypeStruct((t, w), BF16), jax.ShapeDtypeStruct((t, w), BF16),
                   jax.ShapeDtypeStruct((t, LANES), F32),
                   jax.ShapeDtypeStruct((w, t), F32), jax.ShapeDtypeStruct((SUBLANES, t), F32)],
        scratch_shapes=[pltpu.VMEM((tq, w), F32), pltpu.VMEM((tq, w), F32),
                        pltpu.VMEM((ATT_HEADS, tq, LANES), F32)],
        args=[hbuf, hbuf, hbuf, eq, ek, dymix, lse_rows, delta_rows])


GROUP_W = SSD_WIDTH // SSD_GROUPS
HEADS_PER_GROUP = SSD_HEADS // SSD_GROUPS


def _ssd_chunk_common(xr, prev8, sm, cw, cb, dtb, avec):
    c = _conv_taps(xr, prev8, cw, cb)
    sig = _sigmoid(c)
    xa = c * sig
    dt = _softplus(sm + dtb)
    a = dt * avec
    acum = _cumsum_rows(a)
    return c, sig, xa, dt, acum


def _ssd_head_cols(acum, acum_t):
    cols = [_col(acum, LANE_DT + h) for h in range(SSD_HEADS)]
    rows = [_row(acum_t, LANE_DT + h) for h in range(SSD_HEADS)]
    return cols, rows


def _expand_heads(vals, width):
    rows = vals[0].shape[0]
    colhead = _iota((rows, width), 1) // HEAD_DIM
    out = jnp.broadcast_to(vals[0], (rows, width))
    for h in range(1, len(vals)):
        out = jnp.where(colhead == h, vals[h], out)
    return out


def _ssd_decays(cols, g):
    mine = cols[HEADS_PER_GROUP * g:HEADS_PER_GROUP * (g + 1)]
    n = mine[0].shape[0]
    atots = [c[n - 1:n, :] for c in mine]
    e = _expand_heads([jnp.exp(c) for c in mine], GROUP_W)
    dec = _expand_heads([jnp.exp(t - c) for c, t in zip(mine, atots)], GROUP_W)
    etot = _expand_heads([jnp.exp(t) for t in atots], GROUP_W)
    return e, dec, etot


def _ssd_ldec(cols, rows, h, tril):
    return jnp.exp(jnp.where(tril, cols[h] - rows[h], NEG))


def _ssd_fwd(hbuf, conv_w, conv_b, dtb_vec, a_vec, d_exp, norm_g, *, name):
    t = hbuf.shape[0]
    L = SSD_CHUNK
    nc = t // L
    hb = L // SUBLANES
    cs = COL_SMALL // LANES
    cz = COL_Z // SSD_WIDTH

    def body(x_ref, xp_ref, z_ref, s_ref, cw_ref, cb_ref, dtb_ref, av_ref, dx_ref, ng_ref,
             yc_ref, y_ref, st_ref, state):
        i = pl.program_id(0)

        @pl.when(i == 0)
        def _():
            state[...] = jnp.zeros_like(state)

        prev = jnp.where(i == 0, 0.0, xp_ref[...])
        _, _, xa, dt, acum = _ssd_chunk_common(x_ref[...], prev, s_ref[...], cw_ref[...], cb_ref[...],
                                               dtb_ref[...], av_ref[...])
        cols, rows = _ssd_head_cols(acum, acum.T)
        xs = xa[:, :SSD_WIDTH]
        xdt = xs * _head_expand(dt, LANE_DT, SSD_HEADS, SSD_WIDTH)
        tril = _iota((L, L), 0) >= _iota((L, L), 1)
        lane = _iota((1, LANES), 1)
        ys = []
        for g in range(SSD_GROUPS):
            bg = xa[:, SSD_WIDTH + SSD_STATE * g:SSD_WIDTH + SSD_STATE * (g + 1)].astype(BF16)
            cg = xa[:, SSD_WIDTH + SSD_STATE * (SSD_GROUPS + g):SSD_WIDTH + SSD_STATE * (SSD_GROUPS + g + 1)].astype(BF16)
            gm = _dot(cg, bg, 1, 1)
            e, dec, etot = _ssd_decays(cols, g)
            s_in = state[g]
            st_ref[0, g] = s_in
            xg = xdt[:, GROUP_W * g:GROUP_W * (g + 1)]
            y_off = e * _dot(cg, s_in.astype(BF16), 1, 0)
            state[g] = etot * s_in + _dot(bg, (dec * xg).astype(BF16), 0, 0)
            for pr in range(2):
                xp = xg[:, LANES * pr:LANES * (pr + 1)].astype(BF16)
                outs = []
                for hh in range(2):
                    h = HEADS_PER_GROUP * g + 2 * pr + hh
                    m = gm * _ssd_ldec(cols, rows, h, tril)
                    outs.append(_dot(m.astype(BF16), xp, 1, 0))
                ys.append(jnp.where(lane < HEAD_DIM, outs[0], outs[1]) + y_off[:, LANES * pr:LANES * (pr + 1)])
        y = jnp.concatenate(ys, axis=1)
        y_ref[...] = y
        yd = y + dx_ref[...] * xs
        zz = z_ref[...]
        y2 = yd * zz * _sigmoid(zz)
        ng = ng_ref[...]
        outs = []
        for g in range(SSD_GROUPS):
            yg = y2[:, GROUP_W * g:GROUP_W * (g + 1)]
            rs = lax.rsqrt(jnp.mean(yg * yg, axis=1, keepdims=True) + RMS_EPS)
            outs.append(yg * rs * ng[:, GROUP_W * g:GROUP_W * (g + 1)])
        yc_ref[...] = jnp.concatenate(outs, axis=1)

    cdim = SSD_CONV_DIM
    vecc = pl.BlockSpec((1, cdim), lambda i: (0, 0))
    vecl = pl.BlockSpec((1, LANES), lambda i: (0, 0))
    vecw = pl.BlockSpec((1, SSD_WIDTH), lambda i: (0, 0))
    roww = pl.BlockSpec((L, SSD_WIDTH), lambda i: (i, 0))
    return pl.pallas_call(
        body, name=name, grid=(nc,),
        in_specs=[pl.BlockSpec((L, cdim), lambda i: (i, 0)),
                  pl.BlockSpec((SUBLANES, cdim), lambda i: (jnp.maximum(i * hb - 1, 0), 0)),
                  pl.BlockSpec((L, SSD_WIDTH), lambda i: (i, cz)),
                  pl.BlockSpec((L, LANES), lambda i: (i, cs)),
                  pl.BlockSpec((CONV_K, cdim), lambda i: (0, 0)), vecc, vecl, vecl, vecw, vecw],
        out_specs=[roww, roww, pl.BlockSpec((1, SSD_GROUPS, SSD_STATE, GROUP_W), lambda i: (i, 0, 0, 0))],
        out_shape=[jax.ShapeDtypeStruct((t, SSD_WIDTH), F32), jax.ShapeDtypeStruct((t, SSD_WIDTH), F32),
                   jax.ShapeDtypeStruct((nc, SSD_GROUPS, SSD_STATE, GROUP_W), F32)],
        scratch_shapes=[pltpu.VMEM((SSD_GROUPS, SSD_STATE, GROUP_W), F32)],
        compiler_params=_params(1),
    )(hbuf, hbuf, hbuf, hbuf, conv_w, conv_b, dtb_vec, a_vec, d_exp, norm_g)


def _ssd_bwd(dymix, hbuf, y_ssd, states, conv_w, conv_b, dtb_vec, a_vec, d_exp, norm_g, *, name):
    t = hbuf.shape[0]
    L = SSD_CHUNK
    nc = t // L
    hb = L // SUBLANES
    cs = COL_SMALL // LANES
    cz = COL_Z // SSD_WIDTH
    cdy = (LRU_WIDTH + ATT_WIDTH) // SSD_WIDTH
    cdim = SSD_CONV_DIM

    def body(dyc_ref, x_ref, xp_ref, z_ref, s_ref, y_ref, st_ref, cw_ref, cb_ref, dtb_ref, av_ref, dx_ref, ng_ref,
             dxr_ref, dz_ref, dsm_ref, dng_ref, dd_ref, da_ref, ddtb_ref, dcw_ref, dcb_ref,
             dstate, dnext):
        i = pl.program_id(0)
        ic = nc - 1 - i

        @pl.when(i == 0)
        def _():
            dstate[...] = jnp.zeros_like(dstate)
            dnext[...] = jnp.zeros_like(dnext)
            for ref in (dng_ref, dd_ref, da_ref, ddtb_ref, dcw_ref, dcb_ref):
                ref[...] = jnp.zeros_like(ref)

        xr = x_ref[...]
        sm = s_ref[...]
        prev = jnp.where(ic == 0, 0.0, xp_ref[...])
        avec = av_ref[...]
        c, sig, xa, dt, acum = _ssd_chunk_common(xr, prev, sm, cw_ref[...], cb_ref[...], dtb_ref[...], avec)
        cols, rows = _ssd_head_cols(acum, acum.T)
        xs = xa[:, :SSD_WIDTH]
        dtx = _head_expand(dt, LANE_DT, SSD_HEADS, SSD_WIDTH)
        xdt = xs * dtx
        tril = _iota((L, L), 0) >= _iota((L, L), 1)
        lane = _iota((1, LANES), 1)
        hmasks = (lane < HEAD_DIM, lane >= HEAD_DIM)

        y = y_ref[...]
        dexp = dx_ref[...]
        yd = y + dexp * xs
        zz = z_ref[...]
        sz = _sigmoid(zz)
        siluz = zz * sz
        y2 = yd * siluz
        ng = ng_ref[...]
        dyc = dyc_ref[...]
        dy2s, dngs = [], []
        for g in range(SSD_GROUPS):
            sl = slice(GROUP_W * g, GROUP_W * (g + 1))
            yg = y2[:, sl]
            rs = lax.rsqrt(jnp.mean(yg * yg, axis=1, keepdims=True) + RMS_EPS)
            wv = dyc[:, sl] * ng[:, sl]
            dngs.append(jnp.sum(dyc[:, sl] * yg * rs, axis=0, keepdims=True))
            dy2s.append(rs * wv - yg * (rs * rs * rs) * jnp.mean(wv * yg, axis=1, keepdims=True))
        dy2 = jnp.concatenate(dy2s, axis=1)
        dng_ref[...] += jnp.concatenate(dngs, axis=1)
        dz_ref[...] = (dy2 * yd * (sz * (1.0 + zz * (1.0 - sz)))).astype(BF16)
        dy = dy2 * siluz
        dd_ref[...] += jnp.sum(dy * xs, axis=0, keepdims=True)

        dxs, dbs, dcs = [], [], []
        datot = jnp.zeros((1, LANES), F32)
        lanes = _iota((L, LANES), 1)
        dacum = jnp.zeros((L, LANES), F32)
        for g in range(SSD_GROUPS):
            sl = slice(GROUP_W * g, GROUP_W * (g + 1))
            bg = xa[:, SSD_WIDTH + SSD_STATE * g:SSD_WIDTH + SSD_STATE * (g + 1)].astype(BF16)
            cg = xa[:, SSD_WIDTH + SSD_STATE * (SSD_GROUPS + g):SSD_WIDTH + SSD_STATE * (SSD_GROUPS + g + 1)].astype(BF16)
            gm = _dot(cg, bg, 1, 1)
            e, dec, etot = _ssd_decays(cols, g)
            s_in = st_ref[0, g]
            ds_out = dstate[g]
            dyg = dy[:, sl]
            xg = xdt[:, sl]
            edy = (e * dyg).astype(BF16)
            dstate[g] = etot * ds_out + _dot(cg, edy, 0, 0)
            dx_state = dec * _dot(bg, ds_out.astype(BF16), 1, 0)
            y_off = e * _dot(cg, s_in.astype(BF16), 1, 0)
            dacum = dacum + _head_reduce_group(dyg * y_off - xg * dx_state, g)
            dc_off = _dot(edy, s_in.astype(BF16), 1, 1)
            db_state = _dot((dec * xg).astype(BF16), ds_out.astype(BF16), 1, 1)
            dgsum = jnp.zeros((L, L), F32)
            dx_pairs = []
            for pr in range(2):
                psl = slice(LANES * pr, LANES * (pr + 1))
                xp = xg[:, psl]
                dyp = dyg[:, psl]
                dx_pair = jnp.zeros((L, LANES), F32)
                for hh in range(2):
                    h = HEADS_PER_GROUP * g + 2 * pr + hh
                    ldec = _ssd_ldec(cols, rows, h, tril)
                    dym = jnp.where(hmasks[hh], dyp, 0.0).astype(BF16)
                    xm = jnp.where(hmasks[hh], xp, 0.0).astype(BF16)
                    dx_pair = dx_pair + _dot((gm * ldec).astype(BF16), dym, 0, 0)
                    dml = _dot(dym, xm, 1, 1) * ldec
                    dgsum = dgsum + dml
                    qm = dml * gm
                    seg = jnp.sum(qm, axis=1, keepdims=True) - jnp.sum(qm.T, axis=1, keepdims=True)
                    dacum = dacum + jnp.where(lanes == LANE_DT + h, seg, 0.0)
                dx_pairs.append(dx_pair)
            dgb = dgsum.astype(BF16)
            dcs.append(_dot(dgb, bg, 1, 0) + dc_off)
            dbs.append(_dot(dgb, cg, 0, 0) + db_state)
            dxg = jnp.concatenate(dx_pairs, axis=1) + dx_state
            dxs.append(dxg)
            v = jnp.sum(dx_state * xg, axis=0, keepdims=True) + etot * jnp.sum(ds_out * s_in, axis=0, keepdims=True)
            datot = datot + _head_reduce_row(v, LANE_DT + HEADS_PER_GROUP * g, HEADS_PER_GROUP)
        dx = jnp.concatenate(dxs, axis=1)
        dacum = dacum + jnp.where(_iota((L, LANES), 0) == L - 1, datot, 0.0)
        da = _cumsum_rows(dacum, reverse=True)
        ddt = da * avec + _head_reduce(dx * xs, LANE_DT, SSD_HEADS)
        da_ref[...] += jnp.sum(da * dt, axis=0, keepdims=True)
        ddt_raw = ddt * _sigmoid(sm + dtb_ref[...])
        ddt_raw = jnp.where((lanes >= LANE_DT) & (lanes < LANE_DT + SSD_HEADS), ddt_raw, 0.0)
        dsm_ref[...] = ddt_raw
        ddtb_ref[...] += jnp.sum(ddt_raw, axis=0, keepdims=True)
        dxs_total = dx * dtx + dexp * dy
        dxa = jnp.concatenate([dxs_total] + dbs + dcs, axis=1)
        dc = dxa * (sig * (1.0 + c * (1.0 - sig)))
        dxr, dws = _conv_taps_bwd(dc, dnext[...], cw_ref[...], xr)
        dxr_ref[...] = dxr.astype(BF16)
        dcw_ref[...] += dws
        dcb_ref[...] += jnp.sum(dc, axis=0, keepdims=True)
        dnext[...] = dc[:SUBLANES]

    rev = lambda i: nc - 1 - i
    vecc = pl.BlockSpec((1, cdim), lambda i: (0, 0))
    vecl = pl.BlockSpec((1, LANES), lambda i: (0, 0))
    vecw = pl.BlockSpec((1, SSD_WIDTH), lambda i: (0, 0))
    cwspec = pl.BlockSpec((CONV_K, cdim), lambda i: (0, 0))
    roww = pl.BlockSpec((L, SSD_WIDTH), lambda i: (rev(i), 0))
    return pl.pallas_call(
        body, name=name, grid=(nc,),
        in_specs=[pl.BlockSpec((L, SSD_WIDTH), lambda i: (rev(i), cdy)),
                  pl.BlockSpec((L, cdim), lambda i: (rev(i), 0)),
                  pl.BlockSpec((SUBLANES, cdim), lambda i: (jnp.maximum(rev(i) * hb - 1, 0), 0)),
                  pl.BlockSpec((L, SSD_WIDTH), lambda i: (rev(i), cz)),
                  pl.BlockSpec((L, LANES), lambda i: (rev(i), cs)),
                  roww,
                  pl.BlockSpec((1, SSD_GROUPS, SSD_STATE, GROUP_W), lambda i: (rev(i), 0, 0, 0)),
                  cwspec, vecc, vecl, vecl, vecw, vecw],
        out_specs=[pl.BlockSpec((L, cdim), lambda i: (rev(i), 0)), roww,
                   pl.BlockSpec((L, LANES), lambda i: (rev(i), 0)),
                   vecw, vecw, vecl, vecl, cwspec, vecc],
        out_shape=[jax.ShapeDtypeStruct((t, cdim), BF16), jax.ShapeDtypeStruct((t, SSD_WIDTH), BF16),
                   jax.ShapeDtypeStruct((t, LANES), F32),
                   jax.ShapeDtypeStruct((1, SSD_WIDTH), F32), jax.ShapeDtypeStruct((1, SSD_WIDTH), F32),
                   jax.ShapeDtypeStruct((1, LANES), F32), jax.ShapeDtypeStruct((1, LANES), F32),
                   jax.ShapeDtypeStruct((CONV_K, cdim), F32), jax.ShapeDtypeStruct((1, cdim), F32)],
        scratch_shapes=[pltpu.VMEM((SSD_GROUPS, SSD_STATE, GROUP_W), F32), pltpu.VMEM((SUBLANES, cdim), F32)],
        compiler_params=_params(1),
    )(dymix, hbuf, hbuf, hbuf, hbuf, y_ssd, states, conv_w, conv_b, dtb_vec, a_vec, d_exp, norm_g)


def _head_reduce_group(x, g):
    return _head_reduce(x, LANE_DT + HEADS_PER_GROUP * g, HEADS_PER_GROUP)


def _head_reduce_row(v, lane0, nheads):
    colhead = _iota(v.shape, 1) // HEAD_DIM
    lane = _iota((1, LANES), 1)
    out = jnp.zeros((1, LANES), F32)
    for h in range(nheads):
        s = jnp.sum(jnp.where(colhead == h, v, 0.0), axis=1, keepdims=True)
        out = jnp.where(lane == lane0 + h, s, out)
    return out


def _exchange(inps, axes, *, swap=False, name):
    n = 2 ** len(axes)
    assert not swap or n == 2
    counts = [a.shape[0] for a in inps]
    out_shapes = [jax.ShapeDtypeStruct(a.shape if swap else (n,) + a.shape, a.dtype) for a in inps]
    units = sum(counts)
    na = len(inps)

    def body(*refs):
        in_refs, out_refs = refs[:na], refs[na:2 * na]
        send_sems, recv_sems, local_sems = refs[2 * na:]
        pos = {ax: lax.axis_index(ax) for ax in MESH_AXES}

        def slot_of(coord):
            s = 0
            for ax in axes:
                s = s * 2 + coord[ax]
            return s

        me = slot_of(pos)
        copies = []
        unit = 0
        for a in range(na):
            for it in range(counts[a]):
                dst = out_refs[a].at[it] if swap else out_refs[a].at[me, it]
                if not swap:
                    cp = pltpu.make_async_copy(in_refs[a].at[it], dst, local_sems.at[unit])
                    cp.start()
                    copies.append(cp)
                for delta in range(1, n):
                    coord = dict(pos)
                    for b, ax in enumerate(reversed(axes)):
                        if (delta >> b) & 1:
                            coord[ax] = 1 - pos[ax]
                    k = unit * (n - 1) + delta - 1
                    cp = pltpu.make_async_remote_copy(
                        src_ref=in_refs[a].at[it], dst_ref=dst,
                        send_sem=send_sems.at[k], recv_sem=recv_sems.at[k],
                        device_id=(coord["x"], coord["y"], coord["c"]), device_id_type=pl.DeviceIdType.MESH)
                    cp.start()
                    copies.append(cp)
                unit += 1
        for cp in copies:
            cp.wait()

    any_spec = pl.BlockSpec(memory_space=pl.ANY)
    return pl.pallas_call(
        body, name=name,
        in_specs=[any_spec] * na, out_specs=[any_spec] * na, out_shape=out_shapes,
        scratch_shapes=[pltpu.SemaphoreType.DMA((units * (n - 1),)), pltpu.SemaphoreType.DMA((units * (n - 1),)),
                        pltpu.SemaphoreType.DMA((units,))],
    )(*inps)


class _Comm:
    def __init__(self, arrays, out_shapes, n_own, start, finish, base=0, middle=None):
        self.arrays, self.out_shapes, self.start, self.finish = arrays, out_shapes, start, finish
        self.middle = middle or (lambda *refs: None)
        self.base, self.n_own, self.n_sems = base, n_own, base + n_own

    def specs(self):
        any_spec = pl.BlockSpec(memory_space=pl.ANY)
        sems = [pltpu.SemaphoreType.DMA((self.n_sems,)), pltpu.SemaphoreType.DMA((self.n_sems,))]
        return [any_spec] * len(self.arrays), [any_spec] * len(self.out_shapes), sems


def _run_comm(comm, *, name):
    na, no = len(comm.arrays), len(comm.out_shapes)

    def body(*refs):
        args = (refs[:na], refs[na:na + no]) + tuple(refs[na + no:])
        comm.start(*args)
        comm.middle(*args)
        comm.finish(*args)

    in_specs, out_specs, sems = comm.specs()
    return pl.pallas_call(body, name=name, in_specs=in_specs, out_specs=out_specs, out_shape=comm.out_shapes,
                          scratch_shapes=sems)(*comm.arrays)


def _chip_peer(x, y, d):
    px = 1 - x if d & 2 else x
    py = 1 - y if d & 1 else y
    return px, py, 2 * px + py


def _gather_layer_comm(srcs, li, base=0):
    counts = [s.shape[0] for s in srcs]
    units = [(a, it) for a in range(len(srcs)) for it in range(counts[a])]
    n_ici = 3 * len(units)
    out_shapes = [jax.ShapeDtypeStruct((N_CHIPS,) + s.shape, s.dtype) for s in srcs]

    def ici(ins, outs, ssem, rsem, u, d):
        x, y, c = (lax.axis_index(ax) for ax in MESH_AXES)
        a, it = units[u]
        px, py, _ = _chip_peer(x, y, d)
        k = base + 3 * u + d - 1
        return pltpu.make_async_remote_copy(
            src_ref=ins[a].at[it], dst_ref=outs[a].at[2 * x + y, it], send_sem=ssem.at[k], recv_sem=rsem.at[k],
            device_id=(px, py, c), device_id_type=pl.DeviceIdType.MESH)

    def arrived(ins, outs, ssem, rsem, u, d):
        x, y, c = (lax.axis_index(ax) for ax in MESH_AXES)
        a, it = units[u]
        _, _, pk = _chip_peer(x, y, d)
        k = base + 3 * u + d - 1
        return pltpu.make_async_remote_copy(
            src_ref=ins[a].at[it], dst_ref=outs[a].at[pk, it], send_sem=ssem.at[k], recv_sem=rsem.at[k],
            device_id=(x, y, c), device_id_type=pl.DeviceIdType.MESH)

    def forward(ins, outs, ssem, rsem, u, slot):
        x, y, c = (lax.axis_index(ax) for ax in MESH_AXES)
        a, it = units[u]
        pk = 2 * x + y if slot == 0 else _chip_peer(x, y, slot)[2]
        src = ins[a].at[it] if slot == 0 else outs[a].at[pk, it]
        k = base + n_ici + 4 * u + slot
        return pltpu.make_async_remote_copy(
            src_ref=src, dst_ref=outs[a].at[pk, it], send_sem=ssem.at[k], recv_sem=rsem.at[k],
            device_id=(x, y, 1 - c), device_id_type=pl.DeviceIdType.MESH)

    def start(ins, outs, ssem, rsem):
        for u in range(len(units)):
            forward(ins, outs, ssem, rsem, u, 0).start()

        @pl.when(lax.axis_index("c") == li)
        def _():
            for u in range(len(units)):
                for d in range(1, N_CHIPS):
                    ici(ins, outs, ssem, rsem, u, d).start()

    def middle(ins, outs, ssem, rsem):
        @pl.when(lax.axis_index("c") == li)
        def _():
            for u in range(len(units)):
                for d in range(1, N_CHIPS):
                    arrived(ins, outs, ssem, rsem, u, d).wait_recv()
                    forward(ins, outs, ssem, rsem, u, d).start()

    def finish(ins, outs, ssem, rsem):
        c = lax.axis_index("c")

        @pl.when(c == li)
        def _():
            for u in range(len(units)):
                for d in range(1, N_CHIPS):
                    ici(ins, outs, ssem, rsem, u, d).wait_send()
                    forward(ins, outs, ssem, rsem, u, d).wait_send()

        @pl.when(c != li)
        def _():
            for u in range(len(units)):
                for d in range(1, N_CHIPS):
                    forward(ins, outs, ssem, rsem, u, d).wait_recv()

        for u in range(len(units)):
            forward(ins, outs, ssem, rsem, u, 0).wait()

    return _Comm(srcs, out_shapes, n_ici + 4 * len(units), start, finish, base, middle)


def _reduce_chips_comm(sums, li, base=0):
    counts = [s.shape[0] for s in sums]
    units = [(a, it) for a in range(len(sums)) for it in range(counts[a])]
    out_shapes = [jax.ShapeDtypeStruct((N_CHIPS, s.shape[0]) + s.shape[2:], s.dtype) for s in sums]

    def copy(ins, outs, ssem, rsem, u, d):
        x, y, c = (lax.axis_index(ax) for ax in MESH_AXES)
        a, it = units[u]
        px, py, pk = _chip_peer(x, y, d)
        k = base + 3 * u + d - 1
        return pltpu.make_async_remote_copy(
            src_ref=ins[a].at[it, pk], dst_ref=outs[a].at[2 * x + y, it], send_sem=ssem.at[k], recv_sem=rsem.at[k],
            device_id=(px, py, c), device_id_type=pl.DeviceIdType.MESH)

    def start(ins, outs, ssem, rsem):
        @pl.when(lax.axis_index("c") == li)
        def _():
            for u in range(len(units)):
                for d in range(1, N_CHIPS):
                    copy(ins, outs, ssem, rsem, u, d).start()

    def finish(ins, outs, ssem, rsem):
        @pl.when(lax.axis_index("c") == li)
        def _():
            for u in range(len(units)):
                for d in range(1, N_CHIPS):
                    copy(ins, outs, ssem, rsem, u, d).wait()

    return _Comm(sums, out_shapes, 3 * len(units), start, finish, base)


def _sum_slots(buf, out_dtype, *, name):
    n, rows, cols = buf.shape
    tm = _pick(rows, (512, 256, 128, 8))
    if rows % tm:
        tm = rows

    def body(b_ref, o_ref):
        acc = b_ref[0].astype(F32)
        for s in range(1, n):
            acc = acc + b_ref[s].astype(F32)
        o_ref[...] = acc.astype(out_dtype)

    return pl.pallas_call(
        body, name=name, grid=(pl.cdiv(rows, tm),),
        in_specs=[pl.BlockSpec((n, tm, cols), lambda i: (0, i, 0))],
        out_specs=pl.BlockSpec((tm, cols), lambda i: (i, 0)),
        out_shape=jax.ShapeDtypeStruct((rows, cols), out_dtype),
        compiler_params=_params(1),
    )(buf)


def _sum_pair(a, b, out_dtype, *, name):
    shape = a.shape
    cols = shape[-1]
    a2, b2 = a.reshape(-1, cols), b.reshape(-1, cols)
    rows = a2.shape[0]
    tm = _pick(rows, (512, 256, 128, 8))

    def body(a_ref, b_ref, o_ref):
        o_ref[...] = (a_ref[...].astype(F32) + b_ref[...].astype(F32)).astype(out_dtype)

    spec = pl.BlockSpec((tm, cols), lambda i: (i, 0))
    return pl.pallas_call(
        body, name=name, grid=(rows // tm,), in_specs=[spec, spec], out_specs=spec,
        out_shape=jax.ShapeDtypeStruct((rows, cols), out_dtype), compiler_params=_params(1),
    )(a2, b2).reshape(shape)


def _adamw(w, g, m, v, *, name):
    shape = w.shape
    cols = shape[-1]
    rows = w.size // cols
    w2, g2, m2, v2 = (a.reshape(rows, cols) for a in (w, g, m, v))
    tm = _pick(rows, (256, 128, 64, 32, 16, 8))
    if rows % tm:
        tm = rows
    bc1 = 1.0 - ADAM_B1 ** ADAM_STEP
    bc2 = 1.0 - ADAM_B2 ** ADAM_STEP

    def body(w_ref, g_ref, m_ref, v_ref, d_ref, nm_ref, nv_ref):
        gg = g_ref[...]
        mm = ADAM_B1 * m_ref[...] + (1.0 - ADAM_B1) * gg
        vv = ADAM_B2 * v_ref[...] + (1.0 - ADAM_B2) * (gg * gg)
        m_hat = mm / bc1
        v_hat = vv / bc2
        d_ref[...] = -ADAM_LR * (m_hat / (jnp.sqrt(v_hat) + ADAM_EPS) + ADAM_WD * w_ref[...])
        nm_ref[...] = mm
        nv_ref[...] = vv

    spec = pl.BlockSpec((tm, cols), lambda i: (i, 0))
    o = jax.ShapeDtypeStruct((rows, cols), F32)
    outs = pl.pallas_call(
        body, name=name, grid=(rows // tm,), in_specs=[spec] * 4, out_specs=[spec] * 3, out_shape=[o] * 3,
        compiler_params=_params(1),
    )(w2, g2, m2, v2)
    return tuple(a.reshape(shape) for a in outs)


def _layer_fwd(li, x, xb, pb, W, up=None, att=None):
    nm = lambda s: f"l{li}_{s}"
    sv = {"x_in_b": xb}
    (g1, u1, a1), got = _mm_swiglu(xb, W["ffn1_wg"], W["ffn1_wu"], comm=up[0] if up else None, name=nm("ffn1_up"))
    if up:
        W = {**W, **up[1](got)}
    x1, x1b, xh1, rs1 = _mm_ln(a1, W["ffn1_wd"], x, W["ln1_g"], W["ln1_b"], rscale=ALPHA, mscale=0.5, name=nm("ffn1_down_ln"))
    hbuf = _mm(x1b, W["w_in_p"], name=nm("in_proj"))
    ya, lu, lr, lig, la, lh = _lru_fwd(hbuf, W["lru_conv_w"], W["lru_conv_b"], W["lru_wa_bd"], W["lru_ba"],
                                       W["lru_wx_bd"], W["lru_bx"], W["lru_lambda"], name=nm("lru_fwd"))
    eq, ek = _fox_prep(hbuf, W["fox_bf_vec"], name=nm("fox_prep"))
    (yb, lse_rows), got = _fox_fwd(hbuf, eq, ek, comm=att[0] if att else None, name=nm("fox_fwd"))
    if att:
        W = {**W, **att[1](got)}
    yc, yssd, states = _ssd_fwd(hbuf, W["ssd_conv_w"], W["ssd_conv_b"], W["ssd_dtb_vec"], W["ssd_a_vec"],
                                W["ssd_d_exp"], W["ssd_norm_g"], name=nm("ssd_fwd"))
    ymix = _assemble([ya, yb, yc], D_MODEL, name=nm("y_mix"))
    x2, x2b, xh2, rs2 = _mm_ln(ymix, W["w_out"], x1, W["ln2_g"], W["ln2_b"], rscale=ALPHA, mscale=1.0, name=nm("out_proj_ln"))
    (g2, u2, a2), _ = _mm_swiglu(x2b, W["ffn2_wg"], W["ffn2_wu"], name=nm("ffn2_up"))
    x3, x3b, xh3, rs3 = _mm_ln(a2, W["ffn2_wd"], x2, W["ln3_g"], W["ln3_b"], rscale=ALPHA, mscale=0.5, name=nm("ffn2_down_ln"))
    x4, x4b, sg, e = _mm_pe(x3, x3b, pb, W["pe_gate_w"], W["pe_gate_b"], W["pe_proj"], name=nm("ple"))
    sv.update(g1=g1, u1=u1, a1=a1, x1b=x1b, xh1=xh1, rs1=rs1, hbuf=hbuf, lu=lu, lr=lr, lig=lig, la=la, lh=lh,
              eq=eq, ek=ek, lse_rows=lse_rows, yb=yb, yssd=yssd, states=states, ymix=ymix, x2b=x2b, xh2=xh2, rs2=rs2,
              g2=g2, u2=u2, a2=a2, x3b=x3b, xh3=xh3, rs3=rs3, sg=sg, e=e, pb=pb)
    return x4, x4b, sv, W


def _layer_bwd(li, dx4, sv, W, comm=None, late=None, last=None):
    nm = lambda s: f"l{li}_{s}"
    G = {}
    dgp, de, dbg = _pe_bwd_elem(dx4, sv["sg"], sv["e"], name=nm("ple_bwd"))
    G["pe_gate_b"] = dbg
    G["pe_gate_w"] = _mm(sv["x3b"], dgp, ta=True, out_dtype=BF16, name=nm("d_pe_gate_w"))
    G["pe_proj"] = _mm(sv["pb"], de, ta=True, out_dtype=BF16, chip_cols=True, name=nm("d_pe_proj"))
    dr3, dr3b, G["ln3_g"], G["ln3_b"] = _bwd_proj([(dgp, W["pe_gate_w"])], dx4, rscale=1.0,
                                                  ln=(sv["xh3"], sv["rs3"], W["ln3_g"]), name=nm("ln3_bwd"))
    G["ffn2_wd"] = _mm(sv["a2"], dr3b, ta=True, scale=0.5, out_dtype=BF16, name=nm("d_ffn2_wd"))
    dg2, du2 = _mm_swiglu_bwd(dr3b, W["ffn2_wd"], sv["g2"], sv["u2"], scale=0.5, name=nm("ffn2_act_bwd"))
    G["ffn2_wg"] = _mm(sv["x2b"], dg2, ta=True, out_dtype=BF16, chip_cols=True, name=nm("d_ffn2_wg"))
    G["ffn2_wu"] = _mm(sv["x2b"], du2, ta=True, out_dtype=BF16, chip_cols=True, name=nm("d_ffn2_wu"))
    dr2, dr2b, G["ln2_g"], G["ln2_b"] = _bwd_proj([(dg2, W["ffn2_wg"]), (du2, W["ffn2_wu"])], dr3, rscale=ALPHA,
                                                  ln=(sv["xh2"], sv["rs2"], W["ln2_g"]), name=nm("ln2_bwd"))
    G["w_out"] = _mm(sv["ymix"], dr2b, ta=True, out_dtype=BF16, name=nm("d_w_out"))
    dymix = _mm(dr2b, W["w_out"], tb=True, name=nm("d_ymix"))
    hbuf = sv["hbuf"]
    (dur, dgr, G["lru_conv_w"], G["lru_conv_b"], G["lru_wa_bd"], G["lru_ba"], G["lru_wx_bd"], G["lru_bx"],
     G["lru_lambda"]) = _lru_bwd(dymix, hbuf, sv["lu"], sv["lr"], sv["lig"], sv["la"], sv["lh"],
                                 W["lru_conv_w"], W["lru_wa_bd"], W["lru_wx_bd"], W["lru_lambda"], name=nm("lru_bwd"))
    delta = _fox_delta(dymix, sv["yb"], name=nm("fox_delta"))
    delta_rows = jnp.pad(delta[:, :ATT_HEADS].T, ((0, SUBLANES - ATT_HEADS), (0, 0)))
    comm = _merge_comms([comm, late(G) if late else None])
    (dk, dv, dfk, dqt, dfq), comm_out = _fox_bwd(hbuf, sv["eq"], sv["ek"], dymix, sv["lse_rows"], delta_rows,
                                                 comm=comm, name=nm("fox_bwd"))
    dq = dqt.T
    dfc = jnp.pad(dfq[:ATT_HEADS].T, ((0, 0), (0, LANES - ATT_HEADS))) - dfk
    dsm_f, G["fox_bf_vec"] = _fox_post(dfc, hbuf, W["fox_bf_vec"], name=nm("fox_post"))
    (dxr, dz, dsm_dt, G["ssd_norm_g"], G["ssd_d_exp"], G["ssd_a_vec"], G["ssd_dtb_vec"], G["ssd_conv_w"],
     G["ssd_conv_b"]) = _ssd_bwd(dymix, hbuf, sv["yssd"], sv["states"], W["ssd_conv_w"], W["ssd_conv_b"],
                                 W["ssd_dtb_vec"], W["ssd_a_vec"], W["ssd_d_exp"], W["ssd_norm_g"], name=nm("ssd_bwd"))
    dh = _assemble([dxr, dz, dur, dgr, dq, dk, dv, dsm_f + dsm_dt], H_WIDTH, name=nm("d_h"))
    G["w_in_p"] = _mm(sv["x1b"], dh, ta=True, name=nm("d_w_in"))
    dr1, dr1b, G["ln1_g"], G["ln1_b"] = _bwd_proj([(dh, W["w_in_p"])], dr2, rscale=ALPHA,
                                                  ln=(sv["xh1"], sv["rs1"], W["ln1_g"]), name=nm("ln1_bwd"))
    G["ffn1_wd"] = _mm(sv["a1"], dr1b, ta=True, scale=0.5, out_dtype=BF16, name=nm("d_ffn1_wd"))
    dg1, du1 = _mm_swiglu_bwd(dr1b, W["ffn1_wd"], sv["g1"], sv["u1"], scale=0.5, name=nm("ffn1_act_bwd"))
    G["ffn1_wg"] = _mm(sv["x_in_b"], dg1, ta=True, out_dtype=BF16, chip_cols=True, name=nm("d_ffn1_wg"))
    G["ffn1_wu"] = _mm(sv["x_in_b"], du1, ta=True, out_dtype=BF16, chip_cols=True, name=nm("d_ffn1_wu"))
    dx_in, *last_out = _bwd_proj([(dg1, W["ffn1_wg"]), (du1, W["ffn1_wu"])], dr1, rscale=ALPHA, ln=None,
                                 comm=last(G) if last else None, name=nm("x_in_bwd"))
    return dx_in, G, comm_out, (last_out[0] if last_out else None)


def _block_diag(w):
    n, b, _ = w.shape
    eye = jnp.eye(n, dtype=w.dtype)
    return (eye[:, None, :, None] * w[:, :, None, :]).reshape(n * b, n * b)


def _block_diag_extract(m):
    n, b = LRU_HEADS, HEAD_DIM
    return jnp.stack([m[b * i:b * (i + 1), b * i:b * (i + 1)] for i in range(n)])


def _lane_vec(v, lane0):
    return jnp.pad(v.astype(F32), (lane0, LANES - lane0 - v.shape[0])).reshape(1, LANES)


def _w_in_permute(w):
    d = w.shape[0]
    z = lambda n: jnp.zeros((d, n), w.dtype)
    return jnp.concatenate([w[:, 1796:2820], w[:, 1284:1796], w[:, 0:512], w[:, 512:1280],
                            w[:, 1280:1284], w[:, 2820:2828], z(LANES - 12), z(H_WIDTH - COL_SMALL - LANES)], axis=1)


def _w_in_unpermute(wp):
    return jnp.concatenate([wp[:, COL_U:COL_Q], wp[:, COL_Q:COL_SMALL], wp[:, COL_SMALL:COL_SMALL + 4],
                            wp[:, COL_Z:COL_U], wp[:, COL_XBC:COL_Z], wp[:, COL_SMALL + 4:COL_SMALL + 12]], axis=1)


def _big_weights(chipw):
    W = {}
    for n, w in chipw.items():
        if n in ("ffn1_wg", "ffn1_wu", "ffn2_wg", "ffn2_wu"):
            W[n] = w
        elif n in ("ffn1_wd", "ffn2_wd", "w_out", "pe_gate_w"):
            W[n] = w.reshape(-1, D_MODEL)
        elif n == "pe_proj":
            W[n] = jnp.moveaxis(w, 0, 1).reshape(PLE_DIM, D_MODEL)
        else:
            w_in = jnp.moveaxis(w[:, :, :IN_WIDTH // N_CHIPS], 0, 1).reshape(D_MODEL, IN_WIDTH)
            W["w_in_p"] = _w_in_permute(w_in)
    return W


def _small_weights(li, small):
    g = lambda n: small[n][li]
    W = {n: g(n) for n in ("ln1_g", "ln1_b", "ln2_g", "ln2_b", "ln3_g", "ln3_b", "pe_gate_b", "lru_conv_w",
                           "ssd_conv_w")}
    for n in ("lru_conv_b", "lru_ba", "lru_bx", "lru_lambda", "ssd_conv_b", "ssd_norm_g"):
        W[n] = g(n).reshape(1, -1)
    W["lru_wa_bd"] = _block_diag(g("lru_wa")).astype(BF16)
    W["lru_wx_bd"] = _block_diag(g("lru_wx")).astype(BF16)
    W["fox_bf_vec"] = _lane_vec(g("fox_bf"), LANE_F)
    W["ssd_dtb_vec"] = _lane_vec(g("ssd_dt_bias"), LANE_DT)
    W["ssd_a_vec"] = _lane_vec(-jnp.exp(g("ssd_a_log")), LANE_DT)
    W["ssd_d_exp"] = jnp.repeat(g("ssd_d"), HEAD_DIM).reshape(1, SSD_WIDTH)
    return W


def _big_grad_by_chip(G, n):
    if n in ("ffn1_wg", "ffn1_wu", "ffn2_wg", "ffn2_wu", "pe_proj"):
        return G[n]
    if n in ("ffn1_wd", "ffn2_wd", "w_out", "pe_gate_w"):
        return G[n].reshape(N_CHIPS, -1, D_MODEL)
    share = IN_WIDTH // N_CHIPS
    d_w_in = jnp.moveaxis(_w_in_unpermute(G["w_in_p"]).reshape(D_MODEL, N_CHIPS, share), 1, 0)
    return jnp.pad(d_w_in.astype(BF16), ((0, 0), (0, 0), (0, SHARE - share)))


def _layer_small_grads(G, W):
    out = {n: G[n] for n in ("lru_conv_w", "ssd_conv_w")}
    for n in ("ln1_g", "ln1_b", "ln2_g", "ln2_b", "ln3_g", "ln3_b", "pe_gate_b", "lru_conv_b", "lru_ba", "lru_bx",
              "lru_lambda", "ssd_conv_b", "ssd_norm_g"):
        out[n] = G[n].reshape(-1)
    out["lru_wa"] = _block_diag_extract(G["lru_wa_bd"])
    out["lru_wx"] = _block_diag_extract(G["lru_wx_bd"])
    out["fox_bf"] = G["fox_bf_vec"][0, LANE_F:LANE_F + ATT_HEADS]
    out["ssd_dt_bias"] = G["ssd_dtb_vec"][0, LANE_DT:LANE_DT + SSD_HEADS]
    out["ssd_a_log"] = G["ssd_a_vec"][0, LANE_DT:LANE_DT + SSD_HEADS] * W["ssd_a_vec"][0, LANE_DT:LANE_DT + SSD_HEADS]
    out["ssd_d"] = G["ssd_d_exp"].reshape(SSD_HEADS, HEAD_DIM).sum(axis=1)
    return out


WEIGHTS = ['ln1_g', 'ln1_b', 'ffn1_wg', 'ffn1_wu', 'ffn1_wd', 'w_in', 'lru_conv_w', 'lru_conv_b', 'lru_wa', 'lru_ba',
           'lru_wx', 'lru_bx', 'lru_lambda', 'fox_bf', 'ssd_conv_w', 'ssd_conv_b', 'ssd_dt_bias', 'ssd_a_log', 'ssd_d',
           'ssd_norm_g', 'w_out', 'ln2_g', 'ln2_b', 'ffn2_wg', 'ffn2_wu', 'ffn2_wd', 'ln3_g', 'ln3_b', 'pe_proj',
           'pe_gate_w', 'pe_gate_b']
FIRST = ((("ffn1_wg",), 1), (("ffn1_wu",), 1))
NEXT = ((("w_in",), 1),
        (("ffn1_wd",), 0))
EARLY = FIRST + NEXT
LATE = ((("ffn2_wg",), 1), (("ffn2_wu",), 1),
        (("ffn2_wd",), 0),
        (("w_out",), None), (("pe_gate_w",), None),
        (("pe_proj",), None))
BIG = {n: pad for names, pad in EARLY + LATE for n in names}
SMALL_SHARDED = {'lru_conv_w': 2, 'ssd_conv_w': 2}


def _unshard(seg, axis):
    moved = jnp.moveaxis(seg, 0, axis)
    shp = list(moved.shape)
    shp[axis:axis + 2] = [shp[axis] * shp[axis + 1]]
    return moved.reshape(shp)


def _pad_axis(a, axis, size):
    if axis is None or a.shape[axis] == size:
        return a
    pads = [(0, 0)] * a.ndim
    pads[axis] = (0, size - a.shape[axis])
    return jnp.pad(a, pads)


PACK_TILE = SUBLANES * LANES


def _pack(arrs):
    rows = []
    for a in arrs:
        flat = a.astype(F32).reshape(-1)
        rows.append(jnp.pad(flat, (0, (-flat.shape[0]) % PACK_TILE)).reshape(-1, LANES))
    return jnp.concatenate(rows, axis=0)


def _unpack(packed, shapes):
    out, off = [], 0
    for s in shapes:
        n = math.prod(s)
        r = -(-n // PACK_TILE) * SUBLANES
        out.append(packed[off:off + r].reshape(-1)[:n].reshape(s))
        off += r
    return out


def kernel(x, p, ln1_g, ln1_b, ffn1_wg, ffn1_wu, ffn1_wd, w_in, lru_conv_w, lru_conv_b, lru_wa, lru_ba, lru_wx, lru_bx, lru_lambda, fox_bf, ssd_conv_w, ssd_conv_b, ssd_dt_bias, ssd_a_log, ssd_d, ssd_norm_g, w_out, ln2_g, ln2_b, ffn2_wg, ffn2_wu, ffn2_wd, ln3_g, ln3_b, pe_proj, pe_gate_w, pe_gate_b, loss_target, m_ln1_g, m_ln1_b, m_ffn1_wg, m_ffn1_wu, m_ffn1_wd, m_w_in, m_lru_conv_w, m_lru_conv_b, m_lru_wa, m_lru_ba, m_lru_wx, m_lru_bx, m_lru_lambda, m_fox_bf, m_ssd_conv_w, m_ssd_conv_b, m_ssd_dt_bias, m_ssd_a_log, m_ssd_d, m_ssd_norm_g, m_w_out, m_ln2_g, m_ln2_b, m_ffn2_wg, m_ffn2_wu, m_ffn2_wd, m_ln3_g, m_ln3_b, m_pe_proj, m_pe_gate_w, m_pe_gate_b, v_ln1_g, v_ln1_b, v_ffn1_wg, v_ffn1_wu, v_ffn1_wd, v_w_in, v_lru_conv_w, v_lru_conv_b, v_lru_wa, v_lru_ba, v_lru_wx, v_lru_bx, v_lru_lambda, v_fox_bf, v_ssd_conv_w, v_ssd_conv_b, v_ssd_dt_bias, v_ssd_a_log, v_ssd_d, v_ssd_norm_g, v_w_out, v_ln2_g, v_ln2_b, v_ffn2_wg, v_ffn2_wu, v_ffn2_wd, v_ln3_g, v_ln3_b, v_pe_proj, v_pe_gate_w, v_pe_gate_b):
    args = locals()
    w_loc = {n: args[n] for n in WEIGHTS}
    m_loc = {n: args["m_" + n] for n in WEIGHTS}
    v_loc = {n: args["v_" + n] for n in WEIGHTS}
    chip = 2 * lax.axis_index("x") + lax.axis_index("y")
    core = lax.axis_index("c")
    big = list(BIG)
    small_sh = list(SMALL_SHARDED)
    small_rep = [n for n in WEIGHTS if n not in BIG and n not in SMALL_SHARDED]

    def srcs_of(li, groups):
        return [jnp.stack([_pad_axis(w_loc[n][li].astype(BF16), pad, SHARE) for n in names]) for names, pad in groups]

    def gather_comm(li, groups, base=0):
        return _gather_layer_comm(srcs_of(li, groups), li, base)

    def chip_weights(gathered, groups):
        return _big_weights({n: g[:, j] for (names, _), g in zip(groups, gathered) for j, n in enumerate(names)})

    def pair_sums(G, groups, tag):
        gs = [jnp.stack([_big_grad_by_chip(G, n) for n in names]) for names, _ in groups]
        flat = [g.reshape((-1,) + g.shape[2:]) for g in gs]
        theirs = _exchange(flat, ("c",), swap=True, name=f"reduce_cores_{tag}")
        return [_sum_pair(f, r, BF16, name=f"reduce_cores_sum_{tag}_{gi}").reshape(g.shape)
                for gi, (f, r, g) in enumerate(zip(flat, theirs, gs))]

    def finish_reduce(quad, sums, li, groups, tag):
        quad = [lax.dynamic_update_index_in_dim(q, lax.dynamic_index_in_dim(s, chip, 1, keepdims=False), chip, 0)
                for q, s in zip(quad, sums)]
        red = [_sum_slots(q.reshape(N_CHIPS, -1, q.shape[-1]), F32,
                          name=f"reduce_chips_sum_{tag}_{gi}").reshape(q.shape[1:]) for gi, q in enumerate(quad)]
        theirs = _exchange(red, ("c",), swap=True, name=f"reduce_share_{tag}")
        out = {}
        for (names, _), r, rv in zip(groups, red, theirs):
            both = jnp.where(core == li, r, rv)
            for j, n in enumerate(names):
                out[n] = both[j]
        return out

    everything = EARLY + LATE
    first0 = _run_comm(gather_comm(0, FIRST), name="gather_w_l0")
    small = {n: w_loc[n] for n in small_rep}
    (sg,) = _exchange([_pack([w_loc[n] for n in small_sh])[None]], ("x", "y"), name="gather_conv_w")
    shards = [_unpack(sg[k, 0], [w_loc[n].shape for n in small_sh]) for k in range(N_CHIPS)]
    for j, n in enumerate(small_sh):
        small[n] = _unshard(jnp.stack([shards[k][j] for k in range(N_CHIPS)]), SMALL_SHARDED[n])

    W0 = {**_small_weights(0, small), **chip_weights(first0, FIRST)}
    late0_comm = gather_comm(0, LATE)
    early1 = []

    def in_attention0(got):
        early1.extend(got[len(LATE):])
        return chip_weights(got[:len(LATE)], LATE)

    xs = x[0]
    xs, xb, sv0, W0 = _layer_fwd(
        0, xs, xs.astype(BF16), p[0, 0].astype(BF16), W0,
        up=(gather_comm(0, NEXT), lambda got: chip_weights(got, NEXT)),
        att=(_merge_comms([late0_comm, gather_comm(1, EARLY, base=late0_comm.n_sems)]), in_attention0))
    W1 = {**_small_weights(1, small), **chip_weights(early1, EARLY)}
    xs, _, sv1, W1 = _layer_fwd(1, xs, xb, p[1, 0].astype(BF16), W1,
                                att=(gather_comm(1, LATE), lambda got: chip_weights(got, LATE)))
    dx, loss = _loss_kernel(xs, loss_target[0], name="loss")
    loss = lax.psum(loss[0, 0], MESH_AXES)
    dx, G1, _, _ = _layer_bwd(1, dx, sv1, W1)
    sums1 = pair_sums(G1, everything, "l1")
    comm1 = _reduce_chips_comm(sums1, 1)
    late_sums, early_sums = [], []

    def late0(G):
        late_sums.extend(pair_sums(G, LATE, "l0_late"))
        return _reduce_chips_comm(late_sums, 0, base=comm1.n_sems)

    def last0(G):
        early_sums.extend(pair_sums(G, EARLY, "l0"))
        return _reduce_chips_comm(early_sums, 0)

    grad_x, G0, quads, quads0 = _layer_bwd(0, dx, sv0, W0, comm=comm1, late=late0, last=last0)

    n1 = len(comm1.out_shapes)
    red = [{**finish_reduce(quads[n1:], late_sums, 0, LATE, "l0_late"),
            **finish_reduce(quads0, early_sums, 0, EARLY, "l0")},
           finish_reduce(quads[:n1], sums1, 1, everything, "l1")]
    g_red = {}
    for n in big:
        g = jnp.stack([red[li][n] for li in range(DEPTH)])
        g_red[n] = g[tuple(slice(0, s) for s in w_loc[n].shape)]
    small_l = [_layer_small_grads(G0, W0), _layer_small_grads(G1, W1)]
    g_small = {n: jnp.stack([small_l[li][n] for li in range(DEPTH)]) for n in small_l[0]}
    small_all = small_rep + small_sh
    sgp = _pack([g_small[n] for n in small_all])
    (sall,) = _exchange([sgp[None]], MESH_AXES, name="reduce_small")
    sred = _sum_slots(sall.reshape((2 ** len(MESH_AXES),) + sgp.shape), F32, name="reduce_small_sum")
    for n, g in zip(small_all, _unpack(sred, [g_small[n].shape for n in small_all])):
        if n in SMALL_SHARDED:
            width = w_loc[n].shape[-1]
            g = lax.dynamic_slice_in_dim(g, chip * width, width, axis=SMALL_SHARDED[n])
        g_red[n] = g

    delta, new_m, new_v = {}, {}, {}
    for n in big:
        delta[n], new_m[n], new_v[n] = _adamw(w_loc[n], g_red[n], m_loc[n], v_loc[n], name="adamw_" + n)
    shapes = [w_loc[n].shape for n in small_all]
    packs = [_pack([d[n] for n in small_all]) for d in (w_loc, g_red, m_loc, v_loc)]
    outs = _adamw(*packs, name="adamw_small")
    for d, o in zip((delta, new_m, new_v), outs):
        for n, a in zip(small_all, _unpack(o, shapes)):
            d[n] = a
    return (loss, grad_x[None], *[g_red[n] for n in WEIGHTS], *[delta[n] for n in WEIGHTS],
            *[new_m[n] for n in WEIGHTS], *[new_v[n] for n in WEIGHTS])
```

```python
import math

import jax
import jax.numpy as jnp
from jax import lax
from jax.experimental import pallas as pl
from jax.experimental.pallas import tpu as pltpu

F32 = jnp.float32
BF16 = jnp.bfloat16

D_MODEL = 1024
DEPTH = 2
PLE_DIM = 256
HEAD_DIM = 64
LRU_WIDTH = 256
LRU_HEADS = 4
LRU_C = 8.0
CONV_K = 4
ATT_WIDTH = 256
ATT_HEADS = 4
SSD_WIDTH = 512
SSD_HEADS = 8
SSD_GROUPS = 2
SSD_STATE = 128
SSD_CHUNK = 128
SSD_CONV_DIM = 1024
ALPHA = (2.0 * DEPTH) ** 0.25
LN_EPS = 1e-5
RMS_EPS = 1e-5
IN_WIDTH = 2828
ADAM_LR = 0.001
ADAM_B1 = 0.9
ADAM_B2 = 0.999
ADAM_EPS = 1e-08
ADAM_WD = 0.01
ADAM_STEP = 10

H_WIDTH = 3072
COL_XBC, COL_Z, COL_U, COL_G, COL_Q, COL_K, COL_V, COL_SMALL = 0, 1024, 1536, 1792, 2048, 2304, 2560, 2816
LANE_F = 0
LANE_DT = 4
LANES = 128
SUBLANES = 8
NEG = -1e30

VMEM_LIMIT = 48 * 1024 * 1024

N_CHIPS = 4
MESH_AXES = ("x", "y", "c")
SHARE = 768


def _params(n):
    return pltpu.CompilerParams(dimension_semantics=("arbitrary",) * n, vmem_limit_bytes=VMEM_LIMIT)


def _pick(n, cands):
    for c in cands:
        if n % c == 0:
            return c
    return n


def _iota(shape, dim):
    return lax.broadcasted_iota(jnp.int32, shape, dim)


def _shift_down(x, s, prev8):
    if s == 0:
        return x
    r = pltpu.roll(x, s, 0)
    pr = pltpu.roll(prev8, s, 0)
    head = jnp.where(_iota(pr.shape, 0) < s, pr, r[:SUBLANES])
    return jnp.concatenate([head, r[SUBLANES:]], axis=0)


def _shift_up(x, s, next8):
    if s == 0:
        return x
    n = x.shape[0]
    r = pltpu.roll(x, n - s, 0)
    nr = pltpu.roll(next8, SUBLANES - s, 0)
    tail = jnp.where(_iota(nr.shape, 0) >= SUBLANES - s, nr, r[n - SUBLANES:])
    return jnp.concatenate([r[:n - SUBLANES], tail], axis=0)


def _scan_fwd(a, b):
    n = a.shape[0]
    row = _iota(a.shape, 0)
    d = 1
    while d < n:
        keep = row >= d
        a_s = jnp.where(keep, pltpu.roll(a, d, 0), 1.0)
        b_s = jnp.where(keep, pltpu.roll(b, d, 0), 0.0)
        b = a * b_s + b
        a = a * a_s
        d *= 2
    return a, b


def _scan_bwd(a, b):
    n = a.shape[0]
    row = _iota(a.shape, 0)
    d = 1
    while d < n:
        keep = row < n - d
        a_s = jnp.where(keep, pltpu.roll(a, n - d, 0), 1.0)
        b_s = jnp.where(keep, pltpu.roll(b, n - d, 0), 0.0)
        b = a * b_s + b
        a = a * a_s
        d *= 2
    return a, b


def _cumsum_rows(x, reverse=False):
    n = x.shape[0]
    row = _iota(x.shape, 0)
    d = 1
    while d < n:
        if reverse:
            x = x + jnp.where(row < n - d, pltpu.roll(x, n - d, 0), 0.0)
        else:
            x = x + jnp.where(row >= d, pltpu.roll(x, d, 0), 0.0)
        d *= 2
    return x


def _col(x, lane):
    return jnp.sum(jnp.where(_iota(x.shape, 1) == lane, x, 0.0), axis=1, keepdims=True)


def _row(x, r):
    return jnp.sum(jnp.where(_iota(x.shape, 0) == r, x, 0.0), axis=0, keepdims=True)


def _sigmoid(x):
    return jax.nn.sigmoid(x)


def _softplus(x):
    return jnp.maximum(x, 0.0) + jnp.log(1.0 + jnp.exp(-jnp.abs(x)))


def _gelu_and_grad(x):
    c0 = math.sqrt(2.0 / math.pi)
    inner = c0 * (x + 0.044715 * x * x * x)
    t = jnp.tanh(inner)
    g = 0.5 * x * (1.0 + t)
    dg = 0.5 * (1.0 + t) + 0.5 * x * (1.0 - t * t) * c0 * (1.0 + 3.0 * 0.044715 * x * x)
    return g, dg


def _dot(a, b, ca, cb):
    return lax.dot_general(a, b, (((ca,), (cb,)), ((), ())), preferred_element_type=F32)


def _conv_taps(xr, prev8, w, bias):
    y = bias + w[CONV_K - 1:CONV_K, :] * xr
    for j in range(CONV_K - 1):
        y = y + w[j:j + 1, :] * _shift_down(xr, CONV_K - 1 - j, prev8)
    return y


def _conv_taps_bwd(dy, next8, w, xr):
    dx = None
    dws = []
    for j in range(CONV_K):
        sh = _shift_up(dy, CONV_K - 1 - j, next8)
        term = w[j:j + 1, :] * sh
        dx = term if dx is None else dx + term
        dws.append(jnp.sum(sh * xr, axis=0, keepdims=True))
    return dx, jnp.concatenate(dws, axis=0)


def _head_expand(v, lane0, nheads, width):
    rows = v.shape[0]
    colhead = _iota((rows, width), 1) // HEAD_DIM
    out = jnp.zeros((rows, width), F32)
    for h in range(nheads):
        out = jnp.where(colhead == h, _col(v, lane0 + h), out)
    return out


def _head_reduce(x, lane0, nheads):
    rows = x.shape[0]
    colhead = _iota(x.shape, 1) // HEAD_DIM
    lane = _iota((rows, LANES), 1)
    out = jnp.zeros((rows, LANES), F32)
    for h in range(nheads):
        s = jnp.sum(jnp.where(colhead == h, x, 0.0), axis=1, keepdims=True)
        out = jnp.where(lane == lane0 + h, s, out)
    return out


def _mm(a, b, *, ta=False, tb=False, scale=1.0, out_dtype=F32, chip_cols=False, name):
    if ta:
        kk, m = a.shape
    else:
        m, kk = a.shape
    n = b.shape[0] if tb else b.shape[1]
    tm = _pick(m, (1024, 512, 256, 128))
    tk = _pick(kk, (1024, 768, 512, 256, 128))
    nk = kk // tk
    dn_a = 0 if ta else 1
    dn_b = 1 if tb else 0
    share = n // N_CHIPS
    if chip_cols:
        tn = n
        out_spec = pl.BlockSpec((N_CHIPS, tm, share), lambda i, j, k: (0, i, 0))
        out_shape = jax.ShapeDtypeStruct((N_CHIPS, m, share), out_dtype)
    else:
        tn = _pick(n, (1024, 768, 512, 256, 128))
        out_spec = pl.BlockSpec((tm, tn), lambda i, j, k: (i, j))
        out_shape = jax.ShapeDtypeStruct((m, n), out_dtype)

    def body(a_ref, b_ref, o_ref, acc):
        k = pl.program_id(2)

        @pl.when(k == 0)
        def _():
            acc[...] = jnp.zeros_like(acc)

        acc[...] += _dot(a_ref[...].astype(BF16), b_ref[...].astype(BF16), dn_a, dn_b)

        @pl.when(k == nk - 1)
        def _():
            if chip_cols:
                for c in range(N_CHIPS):
                    o_ref[c] = (acc[:, share * c:share * (c + 1)] * scale).astype(out_dtype)
            else:
                o_ref[...] = (acc[...] * scale).astype(out_dtype)

    a_spec = pl.BlockSpec((tk, tm), lambda i, j, k: (k, i)) if ta else pl.BlockSpec((tm, tk), lambda i, j, k: (i, k))
    b_spec = pl.BlockSpec((tn, tk), lambda i, j, k: (j, k)) if tb else pl.BlockSpec((tk, tn), lambda i, j, k: (k, j))
    return pl.pallas_call(
        body, name=name, grid=(m // tm, n // tn, nk),
        in_specs=[a_spec, b_spec],
        out_specs=out_spec, out_shape=out_shape,
        scratch_shapes=[pltpu.VMEM((tm, tn), F32)],
        compiler_params=_params(3),
    )(a, b)


def _mm_swiglu(xb, wg, wu, *, comm=None, name):
    t, d = xb.shape
    share = wg.shape[2]
    n = N_CHIPS * share
    tm = _pick(t, (512, 256, 128))
    tn = _pick(share, (768, 256, 128))
    per = share // tn

    def body(x_ref, wg_ref, wu_ref, g_ref, u_ref, a_ref):
        x = x_ref[...]
        g = _dot(x, wg_ref[...], 1, 0)
        u = _dot(x, wu_ref[...], 1, 0)
        g_ref[...] = g.astype(BF16)
        u_ref[...] = u.astype(BF16)
        a_ref[...] = (g * _sigmoid(g) * u).astype(BF16)

    o = jax.ShapeDtypeStruct((t, n), BF16)
    ospec = pl.BlockSpec((tm, tn), lambda j, i: (i, j))
    return _hosted_call(
        body, comm, (n // tn, t // tm), name=name,
        in_specs=[pl.BlockSpec((tm, d), lambda j, i: (i, 0)),
                  pl.BlockSpec((None, d, tn), lambda j, i: (j // per, 0, j % per)),
                  pl.BlockSpec((None, d, tn), lambda j, i: (j // per, 0, j % per))],
        out_specs=[ospec, ospec, ospec], out_shape=[o, o, o], scratch_shapes=[], args=[xb, wg, wu])


def _mm_swiglu_bwd(dr, wd, g, u, *, scale, name):
    t, d = dr.shape
    n = wd.shape[0]
    tm = _pick(t, (512, 256, 128))
    tn = _pick(n, (768, 256, 128))

    def body(dr_ref, wd_ref, g_ref, u_ref, dg_ref, du_ref):
        da = _dot(dr_ref[...].astype(BF16), wd_ref[...], 1, 1) * scale
        gg = g_ref[...].astype(F32)
        uu = u_ref[...].astype(F32)
        sg = _sigmoid(gg)
        dg_ref[...] = (da * uu * (sg * (1.0 + gg * (1.0 - sg)))).astype(BF16)
        du_ref[...] = (da * gg * sg).astype(BF16)

    o = jax.ShapeDtypeStruct((t, n), BF16)
    ospec = pl.BlockSpec((tm, tn), lambda j, i: (i, j))
    return pl.pallas_call(
        body, name=name, grid=(n // tn, t // tm),
        in_specs=[pl.BlockSpec((tm, d), lambda j, i: (i, 0)),
                  pl.BlockSpec((tn, d), lambda j, i: (j, 0)),
                  ospec, ospec],
        out_specs=[ospec, ospec], out_shape=[o, o],
        compiler_params=_params(2),
    )(dr, wd, g, u)


def _mm_ln(a, w, resid, gain, bias, *, rscale, mscale, name):
    t, kk = a.shape
    d = w.shape[1]
    tm = _pick(t, (512, 256, 128))
    tk = kk
    nk = kk // tk

    def body(a_ref, w_ref, r_ref, g_ref, b_ref, y_ref, yb_ref, xh_ref, rs_ref, acc):
        k = pl.program_id(1)

        @pl.when(k == 0)
        def _():
            acc[...] = jnp.zeros_like(acc)

        acc[...] += _dot(a_ref[...].astype(BF16), w_ref[...], 1, 0)

        @pl.when(k == nk - 1)
        def _():
            r = rscale * r_ref[...] + mscale * acc[...]
            mu = jnp.mean(r, axis=1, keepdims=True)
            xc = r - mu
            var = jnp.mean(xc * xc, axis=1, keepdims=True)
            rstd = lax.rsqrt(var + LN_EPS)
            xh = xc * rstd
            y = xh * g_ref[...] + b_ref[...]
            y_ref[...] = y
            yb_ref[...] = y.astype(BF16)
            xh_ref[...] = xh
            rs_ref[...] = rstd

    row = pl.BlockSpec((tm, d), lambda i, k: (i, 0))
    vec = pl.BlockSpec((1, d), lambda i, k: (0, 0))
    return pl.pallas_call(
        body, name=name, grid=(t // tm, nk),
        in_specs=[pl.BlockSpec((tm, tk), lambda i, k: (i, k)),
                  pl.BlockSpec((tk, d), lambda i, k: (k, 0)), row, vec, vec],
        out_specs=[row, row, row, pl.BlockSpec((tm, 1), lambda i, k: (i, 0))],
        out_shape=[jax.ShapeDtypeStruct((t, d), F32), jax.ShapeDtypeStruct((t, d), BF16),
                   jax.ShapeDtypeStruct((t, d), F32), jax.ShapeDtypeStruct((t, 1), F32)],
        scratch_shapes=[pltpu.VMEM((tm, d), F32)],
        compiler_params=_params(2),
    )(a, w, resid, gain.reshape(1, d), bias.reshape(1, d))


def _bwd_proj(pairs, resid, *, rscale, ln, comm=None, name):
    t, kk = pairs[0][0].shape
    d = pairs[0][1].shape[-2]
    has_ln = ln is not None
    npair = len(pairs)
    tm = _pick(t, (256, 128) if has_ln and npair > 1 else (512, 256, 128))
    nt = t // tm

    def body(*refs):
        ab = refs[:2 * npair]
        r_ref = refs[2 * npair]
        pos = 2 * npair + 1
        if has_ln:
            xh_ref, rs_ref, g_ref = refs[pos:pos + 3]
            pos += 3
            o_ref, ob_ref, dg_ref, db_ref = refs[pos:pos + 4]
        else:
            o_ref = refs[pos]
        i = pl.program_id(0)
        dy = rscale * r_ref[...]
        for q in range(npair):
            a_ref, b_ref = ab[2 * q], ab[2 * q + 1]
            if len(b_ref.shape) == 3:
                share = b_ref.shape[2]
                for c in range(N_CHIPS):
                    dy = dy + _dot(a_ref[:, share * c:share * (c + 1)].astype(BF16), b_ref[c], 1, 1)
            else:
                dy = dy + _dot(a_ref[...].astype(BF16), b_ref[...], 1, 1)
        if not has_ln:
            o_ref[...] = dy
            return
        xh = xh_ref[...]
        w = dy * g_ref[...]
        m1 = jnp.mean(w, axis=1, keepdims=True)
        m2 = jnp.mean(w * xh, axis=1, keepdims=True)
        dr = rs_ref[...] * (w - m1 - xh * m2)
        o_ref[...] = dr
        ob_ref[...] = dr.astype(BF16)

        @pl.when(i == 0)
        def _():
            dg_ref[...] = jnp.zeros_like(dg_ref)
            db_ref[...] = jnp.zeros_like(db_ref)

        dg_ref[...] += jnp.sum(dy * xh, axis=0, keepdims=True)
        db_ref[...] += jnp.sum(dy, axis=0, keepdims=True)

    row = pl.BlockSpec((tm, d), lambda i, k: (i, 0))
    vec = pl.BlockSpec((1, d), lambda i, k: (0, 0))
    in_specs, args = [], []
    for a, b in pairs:
        b_spec = pl.BlockSpec(b.shape, lambda i, k, nd=b.ndim: (0,) * nd, pipeline_mode=pl.Buffered(1))
        in_specs += [pl.BlockSpec((tm, kk), lambda i, k: (i, 0)), b_spec]
        args += [a, b]
    in_specs.append(row)
    args.append(resid)
    out_specs = [row]
    out_shape = [jax.ShapeDtypeStruct((t, d), F32)]
    if has_ln:
        xh, rs, gain = ln
        in_specs += [row, pl.BlockSpec((tm, 1), lambda i, k: (i, 0)), vec]
        args += [xh, rs, gain.reshape(1, d)]
        out_specs += [row, vec, vec]
        out_shape += [jax.ShapeDtypeStruct((t, d), BF16)] + [jax.ShapeDtypeStruct((1, d), F32)] * 2
    outs, got = _hosted_call(body, comm, (nt, 1), name=name, in_specs=in_specs, out_specs=out_specs,
                             out_shape=out_shape, scratch_shapes=[], args=args)
    return tuple(outs) if comm is None else tuple(outs) + (got,)


def _mm_pe(x3, x3b, pb, wgate, bgate, wproj, *, name):
    t, d = x3.shape
    pd = pb.shape[1]
    tm = _pick(t, (512, 256, 128))
    tn = _pick(d, (512, 256, 128))

    def body(x_ref, xb_ref, p_ref, wg_ref, bg_ref, wp_ref, y_ref, yb_ref, sg_ref, e_ref):
        sg = _sigmoid(_dot(xb_ref[...], wg_ref[...], 1, 0) + bg_ref[...])
        e = _dot(p_ref[...], wp_ref[...], 1, 0)
        y = x_ref[...] + sg * e
        y_ref[...] = y
        yb_ref[...] = y.astype(BF16)
        sg_ref[...] = sg.astype(BF16)
        e_ref[...] = e.astype(BF16)

    ospec = pl.BlockSpec((tm, tn), lambda i, j: (i, j))
    ob = jax.ShapeDtypeStruct((t, d), BF16)
    return pl.pallas_call(
        body, name=name, grid=(t // tm, d // tn),
        in_specs=[ospec, pl.BlockSpec((tm, d), lambda i, j: (i, 0)), pl.BlockSpec((tm, pd), lambda i, j: (i, 0)),
                  pl.BlockSpec((d, tn), lambda i, j: (0, j)), pl.BlockSpec((1, tn), lambda i, j: (0, j)),
                  pl.BlockSpec((pd, tn), lambda i, j: (0, j))],
        out_specs=[ospec, ospec, ospec, ospec],
        out_shape=[jax.ShapeDtypeStruct((t, d), F32), ob, ob, ob],
        compiler_params=_params(2),
    )(x3, x3b, pb, wgate, bgate.reshape(1, d), wproj)


def _pe_bwd_elem(dx4, sg, e, *, name):
    t, d = dx4.shape
    tm = _pick(t, (512, 256, 128))

    def body(dx_ref, sg_ref, e_ref, dgp_ref, de_ref, db_ref):
        dx = dx_ref[...]
        s = sg_ref[...].astype(F32)
        dgp = dx * e_ref[...].astype(F32) * s * (1.0 - s)
        dgp_ref[...] = dgp.astype(BF16)
        de_ref[...] = (dx * s).astype(BF16)

        @pl.when(pl.program_id(0) == 0)
        def _():
            db_ref[...] = jnp.zeros_like(db_ref)

        db_ref[...] += jnp.sum(dgp, axis=0, keepdims=True)

    row = pl.BlockSpec((tm, d), lambda i: (i, 0))
    ob = jax.ShapeDtypeStruct((t, d), BF16)
    return pl.pallas_call(
        body, name=name, grid=(t // tm,), in_specs=[row, row, row],
        out_specs=[row, row, pl.BlockSpec((1, d), lambda i: (0, 0))],
        out_shape=[ob, ob, jax.ShapeDtypeStruct((1, d), F32)],
        compiler_params=_params(1),
    )(dx4, sg, e)


def _assemble(pieces, width, *, name):
    t = pieces[0].shape[0]
    tm = _pick(t, (512, 256, 128))
    widths = [p.shape[1] for p in pieces]

    def body(*refs):
        o_ref = refs[-1]
        off = 0
        for p_ref, w in zip(refs[:-1], widths):
            o_ref[:, off:off + w] = p_ref[...].astype(BF16)
            off += w
        if off < width:
            o_ref[:, off:] = jnp.zeros((tm, width - off), BF16)

    return pl.pallas_call(
        body, name=name, grid=(t // tm,),
        in_specs=[pl.BlockSpec((tm, w), lambda i: (i, 0)) for w in widths],
        out_specs=pl.BlockSpec((tm, width), lambda i: (i, 0)),
        out_shape=jax.ShapeDtypeStruct((t, width), BF16),
        compiler_params=_params(1),
    )(*pieces)


def _loss_kernel(y, target, *, name):
    t, d = y.shape
    tm = _pick(t, (512, 256, 128))

    def body(y_ref, t_ref, dy_ref, l_ref):
        diff = y_ref[...] - t_ref[...]
        dy_ref[...] = diff * (1.0 / d)

        @pl.when(pl.program_id(0) == 0)
        def _():
            l_ref[...] = jnp.zeros_like(l_ref)

        part = jnp.sum(jnp.mean(diff * diff, axis=1, keepdims=True), axis=0, keepdims=True)
        l_ref[...] += 0.5 * part

    row = pl.BlockSpec((tm, d), lambda i: (i, 0))
    return pl.pallas_call(
        body, name=name, grid=(t // tm,), in_specs=[row, row],
        out_specs=[row, pl.BlockSpec((1, 1), lambda i: (0, 0))],
        out_shape=[jax.ShapeDtypeStruct((t, d), F32), jax.ShapeDtypeStruct((1, 1), F32)],
        compiler_params=_params(1),
    )(y, target)


LRU_TM = 256


def _lru_gate_terms(r, lam):
    sp = _softplus(-lam)
    la = -LRU_C * r * sp
    a = jnp.exp(la)
    em = jnp.tanh(la) * (jnp.exp(2.0 * la) + 1.0)
    s = jnp.sqrt(-em)
    return la, a, s, sp


def _lru_fwd(hbuf, conv_w, conv_b, wa, ba, wx, bx, lam, *, name):
    t = hbuf.shape[0]
    w = LRU_WIDTH
    tm = _pick(t, (LRU_TM, 128))
    cu, cg = COL_U // w, COL_G // w
    hb = tm // SUBLANES

    def body(u_ref, up_ref, g_ref, cw_ref, cb_ref, wa_ref, ba_ref, wx_ref, bx_ref, lam_ref,
             y_ref, u_out, r_out, i_out, a_out, h_out, carry):
        i = pl.program_id(0)

        @pl.when(i == 0)
        def _():
            carry[...] = jnp.zeros_like(carry)

        prev = jnp.where(i == 0, 0.0, up_ref[...])
        u = _conv_taps(u_ref[...], prev, cw_ref[...], cb_ref[...])
        ub = u.astype(BF16)
        r = _sigmoid(_dot(ub, wa_ref[...], 1, 0) + ba_ref[...])
        ig = _sigmoid(_dot(ub, wx_ref[...], 1, 0) + bx_ref[...])
        _, a, s, _ = _lru_gate_terms(r, lam_ref[...])
        b = s * (ig * u)
        acum, hs = _scan_fwd(a, b)
        h = hs + acum * carry[0:1, :]
        carry[...] = jnp.broadcast_to(h[tm - 1:tm, :], carry.shape)
        gl, _ = _gelu_and_grad(g_ref[...])
        y_ref[...] = h * gl
        u_out[...] = u
        r_out[...] = r
        i_out[...] = ig
        a_out[...] = a
        h_out[...] = h

    row = pl.BlockSpec((tm, w), lambda i: (i, 0))
    vec = pl.BlockSpec((1, w), lambda i: (0, 0))
    mat = pl.BlockSpec((w, w), lambda i: (0, 0))
    o = jax.ShapeDtypeStruct((t, w), F32)
    return pl.pallas_call(
        body, name=name, grid=(t // tm,),
        in_specs=[pl.BlockSpec((tm, w), lambda i: (i, cu)),
                  pl.BlockSpec((SUBLANES, w), lambda i: (jnp.maximum(i * hb - 1, 0), cu)),
                  pl.BlockSpec((tm, w), lambda i: (i, cg)),
                  pl.BlockSpec((CONV_K, w), lambda i: (0, 0)), vec, mat, vec, mat, vec, vec],
        out_specs=[row] * 6, out_shape=[o] * 6,
        scratch_shapes=[pltpu.VMEM((SUBLANES, w), F32)],
        compiler_params=_params(1),
    )(hbuf, hbuf, hbuf, conv_w, conv_b, wa, ba, wx, bx, lam)


def _lru_bwd(dymix, hbuf, u, r, ig, a, h, conv_w, wa, wx, lam, *, name):
    t = hbuf.shape[0]
    w = LRU_WIDTH
    tm = _pick(t, (LRU_TM, 128))
    nb = t // tm
    cu, cg = COL_U // w, COL_G // w
    hb = tm // SUBLANES
    last8 = t // SUBLANES - 1

    def body(dy_ref, ur_ref, g_ref, u_ref, r_ref, i_ref, a_ref, an_ref, h_ref, hp_ref,
             cw_ref, wa_ref, wx_ref, lam_ref,
             dur_ref, dgr_ref, dcw_ref, dcb_ref, dwa_ref, dba_ref, dwx_ref, dbx_ref, dlam_ref,
             lcarry, dnext):
        i = pl.program_id(0)
        ib = nb - 1 - i

        @pl.when(i == 0)
        def _():
            lcarry[...] = jnp.zeros_like(lcarry)
            dnext[...] = jnp.zeros_like(dnext)
            for ref in (dcw_ref, dcb_ref, dwa_ref, dba_ref, dwx_ref, dbx_ref, dlam_ref):
                ref[...] = jnp.zeros_like(ref)

        dy = dy_ref[...]
        hh = h_ref[...]
        av = a_ref[...]
        uu = u_ref[...]
        rr = r_ref[...]
        ii = i_ref[...]
        lam_v = lam_ref[...]
        gl, dgl = _gelu_and_grad(g_ref[...])
        dgr_ref[...] = (dy * hh * dgl).astype(BF16)
        dh_out = dy * gl
        a_next = _shift_up(av, 1, jnp.where(ib == nb - 1, 0.0, an_ref[...]))
        acum, ls = _scan_bwd(a_next, dh_out)
        lam_adj = ls + acum * lcarry[0:1, :]
        lcarry[...] = jnp.broadcast_to(lam_adj[0:1, :], lcarry.shape)
        h_prev = _shift_down(hh, 1, jnp.where(ib == 0, 0.0, hp_ref[...]))
        da = lam_adj * h_prev
        _, a2, s, sp = _lru_gate_terms(rr, lam_v)
        d_igu = lam_adj * s
        ds = lam_adj * ii * uu
        dla = da * a2 - ds * (a2 * a2) / s
        dr = dla * (-LRU_C * sp)
        dlam_ref[...] += jnp.sum(dla * (LRU_C * rr * _sigmoid(-lam_v)), axis=0, keepdims=True)
        dpre_r = dr * rr * (1.0 - rr)
        dpre_i = d_igu * uu * ii * (1.0 - ii)
        prb = dpre_r.astype(BF16)
        pib = dpre_i.astype(BF16)
        ub = uu.astype(BF16)
        du = d_igu * ii + _dot(prb, wa_ref[...], 1, 1) + _dot(pib, wx_ref[...], 1, 1)
        dwa_ref[...] += _dot(ub, prb, 0, 0)
        dwx_ref[...] += _dot(ub, pib, 0, 0)
        dba_ref[...] += jnp.sum(dpre_r, axis=0, keepdims=True)
        dbx_ref[...] += jnp.sum(dpre_i, axis=0, keepdims=True)
        dur, dws = _conv_taps_bwd(du, dnext[...], cw_ref[...], ur_ref[...])
        dur_ref[...] = dur.astype(BF16)
        dcw_ref[...] += dws
        dcb_ref[...] += jnp.sum(du, axis=0, keepdims=True)
        dnext[...] = du[:SUBLANES]

    def rowspec(col):
        return pl.BlockSpec((tm, w), lambda i: (nb - 1 - i, col))

    row = rowspec(0)
    nxt = pl.BlockSpec((SUBLANES, w), lambda i: (jnp.minimum((nb - i) * hb, last8), 0))
    prv = pl.BlockSpec((SUBLANES, w), lambda i: (jnp.maximum((nb - 1 - i) * hb - 1, 0), 0))
    vec = pl.BlockSpec((1, w), lambda i: (0, 0))
    mat = pl.BlockSpec((w, w), lambda i: (0, 0))
    cw = pl.BlockSpec((CONV_K, w), lambda i: (0, 0))
    o = jax.ShapeDtypeStruct((t, w), BF16)
    v1 = jax.ShapeDtypeStruct((1, w), F32)
    m1 = jax.ShapeDtypeStruct((w, w), F32)
    return pl.pallas_call(
        body, name=name, grid=(nb,),
        in_specs=[rowspec(0), rowspec(cu), rowspec(cg), row, row, row, row, nxt, row, prv, cw, mat, mat, vec],
        out_specs=[row, row, cw, vec, mat, vec, mat, vec, vec],
        out_shape=[o, o, jax.ShapeDtypeStruct((CONV_K, w), F32), v1, m1, v1, m1, v1, v1],
        scratch_shapes=[pltpu.VMEM((SUBLANES, w), F32), pltpu.VMEM((SUBLANES, w), F32)],
        compiler_params=_params(1),
    )(dymix, hbuf, hbuf, u, r, ig, a, a, h, h, conv_w, wa, wx, lam)


FOX_T = 512
FOX_PREP_TM = 256


def _log_sigmoid(x):
    return jnp.minimum(x, 0.0) - jnp.log(1.0 + jnp.exp(-jnp.abs(x)))


def _fox_prep(hbuf, bf_vec, *, name):
    t = hbuf.shape[0]
    tm = _pick(t, (FOX_PREP_TM, 128))
    cs = COL_SMALL // LANES

    def body(s_ref, b_ref, eq_ref, ek_ref, carry):
        i = pl.program_id(0)

        @pl.when(i == 0)
        def _():
            carry[...] = jnp.zeros_like(carry)

        lf = _log_sigmoid(s_ref[...] + b_ref[...])
        f = _cumsum_rows(lf) + carry[0:1, :]
        carry[...] = jnp.broadcast_to(f[tm - 1:tm, :], carry.shape)
        lane = _iota((tm, LANES), 1)
        for h in range(ATT_HEADS):
            base = HEAD_DIM * (1 - h % 2)
            fh = _col(f, h)
            hi = fh.astype(BF16).astype(F32)
            mid = (fh - hi).astype(BF16).astype(F32)
            lo = fh - hi - mid
            terms = jnp.where(lane == base, hi, jnp.where(lane == base + 1, mid, jnp.where(lane == base + 2, lo, 0.0)))
            terms_k = jnp.where(lane == base + 3, -hi,
                                jnp.where(lane == base + 4, -mid, jnp.where(lane == base + 5, -lo, 0.0)))
            ones_q = ((lane >= base + 3) & (lane < base + 6)).astype(F32)
            ones_k = ((lane >= base) & (lane < base + 3)).astype(F32)
            eq_ref[:, LANES * h:LANES * (h + 1)] = (terms + ones_q).astype(BF16)
            ek_ref[:, LANES * h:LANES * (h + 1)] = (terms_k + ones_k).astype(BF16)

    ospec = pl.BlockSpec((tm, ATT_HEADS * LANES), lambda i: (i, 0))
    o = jax.ShapeDtypeStruct((t, ATT_HEADS * LANES), BF16)
    return pl.pallas_call(
        body, name=name, grid=(t // tm,),
        in_specs=[pl.BlockSpec((tm, LANES), lambda i: (i, cs)), pl.BlockSpec((1, LANES), lambda i: (0, 0))],
        out_specs=[ospec, ospec], out_shape=[o, o],
        scratch_shapes=[pltpu.VMEM((SUBLANES, LANES), F32)],
        compiler_params=_params(1),
    )(hbuf, bf_vec)


def _fox_post(dfc, hbuf, bf_vec, *, name):
    t = hbuf.shape[0]
    tm = _pick(t, (FOX_PREP_TM, 128))
    nb = t // tm
    cs = COL_SMALL // LANES

    def body(df_ref, s_ref, b_ref, o_ref, db_ref, carry):
        i = pl.program_id(0)

        @pl.when(i == 0)
        def _():
            carry[...] = jnp.zeros_like(carry)
            db_ref[...] = jnp.zeros_like(db_ref)

        dlf = _cumsum_rows(df_ref[...], reverse=True) + carry[0:1, :]
        carry[...] = jnp.broadcast_to(dlf[0:1, :], carry.shape)
        dl = dlf * _sigmoid(-(s_ref[...] + b_ref[...]))
        dl = jnp.where(_iota(dl.shape, 1) < ATT_HEADS, dl, 0.0)
        o_ref[...] = dl
        db_ref[...] += jnp.sum(dl, axis=0, keepdims=True)

    vec = pl.BlockSpec((1, LANES), lambda i: (0, 0))
    return pl.pallas_call(
        body, name=name, grid=(nb,),
        in_specs=[pl.BlockSpec((tm, LANES), lambda i: (nb - 1 - i, 0)),
                  pl.BlockSpec((tm, LANES), lambda i: (nb - 1 - i, cs)), vec],
        out_specs=[pl.BlockSpec((tm, LANES), lambda i: (nb - 1 - i, 0)), vec],
        out_shape=[jax.ShapeDtypeStruct((t, LANES), F32), jax.ShapeDtypeStruct((1, LANES), F32)],
        scratch_shapes=[pltpu.VMEM((SUBLANES, LANES), F32)],
        compiler_params=_params(1),
    )(dfc, hbuf, bf_vec)


def _fox_masks(i, j, tq):
    row = i * tq + _iota((tq, tq), 0)
    col = j * tq + _iota((tq, tq), 1)
    lane = _iota((1, LANES), 1)
    return col <= row, (lane < HEAD_DIM, lane >= HEAD_DIM)


def _hosting(body, n_in, n_out, n_scratch, comm, grid):
    na, no = len(comm.arrays), len(comm.out_shapes)

    def hosted(*refs):
        o0 = n_in + na
        s0 = o0 + n_out + no
        cargs = (refs[n_in:o0], refs[o0 + n_out:s0]) + tuple(refs[s0 + n_scratch:])
        a, b = pl.program_id(0), pl.program_id(1)

        @pl.when((a == 0) & (b == 0))
        def _():
            comm.start(*cargs)

        @pl.when((a == grid[0] - 1) & (b == 0))
        def _():
            comm.middle(*cargs)

        body(*refs[:n_in], *refs[o0:o0 + n_out], *refs[s0:s0 + n_scratch])

        @pl.when((a == grid[0] - 1) & (b == grid[1] - 1))
        def _():
            comm.finish(*cargs)

    return hosted


def _hosted_call(body, comm, grid, *, name, in_specs, out_specs, out_shape, scratch_shapes, args):
    n_out = len(out_shape)
    if comm is not None:
        cin, cout, sems = comm.specs()
        body = _hosting(body, len(in_specs), n_out, len(scratch_shapes), comm, grid)
        in_specs, out_specs = in_specs + cin, out_specs + cout
        out_shape, scratch_shapes, args = out_shape + comm.out_shapes, scratch_shapes + sems, args + list(comm.arrays)
    outs = pl.pallas_call(body, name=name, grid=grid, in_specs=in_specs, out_specs=out_specs,
                          out_shape=out_shape, scratch_shapes=scratch_shapes, compiler_params=_params(2))(*args)
    return outs[:n_out], outs[n_out:]


def _merge_comms(comms):
    comms = [c for c in comms if c is not None]
    if len(comms) <= 1:
        return comms[0] if comms else None

    def both(which):
        def run(ins, outs, ssem, rsem):
            ia = io = 0
            for c in comms:
                na, no = len(c.arrays), len(c.out_shapes)
                getattr(c, which)(ins[ia:ia + na], outs[io:io + no], ssem, rsem)
                ia, io = ia + na, io + no
        return run

    spans = sorted((c.base, c.base + c.n_own) for c in comms)
    assert all(a[1] <= b[0] for a, b in zip(spans, spans[1:])), "semaphore ranges overlap"
    return _Comm(sum((list(c.arrays) for c in comms), []), sum((list(c.out_shapes) for c in comms), []),
                 spans[-1][1], both("start"), both("finish"), middle=both("middle"))


def _fox_fwd(hbuf, eq, ek, *, comm=None, name):
    t = hbuf.shape[0]
    w = ATT_WIDTH
    tq = _pick(t, (FOX_T, 256, 128))
    nq = t // tq
    cq, ck, cv = COL_Q // w, COL_K // w, COL_V // w

    def body(q_ref, k_ref, v_ref, eq_ref, ek_ref, o_ref, lse_ref, m_s, l_s, acc_s):
        i = pl.program_id(0)
        j = pl.program_id(1)

        @pl.when(j == 0)
        def _():
            m_s[...] = jnp.full_like(m_s, NEG)
            l_s[...] = jnp.zeros_like(l_s)
            acc_s[...] = jnp.zeros_like(acc_s)

        def step(diagonal):
            _, hms = _fox_masks(i, j, tq)
            keys_first = (j * tq + _iota((tq, tq), 0)) <= (i * tq + _iota((tq, tq), 1))
            half = _iota((LANES, 1), 0)
            hrows = (half < HEAD_DIM, half >= HEAD_DIM)
            m_all = m_s[...]
            l_all = l_s[...]
            acc_old = [acc_s[LANES * pr:LANES * (pr + 1), :] for pr in range(2)]
            m_out, l_out, acc_out = [], [], []
            for pr in range(2):
                sl = slice(LANES * pr, LANES * (pr + 1))
                qp = q_ref[:, sl]
                kp = k_ref[:, sl]
                vt = v_ref[:, sl].T.astype(BF16)
                acc = acc_old[pr]
                for hh in range(2):
                    h = 2 * pr + hh
                    hsl = slice(LANES * h, LANES * (h + 1))
                    qm = jnp.where(hms[hh], (qp * (HEAD_DIM ** -0.5)).astype(BF16), eq_ref[:, hsl])
                    km = jnp.where(hms[hh], kp.astype(BF16), ek_ref[:, hsl])
                    st = _dot(km, qm, 1, 1)
                    if diagonal:
                        st = jnp.where(keys_first, st, NEG)
                    m_old = m_all[h:h + 1, :]
                    m_new = jnp.maximum(m_old, jnp.max(st, axis=0, keepdims=True))
                    alpha = jnp.exp(m_old - m_new)
                    pt = jnp.exp(st - m_new)
                    l_out.append(alpha * l_all[h:h + 1, :] + jnp.sum(pt, axis=0, keepdims=True))
                    m_out.append(m_new)
                    pv = _dot(vt, pt.astype(BF16), 1, 0)
                    acc = jnp.where(hrows[hh], alpha * acc_old[pr] + pv, acc)
                acc_out.append(acc)
            for h in range(ATT_HEADS):
                m_s[h:h + 1, :] = m_out[h]
                l_s[h:h + 1, :] = l_out[h]
            for pr in range(2):
                acc_s[LANES * pr:LANES * (pr + 1), :] = acc_out[pr]

        @pl.when(j < i)
        def _():
            step(False)

        @pl.when(j == i)
        def _():
            step(True)
            half = _iota((LANES, 1), 0)
            l_all = l_s[...]
            for pr in range(2):
                acc = acc_s[LANES * pr:LANES * (pr + 1), :]
                o_t = jnp.where(half < HEAD_DIM, acc / l_all[2 * pr:2 * pr + 1, :], acc / l_all[2 * pr + 1:2 * pr + 2, :])
                o_ref[:, LANES * pr:LANES * (pr + 1)] = o_t.T
            lse = m_s[...] + jnp.log(l_s[...])
            lse_ref[...] = jnp.where(_iota(lse.shape, 0) < ATT_HEADS, lse, 0.0)

    return _hosted_call(
        body, comm, (nq, nq), name=name,
        in_specs=[pl.BlockSpec((tq, w), lambda i, j: (i, cq)),
                  pl.BlockSpec((tq, w), lambda i, j: (jnp.minimum(j, i), ck)),
                  pl.BlockSpec((tq, w), lambda i, j: (jnp.minimum(j, i), cv)),
                  pl.BlockSpec((tq, ATT_HEADS * LANES), lambda i, j: (i, 0)),
                  pl.BlockSpec((tq, ATT_HEADS * LANES), lambda i, j: (jnp.minimum(j, i), 0))],
        out_specs=[pl.BlockSpec((tq, w), lambda i, j: (i, 0)),
                   pl.BlockSpec((SUBLANES, tq), lambda i, j: (0, i))],
        out_shape=[jax.ShapeDtypeStruct((t, w), F32), jax.ShapeDtypeStruct((SUBLANES, t), F32)],
        scratch_shapes=[pltpu.VMEM((SUBLANES, tq), F32), pltpu.VMEM((SUBLANES, tq), F32),
                        pltpu.VMEM((w, tq), F32)],
        args=[hbuf, hbuf, hbuf, eq, ek])


def _fox_delta(dymix, o, *, name):
    t, w = o.shape
    tm = _pick(t, (512, 256, 128))
    cdo = ATT_WIDTH // w

    def body(do_ref, o_ref, d_ref):
        d_ref[...] = _head_reduce(do_ref[...] * o_ref[...], 0, ATT_HEADS)

    return pl.pallas_call(
        body, name=name, grid=(t // tm,),
        in_specs=[pl.BlockSpec((tm, w), lambda i: (i, cdo)), pl.BlockSpec((tm, w), lambda i: (i, 0))],
        out_specs=pl.BlockSpec((tm, LANES), lambda i: (i, 0)),
        out_shape=jax.ShapeDtypeStruct((t, LANES), F32),
        compiler_params=_params(1),
    )(dymix, o)


def _fox_bwd(hbuf, eq, ek, dymix, lse_rows, delta_rows, *, comm=None, name):
    t = hbuf.shape[0]
    w = ATT_WIDTH
    tq = _pick(t, (FOX_T, 256, 128))
    nq = t // tq
    cq, ck, cv = COL_Q // w, COL_K // w, COL_V // w
    cdo = ATT_WIDTH // w

    def body(q_ref, k_ref, v_ref, eq_ref, ek_ref, do_ref, lse_ref, dl_ref, dk_ref, dv_ref, dfk_ref, dqt_ref, dfq_ref,
             dk_s, dv_s, dfk_s):
        j = pl.program_id(0)
        i = pl.program_id(1)

        @pl.when((i == 0) & (j == 0))
        def _():
            dqt_ref[...] = jnp.zeros_like(dqt_ref)
            dfq_ref[...] = jnp.zeros_like(dfq_ref)

        @pl.when(i == 0)
        def _():
            dk_s[...] = jnp.zeros_like(dk_s)
            dv_s[...] = jnp.zeros_like(dv_s)
            dfk_s[...] = jnp.zeros_like(dfk_s)

        def step(diagonal):
            _, hms = _fox_masks(i, j, tq)
            keys_first = (j * tq + _iota((tq, tq), 0)) <= (i * tq + _iota((tq, tq), 1))
            half = _iota((LANES, 1), 0)
            hrows = (half < HEAD_DIM, half >= HEAD_DIM)
            lse_all = lse_ref[...]
            dl_all = dl_ref[...]
            dvs, dks, dfks, dqts, dfqs = [], [], [], [], []
            for pr in range(2):
                sl = slice(LANES * pr, LANES * (pr + 1))
                qp = q_ref[:, sl]
                kp = k_ref[:, sl]
                kt = kp.T.astype(BF16)
                vpb = v_ref[:, sl].astype(BF16)
                dop = do_ref[:, sl]
                dv_p = jnp.zeros((tq, LANES), F32)
                dk_p = jnp.zeros((tq, LANES), F32)
                dqt_p = jnp.zeros((LANES, tq), F32)
                for hh in range(2):
                    h = 2 * pr + hh
                    hsl = slice(LANES * h, LANES * (h + 1))
                    qm = jnp.where(hms[hh], (qp * (HEAD_DIM ** -0.5)).astype(BF16), eq_ref[:, hsl])
                    km = jnp.where(hms[hh], kp.astype(BF16), ek_ref[:, hsl])
                    st = _dot(km, qm, 1, 1)
                    if diagonal:
                        st = jnp.where(keys_first, st, NEG)
                    pt = jnp.exp(st - lse_all[h:h + 1, :])
                    domb = jnp.where(hms[hh], dop, 0.0).astype(BF16)
                    dv_p = dv_p + _dot(pt.astype(BF16), domb, 1, 0)
                    dpt = _dot(vpb, domb, 1, 1)
                    dst = pt * (dpt - dl_all[h:h + 1, :])
                    dstb = dst.astype(BF16)
                    dk_p = dk_p + jnp.where(hms[hh], _dot(dstb, qm, 1, 0), 0.0)
                    dqt_p = dqt_p + _dot(jnp.where(hrows[hh], kt, 0.0), dstb, 1, 0)
                    part = dst[:, 0:LANES]
                    for c in range(1, tq // LANES):
                        part = part + dst[:, LANES * c:LANES * (c + 1)]
                    dfks.append(part)
                    dfqs.append(jnp.sum(dst, axis=0, keepdims=True))
                dvs.append(dv_p)
                dks.append(dk_p)
                dqts.append(dqt_p)
            dv_s[...] += jnp.concatenate(dvs, axis=1)
            dk_s[...] += jnp.concatenate(dks, axis=1)
            for h in range(ATT_HEADS):
                dfk_s[h] += dfks[h]
            cols = pl.ds(pl.multiple_of(i * tq, tq), tq)
            dqt_ref[:, cols] += jnp.concatenate(dqts, axis=0) * (HEAD_DIM ** -0.5)
            dfq_ref[:, cols] += jnp.concatenate(dfqs + [jnp.zeros((SUBLANES - ATT_HEADS, tq), F32)], axis=0)

        @pl.when(i > j)
        def _():
            step(False)

        @pl.when(i == j)
        def _():
            step(True)

        @pl.when(i == nq - 1)
        def _():
            dk_ref[...] = dk_s[...].astype(BF16)
            dv_ref[...] = dv_s[...].astype(BF16)
            lane = _iota((tq, LANES), 1)
            out = jnp.zeros((tq, LANES), F32)
            for h in range(ATT_HEADS):
                out = jnp.where(lane == h, jnp.sum(dfk_s[h], axis=1, keepdims=True), out)
            dfk_ref[...] = out

    qi = lambda j, i: jnp.maximum(i, j)
    rows = pl.BlockSpec((SUBLANES, tq), lambda j, i: (0, qi(j, i)))
    return _hosted_call(
        body, comm, (nq, nq), name=name,
        in_specs=[pl.BlockSpec((tq, w), lambda j, i: (qi(j, i), cq)),
                  pl.BlockSpec((tq, w), lambda j, i: (j, ck)),
                  pl.BlockSpec((tq, w), lambda j, i: (j, cv)),
                  pl.BlockSpec((tq, ATT_HEADS * LANES), lambda j, i: (qi(j, i), 0)),
                  pl.BlockSpec((tq, ATT_HEADS * LANES), lambda j, i: (j, 0)),
                  pl.BlockSpec((tq, w), lambda j, i: (qi(j, i), cdo)),
                  rows, rows],
        out_specs=[pl.BlockSpec((tq, w), lambda j, i: (j, 0)), pl.BlockSpec((tq, w), lambda j, i: (j, 0)),
                   pl.BlockSpec((tq, LANES), lambda j, i: (j, 0)),
                   pl.BlockSpec((w, t), lambda j, i: (0, 0)), pl.BlockSpec((SUBLANES, t), lambda j, i: (0, 0))],
        out_shape=[jax.ShapeDtypeStruct((t, w), BF16), jax.ShapeDtypeStruct((t, w), BF16),
                   jax.ShapeDtypeStruct((t, LANES), F32),
                   jax.ShapeDtypeStruct((w, t), F32), jax.ShapeDtypeStruct((SUBLANES, t), F32)],
        scratch_shapes=[pltpu.VMEM((tq, w), F32), pltpu.VMEM((tq, w), F32),
                        pltpu.VMEM((ATT_HEADS, tq, LANES), F32)],
        args=[hbuf, hbuf, hbuf, eq, ek, dymix, lse_rows, delta_rows])


GROUP_W = SSD_WIDTH // SSD_GROUPS
HEADS_PER_GROUP = SSD_HEADS // SSD_GROUPS


def _ssd_chunk_common(xr, prev8, sm, cw, cb, dtb, avec):
    c = _conv_taps(xr, prev8, cw, cb)
    sig = _sigmoid(c)
    xa = c * sig
    dt = _softplus(sm + dtb)
    a = dt * avec
    acum = _cumsum_rows(a)
    return c, sig, xa, dt, acum


def _ssd_head_cols(acum, acum_t):
    cols = [_col(acum, LANE_DT + h) for h in range(SSD_HEADS)]
    rows = [_row(acum_t, LANE_DT + h) for h in range(SSD_HEADS)]
    return cols, rows


def _expand_heads(vals, width):
    rows = vals[0].shape[0]
    colhead = _iota((rows, width), 1) // HEAD_DIM
    out = jnp.broadcast_to(vals[0], (rows, width))
    for h in range(1, len(vals)):
        out = jnp.where(colhead == h, vals[h], out)
    return out


def _ssd_decays(cols, g):
    mine = cols[HEADS_PER_GROUP * g:HEADS_PER_GROUP * (g + 1)]
    n = mine[0].shape[0]
    atots = [c[n - 1:n, :] for c in mine]
    e = _expand_heads([jnp.exp(c) for c in mine], GROUP_W)
    dec = _expand_heads([jnp.exp(t - c) for c, t in zip(mine, atots)], GROUP_W)
    etot = _expand_heads([jnp.exp(t) for t in atots], GROUP_W)
    return e, dec, etot


def _ssd_ldec(cols, rows, h, tril):
    return jnp.exp(jnp.where(tril, cols[h] - rows[h], NEG))


def _ssd_fwd(hbuf, conv_w, conv_b, dtb_vec, a_vec, d_exp, norm_g, *, name):
    t = hbuf.shape[0]
    L = SSD_CHUNK
    nc = t // L
    hb = L // SUBLANES
    cs = COL_SMALL // LANES
    cz = COL_Z // SSD_WIDTH

    def body(x_ref, xp_ref, z_ref, s_ref, cw_ref, cb_ref, dtb_ref, av_ref, dx_ref, ng_ref,
             yc_ref, y_ref, st_ref, state):
        i = pl.program_id(0)

        @pl.when(i == 0)
        def _():
            state[...] = jnp.zeros_like(state)

        prev = jnp.where(i == 0, 0.0, xp_ref[...])
        _, _, xa, dt, acum = _ssd_chunk_common(x_ref[...], prev, s_ref[...], cw_ref[...], cb_ref[...],
                                               dtb_ref[...], av_ref[...])
        cols, rows = _ssd_head_cols(acum, acum.T)
        xs = xa[:, :SSD_WIDTH]
        xdt = xs * _head_expand(dt, LANE_DT, SSD_HEADS, SSD_WIDTH)
        tril = _iota((L, L), 0) >= _iota((L, L), 1)
        lane = _iota((1, LANES), 1)
        ys = []
        for g in range(SSD_GROUPS):
            bg = xa[:, SSD_WIDTH + SSD_STATE * g:SSD_WIDTH + SSD_STATE * (g + 1)].astype(BF16)
            cg = xa[:, SSD_WIDTH + SSD_STATE * (SSD_GROUPS + g):SSD_WIDTH + SSD_STATE * (SSD_GROUPS + g + 1)].astype(BF16)
            gm = _dot(cg, bg, 1, 1)
            e, dec, etot = _ssd_decays(cols, g)
            s_in = state[g]
            st_ref[0, g] = s_in
            xg = xdt[:, GROUP_W * g:GROUP_W * (g + 1)]
            y_off = e * _dot(cg, s_in.astype(BF16), 1, 0)
            state[g] = etot * s_in + _dot(bg, (dec * xg).astype(BF16), 0, 0)
            for pr in range(2):
                xp = xg[:, LANES * pr:LANES * (pr + 1)].astype(BF16)
                outs = []
                for hh in range(2):
                    h = HEADS_PER_GROUP * g + 2 * pr + hh
                    m = gm * _ssd_ldec(cols, rows, h, tril)
                    outs.append(_dot(m.astype(BF16), xp, 1, 0))
                ys.append(jnp.where(lane < HEAD_DIM, outs[0], outs[1]) + y_off[:, LANES * pr:LANES * (pr + 1)])
        y = jnp.concatenate(ys, axis=1)
        y_ref[...] = y
        yd = y + dx_ref[...] * xs
        zz = z_ref[...]
        y2 = yd * zz * _sigmoid(zz)
        ng = ng_ref[...]
        outs = []
        for g in range(SSD_GROUPS):
            yg = y2[:, GROUP_W * g:GROUP_W * (g + 1)]
            rs = lax.rsqrt(jnp.mean(yg * yg, axis=1, keepdims=True) + RMS_EPS)
            outs.append(yg * rs * ng[:, GROUP_W * g:GROUP_W * (g + 1)])
        yc_ref[...] = jnp.concatenate(outs, axis=1)

    cdim = SSD_CONV_DIM
    vecc = pl.BlockSpec((1, cdim), lambda i: (0, 0))
    vecl = pl.BlockSpec((1, LANES), lambda i: (0, 0))
    vecw = pl.BlockSpec((1, SSD_WIDTH), lambda i: (0, 0))
    roww = pl.BlockSpec((L, SSD_WIDTH), lambda i: (i, 0))
    return pl.pallas_call(
        body, name=name, grid=(nc,),
        in_specs=[pl.BlockSpec((L, cdim), lambda i: (i, 0)),
                  pl.BlockSpec((SUBLANES, cdim), lambda i: (jnp.maximum(i * hb - 1, 0), 0)),
                  pl.BlockSpec((L, SSD_WIDTH), lambda i: (i, cz)),
                  pl.BlockSpec((L, LANES), lambda i: (i, cs)),
                  pl.BlockSpec((CONV_K, cdim), lambda i: (0, 0)), vecc, vecl, vecl, vecw, vecw],
        out_specs=[roww, roww, pl.BlockSpec((1, SSD_GROUPS, SSD_STATE, GROUP_W), lambda i: (i, 0, 0, 0))],
        out_shape=[jax.ShapeDtypeStruct((t, SSD_WIDTH), F32), jax.ShapeDtypeStruct((t, SSD_WIDTH), F32),
                   jax.ShapeDtypeStruct((nc, SSD_GROUPS, SSD_STATE, GROUP_W), F32)],
        scratch_shapes=[pltpu.VMEM((SSD_GROUPS, SSD_STATE, GROUP_W), F32)],
        compiler_params=_params(1),
    )(hbuf, hbuf, hbuf, hbuf, conv_w, conv_b, dtb_vec, a_vec, d_exp, norm_g)


def _ssd_bwd(dymix, hbuf, y_ssd, states, conv_w, conv_b, dtb_vec, a_vec, d_exp, norm_g, *, name):
    t = hbuf.shape[0]
    L = SSD_CHUNK
    nc = t // L
    hb = L // SUBLANES
    cs = COL_SMALL // LANES
    cz = COL_Z // SSD_WIDTH
    cdy = (LRU_WIDTH + ATT_WIDTH) // SSD_WIDTH
    cdim = SSD_CONV_DIM

    def body(dyc_ref, x_ref, xp_ref, z_ref, s_ref, y_ref, st_ref, cw_ref, cb_ref, dtb_ref, av_ref, dx_ref, ng_ref,
             dxr_ref, dz_ref, dsm_ref, dng_ref, dd_ref, da_ref, ddtb_ref, dcw_ref, dcb_ref,
             dstate, dnext):
        i = pl.program_id(0)
        ic = nc - 1 - i

        @pl.when(i == 0)
        def _():
            dstate[...] = jnp.zeros_like(dstate)
            dnext[...] = jnp.zeros_like(dnext)
            for ref in (dng_ref, dd_ref, da_ref, ddtb_ref, dcw_ref, dcb_ref):
                ref[...] = jnp.zeros_like(ref)

        xr = x_ref[...]
        sm = s_ref[...]
        prev = jnp.where(ic == 0, 0.0, xp_ref[...])
        avec = av_ref[...]
        c, sig, xa, dt, acum = _ssd_chunk_common(xr, prev, sm, cw_ref[...], cb_ref[...], dtb_ref[...], avec)
        cols, rows = _ssd_head_cols(acum, acum.T)
        xs = xa[:, :SSD_WIDTH]
        dtx = _head_expand(dt, LANE_DT, SSD_HEADS, SSD_WIDTH)
        xdt = xs * dtx
        tril = _iota((L, L), 0) >= _iota((L, L), 1)
        lane = _iota((1, LANES), 1)
        hmasks = (lane < HEAD_DIM, lane >= HEAD_DIM)

        y = y_ref[...]
        dexp = dx_ref[...]
        yd = y + dexp * xs
        zz = z_ref[...]
        sz = _sigmoid(zz)
        siluz = zz * sz
        y2 = yd * siluz
        ng = ng_ref[...]
        dyc = dyc_ref[...]
        dy2s, dngs = [], []
        for g in range(SSD_GROUPS):
            sl = slice(GROUP_W * g, GROUP_W * (g + 1))
            yg = y2[:, sl]
            rs = lax.rsqrt(jnp.mean(yg * yg, axis=1, keepdims=True) + RMS_EPS)
            wv = dyc[:, sl] * ng[:, sl]
            dngs.append(jnp.sum(dyc[:, sl] * yg * rs, axis=0, keepdims=True))
            dy2s.append(rs * wv - yg * (rs * rs * rs) * jnp.mean(wv * yg, axis=1, keepdims=True))
        dy2 = jnp.concatenate(dy2s, axis=1)
        dng_ref[...] += jnp.concatenate(dngs, axis=1)
        dz_ref[...] = (dy2 * yd * (sz * (1.0 + zz * (1.0 - sz)))).astype(BF16)
        dy = dy2 * siluz
        dd_ref[...] += jnp.sum(dy * xs, axis=0, keepdims=True)

        dxs, dbs, dcs = [], [], []
        datot = jnp.zeros((1, LANES), F32)
        lanes = _iota((L, LANES), 1)
        dacum = jnp.zeros((L, LANES), F32)
        for g in range(SSD_GROUPS):
            sl = slice(GROUP_W * g, GROUP_W * (g + 1))
            bg = xa[:, SSD_WIDTH + SSD_STATE * g:SSD_WIDTH + SSD_STATE * (g + 1)].astype(BF16)
            cg = xa[:, SSD_WIDTH + SSD_STATE * (SSD_GROUPS + g):SSD_WIDTH + SSD_STATE * (SSD_GROUPS + g + 1)].astype(BF16)
            gm = _dot(cg, bg, 1, 1)
            e, dec, etot = _ssd_decays(cols, g)
            s_in = st_ref[0, g]
            ds_out = dstate[g]
            dyg = dy[:, sl]
            xg = xdt[:, sl]
            edy = (e * dyg).astype(BF16)
            dstate[g] = etot * ds_out + _dot(cg, edy, 0, 0)
            dx_state = dec * _dot(bg, ds_out.astype(BF16), 1, 0)
            y_off = e * _dot(cg, s_in.astype(BF16), 1, 0)
            dacum = dacum + _head_reduce_group(dyg * y_off - xg * dx_state, g)
            dc_off = _dot(edy, s_in.astype(BF16), 1, 1)
            db_state = _dot((dec * xg).astype(BF16), ds_out.astype(BF16), 1, 1)
            dgsum = jnp.zeros((L, L), F32)
            dx_pairs = []
            for pr in range(2):
                psl = slice(LANES * pr, LANES * (pr + 1))
                xp = xg[:, psl]
                dyp = dyg[:, psl]
                dx_pair = jnp.zeros((L, LANES), F32)
                for hh in range(2):
                    h = HEADS_PER_GROUP * g + 2 * pr + hh
                    ldec = _ssd_ldec(cols, rows, h, tril)
                    dym = jnp.where(hmasks[hh], dyp, 0.0).astype(BF16)
                    xm = jnp.where(hmasks[hh], xp, 0.0).astype(BF16)
                    dx_pair = dx_pair + _dot((gm * ldec).astype(BF16), dym, 0, 0)
                    dml = _dot(dym, xm, 1, 1) * ldec
                    dgsum = dgsum + dml
                    qm = dml * gm
                    seg = jnp.sum(qm, axis=1, keepdims=True) - jnp.sum(qm.T, axis=1, keepdims=True)
                    dacum = dacum + jnp.where(lanes == LANE_DT + h, seg, 0.0)
                dx_pairs.append(dx_pair)
            dgb = dgsum.astype(BF16)
            dcs.append(_dot(dgb, bg, 1, 0) + dc_off)
            dbs.append(_dot(dgb, cg, 0, 0) + db_state)
            dxg = jnp.concatenate(dx_pairs, axis=1) + dx_state
            dxs.append(dxg)
            v = jnp.sum(dx_state * xg, axis=0, keepdims=True) + etot * jnp.sum(ds_out * s_in, axis=0, keepdims=True)
            datot = datot + _head_reduce_row(v, LANE_DT + HEADS_PER_GROUP * g, HEADS_PER_GROUP)
        dx = jnp.concatenate(dxs, axis=1)
        dacum = dacum + jnp.where(_iota((L, LANES), 0) == L - 1, datot, 0.0)
        da = _cumsum_rows(dacum, reverse=True)
        ddt = da * avec + _head_reduce(dx * xs, LANE_DT, SSD_HEADS)
        da_ref[...] += jnp.sum(da * dt, axis=0, keepdims=True)
        ddt_raw = ddt * _sigmoid(sm + dtb_ref[...])
        ddt_raw = jnp.where((lanes >= LANE_DT) & (lanes < LANE_DT + SSD_HEADS), ddt_raw, 0.0)
        dsm_ref[...] = ddt_raw
        ddtb_ref[...] += jnp.sum(ddt_raw, axis=0, keepdims=True)
        dxs_total = dx * dtx + dexp * dy
        dxa = jnp.concatenate([dxs_total] + dbs + dcs, axis=1)
        dc = dxa * (sig * (1.0 + c * (1.0 - sig)))
        dxr, dws = _conv_taps_bwd(dc, dnext[...], cw_ref[...], xr)
        dxr_ref[...] = dxr.astype(BF16)
        dcw_ref[...] += dws
        dcb_ref[...] += jnp.sum(dc, axis=0, keepdims=True)
        dnext[...] = dc[:SUBLANES]

    rev = lambda i: nc - 1 - i
    vecc = pl.BlockSpec((1, cdim), lambda i: (0, 0))
    vecl = pl.BlockSpec((1, LANES), lambda i: (0, 0))
    vecw = pl.BlockSpec((1, SSD_WIDTH), lambda i: (0, 0))
    cwspec = pl.BlockSpec((CONV_K, cdim), lambda i: (0, 0))
    roww = pl.BlockSpec((L, SSD_WIDTH), lambda i: (rev(i), 0))
    return pl.pallas_call(
        body, name=name, grid=(nc,),
        in_specs=[pl.BlockSpec((L, SSD_WIDTH), lambda i: (rev(i), cdy)),
                  pl.BlockSpec((L, cdim), lambda i: (rev(i), 0)),
                  pl.BlockSpec((SUBLANES, cdim), lambda i: (jnp.maximum(rev(i) * hb - 1, 0), 0)),
                  pl.BlockSpec((L, SSD_WIDTH), lambda i: (rev(i), cz)),
                  pl.BlockSpec((L, LANES), lambda i: (rev(i), cs)),
                  roww,
                  pl.BlockSpec((1, SSD_GROUPS, SSD_STATE, GROUP_W), lambda i: (rev(i), 0, 0, 0)),
                  cwspec, vecc, vecl, vecl, vecw, vecw],
        out_specs=[pl.BlockSpec((L, cdim), lambda i: (rev(i), 0)), roww,
                   pl.BlockSpec((L, LANES), lambda i: (rev(i), 0)),
                   vecw, vecw, vecl, vecl, cwspec, vecc],
        out_shape=[jax.ShapeDtypeStruct((t, cdim), BF16), jax.ShapeDtypeStruct((t, SSD_WIDTH), BF16),
                   jax.ShapeDtypeStruct((t, LANES), F32),
                   jax.ShapeDtypeStruct((1, SSD_WIDTH), F32), jax.ShapeDtypeStruct((1, SSD_WIDTH), F32),
                   jax.ShapeDtypeStruct((1, LANES), F32), jax.ShapeDtypeStruct((1, LANES), F32),
                   jax.ShapeDtypeStruct((CONV_K, cdim), F32), jax.ShapeDtypeStruct((1, cdim), F32)],
        scratch_shapes=[pltpu.VMEM((SSD_GROUPS, SSD_STATE, GROUP_W), F32), pltpu.VMEM((SUBLANES, cdim), F32)],
        compiler_params=_params(1),
    )(dymix, hbuf, hbuf, hbuf, hbuf, y_ssd, states, conv_w, conv_b, dtb_vec, a_vec, d_exp, norm_g)


def _head_reduce_group(x, g):
    return _head_reduce(x, LANE_DT + HEADS_PER_GROUP * g, HEADS_PER_GROUP)


def _head_reduce_row(v, lane0, nheads):
    colhead = _iota(v.shape, 1) // HEAD_DIM
    lane = _iota((1, LANES), 1)
    out = jnp.zeros((1, LANES), F32)
    for h in range(nheads):
        s = jnp.sum(jnp.where(colhead == h, v, 0.0), axis=1, keepdims=True)
        out = jnp.where(lane == lane0 + h, s, out)
    return out


def _exchange(inps, axes, *, swap=False, name):
    n = 2 ** len(axes)
    assert not swap or n == 2
    counts = [a.shape[0] for a in inps]
    out_shapes = [jax.ShapeDtypeStruct(a.shape if swap else (n,) + a.shape, a.dtype) for a in inps]
    units = sum(counts)
    na = len(inps)

    def body(*refs):
        in_refs, out_refs = refs[:na], refs[na:2 * na]
        send_sems, recv_sems, local_sems = refs[2 * na:]
        pos = {ax: lax.axis_index(ax) for ax in MESH_AXES}

        def slot_of(coord):
            s = 0
            for ax in axes:
                s = s * 2 + coord[ax]
            return s

        me = slot_of(pos)
        copies = []
        unit = 0
        for a in range(na):
            for it in range(counts[a]):
                dst = out_refs[a].at[it] if swap else out_refs[a].at[me, it]
                if not swap:
                    cp = pltpu.make_async_copy(in_refs[a].at[it], dst, local_sems.at[unit])
                    cp.start()
                    copies.append(cp)
                for delta in range(1, n):
                    coord = dict(pos)
                    for b, ax in enumerate(reversed(axes)):
                        if (delta >> b) & 1:
                            coord[ax] = 1 - pos[ax]
                    k = unit * (n - 1) + delta - 1
                    cp = pltpu.make_async_remote_copy(
                        src_ref=in_refs[a].at[it], dst_ref=dst,
                        send_sem=send_sems.at[k], recv_sem=recv_sems.at[k],
                        device_id=(coord["x"], coord["y"], coord["c"]), device_id_type=pl.DeviceIdType.MESH)
                    cp.start()
                    copies.append(cp)
                unit += 1
        for cp in copies:
            cp.wait()

    any_spec = pl.BlockSpec(memory_space=pl.ANY)
    return pl.pallas_call(
        body, name=name,
        in_specs=[any_spec] * na, out_specs=[any_spec] * na, out_shape=out_shapes,
        scratch_shapes=[pltpu.SemaphoreType.DMA((units * (n - 1),)), pltpu.SemaphoreType.DMA((units * (n - 1),)),
                        pltpu.SemaphoreType.DMA((units,))],
    )(*inps)


class _Comm:
    def __init__(self, arrays, out_shapes, n_own, start, finish, base=0, middle=None):
        self.arrays, self.out_shapes, self.start, self.finish = arrays, out_shapes, start, finish
        self.middle = middle or (lambda *refs: None)
        self.base, self.n_own, self.n_sems = base, n_own, base + n_own

    def specs(self):
        any_spec = pl.BlockSpec(memory_space=pl.ANY)
        sems = [pltpu.SemaphoreType.DMA((self.n_sems,)), pltpu.SemaphoreType.DMA((self.n_sems,))]
        return [any_spec] * len(self.arrays), [any_spec] * len(self.out_shapes), sems


def _run_comm(comm, *, name):
    na, no = len(comm.arrays), len(comm.out_shapes)

    def body(*refs):
        args = (refs[:na], refs[na:na + no]) + tuple(refs[na + no:])
        comm.start(*args)
        comm.middle(*args)
        comm.finish(*args)

    in_specs, out_specs, sems = comm.specs()
    return pl.pallas_call(body, name=name, in_specs=in_specs, out_specs=out_specs, out_shape=comm.out_shapes,
                          scratch_shapes=sems)(*comm.arrays)


def _chip_peer(x, y, d):
    px = 1 - x if d & 2 else x
    py = 1 - y if d & 1 else y
    return px, py, 2 * px + py


def _gather_layer_comm(srcs, li, base=0):
    counts = [s.shape[0] for s in srcs]
    units = [(a, it) for a in range(len(srcs)) for it in range(counts[a])]
    n_ici = 3 * len(units)
    out_shapes = [jax.ShapeDtypeStruct((N_CHIPS,) + s.shape, s.dtype) for s in srcs]

    def ici(ins, outs, ssem, rsem, u, d):
        x, y, c = (lax.axis_index(ax) for ax in MESH_AXES)
        a, it = units[u]
        px, py, _ = _chip_peer(x, y, d)
        k = base + 3 * u + d - 1
        return pltpu.make_async_remote_copy(
            src_ref=ins[a].at[it], dst_ref=outs[a].at[2 * x + y, it], send_sem=ssem.at[k], recv_sem=rsem.at[k],
            device_id=(px, py, c), device_id_type=pl.DeviceIdType.MESH)

    def arrived(ins, outs, ssem, rsem, u, d):
        x, y, c = (lax.axis_index(ax) for ax in MESH_AXES)
        a, it = units[u]
        _, _, pk = _chip_peer(x, y, d)
        k = base + 3 * u + d - 1
        return pltpu.make_async_remote_copy(
            src_ref=ins[a].at[it], dst_ref=outs[a].at[pk, it], send_sem=ssem.at[k], recv_sem=rsem.at[k],
            device_id=(x, y, c), device_id_type=pl.DeviceIdType.MESH)

    def forward(ins, outs, ssem, rsem, u, slot):
        x, y, c = (lax.axis_index(ax) for ax in MESH_AXES)
        a, it = units[u]
        pk = 2 * x + y if slot == 0 else _chip_peer(x, y, slot)[2]
        src = ins[a].at[it] if slot == 0 else outs[a].at[pk, it]
        k = base + n_ici + 4 * u + slot
        return pltpu.make_async_remote_copy(
            src_ref=src, dst_ref=outs[a].at[pk, it], send_sem=ssem.at[k], recv_sem=rsem.at[k],
            device_id=(x, y, 1 - c), device_id_type=pl.DeviceIdType.MESH)

    def start(ins, outs, ssem, rsem):
        for u in range(len(units)):
            forward(ins, outs, ssem, rsem, u, 0).start()

        @pl.when(lax.axis_index("c") == li)
        def _():
            for u in range(len(units)):
                for d in range(1, N_CHIPS):
                    ici(ins, outs, ssem, rsem, u, d).start()

    def middle(ins, outs, ssem, rsem):
        @pl.when(lax.axis_index("c") == li)
        def _():
            for u in range(len(units)):
                for d in range(1, N_CHIPS):
                    arrived(ins, outs, ssem, rsem, u, d).wait_recv()
                    forward(ins, outs, ssem, rsem, u, d).start()

    def finish(ins, outs, ssem, rsem):
        c = lax.axis_index("c")

        @pl.when(c == li)
        def _():
            for u in range(len(units)):
                for d in range(1, N_CHIPS):
                    ici(ins, outs, ssem, rsem, u, d).wait_send()
                    forward(ins, outs, ssem, rsem, u, d).wait_send()

        @pl.when(c != li)
        def _():
            for u in range(len(units)):
                for d in range(1, N_CHIPS):
                    forward(ins, outs, ssem, rsem, u, d).wait_recv()

        for u in range(len(units)):
            forward(ins, outs, ssem, rsem, u, 0).wait()

    return _Comm(srcs, out_shapes, n_ici + 4 * len(units), start, finish, base, middle)


def _reduce_chips_comm(sums, li, base=0):
    counts = [s.shape[0] for s in sums]
    units = [(a, it) for a in range(len(sums)) for it in range(counts[a])]
    out_shapes = [jax.ShapeDtypeStruct((N_CHIPS, s.shape[0]) + s.shape[2:], s.dtype) for s in sums]

    def copy(ins, outs, ssem, rsem, u, d):
        x, y, c = (lax.axis_index(ax) for ax in MESH_AXES)
        a, it = units[u]
        px, py, pk = _chip_peer(x, y, d)
        k = base + 3 * u + d - 1
        return pltpu.make_async_remote_copy(
            src_ref=ins[a].at[it, pk], dst_ref=outs[a].at[2 * x + y, it], send_sem=ssem.at[k], recv_sem=rsem.at[k],
            device_id=(px, py, c), device_id_type=pl.DeviceIdType.MESH)

    def start(ins, outs, ssem, rsem):
        @pl.when(lax.axis_index("c") == li)
        def _():
            for u in range(len(units)):
                for d in range(1, N_CHIPS):
                    copy(ins, outs, ssem, rsem, u, d).start()

    def finish(ins, outs, ssem, rsem):
        @pl.when(lax.axis_index("c") == li)
        def _():
            for u in range(len(units)):
                for d in range(1, N_CHIPS):
                    copy(ins, outs, ssem, rsem, u, d).wait()

    return _Comm(sums, out_shapes, 3 * len(units), start, finish, base)


def _sum_slots(buf, out_dtype, *, name):
    n, rows, cols = buf.shape
    tm = _pick(rows, (512, 256, 128, 8))
    if rows % tm:
        tm = rows

    def body(b_ref, o_ref):
        acc = b_ref[0].astype(F32)
        for s in range(1, n):
            acc = acc + b_ref[s].astype(F32)
        o_ref[...] = acc.astype(out_dtype)

    return pl.pallas_call(
        body, name=name, grid=(pl.cdiv(rows, tm),),
        in_specs=[pl.BlockSpec((n, tm, cols), lambda i: (0, i, 0))],
        out_specs=pl.BlockSpec((tm, cols), lambda i: (i, 0)),
        out_shape=jax.ShapeDtypeStruct((rows, cols), out_dtype),
        compiler_params=_params(1),
    )(buf)


def _sum_pair(a, b, out_dtype, *, name):
    shape = a.shape
    cols = shape[-1]
    a2, b2 = a.reshape(-1, cols), b.reshape(-1, cols)
    rows = a2.shape[0]
    tm = _pick(rows, (512, 256, 128, 8))

    def body(a_ref, b_ref, o_ref):
        o_ref[...] = (a_ref[...].astype(F32) + b_ref[...].astype(F32)).astype(out_dtype)

    spec = pl.BlockSpec((tm, cols), lambda i: (i, 0))
    return pl.pallas_call(
        body, name=name, grid=(rows // tm,), in_specs=[spec, spec], out_specs=spec,
        out_shape=jax.ShapeDtypeStruct((rows, cols), out_dtype), compiler_params=_params(1),
    )(a2, b2).reshape(shape)


def _adamw(w, g, m, v, *, name):
    shape = w.shape
    cols = shape[-1]
    rows = w.size // cols
    w2, g2, m2, v2 = (a.reshape(rows, cols) for a in (w, g, m, v))
    tm = _pick(rows, (256, 128, 64, 32, 16, 8))
    if rows % tm:
        tm = rows
    bc1 = 1.0 - ADAM_B1 ** ADAM_STEP
    bc2 = 1.0 - ADAM_B2 ** ADAM_STEP

    def body(w_ref, g_ref, m_ref, v_ref, d_ref, nm_ref, nv_ref):
        gg = g_ref[...]
        mm = ADAM_B1 * m_ref[...] + (1.0 - ADAM_B1) * gg
        vv = ADAM_B2 * v_ref[...] + (1.0 - ADAM_B2) * (gg * gg)
        m_hat = mm / bc1
        v_hat = vv / bc2
        d_ref[...] = -ADAM_LR * (m_hat / (jnp.sqrt(v_hat) + ADAM_EPS) + ADAM_WD * w_ref[...])
        nm_ref[...] = mm
        nv_ref[...] = vv

    spec = pl.BlockSpec((tm, cols), lambda i: (i, 0))
    o = jax.ShapeDtypeStruct((rows, cols), F32)
    outs = pl.pallas_call(
        body, name=name, grid=(rows // tm,), in_specs=[spec] * 4, out_specs=[spec] * 3, out_shape=[o] * 3,
        compiler_params=_params(1),
    )(w2, g2, m2, v2)
    return tuple(a.reshape(shape) for a in outs)


def _layer_fwd(li, x, xb, pb, W, up=None, att=None):
    nm = lambda s: f"l{li}_{s}"
    sv = {"x_in_b": xb}
    (g1, u1, a1), got = _mm_swiglu(xb, W["ffn1_wg"], W["ffn1_wu"], comm=up[0] if up else None, name=nm("ffn1_up"))
    if up:
        W = {**W, **up[1](got)}
    x1, x1b, xh1, rs1 = _mm_ln(a1, W["ffn1_wd"], x, W["ln1_g"], W["ln1_b"], rscale=ALPHA, mscale=0.5, name=nm("ffn1_down_ln"))
    hbuf = _mm(x1b, W["w_in_p"], name=nm("in_proj"))
    ya, lu, lr, lig, la, lh = _lru_fwd(hbuf, W["lru_conv_w"], W["lru_conv_b"], W["lru_wa_bd"], W["lru_ba"],
                                       W["lru_wx_bd"], W["lru_bx"], W["lru_lambda"], name=nm("lru_fwd"))
    eq, ek = _fox_prep(hbuf, W["fox_bf_vec"], name=nm("fox_prep"))
    (yb, lse_rows), got = _fox_fwd(hbuf, eq, ek, comm=att[0] if att else None, name=nm("fox_fwd"))
    if att:
        W = {**W, **att[1](got)}
    yc, yssd, states = _ssd_fwd(hbuf, W["ssd_conv_w"], W["ssd_conv_b"], W["ssd_dtb_vec"], W["ssd_a_vec"],
                                W["ssd_d_exp"], W["ssd_norm_g"], name=nm("ssd_fwd"))
    ymix = _assemble([ya, yb, yc], D_MODEL, name=nm("y_mix"))
    x2, x2b, xh2, rs2 = _mm_ln(ymix, W["w_out"], x1, W["ln2_g"], W["ln2_b"], rscale=ALPHA, mscale=1.0, name=nm("out_proj_ln"))
    (g2, u2, a2), _ = _mm_swiglu(x2b, W["ffn2_wg"], W["ffn2_wu"], name=nm("ffn2_up"))
    x3, x3b, xh3, rs3 = _mm_ln(a2, W["ffn2_wd"], x2, W["ln3_g"], W["ln3_b"], rscale=ALPHA, mscale=0.5, name=nm("ffn2_down_ln"))
    x4, x4b, sg, e = _mm_pe(x3, x3b, pb, W["pe_gate_w"], W["pe_gate_b"], W["pe_proj"], name=nm("ple"))
    sv.update(g1=g1, u1=u1, a1=a1, x1b=x1b, xh1=xh1, rs1=rs1, hbuf=hbuf, lu=lu, lr=lr, lig=lig, la=la, lh=lh,
              eq=eq, ek=ek, lse_rows=lse_rows, yb=yb, yssd=yssd, states=states, ymix=ymix, x2b=x2b, xh2=xh2, rs2=rs2,
              g2=g2, u2=u2, a2=a2, x3b=x3b, xh3=xh3, rs3=rs3, sg=sg, e=e, pb=pb)
    return x4, x4b, sv, W


def _layer_bwd(li, dx4, sv, W, comm=None, late=None, last=None):
    nm = lambda s: f"l{li}_{s}"
    G = {}
    dgp, de, dbg = _pe_bwd_elem(dx4, sv["sg"], sv["e"], name=nm("ple_bwd"))
    G["pe_gate_b"] = dbg
    G["pe_gate_w"] = _mm(sv["x3b"], dgp, ta=True, out_dtype=BF16, name=nm("d_pe_gate_w"))
    G["pe_proj"] = _mm(sv["pb"], de, ta=True, out_dtype=BF16, chip_cols=True, name=nm("d_pe_proj"))
    dr3, dr3b, G["ln3_g"], G["ln3_b"] = _bwd_proj([(dgp, W["pe_gate_w"])], dx4, rscale=1.0,
                                                  ln=(sv["xh3"], sv["rs3"], W["ln3_g"]), name=nm("ln3_bwd"))
    G["ffn2_wd"] = _mm(sv["a2"], dr3b, ta=True, scale=0.5, out_dtype=BF16, name=nm("d_ffn2_wd"))
    dg2, du2 = _mm_swiglu_bwd(dr3b, W["ffn2_wd"], sv["g2"], sv["u2"], scale=0.5, name=nm("ffn2_act_bwd"))
    G["ffn2_wg"] = _mm(sv["x2b"], dg2, ta=True, out_dtype=BF16, chip_cols=True, name=nm("d_ffn2_wg"))
    G["ffn2_wu"] = _mm(sv["x2b"], du2, ta=True, out_dtype=BF16, chip_cols=True, name=nm("d_ffn2_wu"))
    dr2, dr2b, G["ln2_g"], G["ln2_b"] = _bwd_proj([(dg2, W["ffn2_wg"]), (du2, W["ffn2_wu"])], dr3, rscale=ALPHA,
                                                  ln=(sv["xh2"], sv["rs2"], W["ln2_g"]), name=nm("ln2_bwd"))
    G["w_out"] = _mm(sv["ymix"], dr2b, ta=True, out_dtype=BF16, name=nm("d_w_out"))
    dymix = _mm(dr2b, W["w_out"], tb=True, name=nm("d_ymix"))
    hbuf = sv["hbuf"]
    (dur, dgr, G["lru_conv_w"], G["lru_conv_b"], G["lru_wa_bd"], G["lru_ba"], G["lru_wx_bd"], G["lru_bx"],
     G["lru_lambda"]) = _lru_bwd(dymix, hbuf, sv["lu"], sv["lr"], sv["lig"], sv["la"], sv["lh"],
                                 W["lru_conv_w"], W["lru_wa_bd"], W["lru_wx_bd"], W["lru_lambda"], name=nm("lru_bwd"))
    delta = _fox_delta(dymix, sv["yb"], name=nm("fox_delta"))
    delta_rows = jnp.pad(delta[:, :ATT_HEADS].T, ((0, SUBLANES - ATT_HEADS), (0, 0)))
    comm = _merge_comms([comm, late(G) if late else None])
    (dk, dv, dfk, dqt, dfq), comm_out = _fox_bwd(hbuf, sv["eq"], sv["ek"], dymix, sv["lse_rows"], delta_rows,
                                                 comm=comm, name=nm("fox_bwd"))
    dq = dqt.T
    dfc = jnp.pad(dfq[:ATT_HEADS].T, ((0, 0), (0, LANES - ATT_HEADS))) - dfk
    dsm_f, G["fox_bf_vec"] = _fox_post(dfc, hbuf, W["fox_bf_vec"], name=nm("fox_post"))
    (dxr, dz, dsm_dt, G["ssd_norm_g"], G["ssd_d_exp"], G["ssd_a_vec"], G["ssd_dtb_vec"], G["ssd_conv_w"],
     G["ssd_conv_b"]) = _ssd_bwd(dymix, hbuf, sv["yssd"], sv["states"], W["ssd_conv_w"], W["ssd_conv_b"],
                                 W["ssd_dtb_vec"], W["ssd_a_vec"], W["ssd_d_exp"], W["ssd_norm_g"], name=nm("ssd_bwd"))
    dh = _assemble([dxr, dz, dur, dgr, dq, dk, dv, dsm_f + dsm_dt], H_WIDTH, name=nm("d_h"))
    G["w_in_p"] = _mm(sv["x1b"], dh, ta=True, name=nm("d_w_in"))
    dr1, dr1b, G["ln1_g"], G["ln1_b"] = _bwd_proj([(dh, W["w_in_p"])], dr2, rscale=ALPHA,
                                                  ln=(sv["xh1"], sv["rs1"], W["ln1_g"]), name=nm("ln1_bwd"))
    G["ffn1_wd"] = _mm(sv["a1"], dr1b, ta=True, scale=0.5, out_dtype=BF16, name=nm("d_ffn1_wd"))
    dg1, du1 = _mm_swiglu_bwd(dr1b, W["ffn1_wd"], sv["g1"], sv["u1"], scale=0.5, name=nm("ffn1_act_bwd"))
    G["ffn1_wg"] = _mm(sv["x_in_b"], dg1, ta=True, out_dtype=BF16, chip_cols=True, name=nm("d_ffn1_wg"))
    G["ffn1_wu"] = _mm(sv["x_in_b"], du1, ta=True, out_dtype=BF16, chip_cols=True, name=nm("d_ffn1_wu"))
    dx_in, *last_out = _bwd_proj([(dg1, W["ffn1_wg"]), (du1, W["ffn1_wu"])], dr1, rscale=ALPHA, ln=None,
                                 comm=last(G) if last else None, name=nm("x_in_bwd"))
    return dx_in, G, comm_out, (last_out[0] if last_out else None)


def _block_diag(w):
    n, b, _ = w.shape
    eye = jnp.eye(n, dtype=w.dtype)
    return (eye[:, None, :, None] * w[:, :, None, :]).reshape(n * b, n * b)


def _block_diag_extract(m):
    n, b = LRU_HEADS, HEAD_DIM
    return jnp.stack([m[b * i:b * (i + 1), b * i:b * (i + 1)] for i in range(n)])


def _lane_vec(v, lane0):
    return jnp.pad(v.astype(F32), (lane0, LANES - lane0 - v.shape[0])).reshape(1, LANES)


def _w_in_permute(w):
    d = w.shape[0]
    z = lambda n: jnp.zeros((d, n), w.dtype)
    return jnp.concatenate([w[:, 1796:2820], w[:, 1284:1796], w[:, 0:512], w[:, 512:1280],
                            w[:, 1280:1284], w[:, 2820:2828], z(LANES - 12), z(H_WIDTH - COL_SMALL - LANES)], axis=1)


def _w_in_unpermute(wp):
    return jnp.concatenate([wp[:, COL_U:COL_Q], wp[:, COL_Q:COL_SMALL], wp[:, COL_SMALL:COL_SMALL + 4],
                            wp[:, COL_Z:COL_U], wp[:, COL_XBC:COL_Z], wp[:, COL_SMALL + 4:COL_SMALL + 12]], axis=1)


def _big_weights(chipw):
    W = {}
    for n, w in chipw.items():
        if n in ("ffn1_wg", "ffn1_wu", "ffn2_wg", "ffn2_wu"):
            W[n] = w
        elif n in ("ffn1_wd", "ffn2_wd", "w_out", "pe_gate_w"):
            W[n] = w.reshape(-1, D_MODEL)
        elif n == "pe_proj":
            W[n] = jnp.moveaxis(w, 0, 1).reshape(PLE_DIM, D_MODEL)
        else:
            w_in = jnp.moveaxis(w[:, :, :IN_WIDTH // N_CHIPS], 0, 1).reshape(D_MODEL, IN_WIDTH)
            W["w_in_p"] = _w_in_permute(w_in)
    return W


def _small_weights(li, small):
    g = lambda n: small[n][li]
    W = {n: g(n) for n in ("ln1_g", "ln1_b", "ln2_g", "ln2_b", "ln3_g", "ln3_b", "pe_gate_b", "lru_conv_w",
                           "ssd_conv_w")}
    for n in ("lru_conv_b", "lru_ba", "lru_bx", "lru_lambda", "ssd_conv_b", "ssd_norm_g"):
        W[n] = g(n).reshape(1, -1)
    W["lru_wa_bd"] = _block_diag(g("lru_wa")).astype(BF16)
    W["lru_wx_bd"] = _block_diag(g("lru_wx")).astype(BF16)
    W["fox_bf_vec"] = _lane_vec(g("fox_bf"), LANE_F)
    W["ssd_dtb_vec"] = _lane_vec(g("ssd_dt_bias"), LANE_DT)
    W["ssd_a_vec"] = _lane_vec(-jnp.exp(g("ssd_a_log")), LANE_DT)
    W["ssd_d_exp"] = jnp.repeat(g("ssd_d"), HEAD_DIM).reshape(1, SSD_WIDTH)
    return W


def _big_grad_by_chip(G, n):
    if n in ("ffn1_wg", "ffn1_wu", "ffn2_wg", "ffn2_wu", "pe_proj"):
        return G[n]
    if n in ("ffn1_wd", "ffn2_wd", "w_out", "pe_gate_w"):
        return G[n].reshape(N_CHIPS, -1, D_MODEL)
    share = IN_WIDTH // N_CHIPS
    d_w_in = jnp.moveaxis(_w_in_unpermute(G["w_in_p"]).reshape(D_MODEL, N_CHIPS, share), 1, 0)
    return jnp.pad(d_w_in.astype(BF16), ((0, 0), (0, 0), (0, SHARE - share)))


def _layer_small_grads(G, W):
    out = {n: G[n] for n in ("lru_conv_w", "ssd_conv_w")}
    for n in ("ln1_g", "ln1_b", "ln2_g", "ln2_b", "ln3_g", "ln3_b", "pe_gate_b", "lru_conv_b", "lru_ba", "lru_bx",
              "lru_lambda", "ssd_conv_b", "ssd_norm_g"):
        out[n] = G[n].reshape(-1)
    out["lru_wa"] = _block_diag_extract(G["lru_wa_bd"])
    out["lru_wx"] = _block_diag_extract(G["lru_wx_bd"])
    out["fox_bf"] = G["fox_bf_vec"][0, LANE_F:LANE_F + ATT_HEADS]
    out["ssd_dt_bias"] = G["ssd_dtb_vec"][0, LANE_DT:LANE_DT + SSD_HEADS]
    out["ssd_a_log"] = G["ssd_a_vec"][0, LANE_DT:LANE_DT + SSD_HEADS] * W["ssd_a_vec"][0, LANE_DT:LANE_DT + SSD_HEADS]
    out["ssd_d"] = G["ssd_d_exp"].reshape(SSD_HEADS, HEAD_DIM).sum(axis=1)
    return out


WEIGHTS = ['ln1_g', 'ln1_b', 'ffn1_wg', 'ffn1_wu', 'ffn1_wd', 'w_in', 'lru_conv_w', 'lru_conv_b', 'lru_wa', 'lru_ba',
           'lru_wx', 'lru_bx', 'lru_lambda', 'fox_bf', 'ssd_conv_w', 'ssd_conv_b', 'ssd_dt_bias', 'ssd_a_log', 'ssd_d',
           'ssd_norm_g', 'w_out', 'ln2_g', 'ln2_b', 'ffn2_wg', 'ffn2_wu', 'ffn2_wd', 'ln3_g', 'ln3_b', 'pe_proj',
           'pe_gate_w', 'pe_gate_b']
FIRST = ((("ffn1_wg",), 1), (("ffn1_wu",), 1))
NEXT = ((("w_in",), 1),
        (("ffn1_wd",), 0))
EARLY = FIRST + NEXT
LATE = ((("ffn2_wg",), 1), (("ffn2_wu",), 1),
        (("ffn2_wd",), 0),
        (("w_out",), None), (("pe_gate_w",), None),
        (("pe_proj",), None))
BIG = {n: pad for names, pad in EARLY + LATE for n in names}
SMALL_SHARDED = {'lru_conv_w': 2, 'ssd_conv_w': 2}


def _unshard(seg, axis):
    moved = jnp.moveaxis(seg, 0, axis)
    shp = list(moved.shape)
    shp[axis:axis + 2] = [shp[axis] * shp[axis + 1]]
    return moved.reshape(shp)


def _pad_axis(a, axis, size):
    if axis is None or a.shape[axis] == size:
        return a
    pads = [(0, 0)] * a.ndim
    pads[axis] = (0, size - a.shape[axis])
    return jnp.pad(a, pads)


PACK_TILE = SUBLANES * LANES


def _pack(arrs):
    rows = []
    for a in arrs:
        flat = a.astype(F32).reshape(-1)
        rows.append(jnp.pad(flat, (0, (-flat.shape[0]) % PACK_TILE)).reshape(-1, LANES))
    return jnp.concatenate(rows, axis=0)


def _unpack(packed, shapes):
    out, off = [], 0
    for s in shapes:
        n = math.prod(s)
        r = -(-n // PACK_TILE) * SUBLANES
        out.append(packed[off:off + r].reshape(-1)[:n].reshape(s))
        off += r
    return out


def kernel(x, p, ln1_g, ln1_b, ffn1_wg, ffn1_wu, ffn1_wd, w_in, lru_conv_w, lru_conv_b, lru_wa, lru_ba, lru_wx, lru_bx, lru_lambda, fox_bf, ssd_conv_w, ssd_conv_b, ssd_dt_bias, ssd_a_log, ssd_d, ssd_norm_g, w_out, ln2_g, ln2_b, ffn2_wg, ffn2_wu, ffn2_wd, ln3_g, ln3_b, pe_proj, pe_gate_w, pe_gate_b, loss_target, m_ln1_g, m_ln1_b, m_ffn1_wg, m_ffn1_wu, m_ffn1_wd, m_w_in, m_lru_conv_w, m_lru_conv_b, m_lru_wa, m_lru_ba, m_lru_wx, m_lru_bx, m_lru_lambda, m_fox_bf, m_ssd_conv_w, m_ssd_conv_b, m_ssd_dt_bias, m_ssd_a_log, m_ssd_d, m_ssd_norm_g, m_w_out, m_ln2_g, m_ln2_b, m_ffn2_wg, m_ffn2_wu, m_ffn2_wd, m_ln3_g, m_ln3_b, m_pe_proj, m_pe_gate_w, m_pe_gate_b, v_ln1_g, v_ln1_b, v_ffn1_wg, v_ffn1_wu, v_ffn1_wd, v_w_in, v_lru_conv_w, v_lru_conv_b, v_lru_wa, v_lru_ba, v_lru_wx, v_lru_bx, v_lru_lambda, v_fox_bf, v_ssd_conv_w, v_ssd_conv_b, v_ssd_dt_bias, v_ssd_a_log, v_ssd_d, v_ssd_norm_g, v_w_out, v_ln2_g, v_ln2_b, v_ffn2_wg, v_ffn2_wu, v_ffn2_wd, v_ln3_g, v_ln3_b, v_pe_proj, v_pe_gate_w, v_pe_gate_b):
    args = locals()
    w_loc = {n: args[n] for n in WEIGHTS}
    m_loc = {n: args["m_" + n] for n in WEIGHTS}
    v_loc = {n: args["v_" + n] for n in WEIGHTS}
    chip = 2 * lax.axis_index("x") + lax.axis_index("y")
    core = lax.axis_index("c")
    big = list(BIG)
    small_sh = list(SMALL_SHARDED)
    small_rep = [n for n in WEIGHTS if n not in BIG and n not in SMALL_SHARDED]

    def srcs_of(li, groups):
        return [jnp.stack([_pad_axis(w_loc[n][li].astype(BF16), pad, SHARE) for n in names]) for names, pad in groups]

    def gather_comm(li, groups, base=0):
        return _gather_layer_comm(srcs_of(li, groups), li, base)

    def chip_weights(gathered, groups):
        return _big_weights({n: g[:, j] for (names, _), g in zip(groups, gathered) for j, n in enumerate(names)})

    def pair_sums(G, groups, tag):
        gs = [jnp.stack([_big_grad_by_chip(G, n) for n in names]) for names, _ in groups]
        flat = [g.reshape((-1,) + g.shape[2:]) for g in gs]
        theirs = _exchange(flat, ("c",), swap=True, name=f"reduce_cores_{tag}")
        return [_sum_pair(f, r, BF16, name=f"reduce_cores_sum_{tag}_{gi}").reshape(g.shape)
                for gi, (f, r, g) in enumerate(zip(flat, theirs, gs))]

    def finish_reduce(quad, sums, li, groups, tag):
        quad = [lax.dynamic_update_index_in_dim(q, lax.dynamic_index_in_dim(s, chip, 1, keepdims=False), chip, 0)
                for q, s in zip(quad, sums)]
        red = [_sum_slots(q.reshape(N_CHIPS, -1, q.shape[-1]), F32,
                          name=f"reduce_chips_sum_{tag}_{gi}").reshape(q.shape[1:]) for gi, q in enumerate(quad)]
        theirs = _exchange(red, ("c",), swap=True, name=f"reduce_share_{tag}")
        out = {}
        for (names, _), r, rv in zip(groups, red, theirs):
            both = jnp.where(core == li, r, rv)
            for j, n in enumerate(names):
                out[n] = both[j]
        return out

    everything = EARLY + LATE
    first0 = _run_comm(gather_comm(0, FIRST), name="gather_w_l0")
    small = {n: w_loc[n] for n in small_rep}
    (sg,) = _exchange([_pack([w_loc[n] for n in small_sh])[None]], ("x", "y"), name="gather_conv_w")
    shards = [_unpack(sg[k, 0], [w_loc[n].shape for n in small_sh]) for k in range(N_CHIPS)]
    for j, n in enumerate(small_sh):
        small[n] = _unshard(jnp.stack([shards[k][j] for k in range(N_CHIPS)]), SMALL_SHARDED[n])

    W0 = {**_small_weights(0, small), **chip_weights(first0, FIRST)}
    late0_comm = gather_comm(0, LATE)
    early1 = []

    def in_attention0(got):
        early1.extend(got[len(LATE):])
        return chip_weights(got[:len(LATE)], LATE)

    xs = x[0]
    xs, xb, sv0, W0 = _layer_fwd(
        0, xs, xs.astype(BF16), p[0, 0].astype(BF16), W0,
        up=(gather_comm(0, NEXT), lambda got: chip_weights(got, NEXT)),
        att=(_merge_comms([late0_comm, gather_comm(1, EARLY, base=late0_comm.n_sems)]), in_attention0))
    W1 = {**_small_weights(1, small), **chip_weights(early1, EARLY)}
    xs, _, sv1, W1 = _layer_fwd(1, xs, xb, p[1, 0].astype(BF16), W1,
                                att=(gather_comm(1, LATE), lambda got: chip_weights(got, LATE)))
    dx, loss = _loss_kernel(xs, loss_target[0], name="loss")
    loss = lax.psum(loss[0, 0], MESH_AXES)
    dx, G1, _, _ = _layer_bwd(1, dx, sv1, W1)
    sums1 = pair_sums(G1, everything, "l1")
    comm1 = _reduce_chips_comm(sums1, 1)
    late_sums, early_sums = [], []

    def late0(G):
        late_sums.extend(pair_sums(G, LATE, "l0_late"))
        return _reduce_chips_comm(late_sums, 0, base=comm1.n_sems)

    def last0(G):
        early_sums.extend(pair_sums(G, EARLY, "l0"))
        return _reduce_chips_comm(early_sums, 0)

    grad_x, G0, quads, quads0 = _layer_bwd(0, dx, sv0, W0, comm=comm1, late=late0, last=last0)

    n1 = len(comm1.out_shapes)
    red = [{**finish_reduce(quads[n1:], late_sums, 0, LATE, "l0_late"),
            **finish_reduce(quads0, early_sums, 0, EARLY, "l0")},
           finish_reduce(quads[:n1], sums1, 1, everything, "l1")]
    g_red = {}
    for n in big:
        g = jnp.stack([red[li][n] for li in range(DEPTH)])
        g_red[n] = g[tuple(slice(0, s) for s in w_loc[n].shape)]
    small_l = [_layer_small_grads(G0, W0), _layer_small_grads(G1, W1)]
    g_small = {n: jnp.stack([small_l[li][n] for li in range(DEPTH)]) for n in small_l[0]}
    small_all = small_rep + small_sh
    sgp = _pack([g_small[n] for n in small_all])
    (sall,) = _exchange([sgp[None]], MESH_AXES, name="reduce_small")
    sred = _sum_slots(sall.reshape((2 ** len(MESH_AXES),) + sgp.shape), F32, name="reduce_small_sum")
    for n, g in zip(small_all, _unpack(sred, [g_small[n].shape for n in small_all])):
        if n in SMALL_SHARDED:
            width = w_loc[n].shape[-1]
            g = lax.dynamic_slice_in_dim(g, chip * width, width, axis=SMALL_SHARDED[n])
        g_red[n] = g

    delta, new_m, new_v = {}, {}, {}
    for n in big:
        delta[n], new_m[n], new_v[n] = _adamw(w_loc[n], g_red[n], m_loc[n], v_loc[n], name="adamw_" + n)
    shapes = [w_loc[n].shape for n in small_all]
    packs = [_pack([d[n] for n in small_all]) for d in (w_loc, g_red, m_loc, v_loc)]
    outs = _adamw(*packs, name="adamw_small")
    for d, o in zip((delta, new_m, new_v), outs):
        for n, a in zip(small_all, _unpack(o, shapes)):
            d[n] = a
    return (loss, grad_x[None], *[g_red[n] for n in WEIGHTS], *[delta[n] for n in WEIGHTS],
            *[new_m[n] for n in WEIGHTS], *[new_v[n] for n in WEIGHTS])
```

```python
import math

import jax
import jax.numpy as jnp
from jax import lax
from jax.experimental import pallas as pl
from jax.experimental.pallas import tpu as pltpu

F32 = jnp.float32
BF16 = jnp.bfloat16

D_MODEL = 1024
DEPTH = 2
PLE_DIM = 256
HEAD_DIM = 64
LRU_WIDTH = 256
LRU_HEADS = 4
LRU_C = 8.0
CONV_K = 4
ATT_WIDTH = 256
ATT_HEADS = 4
SSD_WIDTH = 512
SSD_HEADS = 8
SSD_GROUPS = 2
SSD_STATE = 128
SSD_CHUNK = 128
SSD_CONV_DIM = 1024
ALPHA = (2.0 * DEPTH) ** 0.25
LN_EPS = 1e-5
RMS_EPS = 1e-5
IN_WIDTH = 2828
ADAM_LR = 0.001
ADAM_B1 = 0.9
ADAM_B2 = 0.999
ADAM_EPS = 1e-08
ADAM_WD = 0.01
ADAM_STEP = 10

H_WIDTH = 3072
COL_XBC, COL_Z, COL_U, COL_G, COL_Q, COL_K, COL_V, COL_SMALL = 0, 1024, 1536, 1792, 2048, 2304, 2560, 2816
LANE_F = 0
LANE_DT = 4
LANES = 128
SUBLANES = 8
NEG = -1e30

VMEM_LIMIT = 48 * 1024 * 1024

N_CHIPS = 4
MESH_AXES = ("x", "y", "c")
SHARE = 768


def _params(n):
    return pltpu.CompilerParams(dimension_semantics=("arbitrary",) * n, vmem_limit_bytes=VMEM_LIMIT)


def _pick(n, cands):
    for c in cands:
        if n % c == 0:
            return c
    return n


def _iota(shape, dim):
    return lax.broadcasted_iota(jnp.int32, shape, dim)


def _shift_down(x, s, prev8):
    if s == 0:
        return x
    r = pltpu.roll(x, s, 0)
    pr = pltpu.roll(prev8, s, 0)
    head = jnp.where(_iota(pr.shape, 0) < s, pr, r[:SUBLANES])
    return jnp.concatenate([head, r[SUBLANES:]], axis=0)


def _shift_up(x, s, next8):
    if s == 0:
        return x
    n = x.shape[0]
    r = pltpu.roll(x, n - s, 0)
    nr = pltpu.roll(next8, SUBLANES - s, 0)
    tail = jnp.where(_iota(nr.shape, 0) >= SUBLANES - s, nr, r[n - SUBLANES:])
    return jnp.concatenate([r[:n - SUBLANES], tail], axis=0)


def _scan_fwd(a, b):
    n = a.shape[0]
    row = _iota(a.shape, 0)
    d = 1
    while d < n:
        keep = row >= d
        a_s = jnp.where(keep, pltpu.roll(a, d, 0), 1.0)
        b_s = jnp.where(keep, pltpu.roll(b, d, 0), 0.0)
        b = a * b_s + b
        a = a * a_s
        d *= 2
    return a, b


def _scan_bwd(a, b):
    n = a.shape[0]
    row = _iota(a.shape, 0)
    d = 1
    while d < n:
        keep = row < n - d
        a_s = jnp.where(keep, pltpu.roll(a, n - d, 0), 1.0)
        b_s = jnp.where(keep, pltpu.roll(b, n - d, 0), 0.0)
        b = a * b_s + b
        a = a * a_s
        d *= 2
    return a, b


def _cumsum_rows(x, reverse=False):
    n = x.shape[0]
    row = _iota(x.shape, 0)
    d = 1
    while d < n:
        if reverse:
            x = x + jnp.where(row < n - d, pltpu.roll(x, n - d, 0), 0.0)
        else:
            x = x + jnp.where(row >= d, pltpu.roll(x, d, 0), 0.0)
        d *= 2
    return x


def _col(x, lane):
    return jnp.sum(jnp.where(_iota(x.shape, 1) == lane, x, 0.0), axis=1, keepdims=True)


def _row(x, r):
    return jnp.sum(jnp.where(_iota(x.shape, 0) == r, x, 0.0), axis=0, keepdims=True)


def _sigmoid(x):
    return jax.nn.sigmoid(x)


def _softplus(x):
    return jnp.maximum(x, 0.0) + jnp.log(1.0 + jnp.exp(-jnp.abs(x)))


def _gelu_and_grad(x):
    c0 = math.sqrt(2.0 / math.pi)
    inner = c0 * (x + 0.044715 * x * x * x)
    t = jnp.tanh(inner)
    g = 0.5 * x * (1.0 + t)
    dg = 0.5 * (1.0 + t) + 0.5 * x * (1.0 - t * t) * c0 * (1.0 + 3.0 * 0.044715 * x * x)
    return g, dg


def _dot(a, b, ca, cb):
    return lax.dot_general(a, b, (((ca,), (cb,)), ((), ())), preferred_element_type=F32)


def _conv_taps(xr, prev8, w, bias):
    y = bias + w[CONV_K - 1:CONV_K, :] * xr
    for j in range(CONV_K - 1):
        y = y + w[j:j + 1, :] * _shift_down(xr, CONV_K - 1 - j, prev8)
    return y


def _conv_taps_bwd(dy, next8, w, xr):
    dx = None
    dws = []
    for j in range(CONV_K):
        sh = _shift_up(dy, CONV_K - 1 - j, next8)
        term = w[j:j + 1, :] * sh
        dx = term if dx is None else dx + term
        dws.append(jnp.sum(sh * xr, axis=0, keepdims=True))
    return dx, jnp.concatenate(dws, axis=0)


def _head_expand(v, lane0, nheads, width):
    rows = v.shape[0]
    colhead = _iota((rows, width), 1) // HEAD_DIM
    out = jnp.zeros((rows, width), F32)
    for h in range(nheads):
        out = jnp.where(colhead == h, _col(v, lane0 + h), out)
    return out


def _head_reduce(x, lane0, nheads):
    rows = x.shape[0]
    colhead = _iota(x.shape, 1) // HEAD_DIM
    lane = _iota((rows, LANES), 1)
    out = jnp.zeros((rows, LANES), F32)
    for h in range(nheads):
        s = jnp.sum(jnp.where(colhead == h, x, 0.0), axis=1, keepdims=True)
        out = jnp.where(lane == lane0 + h, s, out)
    return out


def _mm(a, b, *, ta=False, tb=False, scale=1.0, out_dtype=F32, chip_cols=False, name):
    if ta:
        kk, m = a.shape
    else:
        m, kk = a.shape
    n = b.shape[0] if tb else b.shape[1]
    tm = _pick(m, (1024, 512, 256, 128))
    tk = _pick(kk, (1024, 768, 512, 256, 128))
    nk = kk // tk
    dn_a = 0 if ta else 1
    dn_b = 1 if tb else 0
    share = n // N_CHIPS
    if chip_cols:
        tn = n
        out_spec = pl.BlockSpec((N_CHIPS, tm, share), lambda i, j, k: (0, i, 0))
        out_shape = jax.ShapeDtypeStruct((N_CHIPS, m, share), out_dtype)
    else:
        tn = _pick(n, (1024, 768, 512, 256, 128))
        out_spec = pl.BlockSpec((tm, tn), lambda i, j, k: (i, j))
        out_shape = jax.ShapeDtypeStruct((m, n), out_dtype)

    def body(a_ref, b_ref, o_ref, acc):
        k = pl.program_id(2)

        @pl.when(k == 0)
        def _():
            acc[...] = jnp.zeros_like(acc)

        acc[...] += _dot(a_ref[...].astype(BF16), b_ref[...].astype(BF16), dn_a, dn_b)

        @pl.when(k == nk - 1)
        def _():
            if chip_cols:
                for c in range(N_CHIPS):
                    o_ref[c] = (acc[:, share * c:share * (c + 1)] * scale).astype(out_dtype)
            else:
                o_ref[...] = (acc[...] * scale).astype(out_dtype)

    a_spec = pl.BlockSpec((tk, tm), lambda i, j, k: (k, i)) if ta else pl.BlockSpec((tm, tk), lambda i, j, k: (i, k))
    b_spec = pl.BlockSpec((tn, tk), lambda i, j, k: (j, k)) if tb else pl.BlockSpec((tk, tn), lambda i, j, k: (k, j))
    return pl.pallas_call(
        body, name=name, grid=(m // tm, n // tn, nk),
        in_specs=[a_spec, b_spec],
        out_specs=out_spec, out_shape=out_shape,
        scratch_shapes=[pltpu.VMEM((tm, tn), F32)],
        compiler_params=_params(3),
    )(a, b)


def _mm_swiglu(xb, wg, wu, *, comm=None, name):
    t, d = xb.shape
    share = wg.shape[2]
    n = N_CHIPS * share
    tm = _pick(t, (512, 256, 128))
    tn = _pick(share, (768, 256, 128))
    per = share // tn

    def body(x_ref, wg_ref, wu_ref, g_ref, u_ref, a_ref):
        x = x_ref[pl.ds(pl.multiple_of(pl.program_id(1) * tm, tm), tm), :]
        g = _dot(x, wg_ref[...], 1, 0)
        u = _dot(x, wu_ref[...], 1, 0)
        g_ref[...] = g.astype(BF16)
        u_ref[...] = u.astype(BF16)
        a_ref[...] = (g * _sigmoid(g) * u).astype(BF16)

    o = jax.ShapeDtypeStruct((t, n), BF16)
    ospec = pl.BlockSpec((tm, tn), lambda j, i: (i, j))
    return _hosted_call(
        body, comm, (n // tn, t // tm), name=name,
        in_specs=[pl.BlockSpec((t, d), lambda j, i: (0, 0), pipeline_mode=pl.Buffered(1)),
                  pl.BlockSpec((None, d, tn), lambda j, i: (j // per, 0, j % per)),
                  pl.BlockSpec((None, d, tn), lambda j, i: (j // per, 0, j % per))],
        out_specs=[ospec, ospec, ospec], out_shape=[o, o, o], scratch_shapes=[], args=[xb, wg, wu])


def _mm_swiglu_bwd(dr, wd, g, u, *, scale, name):
    t, d = dr.shape
    n = wd.shape[0]
    tm = _pick(t, (512, 256, 128))
    tn = _pick(n, (768, 256, 128))

    def body(dr_ref, wd_ref, g_ref, u_ref, dg_ref, du_ref):
        dr_rows = dr_ref[pl.ds(pl.multiple_of(pl.program_id(1) * tm, tm), tm), :]
        da = _dot(dr_rows.astype(BF16), wd_ref[...], 1, 1) * scale
        gg = g_ref[...].astype(F32)
        uu = u_ref[...].astype(F32)
        sg = _sigmoid(gg)
        dg_ref[...] = (da * uu * (sg * (1.0 + gg * (1.0 - sg)))).astype(BF16)
        du_ref[...] = (da * gg * sg).astype(BF16)

    o = jax.ShapeDtypeStruct((t, n), BF16)
    ospec = pl.BlockSpec((tm, tn), lambda j, i: (i, j))
    return pl.pallas_call(
        body, name=name, grid=(n // tn, t // tm),
        in_specs=[pl.BlockSpec((t, d), lambda j, i: (0, 0), pipeline_mode=pl.Buffered(1)),
                  pl.BlockSpec((tn, d), lambda j, i: (j, 0)),
                  ospec, ospec],
        out_specs=[ospec, ospec], out_shape=[o, o],
        compiler_params=_params(2),
    )(dr, wd, g, u)


def _mm_ln(a, w, resid, gain, bias, *, rscale, mscale, name):
    t, kk = a.shape
    d = w.shape[1]
    tm = _pick(t, (512, 256, 128))
    tk = kk
    nk = kk // tk

    def body(a_ref, w_ref, r_ref, g_ref, b_ref, y_ref, yb_ref, xh_ref, rs_ref, acc):
        k = pl.program_id(1)

        @pl.when(k == 0)
        def _():
            acc[...] = jnp.zeros_like(acc)

        acc[...] += _dot(a_ref[...].astype(BF16), w_ref[...], 1, 0)

        @pl.when(k == nk - 1)
        def _():
            r = rscale * r_ref[...] + mscale * acc[...]
            mu = jnp.mean(r, axis=1, keepdims=True)
            xc = r - mu
            var = jnp.mean(xc * xc, axis=1, keepdims=True)
            rstd = lax.rsqrt(var + LN_EPS)
            xh = xc * rstd
            y = xh * g_ref[...] + b_ref[...]
            y_ref[...] = y
            yb_ref[...] = y.astype(BF16)
            xh_ref[...] = xh
            rs_ref[...] = rstd

    row = pl.BlockSpec((tm, d), lambda i, k: (i, 0))
    vec = pl.BlockSpec((1, d), lambda i, k: (0, 0))
    return pl.pallas_call(
        body, name=name, grid=(t // tm, nk),
        in_specs=[pl.BlockSpec((tm, tk), lambda i, k: (i, k)),
                  pl.BlockSpec((tk, d), lambda i, k: (k, 0)), row, vec, vec],
        out_specs=[row, row, row, pl.BlockSpec((tm, 1), lambda i, k: (i, 0))],
        out_shape=[jax.ShapeDtypeStruct((t, d), F32), jax.ShapeDtypeStruct((t, d), BF16),
                   jax.ShapeDtypeStruct((t, d), F32), jax.ShapeDtypeStruct((t, 1), F32)],
        scratch_shapes=[pltpu.VMEM((tm, d), F32)],
        compiler_params=_params(2),
    )(a, w, resid, gain.reshape(1, d), bias.reshape(1, d))


def _bwd_proj(pairs, resid, *, rscale, ln, comm=None, name):
    t, kk = pairs[0][0].shape
    d = pairs[0][1].shape[-2]
    has_ln = ln is not None
    npair = len(pairs)
    tm = _pick(t, (256, 128) if has_ln and npair > 1 else (512, 256, 128))
    nt = t // tm

    def body(*refs):
        ab = refs[:2 * npair]
        r_ref = refs[2 * npair]
        pos = 2 * npair + 1
        if has_ln:
            xh_ref, rs_ref, g_ref = refs[pos:pos + 3]
            pos += 3
            o_ref, ob_ref, dg_ref, db_ref = refs[pos:pos + 4]
        else:
            o_ref = refs[pos]
        i = pl.program_id(0)
        dy = rscale * r_ref[...]
        for q in range(npair):
            a_ref, b_ref = ab[2 * q], ab[2 * q + 1]
            if len(b_ref.shape) == 3:
                share = b_ref.shape[2]
                for c in range(N_CHIPS):
                    dy = dy + _dot(a_ref[:, share * c:share * (c + 1)].astype(BF16), b_ref[c], 1, 1)
            else:
                dy = dy + _dot(a_ref[...].astype(BF16), b_ref[...], 1, 1)
        if not has_ln:
            o_ref[...] = dy
            return
        xh = xh_ref[...]
        w = dy * g_ref[...]
        m1 = jnp.mean(w, axis=1, keepdims=True)
        m2 = jnp.mean(w * xh, axis=1, keepdims=True)
        dr = rs_ref[...] * (w - m1 - xh * m2)
        o_ref[...] = dr
        ob_ref[...] = dr.astype(BF16)

        @pl.when(i == 0)
        def _():
            dg_ref[...] = jnp.zeros_like(dg_ref)
            db_ref[...] = jnp.zeros_like(db_ref)

        dg_ref[...] += jnp.sum(dy * xh, axis=0, keepdims=True)
        db_ref[...] += jnp.sum(dy, axis=0, keepdims=True)

    row = pl.BlockSpec((tm, d), lambda i, k: (i, 0))
    vec = pl.BlockSpec((1, d), lambda i, k: (0, 0))
    in_specs, args = [], []
    for a, b in pairs:
        b_spec = pl.BlockSpec(b.shape, lambda i, k, nd=b.ndim: (0,) * nd, pipeline_mode=pl.Buffered(1))
        in_specs += [pl.BlockSpec((tm, kk), lambda i, k: (i, 0)), b_spec]
        args += [a, b]
    in_specs.append(row)
    args.append(resid)
    out_specs = [row]
    out_shape = [jax.ShapeDtypeStruct((t, d), F32)]
    if has_ln:
        xh, rs, gain = ln
        in_specs += [row, pl.BlockSpec((tm, 1), lambda i, k: (i, 0)), vec]
        args += [xh, rs, gain.reshape(1, d)]
        out_specs += [row, vec, vec]
        out_shape += [jax.ShapeDtypeStruct((t, d), BF16)] + [jax.ShapeDtypeStruct((1, d), F32)] * 2
    outs, got = _hosted_call(body, comm, (nt, 1), name=name, in_specs=in_specs, out_specs=out_specs,
                             out_shape=out_shape, scratch_shapes=[], args=args)
    return tuple(outs) if comm is None else tuple(outs) + (got,)


def _mm_pe(x3, x3b, pb, wgate, bgate, wproj, *, name):
    t, d = x3.shape
    pd = pb.shape[1]
    tm = _pick(t, (512, 256, 128))
    tn = _pick(d, (512, 256, 128))

    def body(x_ref, xb_ref, p_ref, wg_ref, bg_ref, wp_ref, y_ref, yb_ref, sg_ref, e_ref):
        sg = _sigmoid(_dot(xb_ref[...], wg_ref[...], 1, 0) + bg_ref[...])
        e = _dot(p_ref[...], wp_ref[...], 1, 0)
        y = x_ref[...] + sg * e
        y_ref[...] = y
        yb_ref[...] = y.astype(BF16)
        sg_ref[...] = sg.astype(BF16)
        e_ref[...] = e.astype(BF16)

    ospec = pl.BlockSpec((tm, tn), lambda i, j: (i, j))
    ob = jax.ShapeDtypeStruct((t, d), BF16)
    return pl.pallas_call(
        body, name=name, grid=(t // tm, d // tn),
        in_specs=[ospec, pl.BlockSpec((tm, d), lambda i, j: (i, 0)), pl.BlockSpec((tm, pd), lambda i, j: (i, 0)),
                  pl.BlockSpec((d, tn), lambda i, j: (0, j)), pl.BlockSpec((1, tn), lambda i, j: (0, j)),
                  pl.BlockSpec((pd, tn), lambda i, j: (0, j))],
        out_specs=[ospec, ospec, ospec, ospec],
        out_shape=[jax.ShapeDtypeStruct((t, d), F32), ob, ob, ob],
        compiler_params=_params(2),
    )(x3, x3b, pb, wgate, bgate.reshape(1, d), wproj)


def _pe_bwd_elem(dx4, sg, e, *, name):
    t, d = dx4.shape
    tm = _pick(t, (512, 256, 128))

    def body(dx_ref, sg_ref, e_ref, dgp_ref, de_ref, db_ref):
        dx = dx_ref[...]
        s = sg_ref[...].astype(F32)
        dgp = dx * e_ref[...].astype(F32) * s * (1.0 - s)
        dgp_ref[...] = dgp.astype(BF16)
        de_ref[...] = (dx * s).astype(BF16)

        @pl.when(pl.program_id(0) == 0)
        def _():
            db_ref[...] = jnp.zeros_like(db_ref)

        db_ref[...] += jnp.sum(dgp, axis=0, keepdims=True)

    row = pl.BlockSpec((tm, d), lambda i: (i, 0))
    ob = jax.ShapeDtypeStruct((t, d), BF16)
    return pl.pallas_call(
        body, name=name, grid=(t // tm,), in_specs=[row, row, row],
        out_specs=[row, row, pl.BlockSpec((1, d), lambda i: (0, 0))],
        out_shape=[ob, ob, jax.ShapeDtypeStruct((1, d), F32)],
        compiler_params=_params(1),
    )(dx4, sg, e)


def _assemble(pieces, width, *, name):
    t = pieces[0].shape[0]
    tm = _pick(t, (512, 256, 128))
    widths = [p.shape[1] for p in pieces]

    def body(*refs):
        o_ref = refs[-1]
        off = 0
        for p_ref, w in zip(refs[:-1], widths):
            o_ref[:, off:off + w] = p_ref[...].astype(BF16)
            off += w
        if off < width:
            o_ref[:, off:] = jnp.zeros((tm, width - off), BF16)

    return pl.pallas_call(
        body, name=name, grid=(t // tm,),
        in_specs=[pl.BlockSpec((tm, w), lambda i: (i, 0)) for w in widths],
        out_specs=pl.BlockSpec((tm, width), lambda i: (i, 0)),
        out_shape=jax.ShapeDtypeStruct((t, width), BF16),
        compiler_params=_params(1),
    )(*pieces)


def _loss_kernel(y, target, *, name):
    t, d = y.shape
    tm = _pick(t, (512, 256, 128))

    def body(y_ref, t_ref, dy_ref, l_ref):
        diff = y_ref[...] - t_ref[...]
        dy_ref[...] = diff * (1.0 / d)

        @pl.when(pl.program_id(0) == 0)
        def _():
            l_ref[...] = jnp.zeros_like(l_ref)

        part = jnp.sum(jnp.mean(diff * diff, axis=1, keepdims=True), axis=0, keepdims=True)
        l_ref[...] += 0.5 * part

    row = pl.BlockSpec((tm, d), lambda i: (i, 0))
    return pl.pallas_call(
        body, name=name, grid=(t // tm,), in_specs=[row, row],
        out_specs=[row, pl.BlockSpec((1, 1), lambda i: (0, 0))],
        out_shape=[jax.ShapeDtypeStruct((t, d), F32), jax.ShapeDtypeStruct((1, 1), F32)],
        compiler_params=_params(1),
    )(y, target)


LRU_TM = 256


def _lru_gate_terms(r, lam):
    sp = _softplus(-lam)
    la = -LRU_C * r * sp
    a = jnp.exp(la)
    em = jnp.tanh(la) * (jnp.exp(2.0 * la) + 1.0)
    s = jnp.sqrt(-em)
    return la, a, s, sp


def _lru_fwd(hbuf, conv_w, conv_b, wa, ba, wx, bx, lam, *, name):
    t = hbuf.shape[0]
    w = LRU_WIDTH
    tm = _pick(t, (LRU_TM, 128))
    cu, cg = COL_U // w, COL_G // w
    hb = tm // SUBLANES

    def body(u_ref, up_ref, g_ref, cw_ref, cb_ref, wa_ref, ba_ref, wx_ref, bx_ref, lam_ref,
             y_ref, u_out, r_out, i_out, a_out, h_out, carry):
        i = pl.program_id(0)

        @pl.when(i == 0)
        def _():
            carry[...] = jnp.zeros_like(carry)

        prev = jnp.where(i == 0, 0.0, up_ref[...])
        u = _conv_taps(u_ref[...], prev, cw_ref[...], cb_ref[...])
        ub = u.astype(BF16)
        r = _sigmoid(_dot(ub, wa_ref[...], 1, 0) + ba_ref[...])
        ig = _sigmoid(_dot(ub, wx_ref[...], 1, 0) + bx_ref[...])
        _, a, s, _ = _lru_gate_terms(r, lam_ref[...])
        b = s * (ig * u)
        acum, hs = _scan_fwd(a, b)
        h = hs + acum * carry[0:1, :]
        carry[...] = jnp.broadcast_to(h[tm - 1:tm, :], carry.shape)
        gl, _ = _gelu_and_grad(g_ref[...])
        y_ref[...] = h * gl
        u_out[...] = u
        r_out[...] = r
        i_out[...] = ig
        a_out[...] = a
        h_out[...] = h

    row = pl.BlockSpec((tm, w), lambda i: (i, 0))
    vec = pl.BlockSpec((1, w), lambda i: (0, 0))
    mat = pl.BlockSpec((w, w), lambda i: (0, 0))
    o = jax.ShapeDtypeStruct((t, w), F32)
    return pl.pallas_call(
        body, name=name, grid=(t // tm,),
        in_specs=[pl.BlockSpec((tm, w), lambda i: (i, cu)),
                  pl.BlockSpec((SUBLANES, w), lambda i: (jnp.maximum(i * hb - 1, 0), cu)),
                  pl.BlockSpec((tm, w), lambda i: (i, cg)),
                  pl.BlockSpec((CONV_K, w), lambda i: (0, 0)), vec, mat, vec, mat, vec, vec],
        out_specs=[row] * 6, out_shape=[o] * 6,
        scratch_shapes=[pltpu.VMEM((SUBLANES, w), F32)],
        compiler_params=_params(1),
    )(hbuf, hbuf, hbuf, conv_w, conv_b, wa, ba, wx, bx, lam)


def _lru_bwd(dymix, hbuf, u, r, ig, a, h, conv_w, wa, wx, lam, *, name):
    t = hbuf.shape[0]
    w = LRU_WIDTH
    tm = _pick(t, (LRU_TM, 128))
    nb = t // tm
    cu, cg = COL_U // w, COL_G // w
    hb = tm // SUBLANES
    last8 = t // SUBLANES - 1

    def body(dy_ref, ur_ref, g_ref, u_ref, r_ref, i_ref, a_ref, an_ref, h_ref, hp_ref,
             cw_ref, wa_ref, wx_ref, lam_ref,
             dur_ref, dgr_ref, dcw_ref, dcb_ref, dwa_ref, dba_ref, dwx_ref, dbx_ref, dlam_ref,
             lcarry, dnext):
        i = pl.program_id(0)
        ib = nb - 1 - i

        @pl.when(i == 0)
        def _():
            lcarry[...] = jnp.zeros_like(lcarry)
            dnext[...] = jnp.zeros_like(dnext)
            for ref in (dcw_ref, dcb_ref, dwa_ref, dba_ref, dwx_ref, dbx_ref, dlam_ref):
                ref[...] = jnp.zeros_like(ref)

        dy = dy_ref[...]
        hh = h_ref[...]
        av = a_ref[...]
        uu = u_ref[...]
        rr = r_ref[...]
        ii = i_ref[...]
        lam_v = lam_ref[...]
        gl, dgl = _gelu_and_grad(g_ref[...])
        dgr_ref[...] = (dy * hh * dgl).astype(BF16)
        dh_out = dy * gl
        a_next = _shift_up(av, 1, jnp.where(ib == nb - 1, 0.0, an_ref[...]))
        acum, ls = _scan_bwd(a_next, dh_out)
        lam_adj = ls + acum * lcarry[0:1, :]
        lcarry[...] = jnp.broadcast_to(lam_adj[0:1, :], lcarry.shape)
        h_prev = _shift_down(hh, 1, jnp.where(ib == 0, 0.0, hp_ref[...]))
        da = lam_adj * h_prev
        _, a2, s, sp = _lru_gate_terms(rr, lam_v)
        d_igu = lam_adj * s
        ds = lam_adj * ii * uu
        dla = da * a2 - ds * (a2 * a2) / s
        dr = dla * (-LRU_C * sp)
        dlam_ref[...] += jnp.sum(dla * (LRU_C * rr * _sigmoid(-lam_v)), axis=0, keepdims=True)
        dpre_r = dr * rr * (1.0 - rr)
        dpre_i = d_igu * uu * ii * (1.0 - ii)
        prb = dpre_r.astype(BF16)
        pib = dpre_i.astype(BF16)
        ub = uu.astype(BF16)
        du = d_igu * ii + _dot(prb, wa_ref[...], 1, 1) + _dot(pib, wx_ref[...], 1, 1)
        dwa_ref[...] += _dot(ub, prb, 0, 0)
        dwx_ref[...] += _dot(ub, pib, 0, 0)
        dba_ref[...] += jnp.sum(dpre_r, axis=0, keepdims=True)
        dbx_ref[...] += jnp.sum(dpre_i, axis=0, keepdims=True)
        dur, dws = _conv_taps_bwd(du, dnext[...], cw_ref[...], ur_ref[...])
        dur_ref[...] = dur.astype(BF16)
        dcw_ref[...] += dws
        dcb_ref[...] += jnp.sum(du, axis=0, keepdims=True)
        dnext[...] = du[:SUBLANES]

    def rowspec(col):
        return pl.BlockSpec((tm, w), lambda i: (nb - 1 - i, col))

    row = rowspec(0)
    nxt = pl.BlockSpec((SUBLANES, w), lambda i: (jnp.minimum((nb - i) * hb, last8), 0))
    prv = pl.BlockSpec((SUBLANES, w), lambda i: (jnp.maximum((nb - 1 - i) * hb - 1, 0), 0))
    vec = pl.BlockSpec((1, w), lambda i: (0, 0))
    mat = pl.BlockSpec((w, w), lambda i: (0, 0))
    cw = pl.BlockSpec((CONV_K, w), lambda i: (0, 0))
    o = jax.ShapeDtypeStruct((t, w), BF16)
    v1 = jax.ShapeDtypeStruct((1, w), F32)
    m1 = jax.ShapeDtypeStruct((w, w), F32)
    return pl.pallas_call(
        body, name=name, grid=(nb,),
        in_specs=[rowspec(0), rowspec(cu), rowspec(cg), row, row, row, row, nxt, row, prv, cw, mat, mat, vec],
        out_specs=[row, row, cw, vec, mat, vec, mat, vec, vec],
        out_shape=[o, o, jax.ShapeDtypeStruct((CONV_K, w), F32), v1, m1, v1, m1, v1, v1],
        scratch_shapes=[pltpu.VMEM((SUBLANES, w), F32), pltpu.VMEM((SUBLANES, w), F32)],
        compiler_params=_params(1),
    )(dymix, hbuf, hbuf, u, r, ig, a, a, h, h, conv_w, wa, wx, lam)


FOX_T = 512
FOX_PREP_TM = 256


def _log_sigmoid(x):
    return jnp.minimum(x, 0.0) - jnp.log(1.0 + jnp.exp(-jnp.abs(x)))


def _fox_prep(hbuf, bf_vec, *, name):
    t = hbuf.shape[0]
    tm = _pick(t, (FOX_PREP_TM, 128))
    cs = COL_SMALL // LANES

    def body(s_ref, b_ref, eq_ref, ek_ref, carry):
        i = pl.program_id(0)

        @pl.when(i == 0)
        def _():
            carry[...] = jnp.zeros_like(carry)

        lf = _log_sigmoid(s_ref[...] + b_ref[...])
        f = _cumsum_rows(lf) + carry[0:1, :]
        carry[...] = jnp.broadcast_to(f[tm - 1:tm, :], carry.shape)
        lane = _iota((tm, LANES), 1)
        for h in range(ATT_HEADS):
            base = HEAD_DIM * (1 - h % 2)
            fh = _col(f, h)
            hi = fh.astype(BF16).astype(F32)
            mid = (fh - hi).astype(BF16).astype(F32)
            lo = fh - hi - mid
            terms = jnp.where(lane == base, hi, jnp.where(lane == base + 1, mid, jnp.where(lane == base + 2, lo, 0.0)))
            terms_k = jnp.where(lane == base + 3, -hi,
                                jnp.where(lane == base + 4, -mid, jnp.where(lane == base + 5, -lo, 0.0)))
            ones_q = ((lane >= base + 3) & (lane < base + 6)).astype(F32)
            ones_k = ((lane >= base) & (lane < base + 3)).astype(F32)
            eq_ref[:, LANES * h:LANES * (h + 1)] = (terms + ones_q).astype(BF16)
            ek_ref[:, LANES * h:LANES * (h + 1)] = (terms_k + ones_k).astype(BF16)

    ospec = pl.BlockSpec((tm, ATT_HEADS * LANES), lambda i: (i, 0))
    o = jax.ShapeDtypeStruct((t, ATT_HEADS * LANES), BF16)
    return pl.pallas_call(
        body, name=name, grid=(t // tm,),
        in_specs=[pl.BlockSpec((tm, LANES), lambda i: (i, cs)), pl.BlockSpec((1, LANES), lambda i: (0, 0))],
        out_specs=[ospec, ospec], out_shape=[o, o],
        scratch_shapes=[pltpu.VMEM((SUBLANES, LANES), F32)],
        compiler_params=_params(1),
    )(hbuf, bf_vec)


def _fox_post(dfc, hbuf, bf_vec, *, name):
    t = hbuf.shape[0]
    tm = _pick(t, (FOX_PREP_TM, 128))
    nb = t // tm
    cs = COL_SMALL // LANES

    def body(df_ref, s_ref, b_ref, o_ref, db_ref, carry):
        i = pl.program_id(0)

        @pl.when(i == 0)
        def _():
            carry[...] = jnp.zeros_like(carry)
            db_ref[...] = jnp.zeros_like(db_ref)

        dlf = _cumsum_rows(df_ref[...], reverse=True) + carry[0:1, :]
        carry[...] = jnp.broadcast_to(dlf[0:1, :], carry.shape)
        dl = dlf * _sigmoid(-(s_ref[...] + b_ref[...]))
        dl = jnp.where(_iota(dl.shape, 1) < ATT_HEADS, dl, 0.0)
        o_ref[...] = dl
        db_ref[...] += jnp.sum(dl, axis=0, keepdims=True)

    vec = pl.BlockSpec((1, LANES), lambda i: (0, 0))
    return pl.pallas_call(
        body, name=name, grid=(nb,),
        in_specs=[pl.BlockSpec((tm, LANES), lambda i: (nb - 1 - i, 0)),
                  pl.BlockSpec((tm, LANES), lambda i: (nb - 1 - i, cs)), vec],
        out_specs=[pl.BlockSpec((tm, LANES), lambda i: (nb - 1 - i, 0)), vec],
        out_shape=[jax.ShapeDtypeStruct((t, LANES), F32), jax.ShapeDtypeStruct((1, LANES), F32)],
        scratch_shapes=[pltpu.VMEM((SUBLANES, LANES), F32)],
        compiler_params=_params(1),
    )(dfc, hbuf, bf_vec)


def _fox_masks(i, j, tq):
    row = i * tq + _iota((tq, tq), 0)
    col = j * tq + _iota((tq, tq), 1)
    lane = _iota((1, LANES), 1)
    return col <= row, (lane < HEAD_DIM, lane >= HEAD_DIM)


def _hosting(body, n_in, n_out, n_scratch, comm, grid):
    na, no = len(comm.arrays), len(comm.out_shapes)

    def hosted(*refs):
        o0 = n_in + na
        s0 = o0 + n_out + no
        cargs = (refs[n_in:o0], refs[o0 + n_out:s0]) + tuple(refs[s0 + n_scratch:])
        a, b = pl.program_id(0), pl.program_id(1)

        @pl.when((a == 0) & (b == 0))
        def _():
            comm.start(*cargs)

        @pl.when((a == grid[0] - 1) & (b == 0))
        def _():
            comm.middle(*cargs)

        body(*refs[:n_in], *refs[o0:o0 + n_out], *refs[s0:s0 + n_scratch])

        @pl.when((a == grid[0] - 1) & (b == grid[1] - 1))
        def _():
            comm.finish(*cargs)

    return hosted


def _hosted_call(body, comm, grid, *, name, in_specs, out_specs, out_shape, scratch_shapes, args):
    n_out = len(out_shape)
    if comm is not None:
        cin, cout, sems = comm.specs()
        body = _hosting(body, len(in_specs), n_out, len(scratch_shapes), comm, grid)
        in_specs, out_specs = in_specs + cin, out_specs + cout
        out_shape, scratch_shapes, args = out_shape + comm.out_shapes, scratch_shapes + sems, args + list(comm.arrays)
    outs = pl.pallas_call(body, name=name, grid=grid, in_specs=in_specs, out_specs=out_specs,
                          out_shape=out_shape, scratch_shapes=scratch_shapes, compiler_params=_params(2))(*args)
    return outs[:n_out], outs[n_out:]


def _merge_comms(comms):
    comms = [c for c in comms if c is not None]
    if len(comms) <= 1:
        return comms[0] if comms else None

    def both(which):
        def run(ins, outs, ssem, rsem):
            ia = io = 0
            for c in comms:
                na, no = len(c.arrays), len(c.out_shapes)
                getattr(c, which)(ins[ia:ia + na], outs[io:io + no], ssem, rsem)
                ia, io = ia + na, io + no
        return run

    spans = sorted((c.base, c.base + c.n_own) for c in comms)
    assert all(a[1] <= b[0] for a, b in zip(spans, spans[1:])), "semaphore ranges overlap"
    return _Comm(sum((list(c.arrays) for c in comms), []), sum((list(c.out_shapes) for c in comms), []),
                 spans[-1][1], both("start"), both("finish"), middle=both("middle"))


def _fox_fwd(hbuf, eq, ek, *, comm=None, name):
    t = hbuf.shape[0]
    w = ATT_WIDTH
    tq = _pick(t, (FOX_T, 256, 128))
    nq = t // tq
    cq, ck, cv = COL_Q // w, COL_K // w, COL_V // w

    def body(q_ref, k_ref, v_ref, eq_ref, ek_ref, o_ref, lse_ref, m_s, l_s, acc_s):
        i = pl.program_id(0)
        j = pl.program_id(1)

        @pl.when(j == 0)
        def _():
            m_s[...] = jnp.full_like(m_s, NEG)
            l_s[...] = jnp.zeros_like(l_s)
            acc_s[...] = jnp.zeros_like(acc_s)

        def step(diagonal):
            _, hms = _fox_masks(i, j, tq)
            keys_first = (j * tq + _iota((tq, tq), 0)) <= (i * tq + _iota((tq, tq), 1))
            half = _iota((LANES, 1), 0)
            hrows = (half < HEAD_DIM, half >= HEAD_DIM)
            m_all = m_s[...]
            l_all = l_s[...]
            acc_old = [acc_s[LANES * pr:LANES * (pr + 1), :] for pr in range(2)]
            m_out, l_out, acc_out = [], [], []
            for pr in range(2):
                sl = slice(LANES * pr, LANES * (pr + 1))
                qp = q_ref[:, sl]
                kp = k_ref[:, sl]
                vt = v_ref[:, sl].T.astype(BF16)
                acc = acc_old[pr]
                for hh in range(2):
                    h = 2 * pr + hh
                    hsl = slice(LANES * h, LANES * (h + 1))
                    qm = jnp.where(hms[hh], (qp * (HEAD_DIM ** -0.5)).astype(BF16), eq_ref[:, hsl])
                    km = jnp.where(hms[hh], kp.astype(BF16), ek_ref[:, hsl])
                    st = _dot(km, qm, 1, 1)
                    if diagonal:
                        st = jnp.where(keys_first, st, NEG)
                    m_old = m_all[h:h + 1, :]
                    m_new = jnp.maximum(m_old, jnp.max(st, axis=0, keepdims=True))
                    alpha = jnp.exp(m_old - m_new)
                    pt = jnp.exp(st - m_new)
                    l_out.append(alpha * l_all[h:h + 1, :] + jnp.sum(pt, axis=0, keepdims=True))
                    m_out.append(m_new)
                    pv = _dot(vt, pt.astype(BF16), 1, 0)
                    acc = jnp.where(hrows[hh], alpha * acc_old[pr] + pv, acc)
                acc_out.append(acc)
            for h in range(ATT_HEADS):
                m_s[h:h + 1, :] = m_out[h]
                l_s[h:h + 1, :] = l_out[h]
            for pr in range(2):
                acc_s[LANES * pr:LANES * (pr + 1), :] = acc_out[pr]

        @pl.when(j < i)
        def _():
            step(False)

        @pl.when(j == i)
        def _():
            step(True)
            half = _iota((LANES, 1), 0)
            l_all = l_s[...]
            for pr in range(2):
                acc = acc_s[LANES * pr:LANES * (pr + 1), :]
                o_t = jnp.where(half < HEAD_DIM, acc / l_all[2 * pr:2 * pr + 1, :], acc / l_all[2 * pr + 1:2 * pr + 2, :])
                o_ref[:, LANES * pr:LANES * (pr + 1)] = o_t.T
            lse = m_s[...] + jnp.log(l_s[...])
            lse_ref[...] = jnp.where(_iota(lse.shape, 0) < ATT_HEADS, lse, 0.0)

    return _hosted_call(
        body, comm, (nq, nq), name=name,
        in_specs=[pl.BlockSpec((tq, w), lambda i, j: (i, cq)),
                  pl.BlockSpec((tq, w), lambda i, j: (jnp.minimum(j, i), ck)),
                  pl.BlockSpec((tq, w), lambda i, j: (jnp.minimum(j, i), cv)),
                  pl.BlockSpec((tq, ATT_HEADS * LANES), lambda i, j: (i, 0)),
                  pl.BlockSpec((tq, ATT_HEADS * LANES), lambda i, j: (jnp.minimum(j, i), 0))],
        out_specs=[pl.BlockSpec((tq, w), lambda i, j: (i, 0)),
                   pl.BlockSpec((SUBLANES, tq), lambda i, j: (0, i))],
        out_shape=[jax.ShapeDtypeStruct((t, w), F32), jax.ShapeDtypeStruct((SUBLANES, t), F32)],
        scratch_shapes=[pltpu.VMEM((SUBLANES, tq), F32), pltpu.VMEM((SUBLANES, tq), F32),
                        pltpu.VMEM((w, tq), F32)],
        args=[hbuf, hbuf, hbuf, eq, ek])


def _fox_delta(dymix, o, *, name):
    t, w = o.shape
    tm = _pick(t, (512, 256, 128))
    cdo = ATT_WIDTH // w

    def body(do_ref, o_ref, d_ref):
        d_ref[...] = _head_reduce(do_ref[...] * o_ref[...], 0, ATT_HEADS)

    return pl.pallas_call(
        body, name=name, grid=(t // tm,),
        in_specs=[pl.BlockSpec((tm, w), lambda i: (i, cdo)), pl.BlockSpec((tm, w), lambda i: (i, 0))],
        out_specs=pl.BlockSpec((tm, LANES), lambda i: (i, 0)),
        out_shape=jax.ShapeDtypeStruct((t, LANES), F32),
        compiler_params=_params(1),
    )(dymix, o)


def _fox_bwd(hbuf, eq, ek, dymix, lse_rows, delta_rows, *, comm=None, name):
    t = hbuf.shape[0]
    w = ATT_WIDTH
    tq = _pick(t, (FOX_T, 256, 128))
    nq = t // tq
    cq, ck, cv = COL_Q // w, COL_K // w, COL_V // w
    cdo = ATT_WIDTH // w

    def body(q_ref, k_ref, v_ref, eq_ref, ek_ref, do_ref, lse_ref, dl_ref, dk_ref, dv_ref, dfk_ref, dqt_ref, dfq_ref,
             dk_s, dv_s, dfk_s):
        j = pl.program_id(0)
        i = pl.program_id(1)

        @pl.when((i == 0) & (j == 0))
        def _():
            dqt_ref[...] = jnp.zeros_like(dqt_ref)
            dfq_ref[...] = jnp.zeros_like(dfq_ref)

        @pl.when(i == 0)
        def _():
            dk_s[...] = jnp.zeros_like(dk_s)
            dv_s[...] = jnp.zeros_like(dv_s)
            dfk_s[...] = jnp.zeros_like(dfk_s)

        def step(diagonal):
            _, hms = _fox_masks(i, j, tq)
            keys_first = (j * tq + _iota((tq, tq), 0)) <= (i * tq + _iota((tq, tq), 1))
            half = _iota((LANES, 1), 0)
            hrows = (half < HEAD_DIM, half >= HEAD_DIM)
            lse_all = lse_ref[...]
            dl_all = dl_ref[...]
            dvs, dks, dfks, dqts, dfqs = [], [], [], [], []
            for pr in range(2):
                sl = slice(LANES * pr, LANES * (pr + 1))
                qp = q_ref[:, sl]
                kp = k_ref[:, sl]
                kt = kp.T.astype(BF16)
                vpb = v_ref[:, sl].astype(BF16)
                dop = do_ref[:, sl]
                dv_p = jnp.zeros((tq, LANES), F32)
                dk_p = jnp.zeros((tq, LANES), F32)
                dqt_p = jnp.zeros((LANES, tq), F32)
                for hh in range(2):
                    h = 2 * pr + hh
                    hsl = slice(LANES * h, LANES * (h + 1))
                    qm = jnp.where(hms[hh], (qp * (HEAD_DIM ** -0.5)).astype(BF16), eq_ref[:, hsl])
                    km = jnp.where(hms[hh], kp.astype(BF16), ek_ref[:, hsl])
                    st = _dot(km, qm, 1, 1)
                    if diagonal:
                        st = jnp.where(keys_first, st, NEG)
                    pt = jnp.exp(st - lse_all[h:h + 1, :])
                    domb = jnp.where(hms[hh], dop, 0.0).astype(BF16)
                    dv_p = dv_p + _dot(pt.astype(BF16), domb, 1, 0)
                    dpt = _dot(vpb, domb, 1, 1)
                    dst = pt * (dpt - dl_all[h:h + 1, :])
                    dstb = dst.astype(BF16)
                    dk_p = dk_p + jnp.where(hms[hh], _dot(dstb, qm, 1, 0), 0.0)
                    dqt_p = dqt_p + _dot(jnp.where(hrows[hh], kt, 0.0), dstb, 1, 0)
                    part = dst[:, 0:LANES]
                    for c in range(1, tq // LANES):
                        part = part + dst[:, LANES * c:LANES * (c + 1)]
                    dfks.append(part)
                    dfqs.append(jnp.sum(dst, axis=0, keepdims=True))
                dvs.append(dv_p)
                dks.append(dk_p)
                dqts.append(dqt_p)
            dv_s[...] += jnp.concatenate(dvs, axis=1)
            dk_s[...] += jnp.concatenate(dks, axis=1)
            for h in range(ATT_HEADS):
                dfk_s[h] += dfks[h]
            cols = pl.ds(pl.multiple_of(i * tq, tq), tq)
            dqt_ref[:, cols] += jnp.concatenate(dqts, axis=0) * (HEAD_DIM ** -0.5)
            dfq_ref[:, cols] += jnp.concatenate(dfqs + [jnp.zeros((SUBLANES - ATT_HEADS, tq), F32)], axis=0)

        @pl.when(i > j)
        def _():
            step(False)

        @pl.when(i == j)
        def _():
            step(True)

        @pl.when(i == nq - 1)
        def _():
            dk_ref[...] = dk_s[...].astype(BF16)
            dv_ref[...] = dv_s[...].astype(BF16)
            lane = _iota((tq, LANES), 1)
            out = jnp.zeros((tq, LANES), F32)
            for h in range(ATT_HEADS):
                out = jnp.where(lane == h, jnp.sum(dfk_s[h], axis=1, keepdims=True), out)
            dfk_ref[...] = out

    qi = lambda j, i: jnp.maximum(i, j)
    rows = pl.BlockSpec((SUBLANES, tq), lambda j, i: (0, qi(j, i)))
    return _hosted_call(
        body, comm, (nq, nq), name=name,
        in_specs=[pl.BlockSpec((tq, w), lambda j, i: (qi(j, i), cq)),
                  pl.BlockSpec((tq, w), lambda j, i: (j, ck)),
                  pl.BlockSpec((tq, w), lambda j, i: (j, cv)),
                  pl.BlockSpec((tq, ATT_HEADS * LANES), lambda j, i: (qi(j, i), 0)),
                  pl.BlockSpec((tq, ATT_HEADS * LANES), lambda j, i: (j, 0)),
                  pl.BlockSpec((tq, w), lambda j, i: (qi(j, i), cdo)),
                  rows, rows],
        out_specs=[pl.BlockSpec((tq, w), lambda j, i: (j, 0)), pl.BlockSpec((tq, w), lambda j, i: (j, 0)),
                   pl.BlockSpec((tq, LANES), lambda j, i: (j, 0)),
                   pl.BlockSpec((w, t), lambda j, i: (0, 0)), pl.BlockSpec((SUBLANES, t), lambda j, i: (0, 0))],
        out_shape=[jax.ShapeDtypeStruct((t, w), BF16), jax.ShapeDtypeStruct((t, w), BF16),
                   jax.ShapeDtypeStruct((t, LANES), F32),
                   jax.ShapeDtypeStruct((w, t), F32), jax.ShapeDtypeStruct((SUBLANES, t), F32)],
        scratch_shapes=[pltpu.VMEM((tq, w), F32), pltpu.VMEM((tq, w), F32),
                        pltpu.VMEM((ATT_HEADS, tq, LANES), F32)],
        args=[hbuf, hbuf, hbuf, eq, ek, dymix, lse_rows, delta_rows])


GROUP_W = SSD_WIDTH // SSD_GROUPS
HEADS_PER_GROUP = SSD_HEADS // SSD_GROUPS


def _ssd_chunk_common(xr, prev8, sm, cw, cb, dtb, avec):
    c = _conv_taps(xr, prev8, cw, cb)
    sig = _sigmoid(c)
    xa = c * sig
    dt = _softplus(sm + dtb)
    a = dt * avec
    acum = _cumsum_rows(a)
    return c, sig, xa, dt, acum


def _ssd_head_cols(acum, acum_t):
    cols = [_col(acum, LANE_DT + h) for h in range(SSD_HEADS)]
    rows = [_row(acum_t, LANE_DT + h) for h in range(SSD_HEADS)]
    return cols, rows


def _expand_heads(vals, width):
    rows = vals[0].shape[0]
    colhead = _iota((rows, width), 1) // HEAD_DIM
    out = jnp.broadcast_to(vals[0], (rows, width))
    for h in range(1, len(vals)):
        out = jnp.where(colhead == h, vals[h], out)
    return out


def _ssd_decays(cols, g):
    mine = cols[HEADS_PER_GROUP * g:HEADS_PER_GROUP * (g + 1)]
    n = mine[0].shape[0]
    atots = [c[n - 1:n, :] for c in mine]
    e = _expand_heads([jnp.exp(c) for c in mine], GROUP_W)
    dec = _expand_heads([jnp.exp(t - c) for c, t in zip(mine, atots)], GROUP_W)
    etot = _expand_heads([jnp.exp(t) for t in atots], GROUP_W)
    return e, dec, etot


def _ssd_ldec(cols, rows, h, tril):
    return jnp.exp(jnp.where(tril, cols[h] - rows[h], NEG))


def _ssd_fwd(hbuf, conv_w, conv_b, dtb_vec, a_vec, d_exp, norm_g, *, name):
    t = hbuf.shape[0]
    L = SSD_CHUNK
    nc = t // L
    hb = L // SUBLANES
    cs = COL_SMALL // LANES
    cz = COL_Z // SSD_WIDTH

    def body(x_ref, xp_ref, z_ref, s_ref, cw_ref, cb_ref, dtb_ref, av_ref, dx_ref, ng_ref,
             yc_ref, y_ref, st_ref, state):
        i = pl.program_id(0)

        @pl.when(i == 0)
        def _():
            state[...] = jnp.zeros_like(state)

        prev = jnp.where(i == 0, 0.0, xp_ref[...])
        _, _, xa, dt, acum = _ssd_chunk_common(x_ref[...], prev, s_ref[...], cw_ref[...], cb_ref[...],
                                               dtb_ref[...], av_ref[...])
        cols, rows = _ssd_head_cols(acum, acum.T)
        xs = xa[:, :SSD_WIDTH]
        xdt = xs * _head_expand(dt, LANE_DT, SSD_HEADS, SSD_WIDTH)
        tril = _iota((L, L), 0) >= _iota((L, L), 1)
        lane = _iota((1, LANES), 1)
        ys = []
        for g in range(SSD_GROUPS):
            bg = xa[:, SSD_WIDTH + SSD_STATE * g:SSD_WIDTH + SSD_STATE * (g + 1)].astype(BF16)
            cg = xa[:, SSD_WIDTH + SSD_STATE * (SSD_GROUPS + g):SSD_WIDTH + SSD_STATE * (SSD_GROUPS + g + 1)].astype(BF16)
            gm = _dot(cg, bg, 1, 1)
            e, dec, etot = _ssd_decays(cols, g)
            s_in = state[g]
            st_ref[0, g] = s_in
            xg = xdt[:, GROUP_W * g:GROUP_W * (g + 1)]
            y_off = e * _dot(cg, s_in.astype(BF16), 1, 0)
            state[g] = etot * s_in + _dot(bg, (dec * xg).astype(BF16), 0, 0)
            for pr in range(2):
                xp = xg[:, LANES * pr:LANES * (pr + 1)].astype(BF16)
                outs = []
                for hh in range(2):
                    h = HEADS_PER_GROUP * g + 2 * pr + hh
                    m = gm * _ssd_ldec(cols, rows, h, tril)
                    outs.append(_dot(m.astype(BF16), xp, 1, 0))
                ys.append(jnp.where(lane < HEAD_DIM, outs[0], outs[1]) + y_off[:, LANES * pr:LANES * (pr + 1)])
        y = jnp.concatenate(ys, axis=1)
        y_ref[...] = y
        yd = y + dx_ref[...] * xs
        zz = z_ref[...]
        y2 = yd * zz * _sigmoid(zz)
        ng = ng_ref[...]
        outs = []
        for g in range(SSD_GROUPS):
            yg = y2[:, GROUP_W * g:GROUP_W * (g + 1)]
            rs = lax.rsqrt(jnp.mean(yg * yg, axis=1, keepdims=True) + RMS_EPS)
            outs.append(yg * rs * ng[:, GROUP_W * g:GROUP_W * (g + 1)])
        yc_ref[...] = jnp.concatenate(outs, axis=1)

    cdim = SSD_CONV_DIM
    vecc = pl.BlockSpec((1, cdim), lambda i: (0, 0))
    vecl = pl.BlockSpec((1, LANES), lambda i: (0, 0))
    vecw = pl.BlockSpec((1, SSD_WIDTH), lambda i: (0, 0))
    roww = pl.BlockSpec((L, SSD_WIDTH), lambda i: (i, 0))
    return pl.pallas_call(
        body, name=name, grid=(nc,),
        in_specs=[pl.BlockSpec((L, cdim), lambda i: (i, 0)),
                  pl.BlockSpec((SUBLANES, cdim), lambda i: (jnp.maximum(i * hb - 1, 0), 0)),
                  pl.BlockSpec((L, SSD_WIDTH), lambda i: (i, cz)),
                  pl.BlockSpec((L, LANES), lambda i: (i, cs)),
                  pl.BlockSpec((CONV_K, cdim), lambda i: (0, 0)), vecc, vecl, vecl, vecw, vecw],
        out_specs=[roww, roww, pl.BlockSpec((1, SSD_GROUPS, SSD_STATE, GROUP_W), lambda i: (i, 0, 0, 0))],
        out_shape=[jax.ShapeDtypeStruct((t, SSD_WIDTH), F32), jax.ShapeDtypeStruct((t, SSD_WIDTH), F32),
                   jax.ShapeDtypeStruct((nc, SSD_GROUPS, SSD_STATE, GROUP_W), F32)],
        scratch_shapes=[pltpu.VMEM((SSD_GROUPS, SSD_STATE, GROUP_W), F32)],
        compiler_params=_params(1),
    )(hbuf, hbuf, hbuf, hbuf, conv_w, conv_b, dtb_vec, a_vec, d_exp, norm_g)


def _ssd_bwd(dymix, hbuf, y_ssd, states, conv_w, conv_b, dtb_vec, a_vec, d_exp, norm_g, *, name):
    t = hbuf.shape[0]
    L = SSD_CHUNK
    nc = t // L
    hb = L // SUBLANES
    cs = COL_SMALL // LANES
    cz = COL_Z // SSD_WIDTH
    cdy = (LRU_WIDTH + ATT_WIDTH) // SSD_WIDTH
    cdim = SSD_CONV_DIM

    def body(dyc_ref, x_ref, xp_ref, z_ref, s_ref, y_ref, st_ref, cw_ref, cb_ref, dtb_ref, av_ref, dx_ref, ng_ref,
             dxr_ref, dz_ref, dsm_ref, dng_ref, dd_ref, da_ref, ddtb_ref, dcw_ref, dcb_ref,
             dstate, dnext):
        i = pl.program_id(0)
        ic = nc - 1 - i

        @pl.when(i == 0)
        def _():
            dstate[...] = jnp.zeros_like(dstate)
            dnext[...] = jnp.zeros_like(dnext)
            for ref in (dng_ref, dd_ref, da_ref, ddtb_ref, dcw_ref, dcb_ref):
                ref[...] = jnp.zeros_like(ref)

        xr = x_ref[...]
        sm = s_ref[...]
        prev = jnp.where(ic == 0, 0.0, xp_ref[...])
        avec = av_ref[...]
        c, sig, xa, dt, acum = _ssd_chunk_common(xr, prev, sm, cw_ref[...], cb_ref[...], dtb_ref[...], avec)
        cols, rows = _ssd_head_cols(acum, acum.T)
        xs = xa[:, :SSD_WIDTH]
        dtx = _head_expand(dt, LANE_DT, SSD_HEADS, SSD_WIDTH)
        xdt = xs * dtx
        tril = _iota((L, L), 0) >= _iota((L, L), 1)
        lane = _iota((1, LANES), 1)
        hmasks = (lane < HEAD_DIM, lane >= HEAD_DIM)

        y = y_ref[...]
        dexp = dx_ref[...]
        yd = y + dexp * xs
        zz = z_ref[...]
        sz = _sigmoid(zz)
        siluz = zz * sz
        y2 = yd * siluz
        ng = ng_ref[...]
        dyc = dyc_ref[...]
        dy2s, dngs = [], []
        for g in range(SSD_GROUPS):
            sl = slice(GROUP_W * g, GROUP_W * (g + 1))
            yg = y2[:, sl]
            rs = lax.rsqrt(jnp.mean(yg * yg, axis=1, keepdims=True) + RMS_EPS)
            wv = dyc[:, sl] * ng[:, sl]
            dngs.append(jnp.sum(dyc[:, sl] * yg * rs, axis=0, keepdims=True))
            dy2s.append(rs * wv - yg * (rs * rs * rs) * jnp.mean(wv * yg, axis=1, keepdims=True))
        dy2 = jnp.concatenate(dy2s, axis=1)
        dng_ref[...] += jnp.concatenate(dngs, axis=1)
        dz_ref[...] = (dy2 * yd * (sz * (1.0 + zz * (1.0 - sz)))).astype(BF16)
        dy = dy2 * siluz
        dd_ref[...] += jnp.sum(dy * xs, axis=0, keepdims=True)

        dxs, dbs, dcs = [], [], []
        datot = jnp.zeros((1, LANES), F32)
        lanes = _iota((L, LANES), 1)
        dacum = jnp.zeros((L, LANES), F32)
        for g in range(SSD_GROUPS):
            sl = slice(GROUP_W * g, GROUP_W * (g + 1))
            bg = xa[:, SSD_WIDTH + SSD_STATE * g:SSD_WIDTH + SSD_STATE * (g + 1)].astype(BF16)
            cg = xa[:, SSD_WIDTH + SSD_STATE * (SSD_GROUPS + g):SSD_WIDTH + SSD_STATE * (SSD_GROUPS + g + 1)].astype(BF16)
            gm = _dot(cg, bg, 1, 1)
            e, dec, etot = _ssd_decays(cols, g)
            s_in = st_ref[0, g]
            ds_out = dstate[g]
            dyg = dy[:, sl]
            xg = xdt[:, sl]
            edy = (e * dyg).astype(BF16)
            dstate[g] = etot * ds_out + _dot(cg, edy, 0, 0)
            dx_state = dec * _dot(bg, ds_out.astype(BF16), 1, 0)
            y_off = e * _dot(cg, s_in.astype(BF16), 1, 0)
            dacum = dacum + _head_reduce_group(dyg * y_off - xg * dx_state, g)
            dc_off = _dot(edy, s_in.astype(BF16), 1, 1)
            db_state = _dot((dec * xg).astype(BF16), ds_out.astype(BF16), 1, 1)
            dgsum = jnp.zeros((L, L), F32)
            dx_pairs = []
            for pr in range(2):
                psl = slice(LANES * pr, LANES * (pr + 1))
                xp = xg[:, psl]
                dyp = dyg[:, psl]
                dx_pair = jnp.zeros((L, LANES), F32)
                for hh in range(2):
                    h = HEADS_PER_GROUP * g + 2 * pr + hh
                    ldec = _ssd_ldec(cols, rows, h, tril)
                    dym = jnp.where(hmasks[hh], dyp, 0.0).astype(BF16)
                    xm = jnp.where(hmasks[hh], xp, 0.0).astype(BF16)
                    dx_pair = dx_pair + _dot((gm * ldec).astype(BF16), dym, 0, 0)
                    dml = _dot(dym, xm, 1, 1) * ldec
                    dgsum = dgsum + dml
                    qm = dml * gm
                    seg = jnp.sum(qm, axis=1, keepdims=True) - jnp.sum(qm.T, axis=1, keepdims=True)
                    dacum = dacum + jnp.where(lanes == LANE_DT + h, seg, 0.0)
                dx_pairs.append(dx_pair)
            dgb = dgsum.astype(BF16)
            dcs.append(_dot(dgb, bg, 1, 0) + dc_off)
            dbs.append(_dot(dgb, cg, 0, 0) + db_state)
            dxg = jnp.concatenate(dx_pairs, axis=1) + dx_state
            dxs.append(dxg)
            v = jnp.sum(dx_state * xg, axis=0, keepdims=True) + etot * jnp.sum(ds_out * s_in, axis=0, keepdims=True)
            datot = datot + _head_reduce_row(v, LANE_DT + HEADS_PER_GROUP * g, HEADS_PER_GROUP)
        dx = jnp.concatenate(dxs, axis=1)
        dacum = dacum + jnp.where(_iota((L, LANES), 0) == L - 1, datot, 0.0)
        da = _cumsum_rows(dacum, reverse=True)
        ddt = da * avec + _head_reduce(dx * xs, LANE_DT, SSD_HEADS)
        da_ref[...] += jnp.sum(da * dt, axis=0, keepdims=True)
        ddt_raw = ddt * _sigmoid(sm + dtb_ref[...])
        ddt_raw = jnp.where((lanes >= LANE_DT) & (lanes < LANE_DT + SSD_HEADS), ddt_raw, 0.0)
        dsm_ref[...] = ddt_raw
        ddtb_ref[...] += jnp.sum(ddt_raw, axis=0, keepdims=True)
        dxs_total = dx * dtx + dexp * dy
        dxa = jnp.concatenate([dxs_total] + dbs + dcs, axis=1)
        dc = dxa * (sig * (1.0 + c * (1.0 - sig)))
        dxr, dws = _conv_taps_bwd(dc, dnext[...], cw_ref[...], xr)
        dxr_ref[...] = dxr.astype(BF16)
        dcw_ref[...] += dws
        dcb_ref[...] += jnp.sum(dc, axis=0, keepdims=True)
        dnext[...] = dc[:SUBLANES]

    rev = lambda i: nc - 1 - i
    vecc = pl.BlockSpec((1, cdim), lambda i: (0, 0))
    vecl = pl.BlockSpec((1, LANES), lambda i: (0, 0))
    vecw = pl.BlockSpec((1, SSD_WIDTH), lambda i: (0, 0))
    cwspec = pl.BlockSpec((CONV_K, cdim), lambda i: (0, 0))
    roww = pl.BlockSpec((L, SSD_WIDTH), lambda i: (rev(i), 0))
    return pl.pallas_call(
        body, name=name, grid=(nc,),
        in_specs=[pl.BlockSpec((L, SSD_WIDTH), lambda i: (rev(i), cdy)),
                  pl.BlockSpec((L, cdim), lambda i: (rev(i), 0)),
                  pl.BlockSpec((SUBLANES, cdim), lambda i: (jnp.maximum(rev(i) * hb - 1, 0), 0)),
                  pl.BlockSpec((L, SSD_WIDTH), lambda i: (rev(i), cz)),
                  pl.BlockSpec((L, LANES), lambda i: (rev(i), cs)),
                  roww,
                  pl.BlockSpec((1, SSD_GROUPS, SSD_STATE, GROUP_W), lambda i: (rev(i), 0, 0, 0)),
                  cwspec, vecc, vecl, vecl, vecw, vecw],
        out_specs=[pl.BlockSpec((L, cdim), lambda i: (rev(i), 0)), roww,
                   pl.BlockSpec((L, LANES), lambda i: (rev(i), 0)),
                   vecw, vecw, vecl, vecl, cwspec, vecc],
        out_shape=[jax.ShapeDtypeStruct((t, cdim), BF16), jax.ShapeDtypeStruct((t, SSD_WIDTH), BF16),
                   jax.ShapeDtypeStruct((t, LANES), F32),
                   jax.ShapeDtypeStruct((1, SSD_WIDTH), F32), jax.ShapeDtypeStruct((1, SSD_WIDTH), F32),
                   jax.ShapeDtypeStruct((1, LANES), F32), jax.ShapeDtypeStruct((1, LANES), F32),
                   jax.ShapeDtypeStruct((CONV_K, cdim), F32), jax.ShapeDtypeStruct((1, cdim), F32)],
        scratch_shapes=[pltpu.VMEM((SSD_GROUPS, SSD_STATE, GROUP_W), F32), pltpu.VMEM((SUBLANES, cdim), F32)],
        compiler_params=_params(1),
    )(dymix, hbuf, hbuf, hbuf, hbuf, y_ssd, states, conv_w, conv_b, dtb_vec, a_vec, d_exp, norm_g)


def _head_reduce_group(x, g):
    return _head_reduce(x, LANE_DT + HEADS_PER_GROUP * g, HEADS_PER_GROUP)


def _head_reduce_row(v, lane0, nheads):
    colhead = _iota(v.shape, 1) // HEAD_DIM
    lane = _iota((1, LANES), 1)
    out = jnp.zeros((1, LANES), F32)
    for h in range(nheads):
        s = jnp.sum(jnp.where(colhead == h, v, 0.0), axis=1, keepdims=True)
        out = jnp.where(lane == lane0 + h, s, out)
    return out


def _exchange(inps, axes, *, swap=False, name):
    n = 2 ** len(axes)
    assert not swap or n == 2
    counts = [a.shape[0] for a in inps]
    out_shapes = [jax.ShapeDtypeStruct(a.shape if swap else (n,) + a.shape, a.dtype) for a in inps]
    units = sum(counts)
    na = len(inps)

    def body(*refs):
        in_refs, out_refs = refs[:na], refs[na:2 * na]
        send_sems, recv_sems, local_sems = refs[2 * na:]
        pos = {ax: lax.axis_index(ax) for ax in MESH_AXES}

        def slot_of(coord):
            s = 0
            for ax in axes:
                s = s * 2 + coord[ax]
            return s

        me = slot_of(pos)
        copies = []
        unit = 0
        for a in range(na):
            for it in range(counts[a]):
                dst = out_refs[a].at[it] if swap else out_refs[a].at[me, it]
                if not swap:
                    cp = pltpu.make_async_copy(in_refs[a].at[it], dst, local_sems.at[unit])
                    cp.start()
                    copies.append(cp)
                for delta in range(1, n):
                    coord = dict(pos)
                    for b, ax in enumerate(reversed(axes)):
                        if (delta >> b) & 1:
                            coord[ax] = 1 - pos[ax]
                    k = unit * (n - 1) + delta - 1
                    cp = pltpu.make_async_remote_copy(
                        src_ref=in_refs[a].at[it], dst_ref=dst,
                        send_sem=send_sems.at[k], recv_sem=recv_sems.at[k],
                        device_id=(coord["x"], coord["y"], coord["c"]), device_id_type=pl.DeviceIdType.MESH)
                    cp.start()
                    copies.append(cp)
                unit += 1
        for cp in copies:
            cp.wait()

    any_spec = pl.BlockSpec(memory_space=pl.ANY)
    return pl.pallas_call(
        body, name=name,
        in_specs=[any_spec] * na, out_specs=[any_spec] * na, out_shape=out_shapes,
        scratch_shapes=[pltpu.SemaphoreType.DMA((units * (n - 1),)), pltpu.SemaphoreType.DMA((units * (n - 1),)),
                        pltpu.SemaphoreType.DMA((units,))],
    )(*inps)


class _Comm:
    def __init__(self, arrays, out_shapes, n_own, start, finish, base=0, middle=None):
        self.arrays, self.out_shapes, self.start, self.finish = arrays, out_shapes, start, finish
        self.middle = middle or (lambda *refs: None)
        self.base, self.n_own, self.n_sems = base, n_own, base + n_own

    def specs(self):
        any_spec = pl.BlockSpec(memory_space=pl.ANY)
        sems = [pltpu.SemaphoreType.DMA((self.n_sems,)), pltpu.SemaphoreType.DMA((self.n_sems,))]
        return [any_spec] * len(self.arrays), [any_spec] * len(self.out_shapes), sems


def _run_comm(comm, *, name):
    na, no = len(comm.arrays), len(comm.out_shapes)

    def body(*refs):
        args = (refs[:na], refs[na:na + no]) + tuple(refs[na + no:])
        comm.start(*args)
        comm.middle(*args)
        comm.finish(*args)

    in_specs, out_specs, sems = comm.specs()
    return pl.pallas_call(body, name=name, in_specs=in_specs, out_specs=out_specs, out_shape=comm.out_shapes,
                          scratch_shapes=sems)(*comm.arrays)


def _chip_peer(x, y, d):
    px = 1 - x if d & 2 else x
    py = 1 - y if d & 1 else y
    return px, py, 2 * px + py


def _gather_layer_comm(srcs, li, base=0):
    counts = [s.shape[0] for s in srcs]
    units = [(a, it) for a in range(len(srcs)) for it in range(counts[a])]
    n_ici = 3 * len(units)
    out_shapes = [jax.ShapeDtypeStruct((N_CHIPS,) + s.shape, s.dtype) for s in srcs]

    def ici(ins, outs, ssem, rsem, u, d):
        x, y, c = (lax.axis_index(ax) for ax in MESH_AXES)
        a, it = units[u]
        px, py, _ = _chip_peer(x, y, d)
        k = base + 3 * u + d - 1
        return pltpu.make_async_remote_copy(
            src_ref=ins[a].at[it], dst_ref=outs[a].at[2 * x + y, it], send_sem=ssem.at[k], recv_sem=rsem.at[k],
            device_id=(px, py, c), device_id_type=pl.DeviceIdType.MESH)

    def arrived(ins, outs, ssem, rsem, u, d):
        x, y, c = (lax.axis_index(ax) for ax in MESH_AXES)
        a, it = units[u]
        _, _, pk = _chip_peer(x, y, d)
        k = base + 3 * u + d - 1
        return pltpu.make_async_remote_copy(
            src_ref=ins[a].at[it], dst_ref=outs[a].at[pk, it], send_sem=ssem.at[k], recv_sem=rsem.at[k],
            device_id=(x, y, c), device_id_type=pl.DeviceIdType.MESH)

    def forward(ins, outs, ssem, rsem, u, slot):
        x, y, c = (lax.axis_index(ax) for ax in MESH_AXES)
        a, it = units[u]
        pk = 2 * x + y if slot == 0 else _chip_peer(x, y, slot)[2]
        src = ins[a].at[it] if slot == 0 else outs[a].at[pk, it]
        k = base + n_ici + 4 * u + slot
        return pltpu.make_async_remote_copy(
            src_ref=src, dst_ref=outs[a].at[pk, it], send_sem=ssem.at[k], recv_sem=rsem.at[k],
            device_id=(x, y, 1 - c), device_id_type=pl.DeviceIdType.MESH)

    def start(ins, outs, ssem, rsem):
        for u in range(len(units)):
            forward(ins, outs, ssem, rsem, u, 0).start()

        @pl.when(lax.axis_index("c") == li)
        def _():
            for u in range(len(units)):
                for d in range(1, N_CHIPS):
                    ici(ins, outs, ssem, rsem, u, d).start()

    def middle(ins, outs, ssem, rsem):
        @pl.when(lax.axis_index("c") == li)
        def _():
            for u in range(len(units)):
                for d in range(1, N_CHIPS):
                    arrived(ins, outs, ssem, rsem, u, d).wait_recv()
                    forward(ins, outs, ssem, rsem, u, d).start()

    def finish(ins, outs, ssem, rsem):
        c = lax.axis_index("c")

        @pl.when(c == li)
        def _():
            for u in range(len(units)):
                for d in range(1, N_CHIPS):
                    ici(ins, outs, ssem, rsem, u, d).wait_send()
                    forward(ins, outs, ssem, rsem, u, d).wait_send()

        @pl.when(c != li)
        def _():
            for u in range(len(units)):
                for d in range(1, N_CHIPS):
                    forward(ins, outs, ssem, rsem, u, d).wait_recv()

        for u in range(len(units)):
            forward(ins, outs, ssem, rsem, u, 0).wait()

    return _Comm(srcs, out_shapes, n_ici + 4 * len(units), start, finish, base, middle)


def _reduce_chips_comm(sums, li, base=0):
    counts = [s.shape[0] for s in sums]
    units = [(a, it) for a in range(len(sums)) for it in range(counts[a])]
    out_shapes = [jax.ShapeDtypeStruct((N_CHIPS, s.shape[0]) + s.shape[2:], s.dtype) for s in sums]

    def copy(ins, outs, ssem, rsem, u, d):
        x, y, c = (lax.axis_index(ax) for ax in MESH_AXES)
        a, it = units[u]
        px, py, pk = _chip_peer(x, y, d)
        k = base + 3 * u + d - 1
        return pltpu.make_async_remote_copy(
            src_ref=ins[a].at[it, pk], dst_ref=outs[a].at[2 * x + y, it], send_sem=ssem.at[k], recv_sem=rsem.at[k],
            device_id=(px, py, c), device_id_type=pl.DeviceIdType.MESH)

    def start(ins, outs, ssem, rsem):
        @pl.when(lax.axis_index("c") == li)
        def _():
            for u in range(len(units)):
                for d in range(1, N_CHIPS):
                    copy(ins, outs, ssem, rsem, u, d).start()

    def finish(ins, outs, ssem, rsem):
        @pl.when(lax.axis_index("c") == li)
        def _():
            for u in range(len(units)):
                for d in range(1, N_CHIPS):
                    copy(ins, outs, ssem, rsem, u, d).wait()

    return _Comm(sums, out_shapes, 3 * len(units), start, finish, base)


def _sum_slots(buf, out_dtype, *, name):
    n, rows, cols = buf.shape
    tm = _pick(rows, (512, 256, 128, 8))
    if rows % tm:
        tm = rows

    def body(b_ref, o_ref):
        acc = b_ref[0].astype(F32)
        for s in range(1, n):
            acc = acc + b_ref[s].astype(F32)
        o_ref[...] = acc.astype(out_dtype)

    return pl.pallas_call(
        body, name=name, grid=(pl.cdiv(rows, tm),),
        in_specs=[pl.BlockSpec((n, tm, cols), lambda i: (0, i, 0))],
        out_specs=pl.BlockSpec((tm, cols), lambda i: (i, 0)),
        out_shape=jax.ShapeDtypeStruct((rows, cols), out_dtype),
        compiler_params=_params(1),
    )(buf)


def _sum_pair(a, b, out_dtype, *, name):
    shape = a.shape
    cols = shape[-1]
    a2, b2 = a.reshape(-1, cols), b.reshape(-1, cols)
    rows = a2.shape[0]
    tm = _pick(rows, (512, 256, 128, 8))

    def body(a_ref, b_ref, o_ref):
        o_ref[...] = (a_ref[...].astype(F32) + b_ref[...].astype(F32)).astype(out_dtype)

    spec = pl.BlockSpec((tm, cols), lambda i: (i, 0))
    return pl.pallas_call(
        body, name=name, grid=(rows // tm,), in_specs=[spec, spec], out_specs=spec,
        out_shape=jax.ShapeDtypeStruct((rows, cols), out_dtype), compiler_params=_params(1),
    )(a2, b2).reshape(shape)


def _adamw(w, g, m, v, *, name):
    shape = w.shape
    cols = shape[-1]
    rows = w.size // cols
    w2, g2, m2, v2 = (a.reshape(rows, cols) for a in (w, g, m, v))
    tm = _pick(rows, (256, 128, 64, 32, 16, 8))
    if rows % tm:
        tm = rows
    bc1 = 1.0 - ADAM_B1 ** ADAM_STEP
    bc2 = 1.0 - ADAM_B2 ** ADAM_STEP

    def body(w_ref, g_ref, m_ref, v_ref, d_ref, nm_ref, nv_ref):
        gg = g_ref[...]
        mm = ADAM_B1 * m_ref[...] + (1.0 - ADAM_B1) * gg
        vv = ADAM_B2 * v_ref[...] + (1.0 - ADAM_B2) * (gg * gg)
        m_hat = mm / bc1
        v_hat = vv / bc2
        d_ref[...] = -ADAM_LR * (m_hat / (jnp.sqrt(v_hat) + ADAM_EPS) + ADAM_WD * w_ref[...])
        nm_ref[...] = mm
        nv_ref[...] = vv

    spec = pl.BlockSpec((tm, cols), lambda i: (i, 0))
    o = jax.ShapeDtypeStruct((rows, cols), F32)
    outs = pl.pallas_call(
        body, name=name, grid=(rows // tm,), in_specs=[spec] * 4, out_specs=[spec] * 3, out_shape=[o] * 3,
        compiler_params=_params(1),
    )(w2, g2, m2, v2)
    return tuple(a.reshape(shape) for a in outs)


def _layer_fwd(li, x, xb, pb, W, up=None, att=None):
    nm = lambda s: f"l{li}_{s}"
    sv = {"x_in_b": xb}
    (g1, u1, a1), got = _mm_swiglu(xb, W["ffn1_wg"], W["ffn1_wu"], comm=up[0] if up else None, name=nm("ffn1_up"))
    if up:
        W = {**W, **up[1](got)}
    x1, x1b, xh1, rs1 = _mm_ln(a1, W["ffn1_wd"], x, W["ln1_g"], W["ln1_b"], rscale=ALPHA, mscale=0.5, name=nm("ffn1_down_ln"))
    hbuf = _mm(x1b, W["w_in_p"], name=nm("in_proj"))
    ya, lu, lr, lig, la, lh = _lru_fwd(hbuf, W["lru_conv_w"], W["lru_conv_b"], W["lru_wa_bd"], W["lru_ba"],
                                       W["lru_wx_bd"], W["lru_bx"], W["lru_lambda"], name=nm("lru_fwd"))
    eq, ek = _fox_prep(hbuf, W["fox_bf_vec"], name=nm("fox_prep"))
    (yb, lse_rows), got = _fox_fwd(hbuf, eq, ek, comm=att[0] if att else None, name=nm("fox_fwd"))
    if att:
        W = {**W, **att[1](got)}
    yc, yssd, states = _ssd_fwd(hbuf, W["ssd_conv_w"], W["ssd_conv_b"], W["ssd_dtb_vec"], W["ssd_a_vec"],
                                W["ssd_d_exp"], W["ssd_norm_g"], name=nm("ssd_fwd"))
    ymix = _assemble([ya, yb, yc], D_MODEL, name=nm("y_mix"))
    x2, x2b, xh2, rs2 = _mm_ln(ymix, W["w_out"], x1, W["ln2_g"], W["ln2_b"], rscale=ALPHA, mscale=1.0, name=nm("out_proj_ln"))
    (g2, u2, a2), _ = _mm_swiglu(x2b, W["ffn2_wg"], W["ffn2_wu"], name=nm("ffn2_up"))
    x3, x3b, xh3, rs3 = _mm_ln(a2, W["ffn2_wd"], x2, W["ln3_g"], W["ln3_b"], rscale=ALPHA, mscale=0.5, name=nm("ffn2_down_ln"))
    x4, x4b, sg, e = _mm_pe(x3, x3b, pb, W["pe_gate_w"], W["pe_gate_b"], W["pe_proj"], name=nm("ple"))
    sv.update(g1=g1, u1=u1, a1=a1, x1b=x1b, xh1=xh1, rs1=rs1, hbuf=hbuf, lu=lu, lr=lr, lig=lig, la=la, lh=lh,
              eq=eq, ek=ek, lse_rows=lse_rows, yb=yb, yssd=yssd, states=states, ymix=ymix, x2b=x2b, xh2=xh2, rs2=rs2,
              g2=g2, u2=u2, a2=a2, x3b=x3b, xh3=xh3, rs3=rs3, sg=sg, e=e, pb=pb)
    return x4, x4b, sv, W


def _layer_bwd(li, dx4, sv, W, comm=None, late=None, last=None):
    nm = lambda s: f"l{li}_{s}"
    G = {}
    dgp, de, dbg = _pe_bwd_elem(dx4, sv["sg"], sv["e"], name=nm("ple_bwd"))
    G["pe_gate_b"] = dbg
    G["pe_gate_w"] = _mm(sv["x3b"], dgp, ta=True, out_dtype=BF16, name=nm("d_pe_gate_w"))
    G["pe_proj"] = _mm(sv["pb"], de, ta=True, out_dtype=BF16, chip_cols=True, name=nm("d_pe_proj"))
    dr3, dr3b, G["ln3_g"], G["ln3_b"] = _bwd_proj([(dgp, W["pe_gate_w"])], dx4, rscale=1.0,
                                                  ln=(sv["xh3"], sv["rs3"], W["ln3_g"]), name=nm("ln3_bwd"))
    G["ffn2_wd"] = _mm(sv["a2"], dr3b, ta=True, scale=0.5, out_dtype=BF16, name=nm("d_ffn2_wd"))
    dg2, du2 = _mm_swiglu_bwd(dr3b, W["ffn2_wd"], sv["g2"], sv["u2"], scale=0.5, name=nm("ffn2_act_bwd"))
    G["ffn2_wg"] = _mm(sv["x2b"], dg2, ta=True, out_dtype=BF16, chip_cols=True, name=nm("d_ffn2_wg"))
    G["ffn2_wu"] = _mm(sv["x2b"], du2, ta=True, out_dtype=BF16, chip_cols=True, name=nm("d_ffn2_wu"))
    dr2, dr2b, G["ln2_g"], G["ln2_b"] = _bwd_proj([(dg2, W["ffn2_wg"]), (du2, W["ffn2_wu"])], dr3, rscale=ALPHA,
                                                  ln=(sv["xh2"], sv["rs2"], W["ln2_g"]), name=nm("ln2_bwd"))
    G["w_out"] = _mm(sv["ymix"], dr2b, ta=True, out_dtype=BF16, name=nm("d_w_out"))
    dymix = _mm(dr2b, W["w_out"], tb=True, name=nm("d_ymix"))
    hbuf = sv["hbuf"]
    (dur, dgr, G["lru_conv_w"], G["lru_conv_b"], G["lru_wa_bd"], G["lru_ba"], G["lru_wx_bd"], G["lru_bx"],
     G["lru_lambda"]) = _lru_bwd(dymix, hbuf, sv["lu"], sv["lr"], sv["lig"], sv["la"], sv["lh"],
                                 W["lru_conv_w"], W["lru_wa_bd"], W["lru_wx_bd"], W["lru_lambda"], name=nm("lru_bwd"))
    delta = _fox_delta(dymix, sv["yb"], name=nm("fox_delta"))
    delta_rows = jnp.pad(delta[:, :ATT_HEADS].T, ((0, SUBLANES - ATT_HEADS), (0, 0)))
    comm = _merge_comms([comm, late(G) if late else None])
    (dk, dv, dfk, dqt, dfq), comm_out = _fox_bwd(hbuf, sv["eq"], sv["ek"], dymix, sv["lse_rows"], delta_rows,
                                                 comm=comm, name=nm("fox_bwd"))
    dq = dqt.T
    dfc = jnp.pad(dfq[:ATT_HEADS].T, ((0, 0), (0, LANES - ATT_HEADS))) - dfk
    dsm_f, G["fox_bf_vec"] = _fox_post(dfc, hbuf, W["fox_bf_vec"], name=nm("fox_post"))
    (dxr, dz, dsm_dt, G["ssd_norm_g"], G["ssd_d_exp"], G["ssd_a_vec"], G["ssd_dtb_vec"], G["ssd_conv_w"],
     G["ssd_conv_b"]) = _ssd_bwd(dymix, hbuf, sv["yssd"], sv["states"], W["ssd_conv_w"], W["ssd_conv_b"],
                                 W["ssd_dtb_vec"], W["ssd_a_vec"], W["ssd_d_exp"], W["ssd_norm_g"], name=nm("ssd_bwd"))
    dh = _assemble([dxr, dz, dur, dgr, dq, dk, dv, dsm_f + dsm_dt], H_WIDTH, name=nm("d_h"))
    G["w_in_p"] = _mm(sv["x1b"], dh, ta=True, name=nm("d_w_in"))
    dr1, dr1b, G["ln1_g"], G["ln1_b"] = _bwd_proj([(dh, W["w_in_p"])], dr2, rscale=ALPHA,
                                                  ln=(sv["xh1"], sv["rs1"], W["ln1_g"]), name=nm("ln1_bwd"))
    G["ffn1_wd"] = _mm(sv["a1"], dr1b, ta=True, scale=0.5, out_dtype=BF16, name=nm("d_ffn1_wd"))
    dg1, du1 = _mm_swiglu_bwd(dr1b, W["ffn1_wd"], sv["g1"], sv["u1"], scale=0.5, name=nm("ffn1_act_bwd"))
    G["ffn1_wg"] = _mm(sv["x_in_b"], dg1, ta=True, out_dtype=BF16, chip_cols=True, name=nm("d_ffn1_wg"))
    G["ffn1_wu"] = _mm(sv["x_in_b"], du1, ta=True, out_dtype=BF16, chip_cols=True, name=nm("d_ffn1_wu"))
    dx_in, *last_out = _bwd_proj([(dg1, W["ffn1_wg"]), (du1, W["ffn1_wu"])], dr1, rscale=ALPHA, ln=None,
                                 comm=last(G) if last else None, name=nm("x_in_bwd"))
    return dx_in, G, comm_out, (last_out[0] if last_out else None)


def _block_diag(w):
    n, b, _ = w.shape
    eye = jnp.eye(n, dtype=w.dtype)
    return (eye[:, None, :, None] * w[:, :, None, :]).reshape(n * b, n * b)


def _block_diag_extract(m):
    n, b = LRU_HEADS, HEAD_DIM
    return jnp.stack([m[b * i:b * (i + 1), b * i:b * (i + 1)] for i in range(n)])


def _lane_vec(v, lane0):
    return jnp.pad(v.astype(F32), (lane0, LANES - lane0 - v.shape[0])).reshape(1, LANES)


def _w_in_permute(w):
    d = w.shape[0]
    z = lambda n: jnp.zeros((d, n), w.dtype)
    return jnp.concatenate([w[:, 1796:2820], w[:, 1284:1796], w[:, 0:512], w[:, 512:1280],
                            w[:, 1280:1284], w[:, 2820:2828], z(LANES - 12), z(H_WIDTH - COL_SMALL - LANES)], axis=1)


def _w_in_unpermute(wp):
    return jnp.concatenate([wp[:, COL_U:COL_Q], wp[:, COL_Q:COL_SMALL], wp[:, COL_SMALL:COL_SMALL + 4],
                            wp[:, COL_Z:COL_U], wp[:, COL_XBC:COL_Z], wp[:, COL_SMALL + 4:COL_SMALL + 12]], axis=1)


def _big_weights(chipw):
    W = {}
    for n, w in chipw.items():
        if n in ("ffn1_wg", "ffn1_wu", "ffn2_wg", "ffn2_wu"):
            W[n] = w
        elif n in ("ffn1_wd", "ffn2_wd", "w_out", "pe_gate_w"):
            W[n] = w.reshape(-1, D_MODEL)
        elif n == "pe_proj":
            W[n] = jnp.moveaxis(w, 0, 1).reshape(PLE_DIM, D_MODEL)
        else:
            w_in = jnp.moveaxis(w[:, :, :IN_WIDTH // N_CHIPS], 0, 1).reshape(D_MODEL, IN_WIDTH)
            W["w_in_p"] = _w_in_permute(w_in)
    return W


def _small_weights(li, small):
    g = lambda n: small[n][li]
    W = {n: g(n) for n in ("ln1_g", "ln1_b", "ln2_g", "ln2_b", "ln3_g", "ln3_b", "pe_gate_b", "lru_conv_w",
                           "ssd_conv_w")}
    for n in ("lru_conv_b", "lru_ba", "lru_bx", "lru_lambda", "ssd_conv_b", "ssd_norm_g"):
        W[n] = g(n).reshape(1, -1)
    W["lru_wa_bd"] = _block_diag(g("lru_wa")).astype(BF16)
    W["lru_wx_bd"] = _block_diag(g("lru_wx")).astype(BF16)
    W["fox_bf_vec"] = _lane_vec(g("fox_bf"), LANE_F)
    W["ssd_dtb_vec"] = _lane_vec(g("ssd_dt_bias"), LANE_DT)
    W["ssd_a_vec"] = _lane_vec(-jnp.exp(g("ssd_a_log")), LANE_DT)
    W["ssd_d_exp"] = jnp.repeat(g("ssd_d"), HEAD_DIM).reshape(1, SSD_WIDTH)
    return W


def _big_grad_by_chip(G, n):
    if n in ("ffn1_wg", "ffn1_wu", "ffn2_wg", "ffn2_wu", "pe_proj"):
        return G[n]
    if n in ("ffn1_wd", "ffn2_wd", "w_out", "pe_gate_w"):
        return G[n].reshape(N_CHIPS, -1, D_MODEL)
    share = IN_WIDTH // N_CHIPS
    d_w_in = jnp.moveaxis(_w_in_unpermute(G["w_in_p"]).reshape(D_MODEL, N_CHIPS, share), 1, 0)
    return jnp.pad(d_w_in.astype(BF16), ((0, 0), (0, 0), (0, SHARE - share)))


def _layer_small_grads(G, W):
    out = {n: G[n] for n in ("lru_conv_w", "ssd_conv_w")}
    for n in ("ln1_g", "ln1_b", "ln2_g", "ln2_b", "ln3_g", "ln3_b", "pe_gate_b", "lru_conv_b", "lru_ba", "lru_bx",
              "lru_lambda", "ssd_conv_b", "ssd_norm_g"):
        out[n] = G[n].reshape(-1)
    out["lru_wa"] = _block_diag_extract(G["lru_wa_bd"])
    out["lru_wx"] = _block_diag_extract(G["lru_wx_bd"])
    out["fox_bf"] = G["fox_bf_vec"][0, LANE_F:LANE_F + ATT_HEADS]
    out["ssd_dt_bias"] = G["ssd_dtb_vec"][0, LANE_DT:LANE_DT + SSD_HEADS]
    out["ssd_a_log"] = G["ssd_a_vec"][0, LANE_DT:LANE_DT + SSD_HEADS] * W["ssd_a_vec"][0, LANE_DT:LANE_DT + SSD_HEADS]
    out["ssd_d"] = G["ssd_d_exp"].reshape(SSD_HEADS, HEAD_DIM).sum(axis=1)
    return out


WEIGHTS = ['ln1_g', 'ln1_b', 'ffn1_wg', 'ffn1_wu', 'ffn1_wd', 'w_in', 'lru_conv_w', 'lru_conv_b', 'lru_wa', 'lru_ba',
           'lru_wx', 'lru_bx', 'lru_lambda', 'fox_bf', 'ssd_conv_w', 'ssd_conv_b', 'ssd_dt_bias', 'ssd_a_log', 'ssd_d',
           'ssd_norm_g', 'w_out', 'ln2_g', 'ln2_b', 'ffn2_wg', 'ffn2_wu', 'ffn2_wd', 'ln3_g', 'ln3_b', 'pe_proj',
           'pe_gate_w', 'pe_gate_b']
FIRST = ((("ffn1_wg",), 1), (("ffn1_wu",), 1))
NEXT = ((("w_in",), 1),
        (("ffn1_wd",), 0))
EARLY = FIRST + NEXT
LATE = ((("ffn2_wg",), 1), (("ffn2_wu",), 1),
        (("ffn2_wd",), 0),
        (("w_out",), None), (("pe_gate_w",), None),
        (("pe_proj",), None))
BIG = {n: pad for names, pad in EARLY + LATE for n in names}
SMALL_SHARDED = {'lru_conv_w': 2, 'ssd_conv_w': 2}


def _unshard(seg, axis):
    moved = jnp.moveaxis(seg, 0, axis)
    shp = list(moved.shape)
    shp[axis:axis + 2] = [shp[axis] * shp[axis + 1]]
    return moved.reshape(shp)


def _pad_axis(a, axis, size):
    if axis is None or a.shape[axis] == size:
        return a
    pads = [(0, 0)] * a.ndim
    pads[axis] = (0, size - a.shape[axis])
    return jnp.pad(a, pads)


PACK_TILE = SUBLANES * LANES


def _pack(arrs):
    rows = []
    for a in arrs:
        flat = a.astype(F32).reshape(-1)
        rows.append(jnp.pad(flat, (0, (-flat.shape[0]) % PACK_TILE)).reshape(-1, LANES))
    return jnp.concatenate(rows, axis=0)


def _unpack(packed, shapes):
    out, off = [], 0
    for s in shapes:
        n = math.prod(s)
        r = -(-n // PACK_TILE) * SUBLANES
        out.append(packed[off:off + r].reshape(-1)[:n].reshape(s))
        off += r
    return out


def kernel(x, p, ln1_g, ln1_b, ffn1_wg, ffn1_wu, ffn1_wd, w_in, lru_conv_w, lru_conv_b, lru_wa, lru_ba, lru_wx, lru_bx, lru_lambda, fox_bf, ssd_conv_w, ssd_conv_b, ssd_dt_bias, ssd_a_log, ssd_d, ssd_norm_g, w_out, ln2_g, ln2_b, ffn2_wg, ffn2_wu, ffn2_wd, ln3_g, ln3_b, pe_proj, pe_gate_w, pe_gate_b, loss_target, m_ln1_g, m_ln1_b, m_ffn1_wg, m_ffn1_wu, m_ffn1_wd, m_w_in, m_lru_conv_w, m_lru_conv_b, m_lru_wa, m_lru_ba, m_lru_wx, m_lru_bx, m_lru_lambda, m_fox_bf, m_ssd_conv_w, m_ssd_conv_b, m_ssd_dt_bias, m_ssd_a_log, m_ssd_d, m_ssd_norm_g, m_w_out, m_ln2_g, m_ln2_b, m_ffn2_wg, m_ffn2_wu, m_ffn2_wd, m_ln3_g, m_ln3_b, m_pe_proj, m_pe_gate_w, m_pe_gate_b, v_ln1_g, v_ln1_b, v_ffn1_wg, v_ffn1_wu, v_ffn1_wd, v_w_in, v_lru_conv_w, v_lru_conv_b, v_lru_wa, v_lru_ba, v_lru_wx, v_lru_bx, v_lru_lambda, v_fox_bf, v_ssd_conv_w, v_ssd_conv_b, v_ssd_dt_bias, v_ssd_a_log, v_ssd_d, v_ssd_norm_g, v_w_out, v_ln2_g, v_ln2_b, v_ffn2_wg, v_ffn2_wu, v_ffn2_wd, v_ln3_g, v_ln3_b, v_pe_proj, v_pe_gate_w, v_pe_gate_b):
    args = locals()
    w_loc = {n: args[n] for n in WEIGHTS}
    m_loc = {n: args["m_" + n] for n in WEIGHTS}
    v_loc = {n: args["v_" + n] for n in WEIGHTS}
    chip = 2 * lax.axis_index("x") + lax.axis_index("y")
    core = lax.axis_index("c")
    big = list(BIG)
    small_sh = list(SMALL_SHARDED)
    small_rep = [n for n in WEIGHTS if n not in BIG and n not in SMALL_SHARDED]

    def srcs_of(li, groups):
        return [jnp.stack([_pad_axis(w_loc[n][li].astype(BF16), pad, SHARE) for n in names]) for names, pad in groups]

    def gather_comm(li, groups, base=0):
        return _gather_layer_comm(srcs_of(li, groups), li, base)

    def chip_weights(gathered, groups):
        return _big_weights({n: g[:, j] for (names, _), g in zip(groups, gathered) for j, n in enumerate(names)})

    def pair_sums(G, groups, tag):
        gs = [jnp.stack([_big_grad_by_chip(G, n) for n in names]) for names, _ in groups]
        flat = [g.reshape((-1,) + g.shape[2:]) for g in gs]
        theirs = _exchange(flat, ("c",), swap=True, name=f"reduce_cores_{tag}")
        return [_sum_pair(f, r, BF16, name=f"reduce_cores_sum_{tag}_{gi}").reshape(g.shape)
                for gi, (f, r, g) in enumerate(zip(flat, theirs, gs))]

    def finish_reduce(quad, sums, li, groups, tag):
        quad = [lax.dynamic_update_index_in_dim(q, lax.dynamic_index_in_dim(s, chip, 1, keepdims=False), chip, 0)
                for q, s in zip(quad, sums)]
        red = [_sum_slots(q.reshape(N_CHIPS, -1, q.shape[-1]), F32,
                          name=f"reduce_chips_sum_{tag}_{gi}").reshape(q.shape[1:]) for gi, q in enumerate(quad)]
        theirs = _exchange(red, ("c",), swap=True, name=f"reduce_share_{tag}")
        out = {}
        for (names, _), r, rv in zip(groups, red, theirs):
            both = jnp.where(core == li, r, rv)
            for j, n in enumerate(names):
                out[n] = both[j]
        return out

    everything = EARLY + LATE
    first0 = _run_comm(gather_comm(0, FIRST), name="gather_w_l0")
    small = {n: w_loc[n] for n in small_rep}
    (sg,) = _exchange([_pack([w_loc[n] for n in small_sh])[None]], ("x", "y"), name="gather_conv_w")
    shards = [_unpack(sg[k, 0], [w_loc[n].shape for n in small_sh]) for k in range(N_CHIPS)]
    for j, n in enumerate(small_sh):
        small[n] = _unshard(jnp.stack([shards[k][j] for k in range(N_CHIPS)]), SMALL_SHARDED[n])

    W0 = {**_small_weights(0, small), **chip_weights(first0, FIRST)}
    late0_comm = gather_comm(0, LATE)
    early1 = []

    def in_attention0(got):
        early1.extend(got[len(LATE):])
        return chip_weights(got[:len(LATE)], LATE)

    xs = x[0]
    xs, xb, sv0, W0 = _layer_fwd(
        0, xs, xs.astype(BF16), p[0, 0].astype(BF16), W0,
        up=(gather_comm(0, NEXT), lambda got: chip_weights(got, NEXT)),
        att=(_merge_comms([late0_comm, gather_comm(1, EARLY, base=late0_comm.n_sems)]), in_attention0))
    W1 = {**_small_weights(1, small), **chip_weights(early1, EARLY)}
    xs, _, sv1, W1 = _layer_fwd(1, xs, xb, p[1, 0].astype(BF16), W1,
                                att=(gather_comm(1, LATE), lambda got: chip_weights(got, LATE)))
    dx, loss = _loss_kernel(xs, loss_target[0], name="loss")
    loss = lax.psum(loss[0, 0], MESH_AXES)
    dx, G1, _, _ = _layer_bwd(1, dx, sv1, W1)
    sums1 = pair_sums(G1, everything, "l1")
    comm1 = _reduce_chips_comm(sums1, 1)
    late_sums, early_sums = [], []

    def late0(G):
        late_sums.extend(pair_sums(G, LATE, "l0_late"))
        return _reduce_chips_comm(late_sums, 0, base=comm1.n_sems)

    def last0(G):
        early_sums.extend(pair_sums(G, EARLY, "l0"))
        return _reduce_chips_comm(early_sums, 0)

    grad_x, G0, quads, quads0 = _layer_bwd(0, dx, sv0, W0, comm=comm1, late=late0, last=last0)

    n1 = len(comm1.out_shapes)
    red = [{**finish_reduce(quads[n1:], late_sums, 0, LATE, "l0_late"),
            **finish_reduce(quads0, early_sums, 0, EARLY, "l0")},
           finish_reduce(quads[:n1], sums1, 1, everything, "l1")]
    g_red = {}
    for n in big:
        g = jnp.stack([red[li][n] for li in range(DEPTH)])
        g_red[n] = g[tuple(slice(0, s) for s in w_loc[n].shape)]
    small_l = [_layer_small_grads(G0, W0), _layer_small_grads(G1, W1)]
    g_small = {n: jnp.stack([small_l[li][n] for li in range(DEPTH)]) for n in small_l[0]}
    small_all = small_rep + small_sh
    sgp = _pack([g_small[n] for n in small_all])
    (sall,) = _exchange([sgp[None]], MESH_AXES, name="reduce_small")
    sred = _sum_slots(sall.reshape((2 ** len(MESH_AXES),) + sgp.shape), F32, name="reduce_small_sum")
    for n, g in zip(small_all, _unpack(sred, [g_small[n].shape for n in small_all])):
        if n in SMALL_SHARDED:
            width = w_loc[n].shape[-1]
            g = lax.dynamic_slice_in_dim(g, chip * width, width, axis=SMALL_SHARDED[n])
        g_red[n] = g

    delta, new_m, new_v = {}, {}, {}
    for n in big:
        delta[n], new_m[n], new_v[n] = _adamw(w_loc[n], g_red[n], m_loc[n], v_loc[n], name="adamw_" + n)
    shapes = [w_loc[n].shape for n in small_all]
    packs = [_pack([d[n] for n in small_all]) for d in (w_loc, g_red, m_loc, v_loc)]
    outs = _adamw(*packs, name="adamw_small")
    for d, o in zip((delta, new_m, new_v), outs):
        for n, a in zip(small_all, _unpack(o, shapes)):
            d[n] = a
    return (loss, grad_x[None], *[g_red[n] for n in WEIGHTS], *[delta[n] for n in WEIGHTS],
            *[new_m[n] for n in WEIGHTS], *[new_v[n] for n in WEIGHTS])
```

```python
import math

import jax
import jax.numpy as jnp
from jax import lax
from jax.experimental import pallas as pl
from jax.experimental.pallas import tpu as pltpu

F32 = jnp.float32
BF16 = jnp.bfloat16

D_MODEL = 1024
DEPTH = 2
PLE_DIM = 256
HEAD_DIM = 64
LRU_WIDTH = 256
LRU_HEADS = 4
LRU_C = 8.0
CONV_K = 4
ATT_WIDTH = 256
ATT_HEADS = 4
SSD_WIDTH = 512
SSD_HEADS = 8
SSD_GROUPS = 2
SSD_STATE = 128
SSD_CHUNK = 128
SSD_CONV_DIM = 1024
ALPHA = (2.0 * DEPTH) ** 0.25
LN_EPS = 1e-5
RMS_EPS = 1e-5
IN_WIDTH = 2828
ADAM_LR = 0.001
ADAM_B1 = 0.9
ADAM_B2 = 0.999
ADAM_EPS = 1e-08
ADAM_WD = 0.01
ADAM_STEP = 10

H_WIDTH = 3072
COL_XBC, COL_Z, COL_U, COL_G, COL_Q, COL_K, COL_V, COL_SMALL = 0, 1024, 1536, 1792, 2048, 2304, 2560, 2816
LANE_F = 0
LANE_DT = 4
LANES = 128
SUBLANES = 8
NEG = -1e30

VMEM_LIMIT = 48 * 1024 * 1024

N_CHIPS = 4
MESH_AXES = ("x", "y", "c")
SHARE = 768


def _params(n):
    return pltpu.CompilerParams(dimension_semantics=("arbitrary",) * n, vmem_limit_bytes=VMEM_LIMIT)


def _pick(n, cands):
    for c in cands:
        if n % c == 0:
            return c
    return n


def _iota(shape, dim):
    return lax.broadcasted_iota(jnp.int32, shape, dim)


def _shift_down(x, s, prev8):
    if s == 0:
        return x
    r = pltpu.roll(x, s, 0)
    pr = pltpu.roll(prev8, s, 0)
    head = jnp.where(_iota(pr.shape, 0) < s, pr, r[:SUBLANES])
    return jnp.concatenate([head, r[SUBLANES:]], axis=0)


def _shift_up(x, s, next8):
    if s == 0:
        return x
    n = x.shape[0]
    r = pltpu.roll(x, n - s, 0)
    nr = pltpu.roll(next8, SUBLANES - s, 0)
    tail = jnp.where(_iota(nr.shape, 0) >= SUBLANES - s, nr, r[n - SUBLANES:])
    return jnp.concatenate([r[:n - SUBLANES], tail], axis=0)


def _scan_fwd(a, b):
    n = a.shape[0]
    row = _iota(a.shape, 0)
    d = 1
    while d < n:
        keep = row >= d
        a_s = jnp.where(keep, pltpu.roll(a, d, 0), 1.0)
        b_s = jnp.where(keep, pltpu.roll(b, d, 0), 0.0)
        b = a * b_s + b
        a = a * a_s
        d *= 2
    return a, b


def _scan_bwd(a, b):
    n = a.shape[0]
    row = _iota(a.shape, 0)
    d = 1
    while d < n:
        keep = row < n - d
        a_s = jnp.where(keep, pltpu.roll(a, n - d, 0), 1.0)
        b_s = jnp.where(keep, pltpu.roll(b, n - d, 0), 0.0)
        b = a * b_s + b
        a = a * a_s
        d *= 2
    return a, b


def _cumsum_rows(x, reverse=False):
    n = x.shape[0]
    row = _iota(x.shape, 0)
    d = 1
    while d < n:
        if reverse:
            x = x + jnp.where(row < n - d, pltpu.roll(x, n - d, 0), 0.0)
        else:
            x = x + jnp.where(row >= d, pltpu.roll(x, d, 0), 0.0)
        d *= 2
    return x


def _col(x, lane):
    return jnp.sum(jnp.where(_iota(x.shape, 1) == lane, x, 0.0), axis=1, keepdims=True)


def _row(x, r):
    return jnp.sum(jnp.where(_iota(x.shape, 0) == r, x, 0.0), axis=0, keepdims=True)


def _sigmoid(x):
    return jax.nn.sigmoid(x)


def _softplus(x):
    return jnp.maximum(x, 0.0) + jnp.log(1.0 + jnp.exp(-jnp.abs(x)))


def _gelu_and_grad(x):
    c0 = math.sqrt(2.0 / math.pi)
    inner = c0 * (x + 0.044715 * x * x * x)
    t = jnp.tanh(inner)
    g = 0.5 * x * (1.0 + t)
    dg = 0.5 * (1.0 + t) + 0.5 * x * (1.0 - t * t) * c0 * (1.0 + 3.0 * 0.044715 * x * x)
    return g, dg


def _dot(a, b, ca, cb):
    return lax.dot_general(a, b, (((ca,), (cb,)), ((), ())), preferred_element_type=F32)


def _conv_taps(xr, prev8, w, bias):
    y = bias + w[CONV_K - 1:CONV_K, :] * xr
    for j in range(CONV_K - 1):
        y = y + w[j:j + 1, :] * _shift_down(xr, CONV_K - 1 - j, prev8)
    return y


def _conv_taps_bwd(dy, next8, w, xr):
    dx = None
    dws = []
    for j in range(CONV_K):
        sh = _shift_up(dy, CONV_K - 1 - j, next8)
        term = w[j:j + 1, :] * sh
        dx = term if dx is None else dx + term
        dws.append(jnp.sum(sh * xr, axis=0, keepdims=True))
    return dx, jnp.concatenate(dws, axis=0)


def _head_expand(v, lane0, nheads, width):
    rows = v.shape[0]
    colhead = _iota((rows, width), 1) // HEAD_DIM
    out = jnp.zeros((rows, width), F32)
    for h in range(nheads):
        out = jnp.where(colhead == h, _col(v, lane0 + h), out)
    return out


def _head_reduce(x, lane0, nheads):
    rows = x.shape[0]
    colhead = _iota(x.shape, 1) // HEAD_DIM
    lane = _iota((rows, LANES), 1)
    out = jnp.zeros((rows, LANES), F32)
    for h in range(nheads):
        s = jnp.sum(jnp.where(colhead == h, x, 0.0), axis=1, keepdims=True)
        out = jnp.where(lane == lane0 + h, s, out)
    return out


def _mm(a, b, *, ta=False, tb=False, scale=1.0, out_dtype=F32, chip_cols=False, name):
    if ta:
        kk, m = a.shape
    else:
        m, kk = a.shape
    n = b.shape[0] if tb else b.shape[1]
    tm = _pick(m, (1024, 512, 256, 128))
    tk = _pick(kk, (1024, 768, 512, 256, 128))
    nk = kk // tk
    dn_a = 0 if ta else 1
    dn_b = 1 if tb else 0
    share = n // N_CHIPS
    if chip_cols:
        tn = n
        out_spec = pl.BlockSpec((N_CHIPS, tm, share), lambda i, j, k: (0, i, 0))
        out_shape = jax.ShapeDtypeStruct((N_CHIPS, m, share), out_dtype)
    else:
        tn = _pick(n, (1024, 768, 512, 256, 128))
        out_spec = pl.BlockSpec((tm, tn), lambda i, j, k: (i, j))
        out_shape = jax.ShapeDtypeStruct((m, n), out_dtype)

    def body(a_ref, b_ref, o_ref, acc):
        k = pl.program_id(2)

        @pl.when(k == 0)
        def _():
            acc[...] = jnp.zeros_like(acc)

        acc[...] += _dot(a_ref[...].astype(BF16), b_ref[...].astype(BF16), dn_a, dn_b)

        @pl.when(k == nk - 1)
        def _():
            if chip_cols:
                for c in range(N_CHIPS):
                    o_ref[c] = (acc[:, share * c:share * (c + 1)] * scale).astype(out_dtype)
            else:
                o_ref[...] = (acc[...] * scale).astype(out_dtype)

    a_spec = pl.BlockSpec((tk, tm), lambda i, j, k: (k, i)) if ta else pl.BlockSpec((tm, tk), lambda i, j, k: (i, k))
    b_spec = pl.BlockSpec((tn, tk), lambda i, j, k: (j, k)) if tb else pl.BlockSpec((tk, tn), lambda i, j, k: (k, j))
    return pl.pallas_call(
        body, name=name, grid=(m // tm, n // tn, nk),
        in_specs=[a_spec, b_spec],
        out_specs=out_spec, out_shape=out_shape,
        scratch_shapes=[pltpu.VMEM((tm, tn), F32)],
        compiler_params=_params(3),
    )(a, b)


def _mm_swiglu(xb, wg, wu, *, comm=None, name):
    t, d = xb.shape
    share = wg.shape[2]
    n = N_CHIPS * share
    tm = _pick(t, (512, 256, 128))
    tn = _pick(share, (768, 256, 128))
    per = share // tn

    def body(x_ref, wg_ref, wu_ref, g_ref, u_ref, a_ref):
        x = x_ref[pl.ds(pl.multiple_of(pl.program_id(1) * tm, tm), tm), :]
        g = _dot(x, wg_ref[...], 1, 0)
        u = _dot(x, wu_ref[...], 1, 0)
        g_ref[...] = g.astype(BF16)
        u_ref[...] = u.astype(BF16)
        a_ref[...] = (g * _sigmoid(g) * u).astype(BF16)

    o = jax.ShapeDtypeStruct((t, n), BF16)
    ospec = pl.BlockSpec((tm, tn), lambda j, i: (i, j))
    return _hosted_call(
        body, comm, (n // tn, t // tm), name=name,
        in_specs=[pl.BlockSpec((t, d), lambda j, i: (0, 0), pipeline_mode=pl.Buffered(1)),
                  pl.BlockSpec((None, d, tn), lambda j, i: (j // per, 0, j % per)),
                  pl.BlockSpec((None, d, tn), lambda j, i: (j // per, 0, j % per))],
        out_specs=[ospec, ospec, ospec], out_shape=[o, o, o], scratch_shapes=[], args=[xb, wg, wu])


def _mm_swiglu_bwd(dr, wd, g, u, *, scale, name):
    t, d = dr.shape
    n = wd.shape[0]
    tm = _pick(t, (512, 256, 128))
    tn = _pick(n, (768, 256, 128))

    def body(dr_ref, wd_ref, g_ref, u_ref, dg_ref, du_ref):
        dr_rows = dr_ref[pl.ds(pl.multiple_of(pl.program_id(1) * tm, tm), tm), :]
        da = _dot(dr_rows.astype(BF16), wd_ref[...], 1, 1) * scale
        gg = g_ref[...].astype(F32)
        uu = u_ref[...].astype(F32)
        sg = _sigmoid(gg)
        dg_ref[...] = (da * uu * (sg * (1.0 + gg * (1.0 - sg)))).astype(BF16)
        du_ref[...] = (da * gg * sg).astype(BF16)

    o = jax.ShapeDtypeStruct((t, n), BF16)
    ospec = pl.BlockSpec((tm, tn), lambda j, i: (i, j))
    return pl.pallas_call(
        body, name=name, grid=(n // tn, t // tm),
        in_specs=[pl.BlockSpec((t, d), lambda j, i: (0, 0), pipeline_mode=pl.Buffered(1)),
                  pl.BlockSpec((tn, d), lambda j, i: (j, 0)),
                  ospec, ospec],
        out_specs=[ospec, ospec], out_shape=[o, o],
        compiler_params=_params(2),
    )(dr, wd, g, u)


def _mm_ln(a, w, resid, gain, bias, *, rscale, mscale, name):
    t, kk = a.shape
    d = w.shape[1]
    tm = _pick(t, (512, 256, 128))
    tk = kk
    nk = kk // tk

    def body(a_ref, w_ref, r_ref, g_ref, b_ref, y_ref, yb_ref, xh_ref, rs_ref, acc):
        k = pl.program_id(1)

        @pl.when(k == 0)
        def _():
            acc[...] = jnp.zeros_like(acc)

        acc[...] += _dot(a_ref[...].astype(BF16), w_ref[...], 1, 0)

        @pl.when(k == nk - 1)
        def _():
            r = rscale * r_ref[...] + mscale * acc[...]
            mu = jnp.mean(r, axis=1, keepdims=True)
            xc = r - mu
            var = jnp.mean(xc * xc, axis=1, keepdims=True)
            rstd = lax.rsqrt(var + LN_EPS)
            xh = xc * rstd
            y = xh * g_ref[...] + b_ref[...]
            y_ref[...] = y
            yb_ref[...] = y.astype(BF16)
            xh_ref[...] = xh
            rs_ref[...] = rstd

    row = pl.BlockSpec((tm, d), lambda i, k: (i, 0))
    vec = pl.BlockSpec((1, d), lambda i, k: (0, 0))
    return pl.pallas_call(
        body, name=name, grid=(t // tm, nk),
        in_specs=[pl.BlockSpec((tm, tk), lambda i, k: (i, k)),
                  pl.BlockSpec((tk, d), lambda i, k: (k, 0)), row, vec, vec],
        out_specs=[row, row, row, pl.BlockSpec((tm, 1), lambda i, k: (i, 0))],
        out_shape=[jax.ShapeDtypeStruct((t, d), F32), jax.ShapeDtypeStruct((t, d), BF16),
                   jax.ShapeDtypeStruct((t, d), F32), jax.ShapeDtypeStruct((t, 1), F32)],
        scratch_shapes=[pltpu.VMEM((tm, d), F32)],
        compiler_params=_params(2),
    )(a, w, resid, gain.reshape(1, d), bias.reshape(1, d))


def _bwd_proj(pairs, resid, *, rscale, ln, comm=None, name):
    t, kk = pairs[0][0].shape
    d = pairs[0][1].shape[-2]
    has_ln = ln is not None
    npair = len(pairs)
    tm = _pick(t, (256, 128) if has_ln and npair > 1 else (512, 256, 128))
    nt = t // tm

    def body(*refs):
        ab = refs[:2 * npair]
        r_ref = refs[2 * npair]
        pos = 2 * npair + 1
        if has_ln:
            xh_ref, rs_ref, g_ref = refs[pos:pos + 3]
            pos += 3
            o_ref, ob_ref, dg_ref, db_ref = refs[pos:pos + 4]
        else:
            o_ref = refs[pos]
        i = pl.program_id(0)
        dy = rscale * r_ref[...]
        for q in range(npair):
            a_ref, b_ref = ab[2 * q], ab[2 * q + 1]
            if len(b_ref.shape) == 3:
                share = b_ref.shape[2]
                for c in range(N_CHIPS):
                    dy = dy + _dot(a_ref[:, share * c:share * (c + 1)].astype(BF16), b_ref[c], 1, 1)
            else:
                dy = dy + _dot(a_ref[...].astype(BF16), b_ref[...], 1, 1)
        if not has_ln:
            o_ref[...] = dy
            return
        xh = xh_ref[...]
        w = dy * g_ref[...]
        m1 = jnp.mean(w, axis=1, keepdims=True)
        m2 = jnp.mean(w * xh, axis=1, keepdims=True)
        dr = rs_ref[...] * (w - m1 - xh * m2)
        o_ref[...] = dr
        ob_ref[...] = dr.astype(BF16)

        @pl.when(i == 0)
        def _():
            dg_ref[...] = jnp.zeros_like(dg_ref)
            db_ref[...] = jnp.zeros_like(db_ref)

        dg_ref[...] += jnp.sum(dy * xh, axis=0, keepdims=True)
        db_ref[...] += jnp.sum(dy, axis=0, keepdims=True)

    row = pl.BlockSpec((tm, d), lambda i, k: (i, 0))
    vec = pl.BlockSpec((1, d), lambda i, k: (0, 0))
    in_specs, args = [], []
    for a, b in pairs:
        b_spec = pl.BlockSpec(b.shape, lambda i, k, nd=b.ndim: (0,) * nd, pipeline_mode=pl.Buffered(1))
        in_specs += [pl.BlockSpec((tm, kk), lambda i, k: (i, 0)), b_spec]
        args += [a, b]
    in_specs.append(row)
    args.append(resid)
    out_specs = [row]
    out_shape = [jax.ShapeDtypeStruct((t, d), F32)]
    if has_ln:
        xh, rs, gain = ln
        in_specs += [row, pl.BlockSpec((tm, 1), lambda i, k: (i, 0)), vec]
        args += [xh, rs, gain.reshape(1, d)]
        out_specs += [row, vec, vec]
        out_shape += [jax.ShapeDtypeStruct((t, d), BF16)] + [jax.ShapeDtypeStruct((1, d), F32)] * 2
    outs, got = _hosted_call(body, comm, (nt, 1), name=name, in_specs=in_specs, out_specs=out_specs,
                             out_shape=out_shape, scratch_shapes=[], args=args)
    return tuple(outs) if comm is None else tuple(outs) + (got,)


def _mm_pe(x3, x3b, pb, wgate, bgate, wproj, *, name):
    t, d = x3.shape
    pd = pb.shape[1]
    tm = _pick(t, (512, 256, 128))
    tn = _pick(d, (512, 256, 128))

    def body(x_ref, xb_ref, p_ref, wg_ref, bg_ref, wp_ref, y_ref, yb_ref, sg_ref, e_ref):
        sg = _sigmoid(_dot(xb_ref[...], wg_ref[...], 1, 0) + bg_ref[...])
        e = _dot(p_ref[...], wp_ref[...], 1, 0)
        y = x_ref[...] + sg * e
        y_ref[...] = y
        yb_ref[...] = y.astype(BF16)
        sg_ref[...] = sg.astype(BF16)
        e_ref[...] = e.astype(BF16)

    ospec = pl.BlockSpec((tm, tn), lambda i, j: (i, j))
    ob = jax.ShapeDtypeStruct((t, d), BF16)
    return pl.pallas_call(
        body, name=name, grid=(t // tm, d // tn),
        in_specs=[ospec, pl.BlockSpec((tm, d), lambda i, j: (i, 0)), pl.BlockSpec((tm, pd), lambda i, j: (i, 0)),
                  pl.BlockSpec((d, tn), lambda i, j: (0, j)), pl.BlockSpec((1, tn), lambda i, j: (0, j)),
                  pl.BlockSpec((pd, tn), lambda i, j: (0, j))],
        out_specs=[ospec, ospec, ospec, ospec],
        out_shape=[jax.ShapeDtypeStruct((t, d), F32), ob, ob, ob],
        compiler_params=_params(2),
    )(x3, x3b, pb, wgate, bgate.reshape(1, d), wproj)


def _pe_bwd_elem(dx4, sg, e, *, name):
    t, d = dx4.shape
    tm = _pick(t, (512, 256, 128))

    def body(dx_ref, sg_ref, e_ref, dgp_ref, de_ref, db_ref):
        dx = dx_ref[...]
        s = sg_ref[...].astype(F32)
        dgp = dx * e_ref[...].astype(F32) * s * (1.0 - s)
        dgp_ref[...] = dgp.astype(BF16)
        de_ref[...] = (dx * s).astype(BF16)

        @pl.when(pl.program_id(0) == 0)
        def _():
            db_ref[...] = jnp.zeros_like(db_ref)

        db_ref[...] += jnp.sum(dgp, axis=0, keepdims=True)

    row = pl.BlockSpec((tm, d), lambda i: (i, 0))
    ob = jax.ShapeDtypeStruct((t, d), BF16)
    return pl.pallas_call(
        body, name=name, grid=(t // tm,), in_specs=[row, row, row],
        out_specs=[row, row, pl.BlockSpec((1, d), lambda i: (0, 0))],
        out_shape=[ob, ob, jax.ShapeDtypeStruct((1, d), F32)],
        compiler_params=_params(1),
    )(dx4, sg, e)


def _assemble(pieces, width, *, name):
    t = pieces[0].shape[0]
    tm = _pick(t, (512, 256, 128))
    widths = [p.shape[1] for p in pieces]

    def body(*refs):
        o_ref = refs[-1]
        off = 0
        for p_ref, w in zip(refs[:-1], widths):
            o_ref[:, off:off + w] = p_ref[...].astype(BF16)
            off += w
        if off < width:
            o_ref[:, off:] = jnp.zeros((tm, width - off), BF16)

    return pl.pallas_call(
        body, name=name, grid=(t // tm,),
        in_specs=[pl.BlockSpec((tm, w), lambda i: (i, 0)) for w in widths],
        out_specs=pl.BlockSpec((tm, width), lambda i: (i, 0)),
        out_shape=jax.ShapeDtypeStruct((t, width), BF16),
        compiler_params=_params(1),
    )(*pieces)


def _loss_kernel(y, target, *, name):
    t, d = y.shape
    tm = _pick(t, (512, 256, 128))

    def body(y_ref, t_ref, dy_ref, l_ref):
        diff = y_ref[...] - t_ref[...]
        dy_ref[...] = diff * (1.0 / d)

        @pl.when(pl.program_id(0) == 0)
        def _():
            l_ref[...] = jnp.zeros_like(l_ref)

        part = jnp.sum(jnp.mean(diff * diff, axis=1, keepdims=True), axis=0, keepdims=True)
        l_ref[...] += 0.5 * part

    row = pl.BlockSpec((tm, d), lambda i: (i, 0))
    return pl.pallas_call(
        body, name=name, grid=(t // tm,), in_specs=[row, row],
        out_specs=[row, pl.BlockSpec((1, 1), lambda i: (0, 0))],
        out_shape=[jax.ShapeDtypeStruct((t, d), F32), jax.ShapeDtypeStruct((1, 1), F32)],
        compiler_params=_params(1),
    )(y, target)


LRU_TM = 256


def _lru_gate_terms(r, lam):
    sp = _softplus(-lam)
    la = -LRU_C * r * sp
    a = jnp.exp(la)
    em = jnp.tanh(la) * (jnp.exp(2.0 * la) + 1.0)
    s = jnp.sqrt(-em)
    return la, a, s, sp


def _lru_fwd(hbuf, conv_w, conv_b, wa, ba, wx, bx, lam, *, name):
    t = hbuf.shape[0]
    w = LRU_WIDTH
    tm = _pick(t, (LRU_TM, 128))
    cu, cg = COL_U // w, COL_G // w
    hb = tm // SUBLANES

    def body(u_ref, up_ref, g_ref, cw_ref, cb_ref, wa_ref, ba_ref, wx_ref, bx_ref, lam_ref,
             y_ref, u_out, r_out, i_out, a_out, h_out, carry):
        i = pl.program_id(0)

        @pl.when(i == 0)
        def _():
            carry[...] = jnp.zeros_like(carry)

        prev = jnp.where(i == 0, 0.0, up_ref[...])
        u = _conv_taps(u_ref[...], prev, cw_ref[...], cb_ref[...])
        ub = u.astype(BF16)
        r = _sigmoid(_dot(ub, wa_ref[...], 1, 0) + ba_ref[...])
        ig = _sigmoid(_dot(ub, wx_ref[...], 1, 0) + bx_ref[...])
        _, a, s, _ = _lru_gate_terms(r, lam_ref[...])
        b = s * (ig * u)
        acum, hs = _scan_fwd(a, b)
        h = hs + acum * carry[0:1, :]
        carry[...] = jnp.broadcast_to(h[tm - 1:tm, :], carry.shape)
        gl, _ = _gelu_and_grad(g_ref[...])
        y_ref[...] = h * gl
        u_out[...] = u
        r_out[...] = r
        i_out[...] = ig
        a_out[...] = a
        h_out[...] = h

    row = pl.BlockSpec((tm, w), lambda i: (i, 0))
    vec = pl.BlockSpec((1, w), lambda i: (0, 0))
    mat = pl.BlockSpec((w, w), lambda i: (0, 0))
    o = jax.ShapeDtypeStruct((t, w), F32)
    return pl.pallas_call(
        body, name=name, grid=(t // tm,),
        in_specs=[pl.BlockSpec((tm, w), lambda i: (i, cu)),
                  pl.BlockSpec((SUBLANES, w), lambda i: (jnp.maximum(i * hb - 1, 0), cu)),
                  pl.BlockSpec((tm, w), lambda i: (i, cg)),
                  pl.BlockSpec((CONV_K, w), lambda i: (0, 0)), vec, mat, vec, mat, vec, vec],
        out_specs=[row] * 6, out_shape=[o] * 6,
        scratch_shapes=[pltpu.VMEM((SUBLANES, w), F32)],
        compiler_params=_params(1),
    )(hbuf, hbuf, hbuf, conv_w, conv_b, wa, ba, wx, bx, lam)


def _lru_bwd(dymix, hbuf, u, r, ig, a, h, conv_w, wa, wx, lam, *, name):
    t = hbuf.shape[0]
    w = LRU_WIDTH
    tm = _pick(t, (LRU_TM, 128))
    nb = t // tm
    cu, cg = COL_U // w, COL_G // w
    hb = tm // SUBLANES
    last8 = t // SUBLANES - 1

    def body(dy_ref, ur_ref, g_ref, u_ref, r_ref, i_ref, a_ref, an_ref, h_ref, hp_ref,
             cw_ref, wa_ref, wx_ref, lam_ref,
             dur_ref, dgr_ref, dcw_ref, dcb_ref, dwa_ref, dba_ref, dwx_ref, dbx_ref, dlam_ref,
             lcarry, dnext):
        i = pl.program_id(0)
        ib = nb - 1 - i

        @pl.when(i == 0)
        def _():
            lcarry[...] = jnp.zeros_like(lcarry)
            dnext[...] = jnp.zeros_like(dnext)
            for ref in (dcw_ref, dcb_ref, dwa_ref, dba_ref, dwx_ref, dbx_ref, dlam_ref):
                ref[...] = jnp.zeros_like(ref)

        dy = dy_ref[...]
        hh = h_ref[...]
        av = a_ref[...]
        uu = u_ref[...]
        rr = r_ref[...]
        ii = i_ref[...]
        lam_v = lam_ref[...]
        gl, dgl = _gelu_and_grad(g_ref[...])
        dgr_ref[...] = (dy * hh * dgl).astype(BF16)
        dh_out = dy * gl
        a_next = _shift_up(av, 1, jnp.where(ib == nb - 1, 0.0, an_ref[...]))
        acum, ls = _scan_bwd(a_next, dh_out)
        lam_adj = ls + acum * lcarry[0:1, :]
        lcarry[...] = jnp.broadcast_to(lam_adj[0:1, :], lcarry.shape)
        h_prev = _shift_down(hh, 1, jnp.where(ib == 0, 0.0, hp_ref[...]))
        da = lam_adj * h_prev
        _, a2, s, sp = _lru_gate_terms(rr, lam_v)
        d_igu = lam_adj * s
        ds = lam_adj * ii * uu
        dla = da * a2 - ds * (a2 * a2) / s
        dr = dla * (-LRU_C * sp)
        dlam_ref[...] += jnp.sum(dla * (LRU_C * rr * _sigmoid(-lam_v)), axis=0, keepdims=True)
        dpre_r = dr * rr * (1.0 - rr)
        dpre_i = d_igu * uu * ii * (1.0 - ii)
        prb = dpre_r.astype(BF16)
        pib = dpre_i.astype(BF16)
        ub = uu.astype(BF16)
        du = d_igu * ii + _dot(prb, wa_ref[...], 1, 1) + _dot(pib, wx_ref[...], 1, 1)
        dwa_ref[...] += _dot(ub, prb, 0, 0)
        dwx_ref[...] += _dot(ub, pib, 0, 0)
        dba_ref[...] += jnp.sum(dpre_r, axis=0, keepdims=True)
        dbx_ref[...] += jnp.sum(dpre_i, axis=0, keepdims=True)
        dur, dws = _conv_taps_bwd(du, dnext[...], cw_ref[...], ur_ref[...])
        dur_ref[...] = dur.astype(BF16)
        dcw_ref[...] += dws
        dcb_ref[...] += jnp.sum(du, axis=0, keepdims=True)
        dnext[...] = du[:SUBLANES]

    def rowspec(col):
        return pl.BlockSpec((tm, w), lambda i: (nb - 1 - i, col))

    row = rowspec(0)
    nxt = pl.BlockSpec((SUBLANES, w), lambda i: (jnp.minimum((nb - i) * hb, last8), 0))
    prv = pl.BlockSpec((SUBLANES, w), lambda i: (jnp.maximum((nb - 1 - i) * hb - 1, 0), 0))
    vec = pl.BlockSpec((1, w), lambda i: (0, 0))
    mat = pl.BlockSpec((w, w), lambda i: (0, 0))
    cw = pl.BlockSpec((CONV_K, w), lambda i: (0, 0))
    o = jax.ShapeDtypeStruct((t, w), BF16)
    v1 = jax.ShapeDtypeStruct((1, w), F32)
    m1 = jax.ShapeDtypeStruct((w, w), F32)
    return pl.pallas_call(
        body, name=name, grid=(nb,),
        in_specs=[rowspec(0), rowspec(cu), rowspec(cg), row, row, row, row, nxt, row, prv, cw, mat, mat, vec],
        out_specs=[row, row, cw, vec, mat, vec, mat, vec, vec],
        out_shape=[o, o, jax.ShapeDtypeStruct((CONV_K, w), F32), v1, m1, v1, m1, v1, v1],
        scratch_shapes=[pltpu.VMEM((SUBLANES, w), F32), pltpu.VMEM((SUBLANES, w), F32)],
        compiler_params=_params(1),
    )(dymix, hbuf, hbuf, u, r, ig, a, a, h, h, conv_w, wa, wx, lam)


FOX_T = 1024
FOX_PREP_TM = 256


def _log_sigmoid(x):
    return jnp.minimum(x, 0.0) - jnp.log(1.0 + jnp.exp(-jnp.abs(x)))


def _fox_prep(hbuf, bf_vec, *, name):
    t = hbuf.shape[0]
    tm = _pick(t, (FOX_PREP_TM, 128))
    cs = COL_SMALL // LANES

    def body(s_ref, b_ref, eq_ref, ek_ref, carry):
        i = pl.program_id(0)

        @pl.when(i == 0)
        def _():
            carry[...] = jnp.zeros_like(carry)

        lf = _log_sigmoid(s_ref[...] + b_ref[...])
        f = _cumsum_rows(lf) + carry[0:1, :]
        carry[...] = jnp.broadcast_to(f[tm - 1:tm, :], carry.shape)
        lane = _iota((tm, LANES), 1)
        for h in range(ATT_HEADS):
            base = HEAD_DIM * (1 - h % 2)
            fh = _col(f, h)
            hi = fh.astype(BF16).astype(F32)
            mid = (fh - hi).astype(BF16).astype(F32)
            lo = fh - hi - mid
            terms = jnp.where(lane == base, hi, jnp.where(lane == base + 1, mid, jnp.where(lane == base + 2, lo, 0.0)))
            terms_k = jnp.where(lane == base + 3, -hi,
                                jnp.where(lane == base + 4, -mid, jnp.where(lane == base + 5, -lo, 0.0)))
            ones_q = ((lane >= base + 3) & (lane < base + 6)).astype(F32)
            ones_k = ((lane >= base) & (lane < base + 3)).astype(F32)
            eq_ref[:, LANES * h:LANES * (h + 1)] = (terms + ones_q).astype(BF16)
            ek_ref[:, LANES * h:LANES * (h + 1)] = (terms_k + ones_k).astype(BF16)

    ospec = pl.BlockSpec((tm, ATT_HEADS * LANES), lambda i: (i, 0))
    o = jax.ShapeDtypeStruct((t, ATT_HEADS * LANES), BF16)
    return pl.pallas_call(
        body, name=name, grid=(t // tm,),
        in_specs=[pl.BlockSpec((tm, LANES), lambda i: (i, cs)), pl.BlockSpec((1, LANES), lambda i: (0, 0))],
        out_specs=[ospec, ospec], out_shape=[o, o],
        scratch_shapes=[pltpu.VMEM((SUBLANES, LANES), F32)],
        compiler_params=_params(1),
    )(hbuf, bf_vec)


def _fox_post(dfc, hbuf, bf_vec, *, name):
    t = hbuf.shape[0]
    tm = _pick(t, (FOX_PREP_TM, 128))
    nb = t // tm
    cs = COL_SMALL // LANES

    def body(df_ref, s_ref, b_ref, o_ref, db_ref, carry):
        i = pl.program_id(0)

        @pl.when(i == 0)
        def _():
            carry[...] = jnp.zeros_like(carry)
            db_ref[...] = jnp.zeros_like(db_ref)

        dlf = _cumsum_rows(df_ref[...], reverse=True) + carry[0:1, :]
        carry[...] = jnp.broadcast_to(dlf[0:1, :], carry.shape)
        dl = dlf * _sigmoid(-(s_ref[...] + b_ref[...]))
        dl = jnp.where(_iota(dl.shape, 1) < ATT_HEADS, dl, 0.0)
        o_ref[...] = dl
        db_ref[...] += jnp.sum(dl, axis=0, keepdims=True)

    vec = pl.BlockSpec((1, LANES), lambda i: (0, 0))
    return pl.pallas_call(
        body, name=name, grid=(nb,),
        in_specs=[pl.BlockSpec((tm, LANES), lambda i: (nb - 1 - i, 0)),
                  pl.BlockSpec((tm, LANES), lambda i: (nb - 1 - i, cs)), vec],
        out_specs=[pl.BlockSpec((tm, LANES), lambda i: (nb - 1 - i, 0)), vec],
        out_shape=[jax.ShapeDtypeStruct((t, LANES), F32), jax.ShapeDtypeStruct((1, LANES), F32)],
        scratch_shapes=[pltpu.VMEM((SUBLANES, LANES), F32)],
        compiler_params=_params(1),
    )(dfc, hbuf, bf_vec)


def _fox_masks(i, j, tq):
    row = i * tq + _iota((tq, tq), 0)
    col = j * tq + _iota((tq, tq), 1)
    lane = _iota((1, LANES), 1)
    return col <= row, (lane < HEAD_DIM, lane >= HEAD_DIM)


def _hosting(body, n_in, n_out, n_scratch, comm, grid):
    na, no = len(comm.arrays), len(comm.out_shapes)

    def hosted(*refs):
        o0 = n_in + na
        s0 = o0 + n_out + no
        cargs = (refs[n_in:o0], refs[o0 + n_out:s0]) + tuple(refs[s0 + n_scratch:])
        a, b = pl.program_id(0), pl.program_id(1)

        @pl.when((a == 0) & (b == 0))
        def _():
            comm.start(*cargs)

        @pl.when((a == grid[0] - 1) & (b == 0))
        def _():
            comm.middle(*cargs)

        body(*refs[:n_in], *refs[o0:o0 + n_out], *refs[s0:s0 + n_scratch])

        @pl.when((a == grid[0] - 1) & (b == grid[1] - 1))
        def _():
            comm.finish(*cargs)

    return hosted


def _hosted_call(body, comm, grid, *, name, in_specs, out_specs, out_shape, scratch_shapes, args):
    n_out = len(out_shape)
    if comm is not None:
        cin, cout, sems = comm.specs()
        body = _hosting(body, len(in_specs), n_out, len(scratch_shapes), comm, grid)
        in_specs, out_specs = in_specs + cin, out_specs + cout
        out_shape, scratch_shapes, args = out_shape + comm.out_shapes, scratch_shapes + sems, args + list(comm.arrays)
    outs = pl.pallas_call(body, name=name, grid=grid, in_specs=in_specs, out_specs=out_specs,
                          out_shape=out_shape, scratch_shapes=scratch_shapes, compiler_params=_params(2))(*args)
    return outs[:n_out], outs[n_out:]


def _merge_comms(comms):
    comms = [c for c in comms if c is not None]
    if len(comms) <= 1:
        return comms[0] if comms else None

    def both(which):
        def run(ins, outs, ssem, rsem):
            ia = io = 0
            for c in comms:
                na, no = len(c.arrays), len(c.out_shapes)
                getattr(c, which)(ins[ia:ia + na], outs[io:io + no], ssem, rsem)
                ia, io = ia + na, io + no
        return run

    spans = sorted((c.base, c.base + c.n_own) for c in comms)
    assert all(a[1] <= b[0] for a, b in zip(spans, spans[1:])), "semaphore ranges overlap"
    return _Comm(sum((list(c.arrays) for c in comms), []), sum((list(c.out_shapes) for c in comms), []),
                 spans[-1][1], both("start"), both("finish"), middle=both("middle"))


def _fox_fwd(hbuf, eq, ek, *, comm=None, name):
    t = hbuf.shape[0]
    w = ATT_WIDTH
    tq = _pick(t, (FOX_T, 256, 128))
    nq = t // tq
    cq, ck, cv = COL_Q // w, COL_K // w, COL_V // w

    def body(q_ref, k_ref, v_ref, eq_ref, ek_ref, o_ref, lse_ref, m_s, l_s, acc_s):
        i = pl.program_id(0)
        j = pl.program_id(1)

        @pl.when(j == 0)
        def _():
            m_s[...] = jnp.full_like(m_s, NEG)
            l_s[...] = jnp.zeros_like(l_s)
            acc_s[...] = jnp.zeros_like(acc_s)

        def step(diagonal):
            _, hms = _fox_masks(i, j, tq)
            keys_first = (j * tq + _iota((tq, tq), 0)) <= (i * tq + _iota((tq, tq), 1))
            half = _iota((LANES, 1), 0)
            hrows = (half < HEAD_DIM, half >= HEAD_DIM)
            m_all = m_s[...]
            l_all = l_s[...]
            acc_old = [acc_s[LANES * pr:LANES * (pr + 1), :] for pr in range(2)]
            m_out, l_out, acc_out = [], [], []
            for pr in range(2):
                sl = slice(LANES * pr, LANES * (pr + 1))
                qp = q_ref[:, sl]
                kp = k_ref[:, sl]
                vt = v_ref[:, sl].T.astype(BF16)
                acc = acc_old[pr]
                for hh in range(2):
                    h = 2 * pr + hh
                    hsl = slice(LANES * h, LANES * (h + 1))
                    qm = jnp.where(hms[hh], (qp * (HEAD_DIM ** -0.5)).astype(BF16), eq_ref[:, hsl])
                    km = jnp.where(hms[hh], kp.astype(BF16), ek_ref[:, hsl])
                    st = _dot(km, qm, 1, 1)
                    if diagonal:
                        st = jnp.where(keys_first, st, NEG)
                    m_old = m_all[h:h + 1, :]
                    m_new = jnp.maximum(m_old, jnp.max(st, axis=0, keepdims=True))
                    alpha = jnp.exp(m_old - m_new)
                    pt = jnp.exp(st - m_new)
                    l_out.append(alpha * l_all[h:h + 1, :] + jnp.sum(pt, axis=0, keepdims=True))
                    m_out.append(m_new)
                    pv = _dot(vt, pt.astype(BF16), 1, 0)
                    acc = jnp.where(hrows[hh], alpha * acc_old[pr] + pv, acc)
                acc_out.append(acc)
            for h in range(ATT_HEADS):
                m_s[h:h + 1, :] = m_out[h]
                l_s[h:h + 1, :] = l_out[h]
            for pr in range(2):
                acc_s[LANES * pr:LANES * (pr + 1), :] = acc_out[pr]

        @pl.when(j < i)
        def _():
            step(False)

        @pl.when(j == i)
        def _():
            step(True)
            half = _iota((LANES, 1), 0)
            l_all = l_s[...]
            for pr in range(2):
                acc = acc_s[LANES * pr:LANES * (pr + 1), :]
                o_t = jnp.where(half < HEAD_DIM, acc / l_all[2 * pr:2 * pr + 1, :], acc / l_all[2 * pr + 1:2 * pr + 2, :])
                o_ref[:, LANES * pr:LANES * (pr + 1)] = o_t.T
            lse = m_s[...] + jnp.log(l_s[...])
            lse_ref[...] = jnp.where(_iota(lse.shape, 0) < ATT_HEADS, lse, 0.0)

    return _hosted_call(
        body, comm, (nq, nq), name=name,
        in_specs=[pl.BlockSpec((tq, w), lambda i, j: (i, cq)),
                  pl.BlockSpec((tq, w), lambda i, j: (jnp.minimum(j, i), ck)),
                  pl.BlockSpec((tq, w), lambda i, j: (jnp.minimum(j, i), cv)),
                  pl.BlockSpec((tq, ATT_HEADS * LANES), lambda i, j: (i, 0)),
                  pl.BlockSpec((tq, ATT_HEADS * LANES), lambda i, j: (jnp.minimum(j, i), 0))],
        out_specs=[pl.BlockSpec((tq, w), lambda i, j: (i, 0)),
                   pl.BlockSpec((SUBLANES, tq), lambda i, j: (0, i))],
        out_shape=[jax.ShapeDtypeStruct((t, w), F32), jax.ShapeDtypeStruct((SUBLANES, t), F32)],
        scratch_shapes=[pltpu.VMEM((SUBLANES, tq), F32), pltpu.VMEM((SUBLANES, tq), F32),
                        pltpu.VMEM((w, tq), F32)],
        args=[hbuf, hbuf, hbuf, eq, ek])


def _fox_delta(dymix, o, *, name):
    t, w = o.shape
    tm = _pick(t, (512, 256, 128))
    cdo = ATT_WIDTH // w

    def body(do_ref, o_ref, d_ref):
        d_ref[...] = _head_reduce(do_ref[...] * o_ref[...], 0, ATT_HEADS)

    return pl.pallas_call(
        body, name=name, grid=(t // tm,),
        in_specs=[pl.BlockSpec((tm, w), lambda i: (i, cdo)), pl.BlockSpec((tm, w), lambda i: (i, 0))],
        out_specs=pl.BlockSpec((tm, LANES), lambda i: (i, 0)),
        out_shape=jax.ShapeDtypeStruct((t, LANES), F32),
        compiler_params=_params(1),
    )(dymix, o)


def _fox_bwd(hbuf, eq, ek, dymix, lse_rows, delta_rows, *, comm=None, name):
    t = hbuf.shape[0]
    w = ATT_WIDTH
    tq = _pick(t, (FOX_T, 256, 128))
    nq = t // tq
    cq, ck, cv = COL_Q // w, COL_K // w, COL_V // w
    cdo = ATT_WIDTH // w

    def body(q_ref, k_ref, v_ref, eq_ref, ek_ref, do_ref, lse_ref, dl_ref, dk_ref, dv_ref, dfk_ref, dqt_ref, dfq_ref,
             dk_s, dv_s, dfk_s):
        j = pl.program_id(0)
        i = pl.program_id(1)

        @pl.when((i == 0) & (j == 0))
        def _():
            dqt_ref[...] = jnp.zeros_like(dqt_ref)
            dfq_ref[...] = jnp.zeros_like(dfq_ref)

        @pl.when(i == 0)
        def _():
            dk_s[...] = jnp.zeros_like(dk_s)
            dv_s[...] = jnp.zeros_like(dv_s)
            dfk_s[...] = jnp.zeros_like(dfk_s)

        def step(diagonal):
            _, hms = _fox_masks(i, j, tq)
            keys_first = (j * tq + _iota((tq, tq), 0)) <= (i * tq + _iota((tq, tq), 1))
            half = _iota((LANES, 1), 0)
            hrows = (half < HEAD_DIM, half >= HEAD_DIM)
            lse_all = lse_ref[...]
            dl_all = dl_ref[...]
            dvs, dks, dfks, dqts, dfqs = [], [], [], [], []
            for pr in range(2):
                sl = slice(LANES * pr, LANES * (pr + 1))
                qp = q_ref[:, sl]
                kp = k_ref[:, sl]
                kt = kp.T.astype(BF16)
                vpb = v_ref[:, sl].astype(BF16)
                dop = do_ref[:, sl]
                dv_p = jnp.zeros((tq, LANES), F32)
                dk_p = jnp.zeros((tq, LANES), F32)
                dqt_p = jnp.zeros((LANES, tq), F32)
                for hh in range(2):
                    h = 2 * pr + hh
                    hsl = slice(LANES * h, LANES * (h + 1))
                    qm = jnp.where(hms[hh], (qp * (HEAD_DIM ** -0.5)).astype(BF16), eq_ref[:, hsl])
                    km = jnp.where(hms[hh], kp.astype(BF16), ek_ref[:, hsl])
                    st = _dot(km, qm, 1, 1)
                    if diagonal:
                        st = jnp.where(keys_first, st, NEG)
                    pt = jnp.exp(st - lse_all[h:h + 1, :])
                    domb = jnp.where(hms[hh], dop, 0.0).astype(BF16)
                    dv_p = dv_p + _dot(pt.astype(BF16), domb, 1, 0)
                    dpt = _dot(vpb, domb, 1, 1)
                    dst = pt * (dpt - dl_all[h:h + 1, :])
                    dstb = dst.astype(BF16)
                    dk_p = dk_p + jnp.where(hms[hh], _dot(dstb, qm, 1, 0), 0.0)
                    dqt_p = dqt_p + _dot(jnp.where(hrows[hh], kt, 0.0), dstb, 1, 0)
                    part = dst[:, 0:LANES]
                    for c in range(1, tq // LANES):
                        part = part + dst[:, LANES * c:LANES * (c + 1)]
                    dfks.append(part)
                    dfqs.append(jnp.sum(dst, axis=0, keepdims=True))
                dvs.append(dv_p)
                dks.append(dk_p)
                dqts.append(dqt_p)
            dv_s[...] += jnp.concatenate(dvs, axis=1)
            dk_s[...] += jnp.concatenate(dks, axis=1)
            for h in range(ATT_HEADS):
                dfk_s[h] += dfks[h]
            cols = pl.ds(pl.multiple_of(i * tq, tq), tq)
            dqt_ref[:, cols] += jnp.concatenate(dqts, axis=0) * (HEAD_DIM ** -0.5)
            dfq_ref[:, cols] += jnp.concatenate(dfqs + [jnp.zeros((SUBLANES - ATT_HEADS, tq), F32)], axis=0)

        @pl.when(i > j)
        def _():
            step(False)

        @pl.when(i == j)
        def _():
            step(True)

        @pl.when(i == nq - 1)
        def _():
            dk_ref[...] = dk_s[...].astype(BF16)
            dv_ref[...] = dv_s[...].astype(BF16)
            lane = _iota((tq, LANES), 1)
            out = jnp.zeros((tq, LANES), F32)
            for h in range(ATT_HEADS):
                out = jnp.where(lane == h, jnp.sum(dfk_s[h], axis=1, keepdims=True), out)
            dfk_ref[...] = out

    qi = lambda j, i: jnp.maximum(i, j)
    rows = pl.BlockSpec((SUBLANES, tq), lambda j, i: (0, qi(j, i)))
    return _hosted_call(
        body, comm, (nq, nq), name=name,
        in_specs=[pl.BlockSpec((tq, w), lambda j, i: (qi(j, i), cq)),
                  pl.BlockSpec((tq, w), lambda j, i: (j, ck)),
                  pl.BlockSpec((tq, w), lambda j, i: (j, cv)),
                  pl.BlockSpec((tq, ATT_HEADS * LANES), lambda j, i: (qi(j, i), 0)),
                  pl.BlockSpec((tq, ATT_HEADS * LANES), lambda j, i: (j, 0)),
                  pl.BlockSpec((tq, w), lambda j, i: (qi(j, i), cdo)),
                  rows, rows],
        out_specs=[pl.BlockSpec((tq, w), lambda j, i: (j, 0)), pl.BlockSpec((tq, w), lambda j, i: (j, 0)),
                   pl.BlockSpec((tq, LANES), lambda j, i: (j, 0)),
                   pl.BlockSpec((w, t), lambda j, i: (0, 0)), pl.BlockSpec((SUBLANES, t), lambda j, i: (0, 0))],
        out_shape=[jax.ShapeDtypeStruct((t, w), BF16), jax.ShapeDtypeStruct((t, w), BF16),
                   jax.ShapeDtypeStruct((t, LANES), F32),
                   jax.ShapeDtypeStruct((w, t), F32), jax.ShapeDtypeStruct((SUBLANES, t), F32)],
        scratch_shapes=[pltpu.VMEM((tq, w), F32), pltpu.VMEM((tq, w), F32),
                        pltpu.VMEM((ATT_HEADS, tq, LANES), F32)],
        args=[hbuf, hbuf, hbuf, eq, ek, dymix, lse_rows, delta_rows])


GROUP_W = SSD_WIDTH // SSD_GROUPS
HEADS_PER_GROUP = SSD_HEADS // SSD_GROUPS


def _ssd_chunk_common(xr, prev8, sm, cw, cb, dtb, avec):
    c = _conv_taps(xr, prev8, cw, cb)
    sig = _sigmoid(c)
    xa = c * sig
    dt = _softplus(sm + dtb)
    a = dt * avec
    acum = _cumsum_rows(a)
    return c, sig, xa, dt, acum


def _ssd_head_cols(acum, acum_t):
    cols = [_col(acum, LANE_DT + h) for h in range(SSD_HEADS)]
    rows = [_row(acum_t, LANE_DT + h) for h in range(SSD_HEADS)]
    return cols, rows


def _expand_heads(vals, width):
    rows = vals[0].shape[0]
    colhead = _iota((rows, width), 1) // HEAD_DIM
    out = jnp.broadcast_to(vals[0], (rows, width))
    for h in range(1, len(vals)):
        out = jnp.where(colhead == h, vals[h], out)
    return out


def _ssd_decays(cols, g):
    mine = cols[HEADS_PER_GROUP * g:HEADS_PER_GROUP * (g + 1)]
    n = mine[0].shape[0]
    atots = [c[n - 1:n, :] for c in mine]
    e = _expand_heads([jnp.exp(c) for c in mine], GROUP_W)
    dec = _expand_heads([jnp.exp(t - c) for c, t in zip(mine, atots)], GROUP_W)
    etot = _expand_heads([jnp.exp(t) for t in atots], GROUP_W)
    return e, dec, etot


def _ssd_ldec(cols, rows, h, tril):
    return jnp.exp(jnp.where(tril, cols[h] - rows[h], NEG))


def _ssd_fwd(hbuf, conv_w, conv_b, dtb_vec, a_vec, d_exp, norm_g, *, name):
    t = hbuf.shape[0]
    L = SSD_CHUNK
    nc = t // L
    hb = L // SUBLANES
    cs = COL_SMALL // LANES
    cz = COL_Z // SSD_WIDTH

    def body(x_ref, xp_ref, z_ref, s_ref, cw_ref, cb_ref, dtb_ref, av_ref, dx_ref, ng_ref,
             yc_ref, y_ref, st_ref, state):
        i = pl.program_id(0)

        @pl.when(i == 0)
        def _():
            state[...] = jnp.zeros_like(state)

        prev = jnp.where(i == 0, 0.0, xp_ref[...])
        _, _, xa, dt, acum = _ssd_chunk_common(x_ref[...], prev, s_ref[...], cw_ref[...], cb_ref[...],
                                               dtb_ref[...], av_ref[...])
        cols, rows = _ssd_head_cols(acum, acum.T)
        xs = xa[:, :SSD_WIDTH]
        xdt = xs * _head_expand(dt, LANE_DT, SSD_HEADS, SSD_WIDTH)
        tril = _iota((L, L), 0) >= _iota((L, L), 1)
        lane = _iota((1, LANES), 1)
        ys = []
        for g in range(SSD_GROUPS):
            bg = xa[:, SSD_WIDTH + SSD_STATE * g:SSD_WIDTH + SSD_STATE * (g + 1)].astype(BF16)
            cg = xa[:, SSD_WIDTH + SSD_STATE * (SSD_GROUPS + g):SSD_WIDTH + SSD_STATE * (SSD_GROUPS + g + 1)].astype(BF16)
            gm = _dot(cg, bg, 1, 1)
            e, dec, etot = _ssd_decays(cols, g)
            s_in = state[g]
            st_ref[0, g] = s_in
            xg = xdt[:, GROUP_W * g:GROUP_W * (g + 1)]
            y_off = e * _dot(cg, s_in.astype(BF16), 1, 0)
            state[g] = etot * s_in + _dot(bg, (dec * xg).astype(BF16), 0, 0)
            for pr in range(2):
                xp = xg[:, LANES * pr:LANES * (pr + 1)].astype(BF16)
                outs = []
                for hh in range(2):
                    h = HEADS_PER_GROUP * g + 2 * pr + hh
                    m = gm * _ssd_ldec(cols, rows, h, tril)
                    outs.append(_dot(m.astype(BF16), xp, 1, 0))
                ys.append(jnp.where(lane < HEAD_DIM, outs[0], outs[1]) + y_off[:, LANES * pr:LANES * (pr + 1)])
        y = jnp.concatenate(ys, axis=1)
        y_ref[...] = y
        yd = y + dx_ref[...] * xs
        zz = z_ref[...]
        y2 = yd * zz * _sigmoid(zz)
        ng = ng_ref[...]
        outs = []
        for g in range(SSD_GROUPS):
            yg = y2[:, GROUP_W * g:GROUP_W * (g + 1)]
            rs = lax.rsqrt(jnp.mean(yg * yg, axis=1, keepdims=True) + RMS_EPS)
            outs.append(yg * rs * ng[:, GROUP_W * g:GROUP_W * (g + 1)])
        yc_ref[...] = jnp.concatenate(outs, axis=1)

    cdim = SSD_CONV_DIM
    vecc = pl.BlockSpec((1, cdim), lambda i: (0, 0))
    vecl = pl.BlockSpec((1, LANES), lambda i: (0, 0))
    vecw = pl.BlockSpec((1, SSD_WIDTH), lambda i: (0, 0))
    roww = pl.BlockSpec((L, SSD_WIDTH), lambda i: (i, 0))
    return pl.pallas_call(
        body, name=name, grid=(nc,),
        in_specs=[pl.BlockSpec((L, cdim), lambda i: (i, 0)),
                  pl.BlockSpec((SUBLANES, cdim), lambda i: (jnp.maximum(i * hb - 1, 0), 0)),
                  pl.BlockSpec((L, SSD_WIDTH), lambda i: (i, cz)),
                  pl.BlockSpec((L, LANES), lambda i: (i, cs)),
                  pl.BlockSpec((CONV_K, cdim), lambda i: (0, 0)), vecc, vecl, vecl, vecw, vecw],
        out_specs=[roww, roww, pl.BlockSpec((1, SSD_GROUPS, SSD_STATE, GROUP_W), lambda i: (i, 0, 0, 0))],
        out_shape=[jax.ShapeDtypeStruct((t, SSD_WIDTH), F32), jax.ShapeDtypeStruct((t, SSD_WIDTH), F32),
                   jax.ShapeDtypeStruct((nc, SSD_GROUPS, SSD_STATE, GROUP_W), F32)],
        scratch_shapes=[pltpu.VMEM((SSD_GROUPS, SSD_STATE, GROUP_W), F32)],
        compiler_params=_params(1),
    )(hbuf, hbuf, hbuf, hbuf, conv_w, conv_b, dtb_vec, a_vec, d_exp, norm_g)


def _ssd_bwd(dymix, hbuf, y_ssd, states, conv_w, conv_b, dtb_vec, a_vec, d_exp, norm_g, *, name):
    t = hbuf.shape[0]
    L = SSD_CHUNK
    nc = t // L
    hb = L // SUBLANES
    cs = COL_SMALL // LANES
    cz = COL_Z // SSD_WIDTH
    cdy = (LRU_WIDTH + ATT_WIDTH) // SSD_WIDTH
    cdim = SSD_CONV_DIM

    def body(dyc_ref, x_ref, xp_ref, z_ref, s_ref, y_ref, st_ref, cw_ref, cb_ref, dtb_ref, av_ref, dx_ref, ng_ref,
             dxr_ref, dz_ref, dsm_ref, dng_ref, dd_ref, da_ref, ddtb_ref, dcw_ref, dcb_ref,
             dstate, dnext):
        i = pl.program_id(0)
        ic = nc - 1 - i

        @pl.when(i == 0)
        def _():
            dstate[...] = jnp.zeros_like(dstate)
            dnext[...] = jnp.zeros_like(dnext)
            for ref in (dng_ref, dd_ref, da_ref, ddtb_ref, dcw_ref, dcb_ref):
                ref[...] = jnp.zeros_like(ref)

        xr = x_ref[...]
        sm = s_ref[...]
        prev = jnp.where(ic == 0, 0.0, xp_ref[...])
        avec = av_ref[...]
        c, sig, xa, dt, acum = _ssd_chunk_common(xr, prev, sm, cw_ref[...], cb_ref[...], dtb_ref[...], avec)
        cols, rows = _ssd_head_cols(acum, acum.T)
        xs = xa[:, :SSD_WIDTH]
        dtx = _head_expand(dt, LANE_DT, SSD_HEADS, SSD_WIDTH)
        xdt = xs * dtx
        tril = _iota((L, L), 0) >= _iota((L, L), 1)
        lane = _iota((1, LANES), 1)
        hmasks = (lane < HEAD_DIM, lane >= HEAD_DIM)

        y = y_ref[...]
        dexp = dx_ref[...]
        yd = y + dexp * xs
        zz = z_ref[...]
        sz = _sigmoid(zz)
        siluz = zz * sz
        y2 = yd * siluz
        ng = ng_ref[...]
        dyc = dyc_ref[...]
        dy2s, dngs = [], []
        for g in range(SSD_GROUPS):
            sl = slice(GROUP_W * g, GROUP_W * (g + 1))
            yg = y2[:, sl]
            rs = lax.rsqrt(jnp.mean(yg * yg, axis=1, keepdims=True) + RMS_EPS)
            wv = dyc[:, sl] * ng[:, sl]
            dngs.append(jnp.sum(dyc[:, sl] * yg * rs, axis=0, keepdims=True))
            dy2s.append(rs * wv - yg * (rs * rs * rs) * jnp.mean(wv * yg, axis=1, keepdims=True))
        dy2 = jnp.concatenate(dy2s, axis=1)
        dng_ref[...] += jnp.concatenate(dngs, axis=1)
        dz_ref[...] = (dy2 * yd * (sz * (1.0 + zz * (1.0 - sz)))).astype(BF16)
        dy = dy2 * siluz
        dd_ref[...] += jnp.sum(dy * xs, axis=0, keepdims=True)

        dxs, dbs, dcs = [], [], []
        datot = jnp.zeros((1, LANES), F32)
        lanes = _iota((L, LANES), 1)
        dacum = jnp.zeros((L, LANES), F32)
        for g in range(SSD_GROUPS):
            sl = slice(GROUP_W * g, GROUP_W * (g + 1))
            bg = xa[:, SSD_WIDTH + SSD_STATE * g:SSD_WIDTH + SSD_STATE * (g + 1)].astype(BF16)
            cg = xa[:, SSD_WIDTH + SSD_STATE * (SSD_GROUPS + g):SSD_WIDTH + SSD_STATE * (SSD_GROUPS + g + 1)].astype(BF16)
            gm = _dot(cg, bg, 1, 1)
            e, dec, etot = _ssd_decays(cols, g)
            s_in = st_ref[0, g]
            ds_out = dstate[g]
            dyg = dy[:, sl]
            xg = xdt[:, sl]
            edy = (e * dyg).astype(BF16)
            dstate[g] = etot * ds_out + _dot(cg, edy, 0, 0)
            dx_state = dec * _dot(bg, ds_out.astype(BF16), 1, 0)
            y_off = e * _dot(cg, s_in.astype(BF16), 1, 0)
            dacum = dacum + _head_reduce_group(dyg * y_off - xg * dx_state, g)
            dc_off = _dot(edy, s_in.astype(BF16), 1, 1)
            db_state = _dot((dec * xg).astype(BF16), ds_out.astype(BF16), 1, 1)
            dgsum = jnp.zeros((L, L), F32)
            dx_pairs = []
            for pr in range(2):
                psl = slice(LANES * pr, LANES * (pr + 1))
                xp = xg[:, psl]
                dyp = dyg[:, psl]
                dx_pair = jnp.zeros((L, LANES), F32)
                for hh in range(2):
                    h = HEADS_PER_GROUP * g + 2 * pr + hh
                    ldec = _ssd_ldec(cols, rows, h, tril)
                    dym = jnp.where(hmasks[hh], dyp, 0.0).astype(BF16)
                    xm = jnp.where(hmasks[hh], xp, 0.0).astype(BF16)
                    dx_pair = dx_pair + _dot((gm * ldec).astype(BF16), dym, 0, 0)
                    dml = _dot(dym, xm, 1, 1) * ldec
                    dgsum = dgsum + dml
                    qm = dml * gm
                    seg = jnp.sum(qm, axis=1, keepdims=True) - jnp.sum(qm.T, axis=1, keepdims=True)
                    dacum = dacum + jnp.where(lanes == LANE_DT + h, seg, 0.0)
                dx_pairs.append(dx_pair)
            dgb = dgsum.astype(BF16)
            dcs.append(_dot(dgb, bg, 1, 0) + dc_off)
            dbs.append(_dot(dgb, cg, 0, 0) + db_state)
            dxg = jnp.concatenate(dx_pairs, axis=1) + dx_state
            dxs.append(dxg)
            v = jnp.sum(dx_state * xg, axis=0, keepdims=True) + etot * jnp.sum(ds_out * s_in, axis=0, keepdims=True)
            datot = datot + _head_reduce_row(v, LANE_DT + HEADS_PER_GROUP * g, HEADS_PER_GROUP)
        dx = jnp.concatenate(dxs, axis=1)
        dacum = dacum + jnp.where(_iota((L, LANES), 0) == L - 1, datot, 0.0)
        da = _cumsum_rows(dacum, reverse=True)
        ddt = da * avec + _head_reduce(dx * xs, LANE_DT, SSD_HEADS)
        da_ref[...] += jnp.sum(da * dt, axis=0, keepdims=True)
        ddt_raw = ddt * _sigmoid(sm + dtb_ref[...])
        ddt_raw = jnp.where((lanes >= LANE_DT) & (lanes < LANE_DT + SSD_HEADS), ddt_raw, 0.0)
        dsm_ref[...] = ddt_raw
        ddtb_ref[...] += jnp.sum(ddt_raw, axis=0, keepdims=True)
        dxs_total = dx * dtx + dexp * dy
        dxa = jnp.concatenate([dxs_total] + dbs + dcs, axis=1)
        dc = dxa * (sig * (1.0 + c * (1.0 - sig)))
        dxr, dws = _conv_taps_bwd(dc, dnext[...], cw_ref[...], xr)
        dxr_ref[...] = dxr.astype(BF16)
        dcw_ref[...] += dws
        dcb_ref[...] += jnp.sum(dc, axis=0, keepdims=True)
        dnext[...] = dc[:SUBLANES]

    rev = lambda i: nc - 1 - i
    vecc = pl.BlockSpec((1, cdim), lambda i: (0, 0))
    vecl = pl.BlockSpec((1, LANES), lambda i: (0, 0))
    vecw = pl.BlockSpec((1, SSD_WIDTH), lambda i: (0, 0))
    cwspec = pl.BlockSpec((CONV_K, cdim), lambda i: (0, 0))
    roww = pl.BlockSpec((L, SSD_WIDTH), lambda i: (rev(i), 0))
    return pl.pallas_call(
        body, name=name, grid=(nc,),
        in_specs=[pl.BlockSpec((L, SSD_WIDTH), lambda i: (rev(i), cdy)),
                  pl.BlockSpec((L, cdim), lambda i: (rev(i), 0)),
                  pl.BlockSpec((SUBLANES, cdim), lambda i: (jnp.maximum(rev(i) * hb - 1, 0), 0)),
                  pl.BlockSpec((L, SSD_WIDTH), lambda i: (rev(i), cz)),
                  pl.BlockSpec((L, LANES), lambda i: (rev(i), cs)),
                  roww,
                  pl.BlockSpec((1, SSD_GROUPS, SSD_STATE, GROUP_W), lambda i: (rev(i), 0, 0, 0)),
                  cwspec, vecc, vecl, vecl, vecw, vecw],
        out_specs=[pl.BlockSpec((L, cdim), lambda i: (rev(i), 0)), roww,
                   pl.BlockSpec((L, LANES), lambda i: (rev(i), 0)),
                   vecw, vecw, vecl, vecl, cwspec, vecc],
        out_shape=[jax.ShapeDtypeStruct((t, cdim), BF16), jax.ShapeDtypeStruct((t, SSD_WIDTH), BF16),
                   jax.ShapeDtypeStruct((t, LANES), F32),
                   jax.ShapeDtypeStruct((1, SSD_WIDTH), F32), jax.ShapeDtypeStruct((1, SSD_WIDTH), F32),
                   jax.ShapeDtypeStruct((1, LANES), F32), jax.ShapeDtypeStruct((1, LANES), F32),
                   jax.ShapeDtypeStruct((CONV_K, cdim), F32), jax.ShapeDtypeStruct((1, cdim), F32)],
        scratch_shapes=[pltpu.VMEM((SSD_GROUPS, SSD_STATE, GROUP_W), F32), pltpu.VMEM((SUBLANES, cdim), F32)],
        compiler_params=_params(1),
    )(dymix, hbuf, hbuf, hbuf, hbuf, y_ssd, states, conv_w, conv_b, dtb_vec, a_vec, d_exp, norm_g)


def _head_reduce_group(x, g):
    return _head_reduce(x, LANE_DT + HEADS_PER_GROUP * g, HEADS_PER_GROUP)


def _head_reduce_row(v, lane0, nheads):
    colhead = _iota(v.shape, 1) // HEAD_DIM
    lane = _iota((1, LANES), 1)
    out = jnp.zeros((1, LANES), F32)
    for h in range(nheads):
        s = jnp.sum(jnp.where(colhead == h, v, 0.0), axis=1, keepdims=True)
        out = jnp.where(lane == lane0 + h, s, out)
    return out


def _exchange(inps, axes, *, swap=False, name):
    n = 2 ** len(axes)
    assert not swap or n == 2
    counts = [a.shape[0] for a in inps]
    out_shapes = [jax.ShapeDtypeStruct(a.shape if swap else (n,) + a.shape, a.dtype) for a in inps]
    units = sum(counts)
    na = len(inps)

    def body(*refs):
        in_refs, out_refs = refs[:na], refs[na:2 * na]
        send_sems, recv_sems, local_sems = refs[2 * na:]
        pos = {ax: lax.axis_index(ax) for ax in MESH_AXES}

        def slot_of(coord):
            s = 0
            for ax in axes:
                s = s * 2 + coord[ax]
            return s

        me = slot_of(pos)
        copies = []
        unit = 0
        for a in range(na):
            for it in range(counts[a]):
                dst = out_refs[a].at[it] if swap else out_refs[a].at[me, it]
                if not swap:
                    cp = pltpu.make_async_copy(in_refs[a].at[it], dst, local_sems.at[unit])
                    cp.start()
                    copies.append(cp)
                for delta in range(1, n):
                    coord = dict(pos)
                    for b, ax in enumerate(reversed(axes)):
                        if (delta >> b) & 1:
                            coord[ax] = 1 - pos[ax]
                    k = unit * (n - 1) + delta - 1
                    cp = pltpu.make_async_remote_copy(
                        src_ref=in_refs[a].at[it], dst_ref=dst,
                        send_sem=send_sems.at[k], recv_sem=recv_sems.at[k],
                        device_id=(coord["x"], coord["y"], coord["c"]), device_id_type=pl.DeviceIdType.MESH)
                    cp.start()
                    copies.append(cp)
                unit += 1
        for cp in copies:
            cp.wait()

    any_spec = pl.BlockSpec(memory_space=pl.ANY)
    return pl.pallas_call(
        body, name=name,
        in_specs=[any_spec] * na, out_specs=[any_spec] * na, out_shape=out_shapes,
        scratch_shapes=[pltpu.SemaphoreType.DMA((units * (n - 1),)), pltpu.SemaphoreType.DMA((units * (n - 1),)),
                        pltpu.SemaphoreType.DMA((units,))],
    )(*inps)


class _Comm:
    def __init__(self, arrays, out_shapes, n_own, start, finish, base=0, middle=None):
        self.arrays, self.out_shapes, self.start, self.finish = arrays, out_shapes, start, finish
        self.middle = middle or (lambda *refs: None)
        self.base, self.n_own, self.n_sems = base, n_own, base + n_own

    def specs(self):
        any_spec = pl.BlockSpec(memory_space=pl.ANY)
        sems = [pltpu.SemaphoreType.DMA((self.n_sems,)), pltpu.SemaphoreType.DMA((self.n_sems,))]
        return [any_spec] * len(self.arrays), [any_spec] * len(self.out_shapes), sems


def _run_comm(comm, *, name):
    na, no = len(comm.arrays), len(comm.out_shapes)

    def body(*refs):
        args = (refs[:na], refs[na:na + no]) + tuple(refs[na + no:])
        comm.start(*args)
        comm.middle(*args)
        comm.finish(*args)

    in_specs, out_specs, sems = comm.specs()
    return pl.pallas_call(body, name=name, in_specs=in_specs, out_specs=out_specs, out_shape=comm.out_shapes,
                          scratch_shapes=sems)(*comm.arrays)


def _chip_peer(x, y, d):
    px = 1 - x if d & 2 else x
    py = 1 - y if d & 1 else y
    return px, py, 2 * px + py


def _gather_layer_comm(srcs, li, base=0):
    counts = [s.shape[0] for s in srcs]
    units = [(a, it) for a in range(len(srcs)) for it in range(counts[a])]
    n_ici = 3 * len(units)
    out_shapes = [jax.ShapeDtypeStruct((N_CHIPS,) + s.shape, s.dtype) for s in srcs]

    def ici(ins, outs, ssem, rsem, u, d):
        x, y, c = (lax.axis_index(ax) for ax in MESH_AXES)
        a, it = units[u]
        px, py, _ = _chip_peer(x, y, d)
        k = base + 3 * u + d - 1
        return pltpu.make_async_remote_copy(
            src_ref=ins[a].at[it], dst_ref=outs[a].at[2 * x + y, it], send_sem=ssem.at[k], recv_sem=rsem.at[k],
            device_id=(px, py, c), device_id_type=pl.DeviceIdType.MESH)

    def arrived(ins, outs, ssem, rsem, u, d):
        x, y, c = (lax.axis_index(ax) for ax in MESH_AXES)
        a, it = units[u]
        _, _, pk = _chip_peer(x, y, d)
        k = base + 3 * u + d - 1
        return pltpu.make_async_remote_copy(
            src_ref=ins[a].at[it], dst_ref=outs[a].at[pk, it], send_sem=ssem.at[k], recv_sem=rsem.at[k],
            device_id=(x, y, c), device_id_type=pl.DeviceIdType.MESH)

    def forward(ins, outs, ssem, rsem, u, slot):
        x, y, c = (lax.axis_index(ax) for ax in MESH_AXES)
        a, it = units[u]
        pk = 2 * x + y if slot == 0 else _chip_peer(x, y, slot)[2]
        src = ins[a].at[it] if slot == 0 else outs[a].at[pk, it]
        k = base + n_ici + 4 * u + slot
        return pltpu.make_async_remote_copy(
            src_ref=src, dst_ref=outs[a].at[pk, it], send_sem=ssem.at[k], recv_sem=rsem.at[k],
            device_id=(x, y, 1 - c), device_id_type=pl.DeviceIdType.MESH)

    def start(ins, outs, ssem, rsem):
        for u in range(len(units)):
            forward(ins, outs, ssem, rsem, u, 0).start()

        @pl.when(lax.axis_index("c") == li)
        def _():
            for u in range(len(units)):
                for d in range(1, N_CHIPS):
                    ici(ins, outs, ssem, rsem, u, d).start()

    def middle(ins, outs, ssem, rsem):
        @pl.when(lax.axis_index("c") == li)
        def _():
            for u in range(len(units)):
                for d in range(1, N_CHIPS):
                    arrived(ins, outs, ssem, rsem, u, d).wait_recv()
                    forward(ins, outs, ssem, rsem, u, d).start()

    def finish(ins, outs, ssem, rsem):
        c = lax.axis_index("c")

        @pl.when(c == li)
        def _():
            for u in range(len(units)):
                for d in range(1, N_CHIPS):
                    ici(ins, outs, ssem, rsem, u, d).wait_send()
                    forward(ins, outs, ssem, rsem, u, d).wait_send()

        @pl.when(c != li)
        def _():
            for u in range(len(units)):
                for d in range(1, N_CHIPS):
                    forward(ins, outs, ssem, rsem, u, d).wait_recv()

        for u in range(len(units)):
            forward(ins, outs, ssem, rsem, u, 0).wait()

    return _Comm(srcs, out_shapes, n_ici + 4 * len(units), start, finish, base, middle)


def _reduce_chips_comm(sums, li, base=0):
    counts = [s.shape[0] for s in sums]
    units = [(a, it) for a in range(len(sums)) for it in range(counts[a])]
    out_shapes = [jax.ShapeDtypeStruct((N_CHIPS, s.shape[0]) + s.shape[2:], s.dtype) for s in sums]

    def copy(ins, outs, ssem, rsem, u, d):
        x, y, c = (lax.axis_index(ax) for ax in MESH_AXES)
        a, it = units[u]
        px, py, pk = _chip_peer(x, y, d)
        k = base + 3 * u + d - 1
        return pltpu.make_async_remote_copy(
            src_ref=ins[a].at[it, pk], dst_ref=outs[a].at[2 * x + y, it], send_sem=ssem.at[k], recv_sem=rsem.at[k],
            device_id=(px, py, c), device_id_type=pl.DeviceIdType.MESH)

    def start(ins, outs, ssem, rsem):
        @pl.when(lax.axis_index("c") == li)
        def _():
            for u in range(len(units)):
                for d in range(1, N_CHIPS):
                    copy(ins, outs, ssem, rsem, u, d).start()

    def finish(ins, outs, ssem, rsem):
        @pl.when(lax.axis_index("c") == li)
        def _():
            for u in range(len(units)):
                for d in range(1, N_CHIPS):
                    copy(ins, outs, ssem, rsem, u, d).wait()

    return _Comm(sums, out_shapes, 3 * len(units), start, finish, base)


def _sum_slots(buf, out_dtype, *, name):
    n, rows, cols = buf.shape
    tm = _pick(rows, (512, 256, 128, 8))
    if rows % tm:
        tm = rows

    def body(b_ref, o_ref):
        acc = b_ref[0].astype(F32)
        for s in range(1, n):
            acc = acc + b_ref[s].astype(F32)
        o_ref[...] = acc.astype(out_dtype)

    return pl.pallas_call(
        body, name=name, grid=(pl.cdiv(rows, tm),),
        in_specs=[pl.BlockSpec((n, tm, cols), lambda i: (0, i, 0))],
        out_specs=pl.BlockSpec((tm, cols), lambda i: (i, 0)),
        out_shape=jax.ShapeDtypeStruct((rows, cols), out_dtype),
        compiler_params=_params(1),
    )(buf)


def _sum_pair(a, b, out_dtype, *, name):
    shape = a.shape
    cols = shape[-1]
    a2, b2 = a.reshape(-1, cols), b.reshape(-1, cols)
    rows = a2.shape[0]
    tm = _pick(rows, (512, 256, 128, 8))

    def body(a_ref, b_ref, o_ref):
        o_ref[...] = (a_ref[...].astype(F32) + b_ref[...].astype(F32)).astype(out_dtype)

    spec = pl.BlockSpec((tm, cols), lambda i: (i, 0))
    return pl.pallas_call(
        body, name=name, grid=(rows // tm,), in_specs=[spec, spec], out_specs=spec,
        out_shape=jax.ShapeDtypeStruct((rows, cols), out_dtype), compiler_params=_params(1),
    )(a2, b2).reshape(shape)


def _adamw(w, g, m, v, *, name):
    shape = w.shape
    cols = shape[-1]
    rows = w.size // cols
    w2, g2, m2, v2 = (a.reshape(rows, cols) for a in (w, g, m, v))
    tm = _pick(rows, (256, 128, 64, 32, 16, 8))
    if rows % tm:
        tm = rows
    bc1 = 1.0 - ADAM_B1 ** ADAM_STEP
    bc2 = 1.0 - ADAM_B2 ** ADAM_STEP

    def body(w_ref, g_ref, m_ref, v_ref, d_ref, nm_ref, nv_ref):
        gg = g_ref[...]
        mm = ADAM_B1 * m_ref[...] + (1.0 - ADAM_B1) * gg
        vv = ADAM_B2 * v_ref[...] + (1.0 - ADAM_B2) * (gg * gg)
        m_hat = mm / bc1
        v_hat = vv / bc2
        d_ref[...] = -ADAM_LR * (m_hat / (jnp.sqrt(v_hat) + ADAM_EPS) + ADAM_WD * w_ref[...])
        nm_ref[...] = mm
        nv_ref[...] = vv

    spec = pl.BlockSpec((tm, cols), lambda i: (i, 0))
    o = jax.ShapeDtypeStruct((rows, cols), F32)
    outs = pl.pallas_call(
        body, name=name, grid=(rows // tm,), in_specs=[spec] * 4, out_specs=[spec] * 3, out_shape=[o] * 3,
        compiler_params=_params(1),
    )(w2, g2, m2, v2)
    return tuple(a.reshape(shape) for a in outs)


def _layer_fwd(li, x, xb, pb, W, up=None, att=None):
    nm = lambda s: f"l{li}_{s}"
    sv = {"x_in_b": xb}
    (g1, u1, a1), got = _mm_swiglu(xb, W["ffn1_wg"], W["ffn1_wu"], comm=up[0] if up else None, name=nm("ffn1_up"))
    if up:
        W = {**W, **up[1](got)}
    x1, x1b, xh1, rs1 = _mm_ln(a1, W["ffn1_wd"], x, W["ln1_g"], W["ln1_b"], rscale=ALPHA, mscale=0.5, name=nm("ffn1_down_ln"))
    hbuf = _mm(x1b, W["w_in_p"], name=nm("in_proj"))
    ya, lu, lr, lig, la, lh = _lru_fwd(hbuf, W["lru_conv_w"], W["lru_conv_b"], W["lru_wa_bd"], W["lru_ba"],
                                       W["lru_wx_bd"], W["lru_bx"], W["lru_lambda"], name=nm("lru_fwd"))
    eq, ek = _fox_prep(hbuf, W["fox_bf_vec"], name=nm("fox_prep"))
    (yb, lse_rows), got = _fox_fwd(hbuf, eq, ek, comm=att[0] if att else None, name=nm("fox_fwd"))
    if att:
        W = {**W, **att[1](got)}
    yc, yssd, states = _ssd_fwd(hbuf, W["ssd_conv_w"], W["ssd_conv_b"], W["ssd_dtb_vec"], W["ssd_a_vec"],
                                W["ssd_d_exp"], W["ssd_norm_g"], name=nm("ssd_fwd"))
    ymix = _assemble([ya, yb, yc], D_MODEL, name=nm("y_mix"))
    x2, x2b, xh2, rs2 = _mm_ln(ymix, W["w_out"], x1, W["ln2_g"], W["ln2_b"], rscale=ALPHA, mscale=1.0, name=nm("out_proj_ln"))
    (g2, u2, a2), _ = _mm_swiglu(x2b, W["ffn2_wg"], W["ffn2_wu"], name=nm("ffn2_up"))
    x3, x3b, xh3, rs3 = _mm_ln(a2, W["ffn2_wd"], x2, W["ln3_g"], W["ln3_b"], rscale=ALPHA, mscale=0.5, name=nm("ffn2_down_ln"))
    x4, x4b, sg, e = _mm_pe(x3, x3b, pb, W["pe_gate_w"], W["pe_gate_b"], W["pe_proj"], name=nm("ple"))
    sv.update(g1=g1, u1=u1, a1=a1, x1b=x1b, xh1=xh1, rs1=rs1, hbuf=hbuf, lu=lu, lr=lr, lig=lig, la=la, lh=lh,
              eq=eq, ek=ek, lse_rows=lse_rows, yb=yb, yssd=yssd, states=states, ymix=ymix, x2b=x2b, xh2=xh2, rs2=rs2,
              g2=g2, u2=u2, a2=a2, x3b=x3b, xh3=xh3, rs3=rs3, sg=sg, e=e, pb=pb)
    return x4, x4b, sv, W


def _layer_bwd(li, dx4, sv, W, comm=None, late=None, last=None):
    nm = lambda s: f"l{li}_{s}"
    G = {}
    dgp, de, dbg = _pe_bwd_elem(dx4, sv["sg"], sv["e"], name=nm("ple_bwd"))
    G["pe_gate_b"] = dbg
    G["pe_gate_w"] = _mm(sv["x3b"], dgp, ta=True, out_dtype=BF16, name=nm("d_pe_gate_w"))
    G["pe_proj"] = _mm(sv["pb"], de, ta=True, out_dtype=BF16, chip_cols=True, name=nm("d_pe_proj"))
    dr3, dr3b, G["ln3_g"], G["ln3_b"] = _bwd_proj([(dgp, W["pe_gate_w"])], dx4, rscale=1.0,
                                                  ln=(sv["xh3"], sv["rs3"], W["ln3_g"]), name=nm("ln3_bwd"))
    G["ffn2_wd"] = _mm(sv["a2"], dr3b, ta=True, scale=0.5, out_dtype=BF16, name=nm("d_ffn2_wd"))
    dg2, du2 = _mm_swiglu_bwd(dr3b, W["ffn2_wd"], sv["g2"], sv["u2"], scale=0.5, name=nm("ffn2_act_bwd"))
    G["ffn2_wg"] = _mm(sv["x2b"], dg2, ta=True, out_dtype=BF16, chip_cols=True, name=nm("d_ffn2_wg"))
    G["ffn2_wu"] = _mm(sv["x2b"], du2, ta=True, out_dtype=BF16, chip_cols=True, name=nm("d_ffn2_wu"))
    dr2, dr2b, G["ln2_g"], G["ln2_b"] = _bwd_proj([(dg2, W["ffn2_wg"]), (du2, W["ffn2_wu"])], dr3, rscale=ALPHA,
                                                  ln=(sv["xh2"], sv["rs2"], W["ln2_g"]), name=nm("ln2_bwd"))
    G["w_out"] = _mm(sv["ymix"], dr2b, ta=True, out_dtype=BF16, name=nm("d_w_out"))
    dymix = _mm(dr2b, W["w_out"], tb=True, name=nm("d_ymix"))
    hbuf = sv["hbuf"]
    (dur, dgr, G["lru_conv_w"], G["lru_conv_b"], G["lru_wa_bd"], G["lru_ba"], G["lru_wx_bd"], G["lru_bx"],
     G["lru_lambda"]) = _lru_bwd(dymix, hbuf, sv["lu"], sv["lr"], sv["lig"], sv["la"], sv["lh"],
                                 W["lru_conv_w"], W["lru_wa_bd"], W["lru_wx_bd"], W["lru_lambda"], name=nm("lru_bwd"))
    delta = _fox_delta(dymix, sv["yb"], name=nm("fox_delta"))
    delta_rows = jnp.pad(delta[:, :ATT_HEADS].T, ((0, SUBLANES - ATT_HEADS), (0, 0)))
    comm = _merge_comms([comm, late(G) if late else None])
    (dk, dv, dfk, dqt, dfq), comm_out = _fox_bwd(hbuf, sv["eq"], sv["ek"], dymix, sv["lse_rows"], delta_rows,
                                                 comm=comm, name=nm("fox_bwd"))
    dq = dqt.T
    dfc = jnp.pad(dfq[:ATT_HEADS].T, ((0, 0), (0, LANES - ATT_HEADS))) - dfk
    dsm_f, G["fox_bf_vec"] = _fox_post(dfc, hbuf, W["fox_bf_vec"], name=nm("fox_post"))
    (dxr, dz, dsm_dt, G["ssd_norm_g"], G["ssd_d_exp"], G["ssd_a_vec"], G["ssd_dtb_vec"], G["ssd_conv_w"],
     G["ssd_conv_b"]) = _ssd_bwd(dymix, hbuf, sv["yssd"], sv["states"], W["ssd_conv_w"], W["ssd_conv_b"],
                                 W["ssd_dtb_vec"], W["ssd_a_vec"], W["ssd_d_exp"], W["ssd_norm_g"], name=nm("ssd_bwd"))
    dh = _assemble([dxr, dz, dur, dgr, dq, dk, dv, dsm_f + dsm_dt], H_WIDTH, name=nm("d_h"))
    G["w_in_p"] = _mm(sv["x1b"], dh, ta=True, name=nm("d_w_in"))
    dr1, dr1b, G["ln1_g"], G["ln1_b"] = _bwd_proj([(dh, W["w_in_p"])], dr2, rscale=ALPHA,
                                                  ln=(sv["xh1"], sv["rs1"], W["ln1_g"]), name=nm("ln1_bwd"))
    G["ffn1_wd"] = _mm(sv["a1"], dr1b, ta=True, scale=0.5, out_dtype=BF16, name=nm("d_ffn1_wd"))
    dg1, du1 = _mm_swiglu_bwd(dr1b, W["ffn1_wd"], sv["g1"], sv["u1"], scale=0.5, name=nm("ffn1_act_bwd"))
    G["ffn1_wg"] = _mm(sv["x_in_b"], dg1, ta=True, out_dtype=BF16, chip_cols=True, name=nm("d_ffn1_wg"))
    G["ffn1_wu"] = _mm(sv["x_in_b"], du1, ta=True, out_dtype=BF16, chip_cols=True, name=nm("d_ffn1_wu"))
    dx_in, *last_out = _bwd_proj([(dg1, W["ffn1_wg"]), (du1, W["ffn1_wu"])], dr1, rscale=ALPHA, ln=None,
                                 comm=last(G) if last else None, name=nm("x_in_bwd"))
    return dx_in, G, comm_out, (last_out[0] if last_out else None)


def _block_diag(w):
    n, b, _ = w.shape
    eye = jnp.eye(n, dtype=w.dtype)
    return (eye[:, None, :, None] * w[:, :, None, :]).reshape(n * b, n * b)


def _block_diag_extract(m):
    n, b = LRU_HEADS, HEAD_DIM
    return jnp.stack([m[b * i:b * (i + 1), b * i:b * (i + 1)] for i in range(n)])


def _lane_vec(v, lane0):
    return jnp.pad(v.astype(F32), (lane0, LANES - lane0 - v.shape[0])).reshape(1, LANES)


def _w_in_permute(w):
    d = w.shape[0]
    z = lambda n: jnp.zeros((d, n), w.dtype)
    return jnp.concatenate([w[:, 1796:2820], w[:, 1284:1796], w[:, 0:512], w[:, 512:1280],
                            w[:, 1280:1284], w[:, 2820:2828], z(LANES - 12), z(H_WIDTH - COL_SMALL - LANES)], axis=1)


def _w_in_unpermute(wp):
    return jnp.concatenate([wp[:, COL_U:COL_Q], wp[:, COL_Q:COL_SMALL], wp[:, COL_SMALL:COL_SMALL + 4],
                            wp[:, COL_Z:COL_U], wp[:, COL_XBC:COL_Z], wp[:, COL_SMALL + 4:COL_SMALL + 12]], axis=1)


def _big_weights(chipw):
    W = {}
    for n, w in chipw.items():
        if n in ("ffn1_wg", "ffn1_wu", "ffn2_wg", "ffn2_wu"):
            W[n] = w
        elif n in ("ffn1_wd", "ffn2_wd", "w_out", "pe_gate_w"):
            W[n] = w.reshape(-1, D_MODEL)
        elif n == "pe_proj":
            W[n] = jnp.moveaxis(w, 0, 1).reshape(PLE_DIM, D_MODEL)
        else:
            w_in = jnp.moveaxis(w[:, :, :IN_WIDTH // N_CHIPS], 0, 1).reshape(D_MODEL, IN_WIDTH)
            W["w_in_p"] = _w_in_permute(w_in)
    return W


def _small_weights(li, small):
    g = lambda n: small[n][li]
    W = {n: g(n) for n in ("ln1_g", "ln1_b", "ln2_g", "ln2_b", "ln3_g", "ln3_b", "pe_gate_b", "lru_conv_w",
                           "ssd_conv_w")}
    for n in ("lru_conv_b", "lru_ba", "lru_bx", "lru_lambda", "ssd_conv_b", "ssd_norm_g"):
        W[n] = g(n).reshape(1, -1)
    W["lru_wa_bd"] = _block_diag(g("lru_wa")).astype(BF16)
    W["lru_wx_bd"] = _block_diag(g("lru_wx")).astype(BF16)
    W["fox_bf_vec"] = _lane_vec(g("fox_bf"), LANE_F)
    W["ssd_dtb_vec"] = _lane_vec(g("ssd_dt_bias"), LANE_DT)
    W["ssd_a_vec"] = _lane_vec(-jnp.exp(g("ssd_a_log")), LANE_DT)
    W["ssd_d_exp"] = jnp.repeat(g("ssd_d"), HEAD_DIM).reshape(1, SSD_WIDTH)
    return W


def _big_grad_by_chip(G, n):
    if n in ("ffn1_wg", "ffn1_wu", "ffn2_wg", "ffn2_wu", "pe_proj"):
        return G[n]
    if n in ("ffn1_wd", "ffn2_wd", "w_out", "pe_gate_w"):
        return G[n].reshape(N_CHIPS, -1, D_MODEL)
    share = IN_WIDTH // N_CHIPS
    d_w_in = jnp.moveaxis(_w_in_unpermute(G["w_in_p"]).reshape(D_MODEL, N_CHIPS, share), 1, 0)
    return jnp.pad(d_w_in.astype(BF16), ((0, 0), (0, 0), (0, SHARE - share)))


def _layer_small_grads(G, W):
    out = {n: G[n] for n in ("lru_conv_w", "ssd_conv_w")}
    for n in ("ln1_g", "ln1_b", "ln2_g", "ln2_b", "ln3_g", "ln3_b", "pe_gate_b", "lru_conv_b", "lru_ba", "lru_bx",
              "lru_lambda", "ssd_conv_b", "ssd_norm_g"):
        out[n] = G[n].reshape(-1)
    out["lru_wa"] = _block_diag_extract(G["lru_wa_bd"])
    out["lru_wx"] = _block_diag_extract(G["lru_wx_bd"])
    out["fox_bf"] = G["fox_bf_vec"][0, LANE_F:LANE_F + ATT_HEADS]
    out["ssd_dt_bias"] = G["ssd_dtb_vec"][0, LANE_DT:LANE_DT + SSD_HEADS]
    out["ssd_a_log"] = G["ssd_a_vec"][0, LANE_DT:LANE_DT + SSD_HEADS] * W["ssd_a_vec"][0, LANE_DT:LANE_DT + SSD_HEADS]
    out["ssd_d"] = G["ssd_d_exp"].reshape(SSD_HEADS, HEAD_DIM).sum(axis=1)
    return out


WEIGHTS = ['ln1_g', 'ln1_b', 'ffn1_wg', 'ffn1_wu', 'ffn1_wd', 'w_in', 'lru_conv_w', 'lru_conv_b', 'lru_wa', 'lru_ba',
           'lru_wx', 'lru_bx', 'lru_lambda', 'fox_bf', 'ssd_conv_w', 'ssd_conv_b', 'ssd_dt_bias', 'ssd_a_log', 'ssd_d',
           'ssd_norm_g', 'w_out', 'ln2_g', 'ln2_b', 'ffn2_wg', 'ffn2_wu', 'ffn2_wd', 'ln3_g', 'ln3_b', 'pe_proj',
           'pe_gate_w', 'pe_gate_b']
FIRST = ((("ffn1_wg",), 1), (("ffn1_wu",), 1))
NEXT = ((("w_in",), 1),
        (("ffn1_wd",), 0))
EARLY = FIRST + NEXT
LATE = ((("ffn2_wg",), 1), (("ffn2_wu",), 1),
        (("ffn2_wd",), 0),
        (("w_out",), None), (("pe_gate_w",), None),
        (("pe_proj",), None))
BIG = {n: pad for names, pad in EARLY + LATE for n in names}
SMALL_SHARDED = {'lru_conv_w': 2, 'ssd_conv_w': 2}


def _unshard(seg, axis):
    moved = jnp.moveaxis(seg, 0, axis)
    shp = list(moved.shape)
    shp[axis:axis + 2] = [shp[axis] * shp[axis + 1]]
    return moved.reshape(shp)


def _pad_axis(a, axis, size):
    if axis is None or a.shape[axis] == size:
        return a
    pads = [(0, 0)] * a.ndim
    pads[axis] = (0, size - a.shape[axis])
    return jnp.pad(a, pads)


PACK_TILE = SUBLANES * LANES


def _pack(arrs):
    rows = []
    for a in arrs:
        flat = a.astype(F32).reshape(-1)
        rows.append(jnp.pad(flat, (0, (-flat.shape[0]) % PACK_TILE)).reshape(-1, LANES))
    return jnp.concatenate(rows, axis=0)


def _unpack(packed, shapes):
    out, off = [], 0
    for s in shapes:
        n = math.prod(s)
        r = -(-n // PACK_TILE) * SUBLANES
        out.append(packed[off:off + r].reshape(-1)[:n].reshape(s))
        off += r
    return out


def kernel(x, p, ln1_g, ln1_b, ffn1_wg, ffn1_wu, ffn1_wd, w_in, lru_conv_w, lru_conv_b, lru_wa, lru_ba, lru_wx, lru_bx, lru_lambda, fox_bf, ssd_conv_w, ssd_conv_b, ssd_dt_bias, ssd_a_log, ssd_d, ssd_norm_g, w_out, ln2_g, ln2_b, ffn2_wg, ffn2_wu, ffn2_wd, ln3_g, ln3_b, pe_proj, pe_gate_w, pe_gate_b, loss_target, m_ln1_g, m_ln1_b, m_ffn1_wg, m_ffn1_wu, m_ffn1_wd, m_w_in, m_lru_conv_w, m_lru_conv_b, m_lru_wa, m_lru_ba, m_lru_wx, m_lru_bx, m_lru_lambda, m_fox_bf, m_ssd_conv_w, m_ssd_conv_b, m_ssd_dt_bias, m_ssd_a_log, m_ssd_d, m_ssd_norm_g, m_w_out, m_ln2_g, m_ln2_b, m_ffn2_wg, m_ffn2_wu, m_ffn2_wd, m_ln3_g, m_ln3_b, m_pe_proj, m_pe_gate_w, m_pe_gate_b, v_ln1_g, v_ln1_b, v_ffn1_wg, v_ffn1_wu, v_ffn1_wd, v_w_in, v_lru_conv_w, v_lru_conv_b, v_lru_wa, v_lru_ba, v_lru_wx, v_lru_bx, v_lru_lambda, v_fox_bf, v_ssd_conv_w, v_ssd_conv_b, v_ssd_dt_bias, v_ssd_a_log, v_ssd_d, v_ssd_norm_g, v_w_out, v_ln2_g, v_ln2_b, v_ffn2_wg, v_ffn2_wu, v_ffn2_wd, v_ln3_g, v_ln3_b, v_pe_proj, v_pe_gate_w, v_pe_gate_b):
    args = locals()
    w_loc = {n: args[n] for n in WEIGHTS}
    m_loc = {n: args["m_" + n] for n in WEIGHTS}
    v_loc = {n: args["v_" + n] for n in WEIGHTS}
    chip = 2 * lax.axis_index("x") + lax.axis_index("y")
    core = lax.axis_index("c")
    big = list(BIG)
    small_sh = list(SMALL_SHARDED)
    small_rep = [n for n in WEIGHTS if n not in BIG and n not in SMALL_SHARDED]

    def srcs_of(li, groups):
        return [jnp.stack([_pad_axis(w_loc[n][li].astype(BF16), pad, SHARE) for n in names]) for names, pad in groups]

    def gather_comm(li, groups, base=0):
        return _gather_layer_comm(srcs_of(li, groups), li, base)

    def chip_weights(gathered, groups):
        return _big_weights({n: g[:, j] for (names, _), g in zip(groups, gathered) for j, n in enumerate(names)})

    def pair_sums(G, groups, tag):
        gs = [jnp.stack([_big_grad_by_chip(G, n) for n in names]) for names, _ in groups]
        flat = [g.reshape((-1,) + g.shape[2:]) for g in gs]
        theirs = _exchange(flat, ("c",), swap=True, name=f"reduce_cores_{tag}")
        return [_sum_pair(f, r, BF16, name=f"reduce_cores_sum_{tag}_{gi}").reshape(g.shape)
                for gi, (f, r, g) in enumerate(zip(flat, theirs, gs))]

    def finish_reduce(quad, sums, li, groups, tag):
        quad = [lax.dynamic_update_index_in_dim(q, lax.dynamic_index_in_dim(s, chip, 1, keepdims=False), chip, 0)
                for q, s in zip(quad, sums)]
        red = [_sum_slots(q.reshape(N_CHIPS, -1, q.shape[-1]), F32,
                          name=f"reduce_chips_sum_{tag}_{gi}").reshape(q.shape[1:]) for gi, q in enumerate(quad)]
        theirs = _exchange(red, ("c",), swap=True, name=f"reduce_share_{tag}")
        out = {}
        for (names, _), r, rv in zip(groups, red, theirs):
            both = jnp.where(core == li, r, rv)
            for j, n in enumerate(names):
                out[n] = both[j]
        return out

    everything = EARLY + LATE
    first0 = _run_comm(gather_comm(0, FIRST), name="gather_w_l0")
    small = {n: w_loc[n] for n in small_rep}
    (sg,) = _exchange([_pack([w_loc[n] for n in small_sh])[None]], ("x", "y"), name="gather_conv_w")
    shards = [_unpack(sg[k, 0], [w_loc[n].shape for n in small_sh]) for k in range(N_CHIPS)]
    for j, n in enumerate(small_sh):
        small[n] = _unshard(jnp.stack([shards[k][j] for k in range(N_CHIPS)]), SMALL_SHARDED[n])

    W0 = {**_small_weights(0, small), **chip_weights(first0, FIRST)}
    late0_comm = gather_comm(0, LATE)
    early1 = []

    def in_attention0(got):
        early1.extend(got[len(LATE):])
        return chip_weights(got[:len(LATE)], LATE)

    xs = x[0]
    xs, xb, sv0, W0 = _layer_fwd(
        0, xs, xs.astype(BF16), p[0, 0].astype(BF16), W0,
        up=(gather_comm(0, NEXT), lambda got: chip_weights(got, NEXT)),
        att=(_merge_comms([late0_comm, gather_comm(1, EARLY, base=late0_comm.n_sems)]), in_attention0))
    W1 = {**_small_weights(1, small), **chip_weights(early1, EARLY)}
    xs, _, sv1, W1 = _layer_fwd(1, xs, xb, p[1, 0].astype(BF16), W1,
                                att=(gather_comm(1, LATE), lambda got: chip_weights(got, LATE)))
    dx, loss = _loss_kernel(xs, loss_target[0], name="loss")
    loss = lax.psum(loss[0, 0], MESH_AXES)
    dx, G1, _, _ = _layer_bwd(1, dx, sv1, W1)
    sums1 = pair_sums(G1, everything, "l1")
    comm1 = _reduce_chips_comm(sums1, 1)
    late_sums, early_sums = [], []

    def late0(G):
        late_sums.extend(pair_sums(G, LATE, "l0_late"))
        return _reduce_chips_comm(late_sums, 0, base=comm1.n_sems)

    def last0(G):
        early_sums.extend(pair_sums(G, EARLY, "l0"))
        return _reduce_chips_comm(early_sums, 0)

    grad_x, G0, quads, quads0 = _layer_bwd(0, dx, sv0, W0, comm=comm1, late=late0, last=last0)

    n1 = len(comm1.out_shapes)
    red = [{**finish_reduce(quads[n1:], late_sums, 0, LATE, "l0_late"),
            **finish_reduce(quads0, early_sums, 0, EARLY, "l0")},
           finish_reduce(quads[:n1], sums1, 1, everything, "l1")]
    g_red = {}
    for n in big:
        g = jnp.stack([red[li][n] for li in range(DEPTH)])
        g_red[n] = g[tuple(slice(0, s) for s in w_loc[n].shape)]
    small_l = [_layer_small_grads(G0, W0), _layer_small_grads(G1, W1)]
    g_small = {n: jnp.stack([small_l[li][n] for li in range(DEPTH)]) for n in small_l[0]}
    small_all = small_rep + small_sh
    sgp = _pack([g_small[n] for n in small_all])
    (sall,) = _exchange([sgp[None]], MESH_AXES, name="reduce_small")
    sred = _sum_slots(sall.reshape((2 ** len(MESH_AXES),) + sgp.shape), F32, name="reduce_small_sum")
    for n, g in zip(small_all, _unpack(sred, [g_small[n].shape for n in small_all])):
        if n in SMALL_SHARDED:
            width = w_loc[n].shape[-1]
            g = lax.dynamic_slice_in_dim(g, chip * width, width, axis=SMALL_SHARDED[n])
        g_red[n] = g

    delta, new_m, new_v = {}, {}, {}
    for n in big:
        delta[n], new_m[n], new_v[n] = _adamw(w_loc[n], g_red[n], m_loc[n], v_loc[n], name="adamw_" + n)
    shapes = [w_loc[n].shape for n in small_all]
    packs = [_pack([d[n] for n in small_all]) for d in (w_loc, g_red, m_loc, v_loc)]
    outs = _adamw(*packs, name="adamw_small")
    for d, o in zip((delta, new_m, new_v), outs):
        for n, a in zip(small_all, _unpack(o, shapes)):
            d[n] = a
    return (loss, grad_x[None], *[g_red[n] for n in WEIGHTS], *[delta[n] for n in WEIGHTS],
            *[new_m[n] for n in WEIGHTS], *[new_v[n] for n in WEIGHTS])
```

```python
import math

import jax
import jax.numpy as jnp
from jax import lax
from jax.experimental import pallas as pl
from jax.experimental.pallas import tpu as pltpu

F32 = jnp.float32
BF16 = jnp.bfloat16

D_MODEL = 1024
DEPTH = 2
PLE_DIM = 256
HEAD_DIM = 64
LRU_WIDTH = 256
LRU_HEADS = 4
LRU_C = 8.0
CONV_K = 4
ATT_WIDTH = 256
ATT_HEADS = 4
SSD_WIDTH = 512
SSD_HEADS = 8
SSD_GROUPS = 2
SSD_STATE = 128
SSD_CHUNK = 128
SSD_CONV_DIM = 1024
ALPHA = (2.0 * DEPTH) ** 0.25
LN_EPS = 1e-5
RMS_EPS = 1e-5
IN_WIDTH = 2828
ADAM_LR = 0.001
ADAM_B1 = 0.9
ADAM_B2 = 0.999
ADAM_EPS = 1e-08
ADAM_WD = 0.01
ADAM_STEP = 10

H_WIDTH = 3072
COL_XBC, COL_Z, COL_U, COL_G, COL_Q, COL_K, COL_V, COL_SMALL = 0, 1024, 1536, 1792, 2048, 2304, 2560, 2816
LANE_F = 0
LANE_DT = 4
LANES = 128
SUBLANES = 8
NEG = -1e30

VMEM_LIMIT = 48 * 1024 * 1024

N_CHIPS = 4
MESH_AXES = ("x", "y", "c")
SHARE = 768


def _params(n):
    return pltpu.CompilerParams(dimension_semantics=("arbitrary",) * n, vmem_limit_bytes=VMEM_LIMIT)


def _pick(n, cands):
    for c in cands:
        if n % c == 0:
            return c
    return n


def _iota(shape, dim):
    return lax.broadcasted_iota(jnp.int32, shape, dim)


def _shift_down(x, s, prev8):
    if s == 0:
        return x
    r = pltpu.roll(x, s, 0)
    pr = pltpu.roll(prev8, s, 0)
    head = jnp.where(_iota(pr.shape, 0) < s, pr, r[:SUBLANES])
    return jnp.concatenate([head, r[SUBLANES:]], axis=0)


def _shift_up(x, s, next8):
    if s == 0:
        return x
    n = x.shape[0]
    r = pltpu.roll(x, n - s, 0)
    nr = pltpu.roll(next8, SUBLANES - s, 0)
    tail = jnp.where(_iota(nr.shape, 0) >= SUBLANES - s, nr, r[n - SUBLANES:])
    return jnp.concatenate([r[:n - SUBLANES], tail], axis=0)


def _scan_fwd(a, b):
    n = a.shape[0]
    row = _iota(a.shape, 0)
    d = 1
    while d < n:
        keep = row >= d
        a_s = jnp.where(keep, pltpu.roll(a, d, 0), 1.0)
        b_s = jnp.where(keep, pltpu.roll(b, d, 0), 0.0)
        b = a * b_s + b
        a = a * a_s
        d *= 2
    return a, b


def _scan_bwd(a, b):
    n = a.shape[0]
    row = _iota(a.shape, 0)
    d = 1
    while d < n:
        keep = row < n - d
        a_s = jnp.where(keep, pltpu.roll(a, n - d, 0), 1.0)
        b_s = jnp.where(keep, pltpu.roll(b, n - d, 0), 0.0)
        b = a * b_s + b
        a = a * a_s
        d *= 2
    return a, b


def _cumsum_rows(x, reverse=False):
    n = x.shape[0]
    row = _iota(x.shape, 0)
    d = 1
    while d < n:
        if reverse:
            x = x + jnp.where(row < n - d, pltpu.roll(x, n - d, 0), 0.0)
        else:
            x = x + jnp.where(row >= d, pltpu.roll(x, d, 0), 0.0)
        d *= 2
    return x


def _col(x, lane):
    return jnp.sum(jnp.where(_iota(x.shape, 1) == lane, x, 0.0), axis=1, keepdims=True)


def _row(x, r):
    return jnp.sum(jnp.where(_iota(x.shape, 0) == r, x, 0.0), axis=0, keepdims=True)


def _sigmoid(x):
    return jax.nn.sigmoid(x)


def _softplus(x):
    return jnp.maximum(x, 0.0) + jnp.log(1.0 + jnp.exp(-jnp.abs(x)))


def _gelu_and_grad(x):
    c0 = math.sqrt(2.0 / math.pi)
    inner = c0 * (x + 0.044715 * x * x * x)
    t = jnp.tanh(inner)
    g = 0.5 * x * (1.0 + t)
    dg = 0.5 * (1.0 + t) + 0.5 * x * (1.0 - t * t) * c0 * (1.0 + 3.0 * 0.044715 * x * x)
    return g, dg


def _dot(a, b, ca, cb):
    return lax.dot_general(a, b, (((ca,), (cb,)), ((), ())), preferred_element_type=F32)


def _conv_taps(xr, prev8, w, bias):
    y = bias + w[CONV_K - 1:CONV_K, :] * xr
    for j in range(CONV_K - 1):
        y = y + w[j:j + 1, :] * _shift_down(xr, CONV_K - 1 - j, prev8)
    return y


def _conv_taps_bwd(dy, next8, w, xr):
    dx = None
    dws = []
    for j in range(CONV_K):
        sh = _shift_up(dy, CONV_K - 1 - j, next8)
        term = w[j:j + 1, :] * sh
        dx = term if dx is None else dx + term
        dws.append(jnp.sum(sh * xr, axis=0, keepdims=True))
    return dx, jnp.concatenate(dws, axis=0)


def _head_expand(v, lane0, nheads, width):
    rows = v.shape[0]
    colhead = _iota((rows, width), 1) // HEAD_DIM
    out = jnp.zeros((rows, width), F32)
    for h in range(nheads):
        out = jnp.where(colhead == h, _col(v, lane0 + h), out)
    return out


def _head_reduce(x, lane0, nheads):
    rows = x.shape[0]
    colhead = _iota(x.shape, 1) // HEAD_DIM
    lane = _iota((rows, LANES), 1)
    out = jnp.zeros((rows, LANES), F32)
    for h in range(nheads):
        s = jnp.sum(jnp.where(colhead == h, x, 0.0), axis=1, keepdims=True)
        out = jnp.where(lane == lane0 + h, s, out)
    return out


def _mm(a, b, *, ta=False, tb=False, scale=1.0, out_dtype=F32, chip_cols=False, name):
    if ta:
        kk, m = a.shape
    else:
        m, kk = a.shape
    n = b.shape[0] if tb else b.shape[1]
    tm = _pick(m, (1024, 512, 256, 128))
    tk = _pick(kk, (1024, 768, 512, 256, 128))
    nk = kk // tk
    dn_a = 0 if ta else 1
    dn_b = 1 if tb else 0
    share = n // N_CHIPS
    if chip_cols:
        tn = n
        out_spec = pl.BlockSpec((N_CHIPS, tm, share), lambda i, j, k: (0, i, 0))
        out_shape = jax.ShapeDtypeStruct((N_CHIPS, m, share), out_dtype)
    else:
        tn = _pick(n, (1024, 768, 512, 256, 128))
        out_spec = pl.BlockSpec((tm, tn), lambda i, j, k: (i, j))
        out_shape = jax.ShapeDtypeStruct((m, n), out_dtype)

    def body(a_ref, b_ref, o_ref, acc):
        k = pl.program_id(2)

        @pl.when(k == 0)
        def _():
            acc[...] = jnp.zeros_like(acc)

        acc[...] += _dot(a_ref[...].astype(BF16), b_ref[...].astype(BF16), dn_a, dn_b)

        @pl.when(k == nk - 1)
        def _():
            if chip_cols:
                for c in range(N_CHIPS):
                    o_ref[c] = (acc[:, share * c:share * (c + 1)] * scale).astype(out_dtype)
            else:
                o_ref[...] = (acc[...] * scale).astype(out_dtype)

    a_spec = pl.BlockSpec((tk, tm), lambda i, j, k: (k, i)) if ta else pl.BlockSpec((tm, tk), lambda i, j, k: (i, k))
    b_spec = pl.BlockSpec((tn, tk), lambda i, j, k: (j, k)) if tb else pl.BlockSpec((tk, tn), lambda i, j, k: (k, j))
    return pl.pallas_call(
        body, name=name, grid=(m // tm, n // tn, nk),
        in_specs=[a_spec, b_spec],
        out_specs=out_spec, out_shape=out_shape,
        scratch_shapes=[pltpu.VMEM((tm, tn), F32)],
        compiler_params=_params(3),
    )(a, b)


def _mm_swiglu(xb, wg, wu, *, comm=None, name):
    t, d = xb.shape
    share = wg.shape[2]
    n = N_CHIPS * share
    tm = _pick(t, (512, 256, 128))
    tn = _pick(share, (768, 256, 128))
    per = share // tn

    def body(x_ref, wg_ref, wu_ref, g_ref, u_ref, a_ref):
        x = x_ref[pl.ds(pl.multiple_of(pl.program_id(1) * tm, tm), tm), :]
        g = _dot(x, wg_ref[...], 1, 0)
        u = _dot(x, wu_ref[...], 1, 0)
        g_ref[...] = g.astype(BF16)
        u_ref[...] = u.astype(BF16)
        a_ref[...] = (g * _sigmoid(g) * u).astype(BF16)

    o = jax.ShapeDtypeStruct((t, n), BF16)
    ospec = pl.BlockSpec((tm, tn), lambda j, i: (i, j))
    return _hosted_call(
        body, comm, (n // tn, t // tm), name=name,
        in_specs=[pl.BlockSpec((t, d), lambda j, i: (0, 0), pipeline_mode=pl.Buffered(1)),
                  pl.BlockSpec((None, d, tn), lambda j, i: (j // per, 0, j % per)),
                  pl.BlockSpec((None, d, tn), lambda j, i: (j // per, 0, j % per))],
        out_specs=[ospec, ospec, ospec], out_shape=[o, o, o], scratch_shapes=[], args=[xb, wg, wu])


def _mm_swiglu_bwd(dr, wd, g, u, *, scale, name):
    t, d = dr.shape
    n = wd.shape[0]
    tm = _pick(t, (512, 256, 128))
    tn = _pick(n, (768, 256, 128))

    def body(dr_ref, wd_ref, g_ref, u_ref, dg_ref, du_ref):
        dr_rows = dr_ref[pl.ds(pl.multiple_of(pl.program_id(1) * tm, tm), tm), :]
        da = _dot(dr_rows.astype(BF16), wd_ref[...], 1, 1) * scale
        gg = g_ref[...].astype(F32)
        uu = u_ref[...].astype(F32)
        sg = _sigmoid(gg)
        dg_ref[...] = (da * uu * (sg * (1.0 + gg * (1.0 - sg)))).astype(BF16)
        du_ref[...] = (da * gg * sg).astype(BF16)

    o = jax.ShapeDtypeStruct((t, n), BF16)
    ospec = pl.BlockSpec((tm, tn), lambda j, i: (i, j))
    return pl.pallas_call(
        body, name=name, grid=(n // tn, t // tm),
        in_specs=[pl.BlockSpec((t, d), lambda j, i: (0, 0), pipeline_mode=pl.Buffered(1)),
                  pl.BlockSpec((tn, d), lambda j, i: (j, 0)),
                  ospec, ospec],
        out_specs=[ospec, ospec], out_shape=[o, o],
        compiler_params=_params(2),
    )(dr, wd, g, u)


def _mm_ln(a, w, resid, gain, bias, *, rscale, mscale, name):
    t, kk = a.shape
    d = w.shape[1]
    tm = _pick(t, (512, 256, 128))
    tk = kk
    nk = kk // tk

    def body(a_ref, w_ref, r_ref, g_ref, b_ref, y_ref, yb_ref, xh_ref, rs_ref, acc):
        k = pl.program_id(1)

        @pl.when(k == 0)
        def _():
            acc[...] = jnp.zeros_like(acc)

        acc[...] += _dot(a_ref[...].astype(BF16), w_ref[...], 1, 0)

        @pl.when(k == nk - 1)
        def _():
            r = rscale * r_ref[...] + mscale * acc[...]
            mu = jnp.mean(r, axis=1, keepdims=True)
            xc = r - mu
            var = jnp.mean(xc * xc, axis=1, keepdims=True)
            rstd = lax.rsqrt(var + LN_EPS)
            xh = xc * rstd
            y = xh * g_ref[...] + b_ref[...]
            y_ref[...] = y
            yb_ref[...] = y.astype(BF16)
            xh_ref[...] = xh
            rs_ref[...] = rstd

    row = pl.BlockSpec((tm, d), lambda i, k: (i, 0))
    vec = pl.BlockSpec((1, d), lambda i, k: (0, 0))
    return pl.pallas_call(
        body, name=name, grid=(t // tm, nk),
        in_specs=[pl.BlockSpec((tm, tk), lambda i, k: (i, k)),
                  pl.BlockSpec((tk, d), lambda i, k: (k, 0)), row, vec, vec],
        out_specs=[row, row, row, pl.BlockSpec((tm, 1), lambda i, k: (i, 0))],
        out_shape=[jax.ShapeDtypeStruct((t, d), F32), jax.ShapeDtypeStruct((t, d), BF16),
                   jax.ShapeDtypeStruct((t, d), F32), jax.ShapeDtypeStruct((t, 1), F32)],
        scratch_shapes=[pltpu.VMEM((tm, d), F32)],
        compiler_params=_params(2),
    )(a, w, resid, gain.reshape(1, d), bias.reshape(1, d))


def _bwd_proj(pairs, resid, *, rscale, ln, comm=None, name):
    t, kk = pairs[0][0].shape
    d = pairs[0][1].shape[-2]
    has_ln = ln is not None
    npair = len(pairs)
    tm = _pick(t, (256, 128) if has_ln and npair > 1 else (512, 256, 128))
    nt = t // tm

    def body(*refs):
        ab = refs[:2 * npair]
        r_ref = refs[2 * npair]
        pos = 2 * npair + 1
        if has_ln:
            xh_ref, rs_ref, g_ref = refs[pos:pos + 3]
            pos += 3
            o_ref, ob_ref, dg_ref, db_ref = refs[pos:pos + 4]
        else:
            o_ref = refs[pos]
        i = pl.program_id(0)
        dy = rscale * r_ref[...]
        for q in range(npair):
            a_ref, b_ref = ab[2 * q], ab[2 * q + 1]
            if len(b_ref.shape) == 3:
                share = b_ref.shape[2]
                for c in range(N_CHIPS):
                    dy = dy + _dot(a_ref[:, share * c:share * (c + 1)].astype(BF16), b_ref[c], 1, 1)
            else:
                dy = dy + _dot(a_ref[...].astype(BF16), b_ref[...], 1, 1)
        if not has_ln:
            o_ref[...] = dy
            return
        xh = xh_ref[...]
        w = dy * g_ref[...]
        m1 = jnp.mean(w, axis=1, keepdims=True)
        m2 = jnp.mean(w * xh, axis=1, keepdims=True)
        dr = rs_ref[...] * (w - m1 - xh * m2)
        o_ref[...] = dr
        ob_ref[...] = dr.astype(BF16)

        @pl.when(i == 0)
        def _():
            dg_ref[...] = jnp.zeros_like(dg_ref)
            db_ref[...] = jnp.zeros_like(db_ref)

        dg_ref[...] += jnp.sum(dy * xh, axis=0, keepdims=True)
        db_ref[...] += jnp.sum(dy, axis=0, keepdims=True)

    row = pl.BlockSpec((tm, d), lambda i, k: (i, 0))
    vec = pl.BlockSpec((1, d), lambda i, k: (0, 0))
    in_specs, args = [], []
    for a, b in pairs:
        b_spec = pl.BlockSpec(b.shape, lambda i, k, nd=b.ndim: (0,) * nd, pipeline_mode=pl.Buffered(1))
        in_specs += [pl.BlockSpec((tm, kk), lambda i, k: (i, 0)), b_spec]
        args += [a, b]
    in_specs.append(row)
    args.append(resid)
    out_specs = [row]
    out_shape = [jax.ShapeDtypeStruct((t, d), F32)]
    if has_ln:
        xh, rs, gain = ln
        in_specs += [row, pl.BlockSpec((tm, 1), lambda i, k: (i, 0)), vec]
        args += [xh, rs, gain.reshape(1, d)]
        out_specs += [row, vec, vec]
        out_shape += [jax.ShapeDtypeStruct((t, d), BF16)] + [jax.ShapeDtypeStruct((1, d), F32)] * 2
    outs, got = _hosted_call(body, comm, (nt, 1), name=name, in_specs=in_specs, out_specs=out_specs,
                             out_shape=out_shape, scratch_shapes=[], args=args)
    return tuple(outs) if comm is None else tuple(outs) + (got,)


def _mm_pe(x3, x3b, pb, wgate, bgate, wproj, *, name):
    t, d = x3.shape
    pd = pb.shape[1]
    tm = _pick(t, (512, 256, 128))
    tn = _pick(d, (512, 256, 128))

    def body(x_ref, xb_ref, p_ref, wg_ref, bg_ref, wp_ref, y_ref, yb_ref, sg_ref, e_ref):
        sg = _sigmoid(_dot(xb_ref[...], wg_ref[...], 1, 0) + bg_ref[...])
        e = _dot(p_ref[...], wp_ref[...], 1, 0)
        y = x_ref[...] + sg * e
        y_ref[...] = y
        yb_ref[...] = y.astype(BF16)
        sg_ref[...] = sg.astype(BF16)
        e_ref[...] = e.astype(BF16)

    ospec = pl.BlockSpec((tm, tn), lambda i, j: (i, j))
    ob = jax.ShapeDtypeStruct((t, d), BF16)
    return pl.pallas_call(
        body, name=name, grid=(t // tm, d // tn),
        in_specs=[ospec, pl.BlockSpec((tm, d), lambda i, j: (i, 0)), pl.BlockSpec((tm, pd), lambda i, j: (i, 0)),
                  pl.BlockSpec((d, tn), lambda i, j: (0, j)), pl.BlockSpec((1, tn), lambda i, j: (0, j)),
                  pl.BlockSpec((pd, tn), lambda i, j: (0, j))],
        out_specs=[ospec, ospec, ospec, ospec],
        out_shape=[jax.ShapeDtypeStruct((t, d), F32), ob, ob, ob],
        compiler_params=_params(2),
    )(x3, x3b, pb, wgate, bgate.reshape(1, d), wproj)


def _pe_bwd_elem(dx4, sg, e, *, name):
    t, d = dx4.shape
    tm = _pick(t, (512, 256, 128))

    def body(dx_ref, sg_ref, e_ref, dgp_ref, de_ref, db_ref):
        dx = dx_ref[...]
        s = sg_ref[...].astype(F32)
        dgp = dx * e_ref[...].astype(F32) * s * (1.0 - s)
        dgp_ref[...] = dgp.astype(BF16)
        de_ref[...] = (dx * s).astype(BF16)

        @pl.when(pl.program_id(0) == 0)
        def _():
            db_ref[...] = jnp.zeros_like(db_ref)

        db_ref[...] += jnp.sum(dgp, axis=0, keepdims=True)

    row = pl.BlockSpec((tm, d), lambda i: (i, 0))
    ob = jax.ShapeDtypeStruct((t, d), BF16)
    return pl.pallas_call(
        body, name=name, grid=(t // tm,), in_specs=[row, row, row],
        out_specs=[row, row, pl.BlockSpec((1, d), lambda i: (0, 0))],
        out_shape=[ob, ob, jax.ShapeDtypeStruct((1, d), F32)],
        compiler_params=_params(1),
    )(dx4, sg, e)


def _assemble(pieces, width, *, name):
    t = pieces[0].shape[0]
    tm = _pick(t, (512, 256, 128))
    widths = [p.shape[1] for p in pieces]

    def body(*refs):
        o_ref = refs[-1]
        off = 0
        for p_ref, w in zip(refs[:-1], widths):
            o_ref[:, off:off + w] = p_ref[...].astype(BF16)
            off += w
        if off < width:
            o_ref[:, off:] = jnp.zeros((tm, width - off), BF16)

    return pl.pallas_call(
        body, name=name, grid=(t // tm,),
        in_specs=[pl.BlockSpec((tm, w), lambda i: (i, 0)) for w in widths],
        out_specs=pl.BlockSpec((tm, width), lambda i: (i, 0)),
        out_shape=jax.ShapeDtypeStruct((t, width), BF16),
        compiler_params=_params(1),
    )(*pieces)


def _loss_kernel(y, target, *, name):
    t, d = y.shape
    tm = _pick(t, (512, 256, 128))

    def body(y_ref, t_ref, dy_ref, l_ref):
        diff = y_ref[...] - t_ref[...]
        dy_ref[...] = diff * (1.0 / d)

        @pl.when(pl.program_id(0) == 0)
        def _():
            l_ref[...] = jnp.zeros_like(l_ref)

        part = jnp.sum(jnp.mean(diff * diff, axis=1, keepdims=True), axis=0, keepdims=True)
        l_ref[...] += 0.5 * part

    row = pl.BlockSpec((tm, d), lambda i: (i, 0))
    return pl.pallas_call(
        body, name=name, grid=(t // tm,), in_specs=[row, row],
        out_specs=[row, pl.BlockSpec((1, 1), lambda i: (0, 0))],
        out_shape=[jax.ShapeDtypeStruct((t, d), F32), jax.ShapeDtypeStruct((1, 1), F32)],
        compiler_params=_params(1),
    )(y, target)


LRU_TM = 256


def _lru_gate_terms(r, lam):
    sp = _softplus(-lam)
    la = -LRU_C * r * sp
    a = jnp.exp(la)
    em = jnp.tanh(la) * (jnp.exp(2.0 * la) + 1.0)
    s = jnp.sqrt(-em)
    return la, a, s, sp


def _lru_fwd(hbuf, conv_w, conv_b, wa, ba, wx, bx, lam, *, name):
    t = hbuf.shape[0]
    w = LRU_WIDTH
    tm = _pick(t, (LRU_TM, 128))
    cu, cg = COL_U // w, COL_G // w
    hb = tm // SUBLANES

    def body(u_ref, up_ref, g_ref, cw_ref, cb_ref, wa_ref, ba_ref, wx_ref, bx_ref, lam_ref,
             y_ref, u_out, r_out, i_out, a_out, h_out, carry):
        i = pl.program_id(0)

        @pl.when(i == 0)
        def _():
            carry[...] = jnp.zeros_like(carry)

        prev = jnp.where(i == 0, 0.0, up_ref[...])
        u = _conv_taps(u_ref[...], prev, cw_ref[...], cb_ref[...])
        ub = u.astype(BF16)
        r = _sigmoid(_dot(ub, wa_ref[...], 1, 0) + ba_ref[...])
        ig = _sigmoid(_dot(ub, wx_ref[...], 1, 0) + bx_ref[...])
        _, a, s, _ = _lru_gate_terms(r, lam_ref[...])
        b = s * (ig * u)
        acum, hs = _scan_fwd(a, b)
        h = hs + acum * carry[0:1, :]
        carry[...] = jnp.broadcast_to(h[tm - 1:tm, :], carry.shape)
        gl, _ = _gelu_and_grad(g_ref[...])
        y_ref[...] = h * gl
        u_out[...] = u
        r_out[...] = r
        i_out[...] = ig
        a_out[...] = a
        h_out[...] = h

    row = pl.BlockSpec((tm, w), lambda i: (i, 0))
    vec = pl.BlockSpec((1, w), lambda i: (0, 0))
    mat = pl.BlockSpec((w, w), lambda i: (0, 0))
    o = jax.ShapeDtypeStruct((t, w), F32)
    return pl.pallas_call(
        body, name=name, grid=(t // tm,),
        in_specs=[pl.BlockSpec((tm, w), lambda i: (i, cu)),
                  pl.BlockSpec((SUBLANES, w), lambda i: (jnp.maximum(i * hb - 1, 0), cu)),
                  pl.BlockSpec((tm, w), lambda i: (i, cg)),
                  pl.BlockSpec((CONV_K, w), lambda i: (0, 0)), vec, mat, vec, mat, vec, vec],
        out_specs=[row] * 6, out_shape=[o] * 6,
        scratch_shapes=[pltpu.VMEM((SUBLANES, w), F32)],
        compiler_params=_params(1),
    )(hbuf, hbuf, hbuf, conv_w, conv_b, wa, ba, wx, bx, lam)


def _lru_bwd(dymix, hbuf, u, r, ig, a, h, conv_w, wa, wx, lam, *, name):
    t = hbuf.shape[0]
    w = LRU_WIDTH
    tm = _pick(t, (LRU_TM, 128))
    nb = t // tm
    cu, cg = COL_U // w, COL_G // w
    hb = tm // SUBLANES
    last8 = t // SUBLANES - 1

    def body(dy_ref, ur_ref, g_ref, u_ref, r_ref, i_ref, a_ref, an_ref, h_ref, hp_ref,
             cw_ref, wa_ref, wx_ref, lam_ref,
             dur_ref, dgr_ref, dcw_ref, dcb_ref, dwa_ref, dba_ref, dwx_ref, dbx_ref, dlam_ref,
             lcarry, dnext):
        i = pl.program_id(0)
        ib = nb - 1 - i

        @pl.when(i == 0)
        def _():
            lcarry[...] = jnp.zeros_like(lcarry)
            dnext[...] = jnp.zeros_like(dnext)
            for ref in (dcw_ref, dcb_ref, dwa_ref, dba_ref, dwx_ref, dbx_ref, dlam_ref):
                ref[...] = jnp.zeros_like(ref)

        dy = dy_ref[...]
        hh = h_ref[...]
        av = a_ref[...]
        uu = u_ref[...]
        rr = r_ref[...]
        ii = i_ref[...]
        lam_v = lam_ref[...]
        gl, dgl = _gelu_and_grad(g_ref[...])
        dgr_ref[...] = (dy * hh * dgl).astype(BF16)
        dh_out = dy * gl
        a_next = _shift_up(av, 1, jnp.where(ib == nb - 1, 0.0, an_ref[...]))
        acum, ls = _scan_bwd(a_next, dh_out)
        lam_adj = ls + acum * lcarry[0:1, :]
        lcarry[...] = jnp.broadcast_to(lam_adj[0:1, :], lcarry.shape)
        h_prev = _shift_down(hh, 1, jnp.where(ib == 0, 0.0, hp_ref[...]))
        da = lam_adj * h_prev
        _, a2, s, sp = _lru_gate_terms(rr, lam_v)
        d_igu = lam_adj * s
        ds = lam_adj * ii * uu
        dla = da * a2 - ds * (a2 * a2) / s
        dr = dla * (-LRU_C * sp)
        dlam_ref[...] += jnp.sum(dla * (LRU_C * rr * _sigmoid(-lam_v)), axis=0, keepdims=True)
        dpre_r = dr * rr * (1.0 - rr)
        dpre_i = d_igu * uu * ii * (1.0 - ii)
        prb = dpre_r.astype(BF16)
        pib = dpre_i.astype(BF16)
        ub = uu.astype(BF16)
        du = d_igu * ii + _dot(prb, wa_ref[...], 1, 1) + _dot(pib, wx_ref[...], 1, 1)
        dwa_ref[...] += _dot(ub, prb, 0, 0)
        dwx_ref[...] += _dot(ub, pib, 0, 0)
        dba_ref[...] += jnp.sum(dpre_r, axis=0, keepdims=True)
        dbx_ref[...] += jnp.sum(dpre_i, axis=0, keepdims=True)
        dur, dws = _conv_taps_bwd(du, dnext[...], cw_ref[...], ur_ref[...])
        dur_ref[...] = dur.astype(BF16)
        dcw_ref[...] += dws
        dcb_ref[...] += jnp.sum(du, axis=0, keepdims=True)
        dnext[...] = du[:SUBLANES]

    def rowspec(col):
        return pl.BlockSpec((tm, w), lambda i: (nb - 1 - i, col))

    row = rowspec(0)
    nxt = pl.BlockSpec((SUBLANES, w), lambda i: (jnp.minimum((nb - i) * hb, last8), 0))
    prv = pl.BlockSpec((SUBLANES, w), lambda i: (jnp.maximum((nb - 1 - i) * hb - 1, 0), 0))
    vec = pl.BlockSpec((1, w), lambda i: (0, 0))
    mat = pl.BlockSpec((w, w), lambda i: (0, 0))
    cw = pl.BlockSpec((CONV_K, w), lambda i: (0, 0))
    o = jax.ShapeDtypeStruct((t, w), BF16)
    v1 = jax.ShapeDtypeStruct((1, w), F32)
    m1 = jax.ShapeDtypeStruct((w, w), F32)
    return pl.pallas_call(
        body, name=name, grid=(nb,),
        in_specs=[rowspec(0), rowspec(cu), rowspec(cg), row, row, row, row, nxt, row, prv, cw, mat, mat, vec],
        out_specs=[row, row, cw, vec, mat, vec, mat, vec, vec],
        out_shape=[o, o, jax.ShapeDtypeStruct((CONV_K, w), F32), v1, m1, v1, m1, v1, v1],
        scratch_shapes=[pltpu.VMEM((SUBLANES, w), F32), pltpu.VMEM((SUBLANES, w), F32)],
        compiler_params=_params(1),
    )(dymix, hbuf, hbuf, u, r, ig, a, a, h, h, conv_w, wa, wx, lam)


FOX_T = 1024
FOX_PREP_TM = 256


def _log_sigmoid(x):
    return jnp.minimum(x, 0.0) - jnp.log(1.0 + jnp.exp(-jnp.abs(x)))


def _fox_prep(hbuf, bf_vec, *, name):
    t = hbuf.shape[0]
    tm = _pick(t, (FOX_PREP_TM, 128))
    cs = COL_SMALL // LANES

    def body(s_ref, b_ref, eq_ref, ek_ref, carry):
        i = pl.program_id(0)

        @pl.when(i == 0)
        def _():
            carry[...] = jnp.zeros_like(carry)

        lf = _log_sigmoid(s_ref[...] + b_ref[...])
        f = _cumsum_rows(lf) + carry[0:1, :]
        carry[...] = jnp.broadcast_to(f[tm - 1:tm, :], carry.shape)
        lane = _iota((tm, LANES), 1)
        for h in range(ATT_HEADS):
            base = HEAD_DIM * (1 - h % 2)
            fh = _col(f, h)
            hi = fh.astype(BF16).astype(F32)
            mid = (fh - hi).astype(BF16).astype(F32)
            lo = fh - hi - mid
            terms = jnp.where(lane == base, hi, jnp.where(lane == base + 1, mid, jnp.where(lane == base + 2, lo, 0.0)))
            terms_k = jnp.where(lane == base + 3, -hi,
                                jnp.where(lane == base + 4, -mid, jnp.where(lane == base + 5, -lo, 0.0)))
            ones_q = ((lane >= base + 3) & (lane < base + 6)).astype(F32)
            ones_k = ((lane >= base) & (lane < base + 3)).astype(F32)
            eq_ref[:, LANES * h:LANES * (h + 1)] = (terms + ones_q).astype(BF16)
            ek_ref[:, LANES * h:LANES * (h + 1)] = (terms_k + ones_k).astype(BF16)

    ospec = pl.BlockSpec((tm, ATT_HEADS * LANES), lambda i: (i, 0))
    o = jax.ShapeDtypeStruct((t, ATT_HEADS * LANES), BF16)
    return pl.pallas_call(
        body, name=name, grid=(t // tm,),
        in_specs=[pl.BlockSpec((tm, LANES), lambda i: (i, cs)), pl.BlockSpec((1, LANES), lambda i: (0, 0))],
        out_specs=[ospec, ospec], out_shape=[o, o],
        scratch_shapes=[pltpu.VMEM((SUBLANES, LANES), F32)],
        compiler_params=_params(1),
    )(hbuf, bf_vec)


def _fox_post(dfc, hbuf, bf_vec, *, name):
    t = hbuf.shape[0]
    tm = _pick(t, (FOX_PREP_TM, 128))
    nb = t // tm
    cs = COL_SMALL // LANES

    def body(df_ref, s_ref, b_ref, o_ref, db_ref, carry):
        i = pl.program_id(0)

        @pl.when(i == 0)
        def _():
            carry[...] = jnp.zeros_like(carry)
            db_ref[...] = jnp.zeros_like(db_ref)

        dlf = _cumsum_rows(df_ref[...], reverse=True) + carry[0:1, :]
        carry[...] = jnp.broadcast_to(dlf[0:1, :], carry.shape)
        dl = dlf * _sigmoid(-(s_ref[...] + b_ref[...]))
        dl = jnp.where(_iota(dl.shape, 1) < ATT_HEADS, dl, 0.0)
        o_ref[...] = dl
        db_ref[...] += jnp.sum(dl, axis=0, keepdims=True)

    vec = pl.BlockSpec((1, LANES), lambda i: (0, 0))
    return pl.pallas_call(
        body, name=name, grid=(nb,),
        in_specs=[pl.BlockSpec((tm, LANES), lambda i: (nb - 1 - i, 0)),
                  pl.BlockSpec((tm, LANES), lambda i: (nb - 1 - i, cs)), vec],
        out_specs=[pl.BlockSpec((tm, LANES), lambda i: (nb - 1 - i, 0)), vec],
        out_shape=[jax.ShapeDtypeStruct((t, LANES), F32), jax.ShapeDtypeStruct((1, LANES), F32)],
        scratch_shapes=[pltpu.VMEM((SUBLANES, LANES), F32)],
        compiler_params=_params(1),
    )(dfc, hbuf, bf_vec)


def _fox_masks(i, j, tq):
    row = i * tq + _iota((tq, tq), 0)
    col = j * tq + _iota((tq, tq), 1)
    lane = _iota((1, LANES), 1)
    return col <= row, (lane < HEAD_DIM, lane >= HEAD_DIM)


def _hosting(body, n_in, n_out, n_scratch, comm, grid):
    na, no = len(comm.arrays), len(comm.out_shapes)

    def hosted(*refs):
        o0 = n_in + na
        s0 = o0 + n_out + no
        cargs = (refs[n_in:o0], refs[o0 + n_out:s0]) + tuple(refs[s0 + n_scratch:])
        a, b = pl.program_id(0), pl.program_id(1)

        @pl.when((a == 0) & (b == 0))
        def _():
            comm.start(*cargs)

        @pl.when((a == grid[0] - 1) & (b == 0))
        def _():
            comm.middle(*cargs)

        body(*refs[:n_in], *refs[o0:o0 + n_out], *refs[s0:s0 + n_scratch])

        @pl.when((a == grid[0] - 1) & (b == grid[1] - 1))
        def _():
            comm.finish(*cargs)

    return hosted


def _hosted_call(body, comm, grid, *, name, in_specs, out_specs, out_shape, scratch_shapes, args):
    n_out = len(out_shape)
    if comm is not None:
        cin, cout, sems = comm.specs()
        body = _hosting(body, len(in_specs), n_out, len(scratch_shapes), comm, grid)
        in_specs, out_specs = in_specs + cin, out_specs + cout
        out_shape, scratch_shapes, args = out_shape + comm.out_shapes, scratch_shapes + sems, args + list(comm.arrays)
    outs = pl.pallas_call(body, name=name, grid=grid, in_specs=in_specs, out_specs=out_specs,
                          out_shape=out_shape, scratch_shapes=scratch_shapes, compiler_params=_params(2))(*args)
    return outs[:n_out], outs[n_out:]


def _merge_comms(comms):
    comms = [c for c in comms if c is not None]
    if len(comms) <= 1:
        return comms[0] if comms else None

    def both(which):
        def run(ins, outs, ssem, rsem):
            ia = io = 0
            for c in comms:
                na, no = len(c.arrays), len(c.out_shapes)
                getattr(c, which)(ins[ia:ia + na], outs[io:io + no], ssem, rsem)
                ia, io = ia + na, io + no
        return run

    spans = sorted((c.base, c.base + c.n_own) for c in comms)
    assert all(a[1] <= b[0] for a, b in zip(spans, spans[1:])), "semaphore ranges overlap"
    return _Comm(sum((list(c.arrays) for c in comms), []), sum((list(c.out_shapes) for c in comms), []),
                 spans[-1][1], both("start"), both("finish"), middle=both("middle"))


def _fox_fwd(hbuf, eq, ek, *, comm=None, name):
    t = hbuf.shape[0]
    w = ATT_WIDTH
    tq = _pick(t, (FOX_T, 256, 128))
    nq = t // tq
    cq, ck, cv = COL_Q // w, COL_K // w, COL_V // w

    def body(q_ref, k_ref, v_ref, eq_ref, ek_ref, o_ref, lse_ref, m_s, l_s, acc_s):
        i = pl.program_id(0)
        j = pl.program_id(1)

        @pl.when(j == 0)
        def _():
            m_s[...] = jnp.full_like(m_s, NEG)
            l_s[...] = jnp.zeros_like(l_s)
            acc_s[...] = jnp.zeros_like(acc_s)

        def step(diagonal):
            _, hms = _fox_masks(i, j, tq)
            keys_first = (j * tq + _iota((tq, tq), 0)) <= (i * tq + _iota((tq, tq), 1))
            half = _iota((LANES, 1), 0)
            hrows = (half < HEAD_DIM, half >= HEAD_DIM)
            m_all = m_s[...]
            l_all = l_s[...]
            acc_old = [acc_s[LANES * pr:LANES * (pr + 1), :] for pr in range(2)]
            m_out, l_out, acc_out = [], [], []
            for pr in range(2):
                sl = slice(LANES * pr, LANES * (pr + 1))
                qp = q_ref[:, sl]
                kp = k_ref[:, sl]
                vt = v_ref[:, sl].T.astype(BF16)
                acc = acc_old[pr]
                for hh in range(2):
                    h = 2 * pr + hh
                    hsl = slice(LANES * h, LANES * (h + 1))
                    qm = jnp.where(hms[hh], (qp * (HEAD_DIM ** -0.5)).astype(BF16), eq_ref[:, hsl])
                    km = jnp.where(hms[hh], kp.astype(BF16), ek_ref[:, hsl])
                    st = _dot(km, qm, 1, 1)
                    if diagonal:
                        st = jnp.where(keys_first, st, NEG)
                    m_old = m_all[h:h + 1, :]
                    m_new = jnp.maximum(m_old, jnp.max(st, axis=0, keepdims=True))
                    alpha = jnp.exp(m_old - m_new)
                    pt = jnp.exp(st - m_new)
                    l_out.append(alpha * l_all[h:h + 1, :] + jnp.sum(pt, axis=0, keepdims=True))
                    m_out.append(m_new)
                    pv = _dot(vt, pt.astype(BF16), 1, 0)
                    acc = jnp.where(hrows[hh], alpha * acc_old[pr] + pv, acc)
                acc_out.append(acc)
            for h in range(ATT_HEADS):
                m_s[h:h + 1, :] = m_out[h]
                l_s[h:h + 1, :] = l_out[h]
            for pr in range(2):
                acc_s[LANES * pr:LANES * (pr + 1), :] = acc_out[pr]

        @pl.when(j < i)
        def _():
            step(False)

        @pl.when(j == i)
        def _():
            step(True)
            half = _iota((LANES, 1), 0)
            l_all = l_s[...]
            for pr in range(2):
                acc = acc_s[LANES * pr:LANES * (pr + 1), :]
                o_t = jnp.where(half < HEAD_DIM, acc / l_all[2 * pr:2 * pr + 1, :], acc / l_all[2 * pr + 1:2 * pr + 2, :])
                o_ref[:, LANES * pr:LANES * (pr + 1)] = o_t.T
            lse = m_s[...] + jnp.log(l_s[...])
            lse_ref[...] = jnp.where(_iota(lse.shape, 0) < ATT_HEADS, lse, 0.0)

    return _hosted_call(
        body, comm, (nq, nq), name=name,
        in_specs=[pl.BlockSpec((tq, w), lambda i, j: (i, cq)),
                  pl.BlockSpec((tq, w), lambda i, j: (jnp.minimum(j, i), ck)),
                  pl.BlockSpec((tq, w), lambda i, j: (jnp.minimum(j, i), cv)),
                  pl.BlockSpec((tq, ATT_HEADS * LANES), lambda i, j: (i, 0)),
                  pl.BlockSpec((tq, ATT_HEADS * LANES), lambda i, j: (jnp.minimum(j, i), 0))],
        out_specs=[pl.BlockSpec((tq, w), lambda i, j: (i, 0)),
                   pl.BlockSpec((SUBLANES, tq), lambda i, j: (0, i))],
        out_shape=[jax.ShapeDtypeStruct((t, w), F32), jax.ShapeDtypeStruct((SUBLANES, t), F32)],
        scratch_shapes=[pltpu.VMEM((SUBLANES, tq), F32), pltpu.VMEM((SUBLANES, tq), F32),
                        pltpu.VMEM((w, tq), F32)],
        args=[hbuf, hbuf, hbuf, eq, ek])


def _fox_delta(dymix, o, *, name):
    t, w = o.shape
    tm = _pick(t, (512, 256, 128))
    cdo = ATT_WIDTH // w

    def body(do_ref, o_ref, d_ref):
        d_ref[...] = _head_reduce(do_ref[...] * o_ref[...], 0, ATT_HEADS)

    return pl.pallas_call(
        body, name=name, grid=(t // tm,),
        in_specs=[pl.BlockSpec((tm, w), lambda i: (i, cdo)), pl.BlockSpec((tm, w), lambda i: (i, 0))],
        out_specs=pl.BlockSpec((tm, LANES), lambda i: (i, 0)),
        out_shape=jax.ShapeDtypeStruct((t, LANES), F32),
        compiler_params=_params(1),
    )(dymix, o)


def _fox_bwd(hbuf, eq, ek, dymix, lse_rows, delta_rows, *, comm=None, name):
    t = hbuf.shape[0]
    w = ATT_WIDTH
    tq = _pick(t, (FOX_T, 256, 128))
    nq = t // tq
    cq, ck, cv = COL_Q // w, COL_K // w, COL_V // w
    cdo = ATT_WIDTH // w

    def body(q_ref, k_ref, v_ref, eq_ref, ek_ref, do_ref, lse_ref, dl_ref, dk_ref, dv_ref, dfk_ref, dqt_ref, dfq_ref,
             dk_s, dv_s, dfk_s):
        j = pl.program_id(0)
        i = pl.program_id(1)

        @pl.when((i == 0) & (j == 0))
        def _():
            dqt_ref[...] = jnp.zeros_like(dqt_ref)
            dfq_ref[...] = jnp.zeros_like(dfq_ref)

        @pl.when(i == 0)
        def _():
            dk_s[...] = jnp.zeros_like(dk_s)
            dv_s[...] = jnp.zeros_like(dv_s)
            dfk_s[...] = jnp.zeros_like(dfk_s)

        def step(diagonal):
            _, hms = _fox_masks(i, j, tq)
            keys_first = (j * tq + _iota((tq, tq), 0)) <= (i * tq + _iota((tq, tq), 1))
            half = _iota((LANES, 1), 0)
            hrows = (half < HEAD_DIM, half >= HEAD_DIM)
            lse_all = lse_ref[...]
            dl_all = dl_ref[...]
            dvs, dks, dfks, dqts, dfqs = [], [], [], [], []
            for pr in range(2):
                sl = slice(LANES * pr, LANES * (pr + 1))
                qp = q_ref[:, sl]
                kp = k_ref[:, sl]
                kt = kp.T.astype(BF16)
                vpb = v_ref[:, sl].astype(BF16)
                dop = do_ref[:, sl]
                dv_p = jnp.zeros((tq, LANES), F32)
                dk_p = jnp.zeros((tq, LANES), F32)
                dqt_p = jnp.zeros((LANES, tq), F32)
                for hh in range(2):
                    h = 2 * pr + hh
                    hsl = slice(LANES * h, LANES * (h + 1))
                    qm = jnp.where(hms[hh], (qp * (HEAD_DIM ** -0.5)).astype(BF16), eq_ref[:, hsl])
                    km = jnp.where(hms[hh], kp.astype(BF16), ek_ref[:, hsl])
                    st = _dot(km, qm, 1, 1)
                    if diagonal:
                        st = jnp.where(keys_first, st, NEG)
                    pt = jnp.exp(st - lse_all[h:h + 1, :])
                    domb = jnp.where(hms[hh], dop, 0.0).astype(BF16)
                    dv_p = dv_p + _dot(pt.astype(BF16), domb, 1, 0)
                    dpt = _dot(vpb, domb, 1, 1)
                    dst = pt * (dpt - dl_all[h:h + 1, :])
                    dstb = dst.astype(BF16)
                    dk_p = dk_p + jnp.where(hms[hh], _dot(dstb, qm, 1, 0), 0.0)
                    dqt_p = dqt_p + _dot(jnp.where(hrows[hh], kt, 0.0), dstb, 1, 0)
                    part = dst[:, 0:LANES]
                    for c in range(1, tq // LANES):
                        part = part + dst[:, LANES * c:LANES * (c + 1)]
                    dfks.append(part)
                    dfqs.append(jnp.sum(dst, axis=0, keepdims=True))
                dvs.append(dv_p)
                dks.append(dk_p)
                dqts.append(dqt_p)
            dv_s[...] += jnp.concatenate(dvs, axis=1)
            dk_s[...] += jnp.concatenate(dks, axis=1)
            for h in range(ATT_HEADS):
                dfk_s[h] += dfks[h]
            cols = pl.ds(pl.multiple_of(i * tq, tq), tq)
            dqt_ref[:, cols] += jnp.concatenate(dqts, axis=0) * (HEAD_DIM ** -0.5)
            dfq_ref[:, cols] += jnp.concatenate(dfqs + [jnp.zeros((SUBLANES - ATT_HEADS, tq), F32)], axis=0)

        @pl.when(i > j)
        def _():
            step(False)

        @pl.when(i == j)
        def _():
            step(True)

        @pl.when(i == nq - 1)
        def _():
            dk_ref[...] = dk_s[...].astype(BF16)
            dv_ref[...] = dv_s[...].astype(BF16)
            lane = _iota((tq, LANES), 1)
            out = jnp.zeros((tq, LANES), F32)
            for h in range(ATT_HEADS):
                out = jnp.where(lane == h, jnp.sum(dfk_s[h], axis=1, keepdims=True), out)
            dfk_ref[...] = out

    qi = lambda j, i: jnp.maximum(i, j)
    rows = pl.BlockSpec((SUBLANES, tq), lambda j, i: (0, qi(j, i)))
    return _hosted_call(
        body, comm, (nq, nq), name=name,
        in_specs=[pl.BlockSpec((tq, w), lambda j, i: (qi(j, i), cq)),
                  pl.BlockSpec((tq, w), lambda j, i: (j, ck)),
                  pl.BlockSpec((tq, w), lambda j, i: (j, cv)),
                  pl.BlockSpec((tq, ATT_HEADS * LANES), lambda j, i: (qi(j, i), 0)),
                  pl.BlockSpec((tq, ATT_HEADS * LANES), lambda j, i: (j, 0)),
                  pl.BlockSpec((tq, w), lambda j, i: (qi(j, i), cdo)),
                  rows, rows],
        out_specs=[pl.BlockSpec((tq, w), lambda j, i: (j, 0)), pl.BlockSpec((tq, w), lambda j, i: (j, 0)),
                   pl.BlockSpec((tq, LANES), lambda j, i: (j, 0)),
                   pl.BlockSpec((w, t), lambda j, i: (0, 0)), pl.BlockSpec((SUBLANES, t), lambda j, i: (0, 0))],
        out_shape=[jax.ShapeDtypeStruct((t, w), BF16), jax.ShapeDtypeStruct((t, w), BF16),
                   jax.ShapeDtypeStruct((t, LANES), F32),
                   jax.ShapeDtypeStruct((w, t), F32), jax.ShapeDtypeStruct((SUBLANES, t), F32)],
        scratch_shapes=[pltpu.VMEM((tq, w), F32), pltpu.VMEM((tq, w), F32),
                        pltpu.VMEM((ATT_HEADS, tq, LANES), F32)],
        args=[hbuf, hbuf, hbuf, eq, ek, dymix, lse_rows, delta_rows])


GROUP_W = SSD_WIDTH // SSD_GROUPS
HEADS_PER_GROUP = SSD_HEADS // SSD_GROUPS


def _ssd_chunk_common(xr, prev8, sm, cw, cb, dtb, avec):
    c = _conv_taps(xr, prev8, cw, cb)
    sig = _sigmoid(c)
    xa = c * sig
    dt = _softplus(sm + dtb)
    a = dt * avec
    acum = _cumsum_rows(a)
    return c, sig, xa, dt, acum


def _ssd_head_cols(acum, acum_t):
    cols = [_col(acum, LANE_DT + h) for h in range(SSD_HEADS)]
    rows = [_row(acum_t, LANE_DT + h) for h in range(SSD_HEADS)]
    return cols, rows


def _expand_heads(vals, width):
    rows = vals[0].shape[0]
    colhead = _iota((rows, width), 1) // HEAD_DIM
    out = jnp.broadcast_to(vals[0], (rows, width))
    for h in range(1, len(vals)):
        out = jnp.where(colhead == h, vals[h], out)
    return out


def _ssd_decays(cols, g):
    mine = cols[HEADS_PER_GROUP * g:HEADS_PER_GROUP * (g + 1)]
    n = mine[0].shape[0]
    atots = [c[n - 1:n, :] for c in mine]
    e = _expand_heads([jnp.exp(c) for c in mine], GROUP_W)
    dec = _expand_heads([jnp.exp(t - c) for c, t in zip(mine, atots)], GROUP_W)
    etot = _expand_heads([jnp.exp(t) for t in atots], GROUP_W)
    return e, dec, etot


def _ssd_ldec(cols, rows, h, tril):
    return jnp.exp(jnp.where(tril, cols[h] - rows[h], NEG))


def _ssd_fwd(hbuf, conv_w, conv_b, dtb_vec, a_vec, d_exp, norm_g, *, name):
    t = hbuf.shape[0]
    L = SSD_CHUNK
    nc = t // L
    hb = L // SUBLANES
    cs = COL_SMALL // LANES
    cz = COL_Z // SSD_WIDTH

    def body(x_ref, xp_ref, z_ref, s_ref, cw_ref, cb_ref, dtb_ref, av_ref, dx_ref, ng_ref,
             yc_ref, y_ref, st_ref, state):
        i = pl.program_id(0)

        @pl.when(i == 0)
        def _():
            state[...] = jnp.zeros_like(state)

        prev = jnp.where(i == 0, 0.0, xp_ref[...])
        _, _, xa, dt, acum = _ssd_chunk_common(x_ref[...], prev, s_ref[...], cw_ref[...], cb_ref[...],
                                               dtb_ref[...], av_ref[...])
        cols, rows = _ssd_head_cols(acum, acum.T)
        xs = xa[:, :SSD_WIDTH]
        xdt = xs * _head_expand(dt, LANE_DT, SSD_HEADS, SSD_WIDTH)
        tril = _iota((L, L), 0) >= _iota((L, L), 1)
        lane = _iota((1, LANES), 1)
        ys = []
        for g in range(SSD_GROUPS):
            bg = xa[:, SSD_WIDTH + SSD_STATE * g:SSD_WIDTH + SSD_STATE * (g + 1)].astype(BF16)
            cg = xa[:, SSD_WIDTH + SSD_STATE * (SSD_GROUPS + g):SSD_WIDTH + SSD_STATE * (SSD_GROUPS + g + 1)].astype(BF16)
            gm = _dot(cg, bg, 1, 1)
            e, dec, etot = _ssd_decays(cols, g)
            s_in = state[g]
            st_ref[0, g] = s_in
            xg = xdt[:, GROUP_W * g:GROUP_W * (g + 1)]
            y_off = e * _dot(cg, s_in.astype(BF16), 1, 0)
            state[g] = etot * s_in + _dot(bg, (dec * xg).astype(BF16), 0, 0)
            for pr in range(2):
                xp = xg[:, LANES * pr:LANES * (pr + 1)].astype(BF16)
                outs = []
                for hh in range(2):
                    h = HEADS_PER_GROUP * g + 2 * pr + hh
                    m = gm * _ssd_ldec(cols, rows, h, tril)
                    outs.append(_dot(m.astype(BF16), xp, 1, 0))
                ys.append(jnp.where(lane < HEAD_DIM, outs[0], outs[1]) + y_off[:, LANES * pr:LANES * (pr + 1)])
        y = jnp.concatenate(ys, axis=1)
        y_ref[...] = y
        yd = y + dx_ref[...] * xs
        zz = z_ref[...]
        y2 = yd * zz * _sigmoid(zz)
        ng = ng_ref[...]
        outs = []
        for g in range(SSD_GROUPS):
            yg = y2[:, GROUP_W * g:GROUP_W * (g + 1)]
            rs = lax.rsqrt(jnp.mean(yg * yg, axis=1, keepdims=True) + RMS_EPS)
            outs.append(yg * rs * ng[:, GROUP_W * g:GROUP_W * (g + 1)])
        yc_ref[...] = jnp.concatenate(outs, axis=1)

    cdim = SSD_CONV_DIM
    vecc = pl.BlockSpec((1, cdim), lambda i: (0, 0))
    vecl = pl.BlockSpec((1, LANES), lambda i: (0, 0))
    vecw = pl.BlockSpec((1, SSD_WIDTH), lambda i: (0, 0))
    roww = pl.BlockSpec((L, SSD_WIDTH), lambda i: (i, 0))
    return pl.pallas_call(
        body, name=name, grid=(nc,),
        in_specs=[pl.BlockSpec((L, cdim), lambda i: (i, 0)),
                  pl.BlockSpec((SUBLANES, cdim), lambda i: (jnp.maximum(i * hb - 1, 0), 0)),
                  pl.BlockSpec((L, SSD_WIDTH), lambda i: (i, cz)),
                  pl.BlockSpec((L, LANES), lambda i: (i, cs)),
                  pl.BlockSpec((CONV_K, cdim), lambda i: (0, 0)), vecc, vecl, vecl, vecw, vecw],
        out_specs=[roww, roww, pl.BlockSpec((1, SSD_GROUPS, SSD_STATE, GROUP_W), lambda i: (i, 0, 0, 0))],
        out_shape=[jax.ShapeDtypeStruct((t, SSD_WIDTH), F32), jax.ShapeDtypeStruct((t, SSD_WIDTH), F32),
                   jax.ShapeDtypeStruct((nc, SSD_GROUPS, SSD_STATE, GROUP_W), F32)],
        scratch_shapes=[pltpu.VMEM((SSD_GROUPS, SSD_STATE, GROUP_W), F32)],
        compiler_params=_params(1),
    )(hbuf, hbuf, hbuf, hbuf, conv_w, conv_b, dtb_vec, a_vec, d_exp, norm_g)


def _ssd_bwd(dymix, hbuf, y_ssd, states, conv_w, conv_b, dtb_vec, a_vec, d_exp, norm_g, *, name):
    t = hbuf.shape[0]
    L = SSD_CHUNK
    nc = t // L
    hb = L // SUBLANES
    cs = COL_SMALL // LANES
    cz = COL_Z // SSD_WIDTH
    cdy = (LRU_WIDTH + ATT_WIDTH) // SSD_WIDTH
    cdim = SSD_CONV_DIM

    def body(dyc_ref, x_ref, xp_ref, z_ref, s_ref, y_ref, st_ref, cw_ref, cb_ref, dtb_ref, av_ref, dx_ref, ng_ref,
             dxr_ref, dz_ref, dsm_ref, dng_ref, dd_ref, da_ref, ddtb_ref, dcw_ref, dcb_ref,
             dstate, dnext):
        i = pl.program_id(0)
        ic = nc - 1 - i

        @pl.when(i == 0)
        def _():
            dstate[...] = jnp.zeros_like(dstate)
            dnext[...] = jnp.zeros_like(dnext)
            for ref in (dng_ref, dd_ref, da_ref, ddtb_ref, dcw_ref, dcb_ref):
                ref[...] = jnp.zeros_like(ref)

        xr = x_ref[...]
        sm = s_ref[...]
        prev = jnp.where(ic == 0, 0.0, xp_ref[...])
        avec = av_ref[...]
        c, sig, xa, dt, acum = _ssd_chunk_common(xr, prev, sm, cw_ref[...], cb_ref[...], dtb_ref[...], avec)
        cols, rows = _ssd_head_cols(acum, acum.T)
        xs = xa[:, :SSD_WIDTH]
        dtx = _head_expand(dt, LANE_DT, SSD_HEADS, SSD_WIDTH)
        xdt = xs * dtx
        tril = _iota((L, L), 0) >= _iota((L, L), 1)
        lane = _iota((1, LANES), 1)
        hmasks = (lane < HEAD_DIM, lane >= HEAD_DIM)

        y = y_ref[...]
        dexp = dx_ref[...]
        yd = y + dexp * xs
        zz = z_ref[...]
        sz = _sigmoid(zz)
        siluz = zz * sz
        y2 = yd * siluz
        ng = ng_ref[...]
        dyc = dyc_ref[...]
        dy2s, dngs = [], []
        for g in range(SSD_GROUPS):
            sl = slice(GROUP_W * g, GROUP_W * (g + 1))
            yg = y2[:, sl]
            rs = lax.rsqrt(jnp.mean(yg * yg, axis=1, keepdims=True) + RMS_EPS)
            wv = dyc[:, sl] * ng[:, sl]
            dngs.append(jnp.sum(dyc[:, sl] * yg * rs, axis=0, keepdims=True))
            dy2s.append(rs * wv - yg * (rs * rs * rs) * jnp.mean(wv * yg, axis=1, keepdims=True))
        dy2 = jnp.concatenate(dy2s, axis=1)
        dng_ref[...] += jnp.concatenate(dngs, axis=1)
        dz_ref[...] = (dy2 * yd * (sz * (1.0 + zz * (1.0 - sz)))).astype(BF16)
        dy = dy2 * siluz
        dd_ref[...] += jnp.sum(dy * xs, axis=0, keepdims=True)

        dxs, dbs, dcs = [], [], []
        datot = jnp.zeros((1, LANES), F32)
        lanes = _iota((L, LANES), 1)
        dacum = jnp.zeros((L, LANES), F32)
        for g in range(SSD_GROUPS):
            sl = slice(GROUP_W * g, GROUP_W * (g + 1))
            bg = xa[:, SSD_WIDTH + SSD_STATE * g:SSD_WIDTH + SSD_STATE * (g + 1)].astype(BF16)
            cg = xa[:, SSD_WIDTH + SSD_STATE * (SSD_GROUPS + g):SSD_WIDTH + SSD_STATE * (SSD_GROUPS + g + 1)].astype(BF16)
            gm = _dot(cg, bg, 1, 1)
            e, dec, etot = _ssd_decays(cols, g)
            s_in = st_ref[0, g]
            ds_out = dstate[g]
            dyg = dy[:, sl]
            xg = xdt[:, sl]
            edy = (e * dyg).astype(BF16)
            dstate[g] = etot * ds_out + _dot(cg, edy, 0, 0)
            dx_state = dec * _dot(bg, ds_out.astype(BF16), 1, 0)
            y_off = e * _dot(cg, s_in.astype(BF16), 1, 0)
            dacum = dacum + _head_reduce_group(dyg * y_off - xg * dx_state, g)
            dc_off = _dot(edy, s_in.astype(BF16), 1, 1)
            db_state = _dot((dec * xg).astype(BF16), ds_out.astype(BF16), 1, 1)
            dgsum = jnp.zeros((L, L), F32)
            dx_pairs = []
            for pr in range(2):
                psl = slice(LANES * pr, LANES * (pr + 1))
                xp = xg[:, psl]
                dyp = dyg[:, psl]
                dx_pair = jnp.zeros((L, LANES), F32)
                for hh in range(2):
                    h = HEADS_PER_GROUP * g + 2 * pr + hh
                    ldec = _ssd_ldec(cols, rows, h, tril)
                    dym = jnp.where(hmasks[hh], dyp, 0.0).astype(BF16)
                    xm = jnp.where(hmasks[hh], xp, 0.0).astype(BF16)
                    dx_pair = dx_pair + _dot((gm * ldec).astype(BF16), dym, 0, 0)
                    dml = _dot(dym, xm, 1, 1) * ldec
                    dgsum = dgsum + dml
                    qm = dml * gm
                    seg = jnp.sum(qm, axis=1, keepdims=True) - jnp.sum(qm.T, axis=1, keepdims=True)
                    dacum = dacum + jnp.where(lanes == LANE_DT + h, seg, 0.0)
                dx_pairs.append(dx_pair)
            dgb = dgsum.astype(BF16)
            dcs.append(_dot(dgb, bg, 1, 0) + dc_off)
            dbs.append(_dot(dgb, cg, 0, 0) + db_state)
            dxg = jnp.concatenate(dx_pairs, axis=1) + dx_state
            dxs.append(dxg)
            v = jnp.sum(dx_state * xg, axis=0, keepdims=True) + etot * jnp.sum(ds_out * s_in, axis=0, keepdims=True)
            datot = datot + _head_reduce_row(v, LANE_DT + HEADS_PER_GROUP * g, HEADS_PER_GROUP)
        dx = jnp.concatenate(dxs, axis=1)
        dacum = dacum + jnp.where(_iota((L, LANES), 0) == L - 1, datot, 0.0)
        da = _cumsum_rows(dacum, reverse=True)
        ddt = da * avec + _head_reduce(dx * xs, LANE_DT, SSD_HEADS)
        da_ref[...] += jnp.sum(da * dt, axis=0, keepdims=True)
        ddt_raw = ddt * _sigmoid(sm + dtb_ref[...])
        ddt_raw = jnp.where((lanes >= LANE_DT) & (lanes < LANE_DT + SSD_HEADS), ddt_raw, 0.0)
        dsm_ref[...] = ddt_raw
        ddtb_ref[...] += jnp.sum(ddt_raw, axis=0, keepdims=True)
        dxs_total = dx * dtx + dexp * dy
        dxa = jnp.concatenate([dxs_total] + dbs + dcs, axis=1)
        dc = dxa * (sig * (1.0 + c * (1.0 - sig)))
        dxr, dws = _conv_taps_bwd(dc, dnext[...], cw_ref[...], xr)
        dxr_ref[...] = dxr.astype(BF16)
        dcw_ref[...] += dws
        dcb_ref[...] += jnp.sum(dc, axis=0, keepdims=True)
        dnext[...] = dc[:SUBLANES]

    rev = lambda i: nc - 1 - i
    vecc = pl.BlockSpec((1, cdim), lambda i: (0, 0))
    vecl = pl.BlockSpec((1, LANES), lambda i: (0, 0))
    vecw = pl.BlockSpec((1, SSD_WIDTH), lambda i: (0, 0))
    cwspec = pl.BlockSpec((CONV_K, cdim), lambda i: (0, 0))
    roww = pl.BlockSpec((L, SSD_WIDTH), lambda i: (rev(i), 0))
    return pl.pallas_call(
        body, name=name, grid=(nc,),
        in_specs=[pl.BlockSpec((L, SSD_WIDTH), lambda i: (rev(i), cdy)),
                  pl.BlockSpec((L, cdim), lambda i: (rev(i), 0)),
                  pl.BlockSpec((SUBLANES, cdim), lambda i: (jnp.maximum(rev(i) * hb - 1, 0), 0)),
                  pl.BlockSpec((L, SSD_WIDTH), lambda i: (rev(i), cz)),
                  pl.BlockSpec((L, LANES), lambda i: (rev(i), cs)),
                  roww,
                  pl.BlockSpec((1, SSD_GROUPS, SSD_STATE, GROUP_W), lambda i: (rev(i), 0, 0, 0)),
                  cwspec, vecc, vecl, vecl, vecw, vecw],
        out_specs=[pl.BlockSpec((L, cdim), lambda i: (rev(i), 0)), roww,
                   pl.BlockSpec((L, LANES), lambda i: (rev(i), 0)),
                   vecw, vecw, vecl, vecl, cwspec, vecc],
        out_shape=[jax.ShapeDtypeStruct((t, cdim), BF16), jax.ShapeDtypeStruct((t, SSD_WIDTH), BF16),
                   jax.ShapeDtypeStruct((t, LANES), F32),
                   jax.ShapeDtypeStruct((1, SSD_WIDTH), F32), jax.ShapeDtypeStruct((1, SSD_WIDTH), F32),
                   jax.ShapeDtypeStruct((1, LANES), F32), jax.ShapeDtypeStruct((1, LANES), F32),
                   jax.ShapeDtypeStruct((CONV_K, cdim), F32), jax.ShapeDtypeStruct((1, cdim), F32)],
        scratch_shapes=[pltpu.VMEM((SSD_GROUPS, SSD_STATE, GROUP_W), F32), pltpu.VMEM((SUBLANES, cdim), F32)],
        compiler_params=_params(1),
    )(dymix, hbuf, hbuf, hbuf, hbuf, y_ssd, states, conv_w, conv_b, dtb_vec, a_vec, d_exp, norm_g)


def _head_reduce_group(x, g):
    return _head_reduce(x, LANE_DT + HEADS_PER_GROUP * g, HEADS_PER_GROUP)


def _head_reduce_row(v, lane0, nheads):
    colhead = _iota(v.shape, 1) // HEAD_DIM
    lane = _iota((1, LANES), 1)
    out = jnp.zeros((1, LANES), F32)
    for h in range(nheads):
        s = jnp.sum(jnp.where(colhead == h, v, 0.0), axis=1, keepdims=True)
        out = jnp.where(lane == lane0 + h, s, out)
    return out


def _exchange(inps, axes, *, swap=False, name):
    n = 2 ** len(axes)
    assert not swap or n == 2
    counts = [a.shape[0] for a in inps]
    out_shapes = [jax.ShapeDtypeStruct(a.shape if swap else (n,) + a.shape, a.dtype) for a in inps]
    units = sum(counts)
    na = len(inps)

    def body(*refs):
        in_refs, out_refs = refs[:na], refs[na:2 * na]
        send_sems, recv_sems, local_sems = refs[2 * na:]
        pos = {ax: lax.axis_index(ax) for ax in MESH_AXES}

        def slot_of(coord):
            s = 0
            for ax in axes:
                s = s * 2 + coord[ax]
            return s

        me = slot_of(pos)
        copies = []
        unit = 0
        for a in range(na):
            for it in range(counts[a]):
                dst = out_refs[a].at[it] if swap else out_refs[a].at[me, it]
                if not swap:
                    cp = pltpu.make_async_copy(in_refs[a].at[it], dst, local_sems.at[unit])
                    cp.start()
                    copies.append(cp)
                for delta in range(1, n):
                    coord = dict(pos)
                    for b, ax in enumerate(reversed(axes)):
                        if (delta >> b) & 1:
                            coord[ax] = 1 - pos[ax]
                    k = unit * (n - 1) + delta - 1
                    cp = pltpu.make_async_remote_copy(
                        src_ref=in_refs[a].at[it], dst_ref=dst,
                        send_sem=send_sems.at[k], recv_sem=recv_sems.at[k],
                        device_id=(coord["x"], coord["y"], coord["c"]), device_id_type=pl.DeviceIdType.MESH)
                    cp.start()
                    copies.append(cp)
                unit += 1
        for cp in copies:
            cp.wait()

    any_spec = pl.BlockSpec(memory_space=pl.ANY)
    return pl.pallas_call(
        body, name=name,
        in_specs=[any_spec] * na, out_specs=[any_spec] * na, out_shape=out_shapes,
        scratch_shapes=[pltpu.SemaphoreType.DMA((units * (n - 1),)), pltpu.SemaphoreType.DMA((units * (n - 1),)),
                        pltpu.SemaphoreType.DMA((units,))],
    )(*inps)


class _Comm:
    def __init__(self, arrays, out_shapes, n_own, start, finish, base=0, middle=None):
        self.arrays, self.out_shapes, self.start, self.finish = arrays, out_shapes, start, finish
        self.middle = middle or (lambda *refs: None)
        self.base, self.n_own, self.n_sems = base, n_own, base + n_own

    def specs(self):
        any_spec = pl.BlockSpec(memory_space=pl.ANY)
        sems = [pltpu.SemaphoreType.DMA((self.n_sems,)), pltpu.SemaphoreType.DMA((self.n_sems,))]
        return [any_spec] * len(self.arrays), [any_spec] * len(self.out_shapes), sems


def _run_comm(comm, *, name):
    na, no = len(comm.arrays), len(comm.out_shapes)

    def body(*refs):
        args = (refs[:na], refs[na:na + no]) + tuple(refs[na + no:])
        comm.start(*args)
        comm.middle(*args)
        comm.finish(*args)

    in_specs, out_specs, sems = comm.specs()
    return pl.pallas_call(body, name=name, in_specs=in_specs, out_specs=out_specs, out_shape=comm.out_shapes,
                          scratch_shapes=sems)(*comm.arrays)


def _chip_peer(x, y, d):
    px = 1 - x if d & 2 else x
    py = 1 - y if d & 1 else y
    return px, py, 2 * px + py


def _gather_layer_comm(srcs, li, base=0):
    counts = [s.shape[0] for s in srcs]
    units = [(a, it) for a in range(len(srcs)) for it in range(counts[a])]
    n_ici = 3 * len(units)
    out_shapes = [jax.ShapeDtypeStruct((N_CHIPS,) + s.shape, s.dtype) for s in srcs]

    def ici(ins, outs, ssem, rsem, u, d):
        x, y, c = (lax.axis_index(ax) for ax in MESH_AXES)
        a, it = units[u]
        px, py, _ = _chip_peer(x, y, d)
        k = base + 3 * u + d - 1
        return pltpu.make_async_remote_copy(
            src_ref=ins[a].at[it], dst_ref=outs[a].at[2 * x + y, it], send_sem=ssem.at[k], recv_sem=rsem.at[k],
            device_id=(px, py, c), device_id_type=pl.DeviceIdType.MESH)

    def arrived(ins, outs, ssem, rsem, u, d):
        x, y, c = (lax.axis_index(ax) for ax in MESH_AXES)
        a, it = units[u]
        _, _, pk = _chip_peer(x, y, d)
        k = base + 3 * u + d - 1
        return pltpu.make_async_remote_copy(
            src_ref=ins[a].at[it], dst_ref=outs[a].at[pk, it], send_sem=ssem.at[k], recv_sem=rsem.at[k],
            device_id=(x, y, c), device_id_type=pl.DeviceIdType.MESH)

    def forward(ins, outs, ssem, rsem, u, slot):
        x, y, c = (lax.axis_index(ax) for ax in MESH_AXES)
        a, it = units[u]
        pk = 2 * x + y if slot == 0 else _chip_peer(x, y, slot)[2]
        src = ins[a].at[it] if slot == 0 else outs[a].at[pk, it]
        k = base + n_ici + 4 * u + slot
        return pltpu.make_async_remote_copy(
            src_ref=src, dst_ref=outs[a].at[pk, it], send_sem=ssem.at[k], recv_sem=rsem.at[k],
            device_id=(x, y, 1 - c), device_id_type=pl.DeviceIdType.MESH)

    def start(ins, outs, ssem, rsem):
        for u in range(len(units)):
            forward(ins, outs, ssem, rsem, u, 0).start()

        @pl.when(lax.axis_index("c") == li)
        def _():
            for u in range(len(units)):
                for d in range(1, N_CHIPS):
                    ici(ins, outs, ssem, rsem, u, d).start()

    def middle(ins, outs, ssem, rsem):
        @pl.when(lax.axis_index("c") == li)
        def _():
            for u in range(len(units)):
                for d in range(1, N_CHIPS):
                    arrived(ins, outs, ssem, rsem, u, d).wait_recv()
                    forward(ins, outs, ssem, rsem, u, d).start()

    def finish(ins, outs, ssem, rsem):
        c = lax.axis_index("c")

        @pl.when(c == li)
        def _():
            for u in range(len(units)):
                for d in range(1, N_CHIPS):
                    ici(ins, outs, ssem, rsem, u, d).wait_send()
                    forward(ins, outs, ssem, rsem, u, d).wait_send()

        @pl.when(c != li)
        def _():
            for u in range(len(units)):
                for d in range(1, N_CHIPS):
                    forward(ins, outs, ssem, rsem, u, d).wait_recv()

        for u in range(len(units)):
            forward(ins, outs, ssem, rsem, u, 0).wait()

    return _Comm(srcs, out_shapes, n_ici + 4 * len(units), start, finish, base, middle)


def _reduce_chips_comm(sums, li, base=0):
    counts = [s.shape[0] for s in sums]
    units = [(a, it) for a in range(len(sums)) for it in range(counts[a])]
    out_shapes = [jax.ShapeDtypeStruct((N_CHIPS, s.shape[0]) + s.shape[2:], s.dtype) for s in sums]

    def copy(ins, outs, ssem, rsem, u, d):
        x, y, c = (lax.axis_index(ax) for ax in MESH_AXES)
        a, it = units[u]
        px, py, pk = _chip_peer(x, y, d)
        k = base + 3 * u + d - 1
        return pltpu.make_async_remote_copy(
            src_ref=ins[a].at[it, pk], dst_ref=outs[a].at[2 * x + y, it], send_sem=ssem.at[k], recv_sem=rsem.at[k],
            device_id=(px, py, c), device_id_type=pl.DeviceIdType.MESH)

    def start(ins, outs, ssem, rsem):
        @pl.when(lax.axis_index("c") == li)
        def _():
            for u in range(len(units)):
                for d in range(1, N_CHIPS):
                    copy(ins, outs, ssem, rsem, u, d).start()

    def finish(ins, outs, ssem, rsem):
        @pl.when(lax.axis_index("c") == li)
        def _():
            for u in range(len(units)):
                for d in range(1, N_CHIPS):
                    copy(ins, outs, ssem, rsem, u, d).wait()

    return _Comm(sums, out_shapes, 3 * len(units), start, finish, base)


def _sum_slots(buf, out_dtype, *, name):
    n, rows, cols = buf.shape
    tm = _pick(rows, (1024, 512, 256, 128, 8))
    if rows % tm:
        tm = rows

    def body(b_ref, o_ref):
        acc = b_ref[0].astype(F32)
        for s in range(1, n):
            acc = acc + b_ref[s].astype(F32)
        o_ref[...] = acc.astype(out_dtype)

    return pl.pallas_call(
        body, name=name, grid=(pl.cdiv(rows, tm),),
        in_specs=[pl.BlockSpec((n, tm, cols), lambda i: (0, i, 0))],
        out_specs=pl.BlockSpec((tm, cols), lambda i: (i, 0)),
        out_shape=jax.ShapeDtypeStruct((rows, cols), out_dtype),
        compiler_params=_params(1),
    )(buf)


def _sum_pair(a, b, out_dtype, *, name):
    shape = a.shape
    cols = shape[-1]
    a2, b2 = a.reshape(-1, cols), b.reshape(-1, cols)
    rows = a2.shape[0]
    tm = _pick(rows, (1024, 512, 256, 128, 8))

    def body(a_ref, b_ref, o_ref):
        o_ref[...] = (a_ref[...].astype(F32) + b_ref[...].astype(F32)).astype(out_dtype)

    spec = pl.BlockSpec((tm, cols), lambda i: (i, 0))
    return pl.pallas_call(
        body, name=name, grid=(rows // tm,), in_specs=[spec, spec], out_specs=spec,
        out_shape=jax.ShapeDtypeStruct((rows, cols), out_dtype), compiler_params=_params(1),
    )(a2, b2).reshape(shape)


def _adamw(w, g, m, v, *, name):
    shape = w.shape
    cols = shape[-1]
    rows = w.size // cols
    w2, g2, m2, v2 = (a.reshape(rows, cols) for a in (w, g, m, v))
    tm = _pick(rows, (512, 256, 128, 64, 32, 16, 8))
    if rows % tm:
        tm = rows
    bc1 = 1.0 - ADAM_B1 ** ADAM_STEP
    bc2 = 1.0 - ADAM_B2 ** ADAM_STEP

    def body(w_ref, g_ref, m_ref, v_ref, d_ref, nm_ref, nv_ref):
        gg = g_ref[...]
        mm = ADAM_B1 * m_ref[...] + (1.0 - ADAM_B1) * gg
        vv = ADAM_B2 * v_ref[...] + (1.0 - ADAM_B2) * (gg * gg)
        m_hat = mm / bc1
        v_hat = vv / bc2
        d_ref[...] = -ADAM_LR * (m_hat / (jnp.sqrt(v_hat) + ADAM_EPS) + ADAM_WD * w_ref[...])
        nm_ref[...] = mm
        nv_ref[...] = vv

    spec = pl.BlockSpec((tm, cols), lambda i: (i, 0))
    o = jax.ShapeDtypeStruct((rows, cols), F32)
    outs = pl.pallas_call(
        body, name=name, grid=(rows // tm,), in_specs=[spec] * 4, out_specs=[spec] * 3, out_shape=[o] * 3,
        compiler_params=_params(1),
    )(w2, g2, m2, v2)
    return tuple(a.reshape(shape) for a in outs)


def _layer_fwd(li, x, xb, pb, W, up=None, att=None):
    nm = lambda s: f"l{li}_{s}"
    sv = {"x_in_b": xb}
    (g1, u1, a1), got = _mm_swiglu(xb, W["ffn1_wg"], W["ffn1_wu"], comm=up[0] if up else None, name=nm("ffn1_up"))
    if up:
        W = {**W, **up[1](got)}
    x1, x1b, xh1, rs1 = _mm_ln(a1, W["ffn1_wd"], x, W["ln1_g"], W["ln1_b"], rscale=ALPHA, mscale=0.5, name=nm("ffn1_down_ln"))
    hbuf = _mm(x1b, W["w_in_p"], name=nm("in_proj"))
    ya, lu, lr, lig, la, lh = _lru_fwd(hbuf, W["lru_conv_w"], W["lru_conv_b"], W["lru_wa_bd"], W["lru_ba"],
                                       W["lru_wx_bd"], W["lru_bx"], W["lru_lambda"], name=nm("lru_fwd"))
    eq, ek = _fox_prep(hbuf, W["fox_bf_vec"], name=nm("fox_prep"))
    (yb, lse_rows), got = _fox_fwd(hbuf, eq, ek, comm=att[0] if att else None, name=nm("fox_fwd"))
    if att:
        W = {**W, **att[1](got)}
    yc, yssd, states = _ssd_fwd(hbuf, W["ssd_conv_w"], W["ssd_conv_b"], W["ssd_dtb_vec"], W["ssd_a_vec"],
                                W["ssd_d_exp"], W["ssd_norm_g"], name=nm("ssd_fwd"))
    ymix = _assemble([ya, yb, yc], D_MODEL, name=nm("y_mix"))
    x2, x2b, xh2, rs2 = _mm_ln(ymix, W["w_out"], x1, W["ln2_g"], W["ln2_b"], rscale=ALPHA, mscale=1.0, name=nm("out_proj_ln"))
    (g2, u2, a2), _ = _mm_swiglu(x2b, W["ffn2_wg"], W["ffn2_wu"], name=nm("ffn2_up"))
    x3, x3b, xh3, rs3 = _mm_ln(a2, W["ffn2_wd"], x2, W["ln3_g"], W["ln3_b"], rscale=ALPHA, mscale=0.5, name=nm("ffn2_down_ln"))
    x4, x4b, sg, e = _mm_pe(x3, x3b, pb, W["pe_gate_w"], W["pe_gate_b"], W["pe_proj"], name=nm("ple"))
    sv.update(g1=g1, u1=u1, a1=a1, x1b=x1b, xh1=xh1, rs1=rs1, hbuf=hbuf, lu=lu, lr=lr, lig=lig, la=la, lh=lh,
              eq=eq, ek=ek, lse_rows=lse_rows, yb=yb, yssd=yssd, states=states, ymix=ymix, x2b=x2b, xh2=xh2, rs2=rs2,
              g2=g2, u2=u2, a2=a2, x3b=x3b, xh3=xh3, rs3=rs3, sg=sg, e=e, pb=pb)
    return x4, x4b, sv, W


def _layer_bwd(li, dx4, sv, W, comm=None, late=None, last=None):
    nm = lambda s: f"l{li}_{s}"
    G = {}
    dgp, de, dbg = _pe_bwd_elem(dx4, sv["sg"], sv["e"], name=nm("ple_bwd"))
    G["pe_gate_b"] = dbg
    G["pe_gate_w"] = _mm(sv["x3b"], dgp, ta=True, out_dtype=BF16, name=nm("d_pe_gate_w"))
    G["pe_proj"] = _mm(sv["pb"], de, ta=True, out_dtype=BF16, chip_cols=True, name=nm("d_pe_proj"))
    dr3, dr3b, G["ln3_g"], G["ln3_b"] = _bwd_proj([(dgp, W["pe_gate_w"])], dx4, rscale=1.0,
                                                  ln=(sv["xh3"], sv["rs3"], W["ln3_g"]), name=nm("ln3_bwd"))
    G["ffn2_wd"] = _mm(sv["a2"], dr3b, ta=True, scale=0.5, out_dtype=BF16, name=nm("d_ffn2_wd"))
    dg2, du2 = _mm_swiglu_bwd(dr3b, W["ffn2_wd"], sv["g2"], sv["u2"], scale=0.5, name=nm("ffn2_act_bwd"))
    G["ffn2_wg"] = _mm(sv["x2b"], dg2, ta=True, out_dtype=BF16, chip_cols=True, name=nm("d_ffn2_wg"))
    G["ffn2_wu"] = _mm(sv["x2b"], du2, ta=True, out_dtype=BF16, chip_cols=True, name=nm("d_ffn2_wu"))
    dr2, dr2b, G["ln2_g"], G["ln2_b"] = _bwd_proj([(dg2, W["ffn2_wg"]), (du2, W["ffn2_wu"])], dr3, rscale=ALPHA,
                                                  ln=(sv["xh2"], sv["rs2"], W["ln2_g"]), name=nm("ln2_bwd"))
    G["w_out"] = _mm(sv["ymix"], dr2b, ta=True, out_dtype=BF16, name=nm("d_w_out"))
    dymix = _mm(dr2b, W["w_out"], tb=True, name=nm("d_ymix"))
    hbuf = sv["hbuf"]
    (dur, dgr, G["lru_conv_w"], G["lru_conv_b"], G["lru_wa_bd"], G["lru_ba"], G["lru_wx_bd"], G["lru_bx"],
     G["lru_lambda"]) = _lru_bwd(dymix, hbuf, sv["lu"], sv["lr"], sv["lig"], sv["la"], sv["lh"],
                                 W["lru_conv_w"], W["lru_wa_bd"], W["lru_wx_bd"], W["lru_lambda"], name=nm("lru_bwd"))
    delta = _fox_delta(dymix, sv["yb"], name=nm("fox_delta"))
    delta_rows = jnp.pad(delta[:, :ATT_HEADS].T, ((0, SUBLANES - ATT_HEADS), (0, 0)))
    comm = _merge_comms([comm, late(G) if late else None])
    (dk, dv, dfk, dqt, dfq), comm_out = _fox_bwd(hbuf, sv["eq"], sv["ek"], dymix, sv["lse_rows"], delta_rows,
                                                 comm=comm, name=nm("fox_bwd"))
    dq = dqt.T
    dfc = jnp.pad(dfq[:ATT_HEADS].T, ((0, 0), (0, LANES - ATT_HEADS))) - dfk
    dsm_f, G["fox_bf_vec"] = _fox_post(dfc, hbuf, W["fox_bf_vec"], name=nm("fox_post"))
    (dxr, dz, dsm_dt, G["ssd_norm_g"], G["ssd_d_exp"], G["ssd_a_vec"], G["ssd_dtb_vec"], G["ssd_conv_w"],
     G["ssd_conv_b"]) = _ssd_bwd(dymix, hbuf, sv["yssd"], sv["states"], W["ssd_conv_w"], W["ssd_conv_b"],
                                 W["ssd_dtb_vec"], W["ssd_a_vec"], W["ssd_d_exp"], W["ssd_norm_g"], name=nm("ssd_bwd"))
    dh = _assemble([dxr, dz, dur, dgr, dq, dk, dv, dsm_f + dsm_dt], H_WIDTH, name=nm("d_h"))
    G["w_in_p"] = _mm(sv["x1b"], dh, ta=True, name=nm("d_w_in"))
    dr1, dr1b, G["ln1_g"], G["ln1_b"] = _bwd_proj([(dh, W["w_in_p"])], dr2, rscale=ALPHA,
                                                  ln=(sv["xh1"], sv["rs1"], W["ln1_g"]), name=nm("ln1_bwd"))
    G["ffn1_wd"] = _mm(sv["a1"], dr1b, ta=True, scale=0.5, out_dtype=BF16, name=nm("d_ffn1_wd"))
    dg1, du1 = _mm_swiglu_bwd(dr1b, W["ffn1_wd"], sv["g1"], sv["u1"], scale=0.5, name=nm("ffn1_act_bwd"))
    G["ffn1_wg"] = _mm(sv["x_in_b"], dg1, ta=True, out_dtype=BF16, chip_cols=True, name=nm("d_ffn1_wg"))
    G["ffn1_wu"] = _mm(sv["x_in_b"], du1, ta=True, out_dtype=BF16, chip_cols=True, name=nm("d_ffn1_wu"))
    dx_in, *last_out = _bwd_proj([(dg1, W["ffn1_wg"]), (du1, W["ffn1_wu"])], dr1, rscale=ALPHA, ln=None,
                                 comm=last(G) if last else None, name=nm("x_in_bwd"))
    return dx_in, G, comm_out, (last_out[0] if last_out else None)


def _block_diag(w):
    n, b, _ = w.shape
    eye = jnp.eye(n, dtype=w.dtype)
    return (eye[:, None, :, None] * w[:, :, None, :]).reshape(n * b, n * b)


def _block_diag_extract(m):
    n, b = LRU_HEADS, HEAD_DIM
    return jnp.stack([m[b * i:b * (i + 1), b * i:b * (i + 1)] for i in range(n)])


def _lane_vec(v, lane0):
    return jnp.pad(v.astype(F32), (lane0, LANES - lane0 - v.shape[0])).reshape(1, LANES)


def _w_in_permute(w):
    d = w.shape[0]
    z = lambda n: jnp.zeros((d, n), w.dtype)
    return jnp.concatenate([w[:, 1796:2820], w[:, 1284:1796], w[:, 0:512], w[:, 512:1280],
                            w[:, 1280:1284], w[:, 2820:2828], z(LANES - 12), z(H_WIDTH - COL_SMALL - LANES)], axis=1)


def _w_in_unpermute(wp):
    return jnp.concatenate([wp[:, COL_U:COL_Q], wp[:, COL_Q:COL_SMALL], wp[:, COL_SMALL:COL_SMALL + 4],
                            wp[:, COL_Z:COL_U], wp[:, COL_XBC:COL_Z], wp[:, COL_SMALL + 4:COL_SMALL + 12]], axis=1)


def _big_weights(chipw):
    W = {}
    for n, w in chipw.items():
        if n in ("ffn1_wg", "ffn1_wu", "ffn2_wg", "ffn2_wu"):
            W[n] = w
        elif n in ("ffn1_wd", "ffn2_wd", "w_out", "pe_gate_w"):
            W[n] = w.reshape(-1, D_MODEL)
        elif n == "pe_proj":
            W[n] = jnp.moveaxis(w, 0, 1).reshape(PLE_DIM, D_MODEL)
        else:
            w_in = jnp.moveaxis(w[:, :, :IN_WIDTH // N_CHIPS], 0, 1).reshape(D_MODEL, IN_WIDTH)
            W["w_in_p"] = _w_in_permute(w_in)
    return W


def _small_weights(li, small):
    g = lambda n: small[n][li]
    W = {n: g(n) for n in ("ln1_g", "ln1_b", "ln2_g", "ln2_b", "ln3_g", "ln3_b", "pe_gate_b", "lru_conv_w",
                           "ssd_conv_w")}
    for n in ("lru_conv_b", "lru_ba", "lru_bx", "lru_lambda", "ssd_conv_b", "ssd_norm_g"):
        W[n] = g(n).reshape(1, -1)
    W["lru_wa_bd"] = _block_diag(g("lru_wa")).astype(BF16)
    W["lru_wx_bd"] = _block_diag(g("lru_wx")).astype(BF16)
    W["fox_bf_vec"] = _lane_vec(g("fox_bf"), LANE_F)
    W["ssd_dtb_vec"] = _lane_vec(g("ssd_dt_bias"), LANE_DT)
    W["ssd_a_vec"] = _lane_vec(-jnp.exp(g("ssd_a_log")), LANE_DT)
    W["ssd_d_exp"] = jnp.repeat(g("ssd_d"), HEAD_DIM).reshape(1, SSD_WIDTH)
    return W


def _big_grad_by_chip(G, n):
    if n in ("ffn1_wg", "ffn1_wu", "ffn2_wg", "ffn2_wu", "pe_proj"):
        return G[n]
    if n in ("ffn1_wd", "ffn2_wd", "w_out", "pe_gate_w"):
        return G[n].reshape(N_CHIPS, -1, D_MODEL)
    share = IN_WIDTH // N_CHIPS
    d_w_in = jnp.moveaxis(_w_in_unpermute(G["w_in_p"]).reshape(D_MODEL, N_CHIPS, share), 1, 0)
    return jnp.pad(d_w_in.astype(BF16), ((0, 0), (0, 0), (0, SHARE - share)))


def _layer_small_grads(G, W):
    out = {n: G[n] for n in ("lru_conv_w", "ssd_conv_w")}
    for n in ("ln1_g", "ln1_b", "ln2_g", "ln2_b", "ln3_g", "ln3_b", "pe_gate_b", "lru_conv_b", "lru_ba", "lru_bx",
              "lru_lambda", "ssd_conv_b", "ssd_norm_g"):
        out[n] = G[n].reshape(-1)
    out["lru_wa"] = _block_diag_extract(G["lru_wa_bd"])
    out["lru_wx"] = _block_diag_extract(G["lru_wx_bd"])
    out["fox_bf"] = G["fox_bf_vec"][0, LANE_F:LANE_F + ATT_HEADS]
    out["ssd_dt_bias"] = G["ssd_dtb_vec"][0, LANE_DT:LANE_DT + SSD_HEADS]
    out["ssd_a_log"] = G["ssd_a_vec"][0, LANE_DT:LANE_DT + SSD_HEADS] * W["ssd_a_vec"][0, LANE_DT:LANE_DT + SSD_HEADS]
    out["ssd_d"] = G["ssd_d_exp"].reshape(SSD_HEADS, HEAD_DIM).sum(axis=1)
    return out


WEIGHTS = ['ln1_g', 'ln1_b', 'ffn1_wg', 'ffn1_wu', 'ffn1_wd', 'w_in', 'lru_conv_w', 'lru_conv_b', 'lru_wa', 'lru_ba',
           'lru_wx', 'lru_bx', 'lru_lambda', 'fox_bf', 'ssd_conv_w', 'ssd_conv_b', 'ssd_dt_bias', 'ssd_a_log', 'ssd_d',
           'ssd_norm_g', 'w_out', 'ln2_g', 'ln2_b', 'ffn2_wg', 'ffn2_wu', 'ffn2_wd', 'ln3_g', 'ln3_b', 'pe_proj',
           'pe_gate_w', 'pe_gate_b']
FIRST = ((("ffn1_wg",), 1), (("ffn1_wu",), 1))
NEXT = ((("w_in",), 1),
        (("ffn1_wd",), 0))
EARLY = FIRST + NEXT
LATE = ((("ffn2_wg",), 1), (("ffn2_wu",), 1),
        (("ffn2_wd",), 0),
        (("w_out",), None), (("pe_gate_w",), None),
        (("pe_proj",), None))
BIG = {n: pad for names, pad in EARLY + LATE for n in names}
SMALL_SHARDED = {'lru_conv_w': 2, 'ssd_conv_w': 2}


def _unshard(seg, axis):
    moved = jnp.moveaxis(seg, 0, axis)
    shp = list(moved.shape)
    shp[axis:axis + 2] = [shp[axis] * shp[axis + 1]]
    return moved.reshape(shp)


def _pad_axis(a, axis, size):
    if axis is None or a.shape[axis] == size:
        return a
    pads = [(0, 0)] * a.ndim
    pads[axis] = (0, size - a.shape[axis])
    return jnp.pad(a, pads)


PACK_TILE = SUBLANES * LANES


def _pack(arrs):
    rows = []
    for a in arrs:
        flat = a.astype(F32).reshape(-1)
        rows.append(jnp.pad(flat, (0, (-flat.shape[0]) % PACK_TILE)).reshape(-1, LANES))
    return jnp.concatenate(rows, axis=0)


def _unpack(packed, shapes):
    out, off = [], 0
    for s in shapes:
        n = math.prod(s)
        r = -(-n // PACK_TILE) * SUBLANES
        out.append(packed[off:off + r].reshape(-1)[:n].reshape(s))
        off += r
    return out


def kernel(x, p, ln1_g, ln1_b, ffn1_wg, ffn1_wu, ffn1_wd, w_in, lru_conv_w, lru_conv_b, lru_wa, lru_ba, lru_wx, lru_bx, lru_lambda, fox_bf, ssd_conv_w, ssd_conv_b, ssd_dt_bias, ssd_a_log, ssd_d, ssd_norm_g, w_out, ln2_g, ln2_b, ffn2_wg, ffn2_wu, ffn2_wd, ln3_g, ln3_b, pe_proj, pe_gate_w, pe_gate_b, loss_target, m_ln1_g, m_ln1_b, m_ffn1_wg, m_ffn1_wu, m_ffn1_wd, m_w_in, m_lru_conv_w, m_lru_conv_b, m_lru_wa, m_lru_ba, m_lru_wx, m_lru_bx, m_lru_lambda, m_fox_bf, m_ssd_conv_w, m_ssd_conv_b, m_ssd_dt_bias, m_ssd_a_log, m_ssd_d, m_ssd_norm_g, m_w_out, m_ln2_g, m_ln2_b, m_ffn2_wg, m_ffn2_wu, m_ffn2_wd, m_ln3_g, m_ln3_b, m_pe_proj, m_pe_gate_w, m_pe_gate_b, v_ln1_g, v_ln1_b, v_ffn1_wg, v_ffn1_wu, v_ffn1_wd, v_w_in, v_lru_conv_w, v_lru_conv_b, v_lru_wa, v_lru_ba, v_lru_wx, v_lru_bx, v_lru_lambda, v_fox_bf, v_ssd_conv_w, v_ssd_conv_b, v_ssd_dt_bias, v_ssd_a_log, v_ssd_d, v_ssd_norm_g, v_w_out, v_ln2_g, v_ln2_b, v_ffn2_wg, v_ffn2_wu, v_ffn2_wd, v_ln3_g, v_ln3_b, v_pe_proj, v_pe_gate_w, v_pe_gate_b):
    args = locals()
    w_loc = {n: args[n] for n in WEIGHTS}
    m_loc = {n: args["m_" + n] for n in WEIGHTS}
    v_loc = {n: args["v_" + n] for n in WEIGHTS}
    chip = 2 * lax.axis_index("x") + lax.axis_index("y")
    core = lax.axis_index("c")
    big = list(BIG)
    small_sh = list(SMALL_SHARDED)
    small_rep = [n for n in WEIGHTS if n not in BIG and n not in SMALL_SHARDED]

    def srcs_of(li, groups):
        return [jnp.stack([_pad_axis(w_loc[n][li].astype(BF16), pad, SHARE) for n in names]) for names, pad in groups]

    def gather_comm(li, groups, base=0):
        return _gather_layer_comm(srcs_of(li, groups), li, base)

    def chip_weights(gathered, groups):
        return _big_weights({n: g[:, j] for (names, _), g in zip(groups, gathered) for j, n in enumerate(names)})

    def pair_sums(G, groups, tag):
        gs = [jnp.stack([_big_grad_by_chip(G, n) for n in names]) for names, _ in groups]
        flat = [g.reshape((-1,) + g.shape[2:]) for g in gs]
        theirs = _exchange(flat, ("c",), swap=True, name=f"reduce_cores_{tag}")
        return [_sum_pair(f, r, BF16, name=f"reduce_cores_sum_{tag}_{gi}").reshape(g.shape)
                for gi, (f, r, g) in enumerate(zip(flat, theirs, gs))]

    def finish_reduce(quad, sums, li, groups, tag):
        quad = [lax.dynamic_update_index_in_dim(q, lax.dynamic_index_in_dim(s, chip, 1, keepdims=False), chip, 0)
                for q, s in zip(quad, sums)]
        red = [_sum_slots(q.reshape(N_CHIPS, -1, q.shape[-1]), F32,
                          name=f"reduce_chips_sum_{tag}_{gi}").reshape(q.shape[1:]) for gi, q in enumerate(quad)]
        theirs = _exchange(red, ("c",), swap=True, name=f"reduce_share_{tag}")
        out = {}
        for (names, _), r, rv in zip(groups, red, theirs):
            both = jnp.where(core == li, r, rv)
            for j, n in enumerate(names):
                out[n] = both[j]
        return out

    everything = EARLY + LATE
    first0 = _run_comm(gather_comm(0, FIRST), name="gather_w_l0")
    small = {n: w_loc[n] for n in small_rep}
    (sg,) = _exchange([_pack([w_loc[n] for n in small_sh])[None]], ("x", "y"), name="gather_conv_w")
    shards = [_unpack(sg[k, 0], [w_loc[n].shape for n in small_sh]) for k in range(N_CHIPS)]
    for j, n in enumerate(small_sh):
        small[n] = _unshard(jnp.stack([shards[k][j] for k in range(N_CHIPS)]), SMALL_SHARDED[n])

    W0 = {**_small_weights(0, small), **chip_weights(first0, FIRST)}
    late0_comm = gather_comm(0, LATE)
    early1 = []

    def in_attention0(got):
        early1.extend(got[len(LATE):])
        return chip_weights(got[:len(LATE)], LATE)

    xs = x[0]
    xs, xb, sv0, W0 = _layer_fwd(
        0, xs, xs.astype(BF16), p[0, 0].astype(BF16), W0,
        up=(gather_comm(0, NEXT), lambda got: chip_weights(got, NEXT)),
        att=(_merge_comms([late0_comm, gather_comm(1, EARLY, base=late0_comm.n_sems)]), in_attention0))
    W1 = {**_small_weights(1, small), **chip_weights(early1, EARLY)}
    xs, _, sv1, W1 = _layer_fwd(1, xs, xb, p[1, 0].astype(BF16), W1,
                                att=(gather_comm(1, LATE), lambda got: chip_weights(got, LATE)))
    dx, loss = _loss_kernel(xs, loss_target[0], name="loss")
    loss = lax.psum(loss[0, 0], MESH_AXES)
    dx, G1, _, _ = _layer_bwd(1, dx, sv1, W1)
    sums1 = pair_sums(G1, everything, "l1")
    comm1 = _reduce_chips_comm(sums1, 1)
    late_sums, early_sums = [], []

    def late0(G):
        late_sums.extend(pair_sums(G, LATE, "l0_late"))
        return _reduce_chips_comm(late_sums, 0, base=comm1.n_sems)

    def last0(G):
        early_sums.extend(pair_sums(G, EARLY, "l0"))
        return _reduce_chips_comm(early_sums, 0)

    grad_x, G0, quads, quads0 = _layer_bwd(0, dx, sv0, W0, comm=comm1, late=late0, last=last0)

    n1 = len(comm1.out_shapes)
    red = [{**finish_reduce(quads[n1:], late_sums, 0, LATE, "l0_late"),
            **finish_reduce(quads0, early_sums, 0, EARLY, "l0")},
           finish_reduce(quads[:n1], sums1, 1, everything, "l1")]
    g_red = {}
    for n in big:
        g = jnp.stack([red[li][n] for li in range(DEPTH)])
        g_red[n] = g[tuple(slice(0, s) for s in w_loc[n].shape)]
    small_l = [_layer_small_grads(G0, W0), _layer_small_grads(G1, W1)]
    g_small = {n: jnp.stack([small_l[li][n] for li in range(DEPTH)]) for n in small_l[0]}
    small_all = small_rep + small_sh
    sgp = _pack([g_small[n] for n in small_all])
    (sall,) = _exchange([sgp[None]], MESH_AXES, name="reduce_small")
    sred = _sum_slots(sall.reshape((2 ** len(MESH_AXES),) + sgp.shape), F32, name="reduce_small_sum")
    for n, g in zip(small_all, _unpack(sred, [g_small[n].shape for n in small_all])):
        if n in SMALL_SHARDED:
            width = w_loc[n].shape[-1]
            g = lax.dynamic_slice_in_dim(g, chip * width, width, axis=SMALL_SHARDED[n])
        g_red[n] = g

    delta, new_m, new_v = {}, {}, {}
    for n in big:
        delta[n], new_m[n], new_v[n] = _adamw(w_loc[n], g_red[n], m_loc[n], v_loc[n], name="adamw_" + n)
    shapes = [w_loc[n].shape for n in small_all]
    packs = [_pack([d[n] for n in small_all]) for d in (w_loc, g_red, m_loc, v_loc)]
    outs = _adamw(*packs, name="adamw_small")
    for d, o in zip((delta, new_m, new_v), outs):
        for n, a in zip(small_all, _unpack(o, shapes)):
            d[n] = a
    return (loss, grad_x[None], *[g_red[n] for n in WEIGHTS], *[delta[n] for n in WEIGHTS],
            *[new_m[n] for n in WEIGHTS], *[new_v[n] for n in WEIGHTS])
```
